```python
import math
import jax, jax.numpy as jnp
from jax import lax
import numpy as np

D_MODEL = 1024
BATCH = 8
SEQ = 4096
DEPTH = 1

HEAD_DIM = 64
CONV_WIDTH = D_MODEL // 2
CONV_GROUPS = CONV_WIDTH // HEAD_DIM
CONV_K = 3
LRU_WIDTH = D_MODEL
LRU_HEADS = LRU_WIDTH // HEAD_DIM
LRU_CONV_K = 4
LRU_C = 8.0
MIX_WIDTH = CONV_WIDTH + LRU_WIDTH
IN_COLS = 3 * CONV_WIDTH + 2 * LRU_WIDTH
D_FF = 4 * D_MODEL
EPS = 1e-6

kernel_name = "hymba_style_shortconv_rglru_block"


def rmsnorm(x, g):
    xf = x.astype(jnp.float32)
    y = xf * lax.rsqrt(jnp.mean(xf * xf, axis=-1, keepdims=True) + EPS)
    return (y * g.astype(jnp.float32)).astype(x.dtype)


def causal_dwconv(x, w):
    k_len = w.shape[0]
    s = x.shape[1]
    xp = jnp.pad(x, ((0, 0), (k_len - 1, 0), (0, 0)))
    y = w[0] * xp[:, 0:s]
    for k in range(1, k_len):
        y = y + w[k] * xp[:, k:k + s]
    return y


def block_diag_linear(x, w, b):
    bt, s, _ = x.shape
    xh = x.reshape(bt, s, LRU_HEADS, HEAD_DIM)
    y = jnp.einsum('bshi,hij->bshj', xh, w).reshape(bt, s, LRU_WIDTH)
    return y + b


def rg_lru(x, w_a, b_a, w_x, b_x, lam):
    r = jax.nn.sigmoid(block_diag_linear(x, w_a, b_a).astype(jnp.float32))
    i = jax.nn.sigmoid(block_diag_linear(x, w_x, b_x).astype(jnp.float32))
    log_a = -LRU_C * r * jax.nn.softplus(-lam.astype(jnp.float32))
    a = jnp.exp(log_a)
    mult = jnp.sqrt(-jnp.expm1(2.0 * log_a))
    bx = mult * (i * x.astype(jnp.float32))

    def combine(lhs, rhs):
        a1, b1 = lhs
        a2, b2 = rhs
        return a1 * a2, a2 * b1 + b2

    _, h = lax.associative_scan(combine, (a, bx), axis=1)
    return h.astype(x.dtype)


def _fwd_setup_inputs(seed: int = 0) -> dict:
    key = jax.random.key(seed)
    ks = jax.random.split(key, 20)
    L = DEPTH
    nrm = jax.random.normal
    x = nrm(ks[0], (BATCH, SEQ, D_MODEL), jnp.float32)
    norm_mix_g = 1.0 + 0.02 * nrm(ks[1], (L, D_MODEL), jnp.float32)
    w_in = nrm(ks[2], (L, D_MODEL, IN_COLS), jnp.float32) * D_MODEL ** -0.5
    conv_w = nrm(ks[3], (L, CONV_K, CONV_WIDTH), jnp.float32) * CONV_K ** -0.5
    rnn_conv_w = nrm(ks[4], (L, LRU_CONV_K, LRU_WIDTH), jnp.float32) * LRU_CONV_K ** -0.5
    rnn_conv_b = 0.01 * nrm(ks[5], (L, LRU_WIDTH), jnp.float32)
    w_a = nrm(ks[6], (L, LRU_HEADS, HEAD_DIM, HEAD_DIM), jnp.float32) * HEAD_DIM ** -0.5
    b_a = 0.01 * nrm(ks[7], (L, LRU_WIDTH), jnp.float32)
    w_x = nrm(ks[8], (L, LRU_HEADS, HEAD_DIM, HEAD_DIM), jnp.float32) * HEAD_DIM ** -0.5
    b_x = 0.01 * nrm(ks[9], (L, LRU_WIDTH), jnp.float32)
    a_c = jax.random.uniform(ks[10], (L, LRU_WIDTH), jnp.float32, 0.9, 0.999)
    s = a_c ** (1.0 / LRU_C)
    lru_lambda = jnp.log(s) - jnp.log1p(-s)
    g_norm_conv = 1.0 + 0.02 * nrm(ks[11], (L, CONV_WIDTH), jnp.float32)
    g_norm_rnn = 1.0 + 0.02 * nrm(ks[12], (L, LRU_WIDTH), jnp.float32)
    w_out = nrm(ks[13], (L, MIX_WIDTH, D_MODEL), jnp.float32) * MIX_WIDTH ** -0.5
    norm_mlp_g = 1.0 + 0.02 * nrm(ks[14], (L, D_MODEL), jnp.float32)
    w_mlp_in = nrm(ks[15], (L, D_MODEL, D_FF), jnp.float32) * D_MODEL ** -0.5
    w_mlp_out = nrm(ks[16], (L, D_FF, D_MODEL), jnp.float32) * D_FF ** -0.5
    final_norm_g = 1.0 + 0.02 * nrm(ks[17], (D_MODEL,), jnp.float32)
    return {"x": x, "norm_mix_g": norm_mix_g, "w_in": w_in, "conv_w": conv_w,
            "rnn_conv_w": rnn_conv_w, "rnn_conv_b": rnn_conv_b,
            "w_a": w_a, "b_a": b_a, "w_x": w_x, "b_x": b_x,
            "lru_lambda": lru_lambda, "g_norm_conv": g_norm_conv,
            "g_norm_rnn": g_norm_rnn, "w_out": w_out, "norm_mlp_g": norm_mlp_g,
            "w_mlp_in": w_mlp_in, "w_mlp_out": w_mlp_out,
            "final_norm_g": final_norm_g}


def _fwd_reference(x, norm_mix_g, w_in, conv_w, rnn_conv_w, rnn_conv_b, w_a, b_a,
              w_x, b_x, lru_lambda, g_norm_conv, g_norm_rnn, w_out,
              norm_mlp_g, w_mlp_in, w_mlp_out, final_norm_g):
    split_pts = [CONV_WIDTH, 2 * CONV_WIDTH, 3 * CONV_WIDTH,
                 3 * CONV_WIDTH + LRU_WIDTH]
    for l in range(DEPTH):
        h = rmsnorm(x, norm_mix_g[l])
        u = jnp.einsum('bsd,dc->bsc', h, w_in[l])
        gate_b, gate_c, v, x_r, g = jnp.split(u, split_pts, axis=-1)
        y_conv = gate_b * causal_dwconv(gate_c * v, conv_w[l])
        xr = causal_dwconv(x_r, rnn_conv_w[l]) + rnn_conv_b[l]
        y_rnn = rg_lru(xr, w_a[l], b_a[l], w_x[l], b_x[l], lru_lambda[l])
        y_rnn = y_rnn * jax.nn.gelu(g)
        y = jnp.concatenate([rmsnorm(y_conv, g_norm_conv[l]),
                             rmsnorm(y_rnn, g_norm_rnn[l])], axis=-1)
        x = x + jnp.einsum('bsc,cd->bsd', y, w_out[l])
        h = rmsnorm(x, norm_mlp_g[l])
        z = jnp.square(jax.nn.relu(jnp.einsum('bsd,df->bsf', h, w_mlp_in[l])))
        x = x + jnp.einsum('bsf,fd->bsd', z, w_mlp_out[l])
    return rmsnorm(x, final_norm_g)


import jax as _jax
import jax.numpy as _jnp

TWIN_FORMAT = 'train_step'
FWD_PARAMS = ['x', 'norm_mix_g', 'w_in', 'conv_w', 'rnn_conv_w', 'rnn_conv_b', 'w_a', 'b_a', 'w_x', 'b_x', 'lru_lambda', 'g_norm_conv', 'g_norm_rnn', 'w_out', 'norm_mlp_g', 'w_mlp_in', 'w_mlp_out', 'final_norm_g']
TWIN_WEIGHTS = ['norm_mix_g', 'w_in', 'conv_w', 'rnn_conv_w', 'rnn_conv_b', 'w_a', 'b_a', 'w_x', 'b_x', 'lru_lambda', 'g_norm_conv', 'g_norm_rnn', 'w_out', 'norm_mlp_g', 'w_mlp_in', 'w_mlp_out', 'final_norm_g']
TWIN_DIFF_INPUT = 'x'
TWIN_INPUTS = ['x', 'norm_mix_g', 'w_in', 'conv_w', 'rnn_conv_w', 'rnn_conv_b', 'w_a', 'b_a', 'w_x', 'b_x', 'lru_lambda', 'g_norm_conv', 'g_norm_rnn', 'w_out', 'norm_mlp_g', 'w_mlp_in', 'w_mlp_out', 'final_norm_g', 'loss_target', 'm_norm_mix_g', 'm_w_in', 'm_conv_w', 'm_rnn_conv_w', 'm_rnn_conv_b', 'm_w_a', 'm_b_a', 'm_w_x', 'm_b_x', 'm_lru_lambda', 'm_g_norm_conv', 'm_g_norm_rnn', 'm_w_out', 'm_norm_mlp_g', 'm_w_mlp_in', 'm_w_mlp_out', 'm_final_norm_g', 'v_norm_mix_g', 'v_w_in', 'v_conv_w', 'v_rnn_conv_w', 'v_rnn_conv_b', 'v_w_a', 'v_b_a', 'v_w_x', 'v_b_x', 'v_lru_lambda', 'v_g_norm_conv', 'v_g_norm_rnn', 'v_w_out', 'v_norm_mlp_g', 'v_w_mlp_in', 'v_w_mlp_out', 'v_final_norm_g']
TWIN_OUTPUTS = ['loss', 'grad_x', 'grad_norm_mix_g', 'grad_w_in', 'grad_conv_w', 'grad_rnn_conv_w', 'grad_rnn_conv_b', 'grad_w_a', 'grad_b_a', 'grad_w_x', 'grad_b_x', 'grad_lru_lambda', 'grad_g_norm_conv', 'grad_g_norm_rnn', 'grad_w_out', 'grad_norm_mlp_g', 'grad_w_mlp_in', 'grad_w_mlp_out', 'grad_final_norm_g', 'delta_norm_mix_g', 'delta_w_in', 'delta_conv_w', 'delta_rnn_conv_w', 'delta_rnn_conv_b', 'delta_w_a', 'delta_b_a', 'delta_w_x', 'delta_b_x', 'delta_lru_lambda', 'delta_g_norm_conv', 'delta_g_norm_rnn', 'delta_w_out', 'delta_norm_mlp_g', 'delta_w_mlp_in', 'delta_w_mlp_out', 'delta_final_norm_g', 'new_m_norm_mix_g', 'new_m_w_in', 'new_m_conv_w', 'new_m_rnn_conv_w', 'new_m_rnn_conv_b', 'new_m_w_a', 'new_m_b_a', 'new_m_w_x', 'new_m_b_x', 'new_m_lru_lambda', 'new_m_g_norm_conv', 'new_m_g_norm_rnn', 'new_m_w_out', 'new_m_norm_mlp_g', 'new_m_w_mlp_in', 'new_m_w_mlp_out', 'new_m_final_norm_g', 'new_v_norm_mix_g', 'new_v_w_in', 'new_v_conv_w', 'new_v_rnn_conv_w', 'new_v_rnn_conv_b', 'new_v_w_a', 'new_v_b_a', 'new_v_w_x', 'new_v_b_x', 'new_v_lru_lambda', 'new_v_g_norm_conv', 'new_v_g_norm_rnn', 'new_v_w_out', 'new_v_norm_mlp_g', 'new_v_w_mlp_in', 'new_v_w_mlp_out', 'new_v_final_norm_g']
TWIN_LEAF_KINDS = {'loss': 'loss', 'grad_x': 'grad_x', 'grad_norm_mix_g': 'grad_w', 'grad_w_in': 'grad_w', 'grad_conv_w': 'grad_w', 'grad_rnn_conv_w': 'grad_w', 'grad_rnn_conv_b': 'grad_w', 'grad_w_a': 'grad_w', 'grad_b_a': 'grad_w', 'grad_w_x': 'grad_w', 'grad_b_x': 'grad_w', 'grad_lru_lambda': 'grad_w', 'grad_g_norm_conv': 'grad_w', 'grad_g_norm_rnn': 'grad_w', 'grad_w_out': 'grad_w', 'grad_norm_mlp_g': 'grad_w', 'grad_w_mlp_in': 'grad_w', 'grad_w_mlp_out': 'grad_w', 'grad_final_norm_g': 'grad_w', 'delta_norm_mix_g': 'delta_w', 'delta_w_in': 'delta_w', 'delta_conv_w': 'delta_w', 'delta_rnn_conv_w': 'delta_w', 'delta_rnn_conv_b': 'delta_w', 'delta_w_a': 'delta_w', 'delta_b_a': 'delta_w', 'delta_w_x': 'delta_w', 'delta_b_x': 'delta_w', 'delta_lru_lambda': 'delta_w', 'delta_g_norm_conv': 'delta_w', 'delta_g_norm_rnn': 'delta_w', 'delta_w_out': 'delta_w', 'delta_norm_mlp_g': 'delta_w', 'delta_w_mlp_in': 'delta_w', 'delta_w_mlp_out': 'delta_w', 'delta_final_norm_g': 'delta_w', 'new_m_norm_mix_g': 'new_m', 'new_m_w_in': 'new_m', 'new_m_conv_w': 'new_m', 'new_m_rnn_conv_w': 'new_m', 'new_m_rnn_conv_b': 'new_m', 'new_m_w_a': 'new_m', 'new_m_b_a': 'new_m', 'new_m_w_x': 'new_m', 'new_m_b_x': 'new_m', 'new_m_lru_lambda': 'new_m', 'new_m_g_norm_conv': 'new_m', 'new_m_g_norm_rnn': 'new_m', 'new_m_w_out': 'new_m', 'new_m_norm_mlp_g': 'new_m', 'new_m_w_mlp_in': 'new_m', 'new_m_w_mlp_out': 'new_m', 'new_m_final_norm_g': 'new_m', 'new_v_norm_mix_g': 'new_v', 'new_v_w_in': 'new_v', 'new_v_conv_w': 'new_v', 'new_v_rnn_conv_w': 'new_v', 'new_v_rnn_conv_b': 'new_v', 'new_v_w_a': 'new_v', 'new_v_b_a': 'new_v', 'new_v_w_x': 'new_v', 'new_v_b_x': 'new_v', 'new_v_lru_lambda': 'new_v', 'new_v_g_norm_conv': 'new_v', 'new_v_g_norm_rnn': 'new_v', 'new_v_w_out': 'new_v', 'new_v_norm_mlp_g': 'new_v', 'new_v_w_mlp_in': 'new_v', 'new_v_w_mlp_out': 'new_v', 'new_v_final_norm_g': 'new_v'}


def _forward(args):
    return _fwd_reference(*[args[k] for k in FWD_PARAMS])


def _output_shape():
    out = _jax.eval_shape(lambda: _forward(_fwd_setup_inputs(0)))
    return out.shape, out.dtype

N_MICROBATCH = 1
ADAM_LR = 0.001
ADAM_B1 = 0.9
ADAM_B2 = 0.999
ADAM_EPS = 1e-08
ADAM_WD = 0.01
ADAM_STEP = 10
PER_EXAMPLE_BATCH_AXIS = {'x': 0, 'loss_target': 0}
SHARED_INPUTS = []
_WEIGHT_DTYPES = {'norm_mix_g': _jnp.float32, 'w_in': _jnp.float32, 'conv_w': _jnp.float32, 'rnn_conv_w': _jnp.float32, 'rnn_conv_b': _jnp.float32, 'w_a': _jnp.float32, 'b_a': _jnp.float32, 'w_x': _jnp.float32, 'b_x': _jnp.float32, 'lru_lambda': _jnp.float32, 'g_norm_conv': _jnp.float32, 'g_norm_rnn': _jnp.float32, 'w_out': _jnp.float32, 'norm_mlp_g': _jnp.float32, 'w_mlp_in': _jnp.float32, 'w_mlp_out': _jnp.float32, 'final_norm_g': _jnp.float32}
MOMENT_SCALE = {'norm_mix_g': 2.107293e-01, 'w_in': 1.105397e-01, 'conv_w': 1.197603e-01, 'rnn_conv_w': 1.147009e-01, 'rnn_conv_b': 1.231331e+00, 'w_a': 4.229340e-02, 'b_a': 3.206370e-02, 'w_x': 7.611773e-02, 'b_x': 3.774828e-02, 'lru_lambda': 5.704777e-02, 'g_norm_conv': 1.178068e-01, 'g_norm_rnn': 1.140540e-01, 'w_out': 1.371852e-01, 'norm_mlp_g': 1.571532e-01, 'w_mlp_in': 6.759541e-02, 'w_mlp_out': 1.294041e-01, 'final_norm_g': 3.224579e+01}


def _to_microbatches(a, axis):
    t = _jnp.moveaxis(a, axis, 0)
    t = t.reshape((N_MICROBATCH, t.shape[0] // N_MICROBATCH) + t.shape[1:])
    return _jnp.moveaxis(t, 1, axis + 1)


def setup_inputs(seed: int = 0) -> dict:
    inp = _fwd_setup_inputs(seed)
    key = _jax.random.fold_in(_jax.random.key(seed), 7919)
    shape, _ = _output_shape()
    out = dict(inp)
    out["loss_target"] = _jax.random.normal(_jax.random.fold_in(key, 0), shape, _jnp.float32)
    for i, name in enumerate(TWIN_WEIGHTS):
        w = inp[name].astype(_jnp.float32)
        if MOMENT_SCALE is None:
            s = _jnp.sqrt(_jnp.mean(_jnp.square(w)) + 1e-30)
        else:
            s = MOMENT_SCALE[name]
        km, kv = _jax.random.split(_jax.random.fold_in(key, i + 1))
        out[name] = w
        out["m_" + name] = s * _jax.random.normal(km, w.shape, _jnp.float32)
        out["v_" + name] = (s * s) * _jax.random.uniform(kv, w.shape, _jnp.float32, 0.5, 1.5)
    if N_MICROBATCH > 1:
        for name, axis in PER_EXAMPLE_BATCH_AXIS.items():
            out[name] = _to_microbatches(out[name], axis)
    return {'x': out['x'], 'norm_mix_g': out['norm_mix_g'], 'w_in': out['w_in'], 'conv_w': out['conv_w'], 'rnn_conv_w': out['rnn_conv_w'], 'rnn_conv_b': out['rnn_conv_b'], 'w_a': out['w_a'], 'b_a': out['b_a'], 'w_x': out['w_x'], 'b_x': out['b_x'], 'lru_lambda': out['lru_lambda'], 'g_norm_conv': out['g_norm_conv'], 'g_norm_rnn': out['g_norm_rnn'], 'w_out': out['w_out'], 'norm_mlp_g': out['norm_mlp_g'], 'w_mlp_in': out['w_mlp_in'], 'w_mlp_out': out['w_mlp_out'], 'final_norm_g': out['final_norm_g'], 'loss_target': out['loss_target'], 'm_norm_mix_g': out['m_norm_mix_g'], 'm_w_in': out['m_w_in'], 'm_conv_w': out['m_conv_w'], 'm_rnn_conv_w': out['m_rnn_conv_w'], 'm_rnn_conv_b': out['m_rnn_conv_b'], 'm_w_a': out['m_w_a'], 'm_b_a': out['m_b_a'], 'm_w_x': out['m_w_x'], 'm_b_x': out['m_b_x'], 'm_lru_lambda': out['m_lru_lambda'], 'm_g_norm_conv': out['m_g_norm_conv'], 'm_g_norm_rnn': out['m_g_norm_rnn'], 'm_w_out': out['m_w_out'], 'm_norm_mlp_g': out['m_norm_mlp_g'], 'm_w_mlp_in': out['m_w_mlp_in'], 'm_w_mlp_out': out['m_w_mlp_out'], 'm_final_norm_g': out['m_final_norm_g'], 'v_norm_mix_g': out['v_norm_mix_g'], 'v_w_in': out['v_w_in'], 'v_conv_w': out['v_conv_w'], 'v_rnn_conv_w': out['v_rnn_conv_w'], 'v_rnn_conv_b': out['v_rnn_conv_b'], 'v_w_a': out['v_w_a'], 'v_b_a': out['v_b_a'], 'v_w_x': out['v_w_x'], 'v_b_x': out['v_b_x'], 'v_lru_lambda': out['v_lru_lambda'], 'v_g_norm_conv': out['v_g_norm_conv'], 'v_g_norm_rnn': out['v_g_norm_rnn'], 'v_w_out': out['v_w_out'], 'v_norm_mlp_g': out['v_norm_mlp_g'], 'v_w_mlp_in': out['v_w_mlp_in'], 'v_w_mlp_out': out['v_w_mlp_out'], 'v_final_norm_g': out['v_final_norm_g']}


def _loss(weights, diff, rest, loss_target):
    with _jax.named_scope("forward"):
        args = {**rest, TWIN_DIFF_INPUT: diff, **{k: w.astype(_WEIGHT_DTYPES[k]) for k, w in weights.items()}}
        y = _forward(args)
    with _jax.named_scope("loss_head"):
        err = _jnp.square(y.astype(_jnp.float32) - loss_target)
        return 0.5 * _jnp.sum(_jnp.mean(err, axis=-1)) if err.ndim else 0.5 * err


def _adamw(w, g, m, v):
    m = ADAM_B1 * m + (1.0 - ADAM_B1) * g
    v = ADAM_B2 * v + (1.0 - ADAM_B2) * _jnp.square(g)
    m_hat = m / (1.0 - ADAM_B1 ** ADAM_STEP)
    v_hat = v / (1.0 - ADAM_B2 ** ADAM_STEP)
    delta = -ADAM_LR * (m_hat / (_jnp.sqrt(v_hat) + ADAM_EPS) + ADAM_WD * w)
    return delta, m, v


def reference(x, norm_mix_g, w_in, conv_w, rnn_conv_w, rnn_conv_b, w_a, b_a, w_x, b_x, lru_lambda, g_norm_conv, g_norm_rnn, w_out, norm_mlp_g, w_mlp_in, w_mlp_out, final_norm_g, loss_target, m_norm_mix_g, m_w_in, m_conv_w, m_rnn_conv_w, m_rnn_conv_b, m_w_a, m_b_a, m_w_x, m_b_x, m_lru_lambda, m_g_norm_conv, m_g_norm_rnn, m_w_out, m_norm_mlp_g, m_w_mlp_in, m_w_mlp_out, m_final_norm_g, v_norm_mix_g, v_w_in, v_conv_w, v_rnn_conv_w, v_rnn_conv_b, v_w_a, v_b_a, v_w_x, v_b_x, v_lru_lambda, v_g_norm_conv, v_g_norm_rnn, v_w_out, v_norm_mlp_g, v_w_mlp_in, v_w_mlp_out, v_final_norm_g):
    given = dict(x=x, norm_mix_g=norm_mix_g, w_in=w_in, conv_w=conv_w, rnn_conv_w=rnn_conv_w, rnn_conv_b=rnn_conv_b, w_a=w_a, b_a=b_a, w_x=w_x, b_x=b_x, lru_lambda=lru_lambda, g_norm_conv=g_norm_conv, g_norm_rnn=g_norm_rnn, w_out=w_out, norm_mlp_g=norm_mlp_g, w_mlp_in=w_mlp_in, w_mlp_out=w_mlp_out, final_norm_g=final_norm_g, loss_target=loss_target, m_norm_mix_g=m_norm_mix_g, m_w_in=m_w_in, m_conv_w=m_conv_w, m_rnn_conv_w=m_rnn_conv_w, m_rnn_conv_b=m_rnn_conv_b, m_w_a=m_w_a, m_b_a=m_b_a, m_w_x=m_w_x, m_b_x=m_b_x, m_lru_lambda=m_lru_lambda, m_g_norm_conv=m_g_norm_conv, m_g_norm_rnn=m_g_norm_rnn, m_w_out=m_w_out, m_norm_mlp_g=m_norm_mlp_g, m_w_mlp_in=m_w_mlp_in, m_w_mlp_out=m_w_mlp_out, m_final_norm_g=m_final_norm_g, v_norm_mix_g=v_norm_mix_g, v_w_in=v_w_in, v_conv_w=v_conv_w, v_rnn_conv_w=v_rnn_conv_w, v_rnn_conv_b=v_rnn_conv_b, v_w_a=v_w_a, v_b_a=v_b_a, v_w_x=v_w_x, v_b_x=v_b_x, v_lru_lambda=v_lru_lambda, v_g_norm_conv=v_g_norm_conv, v_g_norm_rnn=v_g_norm_rnn, v_w_out=v_w_out, v_norm_mlp_g=v_norm_mlp_g, v_w_mlp_in=v_w_mlp_in, v_w_mlp_out=v_w_mlp_out, v_final_norm_g=v_final_norm_g)
    weights = {n: given[n] for n in TWIN_WEIGHTS}
    shared = {n: given[n] for n in SHARED_INPUTS}
    per_example = {n: given[n] for n in ['x']}
    grad_fn = _jax.value_and_grad(_loss, argnums=(0, 1))

    def one_microbatch(ex, loss_target):
        ex = dict(ex)
        diff = ex.pop(TWIN_DIFF_INPUT)
        return grad_fn(weights, diff, {**shared, **ex}, loss_target)

    if N_MICROBATCH == 1:
        loss, (grad_w, grad_x) = one_microbatch(per_example, given["loss_target"])
    else:
        def body(carry, xs):
            loss_sum, grad_sum = carry
            l_k, (gw_k, gx_k) = one_microbatch(xs[0], xs[1])
            with _jax.named_scope("update"):
                return (loss_sum + l_k, _jax.tree.map(_jnp.add, grad_sum, gw_k)), gx_k

        init = (_jnp.zeros((), _jnp.float32), _jax.tree.map(_jnp.zeros_like, weights))
        (loss, grad_w), grad_x = _jax.lax.scan(body, init, (per_example, given["loss_target"]))
    with _jax.named_scope("update"):
        delta_w, new_m, new_v = {}, {}, {}
        for n in TWIN_WEIGHTS:
            delta_w[n], new_m[n], new_v[n] = _adamw(weights[n], grad_w[n], given["m_" + n], given["v_" + n])
    return (loss, grad_x, *[grad_w[n] for n in TWIN_WEIGHTS], *[delta_w[n] for n in TWIN_WEIGHTS],
            *[new_m[n] for n in TWIN_WEIGHTS], *[new_v[n] for n in TWIN_WEIGHTS])
```

```python
import functools
import math

import jax
import jax.numpy as jnp
from jax import lax
from jax.experimental import pallas as pl
from jax.experimental.pallas import tpu as pltpu

F32 = jnp.float32
BF16 = jnp.bfloat16
MESH = pl.DeviceIdType.MESH
ANY = pl.BlockSpec(memory_space=pl.ANY)
VMEM = pl.BlockSpec(memory_space=pltpu.VMEM)

EPS = 1e-6
LRU_C = 8.0
D_MODEL = 1024
CONV_W = 512
LRU_W = 1024
IN_COLS = 3 * CONV_W + 2 * LRU_W
IN_SHARD = IN_COLS // 4
N_CHIPS = 4
BD = 256
N_BD = LRU_W // BD

ADAM_LR = 0.001
ADAM_B1 = 0.9
ADAM_B2 = 0.999
ADAM_EPS = 1e-08
ADAM_WD = 0.01
ADAM_STEP = 10
ADAM_BC1 = 1.0 - ADAM_B1 ** ADAM_STEP
ADAM_BC2 = 1.0 - ADAM_B2 ** ADAM_STEP

TILE_ROWS = 8
TOKEN_TILE = 256
VMEM_LIMIT = 56 * 1024 * 1024

PK_G_NORM_RNN, PK_RCONV_B, PK_B_A, PK_B_X, PK_LAMBDA, PK_RCONV_W, PK_CONV_W, PK_G_NORM_CONV = range(8)
PK_FINAL_G, PK_MLP_G, PK_LOSS, PK_MIX_G = 8, 9, 10, 11
PK_W_A = 12
PK_W_X = 20
PK_BLOCKS = 28
PK_ROWS = PK_BLOCKS * TILE_ROWS


def _params(**kw):
    return pltpu.CompilerParams(vmem_limit_bytes=VMEM_LIMIT, **kw)


def _position():
    x, y, c = lax.axis_index("x"), lax.axis_index("y"), lax.axis_index("c")
    return x, y, c


def _sigmoid(v):
    return 1.0 / (1.0 + jnp.exp(-v))


def _one_minus_exp(v):
    series = -v * (1.0 + v * (0.5 + v * (1.0 / 6.0 + v * (1.0 / 24.0 + v * (1.0 / 120.0)))))
    return jnp.where(v > -0.1, series, 1.0 - jnp.exp(v))


_GELU_C = math.sqrt(2.0 / math.pi)
_GELU_K = 0.044715


def _gelu_and_grad(g):
    th = jnp.tanh(_GELU_C * (g + _GELU_K * g * g * g))
    gelu = 0.5 * g * (1.0 + th)
    dgelu = 0.5 * (1.0 + th) + 0.5 * g * (1.0 - th * th) * (_GELU_C * (1.0 + 3.0 * _GELU_K * g * g))
    return gelu, dgelu


def _rows(shape):
    return lax.broadcasted_iota(jnp.int32, shape, 0)


def _shift_down(v, k, prev8):
    rolled = pltpu.roll(v, k, 0)
    halo = pltpu.roll(prev8, k, 0)
    head = jnp.where(_rows(halo.shape) < k, halo, rolled[:TILE_ROWS])
    return jnp.concatenate([head, rolled[TILE_ROWS:]], axis=0)


def _shift_up(v, k, next8):
    n = v.shape[0]
    rolled = pltpu.roll(v, n - k, 0)
    halo = pltpu.roll(next8, TILE_ROWS - k, 0)
    tail = jnp.where(_rows(halo.shape) >= TILE_ROWS - k, halo, rolled[n - TILE_ROWS:])
    return jnp.concatenate([rolled[: n - TILE_ROWS], tail], axis=0)


def _scan_down(a, b):
    n = a.shape[0]
    row = _rows(a.shape)
    s = 1
    while s < n:
        keep = row >= s
        b = jnp.where(keep, a * pltpu.roll(b, s, 0) + b, b)
        a = jnp.where(keep, a * pltpu.roll(a, s, 0), a)
        s *= 2
    return a, b


def _scan_up(a, b):
    n = a.shape[0]
    row = _rows(a.shape)
    s = 1
    while s < n:
        keep = row < n - s
        b = jnp.where(keep, a * pltpu.roll(b, n - s, 0) + b, b)
        a = jnp.where(keep, a * pltpu.roll(a, n - s, 0), a)
        s *= 2
    return a, b


def _softplus_neg(lam):
    e = jnp.exp(-jnp.abs(lam))
    log1p_e = jnp.where(e < 1e-2, e * (1.0 - e * (0.5 - e * (1.0 / 3.0 - e * 0.25))), jnp.log(1.0 + e))
    sp = jnp.maximum(-lam, 0.0) + log1p_e
    dsp = -_sigmoid(-lam)
    return sp, dsp


def _block_diag_dot(vb, w_ref):
    return jnp.concatenate(
        [jnp.dot(vb[:, j * BD:(j + 1) * BD], w_ref[j], preferred_element_type=F32) for j in range(N_BD)], axis=1)


def _block_diag_dot_t(vb, w_ref):
    return jnp.concatenate(
        [lax.dot_general(vb[:, j * BD:(j + 1) * BD], w_ref[j], (((1,), (1,)), ((), ())), preferred_element_type=F32)
         for j in range(N_BD)], axis=1)


def _dot_nt(a, b):
    return lax.dot_general(a, b, (((1,), (1,)), ((), ())), preferred_element_type=F32)


def _dot_tn(a, b):
    return lax.dot_general(a, b, (((0,), (0,)), ((), ())), preferred_element_type=F32)


def _lru_gates(xr, wa_ref, ba, wx_ref, bx, sp):
    xrb = xr.astype(BF16)
    r = _sigmoid(_block_diag_dot(xrb, wa_ref) + ba)
    ig = _sigmoid(_block_diag_dot(xrb, wx_ref) + bx)
    log_a = (-LRU_C) * r * sp
    a = jnp.exp(log_a)
    mult = jnp.sqrt(_one_minus_exp(2.0 * log_a))
    return xrb, r, ig, a, mult


def _colsum(v):
    return jnp.sum(v, axis=0, keepdims=True)


def _fwd_mix(x, g1, w_in_g, conv_w, rconv_w, rconv_b, wa_bd, b_a, wx_bd, b_x, lam, g_nc, g_nr):
    t, d = x.shape
    tm = TOKEN_TILE

    def body(x_ref, g1_ref, win_ref, cw_ref, rw_ref, rb_ref, wa_ref, ba_ref, wx_ref, bx_ref, lam_ref, gnc_ref, gnr_ref,
             u_ref, h1_ref, xr_ref, hs_ref, c3_ref, y_ref, cv_prev, xin_prev, h_prev):
        @pl.when(pl.program_id(0) == 0)
        def _():
            cv_prev[...] = jnp.zeros_like(cv_prev)
            xin_prev[...] = jnp.zeros_like(xin_prev)
            h_prev[...] = jnp.zeros_like(h_prev)

        xv = x_ref[...]
        rstd = lax.rsqrt(jnp.mean(xv * xv, axis=-1, keepdims=True) + EPS)
        h1b = ((xv * rstd) * g1_ref[...]).astype(BF16)
        h1_ref[...] = h1b
        for j in range(N_CHIPS):
            u_ref[:, j * IN_SHARD:(j + 1) * IN_SHARD] = jnp.dot(h1b, win_ref[j], preferred_element_type=F32)
        gate_b = u_ref[:, 0:CONV_W]
        cv = u_ref[:, CONV_W:2 * CONV_W] * u_ref[:, 2 * CONV_W:3 * CONV_W]
        x_r = u_ref[:, 3 * CONV_W:3 * CONV_W + LRU_W]
        g = u_ref[:, 3 * CONV_W + LRU_W:]

        cw = cw_ref[...]
        cvp = cv_prev[...]
        conv3 = cw[0:1] * _shift_down(cv, 2, cvp) + cw[1:2] * _shift_down(cv, 1, cvp) + cw[2:3] * cv
        cv_prev[...] = cv[tm - TILE_ROWS:]
        c3_ref[...] = conv3
        y_conv = gate_b * conv3

        rw = rw_ref[...]
        xp = xin_prev[...]
        xr = (rw[0:1] * _shift_down(x_r, 3, xp) + rw[1:2] * _shift_down(x_r, 2, xp)
              + rw[2:3] * _shift_down(x_r, 1, xp) + rw[3:4] * x_r) + rb_ref[...]
        xin_prev[...] = x_r[tm - TILE_ROWS:]
        xr_ref[...] = xr
        sp, _ = _softplus_neg(lam_ref[...])
        _, _, ig, a, mult = _lru_gates(xr, wa_ref, ba_ref[...], wx_ref, bx_ref[...], sp)
        a_cum, h = _scan_down(a, mult * (ig * xr))
        h = h + a_cum * h_prev[...]
        h_prev[...] = h[tm - 1:tm]
        hs_ref[...] = h
        gelu, _ = _gelu_and_grad(g)
        y_rnn = h * gelu

        na = y_conv * lax.rsqrt(jnp.mean(y_conv * y_conv, axis=-1, keepdims=True) + EPS) * gnc_ref[...]
        nb = y_rnn * lax.rsqrt(jnp.mean(y_rnn * y_rnn, axis=-1, keepdims=True) + EPS) * gnr_ref[...]
        y_ref[:, :CONV_W] = na.astype(BF16)
        y_ref[:, CONV_W:] = nb.astype(BF16)

    def full(a):
        nd = a.ndim
        return pl.BlockSpec(a.shape, lambda i: (0,) * nd)

    def tok(cols):
        return pl.BlockSpec((tm, cols), lambda i: (i, 0))

    smalls = (g1, w_in_g, conv_w, rconv_w, rconv_b, wa_bd, b_a, wx_bd, b_x, lam, g_nc, g_nr)
    return pl.pallas_call(
        body, name="fwd_mix", grid=(t // tm,),
        in_specs=[tok(d)] + [full(a) for a in smalls],
        out_specs=[tok(IN_COLS), tok(d), tok(LRU_W), tok(LRU_W), tok(CONV_W), tok(CONV_W + LRU_W)],
        out_shape=[jax.ShapeDtypeStruct((t, IN_COLS), F32), jax.ShapeDtypeStruct((t, d), BF16),
                   jax.ShapeDtypeStruct((t, LRU_W), F32), jax.ShapeDtypeStruct((t, LRU_W), F32),
                   jax.ShapeDtypeStruct((t, CONV_W), F32), jax.ShapeDtypeStruct((t, CONV_W + LRU_W), BF16)],
        scratch_shapes=[pltpu.VMEM((TILE_ROWS, CONV_W), F32), pltpu.VMEM((TILE_ROWS, LRU_W), F32),
                        pltpu.VMEM((1, LRU_W), F32)],
        compiler_params=_params(dimension_semantics=("arbitrary",)),
    )(x, *smalls)


def _mlp_fwd_bwd(x, yb, w_out_g, w1_g, w2_g, g2, gf, target):
    t, d = x.shape
    tm = TOKEN_TILE
    ff = w2_g.shape[0]
    mix = w_out_g.shape[0]
    ffs = ff // N_CHIPS

    def body(x_ref, y_ref, g2_ref, gf_ref, tgt_ref, wout_hbm, w1_hbm, w2_hbm,
             z_ref, dp_ref, h2_ref, dx3b_ref, dx2_ref, dx2b_ref, dy_ref, st_ref, wout, w1, w2, p_ref):
        @pl.when(pl.program_id(0) == 0)
        def _():
            pltpu.sync_copy(wout_hbm, wout)
            pltpu.sync_copy(w1_hbm, w1)
            pltpu.sync_copy(w2_hbm, w2)
            st_ref[...] = jnp.zeros_like(st_ref)

        x2 = x_ref[...] + jnp.dot(y_ref[...], wout[...], preferred_element_type=F32)
        r2 = lax.rsqrt(jnp.mean(x2 * x2, axis=-1, keepdims=True) + EPS)
        xh2 = x2 * r2
        g2v = g2_ref[...]
        h2b = (xh2 * g2v).astype(BF16)
        h2_ref[...] = h2b
        for j in range(N_CHIPS):
            p_ref[:, j * ffs:(j + 1) * ffs] = jnp.dot(h2b, w1[j], preferred_element_type=F32)
        rp = jnp.maximum(p_ref[...], 0.0)
        zb = (rp * rp).astype(BF16)
        z_ref[...] = zb
        x3 = x2 + jnp.dot(zb, w2[...], preferred_element_type=F32)
        r3 = lax.rsqrt(jnp.mean(x3 * x3, axis=-1, keepdims=True) + EPS)
        xh3 = x3 * r3
        gfv = gf_ref[...]
        err = xh3 * gfv - tgt_ref[...]
        loss = (0.5 / d) * jnp.sum(err * err)
        dout = err * (1.0 / d)
        st_ref[PK_FINAL_G * 8 - 64:PK_FINAL_G * 8 - 63, :] += _colsum(dout * xh3)
        st_ref[PK_LOSS * 8 - 64:PK_LOSS * 8 - 63, :] += jnp.zeros((1, d), F32) + loss
        dxh3 = dout * gfv
        dx3 = r3 * (dxh3 - xh3 * jnp.mean(dxh3 * xh3, axis=-1, keepdims=True))
        dx3b = dx3.astype(BF16)
        dx3b_ref[...] = dx3b
        dpb = (_dot_nt(dx3b, w2[...]) * (2.0 * rp)).astype(BF16)
        dp_ref[...] = dpb
        dh2 = _dot_nt(dpb[:, 0:ffs], w1[0])
        for j in range(1, N_CHIPS):
            dh2 = dh2 + _dot_nt(dpb[:, j * ffs:(j + 1) * ffs], w1[j])
        st_ref[PK_MLP_G * 8 - 64:PK_MLP_G * 8 - 63, :] += _colsum(dh2 * xh2)
        dxh2 = dh2 * g2v
        dx2 = dx3 + r2 * (dxh2 - xh2 * jnp.mean(dxh2 * xh2, axis=-1, keepdims=True))
        dx2_ref[...] = dx2
        dx2b = dx2.astype(BF16)
        dx2b_ref[...] = dx2b
        dy_ref[...] = _dot_nt(dx2b, wout[...])

    def tok(cols):
        return pl.BlockSpec((tm, cols), lambda i: (i, 0))

    def row(cols):
        return pl.BlockSpec((1, cols), lambda i: (0, 0))

    return pl.pallas_call(
        body, name="mlp_fwd_bwd", grid=(t // tm,),
        in_specs=[tok(d), tok(mix), row(d), row(d), tok(d), ANY, ANY, ANY],
        out_specs=[tok(ff), tok(ff), tok(d), tok(d), tok(d), tok(d), tok(mix),
                   pl.BlockSpec((3 * TILE_ROWS, d), lambda i: (0, 0))],
        out_shape=[jax.ShapeDtypeStruct((t, ff), BF16), jax.ShapeDtypeStruct((t, ff), BF16),
                   jax.ShapeDtypeStruct((t, d), BF16), jax.ShapeDtypeStruct((t, d), BF16),
                   jax.ShapeDtypeStruct((t, d), F32), jax.ShapeDtypeStruct((t, d), BF16),
                   jax.ShapeDtypeStruct((t, mix), F32), jax.ShapeDtypeStruct((3 * TILE_ROWS, d), F32)],
        scratch_shapes=[pltpu.VMEM(w_out_g.shape, BF16), pltpu.VMEM(w1_g.shape, BF16), pltpu.VMEM(w2_g.shape, BF16),
                        pltpu.VMEM((tm, ff), F32)],
        compiler_params=_params(dimension_semantics=("arbitrary",)),
    )(x, yb, g2, gf, target, w_out_g, w1_g, w2_g)


def _mix_bwd(dy, u, xr_all, hs_all, c3_all, conv_w, rconv_w, wa_bd, b_a, wx_bd, b_x, lam, g_nc, g_nr):
    t = dy.shape[0]
    tm = TOKEN_TILE
    nt = t // tm
    hb = tm // TILE_ROWS

    def body(dy_ref, u_ref, uh_ref, xr_ref, hs_ref, hh_ref, c3_ref, cw_ref, rw_ref, wa_ref, ba_ref, wx_ref, bx_ref,
             lam_ref, gnc_ref, gnr_ref, du_ref, st_ref, dwa_ref, dwx_ref, dc_next, a_next, gs_next, dxr_next):
        i = pl.program_id(0)

        @pl.when(i == 0)
        def _():
            dc_next[...] = jnp.zeros_like(dc_next)
            a_next[...] = jnp.zeros_like(a_next)
            gs_next[...] = jnp.zeros_like(gs_next)
            dxr_next[...] = jnp.zeros_like(dxr_next)
            st_ref[...] = jnp.zeros_like(st_ref)
            dwa_ref[...] = jnp.zeros_like(dwa_ref)
            dwx_ref[...] = jnp.zeros_like(dwx_ref)

        first_tile = i == nt - 1
        gate_b = u_ref[:, 0:CONV_W]
        gate_c = u_ref[:, CONV_W:2 * CONV_W]
        v = u_ref[:, 2 * CONV_W:3 * CONV_W]
        x_r = u_ref[:, 3 * CONV_W:3 * CONV_W + LRU_W]
        g = u_ref[:, 3 * CONV_W + LRU_W:]
        cv = gate_c * v
        cv_prev = jnp.where(first_tile, 0.0, uh_ref[:, CONV_W:2 * CONV_W] * uh_ref[:, 2 * CONV_W:3 * CONV_W])
        xin_prev = jnp.where(first_tile, 0.0, uh_ref[:, 3 * CONV_W:3 * CONV_W + LRU_W])
        hs_prev = jnp.where(first_tile, 0.0, hh_ref[...])

        def acc(block, val, width=LRU_W, row=0):
            r0 = block * TILE_ROWS + row
            st_ref[r0:r0 + 1, 0:width] += val

        conv3 = c3_ref[...]
        y_conv = gate_b * conv3
        ra = lax.rsqrt(jnp.mean(y_conv * y_conv, axis=-1, keepdims=True) + EPS)
        xha = y_conv * ra
        dna = dy_ref[:, :CONV_W]
        acc(PK_G_NORM_CONV, _colsum(dna * xha), CONV_W)
        dxha = dna * gnc_ref[...]
        dy_conv = ra * (dxha - xha * jnp.mean(dxha * xha, axis=-1, keepdims=True))
        du_ref[:, 0:CONV_W] = (dy_conv * conv3).astype(BF16)
        dc = dy_conv * gate_b
        cw = cw_ref[...]
        dcn = dc_next[...]
        dcv = cw[2:3] * dc + cw[1:2] * _shift_up(dc, 1, dcn) + cw[0:1] * _shift_up(dc, 2, dcn)
        dc_next[...] = dc[:TILE_ROWS]
        acc(PK_CONV_W, _colsum(dc * _shift_down(cv, 2, cv_prev)), CONV_W, 0)
        acc(PK_CONV_W, _colsum(dc * _shift_down(cv, 1, cv_prev)), CONV_W, 1)
        acc(PK_CONV_W, _colsum(dc * cv), CONV_W, 2)
        du_ref[:, CONV_W:2 * CONV_W] = (dcv * v).astype(BF16)
        du_ref[:, 2 * CONV_W:3 * CONV_W] = (dcv * gate_c).astype(BF16)

        hs = hs_ref[...]
        gelu, dgelu = _gelu_and_grad(g)
        y_rnn = hs * gelu
        rb = lax.rsqrt(jnp.mean(y_rnn * y_rnn, axis=-1, keepdims=True) + EPS)
        xhb = y_rnn * rb
        dnb = dy_ref[:, CONV_W:]
        acc(PK_G_NORM_RNN, _colsum(dnb * xhb))
        dxhb = dnb * gnr_ref[...]
        dy_rnn = rb * (dxhb - xhb * jnp.mean(dxhb * xhb, axis=-1, keepdims=True))
        du_ref[:, 3 * CONV_W + LRU_W:] = (dy_rnn * hs * dgelu).astype(BF16)
        dh = dy_rnn * gelu

        xr = xr_ref[...]
        sp, dsp = _softplus_neg(lam_ref[...])
        xrb, r, ig, a, mult = _lru_gates(xr, wa_ref, ba_ref[...], wx_ref, bx_ref[...], sp)
        a_up = _shift_up(a, 1, a_next[...])
        a_next[...] = a[:TILE_ROWS]
        a_cum, gs = _scan_up(a_up, dh)
        gs = gs + a_cum * gs_next[0:1, :]
        gs_next[...] = gs[:TILE_ROWS]
        da = gs * _shift_down(hs, 1, hs_prev)
        gx = gs * xr
        di = gx * mult
        dmult = gx * ig
        dxr = gs * (mult * ig)
        dlog_a = da * a - dmult * ((a * a) / mult)
        acc(PK_LAMBDA, _colsum(dlog_a * r) * ((-LRU_C) * dsp))
        dpa = (dlog_a * ((-LRU_C) * sp)) * (r * (1.0 - r))
        dpx = di * (ig * (1.0 - ig))
        acc(PK_B_A, _colsum(dpa))
        acc(PK_B_X, _colsum(dpx))
        dpab = dpa.astype(BF16)
        dpxb = dpx.astype(BF16)
        dxr = dxr + _block_diag_dot_t(dpab, wa_ref) + _block_diag_dot_t(dpxb, wx_ref)
        for j in range(N_BD):
            cols = slice(j * BD, (j + 1) * BD)
            dwa_ref[j] += _dot_tn(xrb[:, cols], dpab[:, cols])
            dwx_ref[j] += _dot_tn(xrb[:, cols], dpxb[:, cols])

        acc(PK_RCONV_B, _colsum(dxr))
        rw = rw_ref[...]
        dxn = dxr_next[...]
        dx_r = (rw[3:4] * dxr + rw[2:3] * _shift_up(dxr, 1, dxn) + rw[1:2] * _shift_up(dxr, 2, dxn)
                + rw[0:1] * _shift_up(dxr, 3, dxn))
        dxr_next[...] = dxr[:TILE_ROWS]
        for k in range(3):
            acc(PK_RCONV_W, _colsum(dxr * _shift_down(x_r, 3 - k, xin_prev)), LRU_W, k)
        acc(PK_RCONV_W, _colsum(dxr * x_r), LRU_W, 3)
        du_ref[:, 3 * CONV_W:3 * CONV_W + LRU_W] = dx_r.astype(BF16)

    def full(a):
        nd = a.ndim
        return pl.BlockSpec(a.shape, lambda i: (0,) * nd)

    def tok(cols):
        return pl.BlockSpec((tm, cols), lambda i: (nt - 1 - i, 0))

    def halo(cols):
        return pl.BlockSpec((TILE_ROWS, cols), lambda i: (jnp.maximum((nt - 1 - i) * hb - 1, 0), 0))

    smalls = (conv_w, rconv_w, wa_bd, b_a, wx_bd, b_x, lam, g_nc, g_nr)
    return pl.pallas_call(
        body, name="mix_bwd", grid=(nt,),
        in_specs=[tok(CONV_W + LRU_W), tok(IN_COLS), halo(IN_COLS), tok(LRU_W), tok(LRU_W), halo(LRU_W), tok(CONV_W)]
        + [full(a) for a in smalls],
        out_specs=[tok(IN_COLS), pl.BlockSpec((8 * TILE_ROWS, LRU_W), lambda i: (0, 0)),
                   pl.BlockSpec((N_BD, BD, BD), lambda i: (0, 0, 0)), pl.BlockSpec((N_BD, BD, BD), lambda i: (0, 0, 0))],
        out_shape=[jax.ShapeDtypeStruct((t, IN_COLS), BF16), jax.ShapeDtypeStruct((8 * TILE_ROWS, LRU_W), F32),
                   jax.ShapeDtypeStruct((N_BD, BD, BD), F32), jax.ShapeDtypeStruct((N_BD, BD, BD), F32)],
        scratch_shapes=[pltpu.VMEM((TILE_ROWS, CONV_W), F32), pltpu.VMEM((TILE_ROWS, LRU_W), F32),
                        pltpu.VMEM((TILE_ROWS, LRU_W), F32), pltpu.VMEM((TILE_ROWS, LRU_W), F32)],
        compiler_params=_params(dimension_semantics=("arbitrary",)),
    )(dy, u, u, xr_all, hs_all, hs_all, c3_all, *smalls)


def _in_bwd(dub, w_in_g, x, dx2, g1):
    t, d = x.shape
    tm = TOKEN_TILE

    def body(du_ref, win_ref, x_ref, dx2_ref, g1_ref, gx_ref, st_ref):
        @pl.when(pl.program_id(0) == 0)
        def _():
            st_ref[...] = jnp.zeros_like(st_ref)

        dh1 = _dot_nt(du_ref[:, 0:IN_SHARD], win_ref[0])
        for j in range(1, N_CHIPS):
            dh1 = dh1 + _dot_nt(du_ref[:, j * IN_SHARD:(j + 1) * IN_SHARD], win_ref[j])
        xv = x_ref[...]
        rstd = lax.rsqrt(jnp.mean(xv * xv, axis=-1, keepdims=True) + EPS)
        xh = xv * rstd
        st_ref[0:1, :] += _colsum(dh1 * xh)
        dxh = dh1 * g1_ref[...]
        gx_ref[...] = dx2_ref[...] + rstd * (dxh - xh * jnp.mean(dxh * xh, axis=-1, keepdims=True))

    def tok(cols):
        return pl.BlockSpec((tm, cols), lambda i: (i, 0))

    return pl.pallas_call(
        body, name="in_bwd", grid=(t // tm,),
        in_specs=[tok(IN_COLS), pl.BlockSpec(w_in_g.shape, lambda i: (0, 0, 0)), tok(d), tok(d),
                  pl.BlockSpec((1, d), lambda i: (0, 0))],
        out_specs=[tok(d), pl.BlockSpec((TILE_ROWS, d), lambda i: (0, 0))],
        out_shape=[jax.ShapeDtypeStruct((t, d), F32), jax.ShapeDtypeStruct((TILE_ROWS, d), F32)],
        compiler_params=_params(dimension_semantics=("arbitrary",)),
    )(dub, w_in_g, x, dx2, g1)


def _wgrad(a, b, bm, bn, name):
    t, m = a.shape
    n = b.shape[1]
    kc = 512
    nk = t // kc

    def body(a_ref, b_ref, o_ref):
        o_ref[0] = _dot_tn(a_ref[0:kc, :], b_ref[0:kc, :])
        for k in range(1, nk):
            o_ref[0] += _dot_tn(a_ref[k * kc:(k + 1) * kc, :], b_ref[k * kc:(k + 1) * kc, :])

    return pl.pallas_call(
        body, name=name, grid=(n // bn, m // bm),
        in_specs=[pl.BlockSpec((t, bm), lambda j, i: (0, i)), pl.BlockSpec((t, bn), lambda j, i: (0, j))],
        out_specs=pl.BlockSpec((1, bm, bn), lambda j, i: (j, i, 0)),
        out_shape=jax.ShapeDtypeStruct((n // bn, m, bn), F32),
        compiler_params=_params(dimension_semantics=("arbitrary", "arbitrary")),
    )(a, b)


def _other_chips(x, y):
    return [(1 - x, y), (x, 1 - y), (1 - x, 1 - y)]


def _gather_weights(w_in, w_out, w1, w2, small):
    bigs = (w_in, w_out, w1, w2)
    nb = len(bigs)

    def body(win_ref, wout_ref, w1_ref, w2_ref, sm_ref, gin, gout, g1, g2, gsm, st_in, st_out, st_1, st_2,
             send_sems, recv_sems, local_sems):
        x, y, c = _position()
        j = 2 * x + y
        sibling = (x, y, 1 - c)
        chips = _other_chips(x, y)
        srcs = (win_ref, wout_ref, w1_ref, w2_ref)
        stages = (st_in, st_out, st_1, st_2)
        outs = (gin, gout, g1, g2)
        for src, st in zip(srcs, stages):
            st[...] = src[...].astype(BF16)

        def half(ref, w, which):
            hr = bigs[w].shape[0] // 2
            return ref.at[pl.ds(which * hr, hr), :]

        def remote(src, dst, w, k, to):
            return pltpu.make_async_remote_copy(src_ref=src, dst_ref=dst, send_sem=send_sems.at[w, k],
                                                recv_sem=recv_sems.at[w, k], device_id=to, device_id_type=MESH)

        local = [pltpu.make_async_copy(stages[w], outs[w].at[j], local_sems.at[w]) for w in range(nb)]
        local.append(pltpu.make_async_copy(sm_ref, gsm.at[j], local_sems.at[nb]))
        for cp in local:
            cp.start()
        sends = []
        for w in range(nb):
            for k, (px, py) in enumerate(chips):
                sends.append(remote(half(stages[w], w, c), half(outs[w].at[j], w, c), w, k, (px, py, c)))
        for k, (px, py) in enumerate(chips):
            sends.append(remote(sm_ref, gsm.at[j], nb, k, (px, py, c)))
        for cp in sends:
            cp.start()
        for w in range(nb):
            for k, (px, py) in enumerate(chips):
                jj = 2 * px + py
                landed = half(outs[w].at[jj], w, c)
                remote(landed, landed, w, k, (px, py, c)).wait_recv()
                fwd = remote(landed, landed, w, 3 + k, sibling)
                fwd.start()
                sends.append(fwd)
        for k, (px, py) in enumerate(chips):
            jj = 2 * px + py
            remote(sm_ref, gsm.at[jj], nb, k, (px, py, c)).wait_recv()
        for w in range(nb):
            for k, (px, py) in enumerate(chips):
                jj = 2 * px + py
                other = half(outs[w].at[jj], w, 1 - c)
                remote(other, other, w, 3 + k, sibling).wait_recv()
        for cp in sends:
            cp.wait_send()
        for cp in local:
            cp.wait()

    def gathered(a, dtype):
        return jax.ShapeDtypeStruct((N_CHIPS,) + a.shape, dtype)

    return pl.pallas_call(
        body, name="gather_weights",
        in_specs=[VMEM] * 5, out_specs=[ANY] * 5,
        out_shape=[gathered(a, BF16) for a in bigs] + [gathered(small, F32)],
        scratch_shapes=[pltpu.VMEM(a.shape, BF16) for a in bigs]
        + [pltpu.SemaphoreType.DMA((nb + 1, 6)), pltpu.SemaphoreType.DMA((nb + 1, 6)), pltpu.SemaphoreType.DMA((nb + 1,))],
        compiler_params=_params(),
    )(*bigs, small)


def _swap_halves(grads):
    n = len(grads)

    def body(*refs):
        g_refs, r_refs, send_sems, recv_sems = refs[:n], refs[n:2 * n], refs[2 * n], refs[2 * n + 1]
        x, y, c = _position()
        copies = []
        for w in range(n):
            hr = grads[w].shape[1] // 2
            copies.append(pltpu.make_async_remote_copy(
                src_ref=g_refs[w].at[:, pl.ds((1 - c) * hr, hr), :], dst_ref=r_refs[w],
                send_sem=send_sems.at[w], recv_sem=recv_sems.at[w], device_id=(x, y, 1 - c), device_id_type=MESH))
        for cp in copies:
            cp.start()
        for cp in copies:
            cp.wait()

    return pl.pallas_call(
        body, name="grad_swap_halves", in_specs=[ANY] * n, out_specs=[ANY] * n,
        out_shape=[jax.ShapeDtypeStruct((g.shape[0], g.shape[1] // 2, g.shape[2]), F32) for g in grads],
        scratch_shapes=[pltpu.SemaphoreType.DMA((n,)), pltpu.SemaphoreType.DMA((n,))],
        compiler_params=_params(),
    )(*grads)


def _row_block(rows):
    return 256 if rows % 256 == 0 else rows


def _chip_partial(g, r, core, name):
    s, rows2, cols = g.shape
    hr = rows2 // 2
    rb = _row_block(hr)
    nrb = hr // rb

    def body(c_ref, g_ref, r_ref, p_ref, pb_ref):
        p = g_ref[...] + r_ref[...]
        p_ref[...] = p
        pb_ref[...] = p.astype(BF16)

    blk = (1, rb, cols)
    return pl.pallas_call(
        body, name=name,
        grid_spec=pltpu.PrefetchScalarGridSpec(
            num_scalar_prefetch=1, grid=(s, nrb),
            in_specs=[pl.BlockSpec(blk, lambda k, i, c: (k, c[0] * nrb + i, 0)), pl.BlockSpec(blk, lambda k, i, c: (k, i, 0))],
            out_specs=[pl.BlockSpec(blk, lambda k, i, c: (k, i, 0)), pl.BlockSpec(blk, lambda k, i, c: (k, i, 0))]),
        out_shape=[jax.ShapeDtypeStruct((s, hr, cols), F32), jax.ShapeDtypeStruct((s, hr, cols), BF16)],
        compiler_params=_params(dimension_semantics=("arbitrary", "arbitrary")),
    )(core, g, r)


def _exchange_partials(parts):
    n = len(parts)

    def body(*refs):
        p_refs, r_refs, send_sems, recv_sems = refs[:n], refs[n:2 * n], refs[2 * n], refs[2 * n + 1]
        x, y, c = _position()
        j = 2 * x + y
        copies = []
        for w in range(n):
            for k, (px, py) in enumerate(_other_chips(x, y)):
                copies.append(pltpu.make_async_remote_copy(
                    src_ref=p_refs[w].at[2 * px + py], dst_ref=r_refs[w].at[j],
                    send_sem=send_sems.at[w, k], recv_sem=recv_sems.at[w, k], device_id=(px, py, c), device_id_type=MESH))
        for cp in copies:
            cp.start()
        for w in range(n):
            for k, (px, py) in enumerate(_other_chips(x, y)):
                jj = 2 * px + py
                pltpu.make_async_remote_copy(
                    src_ref=p_refs[w].at[jj], dst_ref=r_refs[w].at[jj],
                    send_sem=send_sems.at[w, k], recv_sem=recv_sems.at[w, k], device_id=(px, py, c), device_id_type=MESH).wait()

    return pl.pallas_call(
        body, name="grad_exchange_partials", in_specs=[ANY] * n, out_specs=[ANY] * n,
        out_shape=[jax.ShapeDtypeStruct(p.shape, BF16) for p in parts],
        scratch_shapes=[pltpu.SemaphoreType.DMA((n, 3)), pltpu.SemaphoreType.DMA((n, 3))],
        compiler_params=_params(),
    )(*parts)


def _sum_partials(p, rb_all, chip, name):
    s, hr, cols = p.shape
    rb = _row_block(hr)

    def body(j_ref, p_ref, r1_ref, r2_ref, r3_ref, o_ref):
        o_ref[...] = ((p_ref[0] + r1_ref[0].astype(F32)) + r2_ref[0].astype(F32)) + r3_ref[0].astype(F32)

    blk = (1, rb, cols)

    def slot(off):
        return pl.BlockSpec(blk, lambda i, j: ((j[0] + off) % N_CHIPS, i, 0))

    return pl.pallas_call(
        body, name=name,
        grid_spec=pltpu.PrefetchScalarGridSpec(
            num_scalar_prefetch=1, grid=(hr // rb,),
            in_specs=[slot(0), slot(1), slot(2), slot(3)],
            out_specs=pl.BlockSpec((rb, cols), lambda i, j: (i, 0))),
        out_shape=jax.ShapeDtypeStruct((hr, cols), F32),
        compiler_params=_params(dimension_semantics=("arbitrary",)),
    )(chip, p, rb_all, rb_all, rb_all)


def _join_halves(halves):
    n = len(halves)

    def body(*refs):
        f_refs, o_refs, send_sems, recv_sems, local_sems = refs[:n], refs[n:2 * n], refs[2 * n], refs[2 * n + 1], refs[2 * n + 2]
        x, y, c = _position()
        local = [pltpu.make_async_copy(f_refs[w], o_refs[w].at[c], local_sems.at[w]) for w in range(n)]
        for cp in local:
            cp.start()
        remote = [pltpu.make_async_remote_copy(
            src_ref=f_refs[w], dst_ref=o_refs[w].at[c], send_sem=send_sems.at[w], recv_sem=recv_sems.at[w],
            device_id=(x, y, 1 - c), device_id_type=MESH) for w in range(n)]
        for cp in remote:
            cp.start()
        for w in range(n):
            pltpu.make_async_remote_copy(
                src_ref=f_refs[w], dst_ref=o_refs[w].at[1 - c], send_sem=send_sems.at[w], recv_sem=recv_sems.at[w],
                device_id=(x, y, 1 - c), device_id_type=MESH).wait()
        for cp in local:
            cp.wait()

    return pl.pallas_call(
        body, name="grad_join_halves", in_specs=[ANY] * n, out_specs=[ANY] * n,
        out_shape=[jax.ShapeDtypeStruct((2,) + f.shape, F32) for f in halves],
        scratch_shapes=[pltpu.SemaphoreType.DMA((n,)), pltpu.SemaphoreType.DMA((n,)), pltpu.SemaphoreType.DMA((n,))],
        compiler_params=_params(),
    )(*halves)


def _adamw(w, g, m, v):
    m = ADAM_B1 * m + (1.0 - ADAM_B1) * g
    v = ADAM_B2 * v + (1.0 - ADAM_B2) * (g * g)
    m_hat = m / ADAM_BC1
    v_hat = v / ADAM_BC2
    delta = -ADAM_LR * (m_hat / (jnp.sqrt(v_hat) + ADAM_EPS) + ADAM_WD * w)
    return delta, m, v


def _adamw_big(w, g, m, v, name):
    rows, cols = w.shape
    rb = _row_block(rows)

    def body(w_ref, g_ref, m_ref, v_ref, d_ref, nm_ref, nv_ref):
        d_ref[...], nm_ref[...], nv_ref[...] = _adamw(w_ref[...], g_ref[...], m_ref[...], v_ref[...])

    spec = pl.BlockSpec((rb, cols), lambda i: (i, 0))
    return pl.pallas_call(
        body, name=name, grid=(rows // rb,), in_specs=[spec] * 4, out_specs=[spec] * 3,
        out_shape=[jax.ShapeDtypeStruct(w.shape, F32)] * 3,
        compiler_params=_params(dimension_semantics=("arbitrary",)),
    )(w, g, m, v)


def _small_step(pack, w_pack, m_pack, v_pack, conv_wmv, rconv_wmv):
    n_dev = 8
    rows, cols = pack.shape
    cshard = conv_wmv.shape[2]
    rshard = rconv_wmv.shape[2]

    def body(p_ref, w_ref, m_ref, v_ref, cw_ref, rw_ref, g_ref, d_ref, nm_ref, nv_ref, co_ref, ro_ref,
             all_ref, send_sems, recv_sems, local_sem):
        x, y, c = _position()
        me, sibling = (x, y, c), (x, y, 1 - c)
        chips = _other_chips(x, y)

        def slot(px, py, pc):
            return all_ref.at[4 * px + 2 * py + pc]

        def copy(k, block, to, src=None):
            return pltpu.make_async_remote_copy(
                src_ref=slot(*block) if src is None else src, dst_ref=slot(*block),
                send_sem=send_sems.at[k], recv_sem=recv_sems.at[k], device_id=to, device_id_type=MESH)

        mine = pltpu.make_async_copy(p_ref, slot(*me), local_sem)
        mine.start()
        first = [copy(0, me, sibling, src=p_ref)]
        first += [copy(1 + k, me, (*chip, c), src=p_ref) for k, chip in enumerate(chips)]
        for cp in first:
            cp.start()
        passed = [copy(4 + k, (*chip, c), sibling) for k, chip in enumerate(chips)]
        for k, chip in enumerate(chips):
            copy(1 + k, (*chip, c), me).wait_recv()
            passed[k].start()
        copy(0, sibling, me).wait_recv()
        for k, chip in enumerate(chips):
            copy(4 + k, (*chip, 1 - c), me).wait_recv()
        for cp in first + passed:
            cp.wait_send()
        mine.wait()

        total = all_ref[0]
        for k in range(1, n_dev):
            total = total + all_ref[k]
        g_ref[...] = total
        d_ref[...], nm_ref[...], nv_ref[...] = _adamw(w_ref[...], total, m_ref[...], v_ref[...])

        j = 2 * x + y
        cblk = total[PK_CONV_W * 8:PK_CONV_W * 8 + 8, :]
        rblk = total[PK_RCONV_W * 8:PK_RCONV_W * 8 + 8, :]
        cg = cblk[:, 0:cshard]
        rg = rblk[:, 0:rshard]
        for k in range(1, N_CHIPS):
            cg = jnp.where(j == k, cblk[:, k * cshard:(k + 1) * cshard], cg)
            rg = jnp.where(j == k, rblk[:, k * rshard:(k + 1) * rshard], rg)
        co_ref[0] = cg
        co_ref[1], co_ref[2], co_ref[3] = _adamw(cw_ref[0], cg, cw_ref[1], cw_ref[2])
        ro_ref[0] = rg
        ro_ref[1], ro_ref[2], ro_ref[3] = _adamw(rw_ref[0], rg, rw_ref[1], rw_ref[2])

    packs = [jax.ShapeDtypeStruct((rows, cols), F32)] * 4
    return pl.pallas_call(
        body, name="small_grads_step", in_specs=[VMEM] * 6, out_specs=[VMEM] * 6,
        out_shape=packs + [jax.ShapeDtypeStruct((4, TILE_ROWS, cshard), F32), jax.ShapeDtypeStruct((4, TILE_ROWS, rshard), F32)],
        scratch_shapes=[pltpu.VMEM((n_dev, rows, cols), F32), pltpu.SemaphoreType.DMA((7,)), pltpu.SemaphoreType.DMA((7,)),
                        pltpu.SemaphoreType.DMA],
        compiler_params=_params(),
    )(pack, w_pack, m_pack, v_pack, conv_wmv, rconv_wmv)


def _blk(a):
    a = a.reshape(-1, a.shape[-1])
    return jnp.pad(a, ((0, TILE_ROWS - a.shape[0]), (0, D_MODEL - a.shape[1])))


def _zero_blk():
    return jnp.zeros((TILE_ROWS, D_MODEL), F32)


def _pack_params(p, pre):
    get = lambda n: p[pre + n]
    return jnp.concatenate([
        _blk(get("g_norm_rnn")), _blk(get("rnn_conv_b")), _blk(get("b_a")), _blk(get("b_x")), _blk(get("lru_lambda")),
        _zero_blk(), _zero_blk(), _blk(get("g_norm_conv")), _blk(get("final_norm_g").reshape(1, -1)), _blk(get("norm_mlp_g")),
        _zero_blk(), _blk(get("norm_mix_g")), get("w_a").reshape(64, D_MODEL), get("w_x").reshape(64, D_MODEL)], axis=0)


def _to_block_diag(w):
    w4 = w.reshape(N_BD, 4, 64, 64)
    eye = jnp.eye(4, dtype=w.dtype)
    return (w4[:, :, :, None, :] * eye[None, :, None, :, None]).reshape(N_BD, BD, BD)


def _from_block_diag(d):
    d5 = d.reshape(N_BD, 4, 64, 4, 64)
    return jnp.stack([d5[:, q, :, q, :] for q in range(4)], axis=1).reshape(64, D_MODEL)


def _pad_rows(a):
    return jnp.pad(a, ((0, TILE_ROWS - a.shape[0]), (0, 0)))


_NAMES = ['norm_mix_g', 'w_in', 'conv_w', 'rnn_conv_w', 'rnn_conv_b', 'w_a', 'b_a', 'w_x', 'b_x', 'lru_lambda',
          'g_norm_conv', 'g_norm_rnn', 'w_out', 'norm_mlp_g', 'w_mlp_in', 'w_mlp_out', 'final_norm_g']


def kernel(x, norm_mix_g, w_in, conv_w, rnn_conv_w, rnn_conv_b, w_a, b_a, w_x, b_x, lru_lambda, g_norm_conv, g_norm_rnn, w_out, norm_mlp_g, w_mlp_in, w_mlp_out, final_norm_g, loss_target, m_norm_mix_g, m_w_in, m_conv_w, m_rnn_conv_w, m_rnn_conv_b, m_w_a, m_b_a, m_w_x, m_b_x, m_lru_lambda, m_g_norm_conv, m_g_norm_rnn, m_w_out, m_norm_mlp_g, m_w_mlp_in, m_w_mlp_out, m_final_norm_g, v_norm_mix_g, v_w_in, v_conv_w, v_rnn_conv_w, v_rnn_conv_b, v_w_a, v_b_a, v_w_x, v_b_x, v_lru_lambda, v_g_norm_conv, v_g_norm_rnn, v_w_out, v_norm_mlp_g, v_w_mlp_in, v_w_mlp_out, v_final_norm_g):
    args = dict(locals())
    p = {}
    for n in _NAMES:
        for pre in ("", "m_", "v_"):
            a = args[pre + n]
            p[pre + n] = a[0] if a.ndim >= 3 else a
    xs = x[0]
    target = loss_target[0]
    core = lax.axis_index("c").astype(jnp.int32).reshape(1)
    chip = (2 * lax.axis_index("x") + lax.axis_index("y")).astype(jnp.int32).reshape(1)
    cshard = p["conv_w"].shape[1]
    rshard = p["rnn_conv_w"].shape[1]

    small = jnp.concatenate([_pad_rows(p["conv_w"]), _pad_rows(p["rnn_conv_w"])], axis=1)
    w_in_g, w_out_g, w1_g, w2_g, small_g = _gather_weights(p["w_in"], p["w_out"], p["w_mlp_in"], p["w_mlp_out"], small)
    w_out_g = w_out_g.reshape(-1, D_MODEL)
    w2_g = w2_g.reshape(-1, D_MODEL)
    conv_full = small_g[:, :3, :cshard].transpose(1, 0, 2).reshape(3, CONV_W)
    rconv_full = small_g[:, :4, cshard:].transpose(1, 0, 2).reshape(4, LRU_W)
    wa_bd = _to_block_diag(p["w_a"]).astype(BF16)
    wx_bd = _to_block_diag(p["w_x"]).astype(BF16)
    gf = p["final_norm_g"].reshape(1, -1)
    lru = (wa_bd, p["b_a"], wx_bd, p["b_x"], p["lru_lambda"], p["g_norm_conv"], p["g_norm_rnn"])

    u, h1b, xr, hs, c3, yb = _fwd_mix(xs, p["norm_mix_g"], w_in_g, conv_full, rconv_full, p["rnn_conv_b"], *lru)
    zb, dpb, h2b, dx3b, dx2, dx2b, dy, st_mlp = _mlp_fwd_bwd(xs, yb, w_out_g, w1_g, w2_g, p["norm_mlp_g"], gf, target)
    dub, st_mix, dwa_bd, dwx_bd = _mix_bwd(dy, u, xr, hs, c3, conv_full, rconv_full, *lru)
    grad_x, st_in = _in_bwd(dub, w_in_g, xs, dx2, p["norm_mix_g"])
    g_in = _wgrad(h1b, dub, 512, IN_SHARD, "wgrad_in")
    g_1 = _wgrad(h2b, dpb, 512, D_MODEL, "wgrad_mlp_in")
    g_2 = _wgrad(zb, dx3b, 512, D_MODEL, "wgrad_mlp_out").reshape(N_CHIPS, -1, D_MODEL)
    g_out = _wgrad(yb, dx2b, 512, D_MODEL, "wgrad_out").reshape(N_CHIPS, -1, D_MODEL)

    grads = (g_in, g_out, g_1, g_2)
    tags = ("in", "out", "mlp_in", "mlp_out")
    from_sibling = _swap_halves(grads)
    partials = [_chip_partial(g, r, core, "grad_chip_partial_" + tag) for g, r, tag in zip(grads, from_sibling, tags)]
    arrived = _exchange_partials([pb for _, pb in partials])
    halves = [_sum_partials(pf, rb, chip, "grad_sum_partials_" + tag) for (pf, _), rb, tag in zip(partials, arrived, tags)]
    joined = _join_halves(halves)
    big = {}
    for n, gj, tag in zip(("w_in", "w_out", "w_mlp_in", "w_mlp_out"), joined, tags):
        g = gj.reshape(-1, gj.shape[-1])
        big[n] = (g,) + tuple(_adamw_big(p[n], g, p["m_" + n], p["v_" + n], "adamw_" + tag))

    pack = jnp.concatenate([st_mix, st_mlp, st_in, _from_block_diag(dwa_bd), _from_block_diag(dwx_bd)], axis=0)
    conv_wmv = jnp.stack([_pad_rows(p[pre + "conv_w"]) for pre in ("", "m_", "v_")])
    rconv_wmv = jnp.stack([_pad_rows(p[pre + "rnn_conv_w"]) for pre in ("", "m_", "v_")])
    g_pack, d_pack, m_pack, v_pack, conv_out, rconv_out = _small_step(
        pack, _pack_params(p, ""), _pack_params(p, "m_"), _pack_params(p, "v_"), conv_wmv, rconv_wmv)

    def unpack(pk, kind):
        def vec(b, width=D_MODEL):
            return pk[b * 8:b * 8 + 1, :width]
        return {
            "norm_mix_g": vec(PK_MIX_G), "rnn_conv_b": vec(PK_RCONV_B), "b_a": vec(PK_B_A), "b_x": vec(PK_B_X),
            "lru_lambda": vec(PK_LAMBDA), "g_norm_conv": vec(PK_G_NORM_CONV, CONV_W), "g_norm_rnn": vec(PK_G_NORM_RNN),
            "norm_mlp_g": vec(PK_MLP_G), "final_norm_g": vec(PK_FINAL_G).reshape(-1),
            "w_a": pk[PK_W_A * 8:PK_W_A * 8 + 64].reshape(1, 16, 64, 64), "w_x": pk[PK_W_X * 8:PK_W_X * 8 + 64].reshape(1, 16, 64, 64),
            "conv_w": conv_out[kind, :3][None], "rnn_conv_w": rconv_out[kind, :4][None],
            "w_in": big["w_in"][kind][None], "w_out": big["w_out"][kind][None],
            "w_mlp_in": big["w_mlp_in"][kind][None], "w_mlp_out": big["w_mlp_out"][kind][None],
        }

    outs = [unpack(pk, kind) for kind, pk in enumerate((g_pack, d_pack, m_pack, v_pack))]
    for o in outs:
        for n in ("norm_mix_g", "rnn_conv_b", "b_a", "b_x", "lru_lambda", "g_norm_conv", "g_norm_rnn", "norm_mlp_g"):
            o[n] = o[n].reshape(1, -1)
    loss = g_pack[PK_LOSS * 8, 0]
    return (loss, grad_x[None], *[o[n] for o in outs for n in _NAMES])
```

```python
import functools
import math

import jax
import jax.numpy as jnp
from jax import lax
from jax.experimental import pallas as pl
from jax.experimental.pallas import tpu as pltpu

F32 = jnp.float32
BF16 = jnp.bfloat16
MESH = pl.DeviceIdType.MESH
ANY = pl.BlockSpec(memory_space=pl.ANY)
VMEM = pl.BlockSpec(memory_space=pltpu.VMEM)

EPS = 1e-6
LRU_C = 8.0
D_MODEL = 1024
CONV_W = 512
LRU_W = 1024
IN_COLS = 3 * CONV_W + 2 * LRU_W
IN_SHARD = IN_COLS // 4
N_CHIPS = 4
BD = 256
N_BD = LRU_W // BD

ADAM_LR = 0.001
ADAM_B1 = 0.9
ADAM_B2 = 0.999
ADAM_EPS = 1e-08
ADAM_WD = 0.01
ADAM_STEP = 10
ADAM_BC1 = 1.0 - ADAM_B1 ** ADAM_STEP
ADAM_BC2 = 1.0 - ADAM_B2 ** ADAM_STEP

TILE_ROWS = 8
TOKEN_TILE = 256
VMEM_LIMIT = 56 * 1024 * 1024

PK_G_NORM_RNN, PK_RCONV_B, PK_B_A, PK_B_X, PK_LAMBDA, PK_RCONV_W, PK_CONV_W, PK_G_NORM_CONV = range(8)
PK_FINAL_G, PK_MLP_G, PK_LOSS, PK_MIX_G = 8, 9, 10, 11
PK_W_A = 12
PK_W_X = 20
PK_BLOCKS = 28
PK_ROWS = PK_BLOCKS * TILE_ROWS


def _params(**kw):
    return pltpu.CompilerParams(vmem_limit_bytes=VMEM_LIMIT, **kw)


def _position():
    x, y, c = lax.axis_index("x"), lax.axis_index("y"), lax.axis_index("c")
    return x, y, c


def _sigmoid(v):
    return 1.0 / (1.0 + jnp.exp(-v))


def _one_minus_exp(v):
    series = -v * (1.0 + v * (0.5 + v * (1.0 / 6.0 + v * (1.0 / 24.0 + v * (1.0 / 120.0)))))
    return jnp.where(v > -0.1, series, 1.0 - jnp.exp(v))


_GELU_C = math.sqrt(2.0 / math.pi)
_GELU_K = 0.044715


def _gelu_and_grad(g):
    th = jnp.tanh(_GELU_C * (g + _GELU_K * g * g * g))
    gelu = 0.5 * g * (1.0 + th)
    dgelu = 0.5 * (1.0 + th) + 0.5 * g * (1.0 - th * th) * (_GELU_C * (1.0 + 3.0 * _GELU_K * g * g))
    return gelu, dgelu


def _rows(shape):
    return lax.broadcasted_iota(jnp.int32, shape, 0)


def _shift_down(v, k, prev8):
    rolled = pltpu.roll(v, k, 0)
    halo = pltpu.roll(prev8, k, 0)
    head = jnp.where(_rows(halo.shape) < k, halo, rolled[:TILE_ROWS])
    return jnp.concatenate([head, rolled[TILE_ROWS:]], axis=0)


def _shift_up(v, k, next8):
    n = v.shape[0]
    rolled = pltpu.roll(v, n - k, 0)
    halo = pltpu.roll(next8, TILE_ROWS - k, 0)
    tail = jnp.where(_rows(halo.shape) >= TILE_ROWS - k, halo, rolled[n - TILE_ROWS:])
    return jnp.concatenate([rolled[: n - TILE_ROWS], tail], axis=0)


def _scan_down(a, b):
    n = a.shape[0]
    row = _rows(a.shape)
    s = 1
    while s < n:
        keep = row >= s
        b = jnp.where(keep, a * pltpu.roll(b, s, 0) + b, b)
        a = jnp.where(keep, a * pltpu.roll(a, s, 0), a)
        s *= 2
    return a, b


def _scan_up(a, b):
    n = a.shape[0]
    row = _rows(a.shape)
    s = 1
    while s < n:
        keep = row < n - s
        b = jnp.where(keep, a * pltpu.roll(b, n - s, 0) + b, b)
        a = jnp.where(keep, a * pltpu.roll(a, n - s, 0), a)
        s *= 2
    return a, b


def _softplus_neg(lam):
    e = jnp.exp(-jnp.abs(lam))
    log1p_e = jnp.where(e < 1e-2, e * (1.0 - e * (0.5 - e * (1.0 / 3.0 - e * 0.25))), jnp.log(1.0 + e))
    sp = jnp.maximum(-lam, 0.0) + log1p_e
    dsp = -_sigmoid(-lam)
    return sp, dsp


def _block_diag_dot(vb, w_ref):
    return jnp.concatenate(
        [jnp.dot(vb[:, j * BD:(j + 1) * BD], w_ref[j], preferred_element_type=F32) for j in range(N_BD)], axis=1)


def _block_diag_dot_t(vb, w_ref):
    return jnp.concatenate(
        [lax.dot_general(vb[:, j * BD:(j + 1) * BD], w_ref[j], (((1,), (1,)), ((), ())), preferred_element_type=F32)
         for j in range(N_BD)], axis=1)


def _dot_nt(a, b):
    return lax.dot_general(a, b, (((1,), (1,)), ((), ())), preferred_element_type=F32)


def _dot_tn(a, b):
    return lax.dot_general(a, b, (((0,), (0,)), ((), ())), preferred_element_type=F32)


def _lru_gates(xr, wa_ref, ba, wx_ref, bx, sp):
    xrb = xr.astype(BF16)
    r = _sigmoid(_block_diag_dot(xrb, wa_ref) + ba)
    ig = _sigmoid(_block_diag_dot(xrb, wx_ref) + bx)
    log_a = (-LRU_C) * r * sp
    a = jnp.exp(log_a)
    mult = jnp.sqrt(_one_minus_exp(2.0 * log_a))
    return xrb, r, ig, a, mult


def _colsum(v):
    return jnp.sum(v, axis=0, keepdims=True)


def _fwd_mix(x, g1, w_in_g, conv_w, rconv_w, rconv_b, wa_bd, b_a, wx_bd, b_x, lam, g_nc, g_nr):
    t, d = x.shape
    tm = TOKEN_TILE

    def body(x_ref, g1_ref, win_ref, cw_ref, rw_ref, rb_ref, wa_ref, ba_ref, wx_ref, bx_ref, lam_ref, gnc_ref, gnr_ref,
             u_ref, h1_ref, xr_ref, hs_ref, c3_ref, y_ref, cv_prev, xin_prev, h_prev):
        @pl.when(pl.program_id(0) == 0)
        def _():
            cv_prev[...] = jnp.zeros_like(cv_prev)
            xin_prev[...] = jnp.zeros_like(xin_prev)
            h_prev[...] = jnp.zeros_like(h_prev)

        xv = x_ref[...]
        rstd = lax.rsqrt(jnp.mean(xv * xv, axis=-1, keepdims=True) + EPS)
        h1b = ((xv * rstd) * g1_ref[...]).astype(BF16)
        h1_ref[...] = h1b
        for j in range(N_CHIPS):
            u_ref[:, j * IN_SHARD:(j + 1) * IN_SHARD] = jnp.dot(h1b, win_ref[j], preferred_element_type=F32)
        gate_b = u_ref[:, 0:CONV_W]
        cv = u_ref[:, CONV_W:2 * CONV_W] * u_ref[:, 2 * CONV_W:3 * CONV_W]
        x_r = u_ref[:, 3 * CONV_W:3 * CONV_W + LRU_W]
        g = u_ref[:, 3 * CONV_W + LRU_W:]

        cw = cw_ref[...]
        cvp = cv_prev[...]
        conv3 = cw[0:1] * _shift_down(cv, 2, cvp) + cw[1:2] * _shift_down(cv, 1, cvp) + cw[2:3] * cv
        cv_prev[...] = cv[tm - TILE_ROWS:]
        c3_ref[...] = conv3
        y_conv = gate_b * conv3

        rw = rw_ref[...]
        xp = xin_prev[...]
        xr = (rw[0:1] * _shift_down(x_r, 3, xp) + rw[1:2] * _shift_down(x_r, 2, xp)
              + rw[2:3] * _shift_down(x_r, 1, xp) + rw[3:4] * x_r) + rb_ref[...]
        xin_prev[...] = x_r[tm - TILE_ROWS:]
        xr_ref[...] = xr
        sp, _ = _softplus_neg(lam_ref[...])
        _, _, ig, a, mult = _lru_gates(xr, wa_ref, ba_ref[...], wx_ref, bx_ref[...], sp)
        a_cum, h = _scan_down(a, mult * (ig * xr))
        h = h + a_cum * h_prev[...]
        h_prev[...] = h[tm - 1:tm]
        hs_ref[...] = h
        gelu, _ = _gelu_and_grad(g)
        y_rnn = h * gelu

        na = y_conv * lax.rsqrt(jnp.mean(y_conv * y_conv, axis=-1, keepdims=True) + EPS) * gnc_ref[...]
        nb = y_rnn * lax.rsqrt(jnp.mean(y_rnn * y_rnn, axis=-1, keepdims=True) + EPS) * gnr_ref[...]
        y_ref[:, :CONV_W] = na.astype(BF16)
        y_ref[:, CONV_W:] = nb.astype(BF16)

    def full(a):
        nd = a.ndim
        return pl.BlockSpec(a.shape, lambda i: (0,) * nd)

    def tok(cols):
        return pl.BlockSpec((tm, cols), lambda i: (i, 0))

    smalls = (g1, w_in_g, conv_w, rconv_w, rconv_b, wa_bd, b_a, wx_bd, b_x, lam, g_nc, g_nr)
    return pl.pallas_call(
        body, name="fwd_mix", grid=(t // tm,),
        in_specs=[tok(d)] + [full(a) for a in smalls],
        out_specs=[tok(IN_COLS), tok(d), tok(LRU_W), tok(LRU_W), tok(CONV_W), tok(CONV_W + LRU_W)],
        out_shape=[jax.ShapeDtypeStruct((t, IN_COLS), F32), jax.ShapeDtypeStruct((t, d), BF16),
                   jax.ShapeDtypeStruct((t, LRU_W), F32), jax.ShapeDtypeStruct((t, LRU_W), F32),
                   jax.ShapeDtypeStruct((t, CONV_W), F32), jax.ShapeDtypeStruct((t, CONV_W + LRU_W), BF16)],
        scratch_shapes=[pltpu.VMEM((TILE_ROWS, CONV_W), F32), pltpu.VMEM((TILE_ROWS, LRU_W), F32),
                        pltpu.VMEM((1, LRU_W), F32)],
        compiler_params=_params(dimension_semantics=("arbitrary",)),
    )(x, *smalls)


def _mlp_fwd_bwd(x, yb, w_out_g, w1_g, w2_g, g2, gf, target):
    t, d = x.shape
    tm = TOKEN_TILE
    ff = w2_g.shape[0]
    mix = w_out_g.shape[0]
    ffs = ff // N_CHIPS

    def body(x_ref, y_ref, g2_ref, gf_ref, tgt_ref, wout_hbm, w1_hbm, w2_hbm,
             z_ref, dp_ref, h2_ref, dx3b_ref, dx2_ref, dx2b_ref, dy_ref, st_ref, wout, w1, w2, p_ref):
        @pl.when(pl.program_id(0) == 0)
        def _():
            pltpu.sync_copy(wout_hbm, wout)
            pltpu.sync_copy(w1_hbm, w1)
            pltpu.sync_copy(w2_hbm, w2)
            st_ref[...] = jnp.zeros_like(st_ref)

        x2 = x_ref[...] + jnp.dot(y_ref[...], wout[...], preferred_element_type=F32)
        r2 = lax.rsqrt(jnp.mean(x2 * x2, axis=-1, keepdims=True) + EPS)
        xh2 = x2 * r2
        g2v = g2_ref[...]
        h2b = (xh2 * g2v).astype(BF16)
        h2_ref[...] = h2b
        for j in range(N_CHIPS):
            p_ref[:, j * ffs:(j + 1) * ffs] = jnp.dot(h2b, w1[j], preferred_element_type=F32)
        rp = jnp.maximum(p_ref[...], 0.0)
        zb = (rp * rp).astype(BF16)
        z_ref[...] = zb
        x3 = x2 + jnp.dot(zb, w2[...], preferred_element_type=F32)
        r3 = lax.rsqrt(jnp.mean(x3 * x3, axis=-1, keepdims=True) + EPS)
        xh3 = x3 * r3
        gfv = gf_ref[...]
        err = xh3 * gfv - tgt_ref[...]
        loss = (0.5 / d) * jnp.sum(err * err)
        dout = err * (1.0 / d)
        st_ref[PK_FINAL_G * 8 - 64:PK_FINAL_G * 8 - 63, :] += _colsum(dout * xh3)
        st_ref[PK_LOSS * 8 - 64:PK_LOSS * 8 - 63, :] += jnp.zeros((1, d), F32) + loss
        dxh3 = dout * gfv
        dx3 = r3 * (dxh3 - xh3 * jnp.mean(dxh3 * xh3, axis=-1, keepdims=True))
        dx3b = dx3.astype(BF16)
        dx3b_ref[...] = dx3b
        dpb = (_dot_nt(dx3b, w2[...]) * (2.0 * rp)).astype(BF16)
        dp_ref[...] = dpb
        dh2 = _dot_nt(dpb[:, 0:ffs], w1[0])
        for j in range(1, N_CHIPS):
            dh2 = dh2 + _dot_nt(dpb[:, j * ffs:(j + 1) * ffs], w1[j])
        st_ref[PK_MLP_G * 8 - 64:PK_MLP_G * 8 - 63, :] += _colsum(dh2 * xh2)
        dxh2 = dh2 * g2v
        dx2 = dx3 + r2 * (dxh2 - xh2 * jnp.mean(dxh2 * xh2, axis=-1, keepdims=True))
        dx2_ref[...] = dx2
        dx2b = dx2.astype(BF16)
        dx2b_ref[...] = dx2b
        dy_ref[...] = _dot_nt(dx2b, wout[...])

    def tok(cols):
        return pl.BlockSpec((tm, cols), lambda i: (i, 0))

    def row(cols):
        return pl.BlockSpec((1, cols), lambda i: (0, 0))

    return pl.pallas_call(
        body, name="mlp_fwd_bwd", grid=(t // tm,),
        in_specs=[tok(d), tok(mix), row(d), row(d), tok(d), ANY, ANY, ANY],
        out_specs=[tok(ff), tok(ff), tok(d), tok(d), tok(d), tok(d), tok(mix),
                   pl.BlockSpec((3 * TILE_ROWS, d), lambda i: (0, 0))],
        out_shape=[jax.ShapeDtypeStruct((t, ff), BF16), jax.ShapeDtypeStruct((t, ff), BF16),
                   jax.ShapeDtypeStruct((t, d), BF16), jax.ShapeDtypeStruct((t, d), BF16),
                   jax.ShapeDtypeStruct((t, d), F32), jax.ShapeDtypeStruct((t, d), BF16),
                   jax.ShapeDtypeStruct((t, mix), F32), jax.ShapeDtypeStruct((3 * TILE_ROWS, d), F32)],
        scratch_shapes=[pltpu.VMEM(w_out_g.shape, BF16), pltpu.VMEM(w1_g.shape, BF16), pltpu.VMEM(w2_g.shape, BF16),
                        pltpu.VMEM((tm, ff), F32)],
        compiler_params=_params(dimension_semantics=("arbitrary",)),
    )(x, yb, g2, gf, target, w_out_g, w1_g, w2_g)


def _mix_bwd(dy, u, xr_all, hs_all, c3_all, conv_w, rconv_w, wa_bd, b_a, wx_bd, b_x, lam, g_nc, g_nr):
    t = dy.shape[0]
    tm = TOKEN_TILE
    nt = t // tm
    hb = tm // TILE_ROWS

    def body(dy_ref, u_ref, uh_ref, xr_ref, hs_ref, hh_ref, c3_ref, cw_ref, rw_ref, wa_ref, ba_ref, wx_ref, bx_ref,
             lam_ref, gnc_ref, gnr_ref, du_ref, st_ref, dwa_ref, dwx_ref, dc_next, a_next, gs_next, dxr_next):
        i = pl.program_id(0)

        @pl.when(i == 0)
        def _():
            dc_next[...] = jnp.zeros_like(dc_next)
            a_next[...] = jnp.zeros_like(a_next)
            gs_next[...] = jnp.zeros_like(gs_next)
            dxr_next[...] = jnp.zeros_like(dxr_next)
            st_ref[...] = jnp.zeros_like(st_ref)
            dwa_ref[...] = jnp.zeros_like(dwa_ref)
            dwx_ref[...] = jnp.zeros_like(dwx_ref)

        first_tile = i == nt - 1
        gate_b = u_ref[:, 0:CONV_W]
        gate_c = u_ref[:, CONV_W:2 * CONV_W]
        v = u_ref[:, 2 * CONV_W:3 * CONV_W]
        x_r = u_ref[:, 3 * CONV_W:3 * CONV_W + LRU_W]
        g = u_ref[:, 3 * CONV_W + LRU_W:]
        cv = gate_c * v
        cv_prev = jnp.where(first_tile, 0.0, uh_ref[:, CONV_W:2 * CONV_W] * uh_ref[:, 2 * CONV_W:3 * CONV_W])
        xin_prev = jnp.where(first_tile, 0.0, uh_ref[:, 3 * CONV_W:3 * CONV_W + LRU_W])
        hs_prev = jnp.where(first_tile, 0.0, hh_ref[...])

        def acc(block, val, width=LRU_W, row=0):
            r0 = block * TILE_ROWS + row
            st_ref[r0:r0 + 1, 0:width] += val

        conv3 = c3_ref[...]
        y_conv = gate_b * conv3
        ra = lax.rsqrt(jnp.mean(y_conv * y_conv, axis=-1, keepdims=True) + EPS)
        xha = y_conv * ra
        dna = dy_ref[:, :CONV_W]
        acc(PK_G_NORM_CONV, _colsum(dna * xha), CONV_W)
        dxha = dna * gnc_ref[...]
        dy_conv = ra * (dxha - xha * jnp.mean(dxha * xha, axis=-1, keepdims=True))
        du_ref[:, 0:CONV_W] = (dy_conv * conv3).astype(BF16)
        dc = dy_conv * gate_b
        cw = cw_ref[...]
        dcn = dc_next[...]
        dcv = cw[2:3] * dc + cw[1:2] * _shift_up(dc, 1, dcn) + cw[0:1] * _shift_up(dc, 2, dcn)
        dc_next[...] = dc[:TILE_ROWS]
        acc(PK_CONV_W, _colsum(dc * _shift_down(cv, 2, cv_prev)), CONV_W, 0)
        acc(PK_CONV_W, _colsum(dc * _shift_down(cv, 1, cv_prev)), CONV_W, 1)
        acc(PK_CONV_W, _colsum(dc * cv), CONV_W, 2)
        du_ref[:, CONV_W:2 * CONV_W] = (dcv * v).astype(BF16)
        du_ref[:, 2 * CONV_W:3 * CONV_W] = (dcv * gate_c).astype(BF16)

        hs = hs_ref[...]
        gelu, dgelu = _gelu_and_grad(g)
        y_rnn = hs * gelu
        rb = lax.rsqrt(jnp.mean(y_rnn * y_rnn, axis=-1, keepdims=True) + EPS)
        xhb = y_rnn * rb
        dnb = dy_ref[:, CONV_W:]
        acc(PK_G_NORM_RNN, _colsum(dnb * xhb))
        dxhb = dnb * gnr_ref[...]
        dy_rnn = rb * (dxhb - xhb * jnp.mean(dxhb * xhb, axis=-1, keepdims=True))
        du_ref[:, 3 * CONV_W + LRU_W:] = (dy_rnn * hs * dgelu).astype(BF16)
        dh = dy_rnn * gelu

        xr = xr_ref[...]
        sp, dsp = _softplus_neg(lam_ref[...])
        xrb, r, ig, a, mult = _lru_gates(xr, wa_ref, ba_ref[...], wx_ref, bx_ref[...], sp)
        a_up = _shift_up(a, 1, a_next[...])
        a_next[...] = a[:TILE_ROWS]
        a_cum, gs = _scan_up(a_up, dh)
        gs = gs + a_cum * gs_next[0:1, :]
        gs_next[...] = gs[:TILE_ROWS]
        da = gs * _shift_down(hs, 1, hs_prev)
        gx = gs * xr
        di = gx * mult
        dmult = gx * ig
        dxr = gs * (mult * ig)
        dlog_a = da * a - dmult * ((a * a) / mult)
        acc(PK_LAMBDA, _colsum(dlog_a * r) * ((-LRU_C) * dsp))
        dpa = (dlog_a * ((-LRU_C) * sp)) * (r * (1.0 - r))
        dpx = di * (ig * (1.0 - ig))
        acc(PK_B_A, _colsum(dpa))
        acc(PK_B_X, _colsum(dpx))
        dpab = dpa.astype(BF16)
        dpxb = dpx.astype(BF16)
        dxr = dxr + _block_diag_dot_t(dpab, wa_ref) + _block_diag_dot_t(dpxb, wx_ref)
        for j in range(N_BD):
            cols = slice(j * BD, (j + 1) * BD)
            dwa_ref[j] += _dot_tn(xrb[:, cols], dpab[:, cols])
            dwx_ref[j] += _dot_tn(xrb[:, cols], dpxb[:, cols])

        acc(PK_RCONV_B, _colsum(dxr))
        rw = rw_ref[...]
        dxn = dxr_next[...]
        dx_r = (rw[3:4] * dxr + rw[2:3] * _shift_up(dxr, 1, dxn) + rw[1:2] * _shift_up(dxr, 2, dxn)
                + rw[0:1] * _shift_up(dxr, 3, dxn))
        dxr_next[...] = dxr[:TILE_ROWS]
        for k in range(3):
            acc(PK_RCONV_W, _colsum(dxr * _shift_down(x_r, 3 - k, xin_prev)), LRU_W, k)
        acc(PK_RCONV_W, _colsum(dxr * x_r), LRU_W, 3)
        du_ref[:, 3 * CONV_W:3 * CONV_W + LRU_W] = dx_r.astype(BF16)

    def full(a):
        nd = a.ndim
        return pl.BlockSpec(a.shape, lambda i: (0,) * nd)

    def tok(cols):
        return pl.BlockSpec((tm, cols), lambda i: (nt - 1 - i, 0))

    def halo(cols):
        return pl.BlockSpec((TILE_ROWS, cols), lambda i: (jnp.maximum((nt - 1 - i) * hb - 1, 0), 0))

    smalls = (conv_w, rconv_w, wa_bd, b_a, wx_bd, b_x, lam, g_nc, g_nr)
    return pl.pallas_call(
        body, name="mix_bwd", grid=(nt,),
        in_specs=[tok(CONV_W + LRU_W), tok(IN_COLS), halo(IN_COLS), tok(LRU_W), tok(LRU_W), halo(LRU_W), tok(CONV_W)]
        + [full(a) for a in smalls],
        out_specs=[tok(IN_COLS), pl.BlockSpec((8 * TILE_ROWS, LRU_W), lambda i: (0, 0)),
                   pl.BlockSpec((N_BD, BD, BD), lambda i: (0, 0, 0)), pl.BlockSpec((N_BD, BD, BD), lambda i: (0, 0, 0))],
        out_shape=[jax.ShapeDtypeStruct((t, IN_COLS), BF16), jax.ShapeDtypeStruct((8 * TILE_ROWS, LRU_W), F32),
                   jax.ShapeDtypeStruct((N_BD, BD, BD), F32), jax.ShapeDtypeStruct((N_BD, BD, BD), F32)],
        scratch_shapes=[pltpu.VMEM((TILE_ROWS, CONV_W), F32), pltpu.VMEM((TILE_ROWS, LRU_W), F32),
                        pltpu.VMEM((TILE_ROWS, LRU_W), F32), pltpu.VMEM((TILE_ROWS, LRU_W), F32)],
        compiler_params=_params(dimension_semantics=("arbitrary",)),
    )(dy, u, u, xr_all, hs_all, hs_all, c3_all, *smalls)


def _in_bwd(dub, w_in_g, x, dx2, g1):
    t, d = x.shape
    tm = TOKEN_TILE

    def body(du_ref, win_ref, x_ref, dx2_ref, g1_ref, gx_ref, st_ref):
        @pl.when(pl.program_id(0) == 0)
        def _():
            st_ref[...] = jnp.zeros_like(st_ref)

        dh1 = _dot_nt(du_ref[:, 0:IN_SHARD], win_ref[0])
        for j in range(1, N_CHIPS):
            dh1 = dh1 + _dot_nt(du_ref[:, j * IN_SHARD:(j + 1) * IN_SHARD], win_ref[j])
        xv = x_ref[...]
        rstd = lax.rsqrt(jnp.mean(xv * xv, axis=-1, keepdims=True) + EPS)
        xh = xv * rstd
        st_ref[0:1, :] += _colsum(dh1 * xh)
        dxh = dh1 * g1_ref[...]
        gx_ref[...] = dx2_ref[...] + rstd * (dxh - xh * jnp.mean(dxh * xh, axis=-1, keepdims=True))

    def tok(cols):
        return pl.BlockSpec((tm, cols), lambda i: (i, 0))

    return pl.pallas_call(
        body, name="in_bwd", grid=(t // tm,),
        in_specs=[tok(IN_COLS), pl.BlockSpec(w_in_g.shape, lambda i: (0, 0, 0)), tok(d), tok(d),
                  pl.BlockSpec((1, d), lambda i: (0, 0))],
        out_specs=[tok(d), pl.BlockSpec((TILE_ROWS, d), lambda i: (0, 0))],
        out_shape=[jax.ShapeDtypeStruct((t, d), F32), jax.ShapeDtypeStruct((TILE_ROWS, d), F32)],
        compiler_params=_params(dimension_semantics=("arbitrary",)),
    )(dub, w_in_g, x, dx2, g1)


WGRAD_GEOMETRY = {
    "in": (512, IN_SHARD, lambda s, h: h, lambda s, h: s),
    "mlp_in": (512, D_MODEL, lambda s, h: h, lambda s, h: s),
    "mlp_out": (512, D_MODEL, lambda s, h: 2 * s + h, lambda s, h: 0),
    "out": (384, 512, lambda s, h: s, lambda s, h: h),
}
K_CHUNK = 512


def _sibling():
    x, y, c = _position()
    return (x, y, 1 - c)


def _wgrad(a, b, tag, core_chip):
    t = a.shape[0]
    pr, pc, a_blk, b_blk = WGRAD_GEOMETRY[tag]
    nk = t // K_CHUNK
    mine = N_CHIPS

    def body(cc_ref, a_ref, b_ref, land_ref, p_ref, pb_ref, stage, rbuf, send_sems, recv_sems, rsem):
        ph, s = pl.program_id(0), pl.program_id(1)
        slot = jnp.where(ph == 0, s, mine)
        acc = stage.at[slot]
        acc[...] = _dot_tn(a_ref[0:K_CHUNK, :], b_ref[0:K_CHUNK, :])
        for k in range(1, nk):
            acc[...] += _dot_tn(a_ref[k * K_CHUNK:(k + 1) * K_CHUNK, :], b_ref[k * K_CHUNK:(k + 1) * K_CHUNK, :])

        def push(k):
            return pltpu.make_async_remote_copy(src_ref=stage.at[k], dst_ref=land_ref.at[k], send_sem=send_sems.at[k],
                                                recv_sem=recv_sems.at[k], device_id=_sibling(), device_id_type=MESH)

        @pl.when(ph == 0)
        def _():
            push(s).start()

        @pl.when(ph == 1)
        def _():
            push(s).wait_recv()
            landed = pltpu.make_async_copy(land_ref.at[s], rbuf, rsem)
            landed.start()
            landed.wait()
            p = stage[mine] + rbuf[...]
            p_ref[0] = p
            pb_ref[0] = p.astype(BF16)

        @pl.when((ph == 1) & (s == N_CHIPS - 1))
        def _():
            for k in range(N_CHIPS):
                push(k).wait_send()

    def half(ph, cc):
        return jnp.where(ph == 0, 1 - cc[0], cc[0])

    def out_slot(ph, s, cc):
        return (jnp.where(ph == 0, 0, s), 0, 0)

    piece = jax.ShapeDtypeStruct((N_CHIPS, pr, pc), F32)
    return pl.pallas_call(
        body, name="wgrad_" + tag,
        grid_spec=pltpu.PrefetchScalarGridSpec(
            num_scalar_prefetch=1, grid=(2, N_CHIPS),
            in_specs=[pl.BlockSpec((t, pr), lambda ph, s, cc: (0, a_blk(s, half(ph, cc)))),
                      pl.BlockSpec((t, pc), lambda ph, s, cc: (0, b_blk(s, half(ph, cc))))],
            out_specs=[ANY, pl.BlockSpec((1, pr, pc), out_slot), pl.BlockSpec((1, pr, pc), out_slot)],
            scratch_shapes=[pltpu.VMEM((N_CHIPS + 1, pr, pc), F32), pltpu.VMEM((pr, pc), F32),
                            pltpu.SemaphoreType.DMA((N_CHIPS,)), pltpu.SemaphoreType.DMA((N_CHIPS,)), pltpu.SemaphoreType.DMA]),
        out_shape=[piece, piece, jax.ShapeDtypeStruct((N_CHIPS, pr, pc), BF16)],
        compiler_params=_params(dimension_semantics=("arbitrary", "arbitrary")),
    )(core_chip, a, b)[1:]


def _other_chips(x, y):
    return [(1 - x, y), (x, 1 - y), (1 - x, 1 - y)]


def _gather_weights(w_in, w_out, w1, w2, small):
    bigs = (w_in, w_out, w1, w2)
    nb = len(bigs)

    def body(win_ref, wout_ref, w1_ref, w2_ref, sm_ref, gin, gout, g1, g2, gsm, st_in, st_out, st_1, st_2,
             send_sems, recv_sems, local_sems):
        x, y, c = _position()
        j = 2 * x + y
        sibling = (x, y, 1 - c)
        chips = _other_chips(x, y)
        srcs = (win_ref, wout_ref, w1_ref, w2_ref)
        stages = (st_in, st_out, st_1, st_2)
        outs = (gin, gout, g1, g2)
        for src, st in zip(srcs, stages):
            st[...] = src[...].astype(BF16)

        def half(ref, w, which):
            hr = bigs[w].shape[0] // 2
            return ref.at[pl.ds(which * hr, hr), :]

        def remote(src, dst, w, k, to):
            return pltpu.make_async_remote_copy(src_ref=src, dst_ref=dst, send_sem=send_sems.at[w, k],
                                                recv_sem=recv_sems.at[w, k], device_id=to, device_id_type=MESH)

        local = [pltpu.make_async_copy(stages[w], outs[w].at[j], local_sems.at[w]) for w in range(nb)]
        local.append(pltpu.make_async_copy(sm_ref, gsm.at[j], local_sems.at[nb]))
        for cp in local:
            cp.start()
        sends = []
        for w in range(nb):
            for k, (px, py) in enumerate(chips):
                sends.append(remote(half(stages[w], w, c), half(outs[w].at[j], w, c), w, k, (px, py, c)))
        for k, (px, py) in enumerate(chips):
            sends.append(remote(sm_ref, gsm.at[j], nb, k, (px, py, c)))
        for cp in sends:
            cp.start()
        for w in range(nb):
            for k, (px, py) in enumerate(chips):
                jj = 2 * px + py
                landed = half(outs[w].at[jj], w, c)
                remote(landed, landed, w, k, (px, py, c)).wait_recv()
                fwd = remote(landed, landed, w, 3 + k, sibling)
                fwd.start()
                sends.append(fwd)
        for k, (px, py) in enumerate(chips):
            jj = 2 * px + py
            remote(sm_ref, gsm.at[jj], nb, k, (px, py, c)).wait_recv()
        for w in range(nb):
            for k, (px, py) in enumerate(chips):
                jj = 2 * px + py
                other = half(outs[w].at[jj], w, 1 - c)
                remote(other, other, w, 3 + k, sibling).wait_recv()
        for cp in sends:
            cp.wait_send()
        for cp in local:
            cp.wait()

    def gathered(a, dtype):
        return jax.ShapeDtypeStruct((N_CHIPS,) + a.shape, dtype)

    return pl.pallas_call(
        body, name="gather_weights",
        in_specs=[VMEM] * 5, out_specs=[ANY] * 5,
        out_shape=[gathered(a, BF16) for a in bigs] + [gathered(small, F32)],
        scratch_shapes=[pltpu.VMEM(a.shape, BF16) for a in bigs]
        + [pltpu.SemaphoreType.DMA((nb + 1, 6)), pltpu.SemaphoreType.DMA((nb + 1, 6)), pltpu.SemaphoreType.DMA((nb + 1,))],
        compiler_params=_params(),
    )(*bigs, small)


def _exchange_partials(parts):
    n = len(parts)

    def body(*refs):
        p_refs, r_refs, send_sems, recv_sems = refs[:n], refs[n:2 * n], refs[2 * n], refs[2 * n + 1]
        x, y, c = _position()
        j = 2 * x + y
        copies = []
        for w in range(n):
            for k, (px, py) in enumerate(_other_chips(x, y)):
                copies.append(pltpu.make_async_remote_copy(
                    src_ref=p_refs[w].at[2 * px + py], dst_ref=r_refs[w].at[j],
                    send_sem=send_sems.at[w, k], recv_sem=recv_sems.at[w, k], device_id=(px, py, c), device_id_type=MESH))
        for cp in copies:
            cp.start()
        for w in range(n):
            for k, (px, py) in enumerate(_other_chips(x, y)):
                jj = 2 * px + py
                pltpu.make_async_remote_copy(
                    src_ref=p_refs[w].at[jj], dst_ref=r_refs[w].at[jj],
                    send_sem=send_sems.at[w, k], recv_sem=recv_sems.at[w, k], device_id=(px, py, c), device_id_type=MESH).wait()

    return pl.pallas_call(
        body, name="grad_exchange_partials", in_specs=[ANY] * n, out_specs=[ANY] * n,
        out_shape=[jax.ShapeDtypeStruct(p.shape, BF16) for p in parts],
        scratch_shapes=[pltpu.SemaphoreType.DMA((n, 3)), pltpu.SemaphoreType.DMA((n, 3))],
        compiler_params=_params(),
    )(*parts)


def _adamw(w, g, m, v):
    m = ADAM_B1 * m + (1.0 - ADAM_B1) * g
    v = ADAM_B2 * v + (1.0 - ADAM_B2) * (g * g)
    m_hat = m / ADAM_BC1
    v_hat = v / ADAM_BC2
    delta = -ADAM_LR * (m_hat / (jnp.sqrt(v_hat) + ADAM_EPS) + ADAM_WD * w)
    return delta, m, v


FINISH_SUB = 4


def _finish(tag, w, m, v, part, arrived, core_chip):
    pr, pc = WGRAD_GEOMETRY[tag][:2]
    rb = pr // FINISH_SUB
    by_rows = w.shape[1] == pc

    def body(cc_ref, p_ref, r1_ref, r2_ref, r3_ref, w_ref, m_ref, v_ref, g_ref, d_ref, nm_ref, nv_ref,
             stage, landing, send_sems, recv_sems):
        ph, i = pl.program_id(0), pl.program_id(1)

        def push(k):
            return pltpu.make_async_remote_copy(src_ref=stage.at[k], dst_ref=landing.at[k], send_sem=send_sems.at[k],
                                                recv_sem=recv_sems.at[k], device_id=_sibling(), device_id_type=MESH)

        def update(g):
            g_ref[...] = g
            d_ref[...], nm_ref[...], nv_ref[...] = _adamw(w_ref[...], g, m_ref[...], v_ref[...])

        @pl.when(ph == 0)
        def _():
            g = ((p_ref[0] + r1_ref[0].astype(F32)) + r2_ref[0].astype(F32)) + r3_ref[0].astype(F32)
            stage[i] = g
            push(i).start()
            update(g)

        @pl.when(ph == 1)
        def _():
            push(i).wait_recv()
            update(landing[i])

        @pl.when((ph == 1) & (i == FINISH_SUB - 1))
        def _():
            for k in range(FINISH_SUB):
                push(k).wait_send()

    def sub(ph, i):
        return jnp.where(ph == 0, i, FINISH_SUB - 1)

    def partial(off):
        return pl.BlockSpec((1, rb, pc), lambda ph, i, cc: ((cc[1] + off) % N_CHIPS, sub(ph, i), 0))

    def window(ph, i, cc):
        h = jnp.where(ph == 0, cc[0], 1 - cc[0])
        return (h * FINISH_SUB + i, 0) if by_rows else (i, h)

    shard = pl.BlockSpec((rb, pc), window)
    return pl.pallas_call(
        body, name="finish_" + tag,
        grid_spec=pltpu.PrefetchScalarGridSpec(
            num_scalar_prefetch=1, grid=(2, FINISH_SUB),
            in_specs=[partial(0), partial(1), partial(2), partial(3), shard, shard, shard],
            out_specs=[shard] * 4,
            scratch_shapes=[pltpu.VMEM((FINISH_SUB, rb, pc), F32), pltpu.VMEM((FINISH_SUB, rb, pc), F32),
                            pltpu.SemaphoreType.DMA((FINISH_SUB,)), pltpu.SemaphoreType.DMA((FINISH_SUB,))]),
        out_shape=[jax.ShapeDtypeStruct(w.shape, F32)] * 4,
        compiler_params=_params(dimension_semantics=("arbitrary", "arbitrary")),
    )(core_chip, part, arrived, arrived, arrived, w, m, v)


def _small_step(pack, w_pack, m_pack, v_pack, conv_wmv, rconv_wmv):
    n_dev = 8
    rows, cols = pack.shape
    cshard = conv_wmv.shape[2]
    rshard = rconv_wmv.shape[2]

    def body(p_ref, w_ref, m_ref, v_ref, cw_ref, rw_ref, g_ref, d_ref, nm_ref, nv_ref, co_ref, ro_ref,
             all_ref, send_sems, recv_sems, local_sem):
        x, y, c = _position()
        me, sibling = (x, y, c), (x, y, 1 - c)
        chips = _other_chips(x, y)

        def slot(px, py, pc):
            return all_ref.at[4 * px + 2 * py + pc]

        def copy(k, block, to, src=None):
            return pltpu.make_async_remote_copy(
                src_ref=slot(*block) if src is None else src, dst_ref=slot(*block),
                send_sem=send_sems.at[k], recv_sem=recv_sems.at[k], device_id=to, device_id_type=MESH)

        mine = pltpu.make_async_copy(p_ref, slot(*me), local_sem)
        mine.start()
        first = [copy(0, me, sibling, src=p_ref)]
        first += [copy(1 + k, me, (*chip, c), src=p_ref) for k, chip in enumerate(chips)]
        for cp in first:
            cp.start()
        passed = [copy(4 + k, (*chip, c), sibling) for k, chip in enumerate(chips)]
        for k, chip in enumerate(chips):
            copy(1 + k, (*chip, c), me).wait_recv()
            passed[k].start()
        copy(0, sibling, me).wait_recv()
        for k, chip in enumerate(chips):
            copy(4 + k, (*chip, 1 - c), me).wait_recv()
        for cp in first + passed:
            cp.wait_send()
        mine.wait()

        total = all_ref[0]
        for k in range(1, n_dev):
            total = total + all_ref[k]
        g_ref[...] = total
        d_ref[...], nm_ref[...], nv_ref[...] = _adamw(w_ref[...], total, m_ref[...], v_ref[...])

        j = 2 * x + y
        cblk = total[PK_CONV_W * 8:PK_CONV_W * 8 + 8, :]
        rblk = total[PK_RCONV_W * 8:PK_RCONV_W * 8 + 8, :]
        cg = cblk[:, 0:cshard]
        rg = rblk[:, 0:rshard]
        for k in range(1, N_CHIPS):
            cg = jnp.where(j == k, cblk[:, k * cshard:(k + 1) * cshard], cg)
            rg = jnp.where(j == k, rblk[:, k * rshard:(k + 1) * rshard], rg)
        co_ref[0] = cg
        co_ref[1], co_ref[2], co_ref[3] = _adamw(cw_ref[0], cg, cw_ref[1], cw_ref[2])
        ro_ref[0] = rg
        ro_ref[1], ro_ref[2], ro_ref[3] = _adamw(rw_ref[0], rg, rw_ref[1], rw_ref[2])

    packs = [jax.ShapeDtypeStruct((rows, cols), F32)] * 4
    return pl.pallas_call(
        body, name="small_grads_step", in_specs=[VMEM] * 6, out_specs=[VMEM] * 6,
        out_shape=packs + [jax.ShapeDtypeStruct((4, TILE_ROWS, cshard), F32), jax.ShapeDtypeStruct((4, TILE_ROWS, rshard), F32)],
        scratch_shapes=[pltpu.VMEM((n_dev, rows, cols), F32), pltpu.SemaphoreType.DMA((7,)), pltpu.SemaphoreType.DMA((7,)),
                        pltpu.SemaphoreType.DMA],
        compiler_params=_params(),
    )(pack, w_pack, m_pack, v_pack, conv_wmv, rconv_wmv)


def _blk(a):
    a = a.reshape(-1, a.shape[-1])
    return jnp.pad(a, ((0, TILE_ROWS - a.shape[0]), (0, D_MODEL - a.shape[1])))


def _zero_blk():
    return jnp.zeros((TILE_ROWS, D_MODEL), F32)


def _pack_params(p, pre):
    get = lambda n: p[pre + n]
    return jnp.concatenate([
        _blk(get("g_norm_rnn")), _blk(get("rnn_conv_b")), _blk(get("b_a")), _blk(get("b_x")), _blk(get("lru_lambda")),
        _zero_blk(), _zero_blk(), _blk(get("g_norm_conv")), _blk(get("final_norm_g").reshape(1, -1)), _blk(get("norm_mlp_g")),
        _zero_blk(), _blk(get("norm_mix_g")), get("w_a").reshape(64, D_MODEL), get("w_x").reshape(64, D_MODEL)], axis=0)


def _to_block_diag(w):
    w4 = w.reshape(N_BD, 4, 64, 64)
    eye = jnp.eye(4, dtype=w.dtype)
    return (w4[:, :, :, None, :] * eye[None, :, None, :, None]).reshape(N_BD, BD, BD)


def _from_block_diag(d):
    d5 = d.reshape(N_BD, 4, 64, 4, 64)
    return jnp.stack([d5[:, q, :, q, :] for q in range(4)], axis=1).reshape(64, D_MODEL)


def _pad_rows(a):
    return jnp.pad(a, ((0, TILE_ROWS - a.shape[0]), (0, 0)))


_NAMES = ['norm_mix_g', 'w_in', 'conv_w', 'rnn_conv_w', 'rnn_conv_b', 'w_a', 'b_a', 'w_x', 'b_x', 'lru_lambda',
          'g_norm_conv', 'g_norm_rnn', 'w_out', 'norm_mlp_g', 'w_mlp_in', 'w_mlp_out', 'final_norm_g']


def kernel(x, norm_mix_g, w_in, conv_w, rnn_conv_w, rnn_conv_b, w_a, b_a, w_x, b_x, lru_lambda, g_norm_conv, g_norm_rnn, w_out, norm_mlp_g, w_mlp_in, w_mlp_out, final_norm_g, loss_target, m_norm_mix_g, m_w_in, m_conv_w, m_rnn_conv_w, m_rnn_conv_b, m_w_a, m_b_a, m_w_x, m_b_x, m_lru_lambda, m_g_norm_conv, m_g_norm_rnn, m_w_out, m_norm_mlp_g, m_w_mlp_in, m_w_mlp_out, m_final_norm_g, v_norm_mix_g, v_w_in, v_conv_w, v_rnn_conv_w, v_rnn_conv_b, v_w_a, v_b_a, v_w_x, v_b_x, v_lru_lambda, v_g_norm_conv, v_g_norm_rnn, v_w_out, v_norm_mlp_g, v_w_mlp_in, v_w_mlp_out, v_final_norm_g):
    args = dict(locals())
    p = {}
    for n in _NAMES:
        for pre in ("", "m_", "v_"):
            a = args[pre + n]
            p[pre + n] = a[0] if a.ndim >= 3 else a
    xs = x[0]
    target = loss_target[0]
    core_chip = jnp.stack([lax.axis_index("c"), 2 * lax.axis_index("x") + lax.axis_index("y")]).astype(jnp.int32)
    cshard = p["conv_w"].shape[1]
    rshard = p["rnn_conv_w"].shape[1]

    small = jnp.concatenate([_pad_rows(p["conv_w"]), _pad_rows(p["rnn_conv_w"])], axis=1)
    w_in_g, w_out_g, w1_g, w2_g, small_g = _gather_weights(p["w_in"], p["w_out"], p["w_mlp_in"], p["w_mlp_out"], small)
    w_out_g = w_out_g.reshape(-1, D_MODEL)
    w2_g = w2_g.reshape(-1, D_MODEL)
    conv_full = small_g[:, :3, :cshard].transpose(1, 0, 2).reshape(3, CONV_W)
    rconv_full = small_g[:, :4, cshard:].transpose(1, 0, 2).reshape(4, LRU_W)
    wa_bd = _to_block_diag(p["w_a"]).astype(BF16)
    wx_bd = _to_block_diag(p["w_x"]).astype(BF16)
    gf = p["final_norm_g"].reshape(1, -1)
    lru = (wa_bd, p["b_a"], wx_bd, p["b_x"], p["lru_lambda"], p["g_norm_conv"], p["g_norm_rnn"])

    u, h1b, xr, hs, c3, yb = _fwd_mix(xs, p["norm_mix_g"], w_in_g, conv_full, rconv_full, p["rnn_conv_b"], *lru)
    zb, dpb, h2b, dx3b, dx2, dx2b, dy, st_mlp = _mlp_fwd_bwd(xs, yb, w_out_g, w1_g, w2_g, p["norm_mlp_g"], gf, target)
    dub, st_mix, dwa_bd, dwx_bd = _mix_bwd(dy, u, xr, hs, c3, conv_full, rconv_full, *lru)
    grad_x, st_in = _in_bwd(dub, w_in_g, xs, dx2, p["norm_mix_g"])
    tags = ("in", "out", "mlp_in", "mlp_out")
    partials = [_wgrad(h1b, dub, "in", core_chip), _wgrad(yb, dx2b, "out", core_chip),
                _wgrad(h2b, dpb, "mlp_in", core_chip), _wgrad(zb, dx3b, "mlp_out", core_chip)]
    arrived = _exchange_partials([pb for _, pb in partials])
    big = {}
    for n, (pf, _), rb, tag in zip(("w_in", "w_out", "w_mlp_in", "w_mlp_out"), partials, arrived, tags):
        big[n] = _finish(tag, p[n], p["m_" + n], p["v_" + n], pf, rb, core_chip)

    pack = jnp.concatenate([st_mix, st_mlp, st_in, _from_block_diag(dwa_bd), _from_block_diag(dwx_bd)], axis=0)
    conv_wmv = jnp.stack([_pad_rows(p[pre + "conv_w"]) for pre in ("", "m_", "v_")])
    rconv_wmv = jnp.stack([_pad_rows(p[pre + "rnn_conv_w"]) for pre in ("", "m_", "v_")])
    g_pack, d_pack, m_pack, v_pack, conv_out, rconv_out = _small_step(
        pack, _pack_params(p, ""), _pack_params(p, "m_"), _pack_params(p, "v_"), conv_wmv, rconv_wmv)

    def unpack(pk, kind):
        def vec(b, width=D_MODEL):
            return pk[b * 8:b * 8 + 1, :width]
        return {
            "norm_mix_g": vec(PK_MIX_G), "rnn_conv_b": vec(PK_RCONV_B), "b_a": vec(PK_B_A), "b_x": vec(PK_B_X),
            "lru_lambda": vec(PK_LAMBDA), "g_norm_conv": vec(PK_G_NORM_CONV, CONV_W), "g_norm_rnn": vec(PK_G_NORM_RNN),
            "norm_mlp_g": vec(PK_MLP_G), "final_norm_g": vec(PK_FINAL_G).reshape(-1),
            "w_a": pk[PK_W_A * 8:PK_W_A * 8 + 64].reshape(1, 16, 64, 64), "w_x": pk[PK_W_X * 8:PK_W_X * 8 + 64].reshape(1, 16, 64, 64),
            "conv_w": conv_out[kind, :3][None], "rnn_conv_w": rconv_out[kind, :4][None],
            "w_in": big["w_in"][kind][None], "w_out": big["w_out"][kind][None],
            "w_mlp_in": big["w_mlp_in"][kind][None], "w_mlp_out": big["w_mlp_out"][kind][None],
        }

    outs = [unpack(pk, kind) for kind, pk in enumerate((g_pack, d_pack, m_pack, v_pack))]
    for o in outs:
        for n in ("norm_mix_g", "rnn_conv_b", "b_a", "b_x", "lru_lambda", "g_norm_conv", "g_norm_rnn", "norm_mlp_g"):
            o[n] = o[n].reshape(1, -1)
    loss = g_pack[PK_LOSS * 8, 0]
    return (loss, grad_x[None], *[o[n] for o in outs for n in _NAMES])
```

```python
import functools
import math

import jax
import jax.numpy as jnp
from jax import lax
from jax.experimental import pallas as pl
from jax.experimental.pallas import tpu as pltpu

F32 = jnp.float32
BF16 = jnp.bfloat16
MESH = pl.DeviceIdType.MESH
ANY = pl.BlockSpec(memory_space=pl.ANY)
VMEM = pl.BlockSpec(memory_space=pltpu.VMEM)

EPS = 1e-6
LRU_C = 8.0
D_MODEL = 1024
CONV_W = 512
LRU_W = 1024
IN_COLS = 3 * CONV_W + 2 * LRU_W
IN_SHARD = IN_COLS // 4
N_CHIPS = 4
BD = 256
N_BD = LRU_W // BD

ADAM_LR = 0.001
ADAM_B1 = 0.9
ADAM_B2 = 0.999
ADAM_EPS = 1e-08
ADAM_WD = 0.01
ADAM_STEP = 10
ADAM_BC1 = 1.0 - ADAM_B1 ** ADAM_STEP
ADAM_BC2 = 1.0 - ADAM_B2 ** ADAM_STEP

TILE_ROWS = 8
TOKEN_TILE = 256
VMEM_LIMIT = 56 * 1024 * 1024

PK_G_NORM_RNN, PK_RCONV_B, PK_B_A, PK_B_X, PK_LAMBDA, PK_RCONV_W, PK_CONV_W, PK_G_NORM_CONV = range(8)
PK_FINAL_G, PK_MLP_G, PK_LOSS, PK_MIX_G = 8, 9, 10, 11
PK_W_A = 12
PK_W_X = 20
PK_BLOCKS = 28
PK_ROWS = PK_BLOCKS * TILE_ROWS


def _params(**kw):
    return pltpu.CompilerParams(vmem_limit_bytes=VMEM_LIMIT, **kw)


def _position():
    x, y, c = lax.axis_index("x"), lax.axis_index("y"), lax.axis_index("c")
    return x, y, c


def _sigmoid(v):
    return 1.0 / (1.0 + jnp.exp(-v))


def _one_minus_exp(v):
    series = -v * (1.0 + v * (0.5 + v * (1.0 / 6.0 + v * (1.0 / 24.0 + v * (1.0 / 120.0)))))
    return jnp.where(v > -0.1, series, 1.0 - jnp.exp(v))


_GELU_C = math.sqrt(2.0 / math.pi)
_GELU_K = 0.044715


def _gelu_and_grad(g):
    th = jnp.tanh(_GELU_C * (g + _GELU_K * g * g * g))
    gelu = 0.5 * g * (1.0 + th)
    dgelu = 0.5 * (1.0 + th) + 0.5 * g * (1.0 - th * th) * (_GELU_C * (1.0 + 3.0 * _GELU_K * g * g))
    return gelu, dgelu


def _rows(shape):
    return lax.broadcasted_iota(jnp.int32, shape, 0)


def _shift_down(v, k, prev8):
    rolled = pltpu.roll(v, k, 0)
    halo = pltpu.roll(prev8, k, 0)
    head = jnp.where(_rows(halo.shape) < k, halo, rolled[:TILE_ROWS])
    return jnp.concatenate([head, rolled[TILE_ROWS:]], axis=0)


def _shift_up(v, k, next8):
    n = v.shape[0]
    rolled = pltpu.roll(v, n - k, 0)
    halo = pltpu.roll(next8, TILE_ROWS - k, 0)
    tail = jnp.where(_rows(halo.shape) >= TILE_ROWS - k, halo, rolled[n - TILE_ROWS:])
    return jnp.concatenate([rolled[: n - TILE_ROWS], tail], axis=0)


def _scan_down(a, b):
    n = a.shape[0]
    row = _rows(a.shape)
    s = 1
    while s < n:
        keep = row >= s
        b = jnp.where(keep, a * pltpu.roll(b, s, 0) + b, b)
        a = jnp.where(keep, a * pltpu.roll(a, s, 0), a)
        s *= 2
    return a, b


def _scan_up(a, b):
    n = a.shape[0]
    row = _rows(a.shape)
    s = 1
    while s < n:
        keep = row < n - s
        b = jnp.where(keep, a * pltpu.roll(b, n - s, 0) + b, b)
        a = jnp.where(keep, a * pltpu.roll(a, n - s, 0), a)
        s *= 2
    return a, b


def _softplus_neg(lam):
    e = jnp.exp(-jnp.abs(lam))
    log1p_e = jnp.where(e < 1e-2, e * (1.0 - e * (0.5 - e * (1.0 / 3.0 - e * 0.25))), jnp.log(1.0 + e))
    sp = jnp.maximum(-lam, 0.0) + log1p_e
    dsp = -_sigmoid(-lam)
    return sp, dsp


def _block_diag_dot(vb, w_ref):
    return jnp.concatenate(
        [jnp.dot(vb[:, j * BD:(j + 1) * BD], w_ref[j], preferred_element_type=F32) for j in range(N_BD)], axis=1)


def _block_diag_dot_t(vb, w_ref):
    return jnp.concatenate(
        [lax.dot_general(vb[:, j * BD:(j + 1) * BD], w_ref[j], (((1,), (1,)), ((), ())), preferred_element_type=F32)
         for j in range(N_BD)], axis=1)


def _dot_nt(a, b):
    return lax.dot_general(a, b, (((1,), (1,)), ((), ())), preferred_element_type=F32)


def _dot_tn(a, b):
    return lax.dot_general(a, b, (((0,), (0,)), ((), ())), preferred_element_type=F32)


def _lru_gates(xr, wa_ref, ba, wx_ref, bx, sp):
    xrb = xr.astype(BF16)
    r = _sigmoid(_block_diag_dot(xrb, wa_ref) + ba)
    ig = _sigmoid(_block_diag_dot(xrb, wx_ref) + bx)
    log_a = (-LRU_C) * r * sp
    a = jnp.exp(log_a)
    mult = jnp.sqrt(_one_minus_exp(2.0 * log_a))
    return xrb, r, ig, a, mult


def _colsum(v):
    return jnp.sum(v, axis=0, keepdims=True)


def _fwd_mix(x, g1, w_in_g, conv_w, rconv_w, rconv_b, wa_bd, b_a, wx_bd, b_x, lam, g_nc, g_nr, later):
    t, d = x.shape
    tm = TOKEN_TILE
    nt = t // tm
    nl = len(later)
    hand_over_at = [min(nt - 1, (nt * (w + 1)) // nl) for w in range(nl)]

    def body(x_ref, g1_ref, win_ref, cw_ref, rw_ref, rb_ref, wa_ref, ba_ref, wx_ref, bx_ref, lam_ref, gnc_ref, gnr_ref,
             *rest):
        later_in, (u_ref, h1_ref, xr_ref, hs_ref, c3_ref, y_ref), rest = rest[:nl], rest[nl:nl + 6], rest[nl + 6:]
        later_out, (cv_prev, xin_prev, h_prev, send_sems, recv_sems) = rest[:nl], rest[nl:]
        del later_in
        step = pl.program_id(0)
        plan = _ShardGather(later_out, send_sems, recv_sems)

        @pl.when(step == 0)
        def _():
            cv_prev[...] = jnp.zeros_like(cv_prev)
            xin_prev[...] = jnp.zeros_like(xin_prev)
            h_prev[...] = jnp.zeros_like(h_prev)
            for w in range(nl):
                for k in range(3):
                    plan.ici(w, k).start()

        for w in range(nl):
            @pl.when(step == hand_over_at[w])
            def _(w=w):
                for k in range(3):
                    plan.landed(w, k).wait_recv()
                    plan.hand_over(w, k).start()

        xv = x_ref[...]
        rstd = lax.rsqrt(jnp.mean(xv * xv, axis=-1, keepdims=True) + EPS)
        h1b = ((xv * rstd) * g1_ref[...]).astype(BF16)
        h1_ref[...] = h1b
        for j in range(N_CHIPS):
            u_ref[:, j * IN_SHARD:(j + 1) * IN_SHARD] = jnp.dot(h1b, win_ref[j], preferred_element_type=F32)
        gate_b = u_ref[:, 0:CONV_W]
        cv = u_ref[:, CONV_W:2 * CONV_W] * u_ref[:, 2 * CONV_W:3 * CONV_W]
        x_r = u_ref[:, 3 * CONV_W:3 * CONV_W + LRU_W]
        g = u_ref[:, 3 * CONV_W + LRU_W:]

        cw = cw_ref[...]
        cvp = cv_prev[...]
        conv3 = cw[0:1] * _shift_down(cv, 2, cvp) + cw[1:2] * _shift_down(cv, 1, cvp) + cw[2:3] * cv
        cv_prev[...] = cv[tm - TILE_ROWS:]
        c3_ref[...] = conv3
        y_conv = gate_b * conv3

        rw = rw_ref[...]
        xp = xin_prev[...]
        xr = (rw[0:1] * _shift_down(x_r, 3, xp) + rw[1:2] * _shift_down(x_r, 2, xp)
              + rw[2:3] * _shift_down(x_r, 1, xp) + rw[3:4] * x_r) + rb_ref[...]
        xin_prev[...] = x_r[tm - TILE_ROWS:]
        xr_ref[...] = xr
        sp, _ = _softplus_neg(lam_ref[...])
        _, _, ig, a, mult = _lru_gates(xr, wa_ref, ba_ref[...], wx_ref, bx_ref[...], sp)
        a_cum, h = _scan_down(a, mult * (ig * xr))
        h = h + a_cum * h_prev[...]
        h_prev[...] = h[tm - 1:tm]
        hs_ref[...] = h
        gelu, _ = _gelu_and_grad(g)
        y_rnn = h * gelu

        na = y_conv * lax.rsqrt(jnp.mean(y_conv * y_conv, axis=-1, keepdims=True) + EPS) * gnc_ref[...]
        nb = y_rnn * lax.rsqrt(jnp.mean(y_rnn * y_rnn, axis=-1, keepdims=True) + EPS) * gnr_ref[...]
        y_ref[:, :CONV_W] = na.astype(BF16)
        y_ref[:, CONV_W:] = nb.astype(BF16)

        @pl.when(step == nt - 1)
        def _():
            for w in range(nl):
                for k in range(3):
                    plan.handed(w, k).wait_recv()
                    plan.ici(w, k).wait_send()
                    plan.hand_over(w, k).wait_send()

    def full(a):
        nd = a.ndim
        return pl.BlockSpec(a.shape, lambda i: (0,) * nd)

    def tok(cols):
        return pl.BlockSpec((tm, cols), lambda i: (i, 0))

    smalls = (g1, w_in_g, conv_w, rconv_w, rconv_b, wa_bd, b_a, wx_bd, b_x, lam, g_nc, g_nr)
    n_in = 1 + len(smalls)
    outs = pl.pallas_call(
        body, name="fwd_mix", grid=(nt,),
        in_specs=[tok(d)] + [full(a) for a in smalls] + [ANY] * nl,
        out_specs=[tok(IN_COLS), tok(d), tok(LRU_W), tok(LRU_W), tok(CONV_W), tok(CONV_W + LRU_W)] + [ANY] * nl,
        out_shape=[jax.ShapeDtypeStruct((t, IN_COLS), F32), jax.ShapeDtypeStruct((t, d), BF16),
                   jax.ShapeDtypeStruct((t, LRU_W), F32), jax.ShapeDtypeStruct((t, LRU_W), F32),
                   jax.ShapeDtypeStruct((t, CONV_W), F32), jax.ShapeDtypeStruct((t, CONV_W + LRU_W), BF16)]
        + [jax.ShapeDtypeStruct(a.shape, a.dtype) for a in later],
        input_output_aliases={n_in + w: 6 + w for w in range(nl)},
        scratch_shapes=[pltpu.VMEM((TILE_ROWS, CONV_W), F32), pltpu.VMEM((TILE_ROWS, LRU_W), F32),
                        pltpu.VMEM((1, LRU_W), F32), pltpu.SemaphoreType.DMA((nl, 6)), pltpu.SemaphoreType.DMA((nl, 6))],
        compiler_params=_params(dimension_semantics=("arbitrary",)),
    )(x, *smalls, *later)
    return outs[:6], outs[6:]


def _mlp_fwd_bwd(x, yb, w_out_g, w1_g, w2_g, g2, gf, target):
    t, d = x.shape
    tm = TOKEN_TILE
    ff = w2_g.shape[0]
    mix = w_out_g.shape[0]
    ffs = ff // N_CHIPS

    def body(x_ref, y_ref, g2_ref, gf_ref, tgt_ref, wout_hbm, w1_hbm, w2_hbm,
             z_ref, dp_ref, h2_ref, dx3b_ref, dx2_ref, dx2b_ref, dy_ref, st_ref, wout, w1, w2, p_ref):
        @pl.when(pl.program_id(0) == 0)
        def _():
            pltpu.sync_copy(wout_hbm, wout)
            pltpu.sync_copy(w1_hbm, w1)
            pltpu.sync_copy(w2_hbm, w2)
            st_ref[...] = jnp.zeros_like(st_ref)

        x2 = x_ref[...] + jnp.dot(y_ref[...], wout[...], preferred_element_type=F32)
        r2 = lax.rsqrt(jnp.mean(x2 * x2, axis=-1, keepdims=True) + EPS)
        xh2 = x2 * r2
        g2v = g2_ref[...]
        h2b = (xh2 * g2v).astype(BF16)
        h2_ref[...] = h2b
        for j in range(N_CHIPS):
            p_ref[:, j * ffs:(j + 1) * ffs] = jnp.dot(h2b, w1[j], preferred_element_type=F32)
        rp = jnp.maximum(p_ref[...], 0.0)
        zb = (rp * rp).astype(BF16)
        z_ref[...] = zb
        x3 = x2 + jnp.dot(zb, w2[...], preferred_element_type=F32)
        r3 = lax.rsqrt(jnp.mean(x3 * x3, axis=-1, keepdims=True) + EPS)
        xh3 = x3 * r3
        gfv = gf_ref[...]
        err = xh3 * gfv - tgt_ref[...]
        loss = (0.5 / d) * jnp.sum(err * err)
        dout = err * (1.0 / d)
        st_ref[PK_FINAL_G * 8 - 64:PK_FINAL_G * 8 - 63, :] += _colsum(dout * xh3)
        st_ref[PK_LOSS * 8 - 64:PK_LOSS * 8 - 63, :] += jnp.zeros((1, d), F32) + loss
        dxh3 = dout * gfv
        dx3 = r3 * (dxh3 - xh3 * jnp.mean(dxh3 * xh3, axis=-1, keepdims=True))
        dx3b = dx3.astype(BF16)
        dx3b_ref[...] = dx3b
        dpb = (_dot_nt(dx3b, w2[...]) * (2.0 * rp)).astype(BF16)
        dp_ref[...] = dpb
        dh2 = _dot_nt(dpb[:, 0:ffs], w1[0])
        for j in range(1, N_CHIPS):
            dh2 = dh2 + _dot_nt(dpb[:, j * ffs:(j + 1) * ffs], w1[j])
        st_ref[PK_MLP_G * 8 - 64:PK_MLP_G * 8 - 63, :] += _colsum(dh2 * xh2)
        dxh2 = dh2 * g2v
        dx2 = dx3 + r2 * (dxh2 - xh2 * jnp.mean(dxh2 * xh2, axis=-1, keepdims=True))
        dx2_ref[...] = dx2
        dx2b = dx2.astype(BF16)
        dx2b_ref[...] = dx2b
        dy_ref[...] = _dot_nt(dx2b, wout[...])

    def tok(cols):
        return pl.BlockSpec((tm, cols), lambda i: (i, 0))

    def row(cols):
        return pl.BlockSpec((1, cols), lambda i: (0, 0))

    return pl.pallas_call(
        body, name="mlp_fwd_bwd", grid=(t // tm,),
        in_specs=[tok(d), tok(mix), row(d), row(d), tok(d), ANY, ANY, ANY],
        out_specs=[tok(ff), tok(ff), tok(d), tok(d), tok(d), tok(d), tok(mix),
                   pl.BlockSpec((3 * TILE_ROWS, d), lambda i: (0, 0))],
        out_shape=[jax.ShapeDtypeStruct((t, ff), BF16), jax.ShapeDtypeStruct((t, ff), BF16),
                   jax.ShapeDtypeStruct((t, d), BF16), jax.ShapeDtypeStruct((t, d), BF16),
                   jax.ShapeDtypeStruct((t, d), F32), jax.ShapeDtypeStruct((t, d), BF16),
                   jax.ShapeDtypeStruct((t, mix), F32), jax.ShapeDtypeStruct((3 * TILE_ROWS, d), F32)],
        scratch_shapes=[pltpu.VMEM(w_out_g.shape, BF16), pltpu.VMEM(w1_g.shape, BF16), pltpu.VMEM(w2_g.shape, BF16),
                        pltpu.VMEM((tm, ff), F32)],
        compiler_params=_params(dimension_semantics=("arbitrary",)),
    )(x, yb, g2, gf, target, w_out_g, w1_g, w2_g)


def _mix_bwd(dy, u, xr_all, hs_all, c3_all, conv_w, rconv_w, wa_bd, b_a, wx_bd, b_x, lam, g_nc, g_nr, parts):
    t = dy.shape[0]
    tm = TOKEN_TILE
    nt = t // tm
    hb = tm // TILE_ROWS
    npart = len(parts)

    def body(dy_ref, u_ref, uh_ref, xr_ref, hs_ref, hh_ref, c3_ref, cw_ref, rw_ref, wa_ref, ba_ref, wx_ref, bx_ref,
             lam_ref, gnc_ref, gnr_ref, *rest):
        part_refs, (du_ref, st_ref, dwa_ref, dwx_ref), rest = rest[:npart], rest[npart:npart + 4], rest[npart + 4:]
        arrived_refs, (dc_next, a_next, gs_next, dxr_next, send_sems, recv_sems) = rest[:npart], rest[npart:]
        exchange = _PartialExchange(part_refs, arrived_refs, send_sems, recv_sems)
        i = pl.program_id(0)

        @pl.when(i == 0)
        def _():
            exchange.start()
            dc_next[...] = jnp.zeros_like(dc_next)
            a_next[...] = jnp.zeros_like(a_next)
            gs_next[...] = jnp.zeros_like(gs_next)
            dxr_next[...] = jnp.zeros_like(dxr_next)
            st_ref[...] = jnp.zeros_like(st_ref)
            dwa_ref[...] = jnp.zeros_like(dwa_ref)
            dwx_ref[...] = jnp.zeros_like(dwx_ref)

        first_tile = i == nt - 1
        gate_b = u_ref[:, 0:CONV_W]
        gate_c = u_ref[:, CONV_W:2 * CONV_W]
        v = u_ref[:, 2 * CONV_W:3 * CONV_W]
        x_r = u_ref[:, 3 * CONV_W:3 * CONV_W + LRU_W]
        g = u_ref[:, 3 * CONV_W + LRU_W:]
        cv = gate_c * v
        cv_prev = jnp.where(first_tile, 0.0, uh_ref[:, CONV_W:2 * CONV_W] * uh_ref[:, 2 * CONV_W:3 * CONV_W])
        xin_prev = jnp.where(first_tile, 0.0, uh_ref[:, 3 * CONV_W:3 * CONV_W + LRU_W])
        hs_prev = jnp.where(first_tile, 0.0, hh_ref[...])

        def acc(block, val, width=LRU_W, row=0):
            r0 = block * TILE_ROWS + row
            st_ref[r0:r0 + 1, 0:width] += val

        conv3 = c3_ref[...]
        y_conv = gate_b * conv3
        ra = lax.rsqrt(jnp.mean(y_conv * y_conv, axis=-1, keepdims=True) + EPS)
        xha = y_conv * ra
        dna = dy_ref[:, :CONV_W]
        acc(PK_G_NORM_CONV, _colsum(dna * xha), CONV_W)
        dxha = dna * gnc_ref[...]
        dy_conv = ra * (dxha - xha * jnp.mean(dxha * xha, axis=-1, keepdims=True))
        du_ref[:, 0:CONV_W] = (dy_conv * conv3).astype(BF16)
        dc = dy_conv * gate_b
        cw = cw_ref[...]
        dcn = dc_next[...]
        dcv = cw[2:3] * dc + cw[1:2] * _shift_up(dc, 1, dcn) + cw[0:1] * _shift_up(dc, 2, dcn)
        dc_next[...] = dc[:TILE_ROWS]
        acc(PK_CONV_W, _colsum(dc * _shift_down(cv, 2, cv_prev)), CONV_W, 0)
        acc(PK_CONV_W, _colsum(dc * _shift_down(cv, 1, cv_prev)), CONV_W, 1)
        acc(PK_CONV_W, _colsum(dc * cv), CONV_W, 2)
        du_ref[:, CONV_W:2 * CONV_W] = (dcv * v).astype(BF16)
        du_ref[:, 2 * CONV_W:3 * CONV_W] = (dcv * gate_c).astype(BF16)

        hs = hs_ref[...]
        gelu, dgelu = _gelu_and_grad(g)
        y_rnn = hs * gelu
        rb = lax.rsqrt(jnp.mean(y_rnn * y_rnn, axis=-1, keepdims=True) + EPS)
        xhb = y_rnn * rb
        dnb = dy_ref[:, CONV_W:]
        acc(PK_G_NORM_RNN, _colsum(dnb * xhb))
        dxhb = dnb * gnr_ref[...]
        dy_rnn = rb * (dxhb - xhb * jnp.mean(dxhb * xhb, axis=-1, keepdims=True))
        du_ref[:, 3 * CONV_W + LRU_W:] = (dy_rnn * hs * dgelu).astype(BF16)
        dh = dy_rnn * gelu

        xr = xr_ref[...]
        sp, dsp = _softplus_neg(lam_ref[...])
        xrb, r, ig, a, mult = _lru_gates(xr, wa_ref, ba_ref[...], wx_ref, bx_ref[...], sp)
        a_up = _shift_up(a, 1, a_next[...])
        a_next[...] = a[:TILE_ROWS]
        a_cum, gs = _scan_up(a_up, dh)
        gs = gs + a_cum * gs_next[0:1, :]
        gs_next[...] = gs[:TILE_ROWS]
        da = gs * _shift_down(hs, 1, hs_prev)
        gx = gs * xr
        di = gx * mult
        dmult = gx * ig
        dxr = gs * (mult * ig)
        dlog_a = da * a - dmult * ((a * a) / mult)
        acc(PK_LAMBDA, _colsum(dlog_a * r) * ((-LRU_C) * dsp))
        dpa = (dlog_a * ((-LRU_C) * sp)) * (r * (1.0 - r))
        dpx = di * (ig * (1.0 - ig))
        acc(PK_B_A, _colsum(dpa))
        acc(PK_B_X, _colsum(dpx))
        dpab = dpa.astype(BF16)
        dpxb = dpx.astype(BF16)
        dxr = dxr + _block_diag_dot_t(dpab, wa_ref) + _block_diag_dot_t(dpxb, wx_ref)
        for j in range(N_BD):
            cols = slice(j * BD, (j + 1) * BD)
            dwa_ref[j] += _dot_tn(xrb[:, cols], dpab[:, cols])
            dwx_ref[j] += _dot_tn(xrb[:, cols], dpxb[:, cols])

        acc(PK_RCONV_B, _colsum(dxr))
        rw = rw_ref[...]
        dxn = dxr_next[...]
        dx_r = (rw[3:4] * dxr + rw[2:3] * _shift_up(dxr, 1, dxn) + rw[1:2] * _shift_up(dxr, 2, dxn)
                + rw[0:1] * _shift_up(dxr, 3, dxn))
        dxr_next[...] = dxr[:TILE_ROWS]
        for k in range(3):
            acc(PK_RCONV_W, _colsum(dxr * _shift_down(x_r, 3 - k, xin_prev)), LRU_W, k)
        acc(PK_RCONV_W, _colsum(dxr * x_r), LRU_W, 3)
        du_ref[:, 3 * CONV_W:3 * CONV_W + LRU_W] = dx_r.astype(BF16)

        @pl.when(i == nt - 1)
        def _():
            exchange.wait()

    def full(a):
        nd = a.ndim
        return pl.BlockSpec(a.shape, lambda i: (0,) * nd)

    def tok(cols):
        return pl.BlockSpec((tm, cols), lambda i: (nt - 1 - i, 0))

    def halo(cols):
        return pl.BlockSpec((TILE_ROWS, cols), lambda i: (jnp.maximum((nt - 1 - i) * hb - 1, 0), 0))

    smalls = (conv_w, rconv_w, wa_bd, b_a, wx_bd, b_x, lam, g_nc, g_nr)
    outs = pl.pallas_call(
        body, name="mix_bwd", grid=(nt,),
        in_specs=[tok(CONV_W + LRU_W), tok(IN_COLS), halo(IN_COLS), tok(LRU_W), tok(LRU_W), halo(LRU_W), tok(CONV_W)]
        + [full(a) for a in smalls] + [ANY] * npart,
        out_specs=[tok(IN_COLS), pl.BlockSpec((8 * TILE_ROWS, LRU_W), lambda i: (0, 0)),
                   pl.BlockSpec((N_BD, BD, BD), lambda i: (0, 0, 0)), pl.BlockSpec((N_BD, BD, BD), lambda i: (0, 0, 0))]
        + [ANY] * npart,
        out_shape=[jax.ShapeDtypeStruct((t, IN_COLS), BF16), jax.ShapeDtypeStruct((8 * TILE_ROWS, LRU_W), F32),
                   jax.ShapeDtypeStruct((N_BD, BD, BD), F32), jax.ShapeDtypeStruct((N_BD, BD, BD), F32)]
        + [jax.ShapeDtypeStruct(a.shape, a.dtype) for a in parts],
        scratch_shapes=[pltpu.VMEM((TILE_ROWS, CONV_W), F32), pltpu.VMEM((TILE_ROWS, LRU_W), F32),
                        pltpu.VMEM((TILE_ROWS, LRU_W), F32), pltpu.VMEM((TILE_ROWS, LRU_W), F32),
                        pltpu.SemaphoreType.DMA((npart, 3)), pltpu.SemaphoreType.DMA((npart, 3))],
        compiler_params=_params(dimension_semantics=("arbitrary",)),
    )(dy, u, u, xr_all, hs_all, hs_all, c3_all, *smalls, *parts)
    return outs[:4], outs[4:]


def _in_bwd(dub, w_in_g, x, dx2, g1, parts):
    t, d = x.shape
    tm = TOKEN_TILE
    nt = t // tm
    npart = len(parts)

    def body(du_ref, win_ref, x_ref, dx2_ref, g1_ref, *rest):
        part_refs, (gx_ref, st_ref), rest = rest[:npart], rest[npart:npart + 2], rest[npart + 2:]
        arrived_refs, (send_sems, recv_sems) = rest[:npart], rest[npart:]
        exchange = _PartialExchange(part_refs, arrived_refs, send_sems, recv_sems)
        i = pl.program_id(0)

        @pl.when(i == 0)
        def _():
            exchange.start()
            st_ref[...] = jnp.zeros_like(st_ref)

        dh1 = _dot_nt(du_ref[:, 0:IN_SHARD], win_ref[0])
        for j in range(1, N_CHIPS):
            dh1 = dh1 + _dot_nt(du_ref[:, j * IN_SHARD:(j + 1) * IN_SHARD], win_ref[j])
        xv = x_ref[...]
        rstd = lax.rsqrt(jnp.mean(xv * xv, axis=-1, keepdims=True) + EPS)
        xh = xv * rstd
        st_ref[0:1, :] += _colsum(dh1 * xh)
        dxh = dh1 * g1_ref[...]
        gx_ref[...] = dx2_ref[...] + rstd * (dxh - xh * jnp.mean(dxh * xh, axis=-1, keepdims=True))

        @pl.when(i == nt - 1)
        def _():
            exchange.wait()

    def tok(cols):
        return pl.BlockSpec((tm, cols), lambda i: (i, 0))

    outs = pl.pallas_call(
        body, name="in_bwd", grid=(nt,),
        in_specs=[tok(IN_COLS), pl.BlockSpec(w_in_g.shape, lambda i: (0, 0, 0)), tok(d), tok(d),
                  pl.BlockSpec((1, d), lambda i: (0, 0))] + [ANY] * npart,
        out_specs=[tok(d), pl.BlockSpec((TILE_ROWS, d), lambda i: (0, 0))] + [ANY] * npart,
        out_shape=[jax.ShapeDtypeStruct((t, d), F32), jax.ShapeDtypeStruct((TILE_ROWS, d), F32)]
        + [jax.ShapeDtypeStruct(a.shape, a.dtype) for a in parts],
        scratch_shapes=[pltpu.SemaphoreType.DMA((npart, 3)), pltpu.SemaphoreType.DMA((npart, 3))],
        compiler_params=_params(dimension_semantics=("arbitrary",)),
    )(dub, w_in_g, x, dx2, g1, *parts)
    return outs[:2], outs[2:]


WGRAD_GEOMETRY = {
    "in": (512, IN_SHARD, lambda s, h: h, lambda s, h: s),
    "mlp_in": (512, D_MODEL, lambda s, h: h, lambda s, h: s),
    "mlp_out": (512, D_MODEL, lambda s, h: 2 * s + h, lambda s, h: 0),
    "out": (384, 512, lambda s, h: s, lambda s, h: h),
}
K_CHUNK = 512


def _sibling():
    x, y, c = _position()
    return (x, y, 1 - c)


def _wgrad(a, b, tag, core_chip):
    t = a.shape[0]
    pr, pc, a_blk, b_blk = WGRAD_GEOMETRY[tag]
    nk = t // K_CHUNK
    mine = N_CHIPS

    def body(cc_ref, a_ref, b_ref, land_ref, p_ref, pb_ref, stage, rbuf, send_sems, recv_sems, rsem):
        ph, s = pl.program_id(0), pl.program_id(1)
        slot = jnp.where(ph == 0, s, mine)
        acc = stage.at[slot]
        acc[...] = _dot_tn(a_ref[0:K_CHUNK, :], b_ref[0:K_CHUNK, :])
        for k in range(1, nk):
            acc[...] += _dot_tn(a_ref[k * K_CHUNK:(k + 1) * K_CHUNK, :], b_ref[k * K_CHUNK:(k + 1) * K_CHUNK, :])

        def push(k):
            return pltpu.make_async_remote_copy(src_ref=stage.at[k], dst_ref=land_ref.at[k], send_sem=send_sems.at[k],
                                                recv_sem=recv_sems.at[k], device_id=_sibling(), device_id_type=MESH)

        @pl.when(ph == 0)
        def _():
            push(s).start()

        @pl.when(ph == 1)
        def _():
            push(s).wait_recv()
            landed = pltpu.make_async_copy(land_ref.at[s], rbuf, rsem)
            landed.start()
            landed.wait()
            p = stage[mine] + rbuf[...]
            p_ref[0] = p
            pb_ref[0] = p.astype(BF16)

        @pl.when((ph == 1) & (s == N_CHIPS - 1))
        def _():
            for k in range(N_CHIPS):
                push(k).wait_send()

    def half(ph, cc):
        return jnp.where(ph == 0, 1 - cc[0], cc[0])

    def out_slot(ph, s, cc):
        return (jnp.where(ph == 0, 0, s), 0, 0)

    piece = jax.ShapeDtypeStruct((N_CHIPS, pr, pc), F32)
    return pl.pallas_call(
        body, name="wgrad_" + tag,
        grid_spec=pltpu.PrefetchScalarGridSpec(
            num_scalar_prefetch=1, grid=(2, N_CHIPS),
            in_specs=[pl.BlockSpec((t, pr), lambda ph, s, cc: (0, a_blk(s, half(ph, cc)))),
                      pl.BlockSpec((t, pc), lambda ph, s, cc: (0, b_blk(s, half(ph, cc))))],
            out_specs=[ANY, pl.BlockSpec((1, pr, pc), out_slot), pl.BlockSpec((1, pr, pc), out_slot)],
            scratch_shapes=[pltpu.VMEM((N_CHIPS + 1, pr, pc), F32), pltpu.VMEM((pr, pc), F32),
                            pltpu.SemaphoreType.DMA((N_CHIPS,)), pltpu.SemaphoreType.DMA((N_CHIPS,)), pltpu.SemaphoreType.DMA]),
        out_shape=[piece, piece, jax.ShapeDtypeStruct((N_CHIPS, pr, pc), BF16)],
        compiler_params=_params(dimension_semantics=("arbitrary", "arbitrary")),
    )(core_chip, a, b)[1:]


def _other_chips(x, y):
    return [(1 - x, y), (x, 1 - y), (1 - x, 1 - y)]


class _ShardGather:
    def __init__(self, outs, send_sems, recv_sems):
        self.outs, self.send_sems, self.recv_sems = outs, send_sems, recv_sems
        x, y, c = _position()
        self.c, self.j = c, 2 * x + y
        self.sibling = (x, y, 1 - c)
        self.chips = _other_chips(x, y)

    def _half(self, w, chip, which):
        hr = self.outs[w].shape[1] // 2
        return self.outs[w].at[chip, pl.ds(which * hr, hr), :]

    def _copy(self, ref, w, k, to, src=None):
        return pltpu.make_async_remote_copy(src_ref=ref if src is None else src, dst_ref=ref, send_sem=self.send_sems.at[w, k],
                                            recv_sem=self.recv_sems.at[w, k], device_id=to, device_id_type=MESH)

    def ici(self, w, k, src=None):
        px, py = self.chips[k]
        return self._copy(self._half(w, self.j, self.c), w, k, (px, py, self.c), src)

    def landed(self, w, k):
        px, py = self.chips[k]
        return self._copy(self._half(w, 2 * px + py, self.c), w, k, (px, py, self.c))

    def hand_over(self, w, k):
        px, py = self.chips[k]
        return self._copy(self._half(w, 2 * px + py, self.c), w, 3 + k, self.sibling)

    def handed(self, w, k):
        px, py = self.chips[k]
        return self._copy(self._half(w, 2 * px + py, 1 - self.c), w, 3 + k, self.sibling)


def _gather_first(w_in, w_out, w1, w2, small):
    bigs = (w_in, w_out, w1, w2)
    nb = len(bigs)

    def body(win_ref, wout_ref, w1_ref, w2_ref, sm_ref, gin, gout, g1, g2, gsm, st_in, st_out, st_1, st_2,
             send_sems, recv_sems, sm_send, sm_recv, local_sems):
        srcs = (win_ref, wout_ref, w1_ref, w2_ref)
        stages = (st_in, st_out, st_1, st_2)
        outs = (gin, gout, g1, g2)
        plan = _ShardGather(outs[:1], send_sems, recv_sems)
        j, c = plan.j, plan.c
        for src, st in zip(srcs, stages):
            st[...] = src[...].astype(BF16)
        local = [pltpu.make_async_copy(stages[w], outs[w].at[j], local_sems.at[w]) for w in range(nb)]
        local.append(pltpu.make_async_copy(sm_ref, gsm.at[j], local_sems.at[nb]))
        for cp in local:
            cp.start()

        def small_copy(k):
            px, py = plan.chips[k]
            return pltpu.make_async_remote_copy(src_ref=sm_ref, dst_ref=gsm.at[j], send_sem=sm_send.at[k],
                                                recv_sem=sm_recv.at[k], device_id=(px, py, c), device_id_type=MESH)

        def small_landed(k):
            px, py = plan.chips[k]
            return pltpu.make_async_remote_copy(src_ref=sm_ref, dst_ref=gsm.at[2 * px + py], send_sem=sm_send.at[k],
                                                recv_sem=sm_recv.at[k], device_id=(px, py, c), device_id_type=MESH)

        hr = w_in.shape[0] // 2
        sends = [plan.ici(0, k, src=st_in.at[pl.ds(c * hr, hr), :]) for k in range(3)] + [small_copy(k) for k in range(3)]
        for cp in sends:
            cp.start()
        for k in range(3):
            plan.landed(0, k).wait_recv()
            fwd = plan.hand_over(0, k)
            fwd.start()
            sends.append(fwd)
        for k in range(3):
            small_landed(k).wait_recv()
        for k in range(3):
            plan.handed(0, k).wait_recv()
        for cp in sends:
            cp.wait_send()
        for cp in local:
            cp.wait()

    def gathered(a, dtype):
        return jax.ShapeDtypeStruct((N_CHIPS,) + a.shape, dtype)

    return pl.pallas_call(
        body, name="gather_first",
        in_specs=[VMEM] * 5, out_specs=[ANY] * 5,
        out_shape=[gathered(a, BF16) for a in bigs] + [gathered(small, F32)],
        scratch_shapes=[pltpu.VMEM(a.shape, BF16) for a in bigs]
        + [pltpu.SemaphoreType.DMA((1, 6)), pltpu.SemaphoreType.DMA((1, 6)), pltpu.SemaphoreType.DMA((3,)),
           pltpu.SemaphoreType.DMA((3,)), pltpu.SemaphoreType.DMA((nb + 1,))],
        compiler_params=_params(),
    )(*bigs, small)


class _PartialExchange:
    def __init__(self, parts, arrived, send_sems, recv_sems):
        self.parts, self.arrived, self.send_sems, self.recv_sems = parts, arrived, send_sems, recv_sems
        x, y, c = _position()
        self.c, self.j = c, 2 * x + y
        self.chips = _other_chips(x, y)

    def _copy(self, w, k, slot):
        px, py = self.chips[k]
        return pltpu.make_async_remote_copy(
            src_ref=self.parts[w].at[2 * px + py], dst_ref=self.arrived[w].at[slot], send_sem=self.send_sems.at[w, k],
            recv_sem=self.recv_sems.at[w, k], device_id=(px, py, self.c), device_id_type=MESH)

    def start(self):
        for w in range(len(self.parts)):
            for k in range(3):
                self._copy(w, k, self.j).start()

    def wait(self):
        for w in range(len(self.parts)):
            for k in range(3):
                px, py = self.chips[k]
                self._copy(w, k, 2 * px + py).wait()


def _adamw(w, g, m, v):
    m = ADAM_B1 * m + (1.0 - ADAM_B1) * g
    v = ADAM_B2 * v + (1.0 - ADAM_B2) * (g * g)
    m_hat = m / ADAM_BC1
    v_hat = v / ADAM_BC2
    delta = -ADAM_LR * (m_hat / (jnp.sqrt(v_hat) + ADAM_EPS) + ADAM_WD * w)
    return delta, m, v


JOIN_SUB = 4


def _join(tag, shard_shape, part, arrived, core_chip):
    pr, pc = WGRAD_GEOMETRY[tag][:2]
    rb = pr // JOIN_SUB
    by_rows = shard_shape[1] == pc

    def body(cc_ref, p_ref, r1_ref, r2_ref, r3_ref, g_ref, stage, send_sems, recv_sems, local_sems):
        i = pl.program_id(0)
        c = cc_ref[0]

        def window(core, k):
            if by_rows:
                return g_ref.at[pl.ds((core * JOIN_SUB + k) * rb, rb), :]
            return g_ref.at[pl.ds(k * rb, rb), pl.ds(core * pc, pc)]

        def keep(k):
            return pltpu.make_async_copy(stage.at[k], window(c, k), local_sems.at[k])

        def push(k):
            return pltpu.make_async_remote_copy(src_ref=stage.at[k], dst_ref=window(c, k), send_sem=send_sems.at[k],
                                                recv_sem=recv_sems.at[k], device_id=_sibling(), device_id_type=MESH)

        def pushed(k):
            return pltpu.make_async_remote_copy(src_ref=stage.at[k], dst_ref=window(1 - c, k), send_sem=send_sems.at[k],
                                                recv_sem=recv_sems.at[k], device_id=_sibling(), device_id_type=MESH)

        stage[i] = ((p_ref[0] + r1_ref[0].astype(F32)) + r2_ref[0].astype(F32)) + r3_ref[0].astype(F32)
        keep(i).start()
        push(i).start()

        @pl.when(i == JOIN_SUB - 1)
        def _():
            for k in range(JOIN_SUB):
                keep(k).wait()
                push(k).wait_send()
                pushed(k).wait_recv()

    def partial(off):
        return pl.BlockSpec((1, rb, pc), lambda i, cc: ((cc[1] + off) % N_CHIPS, i, 0))

    return pl.pallas_call(
        body, name="join_" + tag,
        grid_spec=pltpu.PrefetchScalarGridSpec(
            num_scalar_prefetch=1, grid=(JOIN_SUB,),
            in_specs=[partial(0), partial(1), partial(2), partial(3)],
            out_specs=ANY,
            scratch_shapes=[pltpu.VMEM((JOIN_SUB, rb, pc), F32), pltpu.SemaphoreType.DMA((JOIN_SUB,)),
                            pltpu.SemaphoreType.DMA((JOIN_SUB,)), pltpu.SemaphoreType.DMA((JOIN_SUB,))]),
        out_shape=jax.ShapeDtypeStruct(shard_shape, F32),
        compiler_params=_params(dimension_semantics=("arbitrary",)),
    )(core_chip, part, arrived, arrived, arrived)


def _adamw_big(w, g, m, v, name):
    rows, cols = w.shape
    rb = 256 if rows % 256 == 0 else rows

    def body(w_ref, g_ref, m_ref, v_ref, go_ref, d_ref, nm_ref, nv_ref):
        g = g_ref[...]
        go_ref[...] = g
        d_ref[...], nm_ref[...], nv_ref[...] = _adamw(w_ref[...], g, m_ref[...], v_ref[...])

    spec = pl.BlockSpec((rb, cols), lambda i: (i, 0))
    return pl.pallas_call(
        body, name=name, grid=(rows // rb,), in_specs=[spec] * 4, out_specs=[spec] * 4,
        out_shape=[jax.ShapeDtypeStruct(w.shape, F32)] * 4,
        compiler_params=_params(dimension_semantics=("arbitrary",)),
    )(w, g, m, v)


def _small_step(pack, w_pack, m_pack, v_pack, conv_wmv, rconv_wmv):
    n_dev = 8
    rows, cols = pack.shape
    cshard = conv_wmv.shape[2]
    rshard = rconv_wmv.shape[2]

    def body(p_ref, w_ref, m_ref, v_ref, cw_ref, rw_ref, g_ref, d_ref, nm_ref, nv_ref, co_ref, ro_ref,
             all_ref, send_sems, recv_sems, local_sem):
        x, y, c = _position()
        me, sibling = (x, y, c), (x, y, 1 - c)
        chips = _other_chips(x, y)

        def slot(px, py, pc):
            return all_ref.at[4 * px + 2 * py + pc]

        def copy(k, block, to, src=None):
            return pltpu.make_async_remote_copy(
                src_ref=slot(*block) if src is None else src, dst_ref=slot(*block),
                send_sem=send_sems.at[k], recv_sem=recv_sems.at[k], device_id=to, device_id_type=MESH)

        mine = pltpu.make_async_copy(p_ref, slot(*me), local_sem)
        mine.start()
        first = [copy(0, me, sibling, src=p_ref)]
        first += [copy(1 + k, me, (*chip, c), src=p_ref) for k, chip in enumerate(chips)]
        for cp in first:
            cp.start()
        passed = [copy(4 + k, (*chip, c), sibling) for k, chip in enumerate(chips)]
        for k, chip in enumerate(chips):
            copy(1 + k, (*chip, c), me).wait_recv()
            passed[k].start()
        copy(0, sibling, me).wait_recv()
        for k, chip in enumerate(chips):
            copy(4 + k, (*chip, 1 - c), me).wait_recv()
        for cp in first + passed:
            cp.wait_send()
        mine.wait()

        total = all_ref[0]
        for k in range(1, n_dev):
            total = total + all_ref[k]
        g_ref[...] = total
        d_ref[...], nm_ref[...], nv_ref[...] = _adamw(w_ref[...], total, m_ref[...], v_ref[...])

        j = 2 * x + y
        cblk = total[PK_CONV_W * 8:PK_CONV_W * 8 + 8, :]
        rblk = total[PK_RCONV_W * 8:PK_RCONV_W * 8 + 8, :]
        cg = cblk[:, 0:cshard]
        rg = rblk[:, 0:rshard]
        for k in range(1, N_CHIPS):
            cg = jnp.where(j == k, cblk[:, k * cshard:(k + 1) * cshard], cg)
            rg = jnp.where(j == k, rblk[:, k * rshard:(k + 1) * rshard], rg)
        co_ref[0] = cg
        co_ref[1], co_ref[2], co_ref[3] = _adamw(cw_ref[0], cg, cw_ref[1], cw_ref[2])
        ro_ref[0] = rg
        ro_ref[1], ro_ref[2], ro_ref[3] = _adamw(rw_ref[0], rg, rw_ref[1], rw_ref[2])

    packs = [jax.ShapeDtypeStruct((rows, cols), F32)] * 4
    return pl.pallas_call(
        body, name="small_grads_step", in_specs=[VMEM] * 6, out_specs=[VMEM] * 6,
        out_shape=packs + [jax.ShapeDtypeStruct((4, TILE_ROWS, cshard), F32), jax.ShapeDtypeStruct((4, TILE_ROWS, rshard), F32)],
        scratch_shapes=[pltpu.VMEM((n_dev, rows, cols), F32), pltpu.SemaphoreType.DMA((7,)), pltpu.SemaphoreType.DMA((7,)),
                        pltpu.SemaphoreType.DMA],
        compiler_params=_params(),
    )(pack, w_pack, m_pack, v_pack, conv_wmv, rconv_wmv)


def _blk(a):
    a = a.reshape(-1, a.shape[-1])
    return jnp.pad(a, ((0, TILE_ROWS - a.shape[0]), (0, D_MODEL - a.shape[1])))


def _zero_blk():
    return jnp.zeros((TILE_ROWS, D_MODEL), F32)


def _pack_params(p, pre):
    get = lambda n: p[pre + n]
    return jnp.concatenate([
        _blk(get("g_norm_rnn")), _blk(get("rnn_conv_b")), _blk(get("b_a")), _blk(get("b_x")), _blk(get("lru_lambda")),
        _zero_blk(), _zero_blk(), _blk(get("g_norm_conv")), _blk(get("final_norm_g").reshape(1, -1)), _blk(get("norm_mlp_g")),
        _zero_blk(), _blk(get("norm_mix_g")), get("w_a").reshape(64, D_MODEL), get("w_x").reshape(64, D_MODEL)], axis=0)


def _to_block_diag(w):
    w4 = w.reshape(N_BD, 4, 64, 64)
    eye = jnp.eye(4, dtype=w.dtype)
    return (w4[:, :, :, None, :] * eye[None, :, None, :, None]).reshape(N_BD, BD, BD)


def _from_block_diag(d):
    d5 = d.reshape(N_BD, 4, 64, 4, 64)
    return jnp.stack([d5[:, q, :, q, :] for q in range(4)], axis=1).reshape(64, D_MODEL)


def _pad_rows(a):
    return jnp.pad(a, ((0, TILE_ROWS - a.shape[0]), (0, 0)))


_NAMES = ['norm_mix_g', 'w_in', 'conv_w', 'rnn_conv_w', 'rnn_conv_b', 'w_a', 'b_a', 'w_x', 'b_x', 'lru_lambda',
          'g_norm_conv', 'g_norm_rnn', 'w_out', 'norm_mlp_g', 'w_mlp_in', 'w_mlp_out', 'final_norm_g']


def kernel(x, norm_mix_g, w_in, conv_w, rnn_conv_w, rnn_conv_b, w_a, b_a, w_x, b_x, lru_lambda, g_norm_conv, g_norm_rnn, w_out, norm_mlp_g, w_mlp_in, w_mlp_out, final_norm_g, loss_target, m_norm_mix_g, m_w_in, m_conv_w, m_rnn_conv_w, m_rnn_conv_b, m_w_a, m_b_a, m_w_x, m_b_x, m_lru_lambda, m_g_norm_conv, m_g_norm_rnn, m_w_out, m_norm_mlp_g, m_w_mlp_in, m_w_mlp_out, m_final_norm_g, v_norm_mix_g, v_w_in, v_conv_w, v_rnn_conv_w, v_rnn_conv_b, v_w_a, v_b_a, v_w_x, v_b_x, v_lru_lambda, v_g_norm_conv, v_g_norm_rnn, v_w_out, v_norm_mlp_g, v_w_mlp_in, v_w_mlp_out, v_final_norm_g):
    args = dict(locals())
    p = {}
    for n in _NAMES:
        for pre in ("", "m_", "v_"):
            a = args[pre + n]
            p[pre + n] = a[0] if a.ndim >= 3 else a
    xs = x[0]
    target = loss_target[0]
    core_chip = jnp.stack([lax.axis_index("c"), 2 * lax.axis_index("x") + lax.axis_index("y")]).astype(jnp.int32)
    cshard = p["conv_w"].shape[1]
    rshard = p["rnn_conv_w"].shape[1]

    small = jnp.concatenate([_pad_rows(p["conv_w"]), _pad_rows(p["rnn_conv_w"])], axis=1)
    w_in_g, w_out_g, w1_g, w2_g, small_g = _gather_first(p["w_in"], p["w_out"], p["w_mlp_in"], p["w_mlp_out"], small)
    conv_full = small_g[:, :3, :cshard].transpose(1, 0, 2).reshape(3, CONV_W)
    rconv_full = small_g[:, :4, cshard:].transpose(1, 0, 2).reshape(4, LRU_W)
    wa_bd = _to_block_diag(p["w_a"]).astype(BF16)
    wx_bd = _to_block_diag(p["w_x"]).astype(BF16)
    gf = p["final_norm_g"].reshape(1, -1)
    lru = (wa_bd, p["b_a"], wx_bd, p["b_x"], p["lru_lambda"], p["g_norm_conv"], p["g_norm_rnn"])

    (u, h1b, xr, hs, c3, yb), (w_out_g, w1_g, w2_g) = _fwd_mix(
        xs, p["norm_mix_g"], w_in_g, conv_full, rconv_full, p["rnn_conv_b"], *lru, (w_out_g, w1_g, w2_g))
    zb, dpb, h2b, dx3b, dx2, dx2b, dy, st_mlp = _mlp_fwd_bwd(
        xs, yb, w_out_g.reshape(-1, D_MODEL), w1_g, w2_g.reshape(-1, D_MODEL), p["norm_mlp_g"], gf, target)

    part_out = _wgrad(yb, dx2b, "out", core_chip)
    part_1 = _wgrad(h2b, dpb, "mlp_in", core_chip)
    part_2 = _wgrad(zb, dx3b, "mlp_out", core_chip)
    (dub, st_mix, dwa_bd, dwx_bd), arrived_mlp = _mix_bwd(
        dy, u, xr, hs, c3, conv_full, rconv_full, *lru, (part_out[1], part_1[1], part_2[1]))
    part_in = _wgrad(h1b, dub, "in", core_chip)
    (grad_x, st_in), arrived_in = _in_bwd(dub, w_in_g, xs, dx2, p["norm_mix_g"], (part_in[1],))
    big = {}
    for n, tag, part, arrived in (("w_out", "out", part_out, arrived_mlp[0]), ("w_mlp_in", "mlp_in", part_1, arrived_mlp[1]),
                                  ("w_mlp_out", "mlp_out", part_2, arrived_mlp[2]), ("w_in", "in", part_in, arrived_in[0])):
        g = _join(tag, p[n].shape, part[0], arrived, core_chip)
        big[n] = _adamw_big(p[n], g, p["m_" + n], p["v_" + n], "adamw_" + tag)

    pack = jnp.concatenate([st_mix, st_mlp, st_in, _from_block_diag(dwa_bd), _from_block_diag(dwx_bd)], axis=0)
    conv_wmv = jnp.stack([_pad_rows(p[pre + "conv_w"]) for pre in ("", "m_", "v_")])
    rconv_wmv = jnp.stack([_pad_rows(p[pre + "rnn_conv_w"]) for pre in ("", "m_", "v_")])
    g_pack, d_pack, m_pack, v_pack, conv_out, rconv_out = _small_step(
        pack, _pack_params(p, ""), _pack_params(p, "m_"), _pack_params(p, "v_"), conv_wmv, rconv_wmv)

    def unpack(pk, kind):
        def vec(b, width=D_MODEL):
            return pk[b * 8:b * 8 + 1, :width]
        return {
            "norm_mix_g": vec(PK_MIX_G), "rnn_conv_b": vec(PK_RCONV_B), "b_a": vec(PK_B_A), "b_x": vec(PK_B_X),
            "lru_lambda": vec(PK_LAMBDA), "g_norm_conv": vec(PK_G_NORM_CONV, CONV_W), "g_norm_rnn": vec(PK_G_NORM_RNN),
            "norm_mlp_g": vec(PK_MLP_G), "final_norm_g": vec(PK_FINAL_G).reshape(-1),
            "w_a": pk[PK_W_A * 8:PK_W_A * 8 + 64].reshape(1, 16, 64, 64), "w_x": pk[PK_W_X * 8:PK_W_X * 8 + 64].reshape(1, 16, 64, 64),
            "conv_w": conv_out[kind, :3][None], "rnn_conv_w": rconv_out[kind, :4][None],
            "w_in": big["w_in"][kind][None], "w_out": big["w_out"][kind][None],
            "w_mlp_in": big["w_mlp_in"][kind][None], "w_mlp_out": big["w_mlp_out"][kind][None],
        }

    outs = [unpack(pk, kind) for kind, pk in enumerate((g_pack, d_pack, m_pack, v_pack))]
    for o in outs:
        for n in ("norm_mix_g", "rnn_conv_b", "b_a", "b_x", "lru_lambda", "g_norm_conv", "g_norm_rnn", "norm_mlp_g"):
            o[n] = o[n].reshape(1, -1)
    loss = g_pack[PK_LOSS * 8, 0]
    return (loss, grad_x[None], *[o[n] for o in outs for n in _NAMES])
```

```python
import functools
import math

import jax
import jax.numpy as jnp
from jax import lax
from jax.experimental import pallas as pl
from jax.experimental.pallas import tpu as pltpu

F32 = jnp.float32
BF16 = jnp.bfloat16
MESH = pl.DeviceIdType.MESH
ANY = pl.BlockSpec(memory_space=pl.ANY)
VMEM = pl.BlockSpec(memory_space=pltpu.VMEM)

EPS = 1e-6
LRU_C = 8.0
D_MODEL = 1024
CONV_W = 512
LRU_W = 1024
IN_COLS = 3 * CONV_W + 2 * LRU_W
IN_SHARD = IN_COLS // 4
N_CHIPS = 4
N_DEVICES = 8
BD = 256
N_BD = LRU_W // BD

ADAM_LR = 0.001
ADAM_B1 = 0.9
ADAM_B2 = 0.999
ADAM_EPS = 1e-08
ADAM_WD = 0.01
ADAM_STEP = 10
ADAM_BC1 = 1.0 - ADAM_B1 ** ADAM_STEP
ADAM_BC2 = 1.0 - ADAM_B2 ** ADAM_STEP

TILE_ROWS = 8
TOKEN_TILE = 256
VMEM_LIMIT = 56 * 1024 * 1024

PK_G_NORM_RNN, PK_RCONV_B, PK_B_A, PK_B_X, PK_LAMBDA, PK_RCONV_W, PK_CONV_W, PK_G_NORM_CONV = range(8)
PK_FINAL_G, PK_MLP_G, PK_LOSS, PK_MIX_G = 8, 9, 10, 11
PK_W_A = 12
PK_W_X = 20
PK_BLOCKS = 28
PK_ROWS = PK_BLOCKS * TILE_ROWS


def _params(**kw):
    return pltpu.CompilerParams(vmem_limit_bytes=VMEM_LIMIT, **kw)


def _position():
    x, y, c = lax.axis_index("x"), lax.axis_index("y"), lax.axis_index("c")
    return x, y, c


def _sigmoid(v):
    return 1.0 / (1.0 + jnp.exp(-v))


def _one_minus_square(log_a, a):
    v = 2.0 * log_a
    series = -v * (1.0 + v * (0.5 + v * (1.0 / 6.0)))
    return jnp.where(v > -0.01, series, 1.0 - a * a)


_GELU_C = math.sqrt(2.0 / math.pi)
_GELU_K = 0.044715


def _gelu_and_grad(g):
    th = jnp.tanh(_GELU_C * (g + _GELU_K * g * g * g))
    gelu = 0.5 * g * (1.0 + th)
    dgelu = 0.5 * (1.0 + th) + 0.5 * g * (1.0 - th * th) * (_GELU_C * (1.0 + 3.0 * _GELU_K * g * g))
    return gelu, dgelu


def _rows(shape):
    return lax.broadcasted_iota(jnp.int32, shape, 0)


def _shift_down(v, k, prev8):
    rolled = pltpu.roll(v, k, 0)
    halo = pltpu.roll(prev8, k, 0)
    head = jnp.where(_rows(halo.shape) < k, halo, rolled[:TILE_ROWS])
    return jnp.concatenate([head, rolled[TILE_ROWS:]], axis=0)


def _shift_up(v, k, next8):
    n = v.shape[0]
    rolled = pltpu.roll(v, n - k, 0)
    halo = pltpu.roll(next8, TILE_ROWS - k, 0)
    tail = jnp.where(_rows(halo.shape) >= TILE_ROWS - k, halo, rolled[n - TILE_ROWS:])
    return jnp.concatenate([rolled[: n - TILE_ROWS], tail], axis=0)


def _scan_down(a, b):
    n, w = a.shape
    row = _rows(a.shape)
    s = 1
    while s < n:
        if s < TILE_ROWS:
            keep = row >= s
            b = jnp.where(keep, a * pltpu.roll(b, s, 0) + b, b)
            a = jnp.where(keep, a * pltpu.roll(a, s, 0), a)
        else:
            b = a * jnp.concatenate([jnp.zeros((s, w), F32), b[:n - s]], axis=0) + b
            a = a * jnp.concatenate([jnp.ones((s, w), F32), a[:n - s]], axis=0)
        s *= 2
    return a, b


def _scan_up(a, b):
    n, w = a.shape
    row = _rows(a.shape)
    s = 1
    while s < n:
        if s < TILE_ROWS:
            keep = row < n - s
            b = jnp.where(keep, a * pltpu.roll(b, n - s, 0) + b, b)
            a = jnp.where(keep, a * pltpu.roll(a, n - s, 0), a)
        else:
            b = a * jnp.concatenate([b[s:], jnp.zeros((s, w), F32)], axis=0) + b
            a = a * jnp.concatenate([a[s:], jnp.ones((s, w), F32)], axis=0)
        s *= 2
    return a, b


def _softplus_neg(lam):
    e = jnp.exp(-jnp.abs(lam))
    log1p_e = jnp.where(e < 1e-2, e * (1.0 - e * (0.5 - e * (1.0 / 3.0 - e * 0.25))), jnp.log(1.0 + e))
    sp = jnp.maximum(-lam, 0.0) + log1p_e
    dsp = -_sigmoid(-lam)
    return sp, dsp


def _block_diag_dot(vb, w_ref):
    return jnp.concatenate(
        [jnp.dot(vb[:, j * BD:(j + 1) * BD], w_ref[j], preferred_element_type=F32) for j in range(N_BD)], axis=1)


def _block_diag_dot_t(vb, w_ref):
    return jnp.concatenate(
        [lax.dot_general(vb[:, j * BD:(j + 1) * BD], w_ref[j], (((1,), (1,)), ((), ())), preferred_element_type=F32)
         for j in range(N_BD)], axis=1)


def _dot_nt(a, b):
    return lax.dot_general(a, b, (((1,), (1,)), ((), ())), preferred_element_type=F32)


def _dot_tn(a, b):
    return lax.dot_general(a, b, (((0,), (0,)), ((), ())), preferred_element_type=F32)


def _lru_gates(xr, wa_ref, ba, wx_ref, bx, sp):
    xrb = xr.astype(BF16)
    r = _sigmoid(_block_diag_dot(xrb, wa_ref) + ba)
    ig = _sigmoid(_block_diag_dot(xrb, wx_ref) + bx)
    log_a = (-LRU_C) * r * sp
    a = jnp.exp(log_a)
    mult = jnp.sqrt(_one_minus_square(log_a, a))
    return xrb, r, ig, a, mult


def _colsum(v):
    return jnp.sum(v, axis=0, keepdims=True)


def _fwd_mix(x, g1, w_in_g, conv_w, rconv_w, rconv_b, wa_bd, b_a, wx_bd, b_x, lam, g_nc, g_nr, later):
    t, d = x.shape
    tm = TOKEN_TILE
    nt = t // tm
    nl = len(later)
    hand_over_at = [min(nt - 1, (nt * (w + 1)) // nl) for w in range(nl)]

    def body(x_ref, g1_ref, win_ref, cw_ref, rw_ref, rb_ref, wa_ref, ba_ref, wx_ref, bx_ref, lam_ref, gnc_ref, gnr_ref,
             *rest):
        later_in, (u_ref, h1_ref, xr_ref, hs_ref, c3_ref, y_ref), rest = rest[:nl], rest[nl:nl + 6], rest[nl + 6:]
        later_out, (cv_prev, xin_prev, h_prev, send_sems, recv_sems) = rest[:nl], rest[nl:]
        del later_in
        step = pl.program_id(0)
        plan = _ShardGather(later_out, send_sems, recv_sems)

        @pl.when(step == 0)
        def _():
            cv_prev[...] = jnp.zeros_like(cv_prev)
            xin_prev[...] = jnp.zeros_like(xin_prev)
            h_prev[...] = jnp.zeros_like(h_prev)
            for w in range(nl):
                for k in range(3):
                    plan.ici(w, k).start()

        for w in range(nl):
            @pl.when(step == hand_over_at[w])
            def _(w=w):
                for k in range(3):
                    plan.landed(w, k).wait_recv()
                    plan.hand_over(w, k).start()

        xv = x_ref[...]
        rstd = lax.rsqrt(jnp.mean(xv * xv, axis=-1, keepdims=True) + EPS)
        h1b = ((xv * rstd) * g1_ref[...]).astype(BF16)
        h1_ref[...] = h1b
        for j in range(N_CHIPS):
            u_ref[:, j * IN_SHARD:(j + 1) * IN_SHARD] = jnp.dot(h1b, win_ref[j], preferred_element_type=F32)
        gate_b = u_ref[:, 0:CONV_W]
        cv = u_ref[:, CONV_W:2 * CONV_W] * u_ref[:, 2 * CONV_W:3 * CONV_W]
        x_r = u_ref[:, 3 * CONV_W:3 * CONV_W + LRU_W]
        g = u_ref[:, 3 * CONV_W + LRU_W:]

        cw = cw_ref[...]
        cvp = cv_prev[...]
        conv3 = cw[0:1] * _shift_down(cv, 2, cvp) + cw[1:2] * _shift_down(cv, 1, cvp) + cw[2:3] * cv
        cv_prev[...] = cv[tm - TILE_ROWS:]
        c3_ref[...] = conv3
        y_conv = gate_b * conv3

        rw = rw_ref[...]
        xp = xin_prev[...]
        xr = (rw[0:1] * _shift_down(x_r, 3, xp) + rw[1:2] * _shift_down(x_r, 2, xp)
              + rw[2:3] * _shift_down(x_r, 1, xp) + rw[3:4] * x_r) + rb_ref[...]
        xin_prev[...] = x_r[tm - TILE_ROWS:]
        xr_ref[...] = xr
        sp, _ = _softplus_neg(lam_ref[...])
        _, _, ig, a, mult = _lru_gates(xr, wa_ref, ba_ref[...], wx_ref, bx_ref[...], sp)
        a_cum, h = _scan_down(a, mult * (ig * xr))
        h = h + a_cum * h_prev[...]
        h_prev[...] = h[tm - 1:tm]
        hs_ref[...] = h
        gelu, _ = _gelu_and_grad(g)
        y_rnn = h * gelu

        na = y_conv * lax.rsqrt(jnp.mean(y_conv * y_conv, axis=-1, keepdims=True) + EPS) * gnc_ref[...]
        nb = y_rnn * lax.rsqrt(jnp.mean(y_rnn * y_rnn, axis=-1, keepdims=True) + EPS) * gnr_ref[...]
        y_ref[:, :CONV_W] = na.astype(BF16)
        y_ref[:, CONV_W:] = nb.astype(BF16)

        @pl.when(step == nt - 1)
        def _():
            for w in range(nl):
                for k in range(3):
                    plan.handed(w, k).wait_recv()
                    plan.ici(w, k).wait_send()
                    plan.hand_over(w, k).wait_send()

    def full(a):
        nd = a.ndim
        return pl.BlockSpec(a.shape, lambda i: (0,) * nd)

    def tok(cols):
        return pl.BlockSpec((tm, cols), lambda i: (i, 0))

    smalls = (g1, w_in_g, conv_w, rconv_w, rconv_b, wa_bd, b_a, wx_bd, b_x, lam, g_nc, g_nr)
    n_in = 1 + len(smalls)
    outs = pl.pallas_call(
        body, name="fwd_mix", grid=(nt,),
        in_specs=[tok(d)] + [full(a) for a in smalls] + [ANY] * nl,
        out_specs=[tok(IN_COLS), tok(d), tok(LRU_W), tok(LRU_W), tok(CONV_W), tok(CONV_W + LRU_W)] + [ANY] * nl,
        out_shape=[jax.ShapeDtypeStruct((t, IN_COLS), F32), jax.ShapeDtypeStruct((t, d), BF16),
                   jax.ShapeDtypeStruct((t, LRU_W), F32), jax.ShapeDtypeStruct((t, LRU_W), F32),
                   jax.ShapeDtypeStruct((t, CONV_W), F32), jax.ShapeDtypeStruct((t, CONV_W + LRU_W), BF16)]
        + [jax.ShapeDtypeStruct(a.shape, a.dtype) for a in later],
        input_output_aliases={n_in + w: 6 + w for w in range(nl)},
        scratch_shapes=[pltpu.VMEM((TILE_ROWS, CONV_W), F32), pltpu.VMEM((TILE_ROWS, LRU_W), F32),
                        pltpu.VMEM((1, LRU_W), F32), pltpu.SemaphoreType.DMA((nl, 6)), pltpu.SemaphoreType.DMA((nl, 6))],
        compiler_params=_params(dimension_semantics=("arbitrary",)),
    )(x, *smalls, *later)
    return outs[:6], outs[6:]


def _mlp_fwd_bwd(x, yb, w_out_g, w1_g, w2_g, g2, gf, target):
    t, d = x.shape
    tm = TOKEN_TILE
    ff = w2_g.shape[0]
    mix = w_out_g.shape[0]
    ffs = ff // N_CHIPS

    def body(x_ref, y_ref, g2_ref, gf_ref, tgt_ref, wout_hbm, w1_hbm, w2_hbm,
             z_ref, dp_ref, h2_ref, dx3b_ref, dx2_ref, dx2b_ref, dy_ref, st_ref, wout, w1, w2, p_ref):
        @pl.when(pl.program_id(0) == 0)
        def _():
            pltpu.sync_copy(wout_hbm, wout)
            pltpu.sync_copy(w1_hbm, w1)
            pltpu.sync_copy(w2_hbm, w2)
            st_ref[...] = jnp.zeros_like(st_ref)

        x2 = x_ref[...] + jnp.dot(y_ref[...], wout[...], preferred_element_type=F32)
        r2 = lax.rsqrt(jnp.mean(x2 * x2, axis=-1, keepdims=True) + EPS)
        xh2 = x2 * r2
        g2v = g2_ref[...]
        h2b = (xh2 * g2v).astype(BF16)
        h2_ref[...] = h2b
        for j in range(N_CHIPS):
            p_ref[:, j * ffs:(j + 1) * ffs] = jnp.dot(h2b, w1[j], preferred_element_type=F32)
        rp = jnp.maximum(p_ref[...], 0.0)
        zb = (rp * rp).astype(BF16)
        z_ref[...] = zb
        x3 = x2 + jnp.dot(zb, w2[...], preferred_element_type=F32)
        r3 = lax.rsqrt(jnp.mean(x3 * x3, axis=-1, keepdims=True) + EPS)
        xh3 = x3 * r3
        gfv = gf_ref[...]
        err = xh3 * gfv - tgt_ref[...]
        loss = (0.5 / d) * jnp.sum(err * err)
        dout = err * (1.0 / d)
        st_ref[PK_FINAL_G * 8 - 64:PK_FINAL_G * 8 - 63, :] += _colsum(dout * xh3)
        st_ref[PK_LOSS * 8 - 64:PK_LOSS * 8 - 63, :] += jnp.zeros((1, d), F32) + loss
        dxh3 = dout * gfv
        dx3 = r3 * (dxh3 - xh3 * jnp.mean(dxh3 * xh3, axis=-1, keepdims=True))
        dx3b = dx3.astype(BF16)
        dx3b_ref[...] = dx3b
        dpb = (_dot_nt(dx3b, w2[...]) * (2.0 * rp)).astype(BF16)
        dp_ref[...] = dpb
        dh2 = _dot_nt(dpb[:, 0:ffs], w1[0])
        for j in range(1, N_CHIPS):
            dh2 = dh2 + _dot_nt(dpb[:, j * ffs:(j + 1) * ffs], w1[j])
        st_ref[PK_MLP_G * 8 - 64:PK_MLP_G * 8 - 63, :] += _colsum(dh2 * xh2)
        dxh2 = dh2 * g2v
        dx2 = dx3 + r2 * (dxh2 - xh2 * jnp.mean(dxh2 * xh2, axis=-1, keepdims=True))
        dx2_ref[...] = dx2
        dx2b = dx2.astype(BF16)
        dx2b_ref[...] = dx2b
        dy_ref[...] = _dot_nt(dx2b, wout[...])

    def tok(cols):
        return pl.BlockSpec((tm, cols), lambda i: (i, 0))

    def row(cols):
        return pl.BlockSpec((1, cols), lambda i: (0, 0))

    return pl.pallas_call(
        body, name="mlp_fwd_bwd", grid=(t // tm,),
        in_specs=[tok(d), tok(mix), row(d), row(d), tok(d), ANY, ANY, ANY],
        out_specs=[tok(ff), tok(ff), tok(d), tok(d), tok(d), tok(d), tok(mix),
                   pl.BlockSpec((3 * TILE_ROWS, d), lambda i: (0, 0))],
        out_shape=[jax.ShapeDtypeStruct((t, ff), BF16), jax.ShapeDtypeStruct((t, ff), BF16),
                   jax.ShapeDtypeStruct((t, d), BF16), jax.ShapeDtypeStruct((t, d), BF16),
                   jax.ShapeDtypeStruct((t, d), F32), jax.ShapeDtypeStruct((t, d), BF16),
                   jax.ShapeDtypeStruct((t, mix), F32), jax.ShapeDtypeStruct((3 * TILE_ROWS, d), F32)],
        scratch_shapes=[pltpu.VMEM(w_out_g.shape, BF16), pltpu.VMEM(w1_g.shape, BF16), pltpu.VMEM(w2_g.shape, BF16),
                        pltpu.VMEM((tm, ff), F32)],
        compiler_params=_params(dimension_semantics=("arbitrary",)),
    )(x, yb, g2, gf, target, w_out_g, w1_g, w2_g)


def _mix_bwd(dy, u, xr_all, hs_all, c3_all, conv_w, rconv_w, wa_bd, b_a, wx_bd, b_x, lam, g_nc, g_nr, parts):
    t = dy.shape[0]
    tm = TOKEN_TILE
    nt = t // tm
    hb = tm // TILE_ROWS
    npart = len(parts)

    def body(dy_ref, u_ref, uh_ref, xr_ref, hs_ref, hh_ref, c3_ref, cw_ref, rw_ref, wa_ref, ba_ref, wx_ref, bx_ref,
             lam_ref, gnc_ref, gnr_ref, *rest):
        part_refs, (du_ref, st_ref, dwa_ref, dwx_ref), rest = rest[:npart], rest[npart:npart + 4], rest[npart + 4:]
        arrived_refs, (dc_next, a_next, gs_next, dxr_next, send_sems, recv_sems) = rest[:npart], rest[npart:]
        exchange = _PartialExchange(part_refs, arrived_refs, send_sems, recv_sems)
        i = pl.program_id(0)

        @pl.when(i == 0)
        def _():
            exchange.start()
            dc_next[...] = jnp.zeros_like(dc_next)
            a_next[...] = jnp.zeros_like(a_next)
            gs_next[...] = jnp.zeros_like(gs_next)
            dxr_next[...] = jnp.zeros_like(dxr_next)
            st_ref[...] = jnp.zeros_like(st_ref)
            dwa_ref[...] = jnp.zeros_like(dwa_ref)
            dwx_ref[...] = jnp.zeros_like(dwx_ref)

        first_tile = i == nt - 1
        gate_b = u_ref[:, 0:CONV_W]
        gate_c = u_ref[:, CONV_W:2 * CONV_W]
        v = u_ref[:, 2 * CONV_W:3 * CONV_W]
        x_r = u_ref[:, 3 * CONV_W:3 * CONV_W + LRU_W]
        g = u_ref[:, 3 * CONV_W + LRU_W:]
        cv = gate_c * v
        cv_prev = jnp.where(first_tile, 0.0, uh_ref[:, CONV_W:2 * CONV_W] * uh_ref[:, 2 * CONV_W:3 * CONV_W])
        xin_prev = jnp.where(first_tile, 0.0, uh_ref[:, 3 * CONV_W:3 * CONV_W + LRU_W])
        hs_prev = jnp.where(first_tile, 0.0, hh_ref[...])

        def acc(block, val, width=LRU_W, row=0):
            r0 = block * TILE_ROWS + row
            st_ref[r0:r0 + 1, 0:width] += val

        conv3 = c3_ref[...]
        y_conv = gate_b * conv3
        ra = lax.rsqrt(jnp.mean(y_conv * y_conv, axis=-1, keepdims=True) + EPS)
        xha = y_conv * ra
        dna = dy_ref[:, :CONV_W]
        acc(PK_G_NORM_CONV, _colsum(dna * xha), CONV_W)
        dxha = dna * gnc_ref[...]
        dy_conv = ra * (dxha - xha * jnp.mean(dxha * xha, axis=-1, keepdims=True))
        du_ref[:, 0:CONV_W] = (dy_conv * conv3).astype(BF16)
        dc = dy_conv * gate_b
        cw = cw_ref[...]
        dcn = dc_next[...]
        dcv = cw[2:3] * dc + cw[1:2] * _shift_up(dc, 1, dcn) + cw[0:1] * _shift_up(dc, 2, dcn)
        dc_next[...] = dc[:TILE_ROWS]
        acc(PK_CONV_W, _colsum(dc * _shift_down(cv, 2, cv_prev)), CONV_W, 0)
        acc(PK_CONV_W, _colsum(dc * _shift_down(cv, 1, cv_prev)), CONV_W, 1)
        acc(PK_CONV_W, _colsum(dc * cv), CONV_W, 2)
        du_ref[:, CONV_W:2 * CONV_W] = (dcv * v).astype(BF16)
        du_ref[:, 2 * CONV_W:3 * CONV_W] = (dcv * gate_c).astype(BF16)

        hs = hs_ref[...]
        gelu, dgelu = _gelu_and_grad(g)
        y_rnn = hs * gelu
        rb = lax.rsqrt(jnp.mean(y_rnn * y_rnn, axis=-1, keepdims=True) + EPS)
        xhb = y_rnn * rb
        dnb = dy_ref[:, CONV_W:]
        acc(PK_G_NORM_RNN, _colsum(dnb * xhb))
        dxhb = dnb * gnr_ref[...]
        dy_rnn = rb * (dxhb - xhb * jnp.mean(dxhb * xhb, axis=-1, keepdims=True))
        du_ref[:, 3 * CONV_W + LRU_W:] = (dy_rnn * hs * dgelu).astype(BF16)
        dh = dy_rnn * gelu

        xr = xr_ref[...]
        sp, dsp = _softplus_neg(lam_ref[...])
        xrb, r, ig, a, mult = _lru_gates(xr, wa_ref, ba_ref[...], wx_ref, bx_ref[...], sp)
        a_up = _shift_up(a, 1, a_next[...])
        a_next[...] = a[:TILE_ROWS]
        a_cum, gs = _scan_up(a_up, dh)
        gs = gs + a_cum * gs_next[0:1, :]
        gs_next[...] = gs[:TILE_ROWS]
        da = gs * _shift_down(hs, 1, hs_prev)
        gx = gs * xr
        di = gx * mult
        dmult = gx * ig
        dxr = gs * (mult * ig)
        dlog_a = da * a - dmult * ((a * a) / mult)
        acc(PK_LAMBDA, _colsum(dlog_a * r) * ((-LRU_C) * dsp))
        dpa = (dlog_a * ((-LRU_C) * sp)) * (r * (1.0 - r))
        dpx = di * (ig * (1.0 - ig))
        acc(PK_B_A, _colsum(dpa))
        acc(PK_B_X, _colsum(dpx))
        dpab = dpa.astype(BF16)
        dpxb = dpx.astype(BF16)
        dxr = dxr + _block_diag_dot_t(dpab, wa_ref) + _block_diag_dot_t(dpxb, wx_ref)
        for j in range(N_BD):
            cols = slice(j * BD, (j + 1) * BD)
            dwa_ref[j] += _dot_tn(xrb[:, cols], dpab[:, cols])
            dwx_ref[j] += _dot_tn(xrb[:, cols], dpxb[:, cols])

        acc(PK_RCONV_B, _colsum(dxr))
        rw = rw_ref[...]
        dxn = dxr_next[...]
        dx_r = (rw[3:4] * dxr + rw[2:3] * _shift_up(dxr, 1, dxn) + rw[1:2] * _shift_up(dxr, 2, dxn)
                + rw[0:1] * _shift_up(dxr, 3, dxn))
        dxr_next[...] = dxr[:TILE_ROWS]
        for k in range(3):
            acc(PK_RCONV_W, _colsum(dxr * _shift_down(x_r, 3 - k, xin_prev)), LRU_W, k)
        acc(PK_RCONV_W, _colsum(dxr * x_r), LRU_W, 3)
        du_ref[:, 3 * CONV_W:3 * CONV_W + LRU_W] = dx_r.astype(BF16)

        @pl.when(i == nt - 1)
        def _():
            exchange.wait()

    def full(a):
        nd = a.ndim
        return pl.BlockSpec(a.shape, lambda i: (0,) * nd)

    def tok(cols):
        return pl.BlockSpec((tm, cols), lambda i: (nt - 1 - i, 0))

    def halo(cols):
        return pl.BlockSpec((TILE_ROWS, cols), lambda i: (jnp.maximum((nt - 1 - i) * hb - 1, 0), 0))

    smalls = (conv_w, rconv_w, wa_bd, b_a, wx_bd, b_x, lam, g_nc, g_nr)
    outs = pl.pallas_call(
        body, name="mix_bwd", grid=(nt,),
        in_specs=[tok(CONV_W + LRU_W), tok(IN_COLS), halo(IN_COLS), tok(LRU_W), tok(LRU_W), halo(LRU_W), tok(CONV_W)]
        + [full(a) for a in smalls] + [ANY] * npart,
        out_specs=[tok(IN_COLS), pl.BlockSpec((8 * TILE_ROWS, LRU_W), lambda i: (0, 0)),
                   pl.BlockSpec((N_BD, BD, BD), lambda i: (0, 0, 0)), pl.BlockSpec((N_BD, BD, BD), lambda i: (0, 0, 0))]
        + [ANY] * npart,
        out_shape=[jax.ShapeDtypeStruct((t, IN_COLS), BF16), jax.ShapeDtypeStruct((8 * TILE_ROWS, LRU_W), F32),
                   jax.ShapeDtypeStruct((N_BD, BD, BD), F32), jax.ShapeDtypeStruct((N_BD, BD, BD), F32)]
        + [jax.ShapeDtypeStruct(a.shape, a.dtype) for a in parts],
        scratch_shapes=[pltpu.VMEM((TILE_ROWS, CONV_W), F32), pltpu.VMEM((TILE_ROWS, LRU_W), F32),
                        pltpu.VMEM((TILE_ROWS, LRU_W), F32), pltpu.VMEM((TILE_ROWS, LRU_W), F32),
                        pltpu.SemaphoreType.DMA((npart, 3)), pltpu.SemaphoreType.DMA((npart, 3))],
        compiler_params=_params(dimension_semantics=("arbitrary",)),
    )(dy, u, u, xr_all, hs_all, hs_all, c3_all, *smalls, *parts)
    return outs[:4], outs[4:]


def _in_bwd(dub, w_in_g, x, dx2, g1, parts):
    t, d = x.shape
    tm = TOKEN_TILE
    nt = t // tm
    npart = len(parts)

    def body(du_ref, win_ref, x_ref, dx2_ref, g1_ref, *rest):
        part_refs, (gx_ref, st_ref), rest = rest[:npart], rest[npart:npart + 2], rest[npart + 2:]
        arrived_refs, (send_sems, recv_sems) = rest[:npart], rest[npart:]
        exchange = _PartialExchange(part_refs, arrived_refs, send_sems, recv_sems)
        i = pl.program_id(0)

        @pl.when(i == 0)
        def _():
            exchange.start()
            st_ref[...] = jnp.zeros_like(st_ref)

        dh1 = _dot_nt(du_ref[:, 0:IN_SHARD], win_ref[0])
        for j in range(1, N_CHIPS):
            dh1 = dh1 + _dot_nt(du_ref[:, j * IN_SHARD:(j + 1) * IN_SHARD], win_ref[j])
        xv = x_ref[...]
        rstd = lax.rsqrt(jnp.mean(xv * xv, axis=-1, keepdims=True) + EPS)
        xh = xv * rstd
        st_ref[0:1, :] += _colsum(dh1 * xh)
        dxh = dh1 * g1_ref[...]
        gx_ref[...] = dx2_ref[...] + rstd * (dxh - xh * jnp.mean(dxh * xh, axis=-1, keepdims=True))

        @pl.when(i == nt - 1)
        def _():
            exchange.wait()

    def tok(cols):
        return pl.BlockSpec((tm, cols), lambda i: (i, 0))

    outs = pl.pallas_call(
        body, name="in_bwd", grid=(nt,),
        in_specs=[tok(IN_COLS), pl.BlockSpec(w_in_g.shape, lambda i: (0, 0, 0)), tok(d), tok(d),
                  pl.BlockSpec((1, d), lambda i: (0, 0))] + [ANY] * npart,
        out_specs=[tok(d), pl.BlockSpec((TILE_ROWS, d), lambda i: (0, 0))] + [ANY] * npart,
        out_shape=[jax.ShapeDtypeStruct((t, d), F32), jax.ShapeDtypeStruct((TILE_ROWS, d), F32)]
        + [jax.ShapeDtypeStruct(a.shape, a.dtype) for a in parts],
        scratch_shapes=[pltpu.SemaphoreType.DMA((npart, 3)), pltpu.SemaphoreType.DMA((npart, 3))],
        compiler_params=_params(dimension_semantics=("arbitrary",)),
    )(dub, w_in_g, x, dx2, g1, *parts)
    return outs[:2], outs[2:]


WGRAD_GEOMETRY = {
    "in": (512, IN_SHARD, lambda s, h: h, lambda s, h: s),
    "mlp_in": (512, D_MODEL, lambda s, h: h, lambda s, h: s),
    "mlp_out": (512, D_MODEL, lambda s, h: 2 * s + h, lambda s, h: 0),
    "out": (384, 512, lambda s, h: s, lambda s, h: h),
}
K_CHUNK = 512


def _sibling():
    x, y, c = _position()
    return (x, y, 1 - c)


def _wgrad(a, b, tag, core_chip, pack=None):
    t = a.shape[0]
    pr, pc, a_blk, b_blk = WGRAD_GEOMETRY[tag]
    nk = t // K_CHUNK
    mine = N_CHIPS
    riding = pack is not None

    def body(cc_ref, a_ref, b_ref, *rest):
        if riding:
            pack_ref, land_ref, p_ref, pb_ref, all_ref, stage, rbuf, send_sems, recv_sems, rsem, g_send, g_recv, g_local = rest
            gather = _PackGather(pack_ref, all_ref, g_send, g_recv, g_local)
        else:
            land_ref, p_ref, pb_ref, stage, rbuf, send_sems, recv_sems, rsem = rest
        ph, s = pl.program_id(0), pl.program_id(1)
        if riding:
            @pl.when((ph == 0) & (s == 0))
            def _():
                gather.start()

            @pl.when((ph == 1) & (s == N_CHIPS - 2))
            def _():
                gather.hand_over()
        slot = jnp.where(ph == 0, s, mine)
        acc = stage.at[slot]
        acc[...] = _dot_tn(a_ref[0:K_CHUNK, :], b_ref[0:K_CHUNK, :])
        for k in range(1, nk):
            acc[...] += _dot_tn(a_ref[k * K_CHUNK:(k + 1) * K_CHUNK, :], b_ref[k * K_CHUNK:(k + 1) * K_CHUNK, :])

        def push(k):
            return pltpu.make_async_remote_copy(src_ref=stage.at[k], dst_ref=land_ref.at[k], send_sem=send_sems.at[k],
                                                recv_sem=recv_sems.at[k], device_id=_sibling(), device_id_type=MESH)

        @pl.when(ph == 0)
        def _():
            push(s).start()

        @pl.when(ph == 1)
        def _():
            push(s).wait_recv()
            landed = pltpu.make_async_copy(land_ref.at[s], rbuf, rsem)
            landed.start()
            landed.wait()
            p = stage[mine] + rbuf[...]
            p_ref[0] = p
            pb_ref[0] = p.astype(BF16)

        @pl.when((ph == 1) & (s == N_CHIPS - 1))
        def _():
            for k in range(N_CHIPS):
                push(k).wait_send()
            if riding:
                gather.finish()

    def half(ph, cc):
        return jnp.where(ph == 0, 1 - cc[0], cc[0])

    def out_slot(ph, s, cc):
        return (jnp.where(ph == 0, 0, s), 0, 0)

    piece = jax.ShapeDtypeStruct((N_CHIPS, pr, pc), F32)
    in_specs = [pl.BlockSpec((t, pr), lambda ph, s, cc: (0, a_blk(s, half(ph, cc)))),
                pl.BlockSpec((t, pc), lambda ph, s, cc: (0, b_blk(s, half(ph, cc))))]
    out_specs = [ANY, pl.BlockSpec((1, pr, pc), out_slot), pl.BlockSpec((1, pr, pc), out_slot)]
    out_shape = [piece, piece, jax.ShapeDtypeStruct((N_CHIPS, pr, pc), BF16)]
    scratch = [pltpu.VMEM((N_CHIPS + 1, pr, pc), F32), pltpu.VMEM((pr, pc), F32),
               pltpu.SemaphoreType.DMA((N_CHIPS,)), pltpu.SemaphoreType.DMA((N_CHIPS,)), pltpu.SemaphoreType.DMA]
    operands = [a, b]
    if riding:
        in_specs.append(pl.BlockSpec(pack.shape, lambda ph, s, cc: (0, 0)))
        out_specs.append(ANY)
        out_shape.append(jax.ShapeDtypeStruct((N_DEVICES,) + pack.shape, pack.dtype))
        scratch += _PackGather.semaphores()
        operands.append(pack)
    return pl.pallas_call(
        body, name="wgrad_" + tag,
        grid_spec=pltpu.PrefetchScalarGridSpec(
            num_scalar_prefetch=1, grid=(2, N_CHIPS), in_specs=in_specs, out_specs=out_specs, scratch_shapes=scratch),
        out_shape=out_shape,
        compiler_params=_params(dimension_semantics=("arbitrary", "arbitrary")),
    )(core_chip, *operands)[1:]


def _other_chips(x, y):
    return [(1 - x, y), (x, 1 - y), (1 - x, 1 - y)]


class _ShardGather:
    def __init__(self, outs, send_sems, recv_sems):
        self.outs, self.send_sems, self.recv_sems = outs, send_sems, recv_sems
        x, y, c = _position()
        self.c, self.j = c, 2 * x + y
        self.sibling = (x, y, 1 - c)
        self.chips = _other_chips(x, y)

    def _half(self, w, chip, which):
        hr = self.outs[w].shape[1] // 2
        return self.outs[w].at[chip, pl.ds(which * hr, hr), :]

    def _copy(self, ref, w, k, to, src=None):
        return pltpu.make_async_remote_copy(src_ref=ref if src is None else src, dst_ref=ref, send_sem=self.send_sems.at[w, k],
                                            recv_sem=self.recv_sems.at[w, k], device_id=to, device_id_type=MESH)

    def ici(self, w, k, src=None):
        px, py = self.chips[k]
        return self._copy(self._half(w, self.j, self.c), w, k, (px, py, self.c), src)

    def landed(self, w, k):
        px, py = self.chips[k]
        return self._copy(self._half(w, 2 * px + py, self.c), w, k, (px, py, self.c))

    def hand_over(self, w, k):
        px, py = self.chips[k]
        return self._copy(self._half(w, 2 * px + py, self.c), w, 3 + k, self.sibling)

    def handed(self, w, k):
        px, py = self.chips[k]
        return self._copy(self._half(w, 2 * px + py, 1 - self.c), w, 3 + k, self.sibling)


def _gather_first(w_in, w_out, w1, w2, small):
    bigs = (w_in, w_out, w1, w2)
    nb = len(bigs)

    def body(win_ref, wout_ref, w1_ref, w2_ref, sm_ref, gin, gout, g1, g2, gsm, st_in, st_out, st_1, st_2,
             send_sems, recv_sems, sm_send, sm_recv, local_sems):
        srcs = (win_ref, wout_ref, w1_ref, w2_ref)
        stages = (st_in, st_out, st_1, st_2)
        outs = (gin, gout, g1, g2)
        plan = _ShardGather(outs[:1], send_sems, recv_sems)
        j, c = plan.j, plan.c
        for src, st in zip(srcs, stages):
            st[...] = src[...].astype(BF16)
        local = [pltpu.make_async_copy(stages[w], outs[w].at[j], local_sems.at[w]) for w in range(nb)]
        local.append(pltpu.make_async_copy(sm_ref, gsm.at[j], local_sems.at[nb]))
        for cp in local:
            cp.start()

        def small_copy(k):
            px, py = plan.chips[k]
            return pltpu.make_async_remote_copy(src_ref=sm_ref, dst_ref=gsm.at[j], send_sem=sm_send.at[k],
                                                recv_sem=sm_recv.at[k], device_id=(px, py, c), device_id_type=MESH)

        def small_landed(k):
            px, py = plan.chips[k]
            return pltpu.make_async_remote_copy(src_ref=sm_ref, dst_ref=gsm.at[2 * px + py], send_sem=sm_send.at[k],
                                                recv_sem=sm_recv.at[k], device_id=(px, py, c), device_id_type=MESH)

        hr = w_in.shape[0] // 2
        sends = [plan.ici(0, k, src=st_in.at[pl.ds(c * hr, hr), :]) for k in range(3)] + [small_copy(k) for k in range(3)]
        for cp in sends:
            cp.start()
        for k in range(3):
            plan.landed(0, k).wait_recv()
            fwd = plan.hand_over(0, k)
            fwd.start()
            sends.append(fwd)
        for k in range(3):
            small_landed(k).wait_recv()
        for k in range(3):
            plan.handed(0, k).wait_recv()
        for cp in sends:
            cp.wait_send()
        for cp in local:
            cp.wait()

    def gathered(a, dtype):
        return jax.ShapeDtypeStruct((N_CHIPS,) + a.shape, dtype)

    return pl.pallas_call(
        body, name="gather_first",
        in_specs=[VMEM] * 5, out_specs=[ANY] * 5,
        out_shape=[gathered(a, BF16) for a in bigs] + [gathered(small, F32)],
        scratch_shapes=[pltpu.VMEM(a.shape, BF16) for a in bigs]
        + [pltpu.SemaphoreType.DMA((1, 6)), pltpu.SemaphoreType.DMA((1, 6)), pltpu.SemaphoreType.DMA((3,)),
           pltpu.SemaphoreType.DMA((3,)), pltpu.SemaphoreType.DMA((nb + 1,))],
        compiler_params=_params(),
    )(*bigs, small)


class _PartialExchange:
    def __init__(self, parts, arrived, send_sems, recv_sems):
        self.parts, self.arrived, self.send_sems, self.recv_sems = parts, arrived, send_sems, recv_sems
        x, y, c = _position()
        self.c, self.j = c, 2 * x + y
        self.chips = _other_chips(x, y)

    def _copy(self, w, k, slot):
        px, py = self.chips[k]
        return pltpu.make_async_remote_copy(
            src_ref=self.parts[w].at[2 * px + py], dst_ref=self.arrived[w].at[slot], send_sem=self.send_sems.at[w, k],
            recv_sem=self.recv_sems.at[w, k], device_id=(px, py, self.c), device_id_type=MESH)

    def start(self):
        for w in range(len(self.parts)):
            for k in range(3):
                self._copy(w, k, self.j).start()

    def wait(self):
        for w in range(len(self.parts)):
            for k in range(3):
                px, py = self.chips[k]
                self._copy(w, k, 2 * px + py).wait()


class _PackGather:
    def __init__(self, p_ref, all_ref, send_sems, recv_sems, local_sem):
        self.p_ref, self.all_ref, self.send_sems, self.recv_sems, self.local_sem = p_ref, all_ref, send_sems, recv_sems, local_sem
        x, y, c = _position()
        self.me, self.sibling, self.c = (x, y, c), (x, y, 1 - c), c
        self.chips = _other_chips(x, y)

    @staticmethod
    def semaphores():
        return [pltpu.SemaphoreType.DMA((7,)), pltpu.SemaphoreType.DMA((7,)), pltpu.SemaphoreType.DMA]

    def _copy(self, k, block, to, from_pack=False):
        px, py, pc = block
        slot = self.all_ref.at[4 * px + 2 * py + pc]
        return pltpu.make_async_remote_copy(src_ref=self.p_ref if from_pack else slot, dst_ref=slot, send_sem=self.send_sems.at[k],
                                            recv_sem=self.recv_sems.at[k], device_id=to, device_id_type=MESH)

    def _mine(self):
        x, y, c = self.me
        return pltpu.make_async_copy(self.p_ref, self.all_ref.at[4 * x + 2 * y + c], self.local_sem)

    def _first(self):
        return [self._copy(0, self.me, self.sibling, True)] + [
            self._copy(1 + k, self.me, (*chip, self.c), True) for k, chip in enumerate(self.chips)]

    def _passed(self):
        return [self._copy(4 + k, (*chip, self.c), self.sibling) for k, chip in enumerate(self.chips)]

    def start(self):
        self._mine().start()
        for cp in self._first():
            cp.start()

    def hand_over(self):
        for k, chip in enumerate(self.chips):
            self._copy(1 + k, (*chip, self.c), self.me).wait_recv()
            self._passed()[k].start()

    def finish(self):
        self._copy(0, self.sibling, self.me).wait_recv()
        for k, chip in enumerate(self.chips):
            self._copy(4 + k, (*chip, 1 - self.c), self.me).wait_recv()
        for cp in self._first() + self._passed():
            cp.wait_send()
        self._mine().wait()


class _DirectGather:
    def __init__(self, p_ref, all_ref, send_sems, recv_sems, local_sem):
        self.p_ref, self.all_ref, self.send_sems, self.recv_sems, self.local_sem = p_ref, all_ref, send_sems, recv_sems, local_sem
        self.me = _position()

    semaphores = _PackGather.semaphores

    def _peer(self, r):
        x, y, c = self.me
        return ((1 - x) if r & 4 else x, (1 - y) if r & 2 else y, (1 - c) if r & 1 else c)

    def _copy(self, r, slot_of):
        px, py, pc = slot_of
        return pltpu.make_async_remote_copy(src_ref=self.p_ref, dst_ref=self.all_ref.at[4 * px + 2 * py + pc],
                                            send_sem=self.send_sems.at[r - 1], recv_sem=self.recv_sems.at[r - 1],
                                            device_id=self._peer(r), device_id_type=MESH)

    def _mine(self):
        x, y, c = self.me
        return pltpu.make_async_copy(self.p_ref, self.all_ref.at[4 * x + 2 * y + c], self.local_sem)

    def start(self):
        self._mine().start()
        for r in range(1, N_DEVICES):
            self._copy(r, self.me).start()

    def finish(self):
        for r in range(1, N_DEVICES):
            self._copy(r, self._peer(r)).wait()
        self._mine().wait()


def _adamw(w, g, m, v):
    m = ADAM_B1 * m + (1.0 - ADAM_B1) * g
    v = ADAM_B2 * v + (1.0 - ADAM_B2) * (g * g)
    m_hat = m / ADAM_BC1
    v_hat = v / ADAM_BC2
    delta = -ADAM_LR * (m_hat / (jnp.sqrt(v_hat) + ADAM_EPS) + ADAM_WD * w)
    return delta, m, v


JOIN_SUB = 4


def _join(tag, shard_shape, part, arrived, core_chip, block=None):
    pr, pc = WGRAD_GEOMETRY[tag][:2]
    rb = pr // JOIN_SUB
    by_rows = shard_shape[1] == pc
    riding = block is not None

    def body(cc_ref, p_ref, r1_ref, r2_ref, r3_ref, *rest):
        if riding:
            blk_ref, g_ref, all_ref, stage, send_sems, recv_sems, local_sems, b_send, b_recv, b_local = rest
            gather = _DirectGather(blk_ref, all_ref, b_send, b_recv, b_local)
        else:
            g_ref, stage, send_sems, recv_sems, local_sems = rest
        i = pl.program_id(0)
        c = cc_ref[0]
        if riding:
            @pl.when(i == 0)
            def _():
                gather.start()

        def window(core, k):
            if by_rows:
                return g_ref.at[pl.ds((core * JOIN_SUB + k) * rb, rb), :]
            return g_ref.at[pl.ds(k * rb, rb), pl.ds(core * pc, pc)]

        def keep(k):
            return pltpu.make_async_copy(stage.at[k], window(c, k), local_sems.at[k])

        def push(k):
            return pltpu.make_async_remote_copy(src_ref=stage.at[k], dst_ref=window(c, k), send_sem=send_sems.at[k],
                                                recv_sem=recv_sems.at[k], device_id=_sibling(), device_id_type=MESH)

        def pushed(k):
            return pltpu.make_async_remote_copy(src_ref=stage.at[k], dst_ref=window(1 - c, k), send_sem=send_sems.at[k],
                                                recv_sem=recv_sems.at[k], device_id=_sibling(), device_id_type=MESH)

        stage[i] = ((p_ref[0] + r1_ref[0].astype(F32)) + r2_ref[0].astype(F32)) + r3_ref[0].astype(F32)
        keep(i).start()
        push(i).start()

        @pl.when(i == JOIN_SUB - 1)
        def _():
            for k in range(JOIN_SUB):
                keep(k).wait()
                push(k).wait_send()
                pushed(k).wait_recv()
            if riding:
                gather.finish()

    def partial(off):
        return pl.BlockSpec((1, rb, pc), lambda i, cc: ((cc[1] + off) % N_CHIPS, i, 0))

    in_specs = [partial(0), partial(1), partial(2), partial(3)]
    out_specs = [ANY]
    out_shape = [jax.ShapeDtypeStruct(shard_shape, F32)]
    scratch = [pltpu.VMEM((JOIN_SUB, rb, pc), F32), pltpu.SemaphoreType.DMA((JOIN_SUB,)),
               pltpu.SemaphoreType.DMA((JOIN_SUB,)), pltpu.SemaphoreType.DMA((JOIN_SUB,))]
    operands = [part, arrived, arrived, arrived]
    if riding:
        in_specs.append(pl.BlockSpec(block.shape, lambda i, cc: (0, 0)))
        out_specs.append(ANY)
        out_shape.append(jax.ShapeDtypeStruct((N_DEVICES,) + block.shape, block.dtype))
        scratch += _DirectGather.semaphores()
        operands.append(block)
    outs = pl.pallas_call(
        body, name="join_" + tag,
        grid_spec=pltpu.PrefetchScalarGridSpec(
            num_scalar_prefetch=1, grid=(JOIN_SUB,), in_specs=in_specs, out_specs=out_specs, scratch_shapes=scratch),
        out_shape=out_shape,
        compiler_params=_params(dimension_semantics=("arbitrary",)),
    )(core_chip, *operands)
    return outs if riding else outs[0]


def _adamw_big(w, g, m, v, name):
    rows, cols = w.shape
    rb = 256 if rows % 256 == 0 else rows

    def body(w_ref, g_ref, m_ref, v_ref, go_ref, d_ref, nm_ref, nv_ref):
        g = g_ref[...]
        go_ref[...] = g
        d_ref[...], nm_ref[...], nv_ref[...] = _adamw(w_ref[...], g, m_ref[...], v_ref[...])

    spec = pl.BlockSpec((rb, cols), lambda i: (i, 0))
    return pl.pallas_call(
        body, name=name, grid=(rows // rb,), in_specs=[spec] * 4, out_specs=[spec] * 4,
        out_shape=[jax.ShapeDtypeStruct(w.shape, F32)] * 4,
        compiler_params=_params(dimension_semantics=("arbitrary",)),
    )(w, g, m, v)


def _small_step(packs, mix_g_blocks, w_pack, m_pack, v_pack, conv_wmv, rconv_wmv):
    rows, cols = packs.shape[1:]
    cshard = conv_wmv.shape[2]
    rshard = rconv_wmv.shape[2]
    mix_row = PK_MIX_G * TILE_ROWS

    def body(all_ref, blk_ref, w_ref, m_ref, v_ref, cw_ref, rw_ref, g_ref, d_ref, nm_ref, nv_ref, co_ref, ro_ref):
        total = all_ref[0]
        late = blk_ref[0]
        for k in range(1, N_DEVICES):
            total = total + all_ref[k]
            late = late + blk_ref[k]
        g_ref[...] = total
        g_ref[mix_row:mix_row + TILE_ROWS, :] = late
        g = g_ref[...]
        d_ref[...], nm_ref[...], nv_ref[...] = _adamw(w_ref[...], g, m_ref[...], v_ref[...])

        x, y, _ = _position()
        j = 2 * x + y
        cblk = total[PK_CONV_W * 8:PK_CONV_W * 8 + 8, :]
        rblk = total[PK_RCONV_W * 8:PK_RCONV_W * 8 + 8, :]
        cg = cblk[:, 0:cshard]
        rg = rblk[:, 0:rshard]
        for k in range(1, N_CHIPS):
            cg = jnp.where(j == k, cblk[:, k * cshard:(k + 1) * cshard], cg)
            rg = jnp.where(j == k, rblk[:, k * rshard:(k + 1) * rshard], rg)
        co_ref[0] = cg
        co_ref[1], co_ref[2], co_ref[3] = _adamw(cw_ref[0], cg, cw_ref[1], cw_ref[2])
        ro_ref[0] = rg
        ro_ref[1], ro_ref[2], ro_ref[3] = _adamw(rw_ref[0], rg, rw_ref[1], rw_ref[2])

    pack = [jax.ShapeDtypeStruct((rows, cols), F32)] * 4
    return pl.pallas_call(
        body, name="small_grads_step", in_specs=[VMEM] * 7, out_specs=[VMEM] * 6,
        out_shape=pack + [jax.ShapeDtypeStruct((4, TILE_ROWS, cshard), F32), jax.ShapeDtypeStruct((4, TILE_ROWS, rshard), F32)],
        compiler_params=_params(),
    )(packs, mix_g_blocks, w_pack, m_pack, v_pack, conv_wmv, rconv_wmv)


def _blk(a):
    a = a.reshape(-1, a.shape[-1])
    return jnp.pad(a, ((0, TILE_ROWS - a.shape[0]), (0, D_MODEL - a.shape[1])))


def _zero_blk():
    return jnp.zeros((TILE_ROWS, D_MODEL), F32)


def _pack_params(p, pre):
    get = lambda n: p[pre + n]
    return jnp.concatenate([
        _blk(get("g_norm_rnn")), _blk(get("rnn_conv_b")), _blk(get("b_a")), _blk(get("b_x")), _blk(get("lru_lambda")),
        _zero_blk(), _zero_blk(), _blk(get("g_norm_conv")), _blk(get("final_norm_g").reshape(1, -1)), _blk(get("norm_mlp_g")),
        _zero_blk(), _blk(get("norm_mix_g")), get("w_a").reshape(64, D_MODEL), get("w_x").reshape(64, D_MODEL)], axis=0)


def _to_block_diag(w):
    w4 = w.reshape(N_BD, 4, 64, 64)
    eye = jnp.eye(4, dtype=w.dtype)
    return (w4[:, :, :, None, :] * eye[None, :, None, :, None]).reshape(N_BD, BD, BD)


def _from_block_diag(d):
    d5 = d.reshape(N_BD, 4, 64, 4, 64)
    return jnp.stack([d5[:, q, :, q, :] for q in range(4)], axis=1).reshape(64, D_MODEL)


def _pad_rows(a):
    return jnp.pad(a, ((0, TILE_ROWS - a.shape[0]), (0, 0)))


_NAMES = ['norm_mix_g', 'w_in', 'conv_w', 'rnn_conv_w', 'rnn_conv_b', 'w_a', 'b_a', 'w_x', 'b_x', 'lru_lambda',
          'g_norm_conv', 'g_norm_rnn', 'w_out', 'norm_mlp_g', 'w_mlp_in', 'w_mlp_out', 'final_norm_g']


def kernel(x, norm_mix_g, w_in, conv_w, rnn_conv_w, rnn_conv_b, w_a, b_a, w_x, b_x, lru_lambda, g_norm_conv, g_norm_rnn, w_out, norm_mlp_g, w_mlp_in, w_mlp_out, final_norm_g, loss_target, m_norm_mix_g, m_w_in, m_conv_w, m_rnn_conv_w, m_rnn_conv_b, m_w_a, m_b_a, m_w_x, m_b_x, m_lru_lambda, m_g_norm_conv, m_g_norm_rnn, m_w_out, m_norm_mlp_g, m_w_mlp_in, m_w_mlp_out, m_final_norm_g, v_norm_mix_g, v_w_in, v_conv_w, v_rnn_conv_w, v_rnn_conv_b, v_w_a, v_b_a, v_w_x, v_b_x, v_lru_lambda, v_g_norm_conv, v_g_norm_rnn, v_w_out, v_norm_mlp_g, v_w_mlp_in, v_w_mlp_out, v_final_norm_g):
    args = dict(locals())
    p = {}
    for n in _NAMES:
        for pre in ("", "m_", "v_"):
            a = args[pre + n]
            p[pre + n] = a[0] if a.ndim >= 3 else a
    xs = x[0]
    target = loss_target[0]
    core_chip = jnp.stack([lax.axis_index("c"), 2 * lax.axis_index("x") + lax.axis_index("y")]).astype(jnp.int32)
    cshard = p["conv_w"].shape[1]
    rshard = p["rnn_conv_w"].shape[1]

    small = jnp.concatenate([_pad_rows(p["conv_w"]), _pad_rows(p["rnn_conv_w"])], axis=1)
    w_in_g, w_out_g, w1_g, w2_g, small_g = _gather_first(p["w_in"], p["w_out"], p["w_mlp_in"], p["w_mlp_out"], small)
    conv_full = small_g[:, :3, :cshard].transpose(1, 0, 2).reshape(3, CONV_W)
    rconv_full = small_g[:, :4, cshard:].transpose(1, 0, 2).reshape(4, LRU_W)
    wa_bd = _to_block_diag(p["w_a"]).astype(BF16)
    wx_bd = _to_block_diag(p["w_x"]).astype(BF16)
    gf = p["final_norm_g"].reshape(1, -1)
    lru = (wa_bd, p["b_a"], wx_bd, p["b_x"], p["lru_lambda"], p["g_norm_conv"], p["g_norm_rnn"])

    (u, h1b, xr, hs, c3, yb), (w_out_g, w1_g, w2_g) = _fwd_mix(
        xs, p["norm_mix_g"], w_in_g, conv_full, rconv_full, p["rnn_conv_b"], *lru, (w_out_g, w1_g, w2_g))
    zb, dpb, h2b, dx3b, dx2, dx2b, dy, st_mlp = _mlp_fwd_bwd(
        xs, yb, w_out_g.reshape(-1, D_MODEL), w1_g, w2_g.reshape(-1, D_MODEL), p["norm_mlp_g"], gf, target)

    part_out = _wgrad(yb, dx2b, "out", core_chip)
    part_1 = _wgrad(h2b, dpb, "mlp_in", core_chip)
    part_2 = _wgrad(zb, dx3b, "mlp_out", core_chip)
    (dub, st_mix, dwa_bd, dwx_bd), arrived_mlp = _mix_bwd(
        dy, u, xr, hs, c3, conv_full, rconv_full, *lru, (part_out[1], part_1[1], part_2[1]))
    pack = jnp.concatenate([st_mix, st_mlp, _zero_blk(), _from_block_diag(dwa_bd), _from_block_diag(dwx_bd)], axis=0)
    *part_in, packs = _wgrad(h1b, dub, "in", core_chip, pack)
    (grad_x, st_in), arrived_in = _in_bwd(dub, w_in_g, xs, dx2, p["norm_mix_g"], (part_in[1],))
    big = {}
    for n, tag, part, arrived in (("w_out", "out", part_out, arrived_mlp[0]), ("w_mlp_in", "mlp_in", part_1, arrived_mlp[1]),
                                  ("w_mlp_out", "mlp_out", part_2, arrived_mlp[2]), ("w_in", "in", part_in, arrived_in[0])):
        if tag == "out":
            g, mix_g_blocks = _join(tag, p[n].shape, part[0], arrived, core_chip, st_in)
        else:
            g = _join(tag, p[n].shape, part[0], arrived, core_chip)
        big[n] = _adamw_big(p[n], g, p["m_" + n], p["v_" + n], "adamw_" + tag)

    conv_wmv = jnp.stack([_pad_rows(p[pre + "conv_w"]) for pre in ("", "m_", "v_")])
    rconv_wmv = jnp.stack([_pad_rows(p[pre + "rnn_conv_w"]) for pre in ("", "m_", "v_")])
    g_pack, d_pack, m_pack, v_pack, conv_out, rconv_out = _small_step(
        packs, mix_g_blocks, _pack_params(p, ""), _pack_params(p, "m_"), _pack_params(p, "v_"), conv_wmv, rconv_wmv)

    def unpack(pk, kind):
        def vec(b, width=D_MODEL):
            return pk[b * 8:b * 8 + 1, :width]
        return {
            "norm_mix_g": vec(PK_MIX_G), "rnn_conv_b": vec(PK_RCONV_B), "b_a": vec(PK_B_A), "b_x": vec(PK_B_X),
            "lru_lambda": vec(PK_LAMBDA), "g_norm_conv": vec(PK_G_NORM_CONV, CONV_W), "g_norm_rnn": vec(PK_G_NORM_RNN),
            "norm_mlp_g": vec(PK_MLP_G), "final_norm_g": vec(PK_FINAL_G).reshape(-1),
            "w_a": pk[PK_W_A * 8:PK_W_A * 8 + 64].reshape(1, 16, 64, 64), "w_x": pk[PK_W_X * 8:PK_W_X * 8 + 64].reshape(1, 16, 64, 64),
            "conv_w": conv_out[kind, :3][None], "rnn_conv_w": rconv_out[kind, :4][None],
            "w_in": big["w_in"][kind][None], "w_out": big["w_out"][kind][None],
            "w_mlp_in": big["w_mlp_in"][kind][None], "w_mlp_out": big["w_mlp_out"][kind][None],
        }

    outs = [unpack(pk, kind) for kind, pk in enumerate((g_pack, d_pack, m_pack, v_pack))]
    for o in outs:
        for n in ("norm_mix_g", "rnn_conv_b", "b_a", "b_x", "lru_lambda", "g_norm_conv", "g_norm_rnn", "norm_mlp_g"):
            o[n] = o[n].reshape(1, -1)
    loss = g_pack[PK_LOSS * 8, 0]
    return (loss, grad_x[None], *[o[n] for o in outs for n in _NAMES])
```

```python
import functools
import math

import jax
import jax.numpy as jnp
from jax import lax
from jax.experimental import pallas as pl
from jax.experimental.pallas import tpu as pltpu

F32 = jnp.float32
BF16 = jnp.bfloat16
MESH = pl.DeviceIdType.MESH
ANY = pl.BlockSpec(memory_space=pl.ANY)
VMEM = pl.BlockSpec(memory_space=pltpu.VMEM)

EPS = 1e-6
LRU_C = 8.0
D_MODEL = 1024
CONV_W = 512
LRU_W = 1024
IN_COLS = 3 * CONV_W + 2 * LRU_W
IN_SHARD = IN_COLS // 4
N_CHIPS = 4
N_DEVICES = 8
BD = 256
N_BD = LRU_W // BD

ADAM_LR = 0.001
ADAM_B1 = 0.9
ADAM_B2 = 0.999
ADAM_EPS = 1e-08
ADAM_WD = 0.01
ADAM_STEP = 10
ADAM_BC1 = 1.0 - ADAM_B1 ** ADAM_STEP
ADAM_BC2 = 1.0 - ADAM_B2 ** ADAM_STEP

TILE_ROWS = 8
TOKEN_TILE = 256
VMEM_LIMIT = 56 * 1024 * 1024

PK_G_NORM_RNN, PK_RCONV_B, PK_B_A, PK_B_X, PK_LAMBDA, PK_RCONV_W, PK_CONV_W, PK_G_NORM_CONV = range(8)
PK_FINAL_G, PK_MLP_G, PK_LOSS, PK_MIX_G = 8, 9, 10, 11
PK_W_A = 12
PK_W_X = 20
PK_BLOCKS = 28
PK_ROWS = PK_BLOCKS * TILE_ROWS


def _params(**kw):
    return pltpu.CompilerParams(vmem_limit_bytes=VMEM_LIMIT, **kw)


def _position():
    x, y, c = lax.axis_index("x"), lax.axis_index("y"), lax.axis_index("c")
    return x, y, c


def _sigmoid(v):
    return 1.0 / (1.0 + jnp.exp(-v))


def _one_minus_square(log_a, a):
    v = 2.0 * log_a
    series = -v * (1.0 + v * (0.5 + v * (1.0 / 6.0)))
    return jnp.where(v > -0.01, series, 1.0 - a * a)


_GELU_C = math.sqrt(2.0 / math.pi)
_GELU_K = 0.044715


def _gelu_and_grad(g):
    th = jnp.tanh(_GELU_C * (g + _GELU_K * g * g * g))
    gelu = 0.5 * g * (1.0 + th)
    dgelu = 0.5 * (1.0 + th) + 0.5 * g * (1.0 - th * th) * (_GELU_C * (1.0 + 3.0 * _GELU_K * g * g))
    return gelu, dgelu


def _rows(shape):
    return lax.broadcasted_iota(jnp.int32, shape, 0)


def _shift_down(v, k, prev8):
    rolled = pltpu.roll(v, k, 0)
    halo = pltpu.roll(prev8, k, 0)
    head = jnp.where(_rows(halo.shape) < k, halo, rolled[:TILE_ROWS])
    return jnp.concatenate([head, rolled[TILE_ROWS:]], axis=0)


def _shift_up(v, k, next8):
    n = v.shape[0]
    rolled = pltpu.roll(v, n - k, 0)
    halo = pltpu.roll(next8, TILE_ROWS - k, 0)
    tail = jnp.where(_rows(halo.shape) >= TILE_ROWS - k, halo, rolled[n - TILE_ROWS:])
    return jnp.concatenate([rolled[: n - TILE_ROWS], tail], axis=0)


def _scan_down(a, b):
    n, w = a.shape
    row = _rows(a.shape)
    s = 1
    while s < n:
        if s < TILE_ROWS:
            keep = row >= s
            b = jnp.where(keep, a * pltpu.roll(b, s, 0) + b, b)
            a = jnp.where(keep, a * pltpu.roll(a, s, 0), a)
        else:
            b = a * jnp.concatenate([jnp.zeros((s, w), F32), b[:n - s]], axis=0) + b
            a = a * jnp.concatenate([jnp.ones((s, w), F32), a[:n - s]], axis=0)
        s *= 2
    return a, b


def _scan_up(a, b):
    n, w = a.shape
    row = _rows(a.shape)
    s = 1
    while s < n:
        if s < TILE_ROWS:
            keep = row < n - s
            b = jnp.where(keep, a * pltpu.roll(b, n - s, 0) + b, b)
            a = jnp.where(keep, a * pltpu.roll(a, n - s, 0), a)
        else:
            b = a * jnp.concatenate([b[s:], jnp.zeros((s, w), F32)], axis=0) + b
            a = a * jnp.concatenate([a[s:], jnp.ones((s, w), F32)], axis=0)
        s *= 2
    return a, b


def _softplus_neg(lam):
    e = jnp.exp(-jnp.abs(lam))
    log1p_e = jnp.where(e < 1e-2, e * (1.0 - e * (0.5 - e * (1.0 / 3.0 - e * 0.25))), jnp.log(1.0 + e))
    sp = jnp.maximum(-lam, 0.0) + log1p_e
    dsp = -_sigmoid(-lam)
    return sp, dsp


def _block_diag_dot(vb, w_ref):
    return jnp.concatenate(
        [jnp.dot(vb[:, j * BD:(j + 1) * BD], w_ref[j], preferred_element_type=F32) for j in range(N_BD)], axis=1)


def _block_diag_dot_t(vb, w_ref):
    return jnp.concatenate(
        [lax.dot_general(vb[:, j * BD:(j + 1) * BD], w_ref[j], (((1,), (1,)), ((), ())), preferred_element_type=F32)
         for j in range(N_BD)], axis=1)


def _dot_nt(a, b):
    return lax.dot_general(a, b, (((1,), (1,)), ((), ())), preferred_element_type=F32)


def _dot_tn(a, b):
    return lax.dot_general(a, b, (((0,), (0,)), ((), ())), preferred_element_type=F32)


def _lru_gates(xr, wa_ref, ba, wx_ref, bx, sp):
    xrb = xr.astype(BF16)
    r = _sigmoid(_block_diag_dot(xrb, wa_ref) + ba)
    ig = _sigmoid(_block_diag_dot(xrb, wx_ref) + bx)
    log_a = (-LRU_C) * r * sp
    a = jnp.exp(log_a)
    mult = jnp.sqrt(_one_minus_square(log_a, a))
    return r, ig, a, mult


def _colsum(v):
    return jnp.sum(v, axis=0, keepdims=True)


N_FWD_OUT = 10


def _fwd_mix(x, g1, w_in_g, conv_w, rconv_w, rconv_b, wa_bd, b_a, wx_bd, b_x, lam, g_nc, g_nr, later):
    t, d = x.shape
    tm = TOKEN_TILE
    nt = t // tm
    nl = len(later)
    hand_over_at = [min(nt - 1, (nt * (w + 1)) // nl) for w in range(nl)]

    def body(x_ref, g1_ref, win_ref, cw_ref, rw_ref, rb_ref, wa_ref, ba_ref, wx_ref, bx_ref, lam_ref, gnc_ref, gnr_ref,
             *rest):
        later_in, outs, rest = rest[:nl], rest[nl:nl + N_FWD_OUT], rest[nl + N_FWD_OUT:]
        u_ref, h1_ref, xr_ref, hs_ref, c3_ref, y_ref, r_ref, ig_ref, a_ref, mult_ref = outs
        later_out, (cv_prev, xin_prev, h_prev, send_sems, recv_sems) = rest[:nl], rest[nl:]
        del later_in
        step = pl.program_id(0)
        plan = _ShardGather(later_out, send_sems, recv_sems)

        @pl.when(step == 0)
        def _():
            cv_prev[...] = jnp.zeros_like(cv_prev)
            xin_prev[...] = jnp.zeros_like(xin_prev)
            h_prev[...] = jnp.zeros_like(h_prev)
            for w in range(nl):
                for k in range(3):
                    plan.ici(w, k).start()

        for w in range(nl):
            @pl.when(step == hand_over_at[w])
            def _(w=w):
                for k in range(3):
                    plan.landed(w, k).wait_recv()
                    plan.hand_over(w, k).start()

        xv = x_ref[...]
        rstd = lax.rsqrt(jnp.mean(xv * xv, axis=-1, keepdims=True) + EPS)
        h1b = ((xv * rstd) * g1_ref[...]).astype(BF16)
        h1_ref[...] = h1b
        for j in range(N_CHIPS):
            u_ref[:, j * IN_SHARD:(j + 1) * IN_SHARD] = jnp.dot(h1b, win_ref[j], preferred_element_type=F32)
        gate_b = u_ref[:, 0:CONV_W]
        cv = u_ref[:, CONV_W:2 * CONV_W] * u_ref[:, 2 * CONV_W:3 * CONV_W]
        x_r = u_ref[:, 3 * CONV_W:3 * CONV_W + LRU_W]
        g = u_ref[:, 3 * CONV_W + LRU_W:]

        cw = cw_ref[...]
        cvp = cv_prev[...]
        conv3 = cw[0:1] * _shift_down(cv, 2, cvp) + cw[1:2] * _shift_down(cv, 1, cvp) + cw[2:3] * cv
        cv_prev[...] = cv[tm - TILE_ROWS:]
        c3_ref[...] = conv3
        y_conv = gate_b * conv3

        rw = rw_ref[...]
        xp = xin_prev[...]
        xr = (rw[0:1] * _shift_down(x_r, 3, xp) + rw[1:2] * _shift_down(x_r, 2, xp)
              + rw[2:3] * _shift_down(x_r, 1, xp) + rw[3:4] * x_r) + rb_ref[...]
        xin_prev[...] = x_r[tm - TILE_ROWS:]
        xr_ref[...] = xr
        sp, _ = _softplus_neg(lam_ref[...])
        r, ig, a, mult = _lru_gates(xr, wa_ref, ba_ref[...], wx_ref, bx_ref[...], sp)
        r_ref[...] = r
        ig_ref[...] = ig
        a_ref[...] = a
        mult_ref[...] = mult
        a_cum, h = _scan_down(a, mult * (ig * xr))
        h = h + a_cum * h_prev[...]
        h_prev[...] = h[tm - 1:tm]
        hs_ref[...] = h
        gelu, _ = _gelu_and_grad(g)
        y_rnn = h * gelu

        na = y_conv * lax.rsqrt(jnp.mean(y_conv * y_conv, axis=-1, keepdims=True) + EPS) * gnc_ref[...]
        nb = y_rnn * lax.rsqrt(jnp.mean(y_rnn * y_rnn, axis=-1, keepdims=True) + EPS) * gnr_ref[...]
        y_ref[:, :CONV_W] = na.astype(BF16)
        y_ref[:, CONV_W:] = nb.astype(BF16)

        @pl.when(step == nt - 1)
        def _():
            for w in range(nl):
                for k in range(3):
                    plan.handed(w, k).wait_recv()
                    plan.ici(w, k).wait_send()
                    plan.hand_over(w, k).wait_send()

    def full(a):
        nd = a.ndim
        return pl.BlockSpec(a.shape, lambda i: (0,) * nd)

    def tok(cols):
        return pl.BlockSpec((tm, cols), lambda i: (i, 0))

    def act(cols, dtype=F32):
        return jax.ShapeDtypeStruct((t, cols), dtype)

    smalls = (g1, w_in_g, conv_w, rconv_w, rconv_b, wa_bd, b_a, wx_bd, b_x, lam, g_nc, g_nr)
    n_in = 1 + len(smalls)
    outs = pl.pallas_call(
        body, name="fwd_mix", grid=(nt,),
        in_specs=[tok(d)] + [full(a) for a in smalls] + [ANY] * nl,
        out_specs=[tok(IN_COLS), tok(d), tok(LRU_W), tok(LRU_W), tok(CONV_W), tok(CONV_W + LRU_W)]
        + [tok(LRU_W)] * 4 + [ANY] * nl,
        out_shape=[act(IN_COLS), act(d, BF16), act(LRU_W), act(LRU_W), act(CONV_W), act(CONV_W + LRU_W, BF16)]
        + [act(LRU_W)] * 4 + [jax.ShapeDtypeStruct(a.shape, a.dtype) for a in later],
        input_output_aliases={n_in + w: N_FWD_OUT + w for w in range(nl)},
        scratch_shapes=[pltpu.VMEM((TILE_ROWS, CONV_W), F32), pltpu.VMEM((TILE_ROWS, LRU_W), F32),
                        pltpu.VMEM((1, LRU_W), F32), pltpu.SemaphoreType.DMA((nl, 6)), pltpu.SemaphoreType.DMA((nl, 6))],
        compiler_params=_params(dimension_semantics=("arbitrary",)),
    )(x, *smalls, *later)
    return outs[:N_FWD_OUT], outs[N_FWD_OUT:]


def _mlp_fwd_bwd(x, yb, w_out_g, w1_g, w2_g, g2, gf, target):
    t, d = x.shape
    tm = TOKEN_TILE
    ff = w2_g.shape[0]
    mix = w_out_g.shape[0]
    ffs = ff // N_CHIPS

    def body(x_ref, y_ref, g2_ref, gf_ref, tgt_ref, wout_hbm, w1_hbm, w2_hbm,
             z_ref, dp_ref, h2_ref, dx3b_ref, dx2_ref, dx2b_ref, dy_ref, st_ref, wout, w1, w2, p_ref):
        @pl.when(pl.program_id(0) == 0)
        def _():
            pltpu.sync_copy(wout_hbm, wout)
            pltpu.sync_copy(w1_hbm, w1)
            pltpu.sync_copy(w2_hbm, w2)
            st_ref[...] = jnp.zeros_like(st_ref)

        x2 = x_ref[...] + jnp.dot(y_ref[...], wout[...], preferred_element_type=F32)
        r2 = lax.rsqrt(jnp.mean(x2 * x2, axis=-1, keepdims=True) + EPS)
        xh2 = x2 * r2
        g2v = g2_ref[...]
        h2b = (xh2 * g2v).astype(BF16)
        h2_ref[...] = h2b
        for j in range(N_CHIPS):
            p_ref[:, j * ffs:(j + 1) * ffs] = jnp.dot(h2b, w1[j], preferred_element_type=F32)
        rp = jnp.maximum(p_ref[...], 0.0)
        zb = (rp * rp).astype(BF16)
        z_ref[...] = zb
        x3 = x2 + jnp.dot(zb, w2[...], preferred_element_type=F32)
        r3 = lax.rsqrt(jnp.mean(x3 * x3, axis=-1, keepdims=True) + EPS)
        xh3 = x3 * r3
        gfv = gf_ref[...]
        err = xh3 * gfv - tgt_ref[...]
        loss = (0.5 / d) * jnp.sum(err * err)
        dout = err * (1.0 / d)
        st_ref[PK_FINAL_G * 8 - 64:PK_FINAL_G * 8 - 63, :] += _colsum(dout * xh3)
        st_ref[PK_LOSS * 8 - 64:PK_LOSS * 8 - 63, :] += jnp.zeros((1, d), F32) + loss
        dxh3 = dout * gfv
        dx3 = r3 * (dxh3 - xh3 * jnp.mean(dxh3 * xh3, axis=-1, keepdims=True))
        dx3b = dx3.astype(BF16)
        dx3b_ref[...] = dx3b
        dpb = (_dot_nt(dx3b, w2[...]) * (2.0 * rp)).astype(BF16)
        dp_ref[...] = dpb
        dh2 = _dot_nt(dpb[:, 0:ffs], w1[0])
        for j in range(1, N_CHIPS):
            dh2 = dh2 + _dot_nt(dpb[:, j * ffs:(j + 1) * ffs], w1[j])
        st_ref[PK_MLP_G * 8 - 64:PK_MLP_G * 8 - 63, :] += _colsum(dh2 * xh2)
        dxh2 = dh2 * g2v
        dx2 = dx3 + r2 * (dxh2 - xh2 * jnp.mean(dxh2 * xh2, axis=-1, keepdims=True))
        dx2_ref[...] = dx2
        dx2b = dx2.astype(BF16)
        dx2b_ref[...] = dx2b
        dy_ref[...] = _dot_nt(dx2b, wout[...])

    def tok(cols):
        return pl.BlockSpec((tm, cols), lambda i: (i, 0))

    def row(cols):
        return pl.BlockSpec((1, cols), lambda i: (0, 0))

    return pl.pallas_call(
        body, name="mlp_fwd_bwd", grid=(t // tm,),
        in_specs=[tok(d), tok(mix), row(d), row(d), tok(d), ANY, ANY, ANY],
        out_specs=[tok(ff), tok(ff), tok(d), tok(d), tok(d), tok(d), tok(mix),
                   pl.BlockSpec((3 * TILE_ROWS, d), lambda i: (0, 0))],
        out_shape=[jax.ShapeDtypeStruct((t, ff), BF16), jax.ShapeDtypeStruct((t, ff), BF16),
                   jax.ShapeDtypeStruct((t, d), BF16), jax.ShapeDtypeStruct((t, d), BF16),
                   jax.ShapeDtypeStruct((t, d), F32), jax.ShapeDtypeStruct((t, d), BF16),
                   jax.ShapeDtypeStruct((t, mix), F32), jax.ShapeDtypeStruct((3 * TILE_ROWS, d), F32)],
        scratch_shapes=[pltpu.VMEM(w_out_g.shape, BF16), pltpu.VMEM(w1_g.shape, BF16), pltpu.VMEM(w2_g.shape, BF16),
                        pltpu.VMEM((tm, ff), F32)],
        compiler_params=_params(dimension_semantics=("arbitrary",)),
    )(x, yb, g2, gf, target, w_out_g, w1_g, w2_g)


def _mix_bwd(dy, u, xr_all, hs_all, c3_all, gates, conv_w, rconv_w, wa_bd, wx_bd, lam, g_nc, g_nr, parts):
    t = dy.shape[0]
    tm = TOKEN_TILE
    nt = t // tm
    hb = tm // TILE_ROWS
    npart = len(parts)

    def body(dy_ref, u_ref, uh_ref, xr_ref, hs_ref, hh_ref, c3_ref, r_ref, ig_ref, a_ref, mult_ref,
             cw_ref, rw_ref, wa_ref, wx_ref, lam_ref, gnc_ref, gnr_ref, *rest):
        part_refs, (du_ref, st_ref, dwa_ref, dwx_ref), rest = rest[:npart], rest[npart:npart + 4], rest[npart + 4:]
        arrived_refs, (dc_next, a_next, gs_next, dxr_next, send_sems, recv_sems) = rest[:npart], rest[npart:]
        exchange = _PartialExchange(part_refs, arrived_refs, send_sems, recv_sems)
        i = pl.program_id(0)

        @pl.when(i == 0)
        def _():
            exchange.start()
            dc_next[...] = jnp.zeros_like(dc_next)
            a_next[...] = jnp.zeros_like(a_next)
            gs_next[...] = jnp.zeros_like(gs_next)
            dxr_next[...] = jnp.zeros_like(dxr_next)
            st_ref[...] = jnp.zeros_like(st_ref)
            dwa_ref[...] = jnp.zeros_like(dwa_ref)
            dwx_ref[...] = jnp.zeros_like(dwx_ref)

        first_tile = i == nt - 1
        gate_b = u_ref[:, 0:CONV_W]
        gate_c = u_ref[:, CONV_W:2 * CONV_W]
        v = u_ref[:, 2 * CONV_W:3 * CONV_W]
        x_r = u_ref[:, 3 * CONV_W:3 * CONV_W + LRU_W]
        g = u_ref[:, 3 * CONV_W + LRU_W:]
        cv = gate_c * v
        cv_prev = jnp.where(first_tile, 0.0, uh_ref[:, CONV_W:2 * CONV_W] * uh_ref[:, 2 * CONV_W:3 * CONV_W])
        xin_prev = jnp.where(first_tile, 0.0, uh_ref[:, 3 * CONV_W:3 * CONV_W + LRU_W])
        hs_prev = jnp.where(first_tile, 0.0, hh_ref[...])

        def acc(block, val, width=LRU_W, row=0):
            r0 = block * TILE_ROWS + row
            st_ref[r0:r0 + 1, 0:width] += val

        conv3 = c3_ref[...]
        y_conv = gate_b * conv3
        ra = lax.rsqrt(jnp.mean(y_conv * y_conv, axis=-1, keepdims=True) + EPS)
        xha = y_conv * ra
        dna = dy_ref[:, :CONV_W]
        acc(PK_G_NORM_CONV, _colsum(dna * xha), CONV_W)
        dxha = dna * gnc_ref[...]
        dy_conv = ra * (dxha - xha * jnp.mean(dxha * xha, axis=-1, keepdims=True))
        du_ref[:, 0:CONV_W] = (dy_conv * conv3).astype(BF16)
        dc = dy_conv * gate_b
        cw = cw_ref[...]
        dcn = dc_next[...]
        dcv = cw[2:3] * dc + cw[1:2] * _shift_up(dc, 1, dcn) + cw[0:1] * _shift_up(dc, 2, dcn)
        dc_next[...] = dc[:TILE_ROWS]
        acc(PK_CONV_W, _colsum(dc * _shift_down(cv, 2, cv_prev)), CONV_W, 0)
        acc(PK_CONV_W, _colsum(dc * _shift_down(cv, 1, cv_prev)), CONV_W, 1)
        acc(PK_CONV_W, _colsum(dc * cv), CONV_W, 2)
        du_ref[:, CONV_W:2 * CONV_W] = (dcv * v).astype(BF16)
        du_ref[:, 2 * CONV_W:3 * CONV_W] = (dcv * gate_c).astype(BF16)

        hs = hs_ref[...]
        gelu, dgelu = _gelu_and_grad(g)
        y_rnn = hs * gelu
        rb = lax.rsqrt(jnp.mean(y_rnn * y_rnn, axis=-1, keepdims=True) + EPS)
        xhb = y_rnn * rb
        dnb = dy_ref[:, CONV_W:]
        acc(PK_G_NORM_RNN, _colsum(dnb * xhb))
        dxhb = dnb * gnr_ref[...]
        dy_rnn = rb * (dxhb - xhb * jnp.mean(dxhb * xhb, axis=-1, keepdims=True))
        du_ref[:, 3 * CONV_W + LRU_W:] = (dy_rnn * hs * dgelu).astype(BF16)
        dh = dy_rnn * gelu

        xr = xr_ref[...]
        xrb = xr.astype(BF16)
        sp, dsp = _softplus_neg(lam_ref[...])
        r, ig, a, mult = r_ref[...], ig_ref[...], a_ref[...], mult_ref[...]
        a_up = _shift_up(a, 1, a_next[...])
        a_next[...] = a[:TILE_ROWS]
        a_cum, gs = _scan_up(a_up, dh)
        gs = gs + a_cum * gs_next[0:1, :]
        gs_next[...] = gs[:TILE_ROWS]
        da = gs * _shift_down(hs, 1, hs_prev)
        gx = gs * xr
        di = gx * mult
        dmult = gx * ig
        dxr = gs * (mult * ig)
        dlog_a = da * a - dmult * ((a * a) / mult)
        acc(PK_LAMBDA, _colsum(dlog_a * r) * ((-LRU_C) * dsp))
        dpa = (dlog_a * ((-LRU_C) * sp)) * (r * (1.0 - r))
        dpx = di * (ig * (1.0 - ig))
        acc(PK_B_A, _colsum(dpa))
        acc(PK_B_X, _colsum(dpx))
        dpab = dpa.astype(BF16)
        dpxb = dpx.astype(BF16)
        dxr = dxr + _block_diag_dot_t(dpab, wa_ref) + _block_diag_dot_t(dpxb, wx_ref)
        for j in range(N_BD):
            cols = slice(j * BD, (j + 1) * BD)
            dwa_ref[j] += _dot_tn(xrb[:, cols], dpab[:, cols])
            dwx_ref[j] += _dot_tn(xrb[:, cols], dpxb[:, cols])

        acc(PK_RCONV_B, _colsum(dxr))
        rw = rw_ref[...]
        dxn = dxr_next[...]
        dx_r = (rw[3:4] * dxr + rw[2:3] * _shift_up(dxr, 1, dxn) + rw[1:2] * _shift_up(dxr, 2, dxn)
                + rw[0:1] * _shift_up(dxr, 3, dxn))
        dxr_next[...] = dxr[:TILE_ROWS]
        for k in range(3):
            acc(PK_RCONV_W, _colsum(dxr * _shift_down(x_r, 3 - k, xin_prev)), LRU_W, k)
        acc(PK_RCONV_W, _colsum(dxr * x_r), LRU_W, 3)
        du_ref[:, 3 * CONV_W:3 * CONV_W + LRU_W] = dx_r.astype(BF16)

        @pl.when(i == nt - 1)
        def _():
            exchange.wait()

    def full(a):
        nd = a.ndim
        return pl.BlockSpec(a.shape, lambda i: (0,) * nd)

    def tok(cols):
        return pl.BlockSpec((tm, cols), lambda i: (nt - 1 - i, 0))

    def halo(cols):
        return pl.BlockSpec((TILE_ROWS, cols), lambda i: (jnp.maximum((nt - 1 - i) * hb - 1, 0), 0))

    smalls = (conv_w, rconv_w, wa_bd, wx_bd, lam, g_nc, g_nr)
    outs = pl.pallas_call(
        body, name="mix_bwd", grid=(nt,),
        in_specs=[tok(CONV_W + LRU_W), tok(IN_COLS), halo(IN_COLS), tok(LRU_W), tok(LRU_W), halo(LRU_W), tok(CONV_W)]
        + [tok(LRU_W)] * 4 + [full(a) for a in smalls] + [ANY] * npart,
        out_specs=[tok(IN_COLS), pl.BlockSpec((8 * TILE_ROWS, LRU_W), lambda i: (0, 0)),
                   pl.BlockSpec((N_BD, BD, BD), lambda i: (0, 0, 0)), pl.BlockSpec((N_BD, BD, BD), lambda i: (0, 0, 0))]
        + [ANY] * npart,
        out_shape=[jax.ShapeDtypeStruct((t, IN_COLS), BF16), jax.ShapeDtypeStruct((8 * TILE_ROWS, LRU_W), F32),
                   jax.ShapeDtypeStruct((N_BD, BD, BD), F32), jax.ShapeDtypeStruct((N_BD, BD, BD), F32)]
        + [jax.ShapeDtypeStruct(a.shape, a.dtype) for a in parts],
        scratch_shapes=[pltpu.VMEM((TILE_ROWS, CONV_W), F32), pltpu.VMEM((TILE_ROWS, LRU_W), F32),
                        pltpu.VMEM((TILE_ROWS, LRU_W), F32), pltpu.VMEM((TILE_ROWS, LRU_W), F32),
                        pltpu.SemaphoreType.DMA((npart, 3)), pltpu.SemaphoreType.DMA((npart, 3))],
        compiler_params=_params(dimension_semantics=("arbitrary",)),
    )(dy, u, u, xr_all, hs_all, hs_all, c3_all, *gates, *smalls, *parts)
    return outs[:4], outs[4:]


def _in_bwd(dub, w_in_g, x, dx2, g1, parts):
    t, d = x.shape
    tm = TOKEN_TILE
    nt = t // tm
    npart = len(parts)

    def body(du_ref, win_ref, x_ref, dx2_ref, g1_ref, *rest):
        part_refs, (gx_ref, st_ref), rest = rest[:npart], rest[npart:npart + 2], rest[npart + 2:]
        arrived_refs, (send_sems, recv_sems) = rest[:npart], rest[npart:]
        exchange = _PartialExchange(part_refs, arrived_refs, send_sems, recv_sems)
        i = pl.program_id(0)

        @pl.when(i == 0)
        def _():
            exchange.start()
            st_ref[...] = jnp.zeros_like(st_ref)

        dh1 = _dot_nt(du_ref[:, 0:IN_SHARD], win_ref[0])
        for j in range(1, N_CHIPS):
            dh1 = dh1 + _dot_nt(du_ref[:, j * IN_SHARD:(j + 1) * IN_SHARD], win_ref[j])
        xv = x_ref[...]
        rstd = lax.rsqrt(jnp.mean(xv * xv, axis=-1, keepdims=True) + EPS)
        xh = xv * rstd
        st_ref[0:1, :] += _colsum(dh1 * xh)
        dxh = dh1 * g1_ref[...]
        gx_ref[...] = dx2_ref[...] + rstd * (dxh - xh * jnp.mean(dxh * xh, axis=-1, keepdims=True))

        @pl.when(i == nt - 1)
        def _():
            exchange.wait()

    def tok(cols):
        return pl.BlockSpec((tm, cols), lambda i: (i, 0))

    outs = pl.pallas_call(
        body, name="in_bwd", grid=(nt,),
        in_specs=[tok(IN_COLS), pl.BlockSpec(w_in_g.shape, lambda i: (0, 0, 0)), tok(d), tok(d),
                  pl.BlockSpec((1, d), lambda i: (0, 0))] + [ANY] * npart,
        out_specs=[tok(d), pl.BlockSpec((TILE_ROWS, d), lambda i: (0, 0))] + [ANY] * npart,
        out_shape=[jax.ShapeDtypeStruct((t, d), F32), jax.ShapeDtypeStruct((TILE_ROWS, d), F32)]
        + [jax.ShapeDtypeStruct(a.shape, a.dtype) for a in parts],
        scratch_shapes=[pltpu.SemaphoreType.DMA((npart, 3)), pltpu.SemaphoreType.DMA((npart, 3))],
        compiler_params=_params(dimension_semantics=("arbitrary",)),
    )(dub, w_in_g, x, dx2, g1, *parts)
    return outs[:2], outs[2:]


WGRAD_GEOMETRY = {
    "in": (512, IN_SHARD, lambda s, h: h, lambda s, h: s),
    "mlp_in": (512, D_MODEL, lambda s, h: h, lambda s, h: s),
    "mlp_out": (512, D_MODEL, lambda s, h: 2 * s + h, lambda s, h: 0),
    "out": (384, 512, lambda s, h: s, lambda s, h: h),
}
K_CHUNK = 512


def _sibling():
    x, y, c = _position()
    return (x, y, 1 - c)


def _wgrad(a, b, tag, core_chip, pack=None):
    t = a.shape[0]
    pr, pc, a_blk, b_blk = WGRAD_GEOMETRY[tag]
    nk = t // K_CHUNK
    mine = N_CHIPS
    riding = pack is not None

    def body(cc_ref, a_ref, b_ref, *rest):
        if riding:
            pack_ref, land_ref, p_ref, pb_ref, all_ref, stage, rbuf, send_sems, recv_sems, rsem, g_send, g_recv, g_local = rest
            gather = _PackGather(pack_ref, all_ref, g_send, g_recv, g_local)
        else:
            land_ref, p_ref, pb_ref, stage, rbuf, send_sems, recv_sems, rsem = rest
        ph, s = pl.program_id(0), pl.program_id(1)
        if riding:
            @pl.when((ph == 0) & (s == 0))
            def _():
                gather.start()

            @pl.when((ph == 1) & (s == N_CHIPS - 2))
            def _():
                gather.hand_over()
        slot = jnp.where(ph == 0, s, mine)
        acc = stage.at[slot]
        acc[...] = _dot_tn(a_ref[0:K_CHUNK, :], b_ref[0:K_CHUNK, :])
        for k in range(1, nk):
            acc[...] += _dot_tn(a_ref[k * K_CHUNK:(k + 1) * K_CHUNK, :], b_ref[k * K_CHUNK:(k + 1) * K_CHUNK, :])

        def push(k):
            return pltpu.make_async_remote_copy(src_ref=stage.at[k], dst_ref=land_ref.at[k], send_sem=send_sems.at[k],
                                                recv_sem=recv_sems.at[k], device_id=_sibling(), device_id_type=MESH)

        @pl.when(ph == 0)
        def _():
            push(s).start()

        @pl.when(ph == 1)
        def _():
            push(s).wait_recv()
            landed = pltpu.make_async_copy(land_ref.at[s], rbuf, rsem)
            landed.start()
            landed.wait()
            p = stage[mine] + rbuf[...]
            p_ref[0] = p
            pb_ref[0] = p.astype(BF16)

        @pl.when((ph == 1) & (s == N_CHIPS - 1))
        def _():
            for k in range(N_CHIPS):
                push(k).wait_send()
            if riding:
                gather.finish()

    def half(ph, cc):
        return jnp.where(ph == 0, 1 - cc[0], cc[0])

    def out_slot(ph, s, cc):
        return (jnp.where(ph == 0, 0, s), 0, 0)

    piece = jax.ShapeDtypeStruct((N_CHIPS, pr, pc), F32)
    in_specs = [pl.BlockSpec((t, pr), lambda ph, s, cc: (0, a_blk(s, half(ph, cc)))),
                pl.BlockSpec((t, pc), lambda ph, s, cc: (0, b_blk(s, half(ph, cc))))]
    out_specs = [ANY, pl.BlockSpec((1, pr, pc), out_slot), pl.BlockSpec((1, pr, pc), out_slot)]
    out_shape = [piece, piece, jax.ShapeDtypeStruct((N_CHIPS, pr, pc), BF16)]
    scratch = [pltpu.VMEM((N_CHIPS + 1, pr, pc), F32), pltpu.VMEM((pr, pc), F32),
               pltpu.SemaphoreType.DMA((N_CHIPS,)), pltpu.SemaphoreType.DMA((N_CHIPS,)), pltpu.SemaphoreType.DMA]
    operands = [a, b]
    if riding:
        in_specs.append(pl.BlockSpec(pack.shape, lambda ph, s, cc: (0, 0)))
        out_specs.append(ANY)
        out_shape.append(jax.ShapeDtypeStruct((N_DEVICES,) + pack.shape, pack.dtype))
        scratch += _PackGather.semaphores()
        operands.append(pack)
    return pl.pallas_call(
        body, name="wgrad_" + tag,
        grid_spec=pltpu.PrefetchScalarGridSpec(
            num_scalar_prefetch=1, grid=(2, N_CHIPS), in_specs=in_specs, out_specs=out_specs, scratch_shapes=scratch),
        out_shape=out_shape,
        compiler_params=_params(dimension_semantics=("arbitrary", "arbitrary")),
    )(core_chip, *operands)[1:]


def _other_chips(x, y):
    return [(1 - x, y), (x, 1 - y), (1 - x, 1 - y)]


class _ShardGather:
    def __init__(self, outs, send_sems, recv_sems):
        self.outs, self.send_sems, self.recv_sems = outs, send_sems, recv_sems
        x, y, c = _position()
        self.c, self.j = c, 2 * x + y
        self.sibling = (x, y, 1 - c)
        self.chips = _other_chips(x, y)

    def _half(self, w, chip, which):
        hr = self.outs[w].shape[1] // 2
        return self.outs[w].at[chip, pl.ds(which * hr, hr), :]

    def _copy(self, ref, w, k, to, src=None):
        return pltpu.make_async_remote_copy(src_ref=ref if src is None else src, dst_ref=ref, send_sem=self.send_sems.at[w, k],
                                            recv_sem=self.recv_sems.at[w, k], device_id=to, device_id_type=MESH)

    def ici(self, w, k, src=None):
        px, py = self.chips[k]
        return self._copy(self._half(w, self.j, self.c), w, k, (px, py, self.c), src)

    def landed(self, w, k):
        px, py = self.chips[k]
        return self._copy(self._half(w, 2 * px + py, self.c), w, k, (px, py, self.c))

    def hand_over(self, w, k):
        px, py = self.chips[k]
        return self._copy(self._half(w, 2 * px + py, self.c), w, 3 + k, self.sibling)

    def handed(self, w, k):
        px, py = self.chips[k]
        return self._copy(self._half(w, 2 * px + py, 1 - self.c), w, 3 + k, self.sibling)


def _gather_first(w_in, w_out, w1, w2, small):
    bigs = (w_in, w_out, w1, w2)
    nb = len(bigs)

    def body(win_ref, wout_ref, w1_ref, w2_ref, sm_ref, gin, gout, g1, g2, gsm, st_in, st_out, st_1, st_2,
             send_sems, recv_sems, sm_send, sm_recv, local_sems):
        srcs = (win_ref, wout_ref, w1_ref, w2_ref)
        stages = (st_in, st_out, st_1, st_2)
        outs = (gin, gout, g1, g2)
        plan = _ShardGather(outs[:1], send_sems, recv_sems)
        j, c = plan.j, plan.c
        for src, st in zip(srcs, stages):
            st[...] = src[...].astype(BF16)
        local = [pltpu.make_async_copy(stages[w], outs[w].at[j], local_sems.at[w]) for w in range(nb)]
        local.append(pltpu.make_async_copy(sm_ref, gsm.at[j], local_sems.at[nb]))
        for cp in local:
            cp.start()

        def small_copy(k):
            px, py = plan.chips[k]
            return pltpu.make_async_remote_copy(src_ref=sm_ref, dst_ref=gsm.at[j], send_sem=sm_send.at[k],
                                                recv_sem=sm_recv.at[k], device_id=(px, py, c), device_id_type=MESH)

        def small_landed(k):
            px, py = plan.chips[k]
            return pltpu.make_async_remote_copy(src_ref=sm_ref, dst_ref=gsm.at[2 * px + py], send_sem=sm_send.at[k],
                                                recv_sem=sm_recv.at[k], device_id=(px, py, c), device_id_type=MESH)

        hr = w_in.shape[0] // 2
        sends = [plan.ici(0, k, src=st_in.at[pl.ds(c * hr, hr), :]) for k in range(3)] + [small_copy(k) for k in range(3)]
        for cp in sends:
            cp.start()
        for k in range(3):
            plan.landed(0, k).wait_recv()
            fwd = plan.hand_over(0, k)
            fwd.start()
            sends.append(fwd)
        for k in range(3):
            small_landed(k).wait_recv()
        for k in range(3):
            plan.handed(0, k).wait_recv()
        for cp in sends:
            cp.wait_send()
        for cp in local:
            cp.wait()

    def gathered(a, dtype):
        return jax.ShapeDtypeStruct((N_CHIPS,) + a.shape, dtype)

    return pl.pallas_call(
        body, name="gather_first",
        in_specs=[VMEM] * 5, out_specs=[ANY] * 5,
        out_shape=[gathered(a, BF16) for a in bigs] + [gathered(small, F32)],
        scratch_shapes=[pltpu.VMEM(a.shape, BF16) for a in bigs]
        + [pltpu.SemaphoreType.DMA((1, 6)), pltpu.SemaphoreType.DMA((1, 6)), pltpu.SemaphoreType.DMA((3,)),
           pltpu.SemaphoreType.DMA((3,)), pltpu.SemaphoreType.DMA((nb + 1,))],
        compiler_params=_params(),
    )(*bigs, small)


class _PartialExchange:
    def __init__(self, parts, arrived, send_sems, recv_sems):
        self.parts, self.arrived, self.send_sems, self.recv_sems = parts, arrived, send_sems, recv_sems
        x, y, c = _position()
        self.c, self.j = c, 2 * x + y
        self.chips = _other_chips(x, y)

    def _copy(self, w, k, slot):
        px, py = self.chips[k]
        return pltpu.make_async_remote_copy(
            src_ref=self.parts[w].at[2 * px + py], dst_ref=self.arrived[w].at[slot], send_sem=self.send_sems.at[w, k],
            recv_sem=self.recv_sems.at[w, k], device_id=(px, py, self.c), device_id_type=MESH)

    def start(self):
        for w in range(len(self.parts)):
            for k in range(3):
                self._copy(w, k, self.j).start()

    def wait(self):
        for w in range(len(self.parts)):
            for k in range(3):
                px, py = self.chips[k]
                self._copy(w, k, 2 * px + py).wait()


class _PackGather:
    def __init__(self, p_ref, all_ref, send_sems, recv_sems, local_sem):
        self.p_ref, self.all_ref, self.send_sems, self.recv_sems, self.local_sem = p_ref, all_ref, send_sems, recv_sems, local_sem
        x, y, c = _position()
        self.me, self.sibling, self.c = (x, y, c), (x, y, 1 - c), c
        self.chips = _other_chips(x, y)

    @staticmethod
    def semaphores():
        return [pltpu.SemaphoreType.DMA((7,)), pltpu.SemaphoreType.DMA((7,)), pltpu.SemaphoreType.DMA]

    def _copy(self, k, block, to, from_pack=False):
        px, py, pc = block
        slot = self.all_ref.at[4 * px + 2 * py + pc]
        return pltpu.make_async_remote_copy(src_ref=self.p_ref if from_pack else slot, dst_ref=slot, send_sem=self.send_sems.at[k],
                                            recv_sem=self.recv_sems.at[k], device_id=to, device_id_type=MESH)

    def _mine(self):
        x, y, c = self.me
        return pltpu.make_async_copy(self.p_ref, self.all_ref.at[4 * x + 2 * y + c], self.local_sem)

    def _first(self):
        return [self._copy(0, self.me, self.sibling, True)] + [
            self._copy(1 + k, self.me, (*chip, self.c), True) for k, chip in enumerate(self.chips)]

    def _passed(self):
        return [self._copy(4 + k, (*chip, self.c), self.sibling) for k, chip in enumerate(self.chips)]

    def start(self):
        self._mine().start()
        for cp in self._first():
            cp.start()

    def hand_over(self):
        for k, chip in enumerate(self.chips):
            self._copy(1 + k, (*chip, self.c), self.me).wait_recv()
            self._passed()[k].start()

    def finish(self):
        self._copy(0, self.sibling, self.me).wait_recv()
        for k, chip in enumerate(self.chips):
            self._copy(4 + k, (*chip, 1 - self.c), self.me).wait_recv()
        for cp in self._first() + self._passed():
            cp.wait_send()
        self._mine().wait()


class _DirectGather:
    def __init__(self, p_ref, all_ref, send_sems, recv_sems, local_sem):
        self.p_ref, self.all_ref, self.send_sems, self.recv_sems, self.local_sem = p_ref, all_ref, send_sems, recv_sems, local_sem
        self.me = _position()

    semaphores = _PackGather.semaphores

    def _peer(self, r):
        x, y, c = self.me
        return ((1 - x) if r & 4 else x, (1 - y) if r & 2 else y, (1 - c) if r & 1 else c)

    def _copy(self, r, slot_of):
        px, py, pc = slot_of
        return pltpu.make_async_remote_copy(src_ref=self.p_ref, dst_ref=self.all_ref.at[4 * px + 2 * py + pc],
                                            send_sem=self.send_sems.at[r - 1], recv_sem=self.recv_sems.at[r - 1],
                                            device_id=self._peer(r), device_id_type=MESH)

    def _mine(self):
        x, y, c = self.me
        return pltpu.make_async_copy(self.p_ref, self.all_ref.at[4 * x + 2 * y + c], self.local_sem)

    def start(self):
        self._mine().start()
        for r in range(1, N_DEVICES):
            self._copy(r, self.me).start()

    def finish(self):
        for r in range(1, N_DEVICES):
            self._copy(r, self._peer(r)).wait()
        self._mine().wait()


def _adamw(w, g, m, v):
    m = ADAM_B1 * m + (1.0 - ADAM_B1) * g
    v = ADAM_B2 * v + (1.0 - ADAM_B2) * (g * g)
    m_hat = m / ADAM_BC1
    v_hat = v / ADAM_BC2
    delta = -ADAM_LR * (m_hat / (jnp.sqrt(v_hat) + ADAM_EPS) + ADAM_WD * w)
    return delta, m, v


JOIN_SUB = 4


def _join(tag, shard_shape, part, arrived, core_chip, block=None):
    pr, pc = WGRAD_GEOMETRY[tag][:2]
    rb = pr // JOIN_SUB
    by_rows = shard_shape[1] == pc
    riding = block is not None

    def body(cc_ref, p_ref, r1_ref, r2_ref, r3_ref, *rest):
        if riding:
            blk_ref, g_ref, all_ref, stage, send_sems, recv_sems, local_sems, b_send, b_recv, b_local = rest
            gather = _DirectGather(blk_ref, all_ref, b_send, b_recv, b_local)
        else:
            g_ref, stage, send_sems, recv_sems, local_sems = rest
        i = pl.program_id(0)
        c = cc_ref[0]
        if riding:
            @pl.when(i == 0)
            def _():
                gather.start()

        def window(core, k):
            if by_rows:
                return g_ref.at[pl.ds((core * JOIN_SUB + k) * rb, rb), :]
            return g_ref.at[pl.ds(k * rb, rb), pl.ds(core * pc, pc)]

        def keep(k):
            return pltpu.make_async_copy(stage.at[k], window(c, k), local_sems.at[k])

        def push(k):
            return pltpu.make_async_remote_copy(src_ref=stage.at[k], dst_ref=window(c, k), send_sem=send_sems.at[k],
                                                recv_sem=recv_sems.at[k], device_id=_sibling(), device_id_type=MESH)

        def pushed(k):
            return pltpu.make_async_remote_copy(src_ref=stage.at[k], dst_ref=window(1 - c, k), send_sem=send_sems.at[k],
                                                recv_sem=recv_sems.at[k], device_id=_sibling(), device_id_type=MESH)

        stage[i] = ((p_ref[0] + r1_ref[0].astype(F32)) + r2_ref[0].astype(F32)) + r3_ref[0].astype(F32)
        keep(i).start()
        push(i).start()

        @pl.when(i == JOIN_SUB - 1)
        def _():
            for k in range(JOIN_SUB):
                keep(k).wait()
                push(k).wait_send()
                pushed(k).wait_recv()
            if riding:
                gather.finish()

    def partial(off):
        return pl.BlockSpec((1, rb, pc), lambda i, cc: ((cc[1] + off) % N_CHIPS, i, 0))

    in_specs = [partial(0), partial(1), partial(2), partial(3)]
    out_specs = [ANY]
    out_shape = [jax.ShapeDtypeStruct(shard_shape, F32)]
    scratch = [pltpu.VMEM((JOIN_SUB, rb, pc), F32), pltpu.SemaphoreType.DMA((JOIN_SUB,)),
               pltpu.SemaphoreType.DMA((JOIN_SUB,)), pltpu.SemaphoreType.DMA((JOIN_SUB,))]
    operands = [part, arrived, arrived, arrived]
    if riding:
        in_specs.append(pl.BlockSpec(block.shape, lambda i, cc: (0, 0)))
        out_specs.append(ANY)
        out_shape.append(jax.ShapeDtypeStruct((N_DEVICES,) + block.shape, block.dtype))
        scratch += _DirectGather.semaphores()
        operands.append(block)
    outs = pl.pallas_call(
        body, name="join_" + tag,
        grid_spec=pltpu.PrefetchScalarGridSpec(
            num_scalar_prefetch=1, grid=(JOIN_SUB,), in_specs=in_specs, out_specs=out_specs, scratch_shapes=scratch),
        out_shape=out_shape,
        compiler_params=_params(dimension_semantics=("arbitrary",)),
    )(core_chip, *operands)
    return outs if riding else outs[0]


def _adamw_big(w, g, m, v, name):
    rows, cols = w.shape
    rb = 256 if rows % 256 == 0 else rows

    def body(w_ref, g_ref, m_ref, v_ref, go_ref, d_ref, nm_ref, nv_ref):
        g = g_ref[...]
        go_ref[...] = g
        d_ref[...], nm_ref[...], nv_ref[...] = _adamw(w_ref[...], g, m_ref[...], v_ref[...])

    spec = pl.BlockSpec((rb, cols), lambda i: (i, 0))
    return pl.pallas_call(
        body, name=name, grid=(rows // rb,), in_specs=[spec] * 4, out_specs=[spec] * 4,
        out_shape=[jax.ShapeDtypeStruct(w.shape, F32)] * 4,
        compiler_params=_params(dimension_semantics=("arbitrary",)),
    )(w, g, m, v)


def _small_step(packs, mix_g_blocks, w_pack, m_pack, v_pack, conv_wmv, rconv_wmv):
    rows, cols = packs.shape[1:]
    cshard = conv_wmv.shape[2]
    rshard = rconv_wmv.shape[2]
    mix_row = PK_MIX_G * TILE_ROWS

    def body(all_ref, blk_ref, w_ref, m_ref, v_ref, cw_ref, rw_ref, g_ref, d_ref, nm_ref, nv_ref, co_ref, ro_ref):
        total = all_ref[0]
        late = blk_ref[0]
        for k in range(1, N_DEVICES):
            total = total + all_ref[k]
            late = late + blk_ref[k]
        g_ref[...] = total
        g_ref[mix_row:mix_row + TILE_ROWS, :] = late
        g = g_ref[...]
        d_ref[...], nm_ref[...], nv_ref[...] = _adamw(w_ref[...], g, m_ref[...], v_ref[...])

        x, y, _ = _position()
        j = 2 * x + y
        cblk = total[PK_CONV_W * 8:PK_CONV_W * 8 + 8, :]
        rblk = total[PK_RCONV_W * 8:PK_RCONV_W * 8 + 8, :]
        cg = cblk[:, 0:cshard]
        rg = rblk[:, 0:rshard]
        for k in range(1, N_CHIPS):
            cg = jnp.where(j == k, cblk[:, k * cshard:(k + 1) * cshard], cg)
            rg = jnp.where(j == k, rblk[:, k * rshard:(k + 1) * rshard], rg)
        co_ref[0] = cg
        co_ref[1], co_ref[2], co_ref[3] = _adamw(cw_ref[0], cg, cw_ref[1], cw_ref[2])
        ro_ref[0] = rg
        ro_ref[1], ro_ref[2], ro_ref[3] = _adamw(rw_ref[0], rg, rw_ref[1], rw_ref[2])

    pack = [jax.ShapeDtypeStruct((rows, cols), F32)] * 4
    return pl.pallas_call(
        body, name="small_grads_step", in_specs=[VMEM] * 7, out_specs=[VMEM] * 6,
        out_shape=pack + [jax.ShapeDtypeStruct((4, TILE_ROWS, cshard), F32), jax.ShapeDtypeStruct((4, TILE_ROWS, rshard), F32)],
        compiler_params=_params(),
    )(packs, mix_g_blocks, w_pack, m_pack, v_pack, conv_wmv, rconv_wmv)


def _blk(a):
    a = a.reshape(-1, a.shape[-1])
    return jnp.pad(a, ((0, TILE_ROWS - a.shape[0]), (0, D_MODEL - a.shape[1])))


def _zero_blk():
    return jnp.zeros((TILE_ROWS, D_MODEL), F32)


def _pack_params(p, pre):
    get = lambda n: p[pre + n]
    return jnp.concatenate([
        _blk(get("g_norm_rnn")), _blk(get("rnn_conv_b")), _blk(get("b_a")), _blk(get("b_x")), _blk(get("lru_lambda")),
        _zero_blk(), _zero_blk(), _blk(get("g_norm_conv")), _blk(get("final_norm_g").reshape(1, -1)), _blk(get("norm_mlp_g")),
        _zero_blk(), _blk(get("norm_mix_g")), get("w_a").reshape(64, D_MODEL), get("w_x").reshape(64, D_MODEL)], axis=0)


def _to_block_diag(w):
    w4 = w.reshape(N_BD, 4, 64, 64)
    eye = jnp.eye(4, dtype=w.dtype)
    return (w4[:, :, :, None, :] * eye[None, :, None, :, None]).reshape(N_BD, BD, BD)


def _from_block_diag(d):
    d5 = d.reshape(N_BD, 4, 64, 4, 64)
    return jnp.stack([d5[:, q, :, q, :] for q in range(4)], axis=1).reshape(64, D_MODEL)


def _pad_rows(a):
    return jnp.pad(a, ((0, TILE_ROWS - a.shape[0]), (0, 0)))


_NAMES = ['norm_mix_g', 'w_in', 'conv_w', 'rnn_conv_w', 'rnn_conv_b', 'w_a', 'b_a', 'w_x', 'b_x', 'lru_lambda',
          'g_norm_conv', 'g_norm_rnn', 'w_out', 'norm_mlp_g', 'w_mlp_in', 'w_mlp_out', 'final_norm_g']


def kernel(x, norm_mix_g, w_in, conv_w, rnn_conv_w, rnn_conv_b, w_a, b_a, w_x, b_x, lru_lambda, g_norm_conv, g_norm_rnn, w_out, norm_mlp_g, w_mlp_in, w_mlp_out, final_norm_g, loss_target, m_norm_mix_g, m_w_in, m_conv_w, m_rnn_conv_w, m_rnn_conv_b, m_w_a, m_b_a, m_w_x, m_b_x, m_lru_lambda, m_g_norm_conv, m_g_norm_rnn, m_w_out, m_norm_mlp_g, m_w_mlp_in, m_w_mlp_out, m_final_norm_g, v_norm_mix_g, v_w_in, v_conv_w, v_rnn_conv_w, v_rnn_conv_b, v_w_a, v_b_a, v_w_x, v_b_x, v_lru_lambda, v_g_norm_conv, v_g_norm_rnn, v_w_out, v_norm_mlp_g, v_w_mlp_in, v_w_mlp_out, v_final_norm_g):
    args = dict(locals())
    p = {}
    for n in _NAMES:
        for pre in ("", "m_", "v_"):
            a = args[pre + n]
            p[pre + n] = a[0] if a.ndim >= 3 else a
    xs = x[0]
    target = loss_target[0]
    core_chip = jnp.stack([lax.axis_index("c"), 2 * lax.axis_index("x") + lax.axis_index("y")]).astype(jnp.int32)
    cshard = p["conv_w"].shape[1]
    rshard = p["rnn_conv_w"].shape[1]

    small = jnp.concatenate([_pad_rows(p["conv_w"]), _pad_rows(p["rnn_conv_w"])], axis=1)
    w_in_g, w_out_g, w1_g, w2_g, small_g = _gather_first(p["w_in"], p["w_out"], p["w_mlp_in"], p["w_mlp_out"], small)
    conv_full = small_g[:, :3, :cshard].transpose(1, 0, 2).reshape(3, CONV_W)
    rconv_full = small_g[:, :4, cshard:].transpose(1, 0, 2).reshape(4, LRU_W)
    wa_bd = _to_block_diag(p["w_a"]).astype(BF16)
    wx_bd = _to_block_diag(p["w_x"]).astype(BF16)
    gf = p["final_norm_g"].reshape(1, -1)
    lru = (wa_bd, p["b_a"], wx_bd, p["b_x"], p["lru_lambda"], p["g_norm_conv"], p["g_norm_rnn"])

    (u, h1b, xr, hs, c3, yb, *gates), (w_out_g, w1_g, w2_g) = _fwd_mix(
        xs, p["norm_mix_g"], w_in_g, conv_full, rconv_full, p["rnn_conv_b"], *lru, (w_out_g, w1_g, w2_g))
    zb, dpb, h2b, dx3b, dx2, dx2b, dy, st_mlp = _mlp_fwd_bwd(
        xs, yb, w_out_g.reshape(-1, D_MODEL), w1_g, w2_g.reshape(-1, D_MODEL), p["norm_mlp_g"], gf, target)

    part_out = _wgrad(yb, dx2b, "out", core_chip)
    part_1 = _wgrad(h2b, dpb, "mlp_in", core_chip)
    part_2 = _wgrad(zb, dx3b, "mlp_out", core_chip)
    (dub, st_mix, dwa_bd, dwx_bd), arrived_mlp = _mix_bwd(
        dy, u, xr, hs, c3, gates, conv_full, rconv_full, wa_bd, wx_bd, p["lru_lambda"], p["g_norm_conv"], p["g_norm_rnn"],
        (part_out[1], part_1[1], part_2[1]))
    pack = jnp.concatenate([st_mix, st_mlp, _zero_blk(), _from_block_diag(dwa_bd), _from_block_diag(dwx_bd)], axis=0)
    *part_in, packs = _wgrad(h1b, dub, "in", core_chip, pack)
    (grad_x, st_in), arrived_in = _in_bwd(dub, w_in_g, xs, dx2, p["norm_mix_g"], (part_in[1],))
    big = {}
    for n, tag, part, arrived in (("w_out", "out", part_out, arrived_mlp[0]), ("w_mlp_in", "mlp_in", part_1, arrived_mlp[1]),
                                  ("w_mlp_out", "mlp_out", part_2, arrived_mlp[2]), ("w_in", "in", part_in, arrived_in[0])):
        if tag == "out":
            g, mix_g_blocks = _join(tag, p[n].shape, part[0], arrived, core_chip, st_in)
        else:
            g = _join(tag, p[n].shape, part[0], arrived, core_chip)
        big[n] = _adamw_big(p[n], g, p["m_" + n], p["v_" + n], "adamw_" + tag)

    conv_wmv = jnp.stack([_pad_rows(p[pre + "conv_w"]) for pre in ("", "m_", "v_")])
    rconv_wmv = jnp.stack([_pad_rows(p[pre + "rnn_conv_w"]) for pre in ("", "m_", "v_")])
    g_pack, d_pack, m_pack, v_pack, conv_out, rconv_out = _small_step(
        packs, mix_g_blocks, _pack_params(p, ""), _pack_params(p, "m_"), _pack_params(p, "v_"), conv_wmv, rconv_wmv)

    def unpack(pk, kind):
        def vec(b, width=D_MODEL):
            return pk[b * 8:b * 8 + 1, :width]
        return {
            "norm_mix_g": vec(PK_MIX_G), "rnn_conv_b": vec(PK_RCONV_B), "b_a": vec(PK_B_A), "b_x": vec(PK_B_X),
            "lru_lambda": vec(PK_LAMBDA), "g_norm_conv": vec(PK_G_NORM_CONV, CONV_W), "g_norm_rnn": vec(PK_G_NORM_RNN),
            "norm_mlp_g": vec(PK_MLP_G), "final_norm_g": vec(PK_FINAL_G).reshape(-1),
            "w_a": pk[PK_W_A * 8:PK_W_A * 8 + 64].reshape(1, 16, 64, 64), "w_x": pk[PK_W_X * 8:PK_W_X * 8 + 64].reshape(1, 16, 64, 64),
            "conv_w": conv_out[kind, :3][None], "rnn_conv_w": rconv_out[kind, :4][None],
            "w_in": big["w_in"][kind][None], "w_out": big["w_out"][kind][None],
            "w_mlp_in": big["w_mlp_in"][kind][None], "w_mlp_out": big["w_mlp_out"][kind][None],
        }

    outs = [unpack(pk, kind) for kind, pk in enumerate((g_pack, d_pack, m_pack, v_pack))]
    for o in outs:
        for n in ("norm_mix_g", "rnn_conv_b", "b_a", "b_x", "lru_lambda", "g_norm_conv", "g_norm_rnn", "norm_mlp_g"):
            o[n] = o[n].reshape(1, -1)
    loss = g_pack[PK_LOSS * 8, 0]
    return (loss, grad_x[None], *[o[n] for o in outs for n in _NAMES])
```

```python
import functools
import math

import jax
import jax.numpy as jnp
from jax import lax
from jax.experimental import pallas as pl
from jax.experimental.pallas import tpu as pltpu

F32 = jnp.float32
BF16 = jnp.bfloat16
MESH = pl.DeviceIdType.MESH
ANY = pl.BlockSpec(memory_space=pl.ANY)
VMEM = pl.BlockSpec(memory_space=pltpu.VMEM)

EPS = 1e-6
LRU_C = 8.0
D_MODEL = 1024
CONV_W = 512
LRU_W = 1024
IN_COLS = 3 * CONV_W + 2 * LRU_W
IN_SHARD = IN_COLS // 4
N_CHIPS = 4
N_DEVICES = 8
BD = 256
N_BD = LRU_W // BD

ADAM_LR = 0.001
ADAM_B1 = 0.9
ADAM_B2 = 0.999
ADAM_EPS = 1e-08
ADAM_WD = 0.01
ADAM_STEP = 10
ADAM_BC1 = 1.0 - ADAM_B1 ** ADAM_STEP
ADAM_BC2 = 1.0 - ADAM_B2 ** ADAM_STEP

TILE_ROWS = 8
TOKEN_TILE = 256
VMEM_LIMIT = 56 * 1024 * 1024

PK_G_NORM_RNN, PK_RCONV_B, PK_B_A, PK_B_X, PK_LAMBDA, PK_RCONV_W, PK_CONV_W, PK_G_NORM_CONV = range(8)
PK_FINAL_G, PK_MLP_G, PK_LOSS, PK_MIX_G = 8, 9, 10, 11
PK_W_A = 12
PK_W_X = 20
PK_BLOCKS = 28
PK_ROWS = PK_BLOCKS * TILE_ROWS


def _params(**kw):
    return pltpu.CompilerParams(vmem_limit_bytes=VMEM_LIMIT, **kw)


def _position():
    x, y, c = lax.axis_index("x"), lax.axis_index("y"), lax.axis_index("c")
    return x, y, c


def _sigmoid(v):
    return 1.0 / (1.0 + jnp.exp(-v))


def _one_minus_square(log_a, a):
    v = 2.0 * log_a
    series = -v * (1.0 + v * (0.5 + v * (1.0 / 6.0)))
    return jnp.where(v > -0.01, series, 1.0 - a * a)


_GELU_C = math.sqrt(2.0 / math.pi)
_GELU_K = 0.044715


def _gelu_and_grad(g):
    th = jnp.tanh(_GELU_C * (g + _GELU_K * g * g * g))
    gelu = 0.5 * g * (1.0 + th)
    dgelu = 0.5 * (1.0 + th) + 0.5 * g * (1.0 - th * th) * (_GELU_C * (1.0 + 3.0 * _GELU_K * g * g))
    return gelu, dgelu


def _rows(shape):
    return lax.broadcasted_iota(jnp.int32, shape, 0)


def _shift_down(v, k, prev8):
    rolled = pltpu.roll(v, k, 0)
    halo = pltpu.roll(prev8, k, 0)
    head = jnp.where(_rows(halo.shape) < k, halo, rolled[:TILE_ROWS])
    return jnp.concatenate([head, rolled[TILE_ROWS:]], axis=0)


def _shift_up(v, k, next8):
    n = v.shape[0]
    rolled = pltpu.roll(v, n - k, 0)
    halo = pltpu.roll(next8, TILE_ROWS - k, 0)
    tail = jnp.where(_rows(halo.shape) >= TILE_ROWS - k, halo, rolled[n - TILE_ROWS:])
    return jnp.concatenate([rolled[: n - TILE_ROWS], tail], axis=0)


def _scan_down(a, b):
    n, w = a.shape
    row = _rows(a.shape)
    s = 1
    while s < n:
        if s < TILE_ROWS:
            keep = row >= s
            b = jnp.where(keep, a * pltpu.roll(b, s, 0) + b, b)
            a = jnp.where(keep, a * pltpu.roll(a, s, 0), a)
        else:
            b = a * jnp.concatenate([jnp.zeros((s, w), F32), b[:n - s]], axis=0) + b
            a = a * jnp.concatenate([jnp.ones((s, w), F32), a[:n - s]], axis=0)
        s *= 2
    return a, b


def _scan_up(a, b):
    n, w = a.shape
    row = _rows(a.shape)
    s = 1
    while s < n:
        if s < TILE_ROWS:
            keep = row < n - s
            b = jnp.where(keep, a * pltpu.roll(b, n - s, 0) + b, b)
            a = jnp.where(keep, a * pltpu.roll(a, n - s, 0), a)
        else:
            b = a * jnp.concatenate([b[s:], jnp.zeros((s, w), F32)], axis=0) + b
            a = a * jnp.concatenate([a[s:], jnp.ones((s, w), F32)], axis=0)
        s *= 2
    return a, b


def _softplus_neg(lam):
    e = jnp.exp(-jnp.abs(lam))
    log1p_e = jnp.where(e < 1e-2, e * (1.0 - e * (0.5 - e * (1.0 / 3.0 - e * 0.25))), jnp.log(1.0 + e))
    sp = jnp.maximum(-lam, 0.0) + log1p_e
    dsp = -_sigmoid(-lam)
    return sp, dsp


def _block_diag_dot(vb, w_ref):
    return jnp.concatenate(
        [jnp.dot(vb[:, j * BD:(j + 1) * BD], w_ref[j], preferred_element_type=F32) for j in range(N_BD)], axis=1)


def _block_diag_dot_t(vb, w_ref):
    return jnp.concatenate(
        [lax.dot_general(vb[:, j * BD:(j + 1) * BD], w_ref[j], (((1,), (1,)), ((), ())), preferred_element_type=F32)
         for j in range(N_BD)], axis=1)


def _dot_nt(a, b):
    return lax.dot_general(a, b, (((1,), (1,)), ((), ())), preferred_element_type=F32)


def _dot_tn(a, b):
    return lax.dot_general(a, b, (((0,), (0,)), ((), ())), preferred_element_type=F32)


def _lru_gates(xr, wa_ref, ba, wx_ref, bx, sp):
    xrb = xr.astype(BF16)
    r = _sigmoid(_block_diag_dot(xrb, wa_ref) + ba)
    ig = _sigmoid(_block_diag_dot(xrb, wx_ref) + bx)
    log_a = (-LRU_C) * r * sp
    a = jnp.exp(log_a)
    mult = jnp.sqrt(_one_minus_square(log_a, a))
    return r, ig, a, mult


def _colsum(v):
    return jnp.sum(v, axis=0, keepdims=True)


N_FWD_OUT = 10


def _fwd_mix(x, g1, w_in_g, conv_w, rconv_w, rconv_b, wa_bd, b_a, wx_bd, b_x, lam, g_nc, g_nr, later):
    t, d = x.shape
    tm = TOKEN_TILE
    nt = t // tm
    nl = len(later)
    hand_over_at = [min(nt - 1, (nt * (w + 1)) // nl) for w in range(nl)]

    def body(x_ref, g1_ref, win_ref, cw_ref, rw_ref, rb_ref, wa_ref, ba_ref, wx_ref, bx_ref, lam_ref, gnc_ref, gnr_ref,
             *rest):
        later_in, outs, rest = rest[:nl], rest[nl:nl + N_FWD_OUT], rest[nl + N_FWD_OUT:]
        u_ref, h1_ref, xr_ref, hs_ref, c3_ref, y_ref, r_ref, ig_ref, a_ref, mult_ref = outs
        later_out, (cv_prev, xin_prev, h_prev, send_sems, recv_sems) = rest[:nl], rest[nl:]
        del later_in
        step = pl.program_id(0)
        plan = _ShardGather(later_out, send_sems, recv_sems)

        @pl.when(step == 0)
        def _():
            cv_prev[...] = jnp.zeros_like(cv_prev)
            xin_prev[...] = jnp.zeros_like(xin_prev)
            h_prev[...] = jnp.zeros_like(h_prev)
            for w in range(nl):
                for k in range(3):
                    plan.ici(w, k).start()

        for w in range(nl):
            @pl.when(step == hand_over_at[w])
            def _(w=w):
                for k in range(3):
                    plan.landed(w, k).wait_recv()
                    plan.hand_over(w, k).start()

        xv = x_ref[...]
        rstd = lax.rsqrt(jnp.mean(xv * xv, axis=-1, keepdims=True) + EPS)
        h1b = ((xv * rstd) * g1_ref[...]).astype(BF16)
        h1_ref[...] = h1b
        for j in range(N_CHIPS):
            u_ref[:, j * IN_SHARD:(j + 1) * IN_SHARD] = jnp.dot(h1b, win_ref[j], preferred_element_type=F32)
        gate_b = u_ref[:, 0:CONV_W]
        cv = u_ref[:, CONV_W:2 * CONV_W] * u_ref[:, 2 * CONV_W:3 * CONV_W]
        x_r = u_ref[:, 3 * CONV_W:3 * CONV_W + LRU_W]
        g = u_ref[:, 3 * CONV_W + LRU_W:]

        cw = cw_ref[...]
        cvp = cv_prev[...]
        conv3 = cw[0:1] * _shift_down(cv, 2, cvp) + cw[1:2] * _shift_down(cv, 1, cvp) + cw[2:3] * cv
        cv_prev[...] = cv[tm - TILE_ROWS:]
        c3_ref[...] = conv3
        y_conv = gate_b * conv3

        rw = rw_ref[...]
        xp = xin_prev[...]
        xr = (rw[0:1] * _shift_down(x_r, 3, xp) + rw[1:2] * _shift_down(x_r, 2, xp)
              + rw[2:3] * _shift_down(x_r, 1, xp) + rw[3:4] * x_r) + rb_ref[...]
        xin_prev[...] = x_r[tm - TILE_ROWS:]
        xr_ref[...] = xr
        sp, _ = _softplus_neg(lam_ref[...])
        r, ig, a, mult = _lru_gates(xr, wa_ref, ba_ref[...], wx_ref, bx_ref[...], sp)
        r_ref[...] = r
        ig_ref[...] = ig
        a_ref[...] = a
        mult_ref[...] = mult
        a_cum, h = _scan_down(a, mult * (ig * xr))
        h = h + a_cum * h_prev[...]
        h_prev[...] = h[tm - 1:tm]
        hs_ref[...] = h
        gelu, _ = _gelu_and_grad(g)
        y_rnn = h * gelu

        na = y_conv * lax.rsqrt(jnp.mean(y_conv * y_conv, axis=-1, keepdims=True) + EPS) * gnc_ref[...]
        nb = y_rnn * lax.rsqrt(jnp.mean(y_rnn * y_rnn, axis=-1, keepdims=True) + EPS) * gnr_ref[...]
        y_ref[:, :CONV_W] = na.astype(BF16)
        y_ref[:, CONV_W:] = nb.astype(BF16)

        @pl.when(step == nt - 1)
        def _():
            for w in range(nl):
                for k in range(3):
                    plan.handed(w, k).wait_recv()
                    plan.ici(w, k).wait_send()
                    plan.hand_over(w, k).wait_send()

    def full(a):
        nd = a.ndim
        return pl.BlockSpec(a.shape, lambda i: (0,) * nd)

    def tok(cols):
        return pl.BlockSpec((tm, cols), lambda i: (i, 0))

    def act(cols, dtype=F32):
        return jax.ShapeDtypeStruct((t, cols), dtype)

    smalls = (g1, w_in_g, conv_w, rconv_w, rconv_b, wa_bd, b_a, wx_bd, b_x, lam, g_nc, g_nr)
    n_in = 1 + len(smalls)
    outs = pl.pallas_call(
        body, name="fwd_mix", grid=(nt,),
        in_specs=[tok(d)] + [full(a) for a in smalls] + [ANY] * nl,
        out_specs=[tok(IN_COLS), tok(d), tok(LRU_W), tok(LRU_W), tok(CONV_W), tok(CONV_W + LRU_W)]
        + [tok(LRU_W)] * 4 + [ANY] * nl,
        out_shape=[act(IN_COLS), act(d, BF16), act(LRU_W), act(LRU_W), act(CONV_W), act(CONV_W + LRU_W, BF16)]
        + [act(LRU_W)] * 4 + [jax.ShapeDtypeStruct(a.shape, a.dtype) for a in later],
        input_output_aliases={n_in + w: N_FWD_OUT + w for w in range(nl)},
        scratch_shapes=[pltpu.VMEM((TILE_ROWS, CONV_W), F32), pltpu.VMEM((TILE_ROWS, LRU_W), F32),
                        pltpu.VMEM((1, LRU_W), F32), pltpu.SemaphoreType.DMA((nl, 6)), pltpu.SemaphoreType.DMA((nl, 6))],
        compiler_params=_params(dimension_semantics=("arbitrary",)),
    )(x, *smalls, *later)
    return outs[:N_FWD_OUT], outs[N_FWD_OUT:]


def _mlp_fwd_bwd(x, yb, w_out_g, w1_g, w2_g, g2, gf, target):
    t, d = x.shape
    tm = TOKEN_TILE
    ff = w2_g.shape[0]
    mix = w_out_g.shape[0]
    ffs = ff // N_CHIPS

    def body(x_ref, y_ref, g2_ref, gf_ref, tgt_ref, wout_hbm, w1_hbm, w2_hbm,
             z_ref, dp_ref, h2_ref, dx3b_ref, dx2_ref, dx2b_ref, dy_ref, st_ref, wout, w1, w2, p_ref):
        @pl.when(pl.program_id(0) == 0)
        def _():
            pltpu.sync_copy(wout_hbm, wout)
            pltpu.sync_copy(w1_hbm, w1)
            pltpu.sync_copy(w2_hbm, w2)
            st_ref[...] = jnp.zeros_like(st_ref)

        x2 = x_ref[...] + jnp.dot(y_ref[...], wout[...], preferred_element_type=F32)
        r2 = lax.rsqrt(jnp.mean(x2 * x2, axis=-1, keepdims=True) + EPS)
        xh2 = x2 * r2
        g2v = g2_ref[...]
        h2b = (xh2 * g2v).astype(BF16)
        h2_ref[...] = h2b
        for j in range(N_CHIPS):
            p_ref[:, j * ffs:(j + 1) * ffs] = jnp.dot(h2b, w1[j], preferred_element_type=F32)
        rp = jnp.maximum(p_ref[...], 0.0)
        zb = (rp * rp).astype(BF16)
        z_ref[...] = zb
        x3 = x2 + jnp.dot(zb, w2[...], preferred_element_type=F32)
        r3 = lax.rsqrt(jnp.mean(x3 * x3, axis=-1, keepdims=True) + EPS)
        xh3 = x3 * r3
        gfv = gf_ref[...]
        err = xh3 * gfv - tgt_ref[...]
        loss = (0.5 / d) * jnp.sum(err * err)
        dout = err * (1.0 / d)
        st_ref[PK_FINAL_G * 8 - 64:PK_FINAL_G * 8 - 63, :] += _colsum(dout * xh3)
        st_ref[PK_LOSS * 8 - 64:PK_LOSS * 8 - 63, :] += jnp.zeros((1, d), F32) + loss
        dxh3 = dout * gfv
        dx3 = r3 * (dxh3 - xh3 * jnp.mean(dxh3 * xh3, axis=-1, keepdims=True))
        dx3b = dx3.astype(BF16)
        dx3b_ref[...] = dx3b
        dpb = (_dot_nt(dx3b, w2[...]) * (2.0 * rp)).astype(BF16)
        dp_ref[...] = dpb
        dh2 = _dot_nt(dpb[:, 0:ffs], w1[0])
        for j in range(1, N_CHIPS):
            dh2 = dh2 + _dot_nt(dpb[:, j * ffs:(j + 1) * ffs], w1[j])
        st_ref[PK_MLP_G * 8 - 64:PK_MLP_G * 8 - 63, :] += _colsum(dh2 * xh2)
        dxh2 = dh2 * g2v
        dx2 = dx3 + r2 * (dxh2 - xh2 * jnp.mean(dxh2 * xh2, axis=-1, keepdims=True))
        dx2_ref[...] = dx2
        dx2b = dx2.astype(BF16)
        dx2b_ref[...] = dx2b
        dy_ref[...] = _dot_nt(dx2b, wout[...])

    def tok(cols):
        return pl.BlockSpec((tm, cols), lambda i: (i, 0))

    def row(cols):
        return pl.BlockSpec((1, cols), lambda i: (0, 0))

    return pl.pallas_call(
        body, name="mlp_fwd_bwd", grid=(t // tm,),
        in_specs=[tok(d), tok(mix), row(d), row(d), tok(d), ANY, ANY, ANY],
        out_specs=[tok(ff), tok(ff), tok(d), tok(d), tok(d), tok(d), tok(mix),
                   pl.BlockSpec((3 * TILE_ROWS, d), lambda i: (0, 0))],
        out_shape=[jax.ShapeDtypeStruct((t, ff), BF16), jax.ShapeDtypeStruct((t, ff), BF16),
                   jax.ShapeDtypeStruct((t, d), BF16), jax.ShapeDtypeStruct((t, d), BF16),
                   jax.ShapeDtypeStruct((t, d), F32), jax.ShapeDtypeStruct((t, d), BF16),
                   jax.ShapeDtypeStruct((t, mix), F32), jax.ShapeDtypeStruct((3 * TILE_ROWS, d), F32)],
        scratch_shapes=[pltpu.VMEM(w_out_g.shape, BF16), pltpu.VMEM(w1_g.shape, BF16), pltpu.VMEM(w2_g.shape, BF16),
                        pltpu.VMEM((tm, ff), F32)],
        compiler_params=_params(dimension_semantics=("arbitrary",)),
    )(x, yb, g2, gf, target, w_out_g, w1_g, w2_g)


def _mix_bwd(dy, u, xr_all, hs_all, c3_all, gates, conv_w, rconv_w, wa_bd, wx_bd, lam, g_nc, g_nr, parts):
    t = dy.shape[0]
    tm = TOKEN_TILE
    nt = t // tm
    hb = tm // TILE_ROWS
    npart = len(parts)

    def body(dy_ref, u_ref, uh_ref, xr_ref, hs_ref, hh_ref, c3_ref, r_ref, ig_ref, a_ref, mult_ref,
             cw_ref, rw_ref, wa_ref, wx_ref, lam_ref, gnc_ref, gnr_ref, *rest):
        part_refs, (du_ref, st_ref, dwa_ref, dwx_ref), rest = rest[:npart], rest[npart:npart + 4], rest[npart + 4:]
        arrived_refs, (dc_next, a_next, gs_next, dxr_next, send_sems, recv_sems) = rest[:npart], rest[npart:]
        exchange = _PartialExchange(part_refs, arrived_refs, send_sems, recv_sems)
        i = pl.program_id(0)

        @pl.when(i == 0)
        def _():
            exchange.start()
            dc_next[...] = jnp.zeros_like(dc_next)
            a_next[...] = jnp.zeros_like(a_next)
            gs_next[...] = jnp.zeros_like(gs_next)
            dxr_next[...] = jnp.zeros_like(dxr_next)
            st_ref[...] = jnp.zeros_like(st_ref)
            dwa_ref[...] = jnp.zeros_like(dwa_ref)
            dwx_ref[...] = jnp.zeros_like(dwx_ref)

        first_tile = i == nt - 1
        gate_b = u_ref[:, 0:CONV_W]
        gate_c = u_ref[:, CONV_W:2 * CONV_W]
        v = u_ref[:, 2 * CONV_W:3 * CONV_W]
        x_r = u_ref[:, 3 * CONV_W:3 * CONV_W + LRU_W]
        g = u_ref[:, 3 * CONV_W + LRU_W:]
        cv = gate_c * v
        cv_prev = jnp.where(first_tile, 0.0, uh_ref[:, CONV_W:2 * CONV_W] * uh_ref[:, 2 * CONV_W:3 * CONV_W])
        xin_prev = jnp.where(first_tile, 0.0, uh_ref[:, 3 * CONV_W:3 * CONV_W + LRU_W])
        hs_prev = jnp.where(first_tile, 0.0, hh_ref[...])

        def acc(block, val, width=LRU_W, row=0):
            r0 = block * TILE_ROWS + row
            st_ref[r0:r0 + 1, 0:width] += val

        conv3 = c3_ref[...]
        y_conv = gate_b * conv3
        ra = lax.rsqrt(jnp.mean(y_conv * y_conv, axis=-1, keepdims=True) + EPS)
        xha = y_conv * ra
        dna = dy_ref[:, :CONV_W]
        acc(PK_G_NORM_CONV, _colsum(dna * xha), CONV_W)
        dxha = dna * gnc_ref[...]
        dy_conv = ra * (dxha - xha * jnp.mean(dxha * xha, axis=-1, keepdims=True))
        du_ref[:, 0:CONV_W] = (dy_conv * conv3).astype(BF16)
        dc = dy_conv * gate_b
        cw = cw_ref[...]
        dcn = dc_next[...]
        dcv = cw[2:3] * dc + cw[1:2] * _shift_up(dc, 1, dcn) + cw[0:1] * _shift_up(dc, 2, dcn)
        dc_next[...] = dc[:TILE_ROWS]
        acc(PK_CONV_W, _colsum(dc * _shift_down(cv, 2, cv_prev)), CONV_W, 0)
        acc(PK_CONV_W, _colsum(dc * _shift_down(cv, 1, cv_prev)), CONV_W, 1)
        acc(PK_CONV_W, _colsum(dc * cv), CONV_W, 2)
        du_ref[:, CONV_W:2 * CONV_W] = (dcv * v).astype(BF16)
        du_ref[:, 2 * CONV_W:3 * CONV_W] = (dcv * gate_c).astype(BF16)

        hs = hs_ref[...]
        gelu, dgelu = _gelu_and_grad(g)
        y_rnn = hs * gelu
        rb = lax.rsqrt(jnp.mean(y_rnn * y_rnn, axis=-1, keepdims=True) + EPS)
        xhb = y_rnn * rb
        dnb = dy_ref[:, CONV_W:]
        acc(PK_G_NORM_RNN, _colsum(dnb * xhb))
        dxhb = dnb * gnr_ref[...]
        dy_rnn = rb * (dxhb - xhb * jnp.mean(dxhb * xhb, axis=-1, keepdims=True))
        du_ref[:, 3 * CONV_W + LRU_W:] = (dy_rnn * hs * dgelu).astype(BF16)
        dh = dy_rnn * gelu

        xr = xr_ref[...]
        xrb = xr.astype(BF16)
        sp, dsp = _softplus_neg(lam_ref[...])
        r, ig, a, mult = r_ref[...], ig_ref[...], a_ref[...], mult_ref[...]
        a_up = _shift_up(a, 1, a_next[...])
        a_next[...] = a[:TILE_ROWS]
        a_cum, gs = _scan_up(a_up, dh)
        gs = gs + a_cum * gs_next[0:1, :]
        gs_next[...] = gs[:TILE_ROWS]
        da = gs * _shift_down(hs, 1, hs_prev)
        gx = gs * xr
        di = gx * mult
        dmult = gx * ig
        dxr = gs * (mult * ig)
        dlog_a = da * a - dmult * ((a * a) / mult)
        acc(PK_LAMBDA, _colsum(dlog_a * r) * ((-LRU_C) * dsp))
        dpa = (dlog_a * ((-LRU_C) * sp)) * (r * (1.0 - r))
        dpx = di * (ig * (1.0 - ig))
        acc(PK_B_A, _colsum(dpa))
        acc(PK_B_X, _colsum(dpx))
        dpab = dpa.astype(BF16)
        dpxb = dpx.astype(BF16)
        dxr = dxr + _block_diag_dot_t(dpab, wa_ref) + _block_diag_dot_t(dpxb, wx_ref)
        for j in range(N_BD):
            cols = slice(j * BD, (j + 1) * BD)
            dwa_ref[j] += _dot_tn(xrb[:, cols], dpab[:, cols])
            dwx_ref[j] += _dot_tn(xrb[:, cols], dpxb[:, cols])

        acc(PK_RCONV_B, _colsum(dxr))
        rw = rw_ref[...]
        dxn = dxr_next[...]
        dx_r = (rw[3:4] * dxr + rw[2:3] * _shift_up(dxr, 1, dxn) + rw[1:2] * _shift_up(dxr, 2, dxn)
                + rw[0:1] * _shift_up(dxr, 3, dxn))
        dxr_next[...] = dxr[:TILE_ROWS]
        for k in range(3):
            acc(PK_RCONV_W, _colsum(dxr * _shift_down(x_r, 3 - k, xin_prev)), LRU_W, k)
        acc(PK_RCONV_W, _colsum(dxr * x_r), LRU_W, 3)
        du_ref[:, 3 * CONV_W:3 * CONV_W + LRU_W] = dx_r.astype(BF16)

        @pl.when(i == nt - 1)
        def _():
            exchange.wait()

    def full(a):
        nd = a.ndim
        return pl.BlockSpec(a.shape, lambda i: (0,) * nd)

    def tok(cols):
        return pl.BlockSpec((tm, cols), lambda i: (nt - 1 - i, 0))

    def halo(cols):
        return pl.BlockSpec((TILE_ROWS, cols), lambda i: (jnp.maximum((nt - 1 - i) * hb - 1, 0), 0))

    smalls = (conv_w, rconv_w, wa_bd, wx_bd, lam, g_nc, g_nr)
    outs = pl.pallas_call(
        body, name="mix_bwd", grid=(nt,),
        in_specs=[tok(CONV_W + LRU_W), tok(IN_COLS), halo(IN_COLS), tok(LRU_W), tok(LRU_W), halo(LRU_W), tok(CONV_W)]
        + [tok(LRU_W)] * 4 + [full(a) for a in smalls] + [ANY] * npart,
        out_specs=[tok(IN_COLS), pl.BlockSpec((8 * TILE_ROWS, LRU_W), lambda i: (0, 0)),
                   pl.BlockSpec((N_BD, BD, BD), lambda i: (0, 0, 0)), pl.BlockSpec((N_BD, BD, BD), lambda i: (0, 0, 0))]
        + [ANY] * npart,
        out_shape=[jax.ShapeDtypeStruct((t, IN_COLS), BF16), jax.ShapeDtypeStruct((8 * TILE_ROWS, LRU_W), F32),
                   jax.ShapeDtypeStruct((N_BD, BD, BD), F32), jax.ShapeDtypeStruct((N_BD, BD, BD), F32)]
        + [jax.ShapeDtypeStruct(a.shape, a.dtype) for a in parts],
        scratch_shapes=[pltpu.VMEM((TILE_ROWS, CONV_W), F32), pltpu.VMEM((TILE_ROWS, LRU_W), F32),
                        pltpu.VMEM((TILE_ROWS, LRU_W), F32), pltpu.VMEM((TILE_ROWS, LRU_W), F32),
                        pltpu.SemaphoreType.DMA((npart, 3)), pltpu.SemaphoreType.DMA((npart, 3))],
        compiler_params=_params(dimension_semantics=("arbitrary",)),
    )(dy, u, u, xr_all, hs_all, hs_all, c3_all, *gates, *smalls, *parts)
    return outs[:4], outs[4:]


def _in_bwd(dub, w_in_g, x, dx2, g1, parts, joins, core_chip):
    t, d = x.shape
    tm = TOKEN_TILE
    nt = t // tm
    npart = len(parts)
    nj = len(joins)
    geometry = []
    for tag, shape, _, _ in joins:
        pr, pc = WGRAD_GEOMETRY[tag][:2]
        every = 1 if pr % (nt * 16) == 0 else 2
        geometry.append((pr, pc, pr * every // nt, every, shape[1] == pc))

    def body(cc_ref, du_ref, win_ref, x_ref, dx2_ref, g1_ref, *rest):
        sums, rest = [rest[4 * w:4 * w + 4] for w in range(nj)], rest[4 * nj:]
        part_refs, (gx_ref, st_ref), rest = rest[:npart], rest[npart:npart + 2], rest[npart + 2:]
        arrived_refs, joined, rest = rest[:npart], rest[npart:npart + nj], rest[npart + nj:]
        stages, (send_sems, recv_sems, j_local, j_send, j_recv) = rest[:nj], rest[nj:]
        exchange = _PartialExchange(part_refs, arrived_refs, send_sems, recv_sems)
        i = pl.program_id(0)
        c = cc_ref[0]

        def window(w, core, row0, rows):
            pr, pc, _, _, by_rows = geometry[w]
            if by_rows:
                return joined[w].at[pl.ds(core * pr + row0, rows), :]
            return joined[w].at[pl.ds(row0, rows), pl.ds(core * pc, pc)]

        def to_sibling(w, src, core, row0, rows):
            return pltpu.make_async_remote_copy(src_ref=src, dst_ref=window(w, core, row0, rows), send_sem=j_send.at[w],
                                                recv_sem=j_recv.at[w], device_id=_sibling(), device_id_type=MESH)

        @pl.when(i == 0)
        def _():
            exchange.start()
            st_ref[...] = jnp.zeros_like(st_ref)

        for w in range(nj):
            pr, pc, rb, every, _ = geometry[w]

            @pl.when(i % every == 0)
            def _(w=w, rb=rb, every=every):
                p_ref, r1_ref, r2_ref, r3_ref = sums[w]
                row0 = pl.multiple_of((i // every) * rb, rb)
                rows = stages[w].at[pl.ds(row0, rb), :]
                rows[...] = ((p_ref[0] + r1_ref[0].astype(F32)) + r2_ref[0].astype(F32)) + r3_ref[0].astype(F32)
                pltpu.make_async_copy(rows, window(w, c, row0, rb), j_local.at[w]).start()
                to_sibling(w, rows, c, row0, rb).start()

        dh1 = _dot_nt(du_ref[:, 0:IN_SHARD], win_ref[0])
        for j in range(1, N_CHIPS):
            dh1 = dh1 + _dot_nt(du_ref[:, j * IN_SHARD:(j + 1) * IN_SHARD], win_ref[j])
        xv = x_ref[...]
        rstd = lax.rsqrt(jnp.mean(xv * xv, axis=-1, keepdims=True) + EPS)
        xh = xv * rstd
        st_ref[0:1, :] += _colsum(dh1 * xh)
        dxh = dh1 * g1_ref[...]
        gx_ref[...] = dx2_ref[...] + rstd * (dxh - xh * jnp.mean(dxh * xh, axis=-1, keepdims=True))

        @pl.when(i == nt - 1)
        def _():
            exchange.wait()
            for w in range(nj):
                pr = geometry[w][0]
                pltpu.make_async_copy(stages[w], window(w, c, 0, pr), j_local.at[w]).wait()
                to_sibling(w, stages[w], 1 - c, 0, pr).wait()

    def tok(cols):
        return pl.BlockSpec((tm, cols), lambda i, cc: (i, 0))

    def partial(w, off):
        pr, pc, rb, every, _ = geometry[w]
        return pl.BlockSpec((1, rb, pc), lambda i, cc: ((cc[1] + off) % N_CHIPS, i // every, 0))

    sum_specs, sum_operands = [], []
    for w, (_, _, own, arrived) in enumerate(joins):
        sum_specs += [partial(w, off) for off in range(N_CHIPS)]
        sum_operands += [own, arrived, arrived, arrived]
    dma = pltpu.SemaphoreType.DMA
    outs = pl.pallas_call(
        body, name="in_bwd",
        grid_spec=pltpu.PrefetchScalarGridSpec(
            num_scalar_prefetch=1, grid=(nt,),
            in_specs=[tok(IN_COLS), pl.BlockSpec(w_in_g.shape, lambda i, cc: (0, 0, 0)), tok(d), tok(d),
                      pl.BlockSpec((1, d), lambda i, cc: (0, 0))] + sum_specs + [ANY] * npart,
            out_specs=[tok(d), pl.BlockSpec((TILE_ROWS, d), lambda i, cc: (0, 0))] + [ANY] * (npart + nj),
            scratch_shapes=[pltpu.VMEM((g[0], g[1]), F32) for g in geometry]
            + [dma((npart, 3)), dma((npart, 3)), dma((nj,)), dma((nj,)), dma((nj,))]),
        out_shape=[jax.ShapeDtypeStruct((t, d), F32), jax.ShapeDtypeStruct((TILE_ROWS, d), F32)]
        + [jax.ShapeDtypeStruct(a.shape, a.dtype) for a in parts]
        + [jax.ShapeDtypeStruct(shape, F32) for _, shape, _, _ in joins],
        compiler_params=_params(dimension_semantics=("arbitrary",)),
    )(core_chip, dub, w_in_g, x, dx2, g1, *sum_operands, *parts)
    return outs[:2], outs[2:2 + npart], outs[2 + npart:]


WGRAD_GEOMETRY = {
    "in": (512, IN_SHARD, lambda s, h: h, lambda s, h: s),
    "mlp_in": (512, D_MODEL, lambda s, h: h, lambda s, h: s),
    "mlp_out": (512, D_MODEL, lambda s, h: 2 * s + h, lambda s, h: 0),
    "out": (384, 512, lambda s, h: s, lambda s, h: h),
}
K_CHUNK = 512


def _sibling():
    x, y, c = _position()
    return (x, y, 1 - c)


def _wgrad(a, b, tag, core_chip, pack=None):
    t = a.shape[0]
    pr, pc, a_blk, b_blk = WGRAD_GEOMETRY[tag]
    nk = t // K_CHUNK
    mine = N_CHIPS
    riding = pack is not None

    def body(cc_ref, a_ref, b_ref, *rest):
        if riding:
            pack_ref, land_ref, p_ref, pb_ref, all_ref, stage, rbuf, send_sems, recv_sems, rsem, g_send, g_recv, g_local = rest
            gather = _PackGather(pack_ref, all_ref, g_send, g_recv, g_local)
        else:
            land_ref, p_ref, pb_ref, stage, rbuf, send_sems, recv_sems, rsem = rest
        ph, s = pl.program_id(0), pl.program_id(1)
        if riding:
            @pl.when((ph == 0) & (s == 0))
            def _():
                gather.start()

            @pl.when((ph == 1) & (s == N_CHIPS - 2))
            def _():
                gather.hand_over()
        slot = jnp.where(ph == 0, s, mine)
        acc = stage.at[slot]
        acc[...] = _dot_tn(a_ref[0:K_CHUNK, :], b_ref[0:K_CHUNK, :])
        for k in range(1, nk):
            acc[...] += _dot_tn(a_ref[k * K_CHUNK:(k + 1) * K_CHUNK, :], b_ref[k * K_CHUNK:(k + 1) * K_CHUNK, :])

        def push(k):
            return pltpu.make_async_remote_copy(src_ref=stage.at[k], dst_ref=land_ref.at[k], send_sem=send_sems.at[k],
                                                recv_sem=recv_sems.at[k], device_id=_sibling(), device_id_type=MESH)

        @pl.when(ph == 0)
        def _():
            push(s).start()

        @pl.when(ph == 1)
        def _():
            push(s).wait_recv()
            landed = pltpu.make_async_copy(land_ref.at[s], rbuf, rsem)
            landed.start()
            landed.wait()
            p = stage[mine] + rbuf[...]
            p_ref[0] = p
            pb_ref[0] = p.astype(BF16)

        @pl.when((ph == 1) & (s == N_CHIPS - 1))
        def _():
            for k in range(N_CHIPS):
                push(k).wait_send()
            if riding:
                gather.finish()

    def half(ph, cc):
        return jnp.where(ph == 0, 1 - cc[0], cc[0])

    def out_slot(ph, s, cc):
        return (jnp.where(ph == 0, 0, s), 0, 0)

    piece = jax.ShapeDtypeStruct((N_CHIPS, pr, pc), F32)
    in_specs = [pl.BlockSpec((t, pr), lambda ph, s, cc: (0, a_blk(s, half(ph, cc)))),
                pl.BlockSpec((t, pc), lambda ph, s, cc: (0, b_blk(s, half(ph, cc))))]
    out_specs = [ANY, pl.BlockSpec((1, pr, pc), out_slot), pl.BlockSpec((1, pr, pc), out_slot)]
    out_shape = [piece, piece, jax.ShapeDtypeStruct((N_CHIPS, pr, pc), BF16)]
    scratch = [pltpu.VMEM((N_CHIPS + 1, pr, pc), F32), pltpu.VMEM((pr, pc), F32),
               pltpu.SemaphoreType.DMA((N_CHIPS,)), pltpu.SemaphoreType.DMA((N_CHIPS,)), pltpu.SemaphoreType.DMA]
    operands = [a, b]
    if riding:
        in_specs.append(pl.BlockSpec(pack.shape, lambda ph, s, cc: (0, 0)))
        out_specs.append(ANY)
        out_shape.append(jax.ShapeDtypeStruct((N_DEVICES,) + pack.shape, pack.dtype))
        scratch += _PackGather.semaphores()
        operands.append(pack)
    return pl.pallas_call(
        body, name="wgrad_" + tag,
        grid_spec=pltpu.PrefetchScalarGridSpec(
            num_scalar_prefetch=1, grid=(2, N_CHIPS), in_specs=in_specs, out_specs=out_specs, scratch_shapes=scratch),
        out_shape=out_shape,
        compiler_params=_params(dimension_semantics=("arbitrary", "arbitrary")),
    )(core_chip, *operands)[1:]


def _other_chips(x, y):
    return [(1 - x, y), (x, 1 - y), (1 - x, 1 - y)]


class _ShardGather:
    def __init__(self, outs, send_sems, recv_sems):
        self.outs, self.send_sems, self.recv_sems = outs, send_sems, recv_sems
        x, y, c = _position()
        self.c, self.j = c, 2 * x + y
        self.sibling = (x, y, 1 - c)
        self.chips = _other_chips(x, y)

    def _half(self, w, chip, which):
        hr = self.outs[w].shape[1] // 2
        return self.outs[w].at[chip, pl.ds(which * hr, hr), :]

    def _copy(self, ref, w, k, to, src=None):
        return pltpu.make_async_remote_copy(src_ref=ref if src is None else src, dst_ref=ref, send_sem=self.send_sems.at[w, k],
                                            recv_sem=self.recv_sems.at[w, k], device_id=to, device_id_type=MESH)

    def ici(self, w, k, src=None):
        px, py = self.chips[k]
        return self._copy(self._half(w, self.j, self.c), w, k, (px, py, self.c), src)

    def landed(self, w, k):
        px, py = self.chips[k]
        return self._copy(self._half(w, 2 * px + py, self.c), w, k, (px, py, self.c))

    def hand_over(self, w, k):
        px, py = self.chips[k]
        return self._copy(self._half(w, 2 * px + py, self.c), w, 3 + k, self.sibling)

    def handed(self, w, k):
        px, py = self.chips[k]
        return self._copy(self._half(w, 2 * px + py, 1 - self.c), w, 3 + k, self.sibling)


def _gather_first(w_in, w_out, w1, w2, small):
    bigs = (w_in, w_out, w1, w2)
    nb = len(bigs)

    def body(win_ref, wout_ref, w1_ref, w2_ref, sm_ref, gin, gout, g1, g2, gsm, st_in, st_out, st_1, st_2,
             send_sems, recv_sems, sm_send, sm_recv, local_sems):
        srcs = (win_ref, wout_ref, w1_ref, w2_ref)
        stages = (st_in, st_out, st_1, st_2)
        outs = (gin, gout, g1, g2)
        plan = _ShardGather(outs[:1], send_sems, recv_sems)
        j, c = plan.j, plan.c
        for src, st in zip(srcs, stages):
            st[...] = src[...].astype(BF16)
        local = [pltpu.make_async_copy(stages[w], outs[w].at[j], local_sems.at[w]) for w in range(nb)]
        local.append(pltpu.make_async_copy(sm_ref, gsm.at[j], local_sems.at[nb]))
        for cp in local:
            cp.start()

        def small_copy(k):
            px, py = plan.chips[k]
            return pltpu.make_async_remote_copy(src_ref=sm_ref, dst_ref=gsm.at[j], send_sem=sm_send.at[k],
                                                recv_sem=sm_recv.at[k], device_id=(px, py, c), device_id_type=MESH)

        def small_landed(k):
            px, py = plan.chips[k]
            return pltpu.make_async_remote_copy(src_ref=sm_ref, dst_ref=gsm.at[2 * px + py], send_sem=sm_send.at[k],
                                                recv_sem=sm_recv.at[k], device_id=(px, py, c), device_id_type=MESH)

        hr = w_in.shape[0] // 2
        sends = [plan.ici(0, k, src=st_in.at[pl.ds(c * hr, hr), :]) for k in range(3)] + [small_copy(k) for k in range(3)]
        for cp in sends:
            cp.start()
        for k in range(3):
            plan.landed(0, k).wait_recv()
            fwd = plan.hand_over(0, k)
            fwd.start()
            sends.append(fwd)
        for k in range(3):
            small_landed(k).wait_recv()
        for k in range(3):
            plan.handed(0, k).wait_recv()
        for cp in sends:
            cp.wait_send()
        for cp in local:
            cp.wait()

    def gathered(a, dtype):
        return jax.ShapeDtypeStruct((N_CHIPS,) + a.shape, dtype)

    return pl.pallas_call(
        body, name="gather_first",
        in_specs=[VMEM] * 5, out_specs=[ANY] * 5,
        out_shape=[gathered(a, BF16) for a in bigs] + [gathered(small, F32)],
        scratch_shapes=[pltpu.VMEM(a.shape, BF16) for a in bigs]
        + [pltpu.SemaphoreType.DMA((1, 6)), pltpu.SemaphoreType.DMA((1, 6)), pltpu.SemaphoreType.DMA((3,)),
           pltpu.SemaphoreType.DMA((3,)), pltpu.SemaphoreType.DMA((nb + 1,))],
        compiler_params=_params(),
    )(*bigs, small)


class _PartialExchange:
    def __init__(self, parts, arrived, send_sems, recv_sems):
        self.parts, self.arrived, self.send_sems, self.recv_sems = parts, arrived, send_sems, recv_sems
        x, y, c = _position()
        self.c, self.j = c, 2 * x + y
        self.chips = _other_chips(x, y)

    def _copy(self, w, k, slot):
        px, py = self.chips[k]
        return pltpu.make_async_remote_copy(
            src_ref=self.parts[w].at[2 * px + py], dst_ref=self.arrived[w].at[slot], send_sem=self.send_sems.at[w, k],
            recv_sem=self.recv_sems.at[w, k], device_id=(px, py, self.c), device_id_type=MESH)

    def start(self):
        for w in range(len(self.parts)):
            for k in range(3):
                self._copy(w, k, self.j).start()

    def wait(self):
        for w in range(len(self.parts)):
            for k in range(3):
                px, py = self.chips[k]
                self._copy(w, k, 2 * px + py).wait()


class _PackGather:
    def __init__(self, p_ref, all_ref, send_sems, recv_sems, local_sem):
        self.p_ref, self.all_ref, self.send_sems, self.recv_sems, self.local_sem = p_ref, all_ref, send_sems, recv_sems, local_sem
        x, y, c = _position()
        self.me, self.sibling, self.c = (x, y, c), (x, y, 1 - c), c
        self.chips = _other_chips(x, y)

    @staticmethod
    def semaphores():
        return [pltpu.SemaphoreType.DMA((7,)), pltpu.SemaphoreType.DMA((7,)), pltpu.SemaphoreType.DMA]

    def _copy(self, k, block, to, from_pack=False):
        px, py, pc = block
        slot = self.all_ref.at[4 * px + 2 * py + pc]
        return pltpu.make_async_remote_copy(src_ref=self.p_ref if from_pack else slot, dst_ref=slot, send_sem=self.send_sems.at[k],
                                            recv_sem=self.recv_sems.at[k], device_id=to, device_id_type=MESH)

    def _mine(self):
        x, y, c = self.me
        return pltpu.make_async_copy(self.p_ref, self.all_ref.at[4 * x + 2 * y + c], self.local_sem)

    def _first(self):
        return [self._copy(0, self.me, self.sibling, True)] + [
            self._copy(1 + k, self.me, (*chip, self.c), True) for k, chip in enumerate(self.chips)]

    def _passed(self):
        return [self._copy(4 + k, (*chip, self.c), self.sibling) for k, chip in enumerate(self.chips)]

    def start(self):
        self._mine().start()
        for cp in self._first():
            cp.start()

    def hand_over(self):
        for k, chip in enumerate(self.chips):
            self._copy(1 + k, (*chip, self.c), self.me).wait_recv()
            self._passed()[k].start()

    def finish(self):
        self._copy(0, self.sibling, self.me).wait_recv()
        for k, chip in enumerate(self.chips):
            self._copy(4 + k, (*chip, 1 - self.c), self.me).wait_recv()
        for cp in self._first() + self._passed():
            cp.wait_send()
        self._mine().wait()


class _DirectGather:
    def __init__(self, p_ref, all_ref, send_sems, recv_sems, local_sem):
        self.p_ref, self.all_ref, self.send_sems, self.recv_sems, self.local_sem = p_ref, all_ref, send_sems, recv_sems, local_sem
        self.me = _position()

    semaphores = _PackGather.semaphores

    def _peer(self, r):
        x, y, c = self.me
        return ((1 - x) if r & 4 else x, (1 - y) if r & 2 else y, (1 - c) if r & 1 else c)

    def _copy(self, r, slot_of):
        px, py, pc = slot_of
        return pltpu.make_async_remote_copy(src_ref=self.p_ref, dst_ref=self.all_ref.at[4 * px + 2 * py + pc],
                                            send_sem=self.send_sems.at[r - 1], recv_sem=self.recv_sems.at[r - 1],
                                            device_id=self._peer(r), device_id_type=MESH)

    def _mine(self):
        x, y, c = self.me
        return pltpu.make_async_copy(self.p_ref, self.all_ref.at[4 * x + 2 * y + c], self.local_sem)

    def start(self):
        self._mine().start()
        for r in range(1, N_DEVICES):
            self._copy(r, self.me).start()

    def finish(self):
        for r in range(1, N_DEVICES):
            self._copy(r, self._peer(r)).wait()
        self._mine().wait()


def _adamw(w, g, m, v):
    m = ADAM_B1 * m + (1.0 - ADAM_B1) * g
    v = ADAM_B2 * v + (1.0 - ADAM_B2) * (g * g)
    m_hat = m / ADAM_BC1
    v_hat = v / ADAM_BC2
    delta = -ADAM_LR * (m_hat / (jnp.sqrt(v_hat) + ADAM_EPS) + ADAM_WD * w)
    return delta, m, v


JOIN_SUB = 4


def _join(tag, shard_shape, part, arrived, core_chip, block=None):
    pr, pc = WGRAD_GEOMETRY[tag][:2]
    rb = pr // JOIN_SUB
    by_rows = shard_shape[1] == pc
    riding = block is not None

    def body(cc_ref, p_ref, r1_ref, r2_ref, r3_ref, *rest):
        if riding:
            blk_ref, g_ref, all_ref, stage, send_sems, recv_sems, local_sems, b_send, b_recv, b_local = rest
            gather = _DirectGather(blk_ref, all_ref, b_send, b_recv, b_local)
        else:
            g_ref, stage, send_sems, recv_sems, local_sems = rest
        i = pl.program_id(0)
        c = cc_ref[0]
        if riding:
            @pl.when(i == 0)
            def _():
                gather.start()

        def window(core, k):
            if by_rows:
                return g_ref.at[pl.ds((core * JOIN_SUB + k) * rb, rb), :]
            return g_ref.at[pl.ds(k * rb, rb), pl.ds(core * pc, pc)]

        def keep(k):
            return pltpu.make_async_copy(stage.at[k], window(c, k), local_sems.at[k])

        def push(k):
            return pltpu.make_async_remote_copy(src_ref=stage.at[k], dst_ref=window(c, k), send_sem=send_sems.at[k],
                                                recv_sem=recv_sems.at[k], device_id=_sibling(), device_id_type=MESH)

        def pushed(k):
            return pltpu.make_async_remote_copy(src_ref=stage.at[k], dst_ref=window(1 - c, k), send_sem=send_sems.at[k],
                                                recv_sem=recv_sems.at[k], device_id=_sibling(), device_id_type=MESH)

        stage[i] = ((p_ref[0] + r1_ref[0].astype(F32)) + r2_ref[0].astype(F32)) + r3_ref[0].astype(F32)
        keep(i).start()
        push(i).start()

        @pl.when(i == JOIN_SUB - 1)
        def _():
            for k in range(JOIN_SUB):
                keep(k).wait()
                push(k).wait_send()
                pushed(k).wait_recv()
            if riding:
                gather.finish()

    def partial(off):
        return pl.BlockSpec((1, rb, pc), lambda i, cc: ((cc[1] + off) % N_CHIPS, i, 0))

    in_specs = [partial(0), partial(1), partial(2), partial(3)]
    out_specs = [ANY]
    out_shape = [jax.ShapeDtypeStruct(shard_shape, F32)]
    scratch = [pltpu.VMEM((JOIN_SUB, rb, pc), F32), pltpu.SemaphoreType.DMA((JOIN_SUB,)),
               pltpu.SemaphoreType.DMA((JOIN_SUB,)), pltpu.SemaphoreType.DMA((JOIN_SUB,))]
    operands = [part, arrived, arrived, arrived]
    if riding:
        in_specs.append(pl.BlockSpec(block.shape, lambda i, cc: (0, 0)))
        out_specs.append(ANY)
        out_shape.append(jax.ShapeDtypeStruct((N_DEVICES,) + block.shape, block.dtype))
        scratch += _DirectGather.semaphores()
        operands.append(block)
    outs = pl.pallas_call(
        body, name="join_" + tag,
        grid_spec=pltpu.PrefetchScalarGridSpec(
            num_scalar_prefetch=1, grid=(JOIN_SUB,), in_specs=in_specs, out_specs=out_specs, scratch_shapes=scratch),
        out_shape=out_shape,
        compiler_params=_params(dimension_semantics=("arbitrary",)),
    )(core_chip, *operands)
    return outs if riding else outs[0]


def _adamw_big(w, g, m, v, name):
    rows, cols = w.shape
    rb = 256 if rows % 256 == 0 else rows

    def body(w_ref, g_ref, m_ref, v_ref, go_ref, d_ref, nm_ref, nv_ref):
        g = g_ref[...]
        go_ref[...] = g
        d_ref[...], nm_ref[...], nv_ref[...] = _adamw(w_ref[...], g, m_ref[...], v_ref[...])

    spec = pl.BlockSpec((rb, cols), lambda i: (i, 0))
    return pl.pallas_call(
        body, name=name, grid=(rows // rb,), in_specs=[spec] * 4, out_specs=[spec] * 4,
        out_shape=[jax.ShapeDtypeStruct(w.shape, F32)] * 4,
        compiler_params=_params(dimension_semantics=("arbitrary",)),
    )(w, g, m, v)


def _small_step(packs, mix_g_blocks, w_pack, m_pack, v_pack, conv_wmv, rconv_wmv):
    rows, cols = packs.shape[1:]
    cshard = conv_wmv.shape[2]
    rshard = rconv_wmv.shape[2]
    mix_row = PK_MIX_G * TILE_ROWS

    def body(all_ref, blk_ref, w_ref, m_ref, v_ref, cw_ref, rw_ref, g_ref, d_ref, nm_ref, nv_ref, co_ref, ro_ref):
        total = all_ref[0]
        late = blk_ref[0]
        for k in range(1, N_DEVICES):
            total = total + all_ref[k]
            late = late + blk_ref[k]
        g_ref[...] = total
        g_ref[mix_row:mix_row + TILE_ROWS, :] = late
        g = g_ref[...]
        d_ref[...], nm_ref[...], nv_ref[...] = _adamw(w_ref[...], g, m_ref[...], v_ref[...])

        x, y, _ = _position()
        j = 2 * x + y
        cblk = total[PK_CONV_W * 8:PK_CONV_W * 8 + 8, :]
        rblk = total[PK_RCONV_W * 8:PK_RCONV_W * 8 + 8, :]
        cg = cblk[:, 0:cshard]
        rg = rblk[:, 0:rshard]
        for k in range(1, N_CHIPS):
            cg = jnp.where(j == k, cblk[:, k * cshard:(k + 1) * cshard], cg)
            rg = jnp.where(j == k, rblk[:, k * rshard:(k + 1) * rshard], rg)
        co_ref[0] = cg
        co_ref[1], co_ref[2], co_ref[3] = _adamw(cw_ref[0], cg, cw_ref[1], cw_ref[2])
        ro_ref[0] = rg
        ro_ref[1], ro_ref[2], ro_ref[3] = _adamw(rw_ref[0], rg, rw_ref[1], rw_ref[2])

    pack = [jax.ShapeDtypeStruct((rows, cols), F32)] * 4
    return pl.pallas_call(
        body, name="small_grads_step", in_specs=[VMEM] * 7, out_specs=[VMEM] * 6,
        out_shape=pack + [jax.ShapeDtypeStruct((4, TILE_ROWS, cshard), F32), jax.ShapeDtypeStruct((4, TILE_ROWS, rshard), F32)],
        compiler_params=_params(),
    )(packs, mix_g_blocks, w_pack, m_pack, v_pack, conv_wmv, rconv_wmv)


def _blk(a):
    a = a.reshape(-1, a.shape[-1])
    return jnp.pad(a, ((0, TILE_ROWS - a.shape[0]), (0, D_MODEL - a.shape[1])))


def _zero_blk():
    return jnp.zeros((TILE_ROWS, D_MODEL), F32)


def _pack_params(p, pre):
    get = lambda n: p[pre + n]
    return jnp.concatenate([
        _blk(get("g_norm_rnn")), _blk(get("rnn_conv_b")), _blk(get("b_a")), _blk(get("b_x")), _blk(get("lru_lambda")),
        _zero_blk(), _zero_blk(), _blk(get("g_norm_conv")), _blk(get("final_norm_g").reshape(1, -1)), _blk(get("norm_mlp_g")),
        _zero_blk(), _blk(get("norm_mix_g")), get("w_a").reshape(64, D_MODEL), get("w_x").reshape(64, D_MODEL)], axis=0)


def _to_block_diag(w):
    w4 = w.reshape(N_BD, 4, 64, 64)
    eye = jnp.eye(4, dtype=w.dtype)
    return (w4[:, :, :, None, :] * eye[None, :, None, :, None]).reshape(N_BD, BD, BD)


def _from_block_diag(d):
    d5 = d.reshape(N_BD, 4, 64, 4, 64)
    return jnp.stack([d5[:, q, :, q, :] for q in range(4)], axis=1).reshape(64, D_MODEL)


def _pad_rows(a):
    return jnp.pad(a, ((0, TILE_ROWS - a.shape[0]), (0, 0)))


_NAMES = ['norm_mix_g', 'w_in', 'conv_w', 'rnn_conv_w', 'rnn_conv_b', 'w_a', 'b_a', 'w_x', 'b_x', 'lru_lambda',
          'g_norm_conv', 'g_norm_rnn', 'w_out', 'norm_mlp_g', 'w_mlp_in', 'w_mlp_out', 'final_norm_g']


def kernel(x, norm_mix_g, w_in, conv_w, rnn_conv_w, rnn_conv_b, w_a, b_a, w_x, b_x, lru_lambda, g_norm_conv, g_norm_rnn, w_out, norm_mlp_g, w_mlp_in, w_mlp_out, final_norm_g, loss_target, m_norm_mix_g, m_w_in, m_conv_w, m_rnn_conv_w, m_rnn_conv_b, m_w_a, m_b_a, m_w_x, m_b_x, m_lru_lambda, m_g_norm_conv, m_g_norm_rnn, m_w_out, m_norm_mlp_g, m_w_mlp_in, m_w_mlp_out, m_final_norm_g, v_norm_mix_g, v_w_in, v_conv_w, v_rnn_conv_w, v_rnn_conv_b, v_w_a, v_b_a, v_w_x, v_b_x, v_lru_lambda, v_g_norm_conv, v_g_norm_rnn, v_w_out, v_norm_mlp_g, v_w_mlp_in, v_w_mlp_out, v_final_norm_g):
    args = dict(locals())
    p = {}
    for n in _NAMES:
        for pre in ("", "m_", "v_"):
            a = args[pre + n]
            p[pre + n] = a[0] if a.ndim >= 3 else a
    xs = x[0]
    target = loss_target[0]
    core_chip = jnp.stack([lax.axis_index("c"), 2 * lax.axis_index("x") + lax.axis_index("y")]).astype(jnp.int32)
    cshard = p["conv_w"].shape[1]
    rshard = p["rnn_conv_w"].shape[1]

    small = jnp.concatenate([_pad_rows(p["conv_w"]), _pad_rows(p["rnn_conv_w"])], axis=1)
    w_in_g, w_out_g, w1_g, w2_g, small_g = _gather_first(p["w_in"], p["w_out"], p["w_mlp_in"], p["w_mlp_out"], small)
    conv_full = small_g[:, :3, :cshard].transpose(1, 0, 2).reshape(3, CONV_W)
    rconv_full = small_g[:, :4, cshard:].transpose(1, 0, 2).reshape(4, LRU_W)
    wa_bd = _to_block_diag(p["w_a"]).astype(BF16)
    wx_bd = _to_block_diag(p["w_x"]).astype(BF16)
    gf = p["final_norm_g"].reshape(1, -1)
    lru = (wa_bd, p["b_a"], wx_bd, p["b_x"], p["lru_lambda"], p["g_norm_conv"], p["g_norm_rnn"])

    (u, h1b, xr, hs, c3, yb, *gates), (w_out_g, w1_g, w2_g) = _fwd_mix(
        xs, p["norm_mix_g"], w_in_g, conv_full, rconv_full, p["rnn_conv_b"], *lru, (w_out_g, w1_g, w2_g))
    zb, dpb, h2b, dx3b, dx2, dx2b, dy, st_mlp = _mlp_fwd_bwd(
        xs, yb, w_out_g.reshape(-1, D_MODEL), w1_g, w2_g.reshape(-1, D_MODEL), p["norm_mlp_g"], gf, target)

    part_out = _wgrad(yb, dx2b, "out", core_chip)
    part_1 = _wgrad(h2b, dpb, "mlp_in", core_chip)
    part_2 = _wgrad(zb, dx3b, "mlp_out", core_chip)
    (dub, st_mix, dwa_bd, dwx_bd), arrived_mlp = _mix_bwd(
        dy, u, xr, hs, c3, gates, conv_full, rconv_full, wa_bd, wx_bd, p["lru_lambda"], p["g_norm_conv"], p["g_norm_rnn"],
        (part_out[1], part_1[1], part_2[1]))
    pack = jnp.concatenate([st_mix, st_mlp, _zero_blk(), _from_block_diag(dwa_bd), _from_block_diag(dwx_bd)], axis=0)
    *part_in, packs = _wgrad(h1b, dub, "in", core_chip, pack)
    early = (("w_out", "out", part_out, arrived_mlp[0]), ("w_mlp_in", "mlp_in", part_1, arrived_mlp[1]),
             ("w_mlp_out", "mlp_out", part_2, arrived_mlp[2]))
    (grad_x, st_in), arrived_in, joined = _in_bwd(
        dub, w_in_g, xs, dx2, p["norm_mix_g"], (part_in[1],),
        [(tag, p[n].shape, part[0], arrived) for n, tag, part, arrived in early], core_chip)
    g_in, mix_g_blocks = _join("in", p["w_in"].shape, part_in[0], arrived_in[0], core_chip, st_in)
    big = {}
    for n, tag, g in [(n, tag, g) for (n, tag, _, _), g in zip(early, joined)] + [("w_in", "in", g_in)]:
        big[n] = _adamw_big(p[n], g, p["m_" + n], p["v_" + n], "adamw_" + tag)

    conv_wmv = jnp.stack([_pad_rows(p[pre + "conv_w"]) for pre in ("", "m_", "v_")])
    rconv_wmv = jnp.stack([_pad_rows(p[pre + "rnn_conv_w"]) for pre in ("", "m_", "v_")])
    g_pack, d_pack, m_pack, v_pack, conv_out, rconv_out = _small_step(
        packs, mix_g_blocks, _pack_params(p, ""), _pack_params(p, "m_"), _pack_params(p, "v_"), conv_wmv, rconv_wmv)

    def unpack(pk, kind):
        def vec(b, width=D_MODEL):
            return pk[b * 8:b * 8 + 1, :width]
        return {
            "norm_mix_g": vec(PK_MIX_G), "rnn_conv_b": vec(PK_RCONV_B), "b_a": vec(PK_B_A), "b_x": vec(PK_B_X),
            "lru_lambda": vec(PK_LAMBDA), "g_norm_conv": vec(PK_G_NORM_CONV, CONV_W), "g_norm_rnn": vec(PK_G_NORM_RNN),
            "norm_mlp_g": vec(PK_MLP_G), "final_norm_g": vec(PK_FINAL_G).reshape(-1),
            "w_a": pk[PK_W_A * 8:PK_W_A * 8 + 64].reshape(1, 16, 64, 64), "w_x": pk[PK_W_X * 8:PK_W_X * 8 + 64].reshape(1, 16, 64, 64),
            "conv_w": conv_out[kind, :3][None], "rnn_conv_w": rconv_out[kind, :4][None],
            "w_in": big["w_in"][kind][None], "w_out": big["w_out"][kind][None],
            "w_mlp_in": big["w_mlp_in"][kind][None], "w_mlp_out": big["w_mlp_out"][kind][None],
        }

    outs = [unpack(pk, kind) for kind, pk in enumerate((g_pack, d_pack, m_pack, v_pack))]
    for o in outs:
        for n in ("norm_mix_g", "rnn_conv_b", "b_a", "b_x", "lru_lambda", "g_norm_conv", "g_norm_rnn", "norm_mlp_g"):
            o[n] = o[n].reshape(1, -1)
    loss = g_pack[PK_LOSS * 8, 0]
    return (loss, grad_x[None], *[o[n] for o in outs for n in _NAMES])
```

```python
import functools
import math

import jax
import jax.numpy as jnp
from jax import lax
from jax.experimental import pallas as pl
from jax.experimental.pallas import tpu as pltpu

F32 = jnp.float32
BF16 = jnp.bfloat16
MESH = pl.DeviceIdType.MESH
ANY = pl.BlockSpec(memory_space=pl.ANY)
VMEM = pl.BlockSpec(memory_space=pltpu.VMEM)

EPS = 1e-6
LRU_C = 8.0
D_MODEL = 1024
CONV_W = 512
LRU_W = 1024
IN_COLS = 3 * CONV_W + 2 * LRU_W
IN_SHARD = IN_COLS // 4
N_CHIPS = 4
N_DEVICES = 8
BD = 256
N_BD = LRU_W // BD

ADAM_LR = 0.001
ADAM_B1 = 0.9
ADAM_B2 = 0.999
ADAM_EPS = 1e-08
ADAM_WD = 0.01
ADAM_STEP = 10
ADAM_BC1 = 1.0 - ADAM_B1 ** ADAM_STEP
ADAM_BC2 = 1.0 - ADAM_B2 ** ADAM_STEP

TILE_ROWS = 8
TOKEN_TILE = 256
VMEM_LIMIT = 56 * 1024 * 1024

PK_G_NORM_RNN, PK_RCONV_B, PK_B_A, PK_B_X, PK_LAMBDA, PK_RCONV_W, PK_CONV_W, PK_G_NORM_CONV = range(8)
PK_FINAL_G, PK_MLP_G, PK_LOSS, PK_MIX_G = 8, 9, 10, 11
PK_W_A = 12
PK_W_X = 20
PK_BLOCKS = 28
PK_ROWS = PK_BLOCKS * TILE_ROWS


def _params(**kw):
    return pltpu.CompilerParams(vmem_limit_bytes=VMEM_LIMIT, **kw)


def _position():
    x, y, c = lax.axis_index("x"), lax.axis_index("y"), lax.axis_index("c")
    return x, y, c


def _sigmoid(v):
    return 1.0 / (1.0 + jnp.exp(-v))


def _one_minus_square(log_a, a):
    v = 2.0 * log_a
    series = -v * (1.0 + v * (0.5 + v * (1.0 / 6.0)))
    return jnp.where(v > -0.01, series, 1.0 - a * a)


_GELU_C = math.sqrt(2.0 / math.pi)
_GELU_K = 0.044715


def _gelu_and_grad(g):
    th = jnp.tanh(_GELU_C * (g + _GELU_K * g * g * g))
    gelu = 0.5 * g * (1.0 + th)
    dgelu = 0.5 * (1.0 + th) + 0.5 * g * (1.0 - th * th) * (_GELU_C * (1.0 + 3.0 * _GELU_K * g * g))
    return gelu, dgelu


def _rows(shape):
    return lax.broadcasted_iota(jnp.int32, shape, 0)


def _shift_down(v, k, prev8):
    rolled = pltpu.roll(v, k, 0)
    halo = pltpu.roll(prev8, k, 0)
    head = jnp.where(_rows(halo.shape) < k, halo, rolled[:TILE_ROWS])
    return jnp.concatenate([head, rolled[TILE_ROWS:]], axis=0)


def _shift_up(v, k, next8):
    n = v.shape[0]
    rolled = pltpu.roll(v, n - k, 0)
    halo = pltpu.roll(next8, TILE_ROWS - k, 0)
    tail = jnp.where(_rows(halo.shape) >= TILE_ROWS - k, halo, rolled[n - TILE_ROWS:])
    return jnp.concatenate([rolled[: n - TILE_ROWS], tail], axis=0)


def _scan_down(a, b):
    n, w = a.shape
    row = _rows(a.shape)
    s = 1
    while s < n:
        if s < TILE_ROWS:
            keep = row >= s
            b = jnp.where(keep, a * pltpu.roll(b, s, 0) + b, b)
            a = jnp.where(keep, a * pltpu.roll(a, s, 0), a)
        else:
            b = a * jnp.concatenate([jnp.zeros((s, w), F32), b[:n - s]], axis=0) + b
            a = a * jnp.concatenate([jnp.ones((s, w), F32), a[:n - s]], axis=0)
        s *= 2
    return a, b


def _scan_up(a, b):
    n, w = a.shape
    row = _rows(a.shape)
    s = 1
    while s < n:
        if s < TILE_ROWS:
            keep = row < n - s
            b = jnp.where(keep, a * pltpu.roll(b, n - s, 0) + b, b)
            a = jnp.where(keep, a * pltpu.roll(a, n - s, 0), a)
        else:
            b = a * jnp.concatenate([b[s:], jnp.zeros((s, w), F32)], axis=0) + b
            a = a * jnp.concatenate([a[s:], jnp.ones((s, w), F32)], axis=0)
        s *= 2
    return a, b


def _softplus_neg(lam):
    e = jnp.exp(-jnp.abs(lam))
    log1p_e = jnp.where(e < 1e-2, e * (1.0 - e * (0.5 - e * (1.0 / 3.0 - e * 0.25))), jnp.log(1.0 + e))
    sp = jnp.maximum(-lam, 0.0) + log1p_e
    dsp = -_sigmoid(-lam)
    return sp, dsp


def _block_diag_dot(vb, w_ref):
    return jnp.concatenate(
        [jnp.dot(vb[:, j * BD:(j + 1) * BD], w_ref[j], preferred_element_type=F32) for j in range(N_BD)], axis=1)


def _block_diag_dot_t(vb, w_ref):
    return jnp.concatenate(
        [lax.dot_general(vb[:, j * BD:(j + 1) * BD], w_ref[j], (((1,), (1,)), ((), ())), preferred_element_type=F32)
         for j in range(N_BD)], axis=1)


def _dot_nt(a, b):
    return lax.dot_general(a, b, (((1,), (1,)), ((), ())), preferred_element_type=F32)


def _dot_tn(a, b):
    return lax.dot_general(a, b, (((0,), (0,)), ((), ())), preferred_element_type=F32)


def _lru_gates(xr, wa_ref, ba, wx_ref, bx, sp):
    xrb = xr.astype(BF16)
    r = _sigmoid(_block_diag_dot(xrb, wa_ref) + ba)
    ig = _sigmoid(_block_diag_dot(xrb, wx_ref) + bx)
    log_a = (-LRU_C) * r * sp
    a = jnp.exp(log_a)
    mult = jnp.sqrt(_one_minus_square(log_a, a))
    return r, ig, a, mult


def _colsum(v):
    return jnp.sum(v, axis=0, keepdims=True)


N_FWD_OUT = 10


def _fwd_mix(x, g1, w_in_g, conv_w, rconv_w, rconv_b, wa_bd, b_a, wx_bd, b_x, lam, g_nc, g_nr, later):
    t, d = x.shape
    tm = TOKEN_TILE
    nt = t // tm
    nl = len(later)
    assert nl == 3
    pass_on_at = [nt * f // 16 for f in (1, 3, 7)]
    neighbours_at = [nt * f // 16 for f in (2, 5, 8)]
    diagonal_at = [nt * f // 16 for f in (9, 11, 13)]

    def body(x_ref, g1_ref, win_ref, cw_ref, rw_ref, rb_ref, wa_ref, ba_ref, wx_ref, bx_ref, lam_ref, gnc_ref, gnr_ref,
             *rest):
        later_in, outs, rest = rest[:nl], rest[nl:nl + N_FWD_OUT], rest[nl + N_FWD_OUT:]
        u_ref, h1_ref, xr_ref, hs_ref, c3_ref, y_ref, r_ref, ig_ref, a_ref, mult_ref = outs
        later_out, (cv_prev, xin_prev, h_prev, send_sems, recv_sems) = rest[:nl], rest[nl:]
        del later_in
        step = pl.program_id(0)
        plan = _ShardGather(later_out, send_sems, recv_sems)

        @pl.when(step == 0)
        def _():
            cv_prev[...] = jnp.zeros_like(cv_prev)
            xin_prev[...] = jnp.zeros_like(xin_prev)
            h_prev[...] = jnp.zeros_like(h_prev)
            for w in range(nl):
                plan.start_direct(w)

        for w in range(nl):
            @pl.when(step == pass_on_at[w])
            def _(w=w):
                plan.start_pass_on(w)

            @pl.when(step == neighbours_at[w])
            def _(w=w):
                plan.start_hand_over(w, diagonal=False)

            @pl.when(step == diagonal_at[w])
            def _(w=w):
                plan.start_hand_over(w, diagonal=True)

        xv = x_ref[...]
        rstd = lax.rsqrt(jnp.mean(xv * xv, axis=-1, keepdims=True) + EPS)
        h1b = ((xv * rstd) * g1_ref[...]).astype(BF16)
        h1_ref[...] = h1b
        for j in range(N_CHIPS):
            u_ref[:, j * IN_SHARD:(j + 1) * IN_SHARD] = jnp.dot(h1b, win_ref[j], preferred_element_type=F32)
        gate_b = u_ref[:, 0:CONV_W]
        cv = u_ref[:, CONV_W:2 * CONV_W] * u_ref[:, 2 * CONV_W:3 * CONV_W]
        x_r = u_ref[:, 3 * CONV_W:3 * CONV_W + LRU_W]
        g = u_ref[:, 3 * CONV_W + LRU_W:]

        cw = cw_ref[...]
        cvp = cv_prev[...]
        conv3 = cw[0:1] * _shift_down(cv, 2, cvp) + cw[1:2] * _shift_down(cv, 1, cvp) + cw[2:3] * cv
        cv_prev[...] = cv[tm - TILE_ROWS:]
        c3_ref[...] = conv3
        y_conv = gate_b * conv3

        rw = rw_ref[...]
        xp = xin_prev[...]
        xr = (rw[0:1] * _shift_down(x_r, 3, xp) + rw[1:2] * _shift_down(x_r, 2, xp)
              + rw[2:3] * _shift_down(x_r, 1, xp) + rw[3:4] * x_r) + rb_ref[...]
        xin_prev[...] = x_r[tm - TILE_ROWS:]
        xr_ref[...] = xr
        sp, _ = _softplus_neg(lam_ref[...])
        r, ig, a, mult = _lru_gates(xr, wa_ref, ba_ref[...], wx_ref, bx_ref[...], sp)
        r_ref[...] = r
        ig_ref[...] = ig
        a_ref[...] = a
        mult_ref[...] = mult
        a_cum, h = _scan_down(a, mult * (ig * xr))
        h = h + a_cum * h_prev[...]
        h_prev[...] = h[tm - 1:tm]
        hs_ref[...] = h
        gelu, _ = _gelu_and_grad(g)
        y_rnn = h * gelu

        na = y_conv * lax.rsqrt(jnp.mean(y_conv * y_conv, axis=-1, keepdims=True) + EPS) * gnc_ref[...]
        nb = y_rnn * lax.rsqrt(jnp.mean(y_rnn * y_rnn, axis=-1, keepdims=True) + EPS) * gnr_ref[...]
        y_ref[:, :CONV_W] = na.astype(BF16)
        y_ref[:, CONV_W:] = nb.astype(BF16)

        @pl.when(step == nt - 1)
        def _():
            for w in range(nl):
                plan.finish(w)

    def full(a):
        nd = a.ndim
        return pl.BlockSpec(a.shape, lambda i: (0,) * nd)

    def tok(cols):
        return pl.BlockSpec((tm, cols), lambda i: (i, 0))

    def act(cols, dtype=F32):
        return jax.ShapeDtypeStruct((t, cols), dtype)

    smalls = (g1, w_in_g, conv_w, rconv_w, rconv_b, wa_bd, b_a, wx_bd, b_x, lam, g_nc, g_nr)
    n_in = 1 + len(smalls)
    outs = pl.pallas_call(
        body, name="fwd_mix", grid=(nt,),
        in_specs=[tok(d)] + [full(a) for a in smalls] + [ANY] * nl,
        out_specs=[tok(IN_COLS), tok(d), tok(LRU_W), tok(LRU_W), tok(CONV_W), tok(CONV_W + LRU_W)]
        + [tok(LRU_W)] * 4 + [ANY] * nl,
        out_shape=[act(IN_COLS), act(d, BF16), act(LRU_W), act(LRU_W), act(CONV_W), act(CONV_W + LRU_W, BF16)]
        + [act(LRU_W)] * 4 + [jax.ShapeDtypeStruct(a.shape, a.dtype) for a in later],
        input_output_aliases={n_in + w: N_FWD_OUT + w for w in range(nl)},
        scratch_shapes=[pltpu.VMEM((TILE_ROWS, CONV_W), F32), pltpu.VMEM((TILE_ROWS, LRU_W), F32),
                        pltpu.VMEM((1, LRU_W), F32), pltpu.SemaphoreType.DMA((nl, _ShardGather.PAIRS)),
                        pltpu.SemaphoreType.DMA((nl, _ShardGather.PAIRS))],
        compiler_params=_params(dimension_semantics=("arbitrary",)),
    )(x, *smalls, *later)
    return outs[:N_FWD_OUT], outs[N_FWD_OUT:]


def _mlp_fwd_bwd(x, yb, w_out_g, w1_g, w2_g, g2, gf, target):
    t, d = x.shape
    tm = TOKEN_TILE
    ff = w2_g.shape[0]
    mix = w_out_g.shape[0]
    ffs = ff // N_CHIPS

    def body(x_ref, y_ref, g2_ref, gf_ref, tgt_ref, wout_hbm, w1_hbm, w2_hbm,
             z_ref, dp_ref, h2_ref, dx3b_ref, dx2_ref, dx2b_ref, dy_ref, st_ref, wout, w1, w2, p_ref):
        @pl.when(pl.program_id(0) == 0)
        def _():
            pltpu.sync_copy(wout_hbm, wout)
            pltpu.sync_copy(w1_hbm, w1)
            pltpu.sync_copy(w2_hbm, w2)
            st_ref[...] = jnp.zeros_like(st_ref)

        x2 = x_ref[...] + jnp.dot(y_ref[...], wout[...], preferred_element_type=F32)
        r2 = lax.rsqrt(jnp.mean(x2 * x2, axis=-1, keepdims=True) + EPS)
        xh2 = x2 * r2
        g2v = g2_ref[...]
        h2b = (xh2 * g2v).astype(BF16)
        h2_ref[...] = h2b
        for j in range(N_CHIPS):
            p_ref[:, j * ffs:(j + 1) * ffs] = jnp.dot(h2b, w1[j], preferred_element_type=F32)
        rp = jnp.maximum(p_ref[...], 0.0)
        zb = (rp * rp).astype(BF16)
        z_ref[...] = zb
        x3 = x2 + jnp.dot(zb, w2[...], preferred_element_type=F32)
        r3 = lax.rsqrt(jnp.mean(x3 * x3, axis=-1, keepdims=True) + EPS)
        xh3 = x3 * r3
        gfv = gf_ref[...]
        err = xh3 * gfv - tgt_ref[...]
        loss = (0.5 / d) * jnp.sum(err * err)
        dout = err * (1.0 / d)
        st_ref[PK_FINAL_G * 8 - 64:PK_FINAL_G * 8 - 63, :] += _colsum(dout * xh3)
        st_ref[PK_LOSS * 8 - 64:PK_LOSS * 8 - 63, :] += jnp.zeros((1, d), F32) + loss
        dxh3 = dout * gfv
        dx3 = r3 * (dxh3 - xh3 * jnp.mean(dxh3 * xh3, axis=-1, keepdims=True))
        dx3b = dx3.astype(BF16)
        dx3b_ref[...] = dx3b
        dpb = (_dot_nt(dx3b, w2[...]) * (2.0 * rp)).astype(BF16)
        dp_ref[...] = dpb
        dh2 = _dot_nt(dpb[:, 0:ffs], w1[0])
        for j in range(1, N_CHIPS):
            dh2 = dh2 + _dot_nt(dpb[:, j * ffs:(j + 1) * ffs], w1[j])
        st_ref[PK_MLP_G * 8 - 64:PK_MLP_G * 8 - 63, :] += _colsum(dh2 * xh2)
        dxh2 = dh2 * g2v
        dx2 = dx3 + r2 * (dxh2 - xh2 * jnp.mean(dxh2 * xh2, axis=-1, keepdims=True))
        dx2_ref[...] = dx2
        dx2b = dx2.astype(BF16)
        dx2b_ref[...] = dx2b
        dy_ref[...] = _dot_nt(dx2b, wout[...])

    def tok(cols):
        return pl.BlockSpec((tm, cols), lambda i: (i, 0))

    def row(cols):
        return pl.BlockSpec((1, cols), lambda i: (0, 0))

    return pl.pallas_call(
        body, name="mlp_fwd_bwd", grid=(t // tm,),
        in_specs=[tok(d), tok(mix), row(d), row(d), tok(d), ANY, ANY, ANY],
        out_specs=[tok(ff), tok(ff), tok(d), tok(d), tok(d), tok(d), tok(mix),
                   pl.BlockSpec((3 * TILE_ROWS, d), lambda i: (0, 0))],
        out_shape=[jax.ShapeDtypeStruct((t, ff), BF16), jax.ShapeDtypeStruct((t, ff), BF16),
                   jax.ShapeDtypeStruct((t, d), BF16), jax.ShapeDtypeStruct((t, d), BF16),
                   jax.ShapeDtypeStruct((t, d), F32), jax.ShapeDtypeStruct((t, d), BF16),
                   jax.ShapeDtypeStruct((t, mix), F32), jax.ShapeDtypeStruct((3 * TILE_ROWS, d), F32)],
        scratch_shapes=[pltpu.VMEM(w_out_g.shape, BF16), pltpu.VMEM(w1_g.shape, BF16), pltpu.VMEM(w2_g.shape, BF16),
                        pltpu.VMEM((tm, ff), F32)],
        compiler_params=_params(dimension_semantics=("arbitrary",)),
    )(x, yb, g2, gf, target, w_out_g, w1_g, w2_g)


def _mix_bwd(dy, u, xr_all, hs_all, c3_all, gates, conv_w, rconv_w, wa_bd, wx_bd, lam, g_nc, g_nr, parts):
    t = dy.shape[0]
    tm = TOKEN_TILE
    nt = t // tm
    hb = tm // TILE_ROWS
    npart = len(parts)

    def body(dy_ref, u_ref, uh_ref, xr_ref, hs_ref, hh_ref, c3_ref, r_ref, ig_ref, a_ref, mult_ref,
             cw_ref, rw_ref, wa_ref, wx_ref, lam_ref, gnc_ref, gnr_ref, *rest):
        part_refs, (du_ref, st_ref, dwa_ref, dwx_ref), rest = rest[:npart], rest[npart:npart + 4], rest[npart + 4:]
        arrived_refs, (dc_next, a_next, gs_next, dxr_next, send_sems, recv_sems) = rest[:npart], rest[npart:]
        exchange = _PartialExchange(part_refs, arrived_refs, send_sems, recv_sems)
        i = pl.program_id(0)

        @pl.when(i == 0)
        def _():
            exchange.start()
            dc_next[...] = jnp.zeros_like(dc_next)
            a_next[...] = jnp.zeros_like(a_next)
            gs_next[...] = jnp.zeros_like(gs_next)
            dxr_next[...] = jnp.zeros_like(dxr_next)
            st_ref[...] = jnp.zeros_like(st_ref)
            dwa_ref[...] = jnp.zeros_like(dwa_ref)
            dwx_ref[...] = jnp.zeros_like(dwx_ref)

        first_tile = i == nt - 1
        gate_b = u_ref[:, 0:CONV_W]
        gate_c = u_ref[:, CONV_W:2 * CONV_W]
        v = u_ref[:, 2 * CONV_W:3 * CONV_W]
        x_r = u_ref[:, 3 * CONV_W:3 * CONV_W + LRU_W]
        g = u_ref[:, 3 * CONV_W + LRU_W:]
        cv = gate_c * v
        cv_prev = jnp.where(first_tile, 0.0, uh_ref[:, CONV_W:2 * CONV_W] * uh_ref[:, 2 * CONV_W:3 * CONV_W])
        xin_prev = jnp.where(first_tile, 0.0, uh_ref[:, 3 * CONV_W:3 * CONV_W + LRU_W])
        hs_prev = jnp.where(first_tile, 0.0, hh_ref[...])

        def acc(block, val, width=LRU_W, row=0):
            r0 = block * TILE_ROWS + row
            st_ref[r0:r0 + 1, 0:width] += val

        conv3 = c3_ref[...]
        y_conv = gate_b * conv3
        ra = lax.rsqrt(jnp.mean(y_conv * y_conv, axis=-1, keepdims=True) + EPS)
        xha = y_conv * ra
        dna = dy_ref[:, :CONV_W]
        acc(PK_G_NORM_CONV, _colsum(dna * xha), CONV_W)
        dxha = dna * gnc_ref[...]
        dy_conv = ra * (dxha - xha * jnp.mean(dxha * xha, axis=-1, keepdims=True))
        du_ref[:, 0:CONV_W] = (dy_conv * conv3).astype(BF16)
        dc = dy_conv * gate_b
        cw = cw_ref[...]
        dcn = dc_next[...]
        dcv = cw[2:3] * dc + cw[1:2] * _shift_up(dc, 1, dcn) + cw[0:1] * _shift_up(dc, 2, dcn)
        dc_next[...] = dc[:TILE_ROWS]
        acc(PK_CONV_W, _colsum(dc * _shift_down(cv, 2, cv_prev)), CONV_W, 0)
        acc(PK_CONV_W, _colsum(dc * _shift_down(cv, 1, cv_prev)), CONV_W, 1)
        acc(PK_CONV_W, _colsum(dc * cv), CONV_W, 2)
        du_ref[:, CONV_W:2 * CONV_W] = (dcv * v).astype(BF16)
        du_ref[:, 2 * CONV_W:3 * CONV_W] = (dcv * gate_c).astype(BF16)

        hs = hs_ref[...]
        gelu, dgelu = _gelu_and_grad(g)
        y_rnn = hs * gelu
        rb = lax.rsqrt(jnp.mean(y_rnn * y_rnn, axis=-1, keepdims=True) + EPS)
        xhb = y_rnn * rb
        dnb = dy_ref[:, CONV_W:]
        acc(PK_G_NORM_RNN, _colsum(dnb * xhb))
        dxhb = dnb * gnr_ref[...]
        dy_rnn = rb * (dxhb - xhb * jnp.mean(dxhb * xhb, axis=-1, keepdims=True))
        du_ref[:, 3 * CONV_W + LRU_W:] = (dy_rnn * hs * dgelu).astype(BF16)
        dh = dy_rnn * gelu

        xr = xr_ref[...]
        xrb = xr.astype(BF16)
        sp, dsp = _softplus_neg(lam_ref[...])
        r, ig, a, mult = r_ref[...], ig_ref[...], a_ref[...], mult_ref[...]
        a_up = _shift_up(a, 1, a_next[...])
        a_next[...] = a[:TILE_ROWS]
        a_cum, gs = _scan_up(a_up, dh)
        gs = gs + a_cum * gs_next[0:1, :]
        gs_next[...] = gs[:TILE_ROWS]
        da = gs * _shift_down(hs, 1, hs_prev)
        gx = gs * xr
        di = gx * mult
        dmult = gx * ig
        dxr = gs * (mult * ig)
        dlog_a = da * a - dmult * ((a * a) / mult)
        acc(PK_LAMBDA, _colsum(dlog_a * r) * ((-LRU_C) * dsp))
        dpa = (dlog_a * ((-LRU_C) * sp)) * (r * (1.0 - r))
        dpx = di * (ig * (1.0 - ig))
        acc(PK_B_A, _colsum(dpa))
        acc(PK_B_X, _colsum(dpx))
        dpab = dpa.astype(BF16)
        dpxb = dpx.astype(BF16)
        dxr = dxr + _block_diag_dot_t(dpab, wa_ref) + _block_diag_dot_t(dpxb, wx_ref)
        for j in range(N_BD):
            cols = slice(j * BD, (j + 1) * BD)
            dwa_ref[j] += _dot_tn(xrb[:, cols], dpab[:, cols])
            dwx_ref[j] += _dot_tn(xrb[:, cols], dpxb[:, cols])

        acc(PK_RCONV_B, _colsum(dxr))
        rw = rw_ref[...]
        dxn = dxr_next[...]
        dx_r = (rw[3:4] * dxr + rw[2:3] * _shift_up(dxr, 1, dxn) + rw[1:2] * _shift_up(dxr, 2, dxn)
                + rw[0:1] * _shift_up(dxr, 3, dxn))
        dxr_next[...] = dxr[:TILE_ROWS]
        for k in range(3):
            acc(PK_RCONV_W, _colsum(dxr * _shift_down(x_r, 3 - k, xin_prev)), LRU_W, k)
        acc(PK_RCONV_W, _colsum(dxr * x_r), LRU_W, 3)
        du_ref[:, 3 * CONV_W:3 * CONV_W + LRU_W] = dx_r.astype(BF16)

        @pl.when(i == nt - 1)
        def _():
            exchange.wait()

    def full(a):
        nd = a.ndim
        return pl.BlockSpec(a.shape, lambda i: (0,) * nd)

    def tok(cols):
        return pl.BlockSpec((tm, cols), lambda i: (nt - 1 - i, 0))

    def halo(cols):
        return pl.BlockSpec((TILE_ROWS, cols), lambda i: (jnp.maximum((nt - 1 - i) * hb - 1, 0), 0))

    smalls = (conv_w, rconv_w, wa_bd, wx_bd, lam, g_nc, g_nr)
    outs = pl.pallas_call(
        body, name="mix_bwd", grid=(nt,),
        in_specs=[tok(CONV_W + LRU_W), tok(IN_COLS), halo(IN_COLS), tok(LRU_W), tok(LRU_W), halo(LRU_W), tok(CONV_W)]
        + [tok(LRU_W)] * 4 + [full(a) for a in smalls] + [ANY] * npart,
        out_specs=[tok(IN_COLS), pl.BlockSpec((8 * TILE_ROWS, LRU_W), lambda i: (0, 0)),
                   pl.BlockSpec((N_BD, BD, BD), lambda i: (0, 0, 0)), pl.BlockSpec((N_BD, BD, BD), lambda i: (0, 0, 0))]
        + [ANY] * npart,
        out_shape=[jax.ShapeDtypeStruct((t, IN_COLS), BF16), jax.ShapeDtypeStruct((8 * TILE_ROWS, LRU_W), F32),
                   jax.ShapeDtypeStruct((N_BD, BD, BD), F32), jax.ShapeDtypeStruct((N_BD, BD, BD), F32)]
        + [jax.ShapeDtypeStruct(a.shape, a.dtype) for a in parts],
        scratch_shapes=[pltpu.VMEM((TILE_ROWS, CONV_W), F32), pltpu.VMEM((TILE_ROWS, LRU_W), F32),
                        pltpu.VMEM((TILE_ROWS, LRU_W), F32), pltpu.VMEM((TILE_ROWS, LRU_W), F32),
                        pltpu.SemaphoreType.DMA((npart, 3)), pltpu.SemaphoreType.DMA((npart, 3))],
        compiler_params=_params(dimension_semantics=("arbitrary",)),
    )(dy, u, u, xr_all, hs_all, hs_all, c3_all, *gates, *smalls, *parts)
    return outs[:4], outs[4:]


def _in_bwd(dub, w_in_g, x, dx2, g1, parts, joins, core_chip):
    t, d = x.shape
    tm = TOKEN_TILE
    nt = t // tm
    npart = len(parts)
    nj = len(joins)
    geometry = []
    for tag, shape, _, _ in joins:
        pr, pc = WGRAD_GEOMETRY[tag][:2]
        every = 1 if pr % (nt * 16) == 0 else 2
        geometry.append((pr, pc, pr * every // nt, every, shape[1] == pc))

    def body(cc_ref, du_ref, win_ref, x_ref, dx2_ref, g1_ref, *rest):
        sums, rest = [rest[4 * w:4 * w + 4] for w in range(nj)], rest[4 * nj:]
        part_refs, (gx_ref, st_ref), rest = rest[:npart], rest[npart:npart + 2], rest[npart + 2:]
        arrived_refs, joined, rest = rest[:npart], rest[npart:npart + nj], rest[npart + nj:]
        stages, (send_sems, recv_sems, j_local, j_send, j_recv) = rest[:nj], rest[nj:]
        exchange = _PartialExchange(part_refs, arrived_refs, send_sems, recv_sems)
        i = pl.program_id(0)
        c = cc_ref[0]

        def window(w, core, row0, rows):
            pr, pc, _, _, by_rows = geometry[w]
            if by_rows:
                return joined[w].at[pl.ds(core * pr + row0, rows), :]
            return joined[w].at[pl.ds(row0, rows), pl.ds(core * pc, pc)]

        def to_sibling(w, src, core, row0, rows):
            return pltpu.make_async_remote_copy(src_ref=src, dst_ref=window(w, core, row0, rows), send_sem=j_send.at[w],
                                                recv_sem=j_recv.at[w], device_id=_sibling(), device_id_type=MESH)

        @pl.when(i == 0)
        def _():
            exchange.start()
            st_ref[...] = jnp.zeros_like(st_ref)

        for w in range(nj):
            pr, pc, rb, every, _ = geometry[w]

            @pl.when(i % every == 0)
            def _(w=w, rb=rb, every=every):
                p_ref, r1_ref, r2_ref, r3_ref = sums[w]
                row0 = pl.multiple_of((i // every) * rb, rb)
                rows = stages[w].at[pl.ds(row0, rb), :]
                rows[...] = ((p_ref[0] + r1_ref[0].astype(F32)) + r2_ref[0].astype(F32)) + r3_ref[0].astype(F32)
                pltpu.make_async_copy(rows, window(w, c, row0, rb), j_local.at[w]).start()
                to_sibling(w, rows, c, row0, rb).start()

        dh1 = _dot_nt(du_ref[:, 0:IN_SHARD], win_ref[0])
        for j in range(1, N_CHIPS):
            dh1 = dh1 + _dot_nt(du_ref[:, j * IN_SHARD:(j + 1) * IN_SHARD], win_ref[j])
        xv = x_ref[...]
        rstd = lax.rsqrt(jnp.mean(xv * xv, axis=-1, keepdims=True) + EPS)
        xh = xv * rstd
        st_ref[0:1, :] += _colsum(dh1 * xh)
        dxh = dh1 * g1_ref[...]
        gx_ref[...] = dx2_ref[...] + rstd * (dxh - xh * jnp.mean(dxh * xh, axis=-1, keepdims=True))

        @pl.when(i == nt - 1)
        def _():
            exchange.wait()
            for w in range(nj):
                pr = geometry[w][0]
                pltpu.make_async_copy(stages[w], window(w, c, 0, pr), j_local.at[w]).wait()
                to_sibling(w, stages[w], 1 - c, 0, pr).wait()

    def tok(cols):
        return pl.BlockSpec((tm, cols), lambda i, cc: (i, 0))

    def partial(w, off):
        pr, pc, rb, every, _ = geometry[w]
        return pl.BlockSpec((1, rb, pc), lambda i, cc: ((cc[1] + off) % N_CHIPS, i // every, 0))

    sum_specs, sum_operands = [], []
    for w, (_, _, own, arrived) in enumerate(joins):
        sum_specs += [partial(w, off) for off in range(N_CHIPS)]
        sum_operands += [own, arrived, arrived, arrived]
    dma = pltpu.SemaphoreType.DMA
    outs = pl.pallas_call(
        body, name="in_bwd",
        grid_spec=pltpu.PrefetchScalarGridSpec(
            num_scalar_prefetch=1, grid=(nt,),
            in_specs=[tok(IN_COLS), pl.BlockSpec(w_in_g.shape, lambda i, cc: (0, 0, 0)), tok(d), tok(d),
                      pl.BlockSpec((1, d), lambda i, cc: (0, 0))] + sum_specs + [ANY] * npart,
            out_specs=[tok(d), pl.BlockSpec((TILE_ROWS, d), lambda i, cc: (0, 0))] + [ANY] * (npart + nj),
            scratch_shapes=[pltpu.VMEM((g[0], g[1]), F32) for g in geometry]
            + [dma((npart, 3)), dma((npart, 3)), dma((nj,)), dma((nj,)), dma((nj,))]),
        out_shape=[jax.ShapeDtypeStruct((t, d), F32), jax.ShapeDtypeStruct((TILE_ROWS, d), F32)]
        + [jax.ShapeDtypeStruct(a.shape, a.dtype) for a in parts]
        + [jax.ShapeDtypeStruct(shape, F32) for _, shape, _, _ in joins],
        compiler_params=_params(dimension_semantics=("arbitrary",)),
    )(core_chip, dub, w_in_g, x, dx2, g1, *sum_operands, *parts)
    return outs[:2], outs[2:2 + npart], outs[2 + npart:]


WGRAD_GEOMETRY = {
    "in": (512, IN_SHARD, lambda s, h: h, lambda s, h: s),
    "mlp_in": (512, D_MODEL, lambda s, h: h, lambda s, h: s),
    "mlp_out": (512, D_MODEL, lambda s, h: 2 * s + h, lambda s, h: 0),
    "out": (384, 512, lambda s, h: s, lambda s, h: h),
}
K_CHUNK = 512


def _sibling():
    x, y, c = _position()
    return (x, y, 1 - c)


def _wgrad(a, b, tag, core_chip, pack=None):
    t = a.shape[0]
    pr, pc, a_blk, b_blk = WGRAD_GEOMETRY[tag]
    nk = t // K_CHUNK
    mine = N_CHIPS
    riding = pack is not None

    def body(cc_ref, a_ref, b_ref, *rest):
        if riding:
            pack_ref, land_ref, p_ref, pb_ref, all_ref, stage, rbuf, send_sems, recv_sems, rsem, g_send, g_recv, g_local = rest
            gather = _PackGather(pack_ref, all_ref, g_send, g_recv, g_local)
        else:
            land_ref, p_ref, pb_ref, stage, rbuf, send_sems, recv_sems, rsem = rest
        ph, s = pl.program_id(0), pl.program_id(1)
        if riding:
            @pl.when((ph == 0) & (s == 0))
            def _():
                gather.start()

            @pl.when((ph == 1) & (s == N_CHIPS - 2))
            def _():
                gather.hand_over()
        slot = jnp.where(ph == 0, s, mine)
        acc = stage.at[slot]
        acc[...] = _dot_tn(a_ref[0:K_CHUNK, :], b_ref[0:K_CHUNK, :])
        for k in range(1, nk):
            acc[...] += _dot_tn(a_ref[k * K_CHUNK:(k + 1) * K_CHUNK, :], b_ref[k * K_CHUNK:(k + 1) * K_CHUNK, :])

        def push(k):
            return pltpu.make_async_remote_copy(src_ref=stage.at[k], dst_ref=land_ref.at[k], send_sem=send_sems.at[k],
                                                recv_sem=recv_sems.at[k], device_id=_sibling(), device_id_type=MESH)

        @pl.when(ph == 0)
        def _():
            push(s).start()

        @pl.when(ph == 1)
        def _():
            push(s).wait_recv()
            landed = pltpu.make_async_copy(land_ref.at[s], rbuf, rsem)
            landed.start()
            landed.wait()
            p = stage[mine] + rbuf[...]
            p_ref[0] = p
            pb_ref[0] = p.astype(BF16)

        @pl.when((ph == 1) & (s == N_CHIPS - 1))
        def _():
            for k in range(N_CHIPS):
                push(k).wait_send()
            if riding:
                gather.finish()

    def half(ph, cc):
        return jnp.where(ph == 0, 1 - cc[0], cc[0])

    def out_slot(ph, s, cc):
        return (jnp.where(ph == 0, 0, s), 0, 0)

    piece = jax.ShapeDtypeStruct((N_CHIPS, pr, pc), F32)
    in_specs = [pl.BlockSpec((t, pr), lambda ph, s, cc: (0, a_blk(s, half(ph, cc)))),
                pl.BlockSpec((t, pc), lambda ph, s, cc: (0, b_blk(s, half(ph, cc))))]
    out_specs = [ANY, pl.BlockSpec((1, pr, pc), out_slot), pl.BlockSpec((1, pr, pc), out_slot)]
    out_shape = [piece, piece, jax.ShapeDtypeStruct((N_CHIPS, pr, pc), BF16)]
    scratch = [pltpu.VMEM((N_CHIPS + 1, pr, pc), F32), pltpu.VMEM((pr, pc), F32),
               pltpu.SemaphoreType.DMA((N_CHIPS,)), pltpu.SemaphoreType.DMA((N_CHIPS,)), pltpu.SemaphoreType.DMA]
    operands = [a, b]
    if riding:
        in_specs.append(pl.BlockSpec(pack.shape, lambda ph, s, cc: (0, 0)))
        out_specs.append(ANY)
        out_shape.append(jax.ShapeDtypeStruct((N_DEVICES,) + pack.shape, pack.dtype))
        scratch += _PackGather.semaphores()
        operands.append(pack)
    return pl.pallas_call(
        body, name="wgrad_" + tag,
        grid_spec=pltpu.PrefetchScalarGridSpec(
            num_scalar_prefetch=1, grid=(2, N_CHIPS), in_specs=in_specs, out_specs=out_specs, scratch_shapes=scratch),
        out_shape=out_shape,
        compiler_params=_params(dimension_semantics=("arbitrary", "arbitrary")),
    )(core_chip, *operands)[1:]


def _other_chips(x, y):
    return [(1 - x, y), (x, 1 - y), (1 - x, 1 - y)]


class _ShardGather:
    PAIRS = 9

    def __init__(self, outs, send_sems, recv_sems):
        self.outs, self.send_sems, self.recv_sems = outs, send_sems, recv_sems
        x, y, c = _position()
        self.c, self.j = c, 2 * x + y
        self.sibling = (x, y, 1 - c)
        self.chips = _other_chips(x, y)

    def _chip(self, k):
        px, py = self.chips[k]
        return 2 * px + py

    def _half(self, w, chip, which):
        hr = self.outs[w].shape[1] // 2
        return self.outs[w].at[chip, pl.ds(which * hr, hr), :]

    def _quarter(self, w, chip, q):
        qr = self.outs[w].shape[1] // 4
        return self.outs[w].at[chip, pl.ds(self.c * 2 * qr + q * qr, qr), :]

    def _copy(self, ref, w, pair, to, src=None):
        return pltpu.make_async_remote_copy(src_ref=ref if src is None else src, dst_ref=ref, send_sem=self.send_sems.at[w, pair],
                                            recv_sem=self.recv_sems.at[w, pair], device_id=to, device_id_type=MESH)

    def direct(self, w, k, q, src=None):
        return self._copy(self._quarter(w, self.j, q), w, 2 * k + q, (*self.chips[k], self.c), src)

    def direct_landed(self, w, k, q):
        return self._copy(self._quarter(w, self._chip(k), q), w, 2 * k + q, (*self.chips[k], self.c))

    def pass_on(self, w, q):
        return self._copy(self._quarter(w, self._chip(q), q), w, 4 + q, (*self.chips[1 - q], self.c))

    def passed_landed(self, w, q):
        return self._copy(self._quarter(w, self._chip(2), q), w, 4 + q, (*self.chips[1 - q], self.c))

    def hand_over(self, w, k):
        return self._copy(self._half(w, self._chip(k), self.c), w, 6 + k, self.sibling)

    def handed(self, w, k):
        return self._copy(self._half(w, self._chip(k), 1 - self.c), w, 6 + k, self.sibling)

    def start_direct(self, w, src_half=None):
        qr = self.outs[w].shape[1] // 4
        for k, q in ((0, 0), (1, 1), (0, 1), (1, 0)):
            self.direct(w, k, q, None if src_half is None else src_half.at[pl.ds(q * qr, qr), :]).start()

    def start_pass_on(self, w):
        for q in (0, 1):
            self.direct_landed(w, q, q).wait_recv()
            self.pass_on(w, q).start()

    def start_hand_over(self, w, diagonal):
        if diagonal:
            for q in (0, 1):
                self.passed_landed(w, q).wait_recv()
            self.hand_over(w, 2).start()
        else:
            for k in (0, 1):
                self.direct_landed(w, k, 1 - k).wait_recv()
                self.hand_over(w, k).start()

    def finish(self, w):
        for k in range(3):
            self.handed(w, k).wait_recv()
            self.hand_over(w, k).wait_send()
        for q in (0, 1):
            self.pass_on(w, q).wait_send()
            for k in (0, 1):
                self.direct(w, k, q).wait_send()


def _gather_first(w_in, w_out, w1, w2, small):
    bigs = (w_in, w_out, w1, w2)
    nb = len(bigs)

    def body(win_ref, wout_ref, w1_ref, w2_ref, sm_ref, gin, gout, g1, g2, gsm, st_in, st_out, st_1, st_2,
             send_sems, recv_sems, sm_send, sm_recv, local_sems):
        srcs = (win_ref, wout_ref, w1_ref, w2_ref)
        stages = (st_in, st_out, st_1, st_2)
        outs = (gin, gout, g1, g2)
        plan = _ShardGather(outs[:1], send_sems, recv_sems)
        j, c = plan.j, plan.c
        for src, st in zip(srcs, stages):
            st[...] = src[...].astype(BF16)
        local = [pltpu.make_async_copy(stages[w], outs[w].at[j], local_sems.at[w]) for w in range(nb)]
        local.append(pltpu.make_async_copy(sm_ref, gsm.at[j], local_sems.at[nb]))
        for cp in local:
            cp.start()

        def small_copy(k):
            px, py = plan.chips[k]
            return pltpu.make_async_remote_copy(src_ref=sm_ref, dst_ref=gsm.at[j], send_sem=sm_send.at[k],
                                                recv_sem=sm_recv.at[k], device_id=(px, py, c), device_id_type=MESH)

        def small_landed(k):
            px, py = plan.chips[k]
            return pltpu.make_async_remote_copy(src_ref=sm_ref, dst_ref=gsm.at[2 * px + py], send_sem=sm_send.at[k],
                                                recv_sem=sm_recv.at[k], device_id=(px, py, c), device_id_type=MESH)

        hr = w_in.shape[0] // 2
        plan.start_direct(0, st_in.at[pl.ds(c * hr, hr), :])
        for k in range(3):
            small_copy(k).start()
        plan.start_pass_on(0)
        plan.start_hand_over(0, diagonal=False)
        plan.start_hand_over(0, diagonal=True)
        for k in range(3):
            small_landed(k).wait_recv()
            small_copy(k).wait_send()
        plan.finish(0)
        for cp in local:
            cp.wait()

    def gathered(a, dtype):
        return jax.ShapeDtypeStruct((N_CHIPS,) + a.shape, dtype)

    return pl.pallas_call(
        body, name="gather_first",
        in_specs=[VMEM] * 5, out_specs=[ANY] * 5,
        out_shape=[gathered(a, BF16) for a in bigs] + [gathered(small, F32)],
        scratch_shapes=[pltpu.VMEM(a.shape, BF16) for a in bigs]
        + [pltpu.SemaphoreType.DMA((1, _ShardGather.PAIRS)), pltpu.SemaphoreType.DMA((1, _ShardGather.PAIRS)), pltpu.SemaphoreType.DMA((3,)),
           pltpu.SemaphoreType.DMA((3,)), pltpu.SemaphoreType.DMA((nb + 1,))],
        compiler_params=_params(),
    )(*bigs, small)


class _PartialExchange:
    def __init__(self, parts, arrived, send_sems, recv_sems):
        self.parts, self.arrived, self.send_sems, self.recv_sems = parts, arrived, send_sems, recv_sems
        x, y, c = _position()
        self.c, self.j = c, 2 * x + y
        self.chips = _other_chips(x, y)

    def _copy(self, w, k, slot):
        px, py = self.chips[k]
        return pltpu.make_async_remote_copy(
            src_ref=self.parts[w].at[2 * px + py], dst_ref=self.arrived[w].at[slot], send_sem=self.send_sems.at[w, k],
            recv_sem=self.recv_sems.at[w, k], device_id=(px, py, self.c), device_id_type=MESH)

    def start(self):
        for w in range(len(self.parts)):
            for k in range(3):
                self._copy(w, k, self.j).start()

    def wait(self):
        for w in range(len(self.parts)):
            for k in range(3):
                px, py = self.chips[k]
                self._copy(w, k, 2 * px + py).wait()


class _PackGather:
    def __init__(self, p_ref, all_ref, send_sems, recv_sems, local_sem):
        self.p_ref, self.all_ref, self.send_sems, self.recv_sems, self.local_sem = p_ref, all_ref, send_sems, recv_sems, local_sem
        x, y, c = _position()
        self.me, self.sibling, self.c = (x, y, c), (x, y, 1 - c), c
        self.chips = _other_chips(x, y)

    @staticmethod
    def semaphores():
        return [pltpu.SemaphoreType.DMA((7,)), pltpu.SemaphoreType.DMA((7,)), pltpu.SemaphoreType.DMA]

    def _copy(self, k, block, to, from_pack=False):
        px, py, pc = block
        slot = self.all_ref.at[4 * px + 2 * py + pc]
        return pltpu.make_async_remote_copy(src_ref=self.p_ref if from_pack else slot, dst_ref=slot, send_sem=self.send_sems.at[k],
                                            recv_sem=self.recv_sems.at[k], device_id=to, device_id_type=MESH)

    def _mine(self):
        x, y, c = self.me
        return pltpu.make_async_copy(self.p_ref, self.all_ref.at[4 * x + 2 * y + c], self.local_sem)

    def _first(self):
        return [self._copy(0, self.me, self.sibling, True)] + [
            self._copy(1 + k, self.me, (*chip, self.c), True) for k, chip in enumerate(self.chips)]

    def _passed(self):
        return [self._copy(4 + k, (*chip, self.c), self.sibling) for k, chip in enumerate(self.chips)]

    def start(self):
        self._mine().start()
        for cp in self._first():
            cp.start()

    def hand_over(self):
        for k, chip in enumerate(self.chips):
            self._copy(1 + k, (*chip, self.c), self.me).wait_recv()
            self._passed()[k].start()

    def finish(self):
        self._copy(0, self.sibling, self.me).wait_recv()
        for k, chip in enumerate(self.chips):
            self._copy(4 + k, (*chip, 1 - self.c), self.me).wait_recv()
        for cp in self._first() + self._passed():
            cp.wait_send()
        self._mine().wait()


class _DirectGather:
    def __init__(self, p_ref, all_ref, send_sems, recv_sems, local_sem):
        self.p_ref, self.all_ref, self.send_sems, self.recv_sems, self.local_sem = p_ref, all_ref, send_sems, recv_sems, local_sem
        self.me = _position()

    semaphores = _PackGather.semaphores

    def _peer(self, r):
        x, y, c = self.me
        return ((1 - x) if r & 4 else x, (1 - y) if r & 2 else y, (1 - c) if r & 1 else c)

    def _copy(self, r, slot_of):
        px, py, pc = slot_of
        return pltpu.make_async_remote_copy(src_ref=self.p_ref, dst_ref=self.all_ref.at[4 * px + 2 * py + pc],
                                            send_sem=self.send_sems.at[r - 1], recv_sem=self.recv_sems.at[r - 1],
                                            device_id=self._peer(r), device_id_type=MESH)

    def _mine(self):
        x, y, c = self.me
        return pltpu.make_async_copy(self.p_ref, self.all_ref.at[4 * x + 2 * y + c], self.local_sem)

    def start(self):
        self._mine().start()
        for r in range(1, N_DEVICES):
            self._copy(r, self.me).start()

    def finish(self):
        for r in range(1, N_DEVICES):
            self._copy(r, self._peer(r)).wait()
        self._mine().wait()


def _adamw(w, g, m, v):
    m = ADAM_B1 * m + (1.0 - ADAM_B1) * g
    v = ADAM_B2 * v + (1.0 - ADAM_B2) * (g * g)
    m_hat = m / ADAM_BC1
    v_hat = v / ADAM_BC2
    delta = -ADAM_LR * (m_hat / (jnp.sqrt(v_hat) + ADAM_EPS) + ADAM_WD * w)
    return delta, m, v


JOIN_SUB = 4


def _join(tag, shard_shape, part, arrived, core_chip, block=None):
    pr, pc = WGRAD_GEOMETRY[tag][:2]
    rb = pr // JOIN_SUB
    by_rows = shard_shape[1] == pc
    riding = block is not None

    def body(cc_ref, p_ref, r1_ref, r2_ref, r3_ref, *rest):
        if riding:
            blk_ref, g_ref, all_ref, stage, send_sems, recv_sems, local_sems, b_send, b_recv, b_local = rest
            gather = _DirectGather(blk_ref, all_ref, b_send, b_recv, b_local)
        else:
            g_ref, stage, send_sems, recv_sems, local_sems = rest
        i = pl.program_id(0)
        c = cc_ref[0]
        if riding:
            @pl.when(i == 0)
            def _():
                gather.start()

        def window(core, k):
            if by_rows:
                return g_ref.at[pl.ds((core * JOIN_SUB + k) * rb, rb), :]
            return g_ref.at[pl.ds(k * rb, rb), pl.ds(core * pc, pc)]

        def keep(k):
            return pltpu.make_async_copy(stage.at[k], window(c, k), local_sems.at[k])

        def push(k):
            return pltpu.make_async_remote_copy(src_ref=stage.at[k], dst_ref=window(c, k), send_sem=send_sems.at[k],
                                                recv_sem=recv_sems.at[k], device_id=_sibling(), device_id_type=MESH)

        def pushed(k):
            return pltpu.make_async_remote_copy(src_ref=stage.at[k], dst_ref=window(1 - c, k), send_sem=send_sems.at[k],
                                                recv_sem=recv_sems.at[k], device_id=_sibling(), device_id_type=MESH)

        stage[i] = ((p_ref[0] + r1_ref[0].astype(F32)) + r2_ref[0].astype(F32)) + r3_ref[0].astype(F32)
        keep(i).start()
        push(i).start()

        @pl.when(i == JOIN_SUB - 1)
        def _():
            for k in range(JOIN_SUB):
                keep(k).wait()
                push(k).wait_send()
                pushed(k).wait_recv()
            if riding:
                gather.finish()

    def partial(off):
        return pl.BlockSpec((1, rb, pc), lambda i, cc: ((cc[1] + off) % N_CHIPS, i, 0))

    in_specs = [partial(0), partial(1), partial(2), partial(3)]
    out_specs = [ANY]
    out_shape = [jax.ShapeDtypeStruct(shard_shape, F32)]
    scratch = [pltpu.VMEM((JOIN_SUB, rb, pc), F32), pltpu.SemaphoreType.DMA((JOIN_SUB,)),
               pltpu.SemaphoreType.DMA((JOIN_SUB,)), pltpu.SemaphoreType.DMA((JOIN_SUB,))]
    operands = [part, arrived, arrived, arrived]
    if riding:
        in_specs.append(pl.BlockSpec(block.shape, lambda i, cc: (0, 0)))
        out_specs.append(ANY)
        out_shape.append(jax.ShapeDtypeStruct((N_DEVICES,) + block.shape, block.dtype))
        scratch += _DirectGather.semaphores()
        operands.append(block)
    outs = pl.pallas_call(
        body, name="join_" + tag,
        grid_spec=pltpu.PrefetchScalarGridSpec(
            num_scalar_prefetch=1, grid=(JOIN_SUB,), in_specs=in_specs, out_specs=out_specs, scratch_shapes=scratch),
        out_shape=out_shape,
        compiler_params=_params(dimension_semantics=("arbitrary",)),
    )(core_chip, *operands)
    return outs if riding else outs[0]


def _adamw_big(w, g, m, v, name):
    rows, cols = w.shape
    rb = 256 if rows % 256 == 0 else rows

    def body(w_ref, g_ref, m_ref, v_ref, go_ref, d_ref, nm_ref, nv_ref):
        g = g_ref[...]
        go_ref[...] = g
        d_ref[...], nm_ref[...], nv_ref[...] = _adamw(w_ref[...], g, m_ref[...], v_ref[...])

    spec = pl.BlockSpec((rb, cols), lambda i: (i, 0))
    return pl.pallas_call(
        body, name=name, grid=(rows // rb,), in_specs=[spec] * 4, out_specs=[spec] * 4,
        out_shape=[jax.ShapeDtypeStruct(w.shape, F32)] * 4,
        compiler_params=_params(dimension_semantics=("arbitrary",)),
    )(w, g, m, v)


def _small_step(packs, mix_g_blocks, w_pack, m_pack, v_pack, conv_wmv, rconv_wmv):
    rows, cols = packs.shape[1:]
    cshard = conv_wmv.shape[2]
    rshard = rconv_wmv.shape[2]
    mix_row = PK_MIX_G * TILE_ROWS

    def body(all_ref, blk_ref, w_ref, m_ref, v_ref, cw_ref, rw_ref, g_ref, d_ref, nm_ref, nv_ref, co_ref, ro_ref):
        total = all_ref[0]
        late = blk_ref[0]
        for k in range(1, N_DEVICES):
            total = total + all_ref[k]
            late = late + blk_ref[k]
        g_ref[...] = total
        g_ref[mix_row:mix_row + TILE_ROWS, :] = late
        g = g_ref[...]
        d_ref[...], nm_ref[...], nv_ref[...] = _adamw(w_ref[...], g, m_ref[...], v_ref[...])

        x, y, _ = _position()
        j = 2 * x + y
        cblk = total[PK_CONV_W * 8:PK_CONV_W * 8 + 8, :]
        rblk = total[PK_RCONV_W * 8:PK_RCONV_W * 8 + 8, :]
        cg = cblk[:, 0:cshard]
        rg = rblk[:, 0:rshard]
        for k in range(1, N_CHIPS):
            cg = jnp.where(j == k, cblk[:, k * cshard:(k + 1) * cshard], cg)
            rg = jnp.where(j == k, rblk[:, k * rshard:(k + 1) * rshard], rg)
        co_ref[0] = cg
        co_ref[1], co_ref[2], co_ref[3] = _adamw(cw_ref[0], cg, cw_ref[1], cw_ref[2])
        ro_ref[0] = rg
        ro_ref[1], ro_ref[2], ro_ref[3] = _adamw(rw_ref[0], rg, rw_ref[1], rw_ref[2])

    pack = [jax.ShapeDtypeStruct((rows, cols), F32)] * 4
    return pl.pallas_call(
        body, name="small_grads_step", in_specs=[VMEM] * 7, out_specs=[VMEM] * 6,
        out_shape=pack + [jax.ShapeDtypeStruct((4, TILE_ROWS, cshard), F32), jax.ShapeDtypeStruct((4, TILE_ROWS, rshard), F32)],
        compiler_params=_params(),
    )(packs, mix_g_blocks, w_pack, m_pack, v_pack, conv_wmv, rconv_wmv)


def _blk(a):
    a = a.reshape(-1, a.shape[-1])
    return jnp.pad(a, ((0, TILE_ROWS - a.shape[0]), (0, D_MODEL - a.shape[1])))


def _zero_blk():
    return jnp.zeros((TILE_ROWS, D_MODEL), F32)


def _pack_params(p, pre):
    get = lambda n: p[pre + n]
    return jnp.concatenate([
        _blk(get("g_norm_rnn")), _blk(get("rnn_conv_b")), _blk(get("b_a")), _blk(get("b_x")), _blk(get("lru_lambda")),
        _zero_blk(), _zero_blk(), _blk(get("g_norm_conv")), _blk(get("final_norm_g").reshape(1, -1)), _blk(get("norm_mlp_g")),
        _zero_blk(), _blk(get("norm_mix_g")), get("w_a").reshape(64, D_MODEL), get("w_x").reshape(64, D_MODEL)], axis=0)


def _to_block_diag(w):
    w4 = w.reshape(N_BD, 4, 64, 64)
    eye = jnp.eye(4, dtype=w.dtype)
    return (w4[:, :, :, None, :] * eye[None, :, None, :, None]).reshape(N_BD, BD, BD)


def _from_block_diag(d):
    d5 = d.reshape(N_BD, 4, 64, 4, 64)
    return jnp.stack([d5[:, q, :, q, :] for q in range(4)], axis=1).reshape(64, D_MODEL)


def _pad_rows(a):
    return jnp.pad(a, ((0, TILE_ROWS - a.shape[0]), (0, 0)))


_NAMES = ['norm_mix_g', 'w_in', 'conv_w', 'rnn_conv_w', 'rnn_conv_b', 'w_a', 'b_a', 'w_x', 'b_x', 'lru_lambda',
          'g_norm_conv', 'g_norm_rnn', 'w_out', 'norm_mlp_g', 'w_mlp_in', 'w_mlp_out', 'final_norm_g']


def kernel(x, norm_mix_g, w_in, conv_w, rnn_conv_w, rnn_conv_b, w_a, b_a, w_x, b_x, lru_lambda, g_norm_conv, g_norm_rnn, w_out, norm_mlp_g, w_mlp_in, w_mlp_out, final_norm_g, loss_target, m_norm_mix_g, m_w_in, m_conv_w, m_rnn_conv_w, m_rnn_conv_b, m_w_a, m_b_a, m_w_x, m_b_x, m_lru_lambda, m_g_norm_conv, m_g_norm_rnn, m_w_out, m_norm_mlp_g, m_w_mlp_in, m_w_mlp_out, m_final_norm_g, v_norm_mix_g, v_w_in, v_conv_w, v_rnn_conv_w, v_rnn_conv_b, v_w_a, v_b_a, v_w_x, v_b_x, v_lru_lambda, v_g_norm_conv, v_g_norm_rnn, v_w_out, v_norm_mlp_g, v_w_mlp_in, v_w_mlp_out, v_final_norm_g):
    args = dict(locals())
    p = {}
    for n in _NAMES:
        for pre in ("", "m_", "v_"):
            a = args[pre + n]
            p[pre + n] = a[0] if a.ndim >= 3 else a
    xs = x[0]
    target = loss_target[0]
    core_chip = jnp.stack([lax.axis_index("c"), 2 * lax.axis_index("x") + lax.axis_index("y")]).astype(jnp.int32)
    cshard = p["conv_w"].shape[1]
    rshard = p["rnn_conv_w"].shape[1]

    small = jnp.concatenate([_pad_rows(p["conv_w"]), _pad_rows(p["rnn_conv_w"])], axis=1)
    w_in_g, w_out_g, w1_g, w2_g, small_g = _gather_first(p["w_in"], p["w_out"], p["w_mlp_in"], p["w_mlp_out"], small)
    conv_full = small_g[:, :3, :cshard].transpose(1, 0, 2).reshape(3, CONV_W)
    rconv_full = small_g[:, :4, cshard:].transpose(1, 0, 2).reshape(4, LRU_W)
    wa_bd = _to_block_diag(p["w_a"]).astype(BF16)
    wx_bd = _to_block_diag(p["w_x"]).astype(BF16)
    gf = p["final_norm_g"].reshape(1, -1)
    lru = (wa_bd, p["b_a"], wx_bd, p["b_x"], p["lru_lambda"], p["g_norm_conv"], p["g_norm_rnn"])

    (u, h1b, xr, hs, c3, yb, *gates), (w_out_g, w1_g, w2_g) = _fwd_mix(
        xs, p["norm_mix_g"], w_in_g, conv_full, rconv_full, p["rnn_conv_b"], *lru, (w_out_g, w1_g, w2_g))
    zb, dpb, h2b, dx3b, dx2, dx2b, dy, st_mlp = _mlp_fwd_bwd(
        xs, yb, w_out_g.reshape(-1, D_MODEL), w1_g, w2_g.reshape(-1, D_MODEL), p["norm_mlp_g"], gf, target)

    part_out = _wgrad(yb, dx2b, "out", core_chip)
    part_1 = _wgrad(h2b, dpb, "mlp_in", core_chip)
    part_2 = _wgrad(zb, dx3b, "mlp_out", core_chip)
    (dub, st_mix, dwa_bd, dwx_bd), arrived_mlp = _mix_bwd(
        dy, u, xr, hs, c3, gates, conv_full, rconv_full, wa_bd, wx_bd, p["lru_lambda"], p["g_norm_conv"], p["g_norm_rnn"],
        (part_out[1], part_1[1], part_2[1]))
    pack = jnp.concatenate([st_mix, st_mlp, _zero_blk(), _from_block_diag(dwa_bd), _from_block_diag(dwx_bd)], axis=0)
    *part_in, packs = _wgrad(h1b, dub, "in", core_chip, pack)
    early = (("w_out", "out", part_out, arrived_mlp[0]), ("w_mlp_in", "mlp_in", part_1, arrived_mlp[1]),
             ("w_mlp_out", "mlp_out", part_2, arrived_mlp[2]))
    (grad_x, st_in), arrived_in, joined = _in_bwd(
        dub, w_in_g, xs, dx2, p["norm_mix_g"], (part_in[1],),
        [(tag, p[n].shape, part[0], arrived) for n, tag, part, arrived in early], core_chip)
    g_in, mix_g_blocks = _join("in", p["w_in"].shape, part_in[0], arrived_in[0], core_chip, st_in)
    big = {}
    for n, tag, g in [(n, tag, g) for (n, tag, _, _), g in zip(early, joined)] + [("w_in", "in", g_in)]:
        big[n] = _adamw_big(p[n], g, p["m_" + n], p["v_" + n], "adamw_" + tag)

    conv_wmv = jnp.stack([_pad_rows(p[pre + "conv_w"]) for pre in ("", "m_", "v_")])
    rconv_wmv = jnp.stack([_pad_rows(p[pre + "rnn_conv_w"]) for pre in ("", "m_", "v_")])
    g_pack, d_pack, m_pack, v_pack, conv_out, rconv_out = _small_step(
        packs, mix_g_blocks, _pack_params(p, ""), _pack_params(p, "m_"), _pack_params(p, "v_"), conv_wmv, rconv_wmv)

    def unpack(pk, kind):
        def vec(b, width=D_MODEL):
            return pk[b * 8:b * 8 + 1, :width]
        return {
            "norm_mix_g": vec(PK_MIX_G), "rnn_conv_b": vec(PK_RCONV_B), "b_a": vec(PK_B_A), "b_x": vec(PK_B_X),
            "lru_lambda": vec(PK_LAMBDA), "g_norm_conv": vec(PK_G_NORM_CONV, CONV_W), "g_norm_rnn": vec(PK_G_NORM_RNN),
            "norm_mlp_g": vec(PK_MLP_G), "final_norm_g": vec(PK_FINAL_G).reshape(-1),
            "w_a": pk[PK_W_A * 8:PK_W_A * 8 + 64].reshape(1, 16, 64, 64), "w_x": pk[PK_W_X * 8:PK_W_X * 8 + 64].reshape(1, 16, 64, 64),
            "conv_w": conv_out[kind, :3][None], "rnn_conv_w": rconv_out[kind, :4][None],
            "w_in": big["w_in"][kind][None], "w_out": big["w_out"][kind][None],
            "w_mlp_in": big["w_mlp_in"][kind][None], "w_mlp_out": big["w_mlp_out"][kind][None],
        }

    outs = [unpack(pk, kind) for kind, pk in enumerate((g_pack, d_pack, m_pack, v_pack))]
    for o in outs:
        for n in ("norm_mix_g", "rnn_conv_b", "b_a", "b_x", "lru_lambda", "g_norm_conv", "g_norm_rnn", "norm_mlp_g"):
            o[n] = o[n].reshape(1, -1)
    loss = g_pack[PK_LOSS * 8, 0]
    return (loss, grad_x[None], *[o[n] for o in outs for n in _NAMES])
```

```python
import functools
import math

import jax
import jax.numpy as jnp
from jax import lax
from jax.experimental import pallas as pl
from jax.experimental.pallas import tpu as pltpu

F32 = jnp.float32
BF16 = jnp.bfloat16
MESH = pl.DeviceIdType.MESH
ANY = pl.BlockSpec(memory_space=pl.ANY)
VMEM = pl.BlockSpec(memory_space=pltpu.VMEM)

EPS = 1e-6
LRU_C = 8.0
D_MODEL = 1024
CONV_W = 512
LRU_W = 1024
IN_COLS = 3 * CONV_W + 2 * LRU_W
IN_SHARD = IN_COLS // 4
N_CHIPS = 4
N_DEVICES = 8
BD = 256
N_BD = LRU_W // BD

ADAM_LR = 0.001
ADAM_B1 = 0.9
ADAM_B2 = 0.999
ADAM_EPS = 1e-08
ADAM_WD = 0.01
ADAM_STEP = 10
ADAM_BC1 = 1.0 - ADAM_B1 ** ADAM_STEP
ADAM_BC2 = 1.0 - ADAM_B2 ** ADAM_STEP

TILE_ROWS = 8
TOKEN_TILE = 256
VMEM_LIMIT = 56 * 1024 * 1024

PK_G_NORM_RNN, PK_RCONV_B, PK_B_A, PK_B_X, PK_LAMBDA, PK_RCONV_W, PK_CONV_W, PK_G_NORM_CONV = range(8)
PK_FINAL_G, PK_MLP_G, PK_LOSS, PK_MIX_G = 8, 9, 10, 11
PK_W_A = 12
PK_W_X = 20
PK_BLOCKS = 28
PK_ROWS = PK_BLOCKS * TILE_ROWS


def _params(**kw):
    return pltpu.CompilerParams(vmem_limit_bytes=VMEM_LIMIT, **kw)


def _position():
    x, y, c = lax.axis_index("x"), lax.axis_index("y"), lax.axis_index("c")
    return x, y, c


def _sigmoid(v):
    return 1.0 / (1.0 + jnp.exp(-v))


def _one_minus_square(log_a, a):
    v = 2.0 * log_a
    series = -v * (1.0 + v * (0.5 + v * (1.0 / 6.0)))
    return jnp.where(v > -0.01, series, 1.0 - a * a)


_GELU_C = math.sqrt(2.0 / math.pi)
_GELU_K = 0.044715


def _gelu_and_grad(g):
    th = jnp.tanh(_GELU_C * (g + _GELU_K * g * g * g))
    gelu = 0.5 * g * (1.0 + th)
    dgelu = 0.5 * (1.0 + th) + 0.5 * g * (1.0 - th * th) * (_GELU_C * (1.0 + 3.0 * _GELU_K * g * g))
    return gelu, dgelu


def _rows(shape):
    return lax.broadcasted_iota(jnp.int32, shape, 0)


def _shift_down(v, k, prev8):
    rolled = pltpu.roll(v, k, 0)
    halo = pltpu.roll(prev8, k, 0)
    head = jnp.where(_rows(halo.shape) < k, halo, rolled[:TILE_ROWS])
    return jnp.concatenate([head, rolled[TILE_ROWS:]], axis=0)


def _shift_up(v, k, next8):
    n = v.shape[0]
    rolled = pltpu.roll(v, n - k, 0)
    halo = pltpu.roll(next8, TILE_ROWS - k, 0)
    tail = jnp.where(_rows(halo.shape) >= TILE_ROWS - k, halo, rolled[n - TILE_ROWS:])
    return jnp.concatenate([rolled[: n - TILE_ROWS], tail], axis=0)


def _scan_rows(a, b, carry, reverse=False):
    n, w = a.shape
    groups = n // TILE_ROWS
    a3 = a.reshape(groups, TILE_ROWS, w)
    b3 = b.reshape(groups, TILE_ROWS, w)
    sub = lax.broadcasted_iota(jnp.int32, a3.shape, 1)
    s = 1
    while s < TILE_ROWS:
        shift = TILE_ROWS - s if reverse else s
        keep = (sub < TILE_ROWS - s) if reverse else (sub >= s)
        b3 = b3 + jnp.where(keep, a3 * pltpu.roll(b3, shift, 1), 0.0)
        a3 = a3 * jnp.where(keep, pltpu.roll(a3, shift, 1), 1.0)
        s *= 2
    out = [None] * groups
    edge = 0 if reverse else TILE_ROWS - 1
    for g in (range(groups - 1, -1, -1) if reverse else range(groups)):
        out[g] = b3[g] + a3[g] * carry
        carry = out[g][edge:edge + 1]
    return jnp.concatenate(out, axis=0)


def _softplus_neg(lam):
    e = jnp.exp(-jnp.abs(lam))
    log1p_e = jnp.where(e < 1e-2, e * (1.0 - e * (0.5 - e * (1.0 / 3.0 - e * 0.25))), jnp.log(1.0 + e))
    sp = jnp.maximum(-lam, 0.0) + log1p_e
    dsp = -_sigmoid(-lam)
    return sp, dsp


def _block_diag_dot(vb, w_ref):
    return jnp.concatenate(
        [jnp.dot(vb[:, j * BD:(j + 1) * BD], w_ref[j], preferred_element_type=F32) for j in range(N_BD)], axis=1)


def _block_diag_dot_t(vb, w_ref):
    return jnp.concatenate(
        [lax.dot_general(vb[:, j * BD:(j + 1) * BD], w_ref[j], (((1,), (1,)), ((), ())), preferred_element_type=F32)
         for j in range(N_BD)], axis=1)


def _dot_nt(a, b):
    return lax.dot_general(a, b, (((1,), (1,)), ((), ())), preferred_element_type=F32)


def _dot_tn(a, b):
    return lax.dot_general(a, b, (((0,), (0,)), ((), ())), preferred_element_type=F32)


def _lru_gates(xr, wa_ref, ba, wx_ref, bx, sp):
    xrb = xr.astype(BF16)
    r = _sigmoid(_block_diag_dot(xrb, wa_ref) + ba)
    ig = _sigmoid(_block_diag_dot(xrb, wx_ref) + bx)
    log_a = (-LRU_C) * r * sp
    a = jnp.exp(log_a)
    mult = jnp.sqrt(_one_minus_square(log_a, a))
    return r, ig, a, mult


def _colsum(v):
    return jnp.sum(v, axis=0, keepdims=True)


N_FWD_OUT = 10


def _fwd_mix(x, g1, w_in_g, conv_w, rconv_w, rconv_b, wa_bd, b_a, wx_bd, b_x, lam, g_nc, g_nr, later):
    t, d = x.shape
    tm = TOKEN_TILE
    nt = t // tm
    nl = len(later)
    assert nl == 3
    pass_on_at = [nt * f // 16 for f in (1, 3, 7)]
    neighbours_at = [nt * f // 16 for f in (3, 6, 10)]
    diagonal_at = [nt * f // 16 for f in (10, 12, 14)]

    def body(x_ref, g1_ref, win_ref, cw_ref, rw_ref, rb_ref, wa_ref, ba_ref, wx_ref, bx_ref, lam_ref, gnc_ref, gnr_ref,
             *rest):
        later_in, outs, rest = rest[:nl], rest[nl:nl + N_FWD_OUT], rest[nl + N_FWD_OUT:]
        u_ref, h1_ref, xr_ref, hs_ref, c3_ref, y_ref, r_ref, ig_ref, a_ref, mult_ref = outs
        later_out, (cv_prev, xin_prev, h_prev, send_sems, recv_sems) = rest[:nl], rest[nl:]
        del later_in
        step = pl.program_id(0)
        plan = _ShardGather(later_out, send_sems, recv_sems)

        @pl.when(step == 0)
        def _():
            cv_prev[...] = jnp.zeros_like(cv_prev)
            xin_prev[...] = jnp.zeros_like(xin_prev)
            h_prev[...] = jnp.zeros_like(h_prev)
            for w in range(nl):
                plan.start_direct(w)

        for w in range(nl):
            @pl.when(step == pass_on_at[w])
            def _(w=w):
                plan.start_pass_on(w)

            @pl.when(step == neighbours_at[w])
            def _(w=w):
                plan.start_hand_over(w, diagonal=False)

            @pl.when(step == diagonal_at[w])
            def _(w=w):
                plan.start_hand_over(w, diagonal=True)

        xv = x_ref[...]
        rstd = lax.rsqrt(jnp.mean(xv * xv, axis=-1, keepdims=True) + EPS)
        h1b = ((xv * rstd) * g1_ref[...]).astype(BF16)
        h1_ref[...] = h1b
        for j in range(N_CHIPS):
            u_ref[:, j * IN_SHARD:(j + 1) * IN_SHARD] = jnp.dot(h1b, win_ref[j], preferred_element_type=F32)
        gate_b = u_ref[:, 0:CONV_W]
        cv = u_ref[:, CONV_W:2 * CONV_W] * u_ref[:, 2 * CONV_W:3 * CONV_W]
        x_r = u_ref[:, 3 * CONV_W:3 * CONV_W + LRU_W]
        g = u_ref[:, 3 * CONV_W + LRU_W:]

        cw = cw_ref[...]
        cvp = cv_prev[...]
        conv3 = cw[0:1] * _shift_down(cv, 2, cvp) + cw[1:2] * _shift_down(cv, 1, cvp) + cw[2:3] * cv
        cv_prev[...] = cv[tm - TILE_ROWS:]
        c3_ref[...] = conv3
        y_conv = gate_b * conv3

        rw = rw_ref[...]
        xp = xin_prev[...]
        xr = (rw[0:1] * _shift_down(x_r, 3, xp) + rw[1:2] * _shift_down(x_r, 2, xp)
              + rw[2:3] * _shift_down(x_r, 1, xp) + rw[3:4] * x_r) + rb_ref[...]
        xin_prev[...] = x_r[tm - TILE_ROWS:]
        xr_ref[...] = xr
        sp, _ = _softplus_neg(lam_ref[...])
        r, ig, a, mult = _lru_gates(xr, wa_ref, ba_ref[...], wx_ref, bx_ref[...], sp)
        r_ref[...] = r
        ig_ref[...] = ig
        a_ref[...] = a
        mult_ref[...] = mult
        h = _scan_rows(a, mult * (ig * xr), h_prev[...])
        h_prev[...] = h[tm - 1:tm]
        hs_ref[...] = h
        gelu, _ = _gelu_and_grad(g)
        y_rnn = h * gelu

        na = y_conv * lax.rsqrt(jnp.mean(y_conv * y_conv, axis=-1, keepdims=True) + EPS) * gnc_ref[...]
        nb = y_rnn * lax.rsqrt(jnp.mean(y_rnn * y_rnn, axis=-1, keepdims=True) + EPS) * gnr_ref[...]
        y_ref[:, :CONV_W] = na.astype(BF16)
        y_ref[:, CONV_W:] = nb.astype(BF16)

        @pl.when(step == nt - 1)
        def _():
            for w in range(nl):
                plan.finish(w)

    def full(a):
        nd = a.ndim
        return pl.BlockSpec(a.shape, lambda i: (0,) * nd)

    def tok(cols):
        return pl.BlockSpec((tm, cols), lambda i: (i, 0))

    def act(cols, dtype=F32):
        return jax.ShapeDtypeStruct((t, cols), dtype)

    smalls = (g1, w_in_g, conv_w, rconv_w, rconv_b, wa_bd, b_a, wx_bd, b_x, lam, g_nc, g_nr)
    n_in = 1 + len(smalls)
    outs = pl.pallas_call(
        body, name="fwd_mix", grid=(nt,),
        in_specs=[tok(d)] + [full(a) for a in smalls] + [ANY] * nl,
        out_specs=[tok(IN_COLS), tok(d), tok(LRU_W), tok(LRU_W), tok(CONV_W), tok(CONV_W + LRU_W)]
        + [tok(LRU_W)] * 4 + [ANY] * nl,
        out_shape=[act(IN_COLS), act(d, BF16), act(LRU_W), act(LRU_W), act(CONV_W), act(CONV_W + LRU_W, BF16)]
        + [act(LRU_W)] * 4 + [jax.ShapeDtypeStruct(a.shape, a.dtype) for a in later],
        input_output_aliases={n_in + w: N_FWD_OUT + w for w in range(nl)},
        scratch_shapes=[pltpu.VMEM((TILE_ROWS, CONV_W), F32), pltpu.VMEM((TILE_ROWS, LRU_W), F32),
                        pltpu.VMEM((1, LRU_W), F32), pltpu.SemaphoreType.DMA((nl, _ShardGather.PAIRS)),
                        pltpu.SemaphoreType.DMA((nl, _ShardGather.PAIRS))],
        compiler_params=_params(dimension_semantics=("arbitrary",)),
    )(x, *smalls, *later)
    return outs[:N_FWD_OUT], outs[N_FWD_OUT:]


def _mlp_fwd_bwd(x, yb, w_out_g, w1_g, w2_g, g2, gf, target):
    t, d = x.shape
    tm = TOKEN_TILE
    ff = w2_g.shape[0]
    mix = w_out_g.shape[0]
    ffs = ff // N_CHIPS

    def body(x_ref, y_ref, g2_ref, gf_ref, tgt_ref, wout_hbm, w1_hbm, w2_hbm,
             z_ref, dp_ref, h2_ref, dx3b_ref, dx2_ref, dx2b_ref, dy_ref, st_ref, wout, w1, w2, p_ref):
        @pl.when(pl.program_id(0) == 0)
        def _():
            pltpu.sync_copy(wout_hbm, wout)
            pltpu.sync_copy(w1_hbm, w1)
            pltpu.sync_copy(w2_hbm, w2)
            st_ref[...] = jnp.zeros_like(st_ref)

        x2 = x_ref[...] + jnp.dot(y_ref[...], wout[...], preferred_element_type=F32)
        r2 = lax.rsqrt(jnp.mean(x2 * x2, axis=-1, keepdims=True) + EPS)
        xh2 = x2 * r2
        g2v = g2_ref[...]
        h2b = (xh2 * g2v).astype(BF16)
        h2_ref[...] = h2b
        for j in range(N_CHIPS):
            p_ref[:, j * ffs:(j + 1) * ffs] = jnp.dot(h2b, w1[j], preferred_element_type=F32)
        rp = jnp.maximum(p_ref[...], 0.0)
        zb = (rp * rp).astype(BF16)
        z_ref[...] = zb
        x3 = x2 + jnp.dot(zb, w2[...], preferred_element_type=F32)
        r3 = lax.rsqrt(jnp.mean(x3 * x3, axis=-1, keepdims=True) + EPS)
        xh3 = x3 * r3
        gfv = gf_ref[...]
        err = xh3 * gfv - tgt_ref[...]
        loss = (0.5 / d) * jnp.sum(err * err)
        dout = err * (1.0 / d)
        st_ref[PK_FINAL_G * 8 - 64:PK_FINAL_G * 8 - 63, :] += _colsum(dout * xh3)
        st_ref[PK_LOSS * 8 - 64:PK_LOSS * 8 - 63, :] += jnp.zeros((1, d), F32) + loss
        dxh3 = dout * gfv
        dx3 = r3 * (dxh3 - xh3 * jnp.mean(dxh3 * xh3, axis=-1, keepdims=True))
        dx3b = dx3.astype(BF16)
        dx3b_ref[...] = dx3b
        dpb = (_dot_nt(dx3b, w2[...]) * (2.0 * rp)).astype(BF16)
        dp_ref[...] = dpb
        dh2 = _dot_nt(dpb[:, 0:ffs], w1[0])
        for j in range(1, N_CHIPS):
            dh2 = dh2 + _dot_nt(dpb[:, j * ffs:(j + 1) * ffs], w1[j])
        st_ref[PK_MLP_G * 8 - 64:PK_MLP_G * 8 - 63, :] += _colsum(dh2 * xh2)
        dxh2 = dh2 * g2v
        dx2 = dx3 + r2 * (dxh2 - xh2 * jnp.mean(dxh2 * xh2, axis=-1, keepdims=True))
        dx2_ref[...] = dx2
        dx2b = dx2.astype(BF16)
        dx2b_ref[...] = dx2b
        dy_ref[...] = _dot_nt(dx2b, wout[...])

    def tok(cols):
        return pl.BlockSpec((tm, cols), lambda i: (i, 0))

    def row(cols):
        return pl.BlockSpec((1, cols), lambda i: (0, 0))

    return pl.pallas_call(
        body, name="mlp_fwd_bwd", grid=(t // tm,),
        in_specs=[tok(d), tok(mix), row(d), row(d), tok(d), ANY, ANY, ANY],
        out_specs=[tok(ff), tok(ff), tok(d), tok(d), tok(d), tok(d), tok(mix),
                   pl.BlockSpec((3 * TILE_ROWS, d), lambda i: (0, 0))],
        out_shape=[jax.ShapeDtypeStruct((t, ff), BF16), jax.ShapeDtypeStruct((t, ff), BF16),
                   jax.ShapeDtypeStruct((t, d), BF16), jax.ShapeDtypeStruct((t, d), BF16),
                   jax.ShapeDtypeStruct((t, d), F32), jax.ShapeDtypeStruct((t, d), BF16),
                   jax.ShapeDtypeStruct((t, mix), F32), jax.ShapeDtypeStruct((3 * TILE_ROWS, d), F32)],
        scratch_shapes=[pltpu.VMEM(w_out_g.shape, BF16), pltpu.VMEM(w1_g.shape, BF16), pltpu.VMEM(w2_g.shape, BF16),
                        pltpu.VMEM((tm, ff), F32)],
        compiler_params=_params(dimension_semantics=("arbitrary",)),
    )(x, yb, g2, gf, target, w_out_g, w1_g, w2_g)


def _mix_bwd(dy, u, xr_all, hs_all, c3_all, gates, conv_w, rconv_w, wa_bd, wx_bd, lam, g_nc, g_nr, parts):
    t = dy.shape[0]
    tm = TOKEN_TILE
    nt = t // tm
    hb = tm // TILE_ROWS
    npart = len(parts)

    def body(dy_ref, u_ref, uh_ref, xr_ref, hs_ref, hh_ref, c3_ref, r_ref, ig_ref, a_ref, mult_ref,
             cw_ref, rw_ref, wa_ref, wx_ref, lam_ref, gnc_ref, gnr_ref, *rest):
        part_refs, (du_ref, st_ref, dwa_ref, dwx_ref), rest = rest[:npart], rest[npart:npart + 4], rest[npart + 4:]
        arrived_refs, (dc_next, a_next, gs_next, dxr_next, send_sems, recv_sems) = rest[:npart], rest[npart:]
        exchange = _PartialExchange(part_refs, arrived_refs, send_sems, recv_sems)
        i = pl.program_id(0)

        @pl.when(i == 0)
        def _():
            exchange.start()
            dc_next[...] = jnp.zeros_like(dc_next)
            a_next[...] = jnp.zeros_like(a_next)
            gs_next[...] = jnp.zeros_like(gs_next)
            dxr_next[...] = jnp.zeros_like(dxr_next)
            st_ref[...] = jnp.zeros_like(st_ref)
            dwa_ref[...] = jnp.zeros_like(dwa_ref)
            dwx_ref[...] = jnp.zeros_like(dwx_ref)

        first_tile = i == nt - 1
        gate_b = u_ref[:, 0:CONV_W]
        gate_c = u_ref[:, CONV_W:2 * CONV_W]
        v = u_ref[:, 2 * CONV_W:3 * CONV_W]
        x_r = u_ref[:, 3 * CONV_W:3 * CONV_W + LRU_W]
        g = u_ref[:, 3 * CONV_W + LRU_W:]
        cv = gate_c * v
        cv_prev = jnp.where(first_tile, 0.0, uh_ref[:, CONV_W:2 * CONV_W] * uh_ref[:, 2 * CONV_W:3 * CONV_W])
        xin_prev = jnp.where(first_tile, 0.0, uh_ref[:, 3 * CONV_W:3 * CONV_W + LRU_W])
        hs_prev = jnp.where(first_tile, 0.0, hh_ref[...])

        def acc(block, val, width=LRU_W, row=0):
            r0 = block * TILE_ROWS + row
            st_ref[r0:r0 + 1, 0:width] += val

        conv3 = c3_ref[...]
        y_conv = gate_b * conv3
        ra = lax.rsqrt(jnp.mean(y_conv * y_conv, axis=-1, keepdims=True) + EPS)
        xha = y_conv * ra
        dna = dy_ref[:, :CONV_W]
        acc(PK_G_NORM_CONV, _colsum(dna * xha), CONV_W)
        dxha = dna * gnc_ref[...]
        dy_conv = ra * (dxha - xha * jnp.mean(dxha * xha, axis=-1, keepdims=True))
        du_ref[:, 0:CONV_W] = (dy_conv * conv3).astype(BF16)
        dc = dy_conv * gate_b
        cw = cw_ref[...]
        dcn = dc_next[...]
        dcv = cw[2:3] * dc + cw[1:2] * _shift_up(dc, 1, dcn) + cw[0:1] * _shift_up(dc, 2, dcn)
        dc_next[...] = dc[:TILE_ROWS]
        acc(PK_CONV_W, _colsum(dc * _shift_down(cv, 2, cv_prev)), CONV_W, 0)
        acc(PK_CONV_W, _colsum(dc * _shift_down(cv, 1, cv_prev)), CONV_W, 1)
        acc(PK_CONV_W, _colsum(dc * cv), CONV_W, 2)
        du_ref[:, CONV_W:2 * CONV_W] = (dcv * v).astype(BF16)
        du_ref[:, 2 * CONV_W:3 * CONV_W] = (dcv * gate_c).astype(BF16)

        hs = hs_ref[...]
        gelu, dgelu = _gelu_and_grad(g)
        y_rnn = hs * gelu
        rb = lax.rsqrt(jnp.mean(y_rnn * y_rnn, axis=-1, keepdims=True) + EPS)
        xhb = y_rnn * rb
        dnb = dy_ref[:, CONV_W:]
        acc(PK_G_NORM_RNN, _colsum(dnb * xhb))
        dxhb = dnb * gnr_ref[...]
        dy_rnn = rb * (dxhb - xhb * jnp.mean(dxhb * xhb, axis=-1, keepdims=True))
        du_ref[:, 3 * CONV_W + LRU_W:] = (dy_rnn * hs * dgelu).astype(BF16)
        dh = dy_rnn * gelu

        xr = xr_ref[...]
        xrb = xr.astype(BF16)
        sp, dsp = _softplus_neg(lam_ref[...])
        r, ig, a, mult = r_ref[...], ig_ref[...], a_ref[...], mult_ref[...]
        a_up = _shift_up(a, 1, a_next[...])
        a_next[...] = a[:TILE_ROWS]
        gs = _scan_rows(a_up, dh, gs_next[0:1, :], reverse=True)
        gs_next[...] = gs[:TILE_ROWS]
        da = gs * _shift_down(hs, 1, hs_prev)
        gx = gs * xr
        di = gx * mult
        dmult = gx * ig
        dxr = gs * (mult * ig)
        dlog_a = da * a - dmult * ((a * a) / mult)
        acc(PK_LAMBDA, _colsum(dlog_a * r) * ((-LRU_C) * dsp))
        dpa = (dlog_a * ((-LRU_C) * sp)) * (r * (1.0 - r))
        dpx = di * (ig * (1.0 - ig))
        acc(PK_B_A, _colsum(dpa))
        acc(PK_B_X, _colsum(dpx))
        dpab = dpa.astype(BF16)
        dpxb = dpx.astype(BF16)
        dxr = dxr + _block_diag_dot_t(dpab, wa_ref) + _block_diag_dot_t(dpxb, wx_ref)
        for j in range(N_BD):
            cols = slice(j * BD, (j + 1) * BD)
            dwa_ref[j] += _dot_tn(xrb[:, cols], dpab[:, cols])
            dwx_ref[j] += _dot_tn(xrb[:, cols], dpxb[:, cols])

        acc(PK_RCONV_B, _colsum(dxr))
        rw = rw_ref[...]
        dxn = dxr_next[...]
        dx_r = (rw[3:4] * dxr + rw[2:3] * _shift_up(dxr, 1, dxn) + rw[1:2] * _shift_up(dxr, 2, dxn)
                + rw[0:1] * _shift_up(dxr, 3, dxn))
        dxr_next[...] = dxr[:TILE_ROWS]
        for k in range(3):
            acc(PK_RCONV_W, _colsum(dxr * _shift_down(x_r, 3 - k, xin_prev)), LRU_W, k)
        acc(PK_RCONV_W, _colsum(dxr * x_r), LRU_W, 3)
        du_ref[:, 3 * CONV_W:3 * CONV_W + LRU_W] = dx_r.astype(BF16)

        @pl.when(i == nt - 1)
        def _():
            exchange.wait()

    def full(a):
        nd = a.ndim
        return pl.BlockSpec(a.shape, lambda i: (0,) * nd)

    def tok(cols):
        return pl.BlockSpec((tm, cols), lambda i: (nt - 1 - i, 0))

    def halo(cols):
        return pl.BlockSpec((TILE_ROWS, cols), lambda i: (jnp.maximum((nt - 1 - i) * hb - 1, 0), 0))

    smalls = (conv_w, rconv_w, wa_bd, wx_bd, lam, g_nc, g_nr)
    outs = pl.pallas_call(
        body, name="mix_bwd", grid=(nt,),
        in_specs=[tok(CONV_W + LRU_W), tok(IN_COLS), halo(IN_COLS), tok(LRU_W), tok(LRU_W), halo(LRU_W), tok(CONV_W)]
        + [tok(LRU_W)] * 4 + [full(a) for a in smalls] + [ANY] * npart,
        out_specs=[tok(IN_COLS), pl.BlockSpec((8 * TILE_ROWS, LRU_W), lambda i: (0, 0)),
                   pl.BlockSpec((N_BD, BD, BD), lambda i: (0, 0, 0)), pl.BlockSpec((N_BD, BD, BD), lambda i: (0, 0, 0))]
        + [ANY] * npart,
        out_shape=[jax.ShapeDtypeStruct((t, IN_COLS), BF16), jax.ShapeDtypeStruct((8 * TILE_ROWS, LRU_W), F32),
                   jax.ShapeDtypeStruct((N_BD, BD, BD), F32), jax.ShapeDtypeStruct((N_BD, BD, BD), F32)]
        + [jax.ShapeDtypeStruct(a.shape, a.dtype) for a in parts],
        scratch_shapes=[pltpu.VMEM((TILE_ROWS, CONV_W), F32), pltpu.VMEM((TILE_ROWS, LRU_W), F32),
                        pltpu.VMEM((TILE_ROWS, LRU_W), F32), pltpu.VMEM((TILE_ROWS, LRU_W), F32),
                        pltpu.SemaphoreType.DMA((npart, 3)), pltpu.SemaphoreType.DMA((npart, 3))],
        compiler_params=_params(dimension_semantics=("arbitrary",)),
    )(dy, u, u, xr_all, hs_all, hs_all, c3_all, *gates, *smalls, *parts)
    return outs[:4], outs[4:]


def _in_bwd(dub, w_in_g, x, dx2, g1, parts, joins, core_chip):
    t, d = x.shape
    tm = TOKEN_TILE
    nt = t // tm
    npart = len(parts)
    nj = len(joins)
    geometry = []
    for tag, shape, _, _ in joins:
        pr, pc = WGRAD_GEOMETRY[tag][:2]
        every = 1 if pr % (nt * 16) == 0 else 2
        geometry.append((pr, pc, pr * every // nt, every, shape[1] == pc))

    def body(cc_ref, du_ref, win_ref, x_ref, dx2_ref, g1_ref, *rest):
        sums, rest = [rest[4 * w:4 * w + 4] for w in range(nj)], rest[4 * nj:]
        part_refs, (gx_ref, st_ref), rest = rest[:npart], rest[npart:npart + 2], rest[npart + 2:]
        arrived_refs, joined, rest = rest[:npart], rest[npart:npart + nj], rest[npart + nj:]
        stages, (send_sems, recv_sems, j_local, j_send, j_recv) = rest[:nj], rest[nj:]
        exchange = _PartialExchange(part_refs, arrived_refs, send_sems, recv_sems)
        i = pl.program_id(0)
        c = cc_ref[0]

        def window(w, core, row0, rows):
            pr, pc, _, _, by_rows = geometry[w]
            if by_rows:
                return joined[w].at[pl.ds(core * pr + row0, rows), :]
            return joined[w].at[pl.ds(row0, rows), pl.ds(core * pc, pc)]

        def to_sibling(w, src, core, row0, rows):
            return pltpu.make_async_remote_copy(src_ref=src, dst_ref=window(w, core, row0, rows), send_sem=j_send.at[w],
                                                recv_sem=j_recv.at[w], device_id=_sibling(), device_id_type=MESH)

        @pl.when(i == 0)
        def _():
            exchange.start()
            st_ref[...] = jnp.zeros_like(st_ref)

        for w in range(nj):
            pr, pc, rb, every, _ = geometry[w]

            @pl.when(i % every == 0)
            def _(w=w, rb=rb, every=every):
                p_ref, r1_ref, r2_ref, r3_ref = sums[w]
                row0 = pl.multiple_of((i // every) * rb, rb)
                rows = stages[w].at[pl.ds(row0, rb), :]
                rows[...] = ((p_ref[0] + r1_ref[0].astype(F32)) + r2_ref[0].astype(F32)) + r3_ref[0].astype(F32)
                pltpu.make_async_copy(rows, window(w, c, row0, rb), j_local.at[w]).start()
                to_sibling(w, rows, c, row0, rb).start()

        dh1 = _dot_nt(du_ref[:, 0:IN_SHARD], win_ref[0])
        for j in range(1, N_CHIPS):
            dh1 = dh1 + _dot_nt(du_ref[:, j * IN_SHARD:(j + 1) * IN_SHARD], win_ref[j])
        xv = x_ref[...]
        rstd = lax.rsqrt(jnp.mean(xv * xv, axis=-1, keepdims=True) + EPS)
        xh = xv * rstd
        st_ref[0:1, :] += _colsum(dh1 * xh)
        dxh = dh1 * g1_ref[...]
        gx_ref[...] = dx2_ref[...] + rstd * (dxh - xh * jnp.mean(dxh * xh, axis=-1, keepdims=True))

        @pl.when(i == nt - 1)
        def _():
            exchange.wait()
            for w in range(nj):
                pr = geometry[w][0]
                pltpu.make_async_copy(stages[w], window(w, c, 0, pr), j_local.at[w]).wait()
                to_sibling(w, stages[w], 1 - c, 0, pr).wait()

    def tok(cols):
        return pl.BlockSpec((tm, cols), lambda i, cc: (i, 0))

    def partial(w, off):
        pr, pc, rb, every, _ = geometry[w]
        return pl.BlockSpec((1, rb, pc), lambda i, cc: ((cc[1] + off) % N_CHIPS, i // every, 0))

    sum_specs, sum_operands = [], []
    for w, (_, _, own, arrived) in enumerate(joins):
        sum_specs += [partial(w, off) for off in range(N_CHIPS)]
        sum_operands += [own, arrived, arrived, arrived]
    dma = pltpu.SemaphoreType.DMA
    outs = pl.pallas_call(
        body, name="in_bwd",
        grid_spec=pltpu.PrefetchScalarGridSpec(
            num_scalar_prefetch=1, grid=(nt,),
            in_specs=[tok(IN_COLS), pl.BlockSpec(w_in_g.shape, lambda i, cc: (0, 0, 0)), tok(d), tok(d),
                      pl.BlockSpec((1, d), lambda i, cc: (0, 0))] + sum_specs + [ANY] * npart,
            out_specs=[tok(d), pl.BlockSpec((TILE_ROWS, d), lambda i, cc: (0, 0))] + [ANY] * (npart + nj),
            scratch_shapes=[pltpu.VMEM((g[0], g[1]), F32) for g in geometry]
            + [dma((npart, 3)), dma((npart, 3)), dma((nj,)), dma((nj,)), dma((nj,))]),
        out_shape=[jax.ShapeDtypeStruct((t, d), F32), jax.ShapeDtypeStruct((TILE_ROWS, d), F32)]
        + [jax.ShapeDtypeStruct(a.shape, a.dtype) for a in parts]
        + [jax.ShapeDtypeStruct(shape, F32) for _, shape, _, _ in joins],
        compiler_params=_params(dimension_semantics=("arbitrary",)),
    )(core_chip, dub, w_in_g, x, dx2, g1, *sum_operands, *parts)
    return outs[:2], outs[2:2 + npart], outs[2 + npart:]


WGRAD_GEOMETRY = {
    "in": (512, IN_SHARD, lambda s, h: h, lambda s, h: s),
    "mlp_in": (512, D_MODEL, lambda s, h: h, lambda s, h: s),
    "mlp_out": (512, D_MODEL, lambda s, h: 2 * s + h, lambda s, h: 0),
    "out": (384, 512, lambda s, h: s, lambda s, h: h),
}
K_CHUNK = 512


def _sibling():
    x, y, c = _position()
    return (x, y, 1 - c)


def _wgrad(a, b, tag, core_chip, pack=None):
    t = a.shape[0]
    pr, pc, a_blk, b_blk = WGRAD_GEOMETRY[tag]
    nk = t // K_CHUNK
    mine = N_CHIPS
    riding = pack is not None

    def body(cc_ref, a_ref, b_ref, *rest):
        if riding:
            pack_ref, land_ref, p_ref, pb_ref, all_ref, stage, rbuf, send_sems, recv_sems, rsem, g_send, g_recv, g_local = rest
            gather = _PackGather(pack_ref, all_ref, g_send, g_recv, g_local)
        else:
            land_ref, p_ref, pb_ref, stage, rbuf, send_sems, recv_sems, rsem = rest
        ph, s = pl.program_id(0), pl.program_id(1)
        if riding:
            @pl.when((ph == 0) & (s == 0))
            def _():
                gather.start()

            @pl.when((ph == 1) & (s == N_CHIPS - 2))
            def _():
                gather.hand_over()
        slot = jnp.where(ph == 0, s, mine)
        acc = stage.at[slot]
        acc[...] = _dot_tn(a_ref[0:K_CHUNK, :], b_ref[0:K_CHUNK, :])
        for k in range(1, nk):
            acc[...] += _dot_tn(a_ref[k * K_CHUNK:(k + 1) * K_CHUNK, :], b_ref[k * K_CHUNK:(k + 1) * K_CHUNK, :])

        def push(k):
            return pltpu.make_async_remote_copy(src_ref=stage.at[k], dst_ref=land_ref.at[k], send_sem=send_sems.at[k],
                                                recv_sem=recv_sems.at[k], device_id=_sibling(), device_id_type=MESH)

        @pl.when(ph == 0)
        def _():
            push(s).start()

        @pl.when(ph == 1)
        def _():
            push(s).wait_recv()
            landed = pltpu.make_async_copy(land_ref.at[s], rbuf, rsem)
            landed.start()
            landed.wait()
            p = stage[mine] + rbuf[...]
            p_ref[0] = p
            pb_ref[0] = p.astype(BF16)

        @pl.when((ph == 1) & (s == N_CHIPS - 1))
        def _():
            for k in range(N_CHIPS):
                push(k).wait_send()
            if riding:
                gather.finish()

    def half(ph, cc):
        return jnp.where(ph == 0, 1 - cc[0], cc[0])

    def out_slot(ph, s, cc):
        return (jnp.where(ph == 0, 0, s), 0, 0)

    piece = jax.ShapeDtypeStruct((N_CHIPS, pr, pc), F32)
    in_specs = [pl.BlockSpec((t, pr), lambda ph, s, cc: (0, a_blk(s, half(ph, cc)))),
                pl.BlockSpec((t, pc), lambda ph, s, cc: (0, b_blk(s, half(ph, cc))))]
    out_specs = [ANY, pl.BlockSpec((1, pr, pc), out_slot), pl.BlockSpec((1, pr, pc), out_slot)]
    out_shape = [piece, piece, jax.ShapeDtypeStruct((N_CHIPS, pr, pc), BF16)]
    scratch = [pltpu.VMEM((N_CHIPS + 1, pr, pc), F32), pltpu.VMEM((pr, pc), F32),
               pltpu.SemaphoreType.DMA((N_CHIPS,)), pltpu.SemaphoreType.DMA((N_CHIPS,)), pltpu.SemaphoreType.DMA]
    operands = [a, b]
    if riding:
        in_specs.append(pl.BlockSpec(pack.shape, lambda ph, s, cc: (0, 0)))
        out_specs.append(ANY)
        out_shape.append(jax.ShapeDtypeStruct((N_DEVICES,) + pack.shape, pack.dtype))
        scratch += _PackGather.semaphores()
        operands.append(pack)
    return pl.pallas_call(
        body, name="wgrad_" + tag,
        grid_spec=pltpu.PrefetchScalarGridSpec(
            num_scalar_prefetch=1, grid=(2, N_CHIPS), in_specs=in_specs, out_specs=out_specs, scratch_shapes=scratch),
        out_shape=out_shape,
        compiler_params=_params(dimension_semantics=("arbitrary", "arbitrary")),
    )(core_chip, *operands)[1:]


def _other_chips(x, y):
    return [(1 - x, y), (x, 1 - y), (1 - x, 1 - y)]


class _ShardGather:
    PAIRS = 9

    def __init__(self, outs, send_sems, recv_sems):
        self.outs, self.send_sems, self.recv_sems = outs, send_sems, recv_sems
        x, y, c = _position()
        self.c, self.j = c, 2 * x + y
        self.sibling = (x, y, 1 - c)
        self.chips = _other_chips(x, y)

    def _chip(self, k):
        px, py = self.chips[k]
        return 2 * px + py

    def _half(self, w, chip, which):
        hr = self.outs[w].shape[1] // 2
        return self.outs[w].at[chip, pl.ds(which * hr, hr), :]

    def _quarter(self, w, chip, q):
        qr = self.outs[w].shape[1] // 4
        return self.outs[w].at[chip, pl.ds(self.c * 2 * qr + q * qr, qr), :]

    def _copy(self, ref, w, pair, to, src=None):
        return pltpu.make_async_remote_copy(src_ref=ref if src is None else src, dst_ref=ref, send_sem=self.send_sems.at[w, pair],
                                            recv_sem=self.recv_sems.at[w, pair], device_id=to, device_id_type=MESH)

    def direct(self, w, k, q, src=None):
        return self._copy(self._quarter(w, self.j, q), w, 2 * k + q, (*self.chips[k], self.c), src)

    def direct_landed(self, w, k, q):
        return self._copy(self._quarter(w, self._chip(k), q), w, 2 * k + q, (*self.chips[k], self.c))

    def pass_on(self, w, q):
        return self._copy(self._quarter(w, self._chip(q), q), w, 4 + q, (*self.chips[1 - q], self.c))

    def passed_landed(self, w, q):
        return self._copy(self._quarter(w, self._chip(2), q), w, 4 + q, (*self.chips[1 - q], self.c))

    def hand_over(self, w, k):
        return self._copy(self._half(w, self._chip(k), self.c), w, 6 + k, self.sibling)

    def handed(self, w, k):
        return self._copy(self._half(w, self._chip(k), 1 - self.c), w, 6 + k, self.sibling)

    def start_direct(self, w, src_half=None):
        qr = self.outs[w].shape[1] // 4
        for k, q in ((0, 0), (1, 1), (0, 1), (1, 0)):
            self.direct(w, k, q, None if src_half is None else src_half.at[pl.ds(q * qr, qr), :]).start()

    def start_pass_on(self, w):
        for q in (0, 1):
            self.direct_landed(w, q, q).wait_recv()
            self.pass_on(w, q).start()

    def start_hand_over(self, w, diagonal):
        if diagonal:
            for q in (0, 1):
                self.passed_landed(w, q).wait_recv()
            self.hand_over(w, 2).start()
        else:
            for k in (0, 1):
                self.direct_landed(w, k, 1 - k).wait_recv()
                self.hand_over(w, k).start()

    def finish(self, w):
        for k in range(3):
            self.handed(w, k).wait_recv()
            self.hand_over(w, k).wait_send()
        for q in (0, 1):
            self.pass_on(w, q).wait_send()
            for k in (0, 1):
                self.direct(w, k, q).wait_send()


def _gather_first(w_in, w_out, w1, w2, small):
    bigs = (w_in, w_out, w1, w2)
    nb = len(bigs)

    def body(win_ref, wout_ref, w1_ref, w2_ref, sm_ref, gin, gout, g1, g2, gsm, st_in, st_out, st_1, st_2,
             send_sems, recv_sems, sm_send, sm_recv, local_sems):
        srcs = (win_ref, wout_ref, w1_ref, w2_ref)
        stages = (st_in, st_out, st_1, st_2)
        outs = (gin, gout, g1, g2)
        plan = _ShardGather(outs[:1], send_sems, recv_sems)
        j, c = plan.j, plan.c
        for src, st in zip(srcs, stages):
            st[...] = src[...].astype(BF16)
        local = [pltpu.make_async_copy(stages[w], outs[w].at[j], local_sems.at[w]) for w in range(nb)]
        local.append(pltpu.make_async_copy(sm_ref, gsm.at[j], local_sems.at[nb]))
        for cp in local:
            cp.start()

        def small_copy(k):
            px, py = plan.chips[k]
            return pltpu.make_async_remote_copy(src_ref=sm_ref, dst_ref=gsm.at[j], send_sem=sm_send.at[k],
                                                recv_sem=sm_recv.at[k], device_id=(px, py, c), device_id_type=MESH)

        def small_landed(k):
            px, py = plan.chips[k]
            return pltpu.make_async_remote_copy(src_ref=sm_ref, dst_ref=gsm.at[2 * px + py], send_sem=sm_send.at[k],
                                                recv_sem=sm_recv.at[k], device_id=(px, py, c), device_id_type=MESH)

        hr = w_in.shape[0] // 2
        plan.start_direct(0, st_in.at[pl.ds(c * hr, hr), :])
        for k in range(3):
            small_copy(k).start()
        plan.start_pass_on(0)
        plan.start_hand_over(0, diagonal=False)
        plan.start_hand_over(0, diagonal=True)
        for k in range(3):
            small_landed(k).wait_recv()
            small_copy(k).wait_send()
        plan.finish(0)
        for cp in local:
            cp.wait()

    def gathered(a, dtype):
        return jax.ShapeDtypeStruct((N_CHIPS,) + a.shape, dtype)

    return pl.pallas_call(
        body, name="gather_first",
        in_specs=[VMEM] * 5, out_specs=[ANY] * 5,
        out_shape=[gathered(a, BF16) for a in bigs] + [gathered(small, F32)],
        scratch_shapes=[pltpu.VMEM(a.shape, BF16) for a in bigs]
        + [pltpu.SemaphoreType.DMA((1, _ShardGather.PAIRS)), pltpu.SemaphoreType.DMA((1, _ShardGather.PAIRS)), pltpu.SemaphoreType.DMA((3,)),
           pltpu.SemaphoreType.DMA((3,)), pltpu.SemaphoreType.DMA((nb + 1,))],
        compiler_params=_params(),
    )(*bigs, small)


class _PartialExchange:
    def __init__(self, parts, arrived, send_sems, recv_sems):
        self.parts, self.arrived, self.send_sems, self.recv_sems = parts, arrived, send_sems, recv_sems
        x, y, c = _position()
        self.c, self.j = c, 2 * x + y
        self.chips = _other_chips(x, y)

    def _copy(self, w, k, slot):
        px, py = self.chips[k]
        return pltpu.make_async_remote_copy(
            src_ref=self.parts[w].at[2 * px + py], dst_ref=self.arrived[w].at[slot], send_sem=self.send_sems.at[w, k],
            recv_sem=self.recv_sems.at[w, k], device_id=(px, py, self.c), device_id_type=MESH)

    def start(self):
        for w in range(len(self.parts)):
            for k in range(3):
                self._copy(w, k, self.j).start()

    def wait(self):
        for w in range(len(self.parts)):
            for k in range(3):
                px, py = self.chips[k]
                self._copy(w, k, 2 * px + py).wait()


class _PackGather:
    def __init__(self, p_ref, all_ref, send_sems, recv_sems, local_sem):
        self.p_ref, self.all_ref, self.send_sems, self.recv_sems, self.local_sem = p_ref, all_ref, send_sems, recv_sems, local_sem
        x, y, c = _position()
        self.me, self.sibling, self.c = (x, y, c), (x, y, 1 - c), c
        self.chips = _other_chips(x, y)

    @staticmethod
    def semaphores():
        return [pltpu.SemaphoreType.DMA((7,)), pltpu.SemaphoreType.DMA((7,)), pltpu.SemaphoreType.DMA]

    def _copy(self, k, block, to, from_pack=False):
        px, py, pc = block
        slot = self.all_ref.at[4 * px + 2 * py + pc]
        return pltpu.make_async_remote_copy(src_ref=self.p_ref if from_pack else slot, dst_ref=slot, send_sem=self.send_sems.at[k],
                                            recv_sem=self.recv_sems.at[k], device_id=to, device_id_type=MESH)

    def _mine(self):
        x, y, c = self.me
        return pltpu.make_async_copy(self.p_ref, self.all_ref.at[4 * x + 2 * y + c], self.local_sem)

    def _first(self):
        return [self._copy(0, self.me, self.sibling, True)] + [
            self._copy(1 + k, self.me, (*chip, self.c), True) for k, chip in enumerate(self.chips)]

    def _passed(self):
        return [self._copy(4 + k, (*chip, self.c), self.sibling) for k, chip in enumerate(self.chips)]

    def start(self):
        self._mine().start()
        for cp in self._first():
            cp.start()

    def hand_over(self):
        for k, chip in enumerate(self.chips):
            self._copy(1 + k, (*chip, self.c), self.me).wait_recv()
            self._passed()[k].start()

    def finish(self):
        self._copy(0, self.sibling, self.me).wait_recv()
        for k, chip in enumerate(self.chips):
            self._copy(4 + k, (*chip, 1 - self.c), self.me).wait_recv()
        for cp in self._first() + self._passed():
            cp.wait_send()
        self._mine().wait()


class _DirectGather:
    def __init__(self, p_ref, all_ref, send_sems, recv_sems, local_sem):
        self.p_ref, self.all_ref, self.send_sems, self.recv_sems, self.local_sem = p_ref, all_ref, send_sems, recv_sems, local_sem
        self.me = _position()

    semaphores = _PackGather.semaphores

    def _peer(self, r):
        x, y, c = self.me
        return ((1 - x) if r & 4 else x, (1 - y) if r & 2 else y, (1 - c) if r & 1 else c)

    def _copy(self, r, slot_of):
        px, py, pc = slot_of
        return pltpu.make_async_remote_copy(src_ref=self.p_ref, dst_ref=self.all_ref.at[4 * px + 2 * py + pc],
                                            send_sem=self.send_sems.at[r - 1], recv_sem=self.recv_sems.at[r - 1],
                                            device_id=self._peer(r), device_id_type=MESH)

    def _mine(self):
        x, y, c = self.me
        return pltpu.make_async_copy(self.p_ref, self.all_ref.at[4 * x + 2 * y + c], self.local_sem)

    def start(self):
        self._mine().start()
        for r in range(1, N_DEVICES):
            self._copy(r, self.me).start()

    def finish(self):
        for r in range(1, N_DEVICES):
            self._copy(r, self._peer(r)).wait()
        self._mine().wait()


def _adamw(w, g, m, v):
    m = ADAM_B1 * m + (1.0 - ADAM_B1) * g
    v = ADAM_B2 * v + (1.0 - ADAM_B2) * (g * g)
    m_hat = m / ADAM_BC1
    v_hat = v / ADAM_BC2
    delta = -ADAM_LR * (m_hat / (jnp.sqrt(v_hat) + ADAM_EPS) + ADAM_WD * w)
    return delta, m, v


JOIN_SUB = 4


def _join(tag, shard_shape, part, arrived, core_chip, block=None):
    pr, pc = WGRAD_GEOMETRY[tag][:2]
    rb = pr // JOIN_SUB
    by_rows = shard_shape[1] == pc
    riding = block is not None

    def body(cc_ref, p_ref, r1_ref, r2_ref, r3_ref, *rest):
        if riding:
            blk_ref, g_ref, all_ref, stage, send_sems, recv_sems, local_sems, b_send, b_recv, b_local = rest
            gather = _DirectGather(blk_ref, all_ref, b_send, b_recv, b_local)
        else:
            g_ref, stage, send_sems, recv_sems, local_sems = rest
        i = pl.program_id(0)
        c = cc_ref[0]
        if riding:
            @pl.when(i == 0)
            def _():
                gather.start()

        def window(core, k):
            if by_rows:
                return g_ref.at[pl.ds((core * JOIN_SUB + k) * rb, rb), :]
            return g_ref.at[pl.ds(k * rb, rb), pl.ds(core * pc, pc)]

        def keep(k):
            return pltpu.make_async_copy(stage.at[k], window(c, k), local_sems.at[k])

        def push(k):
            return pltpu.make_async_remote_copy(src_ref=stage.at[k], dst_ref=window(c, k), send_sem=send_sems.at[k],
                                                recv_sem=recv_sems.at[k], device_id=_sibling(), device_id_type=MESH)

        def pushed(k):
            return pltpu.make_async_remote_copy(src_ref=stage.at[k], dst_ref=window(1 - c, k), send_sem=send_sems.at[k],
                                                recv_sem=recv_sems.at[k], device_id=_sibling(), device_id_type=MESH)

        stage[i] = ((p_ref[0] + r1_ref[0].astype(F32)) + r2_ref[0].astype(F32)) + r3_ref[0].astype(F32)
        keep(i).start()
        push(i).start()

        @pl.when(i == JOIN_SUB - 1)
        def _():
            for k in range(JOIN_SUB):
                keep(k).wait()
                push(k).wait_send()
                pushed(k).wait_recv()
            if riding:
                gather.finish()

    def partial(off):
        return pl.BlockSpec((1, rb, pc), lambda i, cc: ((cc[1] + off) % N_CHIPS, i, 0))

    in_specs = [partial(0), partial(1), partial(2), partial(3)]
    out_specs = [ANY]
    out_shape = [jax.ShapeDtypeStruct(shard_shape, F32)]
    scratch = [pltpu.VMEM((JOIN_SUB, rb, pc), F32), pltpu.SemaphoreType.DMA((JOIN_SUB,)),
               pltpu.SemaphoreType.DMA((JOIN_SUB,)), pltpu.SemaphoreType.DMA((JOIN_SUB,))]
    operands = [part, arrived, arrived, arrived]
    if riding:
        in_specs.append(pl.BlockSpec(block.shape, lambda i, cc: (0, 0)))
        out_specs.append(ANY)
        out_shape.append(jax.ShapeDtypeStruct((N_DEVICES,) + block.shape, block.dtype))
        scratch += _DirectGather.semaphores()
        operands.append(block)
    outs = pl.pallas_call(
        body, name="join_" + tag,
        grid_spec=pltpu.PrefetchScalarGridSpec(
            num_scalar_prefetch=1, grid=(JOIN_SUB,), in_specs=in_specs, out_specs=out_specs, scratch_shapes=scratch),
        out_shape=out_shape,
        compiler_params=_params(dimension_semantics=("arbitrary",)),
    )(core_chip, *operands)
    return outs if riding else outs[0]


def _adamw_big(w, g, m, v, name):
    rows, cols = w.shape
    rb = 256 if rows % 256 == 0 else rows

    def body(w_ref, g_ref, m_ref, v_ref, go_ref, d_ref, nm_ref, nv_ref):
        g = g_ref[...]
        go_ref[...] = g
        d_ref[...], nm_ref[...], nv_ref[...] = _adamw(w_ref[...], g, m_ref[...], v_ref[...])

    spec = pl.BlockSpec((rb, cols), lambda i: (i, 0))
    return pl.pallas_call(
        body, name=name, grid=(rows // rb,), in_specs=[spec] * 4, out_specs=[spec] * 4,
        out_shape=[jax.ShapeDtypeStruct(w.shape, F32)] * 4,
        compiler_params=_params(dimension_semantics=("arbitrary",)),
    )(w, g, m, v)


def _small_step(packs, mix_g_blocks, w_pack, m_pack, v_pack, conv_wmv, rconv_wmv):
    rows, cols = packs.shape[1:]
    cshard = conv_wmv.shape[2]
    rshard = rconv_wmv.shape[2]
    mix_row = PK_MIX_G * TILE_ROWS

    def body(all_ref, blk_ref, w_ref, m_ref, v_ref, cw_ref, rw_ref, g_ref, d_ref, nm_ref, nv_ref, co_ref, ro_ref):
        total = all_ref[0]
        late = blk_ref[0]
        for k in range(1, N_DEVICES):
            total = total + all_ref[k]
            late = late + blk_ref[k]
        g_ref[...] = total
        g_ref[mix_row:mix_row + TILE_ROWS, :] = late
        g = g_ref[...]
        d_ref[...], nm_ref[...], nv_ref[...] = _adamw(w_ref[...], g, m_ref[...], v_ref[...])

        x, y, _ = _position()
        j = 2 * x + y
        cblk = total[PK_CONV_W * 8:PK_CONV_W * 8 + 8, :]
        rblk = total[PK_RCONV_W * 8:PK_RCONV_W * 8 + 8, :]
        cg = cblk[:, 0:cshard]
        rg = rblk[:, 0:rshard]
        for k in range(1, N_CHIPS):
            cg = jnp.where(j == k, cblk[:, k * cshard:(k + 1) * cshard], cg)
            rg = jnp.where(j == k, rblk[:, k * rshard:(k + 1) * rshard], rg)
        co_ref[0] = cg
        co_ref[1], co_ref[2], co_ref[3] = _adamw(cw_ref[0], cg, cw_ref[1], cw_ref[2])
        ro_ref[0] = rg
        ro_ref[1], ro_ref[2], ro_ref[3] = _adamw(rw_ref[0], rg, rw_ref[1], rw_ref[2])

    pack = [jax.ShapeDtypeStruct((rows, cols), F32)] * 4
    return pl.pallas_call(
        body, name="small_grads_step", in_specs=[VMEM] * 7, out_specs=[VMEM] * 6,
        out_shape=pack + [jax.ShapeDtypeStruct((4, TILE_ROWS, cshard), F32), jax.ShapeDtypeStruct((4, TILE_ROWS, rshard), F32)],
        compiler_params=_params(),
    )(packs, mix_g_blocks, w_pack, m_pack, v_pack, conv_wmv, rconv_wmv)


def _blk(a):
    a = a.reshape(-1, a.shape[-1])
    return jnp.pad(a, ((0, TILE_ROWS - a.shape[0]), (0, D_MODEL - a.shape[1])))


def _zero_blk():
    return jnp.zeros((TILE_ROWS, D_MODEL), F32)


def _pack_params(p, pre):
    get = lambda n: p[pre + n]
    return jnp.concatenate([
        _blk(get("g_norm_rnn")), _blk(get("rnn_conv_b")), _blk(get("b_a")), _blk(get("b_x")), _blk(get("lru_lambda")),
        _zero_blk(), _zero_blk(), _blk(get("g_norm_conv")), _blk(get("final_norm_g").reshape(1, -1)), _blk(get("norm_mlp_g")),
        _zero_blk(), _blk(get("norm_mix_g")), get("w_a").reshape(64, D_MODEL), get("w_x").reshape(64, D_MODEL)], axis=0)


def _to_block_diag(w):
    w4 = w.reshape(N_BD, 4, 64, 64)
    eye = jnp.eye(4, dtype=w.dtype)
    return (w4[:, :, :, None, :] * eye[None, :, None, :, None]).reshape(N_BD, BD, BD)


def _from_block_diag(d):
    d5 = d.reshape(N_BD, 4, 64, 4, 64)
    return jnp.stack([d5[:, q, :, q, :] for q in range(4)], axis=1).reshape(64, D_MODEL)


def _pad_rows(a):
    return jnp.pad(a, ((0, TILE_ROWS - a.shape[0]), (0, 0)))


_NAMES = ['norm_mix_g', 'w_in', 'conv_w', 'rnn_conv_w', 'rnn_conv_b', 'w_a', 'b_a', 'w_x', 'b_x', 'lru_lambda',
          'g_norm_conv', 'g_norm_rnn', 'w_out', 'norm_mlp_g', 'w_mlp_in', 'w_mlp_out', 'final_norm_g']


def kernel(x, norm_mix_g, w_in, conv_w, rnn_conv_w, rnn_conv_b, w_a, b_a, w_x, b_x, lru_lambda, g_norm_conv, g_norm_rnn, w_out, norm_mlp_g, w_mlp_in, w_mlp_out, final_norm_g, loss_target, m_norm_mix_g, m_w_in, m_conv_w, m_rnn_conv_w, m_rnn_conv_b, m_w_a, m_b_a, m_w_x, m_b_x, m_lru_lambda, m_g_norm_conv, m_g_norm_rnn, m_w_out, m_norm_mlp_g, m_w_mlp_in, m_w_mlp_out, m_final_norm_g, v_norm_mix_g, v_w_in, v_conv_w, v_rnn_conv_w, v_rnn_conv_b, v_w_a, v_b_a, v_w_x, v_b_x, v_lru_lambda, v_g_norm_conv, v_g_norm_rnn, v_w_out, v_norm_mlp_g, v_w_mlp_in, v_w_mlp_out, v_final_norm_g):
    args = dict(locals())
    p = {}
    for n in _NAMES:
        for pre in ("", "m_", "v_"):
            a = args[pre + n]
            p[pre + n] = a[0] if a.ndim >= 3 else a
    xs = x[0]
    target = loss_target[0]
    core_chip = jnp.stack([lax.axis_index("c"), 2 * lax.axis_index("x") + lax.axis_index("y")]).astype(jnp.int32)
    cshard = p["conv_w"].shape[1]
    rshard = p["rnn_conv_w"].shape[1]

    small = jnp.concatenate([_pad_rows(p["conv_w"]), _pad_rows(p["rnn_conv_w"])], axis=1)
    w_in_g, w_out_g, w1_g, w2_g, small_g = _gather_first(p["w_in"], p["w_out"], p["w_mlp_in"], p["w_mlp_out"], small)
    conv_full = small_g[:, :3, :cshard].transpose(1, 0, 2).reshape(3, CONV_W)
    rconv_full = small_g[:, :4, cshard:].transpose(1, 0, 2).reshape(4, LRU_W)
    wa_bd = _to_block_diag(p["w_a"]).astype(BF16)
    wx_bd = _to_block_diag(p["w_x"]).astype(BF16)
    gf = p["final_norm_g"].reshape(1, -1)
    lru = (wa_bd, p["b_a"], wx_bd, p["b_x"], p["lru_lambda"], p["g_norm_conv"], p["g_norm_rnn"])

    (u, h1b, xr, hs, c3, yb, *gates), (w_out_g, w1_g, w2_g) = _fwd_mix(
        xs, p["norm_mix_g"], w_in_g, conv_full, rconv_full, p["rnn_conv_b"], *lru, (w_out_g, w1_g, w2_g))
    zb, dpb, h2b, dx3b, dx2, dx2b, dy, st_mlp = _mlp_fwd_bwd(
        xs, yb, w_out_g.reshape(-1, D_MODEL), w1_g, w2_g.reshape(-1, D_MODEL), p["norm_mlp_g"], gf, target)

    part_out = _wgrad(yb, dx2b, "out", core_chip)
    part_1 = _wgrad(h2b, dpb, "mlp_in", core_chip)
    part_2 = _wgrad(zb, dx3b, "mlp_out", core_chip)
    (dub, st_mix, dwa_bd, dwx_bd), arrived_mlp = _mix_bwd(
        dy, u, xr, hs, c3, gates, conv_full, rconv_full, wa_bd, wx_bd, p["lru_lambda"], p["g_norm_conv"], p["g_norm_rnn"],
        (part_out[1], part_1[1], part_2[1]))
    pack = jnp.concatenate([st_mix, st_mlp, _zero_blk(), _from_block_diag(dwa_bd), _from_block_diag(dwx_bd)], axis=0)
    *part_in, packs = _wgrad(h1b, dub, "in", core_chip, pack)
    early = (("w_out", "out", part_out, arrived_mlp[0]), ("w_mlp_in", "mlp_in", part_1, arrived_mlp[1]),
             ("w_mlp_out", "mlp_out", part_2, arrived_mlp[2]))
    (grad_x, st_in), arrived_in, joined = _in_bwd(
        dub, w_in_g, xs, dx2, p["norm_mix_g"], (part_in[1],),
        [(tag, p[n].shape, part[0], arrived) for n, tag, part, arrived in early], core_chip)
    g_in, mix_g_blocks = _join("in", p["w_in"].shape, part_in[0], arrived_in[0], core_chip, st_in)
    big = {}
    for n, tag, g in [(n, tag, g) for (n, tag, _, _), g in zip(early, joined)] + [("w_in", "in", g_in)]:
        big[n] = _adamw_big(p[n], g, p["m_" + n], p["v_" + n], "adamw_" + tag)

    conv_wmv = jnp.stack([_pad_rows(p[pre + "conv_w"]) for pre in ("", "m_", "v_")])
    rconv_wmv = jnp.stack([_pad_rows(p[pre + "rnn_conv_w"]) for pre in ("", "m_", "v_")])
    g_pack, d_pack, m_pack, v_pack, conv_out, rconv_out = _small_step(
        packs, mix_g_blocks, _pack_params(p, ""), _pack_params(p, "m_"), _pack_params(p, "v_"), conv_wmv, rconv_wmv)

    def unpack(pk, kind):
        def vec(b, width=D_MODEL):
            return pk[b * 8:b * 8 + 1, :width]
        return {
            "norm_mix_g": vec(PK_MIX_G), "rnn_conv_b": vec(PK_RCONV_B), "b_a": vec(PK_B_A), "b_x": vec(PK_B_X),
            "lru_lambda": vec(PK_LAMBDA), "g_norm_conv": vec(PK_G_NORM_CONV, CONV_W), "g_norm_rnn": vec(PK_G_NORM_RNN),
            "norm_mlp_g": vec(PK_MLP_G), "final_norm_g": vec(PK_FINAL_G).reshape(-1),
            "w_a": pk[PK_W_A * 8:PK_W_A * 8 + 64].reshape(1, 16, 64, 64), "w_x": pk[PK_W_X * 8:PK_W_X * 8 + 64].reshape(1, 16, 64, 64),
            "conv_w": conv_out[kind, :3][None], "rnn_conv_w": rconv_out[kind, :4][None],
            "w_in": big["w_in"][kind][None], "w_out": big["w_out"][kind][None],
            "w_mlp_in": big["w_mlp_in"][kind][None], "w_mlp_out": big["w_mlp_out"][kind][None],
        }

    outs = [unpack(pk, kind) for kind, pk in enumerate((g_pack, d_pack, m_pack, v_pack))]
    for o in outs:
        for n in ("norm_mix_g", "rnn_conv_b", "b_a", "b_x", "lru_lambda", "g_norm_conv", "g_norm_rnn", "norm_mlp_g"):
            o[n] = o[n].reshape(1, -1)
    loss = g_pack[PK_LOSS * 8, 0]
    return (loss, grad_x[None], *[o[n] for o in outs for n in _NAMES])
```

```python
import functools
import math

import jax
import jax.numpy as jnp
from jax import lax
from jax.experimental import pallas as pl
from jax.experimental.pallas import tpu as pltpu

F32 = jnp.float32
BF16 = jnp.bfloat16
MESH = pl.DeviceIdType.MESH
ANY = pl.BlockSpec(memory_space=pl.ANY)
VMEM = pl.BlockSpec(memory_space=pltpu.VMEM)

EPS = 1e-6
LRU_C = 8.0
D_MODEL = 1024
CONV_W = 512
LRU_W = 1024
IN_COLS = 3 * CONV_W + 2 * LRU_W
IN_SHARD = IN_COLS // 4
N_CHIPS = 4
N_DEVICES = 8
BD = 256
N_BD = LRU_W // BD

ADAM_LR = 0.001
ADAM_B1 = 0.9
ADAM_B2 = 0.999
ADAM_EPS = 1e-08
ADAM_WD = 0.01
ADAM_STEP = 10
ADAM_BC1 = 1.0 - ADAM_B1 ** ADAM_STEP
ADAM_BC2 = 1.0 - ADAM_B2 ** ADAM_STEP

TILE_ROWS = 8
TOKEN_TILE = 256
VMEM_LIMIT = 56 * 1024 * 1024

PK_G_NORM_RNN, PK_RCONV_B, PK_B_A, PK_B_X, PK_LAMBDA, PK_RCONV_W, PK_CONV_W, PK_G_NORM_CONV = range(8)
PK_FINAL_G, PK_MLP_G, PK_LOSS, PK_MIX_G = 8, 9, 10, 11
PK_W_A = 12
PK_W_X = 20
PK_BLOCKS = 28
PK_ROWS = PK_BLOCKS * TILE_ROWS


def _params(**kw):
    return pltpu.CompilerParams(vmem_limit_bytes=VMEM_LIMIT, **kw)


def _position():
    x, y, c = lax.axis_index("x"), lax.axis_index("y"), lax.axis_index("c")
    return x, y, c


def _sigmoid(v):
    return 1.0 / (1.0 + jnp.exp(-v))


def _one_minus_square(log_a, a):
    v = 2.0 * log_a
    series = -v * (1.0 + v * (0.5 + v * (1.0 / 6.0)))
    return jnp.where(v > -0.01, series, 1.0 - a * a)


_GELU_C = math.sqrt(2.0 / math.pi)
_GELU_K = 0.044715


def _gelu_and_grad(g):
    th = jnp.tanh(_GELU_C * (g + _GELU_K * g * g * g))
    gelu = 0.5 * g * (1.0 + th)
    dgelu = 0.5 * (1.0 + th) + 0.5 * g * (1.0 - th * th) * (_GELU_C * (1.0 + 3.0 * _GELU_K * g * g))
    return gelu, dgelu


def _rows(shape):
    return lax.broadcasted_iota(jnp.int32, shape, 0)


def _shift_down(v, k, prev8):
    rolled = pltpu.roll(v, k, 0)
    halo = pltpu.roll(prev8, k, 0)
    head = jnp.where(_rows(halo.shape) < k, halo, rolled[:TILE_ROWS])
    return jnp.concatenate([head, rolled[TILE_ROWS:]], axis=0)


def _shift_up(v, k, next8):
    n = v.shape[0]
    rolled = pltpu.roll(v, n - k, 0)
    halo = pltpu.roll(next8, TILE_ROWS - k, 0)
    tail = jnp.where(_rows(halo.shape) >= TILE_ROWS - k, halo, rolled[n - TILE_ROWS:])
    return jnp.concatenate([rolled[: n - TILE_ROWS], tail], axis=0)


def _scan_rows(a, b, carry, reverse=False):
    n, w = a.shape
    groups = n // TILE_ROWS
    a3 = a.reshape(groups, TILE_ROWS, w)
    b3 = b.reshape(groups, TILE_ROWS, w)
    sub = lax.broadcasted_iota(jnp.int32, a3.shape, 1)
    s = 1
    while s < TILE_ROWS:
        shift = TILE_ROWS - s if reverse else s
        keep = (sub < TILE_ROWS - s) if reverse else (sub >= s)
        b3 = b3 + jnp.where(keep, a3 * pltpu.roll(b3, shift, 1), 0.0)
        a3 = a3 * jnp.where(keep, pltpu.roll(a3, shift, 1), 1.0)
        s *= 2
    out = [None] * groups
    edge = 0 if reverse else TILE_ROWS - 1
    for g in (range(groups - 1, -1, -1) if reverse else range(groups)):
        out[g] = b3[g] + a3[g] * carry
        carry = out[g][edge:edge + 1]
    return jnp.concatenate(out, axis=0)


def _softplus_neg(lam):
    e = jnp.exp(-jnp.abs(lam))
    log1p_e = jnp.where(e < 1e-2, e * (1.0 - e * (0.5 - e * (1.0 / 3.0 - e * 0.25))), jnp.log(1.0 + e))
    sp = jnp.maximum(-lam, 0.0) + log1p_e
    dsp = -_sigmoid(-lam)
    return sp, dsp


def _block_diag_dot(vb, w_ref):
    return jnp.concatenate(
        [jnp.dot(vb[:, j * BD:(j + 1) * BD], w_ref[j], preferred_element_type=F32) for j in range(N_BD)], axis=1)


def _block_diag_dot_t(vb, w_ref):
    return jnp.concatenate(
        [lax.dot_general(vb[:, j * BD:(j + 1) * BD], w_ref[j], (((1,), (1,)), ((), ())), preferred_element_type=F32)
         for j in range(N_BD)], axis=1)


def _dot_nt(a, b):
    return lax.dot_general(a, b, (((1,), (1,)), ((), ())), preferred_element_type=F32)


def _dot_tn(a, b):
    return lax.dot_general(a, b, (((0,), (0,)), ((), ())), preferred_element_type=F32)


def _lru_gates(xr, wa_ref, ba, wx_ref, bx, sp):
    xrb = xr.astype(BF16)
    r = _sigmoid(_block_diag_dot(xrb, wa_ref) + ba)
    ig = _sigmoid(_block_diag_dot(xrb, wx_ref) + bx)
    log_a = (-LRU_C) * r * sp
    a = jnp.exp(log_a)
    mult = jnp.sqrt(_one_minus_square(log_a, a))
    return r, ig, a, mult


def _colsum(v):
    return jnp.sum(v, axis=0, keepdims=True)


N_FWD_OUT = 10


def _fwd_mix(x, g1, w_in_g, conv_w, rconv_w, rconv_b, wa_bd, b_a, wx_bd, b_x, lam, g_nc, g_nr, later):
    t, d = x.shape
    tm = TOKEN_TILE
    nt = t // tm
    nl = len(later)
    assert nl == 3
    pass_on_at = [nt * f // 16 for f in (1, 3, 7)]
    neighbours_at = [nt * f // 16 for f in (3, 6, 10)]
    diagonal_at = [nt * f // 16 for f in (10, 12, 14)]

    def body(x_ref, g1_ref, win_ref, cw_ref, rw_ref, rb_ref, wa_ref, ba_ref, wx_ref, bx_ref, lam_ref, gnc_ref, gnr_ref,
             *rest):
        later_in, outs, rest = rest[:nl], rest[nl:nl + N_FWD_OUT], rest[nl + N_FWD_OUT:]
        u_ref, h1_ref, xr_ref, hs_ref, c3_ref, y_ref, r_ref, ig_ref, a_ref, mult_ref = outs
        later_out, (cv_prev, xin_prev, h_prev, send_sems, recv_sems) = rest[:nl], rest[nl:]
        del later_in
        step = pl.program_id(0)
        plan = _ShardGather(later_out, send_sems, recv_sems)

        @pl.when(step == 0)
        def _():
            cv_prev[...] = jnp.zeros_like(cv_prev)
            xin_prev[...] = jnp.zeros_like(xin_prev)
            h_prev[...] = jnp.zeros_like(h_prev)
            for w in range(nl):
                plan.start_direct(w)

        for w in range(nl):
            @pl.when(step == pass_on_at[w])
            def _(w=w):
                plan.start_pass_on(w)

            @pl.when(step == neighbours_at[w])
            def _(w=w):
                plan.start_hand_over(w, diagonal=False)

            @pl.when(step == diagonal_at[w])
            def _(w=w):
                plan.start_hand_over(w, diagonal=True)

        xv = x_ref[...]
        rstd = lax.rsqrt(jnp.mean(xv * xv, axis=-1, keepdims=True) + EPS)
        h1b = ((xv * rstd) * g1_ref[...]).astype(BF16)
        h1_ref[...] = h1b
        for j in range(N_CHIPS):
            u_ref[:, j * IN_SHARD:(j + 1) * IN_SHARD] = jnp.dot(h1b, win_ref[j], preferred_element_type=F32)
        gate_b = u_ref[:, 0:CONV_W]
        cv = u_ref[:, CONV_W:2 * CONV_W] * u_ref[:, 2 * CONV_W:3 * CONV_W]
        x_r = u_ref[:, 3 * CONV_W:3 * CONV_W + LRU_W]
        g = u_ref[:, 3 * CONV_W + LRU_W:]

        cw = cw_ref[...]
        cvp = cv_prev[...]
        conv3 = cw[0:1] * _shift_down(cv, 2, cvp) + cw[1:2] * _shift_down(cv, 1, cvp) + cw[2:3] * cv
        cv_prev[...] = cv[tm - TILE_ROWS:]
        c3_ref[...] = conv3
        y_conv = gate_b * conv3

        rw = rw_ref[...]
        xp = xin_prev[...]
        xr = (rw[0:1] * _shift_down(x_r, 3, xp) + rw[1:2] * _shift_down(x_r, 2, xp)
              + rw[2:3] * _shift_down(x_r, 1, xp) + rw[3:4] * x_r) + rb_ref[...]
        xin_prev[...] = x_r[tm - TILE_ROWS:]
        xr_ref[...] = xr
        sp, _ = _softplus_neg(lam_ref[...])
        r, ig, a, mult = _lru_gates(xr, wa_ref, ba_ref[...], wx_ref, bx_ref[...], sp)
        r_ref[...] = r
        ig_ref[...] = ig
        a_ref[...] = a
        mult_ref[...] = mult
        h = _scan_rows(a, mult * (ig * xr), h_prev[...])
        h_prev[...] = h[tm - 1:tm]
        hs_ref[...] = h
        gelu, _ = _gelu_and_grad(g)
        y_rnn = h * gelu

        na = y_conv * lax.rsqrt(jnp.mean(y_conv * y_conv, axis=-1, keepdims=True) + EPS) * gnc_ref[...]
        nb = y_rnn * lax.rsqrt(jnp.mean(y_rnn * y_rnn, axis=-1, keepdims=True) + EPS) * gnr_ref[...]
        y_ref[:, :CONV_W] = na.astype(BF16)
        y_ref[:, CONV_W:] = nb.astype(BF16)

        @pl.when(step == nt - 1)
        def _():
            for w in range(nl):
                plan.finish(w)

    def full(a):
        nd = a.ndim
        return pl.BlockSpec(a.shape, lambda i: (0,) * nd)

    def tok(cols):
        return pl.BlockSpec((tm, cols), lambda i: (i, 0))

    def act(cols, dtype=F32):
        return jax.ShapeDtypeStruct((t, cols), dtype)

    smalls = (g1, w_in_g, conv_w, rconv_w, rconv_b, wa_bd, b_a, wx_bd, b_x, lam, g_nc, g_nr)
    n_in = 1 + len(smalls)
    outs = pl.pallas_call(
        body, name="fwd_mix", grid=(nt,),
        in_specs=[tok(d)] + [full(a) for a in smalls] + [ANY] * nl,
        out_specs=[tok(IN_COLS), tok(d), tok(LRU_W), tok(LRU_W), tok(CONV_W), tok(CONV_W + LRU_W)]
        + [tok(LRU_W)] * 4 + [ANY] * nl,
        out_shape=[act(IN_COLS), act(d, BF16), act(LRU_W), act(LRU_W), act(CONV_W), act(CONV_W + LRU_W, BF16)]
        + [act(LRU_W)] * 4 + [jax.ShapeDtypeStruct(a.shape, a.dtype) for a in later],
        input_output_aliases={n_in + w: N_FWD_OUT + w for w in range(nl)},
        scratch_shapes=[pltpu.VMEM((TILE_ROWS, CONV_W), F32), pltpu.VMEM((TILE_ROWS, LRU_W), F32),
                        pltpu.VMEM((1, LRU_W), F32), pltpu.SemaphoreType.DMA((nl, _ShardGather.PAIRS)),
                        pltpu.SemaphoreType.DMA((nl, _ShardGather.PAIRS))],
        compiler_params=_params(dimension_semantics=("arbitrary",)),
    )(x, *smalls, *later)
    return outs[:N_FWD_OUT], outs[N_FWD_OUT:]


def _mlp_fwd_bwd(x, yb, w_out_g, w1_g, w2_g, g2, gf, target):
    t, d = x.shape
    tm = TOKEN_TILE
    ff = w2_g.shape[0]
    mix = w_out_g.shape[0]
    ffs = ff // N_CHIPS

    def body(x_ref, y_ref, g2_ref, gf_ref, tgt_ref, wout_hbm, w1_hbm, w2_hbm,
             z_ref, dp_ref, h2_ref, dx3b_ref, dx2_ref, dx2b_ref, dy_ref, st_ref, wout, w1, w2, p_ref):
        @pl.when(pl.program_id(0) == 0)
        def _():
            pltpu.sync_copy(wout_hbm, wout)
            pltpu.sync_copy(w1_hbm, w1)
            pltpu.sync_copy(w2_hbm, w2)
            st_ref[...] = jnp.zeros_like(st_ref)

        x2 = x_ref[...] + jnp.dot(y_ref[...], wout[...], preferred_element_type=F32)
        r2 = lax.rsqrt(jnp.mean(x2 * x2, axis=-1, keepdims=True) + EPS)
        xh2 = x2 * r2
        g2v = g2_ref[...]
        h2b = (xh2 * g2v).astype(BF16)
        h2_ref[...] = h2b
        for j in range(N_CHIPS):
            p_ref[:, j * ffs:(j + 1) * ffs] = jnp.dot(h2b, w1[j], preferred_element_type=F32)
        rp = jnp.maximum(p_ref[...], 0.0)
        zb = (rp * rp).astype(BF16)
        z_ref[...] = zb
        x3 = x2 + jnp.dot(zb, w2[...], preferred_element_type=F32)
        r3 = lax.rsqrt(jnp.mean(x3 * x3, axis=-1, keepdims=True) + EPS)
        xh3 = x3 * r3
        gfv = gf_ref[...]
        err = xh3 * gfv - tgt_ref[...]
        loss = (0.5 / d) * jnp.sum(err * err)
        dout = err * (1.0 / d)
        st_ref[PK_FINAL_G * 8 - 64:PK_FINAL_G * 8 - 63, :] += _colsum(dout * xh3)
        st_ref[PK_LOSS * 8 - 64:PK_LOSS * 8 - 63, :] += jnp.zeros((1, d), F32) + loss
        dxh3 = dout * gfv
        dx3 = r3 * (dxh3 - xh3 * jnp.mean(dxh3 * xh3, axis=-1, keepdims=True))
        dx3b = dx3.astype(BF16)
        dx3b_ref[...] = dx3b
        dpb = (_dot_nt(dx3b, w2[...]) * (2.0 * rp)).astype(BF16)
        dp_ref[...] = dpb
        dh2 = _dot_nt(dpb[:, 0:ffs], w1[0])
        for j in range(1, N_CHIPS):
            dh2 = dh2 + _dot_nt(dpb[:, j * ffs:(j + 1) * ffs], w1[j])
        st_ref[PK_MLP_G * 8 - 64:PK_MLP_G * 8 - 63, :] += _colsum(dh2 * xh2)
        dxh2 = dh2 * g2v
        dx2 = dx3 + r2 * (dxh2 - xh2 * jnp.mean(dxh2 * xh2, axis=-1, keepdims=True))
        dx2_ref[...] = dx2
        dx2b = dx2.astype(BF16)
        dx2b_ref[...] = dx2b
        dy_ref[...] = _dot_nt(dx2b, wout[...])

    def tok(cols):
        return pl.BlockSpec((tm, cols), lambda i: (i, 0))

    def row(cols):
        return pl.BlockSpec((1, cols), lambda i: (0, 0))

    return pl.pallas_call(
        body, name="mlp_fwd_bwd", grid=(t // tm,),
        in_specs=[tok(d), tok(mix), row(d), row(d), tok(d), ANY, ANY, ANY],
        out_specs=[tok(ff), tok(ff), tok(d), tok(d), tok(d), tok(d), tok(mix),
                   pl.BlockSpec((3 * TILE_ROWS, d), lambda i: (0, 0))],
        out_shape=[jax.ShapeDtypeStruct((t, ff), BF16), jax.ShapeDtypeStruct((t, ff), BF16),
                   jax.ShapeDtypeStruct((t, d), BF16), jax.ShapeDtypeStruct((t, d), BF16),
                   jax.ShapeDtypeStruct((t, d), F32), jax.ShapeDtypeStruct((t, d), BF16),
                   jax.ShapeDtypeStruct((t, mix), F32), jax.ShapeDtypeStruct((3 * TILE_ROWS, d), F32)],
        scratch_shapes=[pltpu.VMEM(w_out_g.shape, BF16), pltpu.VMEM(w1_g.shape, BF16), pltpu.VMEM(w2_g.shape, BF16),
                        pltpu.VMEM((tm, ff), F32)],
        compiler_params=_params(dimension_semantics=("arbitrary",)),
    )(x, yb, g2, gf, target, w_out_g, w1_g, w2_g)


def _mix_bwd(dy, u, xr_all, hs_all, c3_all, gates, conv_w, rconv_w, wa_bd, wx_bd, lam, g_nc, g_nr, parts):
    t = dy.shape[0]
    tm = TOKEN_TILE
    nt = t // tm
    hb = tm // TILE_ROWS
    npart = len(parts)

    def body(dy_ref, u_ref, uh_ref, xr_ref, hs_ref, hh_ref, c3_ref, r_ref, ig_ref, a_ref, mult_ref,
             cw_ref, rw_ref, wa_ref, wx_ref, lam_ref, gnc_ref, gnr_ref, *rest):
        part_refs, (du_ref, st_ref, dwa_ref, dwx_ref), rest = rest[:npart], rest[npart:npart + 4], rest[npart + 4:]
        arrived_refs, (dc_next, a_next, gs_next, dxr_next, send_sems, recv_sems) = rest[:npart], rest[npart:]
        exchange = _PartialExchange(part_refs, arrived_refs, send_sems, recv_sems)
        i = pl.program_id(0)

        @pl.when(i == 0)
        def _():
            exchange.start()
            dc_next[...] = jnp.zeros_like(dc_next)
            a_next[...] = jnp.zeros_like(a_next)
            gs_next[...] = jnp.zeros_like(gs_next)
            dxr_next[...] = jnp.zeros_like(dxr_next)
            st_ref[...] = jnp.zeros_like(st_ref)
            dwa_ref[...] = jnp.zeros_like(dwa_ref)
            dwx_ref[...] = jnp.zeros_like(dwx_ref)

        first_tile = i == nt - 1
        gate_b = u_ref[:, 0:CONV_W]
        gate_c = u_ref[:, CONV_W:2 * CONV_W]
        v = u_ref[:, 2 * CONV_W:3 * CONV_W]
        x_r = u_ref[:, 3 * CONV_W:3 * CONV_W + LRU_W]
        g = u_ref[:, 3 * CONV_W + LRU_W:]
        cv = gate_c * v
        cv_prev = jnp.where(first_tile, 0.0, uh_ref[:, CONV_W:2 * CONV_W] * uh_ref[:, 2 * CONV_W:3 * CONV_W])
        xin_prev = jnp.where(first_tile, 0.0, uh_ref[:, 3 * CONV_W:3 * CONV_W + LRU_W])
        hs_prev = jnp.where(first_tile, 0.0, hh_ref[...])

        def acc(block, val, width=LRU_W, row=0):
            r0 = block * TILE_ROWS + row
            st_ref[r0:r0 + 1, 0:width] += val

        conv3 = c3_ref[...]
        y_conv = gate_b * conv3
        ra = lax.rsqrt(jnp.mean(y_conv * y_conv, axis=-1, keepdims=True) + EPS)
        xha = y_conv * ra
        dna = dy_ref[:, :CONV_W]
        acc(PK_G_NORM_CONV, _colsum(dna * xha), CONV_W)
        dxha = dna * gnc_ref[...]
        dy_conv = ra * (dxha - xha * jnp.mean(dxha * xha, axis=-1, keepdims=True))
        du_ref[:, 0:CONV_W] = (dy_conv * conv3).astype(BF16)
        dc = dy_conv * gate_b
        cw = cw_ref[...]
        dcn = dc_next[...]
        dcv = cw[2:3] * dc + cw[1:2] * _shift_up(dc, 1, dcn) + cw[0:1] * _shift_up(dc, 2, dcn)
        dc_next[...] = dc[:TILE_ROWS]
        acc(PK_CONV_W, _colsum(dc * _shift_down(cv, 2, cv_prev)), CONV_W, 0)
        acc(PK_CONV_W, _colsum(dc * _shift_down(cv, 1, cv_prev)), CONV_W, 1)
        acc(PK_CONV_W, _colsum(dc * cv), CONV_W, 2)
        du_ref[:, CONV_W:2 * CONV_W] = (dcv * v).astype(BF16)
        du_ref[:, 2 * CONV_W:3 * CONV_W] = (dcv * gate_c).astype(BF16)

        hs = hs_ref[...]
        gelu, dgelu = _gelu_and_grad(g)
        y_rnn = hs * gelu
        rb = lax.rsqrt(jnp.mean(y_rnn * y_rnn, axis=-1, keepdims=True) + EPS)
        xhb = y_rnn * rb
        dnb = dy_ref[:, CONV_W:]
        acc(PK_G_NORM_RNN, _colsum(dnb * xhb))
        dxhb = dnb * gnr_ref[...]
        dy_rnn = rb * (dxhb - xhb * jnp.mean(dxhb * xhb, axis=-1, keepdims=True))
        du_ref[:, 3 * CONV_W + LRU_W:] = (dy_rnn * hs * dgelu).astype(BF16)
        dh = dy_rnn * gelu

        xr = xr_ref[...]
        xrb = xr.astype(BF16)
        sp, dsp = _softplus_neg(lam_ref[...])
        r, ig, a, mult = r_ref[...], ig_ref[...], a_ref[...], mult_ref[...]
        a_up = _shift_up(a, 1, a_next[...])
        a_next[...] = a[:TILE_ROWS]
        gs = _scan_rows(a_up, dh, gs_next[0:1, :], reverse=True)
        gs_next[...] = gs[:TILE_ROWS]
        da = gs * _shift_down(hs, 1, hs_prev)
        gx = gs * xr
        di = gx * mult
        dmult = gx * ig
        dxr = gs * (mult * ig)
        dlog_a = da * a - dmult * ((a * a) / mult)
        acc(PK_LAMBDA, _colsum(dlog_a * r) * ((-LRU_C) * dsp))
        dpa = (dlog_a * ((-LRU_C) * sp)) * (r * (1.0 - r))
        dpx = di * (ig * (1.0 - ig))
        acc(PK_B_A, _colsum(dpa))
        acc(PK_B_X, _colsum(dpx))
        dpab = dpa.astype(BF16)
        dpxb = dpx.astype(BF16)
        dxr = dxr + _block_diag_dot_t(dpab, wa_ref) + _block_diag_dot_t(dpxb, wx_ref)
        for j in range(N_BD):
            cols = slice(j * BD, (j + 1) * BD)
            dwa_ref[j] += _dot_tn(xrb[:, cols], dpab[:, cols])
            dwx_ref[j] += _dot_tn(xrb[:, cols], dpxb[:, cols])

        acc(PK_RCONV_B, _colsum(dxr))
        rw = rw_ref[...]
        dxn = dxr_next[...]
        dx_r = (rw[3:4] * dxr + rw[2:3] * _shift_up(dxr, 1, dxn) + rw[1:2] * _shift_up(dxr, 2, dxn)
                + rw[0:1] * _shift_up(dxr, 3, dxn))
        dxr_next[...] = dxr[:TILE_ROWS]
        for k in range(3):
            acc(PK_RCONV_W, _colsum(dxr * _shift_down(x_r, 3 - k, xin_prev)), LRU_W, k)
        acc(PK_RCONV_W, _colsum(dxr * x_r), LRU_W, 3)
        du_ref[:, 3 * CONV_W:3 * CONV_W + LRU_W] = dx_r.astype(BF16)

        @pl.when(i == nt - 1)
        def _():
            exchange.wait()

    def full(a):
        nd = a.ndim
        return pl.BlockSpec(a.shape, lambda i: (0,) * nd)

    def tok(cols):
        return pl.BlockSpec((tm, cols), lambda i: (nt - 1 - i, 0))

    def halo(cols):
        return pl.BlockSpec((TILE_ROWS, cols), lambda i: (jnp.maximum((nt - 1 - i) * hb - 1, 0), 0))

    smalls = (conv_w, rconv_w, wa_bd, wx_bd, lam, g_nc, g_nr)
    outs = pl.pallas_call(
        body, name="mix_bwd", grid=(nt,),
        in_specs=[tok(CONV_W + LRU_W), tok(IN_COLS), halo(IN_COLS), tok(LRU_W), tok(LRU_W), halo(LRU_W), tok(CONV_W)]
        + [tok(LRU_W)] * 4 + [full(a) for a in smalls] + [ANY] * npart,
        out_specs=[tok(IN_COLS), pl.BlockSpec((8 * TILE_ROWS, LRU_W), lambda i: (0, 0)),
                   pl.BlockSpec((N_BD, BD, BD), lambda i: (0, 0, 0)), pl.BlockSpec((N_BD, BD, BD), lambda i: (0, 0, 0))]
        + [ANY] * npart,
        out_shape=[jax.ShapeDtypeStruct((t, IN_COLS), BF16), jax.ShapeDtypeStruct((8 * TILE_ROWS, LRU_W), F32),
                   jax.ShapeDtypeStruct((N_BD, BD, BD), F32), jax.ShapeDtypeStruct((N_BD, BD, BD), F32)]
        + [jax.ShapeDtypeStruct(a.shape, a.dtype) for a in parts],
        scratch_shapes=[pltpu.VMEM((TILE_ROWS, CONV_W), F32), pltpu.VMEM((TILE_ROWS, LRU_W), F32),
                        pltpu.VMEM((TILE_ROWS, LRU_W), F32), pltpu.VMEM((TILE_ROWS, LRU_W), F32),
                        pltpu.SemaphoreType.DMA((npart, 3)), pltpu.SemaphoreType.DMA((npart, 3))],
        compiler_params=_params(dimension_semantics=("arbitrary",)),
    )(dy, u, u, xr_all, hs_all, hs_all, c3_all, *gates, *smalls, *parts)
    return outs[:4], outs[4:]


def _in_bwd(dub, w_in_g, x, dx2, g1, parts, joins, core_chip):
    t, d = x.shape
    tm = TOKEN_TILE
    nt = t // tm
    npart = len(parts)
    nj = len(joins)
    geometry = []
    for tag, shape, _, _ in joins:
        pr, pc = WGRAD_GEOMETRY[tag][:2]
        every = 1 if pr % (nt * 16) == 0 else 2
        geometry.append((pr, pc, pr * every // nt, every, shape[1] == pc))

    def body(cc_ref, du_ref, win_ref, x_ref, dx2_ref, g1_ref, *rest):
        sums, rest = [rest[4 * w:4 * w + 4] for w in range(nj)], rest[4 * nj:]
        part_refs, (gx_ref, st_ref), rest = rest[:npart], rest[npart:npart + 2], rest[npart + 2:]
        arrived_refs, joined, rest = rest[:npart], rest[npart:npart + nj], rest[npart + nj:]
        stages, (send_sems, recv_sems, j_local, j_send, j_recv) = rest[:nj], rest[nj:]
        exchange = _PartialExchange(part_refs, arrived_refs, send_sems, recv_sems)
        i = pl.program_id(0)
        c = cc_ref[0]

        def window(w, core, row0, rows):
            pr, pc, _, _, by_rows = geometry[w]
            if by_rows:
                return joined[w].at[pl.ds(core * pr + row0, rows), :]
            return joined[w].at[pl.ds(row0, rows), pl.ds(core * pc, pc)]

        def to_sibling(w, src, core, row0, rows):
            return pltpu.make_async_remote_copy(src_ref=src, dst_ref=window(w, core, row0, rows), send_sem=j_send.at[w],
                                                recv_sem=j_recv.at[w], device_id=_sibling(), device_id_type=MESH)

        @pl.when(i == 0)
        def _():
            exchange.start()
            st_ref[...] = jnp.zeros_like(st_ref)

        for w in range(nj):
            pr, pc, rb, every, _ = geometry[w]

            @pl.when(i % every == 0)
            def _(w=w, rb=rb, every=every):
                p_ref, r1_ref, r2_ref, r3_ref = sums[w]
                row0 = pl.multiple_of((i // every) * rb, rb)
                rows = stages[w].at[pl.ds(row0, rb), :]
                rows[...] = ((p_ref[0] + r1_ref[0].astype(F32)) + r2_ref[0].astype(F32)) + r3_ref[0].astype(F32)
                pltpu.make_async_copy(rows, window(w, c, row0, rb), j_local.at[w]).start()
                to_sibling(w, rows, c, row0, rb).start()

        dh1 = _dot_nt(du_ref[:, 0:IN_SHARD], win_ref[0])
        for j in range(1, N_CHIPS):
            dh1 = dh1 + _dot_nt(du_ref[:, j * IN_SHARD:(j + 1) * IN_SHARD], win_ref[j])
        xv = x_ref[...]
        rstd = lax.rsqrt(jnp.mean(xv * xv, axis=-1, keepdims=True) + EPS)
        xh = xv * rstd
        st_ref[0:1, :] += _colsum(dh1 * xh)
        dxh = dh1 * g1_ref[...]
        gx_ref[...] = dx2_ref[...] + rstd * (dxh - xh * jnp.mean(dxh * xh, axis=-1, keepdims=True))

        @pl.when(i == nt - 1)
        def _():
            exchange.wait()
            for w in range(nj):
                pr = geometry[w][0]
                pltpu.make_async_copy(stages[w], window(w, c, 0, pr), j_local.at[w]).wait()
                to_sibling(w, stages[w], 1 - c, 0, pr).wait()

    def tok(cols):
        return pl.BlockSpec((tm, cols), lambda i, cc: (i, 0))

    def partial(w, off):
        pr, pc, rb, every, _ = geometry[w]
        return pl.BlockSpec((1, rb, pc), lambda i, cc: ((cc[1] + off) % N_CHIPS, i // every, 0))

    sum_specs, sum_operands = [], []
    for w, (_, _, own, arrived) in enumerate(joins):
        sum_specs += [partial(w, off) for off in range(N_CHIPS)]
        sum_operands += [own, arrived, arrived, arrived]
    dma = pltpu.SemaphoreType.DMA
    outs = pl.pallas_call(
        body, name="in_bwd",
        grid_spec=pltpu.PrefetchScalarGridSpec(
            num_scalar_prefetch=1, grid=(nt,),
            in_specs=[tok(IN_COLS), pl.BlockSpec(w_in_g.shape, lambda i, cc: (0, 0, 0)), tok(d), tok(d),
                      pl.BlockSpec((1, d), lambda i, cc: (0, 0))] + sum_specs + [ANY] * npart,
            out_specs=[tok(d), pl.BlockSpec((TILE_ROWS, d), lambda i, cc: (0, 0))] + [ANY] * (npart + nj),
            scratch_shapes=[pltpu.VMEM((g[0], g[1]), F32) for g in geometry]
            + [dma((npart, 3)), dma((npart, 3)), dma((nj,)), dma((nj,)), dma((nj,))]),
        out_shape=[jax.ShapeDtypeStruct((t, d), F32), jax.ShapeDtypeStruct((TILE_ROWS, d), F32)]
        + [jax.ShapeDtypeStruct(a.shape, a.dtype) for a in parts]
        + [jax.ShapeDtypeStruct(shape, F32) for _, shape, _, _ in joins],
        compiler_params=_params(dimension_semantics=("arbitrary",)),
    )(core_chip, dub, w_in_g, x, dx2, g1, *sum_operands, *parts)
    return outs[:2], outs[2:2 + npart], outs[2 + npart:]


WGRAD_GEOMETRY = {
    "in": (512, IN_SHARD, lambda s, h: h, lambda s, h: s),
    "mlp_in": (512, D_MODEL, lambda s, h: h, lambda s, h: s),
    "mlp_out": (512, D_MODEL, lambda s, h: 2 * s + h, lambda s, h: 0),
    "out": (384, 512, lambda s, h: s, lambda s, h: h),
}
K_CHUNK = 512


def _sibling():
    x, y, c = _position()
    return (x, y, 1 - c)


def _wgrad(a, b, tag, core_chip, pack=None, parts=()):
    t = a.shape[0]
    pr, pc, a_blk, b_blk = WGRAD_GEOMETRY[tag]
    nk = t // K_CHUNK
    mine = N_CHIPS
    riding = pack is not None
    npart = len(parts)
    assert not (riding and npart)

    def body(cc_ref, a_ref, b_ref, *rest):
        if riding:
            pack_ref, land_ref, p_ref, pb_ref, all_ref, stage, rbuf, send_sems, recv_sems, rsem, g_send, g_recv, g_local = rest
            gather = _PackGather(pack_ref, all_ref, g_send, g_recv, g_local)
        elif npart:
            part_refs, (land_ref, p_ref, pb_ref), rest = rest[:npart], rest[npart:npart + 3], rest[npart + 3:]
            arrived_refs, (stage, rbuf, send_sems, recv_sems, rsem, x_send, x_recv) = rest[:npart], rest[npart:]
            exchange = _PartialExchange(part_refs, arrived_refs, x_send, x_recv)
        else:
            land_ref, p_ref, pb_ref, stage, rbuf, send_sems, recv_sems, rsem = rest
        ph, s = pl.program_id(0), pl.program_id(1)
        if riding:
            @pl.when((ph == 0) & (s == 0))
            def _():
                gather.start()

            @pl.when((ph == 1) & (s == N_CHIPS - 2))
            def _():
                gather.hand_over()
        if npart:
            @pl.when((ph == 0) & (s == 0))
            def _():
                exchange.start()
        slot = jnp.where(ph == 0, s, mine)
        acc = stage.at[slot]
        acc[...] = _dot_tn(a_ref[0:K_CHUNK, :], b_ref[0:K_CHUNK, :])
        for k in range(1, nk):
            acc[...] += _dot_tn(a_ref[k * K_CHUNK:(k + 1) * K_CHUNK, :], b_ref[k * K_CHUNK:(k + 1) * K_CHUNK, :])

        def push(k):
            return pltpu.make_async_remote_copy(src_ref=stage.at[k], dst_ref=land_ref.at[k], send_sem=send_sems.at[k],
                                                recv_sem=recv_sems.at[k], device_id=_sibling(), device_id_type=MESH)

        @pl.when(ph == 0)
        def _():
            push(s).start()

        @pl.when(ph == 1)
        def _():
            push(s).wait_recv()
            landed = pltpu.make_async_copy(land_ref.at[s], rbuf, rsem)
            landed.start()
            landed.wait()
            p = stage[mine] + rbuf[...]
            p_ref[0] = p
            pb_ref[0] = p.astype(BF16)

        @pl.when((ph == 1) & (s == N_CHIPS - 1))
        def _():
            for k in range(N_CHIPS):
                push(k).wait_send()
            if riding:
                gather.finish()
            if npart:
                exchange.wait()

    def half(ph, cc):
        return jnp.where(ph == 0, 1 - cc[0], cc[0])

    def out_slot(ph, s, cc):
        return (jnp.where(ph == 0, 0, s), 0, 0)

    piece = jax.ShapeDtypeStruct((N_CHIPS, pr, pc), F32)
    in_specs = [pl.BlockSpec((t, pr), lambda ph, s, cc: (0, a_blk(s, half(ph, cc)))),
                pl.BlockSpec((t, pc), lambda ph, s, cc: (0, b_blk(s, half(ph, cc))))]
    out_specs = [ANY, pl.BlockSpec((1, pr, pc), out_slot), pl.BlockSpec((1, pr, pc), out_slot)]
    out_shape = [piece, piece, jax.ShapeDtypeStruct((N_CHIPS, pr, pc), BF16)]
    scratch = [pltpu.VMEM((N_CHIPS + 1, pr, pc), F32), pltpu.VMEM((pr, pc), F32),
               pltpu.SemaphoreType.DMA((N_CHIPS,)), pltpu.SemaphoreType.DMA((N_CHIPS,)), pltpu.SemaphoreType.DMA]
    operands = [a, b]
    if riding:
        in_specs.append(pl.BlockSpec(pack.shape, lambda ph, s, cc: (0, 0)))
        out_specs.append(ANY)
        out_shape.append(jax.ShapeDtypeStruct((N_DEVICES,) + pack.shape, pack.dtype))
        scratch += _PackGather.semaphores()
        operands.append(pack)
    if npart:
        in_specs += [ANY] * npart
        out_specs += [ANY] * npart
        out_shape += [jax.ShapeDtypeStruct(p.shape, p.dtype) for p in parts]
        scratch += [pltpu.SemaphoreType.DMA((npart, 3)), pltpu.SemaphoreType.DMA((npart, 3))]
        operands += list(parts)
    return pl.pallas_call(
        body, name="wgrad_" + tag,
        grid_spec=pltpu.PrefetchScalarGridSpec(
            num_scalar_prefetch=1, grid=(2, N_CHIPS), in_specs=in_specs, out_specs=out_specs, scratch_shapes=scratch),
        out_shape=out_shape,
        compiler_params=_params(dimension_semantics=("arbitrary", "arbitrary")),
    )(core_chip, *operands)[1:]


def _other_chips(x, y):
    return [(1 - x, y), (x, 1 - y), (1 - x, 1 - y)]


class _ShardGather:
    PAIRS = 9

    def __init__(self, outs, send_sems, recv_sems):
        self.outs, self.send_sems, self.recv_sems = outs, send_sems, recv_sems
        x, y, c = _position()
        self.c, self.j = c, 2 * x + y
        self.sibling = (x, y, 1 - c)
        self.chips = _other_chips(x, y)

    def _chip(self, k):
        px, py = self.chips[k]
        return 2 * px + py

    def _half(self, w, chip, which):
        hr = self.outs[w].shape[1] // 2
        return self.outs[w].at[chip, pl.ds(which * hr, hr), :]

    def _quarter(self, w, chip, q):
        qr = self.outs[w].shape[1] // 4
        return self.outs[w].at[chip, pl.ds(self.c * 2 * qr + q * qr, qr), :]

    def _copy(self, ref, w, pair, to, src=None):
        return pltpu.make_async_remote_copy(src_ref=ref if src is None else src, dst_ref=ref, send_sem=self.send_sems.at[w, pair],
                                            recv_sem=self.recv_sems.at[w, pair], device_id=to, device_id_type=MESH)

    def direct(self, w, k, q, src=None):
        return self._copy(self._quarter(w, self.j, q), w, 2 * k + q, (*self.chips[k], self.c), src)

    def direct_landed(self, w, k, q):
        return self._copy(self._quarter(w, self._chip(k), q), w, 2 * k + q, (*self.chips[k], self.c))

    def pass_on(self, w, q):
        return self._copy(self._quarter(w, self._chip(q), q), w, 4 + q, (*self.chips[1 - q], self.c))

    def passed_landed(self, w, q):
        return self._copy(self._quarter(w, self._chip(2), q), w, 4 + q, (*self.chips[1 - q], self.c))

    def hand_over(self, w, k):
        return self._copy(self._half(w, self._chip(k), self.c), w, 6 + k, self.sibling)

    def handed(self, w, k):
        return self._copy(self._half(w, self._chip(k), 1 - self.c), w, 6 + k, self.sibling)

    def start_direct(self, w, src_half=None):
        qr = self.outs[w].shape[1] // 4
        for k, q in ((0, 0), (1, 1), (0, 1), (1, 0)):
            self.direct(w, k, q, None if src_half is None else src_half.at[pl.ds(q * qr, qr), :]).start()

    def start_pass_on(self, w):
        for q in (0, 1):
            self.direct_landed(w, q, q).wait_recv()
            self.pass_on(w, q).start()

    def start_hand_over(self, w, diagonal):
        if diagonal:
            for q in (0, 1):
                self.passed_landed(w, q).wait_recv()
            self.hand_over(w, 2).start()
        else:
            for k in (0, 1):
                self.direct_landed(w, k, 1 - k).wait_recv()
                self.hand_over(w, k).start()

    def finish(self, w):
        for k in range(3):
            self.handed(w, k).wait_recv()
            self.hand_over(w, k).wait_send()
        for q in (0, 1):
            self.pass_on(w, q).wait_send()
            for k in (0, 1):
                self.direct(w, k, q).wait_send()


def _gather_first(w_in, w_out, w1, w2, small):
    bigs = (w_in, w_out, w1, w2)
    nb = len(bigs)

    def body(win_ref, wout_ref, w1_ref, w2_ref, sm_ref, gin, gout, g1, g2, gsm, st_in, st_out, st_1, st_2,
             send_sems, recv_sems, sm_send, sm_recv, local_sems):
        srcs = (win_ref, wout_ref, w1_ref, w2_ref)
        stages = (st_in, st_out, st_1, st_2)
        outs = (gin, gout, g1, g2)
        plan = _ShardGather(outs[:1], send_sems, recv_sems)
        j, c = plan.j, plan.c
        for src, st in zip(srcs, stages):
            st[...] = src[...].astype(BF16)
        local = [pltpu.make_async_copy(stages[w], outs[w].at[j], local_sems.at[w]) for w in range(nb)]
        local.append(pltpu.make_async_copy(sm_ref, gsm.at[j], local_sems.at[nb]))
        for cp in local:
            cp.start()

        def small_copy(k):
            px, py = plan.chips[k]
            return pltpu.make_async_remote_copy(src_ref=sm_ref, dst_ref=gsm.at[j], send_sem=sm_send.at[k],
                                                recv_sem=sm_recv.at[k], device_id=(px, py, c), device_id_type=MESH)

        def small_landed(k):
            px, py = plan.chips[k]
            return pltpu.make_async_remote_copy(src_ref=sm_ref, dst_ref=gsm.at[2 * px + py], send_sem=sm_send.at[k],
                                                recv_sem=sm_recv.at[k], device_id=(px, py, c), device_id_type=MESH)

        hr = w_in.shape[0] // 2
        plan.start_direct(0, st_in.at[pl.ds(c * hr, hr), :])
        for k in range(3):
            small_copy(k).start()
        plan.start_pass_on(0)
        plan.start_hand_over(0, diagonal=False)
        plan.start_hand_over(0, diagonal=True)
        for k in range(3):
            small_landed(k).wait_recv()
            small_copy(k).wait_send()
        plan.finish(0)
        for cp in local:
            cp.wait()

    def gathered(a, dtype):
        return jax.ShapeDtypeStruct((N_CHIPS,) + a.shape, dtype)

    return pl.pallas_call(
        body, name="gather_first",
        in_specs=[VMEM] * 5, out_specs=[ANY] * 5,
        out_shape=[gathered(a, BF16) for a in bigs] + [gathered(small, F32)],
        scratch_shapes=[pltpu.VMEM(a.shape, BF16) for a in bigs]
        + [pltpu.SemaphoreType.DMA((1, _ShardGather.PAIRS)), pltpu.SemaphoreType.DMA((1, _ShardGather.PAIRS)), pltpu.SemaphoreType.DMA((3,)),
           pltpu.SemaphoreType.DMA((3,)), pltpu.SemaphoreType.DMA((nb + 1,))],
        compiler_params=_params(),
    )(*bigs, small)


class _PartialExchange:
    def __init__(self, parts, arrived, send_sems, recv_sems):
        self.parts, self.arrived, self.send_sems, self.recv_sems = parts, arrived, send_sems, recv_sems
        x, y, c = _position()
        self.c, self.j = c, 2 * x + y
        self.chips = _other_chips(x, y)

    def _copy(self, w, k, slot):
        px, py = self.chips[k]
        return pltpu.make_async_remote_copy(
            src_ref=self.parts[w].at[2 * px + py], dst_ref=self.arrived[w].at[slot], send_sem=self.send_sems.at[w, k],
            recv_sem=self.recv_sems.at[w, k], device_id=(px, py, self.c), device_id_type=MESH)

    def start(self):
        for w in range(len(self.parts)):
            for k in range(3):
                self._copy(w, k, self.j).start()

    def wait(self):
        for w in range(len(self.parts)):
            for k in range(3):
                px, py = self.chips[k]
                self._copy(w, k, 2 * px + py).wait()


class _PackGather:
    def __init__(self, p_ref, all_ref, send_sems, recv_sems, local_sem):
        self.p_ref, self.all_ref, self.send_sems, self.recv_sems, self.local_sem = p_ref, all_ref, send_sems, recv_sems, local_sem
        x, y, c = _position()
        self.me, self.sibling, self.c = (x, y, c), (x, y, 1 - c), c
        self.chips = _other_chips(x, y)

    @staticmethod
    def semaphores():
        return [pltpu.SemaphoreType.DMA((7,)), pltpu.SemaphoreType.DMA((7,)), pltpu.SemaphoreType.DMA]

    def _copy(self, k, block, to, from_pack=False):
        px, py, pc = block
        slot = self.all_ref.at[4 * px + 2 * py + pc]
        return pltpu.make_async_remote_copy(src_ref=self.p_ref if from_pack else slot, dst_ref=slot, send_sem=self.send_sems.at[k],
                                            recv_sem=self.recv_sems.at[k], device_id=to, device_id_type=MESH)

    def _mine(self):
        x, y, c = self.me
        return pltpu.make_async_copy(self.p_ref, self.all_ref.at[4 * x + 2 * y + c], self.local_sem)

    def _first(self):
        return [self._copy(0, self.me, self.sibling, True)] + [
            self._copy(1 + k, self.me, (*chip, self.c), True) for k, chip in enumerate(self.chips)]

    def _passed(self):
        return [self._copy(4 + k, (*chip, self.c), self.sibling) for k, chip in enumerate(self.chips)]

    def start(self):
        self._mine().start()
        for cp in self._first():
            cp.start()

    def hand_over(self):
        for k, chip in enumerate(self.chips):
            self._copy(1 + k, (*chip, self.c), self.me).wait_recv()
            self._passed()[k].start()

    def finish(self):
        self._copy(0, self.sibling, self.me).wait_recv()
        for k, chip in enumerate(self.chips):
            self._copy(4 + k, (*chip, 1 - self.c), self.me).wait_recv()
        for cp in self._first() + self._passed():
            cp.wait_send()
        self._mine().wait()


class _DirectGather:
    def __init__(self, p_ref, all_ref, send_sems, recv_sems, local_sem):
        self.p_ref, self.all_ref, self.send_sems, self.recv_sems, self.local_sem = p_ref, all_ref, send_sems, recv_sems, local_sem
        self.me = _position()

    semaphores = _PackGather.semaphores

    def _peer(self, r):
        x, y, c = self.me
        return ((1 - x) if r & 4 else x, (1 - y) if r & 2 else y, (1 - c) if r & 1 else c)

    def _copy(self, r, slot_of):
        px, py, pc = slot_of
        return pltpu.make_async_remote_copy(src_ref=self.p_ref, dst_ref=self.all_ref.at[4 * px + 2 * py + pc],
                                            send_sem=self.send_sems.at[r - 1], recv_sem=self.recv_sems.at[r - 1],
                                            device_id=self._peer(r), device_id_type=MESH)

    def _mine(self):
        x, y, c = self.me
        return pltpu.make_async_copy(self.p_ref, self.all_ref.at[4 * x + 2 * y + c], self.local_sem)

    def start(self):
        self._mine().start()
        for r in range(1, N_DEVICES):
            self._copy(r, self.me).start()

    def finish(self):
        for r in range(1, N_DEVICES):
            self._copy(r, self._peer(r)).wait()
        self._mine().wait()


def _adamw(w, g, m, v):
    m = ADAM_B1 * m + (1.0 - ADAM_B1) * g
    v = ADAM_B2 * v + (1.0 - ADAM_B2) * (g * g)
    m_hat = m / ADAM_BC1
    v_hat = v / ADAM_BC2
    delta = -ADAM_LR * (m_hat / (jnp.sqrt(v_hat) + ADAM_EPS) + ADAM_WD * w)
    return delta, m, v


JOIN_SUB = 4


def _join(tag, shard_shape, part, arrived, core_chip, block=None):
    pr, pc = WGRAD_GEOMETRY[tag][:2]
    rb = pr // JOIN_SUB
    by_rows = shard_shape[1] == pc
    riding = block is not None

    def body(cc_ref, p_ref, r1_ref, r2_ref, r3_ref, *rest):
        if riding:
            blk_ref, g_ref, all_ref, stage, send_sems, recv_sems, local_sems, b_send, b_recv, b_local = rest
            gather = _DirectGather(blk_ref, all_ref, b_send, b_recv, b_local)
        else:
            g_ref, stage, send_sems, recv_sems, local_sems = rest
        i = pl.program_id(0)
        c = cc_ref[0]
        if riding:
            @pl.when(i == 0)
            def _():
                gather.start()

        def window(core, k):
            if by_rows:
                return g_ref.at[pl.ds((core * JOIN_SUB + k) * rb, rb), :]
            return g_ref.at[pl.ds(k * rb, rb), pl.ds(core * pc, pc)]

        def keep(k):
            return pltpu.make_async_copy(stage.at[k], window(c, k), local_sems.at[k])

        def push(k):
            return pltpu.make_async_remote_copy(src_ref=stage.at[k], dst_ref=window(c, k), send_sem=send_sems.at[k],
                                                recv_sem=recv_sems.at[k], device_id=_sibling(), device_id_type=MESH)

        def pushed(k):
            return pltpu.make_async_remote_copy(src_ref=stage.at[k], dst_ref=window(1 - c, k), send_sem=send_sems.at[k],
                                                recv_sem=recv_sems.at[k], device_id=_sibling(), device_id_type=MESH)

        stage[i] = ((p_ref[0] + r1_ref[0].astype(F32)) + r2_ref[0].astype(F32)) + r3_ref[0].astype(F32)
        keep(i).start()
        push(i).start()

        @pl.when(i == JOIN_SUB - 1)
        def _():
            for k in range(JOIN_SUB):
                keep(k).wait()
                push(k).wait_send()
                pushed(k).wait_recv()
            if riding:
                gather.finish()

    def partial(off):
        return pl.BlockSpec((1, rb, pc), lambda i, cc: ((cc[1] + off) % N_CHIPS, i, 0))

    in_specs = [partial(0), partial(1), partial(2), partial(3)]
    out_specs = [ANY]
    out_shape = [jax.ShapeDtypeStruct(shard_shape, F32)]
    scratch = [pltpu.VMEM((JOIN_SUB, rb, pc), F32), pltpu.SemaphoreType.DMA((JOIN_SUB,)),
               pltpu.SemaphoreType.DMA((JOIN_SUB,)), pltpu.SemaphoreType.DMA((JOIN_SUB,))]
    operands = [part, arrived, arrived, arrived]
    if riding:
        in_specs.append(pl.BlockSpec(block.shape, lambda i, cc: (0, 0)))
        out_specs.append(ANY)
        out_shape.append(jax.ShapeDtypeStruct((N_DEVICES,) + block.shape, block.dtype))
        scratch += _DirectGather.semaphores()
        operands.append(block)
    outs = pl.pallas_call(
        body, name="join_" + tag,
        grid_spec=pltpu.PrefetchScalarGridSpec(
            num_scalar_prefetch=1, grid=(JOIN_SUB,), in_specs=in_specs, out_specs=out_specs, scratch_shapes=scratch),
        out_shape=out_shape,
        compiler_params=_params(dimension_semantics=("arbitrary",)),
    )(core_chip, *operands)
    return outs if riding else outs[0]


def _adamw_big(w, g, m, v, name):
    rows, cols = w.shape
    rb = 256 if rows % 256 == 0 else rows

    def body(w_ref, g_ref, m_ref, v_ref, go_ref, d_ref, nm_ref, nv_ref):
        g = g_ref[...]
        go_ref[...] = g
        d_ref[...], nm_ref[...], nv_ref[...] = _adamw(w_ref[...], g, m_ref[...], v_ref[...])

    spec = pl.BlockSpec((rb, cols), lambda i: (i, 0))
    return pl.pallas_call(
        body, name=name, grid=(rows // rb,), in_specs=[spec] * 4, out_specs=[spec] * 4,
        out_shape=[jax.ShapeDtypeStruct(w.shape, F32)] * 4,
        compiler_params=_params(dimension_semantics=("arbitrary",)),
    )(w, g, m, v)


def _small_step(packs, mix_g_blocks, w_pack, m_pack, v_pack, conv_wmv, rconv_wmv):
    rows, cols = packs.shape[1:]
    cshard = conv_wmv.shape[2]
    rshard = rconv_wmv.shape[2]
    mix_row = PK_MIX_G * TILE_ROWS

    def body(all_ref, blk_ref, w_ref, m_ref, v_ref, cw_ref, rw_ref, g_ref, d_ref, nm_ref, nv_ref, co_ref, ro_ref):
        total = all_ref[0]
        late = blk_ref[0]
        for k in range(1, N_DEVICES):
            total = total + all_ref[k]
            late = late + blk_ref[k]
        g_ref[...] = total
        g_ref[mix_row:mix_row + TILE_ROWS, :] = late
        g = g_ref[...]
        d_ref[...], nm_ref[...], nv_ref[...] = _adamw(w_ref[...], g, m_ref[...], v_ref[...])

        x, y, _ = _position()
        j = 2 * x + y
        cblk = total[PK_CONV_W * 8:PK_CONV_W * 8 + 8, :]
        rblk = total[PK_RCONV_W * 8:PK_RCONV_W * 8 + 8, :]
        cg = cblk[:, 0:cshard]
        rg = rblk[:, 0:rshard]
        for k in range(1, N_CHIPS):
            cg = jnp.where(j == k, cblk[:, k * cshard:(k + 1) * cshard], cg)
            rg = jnp.where(j == k, rblk[:, k * rshard:(k + 1) * rshard], rg)
        co_ref[0] = cg
        co_ref[1], co_ref[2], co_ref[3] = _adamw(cw_ref[0], cg, cw_ref[1], cw_ref[2])
        ro_ref[0] = rg
        ro_ref[1], ro_ref[2], ro_ref[3] = _adamw(rw_ref[0], rg, rw_ref[1], rw_ref[2])

    pack = [jax.ShapeDtypeStruct((rows, cols), F32)] * 4
    return pl.pallas_call(
        body, name="small_grads_step", in_specs=[VMEM] * 7, out_specs=[VMEM] * 6,
        out_shape=pack + [jax.ShapeDtypeStruct((4, TILE_ROWS, cshard), F32), jax.ShapeDtypeStruct((4, TILE_ROWS, rshard), F32)],
        compiler_params=_params(),
    )(packs, mix_g_blocks, w_pack, m_pack, v_pack, conv_wmv, rconv_wmv)


def _blk(a):
    a = a.reshape(-1, a.shape[-1])
    return jnp.pad(a, ((0, TILE_ROWS - a.shape[0]), (0, D_MODEL - a.shape[1])))


def _zero_blk():
    return jnp.zeros((TILE_ROWS, D_MODEL), F32)


def _pack_params(p, pre):
    get = lambda n: p[pre + n]
    return jnp.concatenate([
        _blk(get("g_norm_rnn")), _blk(get("rnn_conv_b")), _blk(get("b_a")), _blk(get("b_x")), _blk(get("lru_lambda")),
        _zero_blk(), _zero_blk(), _blk(get("g_norm_conv")), _blk(get("final_norm_g").reshape(1, -1)), _blk(get("norm_mlp_g")),
        _zero_blk(), _blk(get("norm_mix_g")), get("w_a").reshape(64, D_MODEL), get("w_x").reshape(64, D_MODEL)], axis=0)


def _to_block_diag(w):
    w4 = w.reshape(N_BD, 4, 64, 64)
    eye = jnp.eye(4, dtype=w.dtype)
    return (w4[:, :, :, None, :] * eye[None, :, None, :, None]).reshape(N_BD, BD, BD)


def _from_block_diag(d):
    d5 = d.reshape(N_BD, 4, 64, 4, 64)
    return jnp.stack([d5[:, q, :, q, :] for q in range(4)], axis=1).reshape(64, D_MODEL)


def _pad_rows(a):
    return jnp.pad(a, ((0, TILE_ROWS - a.shape[0]), (0, 0)))


_NAMES = ['norm_mix_g', 'w_in', 'conv_w', 'rnn_conv_w', 'rnn_conv_b', 'w_a', 'b_a', 'w_x', 'b_x', 'lru_lambda',
          'g_norm_conv', 'g_norm_rnn', 'w_out', 'norm_mlp_g', 'w_mlp_in', 'w_mlp_out', 'final_norm_g']


def kernel(x, norm_mix_g, w_in, conv_w, rnn_conv_w, rnn_conv_b, w_a, b_a, w_x, b_x, lru_lambda, g_norm_conv, g_norm_rnn, w_out, norm_mlp_g, w_mlp_in, w_mlp_out, final_norm_g, loss_target, m_norm_mix_g, m_w_in, m_conv_w, m_rnn_conv_w, m_rnn_conv_b, m_w_a, m_b_a, m_w_x, m_b_x, m_lru_lambda, m_g_norm_conv, m_g_norm_rnn, m_w_out, m_norm_mlp_g, m_w_mlp_in, m_w_mlp_out, m_final_norm_g, v_norm_mix_g, v_w_in, v_conv_w, v_rnn_conv_w, v_rnn_conv_b, v_w_a, v_b_a, v_w_x, v_b_x, v_lru_lambda, v_g_norm_conv, v_g_norm_rnn, v_w_out, v_norm_mlp_g, v_w_mlp_in, v_w_mlp_out, v_final_norm_g):
    args = dict(locals())
    p = {}
    for n in _NAMES:
        for pre in ("", "m_", "v_"):
            a = args[pre + n]
            p[pre + n] = a[0] if a.ndim >= 3 else a
    xs = x[0]
    target = loss_target[0]
    core_chip = jnp.stack([lax.axis_index("c"), 2 * lax.axis_index("x") + lax.axis_index("y")]).astype(jnp.int32)
    cshard = p["conv_w"].shape[1]
    rshard = p["rnn_conv_w"].shape[1]

    small = jnp.concatenate([_pad_rows(p["conv_w"]), _pad_rows(p["rnn_conv_w"])], axis=1)
    w_in_g, w_out_g, w1_g, w2_g, small_g = _gather_first(p["w_in"], p["w_out"], p["w_mlp_in"], p["w_mlp_out"], small)
    conv_full = small_g[:, :3, :cshard].transpose(1, 0, 2).reshape(3, CONV_W)
    rconv_full = small_g[:, :4, cshard:].transpose(1, 0, 2).reshape(4, LRU_W)
    wa_bd = _to_block_diag(p["w_a"]).astype(BF16)
    wx_bd = _to_block_diag(p["w_x"]).astype(BF16)
    gf = p["final_norm_g"].reshape(1, -1)
    lru = (wa_bd, p["b_a"], wx_bd, p["b_x"], p["lru_lambda"], p["g_norm_conv"], p["g_norm_rnn"])

    (u, h1b, xr, hs, c3, yb, *gates), (w_out_g, w1_g, w2_g) = _fwd_mix(
        xs, p["norm_mix_g"], w_in_g, conv_full, rconv_full, p["rnn_conv_b"], *lru, (w_out_g, w1_g, w2_g))
    zb, dpb, h2b, dx3b, dx2, dx2b, dy, st_mlp = _mlp_fwd_bwd(
        xs, yb, w_out_g.reshape(-1, D_MODEL), w1_g, w2_g.reshape(-1, D_MODEL), p["norm_mlp_g"], gf, target)

    part_out = _wgrad(yb, dx2b, "out", core_chip)
    *part_1, arrived_out = _wgrad(h2b, dpb, "mlp_in", core_chip, parts=(part_out[1],))
    *part_2, arrived_1 = _wgrad(zb, dx3b, "mlp_out", core_chip, parts=(part_1[1],))
    (dub, st_mix, dwa_bd, dwx_bd), (arrived_2,) = _mix_bwd(
        dy, u, xr, hs, c3, gates, conv_full, rconv_full, wa_bd, wx_bd, p["lru_lambda"], p["g_norm_conv"], p["g_norm_rnn"],
        (part_2[1],))
    arrived_mlp = (arrived_out, arrived_1, arrived_2)
    pack = jnp.concatenate([st_mix, st_mlp, _zero_blk(), _from_block_diag(dwa_bd), _from_block_diag(dwx_bd)], axis=0)
    *part_in, packs = _wgrad(h1b, dub, "in", core_chip, pack)
    early = (("w_out", "out", part_out, arrived_mlp[0]), ("w_mlp_in", "mlp_in", part_1, arrived_mlp[1]),
             ("w_mlp_out", "mlp_out", part_2, arrived_mlp[2]))
    (grad_x, st_in), arrived_in, joined = _in_bwd(
        dub, w_in_g, xs, dx2, p["norm_mix_g"], (part_in[1],),
        [(tag, p[n].shape, part[0], arrived) for n, tag, part, arrived in early], core_chip)
    g_in, mix_g_blocks = _join("in", p["w_in"].shape, part_in[0], arrived_in[0], core_chip, st_in)
    big = {}
    for n, tag, g in [(n, tag, g) for (n, tag, _, _), g in zip(early, joined)] + [("w_in", "in", g_in)]:
        big[n] = _adamw_big(p[n], g, p["m_" + n], p["v_" + n], "adamw_" + tag)

    conv_wmv = jnp.stack([_pad_rows(p[pre + "conv_w"]) for pre in ("", "m_", "v_")])
    rconv_wmv = jnp.stack([_pad_rows(p[pre + "rnn_conv_w"]) for pre in ("", "m_", "v_")])
    g_pack, d_pack, m_pack, v_pack, conv_out, rconv_out = _small_step(
        packs, mix_g_blocks, _pack_params(p, ""), _pack_params(p, "m_"), _pack_params(p, "v_"), conv_wmv, rconv_wmv)

    def unpack(pk, kind):
        def vec(b, width=D_MODEL):
            return pk[b * 8:b * 8 + 1, :width]
        return {
            "norm_mix_g": vec(PK_MIX_G), "rnn_conv_b": vec(PK_RCONV_B), "b_a": vec(PK_B_A), "b_x": vec(PK_B_X),
            "lru_lambda": vec(PK_LAMBDA), "g_norm_conv": vec(PK_G_NORM_CONV, CONV_W), "g_norm_rnn": vec(PK_G_NORM_RNN),
            "norm_mlp_g": vec(PK_MLP_G), "final_norm_g": vec(PK_FINAL_G).reshape(-1),
            "w_a": pk[PK_W_A * 8:PK_W_A * 8 + 64].reshape(1, 16, 64, 64), "w_x": pk[PK_W_X * 8:PK_W_X * 8 + 64].reshape(1, 16, 64, 64),
            "conv_w": conv_out[kind, :3][None], "rnn_conv_w": rconv_out[kind, :4][None],
            "w_in": big["w_in"][kind][None], "w_out": big["w_out"][kind][None],
            "w_mlp_in": big["w_mlp_in"][kind][None], "w_mlp_out": big["w_mlp_out"][kind][None],
        }

    outs = [unpack(pk, kind) for kind, pk in enumerate((g_pack, d_pack, m_pack, v_pack))]
    for o in outs:
        for n in ("norm_mix_g", "rnn_conv_b", "b_a", "b_x", "lru_lambda", "g_norm_conv", "g_norm_rnn", "norm_mlp_g"):
            o[n] = o[n].reshape(1, -1)
    loss = g_pack[PK_LOSS * 8, 0]
    return (loss, grad_x[None], *[o[n] for o in outs for n in _NAMES])
```

```python
import functools
import math

import jax
import jax.numpy as jnp
from jax import lax
from jax.experimental import pallas as pl
from jax.experimental.pallas import tpu as pltpu

F32 = jnp.float32
BF16 = jnp.bfloat16
MESH = pl.DeviceIdType.MESH
ANY = pl.BlockSpec(memory_space=pl.ANY)
VMEM = pl.BlockSpec(memory_space=pltpu.VMEM)

EPS = 1e-6
LRU_C = 8.0
D_MODEL = 1024
CONV_W = 512
LRU_W = 1024
IN_COLS = 3 * CONV_W + 2 * LRU_W
IN_SHARD = IN_COLS // 4
N_CHIPS = 4
N_DEVICES = 8
BD = 256
N_BD = LRU_W // BD

ADAM_LR = 0.001
ADAM_B1 = 0.9
ADAM_B2 = 0.999
ADAM_EPS = 1e-08
ADAM_WD = 0.01
ADAM_STEP = 10
ADAM_BC1 = 1.0 - ADAM_B1 ** ADAM_STEP
ADAM_BC2 = 1.0 - ADAM_B2 ** ADAM_STEP

TILE_ROWS = 8
TOKEN_TILE = 256
VMEM_LIMIT = 56 * 1024 * 1024

PK_G_NORM_RNN, PK_RCONV_B, PK_B_A, PK_B_X, PK_LAMBDA, PK_RCONV_W, PK_CONV_W, PK_G_NORM_CONV = range(8)
PK_FINAL_G, PK_MLP_G, PK_LOSS, PK_MIX_G = 8, 9, 10, 11
PK_W_A = 12
PK_W_X = 20
PK_BLOCKS = 28
PK_ROWS = PK_BLOCKS * TILE_ROWS


def _params(**kw):
    return pltpu.CompilerParams(vmem_limit_bytes=VMEM_LIMIT, **kw)


def _position():
    x, y, c = lax.axis_index("x"), lax.axis_index("y"), lax.axis_index("c")
    return x, y, c


def _sigmoid(v):
    return 1.0 / (1.0 + jnp.exp(-v))


def _one_minus_square(log_a, a):
    v = 2.0 * log_a
    series = -v * (1.0 + v * (0.5 + v * (1.0 / 6.0)))
    return jnp.where(v > -0.01, series, 1.0 - a * a)


_GELU_C = math.sqrt(2.0 / math.pi)
_GELU_K = 0.044715


def _gelu_and_grad(g):
    th = jnp.tanh(_GELU_C * (g + _GELU_K * g * g * g))
    gelu = 0.5 * g * (1.0 + th)
    dgelu = 0.5 * (1.0 + th) + 0.5 * g * (1.0 - th * th) * (_GELU_C * (1.0 + 3.0 * _GELU_K * g * g))
    return gelu, dgelu


def _rows(shape):
    return lax.broadcasted_iota(jnp.int32, shape, 0)


def _shift_down(v, k, prev8):
    rolled = pltpu.roll(v, k, 0)
    halo = pltpu.roll(prev8, k, 0)
    head = jnp.where(_rows(halo.shape) < k, halo, rolled[:TILE_ROWS])
    return jnp.concatenate([head, rolled[TILE_ROWS:]], axis=0)


def _shift_up(v, k, next8):
    n = v.shape[0]
    rolled = pltpu.roll(v, n - k, 0)
    halo = pltpu.roll(next8, TILE_ROWS - k, 0)
    tail = jnp.where(_rows(halo.shape) >= TILE_ROWS - k, halo, rolled[n - TILE_ROWS:])
    return jnp.concatenate([rolled[: n - TILE_ROWS], tail], axis=0)


def _scan_rows(a, b, carry, reverse=False):
    n, w = a.shape
    groups = n // TILE_ROWS
    a3 = a.reshape(groups, TILE_ROWS, w)
    b3 = b.reshape(groups, TILE_ROWS, w)
    sub = lax.broadcasted_iota(jnp.int32, a3.shape, 1)
    s = 1
    while s < TILE_ROWS:
        shift = TILE_ROWS - s if reverse else s
        keep = (sub < TILE_ROWS - s) if reverse else (sub >= s)
        b3 = b3 + jnp.where(keep, a3 * pltpu.roll(b3, shift, 1), 0.0)
        a3 = a3 * jnp.where(keep, pltpu.roll(a3, shift, 1), 1.0)
        s *= 2
    out = [None] * groups
    edge = 0 if reverse else TILE_ROWS - 1
    for g in (range(groups - 1, -1, -1) if reverse else range(groups)):
        out[g] = b3[g] + a3[g] * carry
        carry = out[g][edge:edge + 1]
    return jnp.concatenate(out, axis=0)


def _softplus_neg(lam):
    e = jnp.exp(-jnp.abs(lam))
    log1p_e = jnp.where(e < 1e-2, e * (1.0 - e * (0.5 - e * (1.0 / 3.0 - e * 0.25))), jnp.log(1.0 + e))
    sp = jnp.maximum(-lam, 0.0) + log1p_e
    dsp = -_sigmoid(-lam)
    return sp, dsp


def _block_diag_dot(vb, w_ref):
    return jnp.concatenate(
        [jnp.dot(vb[:, j * BD:(j + 1) * BD], w_ref[j], preferred_element_type=F32) for j in range(N_BD)], axis=1)


def _block_diag_dot_t(vb, w_ref):
    return jnp.concatenate(
        [lax.dot_general(vb[:, j * BD:(j + 1) * BD], w_ref[j], (((1,), (1,)), ((), ())), preferred_element_type=F32)
         for j in range(N_BD)], axis=1)


def _dot_nt(a, b):
    return lax.dot_general(a, b, (((1,), (1,)), ((), ())), preferred_element_type=F32)


def _dot_tn(a, b):
    return lax.dot_general(a, b, (((0,), (0,)), ((), ())), preferred_element_type=F32)


def _lru_gates(xr, wa_ref, ba, wx_ref, bx, sp):
    xrb = xr.astype(BF16)
    r = _sigmoid(_block_diag_dot(xrb, wa_ref) + ba)
    ig = _sigmoid(_block_diag_dot(xrb, wx_ref) + bx)
    log_a = (-LRU_C) * r * sp
    a = jnp.exp(log_a)
    mult = jnp.sqrt(_one_minus_square(log_a, a))
    return r, ig, a, mult


def _colsum(v):
    return jnp.sum(v, axis=0, keepdims=True)


N_FWD_OUT = 10


def _fwd_mix(x, g1, w_in_g, conv_w, rconv_w, rconv_b, wa_bd, b_a, wx_bd, b_x, lam, g_nc, g_nr, later):
    t, d = x.shape
    tm = TOKEN_TILE
    nt = t // tm
    nl = len(later)
    assert nl == 3
    pass_on_at = [nt * f // 16 for f in (1, 3, 7)]
    neighbours_at = [nt * f // 16 for f in (3, 6, 10)]
    diagonal_at = [nt * f // 16 for f in (10, 12, 14)]

    def body(x_ref, g1_ref, win_ref, cw_ref, rw_ref, rb_ref, wa_ref, ba_ref, wx_ref, bx_ref, lam_ref, gnc_ref, gnr_ref,
             *rest):
        later_in, outs, rest = rest[:nl], rest[nl:nl + N_FWD_OUT], rest[nl + N_FWD_OUT:]
        u_ref, h1_ref, xr_ref, hs_ref, c3_ref, y_ref, r_ref, ig_ref, a_ref, mult_ref = outs
        later_out, (cv_prev, xin_prev, h_prev, send_sems, recv_sems) = rest[:nl], rest[nl:]
        del later_in
        step = pl.program_id(0)
        plan = _ShardGather(later_out, send_sems, recv_sems)

        @pl.when(step == 0)
        def _():
            cv_prev[...] = jnp.zeros_like(cv_prev)
            xin_prev[...] = jnp.zeros_like(xin_prev)
            h_prev[...] = jnp.zeros_like(h_prev)
            for w in range(nl):
                plan.start_direct(w)

        for w in range(nl):
            @pl.when(step == pass_on_at[w])
            def _(w=w):
                plan.start_pass_on(w)

            @pl.when(step == neighbours_at[w])
            def _(w=w):
                plan.start_hand_over(w, diagonal=False)

            @pl.when(step == diagonal_at[w])
            def _(w=w):
                plan.start_hand_over(w, diagonal=True)

        xv = x_ref[...]
        rstd = lax.rsqrt(jnp.mean(xv * xv, axis=-1, keepdims=True) + EPS)
        h1b = ((xv * rstd) * g1_ref[...]).astype(BF16)
        h1_ref[...] = h1b
        for j in range(N_CHIPS):
            u_ref[:, j * IN_SHARD:(j + 1) * IN_SHARD] = jnp.dot(h1b, win_ref[j], preferred_element_type=F32)
        gate_b = u_ref[:, 0:CONV_W]
        cv = u_ref[:, CONV_W:2 * CONV_W] * u_ref[:, 2 * CONV_W:3 * CONV_W]
        x_r = u_ref[:, 3 * CONV_W:3 * CONV_W + LRU_W]
        g = u_ref[:, 3 * CONV_W + LRU_W:]

        cw = cw_ref[...]
        cvp = cv_prev[...]
        conv3 = cw[0:1] * _shift_down(cv, 2, cvp) + cw[1:2] * _shift_down(cv, 1, cvp) + cw[2:3] * cv
        cv_prev[...] = cv[tm - TILE_ROWS:]
        c3_ref[...] = conv3
        y_conv = gate_b * conv3

        rw = rw_ref[...]
        xp = xin_prev[...]
        xr = (rw[0:1] * _shift_down(x_r, 3, xp) + rw[1:2] * _shift_down(x_r, 2, xp)
              + rw[2:3] * _shift_down(x_r, 1, xp) + rw[3:4] * x_r) + rb_ref[...]
        xin_prev[...] = x_r[tm - TILE_ROWS:]
        xr_ref[...] = xr
        sp, _ = _softplus_neg(lam_ref[...])
        r, ig, a, mult = _lru_gates(xr, wa_ref, ba_ref[...], wx_ref, bx_ref[...], sp)
        r_ref[...] = r
        ig_ref[...] = ig
        a_ref[...] = a
        mult_ref[...] = mult
        h = _scan_rows(a, mult * (ig * xr), h_prev[...])
        h_prev[...] = h[tm - 1:tm]
        hs_ref[...] = h
        gelu, _ = _gelu_and_grad(g)
        y_rnn = h * gelu

        na = y_conv * lax.rsqrt(jnp.mean(y_conv * y_conv, axis=-1, keepdims=True) + EPS) * gnc_ref[...]
        nb = y_rnn * lax.rsqrt(jnp.mean(y_rnn * y_rnn, axis=-1, keepdims=True) + EPS) * gnr_ref[...]
        y_ref[:, :CONV_W] = na.astype(BF16)
        y_ref[:, CONV_W:] = nb.astype(BF16)

        @pl.when(step == nt - 1)
        def _():
            for w in range(nl):
                plan.finish(w)

    def full(a):
        nd = a.ndim
        return pl.BlockSpec(a.shape, lambda i: (0,) * nd)

    def tok(cols):
        return pl.BlockSpec((tm, cols), lambda i: (i, 0))

    def act(cols, dtype=F32):
        return jax.ShapeDtypeStruct((t, cols), dtype)

    smalls = (g1, w_in_g, conv_w, rconv_w, rconv_b, wa_bd, b_a, wx_bd, b_x, lam, g_nc, g_nr)
    n_in = 1 + len(smalls)
    outs = pl.pallas_call(
        body, name="fwd_mix", grid=(nt,),
        in_specs=[tok(d)] + [full(a) for a in smalls] + [ANY] * nl,
        out_specs=[tok(IN_COLS), tok(d), tok(LRU_W), tok(LRU_W), tok(CONV_W), tok(CONV_W + LRU_W)]
        + [tok(LRU_W)] * 4 + [ANY] * nl,
        out_shape=[act(IN_COLS), act(d, BF16), act(LRU_W), act(LRU_W), act(CONV_W), act(CONV_W + LRU_W, BF16)]
        + [act(LRU_W)] * 4 + [jax.ShapeDtypeStruct(a.shape, a.dtype) for a in later],
        input_output_aliases={n_in + w: N_FWD_OUT + w for w in range(nl)},
        scratch_shapes=[pltpu.VMEM((TILE_ROWS, CONV_W), F32), pltpu.VMEM((TILE_ROWS, LRU_W), F32),
                        pltpu.VMEM((1, LRU_W), F32), pltpu.SemaphoreType.DMA((nl, _ShardGather.PAIRS)),
                        pltpu.SemaphoreType.DMA((nl, _ShardGather.PAIRS))],
        compiler_params=_params(dimension_semantics=("arbitrary",)),
    )(x, *smalls, *later)
    return outs[:N_FWD_OUT], outs[N_FWD_OUT:]


def _mlp_fwd_bwd(x, yb, w_out_g, w1_g, w2_g, g2, gf, target):
    t, d = x.shape
    tm = TOKEN_TILE
    ff = w2_g.shape[0]
    mix = w_out_g.shape[0]
    ffs = ff // N_CHIPS

    def body(x_ref, y_ref, g2_ref, gf_ref, tgt_ref, wout_hbm, w1_hbm, w2_hbm,
             z_ref, dp_ref, h2_ref, dx3b_ref, dx2_ref, dx2b_ref, dy_ref, st_ref, wout, w1, w2, p_ref, w_sems):
        first = pl.program_id(0) == 0
        loads = [pltpu.make_async_copy(src, dst, w_sems.at[n]) for n, (src, dst) in
                 enumerate(((wout_hbm, wout), (w1_hbm, w1), (w2_hbm, w2)))]

        @pl.when(first)
        def _():
            for cp in loads:
                cp.start()
            st_ref[...] = jnp.zeros_like(st_ref)
            loads[0].wait()

        x2 = x_ref[...] + jnp.dot(y_ref[...], wout[...], preferred_element_type=F32)
        r2 = lax.rsqrt(jnp.mean(x2 * x2, axis=-1, keepdims=True) + EPS)
        xh2 = x2 * r2
        g2v = g2_ref[...]
        h2b = (xh2 * g2v).astype(BF16)
        h2_ref[...] = h2b

        @pl.when(first)
        def _():
            loads[1].wait()

        for j in range(N_CHIPS):
            p_ref[:, j * ffs:(j + 1) * ffs] = jnp.dot(h2b, w1[j], preferred_element_type=F32)
        rp = jnp.maximum(p_ref[...], 0.0)
        zb = (rp * rp).astype(BF16)
        z_ref[...] = zb

        @pl.when(first)
        def _():
            loads[2].wait()

        x3 = x2 + jnp.dot(zb, w2[...], preferred_element_type=F32)
        r3 = lax.rsqrt(jnp.mean(x3 * x3, axis=-1, keepdims=True) + EPS)
        xh3 = x3 * r3
        gfv = gf_ref[...]
        err = xh3 * gfv - tgt_ref[...]
        loss = (0.5 / d) * jnp.sum(err * err)
        dout = err * (1.0 / d)
        st_ref[PK_FINAL_G * 8 - 64:PK_FINAL_G * 8 - 63, :] += _colsum(dout * xh3)
        st_ref[PK_LOSS * 8 - 64:PK_LOSS * 8 - 63, :] += jnp.zeros((1, d), F32) + loss
        dxh3 = dout * gfv
        dx3 = r3 * (dxh3 - xh3 * jnp.mean(dxh3 * xh3, axis=-1, keepdims=True))
        dx3b = dx3.astype(BF16)
        dx3b_ref[...] = dx3b
        dpb = (_dot_nt(dx3b, w2[...]) * (2.0 * rp)).astype(BF16)
        dp_ref[...] = dpb
        dh2 = _dot_nt(dpb[:, 0:ffs], w1[0])
        for j in range(1, N_CHIPS):
            dh2 = dh2 + _dot_nt(dpb[:, j * ffs:(j + 1) * ffs], w1[j])
        st_ref[PK_MLP_G * 8 - 64:PK_MLP_G * 8 - 63, :] += _colsum(dh2 * xh2)
        dxh2 = dh2 * g2v
        dx2 = dx3 + r2 * (dxh2 - xh2 * jnp.mean(dxh2 * xh2, axis=-1, keepdims=True))
        dx2_ref[...] = dx2
        dx2b = dx2.astype(BF16)
        dx2b_ref[...] = dx2b
        dy_ref[...] = _dot_nt(dx2b, wout[...])

    def tok(cols):
        return pl.BlockSpec((tm, cols), lambda i: (i, 0))

    def row(cols):
        return pl.BlockSpec((1, cols), lambda i: (0, 0))

    return pl.pallas_call(
        body, name="mlp_fwd_bwd", grid=(t // tm,),
        in_specs=[tok(d), tok(mix), row(d), row(d), tok(d), ANY, ANY, ANY],
        out_specs=[tok(ff), tok(ff), tok(d), tok(d), tok(d), tok(d), tok(mix),
                   pl.BlockSpec((3 * TILE_ROWS, d), lambda i: (0, 0))],
        out_shape=[jax.ShapeDtypeStruct((t, ff), BF16), jax.ShapeDtypeStruct((t, ff), BF16),
                   jax.ShapeDtypeStruct((t, d), BF16), jax.ShapeDtypeStruct((t, d), BF16),
                   jax.ShapeDtypeStruct((t, d), F32), jax.ShapeDtypeStruct((t, d), BF16),
                   jax.ShapeDtypeStruct((t, mix), F32), jax.ShapeDtypeStruct((3 * TILE_ROWS, d), F32)],
        scratch_shapes=[pltpu.VMEM(w_out_g.shape, BF16), pltpu.VMEM(w1_g.shape, BF16), pltpu.VMEM(w2_g.shape, BF16),
                        pltpu.VMEM((tm, ff), F32), pltpu.SemaphoreType.DMA((3,))],
        compiler_params=_params(dimension_semantics=("arbitrary",)),
    )(x, yb, g2, gf, target, w_out_g, w1_g, w2_g)


def _mix_bwd(dy, u, xr_all, hs_all, c3_all, gates, conv_w, rconv_w, wa_bd, wx_bd, lam, g_nc, g_nr, parts):
    t = dy.shape[0]
    tm = TOKEN_TILE
    nt = t // tm
    hb = tm // TILE_ROWS
    npart = len(parts)

    def body(dy_ref, u_ref, uh_ref, xr_ref, hs_ref, hh_ref, c3_ref, r_ref, ig_ref, a_ref, mult_ref,
             cw_ref, rw_ref, wa_ref, wx_ref, lam_ref, gnc_ref, gnr_ref, *rest):
        part_refs, (du_ref, st_ref, dwa_ref, dwx_ref), rest = rest[:npart], rest[npart:npart + 4], rest[npart + 4:]
        arrived_refs, (dc_next, a_next, gs_next, dxr_next, send_sems, recv_sems) = rest[:npart], rest[npart:]
        exchange = _PartialExchange(part_refs, arrived_refs, send_sems, recv_sems)
        i = pl.program_id(0)

        @pl.when(i == 0)
        def _():
            exchange.start()
            dc_next[...] = jnp.zeros_like(dc_next)
            a_next[...] = jnp.zeros_like(a_next)
            gs_next[...] = jnp.zeros_like(gs_next)
            dxr_next[...] = jnp.zeros_like(dxr_next)
            st_ref[...] = jnp.zeros_like(st_ref)
            dwa_ref[...] = jnp.zeros_like(dwa_ref)
            dwx_ref[...] = jnp.zeros_like(dwx_ref)

        first_tile = i == nt - 1
        gate_b = u_ref[:, 0:CONV_W]
        gate_c = u_ref[:, CONV_W:2 * CONV_W]
        v = u_ref[:, 2 * CONV_W:3 * CONV_W]
        x_r = u_ref[:, 3 * CONV_W:3 * CONV_W + LRU_W]
        g = u_ref[:, 3 * CONV_W + LRU_W:]
        cv = gate_c * v
        cv_prev = jnp.where(first_tile, 0.0, uh_ref[:, CONV_W:2 * CONV_W] * uh_ref[:, 2 * CONV_W:3 * CONV_W])
        xin_prev = jnp.where(first_tile, 0.0, uh_ref[:, 3 * CONV_W:3 * CONV_W + LRU_W])
        hs_prev = jnp.where(first_tile, 0.0, hh_ref[...])

        def acc(block, val, width=LRU_W, row=0):
            r0 = block * TILE_ROWS + row
            st_ref[r0:r0 + 1, 0:width] += val

        conv3 = c3_ref[...]
        y_conv = gate_b * conv3
        ra = lax.rsqrt(jnp.mean(y_conv * y_conv, axis=-1, keepdims=True) + EPS)
        xha = y_conv * ra
        dna = dy_ref[:, :CONV_W]
        acc(PK_G_NORM_CONV, _colsum(dna * xha), CONV_W)
        dxha = dna * gnc_ref[...]
        dy_conv = ra * (dxha - xha * jnp.mean(dxha * xha, axis=-1, keepdims=True))
        du_ref[:, 0:CONV_W] = (dy_conv * conv3).astype(BF16)
        dc = dy_conv * gate_b
        cw = cw_ref[...]
        dcn = dc_next[...]
        dcv = cw[2:3] * dc + cw[1:2] * _shift_up(dc, 1, dcn) + cw[0:1] * _shift_up(dc, 2, dcn)
        dc_next[...] = dc[:TILE_ROWS]
        acc(PK_CONV_W, _colsum(dc * _shift_down(cv, 2, cv_prev)), CONV_W, 0)
        acc(PK_CONV_W, _colsum(dc * _shift_down(cv, 1, cv_prev)), CONV_W, 1)
        acc(PK_CONV_W, _colsum(dc * cv), CONV_W, 2)
        du_ref[:, CONV_W:2 * CONV_W] = (dcv * v).astype(BF16)
        du_ref[:, 2 * CONV_W:3 * CONV_W] = (dcv * gate_c).astype(BF16)

        hs = hs_ref[...]
        gelu, dgelu = _gelu_and_grad(g)
        y_rnn = hs * gelu
        rb = lax.rsqrt(jnp.mean(y_rnn * y_rnn, axis=-1, keepdims=True) + EPS)
        xhb = y_rnn * rb
        dnb = dy_ref[:, CONV_W:]
        acc(PK_G_NORM_RNN, _colsum(dnb * xhb))
        dxhb = dnb * gnr_ref[...]
        dy_rnn = rb * (dxhb - xhb * jnp.mean(dxhb * xhb, axis=-1, keepdims=True))
        du_ref[:, 3 * CONV_W + LRU_W:] = (dy_rnn * hs * dgelu).astype(BF16)
        dh = dy_rnn * gelu

        xr = xr_ref[...]
        xrb = xr.astype(BF16)
        sp, dsp = _softplus_neg(lam_ref[...])
        r, ig, a, mult = r_ref[...], ig_ref[...], a_ref[...], mult_ref[...]
        a_up = _shift_up(a, 1, a_next[...])
        a_next[...] = a[:TILE_ROWS]
        gs = _scan_rows(a_up, dh, gs_next[0:1, :], reverse=True)
        gs_next[...] = gs[:TILE_ROWS]
        da = gs * _shift_down(hs, 1, hs_prev)
        gx = gs * xr
        di = gx * mult
        dmult = gx * ig
        dxr = gs * (mult * ig)
        dlog_a = da * a - dmult * ((a * a) / mult)
        acc(PK_LAMBDA, _colsum(dlog_a * r) * ((-LRU_C) * dsp))
        dpa = (dlog_a * ((-LRU_C) * sp)) * (r * (1.0 - r))
        dpx = di * (ig * (1.0 - ig))
        acc(PK_B_A, _colsum(dpa))
        acc(PK_B_X, _colsum(dpx))
        dpab = dpa.astype(BF16)
        dpxb = dpx.astype(BF16)
        dxr = dxr + _block_diag_dot_t(dpab, wa_ref) + _block_diag_dot_t(dpxb, wx_ref)
        for j in range(N_BD):
            cols = slice(j * BD, (j + 1) * BD)
            dwa_ref[j] += _dot_tn(xrb[:, cols], dpab[:, cols])
            dwx_ref[j] += _dot_tn(xrb[:, cols], dpxb[:, cols])

        acc(PK_RCONV_B, _colsum(dxr))
        rw = rw_ref[...]
        dxn = dxr_next[...]
        dx_r = (rw[3:4] * dxr + rw[2:3] * _shift_up(dxr, 1, dxn) + rw[1:2] * _shift_up(dxr, 2, dxn)
                + rw[0:1] * _shift_up(dxr, 3, dxn))
        dxr_next[...] = dxr[:TILE_ROWS]
        for k in range(3):
            acc(PK_RCONV_W, _colsum(dxr * _shift_down(x_r, 3 - k, xin_prev)), LRU_W, k)
        acc(PK_RCONV_W, _colsum(dxr * x_r), LRU_W, 3)
        du_ref[:, 3 * CONV_W:3 * CONV_W + LRU_W] = dx_r.astype(BF16)

        @pl.when(i == nt - 1)
        def _():
            exchange.wait()

    def full(a):
        nd = a.ndim
        return pl.BlockSpec(a.shape, lambda i: (0,) * nd)

    def tok(cols):
        return pl.BlockSpec((tm, cols), lambda i: (nt - 1 - i, 0))

    def halo(cols):
        return pl.BlockSpec((TILE_ROWS, cols), lambda i: (jnp.maximum((nt - 1 - i) * hb - 1, 0), 0))

    smalls = (conv_w, rconv_w, wa_bd, wx_bd, lam, g_nc, g_nr)
    outs = pl.pallas_call(
        body, name="mix_bwd", grid=(nt,),
        in_specs=[tok(CONV_W + LRU_W), tok(IN_COLS), halo(IN_COLS), tok(LRU_W), tok(LRU_W), halo(LRU_W), tok(CONV_W)]
        + [tok(LRU_W)] * 4 + [full(a) for a in smalls] + [ANY] * npart,
        out_specs=[tok(IN_COLS), pl.BlockSpec((8 * TILE_ROWS, LRU_W), lambda i: (0, 0)),
                   pl.BlockSpec((N_BD, BD, BD), lambda i: (0, 0, 0)), pl.BlockSpec((N_BD, BD, BD), lambda i: (0, 0, 0))]
        + [ANY] * npart,
        out_shape=[jax.ShapeDtypeStruct((t, IN_COLS), BF16), jax.ShapeDtypeStruct((8 * TILE_ROWS, LRU_W), F32),
                   jax.ShapeDtypeStruct((N_BD, BD, BD), F32), jax.ShapeDtypeStruct((N_BD, BD, BD), F32)]
        + [jax.ShapeDtypeStruct(a.shape, a.dtype) for a in parts],
        scratch_shapes=[pltpu.VMEM((TILE_ROWS, CONV_W), F32), pltpu.VMEM((TILE_ROWS, LRU_W), F32),
                        pltpu.VMEM((TILE_ROWS, LRU_W), F32), pltpu.VMEM((TILE_ROWS, LRU_W), F32),
                        pltpu.SemaphoreType.DMA((npart, 3)), pltpu.SemaphoreType.DMA((npart, 3))],
        compiler_params=_params(dimension_semantics=("arbitrary",)),
    )(dy, u, u, xr_all, hs_all, hs_all, c3_all, *gates, *smalls, *parts)
    return outs[:4], outs[4:]


def _in_bwd(dub, w_in_g, x, dx2, g1, parts, joins, core_chip):
    t, d = x.shape
    tm = TOKEN_TILE
    nt = t // tm
    npart = len(parts)
    nj = len(joins)
    geometry = []
    for tag, shape, _, _ in joins:
        pr, pc = WGRAD_GEOMETRY[tag][:2]
        every = 1 if pr % (nt * 16) == 0 else 2
        geometry.append((pr, pc, pr * every // nt, every, shape[1] == pc))

    def body(cc_ref, du_ref, win_ref, x_ref, dx2_ref, g1_ref, *rest):
        sums, rest = [rest[4 * w:4 * w + 4] for w in range(nj)], rest[4 * nj:]
        part_refs, (gx_ref, st_ref), rest = rest[:npart], rest[npart:npart + 2], rest[npart + 2:]
        arrived_refs, joined, rest = rest[:npart], rest[npart:npart + nj], rest[npart + nj:]
        stages, (send_sems, recv_sems, j_local, j_send, j_recv) = rest[:nj], rest[nj:]
        exchange = _PartialExchange(part_refs, arrived_refs, send_sems, recv_sems)
        i = pl.program_id(0)
        c = cc_ref[0]

        def window(w, core, row0, rows):
            pr, pc, _, _, by_rows = geometry[w]
            if by_rows:
                return joined[w].at[pl.ds(core * pr + row0, rows), :]
            return joined[w].at[pl.ds(row0, rows), pl.ds(core * pc, pc)]

        def to_sibling(w, src, core, row0, rows):
            return pltpu.make_async_remote_copy(src_ref=src, dst_ref=window(w, core, row0, rows), send_sem=j_send.at[w],
                                                recv_sem=j_recv.at[w], device_id=_sibling(), device_id_type=MESH)

        @pl.when(i == 0)
        def _():
            exchange.start()
            st_ref[...] = jnp.zeros_like(st_ref)

        for w in range(nj):
            pr, pc, rb, every, _ = geometry[w]

            @pl.when(i % every == 0)
            def _(w=w, rb=rb, every=every):
                p_ref, r1_ref, r2_ref, r3_ref = sums[w]
                row0 = pl.multiple_of((i // every) * rb, rb)
                rows = stages[w].at[pl.ds(row0, rb), :]
                rows[...] = ((p_ref[0] + r1_ref[0].astype(F32)) + r2_ref[0].astype(F32)) + r3_ref[0].astype(F32)
                pltpu.make_async_copy(rows, window(w, c, row0, rb), j_local.at[w]).start()
                to_sibling(w, rows, c, row0, rb).start()

        dh1 = _dot_nt(du_ref[:, 0:IN_SHARD], win_ref[0])
        for j in range(1, N_CHIPS):
            dh1 = dh1 + _dot_nt(du_ref[:, j * IN_SHARD:(j + 1) * IN_SHARD], win_ref[j])
        xv = x_ref[...]
        rstd = lax.rsqrt(jnp.mean(xv * xv, axis=-1, keepdims=True) + EPS)
        xh = xv * rstd
        st_ref[0:1, :] += _colsum(dh1 * xh)
        dxh = dh1 * g1_ref[...]
        gx_ref[...] = dx2_ref[...] + rstd * (dxh - xh * jnp.mean(dxh * xh, axis=-1, keepdims=True))

        @pl.when(i == nt - 1)
        def _():
            exchange.wait()
            for w in range(nj):
                pr = geometry[w][0]
                pltpu.make_async_copy(stages[w], window(w, c, 0, pr), j_local.at[w]).wait()
                to_sibling(w, stages[w], 1 - c, 0, pr).wait()

    def tok(cols):
        return pl.BlockSpec((tm, cols), lambda i, cc: (i, 0))

    def partial(w, off):
        pr, pc, rb, every, _ = geometry[w]
        return pl.BlockSpec((1, rb, pc), lambda i, cc: ((cc[1] + off) % N_CHIPS, i // every, 0))

    sum_specs, sum_operands = [], []
    for w, (_, _, own, arrived) in enumerate(joins):
        sum_specs += [partial(w, off) for off in range(N_CHIPS)]
        sum_operands += [own, arrived, arrived, arrived]
    dma = pltpu.SemaphoreType.DMA
    outs = pl.pallas_call(
        body, name="in_bwd",
        grid_spec=pltpu.PrefetchScalarGridSpec(
            num_scalar_prefetch=1, grid=(nt,),
            in_specs=[tok(IN_COLS), pl.BlockSpec(w_in_g.shape, lambda i, cc: (0, 0, 0)), tok(d), tok(d),
                      pl.BlockSpec((1, d), lambda i, cc: (0, 0))] + sum_specs + [ANY] * npart,
            out_specs=[tok(d), pl.BlockSpec((TILE_ROWS, d), lambda i, cc: (0, 0))] + [ANY] * (npart + nj),
            scratch_shapes=[pltpu.VMEM((g[0], g[1]), F32) for g in geometry]
            + [dma((npart, 3)), dma((npart, 3)), dma((nj,)), dma((nj,)), dma((nj,))]),
        out_shape=[jax.ShapeDtypeStruct((t, d), F32), jax.ShapeDtypeStruct((TILE_ROWS, d), F32)]
        + [jax.ShapeDtypeStruct(a.shape, a.dtype) for a in parts]
        + [jax.ShapeDtypeStruct(shape, F32) for _, shape, _, _ in joins],
        compiler_params=_params(dimension_semantics=("arbitrary",)),
    )(core_chip, dub, w_in_g, x, dx2, g1, *sum_operands, *parts)
    return outs[:2], outs[2:2 + npart], outs[2 + npart:]


WGRAD_GEOMETRY = {
    "in": (512, IN_SHARD, lambda s, h: h, lambda s, h: s),
    "mlp_in": (512, D_MODEL, lambda s, h: h, lambda s, h: s),
    "mlp_out": (512, D_MODEL, lambda s, h: 2 * s + h, lambda s, h: 0),
    "out": (384, 512, lambda s, h: s, lambda s, h: h),
}
K_CHUNK = 512


def _sibling():
    x, y, c = _position()
    return (x, y, 1 - c)


def _wgrad(a, b, tag, core_chip, packs=(), parts=()):
    t = a.shape[0]
    pr, pc, a_blk, b_blk = WGRAD_GEOMETRY[tag]
    nk = t // K_CHUNK
    mine = N_CHIPS
    riding = len(packs)
    npart = len(parts)
    assert not (riding and npart)

    def body(cc_ref, a_ref, b_ref, *rest):
        if riding:
            pack_refs, (land_ref, p_ref, pb_ref), rest = rest[:riding], rest[riding:riding + 3], rest[riding + 3:]
            all_refs, (stage, rbuf, send_sems, recv_sems, rsem), g_sems = rest[:riding], rest[riding:riding + 5], rest[riding + 5:]
            gathers = [_PackGather(pack_refs[n], all_refs[n], *g_sems[3 * n:3 * n + 3]) for n in range(riding)]
        elif npart:
            part_refs, (land_ref, p_ref, pb_ref), rest = rest[:npart], rest[npart:npart + 3], rest[npart + 3:]
            arrived_refs, (stage, rbuf, send_sems, recv_sems, rsem, x_send, x_recv) = rest[:npart], rest[npart:]
            exchange = _PartialExchange(part_refs, arrived_refs, x_send, x_recv)
        else:
            land_ref, p_ref, pb_ref, stage, rbuf, send_sems, recv_sems, rsem = rest
        ph, s = pl.program_id(0), pl.program_id(1)
        if riding:
            @pl.when((ph == 0) & (s == 0))
            def _():
                for gather in gathers:
                    gather.start()

            @pl.when((ph == 1) & (s == N_CHIPS - 2))
            def _():
                for gather in gathers:
                    gather.hand_over()
        if npart:
            @pl.when((ph == 0) & (s == 0))
            def _():
                exchange.start()
        slot = jnp.where(ph == 0, s, mine)
        acc = stage.at[slot]
        acc[...] = _dot_tn(a_ref[0:K_CHUNK, :], b_ref[0:K_CHUNK, :])
        for k in range(1, nk):
            acc[...] += _dot_tn(a_ref[k * K_CHUNK:(k + 1) * K_CHUNK, :], b_ref[k * K_CHUNK:(k + 1) * K_CHUNK, :])

        def push(k):
            return pltpu.make_async_remote_copy(src_ref=stage.at[k], dst_ref=land_ref.at[k], send_sem=send_sems.at[k],
                                                recv_sem=recv_sems.at[k], device_id=_sibling(), device_id_type=MESH)

        @pl.when(ph == 0)
        def _():
            push(s).start()

        @pl.when(ph == 1)
        def _():
            push(s).wait_recv()
            landed = pltpu.make_async_copy(land_ref.at[s], rbuf, rsem)
            landed.start()
            landed.wait()
            p = stage[mine] + rbuf[...]
            p_ref[0] = p
            pb_ref[0] = p.astype(BF16)

        @pl.when((ph == 1) & (s == N_CHIPS - 1))
        def _():
            for k in range(N_CHIPS):
                push(k).wait_send()
            for gather in (gathers if riding else ()):
                gather.finish()
            if npart:
                exchange.wait()

    def half(ph, cc):
        return jnp.where(ph == 0, 1 - cc[0], cc[0])

    def out_slot(ph, s, cc):
        return (jnp.where(ph == 0, 0, s), 0, 0)

    piece = jax.ShapeDtypeStruct((N_CHIPS, pr, pc), F32)
    in_specs = [pl.BlockSpec((t, pr), lambda ph, s, cc: (0, a_blk(s, half(ph, cc)))),
                pl.BlockSpec((t, pc), lambda ph, s, cc: (0, b_blk(s, half(ph, cc))))]
    out_specs = [ANY, pl.BlockSpec((1, pr, pc), out_slot), pl.BlockSpec((1, pr, pc), out_slot)]
    out_shape = [piece, piece, jax.ShapeDtypeStruct((N_CHIPS, pr, pc), BF16)]
    scratch = [pltpu.VMEM((N_CHIPS + 1, pr, pc), F32), pltpu.VMEM((pr, pc), F32),
               pltpu.SemaphoreType.DMA((N_CHIPS,)), pltpu.SemaphoreType.DMA((N_CHIPS,)), pltpu.SemaphoreType.DMA]
    operands = [a, b]
    for pack in packs:
        in_specs.append(pl.BlockSpec(pack.shape, lambda ph, s, cc: (0, 0)))
        out_specs.append(ANY)
        out_shape.append(jax.ShapeDtypeStruct((N_DEVICES,) + pack.shape, pack.dtype))
        operands.append(pack)
    for pack in packs:
        scratch += _PackGather.semaphores()
    if npart:
        in_specs += [ANY] * npart
        out_specs += [ANY] * npart
        out_shape += [jax.ShapeDtypeStruct(p.shape, p.dtype) for p in parts]
        scratch += [pltpu.SemaphoreType.DMA((npart, 3)), pltpu.SemaphoreType.DMA((npart, 3))]
        operands += list(parts)
    return pl.pallas_call(
        body, name="wgrad_" + tag,
        grid_spec=pltpu.PrefetchScalarGridSpec(
            num_scalar_prefetch=1, grid=(2, N_CHIPS), in_specs=in_specs, out_specs=out_specs, scratch_shapes=scratch),
        out_shape=out_shape,
        compiler_params=_params(dimension_semantics=("arbitrary", "arbitrary")),
    )(core_chip, *operands)[1:]


def _other_chips(x, y):
    return [(1 - x, y), (x, 1 - y), (1 - x, 1 - y)]


class _ShardGather:
    PAIRS = 9

    def __init__(self, outs, send_sems, recv_sems):
        self.outs, self.send_sems, self.recv_sems = outs, send_sems, recv_sems
        x, y, c = _position()
        self.c, self.j = c, 2 * x + y
        self.sibling = (x, y, 1 - c)
        self.chips = _other_chips(x, y)

    def _chip(self, k):
        px, py = self.chips[k]
        return 2 * px + py

    def _half(self, w, chip, which):
        hr = self.outs[w].shape[1] // 2
        return self.outs[w].at[chip, pl.ds(which * hr, hr), :]

    def _quarter(self, w, chip, q):
        qr = self.outs[w].shape[1] // 4
        return self.outs[w].at[chip, pl.ds(self.c * 2 * qr + q * qr, qr), :]

    def _copy(self, ref, w, pair, to, src=None):
        return pltpu.make_async_remote_copy(src_ref=ref if src is None else src, dst_ref=ref, send_sem=self.send_sems.at[w, pair],
                                            recv_sem=self.recv_sems.at[w, pair], device_id=to, device_id_type=MESH)

    def direct(self, w, k, q, src=None):
        return self._copy(self._quarter(w, self.j, q), w, 2 * k + q, (*self.chips[k], self.c), src)

    def direct_landed(self, w, k, q):
        return self._copy(self._quarter(w, self._chip(k), q), w, 2 * k + q, (*self.chips[k], self.c))

    def pass_on(self, w, q):
        return self._copy(self._quarter(w, self._chip(q), q), w, 4 + q, (*self.chips[1 - q], self.c))

    def passed_landed(self, w, q):
        return self._copy(self._quarter(w, self._chip(2), q), w, 4 + q, (*self.chips[1 - q], self.c))

    def hand_over(self, w, k):
        return self._copy(self._half(w, self._chip(k), self.c), w, 6 + k, self.sibling)

    def handed(self, w, k):
        return self._copy(self._half(w, self._chip(k), 1 - self.c), w, 6 + k, self.sibling)

    def start_direct(self, w, src_half=None):
        qr = self.outs[w].shape[1] // 4
        for k, q in ((0, 0), (1, 1), (0, 1), (1, 0)):
            self.direct(w, k, q, None if src_half is None else src_half.at[pl.ds(q * qr, qr), :]).start()

    def start_pass_on(self, w):
        for q in (0, 1):
            self.direct_landed(w, q, q).wait_recv()
            self.pass_on(w, q).start()

    def start_hand_over(self, w, diagonal):
        if diagonal:
            for q in (0, 1):
                self.passed_landed(w, q).wait_recv()
            self.hand_over(w, 2).start()
        else:
            for k in (0, 1):
                self.direct_landed(w, k, 1 - k).wait_recv()
                self.hand_over(w, k).start()

    def finish(self, w):
        for k in range(3):
            self.handed(w, k).wait_recv()
            self.hand_over(w, k).wait_send()
        for q in (0, 1):
            self.pass_on(w, q).wait_send()
            for k in (0, 1):
                self.direct(w, k, q).wait_send()


def _gather_first(w_in, w_out, w1, w2, small):
    bigs = (w_in, w_out, w1, w2)
    nb = len(bigs)

    def body(win_ref, wout_ref, w1_ref, w2_ref, sm_ref, gin, gout, g1, g2, gsm, st_in, st_out, st_1, st_2,
             send_sems, recv_sems, sm_send, sm_recv, local_sems):
        srcs = (win_ref, wout_ref, w1_ref, w2_ref)
        stages = (st_in, st_out, st_1, st_2)
        outs = (gin, gout, g1, g2)
        plan = _ShardGather(outs[:1], send_sems, recv_sems)
        j, c = plan.j, plan.c
        for src, st in zip(srcs, stages):
            st[...] = src[...].astype(BF16)
        local = [pltpu.make_async_copy(stages[w], outs[w].at[j], local_sems.at[w]) for w in range(nb)]
        local.append(pltpu.make_async_copy(sm_ref, gsm.at[j], local_sems.at[nb]))
        for cp in local:
            cp.start()

        def small_copy(k):
            px, py = plan.chips[k]
            return pltpu.make_async_remote_copy(src_ref=sm_ref, dst_ref=gsm.at[j], send_sem=sm_send.at[k],
                                                recv_sem=sm_recv.at[k], device_id=(px, py, c), device_id_type=MESH)

        def small_landed(k):
            px, py = plan.chips[k]
            return pltpu.make_async_remote_copy(src_ref=sm_ref, dst_ref=gsm.at[2 * px + py], send_sem=sm_send.at[k],
                                                recv_sem=sm_recv.at[k], device_id=(px, py, c), device_id_type=MESH)

        hr = w_in.shape[0] // 2
        plan.start_direct(0, st_in.at[pl.ds(c * hr, hr), :])
        for k in range(3):
            small_copy(k).start()
        plan.start_pass_on(0)
        plan.start_hand_over(0, diagonal=False)
        plan.start_hand_over(0, diagonal=True)
        for k in range(3):
            small_landed(k).wait_recv()
            small_copy(k).wait_send()
        plan.finish(0)
        for cp in local:
            cp.wait()

    def gathered(a, dtype):
        return jax.ShapeDtypeStruct((N_CHIPS,) + a.shape, dtype)

    return pl.pallas_call(
        body, name="gather_first",
        in_specs=[VMEM] * 5, out_specs=[ANY] * 5,
        out_shape=[gathered(a, BF16) for a in bigs] + [gathered(small, F32)],
        scratch_shapes=[pltpu.VMEM(a.shape, BF16) for a in bigs]
        + [pltpu.SemaphoreType.DMA((1, _ShardGather.PAIRS)), pltpu.SemaphoreType.DMA((1, _ShardGather.PAIRS)), pltpu.SemaphoreType.DMA((3,)),
           pltpu.SemaphoreType.DMA((3,)), pltpu.SemaphoreType.DMA((nb + 1,))],
        compiler_params=_params(),
    )(*bigs, small)


class _PartialExchange:
    def __init__(self, parts, arrived, send_sems, recv_sems):
        self.parts, self.arrived, self.send_sems, self.recv_sems = parts, arrived, send_sems, recv_sems
        x, y, c = _position()
        self.c, self.j = c, 2 * x + y
        self.chips = _other_chips(x, y)

    def _copy(self, w, k, slot):
        px, py = self.chips[k]
        return pltpu.make_async_remote_copy(
            src_ref=self.parts[w].at[2 * px + py], dst_ref=self.arrived[w].at[slot], send_sem=self.send_sems.at[w, k],
            recv_sem=self.recv_sems.at[w, k], device_id=(px, py, self.c), device_id_type=MESH)

    def start(self):
        for w in range(len(self.parts)):
            for k in range(3):
                self._copy(w, k, self.j).start()

    def wait(self):
        for w in range(len(self.parts)):
            for k in range(3):
                px, py = self.chips[k]
                self._copy(w, k, 2 * px + py).wait()


class _PackGather:
    def __init__(self, p_ref, all_ref, send_sems, recv_sems, local_sem):
        self.p_ref, self.all_ref, self.send_sems, self.recv_sems, self.local_sem = p_ref, all_ref, send_sems, recv_sems, local_sem
        x, y, c = _position()
        self.me, self.sibling, self.c = (x, y, c), (x, y, 1 - c), c
        self.chips = _other_chips(x, y)

    @staticmethod
    def semaphores():
        return [pltpu.SemaphoreType.DMA((7,)), pltpu.SemaphoreType.DMA((7,)), pltpu.SemaphoreType.DMA]

    def _copy(self, k, block, to, from_pack=False):
        px, py, pc = block
        slot = self.all_ref.at[4 * px + 2 * py + pc]
        return pltpu.make_async_remote_copy(src_ref=self.p_ref if from_pack else slot, dst_ref=slot, send_sem=self.send_sems.at[k],
                                            recv_sem=self.recv_sems.at[k], device_id=to, device_id_type=MESH)

    def _mine(self):
        x, y, c = self.me
        return pltpu.make_async_copy(self.p_ref, self.all_ref.at[4 * x + 2 * y + c], self.local_sem)

    def _first(self):
        return [self._copy(0, self.me, self.sibling, True)] + [
            self._copy(1 + k, self.me, (*chip, self.c), True) for k, chip in enumerate(self.chips)]

    def _passed(self):
        return [self._copy(4 + k, (*chip, self.c), self.sibling) for k, chip in enumerate(self.chips)]

    def start(self):
        self._mine().start()
        for cp in self._first():
            cp.start()

    def hand_over(self):
        for k, chip in enumerate(self.chips):
            self._copy(1 + k, (*chip, self.c), self.me).wait_recv()
            self._passed()[k].start()

    def finish(self):
        self._copy(0, self.sibling, self.me).wait_recv()
        for k, chip in enumerate(self.chips):
            self._copy(4 + k, (*chip, 1 - self.c), self.me).wait_recv()
        for cp in self._first() + self._passed():
            cp.wait_send()
        self._mine().wait()


class _DirectGather:
    def __init__(self, p_ref, all_ref, send_sems, recv_sems, local_sem):
        self.p_ref, self.all_ref, self.send_sems, self.recv_sems, self.local_sem = p_ref, all_ref, send_sems, recv_sems, local_sem
        self.me = _position()

    semaphores = _PackGather.semaphores

    def _peer(self, r):
        x, y, c = self.me
        return ((1 - x) if r & 4 else x, (1 - y) if r & 2 else y, (1 - c) if r & 1 else c)

    def _copy(self, r, slot_of):
        px, py, pc = slot_of
        return pltpu.make_async_remote_copy(src_ref=self.p_ref, dst_ref=self.all_ref.at[4 * px + 2 * py + pc],
                                            send_sem=self.send_sems.at[r - 1], recv_sem=self.recv_sems.at[r - 1],
                                            device_id=self._peer(r), device_id_type=MESH)

    def _mine(self):
        x, y, c = self.me
        return pltpu.make_async_copy(self.p_ref, self.all_ref.at[4 * x + 2 * y + c], self.local_sem)

    def start(self):
        self._mine().start()
        for r in range(1, N_DEVICES):
            self._copy(r, self.me).start()

    def finish(self):
        for r in range(1, N_DEVICES):
            self._copy(r, self._peer(r)).wait()
        self._mine().wait()


def _adamw(w, g, m, v):
    m = ADAM_B1 * m + (1.0 - ADAM_B1) * g
    v = ADAM_B2 * v + (1.0 - ADAM_B2) * (g * g)
    m_hat = m / ADAM_BC1
    v_hat = v / ADAM_BC2
    delta = -ADAM_LR * (m_hat / (jnp.sqrt(v_hat) + ADAM_EPS) + ADAM_WD * w)
    return delta, m, v


JOIN_SUB = 4


def _join(tag, shard_shape, part, arrived, core_chip, block=None):
    pr, pc = WGRAD_GEOMETRY[tag][:2]
    rb = pr // JOIN_SUB
    by_rows = shard_shape[1] == pc
    riding = block is not None

    def body(cc_ref, p_ref, r1_ref, r2_ref, r3_ref, *rest):
        if riding:
            blk_ref, g_ref, all_ref, stage, send_sems, recv_sems, local_sems, b_send, b_recv, b_local = rest
            gather = _DirectGather(blk_ref, all_ref, b_send, b_recv, b_local)
        else:
            g_ref, stage, send_sems, recv_sems, local_sems = rest
        i = pl.program_id(0)
        c = cc_ref[0]
        if riding:
            @pl.when(i == 0)
            def _():
                gather.start()

        def window(core, k):
            if by_rows:
                return g_ref.at[pl.ds((core * JOIN_SUB + k) * rb, rb), :]
            return g_ref.at[pl.ds(k * rb, rb), pl.ds(core * pc, pc)]

        def keep(k):
            return pltpu.make_async_copy(stage.at[k], window(c, k), local_sems.at[k])

        def push(k):
            return pltpu.make_async_remote_copy(src_ref=stage.at[k], dst_ref=window(c, k), send_sem=send_sems.at[k],
                                                recv_sem=recv_sems.at[k], device_id=_sibling(), device_id_type=MESH)

        def pushed(k):
            return pltpu.make_async_remote_copy(src_ref=stage.at[k], dst_ref=window(1 - c, k), send_sem=send_sems.at[k],
                                                recv_sem=recv_sems.at[k], device_id=_sibling(), device_id_type=MESH)

        stage[i] = ((p_ref[0] + r1_ref[0].astype(F32)) + r2_ref[0].astype(F32)) + r3_ref[0].astype(F32)
        keep(i).start()
        push(i).start()

        @pl.when(i == JOIN_SUB - 1)
        def _():
            for k in range(JOIN_SUB):
                keep(k).wait()
                push(k).wait_send()
                pushed(k).wait_recv()
            if riding:
                gather.finish()

    def partial(off):
        return pl.BlockSpec((1, rb, pc), lambda i, cc: ((cc[1] + off) % N_CHIPS, i, 0))

    in_specs = [partial(0), partial(1), partial(2), partial(3)]
    out_specs = [ANY]
    out_shape = [jax.ShapeDtypeStruct(shard_shape, F32)]
    scratch = [pltpu.VMEM((JOIN_SUB, rb, pc), F32), pltpu.SemaphoreType.DMA((JOIN_SUB,)),
               pltpu.SemaphoreType.DMA((JOIN_SUB,)), pltpu.SemaphoreType.DMA((JOIN_SUB,))]
    operands = [part, arrived, arrived, arrived]
    if riding:
        in_specs.append(pl.BlockSpec(block.shape, lambda i, cc: (0, 0)))
        out_specs.append(ANY)
        out_shape.append(jax.ShapeDtypeStruct((N_DEVICES,) + block.shape, block.dtype))
        scratch += _DirectGather.semaphores()
        operands.append(block)
    outs = pl.pallas_call(
        body, name="join_" + tag,
        grid_spec=pltpu.PrefetchScalarGridSpec(
            num_scalar_prefetch=1, grid=(JOIN_SUB,), in_specs=in_specs, out_specs=out_specs, scratch_shapes=scratch),
        out_shape=out_shape,
        compiler_params=_params(dimension_semantics=("arbitrary",)),
    )(core_chip, *operands)
    return outs if riding else outs[0]


def _adamw_big(w, g, m, v, name):
    rows, cols = w.shape
    rb = 256 if rows % 256 == 0 else rows

    def body(w_ref, g_ref, m_ref, v_ref, go_ref, d_ref, nm_ref, nv_ref):
        g = g_ref[...]
        go_ref[...] = g
        d_ref[...], nm_ref[...], nv_ref[...] = _adamw(w_ref[...], g, m_ref[...], v_ref[...])

    spec = pl.BlockSpec((rb, cols), lambda i: (i, 0))
    return pl.pallas_call(
        body, name=name, grid=(rows // rb,), in_specs=[spec] * 4, out_specs=[spec] * 4,
        out_shape=[jax.ShapeDtypeStruct(w.shape, F32)] * 4,
        compiler_params=_params(dimension_semantics=("arbitrary",)),
    )(w, g, m, v)


def _small_step(vec_packs, mat_packs, mix_g_blocks, w_pack, m_pack, v_pack, conv_wmv, rconv_wmv):
    vec_rows = vec_packs.shape[1]
    rows, cols = vec_rows + mat_packs.shape[1], vec_packs.shape[2]
    cshard = conv_wmv.shape[2]
    rshard = rconv_wmv.shape[2]
    mix_row = PK_MIX_G * TILE_ROWS

    def body(vec_ref, mat_ref, blk_ref, w_ref, m_ref, v_ref, cw_ref, rw_ref, g_ref, d_ref, nm_ref, nv_ref, co_ref, ro_ref):
        total = vec_ref[0]
        mats = mat_ref[0].astype(F32)
        late = blk_ref[0]
        for k in range(1, N_DEVICES):
            total = total + vec_ref[k]
            mats = mats + mat_ref[k].astype(F32)
            late = late + blk_ref[k]
        g_ref[0:vec_rows, :] = total
        g_ref[vec_rows:, :] = mats
        g_ref[mix_row:mix_row + TILE_ROWS, :] = late
        g = g_ref[...]
        d_ref[...], nm_ref[...], nv_ref[...] = _adamw(w_ref[...], g, m_ref[...], v_ref[...])

        x, y, _ = _position()
        j = 2 * x + y
        cblk = total[PK_CONV_W * 8:PK_CONV_W * 8 + 8, :]
        rblk = total[PK_RCONV_W * 8:PK_RCONV_W * 8 + 8, :]
        cg = cblk[:, 0:cshard]
        rg = rblk[:, 0:rshard]
        for k in range(1, N_CHIPS):
            cg = jnp.where(j == k, cblk[:, k * cshard:(k + 1) * cshard], cg)
            rg = jnp.where(j == k, rblk[:, k * rshard:(k + 1) * rshard], rg)
        co_ref[0] = cg
        co_ref[1], co_ref[2], co_ref[3] = _adamw(cw_ref[0], cg, cw_ref[1], cw_ref[2])
        ro_ref[0] = rg
        ro_ref[1], ro_ref[2], ro_ref[3] = _adamw(rw_ref[0], rg, rw_ref[1], rw_ref[2])

    pack = [jax.ShapeDtypeStruct((rows, cols), F32)] * 4
    return pl.pallas_call(
        body, name="small_grads_step", in_specs=[VMEM] * 8, out_specs=[VMEM] * 6,
        out_shape=pack + [jax.ShapeDtypeStruct((4, TILE_ROWS, cshard), F32), jax.ShapeDtypeStruct((4, TILE_ROWS, rshard), F32)],
        compiler_params=_params(),
    )(vec_packs, mat_packs, mix_g_blocks, w_pack, m_pack, v_pack, conv_wmv, rconv_wmv)


def _blk(a):
    a = a.reshape(-1, a.shape[-1])
    return jnp.pad(a, ((0, TILE_ROWS - a.shape[0]), (0, D_MODEL - a.shape[1])))


def _zero_blk():
    return jnp.zeros((TILE_ROWS, D_MODEL), F32)


def _pack_params(p, pre):
    get = lambda n: p[pre + n]
    return jnp.concatenate([
        _blk(get("g_norm_rnn")), _blk(get("rnn_conv_b")), _blk(get("b_a")), _blk(get("b_x")), _blk(get("lru_lambda")),
        _zero_blk(), _zero_blk(), _blk(get("g_norm_conv")), _blk(get("final_norm_g").reshape(1, -1)), _blk(get("norm_mlp_g")),
        _zero_blk(), _blk(get("norm_mix_g")), get("w_a").reshape(64, D_MODEL), get("w_x").reshape(64, D_MODEL)], axis=0)


def _to_block_diag(w):
    w4 = w.reshape(N_BD, 4, 64, 64)
    eye = jnp.eye(4, dtype=w.dtype)
    return (w4[:, :, :, None, :] * eye[None, :, None, :, None]).reshape(N_BD, BD, BD)


def _from_block_diag(d):
    d5 = d.reshape(N_BD, 4, 64, 4, 64)
    return jnp.stack([d5[:, q, :, q, :] for q in range(4)], axis=1).reshape(64, D_MODEL)


def _pad_rows(a):
    return jnp.pad(a, ((0, TILE_ROWS - a.shape[0]), (0, 0)))


_NAMES = ['norm_mix_g', 'w_in', 'conv_w', 'rnn_conv_w', 'rnn_conv_b', 'w_a', 'b_a', 'w_x', 'b_x', 'lru_lambda',
          'g_norm_conv', 'g_norm_rnn', 'w_out', 'norm_mlp_g', 'w_mlp_in', 'w_mlp_out', 'final_norm_g']


def kernel(x, norm_mix_g, w_in, conv_w, rnn_conv_w, rnn_conv_b, w_a, b_a, w_x, b_x, lru_lambda, g_norm_conv, g_norm_rnn, w_out, norm_mlp_g, w_mlp_in, w_mlp_out, final_norm_g, loss_target, m_norm_mix_g, m_w_in, m_conv_w, m_rnn_conv_w, m_rnn_conv_b, m_w_a, m_b_a, m_w_x, m_b_x, m_lru_lambda, m_g_norm_conv, m_g_norm_rnn, m_w_out, m_norm_mlp_g, m_w_mlp_in, m_w_mlp_out, m_final_norm_g, v_norm_mix_g, v_w_in, v_conv_w, v_rnn_conv_w, v_rnn_conv_b, v_w_a, v_b_a, v_w_x, v_b_x, v_lru_lambda, v_g_norm_conv, v_g_norm_rnn, v_w_out, v_norm_mlp_g, v_w_mlp_in, v_w_mlp_out, v_final_norm_g):
    args = dict(locals())
    p = {}
    for n in _NAMES:
        for pre in ("", "m_", "v_"):
            a = args[pre + n]
            p[pre + n] = a[0] if a.ndim >= 3 else a
    xs = x[0]
    target = loss_target[0]
    core_chip = jnp.stack([lax.axis_index("c"), 2 * lax.axis_index("x") + lax.axis_index("y")]).astype(jnp.int32)
    cshard = p["conv_w"].shape[1]
    rshard = p["rnn_conv_w"].shape[1]

    small = jnp.concatenate([_pad_rows(p["conv_w"]), _pad_rows(p["rnn_conv_w"])], axis=1)
    w_in_g, w_out_g, w1_g, w2_g, small_g = _gather_first(p["w_in"], p["w_out"], p["w_mlp_in"], p["w_mlp_out"], small)
    conv_full = small_g[:, :3, :cshard].transpose(1, 0, 2).reshape(3, CONV_W)
    rconv_full = small_g[:, :4, cshard:].transpose(1, 0, 2).reshape(4, LRU_W)
    wa_bd = _to_block_diag(p["w_a"]).astype(BF16)
    wx_bd = _to_block_diag(p["w_x"]).astype(BF16)
    gf = p["final_norm_g"].reshape(1, -1)
    lru = (wa_bd, p["b_a"], wx_bd, p["b_x"], p["lru_lambda"], p["g_norm_conv"], p["g_norm_rnn"])

    (u, h1b, xr, hs, c3, yb, *gates), (w_out_g, w1_g, w2_g) = _fwd_mix(
        xs, p["norm_mix_g"], w_in_g, conv_full, rconv_full, p["rnn_conv_b"], *lru, (w_out_g, w1_g, w2_g))
    zb, dpb, h2b, dx3b, dx2, dx2b, dy, st_mlp = _mlp_fwd_bwd(
        xs, yb, w_out_g.reshape(-1, D_MODEL), w1_g, w2_g.reshape(-1, D_MODEL), p["norm_mlp_g"], gf, target)

    part_out = _wgrad(yb, dx2b, "out", core_chip)
    *part_1, arrived_out = _wgrad(h2b, dpb, "mlp_in", core_chip, parts=(part_out[1],))
    *part_2, arrived_1 = _wgrad(zb, dx3b, "mlp_out", core_chip, parts=(part_1[1],))
    (dub, st_mix, dwa_bd, dwx_bd), (arrived_2,) = _mix_bwd(
        dy, u, xr, hs, c3, gates, conv_full, rconv_full, wa_bd, wx_bd, p["lru_lambda"], p["g_norm_conv"], p["g_norm_rnn"],
        (part_2[1],))
    arrived_mlp = (arrived_out, arrived_1, arrived_2)
    vec_pack = jnp.concatenate([st_mix, st_mlp, _zero_blk()], axis=0)
    mat_pack = jnp.concatenate([_from_block_diag(dwa_bd), _from_block_diag(dwx_bd)], axis=0).astype(BF16)
    *part_in, vec_packs, mat_packs = _wgrad(h1b, dub, "in", core_chip, packs=(vec_pack, mat_pack))
    early = (("w_out", "out", part_out, arrived_mlp[0]), ("w_mlp_in", "mlp_in", part_1, arrived_mlp[1]),
             ("w_mlp_out", "mlp_out", part_2, arrived_mlp[2]))
    (grad_x, st_in), arrived_in, joined = _in_bwd(
        dub, w_in_g, xs, dx2, p["norm_mix_g"], (part_in[1],),
        [(tag, p[n].shape, part[0], arrived) for n, tag, part, arrived in early], core_chip)
    g_in, mix_g_blocks = _join("in", p["w_in"].shape, part_in[0], arrived_in[0], core_chip, st_in)
    big = {}
    for n, tag, g in [(n, tag, g) for (n, tag, _, _), g in zip(early, joined)] + [("w_in", "in", g_in)]:
        big[n] = _adamw_big(p[n], g, p["m_" + n], p["v_" + n], "adamw_" + tag)

    conv_wmv = jnp.stack([_pad_rows(p[pre + "conv_w"]) for pre in ("", "m_", "v_")])
    rconv_wmv = jnp.stack([_pad_rows(p[pre + "rnn_conv_w"]) for pre in ("", "m_", "v_")])
    g_pack, d_pack, m_pack, v_pack, conv_out, rconv_out = _small_step(
        vec_packs, mat_packs, mix_g_blocks, _pack_params(p, ""), _pack_params(p, "m_"), _pack_params(p, "v_"), conv_wmv, rconv_wmv)

    def unpack(pk, kind):
        def vec(b, width=D_MODEL):
            return pk[b * 8:b * 8 + 1, :width]
        return {
            "norm_mix_g": vec(PK_MIX_G), "rnn_conv_b": vec(PK_RCONV_B), "b_a": vec(PK_B_A), "b_x": vec(PK_B_X),
            "lru_lambda": vec(PK_LAMBDA), "g_norm_conv": vec(PK_G_NORM_CONV, CONV_W), "g_norm_rnn": vec(PK_G_NORM_RNN),
            "norm_mlp_g": vec(PK_MLP_G), "final_norm_g": vec(PK_FINAL_G).reshape(-1),
            "w_a": pk[PK_W_A * 8:PK_W_A * 8 + 64].reshape(1, 16, 64, 64), "w_x": pk[PK_W_X * 8:PK_W_X * 8 + 64].reshape(1, 16, 64, 64),
            "conv_w": conv_out[kind, :3][None], "rnn_conv_w": rconv_out[kind, :4][None],
            "w_in": big["w_in"][kind][None], "w_out": big["w_out"][kind][None],
            "w_mlp_in": big["w_mlp_in"][kind][None], "w_mlp_out": big["w_mlp_out"][kind][None],
        }

    outs = [unpack(pk, kind) for kind, pk in enumerate((g_pack, d_pack, m_pack, v_pack))]
    for o in outs:
        for n in ("norm_mix_g", "rnn_conv_b", "b_a", "b_x", "lru_lambda", "g_norm_conv", "g_norm_rnn", "norm_mlp_g"):
            o[n] = o[n].reshape(1, -1)
    loss = g_pack[PK_LOSS * 8, 0]
    return (loss, grad_x[None], *[o[n] for o in outs for n in _NAMES])
```

```python
import functools
import math

import jax
import jax.numpy as jnp
from jax import lax
from jax.experimental import pallas as pl
from jax.experimental.pallas import tpu as pltpu

F32 = jnp.float32
BF16 = jnp.bfloat16
MESH = pl.DeviceIdType.MESH
ANY = pl.BlockSpec(memory_space=pl.ANY)
VMEM = pl.BlockSpec(memory_space=pltpu.VMEM)

EPS = 1e-6
LRU_C = 8.0
D_MODEL = 1024
CONV_W = 512
LRU_W = 1024
IN_COLS = 3 * CONV_W + 2 * LRU_W
IN_SHARD = IN_COLS // 4
N_CHIPS = 4
N_DEVICES = 8
BD = 256
N_BD = LRU_W // BD

ADAM_LR = 0.001
ADAM_B1 = 0.9
ADAM_B2 = 0.999
ADAM_EPS = 1e-08
ADAM_WD = 0.01
ADAM_STEP = 10
ADAM_BC1 = 1.0 - ADAM_B1 ** ADAM_STEP
ADAM_BC2 = 1.0 - ADAM_B2 ** ADAM_STEP

TILE_ROWS = 8
TOKEN_TILE = 256
VMEM_LIMIT = 56 * 1024 * 1024

PK_G_NORM_RNN, PK_RCONV_B, PK_B_A, PK_B_X, PK_LAMBDA, PK_RCONV_W, PK_CONV_W, PK_G_NORM_CONV = range(8)
PK_FINAL_G, PK_MLP_G, PK_LOSS, PK_MIX_G = 8, 9, 10, 11
PK_W_A = 12
PK_W_X = 20
PK_BLOCKS = 28
PK_ROWS = PK_BLOCKS * TILE_ROWS


def _params(**kw):
    return pltpu.CompilerParams(vmem_limit_bytes=VMEM_LIMIT, **kw)


def _position():
    x, y, c = lax.axis_index("x"), lax.axis_index("y"), lax.axis_index("c")
    return x, y, c


def _sigmoid(v):
    return 1.0 / (1.0 + jnp.exp(-v))


def _one_minus_square(log_a, a):
    v = 2.0 * log_a
    series = -v * (1.0 + v * (0.5 + v * (1.0 / 6.0)))
    return jnp.where(v > -0.01, series, 1.0 - a * a)


_GELU_C = math.sqrt(2.0 / math.pi)
_GELU_K = 0.044715


def _gelu_and_grad(g):
    th = jnp.tanh(_GELU_C * (g + _GELU_K * g * g * g))
    gelu = 0.5 * g * (1.0 + th)
    dgelu = 0.5 * (1.0 + th) + 0.5 * g * (1.0 - th * th) * (_GELU_C * (1.0 + 3.0 * _GELU_K * g * g))
    return gelu, dgelu


def _rows(shape):
    return lax.broadcasted_iota(jnp.int32, shape, 0)


def _shift_down(v, k, prev8):
    rolled = pltpu.roll(v, k, 0)
    halo = pltpu.roll(prev8, k, 0)
    head = jnp.where(_rows(halo.shape) < k, halo, rolled[:TILE_ROWS])
    return jnp.concatenate([head, rolled[TILE_ROWS:]], axis=0)


def _shift_up(v, k, next8):
    n = v.shape[0]
    rolled = pltpu.roll(v, n - k, 0)
    halo = pltpu.roll(next8, TILE_ROWS - k, 0)
    tail = jnp.where(_rows(halo.shape) >= TILE_ROWS - k, halo, rolled[n - TILE_ROWS:])
    return jnp.concatenate([rolled[: n - TILE_ROWS], tail], axis=0)


def _scan_rows(a, b, carry, reverse=False):
    n, w = a.shape
    groups = n // TILE_ROWS
    a3 = a.reshape(groups, TILE_ROWS, w)
    b3 = b.reshape(groups, TILE_ROWS, w)
    sub = lax.broadcasted_iota(jnp.int32, a3.shape, 1)
    s = 1
    while s < TILE_ROWS:
        shift = TILE_ROWS - s if reverse else s
        keep = (sub < TILE_ROWS - s) if reverse else (sub >= s)
        b3 = b3 + jnp.where(keep, a3 * pltpu.roll(b3, shift, 1), 0.0)
        a3 = a3 * jnp.where(keep, pltpu.roll(a3, shift, 1), 1.0)
        s *= 2
    out = [None] * groups
    edge = 0 if reverse else TILE_ROWS - 1
    for g in (range(groups - 1, -1, -1) if reverse else range(groups)):
        out[g] = b3[g] + a3[g] * carry
        carry = out[g][edge:edge + 1]
    return jnp.concatenate(out, axis=0)


def _softplus_neg(lam):
    e = jnp.exp(-jnp.abs(lam))
    log1p_e = jnp.where(e < 1e-2, e * (1.0 - e * (0.5 - e * (1.0 / 3.0 - e * 0.25))), jnp.log(1.0 + e))
    sp = jnp.maximum(-lam, 0.0) + log1p_e
    dsp = -_sigmoid(-lam)
    return sp, dsp


def _block_diag_dot(vb, w_ref):
    return jnp.concatenate(
        [jnp.dot(vb[:, j * BD:(j + 1) * BD], w_ref[j], preferred_element_type=F32) for j in range(N_BD)], axis=1)


def _block_diag_dot_t(vb, w_ref):
    return jnp.concatenate(
        [lax.dot_general(vb[:, j * BD:(j + 1) * BD], w_ref[j], (((1,), (1,)), ((), ())), preferred_element_type=F32)
         for j in range(N_BD)], axis=1)


def _dot_nt(a, b):
    return lax.dot_general(a, b, (((1,), (1,)), ((), ())), preferred_element_type=F32)


def _dot_tn(a, b):
    return lax.dot_general(a, b, (((0,), (0,)), ((), ())), preferred_element_type=F32)


def _lru_gates(xr, wa_ref, ba, wx_ref, bx, sp):
    xrb = xr.astype(BF16)
    r = _sigmoid(_block_diag_dot(xrb, wa_ref) + ba)
    ig = _sigmoid(_block_diag_dot(xrb, wx_ref) + bx)
    log_a = (-LRU_C) * r * sp
    a = jnp.exp(log_a)
    mult = jnp.sqrt(_one_minus_square(log_a, a))
    return r, ig, a, mult


def _colsum(v):
    return jnp.sum(v, axis=0, keepdims=True)


N_FWD_OUT = 10


def _fwd_mix(x, g1, w_in_g, conv_w, rconv_w, rconv_b, wa_bd, b_a, wx_bd, b_x, lam, g_nc, g_nr, later):
    t, d = x.shape
    tm = TOKEN_TILE
    nt = t // tm
    nl = len(later)
    assert nl == 3
    pass_on_at = [nt * f // 16 for f in (1, 3, 7)]
    neighbours_at = [nt * f // 16 for f in (3, 6, 10)]
    diagonal_at = [nt * f // 16 for f in (10, 12, 14)]

    def body(x_ref, g1_ref, win_ref, cw_ref, rw_ref, rb_ref, wa_ref, ba_ref, wx_ref, bx_ref, lam_ref, gnc_ref, gnr_ref,
             *rest):
        later_in, outs, rest = rest[:nl], rest[nl:nl + N_FWD_OUT], rest[nl + N_FWD_OUT:]
        u_ref, h1_ref, xr_ref, hs_ref, c3_ref, y_ref, r_ref, ig_ref, a_ref, mult_ref = outs
        later_out, (cv_prev, xin_prev, h_prev, send_sems, recv_sems) = rest[:nl], rest[nl:]
        del later_in
        step = pl.program_id(0)
        plan = _ShardGather(later_out, send_sems, recv_sems)

        @pl.when(step == 0)
        def _():
            cv_prev[...] = jnp.zeros_like(cv_prev)
            xin_prev[...] = jnp.zeros_like(xin_prev)
            h_prev[...] = jnp.zeros_like(h_prev)
            for w in range(nl):
                plan.start_direct(w)

        for w in range(nl):
            @pl.when(step == pass_on_at[w])
            def _(w=w):
                plan.start_pass_on(w)

            @pl.when(step == neighbours_at[w])
            def _(w=w):
                plan.start_hand_over(w, diagonal=False)

            @pl.when(step == diagonal_at[w])
            def _(w=w):
                plan.start_hand_over(w, diagonal=True)

        xv = x_ref[...]
        rstd = lax.rsqrt(jnp.mean(xv * xv, axis=-1, keepdims=True) + EPS)
        h1b = ((xv * rstd) * g1_ref[...]).astype(BF16)
        h1_ref[...] = h1b
        for j in range(N_CHIPS):
            u_ref[:, j * IN_SHARD:(j + 1) * IN_SHARD] = jnp.dot(h1b, win_ref[j], preferred_element_type=F32)
        gate_b = u_ref[:, 0:CONV_W]
        cv = u_ref[:, CONV_W:2 * CONV_W] * u_ref[:, 2 * CONV_W:3 * CONV_W]
        x_r = u_ref[:, 3 * CONV_W:3 * CONV_W + LRU_W]
        g = u_ref[:, 3 * CONV_W + LRU_W:]

        cw = cw_ref[...]
        cvp = cv_prev[...]
        conv3 = cw[0:1] * _shift_down(cv, 2, cvp) + cw[1:2] * _shift_down(cv, 1, cvp) + cw[2:3] * cv
        cv_prev[...] = cv[tm - TILE_ROWS:]
        c3_ref[...] = conv3
        y_conv = gate_b * conv3

        rw = rw_ref[...]
        xp = xin_prev[...]
        xr = (rw[0:1] * _shift_down(x_r, 3, xp) + rw[1:2] * _shift_down(x_r, 2, xp)
              + rw[2:3] * _shift_down(x_r, 1, xp) + rw[3:4] * x_r) + rb_ref[...]
        xin_prev[...] = x_r[tm - TILE_ROWS:]
        xr_ref[...] = xr
        sp, _ = _softplus_neg(lam_ref[...])
        r, ig, a, mult = _lru_gates(xr, wa_ref, ba_ref[...], wx_ref, bx_ref[...], sp)
        r_ref[...] = r
        ig_ref[...] = ig
        a_ref[...] = a
        mult_ref[...] = mult
        h = _scan_rows(a, mult * (ig * xr), h_prev[...])
        h_prev[...] = h[tm - 1:tm]
        hs_ref[...] = h
        gelu, _ = _gelu_and_grad(g)
        y_rnn = h * gelu

        na = y_conv * lax.rsqrt(jnp.mean(y_conv * y_conv, axis=-1, keepdims=True) + EPS) * gnc_ref[...]
        nb = y_rnn * lax.rsqrt(jnp.mean(y_rnn * y_rnn, axis=-1, keepdims=True) + EPS) * gnr_ref[...]
        y_ref[:, :CONV_W] = na.astype(BF16)
        y_ref[:, CONV_W:] = nb.astype(BF16)

        @pl.when(step == nt - 1)
        def _():
            for w in range(nl):
                plan.finish(w)

    def full(a):
        nd = a.ndim
        return pl.BlockSpec(a.shape, lambda i: (0,) * nd)

    def tok(cols):
        return pl.BlockSpec((tm, cols), lambda i: (i, 0))

    def act(cols, dtype=F32):
        return jax.ShapeDtypeStruct((t, cols), dtype)

    smalls = (g1, w_in_g, conv_w, rconv_w, rconv_b, wa_bd, b_a, wx_bd, b_x, lam, g_nc, g_nr)
    n_in = 1 + len(smalls)
    outs = pl.pallas_call(
        body, name="fwd_mix", grid=(nt,),
        in_specs=[tok(d)] + [full(a) for a in smalls] + [ANY] * nl,
        out_specs=[tok(IN_COLS), tok(d), tok(LRU_W), tok(LRU_W), tok(CONV_W), tok(CONV_W + LRU_W)]
        + [tok(LRU_W)] * 4 + [ANY] * nl,
        out_shape=[act(IN_COLS), act(d, BF16), act(LRU_W), act(LRU_W), act(CONV_W), act(CONV_W + LRU_W, BF16)]
        + [act(LRU_W)] * 4 + [jax.ShapeDtypeStruct(a.shape, a.dtype) for a in later],
        input_output_aliases={n_in + w: N_FWD_OUT + w for w in range(nl)},
        scratch_shapes=[pltpu.VMEM((TILE_ROWS, CONV_W), F32), pltpu.VMEM((TILE_ROWS, LRU_W), F32),
                        pltpu.VMEM((1, LRU_W), F32), pltpu.SemaphoreType.DMA((nl, _ShardGather.PAIRS)),
                        pltpu.SemaphoreType.DMA((nl, _ShardGather.PAIRS))],
        compiler_params=_params(dimension_semantics=("arbitrary",)),
    )(x, *smalls, *later)
    return outs[:N_FWD_OUT], outs[N_FWD_OUT:]


def _mlp_fwd_bwd(x, yb, w_out_g, w1_g, w2_g, g2, gf, target):
    t, d = x.shape
    tm = TOKEN_TILE
    ff = w2_g.shape[0]
    mix = w_out_g.shape[0]
    ffs = ff // N_CHIPS

    def body(x_ref, y_ref, g2_ref, gf_ref, tgt_ref, wout_hbm, w1_hbm, w2_hbm,
             z_ref, dp_ref, h2_ref, dx3b_ref, dx2_ref, dx2b_ref, dy_ref, st_ref, wout, w1, w2, p_ref):
        @pl.when(pl.program_id(0) == 0)
        def _():
            pltpu.sync_copy(wout_hbm, wout)
            pltpu.sync_copy(w1_hbm, w1)
            pltpu.sync_copy(w2_hbm, w2)
            st_ref[...] = jnp.zeros_like(st_ref)

        x2 = x_ref[...] + jnp.dot(y_ref[...], wout[...], preferred_element_type=F32)
        r2 = lax.rsqrt(jnp.mean(x2 * x2, axis=-1, keepdims=True) + EPS)
        xh2 = x2 * r2
        g2v = g2_ref[...]
        h2b = (xh2 * g2v).astype(BF16)
        h2_ref[...] = h2b
        for j in range(N_CHIPS):
            p_ref[:, j * ffs:(j + 1) * ffs] = jnp.dot(h2b, w1[j], preferred_element_type=F32)
        rp = jnp.maximum(p_ref[...], 0.0)
        zb = (rp * rp).astype(BF16)
        z_ref[...] = zb
        x3 = x2 + jnp.dot(zb, w2[...], preferred_element_type=F32)
        r3 = lax.rsqrt(jnp.mean(x3 * x3, axis=-1, keepdims=True) + EPS)
        xh3 = x3 * r3
        gfv = gf_ref[...]
        err = xh3 * gfv - tgt_ref[...]
        loss = (0.5 / d) * jnp.sum(err * err)
        dout = err * (1.0 / d)
        st_ref[PK_FINAL_G * 8 - 64:PK_FINAL_G * 8 - 63, :] += _colsum(dout * xh3)
        st_ref[PK_LOSS * 8 - 64:PK_LOSS * 8 - 63, :] += jnp.zeros((1, d), F32) + loss
        dxh3 = dout * gfv
        dx3 = r3 * (dxh3 - xh3 * jnp.mean(dxh3 * xh3, axis=-1, keepdims=True))
        dx3b = dx3.astype(BF16)
        dx3b_ref[...] = dx3b
        dpb = (_dot_nt(dx3b, w2[...]) * (2.0 * rp)).astype(BF16)
        dp_ref[...] = dpb
        dh2 = _dot_nt(dpb[:, 0:ffs], w1[0])
        for j in range(1, N_CHIPS):
            dh2 = dh2 + _dot_nt(dpb[:, j * ffs:(j + 1) * ffs], w1[j])
        st_ref[PK_MLP_G * 8 - 64:PK_MLP_G * 8 - 63, :] += _colsum(dh2 * xh2)
        dxh2 = dh2 * g2v
        dx2 = dx3 + r2 * (dxh2 - xh2 * jnp.mean(dxh2 * xh2, axis=-1, keepdims=True))
        dx2_ref[...] = dx2
        dx2b = dx2.astype(BF16)
        dx2b_ref[...] = dx2b
        dy_ref[...] = _dot_nt(dx2b, wout[...])

    def tok(cols):
        return pl.BlockSpec((tm, cols), lambda i: (i, 0))

    def row(cols):
        return pl.BlockSpec((1, cols), lambda i: (0, 0))

    return pl.pallas_call(
        body, name="mlp_fwd_bwd", grid=(t // tm,),
        in_specs=[tok(d), tok(mix), row(d), row(d), tok(d), ANY, ANY, ANY],
        out_specs=[tok(ff), tok(ff), tok(d), tok(d), tok(d), tok(d), tok(mix),
                   pl.BlockSpec((3 * TILE_ROWS, d), lambda i: (0, 0))],
        out_shape=[jax.ShapeDtypeStruct((t, ff), BF16), jax.ShapeDtypeStruct((t, ff), BF16),
                   jax.ShapeDtypeStruct((t, d), BF16), jax.ShapeDtypeStruct((t, d), BF16),
                   jax.ShapeDtypeStruct((t, d), F32), jax.ShapeDtypeStruct((t, d), BF16),
                   jax.ShapeDtypeStruct((t, mix), F32), jax.ShapeDtypeStruct((3 * TILE_ROWS, d), F32)],
        scratch_shapes=[pltpu.VMEM(w_out_g.shape, BF16), pltpu.VMEM(w1_g.shape, BF16), pltpu.VMEM(w2_g.shape, BF16),
                        pltpu.VMEM((tm, ff), F32)],
        compiler_params=_params(dimension_semantics=("arbitrary",)),
    )(x, yb, g2, gf, target, w_out_g, w1_g, w2_g)


def _mix_bwd(dy, u, xr_all, hs_all, c3_all, gates, conv_w, rconv_w, wa_bd, wx_bd, lam, g_nc, g_nr, parts):
    t = dy.shape[0]
    tm = TOKEN_TILE
    nt = t // tm
    hb = tm // TILE_ROWS
    npart = len(parts)

    def body(dy_ref, u_ref, uh_ref, xr_ref, hs_ref, hh_ref, c3_ref, r_ref, ig_ref, a_ref, mult_ref,
             cw_ref, rw_ref, wa_ref, wx_ref, lam_ref, gnc_ref, gnr_ref, *rest):
        part_refs, (du_ref, st_ref, dwa_ref, dwx_ref), rest = rest[:npart], rest[npart:npart + 4], rest[npart + 4:]
        arrived_refs, (dc_next, a_next, gs_next, dxr_next, send_sems, recv_sems) = rest[:npart], rest[npart:]
        exchange = _PartialExchange(part_refs, arrived_refs, send_sems, recv_sems)
        i = pl.program_id(0)

        @pl.when(i == 0)
        def _():
            exchange.start()
            dc_next[...] = jnp.zeros_like(dc_next)
            a_next[...] = jnp.zeros_like(a_next)
            gs_next[...] = jnp.zeros_like(gs_next)
            dxr_next[...] = jnp.zeros_like(dxr_next)
            st_ref[...] = jnp.zeros_like(st_ref)
            dwa_ref[...] = jnp.zeros_like(dwa_ref)
            dwx_ref[...] = jnp.zeros_like(dwx_ref)

        first_tile = i == nt - 1
        gate_b = u_ref[:, 0:CONV_W]
        gate_c = u_ref[:, CONV_W:2 * CONV_W]
        v = u_ref[:, 2 * CONV_W:3 * CONV_W]
        x_r = u_ref[:, 3 * CONV_W:3 * CONV_W + LRU_W]
        g = u_ref[:, 3 * CONV_W + LRU_W:]
        cv = gate_c * v
        cv_prev = jnp.where(first_tile, 0.0, uh_ref[:, CONV_W:2 * CONV_W] * uh_ref[:, 2 * CONV_W:3 * CONV_W])
        xin_prev = jnp.where(first_tile, 0.0, uh_ref[:, 3 * CONV_W:3 * CONV_W + LRU_W])
        hs_prev = jnp.where(first_tile, 0.0, hh_ref[...])

        def acc(block, val, width=LRU_W, row=0):
            r0 = block * TILE_ROWS + row
            st_ref[r0:r0 + 1, 0:width] += val

        conv3 = c3_ref[...]
        y_conv = gate_b * conv3
        ra = lax.rsqrt(jnp.mean(y_conv * y_conv, axis=-1, keepdims=True) + EPS)
        xha = y_conv * ra
        dna = dy_ref[:, :CONV_W]
        acc(PK_G_NORM_CONV, _colsum(dna * xha), CONV_W)
        dxha = dna * gnc_ref[...]
        dy_conv = ra * (dxha - xha * jnp.mean(dxha * xha, axis=-1, keepdims=True))
        du_ref[:, 0:CONV_W] = (dy_conv * conv3).astype(BF16)
        dc = dy_conv * gate_b
        cw = cw_ref[...]
        dcn = dc_next[...]
        dcv = cw[2:3] * dc + cw[1:2] * _shift_up(dc, 1, dcn) + cw[0:1] * _shift_up(dc, 2, dcn)
        dc_next[...] = dc[:TILE_ROWS]
        acc(PK_CONV_W, _colsum(dc * _shift_down(cv, 2, cv_prev)), CONV_W, 0)
        acc(PK_CONV_W, _colsum(dc * _shift_down(cv, 1, cv_prev)), CONV_W, 1)
        acc(PK_CONV_W, _colsum(dc * cv), CONV_W, 2)
        du_ref[:, CONV_W:2 * CONV_W] = (dcv * v).astype(BF16)
        du_ref[:, 2 * CONV_W:3 * CONV_W] = (dcv * gate_c).astype(BF16)

        hs = hs_ref[...]
        gelu, dgelu = _gelu_and_grad(g)
        y_rnn = hs * gelu
        rb = lax.rsqrt(jnp.mean(y_rnn * y_rnn, axis=-1, keepdims=True) + EPS)
        xhb = y_rnn * rb
        dnb = dy_ref[:, CONV_W:]
        acc(PK_G_NORM_RNN, _colsum(dnb * xhb))
        dxhb = dnb * gnr_ref[...]
        dy_rnn = rb * (dxhb - xhb * jnp.mean(dxhb * xhb, axis=-1, keepdims=True))
        du_ref[:, 3 * CONV_W + LRU_W:] = (dy_rnn * hs * dgelu).astype(BF16)
        dh = dy_rnn * gelu

        xr = xr_ref[...]
        xrb = xr.astype(BF16)
        sp, dsp = _softplus_neg(lam_ref[...])
        r, ig, a, mult = r_ref[...], ig_ref[...], a_ref[...], mult_ref[...]
        a_up = _shift_up(a, 1, a_next[...])
        a_next[...] = a[:TILE_ROWS]
        gs = _scan_rows(a_up, dh, gs_next[0:1, :], reverse=True)
        gs_next[...] = gs[:TILE_ROWS]
        da = gs * _shift_down(hs, 1, hs_prev)
        gx = gs * xr
        di = gx * mult
        dmult = gx * ig
        dxr = gs * (mult * ig)
        dlog_a = da * a - dmult * ((a * a) / mult)
        acc(PK_LAMBDA, _colsum(dlog_a * r) * ((-LRU_C) * dsp))
        dpa = (dlog_a * ((-LRU_C) * sp)) * (r * (1.0 - r))
        dpx = di * (ig * (1.0 - ig))
        acc(PK_B_A, _colsum(dpa))
        acc(PK_B_X, _colsum(dpx))
        dpab = dpa.astype(BF16)
        dpxb = dpx.astype(BF16)
        dxr = dxr + _block_diag_dot_t(dpab, wa_ref) + _block_diag_dot_t(dpxb, wx_ref)
        for j in range(N_BD):
            cols = slice(j * BD, (j + 1) * BD)
            dwa_ref[j] += _dot_tn(xrb[:, cols], dpab[:, cols])
            dwx_ref[j] += _dot_tn(xrb[:, cols], dpxb[:, cols])

        acc(PK_RCONV_B, _colsum(dxr))
        rw = rw_ref[...]
        dxn = dxr_next[...]
        dx_r = (rw[3:4] * dxr + rw[2:3] * _shift_up(dxr, 1, dxn) + rw[1:2] * _shift_up(dxr, 2, dxn)
                + rw[0:1] * _shift_up(dxr, 3, dxn))
        dxr_next[...] = dxr[:TILE_ROWS]
        for k in range(3):
            acc(PK_RCONV_W, _colsum(dxr * _shift_down(x_r, 3 - k, xin_prev)), LRU_W, k)
        acc(PK_RCONV_W, _colsum(dxr * x_r), LRU_W, 3)
        du_ref[:, 3 * CONV_W:3 * CONV_W + LRU_W] = dx_r.astype(BF16)

        @pl.when(i == nt - 1)
        def _():
            exchange.wait()

    def full(a):
        nd = a.ndim
        return pl.BlockSpec(a.shape, lambda i: (0,) * nd)

    def tok(cols):
        return pl.BlockSpec((tm, cols), lambda i: (nt - 1 - i, 0))

    def halo(cols):
        return pl.BlockSpec((TILE_ROWS, cols), lambda i: (jnp.maximum((nt - 1 - i) * hb - 1, 0), 0))

    smalls = (conv_w, rconv_w, wa_bd, wx_bd, lam, g_nc, g_nr)
    outs = pl.pallas_call(
        body, name="mix_bwd", grid=(nt,),
        in_specs=[tok(CONV_W + LRU_W), tok(IN_COLS), halo(IN_COLS), tok(LRU_W), tok(LRU_W), halo(LRU_W), tok(CONV_W)]
        + [tok(LRU_W)] * 4 + [full(a) for a in smalls] + [ANY] * npart,
        out_specs=[tok(IN_COLS), pl.BlockSpec((8 * TILE_ROWS, LRU_W), lambda i: (0, 0)),
                   pl.BlockSpec((N_BD, BD, BD), lambda i: (0, 0, 0)), pl.BlockSpec((N_BD, BD, BD), lambda i: (0, 0, 0))]
        + [ANY] * npart,
        out_shape=[jax.ShapeDtypeStruct((t, IN_COLS), BF16), jax.ShapeDtypeStruct((8 * TILE_ROWS, LRU_W), F32),
                   jax.ShapeDtypeStruct((N_BD, BD, BD), F32), jax.ShapeDtypeStruct((N_BD, BD, BD), F32)]
        + [jax.ShapeDtypeStruct(a.shape, a.dtype) for a in parts],
        scratch_shapes=[pltpu.VMEM((TILE_ROWS, CONV_W), F32), pltpu.VMEM((TILE_ROWS, LRU_W), F32),
                        pltpu.VMEM((TILE_ROWS, LRU_W), F32), pltpu.VMEM((TILE_ROWS, LRU_W), F32),
                        pltpu.SemaphoreType.DMA((npart, 3)), pltpu.SemaphoreType.DMA((npart, 3))],
        compiler_params=_params(dimension_semantics=("arbitrary",)),
    )(dy, u, u, xr_all, hs_all, hs_all, c3_all, *gates, *smalls, *parts)
    return outs[:4], outs[4:]


def _in_bwd(dub, w_in_g, x, dx2, g1, parts, joins, core_chip):
    t, d = x.shape
    tm = TOKEN_TILE
    nt = t // tm
    npart = len(parts)
    nj = len(joins)
    geometry = []
    for tag, shape, _, _ in joins:
        pr, pc = WGRAD_GEOMETRY[tag][:2]
        every = 1 if pr % (nt * 16) == 0 else 2
        geometry.append((pr, pc, pr * every // nt, every, shape[1] == pc))

    def body(cc_ref, du_ref, win_ref, x_ref, dx2_ref, g1_ref, *rest):
        sums, rest = [rest[4 * w:4 * w + 4] for w in range(nj)], rest[4 * nj:]
        part_refs, (gx_ref, st_ref), rest = rest[:npart], rest[npart:npart + 2], rest[npart + 2:]
        arrived_refs, joined, rest = rest[:npart], rest[npart:npart + nj], rest[npart + nj:]
        stages, (send_sems, recv_sems, j_local, j_send, j_recv) = rest[:nj], rest[nj:]
        exchange = _PartialExchange(part_refs, arrived_refs, send_sems, recv_sems)
        i = pl.program_id(0)
        c = cc_ref[0]

        def window(w, core, row0, rows):
            pr, pc, _, _, by_rows = geometry[w]
            if by_rows:
                return joined[w].at[pl.ds(core * pr + row0, rows), :]
            return joined[w].at[pl.ds(row0, rows), pl.ds(core * pc, pc)]

        def to_sibling(w, src, core, row0, rows):
            return pltpu.make_async_remote_copy(src_ref=src, dst_ref=window(w, core, row0, rows), send_sem=j_send.at[w],
                                                recv_sem=j_recv.at[w], device_id=_sibling(), device_id_type=MESH)

        @pl.when(i == 0)
        def _():
            exchange.start()
            st_ref[...] = jnp.zeros_like(st_ref)

        for w in range(nj):
            pr, pc, rb, every, _ = geometry[w]

            @pl.when(i % every == 0)
            def _(w=w, rb=rb, every=every):
                p_ref, r1_ref, r2_ref, r3_ref = sums[w]
                row0 = pl.multiple_of((i // every) * rb, rb)
                rows = stages[w].at[pl.ds(row0, rb), :]
                rows[...] = ((p_ref[0] + r1_ref[0].astype(F32)) + r2_ref[0].astype(F32)) + r3_ref[0].astype(F32)
                pltpu.make_async_copy(rows, window(w, c, row0, rb), j_local.at[w]).start()
                to_sibling(w, rows, c, row0, rb).start()

        dh1 = _dot_nt(du_ref[:, 0:IN_SHARD], win_ref[0])
        for j in range(1, N_CHIPS):
            dh1 = dh1 + _dot_nt(du_ref[:, j * IN_SHARD:(j + 1) * IN_SHARD], win_ref[j])
        xv = x_ref[...]
        rstd = lax.rsqrt(jnp.mean(xv * xv, axis=-1, keepdims=True) + EPS)
        xh = xv * rstd
        st_ref[0:1, :] += _colsum(dh1 * xh)
        dxh = dh1 * g1_ref[...]
        gx_ref[...] = dx2_ref[...] + rstd * (dxh - xh * jnp.mean(dxh * xh, axis=-1, keepdims=True))

        @pl.when(i == nt - 1)
        def _():
            exchange.wait()
            for w in range(nj):
                pr = geometry[w][0]
                pltpu.make_async_copy(stages[w], window(w, c, 0, pr), j_local.at[w]).wait()
                to_sibling(w, stages[w], 1 - c, 0, pr).wait()

    def tok(cols):
        return pl.BlockSpec((tm, cols), lambda i, cc: (i, 0))

    def partial(w, off):
        pr, pc, rb, every, _ = geometry[w]
        return pl.BlockSpec((1, rb, pc), lambda i, cc: ((cc[1] + off) % N_CHIPS, i // every, 0))

    sum_specs, sum_operands = [], []
    for w, (_, _, own, arrived) in enumerate(joins):
        sum_specs += [partial(w, off) for off in range(N_CHIPS)]
        sum_operands += [own, arrived, arrived, arrived]
    dma = pltpu.SemaphoreType.DMA
    outs = pl.pallas_call(
        body, name="in_bwd",
        grid_spec=pltpu.PrefetchScalarGridSpec(
            num_scalar_prefetch=1, grid=(nt,),
            in_specs=[tok(IN_COLS), pl.BlockSpec(w_in_g.shape, lambda i, cc: (0, 0, 0)), tok(d), tok(d),
                      pl.BlockSpec((1, d), lambda i, cc: (0, 0))] + sum_specs + [ANY] * npart,
            out_specs=[tok(d), pl.BlockSpec((TILE_ROWS, d), lambda i, cc: (0, 0))] + [ANY] * (npart + nj),
            scratch_shapes=[pltpu.VMEM((g[0], g[1]), F32) for g in geometry]
            + [dma((npart, 3)), dma((npart, 3)), dma((nj,)), dma((nj,)), dma((nj,))]),
        out_shape=[jax.ShapeDtypeStruct((t, d), F32), jax.ShapeDtypeStruct((TILE_ROWS, d), F32)]
        + [jax.ShapeDtypeStruct(a.shape, a.dtype) for a in parts]
        + [jax.ShapeDtypeStruct(shape, F32) for _, shape, _, _ in joins],
        compiler_params=_params(dimension_semantics=("arbitrary",)),
    )(core_chip, dub, w_in_g, x, dx2, g1, *sum_operands, *parts)
    return outs[:2], outs[2:2 + npart], outs[2 + npart:]


WGRAD_GEOMETRY = {
    "in": (512, IN_SHARD, lambda s, h: h, lambda s, h: s),
    "mlp_in": (512, D_MODEL, lambda s, h: h, lambda s, h: s),
    "mlp_out": (512, D_MODEL, lambda s, h: 2 * s + h, lambda s, h: 0),
    "out": (384, 512, lambda s, h: s, lambda s, h: h),
}
K_CHUNK = 512


def _sibling():
    x, y, c = _position()
    return (x, y, 1 - c)


def _wgrad(a, b, tag, core_chip, packs=(), parts=()):
    t = a.shape[0]
    pr, pc, a_blk, b_blk = WGRAD_GEOMETRY[tag]
    nk = t // K_CHUNK
    mine = N_CHIPS
    riding = len(packs)
    npart = len(parts)
    assert not (riding and npart)

    def body(cc_ref, a_ref, b_ref, *rest):
        if riding:
            pack_refs, (land_ref, p_ref, pb_ref), rest = rest[:riding], rest[riding:riding + 3], rest[riding + 3:]
            all_refs, (stage, rbuf, send_sems, recv_sems, rsem), g_sems = rest[:riding], rest[riding:riding + 5], rest[riding + 5:]
            gathers = [_PackGather(pack_refs[n], all_refs[n], *g_sems[3 * n:3 * n + 3]) for n in range(riding)]
        elif npart:
            part_refs, (land_ref, p_ref, pb_ref), rest = rest[:npart], rest[npart:npart + 3], rest[npart + 3:]
            arrived_refs, (stage, rbuf, send_sems, recv_sems, rsem, x_send, x_recv) = rest[:npart], rest[npart:]
            exchange = _PartialExchange(part_refs, arrived_refs, x_send, x_recv)
        else:
            land_ref, p_ref, pb_ref, stage, rbuf, send_sems, recv_sems, rsem = rest
        ph, s = pl.program_id(0), pl.program_id(1)
        if riding:
            @pl.when((ph == 0) & (s == 0))
            def _():
                for gather in gathers:
                    gather.start()

            @pl.when((ph == 1) & (s == N_CHIPS - 2))
            def _():
                for gather in gathers:
                    gather.hand_over()
        if npart:
            @pl.when((ph == 0) & (s == 0))
            def _():
                exchange.start()
        def push(k):
            return pltpu.make_async_remote_copy(src_ref=stage.at[k], dst_ref=land_ref.at[k], send_sem=send_sems.at[k],
                                                recv_sem=recv_sems.at[k], device_id=_sibling(), device_id_type=MESH)

        def landed():
            return pltpu.make_async_copy(land_ref.at[s], rbuf, rsem)

        @pl.when(ph == 1)
        def _():
            push(s).wait_recv()
            landed().start()

        slot = jnp.where(ph == 0, s, mine)
        acc = stage.at[slot]
        acc[...] = _dot_tn(a_ref[0:K_CHUNK, :], b_ref[0:K_CHUNK, :])
        for k in range(1, nk):
            acc[...] += _dot_tn(a_ref[k * K_CHUNK:(k + 1) * K_CHUNK, :], b_ref[k * K_CHUNK:(k + 1) * K_CHUNK, :])

        @pl.when(ph == 0)
        def _():
            push(s).start()

        @pl.when(ph == 1)
        def _():
            landed().wait()
            p = stage[mine] + rbuf[...]
            p_ref[0] = p
            pb_ref[0] = p.astype(BF16)

        @pl.when((ph == 1) & (s == N_CHIPS - 1))
        def _():
            for k in range(N_CHIPS):
                push(k).wait_send()
            for gather in (gathers if riding else ()):
                gather.finish()
            if npart:
                exchange.wait()

    def half(ph, cc):
        return jnp.where(ph == 0, 1 - cc[0], cc[0])

    def out_slot(ph, s, cc):
        return (jnp.where(ph == 0, 0, s), 0, 0)

    piece = jax.ShapeDtypeStruct((N_CHIPS, pr, pc), F32)
    in_specs = [pl.BlockSpec((t, pr), lambda ph, s, cc: (0, a_blk(s, half(ph, cc)))),
                pl.BlockSpec((t, pc), lambda ph, s, cc: (0, b_blk(s, half(ph, cc))))]
    out_specs = [ANY, pl.BlockSpec((1, pr, pc), out_slot), pl.BlockSpec((1, pr, pc), out_slot)]
    out_shape = [piece, piece, jax.ShapeDtypeStruct((N_CHIPS, pr, pc), BF16)]
    scratch = [pltpu.VMEM((N_CHIPS + 1, pr, pc), F32), pltpu.VMEM((pr, pc), F32),
               pltpu.SemaphoreType.DMA((N_CHIPS,)), pltpu.SemaphoreType.DMA((N_CHIPS,)), pltpu.SemaphoreType.DMA]
    operands = [a, b]
    for pack in packs:
        in_specs.append(pl.BlockSpec(pack.shape, lambda ph, s, cc: (0, 0)))
        out_specs.append(ANY)
        out_shape.append(jax.ShapeDtypeStruct((N_DEVICES,) + pack.shape, pack.dtype))
        operands.append(pack)
    for pack in packs:
        scratch += _PackGather.semaphores()
    if npart:
        in_specs += [ANY] * npart
        out_specs += [ANY] * npart
        out_shape += [jax.ShapeDtypeStruct(p.shape, p.dtype) for p in parts]
        scratch += [pltpu.SemaphoreType.DMA((npart, 3)), pltpu.SemaphoreType.DMA((npart, 3))]
        operands += list(parts)
    return pl.pallas_call(
        body, name="wgrad_" + tag,
        grid_spec=pltpu.PrefetchScalarGridSpec(
            num_scalar_prefetch=1, grid=(2, N_CHIPS), in_specs=in_specs, out_specs=out_specs, scratch_shapes=scratch),
        out_shape=out_shape,
        compiler_params=_params(dimension_semantics=("arbitrary", "arbitrary")),
    )(core_chip, *operands)[1:]


def _other_chips(x, y):
    return [(1 - x, y), (x, 1 - y), (1 - x, 1 - y)]


class _ShardGather:
    PAIRS = 9

    def __init__(self, outs, send_sems, recv_sems):
        self.outs, self.send_sems, self.recv_sems = outs, send_sems, recv_sems
        x, y, c = _position()
        self.c, self.j = c, 2 * x + y
        self.sibling = (x, y, 1 - c)
        self.chips = _other_chips(x, y)

    def _chip(self, k):
        px, py = self.chips[k]
        return 2 * px + py

    def _half(self, w, chip, which):
        hr = self.outs[w].shape[1] // 2
        return self.outs[w].at[chip, pl.ds(which * hr, hr), :]

    def _quarter(self, w, chip, q):
        qr = self.outs[w].shape[1] // 4
        return self.outs[w].at[chip, pl.ds(self.c * 2 * qr + q * qr, qr), :]

    def _copy(self, ref, w, pair, to, src=None):
        return pltpu.make_async_remote_copy(src_ref=ref if src is None else src, dst_ref=ref, send_sem=self.send_sems.at[w, pair],
                                            recv_sem=self.recv_sems.at[w, pair], device_id=to, device_id_type=MESH)

    def direct(self, w, k, q, src=None):
        return self._copy(self._quarter(w, self.j, q), w, 2 * k + q, (*self.chips[k], self.c), src)

    def direct_landed(self, w, k, q):
        return self._copy(self._quarter(w, self._chip(k), q), w, 2 * k + q, (*self.chips[k], self.c))

    def pass_on(self, w, q):
        return self._copy(self._quarter(w, self._chip(q), q), w, 4 + q, (*self.chips[1 - q], self.c))

    def passed_landed(self, w, q):
        return self._copy(self._quarter(w, self._chip(2), q), w, 4 + q, (*self.chips[1 - q], self.c))

    def hand_over(self, w, k):
        return self._copy(self._half(w, self._chip(k), self.c), w, 6 + k, self.sibling)

    def handed(self, w, k):
        return self._copy(self._half(w, self._chip(k), 1 - self.c), w, 6 + k, self.sibling)

    def start_direct(self, w, src_half=None):
        qr = self.outs[w].shape[1] // 4
        for k, q in ((0, 0), (1, 1), (0, 1), (1, 0)):
            self.direct(w, k, q, None if src_half is None else src_half.at[pl.ds(q * qr, qr), :]).start()

    def start_pass_on(self, w):
        for q in (0, 1):
            self.direct_landed(w, q, q).wait_recv()
            self.pass_on(w, q).start()

    def start_hand_over(self, w, diagonal):
        if diagonal:
            for q in (0, 1):
                self.passed_landed(w, q).wait_recv()
            self.hand_over(w, 2).start()
        else:
            for k in (0, 1):
                self.direct_landed(w, k, 1 - k).wait_recv()
                self.hand_over(w, k).start()

    def finish(self, w):
        for k in range(3):
            self.handed(w, k).wait_recv()
            self.hand_over(w, k).wait_send()
        for q in (0, 1):
            self.pass_on(w, q).wait_send()
            for k in (0, 1):
                self.direct(w, k, q).wait_send()


def _gather_first(w_in, w_out, w1, w2, small):
    bigs = (w_in, w_out, w1, w2)
    nb = len(bigs)

    def body(win_ref, wout_ref, w1_ref, w2_ref, sm_ref, gin, gout, g1, g2, gsm, st_in, st_out, st_1, st_2,
             send_sems, recv_sems, sm_send, sm_recv, local_sems):
        srcs = (win_ref, wout_ref, w1_ref, w2_ref)
        stages = (st_in, st_out, st_1, st_2)
        outs = (gin, gout, g1, g2)
        plan = _ShardGather(outs[:1], send_sems, recv_sems)
        j, c = plan.j, plan.c
        for src, st in zip(srcs, stages):
            st[...] = src[...].astype(BF16)
        local = [pltpu.make_async_copy(stages[w], outs[w].at[j], local_sems.at[w]) for w in range(nb)]
        local.append(pltpu.make_async_copy(sm_ref, gsm.at[j], local_sems.at[nb]))
        for cp in local:
            cp.start()

        def small_copy(k):
            px, py = plan.chips[k]
            return pltpu.make_async_remote_copy(src_ref=sm_ref, dst_ref=gsm.at[j], send_sem=sm_send.at[k],
                                                recv_sem=sm_recv.at[k], device_id=(px, py, c), device_id_type=MESH)

        def small_landed(k):
            px, py = plan.chips[k]
            return pltpu.make_async_remote_copy(src_ref=sm_ref, dst_ref=gsm.at[2 * px + py], send_sem=sm_send.at[k],
                                                recv_sem=sm_recv.at[k], device_id=(px, py, c), device_id_type=MESH)

        hr = w_in.shape[0] // 2
        plan.start_direct(0, st_in.at[pl.ds(c * hr, hr), :])
        for k in range(3):
            small_copy(k).start()
        plan.start_pass_on(0)
        plan.start_hand_over(0, diagonal=False)
        plan.start_hand_over(0, diagonal=True)
        for k in range(3):
            small_landed(k).wait_recv()
            small_copy(k).wait_send()
        plan.finish(0)
        for cp in local:
            cp.wait()

    def gathered(a, dtype):
        return jax.ShapeDtypeStruct((N_CHIPS,) + a.shape, dtype)

    return pl.pallas_call(
        body, name="gather_first",
        in_specs=[VMEM] * 5, out_specs=[ANY] * 5,
        out_shape=[gathered(a, BF16) for a in bigs] + [gathered(small, F32)],
        scratch_shapes=[pltpu.VMEM(a.shape, BF16) for a in bigs]
        + [pltpu.SemaphoreType.DMA((1, _ShardGather.PAIRS)), pltpu.SemaphoreType.DMA((1, _ShardGather.PAIRS)), pltpu.SemaphoreType.DMA((3,)),
           pltpu.SemaphoreType.DMA((3,)), pltpu.SemaphoreType.DMA((nb + 1,))],
        compiler_params=_params(),
    )(*bigs, small)


class _PartialExchange:
    def __init__(self, parts, arrived, send_sems, recv_sems):
        self.parts, self.arrived, self.send_sems, self.recv_sems = parts, arrived, send_sems, recv_sems
        x, y, c = _position()
        self.c, self.j = c, 2 * x + y
        self.chips = _other_chips(x, y)

    def _copy(self, w, k, slot):
        px, py = self.chips[k]
        return pltpu.make_async_remote_copy(
            src_ref=self.parts[w].at[2 * px + py], dst_ref=self.arrived[w].at[slot], send_sem=self.send_sems.at[w, k],
            recv_sem=self.recv_sems.at[w, k], device_id=(px, py, self.c), device_id_type=MESH)

    def start(self):
        for w in range(len(self.parts)):
            for k in range(3):
                self._copy(w, k, self.j).start()

    def wait(self):
        for w in range(len(self.parts)):
            for k in range(3):
                px, py = self.chips[k]
                self._copy(w, k, 2 * px + py).wait()


class _PackGather:
    def __init__(self, p_ref, all_ref, send_sems, recv_sems, local_sem):
        self.p_ref, self.all_ref, self.send_sems, self.recv_sems, self.local_sem = p_ref, all_ref, send_sems, recv_sems, local_sem
        x, y, c = _position()
        self.me, self.sibling, self.c = (x, y, c), (x, y, 1 - c), c
        self.chips = _other_chips(x, y)

    @staticmethod
    def semaphores():
        return [pltpu.SemaphoreType.DMA((7,)), pltpu.SemaphoreType.DMA((7,)), pltpu.SemaphoreType.DMA]

    def _copy(self, k, block, to, from_pack=False):
        px, py, pc = block
        slot = self.all_ref.at[4 * px + 2 * py + pc]
        return pltpu.make_async_remote_copy(src_ref=self.p_ref if from_pack else slot, dst_ref=slot, send_sem=self.send_sems.at[k],
                                            recv_sem=self.recv_sems.at[k], device_id=to, device_id_type=MESH)

    def _mine(self):
        x, y, c = self.me
        return pltpu.make_async_copy(self.p_ref, self.all_ref.at[4 * x + 2 * y + c], self.local_sem)

    def _first(self):
        return [self._copy(0, self.me, self.sibling, True)] + [
            self._copy(1 + k, self.me, (*chip, self.c), True) for k, chip in enumerate(self.chips)]

    def _passed(self):
        return [self._copy(4 + k, (*chip, self.c), self.sibling) for k, chip in enumerate(self.chips)]

    def start(self):
        self._mine().start()
        for cp in self._first():
            cp.start()

    def hand_over(self):
        for k, chip in enumerate(self.chips):
            self._copy(1 + k, (*chip, self.c), self.me).wait_recv()
            self._passed()[k].start()

    def finish(self):
        self._copy(0, self.sibling, self.me).wait_recv()
        for k, chip in enumerate(self.chips):
            self._copy(4 + k, (*chip, 1 - self.c), self.me).wait_recv()
        for cp in self._first() + self._passed():
            cp.wait_send()
        self._mine().wait()


class _DirectGather:
    def __init__(self, p_ref, all_ref, send_sems, recv_sems, local_sem):
        self.p_ref, self.all_ref, self.send_sems, self.recv_sems, self.local_sem = p_ref, all_ref, send_sems, recv_sems, local_sem
        self.me = _position()

    semaphores = _PackGather.semaphores

    def _peer(self, r):
        x, y, c = self.me
        return ((1 - x) if r & 4 else x, (1 - y) if r & 2 else y, (1 - c) if r & 1 else c)

    def _copy(self, r, slot_of):
        px, py, pc = slot_of
        return pltpu.make_async_remote_copy(src_ref=self.p_ref, dst_ref=self.all_ref.at[4 * px + 2 * py + pc],
                                            send_sem=self.send_sems.at[r - 1], recv_sem=self.recv_sems.at[r - 1],
                                            device_id=self._peer(r), device_id_type=MESH)

    def _mine(self):
        x, y, c = self.me
        return pltpu.make_async_copy(self.p_ref, self.all_ref.at[4 * x + 2 * y + c], self.local_sem)

    def start(self):
        self._mine().start()
        for r in range(1, N_DEVICES):
            self._copy(r, self.me).start()

    def finish(self):
        for r in range(1, N_DEVICES):
            self._copy(r, self._peer(r)).wait()
        self._mine().wait()


def _adamw(w, g, m, v):
    m = ADAM_B1 * m + (1.0 - ADAM_B1) * g
    v = ADAM_B2 * v + (1.0 - ADAM_B2) * (g * g)
    m_hat = m / ADAM_BC1
    v_hat = v / ADAM_BC2
    delta = -ADAM_LR * (m_hat / (jnp.sqrt(v_hat) + ADAM_EPS) + ADAM_WD * w)
    return delta, m, v


JOIN_SUB = 4


def _join(tag, shard_shape, part, arrived, core_chip, block=None):
    pr, pc = WGRAD_GEOMETRY[tag][:2]
    rb = pr // JOIN_SUB
    by_rows = shard_shape[1] == pc
    riding = block is not None

    def body(cc_ref, p_ref, r1_ref, r2_ref, r3_ref, *rest):
        if riding:
            blk_ref, g_ref, all_ref, stage, send_sems, recv_sems, local_sems, b_send, b_recv, b_local = rest
            gather = _DirectGather(blk_ref, all_ref, b_send, b_recv, b_local)
        else:
            g_ref, stage, send_sems, recv_sems, local_sems = rest
        i = pl.program_id(0)
        c = cc_ref[0]
        if riding:
            @pl.when(i == 0)
            def _():
                gather.start()

        def window(core, k):
            if by_rows:
                return g_ref.at[pl.ds((core * JOIN_SUB + k) * rb, rb), :]
            return g_ref.at[pl.ds(k * rb, rb), pl.ds(core * pc, pc)]

        def keep(k):
            return pltpu.make_async_copy(stage.at[k], window(c, k), local_sems.at[k])

        def push(k):
            return pltpu.make_async_remote_copy(src_ref=stage.at[k], dst_ref=window(c, k), send_sem=send_sems.at[k],
                                                recv_sem=recv_sems.at[k], device_id=_sibling(), device_id_type=MESH)

        def pushed(k):
            return pltpu.make_async_remote_copy(src_ref=stage.at[k], dst_ref=window(1 - c, k), send_sem=send_sems.at[k],
                                                recv_sem=recv_sems.at[k], device_id=_sibling(), device_id_type=MESH)

        stage[i] = ((p_ref[0] + r1_ref[0].astype(F32)) + r2_ref[0].astype(F32)) + r3_ref[0].astype(F32)
        keep(i).start()
        push(i).start()

        @pl.when(i == JOIN_SUB - 1)
        def _():
            for k in range(JOIN_SUB):
                keep(k).wait()
                push(k).wait_send()
                pushed(k).wait_recv()
            if riding:
                gather.finish()

    def partial(off):
        return pl.BlockSpec((1, rb, pc), lambda i, cc: ((cc[1] + off) % N_CHIPS, i, 0))

    in_specs = [partial(0), partial(1), partial(2), partial(3)]
    out_specs = [ANY]
    out_shape = [jax.ShapeDtypeStruct(shard_shape, F32)]
    scratch = [pltpu.VMEM((JOIN_SUB, rb, pc), F32), pltpu.SemaphoreType.DMA((JOIN_SUB,)),
               pltpu.SemaphoreType.DMA((JOIN_SUB,)), pltpu.SemaphoreType.DMA((JOIN_SUB,))]
    operands = [part, arrived, arrived, arrived]
    if riding:
        in_specs.append(pl.BlockSpec(block.shape, lambda i, cc: (0, 0)))
        out_specs.append(ANY)
        out_shape.append(jax.ShapeDtypeStruct((N_DEVICES,) + block.shape, block.dtype))
        scratch += _DirectGather.semaphores()
        operands.append(block)
    outs = pl.pallas_call(
        body, name="join_" + tag,
        grid_spec=pltpu.PrefetchScalarGridSpec(
            num_scalar_prefetch=1, grid=(JOIN_SUB,), in_specs=in_specs, out_specs=out_specs, scratch_shapes=scratch),
        out_shape=out_shape,
        compiler_params=_params(dimension_semantics=("arbitrary",)),
    )(core_chip, *operands)
    return outs if riding else outs[0]


def _adamw_big(w, g, m, v, name):
    rows, cols = w.shape
    rb = 256 if rows % 256 == 0 else rows

    def body(w_ref, g_ref, m_ref, v_ref, go_ref, d_ref, nm_ref, nv_ref):
        g = g_ref[...]
        go_ref[...] = g
        d_ref[...], nm_ref[...], nv_ref[...] = _adamw(w_ref[...], g, m_ref[...], v_ref[...])

    spec = pl.BlockSpec((rb, cols), lambda i: (i, 0))
    return pl.pallas_call(
        body, name=name, grid=(rows // rb,), in_specs=[spec] * 4, out_specs=[spec] * 4,
        out_shape=[jax.ShapeDtypeStruct(w.shape, F32)] * 4,
        compiler_params=_params(dimension_semantics=("arbitrary",)),
    )(w, g, m, v)


def _small_step(vec_packs, mat_packs, mix_g_blocks, w_pack, m_pack, v_pack, conv_wmv, rconv_wmv):
    vec_rows = vec_packs.shape[1]
    rows, cols = vec_rows + mat_packs.shape[1], vec_packs.shape[2]
    cshard = conv_wmv.shape[2]
    rshard = rconv_wmv.shape[2]
    mix_row = PK_MIX_G * TILE_ROWS

    def body(vec_ref, mat_ref, blk_ref, w_ref, m_ref, v_ref, cw_ref, rw_ref, g_ref, d_ref, nm_ref, nv_ref, co_ref, ro_ref):
        total = vec_ref[0]
        mats = mat_ref[0].astype(F32)
        late = blk_ref[0]
        for k in range(1, N_DEVICES):
            total = total + vec_ref[k]
            mats = mats + mat_ref[k].astype(F32)
            late = late + blk_ref[k]
        g_ref[0:vec_rows, :] = total
        g_ref[vec_rows:, :] = mats
        g_ref[mix_row:mix_row + TILE_ROWS, :] = late
        g = g_ref[...]
        d_ref[...], nm_ref[...], nv_ref[...] = _adamw(w_ref[...], g, m_ref[...], v_ref[...])

        x, y, _ = _position()
        j = 2 * x + y
        cblk = total[PK_CONV_W * 8:PK_CONV_W * 8 + 8, :]
        rblk = total[PK_RCONV_W * 8:PK_RCONV_W * 8 + 8, :]
        cg = cblk[:, 0:cshard]
        rg = rblk[:, 0:rshard]
        for k in range(1, N_CHIPS):
            cg = jnp.where(j == k, cblk[:, k * cshard:(k + 1) * cshard], cg)
            rg = jnp.where(j == k, rblk[:, k * rshard:(k + 1) * rshard], rg)
        co_ref[0] = cg
        co_ref[1], co_ref[2], co_ref[3] = _adamw(cw_ref[0], cg, cw_ref[1], cw_ref[2])
        ro_ref[0] = rg
        ro_ref[1], ro_ref[2], ro_ref[3] = _adamw(rw_ref[0], rg, rw_ref[1], rw_ref[2])

    pack = [jax.ShapeDtypeStruct((rows, cols), F32)] * 4
    return pl.pallas_call(
        body, name="small_grads_step", in_specs=[VMEM] * 8, out_specs=[VMEM] * 6,
        out_shape=pack + [jax.ShapeDtypeStruct((4, TILE_ROWS, cshard), F32), jax.ShapeDtypeStruct((4, TILE_ROWS, rshard), F32)],
        compiler_params=_params(),
    )(vec_packs, mat_packs, mix_g_blocks, w_pack, m_pack, v_pack, conv_wmv, rconv_wmv)


def _blk(a):
    a = a.reshape(-1, a.shape[-1])
    return jnp.pad(a, ((0, TILE_ROWS - a.shape[0]), (0, D_MODEL - a.shape[1])))


def _zero_blk():
    return jnp.zeros((TILE_ROWS, D_MODEL), F32)


def _pack_params(p, pre):
    get = lambda n: p[pre + n]
    return jnp.concatenate([
        _blk(get("g_norm_rnn")), _blk(get("rnn_conv_b")), _blk(get("b_a")), _blk(get("b_x")), _blk(get("lru_lambda")),
        _zero_blk(), _zero_blk(), _blk(get("g_norm_conv")), _blk(get("final_norm_g").reshape(1, -1)), _blk(get("norm_mlp_g")),
        _zero_blk(), _blk(get("norm_mix_g")), get("w_a").reshape(64, D_MODEL), get("w_x").reshape(64, D_MODEL)], axis=0)


def _to_block_diag(w):
    w4 = w.reshape(N_BD, 4, 64, 64)
    eye = jnp.eye(4, dtype=w.dtype)
    return (w4[:, :, :, None, :] * eye[None, :, None, :, None]).reshape(N_BD, BD, BD)


def _from_block_diag(d):
    d5 = d.reshape(N_BD, 4, 64, 4, 64)
    return jnp.stack([d5[:, q, :, q, :] for q in range(4)], axis=1).reshape(64, D_MODEL)


def _pad_rows(a):
    return jnp.pad(a, ((0, TILE_ROWS - a.shape[0]), (0, 0)))


_NAMES = ['norm_mix_g', 'w_in', 'conv_w', 'rnn_conv_w', 'rnn_conv_b', 'w_a', 'b_a', 'w_x', 'b_x', 'lru_lambda',
          'g_norm_conv', 'g_norm_rnn', 'w_out', 'norm_mlp_g', 'w_mlp_in', 'w_mlp_out', 'final_norm_g']


def kernel(x, norm_mix_g, w_in, conv_w, rnn_conv_w, rnn_conv_b, w_a, b_a, w_x, b_x, lru_lambda, g_norm_conv, g_norm_rnn, w_out, norm_mlp_g, w_mlp_in, w_mlp_out, final_norm_g, loss_target, m_norm_mix_g, m_w_in, m_conv_w, m_rnn_conv_w, m_rnn_conv_b, m_w_a, m_b_a, m_w_x, m_b_x, m_lru_lambda, m_g_norm_conv, m_g_norm_rnn, m_w_out, m_norm_mlp_g, m_w_mlp_in, m_w_mlp_out, m_final_norm_g, v_norm_mix_g, v_w_in, v_conv_w, v_rnn_conv_w, v_rnn_conv_b, v_w_a, v_b_a, v_w_x, v_b_x, v_lru_lambda, v_g_norm_conv, v_g_norm_rnn, v_w_out, v_norm_mlp_g, v_w_mlp_in, v_w_mlp_out, v_final_norm_g):
    args = dict(locals())
    p = {}
    for n in _NAMES:
        for pre in ("", "m_", "v_"):
            a = args[pre + n]
            p[pre + n] = a[0] if a.ndim >= 3 else a
    xs = x[0]
    target = loss_target[0]
    core_chip = jnp.stack([lax.axis_index("c"), 2 * lax.axis_index("x") + lax.axis_index("y")]).astype(jnp.int32)
    cshard = p["conv_w"].shape[1]
    rshard = p["rnn_conv_w"].shape[1]

    small = jnp.concatenate([_pad_rows(p["conv_w"]), _pad_rows(p["rnn_conv_w"])], axis=1)
    w_in_g, w_out_g, w1_g, w2_g, small_g = _gather_first(p["w_in"], p["w_out"], p["w_mlp_in"], p["w_mlp_out"], small)
    conv_full = small_g[:, :3, :cshard].transpose(1, 0, 2).reshape(3, CONV_W)
    rconv_full = small_g[:, :4, cshard:].transpose(1, 0, 2).reshape(4, LRU_W)
    wa_bd = _to_block_diag(p["w_a"]).astype(BF16)
    wx_bd = _to_block_diag(p["w_x"]).astype(BF16)
    gf = p["final_norm_g"].reshape(1, -1)
    lru = (wa_bd, p["b_a"], wx_bd, p["b_x"], p["lru_lambda"], p["g_norm_conv"], p["g_norm_rnn"])

    (u, h1b, xr, hs, c3, yb, *gates), (w_out_g, w1_g, w2_g) = _fwd_mix(
        xs, p["norm_mix_g"], w_in_g, conv_full, rconv_full, p["rnn_conv_b"], *lru, (w_out_g, w1_g, w2_g))
    zb, dpb, h2b, dx3b, dx2, dx2b, dy, st_mlp = _mlp_fwd_bwd(
        xs, yb, w_out_g.reshape(-1, D_MODEL), w1_g, w2_g.reshape(-1, D_MODEL), p["norm_mlp_g"], gf, target)

    part_out = _wgrad(yb, dx2b, "out", core_chip)
    *part_1, arrived_out = _wgrad(h2b, dpb, "mlp_in", core_chip, parts=(part_out[1],))
    *part_2, arrived_1 = _wgrad(zb, dx3b, "mlp_out", core_chip, parts=(part_1[1],))
    (dub, st_mix, dwa_bd, dwx_bd), (arrived_2,) = _mix_bwd(
        dy, u, xr, hs, c3, gates, conv_full, rconv_full, wa_bd, wx_bd, p["lru_lambda"], p["g_norm_conv"], p["g_norm_rnn"],
        (part_2[1],))
    arrived_mlp = (arrived_out, arrived_1, arrived_2)
    vec_pack = jnp.concatenate([st_mix, st_mlp, _zero_blk()], axis=0)
    mat_pack = jnp.concatenate([_from_block_diag(dwa_bd), _from_block_diag(dwx_bd)], axis=0).astype(BF16)
    *part_in, vec_packs, mat_packs = _wgrad(h1b, dub, "in", core_chip, packs=(vec_pack, mat_pack))
    early = (("w_out", "out", part_out, arrived_mlp[0]), ("w_mlp_in", "mlp_in", part_1, arrived_mlp[1]),
             ("w_mlp_out", "mlp_out", part_2, arrived_mlp[2]))
    (grad_x, st_in), arrived_in, joined = _in_bwd(
        dub, w_in_g, xs, dx2, p["norm_mix_g"], (part_in[1],),
        [(tag, p[n].shape, part[0], arrived) for n, tag, part, arrived in early], core_chip)
    g_in, mix_g_blocks = _join("in", p["w_in"].shape, part_in[0], arrived_in[0], core_chip, st_in)
    big = {}
    for n, tag, g in [(n, tag, g) for (n, tag, _, _), g in zip(early, joined)] + [("w_in", "in", g_in)]:
        big[n] = _adamw_big(p[n], g, p["m_" + n], p["v_" + n], "adamw_" + tag)

    conv_wmv = jnp.stack([_pad_rows(p[pre + "conv_w"]) for pre in ("", "m_", "v_")])
    rconv_wmv = jnp.stack([_pad_rows(p[pre + "rnn_conv_w"]) for pre in ("", "m_", "v_")])
    g_pack, d_pack, m_pack, v_pack, conv_out, rconv_out = _small_step(
        vec_packs, mat_packs, mix_g_blocks, _pack_params(p, ""), _pack_params(p, "m_"), _pack_params(p, "v_"), conv_wmv, rconv_wmv)

    def unpack(pk, kind):
        def vec(b, width=D_MODEL):
            return pk[b * 8:b * 8 + 1, :width]
        return {
            "norm_mix_g": vec(PK_MIX_G), "rnn_conv_b": vec(PK_RCONV_B), "b_a": vec(PK_B_A), "b_x": vec(PK_B_X),
            "lru_lambda": vec(PK_LAMBDA), "g_norm_conv": vec(PK_G_NORM_CONV, CONV_W), "g_norm_rnn": vec(PK_G_NORM_RNN),
            "norm_mlp_g": vec(PK_MLP_G), "final_norm_g": vec(PK_FINAL_G).reshape(-1),
            "w_a": pk[PK_W_A * 8:PK_W_A * 8 + 64].reshape(1, 16, 64, 64), "w_x": pk[PK_W_X * 8:PK_W_X * 8 + 64].reshape(1, 16, 64, 64),
            "conv_w": conv_out[kind, :3][None], "rnn_conv_w": rconv_out[kind, :4][None],
            "w_in": big["w_in"][kind][None], "w_out": big["w_out"][kind][None],
            "w_mlp_in": big["w_mlp_in"][kind][None], "w_mlp_out": big["w_mlp_out"][kind][None],
        }

    outs = [unpack(pk, kind) for kind, pk in enumerate((g_pack, d_pack, m_pack, v_pack))]
    for o in outs:
        for n in ("norm_mix_g", "rnn_conv_b", "b_a", "b_x", "lru_lambda", "g_norm_conv", "g_norm_rnn", "norm_mlp_g"):
            o[n] = o[n].reshape(1, -1)
    loss = g_pack[PK_LOSS * 8, 0]
    return (loss, grad_x[None], *[o[n] for o in outs for n in _NAMES])
```

```python
import functools
import math

import jax
import jax.numpy as jnp
from jax import lax
from jax.experimental import pallas as pl
from jax.experimental.pallas import tpu as pltpu

F32 = jnp.float32
BF16 = jnp.bfloat16
MESH = pl.DeviceIdType.MESH
ANY = pl.BlockSpec(memory_space=pl.ANY)
VMEM = pl.BlockSpec(memory_space=pltpu.VMEM)

EPS = 1e-6
LRU_C = 8.0
D_MODEL = 1024
CONV_W = 512
LRU_W = 1024
IN_COLS = 3 * CONV_W + 2 * LRU_W
IN_SHARD = IN_COLS // 4
N_CHIPS = 4
N_DEVICES = 8
BD = 256
N_BD = LRU_W // BD

ADAM_LR = 0.001
ADAM_B1 = 0.9
ADAM_B2 = 0.999
ADAM_EPS = 1e-08
ADAM_WD = 0.01
ADAM_STEP = 10
ADAM_BC1 = 1.0 - ADAM_B1 ** ADAM_STEP
ADAM_BC2 = 1.0 - ADAM_B2 ** ADAM_STEP

TILE_ROWS = 8
TOKEN_TILE = 256
MATMUL_TOKEN_TILE = 512
VMEM_LIMIT = 56 * 1024 * 1024

PK_G_NORM_RNN, PK_RCONV_B, PK_B_A, PK_B_X, PK_LAMBDA, PK_RCONV_W, PK_CONV_W, PK_G_NORM_CONV = range(8)
PK_FINAL_G, PK_MLP_G, PK_LOSS, PK_MIX_G = 8, 9, 10, 11
PK_W_A = 12
PK_W_X = 20
PK_BLOCKS = 28
PK_ROWS = PK_BLOCKS * TILE_ROWS


def _params(**kw):
    return pltpu.CompilerParams(vmem_limit_bytes=VMEM_LIMIT, **kw)


def _position():
    x, y, c = lax.axis_index("x"), lax.axis_index("y"), lax.axis_index("c")
    return x, y, c


def _sigmoid(v):
    return 1.0 / (1.0 + jnp.exp(-v))


def _one_minus_square(log_a, a):
    v = 2.0 * log_a
    series = -v * (1.0 + v * (0.5 + v * (1.0 / 6.0)))
    return jnp.where(v > -0.01, series, 1.0 - a * a)


_GELU_C = math.sqrt(2.0 / math.pi)
_GELU_K = 0.044715


def _gelu_and_grad(g):
    th = jnp.tanh(_GELU_C * (g + _GELU_K * g * g * g))
    gelu = 0.5 * g * (1.0 + th)
    dgelu = 0.5 * (1.0 + th) + 0.5 * g * (1.0 - th * th) * (_GELU_C * (1.0 + 3.0 * _GELU_K * g * g))
    return gelu, dgelu


def _rows(shape):
    return lax.broadcasted_iota(jnp.int32, shape, 0)


def _shift_down(v, k, prev8):
    rolled = pltpu.roll(v, k, 0)
    halo = pltpu.roll(prev8, k, 0)
    head = jnp.where(_rows(halo.shape) < k, halo, rolled[:TILE_ROWS])
    return jnp.concatenate([head, rolled[TILE_ROWS:]], axis=0)


def _shift_up(v, k, next8):
    n = v.shape[0]
    rolled = pltpu.roll(v, n - k, 0)
    halo = pltpu.roll(next8, TILE_ROWS - k, 0)
    tail = jnp.where(_rows(halo.shape) >= TILE_ROWS - k, halo, rolled[n - TILE_ROWS:])
    return jnp.concatenate([rolled[: n - TILE_ROWS], tail], axis=0)


def _scan_rows(a, b, carry, reverse=False):
    n, w = a.shape
    groups = n // TILE_ROWS
    a3 = a.reshape(groups, TILE_ROWS, w)
    b3 = b.reshape(groups, TILE_ROWS, w)
    sub = lax.broadcasted_iota(jnp.int32, a3.shape, 1)
    s = 1
    while s < TILE_ROWS:
        shift = TILE_ROWS - s if reverse else s
        keep = (sub < TILE_ROWS - s) if reverse else (sub >= s)
        b3 = b3 + jnp.where(keep, a3 * pltpu.roll(b3, shift, 1), 0.0)
        a3 = a3 * jnp.where(keep, pltpu.roll(a3, shift, 1), 1.0)
        s *= 2
    out = [None] * groups
    edge = 0 if reverse else TILE_ROWS - 1
    for g in (range(groups - 1, -1, -1) if reverse else range(groups)):
        out[g] = b3[g] + a3[g] * carry
        carry = out[g][edge:edge + 1]
    return jnp.concatenate(out, axis=0)


def _softplus_neg(lam):
    e = jnp.exp(-jnp.abs(lam))
    log1p_e = jnp.where(e < 1e-2, e * (1.0 - e * (0.5 - e * (1.0 / 3.0 - e * 0.25))), jnp.log(1.0 + e))
    sp = jnp.maximum(-lam, 0.0) + log1p_e
    dsp = -_sigmoid(-lam)
    return sp, dsp


def _block_diag_dot(vb, w_ref):
    return jnp.concatenate(
        [jnp.dot(vb[:, j * BD:(j + 1) * BD], w_ref[j], preferred_element_type=F32) for j in range(N_BD)], axis=1)


def _block_diag_dot_t(vb, w_ref):
    return jnp.concatenate(
        [lax.dot_general(vb[:, j * BD:(j + 1) * BD], w_ref[j], (((1,), (1,)), ((), ())), preferred_element_type=F32)
         for j in range(N_BD)], axis=1)


def _dot_nt(a, b):
    return lax.dot_general(a, b, (((1,), (1,)), ((), ())), preferred_element_type=F32)


def _dot_tn(a, b):
    return lax.dot_general(a, b, (((0,), (0,)), ((), ())), preferred_element_type=F32)


def _lru_gates(xr, wa_ref, ba, wx_ref, bx, sp):
    xrb = xr.astype(BF16)
    r = _sigmoid(_block_diag_dot(xrb, wa_ref) + ba)
    ig = _sigmoid(_block_diag_dot(xrb, wx_ref) + bx)
    log_a = (-LRU_C) * r * sp
    a = jnp.exp(log_a)
    mult = jnp.sqrt(_one_minus_square(log_a, a))
    return r, ig, a, mult


def _colsum(v):
    return jnp.sum(v, axis=0, keepdims=True)


N_FWD_OUT = 10


def _fwd_mix(x, g1, w_in_g, conv_w, rconv_w, rconv_b, wa_bd, b_a, wx_bd, b_x, lam, g_nc, g_nr, later):
    t, d = x.shape
    tm = TOKEN_TILE
    nt = t // tm
    nl = len(later)
    assert nl == 3
    pass_on_at = [nt * f // 16 for f in (1, 3, 7)]
    neighbours_at = [nt * f // 16 for f in (3, 6, 10)]
    diagonal_at = [nt * f // 16 for f in (10, 12, 14)]

    def body(x_ref, g1_ref, win_ref, cw_ref, rw_ref, rb_ref, wa_ref, ba_ref, wx_ref, bx_ref, lam_ref, gnc_ref, gnr_ref,
             *rest):
        later_in, outs, rest = rest[:nl], rest[nl:nl + N_FWD_OUT], rest[nl + N_FWD_OUT:]
        u_ref, h1_ref, xr_ref, hs_ref, c3_ref, y_ref, r_ref, ig_ref, a_ref, mult_ref = outs
        later_out, (cv_prev, xin_prev, h_prev, send_sems, recv_sems) = rest[:nl], rest[nl:]
        del later_in
        step = pl.program_id(0)
        plan = _ShardGather(later_out, send_sems, recv_sems)

        @pl.when(step == 0)
        def _():
            cv_prev[...] = jnp.zeros_like(cv_prev)
            xin_prev[...] = jnp.zeros_like(xin_prev)
            h_prev[...] = jnp.zeros_like(h_prev)
            for w in range(nl):
                plan.start_direct(w)

        for w in range(nl):
            @pl.when(step == pass_on_at[w])
            def _(w=w):
                plan.start_pass_on(w)

            @pl.when(step == neighbours_at[w])
            def _(w=w):
                plan.start_hand_over(w, diagonal=False)

            @pl.when(step == diagonal_at[w])
            def _(w=w):
                plan.start_hand_over(w, diagonal=True)

        xv = x_ref[...]
        rstd = lax.rsqrt(jnp.mean(xv * xv, axis=-1, keepdims=True) + EPS)
        h1b = ((xv * rstd) * g1_ref[...]).astype(BF16)
        h1_ref[...] = h1b
        for j in range(N_CHIPS):
            u_ref[:, j * IN_SHARD:(j + 1) * IN_SHARD] = jnp.dot(h1b, win_ref[j], preferred_element_type=F32)
        gate_b = u_ref[:, 0:CONV_W]
        cv = u_ref[:, CONV_W:2 * CONV_W] * u_ref[:, 2 * CONV_W:3 * CONV_W]
        x_r = u_ref[:, 3 * CONV_W:3 * CONV_W + LRU_W]
        g = u_ref[:, 3 * CONV_W + LRU_W:]

        cw = cw_ref[...]
        cvp = cv_prev[...]
        conv3 = cw[0:1] * _shift_down(cv, 2, cvp) + cw[1:2] * _shift_down(cv, 1, cvp) + cw[2:3] * cv
        cv_prev[...] = cv[tm - TILE_ROWS:]
        c3_ref[...] = conv3
        y_conv = gate_b * conv3

        rw = rw_ref[...]
        xp = xin_prev[...]
        xr = (rw[0:1] * _shift_down(x_r, 3, xp) + rw[1:2] * _shift_down(x_r, 2, xp)
              + rw[2:3] * _shift_down(x_r, 1, xp) + rw[3:4] * x_r) + rb_ref[...]
        xin_prev[...] = x_r[tm - TILE_ROWS:]
        xr_ref[...] = xr
        sp, _ = _softplus_neg(lam_ref[...])
        r, ig, a, mult = _lru_gates(xr, wa_ref, ba_ref[...], wx_ref, bx_ref[...], sp)
        r_ref[...] = r
        ig_ref[...] = ig
        a_ref[...] = a
        mult_ref[...] = mult
        h = _scan_rows(a, mult * (ig * xr), h_prev[...])
        h_prev[...] = h[tm - 1:tm]
        hs_ref[...] = h
        gelu, _ = _gelu_and_grad(g)
        y_rnn = h * gelu

        na = y_conv * lax.rsqrt(jnp.mean(y_conv * y_conv, axis=-1, keepdims=True) + EPS) * gnc_ref[...]
        nb = y_rnn * lax.rsqrt(jnp.mean(y_rnn * y_rnn, axis=-1, keepdims=True) + EPS) * gnr_ref[...]
        y_ref[:, :CONV_W] = na.astype(BF16)
        y_ref[:, CONV_W:] = nb.astype(BF16)

        @pl.when(step == nt - 1)
        def _():
            for w in range(nl):
                plan.finish(w)

    def full(a):
        nd = a.ndim
        return pl.BlockSpec(a.shape, lambda i: (0,) * nd)

    def tok(cols):
        return pl.BlockSpec((tm, cols), lambda i: (i, 0))

    def act(cols, dtype=F32):
        return jax.ShapeDtypeStruct((t, cols), dtype)

    smalls = (g1, w_in_g, conv_w, rconv_w, rconv_b, wa_bd, b_a, wx_bd, b_x, lam, g_nc, g_nr)
    n_in = 1 + len(smalls)
    outs = pl.pallas_call(
        body, name="fwd_mix", grid=(nt,),
        in_specs=[tok(d)] + [full(a) for a in smalls] + [ANY] * nl,
        out_specs=[tok(IN_COLS), tok(d), tok(LRU_W), tok(LRU_W), tok(CONV_W), tok(CONV_W + LRU_W)]
        + [tok(LRU_W)] * 4 + [ANY] * nl,
        out_shape=[act(IN_COLS), act(d, BF16), act(LRU_W), act(LRU_W), act(CONV_W), act(CONV_W + LRU_W, BF16)]
        + [act(LRU_W)] * 4 + [jax.ShapeDtypeStruct(a.shape, a.dtype) for a in later],
        input_output_aliases={n_in + w: N_FWD_OUT + w for w in range(nl)},
        scratch_shapes=[pltpu.VMEM((TILE_ROWS, CONV_W), F32), pltpu.VMEM((TILE_ROWS, LRU_W), F32),
                        pltpu.VMEM((1, LRU_W), F32), pltpu.SemaphoreType.DMA((nl, _ShardGather.PAIRS)),
                        pltpu.SemaphoreType.DMA((nl, _ShardGather.PAIRS))],
        compiler_params=_params(dimension_semantics=("arbitrary",)),
    )(x, *smalls, *later)
    return outs[:N_FWD_OUT], outs[N_FWD_OUT:]


def _mlp_fwd_bwd(x, yb, w_out_g, w1_g, w2_g, g2, gf, target):
    t, d = x.shape
    tm = TOKEN_TILE
    ff = w2_g.shape[0]
    mix = w_out_g.shape[0]
    ffs = ff // N_CHIPS

    def body(x_ref, y_ref, g2_ref, gf_ref, tgt_ref, wout_hbm, w1_hbm, w2_hbm,
             z_ref, dp_ref, h2_ref, dx3b_ref, dx2_ref, dx2b_ref, dy_ref, st_ref, wout, w1, w2, p_ref):
        @pl.when(pl.program_id(0) == 0)
        def _():
            pltpu.sync_copy(wout_hbm, wout)
            pltpu.sync_copy(w1_hbm, w1)
            pltpu.sync_copy(w2_hbm, w2)
            st_ref[...] = jnp.zeros_like(st_ref)

        x2 = x_ref[...] + jnp.dot(y_ref[...], wout[...], preferred_element_type=F32)
        r2 = lax.rsqrt(jnp.mean(x2 * x2, axis=-1, keepdims=True) + EPS)
        xh2 = x2 * r2
        g2v = g2_ref[...]
        h2b = (xh2 * g2v).astype(BF16)
        h2_ref[...] = h2b
        for j in range(N_CHIPS):
            p_ref[:, j * ffs:(j + 1) * ffs] = jnp.dot(h2b, w1[j], preferred_element_type=F32)
        rp = jnp.maximum(p_ref[...], 0.0)
        zb = (rp * rp).astype(BF16)
        z_ref[...] = zb
        x3 = x2 + jnp.dot(zb, w2[...], preferred_element_type=F32)
        r3 = lax.rsqrt(jnp.mean(x3 * x3, axis=-1, keepdims=True) + EPS)
        xh3 = x3 * r3
        gfv = gf_ref[...]
        err = xh3 * gfv - tgt_ref[...]
        loss = (0.5 / d) * jnp.sum(err * err)
        dout = err * (1.0 / d)
        st_ref[PK_FINAL_G * 8 - 64:PK_FINAL_G * 8 - 63, :] += _colsum(dout * xh3)
        st_ref[PK_LOSS * 8 - 64:PK_LOSS * 8 - 63, :] += jnp.zeros((1, d), F32) + loss
        dxh3 = dout * gfv
        dx3 = r3 * (dxh3 - xh3 * jnp.mean(dxh3 * xh3, axis=-1, keepdims=True))
        dx3b = dx3.astype(BF16)
        dx3b_ref[...] = dx3b
        dpb = (_dot_nt(dx3b, w2[...]) * (2.0 * rp)).astype(BF16)
        dp_ref[...] = dpb
        dh2 = _dot_nt(dpb[:, 0:ffs], w1[0])
        for j in range(1, N_CHIPS):
            dh2 = dh2 + _dot_nt(dpb[:, j * ffs:(j + 1) * ffs], w1[j])
        st_ref[PK_MLP_G * 8 - 64:PK_MLP_G * 8 - 63, :] += _colsum(dh2 * xh2)
        dxh2 = dh2 * g2v
        dx2 = dx3 + r2 * (dxh2 - xh2 * jnp.mean(dxh2 * xh2, axis=-1, keepdims=True))
        dx2_ref[...] = dx2
        dx2b = dx2.astype(BF16)
        dx2b_ref[...] = dx2b
        dy_ref[...] = _dot_nt(dx2b, wout[...])

    def tok(cols):
        return pl.BlockSpec((tm, cols), lambda i: (i, 0))

    def row(cols):
        return pl.BlockSpec((1, cols), lambda i: (0, 0))

    return pl.pallas_call(
        body, name="mlp_fwd_bwd", grid=(t // tm,),
        in_specs=[tok(d), tok(mix), row(d), row(d), tok(d), ANY, ANY, ANY],
        out_specs=[tok(ff), tok(ff), tok(d), tok(d), tok(d), tok(d), tok(mix),
                   pl.BlockSpec((3 * TILE_ROWS, d), lambda i: (0, 0))],
        out_shape=[jax.ShapeDtypeStruct((t, ff), BF16), jax.ShapeDtypeStruct((t, ff), BF16),
                   jax.ShapeDtypeStruct((t, d), BF16), jax.ShapeDtypeStruct((t, d), BF16),
                   jax.ShapeDtypeStruct((t, d), F32), jax.ShapeDtypeStruct((t, d), BF16),
                   jax.ShapeDtypeStruct((t, mix), F32), jax.ShapeDtypeStruct((3 * TILE_ROWS, d), F32)],
        scratch_shapes=[pltpu.VMEM(w_out_g.shape, BF16), pltpu.VMEM(w1_g.shape, BF16), pltpu.VMEM(w2_g.shape, BF16),
                        pltpu.VMEM((tm, ff), F32)],
        compiler_params=_params(dimension_semantics=("arbitrary",)),
    )(x, yb, g2, gf, target, w_out_g, w1_g, w2_g)


def _mix_bwd(dy, u, xr_all, hs_all, c3_all, gates, conv_w, rconv_w, wa_bd, wx_bd, lam, g_nc, g_nr, parts):
    t = dy.shape[0]
    tm = TOKEN_TILE
    nt = t // tm
    hb = tm // TILE_ROWS
    npart = len(parts)

    def body(dy_ref, u_ref, uh_ref, xr_ref, hs_ref, hh_ref, c3_ref, r_ref, ig_ref, a_ref, mult_ref,
             cw_ref, rw_ref, wa_ref, wx_ref, lam_ref, gnc_ref, gnr_ref, *rest):
        part_refs, (du_ref, st_ref, dwa_ref, dwx_ref), rest = rest[:npart], rest[npart:npart + 4], rest[npart + 4:]
        arrived_refs, (dc_next, a_next, gs_next, dxr_next, send_sems, recv_sems) = rest[:npart], rest[npart:]
        exchange = _PartialExchange(part_refs, arrived_refs, send_sems, recv_sems)
        i = pl.program_id(0)

        @pl.when(i == 0)
        def _():
            exchange.start()
            dc_next[...] = jnp.zeros_like(dc_next)
            a_next[...] = jnp.zeros_like(a_next)
            gs_next[...] = jnp.zeros_like(gs_next)
            dxr_next[...] = jnp.zeros_like(dxr_next)
            st_ref[...] = jnp.zeros_like(st_ref)
            dwa_ref[...] = jnp.zeros_like(dwa_ref)
            dwx_ref[...] = jnp.zeros_like(dwx_ref)

        first_tile = i == nt - 1
        gate_b = u_ref[:, 0:CONV_W]
        gate_c = u_ref[:, CONV_W:2 * CONV_W]
        v = u_ref[:, 2 * CONV_W:3 * CONV_W]
        x_r = u_ref[:, 3 * CONV_W:3 * CONV_W + LRU_W]
        g = u_ref[:, 3 * CONV_W + LRU_W:]
        cv = gate_c * v
        cv_prev = jnp.where(first_tile, 0.0, uh_ref[:, CONV_W:2 * CONV_W] * uh_ref[:, 2 * CONV_W:3 * CONV_W])
        xin_prev = jnp.where(first_tile, 0.0, uh_ref[:, 3 * CONV_W:3 * CONV_W + LRU_W])
        hs_prev = jnp.where(first_tile, 0.0, hh_ref[...])

        def acc(block, val, width=LRU_W, row=0):
            r0 = block * TILE_ROWS + row
            st_ref[r0:r0 + 1, 0:width] += val

        conv3 = c3_ref[...]
        y_conv = gate_b * conv3
        ra = lax.rsqrt(jnp.mean(y_conv * y_conv, axis=-1, keepdims=True) + EPS)
        xha = y_conv * ra
        dna = dy_ref[:, :CONV_W]
        acc(PK_G_NORM_CONV, _colsum(dna * xha), CONV_W)
        dxha = dna * gnc_ref[...]
        dy_conv = ra * (dxha - xha * jnp.mean(dxha * xha, axis=-1, keepdims=True))
        du_ref[:, 0:CONV_W] = (dy_conv * conv3).astype(BF16)
        dc = dy_conv * gate_b
        cw = cw_ref[...]
        dcn = dc_next[...]
        dcv = cw[2:3] * dc + cw[1:2] * _shift_up(dc, 1, dcn) + cw[0:1] * _shift_up(dc, 2, dcn)
        dc_next[...] = dc[:TILE_ROWS]
        acc(PK_CONV_W, _colsum(dc * _shift_down(cv, 2, cv_prev)), CONV_W, 0)
        acc(PK_CONV_W, _colsum(dc * _shift_down(cv, 1, cv_prev)), CONV_W, 1)
        acc(PK_CONV_W, _colsum(dc * cv), CONV_W, 2)
        du_ref[:, CONV_W:2 * CONV_W] = (dcv * v).astype(BF16)
        du_ref[:, 2 * CONV_W:3 * CONV_W] = (dcv * gate_c).astype(BF16)

        hs = hs_ref[...]
        gelu, dgelu = _gelu_and_grad(g)
        y_rnn = hs * gelu
        rb = lax.rsqrt(jnp.mean(y_rnn * y_rnn, axis=-1, keepdims=True) + EPS)
        xhb = y_rnn * rb
        dnb = dy_ref[:, CONV_W:]
        acc(PK_G_NORM_RNN, _colsum(dnb * xhb))
        dxhb = dnb * gnr_ref[...]
        dy_rnn = rb * (dxhb - xhb * jnp.mean(dxhb * xhb, axis=-1, keepdims=True))
        du_ref[:, 3 * CONV_W + LRU_W:] = (dy_rnn * hs * dgelu).astype(BF16)
        dh = dy_rnn * gelu

        xr = xr_ref[...]
        xrb = xr.astype(BF16)
        sp, dsp = _softplus_neg(lam_ref[...])
        r, ig, a, mult = r_ref[...], ig_ref[...], a_ref[...], mult_ref[...]
        a_up = _shift_up(a, 1, a_next[...])
        a_next[...] = a[:TILE_ROWS]
        gs = _scan_rows(a_up, dh, gs_next[0:1, :], reverse=True)
        gs_next[...] = gs[:TILE_ROWS]
        da = gs * _shift_down(hs, 1, hs_prev)
        gx = gs * xr
        di = gx * mult
        dmult = gx * ig
        dxr = gs * (mult * ig)
        dlog_a = da * a - dmult * ((a * a) / mult)
        acc(PK_LAMBDA, _colsum(dlog_a * r) * ((-LRU_C) * dsp))
        dpa = (dlog_a * ((-LRU_C) * sp)) * (r * (1.0 - r))
        dpx = di * (ig * (1.0 - ig))
        acc(PK_B_A, _colsum(dpa))
        acc(PK_B_X, _colsum(dpx))
        dpab = dpa.astype(BF16)
        dpxb = dpx.astype(BF16)
        dxr = dxr + _block_diag_dot_t(dpab, wa_ref) + _block_diag_dot_t(dpxb, wx_ref)
        for j in range(N_BD):
            cols = slice(j * BD, (j + 1) * BD)
            dwa_ref[j] += _dot_tn(xrb[:, cols], dpab[:, cols])
            dwx_ref[j] += _dot_tn(xrb[:, cols], dpxb[:, cols])

        acc(PK_RCONV_B, _colsum(dxr))
        rw = rw_ref[...]
        dxn = dxr_next[...]
        dx_r = (rw[3:4] * dxr + rw[2:3] * _shift_up(dxr, 1, dxn) + rw[1:2] * _shift_up(dxr, 2, dxn)
                + rw[0:1] * _shift_up(dxr, 3, dxn))
        dxr_next[...] = dxr[:TILE_ROWS]
        for k in range(3):
            acc(PK_RCONV_W, _colsum(dxr * _shift_down(x_r, 3 - k, xin_prev)), LRU_W, k)
        acc(PK_RCONV_W, _colsum(dxr * x_r), LRU_W, 3)
        du_ref[:, 3 * CONV_W:3 * CONV_W + LRU_W] = dx_r.astype(BF16)

        @pl.when(i == nt - 1)
        def _():
            exchange.wait()

    def full(a):
        nd = a.ndim
        return pl.BlockSpec(a.shape, lambda i: (0,) * nd)

    def tok(cols):
        return pl.BlockSpec((tm, cols), lambda i: (nt - 1 - i, 0))

    def halo(cols):
        return pl.BlockSpec((TILE_ROWS, cols), lambda i: (jnp.maximum((nt - 1 - i) * hb - 1, 0), 0))

    smalls = (conv_w, rconv_w, wa_bd, wx_bd, lam, g_nc, g_nr)
    outs = pl.pallas_call(
        body, name="mix_bwd", grid=(nt,),
        in_specs=[tok(CONV_W + LRU_W), tok(IN_COLS), halo(IN_COLS), tok(LRU_W), tok(LRU_W), halo(LRU_W), tok(CONV_W)]
        + [tok(LRU_W)] * 4 + [full(a) for a in smalls] + [ANY] * npart,
        out_specs=[tok(IN_COLS), pl.BlockSpec((8 * TILE_ROWS, LRU_W), lambda i: (0, 0)),
                   pl.BlockSpec((N_BD, BD, BD), lambda i: (0, 0, 0)), pl.BlockSpec((N_BD, BD, BD), lambda i: (0, 0, 0))]
        + [ANY] * npart,
        out_shape=[jax.ShapeDtypeStruct((t, IN_COLS), BF16), jax.ShapeDtypeStruct((8 * TILE_ROWS, LRU_W), F32),
                   jax.ShapeDtypeStruct((N_BD, BD, BD), F32), jax.ShapeDtypeStruct((N_BD, BD, BD), F32)]
        + [jax.ShapeDtypeStruct(a.shape, a.dtype) for a in parts],
        scratch_shapes=[pltpu.VMEM((TILE_ROWS, CONV_W), F32), pltpu.VMEM((TILE_ROWS, LRU_W), F32),
                        pltpu.VMEM((TILE_ROWS, LRU_W), F32), pltpu.VMEM((TILE_ROWS, LRU_W), F32),
                        pltpu.SemaphoreType.DMA((npart, 3)), pltpu.SemaphoreType.DMA((npart, 3))],
        compiler_params=_params(dimension_semantics=("arbitrary",)),
    )(dy, u, u, xr_all, hs_all, hs_all, c3_all, *gates, *smalls, *parts)
    return outs[:4], outs[4:]


def _in_bwd(dub, w_in_g, x, dx2, g1, parts, joins, core_chip):
    t, d = x.shape
    tm = min(t, MATMUL_TOKEN_TILE)
    nt = t // tm
    npart = len(parts)
    nj = len(joins)
    geometry = []
    for tag, shape, _, _ in joins:
        pr, pc = WGRAD_GEOMETRY[tag][:2]
        every = 1 if pr % (nt * 16) == 0 else 2
        geometry.append((pr, pc, pr * every // nt, every, shape[1] == pc))

    def body(cc_ref, du_ref, win_ref, x_ref, dx2_ref, g1_ref, *rest):
        sums, rest = [rest[4 * w:4 * w + 4] for w in range(nj)], rest[4 * nj:]
        part_refs, (gx_ref, st_ref), rest = rest[:npart], rest[npart:npart + 2], rest[npart + 2:]
        arrived_refs, joined, rest = rest[:npart], rest[npart:npart + nj], rest[npart + nj:]
        stages, (send_sems, recv_sems, j_local, j_send, j_recv) = rest[:nj], rest[nj:]
        exchange = _PartialExchange(part_refs, arrived_refs, send_sems, recv_sems)
        i = pl.program_id(0)
        c = cc_ref[0]

        def window(w, core, row0, rows):
            pr, pc, _, _, by_rows = geometry[w]
            if by_rows:
                return joined[w].at[pl.ds(core * pr + row0, rows), :]
            return joined[w].at[pl.ds(row0, rows), pl.ds(core * pc, pc)]

        def to_sibling(w, src, core, row0, rows):
            return pltpu.make_async_remote_copy(src_ref=src, dst_ref=window(w, core, row0, rows), send_sem=j_send.at[w],
                                                recv_sem=j_recv.at[w], device_id=_sibling(), device_id_type=MESH)

        @pl.when(i == 0)
        def _():
            exchange.start()
            st_ref[...] = jnp.zeros_like(st_ref)

        for w in range(nj):
            pr, pc, rb, every, _ = geometry[w]

            @pl.when(i % every == 0)
            def _(w=w, rb=rb, every=every):
                p_ref, r1_ref, r2_ref, r3_ref = sums[w]
                row0 = pl.multiple_of((i // every) * rb, rb)
                rows = stages[w].at[pl.ds(row0, rb), :]
                rows[...] = ((p_ref[0] + r1_ref[0].astype(F32)) + r2_ref[0].astype(F32)) + r3_ref[0].astype(F32)
                pltpu.make_async_copy(rows, window(w, c, row0, rb), j_local.at[w]).start()
                to_sibling(w, rows, c, row0, rb).start()

        dh1 = _dot_nt(du_ref[:, 0:IN_SHARD], win_ref[0])
        for j in range(1, N_CHIPS):
            dh1 = dh1 + _dot_nt(du_ref[:, j * IN_SHARD:(j + 1) * IN_SHARD], win_ref[j])
        xv = x_ref[...]
        rstd = lax.rsqrt(jnp.mean(xv * xv, axis=-1, keepdims=True) + EPS)
        xh = xv * rstd
        st_ref[0:1, :] += _colsum(dh1 * xh)
        dxh = dh1 * g1_ref[...]
        gx_ref[...] = dx2_ref[...] + rstd * (dxh - xh * jnp.mean(dxh * xh, axis=-1, keepdims=True))

        @pl.when(i == nt - 1)
        def _():
            exchange.wait()
            for w in range(nj):
                pr = geometry[w][0]
                pltpu.make_async_copy(stages[w], window(w, c, 0, pr), j_local.at[w]).wait()
                to_sibling(w, stages[w], 1 - c, 0, pr).wait()

    def tok(cols):
        return pl.BlockSpec((tm, cols), lambda i, cc: (i, 0))

    def partial(w, off):
        pr, pc, rb, every, _ = geometry[w]
        return pl.BlockSpec((1, rb, pc), lambda i, cc: ((cc[1] + off) % N_CHIPS, i // every, 0))

    sum_specs, sum_operands = [], []
    for w, (_, _, own, arrived) in enumerate(joins):
        sum_specs += [partial(w, off) for off in range(N_CHIPS)]
        sum_operands += [own, arrived, arrived, arrived]
    dma = pltpu.SemaphoreType.DMA
    outs = pl.pallas_call(
        body, name="in_bwd",
        grid_spec=pltpu.PrefetchScalarGridSpec(
            num_scalar_prefetch=1, grid=(nt,),
            in_specs=[tok(IN_COLS), pl.BlockSpec(w_in_g.shape, lambda i, cc: (0, 0, 0)), tok(d), tok(d),
                      pl.BlockSpec((1, d), lambda i, cc: (0, 0))] + sum_specs + [ANY] * npart,
            out_specs=[tok(d), pl.BlockSpec((TILE_ROWS, d), lambda i, cc: (0, 0))] + [ANY] * (npart + nj),
            scratch_shapes=[pltpu.VMEM((g[0], g[1]), F32) for g in geometry]
            + [dma((npart, 3)), dma((npart, 3)), dma((nj,)), dma((nj,)), dma((nj,))]),
        out_shape=[jax.ShapeDtypeStruct((t, d), F32), jax.ShapeDtypeStruct((TILE_ROWS, d), F32)]
        + [jax.ShapeDtypeStruct(a.shape, a.dtype) for a in parts]
        + [jax.ShapeDtypeStruct(shape, F32) for _, shape, _, _ in joins],
        compiler_params=_params(dimension_semantics=("arbitrary",)),
    )(core_chip, dub, w_in_g, x, dx2, g1, *sum_operands, *parts)
    return outs[:2], outs[2:2 + npart], outs[2 + npart:]


WGRAD_GEOMETRY = {
    "in": (512, IN_SHARD, lambda s, h: h, lambda s, h: s),
    "mlp_in": (512, D_MODEL, lambda s, h: h, lambda s, h: s),
    "mlp_out": (512, D_MODEL, lambda s, h: 2 * s + h, lambda s, h: 0),
    "out": (384, 512, lambda s, h: s, lambda s, h: h),
}
K_CHUNK = 512


def _sibling():
    x, y, c = _position()
    return (x, y, 1 - c)


def _wgrad(a, b, tag, core_chip, packs=(), parts=()):
    t = a.shape[0]
    pr, pc, a_blk, b_blk = WGRAD_GEOMETRY[tag]
    nk = t // K_CHUNK
    mine = N_CHIPS
    riding = len(packs)
    npart = len(parts)
    assert not (riding and npart)

    def body(cc_ref, a_ref, b_ref, *rest):
        if riding:
            pack_refs, (land_ref, p_ref, pb_ref), rest = rest[:riding], rest[riding:riding + 3], rest[riding + 3:]
            all_refs, (stage, rbuf, send_sems, recv_sems, rsem), g_sems = rest[:riding], rest[riding:riding + 5], rest[riding + 5:]
            gathers = [_PackGather(pack_refs[n], all_refs[n], *g_sems[3 * n:3 * n + 3]) for n in range(riding)]
        elif npart:
            part_refs, (land_ref, p_ref, pb_ref), rest = rest[:npart], rest[npart:npart + 3], rest[npart + 3:]
            arrived_refs, (stage, rbuf, send_sems, recv_sems, rsem, x_send, x_recv) = rest[:npart], rest[npart:]
            exchange = _PartialExchange(part_refs, arrived_refs, x_send, x_recv)
        else:
            land_ref, p_ref, pb_ref, stage, rbuf, send_sems, recv_sems, rsem = rest
        ph, s = pl.program_id(0), pl.program_id(1)
        if riding:
            @pl.when((ph == 0) & (s == 0))
            def _():
                for gather in gathers:
                    gather.start()

            @pl.when((ph == 1) & (s == N_CHIPS - 2))
            def _():
                for gather in gathers:
                    gather.hand_over()
        if npart:
            @pl.when((ph == 0) & (s == 0))
            def _():
                exchange.start()
        def push(k):
            return pltpu.make_async_remote_copy(src_ref=stage.at[k], dst_ref=land_ref.at[k], send_sem=send_sems.at[k],
                                                recv_sem=recv_sems.at[k], device_id=_sibling(), device_id_type=MESH)

        def landed():
            return pltpu.make_async_copy(land_ref.at[s], rbuf, rsem)

        @pl.when(ph == 1)
        def _():
            push(s).wait_recv()
            landed().start()

        slot = jnp.where(ph == 0, s, mine)
        acc = stage.at[slot]
        acc[...] = _dot_tn(a_ref[0:K_CHUNK, :], b_ref[0:K_CHUNK, :])
        for k in range(1, nk):
            acc[...] += _dot_tn(a_ref[k * K_CHUNK:(k + 1) * K_CHUNK, :], b_ref[k * K_CHUNK:(k + 1) * K_CHUNK, :])

        @pl.when(ph == 0)
        def _():
            push(s).start()

        @pl.when(ph == 1)
        def _():
            landed().wait()
            p = stage[mine] + rbuf[...]
            p_ref[0] = p
            pb_ref[0] = p.astype(BF16)

        @pl.when((ph == 1) & (s == N_CHIPS - 1))
        def _():
            for k in range(N_CHIPS):
                push(k).wait_send()
            for gather in (gathers if riding else ()):
                gather.finish()
            if npart:
                exchange.wait()

    def half(ph, cc):
        return jnp.where(ph == 0, 1 - cc[0], cc[0])

    def out_slot(ph, s, cc):
        return (jnp.where(ph == 0, 0, s), 0, 0)

    piece = jax.ShapeDtypeStruct((N_CHIPS, pr, pc), F32)
    in_specs = [pl.BlockSpec((t, pr), lambda ph, s, cc: (0, a_blk(s, half(ph, cc)))),
                pl.BlockSpec((t, pc), lambda ph, s, cc: (0, b_blk(s, half(ph, cc))))]
    out_specs = [ANY, pl.BlockSpec((1, pr, pc), out_slot), pl.BlockSpec((1, pr, pc), out_slot)]
    out_shape = [piece, piece, jax.ShapeDtypeStruct((N_CHIPS, pr, pc), BF16)]
    scratch = [pltpu.VMEM((N_CHIPS + 1, pr, pc), F32), pltpu.VMEM((pr, pc), F32),
               pltpu.SemaphoreType.DMA((N_CHIPS,)), pltpu.SemaphoreType.DMA((N_CHIPS,)), pltpu.SemaphoreType.DMA]
    operands = [a, b]
    for pack in packs:
        in_specs.append(pl.BlockSpec(pack.shape, lambda ph, s, cc: (0, 0)))
        out_specs.append(ANY)
        out_shape.append(jax.ShapeDtypeStruct((N_DEVICES,) + pack.shape, pack.dtype))
        operands.append(pack)
    for pack in packs:
        scratch += _PackGather.semaphores()
    if npart:
        in_specs += [ANY] * npart
        out_specs += [ANY] * npart
        out_shape += [jax.ShapeDtypeStruct(p.shape, p.dtype) for p in parts]
        scratch += [pltpu.SemaphoreType.DMA((npart, 3)), pltpu.SemaphoreType.DMA((npart, 3))]
        operands += list(parts)
    return pl.pallas_call(
        body, name="wgrad_" + tag,
        grid_spec=pltpu.PrefetchScalarGridSpec(
            num_scalar_prefetch=1, grid=(2, N_CHIPS), in_specs=in_specs, out_specs=out_specs, scratch_shapes=scratch),
        out_shape=out_shape,
        compiler_params=_params(dimension_semantics=("arbitrary", "arbitrary")),
    )(core_chip, *operands)[1:]


def _other_chips(x, y):
    return [(1 - x, y), (x, 1 - y), (1 - x, 1 - y)]


class _ShardGather:
    PAIRS = 9

    def __init__(self, outs, send_sems, recv_sems):
        self.outs, self.send_sems, self.recv_sems = outs, send_sems, recv_sems
        x, y, c = _position()
        self.c, self.j = c, 2 * x + y
        self.sibling = (x, y, 1 - c)
        self.chips = _other_chips(x, y)

    def _chip(self, k):
        px, py = self.chips[k]
        return 2 * px + py

    def _half(self, w, chip, which):
        hr = self.outs[w].shape[1] // 2
        return self.outs[w].at[chip, pl.ds(which * hr, hr), :]

    def _quarter(self, w, chip, q):
        qr = self.outs[w].shape[1] // 4
        return self.outs[w].at[chip, pl.ds(self.c * 2 * qr + q * qr, qr), :]

    def _copy(self, ref, w, pair, to, src=None):
        return pltpu.make_async_remote_copy(src_ref=ref if src is None else src, dst_ref=ref, send_sem=self.send_sems.at[w, pair],
                                            recv_sem=self.recv_sems.at[w, pair], device_id=to, device_id_type=MESH)

    def direct(self, w, k, q, src=None):
        return self._copy(self._quarter(w, self.j, q), w, 2 * k + q, (*self.chips[k], self.c), src)

    def direct_landed(self, w, k, q):
        return self._copy(self._quarter(w, self._chip(k), q), w, 2 * k + q, (*self.chips[k], self.c))

    def pass_on(self, w, q):
        return self._copy(self._quarter(w, self._chip(q), q), w, 4 + q, (*self.chips[1 - q], self.c))

    def passed_landed(self, w, q):
        return self._copy(self._quarter(w, self._chip(2), q), w, 4 + q, (*self.chips[1 - q], self.c))

    def hand_over(self, w, k):
        return self._copy(self._half(w, self._chip(k), self.c), w, 6 + k, self.sibling)

    def handed(self, w, k):
        return self._copy(self._half(w, self._chip(k), 1 - self.c), w, 6 + k, self.sibling)

    def start_direct(self, w, src_half=None):
        qr = self.outs[w].shape[1] // 4
        for k, q in ((0, 0), (1, 1), (0, 1), (1, 0)):
            self.direct(w, k, q, None if src_half is None else src_half.at[pl.ds(q * qr, qr), :]).start()

    def start_pass_on(self, w):
        for q in (0, 1):
            self.direct_landed(w, q, q).wait_recv()
            self.pass_on(w, q).start()

    def start_hand_over(self, w, diagonal):
        if diagonal:
            for q in (0, 1):
                self.passed_landed(w, q).wait_recv()
            self.hand_over(w, 2).start()
        else:
            for k in (0, 1):
                self.direct_landed(w, k, 1 - k).wait_recv()
                self.hand_over(w, k).start()

    def finish(self, w):
        for k in range(3):
            self.handed(w, k).wait_recv()
            self.hand_over(w, k).wait_send()
        for q in (0, 1):
            self.pass_on(w, q).wait_send()
            for k in (0, 1):
                self.direct(w, k, q).wait_send()


def _gather_first(w_in, w_out, w1, w2, small):
    bigs = (w_in, w_out, w1, w2)
    nb = len(bigs)

    def body(win_ref, wout_ref, w1_ref, w2_ref, sm_ref, gin, gout, g1, g2, gsm, st_in, st_out, st_1, st_2,
             send_sems, recv_sems, sm_send, sm_recv, local_sems):
        srcs = (win_ref, wout_ref, w1_ref, w2_ref)
        stages = (st_in, st_out, st_1, st_2)
        outs = (gin, gout, g1, g2)
        plan = _ShardGather(outs[:1], send_sems, recv_sems)
        j, c = plan.j, plan.c
        for src, st in zip(srcs, stages):
            st[...] = src[...].astype(BF16)
        local = [pltpu.make_async_copy(stages[w], outs[w].at[j], local_sems.at[w]) for w in range(nb)]
        local.append(pltpu.make_async_copy(sm_ref, gsm.at[j], local_sems.at[nb]))
        for cp in local:
            cp.start()

        def small_copy(k):
            px, py = plan.chips[k]
            return pltpu.make_async_remote_copy(src_ref=sm_ref, dst_ref=gsm.at[j], send_sem=sm_send.at[k],
                                                recv_sem=sm_recv.at[k], device_id=(px, py, c), device_id_type=MESH)

        def small_landed(k):
            px, py = plan.chips[k]
            return pltpu.make_async_remote_copy(src_ref=sm_ref, dst_ref=gsm.at[2 * px + py], send_sem=sm_send.at[k],
                                                recv_sem=sm_recv.at[k], device_id=(px, py, c), device_id_type=MESH)

        hr = w_in.shape[0] // 2
        plan.start_direct(0, st_in.at[pl.ds(c * hr, hr), :])
        for k in range(3):
            small_copy(k).start()
        plan.start_pass_on(0)
        plan.start_hand_over(0, diagonal=False)
        plan.start_hand_over(0, diagonal=True)
        for k in range(3):
            small_landed(k).wait_recv()
            small_copy(k).wait_send()
        plan.finish(0)
        for cp in local:
            cp.wait()

    def gathered(a, dtype):
        return jax.ShapeDtypeStruct((N_CHIPS,) + a.shape, dtype)

    return pl.pallas_call(
        body, name="gather_first",
        in_specs=[VMEM] * 5, out_specs=[ANY] * 5,
        out_shape=[gathered(a, BF16) for a in bigs] + [gathered(small, F32)],
        scratch_shapes=[pltpu.VMEM(a.shape, BF16) for a in bigs]
        + [pltpu.SemaphoreType.DMA((1, _ShardGather.PAIRS)), pltpu.SemaphoreType.DMA((1, _ShardGather.PAIRS)), pltpu.SemaphoreType.DMA((3,)),
           pltpu.SemaphoreType.DMA((3,)), pltpu.SemaphoreType.DMA((nb + 1,))],
        compiler_params=_params(),
    )(*bigs, small)


class _PartialExchange:
    def __init__(self, parts, arrived, send_sems, recv_sems):
        self.parts, self.arrived, self.send_sems, self.recv_sems = parts, arrived, send_sems, recv_sems
        x, y, c = _position()
        self.c, self.j = c, 2 * x + y
        self.chips = _other_chips(x, y)

    def _copy(self, w, k, slot):
        px, py = self.chips[k]
        return pltpu.make_async_remote_copy(
            src_ref=self.parts[w].at[2 * px + py], dst_ref=self.arrived[w].at[slot], send_sem=self.send_sems.at[w, k],
            recv_sem=self.recv_sems.at[w, k], device_id=(px, py, self.c), device_id_type=MESH)

    def start(self):
        for w in range(len(self.parts)):
            for k in range(3):
                self._copy(w, k, self.j).start()

    def wait(self):
        for w in range(len(self.parts)):
            for k in range(3):
                px, py = self.chips[k]
                self._copy(w, k, 2 * px + py).wait()


class _PackGather:
    def __init__(self, p_ref, all_ref, send_sems, recv_sems, local_sem):
        self.p_ref, self.all_ref, self.send_sems, self.recv_sems, self.local_sem = p_ref, all_ref, send_sems, recv_sems, local_sem
        x, y, c = _position()
        self.me, self.sibling, self.c = (x, y, c), (x, y, 1 - c), c
        self.chips = _other_chips(x, y)

    @staticmethod
    def semaphores():
        return [pltpu.SemaphoreType.DMA((7,)), pltpu.SemaphoreType.DMA((7,)), pltpu.SemaphoreType.DMA]

    def _copy(self, k, block, to, from_pack=False):
        px, py, pc = block
        slot = self.all_ref.at[4 * px + 2 * py + pc]
        return pltpu.make_async_remote_copy(src_ref=self.p_ref if from_pack else slot, dst_ref=slot, send_sem=self.send_sems.at[k],
                                            recv_sem=self.recv_sems.at[k], device_id=to, device_id_type=MESH)

    def _mine(self):
        x, y, c = self.me
        return pltpu.make_async_copy(self.p_ref, self.all_ref.at[4 * x + 2 * y + c], self.local_sem)

    def _first(self):
        return [self._copy(0, self.me, self.sibling, True)] + [
            self._copy(1 + k, self.me, (*chip, self.c), True) for k, chip in enumerate(self.chips)]

    def _passed(self):
        return [self._copy(4 + k, (*chip, self.c), self.sibling) for k, chip in enumerate(self.chips)]

    def start(self):
        self._mine().start()
        for cp in self._first():
            cp.start()

    def hand_over(self):
        for k, chip in enumerate(self.chips):
            self._copy(1 + k, (*chip, self.c), self.me).wait_recv()
            self._passed()[k].start()

    def finish(self):
        self._copy(0, self.sibling, self.me).wait_recv()
        for k, chip in enumerate(self.chips):
            self._copy(4 + k, (*chip, 1 - self.c), self.me).wait_recv()
        for cp in self._first() + self._passed():
            cp.wait_send()
        self._mine().wait()


class _DirectGather:
    def __init__(self, p_ref, all_ref, send_sems, recv_sems, local_sem):
        self.p_ref, self.all_ref, self.send_sems, self.recv_sems, self.local_sem = p_ref, all_ref, send_sems, recv_sems, local_sem
        self.me = _position()

    semaphores = _PackGather.semaphores

    def _peer(self, r):
        x, y, c = self.me
        return ((1 - x) if r & 4 else x, (1 - y) if r & 2 else y, (1 - c) if r & 1 else c)

    def _copy(self, r, slot_of):
        px, py, pc = slot_of
        return pltpu.make_async_remote_copy(src_ref=self.p_ref, dst_ref=self.all_ref.at[4 * px + 2 * py + pc],
                                            send_sem=self.send_sems.at[r - 1], recv_sem=self.recv_sems.at[r - 1],
                                            device_id=self._peer(r), device_id_type=MESH)

    def _mine(self):
        x, y, c = self.me
        return pltpu.make_async_copy(self.p_ref, self.all_ref.at[4 * x + 2 * y + c], self.local_sem)

    def start(self):
        self._mine().start()
        for r in range(1, N_DEVICES):
            self._copy(r, self.me).start()

    def finish(self):
        for r in range(1, N_DEVICES):
            self._copy(r, self._peer(r)).wait()
        self._mine().wait()


def _adamw(w, g, m, v):
    m = ADAM_B1 * m + (1.0 - ADAM_B1) * g
    v = ADAM_B2 * v + (1.0 - ADAM_B2) * (g * g)
    m_hat = m / ADAM_BC1
    v_hat = v / ADAM_BC2
    delta = -ADAM_LR * (m_hat / (jnp.sqrt(v_hat) + ADAM_EPS) + ADAM_WD * w)
    return delta, m, v


JOIN_SUB = 4


def _join(tag, shard_shape, part, arrived, core_chip, block=None):
    pr, pc = WGRAD_GEOMETRY[tag][:2]
    rb = pr // JOIN_SUB
    by_rows = shard_shape[1] == pc
    riding = block is not None

    def body(cc_ref, p_ref, r1_ref, r2_ref, r3_ref, *rest):
        if riding:
            blk_ref, g_ref, all_ref, stage, send_sems, recv_sems, local_sems, b_send, b_recv, b_local = rest
            gather = _DirectGather(blk_ref, all_ref, b_send, b_recv, b_local)
        else:
            g_ref, stage, send_sems, recv_sems, local_sems = rest
        i = pl.program_id(0)
        c = cc_ref[0]
        if riding:
            @pl.when(i == 0)
            def _():
                gather.start()

        def window(core, k):
            if by_rows:
                return g_ref.at[pl.ds((core * JOIN_SUB + k) * rb, rb), :]
            return g_ref.at[pl.ds(k * rb, rb), pl.ds(core * pc, pc)]

        def keep(k):
            return pltpu.make_async_copy(stage.at[k], window(c, k), local_sems.at[k])

        def push(k):
            return pltpu.make_async_remote_copy(src_ref=stage.at[k], dst_ref=window(c, k), send_sem=send_sems.at[k],
                                                recv_sem=recv_sems.at[k], device_id=_sibling(), device_id_type=MESH)

        def pushed(k):
            return pltpu.make_async_remote_copy(src_ref=stage.at[k], dst_ref=window(1 - c, k), send_sem=send_sems.at[k],
                                                recv_sem=recv_sems.at[k], device_id=_sibling(), device_id_type=MESH)

        stage[i] = ((p_ref[0] + r1_ref[0].astype(F32)) + r2_ref[0].astype(F32)) + r3_ref[0].astype(F32)
        keep(i).start()
        push(i).start()

        @pl.when(i == JOIN_SUB - 1)
        def _():
            for k in range(JOIN_SUB):
                keep(k).wait()
                push(k).wait_send()
                pushed(k).wait_recv()
            if riding:
                gather.finish()

    def partial(off):
        return pl.BlockSpec((1, rb, pc), lambda i, cc: ((cc[1] + off) % N_CHIPS, i, 0))

    in_specs = [partial(0), partial(1), partial(2), partial(3)]
    out_specs = [ANY]
    out_shape = [jax.ShapeDtypeStruct(shard_shape, F32)]
    scratch = [pltpu.VMEM((JOIN_SUB, rb, pc), F32), pltpu.SemaphoreType.DMA((JOIN_SUB,)),
               pltpu.SemaphoreType.DMA((JOIN_SUB,)), pltpu.SemaphoreType.DMA((JOIN_SUB,))]
    operands = [part, arrived, arrived, arrived]
    if riding:
        in_specs.append(pl.BlockSpec(block.shape, lambda i, cc: (0, 0)))
        out_specs.append(ANY)
        out_shape.append(jax.ShapeDtypeStruct((N_DEVICES,) + block.shape, block.dtype))
        scratch += _DirectGather.semaphores()
        operands.append(block)
    outs = pl.pallas_call(
        body, name="join_" + tag,
        grid_spec=pltpu.PrefetchScalarGridSpec(
            num_scalar_prefetch=1, grid=(JOIN_SUB,), in_specs=in_specs, out_specs=out_specs, scratch_shapes=scratch),
        out_shape=out_shape,
        compiler_params=_params(dimension_semantics=("arbitrary",)),
    )(core_chip, *operands)
    return outs if riding else outs[0]


def _adamw_big(w, g, m, v, name):
    rows, cols = w.shape
    rb = 256 if rows % 256 == 0 else rows

    def body(w_ref, g_ref, m_ref, v_ref, go_ref, d_ref, nm_ref, nv_ref):
        g = g_ref[...]
        go_ref[...] = g
        d_ref[...], nm_ref[...], nv_ref[...] = _adamw(w_ref[...], g, m_ref[...], v_ref[...])

    spec = pl.BlockSpec((rb, cols), lambda i: (i, 0))
    return pl.pallas_call(
        body, name=name, grid=(rows // rb,), in_specs=[spec] * 4, out_specs=[spec] * 4,
        out_shape=[jax.ShapeDtypeStruct(w.shape, F32)] * 4,
        compiler_params=_params(dimension_semantics=("arbitrary",)),
    )(w, g, m, v)


def _small_step(vec_packs, mat_packs, mix_g_blocks, w_pack, m_pack, v_pack, conv_wmv, rconv_wmv):
    vec_rows = vec_packs.shape[1]
    rows, cols = vec_rows + mat_packs.shape[1], vec_packs.shape[2]
    cshard = conv_wmv.shape[2]
    rshard = rconv_wmv.shape[2]
    mix_row = PK_MIX_G * TILE_ROWS

    def body(vec_ref, mat_ref, blk_ref, w_ref, m_ref, v_ref, cw_ref, rw_ref, g_ref, d_ref, nm_ref, nv_ref, co_ref, ro_ref):
        total = vec_ref[0]
        mats = mat_ref[0].astype(F32)
        late = blk_ref[0]
        for k in range(1, N_DEVICES):
            total = total + vec_ref[k]
            mats = mats + mat_ref[k].astype(F32)
            late = late + blk_ref[k]
        g_ref[0:vec_rows, :] = total
        g_ref[vec_rows:, :] = mats
        g_ref[mix_row:mix_row + TILE_ROWS, :] = late
        g = g_ref[...]
        d_ref[...], nm_ref[...], nv_ref[...] = _adamw(w_ref[...], g, m_ref[...], v_ref[...])

        x, y, _ = _position()
        j = 2 * x + y
        cblk = total[PK_CONV_W * 8:PK_CONV_W * 8 + 8, :]
        rblk = total[PK_RCONV_W * 8:PK_RCONV_W * 8 + 8, :]
        cg = cblk[:, 0:cshard]
        rg = rblk[:, 0:rshard]
        for k in range(1, N_CHIPS):
            cg = jnp.where(j == k, cblk[:, k * cshard:(k + 1) * cshard], cg)
            rg = jnp.where(j == k, rblk[:, k * rshard:(k + 1) * rshard], rg)
        co_ref[0] = cg
        co_ref[1], co_ref[2], co_ref[3] = _adamw(cw_ref[0], cg, cw_ref[1], cw_ref[2])
        ro_ref[0] = rg
        ro_ref[1], ro_ref[2], ro_ref[3] = _adamw(rw_ref[0], rg, rw_ref[1], rw_ref[2])

    pack = [jax.ShapeDtypeStruct((rows, cols), F32)] * 4
    return pl.pallas_call(
        body, name="small_grads_step", in_specs=[VMEM] * 8, out_specs=[VMEM] * 6,
        out_shape=pack + [jax.ShapeDtypeStruct((4, TILE_ROWS, cshard), F32), jax.ShapeDtypeStruct((4, TILE_ROWS, rshard), F32)],
        compiler_params=_params(),
    )(vec_packs, mat_packs, mix_g_blocks, w_pack, m_pack, v_pack, conv_wmv, rconv_wmv)


def _blk(a):
    a = a.reshape(-1, a.shape[-1])
    return jnp.pad(a, ((0, TILE_ROWS - a.shape[0]), (0, D_MODEL - a.shape[1])))


def _zero_blk():
    return jnp.zeros((TILE_ROWS, D_MODEL), F32)


def _pack_params(p, pre):
    get = lambda n: p[pre + n]
    return jnp.concatenate([
        _blk(get("g_norm_rnn")), _blk(get("rnn_conv_b")), _blk(get("b_a")), _blk(get("b_x")), _blk(get("lru_lambda")),
        _zero_blk(), _zero_blk(), _blk(get("g_norm_conv")), _blk(get("final_norm_g").reshape(1, -1)), _blk(get("norm_mlp_g")),
        _zero_blk(), _blk(get("norm_mix_g")), get("w_a").reshape(64, D_MODEL), get("w_x").reshape(64, D_MODEL)], axis=0)


def _to_block_diag(w):
    w4 = w.reshape(N_BD, 4, 64, 64)
    eye = jnp.eye(4, dtype=w.dtype)
    return (w4[:, :, :, None, :] * eye[None, :, None, :, None]).reshape(N_BD, BD, BD)


def _from_block_diag(d):
    d5 = d.reshape(N_BD, 4, 64, 4, 64)
    return jnp.stack([d5[:, q, :, q, :] for q in range(4)], axis=1).reshape(64, D_MODEL)


def _pad_rows(a):
    return jnp.pad(a, ((0, TILE_ROWS - a.shape[0]), (0, 0)))


_NAMES = ['norm_mix_g', 'w_in', 'conv_w', 'rnn_conv_w', 'rnn_conv_b', 'w_a', 'b_a', 'w_x', 'b_x', 'lru_lambda',
          'g_norm_conv', 'g_norm_rnn', 'w_out', 'norm_mlp_g', 'w_mlp_in', 'w_mlp_out', 'final_norm_g']


def kernel(x, norm_mix_g, w_in, conv_w, rnn_conv_w, rnn_conv_b, w_a, b_a, w_x, b_x, lru_lambda, g_norm_conv, g_norm_rnn, w_out, norm_mlp_g, w_mlp_in, w_mlp_out, final_norm_g, loss_target, m_norm_mix_g, m_w_in, m_conv_w, m_rnn_conv_w, m_rnn_conv_b, m_w_a, m_b_a, m_w_x, m_b_x, m_lru_lambda, m_g_norm_conv, m_g_norm_rnn, m_w_out, m_norm_mlp_g, m_w_mlp_in, m_w_mlp_out, m_final_norm_g, v_norm_mix_g, v_w_in, v_conv_w, v_rnn_conv_w, v_rnn_conv_b, v_w_a, v_b_a, v_w_x, v_b_x, v_lru_lambda, v_g_norm_conv, v_g_norm_rnn, v_w_out, v_norm_mlp_g, v_w_mlp_in, v_w_mlp_out, v_final_norm_g):
    args = dict(locals())
    p = {}
    for n in _NAMES:
        for pre in ("", "m_", "v_"):
            a = args[pre + n]
            p[pre + n] = a[0] if a.ndim >= 3 else a
    xs = x[0]
    target = loss_target[0]
    core_chip = jnp.stack([lax.axis_index("c"), 2 * lax.axis_index("x") + lax.axis_index("y")]).astype(jnp.int32)
    cshard = p["conv_w"].shape[1]
    rshard = p["rnn_conv_w"].shape[1]

    small = jnp.concatenate([_pad_rows(p["conv_w"]), _pad_rows(p["rnn_conv_w"])], axis=1)
    w_in_g, w_out_g, w1_g, w2_g, small_g = _gather_first(p["w_in"], p["w_out"], p["w_mlp_in"], p["w_mlp_out"], small)
    conv_full = small_g[:, :3, :cshard].transpose(1, 0, 2).reshape(3, CONV_W)
    rconv_full = small_g[:, :4, cshard:].transpose(1, 0, 2).reshape(4, LRU_W)
    wa_bd = _to_block_diag(p["w_a"]).astype(BF16)
    wx_bd = _to_block_diag(p["w_x"]).astype(BF16)
    gf = p["final_norm_g"].reshape(1, -1)
    lru = (wa_bd, p["b_a"], wx_bd, p["b_x"], p["lru_lambda"], p["g_norm_conv"], p["g_norm_rnn"])

    (u, h1b, xr, hs, c3, yb, *gates), (w_out_g, w1_g, w2_g) = _fwd_mix(
        xs, p["norm_mix_g"], w_in_g, conv_full, rconv_full, p["rnn_conv_b"], *lru, (w_out_g, w1_g, w2_g))
    zb, dpb, h2b, dx3b, dx2, dx2b, dy, st_mlp = _mlp_fwd_bwd(
        xs, yb, w_out_g.reshape(-1, D_MODEL), w1_g, w2_g.reshape(-1, D_MODEL), p["norm_mlp_g"], gf, target)

    part_out = _wgrad(yb, dx2b, "out", core_chip)
    *part_1, arrived_out = _wgrad(h2b, dpb, "mlp_in", core_chip, parts=(part_out[1],))
    part_2 = _wgrad(zb, dx3b, "mlp_out", core_chip)
    (dub, st_mix, dwa_bd, dwx_bd), (arrived_1, arrived_2) = _mix_bwd(
        dy, u, xr, hs, c3, gates, conv_full, rconv_full, wa_bd, wx_bd, p["lru_lambda"], p["g_norm_conv"], p["g_norm_rnn"],
        (part_1[1], part_2[1]))
    arrived_mlp = (arrived_out, arrived_1, arrived_2)
    vec_pack = jnp.concatenate([st_mix, st_mlp, _zero_blk()], axis=0)
    mat_pack = jnp.concatenate([_from_block_diag(dwa_bd), _from_block_diag(dwx_bd)], axis=0).astype(BF16)
    *part_in, vec_packs, mat_packs = _wgrad(h1b, dub, "in", core_chip, packs=(vec_pack, mat_pack))
    early = (("w_out", "out", part_out, arrived_mlp[0]), ("w_mlp_in", "mlp_in", part_1, arrived_mlp[1]),
             ("w_mlp_out", "mlp_out", part_2, arrived_mlp[2]))
    (grad_x, st_in), arrived_in, joined = _in_bwd(
        dub, w_in_g, xs, dx2, p["norm_mix_g"], (part_in[1],),
        [(tag, p[n].shape, part[0], arrived) for n, tag, part, arrived in early], core_chip)
    g_in, mix_g_blocks = _join("in", p["w_in"].shape, part_in[0], arrived_in[0], core_chip, st_in)
    big = {}
    for n, tag, g in [(n, tag, g) for (n, tag, _, _), g in zip(early, joined)] + [("w_in", "in", g_in)]:
        big[n] = _adamw_big(p[n], g, p["m_" + n], p["v_" + n], "adamw_" + tag)

    conv_wmv = jnp.stack([_pad_rows(p[pre + "conv_w"]) for pre in ("", "m_", "v_")])
    rconv_wmv = jnp.stack([_pad_rows(p[pre + "rnn_conv_w"]) for pre in ("", "m_", "v_")])
    g_pack, d_pack, m_pack, v_pack, conv_out, rconv_out = _small_step(
        vec_packs, mat_packs, mix_g_blocks, _pack_params(p, ""), _pack_params(p, "m_"), _pack_params(p, "v_"), conv_wmv, rconv_wmv)

    def unpack(pk, kind):
        def vec(b, width=D_MODEL):
            return pk[b * 8:b * 8 + 1, :width]
        return {
            "norm_mix_g": vec(PK_MIX_G), "rnn_conv_b": vec(PK_RCONV_B), "b_a": vec(PK_B_A), "b_x": vec(PK_B_X),
            "lru_lambda": vec(PK_LAMBDA), "g_norm_conv": vec(PK_G_NORM_CONV, CONV_W), "g_norm_rnn": vec(PK_G_NORM_RNN),
            "norm_mlp_g": vec(PK_MLP_G), "final_norm_g": vec(PK_FINAL_G).reshape(-1),
            "w_a": pk[PK_W_A * 8:PK_W_A * 8 + 64].reshape(1, 16, 64, 64), "w_x": pk[PK_W_X * 8:PK_W_X * 8 + 64].reshape(1, 16, 64, 64),
            "conv_w": conv_out[kind, :3][None], "rnn_conv_w": rconv_out[kind, :4][None],
            "w_in": big["w_in"][kind][None], "w_out": big["w_out"][kind][None],
            "w_mlp_in": big["w_mlp_in"][kind][None], "w_mlp_out": big["w_mlp_out"][kind][None],
        }

    outs = [unpack(pk, kind) for kind, pk in enumerate((g_pack, d_pack, m_pack, v_pack))]
    for o in outs:
        for n in ("norm_mix_g", "rnn_conv_b", "b_a", "b_x", "lru_lambda", "g_norm_conv", "g_norm_rnn", "norm_mlp_g"):
            o[n] = o[n].reshape(1, -1)
    loss = g_pack[PK_LOSS * 8, 0]
    return (loss, grad_x[None], *[o[n] for o in outs for n in _NAMES])
```

```python
import functools
import math

import jax
import jax.numpy as jnp
from jax import lax
from jax.experimental import pallas as pl
from jax.experimental.pallas import tpu as pltpu

F32 = jnp.float32
BF16 = jnp.bfloat16
MESH = pl.DeviceIdType.MESH
ANY = pl.BlockSpec(memory_space=pl.ANY)
VMEM = pl.BlockSpec(memory_space=pltpu.VMEM)

EPS = 1e-6
LRU_C = 8.0
D_MODEL = 1024
CONV_W = 512
LRU_W = 1024
IN_COLS = 3 * CONV_W + 2 * LRU_W
IN_SHARD = IN_COLS // 4
N_CHIPS = 4
N_DEVICES = 8
BD = 256
N_BD = LRU_W // BD

ADAM_LR = 0.001
ADAM_B1 = 0.9
ADAM_B2 = 0.999
ADAM_EPS = 1e-08
ADAM_WD = 0.01
ADAM_STEP = 10
ADAM_BC1 = 1.0 - ADAM_B1 ** ADAM_STEP
ADAM_BC2 = 1.0 - ADAM_B2 ** ADAM_STEP

TILE_ROWS = 8
TOKEN_TILE = 256
MATMUL_TOKEN_TILE = 512
VMEM_LIMIT = 56 * 1024 * 1024

PK_G_NORM_RNN, PK_RCONV_B, PK_B_A, PK_B_X, PK_LAMBDA, PK_RCONV_W, PK_CONV_W, PK_G_NORM_CONV = range(8)
PK_FINAL_G, PK_MLP_G, PK_LOSS, PK_MIX_G = 8, 9, 10, 11
PK_W_A = 12
PK_W_X = 20
PK_BLOCKS = 28
PK_ROWS = PK_BLOCKS * TILE_ROWS


def _params(**kw):
    return pltpu.CompilerParams(vmem_limit_bytes=VMEM_LIMIT, **kw)


def _position():
    x, y, c = lax.axis_index("x"), lax.axis_index("y"), lax.axis_index("c")
    return x, y, c


def _sigmoid(v):
    return 1.0 / (1.0 + jnp.exp(-v))


def _one_minus_square(log_a, a):
    v = 2.0 * log_a
    series = -v * (1.0 + v * (0.5 + v * (1.0 / 6.0)))
    return jnp.where(v > -0.01, series, 1.0 - a * a)


_GELU_C = math.sqrt(2.0 / math.pi)
_GELU_K = 0.044715


def _gelu_and_grad(g):
    th = jnp.tanh(_GELU_C * (g + _GELU_K * g * g * g))
    gelu = 0.5 * g * (1.0 + th)
    dgelu = 0.5 * (1.0 + th) + 0.5 * g * (1.0 - th * th) * (_GELU_C * (1.0 + 3.0 * _GELU_K * g * g))
    return gelu, dgelu


def _rows(shape):
    return lax.broadcasted_iota(jnp.int32, shape, 0)


def _shift_down(v, k, prev8):
    rolled = pltpu.roll(v, k, 0)
    halo = pltpu.roll(prev8, k, 0)
    head = jnp.where(_rows(halo.shape) < k, halo, rolled[:TILE_ROWS])
    return jnp.concatenate([head, rolled[TILE_ROWS:]], axis=0)


def _shift_up(v, k, next8):
    n = v.shape[0]
    rolled = pltpu.roll(v, n - k, 0)
    halo = pltpu.roll(next8, TILE_ROWS - k, 0)
    tail = jnp.where(_rows(halo.shape) >= TILE_ROWS - k, halo, rolled[n - TILE_ROWS:])
    return jnp.concatenate([rolled[: n - TILE_ROWS], tail], axis=0)


def _scan_rows(a, b, carry, reverse=False):
    n, w = a.shape
    groups = n // TILE_ROWS
    a3 = a.reshape(groups, TILE_ROWS, w)
    b3 = b.reshape(groups, TILE_ROWS, w)
    sub = lax.broadcasted_iota(jnp.int32, a3.shape, 1)
    s = 1
    while s < TILE_ROWS:
        shift = TILE_ROWS - s if reverse else s
        keep = (sub < TILE_ROWS - s) if reverse else (sub >= s)
        b3 = b3 + jnp.where(keep, a3 * pltpu.roll(b3, shift, 1), 0.0)
        a3 = a3 * jnp.where(keep, pltpu.roll(a3, shift, 1), 1.0)
        s *= 2
    out = [None] * groups
    edge = 0 if reverse else TILE_ROWS - 1
    for g in (range(groups - 1, -1, -1) if reverse else range(groups)):
        out[g] = b3[g] + a3[g] * carry
        carry = out[g][edge:edge + 1]
    return jnp.concatenate(out, axis=0)


def _softplus_neg(lam):
    e = jnp.exp(-jnp.abs(lam))
    log1p_e = jnp.where(e < 1e-2, e * (1.0 - e * (0.5 - e * (1.0 / 3.0 - e * 0.25))), jnp.log(1.0 + e))
    sp = jnp.maximum(-lam, 0.0) + log1p_e
    dsp = -_sigmoid(-lam)
    return sp, dsp


def _block_diag_dot(vb, w_ref):
    return jnp.concatenate(
        [jnp.dot(vb[:, j * BD:(j + 1) * BD], w_ref[j], preferred_element_type=F32) for j in range(N_BD)], axis=1)


def _block_diag_dot_t(vb, w_ref):
    return jnp.concatenate(
        [lax.dot_general(vb[:, j * BD:(j + 1) * BD], w_ref[j], (((1,), (1,)), ((), ())), preferred_element_type=F32)
         for j in range(N_BD)], axis=1)


def _dot_nt(a, b):
    return lax.dot_general(a, b, (((1,), (1,)), ((), ())), preferred_element_type=F32)


def _dot_tn(a, b):
    return lax.dot_general(a, b, (((0,), (0,)), ((), ())), preferred_element_type=F32)


def _lru_gates(xr, wa_ref, ba, wx_ref, bx, sp):
    xrb = xr.astype(BF16)
    r = _sigmoid(_block_diag_dot(xrb, wa_ref) + ba)
    ig = _sigmoid(_block_diag_dot(xrb, wx_ref) + bx)
    log_a = (-LRU_C) * r * sp
    a = jnp.exp(log_a)
    mult = jnp.sqrt(_one_minus_square(log_a, a))
    return r, ig, a, mult


def _colsum(v):
    return jnp.sum(v, axis=0, keepdims=True)


N_FWD_OUT = 10


def _fwd_mix(x, g1, w_in_g, conv_w, rconv_w, rconv_b, wa_bd, b_a, wx_bd, b_x, lam, g_nc, g_nr, later):
    t, d = x.shape
    tm = TOKEN_TILE
    nt = t // tm
    nl = len(later)
    assert nl == 3
    pass_on_at = [nt * f // 16 for f in (2, 4, 8)]
    neighbours_at = [nt * f // 16 for f in (4, 7, 11)]
    diagonal_at = [nt * f // 16 for f in (10, 12, 14)]

    def body(x_ref, g1_ref, win_ref, cw_ref, rw_ref, rb_ref, wa_ref, ba_ref, wx_ref, bx_ref, lam_ref, gnc_ref, gnr_ref,
             *rest):
        later_in, outs, rest = rest[:nl], rest[nl:nl + N_FWD_OUT], rest[nl + N_FWD_OUT:]
        u_ref, h1_ref, xr_ref, hs_ref, c3_ref, y_ref, r_ref, ig_ref, a_ref, mult_ref = outs
        later_out, (cv_prev, xin_prev, h_prev, send_sems, recv_sems) = rest[:nl], rest[nl:]
        del later_in
        step = pl.program_id(0)
        plan = _ShardGather(later_out, send_sems, recv_sems)

        @pl.when(step == 0)
        def _():
            cv_prev[...] = jnp.zeros_like(cv_prev)
            xin_prev[...] = jnp.zeros_like(xin_prev)
            h_prev[...] = jnp.zeros_like(h_prev)
            for w in range(nl):
                plan.start_direct(w)

        for w in range(nl):
            @pl.when(step == pass_on_at[w])
            def _(w=w):
                plan.start_pass_on(w)

            @pl.when(step == neighbours_at[w])
            def _(w=w):
                plan.start_hand_over(w, diagonal=False)

            @pl.when(step == diagonal_at[w])
            def _(w=w):
                plan.start_hand_over(w, diagonal=True)

        xv = x_ref[...]
        rstd = lax.rsqrt(jnp.mean(xv * xv, axis=-1, keepdims=True) + EPS)
        h1b = ((xv * rstd) * g1_ref[...]).astype(BF16)
        h1_ref[...] = h1b
        for j in range(N_CHIPS):
            u_ref[:, j * IN_SHARD:(j + 1) * IN_SHARD] = jnp.dot(h1b, win_ref[j], preferred_element_type=F32)
        gate_b = u_ref[:, 0:CONV_W]
        cv = u_ref[:, CONV_W:2 * CONV_W] * u_ref[:, 2 * CONV_W:3 * CONV_W]
        x_r = u_ref[:, 3 * CONV_W:3 * CONV_W + LRU_W]
        g = u_ref[:, 3 * CONV_W + LRU_W:]

        cw = cw_ref[...]
        cvp = cv_prev[...]
        conv3 = cw[0:1] * _shift_down(cv, 2, cvp) + cw[1:2] * _shift_down(cv, 1, cvp) + cw[2:3] * cv
        cv_prev[...] = cv[tm - TILE_ROWS:]
        c3_ref[...] = conv3
        y_conv = gate_b * conv3

        rw = rw_ref[...]
        xp = xin_prev[...]
        xr = (rw[0:1] * _shift_down(x_r, 3, xp) + rw[1:2] * _shift_down(x_r, 2, xp)
              + rw[2:3] * _shift_down(x_r, 1, xp) + rw[3:4] * x_r) + rb_ref[...]
        xin_prev[...] = x_r[tm - TILE_ROWS:]
        xr_ref[...] = xr
        sp, _ = _softplus_neg(lam_ref[...])
        r, ig, a, mult = _lru_gates(xr, wa_ref, ba_ref[...], wx_ref, bx_ref[...], sp)
        r_ref[...] = r
        ig_ref[...] = ig
        a_ref[...] = a
        mult_ref[...] = mult
        h = _scan_rows(a, mult * (ig * xr), h_prev[...])
        h_prev[...] = h[tm - 1:tm]
        hs_ref[...] = h
        gelu, _ = _gelu_and_grad(g)
        y_rnn = h * gelu

        na = y_conv * lax.rsqrt(jnp.mean(y_conv * y_conv, axis=-1, keepdims=True) + EPS) * gnc_ref[...]
        nb = y_rnn * lax.rsqrt(jnp.mean(y_rnn * y_rnn, axis=-1, keepdims=True) + EPS) * gnr_ref[...]
        y_ref[:, :CONV_W] = na.astype(BF16)
        y_ref[:, CONV_W:] = nb.astype(BF16)

        @pl.when(step == nt - 1)
        def _():
            for w in range(nl):
                plan.finish(w)

    def full(a):
        nd = a.ndim
        return pl.BlockSpec(a.shape, lambda i: (0,) * nd)

    def tok(cols):
        return pl.BlockSpec((tm, cols), lambda i: (i, 0))

    def act(cols, dtype=F32):
        return jax.ShapeDtypeStruct((t, cols), dtype)

    smalls = (g1, w_in_g, conv_w, rconv_w, rconv_b, wa_bd, b_a, wx_bd, b_x, lam, g_nc, g_nr)
    n_in = 1 + len(smalls)
    outs = pl.pallas_call(
        body, name="fwd_mix", grid=(nt,),
        in_specs=[tok(d)] + [full(a) for a in smalls] + [ANY] * nl,
        out_specs=[tok(IN_COLS), tok(d), tok(LRU_W), tok(LRU_W), tok(CONV_W), tok(CONV_W + LRU_W)]
        + [tok(LRU_W)] * 4 + [ANY] * nl,
        out_shape=[act(IN_COLS), act(d, BF16), act(LRU_W), act(LRU_W), act(CONV_W), act(CONV_W + LRU_W, BF16)]
        + [act(LRU_W)] * 4 + [jax.ShapeDtypeStruct(a.shape, a.dtype) for a in later],
        input_output_aliases={n_in + w: N_FWD_OUT + w for w in range(nl)},
        scratch_shapes=[pltpu.VMEM((TILE_ROWS, CONV_W), F32), pltpu.VMEM((TILE_ROWS, LRU_W), F32),
                        pltpu.VMEM((1, LRU_W), F32), pltpu.SemaphoreType.DMA((nl, _ShardGather.PAIRS)),
                        pltpu.SemaphoreType.DMA((nl, _ShardGather.PAIRS))],
        compiler_params=_params(dimension_semantics=("arbitrary",)),
    )(x, *smalls, *later)
    return outs[:N_FWD_OUT], outs[N_FWD_OUT:]


def _mlp_fwd_bwd(x, yb, w_out_g, w1_g, w2_g, g2, gf, target):
    t, d = x.shape
    tm = TOKEN_TILE
    ff = w2_g.shape[0]
    mix = w_out_g.shape[0]
    ffs = ff // N_CHIPS

    def body(x_ref, y_ref, g2_ref, gf_ref, tgt_ref, wout_hbm, w1_hbm, w2_hbm,
             z_ref, dp_ref, h2_ref, dx3b_ref, dx2_ref, dx2b_ref, dy_ref, st_ref, wout, w1, w2, p_ref):
        @pl.when(pl.program_id(0) == 0)
        def _():
            pltpu.sync_copy(wout_hbm, wout)
            pltpu.sync_copy(w1_hbm, w1)
            pltpu.sync_copy(w2_hbm, w2)
            st_ref[...] = jnp.zeros_like(st_ref)

        x2 = x_ref[...] + jnp.dot(y_ref[...], wout[...], preferred_element_type=F32)
        r2 = lax.rsqrt(jnp.mean(x2 * x2, axis=-1, keepdims=True) + EPS)
        xh2 = x2 * r2
        g2v = g2_ref[...]
        h2b = (xh2 * g2v).astype(BF16)
        h2_ref[...] = h2b
        for j in range(N_CHIPS):
            p_ref[:, j * ffs:(j + 1) * ffs] = jnp.dot(h2b, w1[j], preferred_element_type=F32)
        rp = jnp.maximum(p_ref[...], 0.0)
        zb = (rp * rp).astype(BF16)
        z_ref[...] = zb
        x3 = x2 + jnp.dot(zb, w2[...], preferred_element_type=F32)
        r3 = lax.rsqrt(jnp.mean(x3 * x3, axis=-1, keepdims=True) + EPS)
        xh3 = x3 * r3
        gfv = gf_ref[...]
        err = xh3 * gfv - tgt_ref[...]
        loss = (0.5 / d) * jnp.sum(err * err)
        dout = err * (1.0 / d)
        st_ref[PK_FINAL_G * 8 - 64:PK_FINAL_G * 8 - 63, :] += _colsum(dout * xh3)
        st_ref[PK_LOSS * 8 - 64:PK_LOSS * 8 - 63, :] += jnp.zeros((1, d), F32) + loss
        dxh3 = dout * gfv
        dx3 = r3 * (dxh3 - xh3 * jnp.mean(dxh3 * xh3, axis=-1, keepdims=True))
        dx3b = dx3.astype(BF16)
        dx3b_ref[...] = dx3b
        dpb = (_dot_nt(dx3b, w2[...]) * (2.0 * rp)).astype(BF16)
        dp_ref[...] = dpb
        dh2 = _dot_nt(dpb[:, 0:ffs], w1[0])
        for j in range(1, N_CHIPS):
            dh2 = dh2 + _dot_nt(dpb[:, j * ffs:(j + 1) * ffs], w1[j])
        st_ref[PK_MLP_G * 8 - 64:PK_MLP_G * 8 - 63, :] += _colsum(dh2 * xh2)
        dxh2 = dh2 * g2v
        dx2 = dx3 + r2 * (dxh2 - xh2 * jnp.mean(dxh2 * xh2, axis=-1, keepdims=True))
        dx2_ref[...] = dx2
        dx2b = dx2.astype(BF16)
        dx2b_ref[...] = dx2b
        dy_ref[...] = _dot_nt(dx2b, wout[...])

    def tok(cols):
        return pl.BlockSpec((tm, cols), lambda i: (i, 0))

    def row(cols):
        return pl.BlockSpec((1, cols), lambda i: (0, 0))

    return pl.pallas_call(
        body, name="mlp_fwd_bwd", grid=(t // tm,),
        in_specs=[tok(d), tok(mix), row(d), row(d), tok(d), ANY, ANY, ANY],
        out_specs=[tok(ff), tok(ff), tok(d), tok(d), tok(d), tok(d), tok(mix),
                   pl.BlockSpec((3 * TILE_ROWS, d), lambda i: (0, 0))],
        out_shape=[jax.ShapeDtypeStruct((t, ff), BF16), jax.ShapeDtypeStruct((t, ff), BF16),
                   jax.ShapeDtypeStruct((t, d), BF16), jax.ShapeDtypeStruct((t, d), BF16),
                   jax.ShapeDtypeStruct((t, d), F32), jax.ShapeDtypeStruct((t, d), BF16),
                   jax.ShapeDtypeStruct((t, mix), F32), jax.ShapeDtypeStruct((3 * TILE_ROWS, d), F32)],
        scratch_shapes=[pltpu.VMEM(w_out_g.shape, BF16), pltpu.VMEM(w1_g.shape, BF16), pltpu.VMEM(w2_g.shape, BF16),
                        pltpu.VMEM((tm, ff), F32)],
        compiler_params=_params(dimension_semantics=("arbitrary",)),
    )(x, yb, g2, gf, target, w_out_g, w1_g, w2_g)


def _mix_bwd(dy, u, xr_all, hs_all, c3_all, gates, conv_w, rconv_w, wa_bd, wx_bd, lam, g_nc, g_nr, parts):
    t = dy.shape[0]
    tm = TOKEN_TILE
    nt = t // tm
    hb = tm // TILE_ROWS
    npart = len(parts)

    def body(dy_ref, u_ref, uh_ref, xr_ref, hs_ref, hh_ref, c3_ref, r_ref, ig_ref, a_ref, mult_ref,
             cw_ref, rw_ref, wa_ref, wx_ref, lam_ref, gnc_ref, gnr_ref, *rest):
        part_refs, (du_ref, st_ref, dwa_ref, dwx_ref), rest = rest[:npart], rest[npart:npart + 4], rest[npart + 4:]
        arrived_refs, (dc_next, a_next, gs_next, dxr_next, send_sems, recv_sems) = rest[:npart], rest[npart:]
        exchange = _PartialExchange(part_refs, arrived_refs, send_sems, recv_sems)
        i = pl.program_id(0)

        @pl.when(i == 0)
        def _():
            exchange.start()
            dc_next[...] = jnp.zeros_like(dc_next)
            a_next[...] = jnp.zeros_like(a_next)
            gs_next[...] = jnp.zeros_like(gs_next)
            dxr_next[...] = jnp.zeros_like(dxr_next)
            st_ref[...] = jnp.zeros_like(st_ref)
            dwa_ref[...] = jnp.zeros_like(dwa_ref)
            dwx_ref[...] = jnp.zeros_like(dwx_ref)

        first_tile = i == nt - 1
        gate_b = u_ref[:, 0:CONV_W]
        gate_c = u_ref[:, CONV_W:2 * CONV_W]
        v = u_ref[:, 2 * CONV_W:3 * CONV_W]
        x_r = u_ref[:, 3 * CONV_W:3 * CONV_W + LRU_W]
        g = u_ref[:, 3 * CONV_W + LRU_W:]
        cv = gate_c * v
        cv_prev = jnp.where(first_tile, 0.0, uh_ref[:, CONV_W:2 * CONV_W] * uh_ref[:, 2 * CONV_W:3 * CONV_W])
        xin_prev = jnp.where(first_tile, 0.0, uh_ref[:, 3 * CONV_W:3 * CONV_W + LRU_W])
        hs_prev = jnp.where(first_tile, 0.0, hh_ref[...])

        def acc(block, val, width=LRU_W, row=0):
            r0 = block * TILE_ROWS + row
            st_ref[r0:r0 + 1, 0:width] += val

        conv3 = c3_ref[...]
        y_conv = gate_b * conv3
        ra = lax.rsqrt(jnp.mean(y_conv * y_conv, axis=-1, keepdims=True) + EPS)
        xha = y_conv * ra
        dna = dy_ref[:, :CONV_W]
        acc(PK_G_NORM_CONV, _colsum(dna * xha), CONV_W)
        dxha = dna * gnc_ref[...]
        dy_conv = ra * (dxha - xha * jnp.mean(dxha * xha, axis=-1, keepdims=True))
        du_ref[:, 0:CONV_W] = (dy_conv * conv3).astype(BF16)
        dc = dy_conv * gate_b
        cw = cw_ref[...]
        dcn = dc_next[...]
        dcv = cw[2:3] * dc + cw[1:2] * _shift_up(dc, 1, dcn) + cw[0:1] * _shift_up(dc, 2, dcn)
        dc_next[...] = dc[:TILE_ROWS]
        acc(PK_CONV_W, _colsum(dc * _shift_down(cv, 2, cv_prev)), CONV_W, 0)
        acc(PK_CONV_W, _colsum(dc * _shift_down(cv, 1, cv_prev)), CONV_W, 1)
        acc(PK_CONV_W, _colsum(dc * cv), CONV_W, 2)
        du_ref[:, CONV_W:2 * CONV_W] = (dcv * v).astype(BF16)
        du_ref[:, 2 * CONV_W:3 * CONV_W] = (dcv * gate_c).astype(BF16)

        hs = hs_ref[...]
        gelu, dgelu = _gelu_and_grad(g)
        y_rnn = hs * gelu
        rb = lax.rsqrt(jnp.mean(y_rnn * y_rnn, axis=-1, keepdims=True) + EPS)
        xhb = y_rnn * rb
        dnb = dy_ref[:, CONV_W:]
        acc(PK_G_NORM_RNN, _colsum(dnb * xhb))
        dxhb = dnb * gnr_ref[...]
        dy_rnn = rb * (dxhb - xhb * jnp.mean(dxhb * xhb, axis=-1, keepdims=True))
        du_ref[:, 3 * CONV_W + LRU_W:] = (dy_rnn * hs * dgelu).astype(BF16)
        dh = dy_rnn * gelu

        xr = xr_ref[...]
        xrb = xr.astype(BF16)
        sp, dsp = _softplus_neg(lam_ref[...])
        r, ig, a, mult = r_ref[...], ig_ref[...], a_ref[...], mult_ref[...]
        a_up = _shift_up(a, 1, a_next[...])
        a_next[...] = a[:TILE_ROWS]
        gs = _scan_rows(a_up, dh, gs_next[0:1, :], reverse=True)
        gs_next[...] = gs[:TILE_ROWS]
        da = gs * _shift_down(hs, 1, hs_prev)
        gx = gs * xr
        di = gx * mult
        dmult = gx * ig
        dxr = gs * (mult * ig)
        dlog_a = da * a - dmult * ((a * a) / mult)
        acc(PK_LAMBDA, _colsum(dlog_a * r) * ((-LRU_C) * dsp))
        dpa = (dlog_a * ((-LRU_C) * sp)) * (r * (1.0 - r))
        dpx = di * (ig * (1.0 - ig))
        acc(PK_B_A, _colsum(dpa))
        acc(PK_B_X, _colsum(dpx))
        dpab = dpa.astype(BF16)
        dpxb = dpx.astype(BF16)
        dxr = dxr + _block_diag_dot_t(dpab, wa_ref) + _block_diag_dot_t(dpxb, wx_ref)
        for j in range(N_BD):
            cols = slice(j * BD, (j + 1) * BD)
            dwa_ref[j] += _dot_tn(xrb[:, cols], dpab[:, cols])
            dwx_ref[j] += _dot_tn(xrb[:, cols], dpxb[:, cols])

        acc(PK_RCONV_B, _colsum(dxr))
        rw = rw_ref[...]
        dxn = dxr_next[...]
        dx_r = (rw[3:4] * dxr + rw[2:3] * _shift_up(dxr, 1, dxn) + rw[1:2] * _shift_up(dxr, 2, dxn)
                + rw[0:1] * _shift_up(dxr, 3, dxn))
        dxr_next[...] = dxr[:TILE_ROWS]
        for k in range(3):
            acc(PK_RCONV_W, _colsum(dxr * _shift_down(x_r, 3 - k, xin_prev)), LRU_W, k)
        acc(PK_RCONV_W, _colsum(dxr * x_r), LRU_W, 3)
        du_ref[:, 3 * CONV_W:3 * CONV_W + LRU_W] = dx_r.astype(BF16)

        @pl.when(i == nt - 1)
        def _():
            exchange.wait()

    def full(a):
        nd = a.ndim
        return pl.BlockSpec(a.shape, lambda i: (0,) * nd)

    def tok(cols):
        return pl.BlockSpec((tm, cols), lambda i: (nt - 1 - i, 0))

    def halo(cols):
        return pl.BlockSpec((TILE_ROWS, cols), lambda i: (jnp.maximum((nt - 1 - i) * hb - 1, 0), 0))

    smalls = (conv_w, rconv_w, wa_bd, wx_bd, lam, g_nc, g_nr)
    outs = pl.pallas_call(
        body, name="mix_bwd", grid=(nt,),
        in_specs=[tok(CONV_W + LRU_W), tok(IN_COLS), halo(IN_COLS), tok(LRU_W), tok(LRU_W), halo(LRU_W), tok(CONV_W)]
        + [tok(LRU_W)] * 4 + [full(a) for a in smalls] + [ANY] * npart,
        out_specs=[tok(IN_COLS), pl.BlockSpec((8 * TILE_ROWS, LRU_W), lambda i: (0, 0)),
                   pl.BlockSpec((N_BD, BD, BD), lambda i: (0, 0, 0)), pl.BlockSpec((N_BD, BD, BD), lambda i: (0, 0, 0))]
        + [ANY] * npart,
        out_shape=[jax.ShapeDtypeStruct((t, IN_COLS), BF16), jax.ShapeDtypeStruct((8 * TILE_ROWS, LRU_W), F32),
                   jax.ShapeDtypeStruct((N_BD, BD, BD), F32), jax.ShapeDtypeStruct((N_BD, BD, BD), F32)]
        + [jax.ShapeDtypeStruct(a.shape, a.dtype) for a in parts],
        scratch_shapes=[pltpu.VMEM((TILE_ROWS, CONV_W), F32), pltpu.VMEM((TILE_ROWS, LRU_W), F32),
                        pltpu.VMEM((TILE_ROWS, LRU_W), F32), pltpu.VMEM((TILE_ROWS, LRU_W), F32),
                        pltpu.SemaphoreType.DMA((npart, 3)), pltpu.SemaphoreType.DMA((npart, 3))],
        compiler_params=_params(dimension_semantics=("arbitrary",)),
    )(dy, u, u, xr_all, hs_all, hs_all, c3_all, *gates, *smalls, *parts)
    return outs[:4], outs[4:]


def _in_bwd(dub, w_in_g, x, dx2, g1, parts, joins, core_chip):
    t, d = x.shape
    tm = min(t, MATMUL_TOKEN_TILE)
    nt = t // tm
    npart = len(parts)
    nj = len(joins)
    geometry = []
    for tag, shape, _, _ in joins:
        pr, pc = WGRAD_GEOMETRY[tag][:2]
        every = 1 if pr % (nt * 16) == 0 else 2
        geometry.append((pr, pc, pr * every // nt, every, shape[1] == pc))

    def body(cc_ref, du_ref, win_ref, x_ref, dx2_ref, g1_ref, *rest):
        sums, rest = [rest[4 * w:4 * w + 4] for w in range(nj)], rest[4 * nj:]
        part_refs, (gx_ref, st_ref), rest = rest[:npart], rest[npart:npart + 2], rest[npart + 2:]
        arrived_refs, joined, rest = rest[:npart], rest[npart:npart + nj], rest[npart + nj:]
        stages, (send_sems, recv_sems, j_local, j_send, j_recv) = rest[:nj], rest[nj:]
        exchange = _PartialExchange(part_refs, arrived_refs, send_sems, recv_sems)
        i = pl.program_id(0)
        c = cc_ref[0]

        def window(w, core, row0, rows):
            pr, pc, _, _, by_rows = geometry[w]
            if by_rows:
                return joined[w].at[pl.ds(core * pr + row0, rows), :]
            return joined[w].at[pl.ds(row0, rows), pl.ds(core * pc, pc)]

        def to_sibling(w, src, core, row0, rows):
            return pltpu.make_async_remote_copy(src_ref=src, dst_ref=window(w, core, row0, rows), send_sem=j_send.at[w],
                                                recv_sem=j_recv.at[w], device_id=_sibling(), device_id_type=MESH)

        @pl.when(i == 0)
        def _():
            exchange.start()
            st_ref[...] = jnp.zeros_like(st_ref)

        for w in range(nj):
            pr, pc, rb, every, _ = geometry[w]

            @pl.when(i % every == 0)
            def _(w=w, rb=rb, every=every):
                p_ref, r1_ref, r2_ref, r3_ref = sums[w]
                row0 = pl.multiple_of((i // every) * rb, rb)
                rows = stages[w].at[pl.ds(row0, rb), :]
                rows[...] = ((p_ref[0] + r1_ref[0].astype(F32)) + r2_ref[0].astype(F32)) + r3_ref[0].astype(F32)
                pltpu.make_async_copy(rows, window(w, c, row0, rb), j_local.at[w]).start()
                to_sibling(w, rows, c, row0, rb).start()

        dh1 = _dot_nt(du_ref[:, 0:IN_SHARD], win_ref[0])
        for j in range(1, N_CHIPS):
            dh1 = dh1 + _dot_nt(du_ref[:, j * IN_SHARD:(j + 1) * IN_SHARD], win_ref[j])
        xv = x_ref[...]
        rstd = lax.rsqrt(jnp.mean(xv * xv, axis=-1, keepdims=True) + EPS)
        xh = xv * rstd
        st_ref[0:1, :] += _colsum(dh1 * xh)
        dxh = dh1 * g1_ref[...]
        gx_ref[...] = dx2_ref[...] + rstd * (dxh - xh * jnp.mean(dxh * xh, axis=-1, keepdims=True))

        @pl.when(i == nt - 1)
        def _():
            exchange.wait()
            for w in range(nj):
                pr = geometry[w][0]
                pltpu.make_async_copy(stages[w], window(w, c, 0, pr), j_local.at[w]).wait()
                to_sibling(w, stages[w], 1 - c, 0, pr).wait()

    def tok(cols):
        return pl.BlockSpec((tm, cols), lambda i, cc: (i, 0))

    def partial(w, off):
        pr, pc, rb, every, _ = geometry[w]
        return pl.BlockSpec((1, rb, pc), lambda i, cc: ((cc[1] + off) % N_CHIPS, i // every, 0))

    sum_specs, sum_operands = [], []
    for w, (_, _, own, arrived) in enumerate(joins):
        sum_specs += [partial(w, off) for off in range(N_CHIPS)]
        sum_operands += [own, arrived, arrived, arrived]
    dma = pltpu.SemaphoreType.DMA
    outs = pl.pallas_call(
        body, name="in_bwd",
        grid_spec=pltpu.PrefetchScalarGridSpec(
            num_scalar_prefetch=1, grid=(nt,),
            in_specs=[tok(IN_COLS), pl.BlockSpec(w_in_g.shape, lambda i, cc: (0, 0, 0)), tok(d), tok(d),
                      pl.BlockSpec((1, d), lambda i, cc: (0, 0))] + sum_specs + [ANY] * npart,
            out_specs=[tok(d), pl.BlockSpec((TILE_ROWS, d), lambda i, cc: (0, 0))] + [ANY] * (npart + nj),
            scratch_shapes=[pltpu.VMEM((g[0], g[1]), F32) for g in geometry]
            + [dma((npart, 3)), dma((npart, 3)), dma((nj,)), dma((nj,)), dma((nj,))]),
        out_shape=[jax.ShapeDtypeStruct((t, d), F32), jax.ShapeDtypeStruct((TILE_ROWS, d), F32)]
        + [jax.ShapeDtypeStruct(a.shape, a.dtype) for a in parts]
        + [jax.ShapeDtypeStruct(shape, F32) for _, shape, _, _ in joins],
        compiler_params=_params(dimension_semantics=("arbitrary",)),
    )(core_chip, dub, w_in_g, x, dx2, g1, *sum_operands, *parts)
    return outs[:2], outs[2:2 + npart], outs[2 + npart:]


WGRAD_GEOMETRY = {
    "in": (512, IN_SHARD, lambda s, h: h, lambda s, h: s),
    "mlp_in": (512, D_MODEL, lambda s, h: h, lambda s, h: s),
    "mlp_out": (512, D_MODEL, lambda s, h: 2 * s + h, lambda s, h: 0),
    "out": (384, 512, lambda s, h: s, lambda s, h: h),
}
K_CHUNK = 512


def _sibling():
    x, y, c = _position()
    return (x, y, 1 - c)


def _wgrad(a, b, tag, core_chip, packs=(), parts=()):
    t = a.shape[0]
    pr, pc, a_blk, b_blk = WGRAD_GEOMETRY[tag]
    nk = t // K_CHUNK
    mine = N_CHIPS
    riding = len(packs)
    npart = len(parts)
    assert not (riding and npart)

    def body(cc_ref, a_ref, b_ref, *rest):
        if riding:
            pack_refs, (land_ref, p_ref, pb_ref), rest = rest[:riding], rest[riding:riding + 3], rest[riding + 3:]
            all_refs, (stage, rbuf, send_sems, recv_sems, rsem), g_sems = rest[:riding], rest[riding:riding + 5], rest[riding + 5:]
            gathers = [_PackGather(pack_refs[n], all_refs[n], *g_sems[3 * n:3 * n + 3]) for n in range(riding)]
        elif npart:
            part_refs, (land_ref, p_ref, pb_ref), rest = rest[:npart], rest[npart:npart + 3], rest[npart + 3:]
            arrived_refs, (stage, rbuf, send_sems, recv_sems, rsem, x_send, x_recv) = rest[:npart], rest[npart:]
            exchange = _PartialExchange(part_refs, arrived_refs, x_send, x_recv)
        else:
            land_ref, p_ref, pb_ref, stage, rbuf, send_sems, recv_sems, rsem = rest
        ph, s = pl.program_id(0), pl.program_id(1)
        if riding:
            @pl.when((ph == 0) & (s == 0))
            def _():
                for gather in gathers:
                    gather.start()

            @pl.when((ph == 1) & (s == N_CHIPS - 2))
            def _():
                for gather in gathers:
                    gather.hand_over()
        if npart:
            @pl.when((ph == 0) & (s == 0))
            def _():
                exchange.start()
        def push(k):
            return pltpu.make_async_remote_copy(src_ref=stage.at[k], dst_ref=land_ref.at[k], send_sem=send_sems.at[k],
                                                recv_sem=recv_sems.at[k], device_id=_sibling(), device_id_type=MESH)

        def landed():
            return pltpu.make_async_copy(land_ref.at[s], rbuf, rsem)

        @pl.when(ph == 1)
        def _():
            push(s).wait_recv()
            landed().start()

        slot = jnp.where(ph == 0, s, mine)
        acc = stage.at[slot]
        acc[...] = _dot_tn(a_ref[0:K_CHUNK, :], b_ref[0:K_CHUNK, :])
        for k in range(1, nk):
            acc[...] += _dot_tn(a_ref[k * K_CHUNK:(k + 1) * K_CHUNK, :], b_ref[k * K_CHUNK:(k + 1) * K_CHUNK, :])

        @pl.when(ph == 0)
        def _():
            push(s).start()

        @pl.when(ph == 1)
        def _():
            landed().wait()
            p = stage[mine] + rbuf[...]
            p_ref[0] = p
            pb_ref[0] = p.astype(BF16)

        @pl.when((ph == 1) & (s == N_CHIPS - 1))
        def _():
            for k in range(N_CHIPS):
                push(k).wait_send()
            for gather in (gathers if riding else ()):
                gather.finish()
            if npart:
                exchange.wait()

    def half(ph, cc):
        return jnp.where(ph == 0, 1 - cc[0], cc[0])

    def out_slot(ph, s, cc):
        return (jnp.where(ph == 0, 0, s), 0, 0)

    piece = jax.ShapeDtypeStruct((N_CHIPS, pr, pc), F32)
    in_specs = [pl.BlockSpec((t, pr), lambda ph, s, cc: (0, a_blk(s, half(ph, cc)))),
                pl.BlockSpec((t, pc), lambda ph, s, cc: (0, b_blk(s, half(ph, cc))))]
    out_specs = [ANY, pl.BlockSpec((1, pr, pc), out_slot), pl.BlockSpec((1, pr, pc), out_slot)]
    out_shape = [piece, piece, jax.ShapeDtypeStruct((N_CHIPS, pr, pc), BF16)]
    scratch = [pltpu.VMEM((N_CHIPS + 1, pr, pc), F32), pltpu.VMEM((pr, pc), F32),
               pltpu.SemaphoreType.DMA((N_CHIPS,)), pltpu.SemaphoreType.DMA((N_CHIPS,)), pltpu.SemaphoreType.DMA]
    operands = [a, b]
    for pack in packs:
        in_specs.append(pl.BlockSpec(pack.shape, lambda ph, s, cc: (0, 0)))
        out_specs.append(ANY)
        out_shape.append(jax.ShapeDtypeStruct((N_DEVICES,) + pack.shape, pack.dtype))
        operands.append(pack)
    for pack in packs:
        scratch += _PackGather.semaphores()
    if npart:
        in_specs += [ANY] * npart
        out_specs += [ANY] * npart
        out_shape += [jax.ShapeDtypeStruct(p.shape, p.dtype) for p in parts]
        scratch += [pltpu.SemaphoreType.DMA((npart, 3)), pltpu.SemaphoreType.DMA((npart, 3))]
        operands += list(parts)
    return pl.pallas_call(
        body, name="wgrad_" + tag,
        grid_spec=pltpu.PrefetchScalarGridSpec(
            num_scalar_prefetch=1, grid=(2, N_CHIPS), in_specs=in_specs, out_specs=out_specs, scratch_shapes=scratch),
        out_shape=out_shape,
        compiler_params=_params(dimension_semantics=("arbitrary", "arbitrary")),
    )(core_chip, *operands)[1:]


def _other_chips(x, y):
    return [(1 - x, y), (x, 1 - y), (1 - x, 1 - y)]


class _ShardGather:
    PAIRS = 9

    def __init__(self, outs, send_sems, recv_sems):
        self.outs, self.send_sems, self.recv_sems = outs, send_sems, recv_sems
        x, y, c = _position()
        self.c, self.j = c, 2 * x + y
        self.sibling = (x, y, 1 - c)
        self.chips = _other_chips(x, y)

    def _chip(self, k):
        px, py = self.chips[k]
        return 2 * px + py

    def _half(self, w, chip, which):
        hr = self.outs[w].shape[1] // 2
        return self.outs[w].at[chip, pl.ds(which * hr, hr), :]

    def _quarter(self, w, chip, q):
        qr = self.outs[w].shape[1] // 4
        return self.outs[w].at[chip, pl.ds(self.c * 2 * qr + q * qr, qr), :]

    def _copy(self, ref, w, pair, to, src=None):
        return pltpu.make_async_remote_copy(src_ref=ref if src is None else src, dst_ref=ref, send_sem=self.send_sems.at[w, pair],
                                            recv_sem=self.recv_sems.at[w, pair], device_id=to, device_id_type=MESH)

    def direct(self, w, k, q, src=None):
        return self._copy(self._quarter(w, self.j, q), w, 2 * k + q, (*self.chips[k], self.c), src)

    def direct_landed(self, w, k, q):
        return self._copy(self._quarter(w, self._chip(k), q), w, 2 * k + q, (*self.chips[k], self.c))

    def pass_on(self, w, q):
        return self._copy(self._quarter(w, self._chip(q), q), w, 4 + q, (*self.chips[1 - q], self.c))

    def passed_landed(self, w, q):
        return self._copy(self._quarter(w, self._chip(2), q), w, 4 + q, (*self.chips[1 - q], self.c))

    def hand_over(self, w, k):
        return self._copy(self._half(w, self._chip(k), self.c), w, 6 + k, self.sibling)

    def handed(self, w, k):
        return self._copy(self._half(w, self._chip(k), 1 - self.c), w, 6 + k, self.sibling)

    def start_direct(self, w, src_half=None):
        qr = self.outs[w].shape[1] // 4
        for k, q in ((0, 0), (1, 1), (0, 1), (1, 0)):
            self.direct(w, k, q, None if src_half is None else src_half.at[pl.ds(q * qr, qr), :]).start()

    def start_pass_on(self, w):
        for q in (0, 1):
            self.direct_landed(w, q, q).wait_recv()
            self.pass_on(w, q).start()

    def start_hand_over(self, w, diagonal):
        if diagonal:
            for q in (0, 1):
                self.passed_landed(w, q).wait_recv()
            self.hand_over(w, 2).start()
        else:
            for k in (0, 1):
                self.direct_landed(w, k, 1 - k).wait_recv()
                self.hand_over(w, k).start()

    def finish(self, w):
        for k in range(3):
            self.handed(w, k).wait_recv()
            self.hand_over(w, k).wait_send()
        for q in (0, 1):
            self.pass_on(w, q).wait_send()
            for k in (0, 1):
                self.direct(w, k, q).wait_send()


def _gather_first(w_in, w_out, w1, w2, small):
    bigs = (w_in, w_out, w1, w2)
    nb = len(bigs)

    def body(win_ref, wout_ref, w1_ref, w2_ref, sm_ref, gin, gout, g1, g2, gsm, st_in, st_out, st_1, st_2,
             send_sems, recv_sems, sm_send, sm_recv, local_sems):
        srcs = (win_ref, wout_ref, w1_ref, w2_ref)
        stages = (st_in, st_out, st_1, st_2)
        outs = (gin, gout, g1, g2)
        plan = _ShardGather(outs[:1], send_sems, recv_sems)
        j, c = plan.j, plan.c
        local = [pltpu.make_async_copy(stages[w], outs[w].at[j], local_sems.at[w]) for w in range(nb)]
        local.append(pltpu.make_async_copy(sm_ref, gsm.at[j], local_sems.at[nb]))

        def small_copy(k):
            px, py = plan.chips[k]
            return pltpu.make_async_remote_copy(src_ref=sm_ref, dst_ref=gsm.at[j], send_sem=sm_send.at[k],
                                                recv_sem=sm_recv.at[k], device_id=(px, py, c), device_id_type=MESH)

        def small_landed(k):
            px, py = plan.chips[k]
            return pltpu.make_async_remote_copy(src_ref=sm_ref, dst_ref=gsm.at[2 * px + py], send_sem=sm_send.at[k],
                                                recv_sem=sm_recv.at[k], device_id=(px, py, c), device_id_type=MESH)

        hr = w_in.shape[0] // 2
        st_in[...] = win_ref[...].astype(BF16)
        plan.start_direct(0, st_in.at[pl.ds(c * hr, hr), :])
        for k in range(3):
            small_copy(k).start()
        for src, st in zip(srcs[1:], stages[1:]):
            st[...] = src[...].astype(BF16)
        for cp in local:
            cp.start()
        plan.start_pass_on(0)
        plan.start_hand_over(0, diagonal=False)
        plan.start_hand_over(0, diagonal=True)
        for k in range(3):
            small_landed(k).wait_recv()
            small_copy(k).wait_send()
        plan.finish(0)
        for cp in local:
            cp.wait()

    def gathered(a, dtype):
        return jax.ShapeDtypeStruct((N_CHIPS,) + a.shape, dtype)

    return pl.pallas_call(
        body, name="gather_first",
        in_specs=[VMEM] * 5, out_specs=[ANY] * 5,
        out_shape=[gathered(a, BF16) for a in bigs] + [gathered(small, F32)],
        scratch_shapes=[pltpu.VMEM(a.shape, BF16) for a in bigs]
        + [pltpu.SemaphoreType.DMA((1, _ShardGather.PAIRS)), pltpu.SemaphoreType.DMA((1, _ShardGather.PAIRS)), pltpu.SemaphoreType.DMA((3,)),
           pltpu.SemaphoreType.DMA((3,)), pltpu.SemaphoreType.DMA((nb + 1,))],
        compiler_params=_params(),
    )(*bigs, small)


class _PartialExchange:
    def __init__(self, parts, arrived, send_sems, recv_sems):
        self.parts, self.arrived, self.send_sems, self.recv_sems = parts, arrived, send_sems, recv_sems
        x, y, c = _position()
        self.c, self.j = c, 2 * x + y
        self.chips = _other_chips(x, y)

    def _copy(self, w, k, slot):
        px, py = self.chips[k]
        return pltpu.make_async_remote_copy(
            src_ref=self.parts[w].at[2 * px + py], dst_ref=self.arrived[w].at[slot], send_sem=self.send_sems.at[w, k],
            recv_sem=self.recv_sems.at[w, k], device_id=(px, py, self.c), device_id_type=MESH)

    def start(self):
        for w in range(len(self.parts)):
            for k in range(3):
                self._copy(w, k, self.j).start()

    def wait(self):
        for w in range(len(self.parts)):
            for k in range(3):
                px, py = self.chips[k]
                self._copy(w, k, 2 * px + py).wait()


class _PackGather:
    def __init__(self, p_ref, all_ref, send_sems, recv_sems, local_sem):
        self.p_ref, self.all_ref, self.send_sems, self.recv_sems, self.local_sem = p_ref, all_ref, send_sems, recv_sems, local_sem
        x, y, c = _position()
        self.me, self.sibling, self.c = (x, y, c), (x, y, 1 - c), c
        self.chips = _other_chips(x, y)

    @staticmethod
    def semaphores():
        return [pltpu.SemaphoreType.DMA((7,)), pltpu.SemaphoreType.DMA((7,)), pltpu.SemaphoreType.DMA]

    def _copy(self, k, block, to, from_pack=False):
        px, py, pc = block
        slot = self.all_ref.at[4 * px + 2 * py + pc]
        return pltpu.make_async_remote_copy(src_ref=self.p_ref if from_pack else slot, dst_ref=slot, send_sem=self.send_sems.at[k],
                                            recv_sem=self.recv_sems.at[k], device_id=to, device_id_type=MESH)

    def _mine(self):
        x, y, c = self.me
        return pltpu.make_async_copy(self.p_ref, self.all_ref.at[4 * x + 2 * y + c], self.local_sem)

    def _first(self):
        return [self._copy(0, self.me, self.sibling, True)] + [
            self._copy(1 + k, self.me, (*chip, self.c), True) for k, chip in enumerate(self.chips)]

    def _passed(self):
        return [self._copy(4 + k, (*chip, self.c), self.sibling) for k, chip in enumerate(self.chips)]

    def start(self):
        self._mine().start()
        for cp in self._first():
            cp.start()

    def hand_over(self):
        for k, chip in enumerate(self.chips):
            self._copy(1 + k, (*chip, self.c), self.me).wait_recv()
            self._passed()[k].start()

    def finish(self):
        self._copy(0, self.sibling, self.me).wait_recv()
        for k, chip in enumerate(self.chips):
            self._copy(4 + k, (*chip, 1 - self.c), self.me).wait_recv()
        for cp in self._first() + self._passed():
            cp.wait_send()
        self._mine().wait()


class _DirectGather:
    def __init__(self, p_ref, all_ref, send_sems, recv_sems, local_sem):
        self.p_ref, self.all_ref, self.send_sems, self.recv_sems, self.local_sem = p_ref, all_ref, send_sems, recv_sems, local_sem
        self.me = _position()

    semaphores = _PackGather.semaphores

    def _peer(self, r):
        x, y, c = self.me
        return ((1 - x) if r & 4 else x, (1 - y) if r & 2 else y, (1 - c) if r & 1 else c)

    def _copy(self, r, slot_of):
        px, py, pc = slot_of
        return pltpu.make_async_remote_copy(src_ref=self.p_ref, dst_ref=self.all_ref.at[4 * px + 2 * py + pc],
                                            send_sem=self.send_sems.at[r - 1], recv_sem=self.recv_sems.at[r - 1],
                                            device_id=self._peer(r), device_id_type=MESH)

    def _mine(self):
        x, y, c = self.me
        return pltpu.make_async_copy(self.p_ref, self.all_ref.at[4 * x + 2 * y + c], self.local_sem)

    def start(self):
        self._mine().start()
        for r in range(1, N_DEVICES):
            self._copy(r, self.me).start()

    def finish(self):
        for r in range(1, N_DEVICES):
            self._copy(r, self._peer(r)).wait()
        self._mine().wait()


def _adamw(w, g, m, v):
    m = ADAM_B1 * m + (1.0 - ADAM_B1) * g
    v = ADAM_B2 * v + (1.0 - ADAM_B2) * (g * g)
    m_hat = m / ADAM_BC1
    v_hat = v / ADAM_BC2
    delta = -ADAM_LR * (m_hat / (jnp.sqrt(v_hat) + ADAM_EPS) + ADAM_WD * w)
    return delta, m, v


JOIN_SUB = 4


def _join(tag, shard_shape, part, arrived, core_chip, block=None):
    pr, pc = WGRAD_GEOMETRY[tag][:2]
    rb = pr // JOIN_SUB
    by_rows = shard_shape[1] == pc
    riding = block is not None

    def body(cc_ref, p_ref, r1_ref, r2_ref, r3_ref, *rest):
        if riding:
            blk_ref, g_ref, all_ref, stage, send_sems, recv_sems, local_sems, b_send, b_recv, b_local = rest
            gather = _DirectGather(blk_ref, all_ref, b_send, b_recv, b_local)
        else:
            g_ref, stage, send_sems, recv_sems, local_sems = rest
        i = pl.program_id(0)
        c = cc_ref[0]
        if riding:
            @pl.when(i == 0)
            def _():
                gather.start()

        def window(core, k):
            if by_rows:
                return g_ref.at[pl.ds((core * JOIN_SUB + k) * rb, rb), :]
            return g_ref.at[pl.ds(k * rb, rb), pl.ds(core * pc, pc)]

        def keep(k):
            return pltpu.make_async_copy(stage.at[k], window(c, k), local_sems.at[k])

        def push(k):
            return pltpu.make_async_remote_copy(src_ref=stage.at[k], dst_ref=window(c, k), send_sem=send_sems.at[k],
                                                recv_sem=recv_sems.at[k], device_id=_sibling(), device_id_type=MESH)

        def pushed(k):
            return pltpu.make_async_remote_copy(src_ref=stage.at[k], dst_ref=window(1 - c, k), send_sem=send_sems.at[k],
                                                recv_sem=recv_sems.at[k], device_id=_sibling(), device_id_type=MESH)

        stage[i] = ((p_ref[0] + r1_ref[0].astype(F32)) + r2_ref[0].astype(F32)) + r3_ref[0].astype(F32)
        keep(i).start()
        push(i).start()

        @pl.when(i == JOIN_SUB - 1)
        def _():
            for k in range(JOIN_SUB):
                keep(k).wait()
                push(k).wait_send()
                pushed(k).wait_recv()
            if riding:
                gather.finish()

    def partial(off):
        return pl.BlockSpec((1, rb, pc), lambda i, cc: ((cc[1] + off) % N_CHIPS, i, 0))

    in_specs = [partial(0), partial(1), partial(2), partial(3)]
    out_specs = [ANY]
    out_shape = [jax.ShapeDtypeStruct(shard_shape, F32)]
    scratch = [pltpu.VMEM((JOIN_SUB, rb, pc), F32), pltpu.SemaphoreType.DMA((JOIN_SUB,)),
               pltpu.SemaphoreType.DMA((JOIN_SUB,)), pltpu.SemaphoreType.DMA((JOIN_SUB,))]
    operands = [part, arrived, arrived, arrived]
    if riding:
        in_specs.append(pl.BlockSpec(block.shape, lambda i, cc: (0, 0)))
        out_specs.append(ANY)
        out_shape.append(jax.ShapeDtypeStruct((N_DEVICES,) + block.shape, block.dtype))
        scratch += _DirectGather.semaphores()
        operands.append(block)
    outs = pl.pallas_call(
        body, name="join_" + tag,
        grid_spec=pltpu.PrefetchScalarGridSpec(
            num_scalar_prefetch=1, grid=(JOIN_SUB,), in_specs=in_specs, out_specs=out_specs, scratch_shapes=scratch),
        out_shape=out_shape,
        compiler_params=_params(dimension_semantics=("arbitrary",)),
    )(core_chip, *operands)
    return outs if riding else outs[0]


def _adamw_big(w, g, m, v, name):
    rows, cols = w.shape
    rb = 256 if rows % 256 == 0 else rows

    def body(w_ref, g_ref, m_ref, v_ref, go_ref, d_ref, nm_ref, nv_ref):
        g = g_ref[...]
        go_ref[...] = g
        d_ref[...], nm_ref[...], nv_ref[...] = _adamw(w_ref[...], g, m_ref[...], v_ref[...])

    spec = pl.BlockSpec((rb, cols), lambda i: (i, 0))
    return pl.pallas_call(
        body, name=name, grid=(rows // rb,), in_specs=[spec] * 4, out_specs=[spec] * 4,
        out_shape=[jax.ShapeDtypeStruct(w.shape, F32)] * 4,
        compiler_params=_params(dimension_semantics=("arbitrary",)),
    )(w, g, m, v)


def _small_step(vec_packs, mat_packs, mix_g_blocks, w_pack, m_pack, v_pack, conv_wmv, rconv_wmv):
    vec_rows = vec_packs.shape[1]
    rows, cols = vec_rows + mat_packs.shape[1], vec_packs.shape[2]
    cshard = conv_wmv.shape[2]
    rshard = rconv_wmv.shape[2]
    mix_row = PK_MIX_G * TILE_ROWS

    def body(vec_ref, mat_ref, blk_ref, w_ref, m_ref, v_ref, cw_ref, rw_ref, g_ref, d_ref, nm_ref, nv_ref, co_ref, ro_ref):
        total = vec_ref[0]
        mats = mat_ref[0].astype(F32)
        late = blk_ref[0]
        for k in range(1, N_DEVICES):
            total = total + vec_ref[k]
            mats = mats + mat_ref[k].astype(F32)
            late = late + blk_ref[k]
        g_ref[0:vec_rows, :] = total
        g_ref[vec_rows:, :] = mats
        g_ref[mix_row:mix_row + TILE_ROWS, :] = late
        g = g_ref[...]
        d_ref[...], nm_ref[...], nv_ref[...] = _adamw(w_ref[...], g, m_ref[...], v_ref[...])

        x, y, _ = _position()
        j = 2 * x + y
        cblk = total[PK_CONV_W * 8:PK_CONV_W * 8 + 8, :]
        rblk = total[PK_RCONV_W * 8:PK_RCONV_W * 8 + 8, :]
        cg = cblk[:, 0:cshard]
        rg = rblk[:, 0:rshard]
        for k in range(1, N_CHIPS):
            cg = jnp.where(j == k, cblk[:, k * cshard:(k + 1) * cshard], cg)
            rg = jnp.where(j == k, rblk[:, k * rshard:(k + 1) * rshard], rg)
        co_ref[0] = cg
        co_ref[1], co_ref[2], co_ref[3] = _adamw(cw_ref[0], cg, cw_ref[1], cw_ref[2])
        ro_ref[0] = rg
        ro_ref[1], ro_ref[2], ro_ref[3] = _adamw(rw_ref[0], rg, rw_ref[1], rw_ref[2])

    pack = [jax.ShapeDtypeStruct((rows, cols), F32)] * 4
    return pl.pallas_call(
        body, name="small_grads_step", in_specs=[VMEM] * 8, out_specs=[VMEM] * 6,
        out_shape=pack + [jax.ShapeDtypeStruct((4, TILE_ROWS, cshard), F32), jax.ShapeDtypeStruct((4, TILE_ROWS, rshard), F32)],
        compiler_params=_params(),
    )(vec_packs, mat_packs, mix_g_blocks, w_pack, m_pack, v_pack, conv_wmv, rconv_wmv)


def _blk(a):
    a = a.reshape(-1, a.shape[-1])
    return jnp.pad(a, ((0, TILE_ROWS - a.shape[0]), (0, D_MODEL - a.shape[1])))


def _zero_blk():
    return jnp.zeros((TILE_ROWS, D_MODEL), F32)


def _pack_params(p, pre):
    get = lambda n: p[pre + n]
    return jnp.concatenate([
        _blk(get("g_norm_rnn")), _blk(get("rnn_conv_b")), _blk(get("b_a")), _blk(get("b_x")), _blk(get("lru_lambda")),
        _zero_blk(), _zero_blk(), _blk(get("g_norm_conv")), _blk(get("final_norm_g").reshape(1, -1)), _blk(get("norm_mlp_g")),
        _zero_blk(), _blk(get("norm_mix_g")), get("w_a").reshape(64, D_MODEL), get("w_x").reshape(64, D_MODEL)], axis=0)


def _to_block_diag(w):
    w4 = w.reshape(N_BD, 4, 64, 64)
    eye = jnp.eye(4, dtype=w.dtype)
    return (w4[:, :, :, None, :] * eye[None, :, None, :, None]).reshape(N_BD, BD, BD)


def _from_block_diag(d):
    d5 = d.reshape(N_BD, 4, 64, 4, 64)
    return jnp.stack([d5[:, q, :, q, :] for q in range(4)], axis=1).reshape(64, D_MODEL)


def _pad_rows(a):
    return jnp.pad(a, ((0, TILE_ROWS - a.shape[0]), (0, 0)))


_NAMES = ['norm_mix_g', 'w_in', 'conv_w', 'rnn_conv_w', 'rnn_conv_b', 'w_a', 'b_a', 'w_x', 'b_x', 'lru_lambda',
          'g_norm_conv', 'g_norm_rnn', 'w_out', 'norm_mlp_g', 'w_mlp_in', 'w_mlp_out', 'final_norm_g']


def kernel(x, norm_mix_g, w_in, conv_w, rnn_conv_w, rnn_conv_b, w_a, b_a, w_x, b_x, lru_lambda, g_norm_conv, g_norm_rnn, w_out, norm_mlp_g, w_mlp_in, w_mlp_out, final_norm_g, loss_target, m_norm_mix_g, m_w_in, m_conv_w, m_rnn_conv_w, m_rnn_conv_b, m_w_a, m_b_a, m_w_x, m_b_x, m_lru_lambda, m_g_norm_conv, m_g_norm_rnn, m_w_out, m_norm_mlp_g, m_w_mlp_in, m_w_mlp_out, m_final_norm_g, v_norm_mix_g, v_w_in, v_conv_w, v_rnn_conv_w, v_rnn_conv_b, v_w_a, v_b_a, v_w_x, v_b_x, v_lru_lambda, v_g_norm_conv, v_g_norm_rnn, v_w_out, v_norm_mlp_g, v_w_mlp_in, v_w_mlp_out, v_final_norm_g):
    args = dict(locals())
    p = {}
    for n in _NAMES:
        for pre in ("", "m_", "v_"):
            a = args[pre + n]
            p[pre + n] = a[0] if a.ndim >= 3 else a
    xs = x[0]
    target = loss_target[0]
    core_chip = jnp.stack([lax.axis_index("c"), 2 * lax.axis_index("x") + lax.axis_index("y")]).astype(jnp.int32)
    cshard = p["conv_w"].shape[1]
    rshard = p["rnn_conv_w"].shape[1]

    small = jnp.concatenate([_pad_rows(p["conv_w"]), _pad_rows(p["rnn_conv_w"])], axis=1)
    w_in_g, w_out_g, w1_g, w2_g, small_g = _gather_first(p["w_in"], p["w_out"], p["w_mlp_in"], p["w_mlp_out"], small)
    conv_full = small_g[:, :3, :cshard].transpose(1, 0, 2).reshape(3, CONV_W)
    rconv_full = small_g[:, :4, cshard:].transpose(1, 0, 2).reshape(4, LRU_W)
    wa_bd = _to_block_diag(p["w_a"]).astype(BF16)
    wx_bd = _to_block_diag(p["w_x"]).astype(BF16)
    gf = p["final_norm_g"].reshape(1, -1)
    lru = (wa_bd, p["b_a"], wx_bd, p["b_x"], p["lru_lambda"], p["g_norm_conv"], p["g_norm_rnn"])

    (u, h1b, xr, hs, c3, yb, *gates), (w_out_g, w1_g, w2_g) = _fwd_mix(
        xs, p["norm_mix_g"], w_in_g, conv_full, rconv_full, p["rnn_conv_b"], *lru, (w_out_g, w1_g, w2_g))
    zb, dpb, h2b, dx3b, dx2, dx2b, dy, st_mlp = _mlp_fwd_bwd(
        xs, yb, w_out_g.reshape(-1, D_MODEL), w1_g, w2_g.reshape(-1, D_MODEL), p["norm_mlp_g"], gf, target)

    part_out = _wgrad(yb, dx2b, "out", core_chip)
    *part_1, arrived_out = _wgrad(h2b, dpb, "mlp_in", core_chip, parts=(part_out[1],))
    part_2 = _wgrad(zb, dx3b, "mlp_out", core_chip)
    (dub, st_mix, dwa_bd, dwx_bd), (arrived_1, arrived_2) = _mix_bwd(
        dy, u, xr, hs, c3, gates, conv_full, rconv_full, wa_bd, wx_bd, p["lru_lambda"], p["g_norm_conv"], p["g_norm_rnn"],
        (part_1[1], part_2[1]))
    arrived_mlp = (arrived_out, arrived_1, arrived_2)
    vec_pack = jnp.concatenate([st_mix, st_mlp, _zero_blk()], axis=0)
    mat_pack = jnp.concatenate([_from_block_diag(dwa_bd), _from_block_diag(dwx_bd)], axis=0).astype(BF16)
    *part_in, vec_packs, mat_packs = _wgrad(h1b, dub, "in", core_chip, packs=(vec_pack, mat_pack))
    early = (("w_out", "out", part_out, arrived_mlp[0]), ("w_mlp_in", "mlp_in", part_1, arrived_mlp[1]),
             ("w_mlp_out", "mlp_out", part_2, arrived_mlp[2]))
    (grad_x, st_in), arrived_in, joined = _in_bwd(
        dub, w_in_g, xs, dx2, p["norm_mix_g"], (part_in[1],),
        [(tag, p[n].shape, part[0], arrived) for n, tag, part, arrived in early], core_chip)
    g_in, mix_g_blocks = _join("in", p["w_in"].shape, part_in[0], arrived_in[0], core_chip, st_in)
    big = {}
    for n, tag, g in [(n, tag, g) for (n, tag, _, _), g in zip(early, joined)] + [("w_in", "in", g_in)]:
        big[n] = _adamw_big(p[n], g, p["m_" + n], p["v_" + n], "adamw_" + tag)

    conv_wmv = jnp.stack([_pad_rows(p[pre + "conv_w"]) for pre in ("", "m_", "v_")])
    rconv_wmv = jnp.stack([_pad_rows(p[pre + "rnn_conv_w"]) for pre in ("", "m_", "v_")])
    g_pack, d_pack, m_pack, v_pack, conv_out, rconv_out = _small_step(
        vec_packs, mat_packs, mix_g_blocks, _pack_params(p, ""), _pack_params(p, "m_"), _pack_params(p, "v_"), conv_wmv, rconv_wmv)

    def unpack(pk, kind):
        def vec(b, width=D_MODEL):
            return pk[b * 8:b * 8 + 1, :width]
        return {
            "norm_mix_g": vec(PK_MIX_G), "rnn_conv_b": vec(PK_RCONV_B), "b_a": vec(PK_B_A), "b_x": vec(PK_B_X),
            "lru_lambda": vec(PK_LAMBDA), "g_norm_conv": vec(PK_G_NORM_CONV, CONV_W), "g_norm_rnn": vec(PK_G_NORM_RNN),
            "norm_mlp_g": vec(PK_MLP_G), "final_norm_g": vec(PK_FINAL_G).reshape(-1),
            "w_a": pk[PK_W_A * 8:PK_W_A * 8 + 64].reshape(1, 16, 64, 64), "w_x": pk[PK_W_X * 8:PK_W_X * 8 + 64].reshape(1, 16, 64, 64),
            "conv_w": conv_out[kind, :3][None], "rnn_conv_w": rconv_out[kind, :4][None],
            "w_in": big["w_in"][kind][None], "w_out": big["w_out"][kind][None],
            "w_mlp_in": big["w_mlp_in"][kind][None], "w_mlp_out": big["w_mlp_out"][kind][None],
        }

    outs = [unpack(pk, kind) for kind, pk in enumerate((g_pack, d_pack, m_pack, v_pack))]
    for o in outs:
        for n in ("norm_mix_g", "rnn_conv_b", "b_a", "b_x", "lru_lambda", "g_norm_conv", "g_norm_rnn", "norm_mlp_g"):
            o[n] = o[n].reshape(1, -1)
    loss = g_pack[PK_LOSS * 8, 0]
    return (loss, grad_x[None], *[o[n] for o in outs for n in _NAMES])
```

```python
import functools
import math

import jax
import jax.numpy as jnp
from jax import lax
from jax.experimental import pallas as pl
from jax.experimental.pallas import tpu as pltpu

F32 = jnp.float32
BF16 = jnp.bfloat16
MESH = pl.DeviceIdType.MESH
ANY = pl.BlockSpec(memory_space=pl.ANY)
VMEM = pl.BlockSpec(memory_space=pltpu.VMEM)

EPS = 1e-6
LRU_C = 8.0
D_MODEL = 1024
CONV_W = 512
LRU_W = 1024
IN_COLS = 3 * CONV_W + 2 * LRU_W
IN_SHARD = IN_COLS // 4
N_CHIPS = 4
N_DEVICES = 8
BD = 256
N_BD = LRU_W // BD

ADAM_LR = 0.001
ADAM_B1 = 0.9
ADAM_B2 = 0.999
ADAM_EPS = 1e-08
ADAM_WD = 0.01
ADAM_STEP = 10
ADAM_BC1 = 1.0 - ADAM_B1 ** ADAM_STEP
ADAM_BC2 = 1.0 - ADAM_B2 ** ADAM_STEP

TILE_ROWS = 8
TOKEN_TILE = 256
MATMUL_TOKEN_TILE = 512
VMEM_LIMIT = 56 * 1024 * 1024

PK_G_NORM_RNN, PK_RCONV_B, PK_B_A, PK_B_X, PK_LAMBDA, PK_RCONV_W, PK_CONV_W, PK_G_NORM_CONV = range(8)
PK_FINAL_G, PK_MLP_G, PK_LOSS, PK_MIX_G = 8, 9, 10, 11
PK_W_A = 12
PK_W_X = 20
PK_BLOCKS = 28
PK_ROWS = PK_BLOCKS * TILE_ROWS


def _params(**kw):
    return pltpu.CompilerParams(vmem_limit_bytes=VMEM_LIMIT, **kw)


def _position():
    x, y, c = lax.axis_index("x"), lax.axis_index("y"), lax.axis_index("c")
    return x, y, c


def _sigmoid(v):
    return 1.0 / (1.0 + jnp.exp(-v))


def _one_minus_square(log_a, a):
    v = 2.0 * log_a
    series = -v * (1.0 + v * (0.5 + v * (1.0 / 6.0)))
    return jnp.where(v > -0.01, series, 1.0 - a * a)


_GELU_C = math.sqrt(2.0 / math.pi)
_GELU_K = 0.044715


def _gelu_and_grad(g):
    th = jnp.tanh(_GELU_C * (g + _GELU_K * g * g * g))
    gelu = 0.5 * g * (1.0 + th)
    dgelu = 0.5 * (1.0 + th) + 0.5 * g * (1.0 - th * th) * (_GELU_C * (1.0 + 3.0 * _GELU_K * g * g))
    return gelu, dgelu


def _rows(shape):
    return lax.broadcasted_iota(jnp.int32, shape, 0)


def _shift_down(v, k, prev8):
    rolled = pltpu.roll(v, k, 0)
    halo = pltpu.roll(prev8, k, 0)
    head = jnp.where(_rows(halo.shape) < k, halo, rolled[:TILE_ROWS])
    return jnp.concatenate([head, rolled[TILE_ROWS:]], axis=0)


def _shift_up(v, k, next8):
    n = v.shape[0]
    rolled = pltpu.roll(v, n - k, 0)
    halo = pltpu.roll(next8, TILE_ROWS - k, 0)
    tail = jnp.where(_rows(halo.shape) >= TILE_ROWS - k, halo, rolled[n - TILE_ROWS:])
    return jnp.concatenate([rolled[: n - TILE_ROWS], tail], axis=0)


def _scan_rows(a, b, carry, reverse=False):
    n, w = a.shape
    groups = n // TILE_ROWS
    a3 = a.reshape(groups, TILE_ROWS, w)
    b3 = b.reshape(groups, TILE_ROWS, w)
    sub = lax.broadcasted_iota(jnp.int32, a3.shape, 1)
    s = 1
    while s < TILE_ROWS:
        shift = TILE_ROWS - s if reverse else s
        keep = (sub < TILE_ROWS - s) if reverse else (sub >= s)
        b3 = b3 + jnp.where(keep, a3 * pltpu.roll(b3, shift, 1), 0.0)
        a3 = a3 * jnp.where(keep, pltpu.roll(a3, shift, 1), 1.0)
        s *= 2
    out = [None] * groups
    edge = 0 if reverse else TILE_ROWS - 1
    for g in (range(groups - 1, -1, -1) if reverse else range(groups)):
        out[g] = b3[g] + a3[g] * carry
        carry = out[g][edge:edge + 1]
    return jnp.concatenate(out, axis=0)


def _softplus_neg(lam):
    e = jnp.exp(-jnp.abs(lam))
    log1p_e = jnp.where(e < 1e-2, e * (1.0 - e * (0.5 - e * (1.0 / 3.0 - e * 0.25))), jnp.log(1.0 + e))
    sp = jnp.maximum(-lam, 0.0) + log1p_e
    dsp = -_sigmoid(-lam)
    return sp, dsp


def _block_diag_dot(vb, w_ref):
    return jnp.concatenate(
        [jnp.dot(vb[:, j * BD:(j + 1) * BD], w_ref[j], preferred_element_type=F32) for j in range(N_BD)], axis=1)


def _block_diag_dot_t(vb, w_ref):
    return jnp.concatenate(
        [lax.dot_general(vb[:, j * BD:(j + 1) * BD], w_ref[j], (((1,), (1,)), ((), ())), preferred_element_type=F32)
         for j in range(N_BD)], axis=1)


def _dot_nt(a, b):
    return lax.dot_general(a, b, (((1,), (1,)), ((), ())), preferred_element_type=F32)


def _dot_tn(a, b):
    return lax.dot_general(a, b, (((0,), (0,)), ((), ())), preferred_element_type=F32)


def _lru_gates(xr, wa_ref, ba, wx_ref, bx, sp):
    xrb = xr.astype(BF16)
    r = _sigmoid(_block_diag_dot(xrb, wa_ref) + ba)
    ig = _sigmoid(_block_diag_dot(xrb, wx_ref) + bx)
    log_a = (-LRU_C) * r * sp
    a = jnp.exp(log_a)
    mult = jnp.sqrt(_one_minus_square(log_a, a))
    return r, ig, a, mult


def _colsum(v):
    return jnp.sum(v, axis=0, keepdims=True)


N_FWD_OUT = 7


def _fwd_mix(x, g1, w_in_g, conv_w, rconv_w, rconv_b, wa_bd, b_a, wx_bd, b_x, lam, g_nc, g_nr, later):
    t, d = x.shape
    tm = TOKEN_TILE
    nt = t // tm
    nl = len(later)
    assert nl == 3
    pass_on_at = [nt * f // 16 for f in (3, 5, 9)]
    neighbours_at = [nt * f // 16 for f in (10, 11, 12)]
    diagonal_at = [nt * f // 16 for f in (13, 14, 14)]

    def body(x_ref, g1_ref, win_ref, cw_ref, rw_ref, rb_ref, wa_ref, ba_ref, wx_ref, bx_ref, lam_ref, gnc_ref, gnr_ref,
             *rest):
        later_in, outs, rest = rest[:nl], rest[nl:nl + N_FWD_OUT], rest[nl + N_FWD_OUT:]
        u_ref, h1_ref, xr_ref, hs_ref, c3_ref, y_ref, gates_ref = outs
        later_out, (cv_prev, xin_prev, h_prev, send_sems, recv_sems) = rest[:nl], rest[nl:]
        del later_in
        step = pl.program_id(0)
        plan = _ShardGather(later_out, send_sems, recv_sems)

        @pl.when(step == 0)
        def _():
            cv_prev[...] = jnp.zeros_like(cv_prev)
            xin_prev[...] = jnp.zeros_like(xin_prev)
            h_prev[...] = jnp.zeros_like(h_prev)
            for w in range(nl):
                plan.start_direct(w)

        for w in range(nl):
            @pl.when(step == pass_on_at[w])
            def _(w=w):
                plan.start_pass_on(w)

            @pl.when(step == neighbours_at[w])
            def _(w=w):
                plan.start_hand_over(w, diagonal=False)

            @pl.when(step == diagonal_at[w])
            def _(w=w):
                plan.start_hand_over(w, diagonal=True)

        xv = x_ref[...]
        rstd = lax.rsqrt(jnp.mean(xv * xv, axis=-1, keepdims=True) + EPS)
        h1b = ((xv * rstd) * g1_ref[...]).astype(BF16)
        h1_ref[...] = h1b
        for j in range(N_CHIPS):
            u_ref[:, j * IN_SHARD:(j + 1) * IN_SHARD] = jnp.dot(h1b, win_ref[j], preferred_element_type=F32)
        gate_b = u_ref[:, 0:CONV_W]
        cv = u_ref[:, CONV_W:2 * CONV_W] * u_ref[:, 2 * CONV_W:3 * CONV_W]
        x_r = u_ref[:, 3 * CONV_W:3 * CONV_W + LRU_W]
        g = u_ref[:, 3 * CONV_W + LRU_W:]

        cw = cw_ref[...]
        cvp = cv_prev[...]
        conv3 = cw[0:1] * _shift_down(cv, 2, cvp) + cw[1:2] * _shift_down(cv, 1, cvp) + cw[2:3] * cv
        cv_prev[...] = cv[tm - TILE_ROWS:]
        c3_ref[...] = conv3
        y_conv = gate_b * conv3

        rw = rw_ref[...]
        xp = xin_prev[...]
        xr = (rw[0:1] * _shift_down(x_r, 3, xp) + rw[1:2] * _shift_down(x_r, 2, xp)
              + rw[2:3] * _shift_down(x_r, 1, xp) + rw[3:4] * x_r) + rb_ref[...]
        xin_prev[...] = x_r[tm - TILE_ROWS:]
        xr_ref[...] = xr
        sp, _ = _softplus_neg(lam_ref[...])
        r, ig, a, mult = _lru_gates(xr, wa_ref, ba_ref[...], wx_ref, bx_ref[...], sp)
        for n, gate in enumerate((r, ig, a, mult)):
            gates_ref[:, n * LRU_W:(n + 1) * LRU_W] = gate
        h = _scan_rows(a, mult * (ig * xr), h_prev[...])
        h_prev[...] = h[tm - 1:tm]
        hs_ref[...] = h
        gelu, _ = _gelu_and_grad(g)
        y_rnn = h * gelu

        na = y_conv * lax.rsqrt(jnp.mean(y_conv * y_conv, axis=-1, keepdims=True) + EPS) * gnc_ref[...]
        nb = y_rnn * lax.rsqrt(jnp.mean(y_rnn * y_rnn, axis=-1, keepdims=True) + EPS) * gnr_ref[...]
        y_ref[:, :CONV_W] = na.astype(BF16)
        y_ref[:, CONV_W:] = nb.astype(BF16)

        @pl.when(step == nt - 1)
        def _():
            for w in range(nl):
                plan.finish(w)

    def full(a):
        nd = a.ndim
        return pl.BlockSpec(a.shape, lambda i: (0,) * nd)

    def tok(cols):
        return pl.BlockSpec((tm, cols), lambda i: (i, 0))

    def act(cols, dtype=F32):
        return jax.ShapeDtypeStruct((t, cols), dtype)

    smalls = (g1, w_in_g, conv_w, rconv_w, rconv_b, wa_bd, b_a, wx_bd, b_x, lam, g_nc, g_nr)
    n_in = 1 + len(smalls)
    outs = pl.pallas_call(
        body, name="fwd_mix", grid=(nt,),
        in_specs=[tok(d)] + [full(a) for a in smalls] + [ANY] * nl,
        out_specs=[tok(IN_COLS), tok(d), tok(LRU_W), tok(LRU_W), tok(CONV_W), tok(CONV_W + LRU_W)]
        + [tok(4 * LRU_W)] + [ANY] * nl,
        out_shape=[act(IN_COLS), act(d, BF16), act(LRU_W), act(LRU_W), act(CONV_W), act(CONV_W + LRU_W, BF16)]
        + [act(4 * LRU_W)] + [jax.ShapeDtypeStruct(a.shape, a.dtype) for a in later],
        input_output_aliases={n_in + w: N_FWD_OUT + w for w in range(nl)},
        scratch_shapes=[pltpu.VMEM((TILE_ROWS, CONV_W), F32), pltpu.VMEM((TILE_ROWS, LRU_W), F32),
                        pltpu.VMEM((1, LRU_W), F32), pltpu.SemaphoreType.DMA((nl, _ShardGather.PAIRS)),
                        pltpu.SemaphoreType.DMA((nl, _ShardGather.PAIRS))],
        compiler_params=_params(dimension_semantics=("arbitrary",)),
    )(x, *smalls, *later)
    return outs[:N_FWD_OUT], outs[N_FWD_OUT:]


def _mlp_fwd_bwd(x, yb, w_out_g, w1_g, w2_g, g2, gf, target):
    t, d = x.shape
    tm = TOKEN_TILE
    ff = w2_g.shape[0]
    mix = w_out_g.shape[0]
    ffs = ff // N_CHIPS

    def body(x_ref, y_ref, g2_ref, gf_ref, tgt_ref, wout_hbm, w1_hbm, w2_hbm,
             z_ref, dp_ref, h2_ref, dx3b_ref, dx2_ref, dx2b_ref, dy_ref, st_ref, wout, w1, w2, p_ref):
        @pl.when(pl.program_id(0) == 0)
        def _():
            pltpu.sync_copy(wout_hbm, wout)
            pltpu.sync_copy(w1_hbm, w1)
            pltpu.sync_copy(w2_hbm, w2)
            st_ref[...] = jnp.zeros_like(st_ref)

        x2 = x_ref[...] + jnp.dot(y_ref[...], wout[...], preferred_element_type=F32)
        r2 = lax.rsqrt(jnp.mean(x2 * x2, axis=-1, keepdims=True) + EPS)
        xh2 = x2 * r2
        g2v = g2_ref[...]
        h2b = (xh2 * g2v).astype(BF16)
        h2_ref[...] = h2b
        for j in range(N_CHIPS):
            p_ref[:, j * ffs:(j + 1) * ffs] = jnp.dot(h2b, w1[j], preferred_element_type=F32)
        rp = jnp.maximum(p_ref[...], 0.0)
        zb = (rp * rp).astype(BF16)
        z_ref[...] = zb
        x3 = x2 + jnp.dot(zb, w2[...], preferred_element_type=F32)
        r3 = lax.rsqrt(jnp.mean(x3 * x3, axis=-1, keepdims=True) + EPS)
        xh3 = x3 * r3
        gfv = gf_ref[...]
        err = xh3 * gfv - tgt_ref[...]
        loss = (0.5 / d) * jnp.sum(err * err)
        dout = err * (1.0 / d)
        st_ref[PK_FINAL_G * 8 - 64:PK_FINAL_G * 8 - 63, :] += _colsum(dout * xh3)
        st_ref[PK_LOSS * 8 - 64:PK_LOSS * 8 - 63, :] += jnp.zeros((1, d), F32) + loss
        dxh3 = dout * gfv
        dx3 = r3 * (dxh3 - xh3 * jnp.mean(dxh3 * xh3, axis=-1, keepdims=True))
        dx3b = dx3.astype(BF16)
        dx3b_ref[...] = dx3b
        dpb = (_dot_nt(dx3b, w2[...]) * (2.0 * rp)).astype(BF16)
        dp_ref[...] = dpb
        dh2 = _dot_nt(dpb[:, 0:ffs], w1[0])
        for j in range(1, N_CHIPS):
            dh2 = dh2 + _dot_nt(dpb[:, j * ffs:(j + 1) * ffs], w1[j])
        st_ref[PK_MLP_G * 8 - 64:PK_MLP_G * 8 - 63, :] += _colsum(dh2 * xh2)
        dxh2 = dh2 * g2v
        dx2 = dx3 + r2 * (dxh2 - xh2 * jnp.mean(dxh2 * xh2, axis=-1, keepdims=True))
        dx2_ref[...] = dx2
        dx2b = dx2.astype(BF16)
        dx2b_ref[...] = dx2b
        dy_ref[...] = _dot_nt(dx2b, wout[...])

    def tok(cols):
        return pl.BlockSpec((tm, cols), lambda i: (i, 0))

    def row(cols):
        return pl.BlockSpec((1, cols), lambda i: (0, 0))

    return pl.pallas_call(
        body, name="mlp_fwd_bwd", grid=(t // tm,),
        in_specs=[tok(d), tok(mix), row(d), row(d), tok(d), ANY, ANY, ANY],
        out_specs=[tok(ff), tok(ff), tok(d), tok(d), tok(d), tok(d), tok(mix),
                   pl.BlockSpec((3 * TILE_ROWS, d), lambda i: (0, 0))],
        out_shape=[jax.ShapeDtypeStruct((t, ff), BF16), jax.ShapeDtypeStruct((t, ff), BF16),
                   jax.ShapeDtypeStruct((t, d), BF16), jax.ShapeDtypeStruct((t, d), BF16),
                   jax.ShapeDtypeStruct((t, d), F32), jax.ShapeDtypeStruct((t, d), BF16),
                   jax.ShapeDtypeStruct((t, mix), F32), jax.ShapeDtypeStruct((3 * TILE_ROWS, d), F32)],
        scratch_shapes=[pltpu.VMEM(w_out_g.shape, BF16), pltpu.VMEM(w1_g.shape, BF16), pltpu.VMEM(w2_g.shape, BF16),
                        pltpu.VMEM((tm, ff), F32)],
        compiler_params=_params(dimension_semantics=("arbitrary",)),
    )(x, yb, g2, gf, target, w_out_g, w1_g, w2_g)


def _mix_bwd(dy, u, xr_all, hs_all, c3_all, gates, conv_w, rconv_w, wa_bd, wx_bd, lam, g_nc, g_nr, parts):
    t = dy.shape[0]
    tm = TOKEN_TILE
    nt = t // tm
    hb = tm // TILE_ROWS
    npart = len(parts)

    def body(dy_ref, u_ref, uh_ref, xr_ref, hs_ref, hh_ref, c3_ref, gates_ref,
             cw_ref, rw_ref, wa_ref, wx_ref, lam_ref, gnc_ref, gnr_ref, *rest):
        part_refs, (du_ref, st_ref, dwa_ref, dwx_ref), rest = rest[:npart], rest[npart:npart + 4], rest[npart + 4:]
        arrived_refs, (dc_next, a_next, gs_next, dxr_next, send_sems, recv_sems) = rest[:npart], rest[npart:]
        exchange = _PartialExchange(part_refs, arrived_refs, send_sems, recv_sems)
        i = pl.program_id(0)

        @pl.when(i == 0)
        def _():
            exchange.start()
            dc_next[...] = jnp.zeros_like(dc_next)
            a_next[...] = jnp.zeros_like(a_next)
            gs_next[...] = jnp.zeros_like(gs_next)
            dxr_next[...] = jnp.zeros_like(dxr_next)
            st_ref[...] = jnp.zeros_like(st_ref)
            dwa_ref[...] = jnp.zeros_like(dwa_ref)
            dwx_ref[...] = jnp.zeros_like(dwx_ref)

        first_tile = i == nt - 1
        gate_b = u_ref[:, 0:CONV_W]
        gate_c = u_ref[:, CONV_W:2 * CONV_W]
        v = u_ref[:, 2 * CONV_W:3 * CONV_W]
        x_r = u_ref[:, 3 * CONV_W:3 * CONV_W + LRU_W]
        g = u_ref[:, 3 * CONV_W + LRU_W:]
        cv = gate_c * v
        cv_prev = jnp.where(first_tile, 0.0, uh_ref[:, CONV_W:2 * CONV_W] * uh_ref[:, 2 * CONV_W:3 * CONV_W])
        xin_prev = jnp.where(first_tile, 0.0, uh_ref[:, 3 * CONV_W:3 * CONV_W + LRU_W])
        hs_prev = jnp.where(first_tile, 0.0, hh_ref[...])

        def acc(block, val, width=LRU_W, row=0):
            r0 = block * TILE_ROWS + row
            st_ref[r0:r0 + 1, 0:width] += val

        conv3 = c3_ref[...]
        y_conv = gate_b * conv3
        ra = lax.rsqrt(jnp.mean(y_conv * y_conv, axis=-1, keepdims=True) + EPS)
        xha = y_conv * ra
        dna = dy_ref[:, :CONV_W]
        acc(PK_G_NORM_CONV, _colsum(dna * xha), CONV_W)
        dxha = dna * gnc_ref[...]
        dy_conv = ra * (dxha - xha * jnp.mean(dxha * xha, axis=-1, keepdims=True))
        du_ref[:, 0:CONV_W] = (dy_conv * conv3).astype(BF16)
        dc = dy_conv * gate_b
        cw = cw_ref[...]
        dcn = dc_next[...]
        dcv = cw[2:3] * dc + cw[1:2] * _shift_up(dc, 1, dcn) + cw[0:1] * _shift_up(dc, 2, dcn)
        dc_next[...] = dc[:TILE_ROWS]
        acc(PK_CONV_W, _colsum(dc * _shift_down(cv, 2, cv_prev)), CONV_W, 0)
        acc(PK_CONV_W, _colsum(dc * _shift_down(cv, 1, cv_prev)), CONV_W, 1)
        acc(PK_CONV_W, _colsum(dc * cv), CONV_W, 2)
        du_ref[:, CONV_W:2 * CONV_W] = (dcv * v).astype(BF16)
        du_ref[:, 2 * CONV_W:3 * CONV_W] = (dcv * gate_c).astype(BF16)

        hs = hs_ref[...]
        gelu, dgelu = _gelu_and_grad(g)
        y_rnn = hs * gelu
        rb = lax.rsqrt(jnp.mean(y_rnn * y_rnn, axis=-1, keepdims=True) + EPS)
        xhb = y_rnn * rb
        dnb = dy_ref[:, CONV_W:]
        acc(PK_G_NORM_RNN, _colsum(dnb * xhb))
        dxhb = dnb * gnr_ref[...]
        dy_rnn = rb * (dxhb - xhb * jnp.mean(dxhb * xhb, axis=-1, keepdims=True))
        du_ref[:, 3 * CONV_W + LRU_W:] = (dy_rnn * hs * dgelu).astype(BF16)
        dh = dy_rnn * gelu

        xr = xr_ref[...]
        xrb = xr.astype(BF16)
        sp, dsp = _softplus_neg(lam_ref[...])
        r, ig, a, mult = [gates_ref[:, n * LRU_W:(n + 1) * LRU_W] for n in range(4)]
        a_up = _shift_up(a, 1, a_next[...])
        a_next[...] = a[:TILE_ROWS]
        gs = _scan_rows(a_up, dh, gs_next[0:1, :], reverse=True)
        gs_next[...] = gs[:TILE_ROWS]
        da = gs * _shift_down(hs, 1, hs_prev)
        gx = gs * xr
        di = gx * mult
        dmult = gx * ig
        dxr = gs * (mult * ig)
        dlog_a = da * a - dmult * ((a * a) / mult)
        acc(PK_LAMBDA, _colsum(dlog_a * r) * ((-LRU_C) * dsp))
        dpa = (dlog_a * ((-LRU_C) * sp)) * (r * (1.0 - r))
        dpx = di * (ig * (1.0 - ig))
        acc(PK_B_A, _colsum(dpa))
        acc(PK_B_X, _colsum(dpx))
        dpab = dpa.astype(BF16)
        dpxb = dpx.astype(BF16)
        dxr = dxr + _block_diag_dot_t(dpab, wa_ref) + _block_diag_dot_t(dpxb, wx_ref)
        for j in range(N_BD):
            cols = slice(j * BD, (j + 1) * BD)
            dwa_ref[j] += _dot_tn(xrb[:, cols], dpab[:, cols])
            dwx_ref[j] += _dot_tn(xrb[:, cols], dpxb[:, cols])

        acc(PK_RCONV_B, _colsum(dxr))
        rw = rw_ref[...]
        dxn = dxr_next[...]
        dx_r = (rw[3:4] * dxr + rw[2:3] * _shift_up(dxr, 1, dxn) + rw[1:2] * _shift_up(dxr, 2, dxn)
                + rw[0:1] * _shift_up(dxr, 3, dxn))
        dxr_next[...] = dxr[:TILE_ROWS]
        for k in range(3):
            acc(PK_RCONV_W, _colsum(dxr * _shift_down(x_r, 3 - k, xin_prev)), LRU_W, k)
        acc(PK_RCONV_W, _colsum(dxr * x_r), LRU_W, 3)
        du_ref[:, 3 * CONV_W:3 * CONV_W + LRU_W] = dx_r.astype(BF16)

        @pl.when(i == nt - 1)
        def _():
            exchange.wait()

    def full(a):
        nd = a.ndim
        return pl.BlockSpec(a.shape, lambda i: (0,) * nd)

    def tok(cols):
        return pl.BlockSpec((tm, cols), lambda i: (nt - 1 - i, 0))

    def halo(cols):
        return pl.BlockSpec((TILE_ROWS, cols), lambda i: (jnp.maximum((nt - 1 - i) * hb - 1, 0), 0))

    smalls = (conv_w, rconv_w, wa_bd, wx_bd, lam, g_nc, g_nr)
    outs = pl.pallas_call(
        body, name="mix_bwd", grid=(nt,),
        in_specs=[tok(CONV_W + LRU_W), tok(IN_COLS), halo(IN_COLS), tok(LRU_W), tok(LRU_W), halo(LRU_W), tok(CONV_W)]
        + [tok(4 * LRU_W)] + [full(a) for a in smalls] + [ANY] * npart,
        out_specs=[tok(IN_COLS), pl.BlockSpec((8 * TILE_ROWS, LRU_W), lambda i: (0, 0)),
                   pl.BlockSpec((N_BD, BD, BD), lambda i: (0, 0, 0)), pl.BlockSpec((N_BD, BD, BD), lambda i: (0, 0, 0))]
        + [ANY] * npart,
        out_shape=[jax.ShapeDtypeStruct((t, IN_COLS), BF16), jax.ShapeDtypeStruct((8 * TILE_ROWS, LRU_W), F32),
                   jax.ShapeDtypeStruct((N_BD, BD, BD), F32), jax.ShapeDtypeStruct((N_BD, BD, BD), F32)]
        + [jax.ShapeDtypeStruct(a.shape, a.dtype) for a in parts],
        scratch_shapes=[pltpu.VMEM((TILE_ROWS, CONV_W), F32), pltpu.VMEM((TILE_ROWS, LRU_W), F32),
                        pltpu.VMEM((TILE_ROWS, LRU_W), F32), pltpu.VMEM((TILE_ROWS, LRU_W), F32),
                        pltpu.SemaphoreType.DMA((npart, 3)), pltpu.SemaphoreType.DMA((npart, 3))],
        compiler_params=_params(dimension_semantics=("arbitrary",)),
    )(dy, u, u, xr_all, hs_all, hs_all, c3_all, gates, *smalls, *parts)
    return outs[:4], outs[4:]


def _in_bwd(dub, w_in_g, x, dx2, g1, parts, joins, core_chip):
    t, d = x.shape
    tm = min(t, MATMUL_TOKEN_TILE)
    nt = t // tm
    npart = len(parts)
    nj = len(joins)
    geometry = []
    for tag, shape, _, _ in joins:
        pr, pc = WGRAD_GEOMETRY[tag][:2]
        every = 1 if pr % (nt * 16) == 0 else 2
        geometry.append((pr, pc, pr * every // nt, every, shape[1] == pc))

    def body(cc_ref, du_ref, win_ref, x_ref, dx2_ref, g1_ref, *rest):
        sums, rest = [rest[4 * w:4 * w + 4] for w in range(nj)], rest[4 * nj:]
        part_refs, (gx_ref, st_ref), rest = rest[:npart], rest[npart:npart + 2], rest[npart + 2:]
        arrived_refs, joined, rest = rest[:npart], rest[npart:npart + nj], rest[npart + nj:]
        stages, (send_sems, recv_sems, j_local, j_send, j_recv) = rest[:nj], rest[nj:]
        exchange = _PartialExchange(part_refs, arrived_refs, send_sems, recv_sems)
        i = pl.program_id(0)
        c = cc_ref[0]

        def window(w, core, row0, rows):
            pr, pc, _, _, by_rows = geometry[w]
            if by_rows:
                return joined[w].at[pl.ds(core * pr + row0, rows), :]
            return joined[w].at[pl.ds(row0, rows), pl.ds(core * pc, pc)]

        def to_sibling(w, src, core, row0, rows):
            return pltpu.make_async_remote_copy(src_ref=src, dst_ref=window(w, core, row0, rows), send_sem=j_send.at[w],
                                                recv_sem=j_recv.at[w], device_id=_sibling(), device_id_type=MESH)

        @pl.when(i == 0)
        def _():
            exchange.start()
            st_ref[...] = jnp.zeros_like(st_ref)

        for w in range(nj):
            pr, pc, rb, every, _ = geometry[w]

            @pl.when(i % every == 0)
            def _(w=w, rb=rb, every=every):
                p_ref, r1_ref, r2_ref, r3_ref = sums[w]
                row0 = pl.multiple_of((i // every) * rb, rb)
                rows = stages[w].at[pl.ds(row0, rb), :]
                rows[...] = ((p_ref[0] + r1_ref[0].astype(F32)) + r2_ref[0].astype(F32)) + r3_ref[0].astype(F32)
                pltpu.make_async_copy(rows, window(w, c, row0, rb), j_local.at[w]).start()
                to_sibling(w, rows, c, row0, rb).start()

        dh1 = _dot_nt(du_ref[:, 0:IN_SHARD], win_ref[0])
        for j in range(1, N_CHIPS):
            dh1 = dh1 + _dot_nt(du_ref[:, j * IN_SHARD:(j + 1) * IN_SHARD], win_ref[j])
        xv = x_ref[...]
        rstd = lax.rsqrt(jnp.mean(xv * xv, axis=-1, keepdims=True) + EPS)
        xh = xv * rstd
        st_ref[0:1, :] += _colsum(dh1 * xh)
        dxh = dh1 * g1_ref[...]
        gx_ref[...] = dx2_ref[...] + rstd * (dxh - xh * jnp.mean(dxh * xh, axis=-1, keepdims=True))

        @pl.when(i == nt - 1)
        def _():
            exchange.wait()
            for w in range(nj):
                pr = geometry[w][0]
                pltpu.make_async_copy(stages[w], window(w, c, 0, pr), j_local.at[w]).wait()
                to_sibling(w, stages[w], 1 - c, 0, pr).wait()

    def tok(cols):
        return pl.BlockSpec((tm, cols), lambda i, cc: (i, 0))

    def partial(w, off):
        pr, pc, rb, every, _ = geometry[w]
        return pl.BlockSpec((1, rb, pc), lambda i, cc: ((cc[1] + off) % N_CHIPS, i // every, 0))

    sum_specs, sum_operands = [], []
    for w, (_, _, own, arrived) in enumerate(joins):
        sum_specs += [partial(w, off) for off in range(N_CHIPS)]
        sum_operands += [own, arrived, arrived, arrived]
    dma = pltpu.SemaphoreType.DMA
    outs = pl.pallas_call(
        body, name="in_bwd",
        grid_spec=pltpu.PrefetchScalarGridSpec(
            num_scalar_prefetch=1, grid=(nt,),
            in_specs=[tok(IN_COLS), pl.BlockSpec(w_in_g.shape, lambda i, cc: (0, 0, 0)), tok(d), tok(d),
                      pl.BlockSpec((1, d), lambda i, cc: (0, 0))] + sum_specs + [ANY] * npart,
            out_specs=[tok(d), pl.BlockSpec((TILE_ROWS, d), lambda i, cc: (0, 0))] + [ANY] * (npart + nj),
            scratch_shapes=[pltpu.VMEM((g[0], g[1]), F32) for g in geometry]
            + [dma((npart, 3)), dma((npart, 3)), dma((nj,)), dma((nj,)), dma((nj,))]),
        out_shape=[jax.ShapeDtypeStruct((t, d), F32), jax.ShapeDtypeStruct((TILE_ROWS, d), F32)]
        + [jax.ShapeDtypeStruct(a.shape, a.dtype) for a in parts]
        + [jax.ShapeDtypeStruct(shape, F32) for _, shape, _, _ in joins],
        compiler_params=_params(dimension_semantics=("arbitrary",)),
    )(core_chip, dub, w_in_g, x, dx2, g1, *sum_operands, *parts)
    return outs[:2], outs[2:2 + npart], outs[2 + npart:]


WGRAD_GEOMETRY = {
    "in": (512, IN_SHARD, lambda s, h: h, lambda s, h: s),
    "mlp_in": (512, D_MODEL, lambda s, h: h, lambda s, h: s),
    "mlp_out": (512, D_MODEL, lambda s, h: 2 * s + h, lambda s, h: 0),
    "out": (384, 512, lambda s, h: s, lambda s, h: h),
}
K_CHUNK = 512


def _sibling():
    x, y, c = _position()
    return (x, y, 1 - c)


def _wgrad(a, b, tag, core_chip, packs=(), parts=()):
    t = a.shape[0]
    pr, pc, a_blk, b_blk = WGRAD_GEOMETRY[tag]
    nk = t // K_CHUNK
    mine = N_CHIPS
    riding = len(packs)
    npart = len(parts)
    assert not (riding and npart)

    def body(cc_ref, a_ref, b_ref, *rest):
        if riding:
            pack_refs, (land_ref, p_ref, pb_ref), rest = rest[:riding], rest[riding:riding + 3], rest[riding + 3:]
            all_refs, (stage, rbuf, send_sems, recv_sems, rsem), g_sems = rest[:riding], rest[riding:riding + 5], rest[riding + 5:]
            gathers = [_PackGather(pack_refs[n], all_refs[n], *g_sems[3 * n:3 * n + 3]) for n in range(riding)]
        elif npart:
            part_refs, (land_ref, p_ref, pb_ref), rest = rest[:npart], rest[npart:npart + 3], rest[npart + 3:]
            arrived_refs, (stage, rbuf, send_sems, recv_sems, rsem, x_send, x_recv) = rest[:npart], rest[npart:]
            exchange = _PartialExchange(part_refs, arrived_refs, x_send, x_recv)
        else:
            land_ref, p_ref, pb_ref, stage, rbuf, send_sems, recv_sems, rsem = rest
        ph, s = pl.program_id(0), pl.program_id(1)
        if riding:
            @pl.when((ph == 0) & (s == 0))
            def _():
                for gather in gathers:
                    gather.start()

            @pl.when((ph == 1) & (s == N_CHIPS - 2))
            def _():
                for gather in gathers:
                    gather.hand_over()
        if npart:
            @pl.when((ph == 0) & (s == 0))
            def _():
                exchange.start()
        def push(k):
            return pltpu.make_async_remote_copy(src_ref=stage.at[k], dst_ref=land_ref.at[k], send_sem=send_sems.at[k],
                                                recv_sem=recv_sems.at[k], device_id=_sibling(), device_id_type=MESH)

        def landed():
            return pltpu.make_async_copy(land_ref.at[s], rbuf, rsem)

        @pl.when(ph == 1)
        def _():
            push(s).wait_recv()
            landed().start()

        slot = jnp.where(ph == 0, s, mine)
        acc = stage.at[slot]
        acc[...] = _dot_tn(a_ref[0:K_CHUNK, :], b_ref[0:K_CHUNK, :])
        for k in range(1, nk):
            acc[...] += _dot_tn(a_ref[k * K_CHUNK:(k + 1) * K_CHUNK, :], b_ref[k * K_CHUNK:(k + 1) * K_CHUNK, :])

        @pl.when(ph == 0)
        def _():
            push(s).start()

        @pl.when(ph == 1)
        def _():
            landed().wait()
            p = stage[mine] + rbuf[...]
            p_ref[0] = p
            pb_ref[0] = p.astype(BF16)

        @pl.when((ph == 1) & (s == N_CHIPS - 1))
        def _():
            for k in range(N_CHIPS):
                push(k).wait_send()
            for gather in (gathers if riding else ()):
                gather.finish()
            if npart:
                exchange.wait()

    def half(ph, cc):
        return jnp.where(ph == 0, 1 - cc[0], cc[0])

    def out_slot(ph, s, cc):
        return (jnp.where(ph == 0, 0, s), 0, 0)

    piece = jax.ShapeDtypeStruct((N_CHIPS, pr, pc), F32)
    in_specs = [pl.BlockSpec((t, pr), lambda ph, s, cc: (0, a_blk(s, half(ph, cc)))),
                pl.BlockSpec((t, pc), lambda ph, s, cc: (0, b_blk(s, half(ph, cc))))]
    out_specs = [ANY, pl.BlockSpec((1, pr, pc), out_slot), pl.BlockSpec((1, pr, pc), out_slot)]
    out_shape = [piece, piece, jax.ShapeDtypeStruct((N_CHIPS, pr, pc), BF16)]
    scratch = [pltpu.VMEM((N_CHIPS + 1, pr, pc), F32), pltpu.VMEM((pr, pc), F32),
               pltpu.SemaphoreType.DMA((N_CHIPS,)), pltpu.SemaphoreType.DMA((N_CHIPS,)), pltpu.SemaphoreType.DMA]
    operands = [a, b]
    for pack in packs:
        in_specs.append(pl.BlockSpec(pack.shape, lambda ph, s, cc: (0, 0)))
        out_specs.append(ANY)
        out_shape.append(jax.ShapeDtypeStruct((N_DEVICES,) + pack.shape, pack.dtype))
        operands.append(pack)
    for pack in packs:
        scratch += _PackGather.semaphores()
    if npart:
        in_specs += [ANY] * npart
        out_specs += [ANY] * npart
        out_shape += [jax.ShapeDtypeStruct(p.shape, p.dtype) for p in parts]
        scratch += [pltpu.SemaphoreType.DMA((npart, 3)), pltpu.SemaphoreType.DMA((npart, 3))]
        operands += list(parts)
    return pl.pallas_call(
        body, name="wgrad_" + tag,
        grid_spec=pltpu.PrefetchScalarGridSpec(
            num_scalar_prefetch=1, grid=(2, N_CHIPS), in_specs=in_specs, out_specs=out_specs, scratch_shapes=scratch),
        out_shape=out_shape,
        compiler_params=_params(dimension_semantics=("arbitrary", "arbitrary")),
    )(core_chip, *operands)[1:]


def _other_chips(x, y):
    return [(1 - x, y), (x, 1 - y), (1 - x, 1 - y)]


class _ShardGather:
    PAIRS = 9

    def __init__(self, outs, send_sems, recv_sems):
        self.outs, self.send_sems, self.recv_sems = outs, send_sems, recv_sems
        x, y, c = _position()
        self.c, self.j = c, 2 * x + y
        self.sibling = (x, y, 1 - c)
        self.chips = _other_chips(x, y)

    def _chip(self, k):
        px, py = self.chips[k]
        return 2 * px + py

    def _half(self, w, chip, which):
        hr = self.outs[w].shape[1] // 2
        return self.outs[w].at[chip, pl.ds(which * hr, hr), :]

    def _quarter(self, w, chip, q):
        qr = self.outs[w].shape[1] // 4
        return self.outs[w].at[chip, pl.ds(self.c * 2 * qr + q * qr, qr), :]

    def _copy(self, ref, w, pair, to, src=None):
        return pltpu.make_async_remote_copy(src_ref=ref if src is None else src, dst_ref=ref, send_sem=self.send_sems.at[w, pair],
                                            recv_sem=self.recv_sems.at[w, pair], device_id=to, device_id_type=MESH)

    def direct(self, w, k, q, src=None):
        return self._copy(self._quarter(w, self.j, q), w, 2 * k + q, (*self.chips[k], self.c), src)

    def direct_landed(self, w, k, q):
        return self._copy(self._quarter(w, self._chip(k), q), w, 2 * k + q, (*self.chips[k], self.c))

    def pass_on(self, w, q):
        return self._copy(self._quarter(w, self._chip(q), q), w, 4 + q, (*self.chips[1 - q], self.c))

    def passed_landed(self, w, q):
        return self._copy(self._quarter(w, self._chip(2), q), w, 4 + q, (*self.chips[1 - q], self.c))

    def hand_over(self, w, k):
        return self._copy(self._half(w, self._chip(k), self.c), w, 6 + k, self.sibling)

    def handed(self, w, k):
        return self._copy(self._half(w, self._chip(k), 1 - self.c), w, 6 + k, self.sibling)

    def start_direct(self, w, src_half=None):
        qr = self.outs[w].shape[1] // 4
        for k, q in ((0, 0), (1, 1), (0, 1), (1, 0)):
            self.direct(w, k, q, None if src_half is None else src_half.at[pl.ds(q * qr, qr), :]).start()

    def start_pass_on(self, w):
        for q in (0, 1):
            self.direct_landed(w, q, q).wait_recv()
            self.pass_on(w, q).start()

    def start_hand_over(self, w, diagonal):
        if diagonal:
            for q in (0, 1):
                self.passed_landed(w, q).wait_recv()
            self.hand_over(w, 2).start()
        else:
            for k in (0, 1):
                self.direct_landed(w, k, 1 - k).wait_recv()
                self.hand_over(w, k).start()

    def finish(self, w):
        for k in range(3):
            self.handed(w, k).wait_recv()
            self.hand_over(w, k).wait_send()
        for q in (0, 1):
            self.pass_on(w, q).wait_send()
            for k in (0, 1):
                self.direct(w, k, q).wait_send()


def _gather_first(w_in, w_out, w1, w2, small):
    bigs = (w_in, w_out, w1, w2)
    nb = len(bigs)

    def body(win_ref, wout_ref, w1_ref, w2_ref, sm_ref, gin, gout, g1, g2, gsm, st_in, st_out, st_1, st_2,
             send_sems, recv_sems, sm_send, sm_recv, local_sems):
        srcs = (win_ref, wout_ref, w1_ref, w2_ref)
        stages = (st_in, st_out, st_1, st_2)
        outs = (gin, gout, g1, g2)
        plan = _ShardGather(outs[:1], send_sems, recv_sems)
        j, c = plan.j, plan.c
        local = [pltpu.make_async_copy(stages[w], outs[w].at[j], local_sems.at[w]) for w in range(nb)]
        local.append(pltpu.make_async_copy(sm_ref, gsm.at[j], local_sems.at[nb]))

        def small_copy(k):
            px, py = plan.chips[k]
            return pltpu.make_async_remote_copy(src_ref=sm_ref, dst_ref=gsm.at[j], send_sem=sm_send.at[k],
                                                recv_sem=sm_recv.at[k], device_id=(px, py, c), device_id_type=MESH)

        def small_landed(k):
            px, py = plan.chips[k]
            return pltpu.make_async_remote_copy(src_ref=sm_ref, dst_ref=gsm.at[2 * px + py], send_sem=sm_send.at[k],
                                                recv_sem=sm_recv.at[k], device_id=(px, py, c), device_id_type=MESH)

        hr = w_in.shape[0] // 2
        st_in[...] = win_ref[...].astype(BF16)
        plan.start_direct(0, st_in.at[pl.ds(c * hr, hr), :])
        for k in range(3):
            small_copy(k).start()
        for src, st in zip(srcs[1:], stages[1:]):
            st[...] = src[...].astype(BF16)
        for cp in local:
            cp.start()
        plan.start_pass_on(0)
        plan.start_hand_over(0, diagonal=False)
        plan.start_hand_over(0, diagonal=True)
        for k in range(3):
            small_landed(k).wait_recv()
            small_copy(k).wait_send()
        plan.finish(0)
        for cp in local:
            cp.wait()

    def gathered(a, dtype):
        return jax.ShapeDtypeStruct((N_CHIPS,) + a.shape, dtype)

    return pl.pallas_call(
        body, name="gather_first",
        in_specs=[VMEM] * 5, out_specs=[ANY] * 5,
        out_shape=[gathered(a, BF16) for a in bigs] + [gathered(small, F32)],
        scratch_shapes=[pltpu.VMEM(a.shape, BF16) for a in bigs]
        + [pltpu.SemaphoreType.DMA((1, _ShardGather.PAIRS)), pltpu.SemaphoreType.DMA((1, _ShardGather.PAIRS)), pltpu.SemaphoreType.DMA((3,)),
           pltpu.SemaphoreType.DMA((3,)), pltpu.SemaphoreType.DMA((nb + 1,))],
        compiler_params=_params(),
    )(*bigs, small)


class _PartialExchange:
    def __init__(self, parts, arrived, send_sems, recv_sems):
        self.parts, self.arrived, self.send_sems, self.recv_sems = parts, arrived, send_sems, recv_sems
        x, y, c = _position()
        self.c, self.j = c, 2 * x + y
        self.chips = _other_chips(x, y)

    def _copy(self, w, k, slot):
        px, py = self.chips[k]
        return pltpu.make_async_remote_copy(
            src_ref=self.parts[w].at[2 * px + py], dst_ref=self.arrived[w].at[slot], send_sem=self.send_sems.at[w, k],
            recv_sem=self.recv_sems.at[w, k], device_id=(px, py, self.c), device_id_type=MESH)

    def start(self):
        for w in range(len(self.parts)):
            for k in range(3):
                self._copy(w, k, self.j).start()

    def wait(self):
        for w in range(len(self.parts)):
            for k in range(3):
                px, py = self.chips[k]
                self._copy(w, k, 2 * px + py).wait()


class _PackGather:
    def __init__(self, p_ref, all_ref, send_sems, recv_sems, local_sem):
        self.p_ref, self.all_ref, self.send_sems, self.recv_sems, self.local_sem = p_ref, all_ref, send_sems, recv_sems, local_sem
        x, y, c = _position()
        self.me, self.sibling, self.c = (x, y, c), (x, y, 1 - c), c
        self.chips = _other_chips(x, y)

    @staticmethod
    def semaphores():
        return [pltpu.SemaphoreType.DMA((7,)), pltpu.SemaphoreType.DMA((7,)), pltpu.SemaphoreType.DMA]

    def _copy(self, k, block, to, from_pack=False):
        px, py, pc = block
        slot = self.all_ref.at[4 * px + 2 * py + pc]
        return pltpu.make_async_remote_copy(src_ref=self.p_ref if from_pack else slot, dst_ref=slot, send_sem=self.send_sems.at[k],
                                            recv_sem=self.recv_sems.at[k], device_id=to, device_id_type=MESH)

    def _mine(self):
        x, y, c = self.me
        return pltpu.make_async_copy(self.p_ref, self.all_ref.at[4 * x + 2 * y + c], self.local_sem)

    def _first(self):
        return [self._copy(0, self.me, self.sibling, True)] + [
            self._copy(1 + k, self.me, (*chip, self.c), True) for k, chip in enumerate(self.chips)]

    def _passed(self):
        return [self._copy(4 + k, (*chip, self.c), self.sibling) for k, chip in enumerate(self.chips)]

    def start(self):
        self._mine().start()
        for cp in self._first():
            cp.start()

    def hand_over(self):
        for k, chip in enumerate(self.chips):
            self._copy(1 + k, (*chip, self.c), self.me).wait_recv()
            self._passed()[k].start()

    def finish(self):
        self._copy(0, self.sibling, self.me).wait_recv()
        for k, chip in enumerate(self.chips):
            self._copy(4 + k, (*chip, 1 - self.c), self.me).wait_recv()
        for cp in self._first() + self._passed():
            cp.wait_send()
        self._mine().wait()


class _DirectGather:
    def __init__(self, p_ref, all_ref, send_sems, recv_sems, local_sem):
        self.p_ref, self.all_ref, self.send_sems, self.recv_sems, self.local_sem = p_ref, all_ref, send_sems, recv_sems, local_sem
        self.me = _position()

    semaphores = _PackGather.semaphores

    def _peer(self, r):
        x, y, c = self.me
        return ((1 - x) if r & 4 else x, (1 - y) if r & 2 else y, (1 - c) if r & 1 else c)

    def _copy(self, r, slot_of):
        px, py, pc = slot_of
        return pltpu.make_async_remote_copy(src_ref=self.p_ref, dst_ref=self.all_ref.at[4 * px + 2 * py + pc],
                                            send_sem=self.send_sems.at[r - 1], recv_sem=self.recv_sems.at[r - 1],
                                            device_id=self._peer(r), device_id_type=MESH)

    def _mine(self):
        x, y, c = self.me
        return pltpu.make_async_copy(self.p_ref, self.all_ref.at[4 * x + 2 * y + c], self.local_sem)

    def start(self):
        self._mine().start()
        for r in range(1, N_DEVICES):
            self._copy(r, self.me).start()

    def finish(self):
        for r in range(1, N_DEVICES):
            self._copy(r, self._peer(r)).wait()
        self._mine().wait()


def _adamw(w, g, m, v):
    m = ADAM_B1 * m + (1.0 - ADAM_B1) * g
    v = ADAM_B2 * v + (1.0 - ADAM_B2) * (g * g)
    m_hat = m / ADAM_BC1
    v_hat = v / ADAM_BC2
    delta = -ADAM_LR * (m_hat / (jnp.sqrt(v_hat) + ADAM_EPS) + ADAM_WD * w)
    return delta, m, v


JOIN_SUB = 4


def _join(tag, shard_shape, part, arrived, core_chip, block=None):
    pr, pc = WGRAD_GEOMETRY[tag][:2]
    rb = pr // JOIN_SUB
    by_rows = shard_shape[1] == pc
    riding = block is not None

    def body(cc_ref, p_ref, r1_ref, r2_ref, r3_ref, *rest):
        if riding:
            blk_ref, g_ref, all_ref, stage, send_sems, recv_sems, local_sems, b_send, b_recv, b_local = rest
            gather = _DirectGather(blk_ref, all_ref, b_send, b_recv, b_local)
        else:
            g_ref, stage, send_sems, recv_sems, local_sems = rest
        i = pl.program_id(0)
        c = cc_ref[0]
        if riding:
            @pl.when(i == 0)
            def _():
                gather.start()

        def window(core, k):
            if by_rows:
                return g_ref.at[pl.ds((core * JOIN_SUB + k) * rb, rb), :]
            return g_ref.at[pl.ds(k * rb, rb), pl.ds(core * pc, pc)]

        def keep(k):
            return pltpu.make_async_copy(stage.at[k], window(c, k), local_sems.at[k])

        def push(k):
            return pltpu.make_async_remote_copy(src_ref=stage.at[k], dst_ref=window(c, k), send_sem=send_sems.at[k],
                                                recv_sem=recv_sems.at[k], device_id=_sibling(), device_id_type=MESH)

        def pushed(k):
            return pltpu.make_async_remote_copy(src_ref=stage.at[k], dst_ref=window(1 - c, k), send_sem=send_sems.at[k],
                                                recv_sem=recv_sems.at[k], device_id=_sibling(), device_id_type=MESH)

        stage[i] = ((p_ref[0] + r1_ref[0].astype(F32)) + r2_ref[0].astype(F32)) + r3_ref[0].astype(F32)
        keep(i).start()
        push(i).start()

        @pl.when(i == JOIN_SUB - 1)
        def _():
            for k in range(JOIN_SUB):
                keep(k).wait()
                push(k).wait_send()
                pushed(k).wait_recv()
            if riding:
                gather.finish()

    def partial(off):
        return pl.BlockSpec((1, rb, pc), lambda i, cc: ((cc[1] + off) % N_CHIPS, i, 0))

    in_specs = [partial(0), partial(1), partial(2), partial(3)]
    out_specs = [ANY]
    out_shape = [jax.ShapeDtypeStruct(shard_shape, F32)]
    scratch = [pltpu.VMEM((JOIN_SUB, rb, pc), F32), pltpu.SemaphoreType.DMA((JOIN_SUB,)),
               pltpu.SemaphoreType.DMA((JOIN_SUB,)), pltpu.SemaphoreType.DMA((JOIN_SUB,))]
    operands = [part, arrived, arrived, arrived]
    if riding:
        in_specs.append(pl.BlockSpec(block.shape, lambda i, cc: (0, 0)))
        out_specs.append(ANY)
        out_shape.append(jax.ShapeDtypeStruct((N_DEVICES,) + block.shape, block.dtype))
        scratch += _DirectGather.semaphores()
        operands.append(block)
    outs = pl.pallas_call(
        body, name="join_" + tag,
        grid_spec=pltpu.PrefetchScalarGridSpec(
            num_scalar_prefetch=1, grid=(JOIN_SUB,), in_specs=in_specs, out_specs=out_specs, scratch_shapes=scratch),
        out_shape=out_shape,
        compiler_params=_params(dimension_semantics=("arbitrary",)),
    )(core_chip, *operands)
    return outs if riding else outs[0]


def _adamw_big(w, g, m, v, name):
    rows, cols = w.shape
    rb = 256 if rows % 256 == 0 else rows

    def body(w_ref, g_ref, m_ref, v_ref, go_ref, d_ref, nm_ref, nv_ref):
        g = g_ref[...]
        go_ref[...] = g
        d_ref[...], nm_ref[...], nv_ref[...] = _adamw(w_ref[...], g, m_ref[...], v_ref[...])

    spec = pl.BlockSpec((rb, cols), lambda i: (i, 0))
    return pl.pallas_call(
        body, name=name, grid=(rows // rb,), in_specs=[spec] * 4, out_specs=[spec] * 4,
        out_shape=[jax.ShapeDtypeStruct(w.shape, F32)] * 4,
        compiler_params=_params(dimension_semantics=("arbitrary",)),
    )(w, g, m, v)


def _small_step(vec_packs, mat_packs, mix_g_blocks, w_pack, m_pack, v_pack, conv_wmv, rconv_wmv):
    vec_rows = vec_packs.shape[1]
    rows, cols = vec_rows + mat_packs.shape[1], vec_packs.shape[2]
    cshard = conv_wmv.shape[2]
    rshard = rconv_wmv.shape[2]
    mix_row = PK_MIX_G * TILE_ROWS

    def body(vec_ref, mat_ref, blk_ref, w_ref, m_ref, v_ref, cw_ref, rw_ref, g_ref, d_ref, nm_ref, nv_ref, co_ref, ro_ref):
        total = vec_ref[0]
        mats = mat_ref[0].astype(F32)
        late = blk_ref[0]
        for k in range(1, N_DEVICES):
            total = total + vec_ref[k]
            mats = mats + mat_ref[k].astype(F32)
            late = late + blk_ref[k]
        g_ref[0:vec_rows, :] = total
        g_ref[vec_rows:, :] = mats
        g_ref[mix_row:mix_row + TILE_ROWS, :] = late
        g = g_ref[...]
        d_ref[...], nm_ref[...], nv_ref[...] = _adamw(w_ref[...], g, m_ref[...], v_ref[...])

        x, y, _ = _position()
        j = 2 * x + y
        cblk = total[PK_CONV_W * 8:PK_CONV_W * 8 + 8, :]
        rblk = total[PK_RCONV_W * 8:PK_RCONV_W * 8 + 8, :]
        cg = cblk[:, 0:cshard]
        rg = rblk[:, 0:rshard]
        for k in range(1, N_CHIPS):
            cg = jnp.where(j == k, cblk[:, k * cshard:(k + 1) * cshard], cg)
            rg = jnp.where(j == k, rblk[:, k * rshard:(k + 1) * rshard], rg)
        co_ref[0] = cg
        co_ref[1], co_ref[2], co_ref[3] = _adamw(cw_ref[0], cg, cw_ref[1], cw_ref[2])
        ro_ref[0] = rg
        ro_ref[1], ro_ref[2], ro_ref[3] = _adamw(rw_ref[0], rg, rw_ref[1], rw_ref[2])

    pack = [jax.ShapeDtypeStruct((rows, cols), F32)] * 4
    return pl.pallas_call(
        body, name="small_grads_step", in_specs=[VMEM] * 8, out_specs=[VMEM] * 6,
        out_shape=pack + [jax.ShapeDtypeStruct((4, TILE_ROWS, cshard), F32), jax.ShapeDtypeStruct((4, TILE_ROWS, rshard), F32)],
        compiler_params=_params(),
    )(vec_packs, mat_packs, mix_g_blocks, w_pack, m_pack, v_pack, conv_wmv, rconv_wmv)


def _blk(a):
    a = a.reshape(-1, a.shape[-1])
    return jnp.pad(a, ((0, TILE_ROWS - a.shape[0]), (0, D_MODEL - a.shape[1])))


def _zero_blk():
    return jnp.zeros((TILE_ROWS, D_MODEL), F32)


def _pack_params(p, pre):
    get = lambda n: p[pre + n]
    return jnp.concatenate([
        _blk(get("g_norm_rnn")), _blk(get("rnn_conv_b")), _blk(get("b_a")), _blk(get("b_x")), _blk(get("lru_lambda")),
        _zero_blk(), _zero_blk(), _blk(get("g_norm_conv")), _blk(get("final_norm_g").reshape(1, -1)), _blk(get("norm_mlp_g")),
        _zero_blk(), _blk(get("norm_mix_g")), get("w_a").reshape(64, D_MODEL), get("w_x").reshape(64, D_MODEL)], axis=0)


def _to_block_diag(w):
    w4 = w.reshape(N_BD, 4, 64, 64)
    eye = jnp.eye(4, dtype=w.dtype)
    return (w4[:, :, :, None, :] * eye[None, :, None, :, None]).reshape(N_BD, BD, BD)


def _from_block_diag(d):
    d5 = d.reshape(N_BD, 4, 64, 4, 64)
    return jnp.stack([d5[:, q, :, q, :] for q in range(4)], axis=1).reshape(64, D_MODEL)


def _pad_rows(a):
    return jnp.pad(a, ((0, TILE_ROWS - a.shape[0]), (0, 0)))


_NAMES = ['norm_mix_g', 'w_in', 'conv_w', 'rnn_conv_w', 'rnn_conv_b', 'w_a', 'b_a', 'w_x', 'b_x', 'lru_lambda',
          'g_norm_conv', 'g_norm_rnn', 'w_out', 'norm_mlp_g', 'w_mlp_in', 'w_mlp_out', 'final_norm_g']


def kernel(x, norm_mix_g, w_in, conv_w, rnn_conv_w, rnn_conv_b, w_a, b_a, w_x, b_x, lru_lambda, g_norm_conv, g_norm_rnn, w_out, norm_mlp_g, w_mlp_in, w_mlp_out, final_norm_g, loss_target, m_norm_mix_g, m_w_in, m_conv_w, m_rnn_conv_w, m_rnn_conv_b, m_w_a, m_b_a, m_w_x, m_b_x, m_lru_lambda, m_g_norm_conv, m_g_norm_rnn, m_w_out, m_norm_mlp_g, m_w_mlp_in, m_w_mlp_out, m_final_norm_g, v_norm_mix_g, v_w_in, v_conv_w, v_rnn_conv_w, v_rnn_conv_b, v_w_a, v_b_a, v_w_x, v_b_x, v_lru_lambda, v_g_norm_conv, v_g_norm_rnn, v_w_out, v_norm_mlp_g, v_w_mlp_in, v_w_mlp_out, v_final_norm_g):
    args = dict(locals())
    p = {}
    for n in _NAMES:
        for pre in ("", "m_", "v_"):
            a = args[pre + n]
            p[pre + n] = a[0] if a.ndim >= 3 else a
    xs = x[0]
    target = loss_target[0]
    core_chip = jnp.stack([lax.axis_index("c"), 2 * lax.axis_index("x") + lax.axis_index("y")]).astype(jnp.int32)
    cshard = p["conv_w"].shape[1]
    rshard = p["rnn_conv_w"].shape[1]

    small = jnp.concatenate([_pad_rows(p["conv_w"]), _pad_rows(p["rnn_conv_w"])], axis=1)
    w_in_g, w_out_g, w1_g, w2_g, small_g = _gather_first(p["w_in"], p["w_out"], p["w_mlp_in"], p["w_mlp_out"], small)
    conv_full = small_g[:, :3, :cshard].transpose(1, 0, 2).reshape(3, CONV_W)
    rconv_full = small_g[:, :4, cshard:].transpose(1, 0, 2).reshape(4, LRU_W)
    wa_bd = _to_block_diag(p["w_a"]).astype(BF16)
    wx_bd = _to_block_diag(p["w_x"]).astype(BF16)
    gf = p["final_norm_g"].reshape(1, -1)
    lru = (wa_bd, p["b_a"], wx_bd, p["b_x"], p["lru_lambda"], p["g_norm_conv"], p["g_norm_rnn"])

    (u, h1b, xr, hs, c3, yb, gates), (w_out_g, w1_g, w2_g) = _fwd_mix(
        xs, p["norm_mix_g"], w_in_g, conv_full, rconv_full, p["rnn_conv_b"], *lru, (w_out_g, w1_g, w2_g))
    zb, dpb, h2b, dx3b, dx2, dx2b, dy, st_mlp = _mlp_fwd_bwd(
        xs, yb, w_out_g.reshape(-1, D_MODEL), w1_g, w2_g.reshape(-1, D_MODEL), p["norm_mlp_g"], gf, target)

    part_out = _wgrad(yb, dx2b, "out", core_chip)
    *part_1, arrived_out = _wgrad(h2b, dpb, "mlp_in", core_chip, parts=(part_out[1],))
    part_2 = _wgrad(zb, dx3b, "mlp_out", core_chip)
    (dub, st_mix, dwa_bd, dwx_bd), (arrived_1, arrived_2) = _mix_bwd(
        dy, u, xr, hs, c3, gates, conv_full, rconv_full, wa_bd, wx_bd, p["lru_lambda"], p["g_norm_conv"], p["g_norm_rnn"],
        (part_1[1], part_2[1]))
    arrived_mlp = (arrived_out, arrived_1, arrived_2)
    vec_pack = jnp.concatenate([st_mix, st_mlp, _zero_blk()], axis=0)
    mat_pack = jnp.concatenate([_from_block_diag(dwa_bd), _from_block_diag(dwx_bd)], axis=0).astype(BF16)
    *part_in, vec_packs, mat_packs = _wgrad(h1b, dub, "in", core_chip, packs=(vec_pack, mat_pack))
    early = (("w_out", "out", part_out, arrived_mlp[0]), ("w_mlp_in", "mlp_in", part_1, arrived_mlp[1]),
             ("w_mlp_out", "mlp_out", part_2, arrived_mlp[2]))
    (grad_x, st_in), arrived_in, joined = _in_bwd(
        dub, w_in_g, xs, dx2, p["norm_mix_g"], (part_in[1],),
        [(tag, p[n].shape, part[0], arrived) for n, tag, part, arrived in early], core_chip)
    g_in, mix_g_blocks = _join("in", p["w_in"].shape, part_in[0], arrived_in[0], core_chip, st_in)
    big = {}
    for n, tag, g in [(n, tag, g) for (n, tag, _, _), g in zip(early, joined)] + [("w_in", "in", g_in)]:
        big[n] = _adamw_big(p[n], g, p["m_" + n], p["v_" + n], "adamw_" + tag)

    conv_wmv = jnp.stack([_pad_rows(p[pre + "conv_w"]) for pre in ("", "m_", "v_")])
    rconv_wmv = jnp.stack([_pad_rows(p[pre + "rnn_conv_w"]) for pre in ("", "m_", "v_")])
    g_pack, d_pack, m_pack, v_pack, conv_out, rconv_out = _small_step(
        vec_packs, mat_packs, mix_g_blocks, _pack_params(p, ""), _pack_params(p, "m_"), _pack_params(p, "v_"), conv_wmv, rconv_wmv)

    def unpack(pk, kind):
        def vec(b, width=D_MODEL):
            return pk[b * 8:b * 8 + 1, :width]
        return {
            "norm_mix_g": vec(PK_MIX_G), "rnn_conv_b": vec(PK_RCONV_B), "b_a": vec(PK_B_A), "b_x": vec(PK_B_X),
            "lru_lambda": vec(PK_LAMBDA), "g_norm_conv": vec(PK_G_NORM_CONV, CONV_W), "g_norm_rnn": vec(PK_G_NORM_RNN),
            "norm_mlp_g": vec(PK_MLP_G), "final_norm_g": vec(PK_FINAL_G).reshape(-1),
            "w_a": pk[PK_W_A * 8:PK_W_A * 8 + 64].reshape(1, 16, 64, 64), "w_x": pk[PK_W_X * 8:PK_W_X * 8 + 64].reshape(1, 16, 64, 64),
            "conv_w": conv_out[kind, :3][None], "rnn_conv_w": rconv_out[kind, :4][None],
            "w_in": big["w_in"][kind][None], "w_out": big["w_out"][kind][None],
            "w_mlp_in": big["w_mlp_in"][kind][None], "w_mlp_out": big["w_mlp_out"][kind][None],
        }

    outs = [unpack(pk, kind) for kind, pk in enumerate((g_pack, d_pack, m_pack, v_pack))]
    for o in outs:
        for n in ("norm_mix_g", "rnn_conv_b", "b_a", "b_x", "lru_lambda", "g_norm_conv", "g_norm_rnn", "norm_mlp_g"):
            o[n] = o[n].reshape(1, -1)
    loss = g_pack[PK_LOSS * 8, 0]
    return (loss, grad_x[None], *[o[n] for o in outs for n in _NAMES])
```

```python
import functools
import math

import jax
import jax.numpy as jnp
from jax import lax
from jax.experimental import pallas as pl
from jax.experimental.pallas import tpu as pltpu

F32 = jnp.float32
BF16 = jnp.bfloat16
MESH = pl.DeviceIdType.MESH
ANY = pl.BlockSpec(memory_space=pl.ANY)
VMEM = pl.BlockSpec(memory_space=pltpu.VMEM)

EPS = 1e-6
LRU_C = 8.0
D_MODEL = 1024
CONV_W = 512
LRU_W = 1024
IN_COLS = 3 * CONV_W + 2 * LRU_W
IN_SHARD = IN_COLS // 4
N_CHIPS = 4
N_DEVICES = 8
BD = 256
N_BD = LRU_W // BD

ADAM_LR = 0.001
ADAM_B1 = 0.9
ADAM_B2 = 0.999
ADAM_EPS = 1e-08
ADAM_WD = 0.01
ADAM_STEP = 10
ADAM_BC1 = 1.0 - ADAM_B1 ** ADAM_STEP
ADAM_BC2 = 1.0 - ADAM_B2 ** ADAM_STEP

TILE_ROWS = 8
TOKEN_TILE = 256
MATMUL_TOKEN_TILE = 512
VMEM_LIMIT = 56 * 1024 * 1024

PK_G_NORM_RNN, PK_RCONV_B, PK_B_A, PK_B_X, PK_LAMBDA, PK_RCONV_W, PK_CONV_W, PK_G_NORM_CONV = range(8)
PK_FINAL_G, PK_MLP_G, PK_LOSS, PK_MIX_G = 8, 9, 10, 11
PK_W_A = 12
PK_W_X = 20
PK_BLOCKS = 28
PK_ROWS = PK_BLOCKS * TILE_ROWS


def _params(**kw):
    return pltpu.CompilerParams(vmem_limit_bytes=VMEM_LIMIT, **kw)


def _position():
    x, y, c = lax.axis_index("x"), lax.axis_index("y"), lax.axis_index("c")
    return x, y, c


def _sigmoid(v):
    return 1.0 / (1.0 + jnp.exp(-v))


def _one_minus_square(log_a, a):
    v = 2.0 * log_a
    series = -v * (1.0 + v * (0.5 + v * (1.0 / 6.0)))
    return jnp.where(v > -0.01, series, 1.0 - a * a)


_GELU_C = math.sqrt(2.0 / math.pi)
_GELU_K = 0.044715


def _gelu_and_grad(g):
    th = jnp.tanh(_GELU_C * (g + _GELU_K * g * g * g))
    gelu = 0.5 * g * (1.0 + th)
    dgelu = 0.5 * (1.0 + th) + 0.5 * g * (1.0 - th * th) * (_GELU_C * (1.0 + 3.0 * _GELU_K * g * g))
    return gelu, dgelu


def _rows(shape):
    return lax.broadcasted_iota(jnp.int32, shape, 0)


def _shift_down(v, k, prev8):
    rolled = pltpu.roll(v, k, 0)
    halo = pltpu.roll(prev8, k, 0)
    head = jnp.where(_rows(halo.shape) < k, halo, rolled[:TILE_ROWS])
    return jnp.concatenate([head, rolled[TILE_ROWS:]], axis=0)


def _shift_up(v, k, next8):
    n = v.shape[0]
    rolled = pltpu.roll(v, n - k, 0)
    halo = pltpu.roll(next8, TILE_ROWS - k, 0)
    tail = jnp.where(_rows(halo.shape) >= TILE_ROWS - k, halo, rolled[n - TILE_ROWS:])
    return jnp.concatenate([rolled[: n - TILE_ROWS], tail], axis=0)


def _scan_rows(a, b, carry, reverse=False):
    n, w = a.shape
    groups = n // TILE_ROWS
    a3 = a.reshape(groups, TILE_ROWS, w)
    b3 = b.reshape(groups, TILE_ROWS, w)
    sub = lax.broadcasted_iota(jnp.int32, a3.shape, 1)
    s = 1
    while s < TILE_ROWS:
        shift = TILE_ROWS - s if reverse else s
        keep = (sub < TILE_ROWS - s) if reverse else (sub >= s)
        b3 = b3 + jnp.where(keep, a3 * pltpu.roll(b3, shift, 1), 0.0)
        a3 = a3 * jnp.where(keep, pltpu.roll(a3, shift, 1), 1.0)
        s *= 2
    out = [None] * groups
    edge = 0 if reverse else TILE_ROWS - 1
    for g in (range(groups - 1, -1, -1) if reverse else range(groups)):
        out[g] = b3[g] + a3[g] * carry
        carry = out[g][edge:edge + 1]
    return jnp.concatenate(out, axis=0)


def _softplus_neg(lam):
    e = jnp.exp(-jnp.abs(lam))
    log1p_e = jnp.where(e < 1e-2, e * (1.0 - e * (0.5 - e * (1.0 / 3.0 - e * 0.25))), jnp.log(1.0 + e))
    sp = jnp.maximum(-lam, 0.0) + log1p_e
    dsp = -_sigmoid(-lam)
    return sp, dsp


def _block_diag_dot(vb, w_ref):
    return jnp.concatenate(
        [jnp.dot(vb[:, j * BD:(j + 1) * BD], w_ref[j], preferred_element_type=F32) for j in range(N_BD)], axis=1)


def _block_diag_dot_t(vb, w_ref):
    return jnp.concatenate(
        [lax.dot_general(vb[:, j * BD:(j + 1) * BD], w_ref[j], (((1,), (1,)), ((), ())), preferred_element_type=F32)
         for j in range(N_BD)], axis=1)


def _dot_nt(a, b):
    return lax.dot_general(a, b, (((1,), (1,)), ((), ())), preferred_element_type=F32)


def _dot_tn(a, b):
    return lax.dot_general(a, b, (((0,), (0,)), ((), ())), preferred_element_type=F32)


def _lru_gates(xr, wa_ref, ba, wx_ref, bx, sp):
    xrb = xr.astype(BF16)
    r = _sigmoid(_block_diag_dot(xrb, wa_ref) + ba)
    ig = _sigmoid(_block_diag_dot(xrb, wx_ref) + bx)
    log_a = (-LRU_C) * r * sp
    a = jnp.exp(log_a)
    mult = jnp.sqrt(_one_minus_square(log_a, a))
    return r, ig, a, mult


def _colsum(v):
    return jnp.sum(v, axis=0, keepdims=True)


N_FWD_OUT = 7


def _fwd_mix(x, g1, w_in_g, conv_w, rconv_w, rconv_b, wa_bd, b_a, wx_bd, b_x, lam, g_nc, g_nr, later):
    t, d = x.shape
    tm = TOKEN_TILE
    nt = t // tm
    nl = len(later)
    assert nl == 3
    pass_on_at = [nt * f // 16 for f in (3, 5, 9)]
    neighbours_at = [nt * f // 16 for f in (10, 11, 12)]
    diagonal_at = [nt * f // 16 for f in (13, 14, 14)]

    def body(x_ref, g1_ref, win_ref, cw_ref, rw_ref, rb_ref, wa_ref, ba_ref, wx_ref, bx_ref, lam_ref, gnc_ref, gnr_ref,
             *rest):
        later_in, outs, rest = rest[:nl], rest[nl:nl + N_FWD_OUT], rest[nl + N_FWD_OUT:]
        u_ref, h1_ref, xr_ref, hs_ref, c3_ref, y_ref, gates_ref = outs
        later_out, (cv_prev, xin_prev, h_prev, send_sems, recv_sems) = rest[:nl], rest[nl:]
        del later_in
        step = pl.program_id(0)
        plan = _ShardGather(later_out, send_sems, recv_sems)

        @pl.when(step == 0)
        def _():
            cv_prev[...] = jnp.zeros_like(cv_prev)
            xin_prev[...] = jnp.zeros_like(xin_prev)
            h_prev[...] = jnp.zeros_like(h_prev)
            for w in range(nl):
                plan.start_direct(w)

        for w in range(nl):
            @pl.when(step == pass_on_at[w])
            def _(w=w):
                plan.start_pass_on(w)

            @pl.when(step == neighbours_at[w])
            def _(w=w):
                plan.start_hand_over(w, diagonal=False)

            @pl.when(step == diagonal_at[w])
            def _(w=w):
                plan.start_hand_over(w, diagonal=True)

        xv = x_ref[...]
        rstd = lax.rsqrt(jnp.mean(xv * xv, axis=-1, keepdims=True) + EPS)
        h1b = ((xv * rstd) * g1_ref[...]).astype(BF16)
        h1_ref[...] = h1b
        for j in range(N_CHIPS):
            u_ref[:, j * IN_SHARD:(j + 1) * IN_SHARD] = jnp.dot(h1b, win_ref[j], preferred_element_type=F32)
        gate_b = u_ref[:, 0:CONV_W]
        cv = u_ref[:, CONV_W:2 * CONV_W] * u_ref[:, 2 * CONV_W:3 * CONV_W]
        x_r = u_ref[:, 3 * CONV_W:3 * CONV_W + LRU_W]
        g = u_ref[:, 3 * CONV_W + LRU_W:]

        cw = cw_ref[...]
        cvp = cv_prev[...]
        conv3 = cw[0:1] * _shift_down(cv, 2, cvp) + cw[1:2] * _shift_down(cv, 1, cvp) + cw[2:3] * cv
        cv_prev[...] = cv[tm - TILE_ROWS:]
        c3_ref[...] = conv3
        y_conv = gate_b * conv3

        rw = rw_ref[...]
        xp = xin_prev[...]
        xr = (rw[0:1] * _shift_down(x_r, 3, xp) + rw[1:2] * _shift_down(x_r, 2, xp)
              + rw[2:3] * _shift_down(x_r, 1, xp) + rw[3:4] * x_r) + rb_ref[...]
        xin_prev[...] = x_r[tm - TILE_ROWS:]
        xr_ref[...] = xr
        sp, _ = _softplus_neg(lam_ref[...])
        r, ig, a, mult = _lru_gates(xr, wa_ref, ba_ref[...], wx_ref, bx_ref[...], sp)
        for n, gate in enumerate((r, ig, a, mult)):
            gates_ref[:, n * LRU_W:(n + 1) * LRU_W] = gate
        h = _scan_rows(a, mult * (ig * xr), h_prev[...])
        h_prev[...] = h[tm - 1:tm]
        hs_ref[...] = h
        gelu, _ = _gelu_and_grad(g)
        y_rnn = h * gelu

        na = y_conv * lax.rsqrt(jnp.mean(y_conv * y_conv, axis=-1, keepdims=True) + EPS) * gnc_ref[...]
        nb = y_rnn * lax.rsqrt(jnp.mean(y_rnn * y_rnn, axis=-1, keepdims=True) + EPS) * gnr_ref[...]
        y_ref[:, :CONV_W] = na.astype(BF16)
        y_ref[:, CONV_W:] = nb.astype(BF16)

        @pl.when(step == nt - 1)
        def _():
            for w in range(nl):
                plan.finish(w)

    def full(a):
        nd = a.ndim
        return pl.BlockSpec(a.shape, lambda i: (0,) * nd)

    def tok(cols):
        return pl.BlockSpec((tm, cols), lambda i: (i, 0))

    def act(cols, dtype=F32):
        return jax.ShapeDtypeStruct((t, cols), dtype)

    smalls = (g1, w_in_g, conv_w, rconv_w, rconv_b, wa_bd, b_a, wx_bd, b_x, lam, g_nc, g_nr)
    n_in = 1 + len(smalls)
    outs = pl.pallas_call(
        body, name="fwd_mix", grid=(nt,),
        in_specs=[tok(d)] + [full(a) for a in smalls] + [ANY] * nl,
        out_specs=[tok(IN_COLS), tok(d), tok(LRU_W), tok(LRU_W), tok(CONV_W), tok(CONV_W + LRU_W)]
        + [tok(4 * LRU_W)] + [ANY] * nl,
        out_shape=[act(IN_COLS), act(d, BF16), act(LRU_W), act(LRU_W), act(CONV_W), act(CONV_W + LRU_W, BF16)]
        + [act(4 * LRU_W)] + [jax.ShapeDtypeStruct(a.shape, a.dtype) for a in later],
        input_output_aliases={n_in + w: N_FWD_OUT + w for w in range(nl)},
        scratch_shapes=[pltpu.VMEM((TILE_ROWS, CONV_W), F32), pltpu.VMEM((TILE_ROWS, LRU_W), F32),
                        pltpu.VMEM((1, LRU_W), F32), pltpu.SemaphoreType.DMA((nl, _ShardGather.PAIRS)),
                        pltpu.SemaphoreType.DMA((nl, _ShardGather.PAIRS))],
        compiler_params=_params(dimension_semantics=("arbitrary",)),
    )(x, *smalls, *later)
    return outs[:N_FWD_OUT], outs[N_FWD_OUT:]


def _mlp_fwd_bwd(x, yb, w_out_g, w1_g, w2_g, g2, gf, target):
    t, d = x.shape
    tm = TOKEN_TILE
    ff = w2_g.shape[0]
    mix = w_out_g.shape[0]
    ffs = ff // N_CHIPS

    def body(x_ref, y_ref, g2_ref, gf_ref, tgt_ref, wout_hbm, w1_hbm, w2_hbm,
             z_ref, dp_ref, h2_ref, dx3b_ref, dx2_ref, dx2b_ref, dy_ref, st_ref, wout, w1, w2, p_ref):
        @pl.when(pl.program_id(0) == 0)
        def _():
            pltpu.sync_copy(wout_hbm, wout)
            pltpu.sync_copy(w1_hbm, w1)
            pltpu.sync_copy(w2_hbm, w2)
            st_ref[...] = jnp.zeros_like(st_ref)

        x2 = x_ref[...] + jnp.dot(y_ref[...], wout[...], preferred_element_type=F32)
        r2 = lax.rsqrt(jnp.mean(x2 * x2, axis=-1, keepdims=True) + EPS)
        xh2 = x2 * r2
        g2v = g2_ref[...]
        h2b = (xh2 * g2v).astype(BF16)
        h2_ref[...] = h2b
        for j in range(N_CHIPS):
            p_ref[:, j * ffs:(j + 1) * ffs] = jnp.dot(h2b, w1[j], preferred_element_type=F32)
        rp = jnp.maximum(p_ref[...], 0.0)
        zb = (rp * rp).astype(BF16)
        z_ref[...] = zb
        x3 = x2 + jnp.dot(zb, w2[...], preferred_element_type=F32)
        r3 = lax.rsqrt(jnp.mean(x3 * x3, axis=-1, keepdims=True) + EPS)
        xh3 = x3 * r3
        gfv = gf_ref[...]
        err = xh3 * gfv - tgt_ref[...]
        loss = (0.5 / d) * jnp.sum(err * err)
        dout = err * (1.0 / d)
        st_ref[PK_FINAL_G * 8 - 64:PK_FINAL_G * 8 - 63, :] += _colsum(dout * xh3)
        st_ref[PK_LOSS * 8 - 64:PK_LOSS * 8 - 63, :] += jnp.zeros((1, d), F32) + loss
        dxh3 = dout * gfv
        dx3 = r3 * (dxh3 - xh3 * jnp.mean(dxh3 * xh3, axis=-1, keepdims=True))
        dx3b = dx3.astype(BF16)
        dx3b_ref[...] = dx3b
        dpb = (_dot_nt(dx3b, w2[...]) * (2.0 * rp)).astype(BF16)
        dp_ref[...] = dpb
        dh2 = _dot_nt(dpb[:, 0:ffs], w1[0])
        for j in range(1, N_CHIPS):
            dh2 = dh2 + _dot_nt(dpb[:, j * ffs:(j + 1) * ffs], w1[j])
        st_ref[PK_MLP_G * 8 - 64:PK_MLP_G * 8 - 63, :] += _colsum(dh2 * xh2)
        dxh2 = dh2 * g2v
        dx2 = dx3 + r2 * (dxh2 - xh2 * jnp.mean(dxh2 * xh2, axis=-1, keepdims=True))
        dx2_ref[...] = dx2
        dx2b = dx2.astype(BF16)
        dx2b_ref[...] = dx2b
        dy_ref[...] = _dot_nt(dx2b, wout[...])

    def tok(cols):
        return pl.BlockSpec((tm, cols), lambda i: (i, 0))

    def row(cols):
        return pl.BlockSpec((1, cols), lambda i: (0, 0))

    return pl.pallas_call(
        body, name="mlp_fwd_bwd", grid=(t // tm,),
        in_specs=[tok(d), tok(mix), row(d), row(d), tok(d), ANY, ANY, ANY],
        out_specs=[tok(ff), tok(ff), tok(d), tok(d), tok(d), tok(d), tok(mix),
                   pl.BlockSpec((3 * TILE_ROWS, d), lambda i: (0, 0))],
        out_shape=[jax.ShapeDtypeStruct((t, ff), BF16), jax.ShapeDtypeStruct((t, ff), BF16),
                   jax.ShapeDtypeStruct((t, d), BF16), jax.ShapeDtypeStruct((t, d), BF16),
                   jax.ShapeDtypeStruct((t, d), F32), jax.ShapeDtypeStruct((t, d), BF16),
                   jax.ShapeDtypeStruct((t, mix), F32), jax.ShapeDtypeStruct((3 * TILE_ROWS, d), F32)],
        scratch_shapes=[pltpu.VMEM(w_out_g.shape, BF16), pltpu.VMEM(w1_g.shape, BF16), pltpu.VMEM(w2_g.shape, BF16),
                        pltpu.VMEM((tm, ff), F32)],
        compiler_params=_params(dimension_semantics=("arbitrary",)),
    )(x, yb, g2, gf, target, w_out_g, w1_g, w2_g)


def _mix_bwd(dy, u, xr_all, hs_all, c3_all, gates, conv_w, rconv_w, wa_bd, wx_bd, lam, g_nc, g_nr, parts):
    t = dy.shape[0]
    tm = TOKEN_TILE
    nt = t // tm
    hb = tm // TILE_ROWS
    npart = len(parts)

    def body(dy_ref, u_ref, uh_ref, xr_ref, hs_ref, hh_ref, c3_ref, gates_ref,
             cw_ref, rw_ref, wa_ref, wx_ref, lam_ref, gnc_ref, gnr_ref, *rest):
        part_refs, (du_ref, st_ref, dwa_ref, dwx_ref), rest = rest[:npart], rest[npart:npart + 4], rest[npart + 4:]
        arrived_refs, (dc_next, a_next, gs_next, dxr_next, send_sems, recv_sems) = rest[:npart], rest[npart:]
        exchange = _PartialExchange(part_refs, arrived_refs, send_sems, recv_sems)
        i = pl.program_id(0)

        @pl.when(i == 0)
        def _():
            exchange.start()
            dc_next[...] = jnp.zeros_like(dc_next)
            a_next[...] = jnp.zeros_like(a_next)
            gs_next[...] = jnp.zeros_like(gs_next)
            dxr_next[...] = jnp.zeros_like(dxr_next)
            st_ref[...] = jnp.zeros_like(st_ref)
            dwa_ref[...] = jnp.zeros_like(dwa_ref)
            dwx_ref[...] = jnp.zeros_like(dwx_ref)

        first_tile = i == nt - 1
        gate_b = u_ref[:, 0:CONV_W]
        gate_c = u_ref[:, CONV_W:2 * CONV_W]
        v = u_ref[:, 2 * CONV_W:3 * CONV_W]
        x_r = u_ref[:, 3 * CONV_W:3 * CONV_W + LRU_W]
        g = u_ref[:, 3 * CONV_W + LRU_W:]
        cv = gate_c * v
        cv_prev = jnp.where(first_tile, 0.0, uh_ref[:, CONV_W:2 * CONV_W] * uh_ref[:, 2 * CONV_W:3 * CONV_W])
        xin_prev = jnp.where(first_tile, 0.0, uh_ref[:, 3 * CONV_W:3 * CONV_W + LRU_W])
        hs_prev = jnp.where(first_tile, 0.0, hh_ref[...])

        def acc(block, val, width=LRU_W, row=0):
            r0 = block * TILE_ROWS + row
            st_ref[r0:r0 + 1, 0:width] += val

        conv3 = c3_ref[...]
        y_conv = gate_b * conv3
        ra = lax.rsqrt(jnp.mean(y_conv * y_conv, axis=-1, keepdims=True) + EPS)
        xha = y_conv * ra
        dna = dy_ref[:, :CONV_W]
        acc(PK_G_NORM_CONV, _colsum(dna * xha), CONV_W)
        dxha = dna * gnc_ref[...]
        dy_conv = ra * (dxha - xha * jnp.mean(dxha * xha, axis=-1, keepdims=True))
        du_ref[:, 0:CONV_W] = (dy_conv * conv3).astype(BF16)
        dc = dy_conv * gate_b
        cw = cw_ref[...]
        dcn = dc_next[...]
        dcv = cw[2:3] * dc + cw[1:2] * _shift_up(dc, 1, dcn) + cw[0:1] * _shift_up(dc, 2, dcn)
        dc_next[...] = dc[:TILE_ROWS]
        acc(PK_CONV_W, _colsum(dc * _shift_down(cv, 2, cv_prev)), CONV_W, 0)
        acc(PK_CONV_W, _colsum(dc * _shift_down(cv, 1, cv_prev)), CONV_W, 1)
        acc(PK_CONV_W, _colsum(dc * cv), CONV_W, 2)
        du_ref[:, CONV_W:2 * CONV_W] = (dcv * v).astype(BF16)
        du_ref[:, 2 * CONV_W:3 * CONV_W] = (dcv * gate_c).astype(BF16)

        hs = hs_ref[...]
        gelu, dgelu = _gelu_and_grad(g)
        y_rnn = hs * gelu
        rb = lax.rsqrt(jnp.mean(y_rnn * y_rnn, axis=-1, keepdims=True) + EPS)
        xhb = y_rnn * rb
        dnb = dy_ref[:, CONV_W:]
        acc(PK_G_NORM_RNN, _colsum(dnb * xhb))
        dxhb = dnb * gnr_ref[...]
        dy_rnn = rb * (dxhb - xhb * jnp.mean(dxhb * xhb, axis=-1, keepdims=True))
        du_ref[:, 3 * CONV_W + LRU_W:] = (dy_rnn * hs * dgelu).astype(BF16)
        dh = dy_rnn * gelu

        xr = xr_ref[...]
        xrb = xr.astype(BF16)
        sp, dsp = _softplus_neg(lam_ref[...])
        r, ig, a, mult = [gates_ref[:, n * LRU_W:(n + 1) * LRU_W] for n in range(4)]
        a_up = _shift_up(a, 1, a_next[...])
        a_next[...] = a[:TILE_ROWS]
        gs = _scan_rows(a_up, dh, gs_next[0:1, :], reverse=True)
        gs_next[...] = gs[:TILE_ROWS]
        da = gs * _shift_down(hs, 1, hs_prev)
        gx = gs * xr
        di = gx * mult
        dmult = gx * ig
        dxr = gs * (mult * ig)
        dlog_a = da * a - dmult * ((a * a) / mult)
        acc(PK_LAMBDA, _colsum(dlog_a * r) * ((-LRU_C) * dsp))
        dpa = (dlog_a * ((-LRU_C) * sp)) * (r * (1.0 - r))
        dpx = di * (ig * (1.0 - ig))
        acc(PK_B_A, _colsum(dpa))
        acc(PK_B_X, _colsum(dpx))
        dpab = dpa.astype(BF16)
        dpxb = dpx.astype(BF16)
        dxr = dxr + _block_diag_dot_t(dpab, wa_ref) + _block_diag_dot_t(dpxb, wx_ref)
        for j in range(N_BD):
            cols = slice(j * BD, (j + 1) * BD)
            dwa_ref[j] += _dot_tn(xrb[:, cols], dpab[:, cols])
            dwx_ref[j] += _dot_tn(xrb[:, cols], dpxb[:, cols])

        acc(PK_RCONV_B, _colsum(dxr))
        rw = rw_ref[...]
        dxn = dxr_next[...]
        dx_r = (rw[3:4] * dxr + rw[2:3] * _shift_up(dxr, 1, dxn) + rw[1:2] * _shift_up(dxr, 2, dxn)
                + rw[0:1] * _shift_up(dxr, 3, dxn))
        dxr_next[...] = dxr[:TILE_ROWS]
        for k in range(3):
            acc(PK_RCONV_W, _colsum(dxr * _shift_down(x_r, 3 - k, xin_prev)), LRU_W, k)
        acc(PK_RCONV_W, _colsum(dxr * x_r), LRU_W, 3)
        du_ref[:, 3 * CONV_W:3 * CONV_W + LRU_W] = dx_r.astype(BF16)

        @pl.when(i == nt - 1)
        def _():
            exchange.wait()

    def full(a):
        nd = a.ndim
        return pl.BlockSpec(a.shape, lambda i: (0,) * nd)

    def tok(cols):
        return pl.BlockSpec((tm, cols), lambda i: (nt - 1 - i, 0))

    def halo(cols):
        return pl.BlockSpec((TILE_ROWS, cols), lambda i: (jnp.maximum((nt - 1 - i) * hb - 1, 0), 0))

    smalls = (conv_w, rconv_w, wa_bd, wx_bd, lam, g_nc, g_nr)
    outs = pl.pallas_call(
        body, name="mix_bwd", grid=(nt,),
        in_specs=[tok(CONV_W + LRU_W), tok(IN_COLS), halo(IN_COLS), tok(LRU_W), tok(LRU_W), halo(LRU_W), tok(CONV_W)]
        + [tok(4 * LRU_W)] + [full(a) for a in smalls] + [ANY] * npart,
        out_specs=[tok(IN_COLS), pl.BlockSpec((8 * TILE_ROWS, LRU_W), lambda i: (0, 0)),
                   pl.BlockSpec((N_BD, BD, BD), lambda i: (0, 0, 0)), pl.BlockSpec((N_BD, BD, BD), lambda i: (0, 0, 0))]
        + [ANY] * npart,
        out_shape=[jax.ShapeDtypeStruct((t, IN_COLS), BF16), jax.ShapeDtypeStruct((8 * TILE_ROWS, LRU_W), F32),
                   jax.ShapeDtypeStruct((N_BD, BD, BD), F32), jax.ShapeDtypeStruct((N_BD, BD, BD), F32)]
        + [jax.ShapeDtypeStruct(a.shape, a.dtype) for a in parts],
        scratch_shapes=[pltpu.VMEM((TILE_ROWS, CONV_W), F32), pltpu.VMEM((TILE_ROWS, LRU_W), F32),
                        pltpu.VMEM((TILE_ROWS, LRU_W), F32), pltpu.VMEM((TILE_ROWS, LRU_W), F32),
                        pltpu.SemaphoreType.DMA((npart, 3)), pltpu.SemaphoreType.DMA((npart, 3))],
        compiler_params=_params(dimension_semantics=("arbitrary",)),
    )(dy, u, u, xr_all, hs_all, hs_all, c3_all, gates, *smalls, *parts)
    return outs[:4], outs[4:]


def _in_bwd(dub, w_in_g, x, dx2, g1, parts, joins, core_chip):
    t, d = x.shape
    tm = min(t, MATMUL_TOKEN_TILE)
    nt = t // tm
    npart = len(parts)
    nj = len(joins)
    geometry = []
    for tag, shape, _, _ in joins:
        pr, pc = WGRAD_GEOMETRY[tag][:2]
        every = 1 if pr % (nt * 16) == 0 else 2
        geometry.append((pr, pc, pr * every // nt, every, shape[1] == pc))

    def body(cc_ref, du_ref, win_ref, x_ref, dx2_ref, g1_ref, *rest):
        sums, rest = [rest[4 * w:4 * w + 4] for w in range(nj)], rest[4 * nj:]
        part_refs, (gx_ref, st_ref), rest = rest[:npart], rest[npart:npart + 2], rest[npart + 2:]
        arrived_refs, joined, rest = rest[:npart], rest[npart:npart + nj], rest[npart + nj:]
        stages, (send_sems, recv_sems, j_local, j_send, j_recv) = rest[:nj], rest[nj:]
        exchange = _PartialExchange(part_refs, arrived_refs, send_sems, recv_sems)
        i = pl.program_id(0)
        c = cc_ref[0]

        def window(w, core, row0, rows):
            pr, pc, _, _, by_rows = geometry[w]
            if by_rows:
                return joined[w].at[pl.ds(core * pr + row0, rows), :]
            return joined[w].at[pl.ds(row0, rows), pl.ds(core * pc, pc)]

        def to_sibling(w, src, core, row0, rows):
            return pltpu.make_async_remote_copy(src_ref=src, dst_ref=window(w, core, row0, rows), send_sem=j_send.at[w],
                                                recv_sem=j_recv.at[w], device_id=_sibling(), device_id_type=MESH)

        @pl.when(i == 0)
        def _():
            exchange.start()
            st_ref[...] = jnp.zeros_like(st_ref)

        for w in range(nj):
            pr, pc, rb, every, _ = geometry[w]

            @pl.when(i % every == 0)
            def _(w=w, rb=rb, every=every):
                p_ref, r1_ref, r2_ref, r3_ref = sums[w]
                row0 = pl.multiple_of((i // every) * rb, rb)
                rows = stages[w].at[pl.ds(row0, rb), :]
                rows[...] = ((p_ref[0] + r1_ref[0].astype(F32)) + r2_ref[0].astype(F32)) + r3_ref[0].astype(F32)
                pltpu.make_async_copy(rows, window(w, c, row0, rb), j_local.at[w]).start()
                to_sibling(w, rows, c, row0, rb).start()

        dh1 = _dot_nt(du_ref[:, 0:IN_SHARD], win_ref[0])
        for j in range(1, N_CHIPS):
            dh1 = dh1 + _dot_nt(du_ref[:, j * IN_SHARD:(j + 1) * IN_SHARD], win_ref[j])
        xv = x_ref[...]
        rstd = lax.rsqrt(jnp.mean(xv * xv, axis=-1, keepdims=True) + EPS)
        xh = xv * rstd
        st_ref[0:1, :] += _colsum(dh1 * xh)
        dxh = dh1 * g1_ref[...]
        gx_ref[...] = dx2_ref[...] + rstd * (dxh - xh * jnp.mean(dxh * xh, axis=-1, keepdims=True))

        @pl.when(i == nt - 1)
        def _():
            exchange.wait()
            for w in range(nj):
                pr = geometry[w][0]
                pltpu.make_async_copy(stages[w], window(w, c, 0, pr), j_local.at[w]).wait()
                to_sibling(w, stages[w], 1 - c, 0, pr).wait()

    def tok(cols):
        return pl.BlockSpec((tm, cols), lambda i, cc: (i, 0))

    def partial(w, off):
        pr, pc, rb, every, _ = geometry[w]
        return pl.BlockSpec((1, rb, pc), lambda i, cc: ((cc[1] + off) % N_CHIPS, i // every, 0))

    sum_specs, sum_operands = [], []
    for w, (_, _, own, arrived) in enumerate(joins):
        sum_specs += [partial(w, off) for off in range(N_CHIPS)]
        sum_operands += [own, arrived, arrived, arrived]
    dma = pltpu.SemaphoreType.DMA
    outs = pl.pallas_call(
        body, name="in_bwd",
        grid_spec=pltpu.PrefetchScalarGridSpec(
            num_scalar_prefetch=1, grid=(nt,),
            in_specs=[tok(IN_COLS), pl.BlockSpec(w_in_g.shape, lambda i, cc: (0, 0, 0)), tok(d), tok(d),
                      pl.BlockSpec((1, d), lambda i, cc: (0, 0))] + sum_specs + [ANY] * npart,
            out_specs=[tok(d), pl.BlockSpec((TILE_ROWS, d), lambda i, cc: (0, 0))] + [ANY] * (npart + nj),
            scratch_shapes=[pltpu.VMEM((g[0], g[1]), F32) for g in geometry]
            + [dma((npart, 3)), dma((npart, 3)), dma((nj,)), dma((nj,)), dma((nj,))]),
        out_shape=[jax.ShapeDtypeStruct((t, d), F32), jax.ShapeDtypeStruct((TILE_ROWS, d), F32)]
        + [jax.ShapeDtypeStruct(a.shape, a.dtype) for a in parts]
        + [jax.ShapeDtypeStruct(shape, F32) for _, shape, _, _ in joins],
        compiler_params=_params(dimension_semantics=("arbitrary",)),
    )(core_chip, dub, w_in_g, x, dx2, g1, *sum_operands, *parts)
    return outs[:2], outs[2:2 + npart], outs[2 + npart:]


WGRAD_GEOMETRY = {
    "in": (512, IN_SHARD, lambda s, h: h, lambda s, h: s),
    "mlp_in": (512, D_MODEL, lambda s, h: h, lambda s, h: s),
    "mlp_out": (512, D_MODEL, lambda s, h: 2 * s + h, lambda s, h: 0),
    "out": (384, 512, lambda s, h: s, lambda s, h: h),
}
K_CHUNK = 512


def _sibling():
    x, y, c = _position()
    return (x, y, 1 - c)


def _wgrad(a, b, tag, core_chip, packs=(), parts=()):
    t = a.shape[0]
    pr, pc, a_blk, b_blk = WGRAD_GEOMETRY[tag]
    nk = t // K_CHUNK
    mine = N_CHIPS
    riding = len(packs)
    npart = len(parts)
    assert not (riding and npart)

    def body(cc_ref, a_ref, b_ref, *rest):
        if riding:
            pack_refs, (land_ref, p_ref, pb_ref), rest = rest[:riding], rest[riding:riding + 3], rest[riding + 3:]
            all_refs, (stage, rbuf, send_sems, recv_sems, rsem), g_sems = rest[:riding], rest[riding:riding + 5], rest[riding + 5:]
            gathers = [_PackGather(pack_refs[n], all_refs[n], *g_sems[3 * n:3 * n + 3]) for n in range(riding)]
        elif npart:
            part_refs, (land_ref, p_ref, pb_ref), rest = rest[:npart], rest[npart:npart + 3], rest[npart + 3:]
            arrived_refs, (stage, rbuf, send_sems, recv_sems, rsem, x_send, x_recv) = rest[:npart], rest[npart:]
            exchange = _PartialExchange(part_refs, arrived_refs, x_send, x_recv)
        else:
            land_ref, p_ref, pb_ref, stage, rbuf, send_sems, recv_sems, rsem = rest
        ph, s = pl.program_id(0), pl.program_id(1)
        if riding:
            @pl.when((ph == 0) & (s == 0))
            def _():
                for gather in gathers:
                    gather.start()

            @pl.when((ph == 1) & (s == N_CHIPS - 2))
            def _():
                for gather in gathers:
                    gather.hand_over()
        if npart:
            @pl.when((ph == 0) & (s == 0))
            def _():
                exchange.start()
        def push(k):
            return pltpu.make_async_remote_copy(src_ref=stage.at[k], dst_ref=land_ref.at[k], send_sem=send_sems.at[k],
                                                recv_sem=recv_sems.at[k], device_id=_sibling(), device_id_type=MESH)

        def landed():
            return pltpu.make_async_copy(land_ref.at[s], rbuf, rsem)

        @pl.when(ph == 1)
        def _():
            push(s).wait_recv()
            landed().start()

        slot = jnp.where(ph == 0, s, mine)
        acc = stage.at[slot]
        acc[...] = _dot_tn(a_ref[0:K_CHUNK, :], b_ref[0:K_CHUNK, :])
        for k in range(1, nk):
            acc[...] += _dot_tn(a_ref[k * K_CHUNK:(k + 1) * K_CHUNK, :], b_ref[k * K_CHUNK:(k + 1) * K_CHUNK, :])

        @pl.when(ph == 0)
        def _():
            push(s).start()

        @pl.when(ph == 1)
        def _():
            landed().wait()
            p = stage[mine] + rbuf[...]
            p_ref[0] = p
            pb_ref[0] = p.astype(BF16)

        @pl.when((ph == 1) & (s == N_CHIPS - 1))
        def _():
            for k in range(N_CHIPS):
                push(k).wait_send()
            for gather in (gathers if riding else ()):
                gather.finish()
            if npart:
                exchange.wait()

    def half(ph, cc):
        return jnp.where(ph == 0, 1 - cc[0], cc[0])

    def out_slot(ph, s, cc):
        return (jnp.where(ph == 0, 0, s), 0, 0)

    piece = jax.ShapeDtypeStruct((N_CHIPS, pr, pc), F32)
    in_specs = [pl.BlockSpec((t, pr), lambda ph, s, cc: (0, a_blk(s, half(ph, cc)))),
                pl.BlockSpec((t, pc), lambda ph, s, cc: (0, b_blk(s, half(ph, cc))))]
    out_specs = [ANY, pl.BlockSpec((1, pr, pc), out_slot), pl.BlockSpec((1, pr, pc), out_slot)]
    out_shape = [piece, piece, jax.ShapeDtypeStruct((N_CHIPS, pr, pc), BF16)]
    scratch = [pltpu.VMEM((N_CHIPS + 1, pr, pc), F32), pltpu.VMEM((pr, pc), F32),
               pltpu.SemaphoreType.DMA((N_CHIPS,)), pltpu.SemaphoreType.DMA((N_CHIPS,)), pltpu.SemaphoreType.DMA]
    operands = [a, b]
    for pack in packs:
        in_specs.append(pl.BlockSpec(pack.shape, lambda ph, s, cc: (0, 0)))
        out_specs.append(ANY)
        out_shape.append(jax.ShapeDtypeStruct((N_DEVICES,) + pack.shape, pack.dtype))
        operands.append(pack)
    for pack in packs:
        scratch += _PackGather.semaphores()
    if npart:
        in_specs += [ANY] * npart
        out_specs += [ANY] * npart
        out_shape += [jax.ShapeDtypeStruct(p.shape, p.dtype) for p in parts]
        scratch += [pltpu.SemaphoreType.DMA((npart, 3)), pltpu.SemaphoreType.DMA((npart, 3))]
        operands += list(parts)
    return pl.pallas_call(
        body, name="wgrad_" + tag,
        grid_spec=pltpu.PrefetchScalarGridSpec(
            num_scalar_prefetch=1, grid=(2, N_CHIPS), in_specs=in_specs, out_specs=out_specs, scratch_shapes=scratch),
        out_shape=out_shape,
        compiler_params=_params(dimension_semantics=("arbitrary", "arbitrary")),
    )(core_chip, *operands)[1:]


def _other_chips(x, y):
    return [(1 - x, y), (x, 1 - y), (1 - x, 1 - y)]


class _ShardGather:
    PAIRS = 9

    def __init__(self, outs, send_sems, recv_sems):
        self.outs, self.send_sems, self.recv_sems = outs, send_sems, recv_sems
        x, y, c = _position()
        self.c, self.j = c, 2 * x + y
        self.sibling = (x, y, 1 - c)
        self.chips = _other_chips(x, y)

    def _chip(self, k):
        px, py = self.chips[k]
        return 2 * px + py

    def _half(self, w, chip, which):
        hr = self.outs[w].shape[1] // 2
        return self.outs[w].at[chip, pl.ds(which * hr, hr), :]

    def _quarter(self, w, chip, q):
        qr = self.outs[w].shape[1] // 4
        return self.outs[w].at[chip, pl.ds(self.c * 2 * qr + q * qr, qr), :]

    def _copy(self, ref, w, pair, to, src=None):
        return pltpu.make_async_remote_copy(src_ref=ref if src is None else src, dst_ref=ref, send_sem=self.send_sems.at[w, pair],
                                            recv_sem=self.recv_sems.at[w, pair], device_id=to, device_id_type=MESH)

    def direct(self, w, k, q, src=None):
        return self._copy(self._quarter(w, self.j, q), w, 2 * k + q, (*self.chips[k], self.c), src)

    def direct_landed(self, w, k, q):
        return self._copy(self._quarter(w, self._chip(k), q), w, 2 * k + q, (*self.chips[k], self.c))

    def pass_on(self, w, q):
        return self._copy(self._quarter(w, self._chip(q), q), w, 4 + q, (*self.chips[1 - q], self.c))

    def passed_landed(self, w, q):
        return self._copy(self._quarter(w, self._chip(2), q), w, 4 + q, (*self.chips[1 - q], self.c))

    def hand_over(self, w, k):
        return self._copy(self._half(w, self._chip(k), self.c), w, 6 + k, self.sibling)

    def handed(self, w, k):
        return self._copy(self._half(w, self._chip(k), 1 - self.c), w, 6 + k, self.sibling)

    def start_direct(self, w, src_half=None):
        qr = self.outs[w].shape[1] // 4
        for k, q in ((0, 0), (1, 1), (0, 1), (1, 0)):
            self.direct(w, k, q, None if src_half is None else src_half.at[pl.ds(q * qr, qr), :]).start()

    def start_pass_on(self, w):
        for q in (0, 1):
            self.direct_landed(w, q, q).wait_recv()
            self.pass_on(w, q).start()

    def start_hand_over(self, w, diagonal):
        if diagonal:
            for q in (0, 1):
                self.passed_landed(w, q).wait_recv()
            self.hand_over(w, 2).start()
        else:
            for k in (0, 1):
                self.direct_landed(w, k, 1 - k).wait_recv()
                self.hand_over(w, k).start()

    def finish(self, w):
        for k in range(3):
            self.handed(w, k).wait_recv()
            self.hand_over(w, k).wait_send()
        for q in (0, 1):
            self.pass_on(w, q).wait_send()
            for k in (0, 1):
                self.direct(w, k, q).wait_send()


def _gather_first(w_in, w_out, w1, w2, small):
    bigs = (w_in, w_out, w1, w2)
    nb = len(bigs)

    def body(win_ref, wout_ref, w1_ref, w2_ref, sm_ref, gin, gout, g1, g2, gsm, st_in, st_out, st_1, st_2,
             send_sems, recv_sems, sm_send, sm_recv, local_sems):
        srcs = (win_ref, wout_ref, w1_ref, w2_ref)
        stages = (st_in, st_out, st_1, st_2)
        outs = (gin, gout, g1, g2)
        plan = _ShardGather(outs[:1], send_sems, recv_sems)
        j, c = plan.j, plan.c
        local = [pltpu.make_async_copy(stages[w], outs[w].at[j], local_sems.at[w]) for w in range(nb)]
        local.append(pltpu.make_async_copy(sm_ref, gsm.at[j], local_sems.at[nb]))

        def small_copy(k):
            px, py = plan.chips[k]
            return pltpu.make_async_remote_copy(src_ref=sm_ref, dst_ref=gsm.at[j], send_sem=sm_send.at[k],
                                                recv_sem=sm_recv.at[k], device_id=(px, py, c), device_id_type=MESH)

        def small_landed(k):
            px, py = plan.chips[k]
            return pltpu.make_async_remote_copy(src_ref=sm_ref, dst_ref=gsm.at[2 * px + py], send_sem=sm_send.at[k],
                                                recv_sem=sm_recv.at[k], device_id=(px, py, c), device_id_type=MESH)

        hr = w_in.shape[0] // 2
        st_in[...] = win_ref[...].astype(BF16)
        plan.start_direct(0, st_in.at[pl.ds(c * hr, hr), :])
        for k in range(3):
            small_copy(k).start()
        for src, st in zip(srcs[1:], stages[1:]):
            st[...] = src[...].astype(BF16)
        for cp in local:
            cp.start()
        plan.start_pass_on(0)
        plan.start_hand_over(0, diagonal=False)
        plan.start_hand_over(0, diagonal=True)
        for k in range(3):
            small_landed(k).wait_recv()
            small_copy(k).wait_send()
        plan.finish(0)
        for cp in local:
            cp.wait()

    def gathered(a, dtype):
        return jax.ShapeDtypeStruct((N_CHIPS,) + a.shape, dtype)

    return pl.pallas_call(
        body, name="gather_first",
        in_specs=[VMEM] * 5, out_specs=[ANY] * 5,
        out_shape=[gathered(a, BF16) for a in bigs] + [gathered(small, F32)],
        scratch_shapes=[pltpu.VMEM(a.shape, BF16) for a in bigs]
        + [pltpu.SemaphoreType.DMA((1, _ShardGather.PAIRS)), pltpu.SemaphoreType.DMA((1, _ShardGather.PAIRS)), pltpu.SemaphoreType.DMA((3,)),
           pltpu.SemaphoreType.DMA((3,)), pltpu.SemaphoreType.DMA((nb + 1,))],
        compiler_params=_params(),
    )(*bigs, small)


class _PartialExchange:
    def __init__(self, parts, arrived, send_sems, recv_sems):
        self.parts, self.arrived, self.send_sems, self.recv_sems = parts, arrived, send_sems, recv_sems
        x, y, c = _position()
        self.c, self.j = c, 2 * x + y
        self.chips = _other_chips(x, y)

    def _copy(self, w, k, slot):
        px, py = self.chips[k]
        return pltpu.make_async_remote_copy(
            src_ref=self.parts[w].at[2 * px + py], dst_ref=self.arrived[w].at[slot], send_sem=self.send_sems.at[w, k],
            recv_sem=self.recv_sems.at[w, k], device_id=(px, py, self.c), device_id_type=MESH)

    def start(self):
        for w in range(len(self.parts)):
            for k in range(3):
                self._copy(w, k, self.j).start()

    def wait(self):
        for w in range(len(self.parts)):
            for k in range(3):
                px, py = self.chips[k]
                self._copy(w, k, 2 * px + py).wait()


class _PackGather:
    def __init__(self, p_ref, all_ref, send_sems, recv_sems, local_sem):
        self.p_ref, self.all_ref, self.send_sems, self.recv_sems, self.local_sem = p_ref, all_ref, send_sems, recv_sems, local_sem
        x, y, c = _position()
        self.me, self.sibling, self.c = (x, y, c), (x, y, 1 - c), c
        self.chips = _other_chips(x, y)

    @staticmethod
    def semaphores():
        return [pltpu.SemaphoreType.DMA((7,)), pltpu.SemaphoreType.DMA((7,)), pltpu.SemaphoreType.DMA]

    def _copy(self, k, block, to, from_pack=False):
        px, py, pc = block
        slot = self.all_ref.at[4 * px + 2 * py + pc]
        return pltpu.make_async_remote_copy(src_ref=self.p_ref if from_pack else slot, dst_ref=slot, send_sem=self.send_sems.at[k],
                                            recv_sem=self.recv_sems.at[k], device_id=to, device_id_type=MESH)

    def _mine(self):
        x, y, c = self.me
        return pltpu.make_async_copy(self.p_ref, self.all_ref.at[4 * x + 2 * y + c], self.local_sem)

    def _first(self):
        return [self._copy(0, self.me, self.sibling, True)] + [
            self._copy(1 + k, self.me, (*chip, self.c), True) for k, chip in enumerate(self.chips)]

    def _passed(self):
        return [self._copy(4 + k, (*chip, self.c), self.sibling) for k, chip in enumerate(self.chips)]

    def start(self):
        self._mine().start()
        for cp in self._first():
            cp.start()

    def hand_over(self):
        for k, chip in enumerate(self.chips):
            self._copy(1 + k, (*chip, self.c), self.me).wait_recv()
            self._passed()[k].start()

    def finish(self):
        self._copy(0, self.sibling, self.me).wait_recv()
        for k, chip in enumerate(self.chips):
            self._copy(4 + k, (*chip, 1 - self.c), self.me).wait_recv()
        for cp in self._first() + self._passed():
            cp.wait_send()
        self._mine().wait()


class _DirectGather:
    def __init__(self, p_ref, all_ref, send_sems, recv_sems, local_sem):
        self.p_ref, self.all_ref, self.send_sems, self.recv_sems, self.local_sem = p_ref, all_ref, send_sems, recv_sems, local_sem
        self.me = _position()

    semaphores = _PackGather.semaphores

    def _peer(self, r):
        x, y, c = self.me
        return ((1 - x) if r & 4 else x, (1 - y) if r & 2 else y, (1 - c) if r & 1 else c)

    def _copy(self, r, slot_of):
        px, py, pc = slot_of
        return pltpu.make_async_remote_copy(src_ref=self.p_ref, dst_ref=self.all_ref.at[4 * px + 2 * py + pc],
                                            send_sem=self.send_sems.at[r - 1], recv_sem=self.recv_sems.at[r - 1],
                                            device_id=self._peer(r), device_id_type=MESH)

    def _mine(self):
        x, y, c = self.me
        return pltpu.make_async_copy(self.p_ref, self.all_ref.at[4 * x + 2 * y + c], self.local_sem)

    def start(self):
        self._mine().start()
        for r in range(1, N_DEVICES):
            self._copy(r, self.me).start()

    def finish(self):
        for r in range(1, N_DEVICES):
            self._copy(r, self._peer(r)).wait()
        self._mine().wait()


def _adamw(w, g, m, v):
    m = ADAM_B1 * m + (1.0 - ADAM_B1) * g
    v = ADAM_B2 * v + (1.0 - ADAM_B2) * (g * g)
    m_hat = m / ADAM_BC1
    v_hat = v / ADAM_BC2
    delta = -ADAM_LR * (m_hat / (jnp.sqrt(v_hat) + ADAM_EPS) + ADAM_WD * w)
    return delta, m, v


JOIN_SUB = 4


def _join(tag, shard_shape, part, arrived, core_chip, block=None):
    pr, pc = WGRAD_GEOMETRY[tag][:2]
    rb = pr // JOIN_SUB
    by_rows = shard_shape[1] == pc
    riding = block is not None

    def body(cc_ref, p_ref, r1_ref, r2_ref, r3_ref, *rest):
        if riding:
            blk_ref, g_ref, all_ref, stage, send_sems, recv_sems, local_sems, b_send, b_recv, b_local = rest
            gather = _DirectGather(blk_ref, all_ref, b_send, b_recv, b_local)
        else:
            g_ref, stage, send_sems, recv_sems, local_sems = rest
        i = pl.program_id(0)
        c = cc_ref[0]
        if riding:
            @pl.when(i == 0)
            def _():
                gather.start()

        def window(core, k):
            if by_rows:
                return g_ref.at[pl.ds((core * JOIN_SUB + k) * rb, rb), :]
            return g_ref.at[pl.ds(k * rb, rb), pl.ds(core * pc, pc)]

        def keep(k):
            return pltpu.make_async_copy(stage.at[k], window(c, k), local_sems.at[k])

        def push(k):
            return pltpu.make_async_remote_copy(src_ref=stage.at[k], dst_ref=window(c, k), send_sem=send_sems.at[k],
                                                recv_sem=recv_sems.at[k], device_id=_sibling(), device_id_type=MESH)

        def pushed(k):
            return pltpu.make_async_remote_copy(src_ref=stage.at[k], dst_ref=window(1 - c, k), send_sem=send_sems.at[k],
                                                recv_sem=recv_sems.at[k], device_id=_sibling(), device_id_type=MESH)

        stage[i] = ((p_ref[0] + r1_ref[0].astype(F32)) + r2_ref[0].astype(F32)) + r3_ref[0].astype(F32)
        keep(i).start()
        push(i).start()

        @pl.when(i == JOIN_SUB - 1)
        def _():
            for k in range(JOIN_SUB):
                keep(k).wait()
                push(k).wait_send()
                pushed(k).wait_recv()
            if riding:
                gather.finish()

    def partial(off):
        return pl.BlockSpec((1, rb, pc), lambda i, cc: ((cc[1] + off) % N_CHIPS, i, 0))

    in_specs = [partial(0), partial(1), partial(2), partial(3)]
    out_specs = [ANY]
    out_shape = [jax.ShapeDtypeStruct(shard_shape, F32)]
    scratch = [pltpu.VMEM((JOIN_SUB, rb, pc), F32), pltpu.SemaphoreType.DMA((JOIN_SUB,)),
               pltpu.SemaphoreType.DMA((JOIN_SUB,)), pltpu.SemaphoreType.DMA((JOIN_SUB,))]
    operands = [part, arrived, arrived, arrived]
    if riding:
        in_specs.append(pl.BlockSpec(block.shape, lambda i, cc: (0, 0)))
        out_specs.append(ANY)
        out_shape.append(jax.ShapeDtypeStruct((N_DEVICES,) + block.shape, block.dtype))
        scratch += _DirectGather.semaphores()
        operands.append(block)
    outs = pl.pallas_call(
        body, name="join_" + tag,
        grid_spec=pltpu.PrefetchScalarGridSpec(
            num_scalar_prefetch=1, grid=(JOIN_SUB,), in_specs=in_specs, out_specs=out_specs, scratch_shapes=scratch),
        out_shape=out_shape,
        compiler_params=_params(dimension_semantics=("arbitrary",)),
    )(core_chip, *operands)
    return outs if riding else outs[0]


def _adamw_big(w, g, m, v, name):
    rows, cols = w.shape
    rb = 256 if rows % 256 == 0 else rows

    def body(w_ref, g_ref, m_ref, v_ref, go_ref, d_ref, nm_ref, nv_ref):
        g = g_ref[...]
        go_ref[...] = g
        d_ref[...], nm_ref[...], nv_ref[...] = _adamw(w_ref[...], g, m_ref[...], v_ref[...])

    spec = pl.BlockSpec((rb, cols), lambda i: (i, 0))
    return pl.pallas_call(
        body, name=name, grid=(rows // rb,), in_specs=[spec] * 4, out_specs=[spec] * 4,
        out_shape=[jax.ShapeDtypeStruct(w.shape, F32)] * 4,
        compiler_params=_params(dimension_semantics=("arbitrary",)),
    )(w, g, m, v)


SMALL_VECTORS = {
    "norm_mix_g": (PK_MIX_G, D_MODEL), "rnn_conv_b": (PK_RCONV_B, LRU_W), "b_a": (PK_B_A, LRU_W), "b_x": (PK_B_X, LRU_W),
    "lru_lambda": (PK_LAMBDA, LRU_W), "g_norm_conv": (PK_G_NORM_CONV, CONV_W), "g_norm_rnn": (PK_G_NORM_RNN, LRU_W),
    "norm_mlp_g": (PK_MLP_G, D_MODEL), "final_norm_g": (PK_FINAL_G, D_MODEL),
}
SMALL_MATRICES = {"w_a": PK_W_A, "w_x": PK_W_X}
MATRIX_ROWS = 64


def _small_step(vec_packs, mat_packs, mix_g_blocks, w_pack, m_pack, v_pack, conv_wmv, rconv_wmv):
    vec_rows = vec_packs.shape[1]
    rows, cols = vec_rows + mat_packs.shape[1], vec_packs.shape[2]
    conv_rows, cshard = 3, conv_wmv.shape[2]
    rconv_rows, rshard = 4, rconv_wmv.shape[2]
    mix_row = PK_MIX_G * TILE_ROWS
    names = list(SMALL_VECTORS) + list(SMALL_MATRICES) + ["conv_w", "rnn_conv_w"]
    shapes = ([(1, width) for _, width in SMALL_VECTORS.values()] + [(MATRIX_ROWS, cols)] * len(SMALL_MATRICES)
              + [(conv_rows, cshard), (rconv_rows, rshard)])

    def body(vec_ref, mat_ref, blk_ref, w_ref, m_ref, v_ref, cw_ref, rw_ref, loss_ref, *rest):
        leaves, g_ref = [rest[k * len(names):(k + 1) * len(names)] for k in range(4)], rest[4 * len(names)]
        total = vec_ref[0]
        mats = mat_ref[0].astype(F32)
        late = blk_ref[0]
        for k in range(1, N_DEVICES):
            total = total + vec_ref[k]
            mats = mats + mat_ref[k].astype(F32)
            late = late + blk_ref[k]
        g_ref[0:vec_rows, :] = total
        g_ref[vec_rows:, :] = mats
        g_ref[mix_row:mix_row + TILE_ROWS, :] = late
        g = g_ref[...]
        loss_ref[...] = g[PK_LOSS * TILE_ROWS:PK_LOSS * TILE_ROWS + 1, 0:1]

        x, y, _ = _position()
        j = 2 * x + y
        cblk = total[PK_CONV_W * 8:PK_CONV_W * 8 + 8, :]
        rblk = total[PK_RCONV_W * 8:PK_RCONV_W * 8 + 8, :]
        cg = cblk[:, 0:cshard]
        rg = rblk[:, 0:rshard]
        for k in range(1, N_CHIPS):
            cg = jnp.where(j == k, cblk[:, k * cshard:(k + 1) * cshard], cg)
            rg = jnp.where(j == k, rblk[:, k * rshard:(k + 1) * rshard], rg)

        packs = (g,) + _adamw(w_ref[...], g, m_ref[...], v_ref[...])
        convs = (cg,) + _adamw(cw_ref[0], cg, cw_ref[1], cw_ref[2])
        rconvs = (rg,) + _adamw(rw_ref[0], rg, rw_ref[1], rw_ref[2])
        for kind in range(4):
            out = dict(zip(names, leaves[kind]))
            for name, (block, width) in SMALL_VECTORS.items():
                out[name][...] = packs[kind][block * TILE_ROWS:block * TILE_ROWS + 1, 0:width]
            for name, block in SMALL_MATRICES.items():
                out[name][...] = packs[kind][block * TILE_ROWS:block * TILE_ROWS + MATRIX_ROWS, :]
            out["conv_w"][...] = convs[kind][0:conv_rows, :]
            out["rnn_conv_w"][...] = rconvs[kind][0:rconv_rows, :]

    outs = pl.pallas_call(
        body, name="small_grads_step", in_specs=[VMEM] * 8, out_specs=[VMEM] * (1 + 4 * len(names)),
        out_shape=[jax.ShapeDtypeStruct((1, 1), F32)] + [jax.ShapeDtypeStruct(sh, F32) for sh in shapes] * 4,
        scratch_shapes=[pltpu.VMEM((rows, cols), F32)],
        compiler_params=_params(),
    )(vec_packs, mat_packs, mix_g_blocks, w_pack, m_pack, v_pack, conv_wmv, rconv_wmv)
    return outs[0], [dict(zip(names, outs[1 + k * len(names):1 + (k + 1) * len(names)])) for k in range(4)]


def _blk(a):
    a = a.reshape(-1, a.shape[-1])
    return jnp.pad(a, ((0, TILE_ROWS - a.shape[0]), (0, D_MODEL - a.shape[1])))


def _zero_blk():
    return jnp.zeros((TILE_ROWS, D_MODEL), F32)


def _pack_params(p, pre):
    get = lambda n: p[pre + n]
    return jnp.concatenate([
        _blk(get("g_norm_rnn")), _blk(get("rnn_conv_b")), _blk(get("b_a")), _blk(get("b_x")), _blk(get("lru_lambda")),
        _zero_blk(), _zero_blk(), _blk(get("g_norm_conv")), _blk(get("final_norm_g").reshape(1, -1)), _blk(get("norm_mlp_g")),
        _zero_blk(), _blk(get("norm_mix_g")), get("w_a").reshape(64, D_MODEL), get("w_x").reshape(64, D_MODEL)], axis=0)


def _to_block_diag(w):
    w4 = w.reshape(N_BD, 4, 64, 64)
    return jnp.concatenate([jnp.pad(w4[:, q], ((0, 0), (0, 0), (64 * q, 64 * (3 - q)))) for q in range(4)], axis=1)


def _from_block_diag(d):
    d5 = d.reshape(N_BD, 4, 64, 4, 64)
    return jnp.stack([d5[:, q, :, q, :] for q in range(4)], axis=1).reshape(64, D_MODEL)


def _pad_rows(a):
    return jnp.pad(a, ((0, TILE_ROWS - a.shape[0]), (0, 0)))


_NAMES = ['norm_mix_g', 'w_in', 'conv_w', 'rnn_conv_w', 'rnn_conv_b', 'w_a', 'b_a', 'w_x', 'b_x', 'lru_lambda',
          'g_norm_conv', 'g_norm_rnn', 'w_out', 'norm_mlp_g', 'w_mlp_in', 'w_mlp_out', 'final_norm_g']


def kernel(x, norm_mix_g, w_in, conv_w, rnn_conv_w, rnn_conv_b, w_a, b_a, w_x, b_x, lru_lambda, g_norm_conv, g_norm_rnn, w_out, norm_mlp_g, w_mlp_in, w_mlp_out, final_norm_g, loss_target, m_norm_mix_g, m_w_in, m_conv_w, m_rnn_conv_w, m_rnn_conv_b, m_w_a, m_b_a, m_w_x, m_b_x, m_lru_lambda, m_g_norm_conv, m_g_norm_rnn, m_w_out, m_norm_mlp_g, m_w_mlp_in, m_w_mlp_out, m_final_norm_g, v_norm_mix_g, v_w_in, v_conv_w, v_rnn_conv_w, v_rnn_conv_b, v_w_a, v_b_a, v_w_x, v_b_x, v_lru_lambda, v_g_norm_conv, v_g_norm_rnn, v_w_out, v_norm_mlp_g, v_w_mlp_in, v_w_mlp_out, v_final_norm_g):
    args = dict(locals())
    p = {}
    for n in _NAMES:
        for pre in ("", "m_", "v_"):
            a = args[pre + n]
            p[pre + n] = a[0] if a.ndim >= 3 else a
    xs = x[0]
    target = loss_target[0]
    core_chip = jnp.stack([lax.axis_index("c"), 2 * lax.axis_index("x") + lax.axis_index("y")]).astype(jnp.int32)
    cshard = p["conv_w"].shape[1]
    rshard = p["rnn_conv_w"].shape[1]

    small = jnp.concatenate([_pad_rows(p["conv_w"]), _pad_rows(p["rnn_conv_w"])], axis=1)
    w_in_g, w_out_g, w1_g, w2_g, small_g = _gather_first(p["w_in"], p["w_out"], p["w_mlp_in"], p["w_mlp_out"], small)
    conv_full = small_g[:, :3, :cshard].transpose(1, 0, 2).reshape(3, CONV_W)
    rconv_full = small_g[:, :4, cshard:].transpose(1, 0, 2).reshape(4, LRU_W)
    wa_bd = _to_block_diag(p["w_a"]).astype(BF16)
    wx_bd = _to_block_diag(p["w_x"]).astype(BF16)
    gf = p["final_norm_g"].reshape(1, -1)
    lru = (wa_bd, p["b_a"], wx_bd, p["b_x"], p["lru_lambda"], p["g_norm_conv"], p["g_norm_rnn"])

    (u, h1b, xr, hs, c3, yb, gates), (w_out_g, w1_g, w2_g) = _fwd_mix(
        xs, p["norm_mix_g"], w_in_g, conv_full, rconv_full, p["rnn_conv_b"], *lru, (w_out_g, w1_g, w2_g))
    zb, dpb, h2b, dx3b, dx2, dx2b, dy, st_mlp = _mlp_fwd_bwd(
        xs, yb, w_out_g.reshape(-1, D_MODEL), w1_g, w2_g.reshape(-1, D_MODEL), p["norm_mlp_g"], gf, target)

    part_out = _wgrad(yb, dx2b, "out", core_chip)
    *part_1, arrived_out = _wgrad(h2b, dpb, "mlp_in", core_chip, parts=(part_out[1],))
    part_2 = _wgrad(zb, dx3b, "mlp_out", core_chip)
    (dub, st_mix, dwa_bd, dwx_bd), (arrived_1, arrived_2) = _mix_bwd(
        dy, u, xr, hs, c3, gates, conv_full, rconv_full, wa_bd, wx_bd, p["lru_lambda"], p["g_norm_conv"], p["g_norm_rnn"],
        (part_1[1], part_2[1]))
    arrived_mlp = (arrived_out, arrived_1, arrived_2)
    vec_pack = jnp.concatenate([st_mix, st_mlp, _zero_blk()], axis=0)
    mat_pack = jnp.concatenate([_from_block_diag(dwa_bd), _from_block_diag(dwx_bd)], axis=0).astype(BF16)
    *part_in, vec_packs, mat_packs = _wgrad(h1b, dub, "in", core_chip, packs=(vec_pack, mat_pack))
    early = (("w_out", "out", part_out, arrived_mlp[0]), ("w_mlp_in", "mlp_in", part_1, arrived_mlp[1]),
             ("w_mlp_out", "mlp_out", part_2, arrived_mlp[2]))
    (grad_x, st_in), arrived_in, joined = _in_bwd(
        dub, w_in_g, xs, dx2, p["norm_mix_g"], (part_in[1],),
        [(tag, p[n].shape, part[0], arrived) for n, tag, part, arrived in early], core_chip)
    g_in, mix_g_blocks = _join("in", p["w_in"].shape, part_in[0], arrived_in[0], core_chip, st_in)
    big = {}
    for n, tag, g in [(n, tag, g) for (n, tag, _, _), g in zip(early, joined)] + [("w_in", "in", g_in)]:
        big[n] = _adamw_big(p[n], g, p["m_" + n], p["v_" + n], "adamw_" + tag)

    conv_wmv = jnp.stack([_pad_rows(p[pre + "conv_w"]) for pre in ("", "m_", "v_")])
    rconv_wmv = jnp.stack([_pad_rows(p[pre + "rnn_conv_w"]) for pre in ("", "m_", "v_")])
    loss, outs = _small_step(
        vec_packs, mat_packs, mix_g_blocks, _pack_params(p, ""), _pack_params(p, "m_"), _pack_params(p, "v_"), conv_wmv, rconv_wmv)
    for kind, o in enumerate(outs):
        o["final_norm_g"] = o["final_norm_g"].reshape(-1)
        for n in SMALL_MATRICES:
            o[n] = o[n].reshape(1, 16, 64, 64)
        for n in ("conv_w", "rnn_conv_w"):
            o[n] = o[n][None]
        for n in ("w_in", "w_out", "w_mlp_in", "w_mlp_out"):
            o[n] = big[n][kind][None]
    loss = loss.reshape(())
    return (loss, grad_x[None], *[o[n] for o in outs for n in _NAMES])
```

```python
import functools
import math

import jax
import jax.numpy as jnp
from jax import lax
from jax.experimental import pallas as pl
from jax.experimental.pallas import tpu as pltpu

F32 = jnp.float32
BF16 = jnp.bfloat16
MESH = pl.DeviceIdType.MESH
ANY = pl.BlockSpec(memory_space=pl.ANY)
VMEM = pl.BlockSpec(memory_space=pltpu.VMEM)

EPS = 1e-6
LRU_C = 8.0
D_MODEL = 1024
CONV_W = 512
LRU_W = 1024
IN_COLS = 3 * CONV_W + 2 * LRU_W
IN_SHARD = IN_COLS // 4
N_CHIPS = 4
N_DEVICES = 8
BD = 256
N_BD = LRU_W // BD

ADAM_LR = 0.001
ADAM_B1 = 0.9
ADAM_B2 = 0.999
ADAM_EPS = 1e-08
ADAM_WD = 0.01
ADAM_STEP = 10
ADAM_BC1 = 1.0 - ADAM_B1 ** ADAM_STEP
ADAM_BC2 = 1.0 - ADAM_B2 ** ADAM_STEP

TILE_ROWS = 8
TOKEN_TILE = 256
MATMUL_TOKEN_TILE = 512
VMEM_LIMIT = 56 * 1024 * 1024

PK_G_NORM_RNN, PK_RCONV_B, PK_B_A, PK_B_X, PK_LAMBDA, PK_RCONV_W, PK_CONV_W, PK_G_NORM_CONV = range(8)
PK_FINAL_G, PK_MLP_G, PK_LOSS, PK_MIX_G = 8, 9, 10, 11
PK_BLOCKS = 12
PK_ROWS = PK_BLOCKS * TILE_ROWS
N_HEADS, HEAD_DIM = 16, 64


def _params(**kw):
    return pltpu.CompilerParams(vmem_limit_bytes=VMEM_LIMIT, **kw)


def _position():
    x, y, c = lax.axis_index("x"), lax.axis_index("y"), lax.axis_index("c")
    return x, y, c


def _sigmoid(v):
    return 1.0 / (1.0 + jnp.exp(-v))


def _one_minus_square(log_a, a):
    v = 2.0 * log_a
    series = -v * (1.0 + v * (0.5 + v * (1.0 / 6.0)))
    return jnp.where(v > -0.01, series, 1.0 - a * a)


_GELU_C = math.sqrt(2.0 / math.pi)
_GELU_K = 0.044715


def _gelu_and_grad(g):
    th = jnp.tanh(_GELU_C * (g + _GELU_K * g * g * g))
    gelu = 0.5 * g * (1.0 + th)
    dgelu = 0.5 * (1.0 + th) + 0.5 * g * (1.0 - th * th) * (_GELU_C * (1.0 + 3.0 * _GELU_K * g * g))
    return gelu, dgelu


def _rows(shape):
    return lax.broadcasted_iota(jnp.int32, shape, 0)


def _shift_down(v, k, prev8):
    rolled = pltpu.roll(v, k, 0)
    halo = pltpu.roll(prev8, k, 0)
    head = jnp.where(_rows(halo.shape) < k, halo, rolled[:TILE_ROWS])
    return jnp.concatenate([head, rolled[TILE_ROWS:]], axis=0)


def _shift_up(v, k, next8):
    n = v.shape[0]
    rolled = pltpu.roll(v, n - k, 0)
    halo = pltpu.roll(next8, TILE_ROWS - k, 0)
    tail = jnp.where(_rows(halo.shape) >= TILE_ROWS - k, halo, rolled[n - TILE_ROWS:])
    return jnp.concatenate([rolled[: n - TILE_ROWS], tail], axis=0)


def _scan_rows(a, b, carry, reverse=False):
    n, w = a.shape
    groups = n // TILE_ROWS
    a3 = a.reshape(groups, TILE_ROWS, w)
    b3 = b.reshape(groups, TILE_ROWS, w)
    sub = lax.broadcasted_iota(jnp.int32, a3.shape, 1)
    s = 1
    while s < TILE_ROWS:
        shift = TILE_ROWS - s if reverse else s
        keep = (sub < TILE_ROWS - s) if reverse else (sub >= s)
        b3 = b3 + jnp.where(keep, a3 * pltpu.roll(b3, shift, 1), 0.0)
        a3 = a3 * jnp.where(keep, pltpu.roll(a3, shift, 1), 1.0)
        s *= 2
    out = [None] * groups
    edge = 0 if reverse else TILE_ROWS - 1
    for g in (range(groups - 1, -1, -1) if reverse else range(groups)):
        out[g] = b3[g] + a3[g] * carry
        carry = out[g][edge:edge + 1]
    return jnp.concatenate(out, axis=0)


def _softplus_neg(lam):
    e = jnp.exp(-jnp.abs(lam))
    log1p_e = jnp.where(e < 1e-2, e * (1.0 - e * (0.5 - e * (1.0 / 3.0 - e * 0.25))), jnp.log(1.0 + e))
    sp = jnp.maximum(-lam, 0.0) + log1p_e
    dsp = -_sigmoid(-lam)
    return sp, dsp


def _block_diag_dot(vb, w_ref):
    return jnp.concatenate(
        [jnp.dot(vb[:, j * BD:(j + 1) * BD], w_ref[j], preferred_element_type=F32) for j in range(N_BD)], axis=1)


def _block_diag_dot_t(vb, w_ref):
    return jnp.concatenate(
        [lax.dot_general(vb[:, j * BD:(j + 1) * BD], w_ref[j], (((1,), (1,)), ((), ())), preferred_element_type=F32)
         for j in range(N_BD)], axis=1)


def _dot_nt(a, b):
    return lax.dot_general(a, b, (((1,), (1,)), ((), ())), preferred_element_type=F32)


def _dot_tn(a, b):
    return lax.dot_general(a, b, (((0,), (0,)), ((), ())), preferred_element_type=F32)


def _lru_gates(xr, wa_ref, ba, wx_ref, bx, sp):
    xrb = xr.astype(BF16)
    r = _sigmoid(_block_diag_dot(xrb, wa_ref) + ba)
    ig = _sigmoid(_block_diag_dot(xrb, wx_ref) + bx)
    log_a = (-LRU_C) * r * sp
    a = jnp.exp(log_a)
    mult = jnp.sqrt(_one_minus_square(log_a, a))
    return r, ig, a, mult


def _colsum(v):
    return jnp.sum(v, axis=0, keepdims=True)


N_FWD_OUT = 7


def _fwd_mix(x, g1, w_in_g, conv_w, rconv_w, rconv_b, wa_bd, b_a, wx_bd, b_x, lam, g_nc, g_nr, later):
    t, d = x.shape
    tm = TOKEN_TILE
    nt = t // tm
    nl = len(later)
    assert nl == 3
    pass_on_at = [nt * f // 16 for f in (3, 5, 9)]
    neighbours_at = [nt * f // 16 for f in (10, 11, 12)]
    diagonal_at = [nt * f // 16 for f in (13, 14, 14)]

    def body(x_ref, g1_ref, win_ref, cw_ref, rw_ref, rb_ref, wa_ref, ba_ref, wx_ref, bx_ref, lam_ref, gnc_ref, gnr_ref,
             *rest):
        later_in, outs, rest = rest[:nl], rest[nl:nl + N_FWD_OUT], rest[nl + N_FWD_OUT:]
        u_ref, h1_ref, xr_ref, hs_ref, c3_ref, y_ref, gates_ref = outs
        later_out, (cv_prev, xin_prev, h_prev, send_sems, recv_sems) = rest[:nl], rest[nl:]
        del later_in
        step = pl.program_id(0)
        plan = _ShardGather(later_out, send_sems, recv_sems)

        @pl.when(step == 0)
        def _():
            cv_prev[...] = jnp.zeros_like(cv_prev)
            xin_prev[...] = jnp.zeros_like(xin_prev)
            h_prev[...] = jnp.zeros_like(h_prev)
            for w in range(nl):
                plan.start_direct(w)

        for w in range(nl):
            @pl.when(step == pass_on_at[w])
            def _(w=w):
                plan.start_pass_on(w)

            @pl.when(step == neighbours_at[w])
            def _(w=w):
                plan.start_hand_over(w, diagonal=False)

            @pl.when(step == diagonal_at[w])
            def _(w=w):
                plan.start_hand_over(w, diagonal=True)

        xv = x_ref[...]
        rstd = lax.rsqrt(jnp.mean(xv * xv, axis=-1, keepdims=True) + EPS)
        h1b = ((xv * rstd) * g1_ref[...]).astype(BF16)
        h1_ref[...] = h1b
        for j in range(N_CHIPS):
            u_ref[:, j * IN_SHARD:(j + 1) * IN_SHARD] = jnp.dot(h1b, win_ref[j], preferred_element_type=F32)
        gate_b = u_ref[:, 0:CONV_W]
        cv = u_ref[:, CONV_W:2 * CONV_W] * u_ref[:, 2 * CONV_W:3 * CONV_W]
        x_r = u_ref[:, 3 * CONV_W:3 * CONV_W + LRU_W]
        g = u_ref[:, 3 * CONV_W + LRU_W:]

        cw = cw_ref[...]
        cvp = cv_prev[...]
        conv3 = cw[0:1] * _shift_down(cv, 2, cvp) + cw[1:2] * _shift_down(cv, 1, cvp) + cw[2:3] * cv
        cv_prev[...] = cv[tm - TILE_ROWS:]
        c3_ref[...] = conv3
        y_conv = gate_b * conv3

        rw = rw_ref[...]
        xp = xin_prev[...]
        xr = (rw[0:1] * _shift_down(x_r, 3, xp) + rw[1:2] * _shift_down(x_r, 2, xp)
              + rw[2:3] * _shift_down(x_r, 1, xp) + rw[3:4] * x_r) + rb_ref[...]
        xin_prev[...] = x_r[tm - TILE_ROWS:]
        xr_ref[...] = xr
        sp, _ = _softplus_neg(lam_ref[...])
        r, ig, a, mult = _lru_gates(xr, wa_ref, ba_ref[...], wx_ref, bx_ref[...], sp)
        for n, gate in enumerate((r, ig, a, mult)):
            gates_ref[:, n * LRU_W:(n + 1) * LRU_W] = gate
        h = _scan_rows(a, mult * (ig * xr), h_prev[...])
        h_prev[...] = h[tm - 1:tm]
        hs_ref[...] = h
        gelu, _ = _gelu_and_grad(g)
        y_rnn = h * gelu

        na = y_conv * lax.rsqrt(jnp.mean(y_conv * y_conv, axis=-1, keepdims=True) + EPS) * gnc_ref[...]
        nb = y_rnn * lax.rsqrt(jnp.mean(y_rnn * y_rnn, axis=-1, keepdims=True) + EPS) * gnr_ref[...]
        y_ref[:, :CONV_W] = na.astype(BF16)
        y_ref[:, CONV_W:] = nb.astype(BF16)

        @pl.when(step == nt - 1)
        def _():
            for w in range(nl):
                plan.finish(w)

    def full(a):
        nd = a.ndim
        return pl.BlockSpec(a.shape, lambda i: (0,) * nd)

    def tok(cols):
        return pl.BlockSpec((tm, cols), lambda i: (i, 0))

    def act(cols, dtype=F32):
        return jax.ShapeDtypeStruct((t, cols), dtype)

    smalls = (g1, w_in_g, conv_w, rconv_w, rconv_b, wa_bd, b_a, wx_bd, b_x, lam, g_nc, g_nr)
    n_in = 1 + len(smalls)
    outs = pl.pallas_call(
        body, name="fwd_mix", grid=(nt,),
        in_specs=[tok(d)] + [full(a) for a in smalls] + [ANY] * nl,
        out_specs=[tok(IN_COLS), tok(d), tok(LRU_W), tok(LRU_W), tok(CONV_W), tok(CONV_W + LRU_W)]
        + [tok(4 * LRU_W)] + [ANY] * nl,
        out_shape=[act(IN_COLS), act(d, BF16), act(LRU_W), act(LRU_W), act(CONV_W), act(CONV_W + LRU_W, BF16)]
        + [act(4 * LRU_W)] + [jax.ShapeDtypeStruct(a.shape, a.dtype) for a in later],
        input_output_aliases={n_in + w: N_FWD_OUT + w for w in range(nl)},
        scratch_shapes=[pltpu.VMEM((TILE_ROWS, CONV_W), F32), pltpu.VMEM((TILE_ROWS, LRU_W), F32),
                        pltpu.VMEM((1, LRU_W), F32), pltpu.SemaphoreType.DMA((nl, _ShardGather.PAIRS)),
                        pltpu.SemaphoreType.DMA((nl, _ShardGather.PAIRS))],
        compiler_params=_params(dimension_semantics=("arbitrary",)),
    )(x, *smalls, *later)
    return outs[:N_FWD_OUT], outs[N_FWD_OUT:]


def _mlp_fwd_bwd(x, yb, w_out_g, w1_g, w2_g, g2, gf, target):
    t, d = x.shape
    tm = TOKEN_TILE
    ff = w2_g.shape[0]
    mix = w_out_g.shape[0]
    ffs = ff // N_CHIPS

    def body(x_ref, y_ref, g2_ref, gf_ref, tgt_ref, wout_hbm, w1_hbm, w2_hbm,
             z_ref, dp_ref, h2_ref, dx3b_ref, dx2_ref, dx2b_ref, dy_ref, st_ref, wout, w1, w2, p_ref):
        @pl.when(pl.program_id(0) == 0)
        def _():
            pltpu.sync_copy(wout_hbm, wout)
            pltpu.sync_copy(w1_hbm, w1)
            pltpu.sync_copy(w2_hbm, w2)
            st_ref[...] = jnp.zeros_like(st_ref)

        x2 = x_ref[...] + jnp.dot(y_ref[...], wout[...], preferred_element_type=F32)
        r2 = lax.rsqrt(jnp.mean(x2 * x2, axis=-1, keepdims=True) + EPS)
        xh2 = x2 * r2
        g2v = g2_ref[...]
        h2b = (xh2 * g2v).astype(BF16)
        h2_ref[...] = h2b
        for j in range(N_CHIPS):
            p_ref[:, j * ffs:(j + 1) * ffs] = jnp.dot(h2b, w1[j], preferred_element_type=F32)
        rp = jnp.maximum(p_ref[...], 0.0)
        zb = (rp * rp).astype(BF16)
        z_ref[...] = zb
        x3 = x2 + jnp.dot(zb, w2[...], preferred_element_type=F32)
        r3 = lax.rsqrt(jnp.mean(x3 * x3, axis=-1, keepdims=True) + EPS)
        xh3 = x3 * r3
        gfv = gf_ref[...]
        err = xh3 * gfv - tgt_ref[...]
        loss = (0.5 / d) * jnp.sum(err * err)
        dout = err * (1.0 / d)
        st_ref[PK_FINAL_G * 8 - 64:PK_FINAL_G * 8 - 63, :] += _colsum(dout * xh3)
        st_ref[PK_LOSS * 8 - 64:PK_LOSS * 8 - 63, :] += jnp.zeros((1, d), F32) + loss
        dxh3 = dout * gfv
        dx3 = r3 * (dxh3 - xh3 * jnp.mean(dxh3 * xh3, axis=-1, keepdims=True))
        dx3b = dx3.astype(BF16)
        dx3b_ref[...] = dx3b
        dpb = (_dot_nt(dx3b, w2[...]) * (2.0 * rp)).astype(BF16)
        dp_ref[...] = dpb
        dh2 = _dot_nt(dpb[:, 0:ffs], w1[0])
        for j in range(1, N_CHIPS):
            dh2 = dh2 + _dot_nt(dpb[:, j * ffs:(j + 1) * ffs], w1[j])
        st_ref[PK_MLP_G * 8 - 64:PK_MLP_G * 8 - 63, :] += _colsum(dh2 * xh2)
        dxh2 = dh2 * g2v
        dx2 = dx3 + r2 * (dxh2 - xh2 * jnp.mean(dxh2 * xh2, axis=-1, keepdims=True))
        dx2_ref[...] = dx2
        dx2b = dx2.astype(BF16)
        dx2b_ref[...] = dx2b
        dy_ref[...] = _dot_nt(dx2b, wout[...])

    def tok(cols):
        return pl.BlockSpec((tm, cols), lambda i: (i, 0))

    def row(cols):
        return pl.BlockSpec((1, cols), lambda i: (0, 0))

    return pl.pallas_call(
        body, name="mlp_fwd_bwd", grid=(t // tm,),
        in_specs=[tok(d), tok(mix), row(d), row(d), tok(d), ANY, ANY, ANY],
        out_specs=[tok(ff), tok(ff), tok(d), tok(d), tok(d), tok(d), tok(mix),
                   pl.BlockSpec((3 * TILE_ROWS, d), lambda i: (0, 0))],
        out_shape=[jax.ShapeDtypeStruct((t, ff), BF16), jax.ShapeDtypeStruct((t, ff), BF16),
                   jax.ShapeDtypeStruct((t, d), BF16), jax.ShapeDtypeStruct((t, d), BF16),
                   jax.ShapeDtypeStruct((t, d), F32), jax.ShapeDtypeStruct((t, d), BF16),
                   jax.ShapeDtypeStruct((t, mix), F32), jax.ShapeDtypeStruct((3 * TILE_ROWS, d), F32)],
        scratch_shapes=[pltpu.VMEM(w_out_g.shape, BF16), pltpu.VMEM(w1_g.shape, BF16), pltpu.VMEM(w2_g.shape, BF16),
                        pltpu.VMEM((tm, ff), F32)],
        compiler_params=_params(dimension_semantics=("arbitrary",)),
    )(x, yb, g2, gf, target, w_out_g, w1_g, w2_g)


def _mix_bwd(dy, u, xr_all, hs_all, c3_all, gates, conv_w, rconv_w, wa_bd, wx_bd, lam, g_nc, g_nr, parts):
    t = dy.shape[0]
    tm = TOKEN_TILE
    nt = t // tm
    hb = tm // TILE_ROWS
    npart = len(parts)

    def body(dy_ref, u_ref, uh_ref, xr_ref, hs_ref, hh_ref, c3_ref, gates_ref,
             cw_ref, rw_ref, wa_ref, wx_ref, lam_ref, gnc_ref, gnr_ref, *rest):
        part_refs, (du_ref, st_ref, dwa_ref, dwx_ref), rest = rest[:npart], rest[npart:npart + 4], rest[npart + 4:]
        arrived_refs, (dc_next, a_next, gs_next, dxr_next, send_sems, recv_sems) = rest[:npart], rest[npart:]
        exchange = _PartialExchange(part_refs, arrived_refs, send_sems, recv_sems)
        i = pl.program_id(0)

        @pl.when(i == 0)
        def _():
            exchange.start()
            dc_next[...] = jnp.zeros_like(dc_next)
            a_next[...] = jnp.zeros_like(a_next)
            gs_next[...] = jnp.zeros_like(gs_next)
            dxr_next[...] = jnp.zeros_like(dxr_next)
            st_ref[...] = jnp.zeros_like(st_ref)
            dwa_ref[...] = jnp.zeros_like(dwa_ref)
            dwx_ref[...] = jnp.zeros_like(dwx_ref)

        first_tile = i == nt - 1
        gate_b = u_ref[:, 0:CONV_W]
        gate_c = u_ref[:, CONV_W:2 * CONV_W]
        v = u_ref[:, 2 * CONV_W:3 * CONV_W]
        x_r = u_ref[:, 3 * CONV_W:3 * CONV_W + LRU_W]
        g = u_ref[:, 3 * CONV_W + LRU_W:]
        cv = gate_c * v
        cv_prev = jnp.where(first_tile, 0.0, uh_ref[:, CONV_W:2 * CONV_W] * uh_ref[:, 2 * CONV_W:3 * CONV_W])
        xin_prev = jnp.where(first_tile, 0.0, uh_ref[:, 3 * CONV_W:3 * CONV_W + LRU_W])
        hs_prev = jnp.where(first_tile, 0.0, hh_ref[...])

        def acc(block, val, width=LRU_W, row=0):
            r0 = block * TILE_ROWS + row
            st_ref[r0:r0 + 1, 0:width] += val

        conv3 = c3_ref[...]
        y_conv = gate_b * conv3
        ra = lax.rsqrt(jnp.mean(y_conv * y_conv, axis=-1, keepdims=True) + EPS)
        xha = y_conv * ra
        dna = dy_ref[:, :CONV_W]
        acc(PK_G_NORM_CONV, _colsum(dna * xha), CONV_W)
        dxha = dna * gnc_ref[...]
        dy_conv = ra * (dxha - xha * jnp.mean(dxha * xha, axis=-1, keepdims=True))
        du_ref[:, 0:CONV_W] = (dy_conv * conv3).astype(BF16)
        dc = dy_conv * gate_b
        cw = cw_ref[...]
        dcn = dc_next[...]
        dcv = cw[2:3] * dc + cw[1:2] * _shift_up(dc, 1, dcn) + cw[0:1] * _shift_up(dc, 2, dcn)
        dc_next[...] = dc[:TILE_ROWS]
        acc(PK_CONV_W, _colsum(dc * _shift_down(cv, 2, cv_prev)), CONV_W, 0)
        acc(PK_CONV_W, _colsum(dc * _shift_down(cv, 1, cv_prev)), CONV_W, 1)
        acc(PK_CONV_W, _colsum(dc * cv), CONV_W, 2)
        du_ref[:, CONV_W:2 * CONV_W] = (dcv * v).astype(BF16)
        du_ref[:, 2 * CONV_W:3 * CONV_W] = (dcv * gate_c).astype(BF16)

        hs = hs_ref[...]
        gelu, dgelu = _gelu_and_grad(g)
        y_rnn = hs * gelu
        rb = lax.rsqrt(jnp.mean(y_rnn * y_rnn, axis=-1, keepdims=True) + EPS)
        xhb = y_rnn * rb
        dnb = dy_ref[:, CONV_W:]
        acc(PK_G_NORM_RNN, _colsum(dnb * xhb))
        dxhb = dnb * gnr_ref[...]
        dy_rnn = rb * (dxhb - xhb * jnp.mean(dxhb * xhb, axis=-1, keepdims=True))
        du_ref[:, 3 * CONV_W + LRU_W:] = (dy_rnn * hs * dgelu).astype(BF16)
        dh = dy_rnn * gelu

        xr = xr_ref[...]
        xrb = xr.astype(BF16)
        sp, dsp = _softplus_neg(lam_ref[...])
        r, ig, a, mult = [gates_ref[:, n * LRU_W:(n + 1) * LRU_W] for n in range(4)]
        a_up = _shift_up(a, 1, a_next[...])
        a_next[...] = a[:TILE_ROWS]
        gs = _scan_rows(a_up, dh, gs_next[0:1, :], reverse=True)
        gs_next[...] = gs[:TILE_ROWS]
        da = gs * _shift_down(hs, 1, hs_prev)
        gx = gs * xr
        di = gx * mult
        dmult = gx * ig
        dxr = gs * (mult * ig)
        dlog_a = da * a - dmult * ((a * a) / mult)
        acc(PK_LAMBDA, _colsum(dlog_a * r) * ((-LRU_C) * dsp))
        dpa = (dlog_a * ((-LRU_C) * sp)) * (r * (1.0 - r))
        dpx = di * (ig * (1.0 - ig))
        acc(PK_B_A, _colsum(dpa))
        acc(PK_B_X, _colsum(dpx))
        dpab = dpa.astype(BF16)
        dpxb = dpx.astype(BF16)
        dxr = dxr + _block_diag_dot_t(dpab, wa_ref) + _block_diag_dot_t(dpxb, wx_ref)
        for j in range(N_BD):
            cols = slice(j * BD, (j + 1) * BD)
            dwa_ref[j] += _dot_tn(xrb[:, cols], dpab[:, cols])
            dwx_ref[j] += _dot_tn(xrb[:, cols], dpxb[:, cols])

        acc(PK_RCONV_B, _colsum(dxr))
        rw = rw_ref[...]
        dxn = dxr_next[...]
        dx_r = (rw[3:4] * dxr + rw[2:3] * _shift_up(dxr, 1, dxn) + rw[1:2] * _shift_up(dxr, 2, dxn)
                + rw[0:1] * _shift_up(dxr, 3, dxn))
        dxr_next[...] = dxr[:TILE_ROWS]
        for k in range(3):
            acc(PK_RCONV_W, _colsum(dxr * _shift_down(x_r, 3 - k, xin_prev)), LRU_W, k)
        acc(PK_RCONV_W, _colsum(dxr * x_r), LRU_W, 3)
        du_ref[:, 3 * CONV_W:3 * CONV_W + LRU_W] = dx_r.astype(BF16)

        @pl.when(i == nt - 1)
        def _():
            exchange.wait()

    def full(a):
        nd = a.ndim
        return pl.BlockSpec(a.shape, lambda i: (0,) * nd)

    def tok(cols):
        return pl.BlockSpec((tm, cols), lambda i: (nt - 1 - i, 0))

    def halo(cols):
        return pl.BlockSpec((TILE_ROWS, cols), lambda i: (jnp.maximum((nt - 1 - i) * hb - 1, 0), 0))

    smalls = (conv_w, rconv_w, wa_bd, wx_bd, lam, g_nc, g_nr)
    outs = pl.pallas_call(
        body, name="mix_bwd", grid=(nt,),
        in_specs=[tok(CONV_W + LRU_W), tok(IN_COLS), halo(IN_COLS), tok(LRU_W), tok(LRU_W), halo(LRU_W), tok(CONV_W)]
        + [tok(4 * LRU_W)] + [full(a) for a in smalls] + [ANY] * npart,
        out_specs=[tok(IN_COLS), pl.BlockSpec((8 * TILE_ROWS, LRU_W), lambda i: (0, 0)),
                   pl.BlockSpec((N_BD, BD, BD), lambda i: (0, 0, 0)), pl.BlockSpec((N_BD, BD, BD), lambda i: (0, 0, 0))]
        + [ANY] * npart,
        out_shape=[jax.ShapeDtypeStruct((t, IN_COLS), BF16), jax.ShapeDtypeStruct((8 * TILE_ROWS, LRU_W), F32),
                   jax.ShapeDtypeStruct((N_BD, BD, BD), F32), jax.ShapeDtypeStruct((N_BD, BD, BD), F32)]
        + [jax.ShapeDtypeStruct(a.shape, a.dtype) for a in parts],
        scratch_shapes=[pltpu.VMEM((TILE_ROWS, CONV_W), F32), pltpu.VMEM((TILE_ROWS, LRU_W), F32),
                        pltpu.VMEM((TILE_ROWS, LRU_W), F32), pltpu.VMEM((TILE_ROWS, LRU_W), F32),
                        pltpu.SemaphoreType.DMA((npart, 3)), pltpu.SemaphoreType.DMA((npart, 3))],
        compiler_params=_params(dimension_semantics=("arbitrary",)),
    )(dy, u, u, xr_all, hs_all, hs_all, c3_all, gates, *smalls, *parts)
    return outs[:4], outs[4:]


def _in_bwd(dub, w_in_g, x, dx2, g1, parts, joins, core_chip):
    t, d = x.shape
    tm = min(t, MATMUL_TOKEN_TILE)
    nt = t // tm
    npart = len(parts)
    nj = len(joins)
    geometry = []
    for tag, shape, _, _ in joins:
        pr, pc = WGRAD_GEOMETRY[tag][:2]
        every = 1 if pr % (nt * 16) == 0 else 2
        geometry.append((pr, pc, pr * every // nt, every, shape[1] == pc))

    def body(cc_ref, du_ref, win_ref, x_ref, dx2_ref, g1_ref, *rest):
        sums, rest = [rest[4 * w:4 * w + 4] for w in range(nj)], rest[4 * nj:]
        part_refs, (gx_ref, st_ref), rest = rest[:npart], rest[npart:npart + 2], rest[npart + 2:]
        arrived_refs, joined, rest = rest[:npart], rest[npart:npart + nj], rest[npart + nj:]
        stages, (send_sems, recv_sems, j_local, j_send, j_recv) = rest[:nj], rest[nj:]
        exchange = _PartialExchange(part_refs, arrived_refs, send_sems, recv_sems)
        i = pl.program_id(0)
        c = cc_ref[0]

        def window(w, core, row0, rows):
            pr, pc, _, _, by_rows = geometry[w]
            if by_rows:
                return joined[w].at[pl.ds(core * pr + row0, rows), :]
            return joined[w].at[pl.ds(row0, rows), pl.ds(core * pc, pc)]

        def to_sibling(w, src, core, row0, rows):
            return pltpu.make_async_remote_copy(src_ref=src, dst_ref=window(w, core, row0, rows), send_sem=j_send.at[w],
                                                recv_sem=j_recv.at[w], device_id=_sibling(), device_id_type=MESH)

        @pl.when(i == 0)
        def _():
            exchange.start()
            st_ref[...] = jnp.zeros_like(st_ref)

        for w in range(nj):
            pr, pc, rb, every, _ = geometry[w]

            @pl.when(i % every == 0)
            def _(w=w, rb=rb, every=every):
                p_ref, r1_ref, r2_ref, r3_ref = sums[w]
                row0 = pl.multiple_of((i // every) * rb, rb)
                rows = stages[w].at[pl.ds(row0, rb), :]
                rows[...] = ((p_ref[0] + r1_ref[0].astype(F32)) + r2_ref[0].astype(F32)) + r3_ref[0].astype(F32)
                pltpu.make_async_copy(rows, window(w, c, row0, rb), j_local.at[w]).start()
                to_sibling(w, rows, c, row0, rb).start()

        dh1 = _dot_nt(du_ref[:, 0:IN_SHARD], win_ref[0])
        for j in range(1, N_CHIPS):
            dh1 = dh1 + _dot_nt(du_ref[:, j * IN_SHARD:(j + 1) * IN_SHARD], win_ref[j])
        xv = x_ref[...]
        rstd = lax.rsqrt(jnp.mean(xv * xv, axis=-1, keepdims=True) + EPS)
        xh = xv * rstd
        st_ref[0:1, :] += _colsum(dh1 * xh)
        dxh = dh1 * g1_ref[...]
        gx_ref[...] = dx2_ref[...] + rstd * (dxh - xh * jnp.mean(dxh * xh, axis=-1, keepdims=True))

        @pl.when(i == nt - 1)
        def _():
            exchange.wait()
            for w in range(nj):
                pr = geometry[w][0]
                pltpu.make_async_copy(stages[w], window(w, c, 0, pr), j_local.at[w]).wait()
                to_sibling(w, stages[w], 1 - c, 0, pr).wait()

    def tok(cols):
        return pl.BlockSpec((tm, cols), lambda i, cc: (i, 0))

    def partial(w, off):
        pr, pc, rb, every, _ = geometry[w]
        return pl.BlockSpec((1, rb, pc), lambda i, cc: ((cc[1] + off) % N_CHIPS, i // every, 0))

    sum_specs, sum_operands = [], []
    for w, (_, _, own, arrived) in enumerate(joins):
        sum_specs += [partial(w, off) for off in range(N_CHIPS)]
        sum_operands += [own, arrived, arrived, arrived]
    dma = pltpu.SemaphoreType.DMA
    outs = pl.pallas_call(
        body, name="in_bwd",
        grid_spec=pltpu.PrefetchScalarGridSpec(
            num_scalar_prefetch=1, grid=(nt,),
            in_specs=[tok(IN_COLS), pl.BlockSpec(w_in_g.shape, lambda i, cc: (0, 0, 0)), tok(d), tok(d),
                      pl.BlockSpec((1, d), lambda i, cc: (0, 0))] + sum_specs + [ANY] * npart,
            out_specs=[tok(d), pl.BlockSpec((TILE_ROWS, d), lambda i, cc: (0, 0))] + [ANY] * (npart + nj),
            scratch_shapes=[pltpu.VMEM((g[0], g[1]), F32) for g in geometry]
            + [dma((npart, 3)), dma((npart, 3)), dma((nj,)), dma((nj,)), dma((nj,))]),
        out_shape=[jax.ShapeDtypeStruct((t, d), F32), jax.ShapeDtypeStruct((TILE_ROWS, d), F32)]
        + [jax.ShapeDtypeStruct(a.shape, a.dtype) for a in parts]
        + [jax.ShapeDtypeStruct(shape, F32) for _, shape, _, _ in joins],
        compiler_params=_params(dimension_semantics=("arbitrary",)),
    )(core_chip, dub, w_in_g, x, dx2, g1, *sum_operands, *parts)
    return outs[:2], outs[2:2 + npart], outs[2 + npart:]


WGRAD_GEOMETRY = {
    "in": (512, IN_SHARD, lambda s, h: h, lambda s, h: s),
    "mlp_in": (512, D_MODEL, lambda s, h: h, lambda s, h: s),
    "mlp_out": (512, D_MODEL, lambda s, h: 2 * s + h, lambda s, h: 0),
    "out": (384, 512, lambda s, h: s, lambda s, h: h),
}
K_CHUNK = 512


def _sibling():
    x, y, c = _position()
    return (x, y, 1 - c)


def _wgrad(a, b, tag, core_chip, packs=(), parts=()):
    t = a.shape[0]
    pr, pc, a_blk, b_blk = WGRAD_GEOMETRY[tag]
    nk = t // K_CHUNK
    mine = N_CHIPS
    riding = len(packs)
    npart = len(parts)
    assert not (riding and npart)

    def body(cc_ref, a_ref, b_ref, *rest):
        if riding:
            pack_refs, (land_ref, p_ref, pb_ref), rest = rest[:riding], rest[riding:riding + 3], rest[riding + 3:]
            all_refs, (stage, rbuf, send_sems, recv_sems, rsem), g_sems = rest[:riding], rest[riding:riding + 5], rest[riding + 5:]
            gathers = [_PackGather(pack_refs[n], all_refs[n], *g_sems[3 * n:3 * n + 3]) for n in range(riding)]
        elif npart:
            part_refs, (land_ref, p_ref, pb_ref), rest = rest[:npart], rest[npart:npart + 3], rest[npart + 3:]
            arrived_refs, (stage, rbuf, send_sems, recv_sems, rsem, x_send, x_recv) = rest[:npart], rest[npart:]
            exchange = _PartialExchange(part_refs, arrived_refs, x_send, x_recv)
        else:
            land_ref, p_ref, pb_ref, stage, rbuf, send_sems, recv_sems, rsem = rest
        ph, s = pl.program_id(0), pl.program_id(1)
        if riding:
            @pl.when((ph == 0) & (s == 0))
            def _():
                for gather in gathers:
                    gather.start()

            @pl.when((ph == 1) & (s == N_CHIPS - 2))
            def _():
                for gather in gathers:
                    gather.hand_over()
        if npart:
            @pl.when((ph == 0) & (s == 0))
            def _():
                exchange.start()
        def push(k):
            return pltpu.make_async_remote_copy(src_ref=stage.at[k], dst_ref=land_ref.at[k], send_sem=send_sems.at[k],
                                                recv_sem=recv_sems.at[k], device_id=_sibling(), device_id_type=MESH)

        def landed():
            return pltpu.make_async_copy(land_ref.at[s], rbuf, rsem)

        @pl.when(ph == 1)
        def _():
            push(s).wait_recv()
            landed().start()

        slot = jnp.where(ph == 0, s, mine)
        acc = stage.at[slot]
        acc[...] = _dot_tn(a_ref[0:K_CHUNK, :], b_ref[0:K_CHUNK, :])
        for k in range(1, nk):
            acc[...] += _dot_tn(a_ref[k * K_CHUNK:(k + 1) * K_CHUNK, :], b_ref[k * K_CHUNK:(k + 1) * K_CHUNK, :])

        @pl.when(ph == 0)
        def _():
            push(s).start()

        @pl.when(ph == 1)
        def _():
            landed().wait()
            p = stage[mine] + rbuf[...]
            p_ref[0] = p
            pb_ref[0] = p.astype(BF16)

        @pl.when((ph == 1) & (s == N_CHIPS - 1))
        def _():
            for k in range(N_CHIPS):
                push(k).wait_send()
            for gather in (gathers if riding else ()):
                gather.finish()
            if npart:
                exchange.wait()

    def half(ph, cc):
        return jnp.where(ph == 0, 1 - cc[0], cc[0])

    def out_slot(ph, s, cc):
        return (jnp.where(ph == 0, 0, s), 0, 0)

    piece = jax.ShapeDtypeStruct((N_CHIPS, pr, pc), F32)
    in_specs = [pl.BlockSpec((t, pr), lambda ph, s, cc: (0, a_blk(s, half(ph, cc)))),
                pl.BlockSpec((t, pc), lambda ph, s, cc: (0, b_blk(s, half(ph, cc))))]
    out_specs = [ANY, pl.BlockSpec((1, pr, pc), out_slot), pl.BlockSpec((1, pr, pc), out_slot)]
    out_shape = [piece, piece, jax.ShapeDtypeStruct((N_CHIPS, pr, pc), BF16)]
    scratch = [pltpu.VMEM((N_CHIPS + 1, pr, pc), F32), pltpu.VMEM((pr, pc), F32),
               pltpu.SemaphoreType.DMA((N_CHIPS,)), pltpu.SemaphoreType.DMA((N_CHIPS,)), pltpu.SemaphoreType.DMA]
    operands = [a, b]
    for pack in packs:
        in_specs.append(pl.BlockSpec(pack.shape, lambda ph, s, cc, nd=pack.ndim: (0,) * nd))
        out_specs.append(ANY)
        out_shape.append(jax.ShapeDtypeStruct((N_DEVICES,) + pack.shape, pack.dtype))
        operands.append(pack)
    for pack in packs:
        scratch += _PackGather.semaphores()
    if npart:
        in_specs += [ANY] * npart
        out_specs += [ANY] * npart
        out_shape += [jax.ShapeDtypeStruct(p.shape, p.dtype) for p in parts]
        scratch += [pltpu.SemaphoreType.DMA((npart, 3)), pltpu.SemaphoreType.DMA((npart, 3))]
        operands += list(parts)
    return pl.pallas_call(
        body, name="wgrad_" + tag,
        grid_spec=pltpu.PrefetchScalarGridSpec(
            num_scalar_prefetch=1, grid=(2, N_CHIPS), in_specs=in_specs, out_specs=out_specs, scratch_shapes=scratch),
        out_shape=out_shape,
        compiler_params=_params(dimension_semantics=("arbitrary", "arbitrary")),
    )(core_chip, *operands)[1:]


def _other_chips(x, y):
    return [(1 - x, y), (x, 1 - y), (1 - x, 1 - y)]


class _ShardGather:
    PAIRS = 9

    def __init__(self, outs, send_sems, recv_sems):
        self.outs, self.send_sems, self.recv_sems = outs, send_sems, recv_sems
        x, y, c = _position()
        self.c, self.j = c, 2 * x + y
        self.sibling = (x, y, 1 - c)
        self.chips = _other_chips(x, y)

    def _chip(self, k):
        px, py = self.chips[k]
        return 2 * px + py

    def _half(self, w, chip, which):
        hr = self.outs[w].shape[1] // 2
        return self.outs[w].at[chip, pl.ds(which * hr, hr), :]

    def _quarter(self, w, chip, q):
        qr = self.outs[w].shape[1] // 4
        return self.outs[w].at[chip, pl.ds(self.c * 2 * qr + q * qr, qr), :]

    def _copy(self, ref, w, pair, to, src=None):
        return pltpu.make_async_remote_copy(src_ref=ref if src is None else src, dst_ref=ref, send_sem=self.send_sems.at[w, pair],
                                            recv_sem=self.recv_sems.at[w, pair], device_id=to, device_id_type=MESH)

    def direct(self, w, k, q, src=None):
        return self._copy(self._quarter(w, self.j, q), w, 2 * k + q, (*self.chips[k], self.c), src)

    def direct_landed(self, w, k, q):
        return self._copy(self._quarter(w, self._chip(k), q), w, 2 * k + q, (*self.chips[k], self.c))

    def pass_on(self, w, q):
        return self._copy(self._quarter(w, self._chip(q), q), w, 4 + q, (*self.chips[1 - q], self.c))

    def passed_landed(self, w, q):
        return self._copy(self._quarter(w, self._chip(2), q), w, 4 + q, (*self.chips[1 - q], self.c))

    def hand_over(self, w, k):
        return self._copy(self._half(w, self._chip(k), self.c), w, 6 + k, self.sibling)

    def handed(self, w, k):
        return self._copy(self._half(w, self._chip(k), 1 - self.c), w, 6 + k, self.sibling)

    def start_direct(self, w, src_half=None):
        qr = self.outs[w].shape[1] // 4
        for k, q in ((0, 0), (1, 1), (0, 1), (1, 0)):
            self.direct(w, k, q, None if src_half is None else src_half.at[pl.ds(q * qr, qr), :]).start()

    def start_pass_on(self, w):
        for q in (0, 1):
            self.direct_landed(w, q, q).wait_recv()
            self.pass_on(w, q).start()

    def start_hand_over(self, w, diagonal):
        if diagonal:
            for q in (0, 1):
                self.passed_landed(w, q).wait_recv()
            self.hand_over(w, 2).start()
        else:
            for k in (0, 1):
                self.direct_landed(w, k, 1 - k).wait_recv()
                self.hand_over(w, k).start()

    def finish(self, w):
        for k in range(3):
            self.handed(w, k).wait_recv()
            self.hand_over(w, k).wait_send()
        for q in (0, 1):
            self.pass_on(w, q).wait_send()
            for k in (0, 1):
                self.direct(w, k, q).wait_send()


def _gather_first(w_in, w_out, w1, w2, small):
    bigs = (w_in, w_out, w1, w2)
    nb = len(bigs)

    def body(win_ref, wout_ref, w1_ref, w2_ref, sm_ref, gin, gout, g1, g2, gsm, st_in, st_out, st_1, st_2,
             send_sems, recv_sems, sm_send, sm_recv, local_sems):
        srcs = (win_ref, wout_ref, w1_ref, w2_ref)
        stages = (st_in, st_out, st_1, st_2)
        outs = (gin, gout, g1, g2)
        plan = _ShardGather(outs[:1], send_sems, recv_sems)
        j, c = plan.j, plan.c
        local = [pltpu.make_async_copy(stages[w], outs[w].at[j], local_sems.at[w]) for w in range(nb)]
        local.append(pltpu.make_async_copy(sm_ref, gsm.at[j], local_sems.at[nb]))

        def small_copy(k):
            px, py = plan.chips[k]
            return pltpu.make_async_remote_copy(src_ref=sm_ref, dst_ref=gsm.at[j], send_sem=sm_send.at[k],
                                                recv_sem=sm_recv.at[k], device_id=(px, py, c), device_id_type=MESH)

        def small_landed(k):
            px, py = plan.chips[k]
            return pltpu.make_async_remote_copy(src_ref=sm_ref, dst_ref=gsm.at[2 * px + py], send_sem=sm_send.at[k],
                                                recv_sem=sm_recv.at[k], device_id=(px, py, c), device_id_type=MESH)

        hr = w_in.shape[0] // 2
        st_in[...] = win_ref[...].astype(BF16)
        plan.start_direct(0, st_in.at[pl.ds(c * hr, hr), :])
        for k in range(3):
            small_copy(k).start()
        for src, st in zip(srcs[1:], stages[1:]):
            st[...] = src[...].astype(BF16)
        for cp in local:
            cp.start()
        plan.start_pass_on(0)
        plan.start_hand_over(0, diagonal=False)
        plan.start_hand_over(0, diagonal=True)
        for k in range(3):
            small_landed(k).wait_recv()
            small_copy(k).wait_send()
        plan.finish(0)
        for cp in local:
            cp.wait()

    def gathered(a, dtype):
        return jax.ShapeDtypeStruct((N_CHIPS,) + a.shape, dtype)

    return pl.pallas_call(
        body, name="gather_first",
        in_specs=[VMEM] * 5, out_specs=[ANY] * 5,
        out_shape=[gathered(a, BF16) for a in bigs] + [gathered(small, F32)],
        scratch_shapes=[pltpu.VMEM(a.shape, BF16) for a in bigs]
        + [pltpu.SemaphoreType.DMA((1, _ShardGather.PAIRS)), pltpu.SemaphoreType.DMA((1, _ShardGather.PAIRS)), pltpu.SemaphoreType.DMA((3,)),
           pltpu.SemaphoreType.DMA((3,)), pltpu.SemaphoreType.DMA((nb + 1,))],
        compiler_params=_params(),
    )(*bigs, small)


class _PartialExchange:
    def __init__(self, parts, arrived, send_sems, recv_sems):
        self.parts, self.arrived, self.send_sems, self.recv_sems = parts, arrived, send_sems, recv_sems
        x, y, c = _position()
        self.c, self.j = c, 2 * x + y
        self.chips = _other_chips(x, y)

    def _copy(self, w, k, slot):
        px, py = self.chips[k]
        return pltpu.make_async_remote_copy(
            src_ref=self.parts[w].at[2 * px + py], dst_ref=self.arrived[w].at[slot], send_sem=self.send_sems.at[w, k],
            recv_sem=self.recv_sems.at[w, k], device_id=(px, py, self.c), device_id_type=MESH)

    def start(self):
        for w in range(len(self.parts)):
            for k in range(3):
                self._copy(w, k, self.j).start()

    def wait(self):
        for w in range(len(self.parts)):
            for k in range(3):
                px, py = self.chips[k]
                self._copy(w, k, 2 * px + py).wait()


class _PackGather:
    def __init__(self, p_ref, all_ref, send_sems, recv_sems, local_sem):
        self.p_ref, self.all_ref, self.send_sems, self.recv_sems, self.local_sem = p_ref, all_ref, send_sems, recv_sems, local_sem
        x, y, c = _position()
        self.me, self.sibling, self.c = (x, y, c), (x, y, 1 - c), c
        self.chips = _other_chips(x, y)

    @staticmethod
    def semaphores():
        return [pltpu.SemaphoreType.DMA((7,)), pltpu.SemaphoreType.DMA((7,)), pltpu.SemaphoreType.DMA]

    def _copy(self, k, block, to, from_pack=False):
        px, py, pc = block
        slot = self.all_ref.at[4 * px + 2 * py + pc]
        return pltpu.make_async_remote_copy(src_ref=self.p_ref if from_pack else slot, dst_ref=slot, send_sem=self.send_sems.at[k],
                                            recv_sem=self.recv_sems.at[k], device_id=to, device_id_type=MESH)

    def _mine(self):
        x, y, c = self.me
        return pltpu.make_async_copy(self.p_ref, self.all_ref.at[4 * x + 2 * y + c], self.local_sem)

    def _first(self):
        return [self._copy(0, self.me, self.sibling, True)] + [
            self._copy(1 + k, self.me, (*chip, self.c), True) for k, chip in enumerate(self.chips)]

    def _passed(self):
        return [self._copy(4 + k, (*chip, self.c), self.sibling) for k, chip in enumerate(self.chips)]

    def start(self):
        self._mine().start()
        for cp in self._first():
            cp.start()

    def hand_over(self):
        for k, chip in enumerate(self.chips):
            self._copy(1 + k, (*chip, self.c), self.me).wait_recv()
            self._passed()[k].start()

    def finish(self):
        self._copy(0, self.sibling, self.me).wait_recv()
        for k, chip in enumerate(self.chips):
            self._copy(4 + k, (*chip, 1 - self.c), self.me).wait_recv()
        for cp in self._first() + self._passed():
            cp.wait_send()
        self._mine().wait()


class _DirectGather:
    def __init__(self, p_ref, all_ref, send_sems, recv_sems, local_sem):
        self.p_ref, self.all_ref, self.send_sems, self.recv_sems, self.local_sem = p_ref, all_ref, send_sems, recv_sems, local_sem
        self.me = _position()

    semaphores = _PackGather.semaphores

    def _peer(self, r):
        x, y, c = self.me
        return ((1 - x) if r & 4 else x, (1 - y) if r & 2 else y, (1 - c) if r & 1 else c)

    def _copy(self, r, slot_of):
        px, py, pc = slot_of
        return pltpu.make_async_remote_copy(src_ref=self.p_ref, dst_ref=self.all_ref.at[4 * px + 2 * py + pc],
                                            send_sem=self.send_sems.at[r - 1], recv_sem=self.recv_sems.at[r - 1],
                                            device_id=self._peer(r), device_id_type=MESH)

    def _mine(self):
        x, y, c = self.me
        return pltpu.make_async_copy(self.p_ref, self.all_ref.at[4 * x + 2 * y + c], self.local_sem)

    def start(self):
        self._mine().start()
        for r in range(1, N_DEVICES):
            self._copy(r, self.me).start()

    def finish(self):
        for r in range(1, N_DEVICES):
            self._copy(r, self._peer(r)).wait()
        self._mine().wait()


def _adamw(w, g, m, v):
    m = ADAM_B1 * m + (1.0 - ADAM_B1) * g
    v = ADAM_B2 * v + (1.0 - ADAM_B2) * (g * g)
    m_hat = m / ADAM_BC1
    v_hat = v / ADAM_BC2
    delta = -ADAM_LR * (m_hat / (jnp.sqrt(v_hat) + ADAM_EPS) + ADAM_WD * w)
    return delta, m, v


JOIN_SUB = 4


def _join(tag, shard_shape, part, arrived, core_chip, block=None):
    pr, pc = WGRAD_GEOMETRY[tag][:2]
    rb = pr // JOIN_SUB
    by_rows = shard_shape[1] == pc
    riding = block is not None

    def body(cc_ref, p_ref, r1_ref, r2_ref, r3_ref, *rest):
        if riding:
            blk_ref, g_ref, all_ref, stage, send_sems, recv_sems, local_sems, b_send, b_recv, b_local = rest
            gather = _DirectGather(blk_ref, all_ref, b_send, b_recv, b_local)
        else:
            g_ref, stage, send_sems, recv_sems, local_sems = rest
        i = pl.program_id(0)
        c = cc_ref[0]
        if riding:
            @pl.when(i == 0)
            def _():
                gather.start()

        def window(core, k):
            if by_rows:
                return g_ref.at[pl.ds((core * JOIN_SUB + k) * rb, rb), :]
            return g_ref.at[pl.ds(k * rb, rb), pl.ds(core * pc, pc)]

        def keep(k):
            return pltpu.make_async_copy(stage.at[k], window(c, k), local_sems.at[k])

        def push(k):
            return pltpu.make_async_remote_copy(src_ref=stage.at[k], dst_ref=window(c, k), send_sem=send_sems.at[k],
                                                recv_sem=recv_sems.at[k], device_id=_sibling(), device_id_type=MESH)

        def pushed(k):
            return pltpu.make_async_remote_copy(src_ref=stage.at[k], dst_ref=window(1 - c, k), send_sem=send_sems.at[k],
                                                recv_sem=recv_sems.at[k], device_id=_sibling(), device_id_type=MESH)

        stage[i] = ((p_ref[0] + r1_ref[0].astype(F32)) + r2_ref[0].astype(F32)) + r3_ref[0].astype(F32)
        keep(i).start()
        push(i).start()

        @pl.when(i == JOIN_SUB - 1)
        def _():
            for k in range(JOIN_SUB):
                keep(k).wait()
                push(k).wait_send()
                pushed(k).wait_recv()
            if riding:
                gather.finish()

    def partial(off):
        return pl.BlockSpec((1, rb, pc), lambda i, cc: ((cc[1] + off) % N_CHIPS, i, 0))

    in_specs = [partial(0), partial(1), partial(2), partial(3)]
    out_specs = [ANY]
    out_shape = [jax.ShapeDtypeStruct(shard_shape, F32)]
    scratch = [pltpu.VMEM((JOIN_SUB, rb, pc), F32), pltpu.SemaphoreType.DMA((JOIN_SUB,)),
               pltpu.SemaphoreType.DMA((JOIN_SUB,)), pltpu.SemaphoreType.DMA((JOIN_SUB,))]
    operands = [part, arrived, arrived, arrived]
    if riding:
        in_specs.append(pl.BlockSpec(block.shape, lambda i, cc: (0, 0)))
        out_specs.append(ANY)
        out_shape.append(jax.ShapeDtypeStruct((N_DEVICES,) + block.shape, block.dtype))
        scratch += _DirectGather.semaphores()
        operands.append(block)
    outs = pl.pallas_call(
        body, name="join_" + tag,
        grid_spec=pltpu.PrefetchScalarGridSpec(
            num_scalar_prefetch=1, grid=(JOIN_SUB,), in_specs=in_specs, out_specs=out_specs, scratch_shapes=scratch),
        out_shape=out_shape,
        compiler_params=_params(dimension_semantics=("arbitrary",)),
    )(core_chip, *operands)
    return outs if riding else outs[0]


def _adamw_big(w, g, m, v, name):
    rows, cols = w.shape
    rb = 256 if rows % 256 == 0 else rows

    def body(w_ref, g_ref, m_ref, v_ref, go_ref, d_ref, nm_ref, nv_ref):
        g = g_ref[...]
        go_ref[...] = g
        d_ref[...], nm_ref[...], nv_ref[...] = _adamw(w_ref[...], g, m_ref[...], v_ref[...])

    spec = pl.BlockSpec((rb, cols), lambda i: (i, 0))
    return pl.pallas_call(
        body, name=name, grid=(rows // rb,), in_specs=[spec] * 4, out_specs=[spec] * 4,
        out_shape=[jax.ShapeDtypeStruct(w.shape, F32)] * 4,
        compiler_params=_params(dimension_semantics=("arbitrary",)),
    )(w, g, m, v)


SMALL_VECTORS = {
    "norm_mix_g": (PK_MIX_G, D_MODEL), "rnn_conv_b": (PK_RCONV_B, LRU_W), "b_a": (PK_B_A, LRU_W), "b_x": (PK_B_X, LRU_W),
    "lru_lambda": (PK_LAMBDA, LRU_W), "g_norm_conv": (PK_G_NORM_CONV, CONV_W), "g_norm_rnn": (PK_G_NORM_RNN, LRU_W),
    "norm_mlp_g": (PK_MLP_G, D_MODEL), "final_norm_g": (PK_FINAL_G, D_MODEL),
}
SMALL_MATRICES = ("w_a", "w_x")


def _small_step(vec_packs, mat_packs, mix_g_blocks, w_pack, m_pack, v_pack, mats_wmv, conv_wmv, rconv_wmv):
    rows, cols = vec_packs.shape[1:]
    conv_rows, cshard = 3, conv_wmv.shape[2]
    rconv_rows, rshard = 4, rconv_wmv.shape[2]
    mix_row = PK_MIX_G * TILE_ROWS
    names = list(SMALL_VECTORS) + list(SMALL_MATRICES) + ["conv_w", "rnn_conv_w"]
    shapes = ([(1, width) for _, width in SMALL_VECTORS.values()] + [mat_packs.shape[2:]] * len(SMALL_MATRICES)
              + [(conv_rows, cshard), (rconv_rows, rshard)])

    def body(vec_ref, mat_ref, blk_ref, w_ref, m_ref, v_ref, mw_ref, cw_ref, rw_ref, loss_ref, *rest):
        leaves, g_ref = [rest[k * len(names):(k + 1) * len(names)] for k in range(4)], rest[4 * len(names)]
        total = vec_ref[0]
        mats = mat_ref[0].astype(F32)
        late = blk_ref[0]
        for k in range(1, N_DEVICES):
            total = total + vec_ref[k]
            mats = mats + mat_ref[k].astype(F32)
            late = late + blk_ref[k]
        g_ref[...] = total
        g_ref[mix_row:mix_row + TILE_ROWS, :] = late
        g = g_ref[...]
        loss_ref[...] = g[PK_LOSS * TILE_ROWS:PK_LOSS * TILE_ROWS + 1, 0:1]

        x, y, _ = _position()
        j = 2 * x + y
        cblk = total[PK_CONV_W * 8:PK_CONV_W * 8 + 8, :]
        rblk = total[PK_RCONV_W * 8:PK_RCONV_W * 8 + 8, :]
        cg = cblk[:, 0:cshard]
        rg = rblk[:, 0:rshard]
        for k in range(1, N_CHIPS):
            cg = jnp.where(j == k, cblk[:, k * cshard:(k + 1) * cshard], cg)
            rg = jnp.where(j == k, rblk[:, k * rshard:(k + 1) * rshard], rg)

        packs = (g,) + _adamw(w_ref[...], g, m_ref[...], v_ref[...])
        matrices = (mats,) + _adamw(mw_ref[0], mats, mw_ref[1], mw_ref[2])
        convs = (cg,) + _adamw(cw_ref[0], cg, cw_ref[1], cw_ref[2])
        rconvs = (rg,) + _adamw(rw_ref[0], rg, rw_ref[1], rw_ref[2])
        for kind in range(4):
            out = dict(zip(names, leaves[kind]))
            for name, (block, width) in SMALL_VECTORS.items():
                out[name][...] = packs[kind][block * TILE_ROWS:block * TILE_ROWS + 1, 0:width]
            for n, name in enumerate(SMALL_MATRICES):
                out[name][...] = matrices[kind][n]
            out["conv_w"][...] = convs[kind][0:conv_rows, :]
            out["rnn_conv_w"][...] = rconvs[kind][0:rconv_rows, :]

    outs = pl.pallas_call(
        body, name="small_grads_step", in_specs=[VMEM] * 9, out_specs=[VMEM] * (1 + 4 * len(names)),
        out_shape=[jax.ShapeDtypeStruct((1, 1), F32)] + [jax.ShapeDtypeStruct(sh, F32) for sh in shapes] * 4,
        scratch_shapes=[pltpu.VMEM((rows, cols), F32)],
        compiler_params=_params(),
    )(vec_packs, mat_packs, mix_g_blocks, w_pack, m_pack, v_pack, mats_wmv, conv_wmv, rconv_wmv)
    return outs[0], [dict(zip(names, outs[1 + k * len(names):1 + (k + 1) * len(names)])) for k in range(4)]


def _blk(a):
    a = a.reshape(-1, a.shape[-1])
    return jnp.pad(a, ((0, TILE_ROWS - a.shape[0]), (0, D_MODEL - a.shape[1])))


def _zero_blk():
    return jnp.zeros((TILE_ROWS, D_MODEL), F32)


def _pack_params(p, pre):
    get = lambda n: p[pre + n]
    return jnp.concatenate([
        _blk(get("g_norm_rnn")), _blk(get("rnn_conv_b")), _blk(get("b_a")), _blk(get("b_x")), _blk(get("lru_lambda")),
        _zero_blk(), _zero_blk(), _blk(get("g_norm_conv")), _blk(get("final_norm_g").reshape(1, -1)), _blk(get("norm_mlp_g")),
        _zero_blk(), _blk(get("norm_mix_g"))], axis=0)


def _to_block_diag(w):
    w4 = w.reshape(N_BD, 4, 64, 64)
    return jnp.concatenate([jnp.pad(w4[:, q], ((0, 0), (0, 0), (64 * q, 64 * (3 - q)))) for q in range(4)], axis=1)


def _from_block_diag(d):
    d5 = d.reshape(N_BD, 4, HEAD_DIM, 4, HEAD_DIM)
    return jnp.stack([d5[:, q, :, q, :] for q in range(4)], axis=1).reshape(N_HEADS, HEAD_DIM, HEAD_DIM)


def _pad_rows(a):
    return jnp.pad(a, ((0, TILE_ROWS - a.shape[0]), (0, 0)))


_NAMES = ['norm_mix_g', 'w_in', 'conv_w', 'rnn_conv_w', 'rnn_conv_b', 'w_a', 'b_a', 'w_x', 'b_x', 'lru_lambda',
          'g_norm_conv', 'g_norm_rnn', 'w_out', 'norm_mlp_g', 'w_mlp_in', 'w_mlp_out', 'final_norm_g']


def kernel(x, norm_mix_g, w_in, conv_w, rnn_conv_w, rnn_conv_b, w_a, b_a, w_x, b_x, lru_lambda, g_norm_conv, g_norm_rnn, w_out, norm_mlp_g, w_mlp_in, w_mlp_out, final_norm_g, loss_target, m_norm_mix_g, m_w_in, m_conv_w, m_rnn_conv_w, m_rnn_conv_b, m_w_a, m_b_a, m_w_x, m_b_x, m_lru_lambda, m_g_norm_conv, m_g_norm_rnn, m_w_out, m_norm_mlp_g, m_w_mlp_in, m_w_mlp_out, m_final_norm_g, v_norm_mix_g, v_w_in, v_conv_w, v_rnn_conv_w, v_rnn_conv_b, v_w_a, v_b_a, v_w_x, v_b_x, v_lru_lambda, v_g_norm_conv, v_g_norm_rnn, v_w_out, v_norm_mlp_g, v_w_mlp_in, v_w_mlp_out, v_final_norm_g):
    args = dict(locals())
    p = {}
    for n in _NAMES:
        for pre in ("", "m_", "v_"):
            a = args[pre + n]
            p[pre + n] = a[0] if a.ndim >= 3 else a
    xs = x[0]
    target = loss_target[0]
    core_chip = jnp.stack([lax.axis_index("c"), 2 * lax.axis_index("x") + lax.axis_index("y")]).astype(jnp.int32)
    cshard = p["conv_w"].shape[1]
    rshard = p["rnn_conv_w"].shape[1]

    small = jnp.concatenate([_pad_rows(p["conv_w"]), _pad_rows(p["rnn_conv_w"])], axis=1)
    w_in_g, w_out_g, w1_g, w2_g, small_g = _gather_first(p["w_in"], p["w_out"], p["w_mlp_in"], p["w_mlp_out"], small)
    conv_full = small_g[:, :3, :cshard].transpose(1, 0, 2).reshape(3, CONV_W)
    rconv_full = small_g[:, :4, cshard:].transpose(1, 0, 2).reshape(4, LRU_W)
    wa_bd = _to_block_diag(p["w_a"]).astype(BF16)
    wx_bd = _to_block_diag(p["w_x"]).astype(BF16)
    gf = p["final_norm_g"].reshape(1, -1)
    lru = (wa_bd, p["b_a"], wx_bd, p["b_x"], p["lru_lambda"], p["g_norm_conv"], p["g_norm_rnn"])

    (u, h1b, xr, hs, c3, yb, gates), (w_out_g, w1_g, w2_g) = _fwd_mix(
        xs, p["norm_mix_g"], w_in_g, conv_full, rconv_full, p["rnn_conv_b"], *lru, (w_out_g, w1_g, w2_g))
    zb, dpb, h2b, dx3b, dx2, dx2b, dy, st_mlp = _mlp_fwd_bwd(
        xs, yb, w_out_g.reshape(-1, D_MODEL), w1_g, w2_g.reshape(-1, D_MODEL), p["norm_mlp_g"], gf, target)

    part_out = _wgrad(yb, dx2b, "out", core_chip)
    *part_1, arrived_out = _wgrad(h2b, dpb, "mlp_in", core_chip, parts=(part_out[1],))
    part_2 = _wgrad(zb, dx3b, "mlp_out", core_chip)
    (dub, st_mix, dwa_bd, dwx_bd), (arrived_1, arrived_2) = _mix_bwd(
        dy, u, xr, hs, c3, gates, conv_full, rconv_full, wa_bd, wx_bd, p["lru_lambda"], p["g_norm_conv"], p["g_norm_rnn"],
        (part_1[1], part_2[1]))
    arrived_mlp = (arrived_out, arrived_1, arrived_2)
    vec_pack = jnp.concatenate([st_mix, st_mlp, _zero_blk()], axis=0)
    mat_pack = jnp.stack([_from_block_diag(dwa_bd), _from_block_diag(dwx_bd)]).astype(BF16)
    *part_in, vec_packs, mat_packs = _wgrad(h1b, dub, "in", core_chip, packs=(vec_pack, mat_pack))
    early = (("w_out", "out", part_out, arrived_mlp[0]), ("w_mlp_in", "mlp_in", part_1, arrived_mlp[1]),
             ("w_mlp_out", "mlp_out", part_2, arrived_mlp[2]))
    (grad_x, st_in), arrived_in, joined = _in_bwd(
        dub, w_in_g, xs, dx2, p["norm_mix_g"], (part_in[1],),
        [(tag, p[n].shape, part[0], arrived) for n, tag, part, arrived in early], core_chip)
    g_in, mix_g_blocks = _join("in", p["w_in"].shape, part_in[0], arrived_in[0], core_chip, st_in)
    big = {}
    for n, tag, g in [(n, tag, g) for (n, tag, _, _), g in zip(early, joined)] + [("w_in", "in", g_in)]:
        big[n] = _adamw_big(p[n], g, p["m_" + n], p["v_" + n], "adamw_" + tag)

    conv_wmv = jnp.stack([_pad_rows(p[pre + "conv_w"]) for pre in ("", "m_", "v_")])
    rconv_wmv = jnp.stack([_pad_rows(p[pre + "rnn_conv_w"]) for pre in ("", "m_", "v_")])
    mats_wmv = jnp.stack([jnp.stack([p[pre + n] for n in SMALL_MATRICES]) for pre in ("", "m_", "v_")])
    loss, outs = _small_step(vec_packs, mat_packs, mix_g_blocks, _pack_params(p, ""), _pack_params(p, "m_"),
                             _pack_params(p, "v_"), mats_wmv, conv_wmv, rconv_wmv)
    for kind, o in enumerate(outs):
        o["final_norm_g"] = o["final_norm_g"].reshape(-1)
        for n in SMALL_MATRICES + ("conv_w", "rnn_conv_w"):
            o[n] = o[n][None]
        for n in ("w_in", "w_out", "w_mlp_in", "w_mlp_out"):
            o[n] = big[n][kind][None]
    loss = loss.reshape(())
    return (loss, grad_x[None], *[o[n] for o in outs for n in _NAMES])
```

```python
import functools
import math

import jax
import jax.numpy as jnp
from jax import lax
from jax.experimental import pallas as pl
from jax.experimental.pallas import tpu as pltpu

F32 = jnp.float32
BF16 = jnp.bfloat16
MESH = pl.DeviceIdType.MESH
ANY = pl.BlockSpec(memory_space=pl.ANY)
VMEM = pl.BlockSpec(memory_space=pltpu.VMEM)

EPS = 1e-6
LRU_C = 8.0
D_MODEL = 1024
CONV_W = 512
LRU_W = 1024
IN_COLS = 3 * CONV_W + 2 * LRU_W
IN_SHARD = IN_COLS // 4
N_CHIPS = 4
N_DEVICES = 8
BD = 256
N_BD = LRU_W // BD

ADAM_LR = 0.001
ADAM_B1 = 0.9
ADAM_B2 = 0.999
ADAM_EPS = 1e-08
ADAM_WD = 0.01
ADAM_STEP = 10
ADAM_BC1 = 1.0 - ADAM_B1 ** ADAM_STEP
ADAM_BC2 = 1.0 - ADAM_B2 ** ADAM_STEP

TILE_ROWS = 8
TOKEN_TILE = 256
MATMUL_TOKEN_TILE = 512
VMEM_LIMIT = 56 * 1024 * 1024

PK_G_NORM_RNN, PK_RCONV_B, PK_B_A, PK_B_X, PK_LAMBDA, PK_CONV_W = 0, 1, 2, 3, 4, 5
PK_RCONV_W, PK_G_NORM_CONV = 8, 12
PK_MIX_ROWS = 16
PK_FINAL_G, PK_MLP_G, PK_LOSS = 16, 17, 18
PK_MLP_ROWS = 8
PK_MIX_G = 24
PK_ROWS = 32
N_HEADS, HEAD_DIM = 16, 64


def _params(**kw):
    return pltpu.CompilerParams(vmem_limit_bytes=VMEM_LIMIT, **kw)


def _position():
    x, y, c = lax.axis_index("x"), lax.axis_index("y"), lax.axis_index("c")
    return x, y, c


def _sigmoid(v):
    return 1.0 / (1.0 + jnp.exp(-v))


def _one_minus_square(log_a, a):
    v = 2.0 * log_a
    series = -v * (1.0 + v * (0.5 + v * (1.0 / 6.0)))
    return jnp.where(v > -0.01, series, 1.0 - a * a)


_GELU_C = math.sqrt(2.0 / math.pi)
_GELU_K = 0.044715


def _gelu_and_grad(g):
    th = jnp.tanh(_GELU_C * (g + _GELU_K * g * g * g))
    gelu = 0.5 * g * (1.0 + th)
    dgelu = 0.5 * (1.0 + th) + 0.5 * g * (1.0 - th * th) * (_GELU_C * (1.0 + 3.0 * _GELU_K * g * g))
    return gelu, dgelu


def _rows(shape):
    return lax.broadcasted_iota(jnp.int32, shape, 0)


def _shift_down(v, k, prev8):
    rolled = pltpu.roll(v, k, 0)
    halo = pltpu.roll(prev8, k, 0)
    head = jnp.where(_rows(halo.shape) < k, halo, rolled[:TILE_ROWS])
    return jnp.concatenate([head, rolled[TILE_ROWS:]], axis=0)


def _shift_up(v, k, next8):
    n = v.shape[0]
    rolled = pltpu.roll(v, n - k, 0)
    halo = pltpu.roll(next8, TILE_ROWS - k, 0)
    tail = jnp.where(_rows(halo.shape) >= TILE_ROWS - k, halo, rolled[n - TILE_ROWS:])
    return jnp.concatenate([rolled[: n - TILE_ROWS], tail], axis=0)


def _scan_rows(a, b, carry, reverse=False):
    n, w = a.shape
    groups = n // TILE_ROWS
    a3 = a.reshape(groups, TILE_ROWS, w)
    b3 = b.reshape(groups, TILE_ROWS, w)
    sub = lax.broadcasted_iota(jnp.int32, a3.shape, 1)
    s = 1
    while s < TILE_ROWS:
        shift = TILE_ROWS - s if reverse else s
        keep = (sub < TILE_ROWS - s) if reverse else (sub >= s)
        b3 = b3 + jnp.where(keep, a3 * pltpu.roll(b3, shift, 1), 0.0)
        a3 = a3 * jnp.where(keep, pltpu.roll(a3, shift, 1), 1.0)
        s *= 2
    out = [None] * groups
    edge = 0 if reverse else TILE_ROWS - 1
    for g in (range(groups - 1, -1, -1) if reverse else range(groups)):
        out[g] = b3[g] + a3[g] * carry
        carry = out[g][edge:edge + 1]
    return jnp.concatenate(out, axis=0)


def _softplus_neg(lam):
    e = jnp.exp(-jnp.abs(lam))
    log1p_e = jnp.where(e < 1e-2, e * (1.0 - e * (0.5 - e * (1.0 / 3.0 - e * 0.25))), jnp.log(1.0 + e))
    sp = jnp.maximum(-lam, 0.0) + log1p_e
    dsp = -_sigmoid(-lam)
    return sp, dsp


def _block_diag_dot(vb, w_ref):
    return jnp.concatenate(
        [jnp.dot(vb[:, j * BD:(j + 1) * BD], w_ref[j], preferred_element_type=F32) for j in range(N_BD)], axis=1)


def _block_diag_dot_t(vb, w_ref):
    return jnp.concatenate(
        [lax.dot_general(vb[:, j * BD:(j + 1) * BD], w_ref[j], (((1,), (1,)), ((), ())), preferred_element_type=F32)
         for j in range(N_BD)], axis=1)


def _dot_nt(a, b):
    return lax.dot_general(a, b, (((1,), (1,)), ((), ())), preferred_element_type=F32)


def _dot_tn(a, b):
    return lax.dot_general(a, b, (((0,), (0,)), ((), ())), preferred_element_type=F32)


def _lru_gates(xr, wa_ref, ba, wx_ref, bx, sp):
    xrb = xr.astype(BF16)
    r = _sigmoid(_block_diag_dot(xrb, wa_ref) + ba)
    ig = _sigmoid(_block_diag_dot(xrb, wx_ref) + bx)
    log_a = (-LRU_C) * r * sp
    a = jnp.exp(log_a)
    mult = jnp.sqrt(_one_minus_square(log_a, a))
    return r, ig, a, mult


def _colsum(v):
    return jnp.sum(v, axis=0, keepdims=True)


N_FWD_OUT = 7


def _fwd_mix(x, g1, w_in_g, conv_w, rconv_w, rconv_b, wa_bd, b_a, wx_bd, b_x, lam, g_nc, g_nr, later):
    t, d = x.shape
    tm = TOKEN_TILE
    nt = t // tm
    nl = len(later)
    assert nl == 3
    pass_on_at = [nt * f // 16 for f in (3, 5, 9)]
    neighbours_at = [nt * f // 16 for f in (10, 11, 12)]
    diagonal_at = [nt * f // 16 for f in (13, 14, 14)]

    def body(x_ref, g1_ref, win_ref, cw_ref, rw_ref, rb_ref, wa_ref, ba_ref, wx_ref, bx_ref, lam_ref, gnc_ref, gnr_ref,
             *rest):
        later_in, outs, rest = rest[:nl], rest[nl:nl + N_FWD_OUT], rest[nl + N_FWD_OUT:]
        u_ref, h1_ref, xr_ref, hs_ref, c3_ref, y_ref, gates_ref = outs
        later_out, (cv_prev, xin_prev, h_prev, send_sems, recv_sems) = rest[:nl], rest[nl:]
        del later_in
        step = pl.program_id(0)
        plan = _ShardGather(later_out, send_sems, recv_sems)

        @pl.when(step == 0)
        def _():
            cv_prev[...] = jnp.zeros_like(cv_prev)
            xin_prev[...] = jnp.zeros_like(xin_prev)
            h_prev[...] = jnp.zeros_like(h_prev)
            for w in range(nl):
                plan.start_direct(w)

        for w in range(nl):
            @pl.when(step == pass_on_at[w])
            def _(w=w):
                plan.start_pass_on(w)

            @pl.when(step == neighbours_at[w])
            def _(w=w):
                plan.start_hand_over(w, diagonal=False)

            @pl.when(step == diagonal_at[w])
            def _(w=w):
                plan.start_hand_over(w, diagonal=True)

        xv = x_ref[...]
        rstd = lax.rsqrt(jnp.mean(xv * xv, axis=-1, keepdims=True) + EPS)
        h1b = ((xv * rstd) * g1_ref[...]).astype(BF16)
        h1_ref[...] = h1b
        for j in range(N_CHIPS):
            u_ref[:, j * IN_SHARD:(j + 1) * IN_SHARD] = jnp.dot(h1b, win_ref[j], preferred_element_type=F32)
        gate_b = u_ref[:, 0:CONV_W]
        cv = u_ref[:, CONV_W:2 * CONV_W] * u_ref[:, 2 * CONV_W:3 * CONV_W]
        x_r = u_ref[:, 3 * CONV_W:3 * CONV_W + LRU_W]
        g = u_ref[:, 3 * CONV_W + LRU_W:]

        cw = cw_ref[...]
        cvp = cv_prev[...]
        conv3 = cw[0:1] * _shift_down(cv, 2, cvp) + cw[1:2] * _shift_down(cv, 1, cvp) + cw[2:3] * cv
        cv_prev[...] = cv[tm - TILE_ROWS:]
        c3_ref[...] = conv3
        y_conv = gate_b * conv3

        rw = rw_ref[...]
        xp = xin_prev[...]
        xr = (rw[0:1] * _shift_down(x_r, 3, xp) + rw[1:2] * _shift_down(x_r, 2, xp)
              + rw[2:3] * _shift_down(x_r, 1, xp) + rw[3:4] * x_r) + rb_ref[...]
        xin_prev[...] = x_r[tm - TILE_ROWS:]
        xr_ref[...] = xr
        sp, _ = _softplus_neg(lam_ref[...])
        r, ig, a, mult = _lru_gates(xr, wa_ref, ba_ref[...], wx_ref, bx_ref[...], sp)
        for n, gate in enumerate((r, ig, a, mult)):
            gates_ref[:, n * LRU_W:(n + 1) * LRU_W] = gate
        h = _scan_rows(a, mult * (ig * xr), h_prev[...])
        h_prev[...] = h[tm - 1:tm]
        hs_ref[...] = h
        gelu, _ = _gelu_and_grad(g)
        y_rnn = h * gelu

        na = y_conv * lax.rsqrt(jnp.mean(y_conv * y_conv, axis=-1, keepdims=True) + EPS) * gnc_ref[...]
        nb = y_rnn * lax.rsqrt(jnp.mean(y_rnn * y_rnn, axis=-1, keepdims=True) + EPS) * gnr_ref[...]
        y_ref[:, :CONV_W] = na.astype(BF16)
        y_ref[:, CONV_W:] = nb.astype(BF16)

        @pl.when(step == nt - 1)
        def _():
            for w in range(nl):
                plan.finish(w)

    def full(a):
        nd = a.ndim
        return pl.BlockSpec(a.shape, lambda i: (0,) * nd)

    def tok(cols):
        return pl.BlockSpec((tm, cols), lambda i: (i, 0))

    def act(cols, dtype=F32):
        return jax.ShapeDtypeStruct((t, cols), dtype)

    smalls = (g1, w_in_g, conv_w, rconv_w, rconv_b, wa_bd, b_a, wx_bd, b_x, lam, g_nc, g_nr)
    n_in = 1 + len(smalls)
    outs = pl.pallas_call(
        body, name="fwd_mix", grid=(nt,),
        in_specs=[tok(d)] + [full(a) for a in smalls] + [ANY] * nl,
        out_specs=[tok(IN_COLS), tok(d), tok(LRU_W), tok(LRU_W), tok(CONV_W), tok(CONV_W + LRU_W)]
        + [tok(4 * LRU_W)] + [ANY] * nl,
        out_shape=[act(IN_COLS), act(d, BF16), act(LRU_W), act(LRU_W), act(CONV_W), act(CONV_W + LRU_W, BF16)]
        + [act(4 * LRU_W)] + [jax.ShapeDtypeStruct(a.shape, a.dtype) for a in later],
        input_output_aliases={n_in + w: N_FWD_OUT + w for w in range(nl)},
        scratch_shapes=[pltpu.VMEM((TILE_ROWS, CONV_W), F32), pltpu.VMEM((TILE_ROWS, LRU_W), F32),
                        pltpu.VMEM((1, LRU_W), F32), pltpu.SemaphoreType.DMA((nl, _ShardGather.PAIRS)),
                        pltpu.SemaphoreType.DMA((nl, _ShardGather.PAIRS))],
        compiler_params=_params(dimension_semantics=("arbitrary",)),
    )(x, *smalls, *later)
    return outs[:N_FWD_OUT], outs[N_FWD_OUT:]


def _mlp_fwd_bwd(x, yb, w_out_g, w1_g, w2_g, g2, gf, target):
    t, d = x.shape
    tm = TOKEN_TILE
    ff = w2_g.shape[0]
    mix = w_out_g.shape[0]
    ffs = ff // N_CHIPS

    def body(x_ref, y_ref, g2_ref, gf_ref, tgt_ref, wout_hbm, w1_hbm, w2_hbm,
             z_ref, dp_ref, h2_ref, dx3b_ref, dx2_ref, dx2b_ref, dy_ref, st_ref, wout, w1, w2, p_ref):
        @pl.when(pl.program_id(0) == 0)
        def _():
            pltpu.sync_copy(wout_hbm, wout)
            pltpu.sync_copy(w1_hbm, w1)
            pltpu.sync_copy(w2_hbm, w2)
            st_ref[...] = jnp.zeros_like(st_ref)

        x2 = x_ref[...] + jnp.dot(y_ref[...], wout[...], preferred_element_type=F32)
        r2 = lax.rsqrt(jnp.mean(x2 * x2, axis=-1, keepdims=True) + EPS)
        xh2 = x2 * r2
        g2v = g2_ref[...]
        h2b = (xh2 * g2v).astype(BF16)
        h2_ref[...] = h2b
        for j in range(N_CHIPS):
            p_ref[:, j * ffs:(j + 1) * ffs] = jnp.dot(h2b, w1[j], preferred_element_type=F32)
        rp = jnp.maximum(p_ref[...], 0.0)
        zb = (rp * rp).astype(BF16)
        z_ref[...] = zb
        x3 = x2 + jnp.dot(zb, w2[...], preferred_element_type=F32)
        r3 = lax.rsqrt(jnp.mean(x3 * x3, axis=-1, keepdims=True) + EPS)
        xh3 = x3 * r3
        gfv = gf_ref[...]
        err = xh3 * gfv - tgt_ref[...]
        loss = (0.5 / d) * jnp.sum(err * err)
        dout = err * (1.0 / d)
        st_ref[PK_FINAL_G - PK_MIX_ROWS:PK_FINAL_G - PK_MIX_ROWS + 1, :] += _colsum(dout * xh3)
        st_ref[PK_LOSS - PK_MIX_ROWS:PK_LOSS - PK_MIX_ROWS + 1, :] += jnp.zeros((1, d), F32) + loss
        dxh3 = dout * gfv
        dx3 = r3 * (dxh3 - xh3 * jnp.mean(dxh3 * xh3, axis=-1, keepdims=True))
        dx3b = dx3.astype(BF16)
        dx3b_ref[...] = dx3b
        dpb = (_dot_nt(dx3b, w2[...]) * (2.0 * rp)).astype(BF16)
        dp_ref[...] = dpb
        dh2 = _dot_nt(dpb[:, 0:ffs], w1[0])
        for j in range(1, N_CHIPS):
            dh2 = dh2 + _dot_nt(dpb[:, j * ffs:(j + 1) * ffs], w1[j])
        st_ref[PK_MLP_G - PK_MIX_ROWS:PK_MLP_G - PK_MIX_ROWS + 1, :] += _colsum(dh2 * xh2)
        dxh2 = dh2 * g2v
        dx2 = dx3 + r2 * (dxh2 - xh2 * jnp.mean(dxh2 * xh2, axis=-1, keepdims=True))
        dx2_ref[...] = dx2
        dx2b = dx2.astype(BF16)
        dx2b_ref[...] = dx2b
        dy_ref[...] = _dot_nt(dx2b, wout[...])

    def tok(cols):
        return pl.BlockSpec((tm, cols), lambda i: (i, 0))

    def row(cols):
        return pl.BlockSpec((1, cols), lambda i: (0, 0))

    return pl.pallas_call(
        body, name="mlp_fwd_bwd", grid=(t // tm,),
        in_specs=[tok(d), tok(mix), row(d), row(d), tok(d), ANY, ANY, ANY],
        out_specs=[tok(ff), tok(ff), tok(d), tok(d), tok(d), tok(d), tok(mix),
                   pl.BlockSpec((PK_MLP_ROWS, d), lambda i: (0, 0))],
        out_shape=[jax.ShapeDtypeStruct((t, ff), BF16), jax.ShapeDtypeStruct((t, ff), BF16),
                   jax.ShapeDtypeStruct((t, d), BF16), jax.ShapeDtypeStruct((t, d), BF16),
                   jax.ShapeDtypeStruct((t, d), F32), jax.ShapeDtypeStruct((t, d), BF16),
                   jax.ShapeDtypeStruct((t, mix), F32), jax.ShapeDtypeStruct((PK_MLP_ROWS, d), F32)],
        scratch_shapes=[pltpu.VMEM(w_out_g.shape, BF16), pltpu.VMEM(w1_g.shape, BF16), pltpu.VMEM(w2_g.shape, BF16),
                        pltpu.VMEM((tm, ff), F32)],
        compiler_params=_params(dimension_semantics=("arbitrary",)),
    )(x, yb, g2, gf, target, w_out_g, w1_g, w2_g)


def _mix_bwd(dy, u, xr_all, hs_all, c3_all, gates, conv_w, rconv_w, wa_bd, wx_bd, lam, g_nc, g_nr, parts):
    t = dy.shape[0]
    tm = TOKEN_TILE
    nt = t // tm
    hb = tm // TILE_ROWS
    npart = len(parts)

    def body(dy_ref, u_ref, uh_ref, xr_ref, hs_ref, hh_ref, c3_ref, gates_ref,
             cw_ref, rw_ref, wa_ref, wx_ref, lam_ref, gnc_ref, gnr_ref, *rest):
        part_refs, (du_ref, st_ref, dwa_ref, dwx_ref), rest = rest[:npart], rest[npart:npart + 4], rest[npart + 4:]
        arrived_refs, (dc_next, a_next, gs_next, dxr_next, send_sems, recv_sems) = rest[:npart], rest[npart:]
        exchange = _PartialExchange(part_refs, arrived_refs, send_sems, recv_sems)
        i = pl.program_id(0)

        @pl.when(i == 0)
        def _():
            exchange.start()
            dc_next[...] = jnp.zeros_like(dc_next)
            a_next[...] = jnp.zeros_like(a_next)
            gs_next[...] = jnp.zeros_like(gs_next)
            dxr_next[...] = jnp.zeros_like(dxr_next)
            st_ref[...] = jnp.zeros_like(st_ref)
            dwa_ref[...] = jnp.zeros_like(dwa_ref)
            dwx_ref[...] = jnp.zeros_like(dwx_ref)

        first_tile = i == nt - 1
        gate_b = u_ref[:, 0:CONV_W]
        gate_c = u_ref[:, CONV_W:2 * CONV_W]
        v = u_ref[:, 2 * CONV_W:3 * CONV_W]
        x_r = u_ref[:, 3 * CONV_W:3 * CONV_W + LRU_W]
        g = u_ref[:, 3 * CONV_W + LRU_W:]
        cv = gate_c * v
        cv_prev = jnp.where(first_tile, 0.0, uh_ref[:, CONV_W:2 * CONV_W] * uh_ref[:, 2 * CONV_W:3 * CONV_W])
        xin_prev = jnp.where(first_tile, 0.0, uh_ref[:, 3 * CONV_W:3 * CONV_W + LRU_W])
        hs_prev = jnp.where(first_tile, 0.0, hh_ref[...])

        def acc(first_row, val, width=LRU_W, row=0):
            r0 = first_row + row
            st_ref[r0:r0 + 1, 0:width] += val

        conv3 = c3_ref[...]
        y_conv = gate_b * conv3
        ra = lax.rsqrt(jnp.mean(y_conv * y_conv, axis=-1, keepdims=True) + EPS)
        xha = y_conv * ra
        dna = dy_ref[:, :CONV_W]
        acc(PK_G_NORM_CONV, _colsum(dna * xha), CONV_W)
        dxha = dna * gnc_ref[...]
        dy_conv = ra * (dxha - xha * jnp.mean(dxha * xha, axis=-1, keepdims=True))
        du_ref[:, 0:CONV_W] = (dy_conv * conv3).astype(BF16)
        dc = dy_conv * gate_b
        cw = cw_ref[...]
        dcn = dc_next[...]
        dcv = cw[2:3] * dc + cw[1:2] * _shift_up(dc, 1, dcn) + cw[0:1] * _shift_up(dc, 2, dcn)
        dc_next[...] = dc[:TILE_ROWS]
        acc(PK_CONV_W, _colsum(dc * _shift_down(cv, 2, cv_prev)), CONV_W, 0)
        acc(PK_CONV_W, _colsum(dc * _shift_down(cv, 1, cv_prev)), CONV_W, 1)
        acc(PK_CONV_W, _colsum(dc * cv), CONV_W, 2)
        du_ref[:, CONV_W:2 * CONV_W] = (dcv * v).astype(BF16)
        du_ref[:, 2 * CONV_W:3 * CONV_W] = (dcv * gate_c).astype(BF16)

        hs = hs_ref[...]
        gelu, dgelu = _gelu_and_grad(g)
        y_rnn = hs * gelu
        rb = lax.rsqrt(jnp.mean(y_rnn * y_rnn, axis=-1, keepdims=True) + EPS)
        xhb = y_rnn * rb
        dnb = dy_ref[:, CONV_W:]
        acc(PK_G_NORM_RNN, _colsum(dnb * xhb))
        dxhb = dnb * gnr_ref[...]
        dy_rnn = rb * (dxhb - xhb * jnp.mean(dxhb * xhb, axis=-1, keepdims=True))
        du_ref[:, 3 * CONV_W + LRU_W:] = (dy_rnn * hs * dgelu).astype(BF16)
        dh = dy_rnn * gelu

        xr = xr_ref[...]
        xrb = xr.astype(BF16)
        sp, dsp = _softplus_neg(lam_ref[...])
        r, ig, a, mult = [gates_ref[:, n * LRU_W:(n + 1) * LRU_W] for n in range(4)]
        a_up = _shift_up(a, 1, a_next[...])
        a_next[...] = a[:TILE_ROWS]
        gs = _scan_rows(a_up, dh, gs_next[0:1, :], reverse=True)
        gs_next[...] = gs[:TILE_ROWS]
        da = gs * _shift_down(hs, 1, hs_prev)
        gx = gs * xr
        di = gx * mult
        dmult = gx * ig
        dxr = gs * (mult * ig)
        dlog_a = da * a - dmult * ((a * a) / mult)
        acc(PK_LAMBDA, _colsum(dlog_a * r) * ((-LRU_C) * dsp))
        dpa = (dlog_a * ((-LRU_C) * sp)) * (r * (1.0 - r))
        dpx = di * (ig * (1.0 - ig))
        acc(PK_B_A, _colsum(dpa))
        acc(PK_B_X, _colsum(dpx))
        dpab = dpa.astype(BF16)
        dpxb = dpx.astype(BF16)
        dxr = dxr + _block_diag_dot_t(dpab, wa_ref) + _block_diag_dot_t(dpxb, wx_ref)
        for j in range(N_BD):
            cols = slice(j * BD, (j + 1) * BD)
            dwa_ref[j] += _dot_tn(xrb[:, cols], dpab[:, cols])
            dwx_ref[j] += _dot_tn(xrb[:, cols], dpxb[:, cols])

        acc(PK_RCONV_B, _colsum(dxr))
        rw = rw_ref[...]
        dxn = dxr_next[...]
        dx_r = (rw[3:4] * dxr + rw[2:3] * _shift_up(dxr, 1, dxn) + rw[1:2] * _shift_up(dxr, 2, dxn)
                + rw[0:1] * _shift_up(dxr, 3, dxn))
        dxr_next[...] = dxr[:TILE_ROWS]
        for k in range(3):
            acc(PK_RCONV_W, _colsum(dxr * _shift_down(x_r, 3 - k, xin_prev)), LRU_W, k)
        acc(PK_RCONV_W, _colsum(dxr * x_r), LRU_W, 3)
        du_ref[:, 3 * CONV_W:3 * CONV_W + LRU_W] = dx_r.astype(BF16)

        @pl.when(i == nt - 1)
        def _():
            exchange.wait()

    def full(a):
        nd = a.ndim
        return pl.BlockSpec(a.shape, lambda i: (0,) * nd)

    def tok(cols):
        return pl.BlockSpec((tm, cols), lambda i: (nt - 1 - i, 0))

    def halo(cols):
        return pl.BlockSpec((TILE_ROWS, cols), lambda i: (jnp.maximum((nt - 1 - i) * hb - 1, 0), 0))

    smalls = (conv_w, rconv_w, wa_bd, wx_bd, lam, g_nc, g_nr)
    outs = pl.pallas_call(
        body, name="mix_bwd", grid=(nt,),
        in_specs=[tok(CONV_W + LRU_W), tok(IN_COLS), halo(IN_COLS), tok(LRU_W), tok(LRU_W), halo(LRU_W), tok(CONV_W)]
        + [tok(4 * LRU_W)] + [full(a) for a in smalls] + [ANY] * npart,
        out_specs=[tok(IN_COLS), pl.BlockSpec((PK_MIX_ROWS, LRU_W), lambda i: (0, 0)),
                   pl.BlockSpec((N_BD, BD, BD), lambda i: (0, 0, 0)), pl.BlockSpec((N_BD, BD, BD), lambda i: (0, 0, 0))]
        + [ANY] * npart,
        out_shape=[jax.ShapeDtypeStruct((t, IN_COLS), BF16), jax.ShapeDtypeStruct((PK_MIX_ROWS, LRU_W), F32),
                   jax.ShapeDtypeStruct((N_BD, BD, BD), F32), jax.ShapeDtypeStruct((N_BD, BD, BD), F32)]
        + [jax.ShapeDtypeStruct(a.shape, a.dtype) for a in parts],
        scratch_shapes=[pltpu.VMEM((TILE_ROWS, CONV_W), F32), pltpu.VMEM((TILE_ROWS, LRU_W), F32),
                        pltpu.VMEM((TILE_ROWS, LRU_W), F32), pltpu.VMEM((TILE_ROWS, LRU_W), F32),
                        pltpu.SemaphoreType.DMA((npart, 3)), pltpu.SemaphoreType.DMA((npart, 3))],
        compiler_params=_params(dimension_semantics=("arbitrary",)),
    )(dy, u, u, xr_all, hs_all, hs_all, c3_all, gates, *smalls, *parts)
    return outs[:4], outs[4:]


def _in_bwd(dub, w_in_g, x, dx2, g1, parts, joins, core_chip):
    t, d = x.shape
    tm = min(t, MATMUL_TOKEN_TILE)
    nt = t // tm
    npart = len(parts)
    nj = len(joins)
    geometry = []
    for tag, shape, _, _ in joins:
        pr, pc = WGRAD_GEOMETRY[tag][:2]
        every = 1 if pr % (nt * 16) == 0 else 2
        geometry.append((pr, pc, pr * every // nt, every, shape[1] == pc))

    def body(cc_ref, du_ref, win_ref, x_ref, dx2_ref, g1_ref, *rest):
        sums, rest = [rest[4 * w:4 * w + 4] for w in range(nj)], rest[4 * nj:]
        part_refs, (gx_ref, st_ref), rest = rest[:npart], rest[npart:npart + 2], rest[npart + 2:]
        arrived_refs, joined, rest = rest[:npart], rest[npart:npart + nj], rest[npart + nj:]
        stages, (send_sems, recv_sems, j_local, j_send, j_recv) = rest[:nj], rest[nj:]
        exchange = _PartialExchange(part_refs, arrived_refs, send_sems, recv_sems)
        i = pl.program_id(0)
        c = cc_ref[0]

        def window(w, core, row0, rows):
            pr, pc, _, _, by_rows = geometry[w]
            if by_rows:
                return joined[w].at[pl.ds(core * pr + row0, rows), :]
            return joined[w].at[pl.ds(row0, rows), pl.ds(core * pc, pc)]

        def to_sibling(w, src, core, row0, rows):
            return pltpu.make_async_remote_copy(src_ref=src, dst_ref=window(w, core, row0, rows), send_sem=j_send.at[w],
                                                recv_sem=j_recv.at[w], device_id=_sibling(), device_id_type=MESH)

        @pl.when(i == 0)
        def _():
            exchange.start()
            st_ref[...] = jnp.zeros_like(st_ref)

        for w in range(nj):
            pr, pc, rb, every, _ = geometry[w]

            @pl.when(i % every == 0)
            def _(w=w, rb=rb, every=every):
                p_ref, r1_ref, r2_ref, r3_ref = sums[w]
                row0 = pl.multiple_of((i // every) * rb, rb)
                rows = stages[w].at[pl.ds(row0, rb), :]
                rows[...] = ((p_ref[0] + r1_ref[0].astype(F32)) + r2_ref[0].astype(F32)) + r3_ref[0].astype(F32)
                pltpu.make_async_copy(rows, window(w, c, row0, rb), j_local.at[w]).start()
                to_sibling(w, rows, c, row0, rb).start()

        dh1 = _dot_nt(du_ref[:, 0:IN_SHARD], win_ref[0])
        for j in range(1, N_CHIPS):
            dh1 = dh1 + _dot_nt(du_ref[:, j * IN_SHARD:(j + 1) * IN_SHARD], win_ref[j])
        xv = x_ref[...]
        rstd = lax.rsqrt(jnp.mean(xv * xv, axis=-1, keepdims=True) + EPS)
        xh = xv * rstd
        st_ref[0:1, :] += _colsum(dh1 * xh)
        dxh = dh1 * g1_ref[...]
        gx_ref[...] = dx2_ref[...] + rstd * (dxh - xh * jnp.mean(dxh * xh, axis=-1, keepdims=True))

        @pl.when(i == nt - 1)
        def _():
            exchange.wait()
            for w in range(nj):
                pr = geometry[w][0]
                pltpu.make_async_copy(stages[w], window(w, c, 0, pr), j_local.at[w]).wait()
                to_sibling(w, stages[w], 1 - c, 0, pr).wait()

    def tok(cols):
        return pl.BlockSpec((tm, cols), lambda i, cc: (i, 0))

    def partial(w, off):
        pr, pc, rb, every, _ = geometry[w]
        return pl.BlockSpec((1, rb, pc), lambda i, cc: ((cc[1] + off) % N_CHIPS, i // every, 0))

    sum_specs, sum_operands = [], []
    for w, (_, _, own, arrived) in enumerate(joins):
        sum_specs += [partial(w, off) for off in range(N_CHIPS)]
        sum_operands += [own, arrived, arrived, arrived]
    dma = pltpu.SemaphoreType.DMA
    outs = pl.pallas_call(
        body, name="in_bwd",
        grid_spec=pltpu.PrefetchScalarGridSpec(
            num_scalar_prefetch=1, grid=(nt,),
            in_specs=[tok(IN_COLS), pl.BlockSpec(w_in_g.shape, lambda i, cc: (0, 0, 0)), tok(d), tok(d),
                      pl.BlockSpec((1, d), lambda i, cc: (0, 0))] + sum_specs + [ANY] * npart,
            out_specs=[tok(d), pl.BlockSpec((TILE_ROWS, d), lambda i, cc: (0, 0))] + [ANY] * (npart + nj),
            scratch_shapes=[pltpu.VMEM((g[0], g[1]), F32) for g in geometry]
            + [dma((npart, 3)), dma((npart, 3)), dma((nj,)), dma((nj,)), dma((nj,))]),
        out_shape=[jax.ShapeDtypeStruct((t, d), F32), jax.ShapeDtypeStruct((TILE_ROWS, d), F32)]
        + [jax.ShapeDtypeStruct(a.shape, a.dtype) for a in parts]
        + [jax.ShapeDtypeStruct(shape, F32) for _, shape, _, _ in joins],
        compiler_params=_params(dimension_semantics=("arbitrary",)),
    )(core_chip, dub, w_in_g, x, dx2, g1, *sum_operands, *parts)
    return outs[:2], outs[2:2 + npart], outs[2 + npart:]


WGRAD_GEOMETRY = {
    "in": (512, IN_SHARD, lambda s, h: h, lambda s, h: s),
    "mlp_in": (512, D_MODEL, lambda s, h: h, lambda s, h: s),
    "mlp_out": (512, D_MODEL, lambda s, h: 2 * s + h, lambda s, h: 0),
    "out": (384, 512, lambda s, h: s, lambda s, h: h),
}
K_CHUNK = 512


def _sibling():
    x, y, c = _position()
    return (x, y, 1 - c)


def _wgrad(a, b, tag, core_chip, packs=(), parts=()):
    t = a.shape[0]
    pr, pc, a_blk, b_blk = WGRAD_GEOMETRY[tag]
    nk = t // K_CHUNK
    mine = N_CHIPS
    riding = len(packs)
    npart = len(parts)
    assert not (riding and npart)

    def body(cc_ref, a_ref, b_ref, *rest):
        if riding:
            pack_refs, (land_ref, p_ref, pb_ref), rest = rest[:riding], rest[riding:riding + 3], rest[riding + 3:]
            all_refs, (stage, rbuf, send_sems, recv_sems, rsem), g_sems = rest[:riding], rest[riding:riding + 5], rest[riding + 5:]
            gathers = [_PackGather(pack_refs[n], all_refs[n], *g_sems[3 * n:3 * n + 3]) for n in range(riding)]
        elif npart:
            part_refs, (land_ref, p_ref, pb_ref), rest = rest[:npart], rest[npart:npart + 3], rest[npart + 3:]
            arrived_refs, (stage, rbuf, send_sems, recv_sems, rsem, x_send, x_recv) = rest[:npart], rest[npart:]
            exchange = _PartialExchange(part_refs, arrived_refs, x_send, x_recv)
        else:
            land_ref, p_ref, pb_ref, stage, rbuf, send_sems, recv_sems, rsem = rest
        ph, s = pl.program_id(0), pl.program_id(1)
        if riding:
            @pl.when((ph == 0) & (s == 0))
            def _():
                for gather in gathers:
                    gather.start()

            @pl.when((ph == 1) & (s == N_CHIPS - 2))
            def _():
                for gather in gathers:
                    gather.hand_over()
        if npart:
            @pl.when((ph == 0) & (s == 0))
            def _():
                exchange.start()
        def push(k):
            return pltpu.make_async_remote_copy(src_ref=stage.at[k], dst_ref=land_ref.at[k], send_sem=send_sems.at[k],
                                                recv_sem=recv_sems.at[k], device_id=_sibling(), device_id_type=MESH)

        def landed():
            return pltpu.make_async_copy(land_ref.at[s], rbuf, rsem)

        @pl.when(ph == 1)
        def _():
            push(s).wait_recv()
            landed().start()

        slot = jnp.where(ph == 0, s, mine)
        acc = stage.at[slot]
        acc[...] = _dot_tn(a_ref[0:K_CHUNK, :], b_ref[0:K_CHUNK, :])
        for k in range(1, nk):
            acc[...] += _dot_tn(a_ref[k * K_CHUNK:(k + 1) * K_CHUNK, :], b_ref[k * K_CHUNK:(k + 1) * K_CHUNK, :])

        @pl.when(ph == 0)
        def _():
            push(s).start()

        @pl.when(ph == 1)
        def _():
            landed().wait()
            p = stage[mine] + rbuf[...]
            p_ref[0] = p
            pb_ref[0] = p.astype(BF16)

        @pl.when((ph == 1) & (s == N_CHIPS - 1))
        def _():
            for k in range(N_CHIPS):
                push(k).wait_send()
            for gather in (gathers if riding else ()):
                gather.finish()
            if npart:
                exchange.wait()

    def half(ph, cc):
        return jnp.where(ph == 0, 1 - cc[0], cc[0])

    def out_slot(ph, s, cc):
        return (jnp.where(ph == 0, 0, s), 0, 0)

    piece = jax.ShapeDtypeStruct((N_CHIPS, pr, pc), F32)
    in_specs = [pl.BlockSpec((t, pr), lambda ph, s, cc: (0, a_blk(s, half(ph, cc)))),
                pl.BlockSpec((t, pc), lambda ph, s, cc: (0, b_blk(s, half(ph, cc))))]
    out_specs = [ANY, pl.BlockSpec((1, pr, pc), out_slot), pl.BlockSpec((1, pr, pc), out_slot)]
    out_shape = [piece, piece, jax.ShapeDtypeStruct((N_CHIPS, pr, pc), BF16)]
    scratch = [pltpu.VMEM((N_CHIPS + 1, pr, pc), F32), pltpu.VMEM((pr, pc), F32),
               pltpu.SemaphoreType.DMA((N_CHIPS,)), pltpu.SemaphoreType.DMA((N_CHIPS,)), pltpu.SemaphoreType.DMA]
    operands = [a, b]
    for pack in packs:
        in_specs.append(pl.BlockSpec(pack.shape, lambda ph, s, cc, nd=pack.ndim: (0,) * nd))
        out_specs.append(ANY)
        out_shape.append(jax.ShapeDtypeStruct((N_DEVICES,) + pack.shape, pack.dtype))
        operands.append(pack)
    for pack in packs:
        scratch += _PackGather.semaphores()
    if npart:
        in_specs += [ANY] * npart
        out_specs += [ANY] * npart
        out_shape += [jax.ShapeDtypeStruct(p.shape, p.dtype) for p in parts]
        scratch += [pltpu.SemaphoreType.DMA((npart, 3)), pltpu.SemaphoreType.DMA((npart, 3))]
        operands += list(parts)
    return pl.pallas_call(
        body, name="wgrad_" + tag,
        grid_spec=pltpu.PrefetchScalarGridSpec(
            num_scalar_prefetch=1, grid=(2, N_CHIPS), in_specs=in_specs, out_specs=out_specs, scratch_shapes=scratch),
        out_shape=out_shape,
        compiler_params=_params(dimension_semantics=("arbitrary", "arbitrary")),
    )(core_chip, *operands)[1:]


def _other_chips(x, y):
    return [(1 - x, y), (x, 1 - y), (1 - x, 1 - y)]


class _ShardGather:
    PAIRS = 9

    def __init__(self, outs, send_sems, recv_sems):
        self.outs, self.send_sems, self.recv_sems = outs, send_sems, recv_sems
        x, y, c = _position()
        self.c, self.j = c, 2 * x + y
        self.sibling = (x, y, 1 - c)
        self.chips = _other_chips(x, y)

    def _chip(self, k):
        px, py = self.chips[k]
        return 2 * px + py

    def _half(self, w, chip, which):
        hr = self.outs[w].shape[1] // 2
        return self.outs[w].at[chip, pl.ds(which * hr, hr), :]

    def _quarter(self, w, chip, q):
        qr = self.outs[w].shape[1] // 4
        return self.outs[w].at[chip, pl.ds(self.c * 2 * qr + q * qr, qr), :]

    def _copy(self, ref, w, pair, to, src=None):
        return pltpu.make_async_remote_copy(src_ref=ref if src is None else src, dst_ref=ref, send_sem=self.send_sems.at[w, pair],
                                            recv_sem=self.recv_sems.at[w, pair], device_id=to, device_id_type=MESH)

    def direct(self, w, k, q, src=None):
        return self._copy(self._quarter(w, self.j, q), w, 2 * k + q, (*self.chips[k], self.c), src)

    def direct_landed(self, w, k, q):
        return self._copy(self._quarter(w, self._chip(k), q), w, 2 * k + q, (*self.chips[k], self.c))

    def pass_on(self, w, q):
        return self._copy(self._quarter(w, self._chip(q), q), w, 4 + q, (*self.chips[1 - q], self.c))

    def passed_landed(self, w, q):
        return self._copy(self._quarter(w, self._chip(2), q), w, 4 + q, (*self.chips[1 - q], self.c))

    def hand_over(self, w, k):
        return self._copy(self._half(w, self._chip(k), self.c), w, 6 + k, self.sibling)

    def handed(self, w, k):
        return self._copy(self._half(w, self._chip(k), 1 - self.c), w, 6 + k, self.sibling)

    def start_direct(self, w, src_half=None):
        qr = self.outs[w].shape[1] // 4
        for k, q in ((0, 0), (1, 1), (0, 1), (1, 0)):
            self.direct(w, k, q, None if src_half is None else src_half.at[pl.ds(q * qr, qr), :]).start()

    def start_pass_on(self, w):
        for q in (0, 1):
            self.direct_landed(w, q, q).wait_recv()
            self.pass_on(w, q).start()

    def start_hand_over(self, w, diagonal):
        if diagonal:
            for q in (0, 1):
                self.passed_landed(w, q).wait_recv()
            self.hand_over(w, 2).start()
        else:
            for k in (0, 1):
                self.direct_landed(w, k, 1 - k).wait_recv()
                self.hand_over(w, k).start()

    def finish(self, w):
        for k in range(3):
            self.handed(w, k).wait_recv()
            self.hand_over(w, k).wait_send()
        for q in (0, 1):
            self.pass_on(w, q).wait_send()
            for k in (0, 1):
                self.direct(w, k, q).wait_send()


def _gather_first(w_in, w_out, w1, w2, small):
    bigs = (w_in, w_out, w1, w2)
    nb = len(bigs)

    def body(win_ref, wout_ref, w1_ref, w2_ref, sm_ref, gin, gout, g1, g2, gsm, st_in, st_out, st_1, st_2,
             send_sems, recv_sems, sm_send, sm_recv, local_sems):
        srcs = (win_ref, wout_ref, w1_ref, w2_ref)
        stages = (st_in, st_out, st_1, st_2)
        outs = (gin, gout, g1, g2)
        plan = _ShardGather(outs[:1], send_sems, recv_sems)
        j, c = plan.j, plan.c
        local = [pltpu.make_async_copy(stages[w], outs[w].at[j], local_sems.at[w]) for w in range(nb)]
        local.append(pltpu.make_async_copy(sm_ref, gsm.at[j], local_sems.at[nb]))

        def small_copy(k):
            px, py = plan.chips[k]
            return pltpu.make_async_remote_copy(src_ref=sm_ref, dst_ref=gsm.at[j], send_sem=sm_send.at[k],
                                                recv_sem=sm_recv.at[k], device_id=(px, py, c), device_id_type=MESH)

        def small_landed(k):
            px, py = plan.chips[k]
            return pltpu.make_async_remote_copy(src_ref=sm_ref, dst_ref=gsm.at[2 * px + py], send_sem=sm_send.at[k],
                                                recv_sem=sm_recv.at[k], device_id=(px, py, c), device_id_type=MESH)

        hr = w_in.shape[0] // 2
        st_in[...] = win_ref[...].astype(BF16)
        plan.start_direct(0, st_in.at[pl.ds(c * hr, hr), :])
        for k in range(3):
            small_copy(k).start()
        for src, st in zip(srcs[1:], stages[1:]):
            st[...] = src[...].astype(BF16)
        for cp in local:
            cp.start()
        plan.start_pass_on(0)
        plan.start_hand_over(0, diagonal=False)
        plan.start_hand_over(0, diagonal=True)
        for k in range(3):
            small_landed(k).wait_recv()
            small_copy(k).wait_send()
        plan.finish(0)
        for cp in local:
            cp.wait()

    def gathered(a, dtype):
        return jax.ShapeDtypeStruct((N_CHIPS,) + a.shape, dtype)

    return pl.pallas_call(
        body, name="gather_first",
        in_specs=[VMEM] * 5, out_specs=[ANY] * 5,
        out_shape=[gathered(a, BF16) for a in bigs] + [gathered(small, F32)],
        scratch_shapes=[pltpu.VMEM(a.shape, BF16) for a in bigs]
        + [pltpu.SemaphoreType.DMA((1, _ShardGather.PAIRS)), pltpu.SemaphoreType.DMA((1, _ShardGather.PAIRS)), pltpu.SemaphoreType.DMA((3,)),
           pltpu.SemaphoreType.DMA((3,)), pltpu.SemaphoreType.DMA((nb + 1,))],
        compiler_params=_params(),
    )(*bigs, small)


class _PartialExchange:
    def __init__(self, parts, arrived, send_sems, recv_sems):
        self.parts, self.arrived, self.send_sems, self.recv_sems = parts, arrived, send_sems, recv_sems
        x, y, c = _position()
        self.c, self.j = c, 2 * x + y
        self.chips = _other_chips(x, y)

    def _copy(self, w, k, slot):
        px, py = self.chips[k]
        return pltpu.make_async_remote_copy(
            src_ref=self.parts[w].at[2 * px + py], dst_ref=self.arrived[w].at[slot], send_sem=self.send_sems.at[w, k],
            recv_sem=self.recv_sems.at[w, k], device_id=(px, py, self.c), device_id_type=MESH)

    def start(self):
        for w in range(len(self.parts)):
            for k in range(3):
                self._copy(w, k, self.j).start()

    def wait(self):
        for w in range(len(self.parts)):
            for k in range(3):
                px, py = self.chips[k]
                self._copy(w, k, 2 * px + py).wait()


class _PackGather:
    def __init__(self, p_ref, all_ref, send_sems, recv_sems, local_sem):
        self.p_ref, self.all_ref, self.send_sems, self.recv_sems, self.local_sem = p_ref, all_ref, send_sems, recv_sems, local_sem
        x, y, c = _position()
        self.me, self.sibling, self.c = (x, y, c), (x, y, 1 - c), c
        self.chips = _other_chips(x, y)

    @staticmethod
    def semaphores():
        return [pltpu.SemaphoreType.DMA((7,)), pltpu.SemaphoreType.DMA((7,)), pltpu.SemaphoreType.DMA]

    def _copy(self, k, block, to, from_pack=False):
        px, py, pc = block
        slot = self.all_ref.at[4 * px + 2 * py + pc]
        return pltpu.make_async_remote_copy(src_ref=self.p_ref if from_pack else slot, dst_ref=slot, send_sem=self.send_sems.at[k],
                                            recv_sem=self.recv_sems.at[k], device_id=to, device_id_type=MESH)

    def _mine(self):
        x, y, c = self.me
        return pltpu.make_async_copy(self.p_ref, self.all_ref.at[4 * x + 2 * y + c], self.local_sem)

    def _first(self):
        return [self._copy(0, self.me, self.sibling, True)] + [
            self._copy(1 + k, self.me, (*chip, self.c), True) for k, chip in enumerate(self.chips)]

    def _passed(self):
        return [self._copy(4 + k, (*chip, self.c), self.sibling) for k, chip in enumerate(self.chips)]

    def start(self):
        self._mine().start()
        for cp in self._first():
            cp.start()

    def hand_over(self):
        for k, chip in enumerate(self.chips):
            self._copy(1 + k, (*chip, self.c), self.me).wait_recv()
            self._passed()[k].start()

    def finish(self):
        self._copy(0, self.sibling, self.me).wait_recv()
        for k, chip in enumerate(self.chips):
            self._copy(4 + k, (*chip, 1 - self.c), self.me).wait_recv()
        for cp in self._first() + self._passed():
            cp.wait_send()
        self._mine().wait()


class _DirectGather:
    def __init__(self, p_ref, all_ref, send_sems, recv_sems, local_sem):
        self.p_ref, self.all_ref, self.send_sems, self.recv_sems, self.local_sem = p_ref, all_ref, send_sems, recv_sems, local_sem
        self.me = _position()

    semaphores = _PackGather.semaphores

    def _peer(self, r):
        x, y, c = self.me
        return ((1 - x) if r & 4 else x, (1 - y) if r & 2 else y, (1 - c) if r & 1 else c)

    def _copy(self, r, slot_of):
        px, py, pc = slot_of
        return pltpu.make_async_remote_copy(src_ref=self.p_ref, dst_ref=self.all_ref.at[4 * px + 2 * py + pc],
                                            send_sem=self.send_sems.at[r - 1], recv_sem=self.recv_sems.at[r - 1],
                                            device_id=self._peer(r), device_id_type=MESH)

    def _mine(self):
        x, y, c = self.me
        return pltpu.make_async_copy(self.p_ref, self.all_ref.at[4 * x + 2 * y + c], self.local_sem)

    def start(self):
        self._mine().start()
        for r in range(1, N_DEVICES):
            self._copy(r, self.me).start()

    def finish(self):
        for r in range(1, N_DEVICES):
            self._copy(r, self._peer(r)).wait()
        self._mine().wait()


def _adamw(w, g, m, v):
    m = ADAM_B1 * m + (1.0 - ADAM_B1) * g
    v = ADAM_B2 * v + (1.0 - ADAM_B2) * (g * g)
    m_hat = m / ADAM_BC1
    v_hat = v / ADAM_BC2
    delta = -ADAM_LR * (m_hat / (jnp.sqrt(v_hat) + ADAM_EPS) + ADAM_WD * w)
    return delta, m, v


JOIN_SUB = 4


def _join(tag, shard_shape, part, arrived, core_chip, block=None):
    pr, pc = WGRAD_GEOMETRY[tag][:2]
    rb = pr // JOIN_SUB
    by_rows = shard_shape[1] == pc
    riding = block is not None

    def body(cc_ref, p_ref, r1_ref, r2_ref, r3_ref, *rest):
        if riding:
            blk_ref, g_ref, all_ref, stage, send_sems, recv_sems, local_sems, b_send, b_recv, b_local = rest
            gather = _DirectGather(blk_ref, all_ref, b_send, b_recv, b_local)
        else:
            g_ref, stage, send_sems, recv_sems, local_sems = rest
        i = pl.program_id(0)
        c = cc_ref[0]
        if riding:
            @pl.when(i == 0)
            def _():
                gather.start()

        def window(core, k):
            if by_rows:
                return g_ref.at[pl.ds((core * JOIN_SUB + k) * rb, rb), :]
            return g_ref.at[pl.ds(k * rb, rb), pl.ds(core * pc, pc)]

        def keep(k):
            return pltpu.make_async_copy(stage.at[k], window(c, k), local_sems.at[k])

        def push(k):
            return pltpu.make_async_remote_copy(src_ref=stage.at[k], dst_ref=window(c, k), send_sem=send_sems.at[k],
                                                recv_sem=recv_sems.at[k], device_id=_sibling(), device_id_type=MESH)

        def pushed(k):
            return pltpu.make_async_remote_copy(src_ref=stage.at[k], dst_ref=window(1 - c, k), send_sem=send_sems.at[k],
                                                recv_sem=recv_sems.at[k], device_id=_sibling(), device_id_type=MESH)

        stage[i] = ((p_ref[0] + r1_ref[0].astype(F32)) + r2_ref[0].astype(F32)) + r3_ref[0].astype(F32)
        keep(i).start()
        push(i).start()

        @pl.when(i == JOIN_SUB - 1)
        def _():
            for k in range(JOIN_SUB):
                keep(k).wait()
                push(k).wait_send()
                pushed(k).wait_recv()
            if riding:
                gather.finish()

    def partial(off):
        return pl.BlockSpec((1, rb, pc), lambda i, cc: ((cc[1] + off) % N_CHIPS, i, 0))

    in_specs = [partial(0), partial(1), partial(2), partial(3)]
    out_specs = [ANY]
    out_shape = [jax.ShapeDtypeStruct(shard_shape, F32)]
    scratch = [pltpu.VMEM((JOIN_SUB, rb, pc), F32), pltpu.SemaphoreType.DMA((JOIN_SUB,)),
               pltpu.SemaphoreType.DMA((JOIN_SUB,)), pltpu.SemaphoreType.DMA((JOIN_SUB,))]
    operands = [part, arrived, arrived, arrived]
    if riding:
        in_specs.append(pl.BlockSpec(block.shape, lambda i, cc: (0, 0)))
        out_specs.append(ANY)
        out_shape.append(jax.ShapeDtypeStruct((N_DEVICES,) + block.shape, block.dtype))
        scratch += _DirectGather.semaphores()
        operands.append(block)
    outs = pl.pallas_call(
        body, name="join_" + tag,
        grid_spec=pltpu.PrefetchScalarGridSpec(
            num_scalar_prefetch=1, grid=(JOIN_SUB,), in_specs=in_specs, out_specs=out_specs, scratch_shapes=scratch),
        out_shape=out_shape,
        compiler_params=_params(dimension_semantics=("arbitrary",)),
    )(core_chip, *operands)
    return outs if riding else outs[0]


def _adamw_big(w, g, m, v, name):
    rows, cols = w.shape
    rb = 256 if rows % 256 == 0 else rows

    def body(w_ref, g_ref, m_ref, v_ref, go_ref, d_ref, nm_ref, nv_ref):
        g = g_ref[...]
        go_ref[...] = g
        d_ref[...], nm_ref[...], nv_ref[...] = _adamw(w_ref[...], g, m_ref[...], v_ref[...])

    spec = pl.BlockSpec((rb, cols), lambda i: (i, 0))
    return pl.pallas_call(
        body, name=name, grid=(rows // rb,), in_specs=[spec] * 4, out_specs=[spec] * 4,
        out_shape=[jax.ShapeDtypeStruct(w.shape, F32)] * 4,
        compiler_params=_params(dimension_semantics=("arbitrary",)),
    )(w, g, m, v)


SMALL_VECTORS = {
    "norm_mix_g": (PK_MIX_G, D_MODEL), "rnn_conv_b": (PK_RCONV_B, LRU_W), "b_a": (PK_B_A, LRU_W), "b_x": (PK_B_X, LRU_W),
    "lru_lambda": (PK_LAMBDA, LRU_W), "g_norm_conv": (PK_G_NORM_CONV, CONV_W), "g_norm_rnn": (PK_G_NORM_RNN, LRU_W),
    "norm_mlp_g": (PK_MLP_G, D_MODEL), "final_norm_g": (PK_FINAL_G, D_MODEL),
}
SMALL_MATRICES = ("w_a", "w_x")


def _small_step(vec_packs, mat_packs, mix_g_blocks, w_pack, m_pack, v_pack, mats_wmv, conv_wmv, rconv_wmv):
    vec_rows, cols = vec_packs.shape[1:]
    conv_rows, cshard = 3, conv_wmv.shape[2]
    rconv_rows, rshard = 4, rconv_wmv.shape[2]
    names = list(SMALL_VECTORS) + list(SMALL_MATRICES) + ["conv_w", "rnn_conv_w"]
    shapes = ([(1, width) for _, width in SMALL_VECTORS.values()] + [mat_packs.shape[2:]] * len(SMALL_MATRICES)
              + [(conv_rows, cshard), (rconv_rows, rshard)])

    def body(vec_ref, mat_ref, blk_ref, w_ref, m_ref, v_ref, mw_ref, cw_ref, rw_ref, loss_ref, *rest):
        leaves, g_ref = [rest[k * len(names):(k + 1) * len(names)] for k in range(4)], rest[4 * len(names)]
        total = vec_ref[0]
        mats = mat_ref[0].astype(F32)
        late = blk_ref[0]
        for k in range(1, N_DEVICES):
            total = total + vec_ref[k]
            mats = mats + mat_ref[k].astype(F32)
            late = late + blk_ref[k]
        g_ref[0:vec_rows, :] = total
        g_ref[vec_rows:, :] = late
        g = g_ref[...]
        loss_ref[...] = g[PK_LOSS:PK_LOSS + 1, 0:1]

        x, y, _ = _position()
        j = 2 * x + y
        cblk = total[0:TILE_ROWS, :]
        rblk = total[PK_RCONV_W:PK_RCONV_W + TILE_ROWS, :]
        cg = cblk[:, 0:cshard]
        rg = rblk[:, 0:rshard]
        for k in range(1, N_CHIPS):
            cg = jnp.where(j == k, cblk[:, k * cshard:(k + 1) * cshard], cg)
            rg = jnp.where(j == k, rblk[:, k * rshard:(k + 1) * rshard], rg)

        packs = (g,) + _adamw(w_ref[...], g, m_ref[...], v_ref[...])
        matrices = (mats,) + _adamw(mw_ref[0], mats, mw_ref[1], mw_ref[2])
        convs = (cg,) + _adamw(cw_ref[0], cg, cw_ref[1], cw_ref[2])
        rconvs = (rg,) + _adamw(rw_ref[0], rg, rw_ref[1], rw_ref[2])
        for kind in range(4):
            out = dict(zip(names, leaves[kind]))
            for name, (row, width) in SMALL_VECTORS.items():
                out[name][...] = packs[kind][row:row + 1, 0:width]
            for n, name in enumerate(SMALL_MATRICES):
                out[name][...] = matrices[kind][n]
            out["conv_w"][...] = convs[kind][PK_CONV_W:PK_CONV_W + conv_rows, :]
            out["rnn_conv_w"][...] = rconvs[kind][0:rconv_rows, :]

    outs = pl.pallas_call(
        body, name="small_grads_step", in_specs=[VMEM] * 9, out_specs=[VMEM] * (1 + 4 * len(names)),
        out_shape=[jax.ShapeDtypeStruct((1, 1), F32)] + [jax.ShapeDtypeStruct(sh, F32) for sh in shapes] * 4,
        scratch_shapes=[pltpu.VMEM((PK_ROWS, cols), F32)],
        compiler_params=_params(),
    )(vec_packs, mat_packs, mix_g_blocks, w_pack, m_pack, v_pack, mats_wmv, conv_wmv, rconv_wmv)
    return outs[0], [dict(zip(names, outs[1 + k * len(names):1 + (k + 1) * len(names)])) for k in range(4)]


def _pack_params(p, pre):
    pack = jnp.zeros((PK_ROWS, D_MODEL), F32)
    for name, (row, width) in SMALL_VECTORS.items():
        pack = pack.at[row, :width].set(p[pre + name].reshape(-1))
    return pack


def _to_block_diag(w):
    w4 = w.reshape(N_BD, 4, 64, 64)
    return jnp.concatenate([jnp.pad(w4[:, q], ((0, 0), (0, 0), (64 * q, 64 * (3 - q)))) for q in range(4)], axis=1)


def _from_block_diag(d):
    d5 = d.reshape(N_BD, 4, HEAD_DIM, 4, HEAD_DIM)
    return jnp.stack([d5[:, q, :, q, :] for q in range(4)], axis=1).reshape(N_HEADS, HEAD_DIM, HEAD_DIM)


def _pad_rows(a):
    return jnp.pad(a, ((0, TILE_ROWS - a.shape[0]), (0, 0)))


_NAMES = ['norm_mix_g', 'w_in', 'conv_w', 'rnn_conv_w', 'rnn_conv_b', 'w_a', 'b_a', 'w_x', 'b_x', 'lru_lambda',
          'g_norm_conv', 'g_norm_rnn', 'w_out', 'norm_mlp_g', 'w_mlp_in', 'w_mlp_out', 'final_norm_g']


def kernel(x, norm_mix_g, w_in, conv_w, rnn_conv_w, rnn_conv_b, w_a, b_a, w_x, b_x, lru_lambda, g_norm_conv, g_norm_rnn, w_out, norm_mlp_g, w_mlp_in, w_mlp_out, final_norm_g, loss_target, m_norm_mix_g, m_w_in, m_conv_w, m_rnn_conv_w, m_rnn_conv_b, m_w_a, m_b_a, m_w_x, m_b_x, m_lru_lambda, m_g_norm_conv, m_g_norm_rnn, m_w_out, m_norm_mlp_g, m_w_mlp_in, m_w_mlp_out, m_final_norm_g, v_norm_mix_g, v_w_in, v_conv_w, v_rnn_conv_w, v_rnn_conv_b, v_w_a, v_b_a, v_w_x, v_b_x, v_lru_lambda, v_g_norm_conv, v_g_norm_rnn, v_w_out, v_norm_mlp_g, v_w_mlp_in, v_w_mlp_out, v_final_norm_g):
    args = dict(locals())
    p = {}
    for n in _NAMES:
        for pre in ("", "m_", "v_"):
            a = args[pre + n]
            p[pre + n] = a[0] if a.ndim >= 3 else a
    xs = x[0]
    target = loss_target[0]
    core_chip = jnp.stack([lax.axis_index("c"), 2 * lax.axis_index("x") + lax.axis_index("y")]).astype(jnp.int32)
    cshard = p["conv_w"].shape[1]
    rshard = p["rnn_conv_w"].shape[1]

    small = jnp.concatenate([_pad_rows(p["conv_w"]), _pad_rows(p["rnn_conv_w"])], axis=1)
    w_in_g, w_out_g, w1_g, w2_g, small_g = _gather_first(p["w_in"], p["w_out"], p["w_mlp_in"], p["w_mlp_out"], small)
    conv_full = small_g[:, :3, :cshard].transpose(1, 0, 2).reshape(3, CONV_W)
    rconv_full = small_g[:, :4, cshard:].transpose(1, 0, 2).reshape(4, LRU_W)
    wa_bd = _to_block_diag(p["w_a"]).astype(BF16)
    wx_bd = _to_block_diag(p["w_x"]).astype(BF16)
    gf = p["final_norm_g"].reshape(1, -1)
    lru = (wa_bd, p["b_a"], wx_bd, p["b_x"], p["lru_lambda"], p["g_norm_conv"], p["g_norm_rnn"])

    (u, h1b, xr, hs, c3, yb, gates), (w_out_g, w1_g, w2_g) = _fwd_mix(
        xs, p["norm_mix_g"], w_in_g, conv_full, rconv_full, p["rnn_conv_b"], *lru, (w_out_g, w1_g, w2_g))
    zb, dpb, h2b, dx3b, dx2, dx2b, dy, st_mlp = _mlp_fwd_bwd(
        xs, yb, w_out_g.reshape(-1, D_MODEL), w1_g, w2_g.reshape(-1, D_MODEL), p["norm_mlp_g"], gf, target)

    part_out = _wgrad(yb, dx2b, "out", core_chip)
    *part_1, arrived_out = _wgrad(h2b, dpb, "mlp_in", core_chip, parts=(part_out[1],))
    part_2 = _wgrad(zb, dx3b, "mlp_out", core_chip)
    (dub, st_mix, dwa_bd, dwx_bd), (arrived_1, arrived_2) = _mix_bwd(
        dy, u, xr, hs, c3, gates, conv_full, rconv_full, wa_bd, wx_bd, p["lru_lambda"], p["g_norm_conv"], p["g_norm_rnn"],
        (part_1[1], part_2[1]))
    arrived_mlp = (arrived_out, arrived_1, arrived_2)
    vec_pack = jnp.concatenate([st_mix, st_mlp], axis=0)
    mat_pack = jnp.stack([_from_block_diag(dwa_bd), _from_block_diag(dwx_bd)]).astype(BF16)
    *part_in, vec_packs, mat_packs = _wgrad(h1b, dub, "in", core_chip, packs=(vec_pack, mat_pack))
    early = (("w_out", "out", part_out, arrived_mlp[0]), ("w_mlp_in", "mlp_in", part_1, arrived_mlp[1]),
             ("w_mlp_out", "mlp_out", part_2, arrived_mlp[2]))
    (grad_x, st_in), arrived_in, joined = _in_bwd(
        dub, w_in_g, xs, dx2, p["norm_mix_g"], (part_in[1],),
        [(tag, p[n].shape, part[0], arrived) for n, tag, part, arrived in early], core_chip)
    g_in, mix_g_blocks = _join("in", p["w_in"].shape, part_in[0], arrived_in[0], core_chip, st_in)
    big = {}
    for n, tag, g in [(n, tag, g) for (n, tag, _, _), g in zip(early, joined)] + [("w_in", "in", g_in)]:
        big[n] = _adamw_big(p[n], g, p["m_" + n], p["v_" + n], "adamw_" + tag)

    conv_wmv = jnp.stack([jnp.pad(p[pre + "conv_w"], ((PK_CONV_W, 0), (0, 0))) for pre in ("", "m_", "v_")])
    rconv_wmv = jnp.stack([_pad_rows(p[pre + "rnn_conv_w"]) for pre in ("", "m_", "v_")])
    mats_wmv = jnp.stack([jnp.stack([p[pre + n] for n in SMALL_MATRICES]) for pre in ("", "m_", "v_")])
    loss, outs = _small_step(vec_packs, mat_packs, mix_g_blocks, _pack_params(p, ""), _pack_params(p, "m_"),
                             _pack_params(p, "v_"), mats_wmv, conv_wmv, rconv_wmv)
    for kind, o in enumerate(outs):
        o["final_norm_g"] = o["final_norm_g"].reshape(-1)
        for n in SMALL_MATRICES + ("conv_w", "rnn_conv_w"):
            o[n] = o[n][None]
        for n in ("w_in", "w_out", "w_mlp_in", "w_mlp_out"):
            o[n] = big[n][kind][None]
    loss = loss.reshape(())
    return (loss, grad_x[None], *[o[n] for o in outs for n in _NAMES])
```

```python
import functools
import math

import jax
import jax.numpy as jnp
from jax import lax
from jax.experimental import pallas as pl
from jax.experimental.pallas import tpu as pltpu

F32 = jnp.float32
BF16 = jnp.bfloat16
MESH = pl.DeviceIdType.MESH
ANY = pl.BlockSpec(memory_space=pl.ANY)
VMEM = pl.BlockSpec(memory_space=pltpu.VMEM)

EPS = 1e-6
LRU_C = 8.0
D_MODEL = 1024
CONV_W = 512
LRU_W = 1024
IN_COLS = 3 * CONV_W + 2 * LRU_W
IN_SHARD = IN_COLS // 4
N_CHIPS = 4
N_DEVICES = 8
BD = 256
N_BD = LRU_W // BD

ADAM_LR = 0.001
ADAM_B1 = 0.9
ADAM_B2 = 0.999
ADAM_EPS = 1e-08
ADAM_WD = 0.01
ADAM_STEP = 10
ADAM_BC1 = 1.0 - ADAM_B1 ** ADAM_STEP
ADAM_BC2 = 1.0 - ADAM_B2 ** ADAM_STEP

TILE_ROWS = 8
TOKEN_TILE = 256
MATMUL_TOKEN_TILE = 512
ADAMW_ROWS = 128
VMEM_LIMIT = 56 * 1024 * 1024

PK_G_NORM_RNN, PK_RCONV_B, PK_B_A, PK_B_X, PK_LAMBDA, PK_CONV_W = 0, 1, 2, 3, 4, 5
PK_RCONV_W, PK_G_NORM_CONV = 8, 12
PK_MIX_ROWS = 16
PK_FINAL_G, PK_MLP_G, PK_LOSS = 16, 17, 18
PK_MLP_ROWS = 8
PK_MIX_G = 24
PK_ROWS = 32
N_HEADS, HEAD_DIM = 16, 64


def _params(**kw):
    return pltpu.CompilerParams(vmem_limit_bytes=VMEM_LIMIT, **kw)


def _position():
    x, y, c = lax.axis_index("x"), lax.axis_index("y"), lax.axis_index("c")
    return x, y, c


def _sigmoid(v):
    return 1.0 / (1.0 + jnp.exp(-v))


def _one_minus_square(log_a, a):
    v = 2.0 * log_a
    series = -v * (1.0 + v * (0.5 + v * (1.0 / 6.0)))
    return jnp.where(v > -0.01, series, 1.0 - a * a)


_GELU_C = math.sqrt(2.0 / math.pi)
_GELU_K = 0.044715


def _gelu_and_grad(g):
    th = jnp.tanh(_GELU_C * (g + _GELU_K * g * g * g))
    gelu = 0.5 * g * (1.0 + th)
    dgelu = 0.5 * (1.0 + th) + 0.5 * g * (1.0 - th * th) * (_GELU_C * (1.0 + 3.0 * _GELU_K * g * g))
    return gelu, dgelu


def _rows(shape):
    return lax.broadcasted_iota(jnp.int32, shape, 0)


def _shift_down(v, k, prev8):
    rolled = pltpu.roll(v, k, 0)
    halo = pltpu.roll(prev8, k, 0)
    head = jnp.where(_rows(halo.shape) < k, halo, rolled[:TILE_ROWS])
    return jnp.concatenate([head, rolled[TILE_ROWS:]], axis=0)


def _shift_up(v, k, next8):
    n = v.shape[0]
    rolled = pltpu.roll(v, n - k, 0)
    halo = pltpu.roll(next8, TILE_ROWS - k, 0)
    tail = jnp.where(_rows(halo.shape) >= TILE_ROWS - k, halo, rolled[n - TILE_ROWS:])
    return jnp.concatenate([rolled[: n - TILE_ROWS], tail], axis=0)


def _scan_rows(a, b, carry, reverse=False):
    n, w = a.shape
    groups = n // TILE_ROWS
    a3 = a.reshape(groups, TILE_ROWS, w)
    b3 = b.reshape(groups, TILE_ROWS, w)
    sub = lax.broadcasted_iota(jnp.int32, a3.shape, 1)
    s = 1
    while s < TILE_ROWS:
        shift = TILE_ROWS - s if reverse else s
        keep = (sub < TILE_ROWS - s) if reverse else (sub >= s)
        b3 = b3 + jnp.where(keep, a3 * pltpu.roll(b3, shift, 1), 0.0)
        a3 = a3 * jnp.where(keep, pltpu.roll(a3, shift, 1), 1.0)
        s *= 2
    out = [None] * groups
    edge = 0 if reverse else TILE_ROWS - 1
    for g in (range(groups - 1, -1, -1) if reverse else range(groups)):
        out[g] = b3[g] + a3[g] * carry
        carry = out[g][edge:edge + 1]
    return jnp.concatenate(out, axis=0)


def _softplus_neg(lam):
    e = jnp.exp(-jnp.abs(lam))
    log1p_e = jnp.where(e < 1e-2, e * (1.0 - e * (0.5 - e * (1.0 / 3.0 - e * 0.25))), jnp.log(1.0 + e))
    sp = jnp.maximum(-lam, 0.0) + log1p_e
    dsp = -_sigmoid(-lam)
    return sp, dsp


def _block_diag_dot(vb, w_ref):
    return jnp.concatenate(
        [jnp.dot(vb[:, j * BD:(j + 1) * BD], w_ref[j], preferred_element_type=F32) for j in range(N_BD)], axis=1)


def _block_diag_dot_t(vb, w_ref):
    return jnp.concatenate(
        [lax.dot_general(vb[:, j * BD:(j + 1) * BD], w_ref[j], (((1,), (1,)), ((), ())), preferred_element_type=F32)
         for j in range(N_BD)], axis=1)


def _dot_nt(a, b):
    return lax.dot_general(a, b, (((1,), (1,)), ((), ())), preferred_element_type=F32)


def _dot_tn(a, b):
    return lax.dot_general(a, b, (((0,), (0,)), ((), ())), preferred_element_type=F32)


def _lru_gates(xr, wa_ref, ba, wx_ref, bx, sp):
    xrb = xr.astype(BF16)
    r = _sigmoid(_block_diag_dot(xrb, wa_ref) + ba)
    ig = _sigmoid(_block_diag_dot(xrb, wx_ref) + bx)
    log_a = (-LRU_C) * r * sp
    a = jnp.exp(log_a)
    mult = jnp.sqrt(_one_minus_square(log_a, a))
    return r, ig, a, mult


def _colsum(v):
    return jnp.sum(v, axis=0, keepdims=True)


N_FWD_OUT = 7


def _fwd_mix(x, g1, w_in_g, conv_w, rconv_w, rconv_b, wa_bd, b_a, wx_bd, b_x, lam, g_nc, g_nr, later):
    t, d = x.shape
    tm = TOKEN_TILE
    nt = t // tm
    nl = len(later)
    assert nl == 3
    pass_on_at = [nt * f // 16 for f in (3, 5, 9)]
    neighbours_at = [nt * f // 16 for f in (10, 11, 12)]
    diagonal_at = [nt * f // 16 for f in (13, 14, 14)]

    def body(x_ref, g1_ref, win_ref, cw_ref, rw_ref, rb_ref, wa_ref, ba_ref, wx_ref, bx_ref, lam_ref, gnc_ref, gnr_ref,
             *rest):
        later_in, outs, rest = rest[:nl], rest[nl:nl + N_FWD_OUT], rest[nl + N_FWD_OUT:]
        u_ref, h1_ref, xr_ref, hs_ref, c3_ref, y_ref, gates_ref = outs
        later_out, (cv_prev, xin_prev, h_prev, send_sems, recv_sems) = rest[:nl], rest[nl:]
        del later_in
        step = pl.program_id(0)
        plan = _ShardGather(later_out, send_sems, recv_sems)

        @pl.when(step == 0)
        def _():
            cv_prev[...] = jnp.zeros_like(cv_prev)
            xin_prev[...] = jnp.zeros_like(xin_prev)
            h_prev[...] = jnp.zeros_like(h_prev)
            for w in range(nl):
                plan.start_direct(w)

        for w in range(nl):
            @pl.when(step == pass_on_at[w])
            def _(w=w):
                plan.start_pass_on(w)

            @pl.when(step == neighbours_at[w])
            def _(w=w):
                plan.start_hand_over(w, diagonal=False)

            @pl.when(step == diagonal_at[w])
            def _(w=w):
                plan.start_hand_over(w, diagonal=True)

        xv = x_ref[...]
        rstd = lax.rsqrt(jnp.mean(xv * xv, axis=-1, keepdims=True) + EPS)
        h1b = ((xv * rstd) * g1_ref[...]).astype(BF16)
        h1_ref[...] = h1b
        for j in range(N_CHIPS):
            u_ref[:, j * IN_SHARD:(j + 1) * IN_SHARD] = jnp.dot(h1b, win_ref[j], preferred_element_type=F32)
        gate_b = u_ref[:, 0:CONV_W]
        cv = u_ref[:, CONV_W:2 * CONV_W] * u_ref[:, 2 * CONV_W:3 * CONV_W]
        x_r = u_ref[:, 3 * CONV_W:3 * CONV_W + LRU_W]
        g = u_ref[:, 3 * CONV_W + LRU_W:]

        cw = cw_ref[...]
        cvp = cv_prev[...]
        conv3 = cw[0:1] * _shift_down(cv, 2, cvp) + cw[1:2] * _shift_down(cv, 1, cvp) + cw[2:3] * cv
        cv_prev[...] = cv[tm - TILE_ROWS:]
        c3_ref[...] = conv3
        y_conv = gate_b * conv3

        rw = rw_ref[...]
        xp = xin_prev[...]
        xr = (rw[0:1] * _shift_down(x_r, 3, xp) + rw[1:2] * _shift_down(x_r, 2, xp)
              + rw[2:3] * _shift_down(x_r, 1, xp) + rw[3:4] * x_r) + rb_ref[...]
        xin_prev[...] = x_r[tm - TILE_ROWS:]
        xr_ref[...] = xr
        sp, _ = _softplus_neg(lam_ref[...])
        r, ig, a, mult = _lru_gates(xr, wa_ref, ba_ref[...], wx_ref, bx_ref[...], sp)
        for n, gate in enumerate((r, ig, a, mult)):
            gates_ref[:, n * LRU_W:(n + 1) * LRU_W] = gate
        h = _scan_rows(a, mult * (ig * xr), h_prev[...])
        h_prev[...] = h[tm - 1:tm]
        hs_ref[...] = h
        gelu, _ = _gelu_and_grad(g)
        y_rnn = h * gelu

        na = y_conv * lax.rsqrt(jnp.mean(y_conv * y_conv, axis=-1, keepdims=True) + EPS) * gnc_ref[...]
        nb = y_rnn * lax.rsqrt(jnp.mean(y_rnn * y_rnn, axis=-1, keepdims=True) + EPS) * gnr_ref[...]
        y_ref[:, :CONV_W] = na.astype(BF16)
        y_ref[:, CONV_W:] = nb.astype(BF16)

        @pl.when(step == nt - 1)
        def _():
            for w in range(nl):
                plan.finish(w)

    def full(a):
        nd = a.ndim
        return pl.BlockSpec(a.shape, lambda i: (0,) * nd)

    def tok(cols):
        return pl.BlockSpec((tm, cols), lambda i: (i, 0))

    def act(cols, dtype=F32):
        return jax.ShapeDtypeStruct((t, cols), dtype)

    smalls = (g1, w_in_g, conv_w, rconv_w, rconv_b, wa_bd, b_a, wx_bd, b_x, lam, g_nc, g_nr)
    n_in = 1 + len(smalls)
    outs = pl.pallas_call(
        body, name="fwd_mix", grid=(nt,),
        in_specs=[tok(d)] + [full(a) for a in smalls] + [ANY] * nl,
        out_specs=[tok(IN_COLS), tok(d), tok(LRU_W), tok(LRU_W), tok(CONV_W), tok(CONV_W + LRU_W)]
        + [tok(4 * LRU_W)] + [ANY] * nl,
        out_shape=[act(IN_COLS), act(d, BF16), act(LRU_W), act(LRU_W), act(CONV_W), act(CONV_W + LRU_W, BF16)]
        + [act(4 * LRU_W)] + [jax.ShapeDtypeStruct(a.shape, a.dtype) for a in later],
        input_output_aliases={n_in + w: N_FWD_OUT + w for w in range(nl)},
        scratch_shapes=[pltpu.VMEM((TILE_ROWS, CONV_W), F32), pltpu.VMEM((TILE_ROWS, LRU_W), F32),
                        pltpu.VMEM((1, LRU_W), F32), pltpu.SemaphoreType.DMA((nl, _ShardGather.PAIRS)),
                        pltpu.SemaphoreType.DMA((nl, _ShardGather.PAIRS))],
        compiler_params=_params(dimension_semantics=("arbitrary",)),
    )(x, *smalls, *later)
    return outs[:N_FWD_OUT], outs[N_FWD_OUT:]


def _mlp_fwd_bwd(x, yb, w_out_g, w1_g, w2_g, g2, gf, target):
    t, d = x.shape
    tm = TOKEN_TILE
    ff = w2_g.shape[0]
    mix = w_out_g.shape[0]
    ffs = ff // N_CHIPS

    def body(x_ref, y_ref, g2_ref, gf_ref, tgt_ref, wout_hbm, w1_hbm, w2_hbm,
             z_ref, dp_ref, h2_ref, dx3b_ref, dx2_ref, dx2b_ref, dy_ref, st_ref, wout, w1, w2, p_ref):
        @pl.when(pl.program_id(0) == 0)
        def _():
            pltpu.sync_copy(wout_hbm, wout)
            pltpu.sync_copy(w1_hbm, w1)
            pltpu.sync_copy(w2_hbm, w2)
            st_ref[...] = jnp.zeros_like(st_ref)

        x2 = x_ref[...] + jnp.dot(y_ref[...], wout[...], preferred_element_type=F32)
        r2 = lax.rsqrt(jnp.mean(x2 * x2, axis=-1, keepdims=True) + EPS)
        xh2 = x2 * r2
        g2v = g2_ref[...]
        h2b = (xh2 * g2v).astype(BF16)
        h2_ref[...] = h2b
        for j in range(N_CHIPS):
            p_ref[:, j * ffs:(j + 1) * ffs] = jnp.dot(h2b, w1[j], preferred_element_type=F32)
        rp = jnp.maximum(p_ref[...], 0.0)
        zb = (rp * rp).astype(BF16)
        z_ref[...] = zb
        x3 = x2 + jnp.dot(zb, w2[...], preferred_element_type=F32)
        r3 = lax.rsqrt(jnp.mean(x3 * x3, axis=-1, keepdims=True) + EPS)
        xh3 = x3 * r3
        gfv = gf_ref[...]
        err = xh3 * gfv - tgt_ref[...]
        loss = (0.5 / d) * jnp.sum(err * err)
        dout = err * (1.0 / d)
        st_ref[PK_FINAL_G - PK_MIX_ROWS:PK_FINAL_G - PK_MIX_ROWS + 1, :] += _colsum(dout * xh3)
        st_ref[PK_LOSS - PK_MIX_ROWS:PK_LOSS - PK_MIX_ROWS + 1, :] += jnp.zeros((1, d), F32) + loss
        dxh3 = dout * gfv
        dx3 = r3 * (dxh3 - xh3 * jnp.mean(dxh3 * xh3, axis=-1, keepdims=True))
        dx3b = dx3.astype(BF16)
        dx3b_ref[...] = dx3b
        dpb = (_dot_nt(dx3b, w2[...]) * (2.0 * rp)).astype(BF16)
        dp_ref[...] = dpb
        dh2 = _dot_nt(dpb[:, 0:ffs], w1[0])
        for j in range(1, N_CHIPS):
            dh2 = dh2 + _dot_nt(dpb[:, j * ffs:(j + 1) * ffs], w1[j])
        st_ref[PK_MLP_G - PK_MIX_ROWS:PK_MLP_G - PK_MIX_ROWS + 1, :] += _colsum(dh2 * xh2)
        dxh2 = dh2 * g2v
        dx2 = dx3 + r2 * (dxh2 - xh2 * jnp.mean(dxh2 * xh2, axis=-1, keepdims=True))
        dx2_ref[...] = dx2
        dx2b = dx2.astype(BF16)
        dx2b_ref[...] = dx2b
        dy_ref[...] = _dot_nt(dx2b, wout[...])

    def tok(cols):
        return pl.BlockSpec((tm, cols), lambda i: (i, 0))

    def row(cols):
        return pl.BlockSpec((1, cols), lambda i: (0, 0))

    return pl.pallas_call(
        body, name="mlp_fwd_bwd", grid=(t // tm,),
        in_specs=[tok(d), tok(mix), row(d), row(d), tok(d), ANY, ANY, ANY],
        out_specs=[tok(ff), tok(ff), tok(d), tok(d), tok(d), tok(d), tok(mix),
                   pl.BlockSpec((PK_MLP_ROWS, d), lambda i: (0, 0))],
        out_shape=[jax.ShapeDtypeStruct((t, ff), BF16), jax.ShapeDtypeStruct((t, ff), BF16),
                   jax.ShapeDtypeStruct((t, d), BF16), jax.ShapeDtypeStruct((t, d), BF16),
                   jax.ShapeDtypeStruct((t, d), F32), jax.ShapeDtypeStruct((t, d), BF16),
                   jax.ShapeDtypeStruct((t, mix), F32), jax.ShapeDtypeStruct((PK_MLP_ROWS, d), F32)],
        scratch_shapes=[pltpu.VMEM(w_out_g.shape, BF16), pltpu.VMEM(w1_g.shape, BF16), pltpu.VMEM(w2_g.shape, BF16),
                        pltpu.VMEM((tm, ff), F32)],
        compiler_params=_params(dimension_semantics=("arbitrary",)),
    )(x, yb, g2, gf, target, w_out_g, w1_g, w2_g)


def _mix_bwd(dy, u, xr_all, hs_all, c3_all, gates, conv_w, rconv_w, wa_bd, wx_bd, lam, g_nc, g_nr, parts):
    t = dy.shape[0]
    tm = TOKEN_TILE
    nt = t // tm
    hb = tm // TILE_ROWS
    npart = len(parts)

    def body(dy_ref, u_ref, uh_ref, xr_ref, hs_ref, hh_ref, c3_ref, gates_ref,
             cw_ref, rw_ref, wa_ref, wx_ref, lam_ref, gnc_ref, gnr_ref, *rest):
        part_refs, (du_ref, st_ref, dwa_ref, dwx_ref), rest = rest[:npart], rest[npart:npart + 4], rest[npart + 4:]
        arrived_refs, (dc_next, a_next, gs_next, dxr_next, send_sems, recv_sems) = rest[:npart], rest[npart:]
        exchange = _PartialExchange(part_refs, arrived_refs, send_sems, recv_sems)
        i = pl.program_id(0)

        @pl.when(i == 0)
        def _():
            exchange.start()
            dc_next[...] = jnp.zeros_like(dc_next)
            a_next[...] = jnp.zeros_like(a_next)
            gs_next[...] = jnp.zeros_like(gs_next)
            dxr_next[...] = jnp.zeros_like(dxr_next)
            st_ref[...] = jnp.zeros_like(st_ref)
            dwa_ref[...] = jnp.zeros_like(dwa_ref)
            dwx_ref[...] = jnp.zeros_like(dwx_ref)

        first_tile = i == nt - 1
        gate_b = u_ref[:, 0:CONV_W]
        gate_c = u_ref[:, CONV_W:2 * CONV_W]
        v = u_ref[:, 2 * CONV_W:3 * CONV_W]
        x_r = u_ref[:, 3 * CONV_W:3 * CONV_W + LRU_W]
        g = u_ref[:, 3 * CONV_W + LRU_W:]
        cv = gate_c * v
        cv_prev = jnp.where(first_tile, 0.0, uh_ref[:, CONV_W:2 * CONV_W] * uh_ref[:, 2 * CONV_W:3 * CONV_W])
        xin_prev = jnp.where(first_tile, 0.0, uh_ref[:, 3 * CONV_W:3 * CONV_W + LRU_W])
        hs_prev = jnp.where(first_tile, 0.0, hh_ref[...])

        def acc(first_row, val, width=LRU_W, row=0):
            r0 = first_row + row
            st_ref[r0:r0 + 1, 0:width] += val

        conv3 = c3_ref[...]
        y_conv = gate_b * conv3
        ra = lax.rsqrt(jnp.mean(y_conv * y_conv, axis=-1, keepdims=True) + EPS)
        xha = y_conv * ra
        dna = dy_ref[:, :CONV_W]
        acc(PK_G_NORM_CONV, _colsum(dna * xha), CONV_W)
        dxha = dna * gnc_ref[...]
        dy_conv = ra * (dxha - xha * jnp.mean(dxha * xha, axis=-1, keepdims=True))
        du_ref[:, 0:CONV_W] = (dy_conv * conv3).astype(BF16)
        dc = dy_conv * gate_b
        cw = cw_ref[...]
        dcn = dc_next[...]
        dcv = cw[2:3] * dc + cw[1:2] * _shift_up(dc, 1, dcn) + cw[0:1] * _shift_up(dc, 2, dcn)
        dc_next[...] = dc[:TILE_ROWS]
        acc(PK_CONV_W, _colsum(dc * _shift_down(cv, 2, cv_prev)), CONV_W, 0)
        acc(PK_CONV_W, _colsum(dc * _shift_down(cv, 1, cv_prev)), CONV_W, 1)
        acc(PK_CONV_W, _colsum(dc * cv), CONV_W, 2)
        du_ref[:, CONV_W:2 * CONV_W] = (dcv * v).astype(BF16)
        du_ref[:, 2 * CONV_W:3 * CONV_W] = (dcv * gate_c).astype(BF16)

        hs = hs_ref[...]
        gelu, dgelu = _gelu_and_grad(g)
        y_rnn = hs * gelu
        rb = lax.rsqrt(jnp.mean(y_rnn * y_rnn, axis=-1, keepdims=True) + EPS)
        xhb = y_rnn * rb
        dnb = dy_ref[:, CONV_W:]
        acc(PK_G_NORM_RNN, _colsum(dnb * xhb))
        dxhb = dnb * gnr_ref[...]
        dy_rnn = rb * (dxhb - xhb * jnp.mean(dxhb * xhb, axis=-1, keepdims=True))
        du_ref[:, 3 * CONV_W + LRU_W:] = (dy_rnn * hs * dgelu).astype(BF16)
        dh = dy_rnn * gelu

        xr = xr_ref[...]
        xrb = xr.astype(BF16)
        sp, dsp = _softplus_neg(lam_ref[...])
        r, ig, a, mult = [gates_ref[:, n * LRU_W:(n + 1) * LRU_W] for n in range(4)]
        a_up = _shift_up(a, 1, a_next[...])
        a_next[...] = a[:TILE_ROWS]
        gs = _scan_rows(a_up, dh, gs_next[0:1, :], reverse=True)
        gs_next[...] = gs[:TILE_ROWS]
        da = gs * _shift_down(hs, 1, hs_prev)
        gx = gs * xr
        di = gx * mult
        dmult = gx * ig
        dxr = gs * (mult * ig)
        dlog_a = da * a - dmult * ((a * a) / mult)
        acc(PK_LAMBDA, _colsum(dlog_a * r) * ((-LRU_C) * dsp))
        dpa = (dlog_a * ((-LRU_C) * sp)) * (r * (1.0 - r))
        dpx = di * (ig * (1.0 - ig))
        acc(PK_B_A, _colsum(dpa))
        acc(PK_B_X, _colsum(dpx))
        dpab = dpa.astype(BF16)
        dpxb = dpx.astype(BF16)
        dxr = dxr + _block_diag_dot_t(dpab, wa_ref) + _block_diag_dot_t(dpxb, wx_ref)
        for j in range(N_BD):
            cols = slice(j * BD, (j + 1) * BD)
            dwa_ref[j] += _dot_tn(xrb[:, cols], dpab[:, cols])
            dwx_ref[j] += _dot_tn(xrb[:, cols], dpxb[:, cols])

        acc(PK_RCONV_B, _colsum(dxr))
        rw = rw_ref[...]
        dxn = dxr_next[...]
        dx_r = (rw[3:4] * dxr + rw[2:3] * _shift_up(dxr, 1, dxn) + rw[1:2] * _shift_up(dxr, 2, dxn)
                + rw[0:1] * _shift_up(dxr, 3, dxn))
        dxr_next[...] = dxr[:TILE_ROWS]
        for k in range(3):
            acc(PK_RCONV_W, _colsum(dxr * _shift_down(x_r, 3 - k, xin_prev)), LRU_W, k)
        acc(PK_RCONV_W, _colsum(dxr * x_r), LRU_W, 3)
        du_ref[:, 3 * CONV_W:3 * CONV_W + LRU_W] = dx_r.astype(BF16)

        @pl.when(i == nt - 1)
        def _():
            exchange.wait()

    def full(a):
        nd = a.ndim
        return pl.BlockSpec(a.shape, lambda i: (0,) * nd)

    def tok(cols):
        return pl.BlockSpec((tm, cols), lambda i: (nt - 1 - i, 0))

    def halo(cols):
        return pl.BlockSpec((TILE_ROWS, cols), lambda i: (jnp.maximum((nt - 1 - i) * hb - 1, 0), 0))

    smalls = (conv_w, rconv_w, wa_bd, wx_bd, lam, g_nc, g_nr)
    outs = pl.pallas_call(
        body, name="mix_bwd", grid=(nt,),
        in_specs=[tok(CONV_W + LRU_W), tok(IN_COLS), halo(IN_COLS), tok(LRU_W), tok(LRU_W), halo(LRU_W), tok(CONV_W)]
        + [tok(4 * LRU_W)] + [full(a) for a in smalls] + [ANY] * npart,
        out_specs=[tok(IN_COLS), pl.BlockSpec((PK_MIX_ROWS, LRU_W), lambda i: (0, 0)),
                   pl.BlockSpec((N_BD, BD, BD), lambda i: (0, 0, 0)), pl.BlockSpec((N_BD, BD, BD), lambda i: (0, 0, 0))]
        + [ANY] * npart,
        out_shape=[jax.ShapeDtypeStruct((t, IN_COLS), BF16), jax.ShapeDtypeStruct((PK_MIX_ROWS, LRU_W), F32),
                   jax.ShapeDtypeStruct((N_BD, BD, BD), F32), jax.ShapeDtypeStruct((N_BD, BD, BD), F32)]
        + [jax.ShapeDtypeStruct(a.shape, a.dtype) for a in parts],
        scratch_shapes=[pltpu.VMEM((TILE_ROWS, CONV_W), F32), pltpu.VMEM((TILE_ROWS, LRU_W), F32),
                        pltpu.VMEM((TILE_ROWS, LRU_W), F32), pltpu.VMEM((TILE_ROWS, LRU_W), F32),
                        pltpu.SemaphoreType.DMA((npart, 3)), pltpu.SemaphoreType.DMA((npart, 3))],
        compiler_params=_params(dimension_semantics=("arbitrary",)),
    )(dy, u, u, xr_all, hs_all, hs_all, c3_all, gates, *smalls, *parts)
    return outs[:4], outs[4:]


def _in_bwd(dub, w_in_g, x, dx2, g1, parts, joins, core_chip):
    t, d = x.shape
    tm = min(t, MATMUL_TOKEN_TILE)
    nt = t // tm
    npart = len(parts)
    nj = len(joins)
    geometry = []
    for tag, shape, _, _ in joins:
        pr, pc = WGRAD_GEOMETRY[tag][:2]
        every = 1 if pr % (nt * 16) == 0 else 2
        geometry.append((pr, pc, pr * every // nt, every, shape[1] == pc))

    def body(cc_ref, du_ref, win_ref, x_ref, dx2_ref, g1_ref, *rest):
        sums, rest = [rest[4 * w:4 * w + 4] for w in range(nj)], rest[4 * nj:]
        part_refs, (gx_ref, st_ref), rest = rest[:npart], rest[npart:npart + 2], rest[npart + 2:]
        arrived_refs, joined, rest = rest[:npart], rest[npart:npart + nj], rest[npart + nj:]
        stages, (send_sems, recv_sems, j_local, j_send, j_recv) = rest[:nj], rest[nj:]
        exchange = _PartialExchange(part_refs, arrived_refs, send_sems, recv_sems)
        i = pl.program_id(0)
        c = cc_ref[0]

        def window(w, core, row0, rows):
            pr, pc, _, _, by_rows = geometry[w]
            if by_rows:
                return joined[w].at[pl.ds(core * pr + row0, rows), :]
            return joined[w].at[pl.ds(row0, rows), pl.ds(core * pc, pc)]

        def to_sibling(w, src, core, row0, rows):
            return pltpu.make_async_remote_copy(src_ref=src, dst_ref=window(w, core, row0, rows), send_sem=j_send.at[w],
                                                recv_sem=j_recv.at[w], device_id=_sibling(), device_id_type=MESH)

        @pl.when(i == 0)
        def _():
            exchange.start()
            st_ref[...] = jnp.zeros_like(st_ref)

        for w in range(nj):
            pr, pc, rb, every, _ = geometry[w]

            @pl.when(i % every == 0)
            def _(w=w, rb=rb, every=every):
                p_ref, r1_ref, r2_ref, r3_ref = sums[w]
                row0 = pl.multiple_of((i // every) * rb, rb)
                rows = stages[w].at[pl.ds(row0, rb), :]
                rows[...] = ((p_ref[0] + r1_ref[0].astype(F32)) + r2_ref[0].astype(F32)) + r3_ref[0].astype(F32)
                pltpu.make_async_copy(rows, window(w, c, row0, rb), j_local.at[w]).start()
                to_sibling(w, rows, c, row0, rb).start()

        dh1 = _dot_nt(du_ref[:, 0:IN_SHARD], win_ref[0])
        for j in range(1, N_CHIPS):
            dh1 = dh1 + _dot_nt(du_ref[:, j * IN_SHARD:(j + 1) * IN_SHARD], win_ref[j])
        xv = x_ref[...]
        rstd = lax.rsqrt(jnp.mean(xv * xv, axis=-1, keepdims=True) + EPS)
        xh = xv * rstd
        st_ref[0:1, :] += _colsum(dh1 * xh)
        dxh = dh1 * g1_ref[...]
        gx_ref[...] = dx2_ref[...] + rstd * (dxh - xh * jnp.mean(dxh * xh, axis=-1, keepdims=True))

        @pl.when(i == nt - 1)
        def _():
            exchange.wait()
            for w in range(nj):
                pr = geometry[w][0]
                pltpu.make_async_copy(stages[w], window(w, c, 0, pr), j_local.at[w]).wait()
                to_sibling(w, stages[w], 1 - c, 0, pr).wait()

    def tok(cols):
        return pl.BlockSpec((tm, cols), lambda i, cc: (i, 0))

    def partial(w, off):
        pr, pc, rb, every, _ = geometry[w]
        return pl.BlockSpec((1, rb, pc), lambda i, cc: ((cc[1] + off) % N_CHIPS, i // every, 0))

    sum_specs, sum_operands = [], []
    for w, (_, _, own, arrived) in enumerate(joins):
        sum_specs += [partial(w, off) for off in range(N_CHIPS)]
        sum_operands += [own, arrived, arrived, arrived]
    dma = pltpu.SemaphoreType.DMA
    outs = pl.pallas_call(
        body, name="in_bwd",
        grid_spec=pltpu.PrefetchScalarGridSpec(
            num_scalar_prefetch=1, grid=(nt,),
            in_specs=[tok(IN_COLS), pl.BlockSpec(w_in_g.shape, lambda i, cc: (0, 0, 0)), tok(d), tok(d),
                      pl.BlockSpec((1, d), lambda i, cc: (0, 0))] + sum_specs + [ANY] * npart,
            out_specs=[tok(d), pl.BlockSpec((TILE_ROWS, d), lambda i, cc: (0, 0))] + [ANY] * (npart + nj),
            scratch_shapes=[pltpu.VMEM((g[0], g[1]), F32) for g in geometry]
            + [dma((npart, 3)), dma((npart, 3)), dma((nj,)), dma((nj,)), dma((nj,))]),
        out_shape=[jax.ShapeDtypeStruct((t, d), F32), jax.ShapeDtypeStruct((TILE_ROWS, d), F32)]
        + [jax.ShapeDtypeStruct(a.shape, a.dtype) for a in parts]
        + [jax.ShapeDtypeStruct(shape, F32) for _, shape, _, _ in joins],
        compiler_params=_params(dimension_semantics=("arbitrary",)),
    )(core_chip, dub, w_in_g, x, dx2, g1, *sum_operands, *parts)
    return outs[:2], outs[2:2 + npart], outs[2 + npart:]


WGRAD_GEOMETRY = {
    "in": (512, IN_SHARD, lambda s, h: h, lambda s, h: s),
    "mlp_in": (512, D_MODEL, lambda s, h: h, lambda s, h: s),
    "mlp_out": (512, D_MODEL, lambda s, h: 2 * s + h, lambda s, h: 0),
    "out": (384, 512, lambda s, h: s, lambda s, h: h),
}
K_CHUNK = 512


def _sibling():
    x, y, c = _position()
    return (x, y, 1 - c)


def _wgrad(a, b, tag, core_chip, packs=(), parts=()):
    t = a.shape[0]
    pr, pc, a_blk, b_blk = WGRAD_GEOMETRY[tag]
    nk = t // K_CHUNK
    mine = N_CHIPS
    riding = len(packs)
    npart = len(parts)
    assert not (riding and npart)

    def body(cc_ref, a_ref, b_ref, *rest):
        if riding:
            pack_refs, (land_ref, p_ref, pb_ref), rest = rest[:riding], rest[riding:riding + 3], rest[riding + 3:]
            all_refs, (stage, rbuf, send_sems, recv_sems, rsem), g_sems = rest[:riding], rest[riding:riding + 5], rest[riding + 5:]
            gathers = [_PackGather(pack_refs[n], all_refs[n], *g_sems[3 * n:3 * n + 3]) for n in range(riding)]
        elif npart:
            part_refs, (land_ref, p_ref, pb_ref), rest = rest[:npart], rest[npart:npart + 3], rest[npart + 3:]
            arrived_refs, (stage, rbuf, send_sems, recv_sems, rsem, x_send, x_recv) = rest[:npart], rest[npart:]
            exchange = _PartialExchange(part_refs, arrived_refs, x_send, x_recv)
        else:
            land_ref, p_ref, pb_ref, stage, rbuf, send_sems, recv_sems, rsem = rest
        ph, s = pl.program_id(0), pl.program_id(1)
        if riding:
            @pl.when((ph == 0) & (s == 0))
            def _():
                for gather in gathers:
                    gather.start()

            @pl.when((ph == 1) & (s == N_CHIPS - 2))
            def _():
                for gather in gathers:
                    gather.hand_over()
        if npart:
            @pl.when((ph == 0) & (s == 0))
            def _():
                exchange.start()
        def push(k):
            return pltpu.make_async_remote_copy(src_ref=stage.at[k], dst_ref=land_ref.at[k], send_sem=send_sems.at[k],
                                                recv_sem=recv_sems.at[k], device_id=_sibling(), device_id_type=MESH)

        def landed():
            return pltpu.make_async_copy(land_ref.at[s], rbuf, rsem)

        @pl.when(ph == 1)
        def _():
            push(s).wait_recv()
            landed().start()

        slot = jnp.where(ph == 0, s, mine)
        acc = stage.at[slot]
        acc[...] = _dot_tn(a_ref[0:K_CHUNK, :], b_ref[0:K_CHUNK, :])
        for k in range(1, nk):
            acc[...] += _dot_tn(a_ref[k * K_CHUNK:(k + 1) * K_CHUNK, :], b_ref[k * K_CHUNK:(k + 1) * K_CHUNK, :])

        @pl.when(ph == 0)
        def _():
            push(s).start()

        @pl.when(ph == 1)
        def _():
            landed().wait()
            p = stage[mine] + rbuf[...]
            p_ref[0] = p
            pb_ref[0] = p.astype(BF16)

        @pl.when((ph == 1) & (s == N_CHIPS - 1))
        def _():
            for k in range(N_CHIPS):
                push(k).wait_send()
            for gather in (gathers if riding else ()):
                gather.finish()
            if npart:
                exchange.wait()

    def half(ph, cc):
        return jnp.where(ph == 0, 1 - cc[0], cc[0])

    def out_slot(ph, s, cc):
        return (jnp.where(ph == 0, 0, s), 0, 0)

    piece = jax.ShapeDtypeStruct((N_CHIPS, pr, pc), F32)
    in_specs = [pl.BlockSpec((t, pr), lambda ph, s, cc: (0, a_blk(s, half(ph, cc)))),
                pl.BlockSpec((t, pc), lambda ph, s, cc: (0, b_blk(s, half(ph, cc))))]
    out_specs = [ANY, pl.BlockSpec((1, pr, pc), out_slot), pl.BlockSpec((1, pr, pc), out_slot)]
    out_shape = [piece, piece, jax.ShapeDtypeStruct((N_CHIPS, pr, pc), BF16)]
    scratch = [pltpu.VMEM((N_CHIPS + 1, pr, pc), F32), pltpu.VMEM((pr, pc), F32),
               pltpu.SemaphoreType.DMA((N_CHIPS,)), pltpu.SemaphoreType.DMA((N_CHIPS,)), pltpu.SemaphoreType.DMA]
    operands = [a, b]
    for pack in packs:
        in_specs.append(pl.BlockSpec(pack.shape, lambda ph, s, cc, nd=pack.ndim: (0,) * nd))
        out_specs.append(ANY)
        out_shape.append(jax.ShapeDtypeStruct((N_DEVICES,) + pack.shape, pack.dtype))
        operands.append(pack)
    for pack in packs:
        scratch += _PackGather.semaphores()
    if npart:
        in_specs += [ANY] * npart
        out_specs += [ANY] * npart
        out_shape += [jax.ShapeDtypeStruct(p.shape, p.dtype) for p in parts]
        scratch += [pltpu.SemaphoreType.DMA((npart, 3)), pltpu.SemaphoreType.DMA((npart, 3))]
        operands += list(parts)
    return pl.pallas_call(
        body, name="wgrad_" + tag,
        grid_spec=pltpu.PrefetchScalarGridSpec(
            num_scalar_prefetch=1, grid=(2, N_CHIPS), in_specs=in_specs, out_specs=out_specs, scratch_shapes=scratch),
        out_shape=out_shape,
        compiler_params=_params(dimension_semantics=("arbitrary", "arbitrary")),
    )(core_chip, *operands)[1:]


def _other_chips(x, y):
    return [(1 - x, y), (x, 1 - y), (1 - x, 1 - y)]


class _ShardGather:
    PAIRS = 9

    def __init__(self, outs, send_sems, recv_sems):
        self.outs, self.send_sems, self.recv_sems = outs, send_sems, recv_sems
        x, y, c = _position()
        self.c, self.j = c, 2 * x + y
        self.sibling = (x, y, 1 - c)
        self.chips = _other_chips(x, y)

    def _chip(self, k):
        px, py = self.chips[k]
        return 2 * px + py

    def _half(self, w, chip, which):
        hr = self.outs[w].shape[1] // 2
        return self.outs[w].at[chip, pl.ds(which * hr, hr), :]

    def _quarter(self, w, chip, q):
        qr = self.outs[w].shape[1] // 4
        return self.outs[w].at[chip, pl.ds(self.c * 2 * qr + q * qr, qr), :]

    def _copy(self, ref, w, pair, to, src=None):
        return pltpu.make_async_remote_copy(src_ref=ref if src is None else src, dst_ref=ref, send_sem=self.send_sems.at[w, pair],
                                            recv_sem=self.recv_sems.at[w, pair], device_id=to, device_id_type=MESH)

    def direct(self, w, k, q, src=None):
        return self._copy(self._quarter(w, self.j, q), w, 2 * k + q, (*self.chips[k], self.c), src)

    def direct_landed(self, w, k, q):
        return self._copy(self._quarter(w, self._chip(k), q), w, 2 * k + q, (*self.chips[k], self.c))

    def pass_on(self, w, q):
        return self._copy(self._quarter(w, self._chip(q), q), w, 4 + q, (*self.chips[1 - q], self.c))

    def passed_landed(self, w, q):
        return self._copy(self._quarter(w, self._chip(2), q), w, 4 + q, (*self.chips[1 - q], self.c))

    def hand_over(self, w, k):
        return self._copy(self._half(w, self._chip(k), self.c), w, 6 + k, self.sibling)

    def handed(self, w, k):
        return self._copy(self._half(w, self._chip(k), 1 - self.c), w, 6 + k, self.sibling)

    def start_direct(self, w, src_half=None):
        qr = self.outs[w].shape[1] // 4
        for k, q in ((0, 0), (1, 1), (0, 1), (1, 0)):
            self.direct(w, k, q, None if src_half is None else src_half.at[pl.ds(q * qr, qr), :]).start()

    def start_pass_on(self, w):
        for q in (0, 1):
            self.direct_landed(w, q, q).wait_recv()
            self.pass_on(w, q).start()

    def start_hand_over(self, w, diagonal):
        if diagonal:
            for q in (0, 1):
                self.passed_landed(w, q).wait_recv()
            self.hand_over(w, 2).start()
        else:
            for k in (0, 1):
                self.direct_landed(w, k, 1 - k).wait_recv()
                self.hand_over(w, k).start()

    def finish(self, w):
        for k in range(3):
            self.handed(w, k).wait_recv()
            self.hand_over(w, k).wait_send()
        for q in (0, 1):
            self.pass_on(w, q).wait_send()
            for k in (0, 1):
                self.direct(w, k, q).wait_send()


def _gather_first(w_in, w_out, w1, w2, small):
    bigs = (w_in, w_out, w1, w2)
    nb = len(bigs)

    def body(win_ref, wout_ref, w1_ref, w2_ref, sm_ref, gin, gout, g1, g2, gsm, st_in, st_out, st_1, st_2,
             send_sems, recv_sems, sm_send, sm_recv, local_sems):
        srcs = (win_ref, wout_ref, w1_ref, w2_ref)
        stages = (st_in, st_out, st_1, st_2)
        outs = (gin, gout, g1, g2)
        plan = _ShardGather(outs[:1], send_sems, recv_sems)
        j, c = plan.j, plan.c
        local = [pltpu.make_async_copy(stages[w], outs[w].at[j], local_sems.at[w]) for w in range(nb)]
        local.append(pltpu.make_async_copy(sm_ref, gsm.at[j], local_sems.at[nb]))

        def small_copy(k):
            px, py = plan.chips[k]
            return pltpu.make_async_remote_copy(src_ref=sm_ref, dst_ref=gsm.at[j], send_sem=sm_send.at[k],
                                                recv_sem=sm_recv.at[k], device_id=(px, py, c), device_id_type=MESH)

        def small_landed(k):
            px, py = plan.chips[k]
            return pltpu.make_async_remote_copy(src_ref=sm_ref, dst_ref=gsm.at[2 * px + py], send_sem=sm_send.at[k],
                                                recv_sem=sm_recv.at[k], device_id=(px, py, c), device_id_type=MESH)

        hr = w_in.shape[0] // 2
        st_in[...] = win_ref[...].astype(BF16)
        plan.start_direct(0, st_in.at[pl.ds(c * hr, hr), :])
        for k in range(3):
            small_copy(k).start()
        for src, st in zip(srcs[1:], stages[1:]):
            st[...] = src[...].astype(BF16)
        for cp in local:
            cp.start()
        plan.start_pass_on(0)
        plan.start_hand_over(0, diagonal=False)
        plan.start_hand_over(0, diagonal=True)
        for k in range(3):
            small_landed(k).wait_recv()
            small_copy(k).wait_send()
        plan.finish(0)
        for cp in local:
            cp.wait()

    def gathered(a, dtype):
        return jax.ShapeDtypeStruct((N_CHIPS,) + a.shape, dtype)

    return pl.pallas_call(
        body, name="gather_first",
        in_specs=[VMEM] * 5, out_specs=[ANY] * 5,
        out_shape=[gathered(a, BF16) for a in bigs] + [gathered(small, F32)],
        scratch_shapes=[pltpu.VMEM(a.shape, BF16) for a in bigs]
        + [pltpu.SemaphoreType.DMA((1, _ShardGather.PAIRS)), pltpu.SemaphoreType.DMA((1, _ShardGather.PAIRS)), pltpu.SemaphoreType.DMA((3,)),
           pltpu.SemaphoreType.DMA((3,)), pltpu.SemaphoreType.DMA((nb + 1,))],
        compiler_params=_params(),
    )(*bigs, small)


class _PartialExchange:
    def __init__(self, parts, arrived, send_sems, recv_sems):
        self.parts, self.arrived, self.send_sems, self.recv_sems = parts, arrived, send_sems, recv_sems
        x, y, c = _position()
        self.c, self.j = c, 2 * x + y
        self.chips = _other_chips(x, y)

    def _copy(self, w, k, slot):
        px, py = self.chips[k]
        return pltpu.make_async_remote_copy(
            src_ref=self.parts[w].at[2 * px + py], dst_ref=self.arrived[w].at[slot], send_sem=self.send_sems.at[w, k],
            recv_sem=self.recv_sems.at[w, k], device_id=(px, py, self.c), device_id_type=MESH)

    def start(self):
        for w in range(len(self.parts)):
            for k in range(3):
                self._copy(w, k, self.j).start()

    def wait(self):
        for w in range(len(self.parts)):
            for k in range(3):
                px, py = self.chips[k]
                self._copy(w, k, 2 * px + py).wait()


class _PackGather:
    def __init__(self, p_ref, all_ref, send_sems, recv_sems, local_sem):
        self.p_ref, self.all_ref, self.send_sems, self.recv_sems, self.local_sem = p_ref, all_ref, send_sems, recv_sems, local_sem
        x, y, c = _position()
        self.me, self.sibling, self.c = (x, y, c), (x, y, 1 - c), c
        self.chips = _other_chips(x, y)

    @staticmethod
    def semaphores():
        return [pltpu.SemaphoreType.DMA((7,)), pltpu.SemaphoreType.DMA((7,)), pltpu.SemaphoreType.DMA]

    def _copy(self, k, block, to, from_pack=False):
        px, py, pc = block
        slot = self.all_ref.at[4 * px + 2 * py + pc]
        return pltpu.make_async_remote_copy(src_ref=self.p_ref if from_pack else slot, dst_ref=slot, send_sem=self.send_sems.at[k],
                                            recv_sem=self.recv_sems.at[k], device_id=to, device_id_type=MESH)

    def _mine(self):
        x, y, c = self.me
        return pltpu.make_async_copy(self.p_ref, self.all_ref.at[4 * x + 2 * y + c], self.local_sem)

    def _first(self):
        return [self._copy(0, self.me, self.sibling, True)] + [
            self._copy(1 + k, self.me, (*chip, self.c), True) for k, chip in enumerate(self.chips)]

    def _passed(self):
        return [self._copy(4 + k, (*chip, self.c), self.sibling) for k, chip in enumerate(self.chips)]

    def start(self):
        self._mine().start()
        for cp in self._first():
            cp.start()

    def hand_over(self):
        for k, chip in enumerate(self.chips):
            self._copy(1 + k, (*chip, self.c), self.me).wait_recv()
            self._passed()[k].start()

    def finish(self):
        self._copy(0, self.sibling, self.me).wait_recv()
        for k, chip in enumerate(self.chips):
            self._copy(4 + k, (*chip, 1 - self.c), self.me).wait_recv()
        for cp in self._first() + self._passed():
            cp.wait_send()
        self._mine().wait()


class _DirectGather:
    def __init__(self, p_ref, all_ref, send_sems, recv_sems, local_sem):
        self.p_ref, self.all_ref, self.send_sems, self.recv_sems, self.local_sem = p_ref, all_ref, send_sems, recv_sems, local_sem
        self.me = _position()

    semaphores = _PackGather.semaphores

    def _peer(self, r):
        x, y, c = self.me
        return ((1 - x) if r & 4 else x, (1 - y) if r & 2 else y, (1 - c) if r & 1 else c)

    def _copy(self, r, slot_of):
        px, py, pc = slot_of
        return pltpu.make_async_remote_copy(src_ref=self.p_ref, dst_ref=self.all_ref.at[4 * px + 2 * py + pc],
                                            send_sem=self.send_sems.at[r - 1], recv_sem=self.recv_sems.at[r - 1],
                                            device_id=self._peer(r), device_id_type=MESH)

    def _mine(self):
        x, y, c = self.me
        return pltpu.make_async_copy(self.p_ref, self.all_ref.at[4 * x + 2 * y + c], self.local_sem)

    def start(self):
        self._mine().start()
        for r in range(1, N_DEVICES):
            self._copy(r, self.me).start()

    def finish(self):
        for r in range(1, N_DEVICES):
            self._copy(r, self._peer(r)).wait()
        self._mine().wait()


def _adamw(w, g, m, v):
    m = ADAM_B1 * m + (1.0 - ADAM_B1) * g
    v = ADAM_B2 * v + (1.0 - ADAM_B2) * (g * g)
    m_hat = m / ADAM_BC1
    v_hat = v / ADAM_BC2
    delta = -ADAM_LR * (m_hat / (jnp.sqrt(v_hat) + ADAM_EPS) + ADAM_WD * w)
    return delta, m, v


JOIN_SUB = 4


def _join(tag, shard_shape, part, arrived, core_chip, block=None):
    pr, pc = WGRAD_GEOMETRY[tag][:2]
    rb = pr // JOIN_SUB
    by_rows = shard_shape[1] == pc
    riding = block is not None

    def body(cc_ref, p_ref, r1_ref, r2_ref, r3_ref, *rest):
        if riding:
            blk_ref, g_ref, all_ref, stage, send_sems, recv_sems, local_sems, b_send, b_recv, b_local = rest
            gather = _DirectGather(blk_ref, all_ref, b_send, b_recv, b_local)
        else:
            g_ref, stage, send_sems, recv_sems, local_sems = rest
        i = pl.program_id(0)
        c = cc_ref[0]
        if riding:
            @pl.when(i == 0)
            def _():
                gather.start()

        def window(core, k):
            if by_rows:
                return g_ref.at[pl.ds((core * JOIN_SUB + k) * rb, rb), :]
            return g_ref.at[pl.ds(k * rb, rb), pl.ds(core * pc, pc)]

        def keep(k):
            return pltpu.make_async_copy(stage.at[k], window(c, k), local_sems.at[k])

        def push(k):
            return pltpu.make_async_remote_copy(src_ref=stage.at[k], dst_ref=window(c, k), send_sem=send_sems.at[k],
                                                recv_sem=recv_sems.at[k], device_id=_sibling(), device_id_type=MESH)

        def pushed(k):
            return pltpu.make_async_remote_copy(src_ref=stage.at[k], dst_ref=window(1 - c, k), send_sem=send_sems.at[k],
                                                recv_sem=recv_sems.at[k], device_id=_sibling(), device_id_type=MESH)

        stage[i] = ((p_ref[0] + r1_ref[0].astype(F32)) + r2_ref[0].astype(F32)) + r3_ref[0].astype(F32)
        keep(i).start()
        push(i).start()

        @pl.when(i == JOIN_SUB - 1)
        def _():
            for k in range(JOIN_SUB):
                keep(k).wait()
                push(k).wait_send()
                pushed(k).wait_recv()
            if riding:
                gather.finish()

    def partial(off):
        return pl.BlockSpec((1, rb, pc), lambda i, cc: ((cc[1] + off) % N_CHIPS, i, 0))

    in_specs = [partial(0), partial(1), partial(2), partial(3)]
    out_specs = [ANY]
    out_shape = [jax.ShapeDtypeStruct(shard_shape, F32)]
    scratch = [pltpu.VMEM((JOIN_SUB, rb, pc), F32), pltpu.SemaphoreType.DMA((JOIN_SUB,)),
               pltpu.SemaphoreType.DMA((JOIN_SUB,)), pltpu.SemaphoreType.DMA((JOIN_SUB,))]
    operands = [part, arrived, arrived, arrived]
    if riding:
        in_specs.append(pl.BlockSpec(block.shape, lambda i, cc: (0, 0)))
        out_specs.append(ANY)
        out_shape.append(jax.ShapeDtypeStruct((N_DEVICES,) + block.shape, block.dtype))
        scratch += _DirectGather.semaphores()
        operands.append(block)
    outs = pl.pallas_call(
        body, name="join_" + tag,
        grid_spec=pltpu.PrefetchScalarGridSpec(
            num_scalar_prefetch=1, grid=(JOIN_SUB,), in_specs=in_specs, out_specs=out_specs, scratch_shapes=scratch),
        out_shape=out_shape,
        compiler_params=_params(dimension_semantics=("arbitrary",)),
    )(core_chip, *operands)
    return outs if riding else outs[0]


def _adamw_big(w, g, m, v, name):
    rows, cols = w.shape
    rb = ADAMW_ROWS

    def body(w_ref, g_ref, m_ref, v_ref, go_ref, d_ref, nm_ref, nv_ref):
        g = g_ref[...]
        go_ref[...] = g
        d_ref[...], nm_ref[...], nv_ref[...] = _adamw(w_ref[...], g, m_ref[...], v_ref[...])

    spec = pl.BlockSpec((rb, cols), lambda i: (i, 0))
    return pl.pallas_call(
        body, name=name, grid=(rows // rb,), in_specs=[spec] * 4, out_specs=[spec] * 4,
        out_shape=[jax.ShapeDtypeStruct(w.shape, F32)] * 4,
        compiler_params=_params(dimension_semantics=("arbitrary",)),
    )(w, g, m, v)


SMALL_VECTORS = {
    "norm_mix_g": (PK_MIX_G, D_MODEL), "rnn_conv_b": (PK_RCONV_B, LRU_W), "b_a": (PK_B_A, LRU_W), "b_x": (PK_B_X, LRU_W),
    "lru_lambda": (PK_LAMBDA, LRU_W), "g_norm_conv": (PK_G_NORM_CONV, CONV_W), "g_norm_rnn": (PK_G_NORM_RNN, LRU_W),
    "norm_mlp_g": (PK_MLP_G, D_MODEL), "final_norm_g": (PK_FINAL_G, D_MODEL),
}
SMALL_MATRICES = ("w_a", "w_x")


def _small_step(vec_packs, mat_packs, mix_g_blocks, p):
    vec_rows, cols = vec_packs.shape[1:]
    conv_rows, cshard = p["conv_w"].shape
    rconv_rows, rshard = p["rnn_conv_w"].shape
    names = list(SMALL_VECTORS) + list(SMALL_MATRICES) + ["conv_w", "rnn_conv_w"]
    shapes = ([(1, width) for _, width in SMALL_VECTORS.values()] + [mat_packs.shape[2:]] * len(SMALL_MATRICES)
              + [(conv_rows, cshard), (rconv_rows, rshard)])
    kinds = ("", "m_", "v_")
    params = [p[pre + n].reshape(1, -1) if n in SMALL_VECTORS else p[pre + n] for pre in kinds for n in names]

    def body(vec_ref, mat_ref, blk_ref, *rest):
        wmv = [dict(zip(names, rest[k * len(names):(k + 1) * len(names)])) for k in range(3)]
        loss_ref, rest = rest[3 * len(names)], rest[3 * len(names) + 1:]
        leaves, (g_ref, w_ref, m_ref, v_ref) = [rest[k * len(names):(k + 1) * len(names)] for k in range(4)], rest[4 * len(names):]
        total = vec_ref[0]
        mats = mat_ref[0].astype(F32)
        late = blk_ref[0]
        for k in range(1, N_DEVICES):
            total = total + vec_ref[k]
            mats = mats + mat_ref[k].astype(F32)
            late = late + blk_ref[k]
        g_ref[0:vec_rows, :] = total
        g_ref[vec_rows:, :] = late
        g = g_ref[...]
        loss_ref[...] = g[PK_LOSS:PK_LOSS + 1, 0:1]

        for pack_ref, given in zip((w_ref, m_ref, v_ref), wmv):
            pack_ref[...] = jnp.zeros_like(pack_ref)
            for name, (row, width) in SMALL_VECTORS.items():
                pack_ref[row:row + 1, 0:width] = given[name][...]

        x, y, _ = _position()
        j = 2 * x + y
        cblk = total[0:TILE_ROWS, :]
        rblk = total[PK_RCONV_W:PK_RCONV_W + TILE_ROWS, :]
        cg = cblk[:, 0:cshard]
        rg = rblk[:, 0:rshard]
        for k in range(1, N_CHIPS):
            cg = jnp.where(j == k, cblk[:, k * cshard:(k + 1) * cshard], cg)
            rg = jnp.where(j == k, rblk[:, k * rshard:(k + 1) * rshard], rg)
        cg = cg[PK_CONV_W:PK_CONV_W + conv_rows, :]
        rg = rg[0:rconv_rows, :]

        def step(name, grad):
            return (grad,) + _adamw(wmv[0][name][...], grad, wmv[1][name][...], wmv[2][name][...])

        packs = (g,) + _adamw(w_ref[...], g, m_ref[...], v_ref[...])
        matrices = [step(name, mats[n]) for n, name in enumerate(SMALL_MATRICES)]
        convs, rconvs = step("conv_w", cg), step("rnn_conv_w", rg)
        for kind in range(4):
            out = dict(zip(names, leaves[kind]))
            for name, (row, width) in SMALL_VECTORS.items():
                out[name][...] = packs[kind][row:row + 1, 0:width]
            for n, name in enumerate(SMALL_MATRICES):
                out[name][...] = matrices[n][kind]
            out["conv_w"][...] = convs[kind]
            out["rnn_conv_w"][...] = rconvs[kind]

    outs = pl.pallas_call(
        body, name="small_grads_step", in_specs=[VMEM] * (3 + len(params)), out_specs=[VMEM] * (1 + 4 * len(names)),
        out_shape=[jax.ShapeDtypeStruct((1, 1), F32)] + [jax.ShapeDtypeStruct(sh, F32) for sh in shapes] * 4,
        scratch_shapes=[pltpu.VMEM((PK_ROWS, cols), F32)] * 4,
        compiler_params=_params(),
    )(vec_packs, mat_packs, mix_g_blocks, *params)
    return outs[0], [dict(zip(names, outs[1 + k * len(names):1 + (k + 1) * len(names)])) for k in range(4)]


def _to_block_diag(w):
    w4 = w.reshape(N_BD, 4, 64, 64)
    return jnp.concatenate([jnp.pad(w4[:, q], ((0, 0), (0, 0), (64 * q, 64 * (3 - q)))) for q in range(4)], axis=1)


def _from_block_diag(d):
    d5 = d.reshape(N_BD, 4, HEAD_DIM, 4, HEAD_DIM)
    return jnp.stack([d5[:, q, :, q, :] for q in range(4)], axis=1).reshape(N_HEADS, HEAD_DIM, HEAD_DIM)


def _pad_rows(a):
    return jnp.pad(a, ((0, TILE_ROWS - a.shape[0]), (0, 0)))


_NAMES = ['norm_mix_g', 'w_in', 'conv_w', 'rnn_conv_w', 'rnn_conv_b', 'w_a', 'b_a', 'w_x', 'b_x', 'lru_lambda',
          'g_norm_conv', 'g_norm_rnn', 'w_out', 'norm_mlp_g', 'w_mlp_in', 'w_mlp_out', 'final_norm_g']


def kernel(x, norm_mix_g, w_in, conv_w, rnn_conv_w, rnn_conv_b, w_a, b_a, w_x, b_x, lru_lambda, g_norm_conv, g_norm_rnn, w_out, norm_mlp_g, w_mlp_in, w_mlp_out, final_norm_g, loss_target, m_norm_mix_g, m_w_in, m_conv_w, m_rnn_conv_w, m_rnn_conv_b, m_w_a, m_b_a, m_w_x, m_b_x, m_lru_lambda, m_g_norm_conv, m_g_norm_rnn, m_w_out, m_norm_mlp_g, m_w_mlp_in, m_w_mlp_out, m_final_norm_g, v_norm_mix_g, v_w_in, v_conv_w, v_rnn_conv_w, v_rnn_conv_b, v_w_a, v_b_a, v_w_x, v_b_x, v_lru_lambda, v_g_norm_conv, v_g_norm_rnn, v_w_out, v_norm_mlp_g, v_w_mlp_in, v_w_mlp_out, v_final_norm_g):
    args = dict(locals())
    p = {}
    for n in _NAMES:
        for pre in ("", "m_", "v_"):
            a = args[pre + n]
            p[pre + n] = a[0] if a.ndim >= 3 else a
    xs = x[0]
    target = loss_target[0]
    core_chip = jnp.stack([lax.axis_index("c"), 2 * lax.axis_index("x") + lax.axis_index("y")]).astype(jnp.int32)
    cshard = p["conv_w"].shape[1]
    rshard = p["rnn_conv_w"].shape[1]

    small = jnp.concatenate([_pad_rows(p["conv_w"]), _pad_rows(p["rnn_conv_w"])], axis=1)
    w_in_g, w_out_g, w1_g, w2_g, small_g = _gather_first(p["w_in"], p["w_out"], p["w_mlp_in"], p["w_mlp_out"], small)
    conv_full = small_g[:, :3, :cshard].transpose(1, 0, 2).reshape(3, CONV_W)
    rconv_full = small_g[:, :4, cshard:].transpose(1, 0, 2).reshape(4, LRU_W)
    wa_bd = _to_block_diag(p["w_a"]).astype(BF16)
    wx_bd = _to_block_diag(p["w_x"]).astype(BF16)
    gf = p["final_norm_g"].reshape(1, -1)
    lru = (wa_bd, p["b_a"], wx_bd, p["b_x"], p["lru_lambda"], p["g_norm_conv"], p["g_norm_rnn"])

    (u, h1b, xr, hs, c3, yb, gates), (w_out_g, w1_g, w2_g) = _fwd_mix(
        xs, p["norm_mix_g"], w_in_g, conv_full, rconv_full, p["rnn_conv_b"], *lru, (w_out_g, w1_g, w2_g))
    zb, dpb, h2b, dx3b, dx2, dx2b, dy, st_mlp = _mlp_fwd_bwd(
        xs, yb, w_out_g.reshape(-1, D_MODEL), w1_g, w2_g.reshape(-1, D_MODEL), p["norm_mlp_g"], gf, target)

    part_out = _wgrad(yb, dx2b, "out", core_chip)
    *part_1, arrived_out = _wgrad(h2b, dpb, "mlp_in", core_chip, parts=(part_out[1],))
    part_2 = _wgrad(zb, dx3b, "mlp_out", core_chip)
    (dub, st_mix, dwa_bd, dwx_bd), (arrived_1, arrived_2) = _mix_bwd(
        dy, u, xr, hs, c3, gates, conv_full, rconv_full, wa_bd, wx_bd, p["lru_lambda"], p["g_norm_conv"], p["g_norm_rnn"],
        (part_1[1], part_2[1]))
    arrived_mlp = (arrived_out, arrived_1, arrived_2)
    vec_pack = jnp.concatenate([st_mix, st_mlp], axis=0)
    mat_pack = jnp.stack([_from_block_diag(dwa_bd), _from_block_diag(dwx_bd)]).astype(BF16)
    *part_in, vec_packs, mat_packs = _wgrad(h1b, dub, "in", core_chip, packs=(vec_pack, mat_pack))
    early = (("w_out", "out", part_out, arrived_mlp[0]), ("w_mlp_in", "mlp_in", part_1, arrived_mlp[1]),
             ("w_mlp_out", "mlp_out", part_2, arrived_mlp[2]))
    (grad_x, st_in), arrived_in, joined = _in_bwd(
        dub, w_in_g, xs, dx2, p["norm_mix_g"], (part_in[1],),
        [(tag, p[n].shape, part[0], arrived) for n, tag, part, arrived in early], core_chip)
    g_in, mix_g_blocks = _join("in", p["w_in"].shape, part_in[0], arrived_in[0], core_chip, st_in)
    big = {}
    for n, tag, g in [(n, tag, g) for (n, tag, _, _), g in zip(early, joined)] + [("w_in", "in", g_in)]:
        big[n] = _adamw_big(p[n], g, p["m_" + n], p["v_" + n], "adamw_" + tag)

    loss, outs = _small_step(vec_packs, mat_packs, mix_g_blocks, p)
    for kind, o in enumerate(outs):
        o["final_norm_g"] = o["final_norm_g"].reshape(-1)
        for n in SMALL_MATRICES + ("conv_w", "rnn_conv_w"):
            o[n] = o[n][None]
        for n in ("w_in", "w_out", "w_mlp_in", "w_mlp_out"):
            o[n] = big[n][kind][None]
    loss = loss.reshape(())
    return (loss, grad_x[None], *[o[n] for o in outs for n in _NAMES])
```

```python
import functools
import math

import jax
import jax.numpy as jnp
from jax import lax
from jax.experimental import pallas as pl
from jax.experimental.pallas import tpu as pltpu

F32 = jnp.float32
BF16 = jnp.bfloat16
MESH = pl.DeviceIdType.MESH
ANY = pl.BlockSpec(memory_space=pl.ANY)
VMEM = pl.BlockSpec(memory_space=pltpu.VMEM)

EPS = 1e-6
LRU_C = 8.0
D_MODEL = 1024
CONV_W = 512
LRU_W = 1024
IN_COLS = 3 * CONV_W + 2 * LRU_W
IN_SHARD = IN_COLS // 4
N_CHIPS = 4
N_DEVICES = 8
BD = 256
N_BD = LRU_W // BD

ADAM_LR = 0.001
ADAM_B1 = 0.9
ADAM_B2 = 0.999
ADAM_EPS = 1e-08
ADAM_WD = 0.01
ADAM_STEP = 10
ADAM_BC1 = 1.0 - ADAM_B1 ** ADAM_STEP
ADAM_BC2 = 1.0 - ADAM_B2 ** ADAM_STEP

TILE_ROWS, LANES = 8, 128
TOKEN_TILE = 256
MATMUL_TOKEN_TILE = 512
ADAMW_ROWS = 256
VMEM_LIMIT = 56 * 1024 * 1024

PK_G_NORM_RNN, PK_RCONV_B, PK_B_A, PK_B_X, PK_LAMBDA, PK_CONV_W = 0, 1, 2, 3, 4, 5
PK_RCONV_W, PK_G_NORM_CONV = 8, 12
PK_MIX_ROWS = 16
PK_FINAL_G, PK_MLP_G, PK_LOSS = 16, 17, 18
PK_MLP_ROWS = 8
PK_MIX_G = 24
PK_ROWS = 32
N_HEADS, HEAD_DIM = 16, 64


def _params(**kw):
    return pltpu.CompilerParams(vmem_limit_bytes=VMEM_LIMIT, **kw)


def _position():
    x, y, c = lax.axis_index("x"), lax.axis_index("y"), lax.axis_index("c")
    return x, y, c


def _sigmoid(v):
    return 1.0 / (1.0 + jnp.exp(-v))


def _one_minus_square(log_a, a):
    v = 2.0 * log_a
    series = -v * (1.0 + v * (0.5 + v * (1.0 / 6.0)))
    return jnp.where(v > -0.01, series, 1.0 - a * a)


_GELU_C = math.sqrt(2.0 / math.pi)
_GELU_K = 0.044715


def _gelu_and_grad(g):
    th = jnp.tanh(_GELU_C * (g + _GELU_K * g * g * g))
    gelu = 0.5 * g * (1.0 + th)
    dgelu = 0.5 * (1.0 + th) + 0.5 * g * (1.0 - th * th) * (_GELU_C * (1.0 + 3.0 * _GELU_K * g * g))
    return gelu, dgelu


def _rows(shape):
    return lax.broadcasted_iota(jnp.int32, shape, 0)


def _shift_down(v, k, prev8):
    rolled = pltpu.roll(v, k, 0)
    halo = pltpu.roll(prev8, k, 0)
    head = jnp.where(_rows(halo.shape) < k, halo, rolled[:TILE_ROWS])
    return jnp.concatenate([head, rolled[TILE_ROWS:]], axis=0)


def _shift_up(v, k, next8):
    n = v.shape[0]
    rolled = pltpu.roll(v, n - k, 0)
    halo = pltpu.roll(next8, TILE_ROWS - k, 0)
    tail = jnp.where(_rows(halo.shape) >= TILE_ROWS - k, halo, rolled[n - TILE_ROWS:])
    return jnp.concatenate([rolled[: n - TILE_ROWS], tail], axis=0)


def _scan_rows(a, b, carry, reverse=False):
    n, w = a.shape
    groups = n // TILE_ROWS
    a3 = a.reshape(groups, TILE_ROWS, w)
    b3 = b.reshape(groups, TILE_ROWS, w)
    sub = lax.broadcasted_iota(jnp.int32, a3.shape, 1)
    s = 1
    while s < TILE_ROWS:
        shift = TILE_ROWS - s if reverse else s
        keep = (sub < TILE_ROWS - s) if reverse else (sub >= s)
        b3 = b3 + jnp.where(keep, a3 * pltpu.roll(b3, shift, 1), 0.0)
        a3 = a3 * jnp.where(keep, pltpu.roll(a3, shift, 1), 1.0)
        s *= 2
    out = [None] * groups
    edge = 0 if reverse else TILE_ROWS - 1
    for g in (range(groups - 1, -1, -1) if reverse else range(groups)):
        out[g] = b3[g] + a3[g] * carry
        carry = out[g][edge:edge + 1]
    return jnp.concatenate(out, axis=0)


def _softplus_neg(lam):
    e = jnp.exp(-jnp.abs(lam))
    log1p_e = jnp.where(e < 1e-2, e * (1.0 - e * (0.5 - e * (1.0 / 3.0 - e * 0.25))), jnp.log(1.0 + e))
    sp = jnp.maximum(-lam, 0.0) + log1p_e
    dsp = -_sigmoid(-lam)
    return sp, dsp


def _block_diag_dot(vb, w_ref):
    return jnp.concatenate(
        [jnp.dot(vb[:, j * BD:(j + 1) * BD], w_ref[j], preferred_element_type=F32) for j in range(N_BD)], axis=1)


def _block_diag_dot_t(vb, w_ref):
    return jnp.concatenate(
        [lax.dot_general(vb[:, j * BD:(j + 1) * BD], w_ref[j], (((1,), (1,)), ((), ())), preferred_element_type=F32)
         for j in range(N_BD)], axis=1)


def _dot_nt(a, b):
    return lax.dot_general(a, b, (((1,), (1,)), ((), ())), preferred_element_type=F32)


def _dot_tn(a, b):
    return lax.dot_general(a, b, (((0,), (0,)), ((), ())), preferred_element_type=F32)


def _lru_gates(xr, wa_ref, ba, wx_ref, bx, sp):
    xrb = xr.astype(BF16)
    r = _sigmoid(_block_diag_dot(xrb, wa_ref) + ba)
    ig = _sigmoid(_block_diag_dot(xrb, wx_ref) + bx)
    log_a = (-LRU_C) * r * sp
    a = jnp.exp(log_a)
    mult = jnp.sqrt(_one_minus_square(log_a, a))
    return r, ig, a, mult


def _colsum(v):
    return jnp.sum(v, axis=0, keepdims=True)


N_FWD_OUT = 7


def _fwd_mix(x, g1, w_in_g, conv_w, rconv_w, rconv_b, wa_bd, b_a, wx_bd, b_x, lam, g_nc, g_nr, later):
    t, d = x.shape
    tm = TOKEN_TILE
    nt = t // tm
    nl = len(later)
    assert nl == 3
    pass_on_at = [nt * f // 16 for f in (3, 5, 9)]
    neighbours_at = [nt * f // 16 for f in (10, 11, 12)]
    diagonal_at = [nt * f // 16 for f in (13, 14, 14)]

    def body(x_ref, g1_ref, win_ref, cw_ref, rw_ref, rb_ref, wa_ref, ba_ref, wx_ref, bx_ref, lam_ref, gnc_ref, gnr_ref,
             *rest):
        later_in, outs, rest = rest[:nl], rest[nl:nl + N_FWD_OUT], rest[nl + N_FWD_OUT:]
        u_ref, h1_ref, xr_ref, hs_ref, c3_ref, y_ref, gates_ref = outs
        later_out, (cv_prev, xin_prev, h_prev, send_sems, recv_sems) = rest[:nl], rest[nl:]
        del later_in
        step = pl.program_id(0)
        plan = _ShardGather(later_out, send_sems, recv_sems)

        @pl.when(step == 0)
        def _():
            cv_prev[...] = jnp.zeros_like(cv_prev)
            xin_prev[...] = jnp.zeros_like(xin_prev)
            h_prev[...] = jnp.zeros_like(h_prev)
            for w in range(nl):
                plan.start_direct(w)

        for w in range(nl):
            @pl.when(step == pass_on_at[w])
            def _(w=w):
                plan.start_pass_on(w)

            @pl.when(step == neighbours_at[w])
            def _(w=w):
                plan.start_hand_over(w, diagonal=False)

            @pl.when(step == diagonal_at[w])
            def _(w=w):
                plan.start_hand_over(w, diagonal=True)

        xv = x_ref[...]
        rstd = lax.rsqrt(jnp.mean(xv * xv, axis=-1, keepdims=True) + EPS)
        h1b = ((xv * rstd) * g1_ref[...]).astype(BF16)
        h1_ref[...] = h1b
        for j in range(N_CHIPS):
            u_ref[:, j * IN_SHARD:(j + 1) * IN_SHARD] = jnp.dot(h1b, win_ref[j], preferred_element_type=F32)
        gate_b = u_ref[:, 0:CONV_W]
        cv = u_ref[:, CONV_W:2 * CONV_W] * u_ref[:, 2 * CONV_W:3 * CONV_W]
        x_r = u_ref[:, 3 * CONV_W:3 * CONV_W + LRU_W]
        g = u_ref[:, 3 * CONV_W + LRU_W:]

        cw = cw_ref[...]
        cvp = cv_prev[...]
        conv3 = cw[0:1] * _shift_down(cv, 2, cvp) + cw[1:2] * _shift_down(cv, 1, cvp) + cw[2:3] * cv
        cv_prev[...] = cv[tm - TILE_ROWS:]
        c3_ref[...] = conv3
        y_conv = gate_b * conv3

        rw = rw_ref[...]
        xp = xin_prev[...]
        xr = (rw[0:1] * _shift_down(x_r, 3, xp) + rw[1:2] * _shift_down(x_r, 2, xp)
              + rw[2:3] * _shift_down(x_r, 1, xp) + rw[3:4] * x_r) + rb_ref[...]
        xin_prev[...] = x_r[tm - TILE_ROWS:]
        xr_ref[...] = xr
        sp, _ = _softplus_neg(lam_ref[...])
        r, ig, a, mult = _lru_gates(xr, wa_ref, ba_ref[...], wx_ref, bx_ref[...], sp)
        for n, gate in enumerate((r, ig, a, mult)):
            gates_ref[:, n * LRU_W:(n + 1) * LRU_W] = gate
        h = _scan_rows(a, mult * (ig * xr), h_prev[...])
        h_prev[...] = h[tm - 1:tm]
        hs_ref[...] = h
        gelu, _ = _gelu_and_grad(g)
        y_rnn = h * gelu

        na = y_conv * lax.rsqrt(jnp.mean(y_conv * y_conv, axis=-1, keepdims=True) + EPS) * gnc_ref[...]
        nb = y_rnn * lax.rsqrt(jnp.mean(y_rnn * y_rnn, axis=-1, keepdims=True) + EPS) * gnr_ref[...]
        y_ref[:, :CONV_W] = na.astype(BF16)
        y_ref[:, CONV_W:] = nb.astype(BF16)

        @pl.when(step == nt - 1)
        def _():
            for w in range(nl):
                plan.finish(w)

    def full(a):
        nd = a.ndim
        return pl.BlockSpec(a.shape, lambda i: (0,) * nd)

    def tok(cols):
        return pl.BlockSpec((tm, cols), lambda i: (i, 0))

    def act(cols, dtype=F32):
        return jax.ShapeDtypeStruct((t, cols), dtype)

    smalls = (g1, w_in_g, conv_w, rconv_w, rconv_b, wa_bd, b_a, wx_bd, b_x, lam, g_nc, g_nr)
    n_in = 1 + len(smalls)
    outs = pl.pallas_call(
        body, name="fwd_mix", grid=(nt,),
        in_specs=[tok(d)] + [full(a) for a in smalls] + [ANY] * nl,
        out_specs=[tok(IN_COLS), tok(d), tok(LRU_W), tok(LRU_W), tok(CONV_W), tok(CONV_W + LRU_W)]
        + [tok(4 * LRU_W)] + [ANY] * nl,
        out_shape=[act(IN_COLS), act(d, BF16), act(LRU_W), act(LRU_W), act(CONV_W), act(CONV_W + LRU_W, BF16)]
        + [act(4 * LRU_W)] + [jax.ShapeDtypeStruct(a.shape, a.dtype) for a in later],
        input_output_aliases={n_in + w: N_FWD_OUT + w for w in range(nl)},
        scratch_shapes=[pltpu.VMEM((TILE_ROWS, CONV_W), F32), pltpu.VMEM((TILE_ROWS, LRU_W), F32),
                        pltpu.VMEM((1, LRU_W), F32), pltpu.SemaphoreType.DMA((nl, _ShardGather.PAIRS)),
                        pltpu.SemaphoreType.DMA((nl, _ShardGather.PAIRS))],
        compiler_params=_params(dimension_semantics=("arbitrary",)),
    )(x, *smalls, *later)
    return outs[:N_FWD_OUT], outs[N_FWD_OUT:]


def _mlp_fwd_bwd(x, yb, w_out_g, w1_g, w2_g, g2, gf, target):
    t, d = x.shape
    tm = TOKEN_TILE
    ff = w2_g.shape[0]
    mix = w_out_g.shape[0]
    ffs = ff // N_CHIPS

    def body(x_ref, y_ref, g2_ref, gf_ref, tgt_ref, wout_hbm, w1_hbm, w2_hbm,
             z_ref, dp_ref, h2_ref, dx3b_ref, dx2_ref, dx2b_ref, dy_ref, st_ref, wout, w1, w2, p_ref):
        @pl.when(pl.program_id(0) == 0)
        def _():
            pltpu.sync_copy(wout_hbm, wout)
            pltpu.sync_copy(w1_hbm, w1)
            pltpu.sync_copy(w2_hbm, w2)
            st_ref[...] = jnp.zeros_like(st_ref)

        x2 = x_ref[...] + jnp.dot(y_ref[...], wout[...], preferred_element_type=F32)
        r2 = lax.rsqrt(jnp.mean(x2 * x2, axis=-1, keepdims=True) + EPS)
        xh2 = x2 * r2
        g2v = g2_ref[...]
        h2b = (xh2 * g2v).astype(BF16)
        h2_ref[...] = h2b
        for j in range(N_CHIPS):
            p_ref[:, j * ffs:(j + 1) * ffs] = jnp.dot(h2b, w1[j], preferred_element_type=F32)
        rp = jnp.maximum(p_ref[...], 0.0)
        zb = (rp * rp).astype(BF16)
        z_ref[...] = zb
        x3 = x2 + jnp.dot(zb, w2[...], preferred_element_type=F32)
        r3 = lax.rsqrt(jnp.mean(x3 * x3, axis=-1, keepdims=True) + EPS)
        xh3 = x3 * r3
        gfv = gf_ref[...]
        err = xh3 * gfv - tgt_ref[...]
        loss = (0.5 / d) * jnp.sum(err * err)
        dout = err * (1.0 / d)
        st_ref[PK_FINAL_G - PK_MIX_ROWS:PK_FINAL_G - PK_MIX_ROWS + 1, :] += _colsum(dout * xh3)
        st_ref[PK_LOSS - PK_MIX_ROWS:PK_LOSS - PK_MIX_ROWS + 1, :] += jnp.zeros((1, d), F32) + loss
        dxh3 = dout * gfv
        dx3 = r3 * (dxh3 - xh3 * jnp.mean(dxh3 * xh3, axis=-1, keepdims=True))
        dx3b = dx3.astype(BF16)
        dx3b_ref[...] = dx3b
        dpb = (_dot_nt(dx3b, w2[...]) * (2.0 * rp)).astype(BF16)
        dp_ref[...] = dpb
        dh2 = _dot_nt(dpb[:, 0:ffs], w1[0])
        for j in range(1, N_CHIPS):
            dh2 = dh2 + _dot_nt(dpb[:, j * ffs:(j + 1) * ffs], w1[j])
        st_ref[PK_MLP_G - PK_MIX_ROWS:PK_MLP_G - PK_MIX_ROWS + 1, :] += _colsum(dh2 * xh2)
        dxh2 = dh2 * g2v
        dx2 = dx3 + r2 * (dxh2 - xh2 * jnp.mean(dxh2 * xh2, axis=-1, keepdims=True))
        dx2_ref[...] = dx2
        dx2b = dx2.astype(BF16)
        dx2b_ref[...] = dx2b
        dy_ref[...] = _dot_nt(dx2b, wout[...])

    def tok(cols):
        return pl.BlockSpec((tm, cols), lambda i: (i, 0))

    def row(cols):
        return pl.BlockSpec((1, cols), lambda i: (0, 0))

    return pl.pallas_call(
        body, name="mlp_fwd_bwd", grid=(t // tm,),
        in_specs=[tok(d), tok(mix), row(d), row(d), tok(d), ANY, ANY, ANY],
        out_specs=[tok(ff), tok(ff), tok(d), tok(d), tok(d), tok(d), tok(mix),
                   pl.BlockSpec((PK_MLP_ROWS, d), lambda i: (0, 0))],
        out_shape=[jax.ShapeDtypeStruct((t, ff), BF16), jax.ShapeDtypeStruct((t, ff), BF16),
                   jax.ShapeDtypeStruct((t, d), BF16), jax.ShapeDtypeStruct((t, d), BF16),
                   jax.ShapeDtypeStruct((t, d), F32), jax.ShapeDtypeStruct((t, d), BF16),
                   jax.ShapeDtypeStruct((t, mix), F32), jax.ShapeDtypeStruct((PK_MLP_ROWS, d), F32)],
        scratch_shapes=[pltpu.VMEM(w_out_g.shape, BF16), pltpu.VMEM(w1_g.shape, BF16), pltpu.VMEM(w2_g.shape, BF16),
                        pltpu.VMEM((tm, ff), F32)],
        compiler_params=_params(dimension_semantics=("arbitrary",)),
    )(x, yb, g2, gf, target, w_out_g, w1_g, w2_g)


def _mix_bwd(dy, u, xr_all, hs_all, c3_all, gates, conv_w, rconv_w, wa_bd, wx_bd, lam, g_nc, g_nr, st_mlp, parts):
    t = dy.shape[0]
    tm = TOKEN_TILE
    nt = t // tm
    hb = tm // TILE_ROWS
    npart = len(parts)

    def body(dy_ref, u_ref, uh_ref, xr_ref, hs_ref, hh_ref, c3_ref, gates_ref,
             cw_ref, rw_ref, wa_ref, wx_ref, lam_ref, gnc_ref, gnr_ref, stm_ref, *rest):
        part_refs, (du_ref, st_ref, heads_ref), rest = rest[:npart], rest[npart:npart + 3], rest[npart + 3:]
        arrived_refs, (dc_next, a_next, gs_next, dxr_next, dwa_ref, dwx_ref, send_sems, recv_sems) = rest[:npart], rest[npart:]
        exchange = _PartialExchange(part_refs, arrived_refs, send_sems, recv_sems)
        i = pl.program_id(0)

        @pl.when(i == 0)
        def _():
            exchange.start()
            dc_next[...] = jnp.zeros_like(dc_next)
            a_next[...] = jnp.zeros_like(a_next)
            gs_next[...] = jnp.zeros_like(gs_next)
            dxr_next[...] = jnp.zeros_like(dxr_next)
            st_ref[0:PK_MIX_ROWS, :] = jnp.zeros((PK_MIX_ROWS, LRU_W), F32)
            st_ref[PK_MIX_ROWS:, :] = stm_ref[...]
            dwa_ref[...] = jnp.zeros_like(dwa_ref)
            dwx_ref[...] = jnp.zeros_like(dwx_ref)

        first_tile = i == nt - 1
        gate_b = u_ref[:, 0:CONV_W]
        gate_c = u_ref[:, CONV_W:2 * CONV_W]
        v = u_ref[:, 2 * CONV_W:3 * CONV_W]
        x_r = u_ref[:, 3 * CONV_W:3 * CONV_W + LRU_W]
        g = u_ref[:, 3 * CONV_W + LRU_W:]
        cv = gate_c * v
        cv_prev = jnp.where(first_tile, 0.0, uh_ref[:, CONV_W:2 * CONV_W] * uh_ref[:, 2 * CONV_W:3 * CONV_W])
        xin_prev = jnp.where(first_tile, 0.0, uh_ref[:, 3 * CONV_W:3 * CONV_W + LRU_W])
        hs_prev = jnp.where(first_tile, 0.0, hh_ref[...])

        def acc(first_row, val, width=LRU_W, row=0):
            r0 = first_row + row
            st_ref[r0:r0 + 1, 0:width] += val

        conv3 = c3_ref[...]
        y_conv = gate_b * conv3
        ra = lax.rsqrt(jnp.mean(y_conv * y_conv, axis=-1, keepdims=True) + EPS)
        xha = y_conv * ra
        dna = dy_ref[:, :CONV_W]
        acc(PK_G_NORM_CONV, _colsum(dna * xha), CONV_W)
        dxha = dna * gnc_ref[...]
        dy_conv = ra * (dxha - xha * jnp.mean(dxha * xha, axis=-1, keepdims=True))
        du_ref[:, 0:CONV_W] = (dy_conv * conv3).astype(BF16)
        dc = dy_conv * gate_b
        cw = cw_ref[...]
        dcn = dc_next[...]
        dcv = cw[2:3] * dc + cw[1:2] * _shift_up(dc, 1, dcn) + cw[0:1] * _shift_up(dc, 2, dcn)
        dc_next[...] = dc[:TILE_ROWS]
        acc(PK_CONV_W, _colsum(dc * _shift_down(cv, 2, cv_prev)), CONV_W, 0)
        acc(PK_CONV_W, _colsum(dc * _shift_down(cv, 1, cv_prev)), CONV_W, 1)
        acc(PK_CONV_W, _colsum(dc * cv), CONV_W, 2)
        du_ref[:, CONV_W:2 * CONV_W] = (dcv * v).astype(BF16)
        du_ref[:, 2 * CONV_W:3 * CONV_W] = (dcv * gate_c).astype(BF16)

        hs = hs_ref[...]
        gelu, dgelu = _gelu_and_grad(g)
        y_rnn = hs * gelu
        rb = lax.rsqrt(jnp.mean(y_rnn * y_rnn, axis=-1, keepdims=True) + EPS)
        xhb = y_rnn * rb
        dnb = dy_ref[:, CONV_W:]
        acc(PK_G_NORM_RNN, _colsum(dnb * xhb))
        dxhb = dnb * gnr_ref[...]
        dy_rnn = rb * (dxhb - xhb * jnp.mean(dxhb * xhb, axis=-1, keepdims=True))
        du_ref[:, 3 * CONV_W + LRU_W:] = (dy_rnn * hs * dgelu).astype(BF16)
        dh = dy_rnn * gelu

        xr = xr_ref[...]
        xrb = xr.astype(BF16)
        sp, dsp = _softplus_neg(lam_ref[...])
        r, ig, a, mult = [gates_ref[:, n * LRU_W:(n + 1) * LRU_W] for n in range(4)]
        a_up = _shift_up(a, 1, a_next[...])
        a_next[...] = a[:TILE_ROWS]
        gs = _scan_rows(a_up, dh, gs_next[0:1, :], reverse=True)
        gs_next[...] = gs[:TILE_ROWS]
        da = gs * _shift_down(hs, 1, hs_prev)
        gx = gs * xr
        di = gx * mult
        dmult = gx * ig
        dxr = gs * (mult * ig)
        dlog_a = da * a - dmult * ((a * a) / mult)
        acc(PK_LAMBDA, _colsum(dlog_a * r) * ((-LRU_C) * dsp))
        dpa = (dlog_a * ((-LRU_C) * sp)) * (r * (1.0 - r))
        dpx = di * (ig * (1.0 - ig))
        acc(PK_B_A, _colsum(dpa))
        acc(PK_B_X, _colsum(dpx))
        dpab = dpa.astype(BF16)
        dpxb = dpx.astype(BF16)
        dxr = dxr + _block_diag_dot_t(dpab, wa_ref) + _block_diag_dot_t(dpxb, wx_ref)
        for j in range(N_BD):
            cols = slice(j * BD, (j + 1) * BD)
            dwa_ref[j] += _dot_tn(xrb[:, cols], dpab[:, cols])
            dwx_ref[j] += _dot_tn(xrb[:, cols], dpxb[:, cols])

        acc(PK_RCONV_B, _colsum(dxr))
        rw = rw_ref[...]
        dxn = dxr_next[...]
        dx_r = (rw[3:4] * dxr + rw[2:3] * _shift_up(dxr, 1, dxn) + rw[1:2] * _shift_up(dxr, 2, dxn)
                + rw[0:1] * _shift_up(dxr, 3, dxn))
        dxr_next[...] = dxr[:TILE_ROWS]
        for k in range(3):
            acc(PK_RCONV_W, _colsum(dxr * _shift_down(x_r, 3 - k, xin_prev)), LRU_W, k)
        acc(PK_RCONV_W, _colsum(dxr * x_r), LRU_W, 3)
        du_ref[:, 3 * CONV_W:3 * CONV_W + LRU_W] = dx_r.astype(BF16)

        @pl.when(i == nt - 1)
        def _():
            for n, d_ref in enumerate((dwa_ref, dwx_ref)):
                for b in range(N_BD):
                    for q in range(BD // HEAD_DIM):
                        lane0 = q * HEAD_DIM // LANES * LANES
                        wide = d_ref[b, q * HEAD_DIM:(q + 1) * HEAD_DIM, lane0:lane0 + LANES]
                        if q * HEAD_DIM != lane0:
                            wide = pltpu.roll(wide, LANES - (q * HEAD_DIM - lane0), axis=1)
                        heads_ref[n, b * (BD // HEAD_DIM) + q] = wide[:, 0:HEAD_DIM].astype(BF16)
            exchange.wait()

    def full(a):
        nd = a.ndim
        return pl.BlockSpec(a.shape, lambda i: (0,) * nd)

    def tok(cols):
        return pl.BlockSpec((tm, cols), lambda i: (nt - 1 - i, 0))

    def halo(cols):
        return pl.BlockSpec((TILE_ROWS, cols), lambda i: (jnp.maximum((nt - 1 - i) * hb - 1, 0), 0))

    smalls = (conv_w, rconv_w, wa_bd, wx_bd, lam, g_nc, g_nr, st_mlp)
    st_rows = PK_MIX_ROWS + st_mlp.shape[0]
    heads = (2, N_HEADS, HEAD_DIM, HEAD_DIM)
    outs = pl.pallas_call(
        body, name="mix_bwd", grid=(nt,),
        in_specs=[tok(CONV_W + LRU_W), tok(IN_COLS), halo(IN_COLS), tok(LRU_W), tok(LRU_W), halo(LRU_W), tok(CONV_W)]
        + [tok(4 * LRU_W)] + [full(a) for a in smalls] + [ANY] * npart,
        out_specs=[tok(IN_COLS), pl.BlockSpec((st_rows, LRU_W), lambda i: (0, 0)),
                   pl.BlockSpec(heads, lambda i: (0, 0, 0, 0))]
        + [ANY] * npart,
        out_shape=[jax.ShapeDtypeStruct((t, IN_COLS), BF16), jax.ShapeDtypeStruct((st_rows, LRU_W), F32),
                   jax.ShapeDtypeStruct(heads, BF16)]
        + [jax.ShapeDtypeStruct(a.shape, a.dtype) for a in parts],
        scratch_shapes=[pltpu.VMEM((TILE_ROWS, CONV_W), F32), pltpu.VMEM((TILE_ROWS, LRU_W), F32),
                        pltpu.VMEM((TILE_ROWS, LRU_W), F32), pltpu.VMEM((TILE_ROWS, LRU_W), F32),
                        pltpu.VMEM((N_BD, BD, BD), F32), pltpu.VMEM((N_BD, BD, BD), F32),
                        pltpu.SemaphoreType.DMA((npart, 3)), pltpu.SemaphoreType.DMA((npart, 3))],
        compiler_params=_params(dimension_semantics=("arbitrary",)),
    )(dy, u, u, xr_all, hs_all, hs_all, c3_all, gates, *smalls, *parts)
    return outs[:3], outs[3:]


def _in_bwd(dub, w_in_g, x, dx2, g1, parts, joins, core_chip):
    t, d = x.shape
    tm = min(t, MATMUL_TOKEN_TILE)
    nt = t // tm
    npart = len(parts)
    nj = len(joins)
    geometry = []
    for tag, shape, _, _ in joins:
        pr, pc = WGRAD_GEOMETRY[tag][:2]
        every = 1 if pr % (nt * 16) == 0 else 2
        geometry.append((pr, pc, pr * every // nt, every, shape[1] == pc))

    def body(cc_ref, du_ref, win_ref, x_ref, dx2_ref, g1_ref, *rest):
        sums, rest = [rest[4 * w:4 * w + 4] for w in range(nj)], rest[4 * nj:]
        part_refs, (gx_ref, st_ref), rest = rest[:npart], rest[npart:npart + 2], rest[npart + 2:]
        arrived_refs, joined, rest = rest[:npart], rest[npart:npart + nj], rest[npart + nj:]
        stages, (send_sems, recv_sems, j_local, j_send, j_recv) = rest[:nj], rest[nj:]
        exchange = _PartialExchange(part_refs, arrived_refs, send_sems, recv_sems)
        i = pl.program_id(0)
        c = cc_ref[0]

        def window(w, core, row0, rows):
            pr, pc, _, _, by_rows = geometry[w]
            if by_rows:
                return joined[w].at[pl.ds(core * pr + row0, rows), :]
            return joined[w].at[pl.ds(row0, rows), pl.ds(core * pc, pc)]

        def to_sibling(w, src, core, row0, rows):
            return pltpu.make_async_remote_copy(src_ref=src, dst_ref=window(w, core, row0, rows), send_sem=j_send.at[w],
                                                recv_sem=j_recv.at[w], device_id=_sibling(), device_id_type=MESH)

        @pl.when(i == 0)
        def _():
            exchange.start()
            st_ref[...] = jnp.zeros_like(st_ref)

        for w in range(nj):
            pr, pc, rb, every, _ = geometry[w]

            @pl.when(i % every == 0)
            def _(w=w, rb=rb, every=every):
                p_ref, r1_ref, r2_ref, r3_ref = sums[w]
                row0 = pl.multiple_of((i // every) * rb, rb)
                rows = stages[w].at[pl.ds(row0, rb), :]
                rows[...] = ((p_ref[0] + r1_ref[0].astype(F32)) + r2_ref[0].astype(F32)) + r3_ref[0].astype(F32)
                pltpu.make_async_copy(rows, window(w, c, row0, rb), j_local.at[w]).start()
                to_sibling(w, rows, c, row0, rb).start()

        dh1 = _dot_nt(du_ref[:, 0:IN_SHARD], win_ref[0])
        for j in range(1, N_CHIPS):
            dh1 = dh1 + _dot_nt(du_ref[:, j * IN_SHARD:(j + 1) * IN_SHARD], win_ref[j])
        xv = x_ref[...]
        rstd = lax.rsqrt(jnp.mean(xv * xv, axis=-1, keepdims=True) + EPS)
        xh = xv * rstd
        st_ref[0:1, :] += _colsum(dh1 * xh)
        dxh = dh1 * g1_ref[...]
        gx_ref[...] = dx2_ref[...] + rstd * (dxh - xh * jnp.mean(dxh * xh, axis=-1, keepdims=True))

        @pl.when(i == nt - 1)
        def _():
            exchange.wait()
            for w in range(nj):
                pr = geometry[w][0]
                pltpu.make_async_copy(stages[w], window(w, c, 0, pr), j_local.at[w]).wait()
                to_sibling(w, stages[w], 1 - c, 0, pr).wait()

    def tok(cols):
        return pl.BlockSpec((tm, cols), lambda i, cc: (i, 0))

    def partial(w, off):
        pr, pc, rb, every, _ = geometry[w]
        return pl.BlockSpec((1, rb, pc), lambda i, cc: ((cc[1] + off) % N_CHIPS, i // every, 0))

    sum_specs, sum_operands = [], []
    for w, (_, _, own, arrived) in enumerate(joins):
        sum_specs += [partial(w, off) for off in range(N_CHIPS)]
        sum_operands += [own, arrived, arrived, arrived]
    dma = pltpu.SemaphoreType.DMA
    outs = pl.pallas_call(
        body, name="in_bwd",
        grid_spec=pltpu.PrefetchScalarGridSpec(
            num_scalar_prefetch=1, grid=(nt,),
            in_specs=[tok(IN_COLS), pl.BlockSpec(w_in_g.shape, lambda i, cc: (0, 0, 0)), tok(d), tok(d),
                      pl.BlockSpec((1, d), lambda i, cc: (0, 0))] + sum_specs + [ANY] * npart,
            out_specs=[tok(d), pl.BlockSpec((TILE_ROWS, d), lambda i, cc: (0, 0))] + [ANY] * (npart + nj),
            scratch_shapes=[pltpu.VMEM((g[0], g[1]), F32) for g in geometry]
            + [dma((npart, 3)), dma((npart, 3)), dma((nj,)), dma((nj,)), dma((nj,))]),
        out_shape=[jax.ShapeDtypeStruct((t, d), F32), jax.ShapeDtypeStruct((TILE_ROWS, d), F32)]
        + [jax.ShapeDtypeStruct(a.shape, a.dtype) for a in parts]
        + [jax.ShapeDtypeStruct(shape, F32) for _, shape, _, _ in joins],
        compiler_params=_params(dimension_semantics=("arbitrary",)),
    )(core_chip, dub, w_in_g, x, dx2, g1, *sum_operands, *parts)
    return outs[:2], outs[2:2 + npart], outs[2 + npart:]


WGRAD_GEOMETRY = {
    "in": (512, IN_SHARD, lambda s, h: h, lambda s, h: s),
    "mlp_in": (512, D_MODEL, lambda s, h: h, lambda s, h: s),
    "mlp_out": (512, D_MODEL, lambda s, h: 2 * s + h, lambda s, h: 0),
    "out": (384, 512, lambda s, h: s, lambda s, h: h),
}
K_CHUNK = 512


def _sibling():
    x, y, c = _position()
    return (x, y, 1 - c)


def _wgrad(a, b, tag, core_chip, packs=(), parts=()):
    t = a.shape[0]
    pr, pc, a_blk, b_blk = WGRAD_GEOMETRY[tag]
    nk = t // K_CHUNK
    mine = N_CHIPS
    riding = len(packs)
    npart = len(parts)
    assert not (riding and npart)

    def body(cc_ref, a_ref, b_ref, *rest):
        if riding:
            pack_refs, (land_ref, p_ref, pb_ref), rest = rest[:riding], rest[riding:riding + 3], rest[riding + 3:]
            all_refs, (stage, rbuf, send_sems, recv_sems, rsem), g_sems = rest[:riding], rest[riding:riding + 5], rest[riding + 5:]
            gathers = [_PackGather(pack_refs[n], all_refs[n], *g_sems[3 * n:3 * n + 3]) for n in range(riding)]
        elif npart:
            part_refs, (land_ref, p_ref, pb_ref), rest = rest[:npart], rest[npart:npart + 3], rest[npart + 3:]
            arrived_refs, (stage, rbuf, send_sems, recv_sems, rsem, x_send, x_recv) = rest[:npart], rest[npart:]
            exchange = _PartialExchange(part_refs, arrived_refs, x_send, x_recv)
        else:
            land_ref, p_ref, pb_ref, stage, rbuf, send_sems, recv_sems, rsem = rest
        ph, s = pl.program_id(0), pl.program_id(1)
        if riding:
            @pl.when((ph == 0) & (s == 0))
            def _():
                for gather in gathers:
                    gather.start()

            @pl.when((ph == 1) & (s == N_CHIPS - 2))
            def _():
                for gather in gathers:
                    gather.hand_over()
        if npart:
            @pl.when((ph == 0) & (s == 0))
            def _():
                exchange.start()
        def push(k):
            return pltpu.make_async_remote_copy(src_ref=stage.at[k], dst_ref=land_ref.at[k], send_sem=send_sems.at[k],
                                                recv_sem=recv_sems.at[k], device_id=_sibling(), device_id_type=MESH)

        def landed():
            return pltpu.make_async_copy(land_ref.at[s], rbuf, rsem)

        @pl.when(ph == 1)
        def _():
            push(s).wait_recv()
            landed().start()

        slot = jnp.where(ph == 0, s, mine)
        acc = stage.at[slot]
        acc[...] = _dot_tn(a_ref[0:K_CHUNK, :], b_ref[0:K_CHUNK, :])
        for k in range(1, nk):
            acc[...] += _dot_tn(a_ref[k * K_CHUNK:(k + 1) * K_CHUNK, :], b_ref[k * K_CHUNK:(k + 1) * K_CHUNK, :])

        @pl.when(ph == 0)
        def _():
            push(s).start()

        @pl.when(ph == 1)
        def _():
            landed().wait()
            p = stage[mine] + rbuf[...]
            p_ref[0] = p
            pb_ref[0] = p.astype(BF16)

        @pl.when((ph == 1) & (s == N_CHIPS - 1))
        def _():
            for k in range(N_CHIPS):
                push(k).wait_send()
            for gather in (gathers if riding else ()):
                gather.finish()
            if npart:
                exchange.wait()

    def half(ph, cc):
        return jnp.where(ph == 0, 1 - cc[0], cc[0])

    def out_slot(ph, s, cc):
        return (jnp.where(ph == 0, 0, s), 0, 0)

    piece = jax.ShapeDtypeStruct((N_CHIPS, pr, pc), F32)
    in_specs = [pl.BlockSpec((t, pr), lambda ph, s, cc: (0, a_blk(s, half(ph, cc)))),
                pl.BlockSpec((t, pc), lambda ph, s, cc: (0, b_blk(s, half(ph, cc))))]
    out_specs = [ANY, pl.BlockSpec((1, pr, pc), out_slot), pl.BlockSpec((1, pr, pc), out_slot)]
    out_shape = [piece, piece, jax.ShapeDtypeStruct((N_CHIPS, pr, pc), BF16)]
    scratch = [pltpu.VMEM((N_CHIPS + 1, pr, pc), F32), pltpu.VMEM((pr, pc), F32),
               pltpu.SemaphoreType.DMA((N_CHIPS,)), pltpu.SemaphoreType.DMA((N_CHIPS,)), pltpu.SemaphoreType.DMA]
    operands = [a, b]
    for pack in packs:
        in_specs.append(pl.BlockSpec(pack.shape, lambda ph, s, cc, nd=pack.ndim: (0,) * nd))
        out_specs.append(ANY)
        out_shape.append(jax.ShapeDtypeStruct((N_DEVICES,) + pack.shape, pack.dtype))
        operands.append(pack)
    for pack in packs:
        scratch += _PackGather.semaphores()
    if npart:
        in_specs += [ANY] * npart
        out_specs += [ANY] * npart
        out_shape += [jax.ShapeDtypeStruct(p.shape, p.dtype) for p in parts]
        scratch += [pltpu.SemaphoreType.DMA((npart, 3)), pltpu.SemaphoreType.DMA((npart, 3))]
        operands += list(parts)
    return pl.pallas_call(
        body, name="wgrad_" + tag,
        grid_spec=pltpu.PrefetchScalarGridSpec(
            num_scalar_prefetch=1, grid=(2, N_CHIPS), in_specs=in_specs, out_specs=out_specs, scratch_shapes=scratch),
        out_shape=out_shape,
        compiler_params=_params(dimension_semantics=("arbitrary", "arbitrary")),
    )(core_chip, *operands)[1:]


def _other_chips(x, y):
    return [(1 - x, y), (x, 1 - y), (1 - x, 1 - y)]


class _ShardGather:
    PAIRS = 9

    def __init__(self, outs, send_sems, recv_sems):
        self.outs, self.send_sems, self.recv_sems = outs, send_sems, recv_sems
        x, y, c = _position()
        self.c, self.j = c, 2 * x + y
        self.sibling = (x, y, 1 - c)
        self.chips = _other_chips(x, y)

    def _chip(self, k):
        px, py = self.chips[k]
        return 2 * px + py

    def _half(self, w, chip, which):
        hr = self.outs[w].shape[1] // 2
        return self.outs[w].at[chip, pl.ds(which * hr, hr), :]

    def _quarter(self, w, chip, q):
        qr = self.outs[w].shape[1] // 4
        return self.outs[w].at[chip, pl.ds(self.c * 2 * qr + q * qr, qr), :]

    def _copy(self, ref, w, pair, to, src=None):
        return pltpu.make_async_remote_copy(src_ref=ref if src is None else src, dst_ref=ref, send_sem=self.send_sems.at[w, pair],
                                            recv_sem=self.recv_sems.at[w, pair], device_id=to, device_id_type=MESH)

    def direct(self, w, k, q, src=None):
        return self._copy(self._quarter(w, self.j, q), w, 2 * k + q, (*self.chips[k], self.c), src)

    def direct_landed(self, w, k, q):
        return self._copy(self._quarter(w, self._chip(k), q), w, 2 * k + q, (*self.chips[k], self.c))

    def pass_on(self, w, q):
        return self._copy(self._quarter(w, self._chip(q), q), w, 4 + q, (*self.chips[1 - q], self.c))

    def passed_landed(self, w, q):
        return self._copy(self._quarter(w, self._chip(2), q), w, 4 + q, (*self.chips[1 - q], self.c))

    def hand_over(self, w, k):
        return self._copy(self._half(w, self._chip(k), self.c), w, 6 + k, self.sibling)

    def handed(self, w, k):
        return self._copy(self._half(w, self._chip(k), 1 - self.c), w, 6 + k, self.sibling)

    def start_direct(self, w, src_half=None):
        qr = self.outs[w].shape[1] // 4
        for k, q in ((0, 0), (1, 1), (0, 1), (1, 0)):
            self.direct(w, k, q, None if src_half is None else src_half.at[pl.ds(q * qr, qr), :]).start()

    def start_pass_on(self, w):
        for q in (0, 1):
            self.direct_landed(w, q, q).wait_recv()
            self.pass_on(w, q).start()

    def start_hand_over(self, w, diagonal):
        if diagonal:
            for q in (0, 1):
                self.passed_landed(w, q).wait_recv()
            self.hand_over(w, 2).start()
        else:
            for k in (0, 1):
                self.direct_landed(w, k, 1 - k).wait_recv()
                self.hand_over(w, k).start()

    def finish(self, w):
        for k in range(3):
            self.handed(w, k).wait_recv()
            self.hand_over(w, k).wait_send()
        for q in (0, 1):
            self.pass_on(w, q).wait_send()
            for k in (0, 1):
                self.direct(w, k, q).wait_send()


def _gather_first(w_in, w_out, w1, w2, conv_w, rconv_w):
    bigs = (w_in, w_out, w1, w2)
    convs = (conv_w, rconv_w)
    nb, nc = len(bigs), len(convs)

    def body(win_ref, wout_ref, w1_ref, w2_ref, cw_ref, rw_ref, gin, gout, g1, g2, gcw, grw, st_in, st_out, st_1, st_2,
             st_cw, st_rw, send_sems, recv_sems, sm_send, sm_recv, local_sems):
        srcs = (win_ref, wout_ref, w1_ref, w2_ref)
        stages = (st_in, st_out, st_1, st_2)
        outs = (gin, gout, g1, g2)
        conv_stages, conv_outs = (st_cw, st_rw), (gcw, grw)
        plan = _ShardGather(outs[:1], send_sems, recv_sems)
        j, c = plan.j, plan.c
        local = [pltpu.make_async_copy(stages[w], outs[w].at[j], local_sems.at[w]) for w in range(nb)]

        def columns(n, chip):
            width = convs[n].shape[1]
            return conv_outs[n].at[:, pl.ds(chip * width, width)]

        local += [pltpu.make_async_copy(conv_stages[n], columns(n, j), local_sems.at[nb + n]) for n in range(nc)]

        def small_copy(k, n, landed=False):
            px, py = plan.chips[k]
            return pltpu.make_async_remote_copy(
                src_ref=conv_stages[n], dst_ref=columns(n, 2 * px + py if landed else j), send_sem=sm_send.at[k, n],
                recv_sem=sm_recv.at[k, n], device_id=(px, py, c), device_id_type=MESH)

        hr = w_in.shape[0] // 2
        st_in[...] = win_ref[...].astype(BF16)
        plan.start_direct(0, st_in.at[pl.ds(c * hr, hr), :])
        for src, st in zip((cw_ref, rw_ref), conv_stages):
            st[...] = jnp.zeros_like(st)
            st[0:src.shape[0], :] = src[...]
        for k in range(3):
            for n in range(nc):
                small_copy(k, n).start()
        for src, st in zip(srcs[1:], stages[1:]):
            st[...] = src[...].astype(BF16)
        for cp in local:
            cp.start()
        plan.start_pass_on(0)
        plan.start_hand_over(0, diagonal=False)
        plan.start_hand_over(0, diagonal=True)
        for k in range(3):
            for n in range(nc):
                small_copy(k, n, landed=True).wait_recv()
                small_copy(k, n).wait_send()
        plan.finish(0)
        for cp in local:
            cp.wait()

    def gathered(a, dtype):
        return jax.ShapeDtypeStruct((N_CHIPS,) + a.shape, dtype)

    return pl.pallas_call(
        body, name="gather_first",
        in_specs=[VMEM] * (nb + nc), out_specs=[ANY] * (nb + nc),
        out_shape=[gathered(a, BF16) for a in bigs]
        + [jax.ShapeDtypeStruct((TILE_ROWS, N_CHIPS * a.shape[1]), F32) for a in convs],
        scratch_shapes=[pltpu.VMEM(a.shape, BF16) for a in bigs] + [pltpu.VMEM((TILE_ROWS, a.shape[1]), F32) for a in convs]
        + [pltpu.SemaphoreType.DMA((1, _ShardGather.PAIRS)), pltpu.SemaphoreType.DMA((1, _ShardGather.PAIRS)),
           pltpu.SemaphoreType.DMA((3, nc)), pltpu.SemaphoreType.DMA((3, nc)), pltpu.SemaphoreType.DMA((nb + nc,))],
        compiler_params=_params(),
    )(*bigs, *convs)


class _PartialExchange:
    def __init__(self, parts, arrived, send_sems, recv_sems):
        self.parts, self.arrived, self.send_sems, self.recv_sems = parts, arrived, send_sems, recv_sems
        x, y, c = _position()
        self.c, self.j = c, 2 * x + y
        self.chips = _other_chips(x, y)

    def _copy(self, w, k, slot):
        px, py = self.chips[k]
        return pltpu.make_async_remote_copy(
            src_ref=self.parts[w].at[2 * px + py], dst_ref=self.arrived[w].at[slot], send_sem=self.send_sems.at[w, k],
            recv_sem=self.recv_sems.at[w, k], device_id=(px, py, self.c), device_id_type=MESH)

    def start(self):
        for w in range(len(self.parts)):
            for k in range(3):
                self._copy(w, k, self.j).start()

    def wait(self):
        for w in range(len(self.parts)):
            for k in range(3):
                px, py = self.chips[k]
                self._copy(w, k, 2 * px + py).wait()


class _PackGather:
    def __init__(self, p_ref, all_ref, send_sems, recv_sems, local_sem):
        self.p_ref, self.all_ref, self.send_sems, self.recv_sems, self.local_sem = p_ref, all_ref, send_sems, recv_sems, local_sem
        x, y, c = _position()
        self.me, self.sibling, self.c = (x, y, c), (x, y, 1 - c), c
        self.chips = _other_chips(x, y)

    @staticmethod
    def semaphores():
        return [pltpu.SemaphoreType.DMA((7,)), pltpu.SemaphoreType.DMA((7,)), pltpu.SemaphoreType.DMA]

    def _copy(self, k, block, to, from_pack=False):
        px, py, pc = block
        slot = self.all_ref.at[4 * px + 2 * py + pc]
        return pltpu.make_async_remote_copy(src_ref=self.p_ref if from_pack else slot, dst_ref=slot, send_sem=self.send_sems.at[k],
                                            recv_sem=self.recv_sems.at[k], device_id=to, device_id_type=MESH)

    def _mine(self):
        x, y, c = self.me
        return pltpu.make_async_copy(self.p_ref, self.all_ref.at[4 * x + 2 * y + c], self.local_sem)

    def _first(self):
        return [self._copy(0, self.me, self.sibling, True)] + [
            self._copy(1 + k, self.me, (*chip, self.c), True) for k, chip in enumerate(self.chips)]

    def _passed(self):
        return [self._copy(4 + k, (*chip, self.c), self.sibling) for k, chip in enumerate(self.chips)]

    def start(self):
        self._mine().start()
        for cp in self._first():
            cp.start()

    def hand_over(self):
        for k, chip in enumerate(self.chips):
            self._copy(1 + k, (*chip, self.c), self.me).wait_recv()
            self._passed()[k].start()

    def finish(self):
        self._copy(0, self.sibling, self.me).wait_recv()
        for k, chip in enumerate(self.chips):
            self._copy(4 + k, (*chip, 1 - self.c), self.me).wait_recv()
        for cp in self._first() + self._passed():
            cp.wait_send()
        self._mine().wait()


class _DirectGather:
    def __init__(self, p_ref, all_ref, send_sems, recv_sems, local_sem):
        self.p_ref, self.all_ref, self.send_sems, self.recv_sems, self.local_sem = p_ref, all_ref, send_sems, recv_sems, local_sem
        self.me = _position()

    semaphores = _PackGather.semaphores

    def _peer(self, r):
        x, y, c = self.me
        return ((1 - x) if r & 4 else x, (1 - y) if r & 2 else y, (1 - c) if r & 1 else c)

    def _copy(self, r, slot_of):
        px, py, pc = slot_of
        return pltpu.make_async_remote_copy(src_ref=self.p_ref, dst_ref=self.all_ref.at[4 * px + 2 * py + pc],
                                            send_sem=self.send_sems.at[r - 1], recv_sem=self.recv_sems.at[r - 1],
                                            device_id=self._peer(r), device_id_type=MESH)

    def _mine(self):
        x, y, c = self.me
        return pltpu.make_async_copy(self.p_ref, self.all_ref.at[4 * x + 2 * y + c], self.local_sem)

    def start(self):
        self._mine().start()
        for r in range(1, N_DEVICES):
            self._copy(r, self.me).start()

    def finish(self):
        for r in range(1, N_DEVICES):
            self._copy(r, self._peer(r)).wait()
        self._mine().wait()


def _adamw(w, g, m, v):
    m = ADAM_B1 * m + (1.0 - ADAM_B1) * g
    v = ADAM_B2 * v + (1.0 - ADAM_B2) * (g * g)
    m_hat = m / ADAM_BC1
    v_hat = v / ADAM_BC2
    delta = -ADAM_LR * (m_hat / (jnp.sqrt(v_hat) + ADAM_EPS) + ADAM_WD * w)
    return delta, m, v


JOIN_SUB = 4


def _join(tag, shard_shape, part, arrived, core_chip, block=None):
    pr, pc = WGRAD_GEOMETRY[tag][:2]
    rb = pr // JOIN_SUB
    by_rows = shard_shape[1] == pc
    riding = block is not None

    def body(cc_ref, p_ref, r1_ref, r2_ref, r3_ref, *rest):
        if riding:
            blk_ref, g_ref, all_ref, stage, send_sems, recv_sems, local_sems, b_send, b_recv, b_local = rest
            gather = _DirectGather(blk_ref, all_ref, b_send, b_recv, b_local)
        else:
            g_ref, stage, send_sems, recv_sems, local_sems = rest
        i = pl.program_id(0)
        c = cc_ref[0]
        if riding:
            @pl.when(i == 0)
            def _():
                gather.start()

        def window(core, k):
            if by_rows:
                return g_ref.at[pl.ds((core * JOIN_SUB + k) * rb, rb), :]
            return g_ref.at[pl.ds(k * rb, rb), pl.ds(core * pc, pc)]

        def keep(k):
            return pltpu.make_async_copy(stage.at[k], window(c, k), local_sems.at[k])

        def push(k):
            return pltpu.make_async_remote_copy(src_ref=stage.at[k], dst_ref=window(c, k), send_sem=send_sems.at[k],
                                                recv_sem=recv_sems.at[k], device_id=_sibling(), device_id_type=MESH)

        def pushed(k):
            return pltpu.make_async_remote_copy(src_ref=stage.at[k], dst_ref=window(1 - c, k), send_sem=send_sems.at[k],
                                                recv_sem=recv_sems.at[k], device_id=_sibling(), device_id_type=MESH)

        stage[i] = ((p_ref[0] + r1_ref[0].astype(F32)) + r2_ref[0].astype(F32)) + r3_ref[0].astype(F32)
        keep(i).start()
        push(i).start()

        @pl.when(i == JOIN_SUB - 1)
        def _():
            for k in range(JOIN_SUB):
                keep(k).wait()
                push(k).wait_send()
                pushed(k).wait_recv()
            if riding:
                gather.finish()

    def partial(off):
        return pl.BlockSpec((1, rb, pc), lambda i, cc: ((cc[1] + off) % N_CHIPS, i, 0))

    in_specs = [partial(0), partial(1), partial(2), partial(3)]
    out_specs = [ANY]
    out_shape = [jax.ShapeDtypeStruct(shard_shape, F32)]
    scratch = [pltpu.VMEM((JOIN_SUB, rb, pc), F32), pltpu.SemaphoreType.DMA((JOIN_SUB,)),
               pltpu.SemaphoreType.DMA((JOIN_SUB,)), pltpu.SemaphoreType.DMA((JOIN_SUB,))]
    operands = [part, arrived, arrived, arrived]
    if riding:
        in_specs.append(pl.BlockSpec(block.shape, lambda i, cc: (0, 0)))
        out_specs.append(ANY)
        out_shape.append(jax.ShapeDtypeStruct((N_DEVICES,) + block.shape, block.dtype))
        scratch += _DirectGather.semaphores()
        operands.append(block)
    outs = pl.pallas_call(
        body, name="join_" + tag,
        grid_spec=pltpu.PrefetchScalarGridSpec(
            num_scalar_prefetch=1, grid=(JOIN_SUB,), in_specs=in_specs, out_specs=out_specs, scratch_shapes=scratch),
        out_shape=out_shape,
        compiler_params=_params(dimension_semantics=("arbitrary",)),
    )(core_chip, *operands)
    return outs if riding else outs[0]


def _adamw_big(w, g, m, v, name):
    rows, cols = w.shape
    rb = ADAMW_ROWS if rows % ADAMW_ROWS == 0 else rows

    def body(w_ref, g_ref, m_ref, v_ref, go_ref, d_ref, nm_ref, nv_ref):
        g = g_ref[...]
        go_ref[...] = g
        d_ref[...], nm_ref[...], nv_ref[...] = _adamw(w_ref[...], g, m_ref[...], v_ref[...])

    spec = pl.BlockSpec((rb, cols), lambda i: (i, 0))
    return pl.pallas_call(
        body, name=name, grid=(rows // rb,), in_specs=[spec] * 4, out_specs=[spec] * 4,
        out_shape=[jax.ShapeDtypeStruct(w.shape, F32)] * 4,
        compiler_params=_params(dimension_semantics=("arbitrary",)),
    )(w, g, m, v)


SMALL_VECTORS = {
    "norm_mix_g": (PK_MIX_G, D_MODEL), "rnn_conv_b": (PK_RCONV_B, LRU_W), "b_a": (PK_B_A, LRU_W), "b_x": (PK_B_X, LRU_W),
    "lru_lambda": (PK_LAMBDA, LRU_W), "g_norm_conv": (PK_G_NORM_CONV, CONV_W), "g_norm_rnn": (PK_G_NORM_RNN, LRU_W),
    "norm_mlp_g": (PK_MLP_G, D_MODEL), "final_norm_g": (PK_FINAL_G, D_MODEL),
}
SMALL_MATRICES = ("w_a", "w_x")


def _small_step(vec_packs, mat_packs, mix_g_blocks, p):
    vec_rows, cols = vec_packs.shape[1:]
    conv_rows, cshard = p["conv_w"].shape
    rconv_rows, rshard = p["rnn_conv_w"].shape
    names = list(SMALL_VECTORS) + list(SMALL_MATRICES) + ["conv_w", "rnn_conv_w"]
    shapes = ([(1, width) for _, width in SMALL_VECTORS.values()] + [mat_packs.shape[2:]] * len(SMALL_MATRICES)
              + [(conv_rows, cshard), (rconv_rows, rshard)])
    kinds = ("", "m_", "v_")
    params = [p[pre + n].reshape(1, -1) if n in SMALL_VECTORS else p[pre + n] for pre in kinds for n in names]

    def body(vec_ref, mat_ref, blk_ref, *rest):
        wmv = [dict(zip(names, rest[k * len(names):(k + 1) * len(names)])) for k in range(3)]
        loss_ref, rest = rest[3 * len(names)], rest[3 * len(names) + 1:]
        leaves, (g_ref, w_ref, m_ref, v_ref) = [rest[k * len(names):(k + 1) * len(names)] for k in range(4)], rest[4 * len(names):]
        total = vec_ref[0]
        mats = mat_ref[0].astype(F32)
        late = blk_ref[0]
        for k in range(1, N_DEVICES):
            total = total + vec_ref[k]
            mats = mats + mat_ref[k].astype(F32)
            late = late + blk_ref[k]
        g_ref[0:vec_rows, :] = total
        g_ref[vec_rows:, :] = late
        g = g_ref[...]
        loss_ref[...] = g[PK_LOSS:PK_LOSS + 1, 0:1]

        for pack_ref, given in zip((w_ref, m_ref, v_ref), wmv):
            pack_ref[...] = jnp.zeros_like(pack_ref)
            for name, (row, width) in SMALL_VECTORS.items():
                pack_ref[row:row + 1, 0:width] = given[name][...]

        x, y, _ = _position()
        j = 2 * x + y
        cblk = total[0:TILE_ROWS, :]
        rblk = total[PK_RCONV_W:PK_RCONV_W + TILE_ROWS, :]
        cg = cblk[:, 0:cshard]
        rg = rblk[:, 0:rshard]
        for k in range(1, N_CHIPS):
            cg = jnp.where(j == k, cblk[:, k * cshard:(k + 1) * cshard], cg)
            rg = jnp.where(j == k, rblk[:, k * rshard:(k + 1) * rshard], rg)
        cg = cg[PK_CONV_W:PK_CONV_W + conv_rows, :]
        rg = rg[0:rconv_rows, :]

        def step(name, grad):
            return (grad,) + _adamw(wmv[0][name][...], grad, wmv[1][name][...], wmv[2][name][...])

        packs = (g,) + _adamw(w_ref[...], g, m_ref[...], v_ref[...])
        matrices = [step(name, mats[n]) for n, name in enumerate(SMALL_MATRICES)]
        convs, rconvs = step("conv_w", cg), step("rnn_conv_w", rg)
        for kind in range(4):
            out = dict(zip(names, leaves[kind]))
            for name, (row, width) in SMALL_VECTORS.items():
                out[name][...] = packs[kind][row:row + 1, 0:width]
            for n, name in enumerate(SMALL_MATRICES):
                out[name][...] = matrices[n][kind]
            out["conv_w"][...] = convs[kind]
            out["rnn_conv_w"][...] = rconvs[kind]

    outs = pl.pallas_call(
        body, name="small_grads_step", in_specs=[VMEM] * (3 + len(params)), out_specs=[VMEM] * (1 + 4 * len(names)),
        out_shape=[jax.ShapeDtypeStruct((1, 1), F32)] + [jax.ShapeDtypeStruct(sh, F32) for sh in shapes] * 4,
        scratch_shapes=[pltpu.VMEM((PK_ROWS, cols), F32)] * 4,
        compiler_params=_params(),
    )(vec_packs, mat_packs, mix_g_blocks, *params)
    return outs[0], [dict(zip(names, outs[1 + k * len(names):1 + (k + 1) * len(names)])) for k in range(4)]


def _to_block_diag(w):
    w4 = w.reshape(N_BD, 4, 64, 64)
    return jnp.concatenate([jnp.pad(w4[:, q], ((0, 0), (0, 0), (64 * q, 64 * (3 - q)))) for q in range(4)], axis=1)


_NAMES = ['norm_mix_g', 'w_in', 'conv_w', 'rnn_conv_w', 'rnn_conv_b', 'w_a', 'b_a', 'w_x', 'b_x', 'lru_lambda',
          'g_norm_conv', 'g_norm_rnn', 'w_out', 'norm_mlp_g', 'w_mlp_in', 'w_mlp_out', 'final_norm_g']


def kernel(x, norm_mix_g, w_in, conv_w, rnn_conv_w, rnn_conv_b, w_a, b_a, w_x, b_x, lru_lambda, g_norm_conv, g_norm_rnn, w_out, norm_mlp_g, w_mlp_in, w_mlp_out, final_norm_g, loss_target, m_norm_mix_g, m_w_in, m_conv_w, m_rnn_conv_w, m_rnn_conv_b, m_w_a, m_b_a, m_w_x, m_b_x, m_lru_lambda, m_g_norm_conv, m_g_norm_rnn, m_w_out, m_norm_mlp_g, m_w_mlp_in, m_w_mlp_out, m_final_norm_g, v_norm_mix_g, v_w_in, v_conv_w, v_rnn_conv_w, v_rnn_conv_b, v_w_a, v_b_a, v_w_x, v_b_x, v_lru_lambda, v_g_norm_conv, v_g_norm_rnn, v_w_out, v_norm_mlp_g, v_w_mlp_in, v_w_mlp_out, v_final_norm_g):
    args = dict(locals())
    p = {}
    for n in _NAMES:
        for pre in ("", "m_", "v_"):
            a = args[pre + n]
            p[pre + n] = a[0] if a.ndim >= 3 else a
    xs = x[0]
    target = loss_target[0]
    core_chip = jnp.stack([lax.axis_index("c"), 2 * lax.axis_index("x") + lax.axis_index("y")]).astype(jnp.int32)

    w_in_g, w_out_g, w1_g, w2_g, conv_full, rconv_full = _gather_first(
        p["w_in"], p["w_out"], p["w_mlp_in"], p["w_mlp_out"], p["conv_w"], p["rnn_conv_w"])
    wa_bd = _to_block_diag(p["w_a"]).astype(BF16)
    wx_bd = _to_block_diag(p["w_x"]).astype(BF16)
    gf = p["final_norm_g"].reshape(1, -1)
    lru = (wa_bd, p["b_a"], wx_bd, p["b_x"], p["lru_lambda"], p["g_norm_conv"], p["g_norm_rnn"])

    (u, h1b, xr, hs, c3, yb, gates), (w_out_g, w1_g, w2_g) = _fwd_mix(
        xs, p["norm_mix_g"], w_in_g, conv_full, rconv_full, p["rnn_conv_b"], *lru, (w_out_g, w1_g, w2_g))
    zb, dpb, h2b, dx3b, dx2, dx2b, dy, st_mlp = _mlp_fwd_bwd(
        xs, yb, w_out_g.reshape(-1, D_MODEL), w1_g, w2_g.reshape(-1, D_MODEL), p["norm_mlp_g"], gf, target)

    part_out = _wgrad(yb, dx2b, "out", core_chip)
    *part_1, arrived_out = _wgrad(h2b, dpb, "mlp_in", core_chip, parts=(part_out[1],))
    part_2 = _wgrad(zb, dx3b, "mlp_out", core_chip)
    (dub, vec_pack, mat_pack), (arrived_1, arrived_2) = _mix_bwd(
        dy, u, xr, hs, c3, gates, conv_full, rconv_full, wa_bd, wx_bd, p["lru_lambda"], p["g_norm_conv"], p["g_norm_rnn"],
        st_mlp, (part_1[1], part_2[1]))
    arrived_mlp = (arrived_out, arrived_1, arrived_2)
    *part_in, vec_packs, mat_packs = _wgrad(h1b, dub, "in", core_chip, packs=(vec_pack, mat_pack))
    early = (("w_out", "out", part_out, arrived_mlp[0]), ("w_mlp_in", "mlp_in", part_1, arrived_mlp[1]),
             ("w_mlp_out", "mlp_out", part_2, arrived_mlp[2]))
    (grad_x, st_in), arrived_in, joined = _in_bwd(
        dub, w_in_g, xs, dx2, p["norm_mix_g"], (part_in[1],),
        [(tag, p[n].shape, part[0], arrived) for n, tag, part, arrived in early], core_chip)
    g_in, mix_g_blocks = _join("in", p["w_in"].shape, part_in[0], arrived_in[0], core_chip, st_in)
    big = {}
    for n, tag, g in [(n, tag, g) for (n, tag, _, _), g in zip(early, joined)] + [("w_in", "in", g_in)]:
        big[n] = _adamw_big(p[n], g, p["m_" + n], p["v_" + n], "adamw_" + tag)

    loss, outs = _small_step(vec_packs, mat_packs, mix_g_blocks, p)
    for kind, o in enumerate(outs):
        o["final_norm_g"] = o["final_norm_g"].reshape(-1)
        for n in SMALL_MATRICES + ("conv_w", "rnn_conv_w"):
            o[n] = o[n][None]
        for n in ("w_in", "w_out", "w_mlp_in", "w_mlp_out"):
            o[n] = big[n][kind][None]
    loss = loss.reshape(())
    return (loss, grad_x[None], *[o[n] for o in outs for n in _NAMES])
```

```python
import functools
import math

import jax
import jax.numpy as jnp
from jax import lax
from jax.experimental import pallas as pl
from jax.experimental.pallas import tpu as pltpu

F32 = jnp.float32
BF16 = jnp.bfloat16
MESH = pl.DeviceIdType.MESH
ANY = pl.BlockSpec(memory_space=pl.ANY)
VMEM = pl.BlockSpec(memory_space=pltpu.VMEM)

EPS = 1e-6
LRU_C = 8.0
D_MODEL = 1024
CONV_W = 512
LRU_W = 1024
IN_COLS = 3 * CONV_W + 2 * LRU_W
IN_SHARD = IN_COLS // 4
N_CHIPS = 4
N_DEVICES = 8
BD = 256
N_BD = LRU_W // BD

ADAM_LR = 0.001
ADAM_B1 = 0.9
ADAM_B2 = 0.999
ADAM_EPS = 1e-08
ADAM_WD = 0.01
ADAM_STEP = 10
ADAM_BC1 = 1.0 - ADAM_B1 ** ADAM_STEP
ADAM_BC2 = 1.0 - ADAM_B2 ** ADAM_STEP

TILE_ROWS, LANES = 8, 128
TOKEN_TILE = 256
MATMUL_TOKEN_TILE = 512
ADAMW_ROWS = 256
VMEM_LIMIT = 56 * 1024 * 1024

PK_G_NORM_RNN, PK_RCONV_B, PK_B_A, PK_B_X, PK_LAMBDA, PK_CONV_W = 0, 1, 2, 3, 4, 5
PK_RCONV_W, PK_G_NORM_CONV = 8, 12
PK_MIX_ROWS = 16
PK_FINAL_G, PK_MLP_G, PK_LOSS = 16, 17, 18
PK_MLP_ROWS = 8
PK_MIX_G = 24
PK_ROWS = 32
N_HEADS, HEAD_DIM = 16, 64


def _params(**kw):
    return pltpu.CompilerParams(vmem_limit_bytes=VMEM_LIMIT, **kw)


def _position():
    x, y, c = lax.axis_index("x"), lax.axis_index("y"), lax.axis_index("c")
    return x, y, c


def _sigmoid(v):
    return 1.0 / (1.0 + jnp.exp(-v))


def _one_minus_square(log_a, a):
    v = 2.0 * log_a
    series = -v * (1.0 + v * (0.5 + v * (1.0 / 6.0)))
    return jnp.where(v > -0.01, series, 1.0 - a * a)


_GELU_C = math.sqrt(2.0 / math.pi)
_GELU_K = 0.044715


def _gelu_and_grad(g):
    th = jnp.tanh(_GELU_C * (g + _GELU_K * g * g * g))
    gelu = 0.5 * g * (1.0 + th)
    dgelu = 0.5 * (1.0 + th) + 0.5 * g * (1.0 - th * th) * (_GELU_C * (1.0 + 3.0 * _GELU_K * g * g))
    return gelu, dgelu


def _rows(shape):
    return lax.broadcasted_iota(jnp.int32, shape, 0)


def _shift_down(v, k, prev8):
    rolled = pltpu.roll(v, k, 0)
    halo = pltpu.roll(prev8, k, 0)
    head = jnp.where(_rows(halo.shape) < k, halo, rolled[:TILE_ROWS])
    return jnp.concatenate([head, rolled[TILE_ROWS:]], axis=0)


def _shift_up(v, k, next8):
    n = v.shape[0]
    rolled = pltpu.roll(v, n - k, 0)
    halo = pltpu.roll(next8, TILE_ROWS - k, 0)
    tail = jnp.where(_rows(halo.shape) >= TILE_ROWS - k, halo, rolled[n - TILE_ROWS:])
    return jnp.concatenate([rolled[: n - TILE_ROWS], tail], axis=0)


def _scan_rows(a, b, carry, reverse=False):
    n, w = a.shape
    groups = n // TILE_ROWS
    a3 = a.reshape(groups, TILE_ROWS, w)
    b3 = b.reshape(groups, TILE_ROWS, w)
    sub = lax.broadcasted_iota(jnp.int32, a3.shape, 1)
    s = 1
    while s < TILE_ROWS:
        shift = TILE_ROWS - s if reverse else s
        keep = (sub < TILE_ROWS - s) if reverse else (sub >= s)
        b3 = b3 + jnp.where(keep, a3 * pltpu.roll(b3, shift, 1), 0.0)
        a3 = a3 * jnp.where(keep, pltpu.roll(a3, shift, 1), 1.0)
        s *= 2
    out = [None] * groups
    edge = 0 if reverse else TILE_ROWS - 1
    for g in (range(groups - 1, -1, -1) if reverse else range(groups)):
        out[g] = b3[g] + a3[g] * carry
        carry = out[g][edge:edge + 1]
    return jnp.concatenate(out, axis=0)


def _softplus_neg(lam):
    e = jnp.exp(-jnp.abs(lam))
    log1p_e = jnp.where(e < 1e-2, e * (1.0 - e * (0.5 - e * (1.0 / 3.0 - e * 0.25))), jnp.log(1.0 + e))
    sp = jnp.maximum(-lam, 0.0) + log1p_e
    dsp = -_sigmoid(-lam)
    return sp, dsp


def _block_diag_dot(vb, w_ref):
    return jnp.concatenate(
        [jnp.dot(vb[:, j * BD:(j + 1) * BD], w_ref[j], preferred_element_type=F32) for j in range(N_BD)], axis=1)


def _block_diag_dot_t(vb, w_ref):
    return jnp.concatenate(
        [lax.dot_general(vb[:, j * BD:(j + 1) * BD], w_ref[j], (((1,), (1,)), ((), ())), preferred_element_type=F32)
         for j in range(N_BD)], axis=1)


def _dot_nt(a, b):
    return lax.dot_general(a, b, (((1,), (1,)), ((), ())), preferred_element_type=F32)


def _dot_tn(a, b):
    return lax.dot_general(a, b, (((0,), (0,)), ((), ())), preferred_element_type=F32)


def _lru_gates(xr, wa_ref, ba, wx_ref, bx, sp):
    xrb = xr.astype(BF16)
    r = _sigmoid(_block_diag_dot(xrb, wa_ref) + ba)
    ig = _sigmoid(_block_diag_dot(xrb, wx_ref) + bx)
    log_a = (-LRU_C) * r * sp
    a = jnp.exp(log_a)
    mult = jnp.sqrt(_one_minus_square(log_a, a))
    return r, ig, a, mult


def _colsum(v):
    return jnp.sum(v, axis=0, keepdims=True)


N_FWD_OUT = 7


def _fwd_mix(x, g1, w_in_g, conv_w, rconv_w, rconv_b, wa_bd, b_a, wx_bd, b_x, lam, g_nc, g_nr, later):
    t, d = x.shape
    tm = TOKEN_TILE
    nt = t // tm
    nl = len(later)
    assert nl == 3
    pass_on_at = [nt * f // 16 for f in (3, 5, 9)]
    neighbours_at = [nt * f // 16 for f in (10, 11, 12)]
    diagonal_at = [nt * f // 16 for f in (13, 14, 14)]

    def body(x_ref, g1_ref, win_ref, cw_ref, rw_ref, rb_ref, wa_ref, ba_ref, wx_ref, bx_ref, lam_ref, gnc_ref, gnr_ref,
             *rest):
        later_in, outs, rest = rest[:nl], rest[nl:nl + N_FWD_OUT], rest[nl + N_FWD_OUT:]
        u_ref, h1_ref, xr_ref, hs_ref, c3_ref, y_ref, gates_ref = outs
        later_out, (cv_prev, xin_prev, h_prev, send_sems, recv_sems) = rest[:nl], rest[nl:]
        del later_in
        step = pl.program_id(0)
        plan = _ShardGather(later_out, send_sems, recv_sems)

        @pl.when(step == 0)
        def _():
            cv_prev[...] = jnp.zeros_like(cv_prev)
            xin_prev[...] = jnp.zeros_like(xin_prev)
            h_prev[...] = jnp.zeros_like(h_prev)
            for w in range(nl):
                plan.start_direct(w)

        for w in range(nl):
            @pl.when(step == pass_on_at[w])
            def _(w=w):
                plan.start_pass_on(w)

            @pl.when(step == neighbours_at[w])
            def _(w=w):
                plan.start_hand_over(w, diagonal=False)

            @pl.when(step == diagonal_at[w])
            def _(w=w):
                plan.start_hand_over(w, diagonal=True)

        xv = x_ref[...]
        rstd = lax.rsqrt(jnp.mean(xv * xv, axis=-1, keepdims=True) + EPS)
        h1b = ((xv * rstd) * g1_ref[...]).astype(BF16)
        h1_ref[...] = h1b
        for j in range(N_CHIPS):
            u_ref[:, j * IN_SHARD:(j + 1) * IN_SHARD] = jnp.dot(h1b, win_ref[j], preferred_element_type=F32)
        gate_b = u_ref[:, 0:CONV_W]
        cv = u_ref[:, CONV_W:2 * CONV_W] * u_ref[:, 2 * CONV_W:3 * CONV_W]
        x_r = u_ref[:, 3 * CONV_W:3 * CONV_W + LRU_W]
        g = u_ref[:, 3 * CONV_W + LRU_W:]

        cw = cw_ref[...]
        cvp = cv_prev[...]
        conv3 = cw[0:1] * _shift_down(cv, 2, cvp) + cw[1:2] * _shift_down(cv, 1, cvp) + cw[2:3] * cv
        cv_prev[...] = cv[tm - TILE_ROWS:]
        c3_ref[...] = conv3
        y_conv = gate_b * conv3

        rw = rw_ref[...]
        xp = xin_prev[...]
        xr = (rw[0:1] * _shift_down(x_r, 3, xp) + rw[1:2] * _shift_down(x_r, 2, xp)
              + rw[2:3] * _shift_down(x_r, 1, xp) + rw[3:4] * x_r) + rb_ref[...]
        xin_prev[...] = x_r[tm - TILE_ROWS:]
        xr_ref[...] = xr
        sp, _ = _softplus_neg(lam_ref[...])
        r, ig, a, mult = _lru_gates(xr, wa_ref, ba_ref[...], wx_ref, bx_ref[...], sp)
        for n, gate in enumerate((r, ig, a, mult)):
            gates_ref[:, n * LRU_W:(n + 1) * LRU_W] = gate
        h = _scan_rows(a, mult * (ig * xr), h_prev[...])
        h_prev[...] = h[tm - 1:tm]
        hs_ref[...] = h
        gelu, _ = _gelu_and_grad(g)
        y_rnn = h * gelu

        na = y_conv * lax.rsqrt(jnp.mean(y_conv * y_conv, axis=-1, keepdims=True) + EPS) * gnc_ref[...]
        nb = y_rnn * lax.rsqrt(jnp.mean(y_rnn * y_rnn, axis=-1, keepdims=True) + EPS) * gnr_ref[...]
        y_ref[:, :CONV_W] = na.astype(BF16)
        y_ref[:, CONV_W:] = nb.astype(BF16)

        @pl.when(step == nt - 1)
        def _():
            for w in range(nl):
                plan.finish(w)

    def full(a):
        nd = a.ndim
        return pl.BlockSpec(a.shape, lambda i: (0,) * nd)

    def tok(cols):
        return pl.BlockSpec((tm, cols), lambda i: (i, 0))

    def act(cols, dtype=F32):
        return jax.ShapeDtypeStruct((t, cols), dtype)

    smalls = (g1, w_in_g, conv_w, rconv_w, rconv_b, wa_bd, b_a, wx_bd, b_x, lam, g_nc, g_nr)
    n_in = 1 + len(smalls)
    outs = pl.pallas_call(
        body, name="fwd_mix", grid=(nt,),
        in_specs=[tok(d)] + [full(a) for a in smalls] + [ANY] * nl,
        out_specs=[tok(IN_COLS), tok(d), tok(LRU_W), tok(LRU_W), tok(CONV_W), tok(CONV_W + LRU_W)]
        + [tok(4 * LRU_W)] + [ANY] * nl,
        out_shape=[act(IN_COLS), act(d, BF16), act(LRU_W), act(LRU_W), act(CONV_W), act(CONV_W + LRU_W, BF16)]
        + [act(4 * LRU_W)] + [jax.ShapeDtypeStruct(a.shape, a.dtype) for a in later],
        input_output_aliases={n_in + w: N_FWD_OUT + w for w in range(nl)},
        scratch_shapes=[pltpu.VMEM((TILE_ROWS, CONV_W), F32), pltpu.VMEM((TILE_ROWS, LRU_W), F32),
                        pltpu.VMEM((1, LRU_W), F32), pltpu.SemaphoreType.DMA((nl, _ShardGather.PAIRS)),
                        pltpu.SemaphoreType.DMA((nl, _ShardGather.PAIRS))],
        compiler_params=_params(dimension_semantics=("arbitrary",)),
    )(x, *smalls, *later)
    return outs[:N_FWD_OUT], outs[N_FWD_OUT:]


def _mlp_fwd_bwd(x, yb, w_out_g, w1_g, w2_g, g2, gf, target):
    t, d = x.shape
    tm = TOKEN_TILE
    ff = w2_g.shape[0]
    mix = w_out_g.shape[0]
    ffs = ff // N_CHIPS

    def body(x_ref, y_ref, g2_ref, gf_ref, tgt_ref, wout_hbm, w1_hbm, w2_hbm,
             z_ref, dp_ref, h2_ref, dx3b_ref, dx2_ref, dx2b_ref, dy_ref, st_ref, wout, w1, w2, p_ref):
        @pl.when(pl.program_id(0) == 0)
        def _():
            pltpu.sync_copy(wout_hbm, wout)
            pltpu.sync_copy(w1_hbm, w1)
            pltpu.sync_copy(w2_hbm, w2)
            st_ref[...] = jnp.zeros_like(st_ref)

        x2 = x_ref[...] + jnp.dot(y_ref[...], wout[...], preferred_element_type=F32)
        r2 = lax.rsqrt(jnp.mean(x2 * x2, axis=-1, keepdims=True) + EPS)
        xh2 = x2 * r2
        g2v = g2_ref[...]
        h2b = (xh2 * g2v).astype(BF16)
        h2_ref[...] = h2b
        for j in range(N_CHIPS):
            p_ref[:, j * ffs:(j + 1) * ffs] = jnp.dot(h2b, w1[j], preferred_element_type=F32)
        rp = jnp.maximum(p_ref[...], 0.0)
        zb = (rp * rp).astype(BF16)
        z_ref[...] = zb
        x3 = x2 + jnp.dot(zb, w2[...], preferred_element_type=F32)
        r3 = lax.rsqrt(jnp.mean(x3 * x3, axis=-1, keepdims=True) + EPS)
        xh3 = x3 * r3
        gfv = gf_ref[...]
        err = xh3 * gfv - tgt_ref[...]
        loss = (0.5 / d) * jnp.sum(err * err)
        dout = err * (1.0 / d)
        st_ref[PK_FINAL_G - PK_MIX_ROWS:PK_FINAL_G - PK_MIX_ROWS + 1, :] += _colsum(dout * xh3)
        st_ref[PK_LOSS - PK_MIX_ROWS:PK_LOSS - PK_MIX_ROWS + 1, :] += jnp.zeros((1, d), F32) + loss
        dxh3 = dout * gfv
        dx3 = r3 * (dxh3 - xh3 * jnp.mean(dxh3 * xh3, axis=-1, keepdims=True))
        dx3b = dx3.astype(BF16)
        dx3b_ref[...] = dx3b
        dpb = (_dot_nt(dx3b, w2[...]) * (2.0 * rp)).astype(BF16)
        dp_ref[...] = dpb
        dh2 = _dot_nt(dpb[:, 0:ffs], w1[0])
        for j in range(1, N_CHIPS):
            dh2 = dh2 + _dot_nt(dpb[:, j * ffs:(j + 1) * ffs], w1[j])
        st_ref[PK_MLP_G - PK_MIX_ROWS:PK_MLP_G - PK_MIX_ROWS + 1, :] += _colsum(dh2 * xh2)
        dxh2 = dh2 * g2v
        dx2 = dx3 + r2 * (dxh2 - xh2 * jnp.mean(dxh2 * xh2, axis=-1, keepdims=True))
        dx2_ref[...] = dx2
        dx2b = dx2.astype(BF16)
        dx2b_ref[...] = dx2b
        dy_ref[...] = _dot_nt(dx2b, wout[...])

    def tok(cols):
        return pl.BlockSpec((tm, cols), lambda i: (i, 0))

    def row(cols):
        return pl.BlockSpec((1, cols), lambda i: (0, 0))

    return pl.pallas_call(
        body, name="mlp_fwd_bwd", grid=(t // tm,),
        in_specs=[tok(d), tok(mix), row(d), row(d), tok(d), ANY, ANY, ANY],
        out_specs=[tok(ff), tok(ff), tok(d), tok(d), tok(d), tok(d), tok(mix),
                   pl.BlockSpec((PK_MLP_ROWS, d), lambda i: (0, 0))],
        out_shape=[jax.ShapeDtypeStruct((t, ff), BF16), jax.ShapeDtypeStruct((t, ff), BF16),
                   jax.ShapeDtypeStruct((t, d), BF16), jax.ShapeDtypeStruct((t, d), BF16),
                   jax.ShapeDtypeStruct((t, d), F32), jax.ShapeDtypeStruct((t, d), BF16),
                   jax.ShapeDtypeStruct((t, mix), F32), jax.ShapeDtypeStruct((PK_MLP_ROWS, d), F32)],
        scratch_shapes=[pltpu.VMEM(w_out_g.shape, BF16), pltpu.VMEM(w1_g.shape, BF16), pltpu.VMEM(w2_g.shape, BF16),
                        pltpu.VMEM((tm, ff), F32)],
        compiler_params=_params(dimension_semantics=("arbitrary",)),
    )(x, yb, g2, gf, target, w_out_g, w1_g, w2_g)


def _mix_bwd(dy, u, xr_all, hs_all, c3_all, gates, conv_w, rconv_w, wa_bd, wx_bd, lam, g_nc, g_nr, st_mlp, parts):
    t = dy.shape[0]
    tm = TOKEN_TILE
    nt = t // tm
    hb = tm // TILE_ROWS
    npart = len(parts)

    def body(dy_ref, u_ref, uh_ref, xr_ref, hs_ref, hh_ref, c3_ref, gates_ref,
             cw_ref, rw_ref, wa_ref, wx_ref, lam_ref, gnc_ref, gnr_ref, stm_ref, *rest):
        part_refs, (du_ref, st_ref, heads_ref), rest = rest[:npart], rest[npart:npart + 3], rest[npart + 3:]
        arrived_refs, (dc_next, a_next, gs_next, dxr_next, dwa_ref, dwx_ref, send_sems, recv_sems) = rest[:npart], rest[npart:]
        exchange = _PartialExchange(part_refs, arrived_refs, send_sems, recv_sems)
        i = pl.program_id(0)

        @pl.when(i == 0)
        def _():
            exchange.start()
            dc_next[...] = jnp.zeros_like(dc_next)
            a_next[...] = jnp.zeros_like(a_next)
            gs_next[...] = jnp.zeros_like(gs_next)
            dxr_next[...] = jnp.zeros_like(dxr_next)
            st_ref[0:PK_MIX_ROWS, :] = jnp.zeros((PK_MIX_ROWS, LRU_W), F32)
            st_ref[PK_MIX_ROWS:, :] = stm_ref[...]
            dwa_ref[...] = jnp.zeros_like(dwa_ref)
            dwx_ref[...] = jnp.zeros_like(dwx_ref)

        first_tile = i == nt - 1
        gate_b = u_ref[:, 0:CONV_W]
        gate_c = u_ref[:, CONV_W:2 * CONV_W]
        v = u_ref[:, 2 * CONV_W:3 * CONV_W]
        x_r = u_ref[:, 3 * CONV_W:3 * CONV_W + LRU_W]
        g = u_ref[:, 3 * CONV_W + LRU_W:]
        cv = gate_c * v
        cv_prev = jnp.where(first_tile, 0.0, uh_ref[:, CONV_W:2 * CONV_W] * uh_ref[:, 2 * CONV_W:3 * CONV_W])
        xin_prev = jnp.where(first_tile, 0.0, uh_ref[:, 3 * CONV_W:3 * CONV_W + LRU_W])
        hs_prev = jnp.where(first_tile, 0.0, hh_ref[...])

        def acc(first_row, val, width=LRU_W, row=0):
            r0 = first_row + row
            st_ref[r0:r0 + 1, 0:width] += val

        conv3 = c3_ref[...]
        y_conv = gate_b * conv3
        ra = lax.rsqrt(jnp.mean(y_conv * y_conv, axis=-1, keepdims=True) + EPS)
        xha = y_conv * ra
        dna = dy_ref[:, :CONV_W]
        acc(PK_G_NORM_CONV, _colsum(dna * xha), CONV_W)
        dxha = dna * gnc_ref[...]
        dy_conv = ra * (dxha - xha * jnp.mean(dxha * xha, axis=-1, keepdims=True))
        du_ref[:, 0:CONV_W] = (dy_conv * conv3).astype(BF16)
        dc = dy_conv * gate_b
        cw = cw_ref[...]
        dcn = dc_next[...]
        dcv = cw[2:3] * dc + cw[1:2] * _shift_up(dc, 1, dcn) + cw[0:1] * _shift_up(dc, 2, dcn)
        dc_next[...] = dc[:TILE_ROWS]
        acc(PK_CONV_W, _colsum(dc * _shift_down(cv, 2, cv_prev)), CONV_W, 0)
        acc(PK_CONV_W, _colsum(dc * _shift_down(cv, 1, cv_prev)), CONV_W, 1)
        acc(PK_CONV_W, _colsum(dc * cv), CONV_W, 2)
        du_ref[:, CONV_W:2 * CONV_W] = (dcv * v).astype(BF16)
        du_ref[:, 2 * CONV_W:3 * CONV_W] = (dcv * gate_c).astype(BF16)

        hs = hs_ref[...]
        gelu, dgelu = _gelu_and_grad(g)
        y_rnn = hs * gelu
        rb = lax.rsqrt(jnp.mean(y_rnn * y_rnn, axis=-1, keepdims=True) + EPS)
        xhb = y_rnn * rb
        dnb = dy_ref[:, CONV_W:]
        acc(PK_G_NORM_RNN, _colsum(dnb * xhb))
        dxhb = dnb * gnr_ref[...]
        dy_rnn = rb * (dxhb - xhb * jnp.mean(dxhb * xhb, axis=-1, keepdims=True))
        du_ref[:, 3 * CONV_W + LRU_W:] = (dy_rnn * hs * dgelu).astype(BF16)
        dh = dy_rnn * gelu

        xr = xr_ref[...]
        xrb = xr.astype(BF16)
        sp, dsp = _softplus_neg(lam_ref[...])
        r, ig, a, mult = [gates_ref[:, n * LRU_W:(n + 1) * LRU_W] for n in range(4)]
        a_up = _shift_up(a, 1, a_next[...])
        a_next[...] = a[:TILE_ROWS]
        gs = _scan_rows(a_up, dh, gs_next[0:1, :], reverse=True)
        gs_next[...] = gs[:TILE_ROWS]
        da = gs * _shift_down(hs, 1, hs_prev)
        gx = gs * xr
        di = gx * mult
        dmult = gx * ig
        dxr = gs * (mult * ig)
        dlog_a = da * a - dmult * ((a * a) / mult)
        acc(PK_LAMBDA, _colsum(dlog_a * r) * ((-LRU_C) * dsp))
        dpa = (dlog_a * ((-LRU_C) * sp)) * (r * (1.0 - r))
        dpx = di * (ig * (1.0 - ig))
        acc(PK_B_A, _colsum(dpa))
        acc(PK_B_X, _colsum(dpx))
        dpab = dpa.astype(BF16)
        dpxb = dpx.astype(BF16)
        dxr = dxr + _block_diag_dot_t(dpab, wa_ref) + _block_diag_dot_t(dpxb, wx_ref)
        for j in range(N_BD):
            cols = slice(j * BD, (j + 1) * BD)
            dwa_ref[j] += _dot_tn(xrb[:, cols], dpab[:, cols])
            dwx_ref[j] += _dot_tn(xrb[:, cols], dpxb[:, cols])

        acc(PK_RCONV_B, _colsum(dxr))
        rw = rw_ref[...]
        dxn = dxr_next[...]
        dx_r = (rw[3:4] * dxr + rw[2:3] * _shift_up(dxr, 1, dxn) + rw[1:2] * _shift_up(dxr, 2, dxn)
                + rw[0:1] * _shift_up(dxr, 3, dxn))
        dxr_next[...] = dxr[:TILE_ROWS]
        for k in range(3):
            acc(PK_RCONV_W, _colsum(dxr * _shift_down(x_r, 3 - k, xin_prev)), LRU_W, k)
        acc(PK_RCONV_W, _colsum(dxr * x_r), LRU_W, 3)
        du_ref[:, 3 * CONV_W:3 * CONV_W + LRU_W] = dx_r.astype(BF16)

        @pl.when(i == nt - 1)
        def _():
            for n, d_ref in enumerate((dwa_ref, dwx_ref)):
                for b in range(N_BD):
                    for q in range(BD // HEAD_DIM):
                        lane0 = q * HEAD_DIM // LANES * LANES
                        wide = d_ref[b, q * HEAD_DIM:(q + 1) * HEAD_DIM, lane0:lane0 + LANES]
                        if q * HEAD_DIM != lane0:
                            wide = pltpu.roll(wide, LANES - (q * HEAD_DIM - lane0), axis=1)
                        heads_ref[n, b * (BD // HEAD_DIM) + q] = wide[:, 0:HEAD_DIM].astype(BF16)
            exchange.wait()

    def full(a):
        nd = a.ndim
        return pl.BlockSpec(a.shape, lambda i: (0,) * nd)

    def tok(cols):
        return pl.BlockSpec((tm, cols), lambda i: (nt - 1 - i, 0))

    def halo(cols):
        return pl.BlockSpec((TILE_ROWS, cols), lambda i: (jnp.maximum((nt - 1 - i) * hb - 1, 0), 0))

    smalls = (conv_w, rconv_w, wa_bd, wx_bd, lam, g_nc, g_nr, st_mlp)
    st_rows = PK_MIX_ROWS + st_mlp.shape[0]
    heads = (2, N_HEADS, HEAD_DIM, HEAD_DIM)
    outs = pl.pallas_call(
        body, name="mix_bwd", grid=(nt,),
        in_specs=[tok(CONV_W + LRU_W), tok(IN_COLS), halo(IN_COLS), tok(LRU_W), tok(LRU_W), halo(LRU_W), tok(CONV_W)]
        + [tok(4 * LRU_W)] + [full(a) for a in smalls] + [ANY] * npart,
        out_specs=[tok(IN_COLS), pl.BlockSpec((st_rows, LRU_W), lambda i: (0, 0)),
                   pl.BlockSpec(heads, lambda i: (0, 0, 0, 0))]
        + [ANY] * npart,
        out_shape=[jax.ShapeDtypeStruct((t, IN_COLS), BF16), jax.ShapeDtypeStruct((st_rows, LRU_W), F32),
                   jax.ShapeDtypeStruct(heads, BF16)]
        + [jax.ShapeDtypeStruct(a.shape, a.dtype) for a in parts],
        scratch_shapes=[pltpu.VMEM((TILE_ROWS, CONV_W), F32), pltpu.VMEM((TILE_ROWS, LRU_W), F32),
                        pltpu.VMEM((TILE_ROWS, LRU_W), F32), pltpu.VMEM((TILE_ROWS, LRU_W), F32),
                        pltpu.VMEM((N_BD, BD, BD), F32), pltpu.VMEM((N_BD, BD, BD), F32),
                        pltpu.SemaphoreType.DMA((npart, 3)), pltpu.SemaphoreType.DMA((npart, 3))],
        compiler_params=_params(dimension_semantics=("arbitrary",)),
    )(dy, u, u, xr_all, hs_all, hs_all, c3_all, gates, *smalls, *parts)
    return outs[:3], outs[3:]


def _in_bwd(dub, w_in_g, x, dx2, g1, parts, joins, core_chip):
    t, d = x.shape
    tm = min(t, MATMUL_TOKEN_TILE)
    nt = t // tm
    npart = len(parts)
    nj = len(joins)
    geometry = []
    for tag, shape, _, _ in joins:
        pr, pc = WGRAD_GEOMETRY[tag][:2]
        every = 1 if pr % (nt * 16) == 0 else 2
        geometry.append((pr, pc, pr * every // nt, every, shape[1] == pc))

    def body(cc_ref, du_ref, win_ref, x_ref, dx2_ref, g1_ref, *rest):
        sums, rest = [rest[4 * w:4 * w + 4] for w in range(nj)], rest[4 * nj:]
        part_refs, (gx_ref, st_ref), rest = rest[:npart], rest[npart:npart + 2], rest[npart + 2:]
        arrived_refs, joined, rest = rest[:npart], rest[npart:npart + nj], rest[npart + nj:]
        stages, (send_sems, recv_sems, j_local, j_send, j_recv) = rest[:nj], rest[nj:]
        exchange = _PartialExchange(part_refs, arrived_refs, send_sems, recv_sems)
        i = pl.program_id(0)
        c = cc_ref[0]

        def window(w, core, row0, rows):
            pr, pc, _, _, by_rows = geometry[w]
            if by_rows:
                return joined[w].at[pl.ds(core * pr + row0, rows), :]
            return joined[w].at[pl.ds(row0, rows), pl.ds(core * pc, pc)]

        def to_sibling(w, src, core, row0, rows):
            return pltpu.make_async_remote_copy(src_ref=src, dst_ref=window(w, core, row0, rows), send_sem=j_send.at[w],
                                                recv_sem=j_recv.at[w], device_id=_sibling(), device_id_type=MESH)

        @pl.when(i == 0)
        def _():
            exchange.start()
            st_ref[...] = jnp.zeros_like(st_ref)

        for w in range(nj):
            pr, pc, rb, every, _ = geometry[w]

            @pl.when(i % every == 0)
            def _(w=w, rb=rb, every=every):
                p_ref, r1_ref, r2_ref, r3_ref = sums[w]
                row0 = pl.multiple_of((i // every) * rb, rb)
                rows = stages[w].at[pl.ds(row0, rb), :]
                rows[...] = ((p_ref[0] + r1_ref[0].astype(F32)) + r2_ref[0].astype(F32)) + r3_ref[0].astype(F32)
                pltpu.make_async_copy(rows, window(w, c, row0, rb), j_local.at[w]).start()
                to_sibling(w, rows, c, row0, rb).start()

        dh1 = _dot_nt(du_ref[:, 0:IN_SHARD], win_ref[0])
        for j in range(1, N_CHIPS):
            dh1 = dh1 + _dot_nt(du_ref[:, j * IN_SHARD:(j + 1) * IN_SHARD], win_ref[j])
        xv = x_ref[...]
        rstd = lax.rsqrt(jnp.mean(xv * xv, axis=-1, keepdims=True) + EPS)
        xh = xv * rstd
        st_ref[0:1, :] += _colsum(dh1 * xh)
        dxh = dh1 * g1_ref[...]
        gx_ref[...] = dx2_ref[...] + rstd * (dxh - xh * jnp.mean(dxh * xh, axis=-1, keepdims=True))

        @pl.when(i == nt - 1)
        def _():
            exchange.wait()
            for w in range(nj):
                pr = geometry[w][0]
                pltpu.make_async_copy(stages[w], window(w, c, 0, pr), j_local.at[w]).wait()
                to_sibling(w, stages[w], 1 - c, 0, pr).wait()

    def tok(cols):
        return pl.BlockSpec((tm, cols), lambda i, cc: (i, 0))

    def partial(w, off):
        pr, pc, rb, every, _ = geometry[w]
        return pl.BlockSpec((1, rb, pc), lambda i, cc: ((cc[1] + off) % N_CHIPS, i // every, 0))

    sum_specs, sum_operands = [], []
    for w, (_, _, own, arrived) in enumerate(joins):
        sum_specs += [partial(w, off) for off in range(N_CHIPS)]
        sum_operands += [own, arrived, arrived, arrived]
    dma = pltpu.SemaphoreType.DMA
    outs = pl.pallas_call(
        body, name="in_bwd",
        grid_spec=pltpu.PrefetchScalarGridSpec(
            num_scalar_prefetch=1, grid=(nt,),
            in_specs=[tok(IN_COLS), pl.BlockSpec(w_in_g.shape, lambda i, cc: (0, 0, 0)), tok(d), tok(d),
                      pl.BlockSpec((1, d), lambda i, cc: (0, 0))] + sum_specs + [ANY] * npart,
            out_specs=[tok(d), pl.BlockSpec((TILE_ROWS, d), lambda i, cc: (0, 0))] + [ANY] * (npart + nj),
            scratch_shapes=[pltpu.VMEM((g[0], g[1]), F32) for g in geometry]
            + [dma((npart, 3)), dma((npart, 3)), dma((nj,)), dma((nj,)), dma((nj,))]),
        out_shape=[jax.ShapeDtypeStruct((t, d), F32), jax.ShapeDtypeStruct((TILE_ROWS, d), F32)]
        + [jax.ShapeDtypeStruct(a.shape, a.dtype) for a in parts]
        + [jax.ShapeDtypeStruct(shape, F32) for _, shape, _, _ in joins],
        compiler_params=_params(dimension_semantics=("arbitrary",)),
    )(core_chip, dub, w_in_g, x, dx2, g1, *sum_operands, *parts)
    return outs[:2], outs[2:2 + npart], outs[2 + npart:]


WGRAD_GEOMETRY = {
    "in": (512, IN_SHARD, lambda s, h: h, lambda s, h: s),
    "mlp_in": (512, D_MODEL, lambda s, h: h, lambda s, h: s),
    "mlp_out": (512, D_MODEL, lambda s, h: 2 * s + h, lambda s, h: 0),
    "out": (384, 512, lambda s, h: s, lambda s, h: h),
}
K_CHUNK = 512
TOKEN_STREAMS = 2


def _sibling():
    x, y, c = _position()
    return (x, y, 1 - c)


def _wgrad(a, b, tag, core_chip, packs=(), parts=()):
    t = a.shape[0]
    pr, pc, a_blk, b_blk = WGRAD_GEOMETRY[tag]
    ns = TOKEN_STREAMS
    ts = t // ns
    kc = min(K_CHUNK, ts)
    mine = N_CHIPS
    riding = len(packs)
    npart = len(parts)
    assert not (riding and npart)

    def body(cc_ref, *rest):
        a_refs, b_refs, rest = rest[:ns], rest[ns:2 * ns], rest[2 * ns:]
        if riding:
            pack_refs, (land_ref, p_ref, pb_ref), rest = rest[:riding], rest[riding:riding + 3], rest[riding + 3:]
            all_refs, (stage, rbuf, send_sems, recv_sems, rsem), g_sems = rest[:riding], rest[riding:riding + 5], rest[riding + 5:]
            gathers = [_PackGather(pack_refs[n], all_refs[n], *g_sems[3 * n:3 * n + 3]) for n in range(riding)]
        elif npart:
            part_refs, (land_ref, p_ref, pb_ref), rest = rest[:npart], rest[npart:npart + 3], rest[npart + 3:]
            arrived_refs, (stage, rbuf, send_sems, recv_sems, rsem, x_send, x_recv) = rest[:npart], rest[npart:]
            exchange = _PartialExchange(part_refs, arrived_refs, x_send, x_recv)
        else:
            land_ref, p_ref, pb_ref, stage, rbuf, send_sems, recv_sems, rsem = rest
        ph, s = pl.program_id(0), pl.program_id(1)
        if riding:
            @pl.when((ph == 0) & (s == 0))
            def _():
                for gather in gathers:
                    gather.start()

            @pl.when((ph == 1) & (s == N_CHIPS - 2))
            def _():
                for gather in gathers:
                    gather.hand_over()
        if npart:
            @pl.when((ph == 0) & (s == 0))
            def _():
                exchange.start()
        def push(k):
            return pltpu.make_async_remote_copy(src_ref=stage.at[k], dst_ref=land_ref.at[k], send_sem=send_sems.at[k],
                                                recv_sem=recv_sems.at[k], device_id=_sibling(), device_id_type=MESH)

        def landed():
            return pltpu.make_async_copy(land_ref.at[s], rbuf, rsem)

        @pl.when(ph == 1)
        def _():
            push(s).wait_recv()
            landed().start()

        slot = jnp.where(ph == 0, s, mine)
        acc = stage.at[slot]
        chunks = [(a_ref, b_ref, k) for a_ref, b_ref in zip(a_refs, b_refs) for k in range(0, ts, kc)]
        for n, (a_ref, b_ref, k) in enumerate(chunks):
            part = _dot_tn(a_ref[k:k + kc, :], b_ref[k:k + kc, :])
            if n == 0:
                acc[...] = part
            else:
                acc[...] += part

        @pl.when(ph == 0)
        def _():
            push(s).start()

        @pl.when(ph == 1)
        def _():
            landed().wait()
            p = stage[mine] + rbuf[...]
            p_ref[0] = p
            pb_ref[0] = p.astype(BF16)

        @pl.when((ph == 1) & (s == N_CHIPS - 1))
        def _():
            for k in range(N_CHIPS):
                push(k).wait_send()
            for gather in (gathers if riding else ()):
                gather.finish()
            if npart:
                exchange.wait()

    def half(ph, cc):
        return jnp.where(ph == 0, 1 - cc[0], cc[0])

    def out_slot(ph, s, cc):
        return (jnp.where(ph == 0, 0, s), 0, 0)

    piece = jax.ShapeDtypeStruct((N_CHIPS, pr, pc), F32)
    in_specs = [pl.BlockSpec((ts, pr), lambda ph, s, cc, n=n: (n, a_blk(s, half(ph, cc)))) for n in range(ns)]
    in_specs += [pl.BlockSpec((ts, pc), lambda ph, s, cc, n=n: (n, b_blk(s, half(ph, cc)))) for n in range(ns)]
    out_specs = [ANY, pl.BlockSpec((1, pr, pc), out_slot), pl.BlockSpec((1, pr, pc), out_slot)]
    out_shape = [piece, piece, jax.ShapeDtypeStruct((N_CHIPS, pr, pc), BF16)]
    scratch = [pltpu.VMEM((N_CHIPS + 1, pr, pc), F32), pltpu.VMEM((pr, pc), F32),
               pltpu.SemaphoreType.DMA((N_CHIPS,)), pltpu.SemaphoreType.DMA((N_CHIPS,)), pltpu.SemaphoreType.DMA]
    operands = [a] * ns + [b] * ns
    for pack in packs:
        in_specs.append(pl.BlockSpec(pack.shape, lambda ph, s, cc, nd=pack.ndim: (0,) * nd))
        out_specs.append(ANY)
        out_shape.append(jax.ShapeDtypeStruct((N_DEVICES,) + pack.shape, pack.dtype))
        operands.append(pack)
    for pack in packs:
        scratch += _PackGather.semaphores()
    if npart:
        in_specs += [ANY] * npart
        out_specs += [ANY] * npart
        out_shape += [jax.ShapeDtypeStruct(p.shape, p.dtype) for p in parts]
        scratch += [pltpu.SemaphoreType.DMA((npart, 3)), pltpu.SemaphoreType.DMA((npart, 3))]
        operands += list(parts)
    return pl.pallas_call(
        body, name="wgrad_" + tag,
        grid_spec=pltpu.PrefetchScalarGridSpec(
            num_scalar_prefetch=1, grid=(2, N_CHIPS), in_specs=in_specs, out_specs=out_specs, scratch_shapes=scratch),
        out_shape=out_shape,
        compiler_params=_params(dimension_semantics=("arbitrary", "arbitrary")),
    )(core_chip, *operands)[1:]


def _other_chips(x, y):
    return [(1 - x, y), (x, 1 - y), (1 - x, 1 - y)]


class _ShardGather:
    PAIRS = 9

    def __init__(self, outs, send_sems, recv_sems):
        self.outs, self.send_sems, self.recv_sems = outs, send_sems, recv_sems
        x, y, c = _position()
        self.c, self.j = c, 2 * x + y
        self.sibling = (x, y, 1 - c)
        self.chips = _other_chips(x, y)

    def _chip(self, k):
        px, py = self.chips[k]
        return 2 * px + py

    def _half(self, w, chip, which):
        hr = self.outs[w].shape[1] // 2
        return self.outs[w].at[chip, pl.ds(which * hr, hr), :]

    def _quarter(self, w, chip, q):
        qr = self.outs[w].shape[1] // 4
        return self.outs[w].at[chip, pl.ds(self.c * 2 * qr + q * qr, qr), :]

    def _copy(self, ref, w, pair, to, src=None):
        return pltpu.make_async_remote_copy(src_ref=ref if src is None else src, dst_ref=ref, send_sem=self.send_sems.at[w, pair],
                                            recv_sem=self.recv_sems.at[w, pair], device_id=to, device_id_type=MESH)

    def direct(self, w, k, q, src=None):
        return self._copy(self._quarter(w, self.j, q), w, 2 * k + q, (*self.chips[k], self.c), src)

    def direct_landed(self, w, k, q):
        return self._copy(self._quarter(w, self._chip(k), q), w, 2 * k + q, (*self.chips[k], self.c))

    def pass_on(self, w, q):
        return self._copy(self._quarter(w, self._chip(q), q), w, 4 + q, (*self.chips[1 - q], self.c))

    def passed_landed(self, w, q):
        return self._copy(self._quarter(w, self._chip(2), q), w, 4 + q, (*self.chips[1 - q], self.c))

    def hand_over(self, w, k):
        return self._copy(self._half(w, self._chip(k), self.c), w, 6 + k, self.sibling)

    def handed(self, w, k):
        return self._copy(self._half(w, self._chip(k), 1 - self.c), w, 6 + k, self.sibling)

    def start_direct(self, w, src_half=None):
        qr = self.outs[w].shape[1] // 4
        for k, q in ((0, 0), (1, 1), (0, 1), (1, 0)):
            self.direct(w, k, q, None if src_half is None else src_half.at[pl.ds(q * qr, qr), :]).start()

    def start_pass_on(self, w):
        for q in (0, 1):
            self.direct_landed(w, q, q).wait_recv()
            self.pass_on(w, q).start()

    def start_hand_over(self, w, diagonal):
        if diagonal:
            for q in (0, 1):
                self.passed_landed(w, q).wait_recv()
            self.hand_over(w, 2).start()
        else:
            for k in (0, 1):
                self.direct_landed(w, k, 1 - k).wait_recv()
                self.hand_over(w, k).start()

    def finish(self, w):
        for k in range(3):
            self.handed(w, k).wait_recv()
            self.hand_over(w, k).wait_send()
        for q in (0, 1):
            self.pass_on(w, q).wait_send()
            for k in (0, 1):
                self.direct(w, k, q).wait_send()


def _gather_first(w_in, w_out, w1, w2, conv_w, rconv_w):
    bigs = (w_in, w_out, w1, w2)
    convs = (conv_w, rconv_w)
    nb, nc = len(bigs), len(convs)

    def body(win_ref, wout_ref, w1_ref, w2_ref, cw_ref, rw_ref, gin, gout, g1, g2, gcw, grw, st_in, st_out, st_1, st_2,
             st_cw, st_rw, send_sems, recv_sems, sm_send, sm_recv, local_sems):
        srcs = (win_ref, wout_ref, w1_ref, w2_ref)
        stages = (st_in, st_out, st_1, st_2)
        outs = (gin, gout, g1, g2)
        conv_stages, conv_outs = (st_cw, st_rw), (gcw, grw)
        plan = _ShardGather(outs[:1], send_sems, recv_sems)
        j, c = plan.j, plan.c
        local = [pltpu.make_async_copy(stages[w], outs[w].at[j], local_sems.at[w]) for w in range(nb)]

        def columns(n, chip):
            width = convs[n].shape[1]
            return conv_outs[n].at[:, pl.ds(chip * width, width)]

        local += [pltpu.make_async_copy(conv_stages[n], columns(n, j), local_sems.at[nb + n]) for n in range(nc)]

        def small_copy(k, n, landed=False):
            px, py = plan.chips[k]
            return pltpu.make_async_remote_copy(
                src_ref=conv_stages[n], dst_ref=columns(n, 2 * px + py if landed else j), send_sem=sm_send.at[k, n],
                recv_sem=sm_recv.at[k, n], device_id=(px, py, c), device_id_type=MESH)

        hr = w_in.shape[0] // 2
        st_in[...] = win_ref[...].astype(BF16)
        plan.start_direct(0, st_in.at[pl.ds(c * hr, hr), :])
        for src, st in zip((cw_ref, rw_ref), conv_stages):
            st[...] = jnp.zeros_like(st)
            st[0:src.shape[0], :] = src[...]
        for k in range(3):
            for n in range(nc):
                small_copy(k, n).start()
        for src, st in zip(srcs[1:], stages[1:]):
            st[...] = src[...].astype(BF16)
        for cp in local:
            cp.start()
        plan.start_pass_on(0)
        plan.start_hand_over(0, diagonal=False)
        plan.start_hand_over(0, diagonal=True)
        for k in range(3):
            for n in range(nc):
                small_copy(k, n, landed=True).wait_recv()
                small_copy(k, n).wait_send()
        plan.finish(0)
        for cp in local:
            cp.wait()

    def gathered(a, dtype):
        return jax.ShapeDtypeStruct((N_CHIPS,) + a.shape, dtype)

    return pl.pallas_call(
        body, name="gather_first",
        in_specs=[VMEM] * (nb + nc), out_specs=[ANY] * (nb + nc),
        out_shape=[gathered(a, BF16) for a in bigs]
        + [jax.ShapeDtypeStruct((TILE_ROWS, N_CHIPS * a.shape[1]), F32) for a in convs],
        scratch_shapes=[pltpu.VMEM(a.shape, BF16) for a in bigs] + [pltpu.VMEM((TILE_ROWS, a.shape[1]), F32) for a in convs]
        + [pltpu.SemaphoreType.DMA((1, _ShardGather.PAIRS)), pltpu.SemaphoreType.DMA((1, _ShardGather.PAIRS)),
           pltpu.SemaphoreType.DMA((3, nc)), pltpu.SemaphoreType.DMA((3, nc)), pltpu.SemaphoreType.DMA((nb + nc,))],
        compiler_params=_params(),
    )(*bigs, *convs)


class _PartialExchange:
    def __init__(self, parts, arrived, send_sems, recv_sems):
        self.parts, self.arrived, self.send_sems, self.recv_sems = parts, arrived, send_sems, recv_sems
        x, y, c = _position()
        self.c, self.j = c, 2 * x + y
        self.chips = _other_chips(x, y)

    def _copy(self, w, k, slot):
        px, py = self.chips[k]
        return pltpu.make_async_remote_copy(
            src_ref=self.parts[w].at[2 * px + py], dst_ref=self.arrived[w].at[slot], send_sem=self.send_sems.at[w, k],
            recv_sem=self.recv_sems.at[w, k], device_id=(px, py, self.c), device_id_type=MESH)

    def start(self):
        for w in range(len(self.parts)):
            for k in range(3):
                self._copy(w, k, self.j).start()

    def wait(self):
        for w in range(len(self.parts)):
            for k in range(3):
                px, py = self.chips[k]
                self._copy(w, k, 2 * px + py).wait()


class _PackGather:
    def __init__(self, p_ref, all_ref, send_sems, recv_sems, local_sem):
        self.p_ref, self.all_ref, self.send_sems, self.recv_sems, self.local_sem = p_ref, all_ref, send_sems, recv_sems, local_sem
        x, y, c = _position()
        self.me, self.sibling, self.c = (x, y, c), (x, y, 1 - c), c
        self.chips = _other_chips(x, y)

    @staticmethod
    def semaphores():
        return [pltpu.SemaphoreType.DMA((7,)), pltpu.SemaphoreType.DMA((7,)), pltpu.SemaphoreType.DMA]

    def _copy(self, k, block, to, from_pack=False):
        px, py, pc = block
        slot = self.all_ref.at[4 * px + 2 * py + pc]
        return pltpu.make_async_remote_copy(src_ref=self.p_ref if from_pack else slot, dst_ref=slot, send_sem=self.send_sems.at[k],
                                            recv_sem=self.recv_sems.at[k], device_id=to, device_id_type=MESH)

    def _mine(self):
        x, y, c = self.me
        return pltpu.make_async_copy(self.p_ref, self.all_ref.at[4 * x + 2 * y + c], self.local_sem)

    def _first(self):
        return [self._copy(0, self.me, self.sibling, True)] + [
            self._copy(1 + k, self.me, (*chip, self.c), True) for k, chip in enumerate(self.chips)]

    def _passed(self):
        return [self._copy(4 + k, (*chip, self.c), self.sibling) for k, chip in enumerate(self.chips)]

    def start(self):
        self._mine().start()
        for cp in self._first():
            cp.start()

    def hand_over(self):
        for k, chip in enumerate(self.chips):
            self._copy(1 + k, (*chip, self.c), self.me).wait_recv()
            self._passed()[k].start()

    def finish(self):
        self._copy(0, self.sibling, self.me).wait_recv()
        for k, chip in enumerate(self.chips):
            self._copy(4 + k, (*chip, 1 - self.c), self.me).wait_recv()
        for cp in self._first() + self._passed():
            cp.wait_send()
        self._mine().wait()


class _DirectGather:
    def __init__(self, p_ref, all_ref, send_sems, recv_sems, local_sem):
        self.p_ref, self.all_ref, self.send_sems, self.recv_sems, self.local_sem = p_ref, all_ref, send_sems, recv_sems, local_sem
        self.me = _position()

    semaphores = _PackGather.semaphores

    def _peer(self, r):
        x, y, c = self.me
        return ((1 - x) if r & 4 else x, (1 - y) if r & 2 else y, (1 - c) if r & 1 else c)

    def _copy(self, r, slot_of):
        px, py, pc = slot_of
        return pltpu.make_async_remote_copy(src_ref=self.p_ref, dst_ref=self.all_ref.at[4 * px + 2 * py + pc],
                                            send_sem=self.send_sems.at[r - 1], recv_sem=self.recv_sems.at[r - 1],
                                            device_id=self._peer(r), device_id_type=MESH)

    def _mine(self):
        x, y, c = self.me
        return pltpu.make_async_copy(self.p_ref, self.all_ref.at[4 * x + 2 * y + c], self.local_sem)

    def start(self):
        self._mine().start()
        for r in range(1, N_DEVICES):
            self._copy(r, self.me).start()

    def finish(self):
        for r in range(1, N_DEVICES):
            self._copy(r, self._peer(r)).wait()
        self._mine().wait()


def _adamw(w, g, m, v):
    m = ADAM_B1 * m + (1.0 - ADAM_B1) * g
    v = ADAM_B2 * v + (1.0 - ADAM_B2) * (g * g)
    m_hat = m / ADAM_BC1
    v_hat = v / ADAM_BC2
    delta = -ADAM_LR * (m_hat / (jnp.sqrt(v_hat) + ADAM_EPS) + ADAM_WD * w)
    return delta, m, v


JOIN_SUB = 4


def _join(tag, shard_shape, part, arrived, core_chip, block=None):
    pr, pc = WGRAD_GEOMETRY[tag][:2]
    rb = pr // JOIN_SUB
    by_rows = shard_shape[1] == pc
    riding = block is not None

    def body(cc_ref, p_ref, r1_ref, r2_ref, r3_ref, *rest):
        if riding:
            blk_ref, g_ref, all_ref, stage, send_sems, recv_sems, local_sems, b_send, b_recv, b_local = rest
            gather = _DirectGather(blk_ref, all_ref, b_send, b_recv, b_local)
        else:
            g_ref, stage, send_sems, recv_sems, local_sems = rest
        i = pl.program_id(0)
        c = cc_ref[0]
        if riding:
            @pl.when(i == 0)
            def _():
                gather.start()

        def window(core, k):
            if by_rows:
                return g_ref.at[pl.ds((core * JOIN_SUB + k) * rb, rb), :]
            return g_ref.at[pl.ds(k * rb, rb), pl.ds(core * pc, pc)]

        def keep(k):
            return pltpu.make_async_copy(stage.at[k], window(c, k), local_sems.at[k])

        def push(k):
            return pltpu.make_async_remote_copy(src_ref=stage.at[k], dst_ref=window(c, k), send_sem=send_sems.at[k],
                                                recv_sem=recv_sems.at[k], device_id=_sibling(), device_id_type=MESH)

        def pushed(k):
            return pltpu.make_async_remote_copy(src_ref=stage.at[k], dst_ref=window(1 - c, k), send_sem=send_sems.at[k],
                                                recv_sem=recv_sems.at[k], device_id=_sibling(), device_id_type=MESH)

        stage[i] = ((p_ref[0] + r1_ref[0].astype(F32)) + r2_ref[0].astype(F32)) + r3_ref[0].astype(F32)
        keep(i).start()
        push(i).start()

        @pl.when(i == JOIN_SUB - 1)
        def _():
            for k in range(JOIN_SUB):
                keep(k).wait()
                push(k).wait_send()
                pushed(k).wait_recv()
            if riding:
                gather.finish()

    def partial(off):
        return pl.BlockSpec((1, rb, pc), lambda i, cc: ((cc[1] + off) % N_CHIPS, i, 0))

    in_specs = [partial(0), partial(1), partial(2), partial(3)]
    out_specs = [ANY]
    out_shape = [jax.ShapeDtypeStruct(shard_shape, F32)]
    scratch = [pltpu.VMEM((JOIN_SUB, rb, pc), F32), pltpu.SemaphoreType.DMA((JOIN_SUB,)),
               pltpu.SemaphoreType.DMA((JOIN_SUB,)), pltpu.SemaphoreType.DMA((JOIN_SUB,))]
    operands = [part, arrived, arrived, arrived]
    if riding:
        in_specs.append(pl.BlockSpec(block.shape, lambda i, cc: (0, 0)))
        out_specs.append(ANY)
        out_shape.append(jax.ShapeDtypeStruct((N_DEVICES,) + block.shape, block.dtype))
        scratch += _DirectGather.semaphores()
        operands.append(block)
    outs = pl.pallas_call(
        body, name="join_" + tag,
        grid_spec=pltpu.PrefetchScalarGridSpec(
            num_scalar_prefetch=1, grid=(JOIN_SUB,), in_specs=in_specs, out_specs=out_specs, scratch_shapes=scratch),
        out_shape=out_shape,
        compiler_params=_params(dimension_semantics=("arbitrary",)),
    )(core_chip, *operands)
    return outs if riding else outs[0]


def _adamw_big(w, g, m, v, name):
    rows, cols = w.shape
    rb = ADAMW_ROWS if rows % ADAMW_ROWS == 0 else rows

    def body(w_ref, g_ref, m_ref, v_ref, go_ref, d_ref, nm_ref, nv_ref):
        g = g_ref[...]
        go_ref[...] = g
        d_ref[...], nm_ref[...], nv_ref[...] = _adamw(w_ref[...], g, m_ref[...], v_ref[...])

    spec = pl.BlockSpec((rb, cols), lambda i: (i, 0))
    return pl.pallas_call(
        body, name=name, grid=(rows // rb,), in_specs=[spec] * 4, out_specs=[spec] * 4,
        out_shape=[jax.ShapeDtypeStruct(w.shape, F32)] * 4,
        compiler_params=_params(dimension_semantics=("arbitrary",)),
    )(w, g, m, v)


SMALL_VECTORS = {
    "norm_mix_g": (PK_MIX_G, D_MODEL), "rnn_conv_b": (PK_RCONV_B, LRU_W), "b_a": (PK_B_A, LRU_W), "b_x": (PK_B_X, LRU_W),
    "lru_lambda": (PK_LAMBDA, LRU_W), "g_norm_conv": (PK_G_NORM_CONV, CONV_W), "g_norm_rnn": (PK_G_NORM_RNN, LRU_W),
    "norm_mlp_g": (PK_MLP_G, D_MODEL), "final_norm_g": (PK_FINAL_G, D_MODEL),
}
SMALL_MATRICES = ("w_a", "w_x")


def _small_step(vec_packs, mat_packs, mix_g_blocks, p):
    vec_rows, cols = vec_packs.shape[1:]
    conv_rows, cshard = p["conv_w"].shape
    rconv_rows, rshard = p["rnn_conv_w"].shape
    names = list(SMALL_VECTORS) + list(SMALL_MATRICES) + ["conv_w", "rnn_conv_w"]
    shapes = ([(1, width) for _, width in SMALL_VECTORS.values()] + [mat_packs.shape[2:]] * len(SMALL_MATRICES)
              + [(conv_rows, cshard), (rconv_rows, rshard)])
    kinds = ("", "m_", "v_")
    params = [p[pre + n].reshape(1, -1) if n in SMALL_VECTORS else p[pre + n] for pre in kinds for n in names]

    def body(vec_ref, mat_ref, blk_ref, *rest):
        wmv = [dict(zip(names, rest[k * len(names):(k + 1) * len(names)])) for k in range(3)]
        loss_ref, rest = rest[3 * len(names)], rest[3 * len(names) + 1:]
        leaves, (g_ref, w_ref, m_ref, v_ref) = [rest[k * len(names):(k + 1) * len(names)] for k in range(4)], rest[4 * len(names):]
        total = vec_ref[0]
        mats = mat_ref[0].astype(F32)
        late = blk_ref[0]
        for k in range(1, N_DEVICES):
            total = total + vec_ref[k]
            mats = mats + mat_ref[k].astype(F32)
            late = late + blk_ref[k]
        g_ref[0:vec_rows, :] = total
        g_ref[vec_rows:, :] = late
        g = g_ref[...]
        loss_ref[...] = g[PK_LOSS:PK_LOSS + 1, 0:1]

        for pack_ref, given in zip((w_ref, m_ref, v_ref), wmv):
            pack_ref[...] = jnp.zeros_like(pack_ref)
            for name, (row, width) in SMALL_VECTORS.items():
                pack_ref[row:row + 1, 0:width] = given[name][...]

        x, y, _ = _position()
        j = 2 * x + y
        cblk = total[0:TILE_ROWS, :]
        rblk = total[PK_RCONV_W:PK_RCONV_W + TILE_ROWS, :]
        cg = cblk[:, 0:cshard]
        rg = rblk[:, 0:rshard]
        for k in range(1, N_CHIPS):
            cg = jnp.where(j == k, cblk[:, k * cshard:(k + 1) * cshard], cg)
            rg = jnp.where(j == k, rblk[:, k * rshard:(k + 1) * rshard], rg)
        cg = cg[PK_CONV_W:PK_CONV_W + conv_rows, :]
        rg = rg[0:rconv_rows, :]

        def step(name, grad):
            return (grad,) + _adamw(wmv[0][name][...], grad, wmv[1][name][...], wmv[2][name][...])

        packs = (g,) + _adamw(w_ref[...], g, m_ref[...], v_ref[...])
        matrices = [step(name, mats[n]) for n, name in enumerate(SMALL_MATRICES)]
        convs, rconvs = step("conv_w", cg), step("rnn_conv_w", rg)
        for kind in range(4):
            out = dict(zip(names, leaves[kind]))
            for name, (row, width) in SMALL_VECTORS.items():
                out[name][...] = packs[kind][row:row + 1, 0:width]
            for n, name in enumerate(SMALL_MATRICES):
                out[name][...] = matrices[n][kind]
            out["conv_w"][...] = convs[kind]
            out["rnn_conv_w"][...] = rconvs[kind]

    outs = pl.pallas_call(
        body, name="small_grads_step", in_specs=[VMEM] * (3 + len(params)), out_specs=[VMEM] * (1 + 4 * len(names)),
        out_shape=[jax.ShapeDtypeStruct((1, 1), F32)] + [jax.ShapeDtypeStruct(sh, F32) for sh in shapes] * 4,
        scratch_shapes=[pltpu.VMEM((PK_ROWS, cols), F32)] * 4,
        compiler_params=_params(),
    )(vec_packs, mat_packs, mix_g_blocks, *params)
    return outs[0], [dict(zip(names, outs[1 + k * len(names):1 + (k + 1) * len(names)])) for k in range(4)]


def _to_block_diag(w):
    w4 = w.reshape(N_BD, 4, 64, 64)
    return jnp.concatenate([jnp.pad(w4[:, q], ((0, 0), (0, 0), (64 * q, 64 * (3 - q)))) for q in range(4)], axis=1)


_NAMES = ['norm_mix_g', 'w_in', 'conv_w', 'rnn_conv_w', 'rnn_conv_b', 'w_a', 'b_a', 'w_x', 'b_x', 'lru_lambda',
          'g_norm_conv', 'g_norm_rnn', 'w_out', 'norm_mlp_g', 'w_mlp_in', 'w_mlp_out', 'final_norm_g']


def kernel(x, norm_mix_g, w_in, conv_w, rnn_conv_w, rnn_conv_b, w_a, b_a, w_x, b_x, lru_lambda, g_norm_conv, g_norm_rnn, w_out, norm_mlp_g, w_mlp_in, w_mlp_out, final_norm_g, loss_target, m_norm_mix_g, m_w_in, m_conv_w, m_rnn_conv_w, m_rnn_conv_b, m_w_a, m_b_a, m_w_x, m_b_x, m_lru_lambda, m_g_norm_conv, m_g_norm_rnn, m_w_out, m_norm_mlp_g, m_w_mlp_in, m_w_mlp_out, m_final_norm_g, v_norm_mix_g, v_w_in, v_conv_w, v_rnn_conv_w, v_rnn_conv_b, v_w_a, v_b_a, v_w_x, v_b_x, v_lru_lambda, v_g_norm_conv, v_g_norm_rnn, v_w_out, v_norm_mlp_g, v_w_mlp_in, v_w_mlp_out, v_final_norm_g):
    args = dict(locals())
    p = {}
    for n in _NAMES:
        for pre in ("", "m_", "v_"):
            a = args[pre + n]
            p[pre + n] = a[0] if a.ndim >= 3 else a
    xs = x[0]
    target = loss_target[0]
    core_chip = jnp.stack([lax.axis_index("c"), 2 * lax.axis_index("x") + lax.axis_index("y")]).astype(jnp.int32)

    w_in_g, w_out_g, w1_g, w2_g, conv_full, rconv_full = _gather_first(
        p["w_in"], p["w_out"], p["w_mlp_in"], p["w_mlp_out"], p["conv_w"], p["rnn_conv_w"])
    wa_bd = _to_block_diag(p["w_a"]).astype(BF16)
    wx_bd = _to_block_diag(p["w_x"]).astype(BF16)
    gf = p["final_norm_g"].reshape(1, -1)
    lru = (wa_bd, p["b_a"], wx_bd, p["b_x"], p["lru_lambda"], p["g_norm_conv"], p["g_norm_rnn"])

    (u, h1b, xr, hs, c3, yb, gates), (w_out_g, w1_g, w2_g) = _fwd_mix(
        xs, p["norm_mix_g"], w_in_g, conv_full, rconv_full, p["rnn_conv_b"], *lru, (w_out_g, w1_g, w2_g))
    zb, dpb, h2b, dx3b, dx2, dx2b, dy, st_mlp = _mlp_fwd_bwd(
        xs, yb, w_out_g.reshape(-1, D_MODEL), w1_g, w2_g.reshape(-1, D_MODEL), p["norm_mlp_g"], gf, target)

    part_out = _wgrad(yb, dx2b, "out", core_chip)
    *part_1, arrived_out = _wgrad(h2b, dpb, "mlp_in", core_chip, parts=(part_out[1],))
    part_2 = _wgrad(zb, dx3b, "mlp_out", core_chip)
    (dub, vec_pack, mat_pack), (arrived_1, arrived_2) = _mix_bwd(
        dy, u, xr, hs, c3, gates, conv_full, rconv_full, wa_bd, wx_bd, p["lru_lambda"], p["g_norm_conv"], p["g_norm_rnn"],
        st_mlp, (part_1[1], part_2[1]))
    arrived_mlp = (arrived_out, arrived_1, arrived_2)
    *part_in, vec_packs, mat_packs = _wgrad(h1b, dub, "in", core_chip, packs=(vec_pack, mat_pack))
    early = (("w_out", "out", part_out, arrived_mlp[0]), ("w_mlp_in", "mlp_in", part_1, arrived_mlp[1]),
             ("w_mlp_out", "mlp_out", part_2, arrived_mlp[2]))
    (grad_x, st_in), arrived_in, joined = _in_bwd(
        dub, w_in_g, xs, dx2, p["norm_mix_g"], (part_in[1],),
        [(tag, p[n].shape, part[0], arrived) for n, tag, part, arrived in early], core_chip)
    g_in, mix_g_blocks = _join("in", p["w_in"].shape, part_in[0], arrived_in[0], core_chip, st_in)
    big = {}
    for n, tag, g in [(n, tag, g) for (n, tag, _, _), g in zip(early, joined)] + [("w_in", "in", g_in)]:
        big[n] = _adamw_big(p[n], g, p["m_" + n], p["v_" + n], "adamw_" + tag)

    loss, outs = _small_step(vec_packs, mat_packs, mix_g_blocks, p)
    for kind, o in enumerate(outs):
        o["final_norm_g"] = o["final_norm_g"].reshape(-1)
        for n in SMALL_MATRICES + ("conv_w", "rnn_conv_w"):
            o[n] = o[n][None]
        for n in ("w_in", "w_out", "w_mlp_in", "w_mlp_out"):
            o[n] = big[n][kind][None]
    loss = loss.reshape(())
    return (loss, grad_x[None], *[o[n] for o in outs for n in _NAMES])
```

```python
import functools
import math

import jax
import jax.numpy as jnp
from jax import lax
from jax.experimental import pallas as pl
from jax.experimental.pallas import tpu as pltpu

F32 = jnp.float32
BF16 = jnp.bfloat16
MESH = pl.DeviceIdType.MESH
ANY = pl.BlockSpec(memory_space=pl.ANY)
VMEM = pl.BlockSpec(memory_space=pltpu.VMEM)

EPS = 1e-6
LRU_C = 8.0
D_MODEL = 1024
CONV_W = 512
LRU_W = 1024
IN_COLS = 3 * CONV_W + 2 * LRU_W
IN_SHARD = IN_COLS // 4
N_CHIPS = 4
N_DEVICES = 8
BD = 256
N_BD = LRU_W // BD

ADAM_LR = 0.001
ADAM_B1 = 0.9
ADAM_B2 = 0.999
ADAM_EPS = 1e-08
ADAM_WD = 0.01
ADAM_STEP = 10
ADAM_BC1 = 1.0 - ADAM_B1 ** ADAM_STEP
ADAM_BC2 = 1.0 - ADAM_B2 ** ADAM_STEP

TILE_ROWS, LANES = 8, 128
TOKEN_TILE = 256
MATMUL_TOKEN_TILE = 512
ADAMW_ROWS = 256
VMEM_LIMIT = 56 * 1024 * 1024

PK_G_NORM_RNN, PK_RCONV_B, PK_B_A, PK_B_X, PK_LAMBDA, PK_CONV_W = 0, 1, 2, 3, 4, 5
PK_RCONV_W, PK_G_NORM_CONV = 8, 12
PK_MIX_ROWS = 16
PK_FINAL_G, PK_MLP_G, PK_LOSS = 16, 17, 18
PK_MLP_ROWS = 8
PK_MIX_G = 24
PK_ROWS = 32
N_HEADS, HEAD_DIM = 16, 64


def _params(**kw):
    return pltpu.CompilerParams(vmem_limit_bytes=VMEM_LIMIT, **kw)


def _position():
    x, y, c = lax.axis_index("x"), lax.axis_index("y"), lax.axis_index("c")
    return x, y, c


def _sigmoid(v):
    return 1.0 / (1.0 + jnp.exp(-v))


def _one_minus_square(log_a, a):
    v = 2.0 * log_a
    series = -v * (1.0 + v * (0.5 + v * (1.0 / 6.0)))
    return jnp.where(v > -0.01, series, 1.0 - a * a)


_GELU_C = math.sqrt(2.0 / math.pi)
_GELU_K = 0.044715


def _gelu_and_grad(g):
    th = jnp.tanh(_GELU_C * (g + _GELU_K * g * g * g))
    gelu = 0.5 * g * (1.0 + th)
    dgelu = 0.5 * (1.0 + th) + 0.5 * g * (1.0 - th * th) * (_GELU_C * (1.0 + 3.0 * _GELU_K * g * g))
    return gelu, dgelu


def _rows(shape):
    return lax.broadcasted_iota(jnp.int32, shape, 0)


def _shift_down(v, k, prev8):
    rolled = pltpu.roll(v, k, 0)
    halo = pltpu.roll(prev8, k, 0)
    head = jnp.where(_rows(halo.shape) < k, halo, rolled[:TILE_ROWS])
    return jnp.concatenate([head, rolled[TILE_ROWS:]], axis=0)


def _shift_up(v, k, next8):
    n = v.shape[0]
    rolled = pltpu.roll(v, n - k, 0)
    halo = pltpu.roll(next8, TILE_ROWS - k, 0)
    tail = jnp.where(_rows(halo.shape) >= TILE_ROWS - k, halo, rolled[n - TILE_ROWS:])
    return jnp.concatenate([rolled[: n - TILE_ROWS], tail], axis=0)


def _scan_rows(a, b, carry, reverse=False):
    n, w = a.shape
    groups = n // TILE_ROWS
    a3 = a.reshape(groups, TILE_ROWS, w)
    b3 = b.reshape(groups, TILE_ROWS, w)
    sub = lax.broadcasted_iota(jnp.int32, a3.shape, 1)
    s = 1
    while s < TILE_ROWS:
        shift = TILE_ROWS - s if reverse else s
        keep = (sub < TILE_ROWS - s) if reverse else (sub >= s)
        b3 = b3 + jnp.where(keep, a3 * pltpu.roll(b3, shift, 1), 0.0)
        a3 = a3 * jnp.where(keep, pltpu.roll(a3, shift, 1), 1.0)
        s *= 2
    out = [None] * groups
    edge = 0 if reverse else TILE_ROWS - 1
    for g in (range(groups - 1, -1, -1) if reverse else range(groups)):
        out[g] = b3[g] + a3[g] * carry
        carry = out[g][edge:edge + 1]
    return jnp.concatenate(out, axis=0)


def _softplus_neg(lam):
    e = jnp.exp(-jnp.abs(lam))
    log1p_e = jnp.where(e < 1e-2, e * (1.0 - e * (0.5 - e * (1.0 / 3.0 - e * 0.25))), jnp.log(1.0 + e))
    sp = jnp.maximum(-lam, 0.0) + log1p_e
    dsp = -_sigmoid(-lam)
    return sp, dsp


def _block_diag_dot(vb, w_ref):
    return jnp.concatenate(
        [jnp.dot(vb[:, j * BD:(j + 1) * BD], w_ref[j], preferred_element_type=F32) for j in range(N_BD)], axis=1)


def _block_diag_dot_t(vb, w_ref):
    return jnp.concatenate(
        [lax.dot_general(vb[:, j * BD:(j + 1) * BD], w_ref[j], (((1,), (1,)), ((), ())), preferred_element_type=F32)
         for j in range(N_BD)], axis=1)


def _dot_nt(a, b):
    return lax.dot_general(a, b, (((1,), (1,)), ((), ())), preferred_element_type=F32)


def _dot_tn(a, b):
    return lax.dot_general(a, b, (((0,), (0,)), ((), ())), preferred_element_type=F32)


def _lru_gates(xr, wa_ref, ba, wx_ref, bx, sp):
    xrb = xr.astype(BF16)
    r = _sigmoid(_block_diag_dot(xrb, wa_ref) + ba)
    ig = _sigmoid(_block_diag_dot(xrb, wx_ref) + bx)
    log_a = (-LRU_C) * r * sp
    a = jnp.exp(log_a)
    mult = jnp.sqrt(_one_minus_square(log_a, a))
    return r, ig, a, mult


def _colsum(v):
    return jnp.sum(v, axis=0, keepdims=True)


N_FWD_OUT = 7


def _fwd_mix(x, g1, w_in_g, conv_w, rconv_w, rconv_b, wa_bd, b_a, wx_bd, b_x, lam, g_nc, g_nr, later):
    t, d = x.shape
    tm = TOKEN_TILE
    nt = t // tm
    nl = len(later)
    assert nl == 3
    pass_on_at = [nt * f // 16 for f in (3, 5, 9)]
    neighbours_at = [nt * f // 16 for f in (10, 11, 12)]
    diagonal_at = [nt * f // 16 for f in (13, 14, 14)]

    def body(x_ref, g1_ref, win_ref, cw_ref, rw_ref, rb_ref, wa_ref, ba_ref, wx_ref, bx_ref, lam_ref, gnc_ref, gnr_ref,
             *rest):
        later_in, outs, rest = rest[:nl], rest[nl:nl + N_FWD_OUT], rest[nl + N_FWD_OUT:]
        u_ref, h1_ref, xr_ref, hs_ref, c3_ref, y_ref, gates_ref = outs
        later_out, (cv_prev, xin_prev, h_prev, send_sems, recv_sems) = rest[:nl], rest[nl:]
        del later_in
        step = pl.program_id(0)
        plan = _ShardGather(later_out, send_sems, recv_sems)

        @pl.when(step == 0)
        def _():
            cv_prev[...] = jnp.zeros_like(cv_prev)
            xin_prev[...] = jnp.zeros_like(xin_prev)
            h_prev[...] = jnp.zeros_like(h_prev)
            for w in range(nl):
                plan.start_direct(w)

        for w in range(nl):
            @pl.when(step == pass_on_at[w])
            def _(w=w):
                plan.start_pass_on(w)

            @pl.when(step == neighbours_at[w])
            def _(w=w):
                plan.start_hand_over(w, diagonal=False)

            @pl.when(step == diagonal_at[w])
            def _(w=w):
                plan.start_hand_over(w, diagonal=True)

        xv = x_ref[...]
        rstd = lax.rsqrt(jnp.mean(xv * xv, axis=-1, keepdims=True) + EPS)
        h1b = ((xv * rstd) * g1_ref[...]).astype(BF16)
        h1_ref[...] = h1b
        for j in range(N_CHIPS):
            u_ref[:, j * IN_SHARD:(j + 1) * IN_SHARD] = jnp.dot(h1b, win_ref[j], preferred_element_type=F32)
        gate_b = u_ref[:, 0:CONV_W]
        cv = u_ref[:, CONV_W:2 * CONV_W] * u_ref[:, 2 * CONV_W:3 * CONV_W]
        x_r = u_ref[:, 3 * CONV_W:3 * CONV_W + LRU_W]
        g = u_ref[:, 3 * CONV_W + LRU_W:]

        cw = cw_ref[...]
        cvp = cv_prev[...]
        conv3 = cw[0:1] * _shift_down(cv, 2, cvp) + cw[1:2] * _shift_down(cv, 1, cvp) + cw[2:3] * cv
        cv_prev[...] = cv[tm - TILE_ROWS:]
        c3_ref[...] = conv3
        y_conv = gate_b * conv3

        rw = rw_ref[...]
        xp = xin_prev[...]
        xr = (rw[0:1] * _shift_down(x_r, 3, xp) + rw[1:2] * _shift_down(x_r, 2, xp)
              + rw[2:3] * _shift_down(x_r, 1, xp) + rw[3:4] * x_r) + rb_ref[...]
        xin_prev[...] = x_r[tm - TILE_ROWS:]
        xr_ref[...] = xr
        sp, _ = _softplus_neg(lam_ref[...])
        r, ig, a, mult = _lru_gates(xr, wa_ref, ba_ref[...], wx_ref, bx_ref[...], sp)
        for n, gate in enumerate((r, ig, a, mult)):
            gates_ref[:, n * LRU_W:(n + 1) * LRU_W] = gate
        h = _scan_rows(a, mult * (ig * xr), h_prev[...])
        h_prev[...] = h[tm - 1:tm]
        hs_ref[...] = h
        gelu, _ = _gelu_and_grad(g)
        y_rnn = h * gelu

        na = y_conv * lax.rsqrt(jnp.mean(y_conv * y_conv, axis=-1, keepdims=True) + EPS) * gnc_ref[...]
        nb = y_rnn * lax.rsqrt(jnp.mean(y_rnn * y_rnn, axis=-1, keepdims=True) + EPS) * gnr_ref[...]
        y_ref[:, :CONV_W] = na.astype(BF16)
        y_ref[:, CONV_W:] = nb.astype(BF16)

        @pl.when(step == nt - 1)
        def _():
            for w in range(nl):
                plan.finish(w)

    def full(a):
        nd = a.ndim
        return pl.BlockSpec(a.shape, lambda i: (0,) * nd)

    def tok(cols):
        return pl.BlockSpec((tm, cols), lambda i: (i, 0))

    def act(cols, dtype=F32):
        return jax.ShapeDtypeStruct((t, cols), dtype)

    smalls = (g1, w_in_g, conv_w, rconv_w, rconv_b, wa_bd, b_a, wx_bd, b_x, lam, g_nc, g_nr)
    n_in = 1 + len(smalls)
    outs = pl.pallas_call(
        body, name="fwd_mix", grid=(nt,),
        in_specs=[tok(d)] + [full(a) for a in smalls] + [ANY] * nl,
        out_specs=[tok(IN_COLS), tok(d), tok(LRU_W), tok(LRU_W), tok(CONV_W), tok(CONV_W + LRU_W)]
        + [tok(4 * LRU_W)] + [ANY] * nl,
        out_shape=[act(IN_COLS), act(d, BF16), act(LRU_W), act(LRU_W), act(CONV_W), act(CONV_W + LRU_W, BF16)]
        + [act(4 * LRU_W)] + [jax.ShapeDtypeStruct(a.shape, a.dtype) for a in later],
        input_output_aliases={n_in + w: N_FWD_OUT + w for w in range(nl)},
        scratch_shapes=[pltpu.VMEM((TILE_ROWS, CONV_W), F32), pltpu.VMEM((TILE_ROWS, LRU_W), F32),
                        pltpu.VMEM((1, LRU_W), F32), pltpu.SemaphoreType.DMA((nl, _ShardGather.PAIRS)),
                        pltpu.SemaphoreType.DMA((nl, _ShardGather.PAIRS))],
        compiler_params=_params(dimension_semantics=("arbitrary",)),
    )(x, *smalls, *later)
    return outs[:N_FWD_OUT], outs[N_FWD_OUT:]


def _mlp_fwd_bwd(x, yb, w_out_g, w1_g, w2_g, g2, gf, target):
    t, d = x.shape
    tm = TOKEN_TILE
    ff = w2_g.shape[0]
    mix = w_out_g.shape[0]
    ffs = ff // N_CHIPS

    def body(x_ref, y_ref, g2_ref, gf_ref, tgt_ref, wout_hbm, w1_hbm, w2_hbm,
             z_ref, dp_ref, h2_ref, dx3b_ref, dx2_ref, dx2b_ref, dy_ref, st_ref, wout, w1, w2, p_ref):
        @pl.when(pl.program_id(0) == 0)
        def _():
            pltpu.sync_copy(wout_hbm, wout)
            pltpu.sync_copy(w1_hbm, w1)
            pltpu.sync_copy(w2_hbm, w2)
            st_ref[...] = jnp.zeros_like(st_ref)

        x2 = x_ref[...] + jnp.dot(y_ref[...], wout[...], preferred_element_type=F32)
        r2 = lax.rsqrt(jnp.mean(x2 * x2, axis=-1, keepdims=True) + EPS)
        xh2 = x2 * r2
        g2v = g2_ref[...]
        h2b = (xh2 * g2v).astype(BF16)
        h2_ref[...] = h2b
        for j in range(N_CHIPS):
            p_ref[:, j * ffs:(j + 1) * ffs] = jnp.dot(h2b, w1[j], preferred_element_type=F32)
        rp = jnp.maximum(p_ref[...], 0.0)
        zb = (rp * rp).astype(BF16)
        z_ref[...] = zb
        x3 = x2 + jnp.dot(zb, w2[...], preferred_element_type=F32)
        r3 = lax.rsqrt(jnp.mean(x3 * x3, axis=-1, keepdims=True) + EPS)
        xh3 = x3 * r3
        gfv = gf_ref[...]
        err = xh3 * gfv - tgt_ref[...]
        loss = (0.5 / d) * jnp.sum(err * err)
        dout = err * (1.0 / d)
        st_ref[PK_FINAL_G - PK_MIX_ROWS:PK_FINAL_G - PK_MIX_ROWS + 1, :] += _colsum(dout * xh3)
        st_ref[PK_LOSS - PK_MIX_ROWS:PK_LOSS - PK_MIX_ROWS + 1, :] += jnp.zeros((1, d), F32) + loss
        dxh3 = dout * gfv
        dx3 = r3 * (dxh3 - xh3 * jnp.mean(dxh3 * xh3, axis=-1, keepdims=True))
        dx3b = dx3.astype(BF16)
        dx3b_ref[...] = dx3b
        dpb = (_dot_nt(dx3b, w2[...]) * (2.0 * rp)).astype(BF16)
        dp_ref[...] = dpb
        dh2 = _dot_nt(dpb[:, 0:ffs], w1[0])
        for j in range(1, N_CHIPS):
            dh2 = dh2 + _dot_nt(dpb[:, j * ffs:(j + 1) * ffs], w1[j])
        st_ref[PK_MLP_G - PK_MIX_ROWS:PK_MLP_G - PK_MIX_ROWS + 1, :] += _colsum(dh2 * xh2)
        dxh2 = dh2 * g2v
        dx2 = dx3 + r2 * (dxh2 - xh2 * jnp.mean(dxh2 * xh2, axis=-1, keepdims=True))
        dx2_ref[...] = dx2
        dx2b = dx2.astype(BF16)
        dx2b_ref[...] = dx2b
        dy_ref[...] = _dot_nt(dx2b, wout[...])

    def tok(cols):
        return pl.BlockSpec((tm, cols), lambda i: (i, 0))

    def row(cols):
        return pl.BlockSpec((1, cols), lambda i: (0, 0))

    return pl.pallas_call(
        body, name="mlp_fwd_bwd", grid=(t // tm,),
        in_specs=[tok(d), tok(mix), row(d), row(d), tok(d), ANY, ANY, ANY],
        out_specs=[tok(ff), tok(ff), tok(d), tok(d), tok(d), tok(d), tok(mix),
                   pl.BlockSpec((PK_MLP_ROWS, d), lambda i: (0, 0))],
        out_shape=[jax.ShapeDtypeStruct((t, ff), BF16), jax.ShapeDtypeStruct((t, ff), BF16),
                   jax.ShapeDtypeStruct((t, d), BF16), jax.ShapeDtypeStruct((t, d), BF16),
                   jax.ShapeDtypeStruct((t, d), F32), jax.ShapeDtypeStruct((t, d), BF16),
                   jax.ShapeDtypeStruct((t, mix), F32), jax.ShapeDtypeStruct((PK_MLP_ROWS, d), F32)],
        scratch_shapes=[pltpu.VMEM(w_out_g.shape, BF16), pltpu.VMEM(w1_g.shape, BF16), pltpu.VMEM(w2_g.shape, BF16),
                        pltpu.VMEM((tm, ff), F32)],
        compiler_params=_params(dimension_semantics=("arbitrary",)),
    )(x, yb, g2, gf, target, w_out_g, w1_g, w2_g)


def _mix_bwd(dy, u, xr_all, hs_all, c3_all, gates, conv_w, rconv_w, wa_bd, wx_bd, lam, g_nc, g_nr, st_mlp, parts):
    t = dy.shape[0]
    tm = TOKEN_TILE
    nt = t // tm
    hb = tm // TILE_ROWS
    npart = len(parts)

    def body(dy_ref, u_ref, uh_ref, xr_ref, hs_ref, hh_ref, c3_ref, gates_ref,
             cw_ref, rw_ref, wa_ref, wx_ref, lam_ref, gnc_ref, gnr_ref, stm_ref, *rest):
        part_refs, (du_ref, st_ref, heads_ref), rest = rest[:npart], rest[npart:npart + 3], rest[npart + 3:]
        arrived_refs, (dc_next, a_next, gs_next, dxr_next, dwa_ref, dwx_ref, send_sems, recv_sems) = rest[:npart], rest[npart:]
        exchange = _PartialExchange(part_refs, arrived_refs, send_sems, recv_sems)
        i = pl.program_id(0)

        @pl.when(i == 0)
        def _():
            exchange.start()
            dc_next[...] = jnp.zeros_like(dc_next)
            a_next[...] = jnp.zeros_like(a_next)
            gs_next[...] = jnp.zeros_like(gs_next)
            dxr_next[...] = jnp.zeros_like(dxr_next)
            st_ref[0:PK_MIX_ROWS, :] = jnp.zeros((PK_MIX_ROWS, LRU_W), F32)
            st_ref[PK_MIX_ROWS:, :] = stm_ref[...]
            dwa_ref[...] = jnp.zeros_like(dwa_ref)
            dwx_ref[...] = jnp.zeros_like(dwx_ref)

        first_tile = i == nt - 1
        gate_b = u_ref[:, 0:CONV_W]
        gate_c = u_ref[:, CONV_W:2 * CONV_W]
        v = u_ref[:, 2 * CONV_W:3 * CONV_W]
        x_r = u_ref[:, 3 * CONV_W:3 * CONV_W + LRU_W]
        g = u_ref[:, 3 * CONV_W + LRU_W:]
        cv = gate_c * v
        cv_prev = jnp.where(first_tile, 0.0, uh_ref[:, CONV_W:2 * CONV_W] * uh_ref[:, 2 * CONV_W:3 * CONV_W])
        xin_prev = jnp.where(first_tile, 0.0, uh_ref[:, 3 * CONV_W:3 * CONV_W + LRU_W])
        hs_prev = jnp.where(first_tile, 0.0, hh_ref[...])

        def acc(first_row, val, width=LRU_W, row=0):
            r0 = first_row + row
            st_ref[r0:r0 + 1, 0:width] += val

        conv3 = c3_ref[...]
        y_conv = gate_b * conv3
        ra = lax.rsqrt(jnp.mean(y_conv * y_conv, axis=-1, keepdims=True) + EPS)
        xha = y_conv * ra
        dna = dy_ref[:, :CONV_W]
        acc(PK_G_NORM_CONV, _colsum(dna * xha), CONV_W)
        dxha = dna * gnc_ref[...]
        dy_conv = ra * (dxha - xha * jnp.mean(dxha * xha, axis=-1, keepdims=True))
        du_ref[:, 0:CONV_W] = (dy_conv * conv3).astype(BF16)
        dc = dy_conv * gate_b
        cw = cw_ref[...]
        dcn = dc_next[...]
        dcv = cw[2:3] * dc + cw[1:2] * _shift_up(dc, 1, dcn) + cw[0:1] * _shift_up(dc, 2, dcn)
        dc_next[...] = dc[:TILE_ROWS]
        acc(PK_CONV_W, _colsum(dc * _shift_down(cv, 2, cv_prev)), CONV_W, 0)
        acc(PK_CONV_W, _colsum(dc * _shift_down(cv, 1, cv_prev)), CONV_W, 1)
        acc(PK_CONV_W, _colsum(dc * cv), CONV_W, 2)
        du_ref[:, CONV_W:2 * CONV_W] = (dcv * v).astype(BF16)
        du_ref[:, 2 * CONV_W:3 * CONV_W] = (dcv * gate_c).astype(BF16)

        hs = hs_ref[...]
        gelu, dgelu = _gelu_and_grad(g)
        y_rnn = hs * gelu
        rb = lax.rsqrt(jnp.mean(y_rnn * y_rnn, axis=-1, keepdims=True) + EPS)
        xhb = y_rnn * rb
        dnb = dy_ref[:, CONV_W:]
        acc(PK_G_NORM_RNN, _colsum(dnb * xhb))
        dxhb = dnb * gnr_ref[...]
        dy_rnn = rb * (dxhb - xhb * jnp.mean(dxhb * xhb, axis=-1, keepdims=True))
        du_ref[:, 3 * CONV_W + LRU_W:] = (dy_rnn * hs * dgelu).astype(BF16)
        dh = dy_rnn * gelu

        xr = xr_ref[...]
        xrb = xr.astype(BF16)
        sp, dsp = _softplus_neg(lam_ref[...])
        r, ig, a, mult = [gates_ref[:, n * LRU_W:(n + 1) * LRU_W] for n in range(4)]
        a_up = _shift_up(a, 1, a_next[...])
        a_next[...] = a[:TILE_ROWS]
        gs = _scan_rows(a_up, dh, gs_next[0:1, :], reverse=True)
        gs_next[...] = gs[:TILE_ROWS]
        da = gs * _shift_down(hs, 1, hs_prev)
        gx = gs * xr
        di = gx * mult
        dmult = gx * ig
        dxr = gs * (mult * ig)
        dlog_a = da * a - dmult * ((a * a) / mult)
        acc(PK_LAMBDA, _colsum(dlog_a * r) * ((-LRU_C) * dsp))
        dpa = (dlog_a * ((-LRU_C) * sp)) * (r * (1.0 - r))
        dpx = di * (ig * (1.0 - ig))
        acc(PK_B_A, _colsum(dpa))
        acc(PK_B_X, _colsum(dpx))
        dpab = dpa.astype(BF16)
        dpxb = dpx.astype(BF16)
        dxr = dxr + _block_diag_dot_t(dpab, wa_ref) + _block_diag_dot_t(dpxb, wx_ref)
        for j in range(N_BD):
            cols = slice(j * BD, (j + 1) * BD)
            dwa_ref[j] += _dot_tn(xrb[:, cols], dpab[:, cols])
            dwx_ref[j] += _dot_tn(xrb[:, cols], dpxb[:, cols])

        acc(PK_RCONV_B, _colsum(dxr))
        rw = rw_ref[...]
        dxn = dxr_next[...]
        dx_r = (rw[3:4] * dxr + rw[2:3] * _shift_up(dxr, 1, dxn) + rw[1:2] * _shift_up(dxr, 2, dxn)
                + rw[0:1] * _shift_up(dxr, 3, dxn))
        dxr_next[...] = dxr[:TILE_ROWS]
        for k in range(3):
            acc(PK_RCONV_W, _colsum(dxr * _shift_down(x_r, 3 - k, xin_prev)), LRU_W, k)
        acc(PK_RCONV_W, _colsum(dxr * x_r), LRU_W, 3)
        du_ref[:, 3 * CONV_W:3 * CONV_W + LRU_W] = dx_r.astype(BF16)

        @pl.when(i == nt - 1)
        def _():
            for n, d_ref in enumerate((dwa_ref, dwx_ref)):
                for b in range(N_BD):
                    for q in range(BD // HEAD_DIM):
                        lane0 = q * HEAD_DIM // LANES * LANES
                        wide = d_ref[b, q * HEAD_DIM:(q + 1) * HEAD_DIM, lane0:lane0 + LANES]
                        if q * HEAD_DIM != lane0:
                            wide = pltpu.roll(wide, LANES - (q * HEAD_DIM - lane0), axis=1)
                        heads_ref[n, b * (BD // HEAD_DIM) + q] = wide[:, 0:HEAD_DIM].astype(BF16)
            exchange.wait()

    def full(a):
        nd = a.ndim
        return pl.BlockSpec(a.shape, lambda i: (0,) * nd)

    def tok(cols):
        return pl.BlockSpec((tm, cols), lambda i: (nt - 1 - i, 0))

    def halo(cols):
        return pl.BlockSpec((TILE_ROWS, cols), lambda i: (jnp.maximum((nt - 1 - i) * hb - 1, 0), 0))

    smalls = (conv_w, rconv_w, wa_bd, wx_bd, lam, g_nc, g_nr, st_mlp)
    st_rows = PK_MIX_ROWS + st_mlp.shape[0]
    heads = (2, N_HEADS, HEAD_DIM, HEAD_DIM)
    outs = pl.pallas_call(
        body, name="mix_bwd", grid=(nt,),
        in_specs=[tok(CONV_W + LRU_W), tok(IN_COLS), halo(IN_COLS), tok(LRU_W), tok(LRU_W), halo(LRU_W), tok(CONV_W)]
        + [tok(4 * LRU_W)] + [full(a) for a in smalls] + [ANY] * npart,
        out_specs=[tok(IN_COLS), pl.BlockSpec((st_rows, LRU_W), lambda i: (0, 0)),
                   pl.BlockSpec(heads, lambda i: (0, 0, 0, 0))]
        + [ANY] * npart,
        out_shape=[jax.ShapeDtypeStruct((t, IN_COLS), BF16), jax.ShapeDtypeStruct((st_rows, LRU_W), F32),
                   jax.ShapeDtypeStruct(heads, BF16)]
        + [jax.ShapeDtypeStruct(a.shape, a.dtype) for a in parts],
        scratch_shapes=[pltpu.VMEM((TILE_ROWS, CONV_W), F32), pltpu.VMEM((TILE_ROWS, LRU_W), F32),
                        pltpu.VMEM((TILE_ROWS, LRU_W), F32), pltpu.VMEM((TILE_ROWS, LRU_W), F32),
                        pltpu.VMEM((N_BD, BD, BD), F32), pltpu.VMEM((N_BD, BD, BD), F32),
                        pltpu.SemaphoreType.DMA((npart, 3)), pltpu.SemaphoreType.DMA((npart, 3))],
        compiler_params=_params(dimension_semantics=("arbitrary",)),
    )(dy, u, u, xr_all, hs_all, hs_all, c3_all, gates, *smalls, *parts)
    return outs[:3], outs[3:]


def _in_bwd(dub, w_in_g, x, dx2, g1, parts, joins, core_chip):
    t, d = x.shape
    tm = min(t, MATMUL_TOKEN_TILE)
    nt = t // tm
    npart = len(parts)
    nj = len(joins)
    geometry = []
    for tag, shape, _, _ in joins:
        pr, pc = WGRAD_GEOMETRY[tag][:2]
        every = 1 if pr % (nt * 16) == 0 else 2
        geometry.append((pr, pc, pr * every // nt, every, shape[1] == pc))

    def body(cc_ref, du_ref, win_hbm, x_ref, dx2_ref, g1_ref, *rest):
        sums, rest = [rest[4 * w:4 * w + 4] for w in range(nj)], rest[4 * nj:]
        part_refs, (gx_ref, st_ref), rest = rest[:npart], rest[npart:npart + 2], rest[npart + 2:]
        arrived_refs, joined, win_ref, rest = rest[:npart], rest[npart:npart + nj], rest[npart + nj], rest[npart + nj + 1:]
        stages, (send_sems, recv_sems, j_local, j_send, j_recv) = rest[:nj], rest[nj:]
        exchange = _PartialExchange(part_refs, arrived_refs, send_sems, recv_sems)
        i = pl.program_id(0)
        c = cc_ref[0]

        def window(w, core, row0, rows):
            pr, pc, _, _, by_rows = geometry[w]
            if by_rows:
                return joined[w].at[pl.ds(core * pr + row0, rows), :]
            return joined[w].at[pl.ds(row0, rows), pl.ds(core * pc, pc)]

        def to_sibling(w, src, core, row0, rows):
            return pltpu.make_async_remote_copy(src_ref=src, dst_ref=window(w, core, row0, rows), send_sem=j_send.at[w],
                                                recv_sem=j_recv.at[w], device_id=_sibling(), device_id_type=MESH)

        @pl.when(i == 0)
        def _():
            exchange.start()
            pltpu.sync_copy(win_hbm, win_ref)
            st_ref[...] = jnp.zeros_like(st_ref)

        for w in range(nj):
            pr, pc, rb, every, _ = geometry[w]

            @pl.when(i % every == 0)
            def _(w=w, rb=rb, every=every):
                p_ref, r1_ref, r2_ref, r3_ref = sums[w]
                row0 = pl.multiple_of((i // every) * rb, rb)
                rows = stages[w].at[pl.ds(row0, rb), :]
                rows[...] = ((p_ref[0] + r1_ref[0].astype(F32)) + r2_ref[0].astype(F32)) + r3_ref[0].astype(F32)
                pltpu.make_async_copy(rows, window(w, c, row0, rb), j_local.at[w]).start()
                to_sibling(w, rows, c, row0, rb).start()

        dh1 = _dot_nt(du_ref[:, 0:IN_SHARD], win_ref[0])
        for j in range(1, N_CHIPS):
            dh1 = dh1 + _dot_nt(du_ref[:, j * IN_SHARD:(j + 1) * IN_SHARD], win_ref[j])
        xv = x_ref[...]
        rstd = lax.rsqrt(jnp.mean(xv * xv, axis=-1, keepdims=True) + EPS)
        xh = xv * rstd
        st_ref[0:1, :] += _colsum(dh1 * xh)
        dxh = dh1 * g1_ref[...]
        gx_ref[...] = dx2_ref[...] + rstd * (dxh - xh * jnp.mean(dxh * xh, axis=-1, keepdims=True))

        @pl.when(i == nt - 1)
        def _():
            exchange.wait()
            for w in range(nj):
                pr = geometry[w][0]
                pltpu.make_async_copy(stages[w], window(w, c, 0, pr), j_local.at[w]).wait()
                to_sibling(w, stages[w], 1 - c, 0, pr).wait()

    def tok(cols):
        return pl.BlockSpec((tm, cols), lambda i, cc: (i, 0))

    def partial(w, off):
        pr, pc, rb, every, _ = geometry[w]
        return pl.BlockSpec((1, rb, pc), lambda i, cc: ((cc[1] + off) % N_CHIPS, i // every, 0))

    sum_specs, sum_operands = [], []
    for w, (_, _, own, arrived) in enumerate(joins):
        sum_specs += [partial(w, off) for off in range(N_CHIPS)]
        sum_operands += [own, arrived, arrived, arrived]
    dma = pltpu.SemaphoreType.DMA
    outs = pl.pallas_call(
        body, name="in_bwd",
        grid_spec=pltpu.PrefetchScalarGridSpec(
            num_scalar_prefetch=1, grid=(nt,),
            in_specs=[tok(IN_COLS), ANY, tok(d), tok(d), pl.BlockSpec((1, d), lambda i, cc: (0, 0))] + sum_specs
            + [ANY] * npart,
            out_specs=[tok(d), pl.BlockSpec((TILE_ROWS, d), lambda i, cc: (0, 0))] + [ANY] * (npart + nj),
            scratch_shapes=[pltpu.VMEM(w_in_g.shape, w_in_g.dtype)] + [pltpu.VMEM((g[0], g[1]), F32) for g in geometry]
            + [dma((npart, 3)), dma((npart, 3)), dma((nj,)), dma((nj,)), dma((nj,))]),
        out_shape=[jax.ShapeDtypeStruct((t, d), F32), jax.ShapeDtypeStruct((TILE_ROWS, d), F32)]
        + [jax.ShapeDtypeStruct(a.shape, a.dtype) for a in parts]
        + [jax.ShapeDtypeStruct(shape, F32) for _, shape, _, _ in joins],
        compiler_params=_params(dimension_semantics=("arbitrary",)),
    )(core_chip, dub, w_in_g, x, dx2, g1, *sum_operands, *parts)
    return outs[:2], outs[2:2 + npart], outs[2 + npart:]


WGRAD_GEOMETRY = {
    "in": (512, IN_SHARD, lambda s, h: h, lambda s, h: s),
    "mlp_in": (512, D_MODEL, lambda s, h: h, lambda s, h: s),
    "mlp_out": (512, D_MODEL, lambda s, h: 2 * s + h, lambda s, h: 0),
    "out": (384, 512, lambda s, h: s, lambda s, h: h),
}
K_CHUNK = 512
TOKEN_STREAMS = 2


def _sibling():
    x, y, c = _position()
    return (x, y, 1 - c)


def _wgrad(a, b, tag, core_chip, packs=(), parts=()):
    t = a.shape[0]
    pr, pc, a_blk, b_blk = WGRAD_GEOMETRY[tag]
    ns = TOKEN_STREAMS
    ts = t // ns
    kc = min(K_CHUNK, ts)
    mine = N_CHIPS
    riding = len(packs)
    npart = len(parts)
    assert not (riding and npart)

    def body(cc_ref, *rest):
        a_refs, b_refs, rest = rest[:ns], rest[ns:2 * ns], rest[2 * ns:]
        if riding:
            pack_refs, (land_ref, p_ref, pb_ref), rest = rest[:riding], rest[riding:riding + 3], rest[riding + 3:]
            all_refs, (stage, rbuf, send_sems, recv_sems, rsem), g_sems = rest[:riding], rest[riding:riding + 5], rest[riding + 5:]
            gathers = [_PackGather(pack_refs[n], all_refs[n], *g_sems[3 * n:3 * n + 3]) for n in range(riding)]
        elif npart:
            part_refs, (land_ref, p_ref, pb_ref), rest = rest[:npart], rest[npart:npart + 3], rest[npart + 3:]
            arrived_refs, (stage, rbuf, send_sems, recv_sems, rsem, x_send, x_recv) = rest[:npart], rest[npart:]
            exchange = _PartialExchange(part_refs, arrived_refs, x_send, x_recv)
        else:
            land_ref, p_ref, pb_ref, stage, rbuf, send_sems, recv_sems, rsem = rest
        ph, s = pl.program_id(0), pl.program_id(1)
        if riding:
            @pl.when((ph == 0) & (s == 0))
            def _():
                for gather in gathers:
                    gather.start()

            @pl.when((ph == 1) & (s == N_CHIPS - 2))
            def _():
                for gather in gathers:
                    gather.hand_over()
        if npart:
            @pl.when((ph == 0) & (s == 0))
            def _():
                exchange.start()
        def push(k):
            return pltpu.make_async_remote_copy(src_ref=stage.at[k], dst_ref=land_ref.at[k], send_sem=send_sems.at[k],
                                                recv_sem=recv_sems.at[k], device_id=_sibling(), device_id_type=MESH)

        def landed():
            return pltpu.make_async_copy(land_ref.at[s], rbuf, rsem)

        @pl.when(ph == 1)
        def _():
            push(s).wait_recv()
            landed().start()

        slot = jnp.where(ph == 0, s, mine)
        acc = stage.at[slot]
        chunks = [(a_ref, b_ref, k) for a_ref, b_ref in zip(a_refs, b_refs) for k in range(0, ts, kc)]
        for n, (a_ref, b_ref, k) in enumerate(chunks):
            part = _dot_tn(a_ref[k:k + kc, :], b_ref[k:k + kc, :])
            if n == 0:
                acc[...] = part
            else:
                acc[...] += part

        @pl.when(ph == 0)
        def _():
            push(s).start()

        @pl.when(ph == 1)
        def _():
            landed().wait()
            p = stage[mine] + rbuf[...]
            p_ref[0] = p
            pb_ref[0] = p.astype(BF16)

        @pl.when((ph == 1) & (s == N_CHIPS - 1))
        def _():
            for k in range(N_CHIPS):
                push(k).wait_send()
            for gather in (gathers if riding else ()):
                gather.finish()
            if npart:
                exchange.wait()

    def half(ph, cc):
        return jnp.where(ph == 0, 1 - cc[0], cc[0])

    def out_slot(ph, s, cc):
        return (jnp.where(ph == 0, 0, s), 0, 0)

    piece = jax.ShapeDtypeStruct((N_CHIPS, pr, pc), F32)
    in_specs = [pl.BlockSpec((ts, pr), lambda ph, s, cc, n=n: (n, a_blk(s, half(ph, cc)))) for n in range(ns)]
    in_specs += [pl.BlockSpec((ts, pc), lambda ph, s, cc, n=n: (n, b_blk(s, half(ph, cc)))) for n in range(ns)]
    out_specs = [ANY, pl.BlockSpec((1, pr, pc), out_slot), pl.BlockSpec((1, pr, pc), out_slot)]
    out_shape = [piece, piece, jax.ShapeDtypeStruct((N_CHIPS, pr, pc), BF16)]
    scratch = [pltpu.VMEM((N_CHIPS + 1, pr, pc), F32), pltpu.VMEM((pr, pc), F32),
               pltpu.SemaphoreType.DMA((N_CHIPS,)), pltpu.SemaphoreType.DMA((N_CHIPS,)), pltpu.SemaphoreType.DMA]
    operands = [a] * ns + [b] * ns
    for pack in packs:
        in_specs.append(pl.BlockSpec(pack.shape, lambda ph, s, cc, nd=pack.ndim: (0,) * nd))
        out_specs.append(ANY)
        out_shape.append(jax.ShapeDtypeStruct((N_DEVICES,) + pack.shape, pack.dtype))
        operands.append(pack)
    for pack in packs:
        scratch += _PackGather.semaphores()
    if npart:
        in_specs += [ANY] * npart
        out_specs += [ANY] * npart
        out_shape += [jax.ShapeDtypeStruct(p.shape, p.dtype) for p in parts]
        scratch += [pltpu.SemaphoreType.DMA((npart, 3)), pltpu.SemaphoreType.DMA((npart, 3))]
        operands += list(parts)
    return pl.pallas_call(
        body, name="wgrad_" + tag,
        grid_spec=pltpu.PrefetchScalarGridSpec(
            num_scalar_prefetch=1, grid=(2, N_CHIPS), in_specs=in_specs, out_specs=out_specs, scratch_shapes=scratch),
        out_shape=out_shape,
        compiler_params=_params(dimension_semantics=("arbitrary", "arbitrary")),
    )(core_chip, *operands)[1:]


def _other_chips(x, y):
    return [(1 - x, y), (x, 1 - y), (1 - x, 1 - y)]


class _ShardGather:
    PAIRS = 9

    def __init__(self, outs, send_sems, recv_sems):
        self.outs, self.send_sems, self.recv_sems = outs, send_sems, recv_sems
        x, y, c = _position()
        self.c, self.j = c, 2 * x + y
        self.sibling = (x, y, 1 - c)
        self.chips = _other_chips(x, y)

    def _chip(self, k):
        px, py = self.chips[k]
        return 2 * px + py

    def _half(self, w, chip, which):
        hr = self.outs[w].shape[1] // 2
        return self.outs[w].at[chip, pl.ds(which * hr, hr), :]

    def _quarter(self, w, chip, q):
        qr = self.outs[w].shape[1] // 4
        return self.outs[w].at[chip, pl.ds(self.c * 2 * qr + q * qr, qr), :]

    def _copy(self, ref, w, pair, to, src=None):
        return pltpu.make_async_remote_copy(src_ref=ref if src is None else src, dst_ref=ref, send_sem=self.send_sems.at[w, pair],
                                            recv_sem=self.recv_sems.at[w, pair], device_id=to, device_id_type=MESH)

    def direct(self, w, k, q, src=None):
        return self._copy(self._quarter(w, self.j, q), w, 2 * k + q, (*self.chips[k], self.c), src)

    def direct_landed(self, w, k, q):
        return self._copy(self._quarter(w, self._chip(k), q), w, 2 * k + q, (*self.chips[k], self.c))

    def pass_on(self, w, q):
        return self._copy(self._quarter(w, self._chip(q), q), w, 4 + q, (*self.chips[1 - q], self.c))

    def passed_landed(self, w, q):
        return self._copy(self._quarter(w, self._chip(2), q), w, 4 + q, (*self.chips[1 - q], self.c))

    def hand_over(self, w, k):
        return self._copy(self._half(w, self._chip(k), self.c), w, 6 + k, self.sibling)

    def handed(self, w, k):
        return self._copy(self._half(w, self._chip(k), 1 - self.c), w, 6 + k, self.sibling)

    def start_direct(self, w, src_half=None):
        qr = self.outs[w].shape[1] // 4
        for k, q in ((0, 0), (1, 1), (0, 1), (1, 0)):
            self.direct(w, k, q, None if src_half is None else src_half.at[pl.ds(q * qr, qr), :]).start()

    def start_pass_on(self, w):
        for q in (0, 1):
            self.direct_landed(w, q, q).wait_recv()
            self.pass_on(w, q).start()

    def start_hand_over(self, w, diagonal):
        if diagonal:
            for q in (0, 1):
                self.passed_landed(w, q).wait_recv()
            self.hand_over(w, 2).start()
        else:
            for k in (0, 1):
                self.direct_landed(w, k, 1 - k).wait_recv()
                self.hand_over(w, k).start()

    def finish(self, w):
        for k in range(3):
            self.handed(w, k).wait_recv()
            self.hand_over(w, k).wait_send()
        for q in (0, 1):
            self.pass_on(w, q).wait_send()
            for k in (0, 1):
                self.direct(w, k, q).wait_send()


def _gather_first(w_in, w_out, w1, w2, conv_w, rconv_w):
    bigs = (w_in, w_out, w1, w2)
    convs = (conv_w, rconv_w)
    nb, nc = len(bigs), len(convs)

    def body(win_ref, wout_ref, w1_ref, w2_ref, cw_ref, rw_ref, gin, gout, g1, g2, gcw, grw, st_in, st_out, st_1, st_2,
             st_cw, st_rw, send_sems, recv_sems, sm_send, sm_recv, local_sems):
        srcs = (win_ref, wout_ref, w1_ref, w2_ref)
        stages = (st_in, st_out, st_1, st_2)
        outs = (gin, gout, g1, g2)
        conv_stages, conv_outs = (st_cw, st_rw), (gcw, grw)
        plan = _ShardGather(outs[:1], send_sems, recv_sems)
        j, c = plan.j, plan.c
        local = [pltpu.make_async_copy(stages[w], outs[w].at[j], local_sems.at[w]) for w in range(nb)]

        def columns(n, chip):
            width = convs[n].shape[1]
            return conv_outs[n].at[:, pl.ds(chip * width, width)]

        local += [pltpu.make_async_copy(conv_stages[n], columns(n, j), local_sems.at[nb + n]) for n in range(nc)]

        def small_copy(k, n, landed=False):
            px, py = plan.chips[k]
            return pltpu.make_async_remote_copy(
                src_ref=conv_stages[n], dst_ref=columns(n, 2 * px + py if landed else j), send_sem=sm_send.at[k, n],
                recv_sem=sm_recv.at[k, n], device_id=(px, py, c), device_id_type=MESH)

        hr = w_in.shape[0] // 2
        st_in[...] = win_ref[...].astype(BF16)
        plan.start_direct(0, st_in.at[pl.ds(c * hr, hr), :])
        for src, st in zip((cw_ref, rw_ref), conv_stages):
            st[...] = jnp.zeros_like(st)
            st[0:src.shape[0], :] = src[...]
        for k in range(3):
            for n in range(nc):
                small_copy(k, n).start()
        for src, st in zip(srcs[1:], stages[1:]):
            st[...] = src[...].astype(BF16)
        for cp in local:
            cp.start()
        plan.start_pass_on(0)
        plan.start_hand_over(0, diagonal=False)
        plan.start_hand_over(0, diagonal=True)
        for k in range(3):
            for n in range(nc):
                small_copy(k, n, landed=True).wait_recv()
                small_copy(k, n).wait_send()
        plan.finish(0)
        for cp in local:
            cp.wait()

    def gathered(a, dtype):
        return jax.ShapeDtypeStruct((N_CHIPS,) + a.shape, dtype)

    return pl.pallas_call(
        body, name="gather_first",
        in_specs=[VMEM] * (nb + nc), out_specs=[ANY] * (nb + nc),
        out_shape=[gathered(a, BF16) for a in bigs]
        + [jax.ShapeDtypeStruct((TILE_ROWS, N_CHIPS * a.shape[1]), F32) for a in convs],
        scratch_shapes=[pltpu.VMEM(a.shape, BF16) for a in bigs] + [pltpu.VMEM((TILE_ROWS, a.shape[1]), F32) for a in convs]
        + [pltpu.SemaphoreType.DMA((1, _ShardGather.PAIRS)), pltpu.SemaphoreType.DMA((1, _ShardGather.PAIRS)),
           pltpu.SemaphoreType.DMA((3, nc)), pltpu.SemaphoreType.DMA((3, nc)), pltpu.SemaphoreType.DMA((nb + nc,))],
        compiler_params=_params(),
    )(*bigs, *convs)


class _PartialExchange:
    def __init__(self, parts, arrived, send_sems, recv_sems):
        self.parts, self.arrived, self.send_sems, self.recv_sems = parts, arrived, send_sems, recv_sems
        x, y, c = _position()
        self.c, self.j = c, 2 * x + y
        self.chips = _other_chips(x, y)

    def _copy(self, w, k, slot):
        px, py = self.chips[k]
        return pltpu.make_async_remote_copy(
            src_ref=self.parts[w].at[2 * px + py], dst_ref=self.arrived[w].at[slot], send_sem=self.send_sems.at[w, k],
            recv_sem=self.recv_sems.at[w, k], device_id=(px, py, self.c), device_id_type=MESH)

    def start(self):
        for w in range(len(self.parts)):
            for k in range(3):
                self._copy(w, k, self.j).start()

    def wait(self):
        for w in range(len(self.parts)):
            for k in range(3):
                px, py = self.chips[k]
                self._copy(w, k, 2 * px + py).wait()


class _PackGather:
    def __init__(self, p_ref, all_ref, send_sems, recv_sems, local_sem):
        self.p_ref, self.all_ref, self.send_sems, self.recv_sems, self.local_sem = p_ref, all_ref, send_sems, recv_sems, local_sem
        x, y, c = _position()
        self.me, self.sibling, self.c = (x, y, c), (x, y, 1 - c), c
        self.chips = _other_chips(x, y)

    @staticmethod
    def semaphores():
        return [pltpu.SemaphoreType.DMA((7,)), pltpu.SemaphoreType.DMA((7,)), pltpu.SemaphoreType.DMA]

    def _copy(self, k, block, to, from_pack=False):
        px, py, pc = block
        slot = self.all_ref.at[4 * px + 2 * py + pc]
        return pltpu.make_async_remote_copy(src_ref=self.p_ref if from_pack else slot, dst_ref=slot, send_sem=self.send_sems.at[k],
                                            recv_sem=self.recv_sems.at[k], device_id=to, device_id_type=MESH)

    def _mine(self):
        x, y, c = self.me
        return pltpu.make_async_copy(self.p_ref, self.all_ref.at[4 * x + 2 * y + c], self.local_sem)

    def _first(self):
        return [self._copy(0, self.me, self.sibling, True)] + [
            self._copy(1 + k, self.me, (*chip, self.c), True) for k, chip in enumerate(self.chips)]

    def _passed(self):
        return [self._copy(4 + k, (*chip, self.c), self.sibling) for k, chip in enumerate(self.chips)]

    def start(self):
        self._mine().start()
        for cp in self._first():
            cp.start()

    def hand_over(self):
        for k, chip in enumerate(self.chips):
            self._copy(1 + k, (*chip, self.c), self.me).wait_recv()
            self._passed()[k].start()

    def finish(self):
        self._copy(0, self.sibling, self.me).wait_recv()
        for k, chip in enumerate(self.chips):
            self._copy(4 + k, (*chip, 1 - self.c), self.me).wait_recv()
        for cp in self._first() + self._passed():
            cp.wait_send()
        self._mine().wait()


class _DirectGather:
    def __init__(self, p_ref, all_ref, send_sems, recv_sems, local_sem):
        self.p_ref, self.all_ref, self.send_sems, self.recv_sems, self.local_sem = p_ref, all_ref, send_sems, recv_sems, local_sem
        self.me = _position()

    semaphores = _PackGather.semaphores

    def _peer(self, r):
        x, y, c = self.me
        return ((1 - x) if r & 4 else x, (1 - y) if r & 2 else y, (1 - c) if r & 1 else c)

    def _copy(self, r, slot_of):
        px, py, pc = slot_of
        return pltpu.make_async_remote_copy(src_ref=self.p_ref, dst_ref=self.all_ref.at[4 * px + 2 * py + pc],
                                            send_sem=self.send_sems.at[r - 1], recv_sem=self.recv_sems.at[r - 1],
                                            device_id=self._peer(r), device_id_type=MESH)

    def _mine(self):
        x, y, c = self.me
        return pltpu.make_async_copy(self.p_ref, self.all_ref.at[4 * x + 2 * y + c], self.local_sem)

    def start(self):
        self._mine().start()
        for r in range(1, N_DEVICES):
            self._copy(r, self.me).start()

    def finish(self):
        for r in range(1, N_DEVICES):
            self._copy(r, self._peer(r)).wait()
        self._mine().wait()


def _adamw(w, g, m, v):
    m = ADAM_B1 * m + (1.0 - ADAM_B1) * g
    v = ADAM_B2 * v + (1.0 - ADAM_B2) * (g * g)
    m_hat = m / ADAM_BC1
    v_hat = v / ADAM_BC2
    delta = -ADAM_LR * (m_hat / (jnp.sqrt(v_hat) + ADAM_EPS) + ADAM_WD * w)
    return delta, m, v


JOIN_SUB = 4


def _join(tag, shard_shape, part, arrived, core_chip, block=None):
    pr, pc = WGRAD_GEOMETRY[tag][:2]
    rb = pr // JOIN_SUB
    by_rows = shard_shape[1] == pc
    riding = block is not None

    def body(cc_ref, p_ref, r1_ref, r2_ref, r3_ref, *rest):
        if riding:
            blk_ref, g_ref, all_ref, stage, send_sems, recv_sems, local_sems, b_send, b_recv, b_local = rest
            gather = _DirectGather(blk_ref, all_ref, b_send, b_recv, b_local)
        else:
            g_ref, stage, send_sems, recv_sems, local_sems = rest
        i = pl.program_id(0)
        c = cc_ref[0]
        if riding:
            @pl.when(i == 0)
            def _():
                gather.start()

        def window(core, k):
            if by_rows:
                return g_ref.at[pl.ds((core * JOIN_SUB + k) * rb, rb), :]
            return g_ref.at[pl.ds(k * rb, rb), pl.ds(core * pc, pc)]

        def keep(k):
            return pltpu.make_async_copy(stage.at[k], window(c, k), local_sems.at[k])

        def push(k):
            return pltpu.make_async_remote_copy(src_ref=stage.at[k], dst_ref=window(c, k), send_sem=send_sems.at[k],
                                                recv_sem=recv_sems.at[k], device_id=_sibling(), device_id_type=MESH)

        def pushed(k):
            return pltpu.make_async_remote_copy(src_ref=stage.at[k], dst_ref=window(1 - c, k), send_sem=send_sems.at[k],
                                                recv_sem=recv_sems.at[k], device_id=_sibling(), device_id_type=MESH)

        stage[i] = ((p_ref[0] + r1_ref[0].astype(F32)) + r2_ref[0].astype(F32)) + r3_ref[0].astype(F32)
        keep(i).start()
        push(i).start()

        @pl.when(i == JOIN_SUB - 1)
        def _():
            for k in range(JOIN_SUB):
                keep(k).wait()
                push(k).wait_send()
                pushed(k).wait_recv()
            if riding:
                gather.finish()

    def partial(off):
        return pl.BlockSpec((1, rb, pc), lambda i, cc: ((cc[1] + off) % N_CHIPS, i, 0))

    in_specs = [partial(0), partial(1), partial(2), partial(3)]
    out_specs = [ANY]
    out_shape = [jax.ShapeDtypeStruct(shard_shape, F32)]
    scratch = [pltpu.VMEM((JOIN_SUB, rb, pc), F32), pltpu.SemaphoreType.DMA((JOIN_SUB,)),
               pltpu.SemaphoreType.DMA((JOIN_SUB,)), pltpu.SemaphoreType.DMA((JOIN_SUB,))]
    operands = [part, arrived, arrived, arrived]
    if riding:
        in_specs.append(pl.BlockSpec(block.shape, lambda i, cc: (0, 0)))
        out_specs.append(ANY)
        out_shape.append(jax.ShapeDtypeStruct((N_DEVICES,) + block.shape, block.dtype))
        scratch += _DirectGather.semaphores()
        operands.append(block)
    outs = pl.pallas_call(
        body, name="join_" + tag,
        grid_spec=pltpu.PrefetchScalarGridSpec(
            num_scalar_prefetch=1, grid=(JOIN_SUB,), in_specs=in_specs, out_specs=out_specs, scratch_shapes=scratch),
        out_shape=out_shape,
        compiler_params=_params(dimension_semantics=("arbitrary",)),
    )(core_chip, *operands)
    return outs if riding else outs[0]


def _adamw_big(w, g, m, v, name):
    rows, cols = w.shape
    rb = ADAMW_ROWS if rows % ADAMW_ROWS == 0 else rows

    def body(w_ref, g_ref, m_ref, v_ref, go_ref, d_ref, nm_ref, nv_ref):
        g = g_ref[...]
        go_ref[...] = g
        d_ref[...], nm_ref[...], nv_ref[...] = _adamw(w_ref[...], g, m_ref[...], v_ref[...])

    spec = pl.BlockSpec((rb, cols), lambda i: (i, 0))
    return pl.pallas_call(
        body, name=name, grid=(rows // rb,), in_specs=[spec] * 4, out_specs=[spec] * 4,
        out_shape=[jax.ShapeDtypeStruct(w.shape, F32)] * 4,
        compiler_params=_params(dimension_semantics=("arbitrary",)),
    )(w, g, m, v)


SMALL_VECTORS = {
    "norm_mix_g": (PK_MIX_G, D_MODEL), "rnn_conv_b": (PK_RCONV_B, LRU_W), "b_a": (PK_B_A, LRU_W), "b_x": (PK_B_X, LRU_W),
    "lru_lambda": (PK_LAMBDA, LRU_W), "g_norm_conv": (PK_G_NORM_CONV, CONV_W), "g_norm_rnn": (PK_G_NORM_RNN, LRU_W),
    "norm_mlp_g": (PK_MLP_G, D_MODEL), "final_norm_g": (PK_FINAL_G, D_MODEL),
}
SMALL_MATRICES = ("w_a", "w_x")


def _small_step(vec_packs, mat_packs, mix_g_blocks, p):
    vec_rows, cols = vec_packs.shape[1:]
    conv_rows, cshard = p["conv_w"].shape
    rconv_rows, rshard = p["rnn_conv_w"].shape
    names = list(SMALL_VECTORS) + list(SMALL_MATRICES) + ["conv_w", "rnn_conv_w"]
    shapes = ([(1, width) for _, width in SMALL_VECTORS.values()] + [mat_packs.shape[2:]] * len(SMALL_MATRICES)
              + [(conv_rows, cshard), (rconv_rows, rshard)])
    kinds = ("", "m_", "v_")
    params = [p[pre + n].reshape(1, -1) if n in SMALL_VECTORS else p[pre + n] for pre in kinds for n in names]

    def body(vec_ref, mat_ref, blk_ref, *rest):
        wmv = [dict(zip(names, rest[k * len(names):(k + 1) * len(names)])) for k in range(3)]
        loss_ref, rest = rest[3 * len(names)], rest[3 * len(names) + 1:]
        leaves, (g_ref, w_ref, m_ref, v_ref) = [rest[k * len(names):(k + 1) * len(names)] for k in range(4)], rest[4 * len(names):]
        total = vec_ref[0]
        mats = mat_ref[0].astype(F32)
        late = blk_ref[0]
        for k in range(1, N_DEVICES):
            total = total + vec_ref[k]
            mats = mats + mat_ref[k].astype(F32)
            late = late + blk_ref[k]
        g_ref[0:vec_rows, :] = total
        g_ref[vec_rows:, :] = late
        g = g_ref[...]
        loss_ref[...] = g[PK_LOSS:PK_LOSS + 1, 0:1]

        for pack_ref, given in zip((w_ref, m_ref, v_ref), wmv):
            pack_ref[...] = jnp.zeros_like(pack_ref)
            for name, (row, width) in SMALL_VECTORS.items():
                pack_ref[row:row + 1, 0:width] = given[name][...]

        x, y, _ = _position()
        j = 2 * x + y
        cblk = total[0:TILE_ROWS, :]
        rblk = total[PK_RCONV_W:PK_RCONV_W + TILE_ROWS, :]
        cg = cblk[:, 0:cshard]
        rg = rblk[:, 0:rshard]
        for k in range(1, N_CHIPS):
            cg = jnp.where(j == k, cblk[:, k * cshard:(k + 1) * cshard], cg)
            rg = jnp.where(j == k, rblk[:, k * rshard:(k + 1) * rshard], rg)
        cg = cg[PK_CONV_W:PK_CONV_W + conv_rows, :]
        rg = rg[0:rconv_rows, :]

        def step(name, grad):
            return (grad,) + _adamw(wmv[0][name][...], grad, wmv[1][name][...], wmv[2][name][...])

        packs = (g,) + _adamw(w_ref[...], g, m_ref[...], v_ref[...])
        matrices = [step(name, mats[n]) for n, name in enumerate(SMALL_MATRICES)]
        convs, rconvs = step("conv_w", cg), step("rnn_conv_w", rg)
        for kind in range(4):
            out = dict(zip(names, leaves[kind]))
            for name, (row, width) in SMALL_VECTORS.items():
                out[name][...] = packs[kind][row:row + 1, 0:width]
            for n, name in enumerate(SMALL_MATRICES):
                out[name][...] = matrices[n][kind]
            out["conv_w"][...] = convs[kind]
            out["rnn_conv_w"][...] = rconvs[kind]

    outs = pl.pallas_call(
        body, name="small_grads_step", in_specs=[VMEM] * (3 + len(params)), out_specs=[VMEM] * (1 + 4 * len(names)),
        out_shape=[jax.ShapeDtypeStruct((1, 1), F32)] + [jax.ShapeDtypeStruct(sh, F32) for sh in shapes] * 4,
        scratch_shapes=[pltpu.VMEM((PK_ROWS, cols), F32)] * 4,
        compiler_params=_params(),
    )(vec_packs, mat_packs, mix_g_blocks, *params)
    return outs[0], [dict(zip(names, outs[1 + k * len(names):1 + (k + 1) * len(names)])) for k in range(4)]


def _to_block_diag(w):
    w4 = w.reshape(N_BD, 4, 64, 64)
    return jnp.concatenate([jnp.pad(w4[:, q], ((0, 0), (0, 0), (64 * q, 64 * (3 - q)))) for q in range(4)], axis=1)


_NAMES = ['norm_mix_g', 'w_in', 'conv_w', 'rnn_conv_w', 'rnn_conv_b', 'w_a', 'b_a', 'w_x', 'b_x', 'lru_lambda',
          'g_norm_conv', 'g_norm_rnn', 'w_out', 'norm_mlp_g', 'w_mlp_in', 'w_mlp_out', 'final_norm_g']


def kernel(x, norm_mix_g, w_in, conv_w, rnn_conv_w, rnn_conv_b, w_a, b_a, w_x, b_x, lru_lambda, g_norm_conv, g_norm_rnn, w_out, norm_mlp_g, w_mlp_in, w_mlp_out, final_norm_g, loss_target, m_norm_mix_g, m_w_in, m_conv_w, m_rnn_conv_w, m_rnn_conv_b, m_w_a, m_b_a, m_w_x, m_b_x, m_lru_lambda, m_g_norm_conv, m_g_norm_rnn, m_w_out, m_norm_mlp_g, m_w_mlp_in, m_w_mlp_out, m_final_norm_g, v_norm_mix_g, v_w_in, v_conv_w, v_rnn_conv_w, v_rnn_conv_b, v_w_a, v_b_a, v_w_x, v_b_x, v_lru_lambda, v_g_norm_conv, v_g_norm_rnn, v_w_out, v_norm_mlp_g, v_w_mlp_in, v_w_mlp_out, v_final_norm_g):
    args = dict(locals())
    p = {}
    for n in _NAMES:
        for pre in ("", "m_", "v_"):
            a = args[pre + n]
            p[pre + n] = a[0] if a.ndim >= 3 else a
    xs = x[0]
    target = loss_target[0]
    core_chip = jnp.stack([lax.axis_index("c"), 2 * lax.axis_index("x") + lax.axis_index("y")]).astype(jnp.int32)

    w_in_g, w_out_g, w1_g, w2_g, conv_full, rconv_full = _gather_first(
        p["w_in"], p["w_out"], p["w_mlp_in"], p["w_mlp_out"], p["conv_w"], p["rnn_conv_w"])
    wa_bd = _to_block_diag(p["w_a"]).astype(BF16)
    wx_bd = _to_block_diag(p["w_x"]).astype(BF16)
    gf = p["final_norm_g"].reshape(1, -1)
    lru = (wa_bd, p["b_a"], wx_bd, p["b_x"], p["lru_lambda"], p["g_norm_conv"], p["g_norm_rnn"])

    (u, h1b, xr, hs, c3, yb, gates), (w_out_g, w1_g, w2_g) = _fwd_mix(
        xs, p["norm_mix_g"], w_in_g, conv_full, rconv_full, p["rnn_conv_b"], *lru, (w_out_g, w1_g, w2_g))
    zb, dpb, h2b, dx3b, dx2, dx2b, dy, st_mlp = _mlp_fwd_bwd(
        xs, yb, w_out_g.reshape(-1, D_MODEL), w1_g, w2_g.reshape(-1, D_MODEL), p["norm_mlp_g"], gf, target)

    part_out = _wgrad(yb, dx2b, "out", core_chip)
    *part_1, arrived_out = _wgrad(h2b, dpb, "mlp_in", core_chip, parts=(part_out[1],))
    part_2 = _wgrad(zb, dx3b, "mlp_out", core_chip)
    (dub, vec_pack, mat_pack), (arrived_1, arrived_2) = _mix_bwd(
        dy, u, xr, hs, c3, gates, conv_full, rconv_full, wa_bd, wx_bd, p["lru_lambda"], p["g_norm_conv"], p["g_norm_rnn"],
        st_mlp, (part_1[1], part_2[1]))
    arrived_mlp = (arrived_out, arrived_1, arrived_2)
    *part_in, vec_packs, mat_packs = _wgrad(h1b, dub, "in", core_chip, packs=(vec_pack, mat_pack))
    early = (("w_out", "out", part_out, arrived_mlp[0]), ("w_mlp_in", "mlp_in", part_1, arrived_mlp[1]),
             ("w_mlp_out", "mlp_out", part_2, arrived_mlp[2]))
    (grad_x, st_in), arrived_in, joined = _in_bwd(
        dub, w_in_g, xs, dx2, p["norm_mix_g"], (part_in[1],),
        [(tag, p[n].shape, part[0], arrived) for n, tag, part, arrived in early], core_chip)
    g_in, mix_g_blocks = _join("in", p["w_in"].shape, part_in[0], arrived_in[0], core_chip, st_in)
    big = {}
    for n, tag, g in [(n, tag, g) for (n, tag, _, _), g in zip(early, joined)] + [("w_in", "in", g_in)]:
        big[n] = _adamw_big(p[n], g, p["m_" + n], p["v_" + n], "adamw_" + tag)

    loss, outs = _small_step(vec_packs, mat_packs, mix_g_blocks, p)
    for kind, o in enumerate(outs):
        o["final_norm_g"] = o["final_norm_g"].reshape(-1)
        for n in SMALL_MATRICES + ("conv_w", "rnn_conv_w"):
            o[n] = o[n][None]
        for n in ("w_in", "w_out", "w_mlp_in", "w_mlp_out"):
            o[n] = big[n][kind][None]
    loss = loss.reshape(())
    return (loss, grad_x[None], *[o[n] for o in outs for n in _NAMES])
```

```python
import functools
import math

import jax
import jax.numpy as jnp
from jax import lax
from jax.experimental import pallas as pl
from jax.experimental.pallas import tpu as pltpu

F32 = jnp.float32
BF16 = jnp.bfloat16
MESH = pl.DeviceIdType.MESH
ANY = pl.BlockSpec(memory_space=pl.ANY)
VMEM = pl.BlockSpec(memory_space=pltpu.VMEM)

EPS = 1e-6
LRU_C = 8.0
D_MODEL = 1024
CONV_W = 512
LRU_W = 1024
IN_COLS = 3 * CONV_W + 2 * LRU_W
IN_SHARD = IN_COLS // 4
N_CHIPS = 4
N_DEVICES = 8
BD = 256
N_BD = LRU_W // BD

ADAM_LR = 0.001
ADAM_B1 = 0.9
ADAM_B2 = 0.999
ADAM_EPS = 1e-08
ADAM_WD = 0.01
ADAM_STEP = 10
ADAM_BC1 = 1.0 - ADAM_B1 ** ADAM_STEP
ADAM_BC2 = 1.0 - ADAM_B2 ** ADAM_STEP

TILE_ROWS, LANES = 8, 128
TOKEN_TILE = 256
MATMUL_TOKEN_TILE = 512
ADAMW_ROWS = 256
VMEM_LIMIT = 56 * 1024 * 1024

PK_G_NORM_RNN, PK_RCONV_B, PK_B_A, PK_B_X, PK_LAMBDA, PK_CONV_W = 0, 1, 2, 3, 4, 5
PK_RCONV_W, PK_G_NORM_CONV = 8, 12
PK_MIX_ROWS = 16
PK_FINAL_G, PK_MLP_G, PK_LOSS = 16, 17, 18
PK_MLP_ROWS = 8
PK_MIX_G = 24
PK_ROWS = 32
N_HEADS, HEAD_DIM = 16, 64


def _params(**kw):
    return pltpu.CompilerParams(vmem_limit_bytes=VMEM_LIMIT, **kw)


def _position():
    x, y, c = lax.axis_index("x"), lax.axis_index("y"), lax.axis_index("c")
    return x, y, c


def _sigmoid(v):
    return 1.0 / (1.0 + jnp.exp(-v))


def _one_minus_square(log_a, a):
    v = 2.0 * log_a
    series = -v * (1.0 + v * (0.5 + v * (1.0 / 6.0)))
    return jnp.where(v > -0.01, series, 1.0 - a * a)


_GELU_C = math.sqrt(2.0 / math.pi)
_GELU_K = 0.044715


def _gelu_and_grad(g):
    th = jnp.tanh(_GELU_C * (g + _GELU_K * g * g * g))
    gelu = 0.5 * g * (1.0 + th)
    dgelu = 0.5 * (1.0 + th) + 0.5 * g * (1.0 - th * th) * (_GELU_C * (1.0 + 3.0 * _GELU_K * g * g))
    return gelu, dgelu


def _rows(shape):
    return lax.broadcasted_iota(jnp.int32, shape, 0)


def _shift_down(v, k, prev8):
    rolled = pltpu.roll(v, k, 0)
    halo = pltpu.roll(prev8, k, 0)
    head = jnp.where(_rows(halo.shape) < k, halo, rolled[:TILE_ROWS])
    return jnp.concatenate([head, rolled[TILE_ROWS:]], axis=0)


def _shift_up(v, k, next8):
    n = v.shape[0]
    rolled = pltpu.roll(v, n - k, 0)
    halo = pltpu.roll(next8, TILE_ROWS - k, 0)
    tail = jnp.where(_rows(halo.shape) >= TILE_ROWS - k, halo, rolled[n - TILE_ROWS:])
    return jnp.concatenate([rolled[: n - TILE_ROWS], tail], axis=0)


def _scan_rows(a, b, carry, reverse=False):
    n, w = a.shape
    groups = n // TILE_ROWS
    a3 = a.reshape(groups, TILE_ROWS, w)
    b3 = b.reshape(groups, TILE_ROWS, w)
    sub = lax.broadcasted_iota(jnp.int32, a3.shape, 1)
    s = 1
    while s < TILE_ROWS:
        shift = TILE_ROWS - s if reverse else s
        keep = (sub < TILE_ROWS - s) if reverse else (sub >= s)
        b3 = b3 + jnp.where(keep, a3 * pltpu.roll(b3, shift, 1), 0.0)
        a3 = a3 * jnp.where(keep, pltpu.roll(a3, shift, 1), 1.0)
        s *= 2
    out = [None] * groups
    edge = 0 if reverse else TILE_ROWS - 1
    for g in (range(groups - 1, -1, -1) if reverse else range(groups)):
        out[g] = b3[g] + a3[g] * carry
        carry = out[g][edge:edge + 1]
    return jnp.concatenate(out, axis=0)


def _softplus_neg(lam):
    e = jnp.exp(-jnp.abs(lam))
    log1p_e = jnp.where(e < 1e-2, e * (1.0 - e * (0.5 - e * (1.0 / 3.0 - e * 0.25))), jnp.log(1.0 + e))
    sp = jnp.maximum(-lam, 0.0) + log1p_e
    dsp = -_sigmoid(-lam)
    return sp, dsp


def _block_diag_dot(vb, w_ref):
    return jnp.concatenate(
        [jnp.dot(vb[:, j * BD:(j + 1) * BD], w_ref[j], preferred_element_type=F32) for j in range(N_BD)], axis=1)


def _block_diag_dot_t(vb, w_ref):
    return jnp.concatenate(
        [lax.dot_general(vb[:, j * BD:(j + 1) * BD], w_ref[j], (((1,), (1,)), ((), ())), preferred_element_type=F32)
         for j in range(N_BD)], axis=1)


def _dot_nt(a, b):
    return lax.dot_general(a, b, (((1,), (1,)), ((), ())), preferred_element_type=F32)


def _dot_tn(a, b):
    return lax.dot_general(a, b, (((0,), (0,)), ((), ())), preferred_element_type=F32)


def _lru_gates(xr, wa_ref, ba, wx_ref, bx, sp):
    xrb = xr.astype(BF16)
    r = _sigmoid(_block_diag_dot(xrb, wa_ref) + ba)
    ig = _sigmoid(_block_diag_dot(xrb, wx_ref) + bx)
    log_a = (-LRU_C) * r * sp
    a = jnp.exp(log_a)
    mult = jnp.sqrt(_one_minus_square(log_a, a))
    return r, ig, a, mult


def _colsum(v):
    return jnp.sum(v, axis=0, keepdims=True)


N_FWD_OUT = 7


def _fwd_mix(x, g1, w_in_g, conv_w, rconv_w, rconv_b, wa_bd, b_a, wx_bd, b_x, lam, g_nc, g_nr, later):
    t, d = x.shape
    tm = TOKEN_TILE
    nt = t // tm
    nl = len(later)
    assert nl == 3
    pass_on_at = [nt * f // 16 for f in (3, 5, 9)]
    neighbours_at = [nt * f // 16 for f in (10, 11, 12)]
    diagonal_at = [nt * f // 16 for f in (13, 14, 14)]

    def body(x_ref, g1_ref, win_ref, cw_ref, rw_ref, rb_ref, wa_ref, ba_ref, wx_ref, bx_ref, lam_ref, gnc_ref, gnr_ref,
             *rest):
        later_in, outs, rest = rest[:nl], rest[nl:nl + N_FWD_OUT], rest[nl + N_FWD_OUT:]
        u_ref, h1_ref, xr_ref, hs_ref, c3_ref, y_ref, gates_ref = outs
        later_out, (cv_prev, xin_prev, h_prev, send_sems, recv_sems) = rest[:nl], rest[nl:]
        del later_in
        step = pl.program_id(0)
        plan = _ShardGather(later_out, send_sems, recv_sems)

        @pl.when(step == 0)
        def _():
            cv_prev[...] = jnp.zeros_like(cv_prev)
            xin_prev[...] = jnp.zeros_like(xin_prev)
            h_prev[...] = jnp.zeros_like(h_prev)
            for w in range(nl):
                plan.start_direct(w)

        for w in range(nl):
            @pl.when(step == pass_on_at[w])
            def _(w=w):
                plan.start_pass_on(w)

            @pl.when(step == neighbours_at[w])
            def _(w=w):
                plan.start_hand_over(w, diagonal=False)

            @pl.when(step == diagonal_at[w])
            def _(w=w):
                plan.start_hand_over(w, diagonal=True)

        xv = x_ref[...]
        rstd = lax.rsqrt(jnp.mean(xv * xv, axis=-1, keepdims=True) + EPS)
        h1b = ((xv * rstd) * g1_ref[...]).astype(BF16)
        h1_ref[...] = h1b
        for j in range(N_CHIPS):
            u_ref[:, j * IN_SHARD:(j + 1) * IN_SHARD] = jnp.dot(h1b, win_ref[j], preferred_element_type=F32)
        gate_b = u_ref[:, 0:CONV_W]
        cv = u_ref[:, CONV_W:2 * CONV_W] * u_ref[:, 2 * CONV_W:3 * CONV_W]
        x_r = u_ref[:, 3 * CONV_W:3 * CONV_W + LRU_W]
        g = u_ref[:, 3 * CONV_W + LRU_W:]

        cw = cw_ref[...]
        cvp = cv_prev[...]
        conv3 = cw[0:1] * _shift_down(cv, 2, cvp) + cw[1:2] * _shift_down(cv, 1, cvp) + cw[2:3] * cv
        cv_prev[...] = cv[tm - TILE_ROWS:]
        c3_ref[...] = conv3
        y_conv = gate_b * conv3

        rw = rw_ref[...]
        xp = xin_prev[...]
        xr = (rw[0:1] * _shift_down(x_r, 3, xp) + rw[1:2] * _shift_down(x_r, 2, xp)
              + rw[2:3] * _shift_down(x_r, 1, xp) + rw[3:4] * x_r) + rb_ref[...]
        xin_prev[...] = x_r[tm - TILE_ROWS:]
        xr_ref[...] = xr
        sp, _ = _softplus_neg(lam_ref[...])
        r, ig, a, mult = _lru_gates(xr, wa_ref, ba_ref[...], wx_ref, bx_ref[...], sp)
        for n, gate in enumerate((r, ig, a, mult)):
            gates_ref[:, n * LRU_W:(n + 1) * LRU_W] = gate
        h = _scan_rows(a, mult * (ig * xr), h_prev[...])
        h_prev[...] = h[tm - 1:tm]
        hs_ref[...] = h
        gelu, _ = _gelu_and_grad(g)
        y_rnn = h * gelu

        na = y_conv * lax.rsqrt(jnp.mean(y_conv * y_conv, axis=-1, keepdims=True) + EPS) * gnc_ref[...]
        nb = y_rnn * lax.rsqrt(jnp.mean(y_rnn * y_rnn, axis=-1, keepdims=True) + EPS) * gnr_ref[...]
        y_ref[:, :CONV_W] = na.astype(BF16)
        y_ref[:, CONV_W:] = nb.astype(BF16)

        @pl.when(step == nt - 1)
        def _():
            for w in range(nl):
                plan.finish(w)

    def full(a):
        nd = a.ndim
        return pl.BlockSpec(a.shape, lambda i: (0,) * nd)

    def tok(cols):
        return pl.BlockSpec((tm, cols), lambda i: (i, 0))

    def act(cols, dtype=F32):
        return jax.ShapeDtypeStruct((t, cols), dtype)

    smalls = (g1, w_in_g, conv_w, rconv_w, rconv_b, wa_bd, b_a, wx_bd, b_x, lam, g_nc, g_nr)
    n_in = 1 + len(smalls)
    outs = pl.pallas_call(
        body, name="fwd_mix", grid=(nt,),
        in_specs=[tok(d)] + [full(a) for a in smalls] + [ANY] * nl,
        out_specs=[tok(IN_COLS), tok(d), tok(LRU_W), tok(LRU_W), tok(CONV_W), tok(CONV_W + LRU_W)]
        + [tok(4 * LRU_W)] + [ANY] * nl,
        out_shape=[act(IN_COLS), act(d, BF16), act(LRU_W), act(LRU_W), act(CONV_W), act(CONV_W + LRU_W, BF16)]
        + [act(4 * LRU_W)] + [jax.ShapeDtypeStruct(a.shape, a.dtype) for a in later],
        input_output_aliases={n_in + w: N_FWD_OUT + w for w in range(nl)},
        scratch_shapes=[pltpu.VMEM((TILE_ROWS, CONV_W), F32), pltpu.VMEM((TILE_ROWS, LRU_W), F32),
                        pltpu.VMEM((1, LRU_W), F32), pltpu.SemaphoreType.DMA((nl, _ShardGather.PAIRS)),
                        pltpu.SemaphoreType.DMA((nl, _ShardGather.PAIRS))],
        compiler_params=_params(dimension_semantics=("arbitrary",)),
    )(x, *smalls, *later)
    return outs[:N_FWD_OUT], outs[N_FWD_OUT:]


def _mlp_fwd_bwd(x, yb, w_out_g, w1_g, w2_g, g2, gf, target):
    t, d = x.shape
    tm = TOKEN_TILE
    ff = w2_g.shape[0]
    mix = w_out_g.shape[0]
    ffs = ff // N_CHIPS

    def body(x_ref, y_ref, g2_ref, gf_ref, tgt_ref, wout_hbm, w1_hbm, w2_hbm,
             z_ref, dp_ref, h2_ref, dx3b_ref, dx2_ref, dx2b_ref, dy_ref, st_ref, wout, w1, w2, p_ref):
        @pl.when(pl.program_id(0) == 0)
        def _():
            pltpu.sync_copy(wout_hbm, wout)
            pltpu.sync_copy(w1_hbm, w1)
            pltpu.sync_copy(w2_hbm, w2)
            st_ref[...] = jnp.zeros_like(st_ref)

        x2 = x_ref[...] + jnp.dot(y_ref[...], wout[...], preferred_element_type=F32)
        r2 = lax.rsqrt(jnp.mean(x2 * x2, axis=-1, keepdims=True) + EPS)
        xh2 = x2 * r2
        g2v = g2_ref[...]
        h2b = (xh2 * g2v).astype(BF16)
        h2_ref[...] = h2b
        for j in range(N_CHIPS):
            p_ref[:, j * ffs:(j + 1) * ffs] = jnp.dot(h2b, w1[j], preferred_element_type=F32)
        rp = jnp.maximum(p_ref[...], 0.0)
        zb = (rp * rp).astype(BF16)
        z_ref[...] = zb
        x3 = x2 + jnp.dot(zb, w2[...], preferred_element_type=F32)
        r3 = lax.rsqrt(jnp.mean(x3 * x3, axis=-1, keepdims=True) + EPS)
        xh3 = x3 * r3
        gfv = gf_ref[...]
        err = xh3 * gfv - tgt_ref[...]
        loss = (0.5 / d) * jnp.sum(err * err)
        dout = err * (1.0 / d)
        st_ref[PK_FINAL_G - PK_MIX_ROWS:PK_FINAL_G - PK_MIX_ROWS + 1, :] += _colsum(dout * xh3)
        st_ref[PK_LOSS - PK_MIX_ROWS:PK_LOSS - PK_MIX_ROWS + 1, :] += jnp.zeros((1, d), F32) + loss
        dxh3 = dout * gfv
        dx3 = r3 * (dxh3 - xh3 * jnp.mean(dxh3 * xh3, axis=-1, keepdims=True))
        dx3b = dx3.astype(BF16)
        dx3b_ref[...] = dx3b
        dpb = (_dot_nt(dx3b, w2[...]) * (2.0 * rp)).astype(BF16)
        dp_ref[...] = dpb
        dh2 = _dot_nt(dpb[:, 0:ffs], w1[0])
        for j in range(1, N_CHIPS):
            dh2 = dh2 + _dot_nt(dpb[:, j * ffs:(j + 1) * ffs], w1[j])
        st_ref[PK_MLP_G - PK_MIX_ROWS:PK_MLP_G - PK_MIX_ROWS + 1, :] += _colsum(dh2 * xh2)
        dxh2 = dh2 * g2v
        dx2 = dx3 + r2 * (dxh2 - xh2 * jnp.mean(dxh2 * xh2, axis=-1, keepdims=True))
        dx2_ref[...] = dx2
        dx2b = dx2.astype(BF16)
        dx2b_ref[...] = dx2b
        dy_ref[...] = _dot_nt(dx2b, wout[...])

    def tok(cols):
        return pl.BlockSpec((tm, cols), lambda i: (i, 0))

    def row(cols):
        return pl.BlockSpec((1, cols), lambda i: (0, 0))

    return pl.pallas_call(
        body, name="mlp_fwd_bwd", grid=(t // tm,),
        in_specs=[tok(d), tok(mix), row(d), row(d), tok(d), ANY, ANY, ANY],
        out_specs=[tok(ff), tok(ff), tok(d), tok(d), tok(d), tok(d), tok(mix),
                   pl.BlockSpec((PK_MLP_ROWS, d), lambda i: (0, 0))],
        out_shape=[jax.ShapeDtypeStruct((t, ff), BF16), jax.ShapeDtypeStruct((t, ff), BF16),
                   jax.ShapeDtypeStruct((t, d), BF16), jax.ShapeDtypeStruct((t, d), BF16),
                   jax.ShapeDtypeStruct((t, d), F32), jax.ShapeDtypeStruct((t, d), BF16),
                   jax.ShapeDtypeStruct((t, mix), F32), jax.ShapeDtypeStruct((PK_MLP_ROWS, d), F32)],
        scratch_shapes=[pltpu.VMEM(w_out_g.shape, BF16), pltpu.VMEM(w1_g.shape, BF16), pltpu.VMEM(w2_g.shape, BF16),
                        pltpu.VMEM((tm, ff), F32)],
        compiler_params=_params(dimension_semantics=("arbitrary",)),
    )(x, yb, g2, gf, target, w_out_g, w1_g, w2_g)


def _mix_bwd(dy, u, xr_all, hs_all, c3_all, gates, conv_w, rconv_w, wa_bd, wx_bd, lam, g_nc, g_nr, st_mlp, parts):
    t = dy.shape[0]
    tm = TOKEN_TILE
    nt = t // tm
    hb = tm // TILE_ROWS
    npart = len(parts)

    def body(dy_ref, u_ref, uh_ref, xr_ref, hs_ref, hh_ref, c3_ref, gates_ref,
             cw_ref, rw_ref, wa_ref, wx_ref, lam_ref, gnc_ref, gnr_ref, stm_ref, *rest):
        part_refs, (du_ref, st_ref, heads_ref), rest = rest[:npart], rest[npart:npart + 3], rest[npart + 3:]
        arrived_refs, (dc_next, a_next, gs_next, dxr_next, dwa_ref, dwx_ref, send_sems, recv_sems) = rest[:npart], rest[npart:]
        exchange = _PartialExchange(part_refs, arrived_refs, send_sems, recv_sems)
        i = pl.program_id(0)

        @pl.when(i == 0)
        def _():
            exchange.start()
            dc_next[...] = jnp.zeros_like(dc_next)
            a_next[...] = jnp.zeros_like(a_next)
            gs_next[...] = jnp.zeros_like(gs_next)
            dxr_next[...] = jnp.zeros_like(dxr_next)
            st_ref[0:PK_MIX_ROWS, :] = jnp.zeros((PK_MIX_ROWS, LRU_W), F32)
            st_ref[PK_MIX_ROWS:, :] = stm_ref[...]
            dwa_ref[...] = jnp.zeros_like(dwa_ref)
            dwx_ref[...] = jnp.zeros_like(dwx_ref)

        first_tile = i == nt - 1
        gate_b = u_ref[:, 0:CONV_W]
        gate_c = u_ref[:, CONV_W:2 * CONV_W]
        v = u_ref[:, 2 * CONV_W:3 * CONV_W]
        x_r = u_ref[:, 3 * CONV_W:3 * CONV_W + LRU_W]
        g = u_ref[:, 3 * CONV_W + LRU_W:]
        cv = gate_c * v
        cv_prev = jnp.where(first_tile, 0.0, uh_ref[:, CONV_W:2 * CONV_W] * uh_ref[:, 2 * CONV_W:3 * CONV_W])
        xin_prev = jnp.where(first_tile, 0.0, uh_ref[:, 3 * CONV_W:3 * CONV_W + LRU_W])
        hs_prev = jnp.where(first_tile, 0.0, hh_ref[...])

        def acc(first_row, val, width=LRU_W, row=0):
            r0 = first_row + row
            st_ref[r0:r0 + 1, 0:width] += val

        conv3 = c3_ref[...]
        y_conv = gate_b * conv3
        ra = lax.rsqrt(jnp.mean(y_conv * y_conv, axis=-1, keepdims=True) + EPS)
        xha = y_conv * ra
        dna = dy_ref[:, :CONV_W]
        acc(PK_G_NORM_CONV, _colsum(dna * xha), CONV_W)
        dxha = dna * gnc_ref[...]
        dy_conv = ra * (dxha - xha * jnp.mean(dxha * xha, axis=-1, keepdims=True))
        du_ref[:, 0:CONV_W] = (dy_conv * conv3).astype(BF16)
        dc = dy_conv * gate_b
        cw = cw_ref[...]
        dcn = dc_next[...]
        dcv = cw[2:3] * dc + cw[1:2] * _shift_up(dc, 1, dcn) + cw[0:1] * _shift_up(dc, 2, dcn)
        dc_next[...] = dc[:TILE_ROWS]
        acc(PK_CONV_W, _colsum(dc * _shift_down(cv, 2, cv_prev)), CONV_W, 0)
        acc(PK_CONV_W, _colsum(dc * _shift_down(cv, 1, cv_prev)), CONV_W, 1)
        acc(PK_CONV_W, _colsum(dc * cv), CONV_W, 2)
        du_ref[:, CONV_W:2 * CONV_W] = (dcv * v).astype(BF16)
        du_ref[:, 2 * CONV_W:3 * CONV_W] = (dcv * gate_c).astype(BF16)

        hs = hs_ref[...]
        gelu, dgelu = _gelu_and_grad(g)
        y_rnn = hs * gelu
        rb = lax.rsqrt(jnp.mean(y_rnn * y_rnn, axis=-1, keepdims=True) + EPS)
        xhb = y_rnn * rb
        dnb = dy_ref[:, CONV_W:]
        acc(PK_G_NORM_RNN, _colsum(dnb * xhb))
        dxhb = dnb * gnr_ref[...]
        dy_rnn = rb * (dxhb - xhb * jnp.mean(dxhb * xhb, axis=-1, keepdims=True))
        du_ref[:, 3 * CONV_W + LRU_W:] = (dy_rnn * hs * dgelu).astype(BF16)
        dh = dy_rnn * gelu

        xr = xr_ref[...]
        xrb = xr.astype(BF16)
        sp, dsp = _softplus_neg(lam_ref[...])
        r, ig, a, mult = [gates_ref[:, n * LRU_W:(n + 1) * LRU_W] for n in range(4)]
        a_up = _shift_up(a, 1, a_next[...])
        a_next[...] = a[:TILE_ROWS]
        gs = _scan_rows(a_up, dh, gs_next[0:1, :], reverse=True)
        gs_next[...] = gs[:TILE_ROWS]
        da = gs * _shift_down(hs, 1, hs_prev)
        gx = gs * xr
        di = gx * mult
        dmult = gx * ig
        dxr = gs * (mult * ig)
        dlog_a = da * a - dmult * ((a * a) / mult)
        acc(PK_LAMBDA, _colsum(dlog_a * r) * ((-LRU_C) * dsp))
        dpa = (dlog_a * ((-LRU_C) * sp)) * (r * (1.0 - r))
        dpx = di * (ig * (1.0 - ig))
        acc(PK_B_A, _colsum(dpa))
        acc(PK_B_X, _colsum(dpx))
        dpab = dpa.astype(BF16)
        dpxb = dpx.astype(BF16)
        dxr = dxr + _block_diag_dot_t(dpab, wa_ref) + _block_diag_dot_t(dpxb, wx_ref)
        for j in range(N_BD):
            cols = slice(j * BD, (j + 1) * BD)
            dwa_ref[j] += _dot_tn(xrb[:, cols], dpab[:, cols])
            dwx_ref[j] += _dot_tn(xrb[:, cols], dpxb[:, cols])

        acc(PK_RCONV_B, _colsum(dxr))
        rw = rw_ref[...]
        dxn = dxr_next[...]
        dx_r = (rw[3:4] * dxr + rw[2:3] * _shift_up(dxr, 1, dxn) + rw[1:2] * _shift_up(dxr, 2, dxn)
                + rw[0:1] * _shift_up(dxr, 3, dxn))
        dxr_next[...] = dxr[:TILE_ROWS]
        for k in range(3):
            acc(PK_RCONV_W, _colsum(dxr * _shift_down(x_r, 3 - k, xin_prev)), LRU_W, k)
        acc(PK_RCONV_W, _colsum(dxr * x_r), LRU_W, 3)
        du_ref[:, 3 * CONV_W:3 * CONV_W + LRU_W] = dx_r.astype(BF16)

        @pl.when(i == nt - 1)
        def _():
            for n, d_ref in enumerate((dwa_ref, dwx_ref)):
                for b in range(N_BD):
                    for q in range(BD // HEAD_DIM):
                        lane0 = q * HEAD_DIM // LANES * LANES
                        wide = d_ref[b, q * HEAD_DIM:(q + 1) * HEAD_DIM, lane0:lane0 + LANES]
                        if q * HEAD_DIM != lane0:
                            wide = pltpu.roll(wide, LANES - (q * HEAD_DIM - lane0), axis=1)
                        heads_ref[n, b * (BD // HEAD_DIM) + q] = wide[:, 0:HEAD_DIM].astype(BF16)
            exchange.wait()

    def full(a):
        nd = a.ndim
        return pl.BlockSpec(a.shape, lambda i: (0,) * nd)

    def tok(cols):
        return pl.BlockSpec((tm, cols), lambda i: (nt - 1 - i, 0))

    def halo(cols):
        return pl.BlockSpec((TILE_ROWS, cols), lambda i: (jnp.maximum((nt - 1 - i) * hb - 1, 0), 0))

    smalls = (conv_w, rconv_w, wa_bd, wx_bd, lam, g_nc, g_nr, st_mlp)
    st_rows = PK_MIX_ROWS + st_mlp.shape[0]
    heads = (2, N_HEADS, HEAD_DIM, HEAD_DIM)
    outs = pl.pallas_call(
        body, name="mix_bwd", grid=(nt,),
        in_specs=[tok(CONV_W + LRU_W), tok(IN_COLS), halo(IN_COLS), tok(LRU_W), tok(LRU_W), halo(LRU_W), tok(CONV_W)]
        + [tok(4 * LRU_W)] + [full(a) for a in smalls] + [ANY] * npart,
        out_specs=[tok(IN_COLS), pl.BlockSpec((st_rows, LRU_W), lambda i: (0, 0)),
                   pl.BlockSpec(heads, lambda i: (0, 0, 0, 0))]
        + [ANY] * npart,
        out_shape=[jax.ShapeDtypeStruct((t, IN_COLS), BF16), jax.ShapeDtypeStruct((st_rows, LRU_W), F32),
                   jax.ShapeDtypeStruct(heads, BF16)]
        + [jax.ShapeDtypeStruct(a.shape, a.dtype) for a in parts],
        scratch_shapes=[pltpu.VMEM((TILE_ROWS, CONV_W), F32), pltpu.VMEM((TILE_ROWS, LRU_W), F32),
                        pltpu.VMEM((TILE_ROWS, LRU_W), F32), pltpu.VMEM((TILE_ROWS, LRU_W), F32),
                        pltpu.VMEM((N_BD, BD, BD), F32), pltpu.VMEM((N_BD, BD, BD), F32),
                        pltpu.SemaphoreType.DMA((npart, 3)), pltpu.SemaphoreType.DMA((npart, 3))],
        compiler_params=_params(dimension_semantics=("arbitrary",)),
    )(dy, u, u, xr_all, hs_all, hs_all, c3_all, gates, *smalls, *parts)
    return outs[:3], outs[3:]


def _in_bwd(dub, w_in_g, x, dx2, g1, parts, joins, core_chip):
    t, d = x.shape
    tm = min(t, MATMUL_TOKEN_TILE)
    nt = t // tm
    npart = len(parts)
    nj = len(joins)
    geometry = []
    for tag, shape, _, _ in joins:
        pr, pc = WGRAD_GEOMETRY[tag][:2]
        every = 1 if pr % (nt * 16) == 0 else 2
        geometry.append((pr, pc, pr * every // nt, every, shape[1] == pc))

    def body(cc_ref, du_ref, win_ref, x_ref, dx2_ref, g1_ref, *rest):
        sums, rest = [rest[4 * w:4 * w + 4] for w in range(nj)], rest[4 * nj:]
        part_refs, (gx_ref, st_ref), rest = rest[:npart], rest[npart:npart + 2], rest[npart + 2:]
        arrived_refs, joined, rest = rest[:npart], rest[npart:npart + nj], rest[npart + nj:]
        stages, (send_sems, recv_sems, j_local, j_send, j_recv) = rest[:nj], rest[nj:]
        exchange = _PartialExchange(part_refs, arrived_refs, send_sems, recv_sems)
        i = pl.program_id(0)
        c = cc_ref[0]

        def window(w, core, row0, rows):
            pr, pc, _, _, by_rows = geometry[w]
            if by_rows:
                return joined[w].at[pl.ds(core * pr + row0, rows), :]
            return joined[w].at[pl.ds(row0, rows), pl.ds(core * pc, pc)]

        def to_sibling(w, src, core, row0, rows):
            return pltpu.make_async_remote_copy(src_ref=src, dst_ref=window(w, core, row0, rows), send_sem=j_send.at[w],
                                                recv_sem=j_recv.at[w], device_id=_sibling(), device_id_type=MESH)

        @pl.when(i == 0)
        def _():
            exchange.start()
            st_ref[...] = jnp.zeros_like(st_ref)

        for w in range(nj):
            pr, pc, rb, every, _ = geometry[w]

            @pl.when(i % every == 0)
            def _(w=w, rb=rb, every=every):
                p_ref, r1_ref, r2_ref, r3_ref = sums[w]
                row0 = pl.multiple_of((i // every) * rb, rb)
                rows = stages[w].at[pl.ds(row0, rb), :]
                rows[...] = ((p_ref[0] + r1_ref[0].astype(F32)) + r2_ref[0].astype(F32)) + r3_ref[0].astype(F32)
                pltpu.make_async_copy(rows, window(w, c, row0, rb), j_local.at[w]).start()
                to_sibling(w, rows, c, row0, rb).start()

        dh1 = _dot_nt(du_ref[:, 0:IN_SHARD], win_ref[0])
        for j in range(1, N_CHIPS):
            dh1 = dh1 + _dot_nt(du_ref[:, j * IN_SHARD:(j + 1) * IN_SHARD], win_ref[j])
        xv = x_ref[...]
        rstd = lax.rsqrt(jnp.mean(xv * xv, axis=-1, keepdims=True) + EPS)
        xh = xv * rstd
        st_ref[0:1, :] += _colsum(dh1 * xh)
        dxh = dh1 * g1_ref[...]
        gx_ref[...] = dx2_ref[...] + rstd * (dxh - xh * jnp.mean(dxh * xh, axis=-1, keepdims=True))

        @pl.when(i == nt - 1)
        def _():
            exchange.wait()
            for w in range(nj):
                pr = geometry[w][0]
                pltpu.make_async_copy(stages[w], window(w, c, 0, pr), j_local.at[w]).wait()
                to_sibling(w, stages[w], 1 - c, 0, pr).wait()

    def tok(cols):
        return pl.BlockSpec((tm, cols), lambda i, cc: (i, 0))

    def partial(w, off):
        pr, pc, rb, every, _ = geometry[w]
        return pl.BlockSpec((1, rb, pc), lambda i, cc: ((cc[1] + off) % N_CHIPS, i // every, 0))

    sum_specs, sum_operands = [], []
    for w, (_, _, own, arrived) in enumerate(joins):
        sum_specs += [partial(w, off) for off in range(N_CHIPS)]
        sum_operands += [own, arrived, arrived, arrived]
    dma = pltpu.SemaphoreType.DMA
    outs = pl.pallas_call(
        body, name="in_bwd",
        grid_spec=pltpu.PrefetchScalarGridSpec(
            num_scalar_prefetch=1, grid=(nt,),
            in_specs=[tok(IN_COLS), pl.BlockSpec(w_in_g.shape, lambda i, cc: (0, 0, 0)), tok(d), tok(d),
                      pl.BlockSpec((1, d), lambda i, cc: (0, 0))] + sum_specs + [ANY] * npart,
            out_specs=[tok(d), pl.BlockSpec((TILE_ROWS, d), lambda i, cc: (0, 0))] + [ANY] * (npart + nj),
            scratch_shapes=[pltpu.VMEM((g[0], g[1]), F32) for g in geometry]
            + [dma((npart, 3)), dma((npart, 3)), dma((nj,)), dma((nj,)), dma((nj,))]),
        out_shape=[jax.ShapeDtypeStruct((t, d), F32), jax.ShapeDtypeStruct((TILE_ROWS, d), F32)]
        + [jax.ShapeDtypeStruct(a.shape, a.dtype) for a in parts]
        + [jax.ShapeDtypeStruct(shape, F32) for _, shape, _, _ in joins],
        compiler_params=_params(dimension_semantics=("arbitrary",)),
    )(core_chip, dub, w_in_g, x, dx2, g1, *sum_operands, *parts)
    return outs[:2], outs[2:2 + npart], outs[2 + npart:]


WGRAD_GEOMETRY = {
    "in": (512, IN_SHARD, lambda s, h: h, lambda s, h: s),
    "mlp_in": (512, D_MODEL, lambda s, h: h, lambda s, h: s),
    "mlp_out": (512, D_MODEL, lambda s, h: 2 * s + h, lambda s, h: 0),
    "out": (384, 512, lambda s, h: s, lambda s, h: h),
}
K_CHUNK = 512
TOKEN_STREAMS = 2


def _sibling():
    x, y, c = _position()
    return (x, y, 1 - c)


def _wgrad(a, b, tag, core_chip, packs=(), parts=()):
    t = a.shape[0]
    pr, pc, a_blk, b_blk = WGRAD_GEOMETRY[tag]
    ns = TOKEN_STREAMS
    ts = t // ns
    kc = min(K_CHUNK, ts)
    mine = N_CHIPS
    riding = len(packs)
    npart = len(parts)
    assert not (riding and npart)

    def body(cc_ref, *rest):
        a_refs, b_refs, rest = rest[:ns], rest[ns:2 * ns], rest[2 * ns:]
        if riding:
            pack_refs, (land_ref, p_ref, pb_ref), rest = rest[:riding], rest[riding:riding + 3], rest[riding + 3:]
            all_refs, (stage, rbuf, send_sems, recv_sems, rsem), g_sems = rest[:riding], rest[riding:riding + 5], rest[riding + 5:]
            gathers = [_PackGather(pack_refs[n], all_refs[n], *g_sems[3 * n:3 * n + 3]) for n in range(riding)]
        elif npart:
            part_refs, (land_ref, p_ref, pb_ref), rest = rest[:npart], rest[npart:npart + 3], rest[npart + 3:]
            arrived_refs, (stage, rbuf, send_sems, recv_sems, rsem, x_send, x_recv) = rest[:npart], rest[npart:]
            exchange = _PartialExchange(part_refs, arrived_refs, x_send, x_recv)
        else:
            land_ref, p_ref, pb_ref, stage, rbuf, send_sems, recv_sems, rsem = rest
        ph, s = pl.program_id(0), pl.program_id(1)
        if riding:
            @pl.when((ph == 0) & (s == 0))
            def _():
                for gather in gathers:
                    gather.start()

            @pl.when((ph == 1) & (s == N_CHIPS - 2))
            def _():
                for gather in gathers:
                    gather.hand_over()
        if npart:
            @pl.when((ph == 0) & (s == 0))
            def _():
                exchange.start()
        def push(k):
            return pltpu.make_async_remote_copy(src_ref=stage.at[k], dst_ref=land_ref.at[k], send_sem=send_sems.at[k],
                                                recv_sem=recv_sems.at[k], device_id=_sibling(), device_id_type=MESH)

        def landed():
            return pltpu.make_async_copy(land_ref.at[s], rbuf, rsem)

        @pl.when(ph == 1)
        def _():
            push(s).wait_recv()
            landed().start()

        slot = jnp.where(ph == 0, s, mine)
        acc = stage.at[slot]
        chunks = [(a_ref, b_ref, k) for a_ref, b_ref in zip(a_refs, b_refs) for k in range(0, ts, kc)]
        for n, (a_ref, b_ref, k) in enumerate(chunks):
            part = _dot_tn(a_ref[k:k + kc, :], b_ref[k:k + kc, :])
            if n == 0:
                acc[...] = part
            else:
                acc[...] += part

        @pl.when(ph == 0)
        def _():
            push(s).start()

        @pl.when(ph == 1)
        def _():
            landed().wait()
            p = stage[mine] + rbuf[...]
            p_ref[0] = p
            pb_ref[0] = p.astype(BF16)

        @pl.when((ph == 1) & (s == N_CHIPS - 1))
        def _():
            for k in range(N_CHIPS):
                push(k).wait_send()
            for gather in (gathers if riding else ()):
                gather.finish()
            if npart:
                exchange.wait()

    def half(ph, cc):
        return jnp.where(ph == 0, 1 - cc[0], cc[0])

    def out_slot(ph, s, cc):
        return (jnp.where(ph == 0, 0, s), 0, 0)

    piece = jax.ShapeDtypeStruct((N_CHIPS, pr, pc), F32)
    in_specs = [pl.BlockSpec((ts, pr), lambda ph, s, cc, n=n: (n, a_blk(s, half(ph, cc)))) for n in range(ns)]
    in_specs += [pl.BlockSpec((ts, pc), lambda ph, s, cc, n=n: (n, b_blk(s, half(ph, cc)))) for n in range(ns)]
    out_specs = [ANY, pl.BlockSpec((1, pr, pc), out_slot), pl.BlockSpec((1, pr, pc), out_slot)]
    out_shape = [piece, piece, jax.ShapeDtypeStruct((N_CHIPS, pr, pc), BF16)]
    scratch = [pltpu.VMEM((N_CHIPS + 1, pr, pc), F32), pltpu.VMEM((pr, pc), F32),
               pltpu.SemaphoreType.DMA((N_CHIPS,)), pltpu.SemaphoreType.DMA((N_CHIPS,)), pltpu.SemaphoreType.DMA]
    operands = [a] * ns + [b] * ns
    for pack in packs:
        in_specs.append(pl.BlockSpec(pack.shape, lambda ph, s, cc, nd=pack.ndim: (0,) * nd))
        out_specs.append(ANY)
        out_shape.append(jax.ShapeDtypeStruct((N_DEVICES,) + pack.shape, pack.dtype))
        operands.append(pack)
    for pack in packs:
        scratch += _PackGather.semaphores()
    if npart:
        in_specs += [ANY] * npart
        out_specs += [ANY] * npart
        out_shape += [jax.ShapeDtypeStruct(p.shape, p.dtype) for p in parts]
        scratch += [pltpu.SemaphoreType.DMA((npart, 3)), pltpu.SemaphoreType.DMA((npart, 3))]
        operands += list(parts)
    return pl.pallas_call(
        body, name="wgrad_" + tag,
        grid_spec=pltpu.PrefetchScalarGridSpec(
            num_scalar_prefetch=1, grid=(2, N_CHIPS), in_specs=in_specs, out_specs=out_specs, scratch_shapes=scratch),
        out_shape=out_shape,
        compiler_params=_params(dimension_semantics=("arbitrary", "arbitrary")),
    )(core_chip, *operands)[1:]


def _other_chips(x, y):
    return [(1 - x, y), (x, 1 - y), (1 - x, 1 - y)]


class _ShardGather:
    PAIRS = 9

    def __init__(self, outs, send_sems, recv_sems):
        self.outs, self.send_sems, self.recv_sems = outs, send_sems, recv_sems
        x, y, c = _position()
        self.c, self.j = c, 2 * x + y
        self.sibling = (x, y, 1 - c)
        self.chips = _other_chips(x, y)

    def _chip(self, k):
        px, py = self.chips[k]
        return 2 * px + py

    def _half(self, w, chip, which):
        hr = self.outs[w].shape[1] // 2
        return self.outs[w].at[chip, pl.ds(which * hr, hr), :]

    def _quarter(self, w, chip, q):
        qr = self.outs[w].shape[1] // 4
        return self.outs[w].at[chip, pl.ds(self.c * 2 * qr + q * qr, qr), :]

    def _copy(self, ref, w, pair, to, src=None):
        return pltpu.make_async_remote_copy(src_ref=ref if src is None else src, dst_ref=ref, send_sem=self.send_sems.at[w, pair],
                                            recv_sem=self.recv_sems.at[w, pair], device_id=to, device_id_type=MESH)

    def direct(self, w, k, q, src=None):
        return self._copy(self._quarter(w, self.j, q), w, 2 * k + q, (*self.chips[k], self.c), src)

    def direct_landed(self, w, k, q):
        return self._copy(self._quarter(w, self._chip(k), q), w, 2 * k + q, (*self.chips[k], self.c))

    def pass_on(self, w, q):
        return self._copy(self._quarter(w, self._chip(q), q), w, 4 + q, (*self.chips[1 - q], self.c))

    def passed_landed(self, w, q):
        return self._copy(self._quarter(w, self._chip(2), q), w, 4 + q, (*self.chips[1 - q], self.c))

    def hand_over(self, w, k):
        return self._copy(self._half(w, self._chip(k), self.c), w, 6 + k, self.sibling)

    def handed(self, w, k):
        return self._copy(self._half(w, self._chip(k), 1 - self.c), w, 6 + k, self.sibling)

    def start_direct(self, w, src_half=None):
        qr = self.outs[w].shape[1] // 4
        for k, q in ((0, 0), (1, 1), (0, 1), (1, 0)):
            self.direct(w, k, q, None if src_half is None else src_half.at[pl.ds(q * qr, qr), :]).start()

    def start_pass_on(self, w):
        for q in (0, 1):
            self.direct_landed(w, q, q).wait_recv()
            self.pass_on(w, q).start()

    def start_hand_over(self, w, diagonal):
        if diagonal:
            for q in (0, 1):
                self.passed_landed(w, q).wait_recv()
            self.hand_over(w, 2).start()
        else:
            for k in (0, 1):
                self.direct_landed(w, k, 1 - k).wait_recv()
                self.hand_over(w, k).start()

    def finish(self, w):
        for k in range(3):
            self.handed(w, k).wait_recv()
            self.hand_over(w, k).wait_send()
        for q in (0, 1):
            self.pass_on(w, q).wait_send()
            for k in (0, 1):
                self.direct(w, k, q).wait_send()


def _gather_first(w_in, w_out, w1, w2, conv_w, rconv_w, w_a, w_x):
    bigs = (w_in, w_out, w1, w2)
    convs = (conv_w, rconv_w)
    heads = (w_a, w_x)
    nb, nc = len(bigs), len(convs)

    def body(win_ref, wout_hbm, w1_hbm, w2_hbm, cw_ref, rw_ref, wa_ref, wx_ref, gin, gout, g1, g2, gcw, grw, bda, bdx,
             st_in, st_out, st_1, st_2, f_out, f_1, f_2, st_cw, st_rw, send_sems, recv_sems, sm_send, sm_recv, local_sems,
             load_sems):
        stages = (st_in, st_out, st_1, st_2)
        outs = (gin, gout, g1, g2)
        conv_stages, conv_outs = (st_cw, st_rw), (gcw, grw)
        plan = _ShardGather(outs[:1], send_sems, recv_sems)
        j, c = plan.j, plan.c
        local = [pltpu.make_async_copy(stages[w], outs[w].at[j], local_sems.at[w]) for w in range(nb)]
        loads = [pltpu.make_async_copy(src, dst, load_sems.at[n])
                 for n, (src, dst) in enumerate(((wout_hbm, f_out), (w1_hbm, f_1), (w2_hbm, f_2)))]

        def columns(n, chip):
            width = convs[n].shape[1]
            return conv_outs[n].at[:, pl.ds(chip * width, width)]

        local += [pltpu.make_async_copy(conv_stages[n], columns(n, j), local_sems.at[nb + n]) for n in range(nc)]

        def small_copy(k, n, landed=False):
            px, py = plan.chips[k]
            return pltpu.make_async_remote_copy(
                src_ref=conv_stages[n], dst_ref=columns(n, 2 * px + py if landed else j), send_sem=sm_send.at[k, n],
                recv_sem=sm_recv.at[k, n], device_id=(px, py, c), device_id_type=MESH)

        for cp in loads:
            cp.start()
        hr = w_in.shape[0] // 2
        st_in[...] = win_ref[...].astype(BF16)
        plan.start_direct(0, st_in.at[pl.ds(c * hr, hr), :])
        for src, st in zip((cw_ref, rw_ref), conv_stages):
            st[...] = jnp.zeros_like(st)
            st[0:src.shape[0], :] = src[...]
        for k in range(3):
            for n in range(nc):
                small_copy(k, n).start()
        for src, bd in ((wa_ref, bda), (wx_ref, bdx)):
            bd[...] = jnp.zeros_like(bd)
            for h in range(N_HEADS):
                q = h % (BD // HEAD_DIM)
                bd[h // (BD // HEAD_DIM), q * HEAD_DIM:(q + 1) * HEAD_DIM, q * HEAD_DIM:(q + 1) * HEAD_DIM] = src[h].astype(BF16)
        for cp, full, st in zip(loads, (f_out, f_1, f_2), stages[1:]):
            cp.wait()
            st[...] = full[...].astype(BF16)
        for cp in local:
            cp.start()
        plan.start_pass_on(0)
        plan.start_hand_over(0, diagonal=False)
        plan.start_hand_over(0, diagonal=True)
        for k in range(3):
            for n in range(nc):
                small_copy(k, n, landed=True).wait_recv()
                small_copy(k, n).wait_send()
        plan.finish(0)
        for cp in local:
            cp.wait()

    def gathered(a, dtype):
        return jax.ShapeDtypeStruct((N_CHIPS,) + a.shape, dtype)

    return pl.pallas_call(
        body, name="gather_first",
        in_specs=[VMEM] + [ANY] * (nb - 1) + [VMEM] * (nc + len(heads)),
        out_specs=[ANY] * (nb + nc) + [VMEM] * len(heads),
        out_shape=[gathered(a, BF16) for a in bigs]
        + [jax.ShapeDtypeStruct((TILE_ROWS, N_CHIPS * a.shape[1]), F32) for a in convs]
        + [jax.ShapeDtypeStruct((N_BD, BD, BD), BF16) for _ in heads],
        scratch_shapes=[pltpu.VMEM(a.shape, BF16) for a in bigs] + [pltpu.VMEM(a.shape, F32) for a in bigs[1:]]
        + [pltpu.VMEM((TILE_ROWS, a.shape[1]), F32) for a in convs]
        + [pltpu.SemaphoreType.DMA((1, _ShardGather.PAIRS)), pltpu.SemaphoreType.DMA((1, _ShardGather.PAIRS)),
           pltpu.SemaphoreType.DMA((3, nc)), pltpu.SemaphoreType.DMA((3, nc)), pltpu.SemaphoreType.DMA((nb + nc,)),
           pltpu.SemaphoreType.DMA((nb - 1,))],
        compiler_params=_params(),
    )(*bigs, *convs, *heads)


class _PartialExchange:
    def __init__(self, parts, arrived, send_sems, recv_sems):
        self.parts, self.arrived, self.send_sems, self.recv_sems = parts, arrived, send_sems, recv_sems
        x, y, c = _position()
        self.c, self.j = c, 2 * x + y
        self.chips = _other_chips(x, y)

    def _copy(self, w, k, slot):
        px, py = self.chips[k]
        return pltpu.make_async_remote_copy(
            src_ref=self.parts[w].at[2 * px + py], dst_ref=self.arrived[w].at[slot], send_sem=self.send_sems.at[w, k],
            recv_sem=self.recv_sems.at[w, k], device_id=(px, py, self.c), device_id_type=MESH)

    def start(self):
        for w in range(len(self.parts)):
            for k in range(3):
                self._copy(w, k, self.j).start()

    def wait(self):
        for w in range(len(self.parts)):
            for k in range(3):
                px, py = self.chips[k]
                self._copy(w, k, 2 * px + py).wait()


class _PackGather:
    def __init__(self, p_ref, all_ref, send_sems, recv_sems, local_sem):
        self.p_ref, self.all_ref, self.send_sems, self.recv_sems, self.local_sem = p_ref, all_ref, send_sems, recv_sems, local_sem
        x, y, c = _position()
        self.me, self.sibling, self.c = (x, y, c), (x, y, 1 - c), c
        self.chips = _other_chips(x, y)

    @staticmethod
    def semaphores():
        return [pltpu.SemaphoreType.DMA((7,)), pltpu.SemaphoreType.DMA((7,)), pltpu.SemaphoreType.DMA]

    def _copy(self, k, block, to, from_pack=False):
        px, py, pc = block
        slot = self.all_ref.at[4 * px + 2 * py + pc]
        return pltpu.make_async_remote_copy(src_ref=self.p_ref if from_pack else slot, dst_ref=slot, send_sem=self.send_sems.at[k],
                                            recv_sem=self.recv_sems.at[k], device_id=to, device_id_type=MESH)

    def _mine(self):
        x, y, c = self.me
        return pltpu.make_async_copy(self.p_ref, self.all_ref.at[4 * x + 2 * y + c], self.local_sem)

    def _first(self):
        return [self._copy(0, self.me, self.sibling, True)] + [
            self._copy(1 + k, self.me, (*chip, self.c), True) for k, chip in enumerate(self.chips)]

    def _passed(self):
        return [self._copy(4 + k, (*chip, self.c), self.sibling) for k, chip in enumerate(self.chips)]

    def start(self):
        self._mine().start()
        for cp in self._first():
            cp.start()

    def hand_over(self):
        for k, chip in enumerate(self.chips):
            self._copy(1 + k, (*chip, self.c), self.me).wait_recv()
            self._passed()[k].start()

    def finish(self):
        self._copy(0, self.sibling, self.me).wait_recv()
        for k, chip in enumerate(self.chips):
            self._copy(4 + k, (*chip, 1 - self.c), self.me).wait_recv()
        for cp in self._first() + self._passed():
            cp.wait_send()
        self._mine().wait()


class _DirectGather:
    def __init__(self, p_ref, all_ref, send_sems, recv_sems, local_sem):
        self.p_ref, self.all_ref, self.send_sems, self.recv_sems, self.local_sem = p_ref, all_ref, send_sems, recv_sems, local_sem
        self.me = _position()

    semaphores = _PackGather.semaphores

    def _peer(self, r):
        x, y, c = self.me
        return ((1 - x) if r & 4 else x, (1 - y) if r & 2 else y, (1 - c) if r & 1 else c)

    def _copy(self, r, slot_of):
        px, py, pc = slot_of
        return pltpu.make_async_remote_copy(src_ref=self.p_ref, dst_ref=self.all_ref.at[4 * px + 2 * py + pc],
                                            send_sem=self.send_sems.at[r - 1], recv_sem=self.recv_sems.at[r - 1],
                                            device_id=self._peer(r), device_id_type=MESH)

    def _mine(self):
        x, y, c = self.me
        return pltpu.make_async_copy(self.p_ref, self.all_ref.at[4 * x + 2 * y + c], self.local_sem)

    def start(self):
        self._mine().start()
        for r in range(1, N_DEVICES):
            self._copy(r, self.me).start()

    def finish(self):
        for r in range(1, N_DEVICES):
            self._copy(r, self._peer(r)).wait()
        self._mine().wait()


def _adamw(w, g, m, v):
    m = ADAM_B1 * m + (1.0 - ADAM_B1) * g
    v = ADAM_B2 * v + (1.0 - ADAM_B2) * (g * g)
    m_hat = m / ADAM_BC1
    v_hat = v / ADAM_BC2
    delta = -ADAM_LR * (m_hat / (jnp.sqrt(v_hat) + ADAM_EPS) + ADAM_WD * w)
    return delta, m, v


JOIN_SUB = 4


def _join(tag, shard_shape, part, arrived, core_chip, block=None):
    pr, pc = WGRAD_GEOMETRY[tag][:2]
    rb = pr // JOIN_SUB
    by_rows = shard_shape[1] == pc
    riding = block is not None

    def body(cc_ref, p_ref, r1_ref, r2_ref, r3_ref, *rest):
        if riding:
            blk_ref, g_ref, all_ref, stage, send_sems, recv_sems, local_sems, b_send, b_recv, b_local = rest
            gather = _DirectGather(blk_ref, all_ref, b_send, b_recv, b_local)
        else:
            g_ref, stage, send_sems, recv_sems, local_sems = rest
        i = pl.program_id(0)
        c = cc_ref[0]
        if riding:
            @pl.when(i == 0)
            def _():
                gather.start()

        def window(core, k):
            if by_rows:
                return g_ref.at[pl.ds((core * JOIN_SUB + k) * rb, rb), :]
            return g_ref.at[pl.ds(k * rb, rb), pl.ds(core * pc, pc)]

        def keep(k):
            return pltpu.make_async_copy(stage.at[k], window(c, k), local_sems.at[k])

        def push(k):
            return pltpu.make_async_remote_copy(src_ref=stage.at[k], dst_ref=window(c, k), send_sem=send_sems.at[k],
                                                recv_sem=recv_sems.at[k], device_id=_sibling(), device_id_type=MESH)

        def pushed(k):
            return pltpu.make_async_remote_copy(src_ref=stage.at[k], dst_ref=window(1 - c, k), send_sem=send_sems.at[k],
                                                recv_sem=recv_sems.at[k], device_id=_sibling(), device_id_type=MESH)

        stage[i] = ((p_ref[0] + r1_ref[0].astype(F32)) + r2_ref[0].astype(F32)) + r3_ref[0].astype(F32)
        keep(i).start()
        push(i).start()

        @pl.when(i == JOIN_SUB - 1)
        def _():
            for k in range(JOIN_SUB):
                keep(k).wait()
                push(k).wait_send()
                pushed(k).wait_recv()
            if riding:
                gather.finish()

    def partial(off):
        return pl.BlockSpec((1, rb, pc), lambda i, cc: ((cc[1] + off) % N_CHIPS, i, 0))

    in_specs = [partial(0), partial(1), partial(2), partial(3)]
    out_specs = [ANY]
    out_shape = [jax.ShapeDtypeStruct(shard_shape, F32)]
    scratch = [pltpu.VMEM((JOIN_SUB, rb, pc), F32), pltpu.SemaphoreType.DMA((JOIN_SUB,)),
               pltpu.SemaphoreType.DMA((JOIN_SUB,)), pltpu.SemaphoreType.DMA((JOIN_SUB,))]
    operands = [part, arrived, arrived, arrived]
    if riding:
        in_specs.append(pl.BlockSpec(block.shape, lambda i, cc: (0, 0)))
        out_specs.append(ANY)
        out_shape.append(jax.ShapeDtypeStruct((N_DEVICES,) + block.shape, block.dtype))
        scratch += _DirectGather.semaphores()
        operands.append(block)
    outs = pl.pallas_call(
        body, name="join_" + tag,
        grid_spec=pltpu.PrefetchScalarGridSpec(
            num_scalar_prefetch=1, grid=(JOIN_SUB,), in_specs=in_specs, out_specs=out_specs, scratch_shapes=scratch),
        out_shape=out_shape,
        compiler_params=_params(dimension_semantics=("arbitrary",)),
    )(core_chip, *operands)
    return outs if riding else outs[0]


def _adamw_big(w, g, m, v, name):
    rows, cols = w.shape
    rb = ADAMW_ROWS if rows % ADAMW_ROWS == 0 else rows

    def body(w_ref, g_ref, m_ref, v_ref, go_ref, d_ref, nm_ref, nv_ref):
        g = g_ref[...]
        go_ref[...] = g
        d_ref[...], nm_ref[...], nv_ref[...] = _adamw(w_ref[...], g, m_ref[...], v_ref[...])

    spec = pl.BlockSpec((rb, cols), lambda i: (i, 0))
    return pl.pallas_call(
        body, name=name, grid=(rows // rb,), in_specs=[spec] * 4, out_specs=[spec] * 4,
        out_shape=[jax.ShapeDtypeStruct(w.shape, F32)] * 4,
        compiler_params=_params(dimension_semantics=("arbitrary",)),
    )(w, g, m, v)


SMALL_VECTORS = {
    "norm_mix_g": (PK_MIX_G, D_MODEL), "rnn_conv_b": (PK_RCONV_B, LRU_W), "b_a": (PK_B_A, LRU_W), "b_x": (PK_B_X, LRU_W),
    "lru_lambda": (PK_LAMBDA, LRU_W), "g_norm_conv": (PK_G_NORM_CONV, CONV_W), "g_norm_rnn": (PK_G_NORM_RNN, LRU_W),
    "norm_mlp_g": (PK_MLP_G, D_MODEL), "final_norm_g": (PK_FINAL_G, D_MODEL),
}
SMALL_MATRICES = ("w_a", "w_x")


def _small_step(vec_packs, mat_packs, mix_g_blocks, p):
    vec_rows, cols = vec_packs.shape[1:]
    conv_rows, cshard = p["conv_w"].shape
    rconv_rows, rshard = p["rnn_conv_w"].shape
    names = list(SMALL_VECTORS) + list(SMALL_MATRICES) + ["conv_w", "rnn_conv_w"]
    shapes = ([(1, width) for _, width in SMALL_VECTORS.values()] + [mat_packs.shape[2:]] * len(SMALL_MATRICES)
              + [(conv_rows, cshard), (rconv_rows, rshard)])
    kinds = ("", "m_", "v_")
    params = [p[pre + n].reshape(1, -1) if n in SMALL_VECTORS else p[pre + n] for pre in kinds for n in names]

    def body(vec_ref, mat_ref, blk_ref, *rest):
        wmv = [dict(zip(names, rest[k * len(names):(k + 1) * len(names)])) for k in range(3)]
        loss_ref, rest = rest[3 * len(names)], rest[3 * len(names) + 1:]
        leaves, (g_ref, w_ref, m_ref, v_ref) = [rest[k * len(names):(k + 1) * len(names)] for k in range(4)], rest[4 * len(names):]
        total = vec_ref[0]
        mats = mat_ref[0].astype(F32)
        late = blk_ref[0]
        for k in range(1, N_DEVICES):
            total = total + vec_ref[k]
            mats = mats + mat_ref[k].astype(F32)
            late = late + blk_ref[k]
        g_ref[0:vec_rows, :] = total
        g_ref[vec_rows:, :] = late
        g = g_ref[...]
        loss_ref[...] = g[PK_LOSS:PK_LOSS + 1, 0:1]

        for pack_ref, given in zip((w_ref, m_ref, v_ref), wmv):
            pack_ref[...] = jnp.zeros_like(pack_ref)
            for name, (row, width) in SMALL_VECTORS.items():
                pack_ref[row:row + 1, 0:width] = given[name][...]

        x, y, _ = _position()
        j = 2 * x + y
        cblk = total[0:TILE_ROWS, :]
        rblk = total[PK_RCONV_W:PK_RCONV_W + TILE_ROWS, :]
        cg = cblk[:, 0:cshard]
        rg = rblk[:, 0:rshard]
        for k in range(1, N_CHIPS):
            cg = jnp.where(j == k, cblk[:, k * cshard:(k + 1) * cshard], cg)
            rg = jnp.where(j == k, rblk[:, k * rshard:(k + 1) * rshard], rg)
        cg = cg[PK_CONV_W:PK_CONV_W + conv_rows, :]
        rg = rg[0:rconv_rows, :]

        def step(name, grad):
            return (grad,) + _adamw(wmv[0][name][...], grad, wmv[1][name][...], wmv[2][name][...])

        packs = (g,) + _adamw(w_ref[...], g, m_ref[...], v_ref[...])
        matrices = [step(name, mats[n]) for n, name in enumerate(SMALL_MATRICES)]
        convs, rconvs = step("conv_w", cg), step("rnn_conv_w", rg)
        for kind in range(4):
            out = dict(zip(names, leaves[kind]))
            for name, (row, width) in SMALL_VECTORS.items():
                out[name][...] = packs[kind][row:row + 1, 0:width]
            for n, name in enumerate(SMALL_MATRICES):
                out[name][...] = matrices[n][kind]
            out["conv_w"][...] = convs[kind]
            out["rnn_conv_w"][...] = rconvs[kind]

    outs = pl.pallas_call(
        body, name="small_grads_step", in_specs=[VMEM] * (3 + len(params)), out_specs=[VMEM] * (1 + 4 * len(names)),
        out_shape=[jax.ShapeDtypeStruct((1, 1), F32)] + [jax.ShapeDtypeStruct(sh, F32) for sh in shapes] * 4,
        scratch_shapes=[pltpu.VMEM((PK_ROWS, cols), F32)] * 4,
        compiler_params=_params(),
    )(vec_packs, mat_packs, mix_g_blocks, *params)
    return outs[0], [dict(zip(names, outs[1 + k * len(names):1 + (k + 1) * len(names)])) for k in range(4)]


_NAMES = ['norm_mix_g', 'w_in', 'conv_w', 'rnn_conv_w', 'rnn_conv_b', 'w_a', 'b_a', 'w_x', 'b_x', 'lru_lambda',
          'g_norm_conv', 'g_norm_rnn', 'w_out', 'norm_mlp_g', 'w_mlp_in', 'w_mlp_out', 'final_norm_g']


def kernel(x, norm_mix_g, w_in, conv_w, rnn_conv_w, rnn_conv_b, w_a, b_a, w_x, b_x, lru_lambda, g_norm_conv, g_norm_rnn, w_out, norm_mlp_g, w_mlp_in, w_mlp_out, final_norm_g, loss_target, m_norm_mix_g, m_w_in, m_conv_w, m_rnn_conv_w, m_rnn_conv_b, m_w_a, m_b_a, m_w_x, m_b_x, m_lru_lambda, m_g_norm_conv, m_g_norm_rnn, m_w_out, m_norm_mlp_g, m_w_mlp_in, m_w_mlp_out, m_final_norm_g, v_norm_mix_g, v_w_in, v_conv_w, v_rnn_conv_w, v_rnn_conv_b, v_w_a, v_b_a, v_w_x, v_b_x, v_lru_lambda, v_g_norm_conv, v_g_norm_rnn, v_w_out, v_norm_mlp_g, v_w_mlp_in, v_w_mlp_out, v_final_norm_g):
    args = dict(locals())
    p = {}
    for n in _NAMES:
        for pre in ("", "m_", "v_"):
            a = args[pre + n]
            p[pre + n] = a[0] if a.ndim >= 3 else a
    xs = x[0]
    target = loss_target[0]
    core_chip = jnp.stack([lax.axis_index("c"), 2 * lax.axis_index("x") + lax.axis_index("y")]).astype(jnp.int32)

    w_in_g, w_out_g, w1_g, w2_g, conv_full, rconv_full, wa_bd, wx_bd = _gather_first(
        p["w_in"], p["w_out"], p["w_mlp_in"], p["w_mlp_out"], p["conv_w"], p["rnn_conv_w"], p["w_a"], p["w_x"])
    gf = p["final_norm_g"].reshape(1, -1)
    lru = (wa_bd, p["b_a"], wx_bd, p["b_x"], p["lru_lambda"], p["g_norm_conv"], p["g_norm_rnn"])

    (u, h1b, xr, hs, c3, yb, gates), (w_out_g, w1_g, w2_g) = _fwd_mix(
        xs, p["norm_mix_g"], w_in_g, conv_full, rconv_full, p["rnn_conv_b"], *lru, (w_out_g, w1_g, w2_g))
    zb, dpb, h2b, dx3b, dx2, dx2b, dy, st_mlp = _mlp_fwd_bwd(
        xs, yb, w_out_g.reshape(-1, D_MODEL), w1_g, w2_g.reshape(-1, D_MODEL), p["norm_mlp_g"], gf, target)

    part_out = _wgrad(yb, dx2b, "out", core_chip)
    *part_1, arrived_out = _wgrad(h2b, dpb, "mlp_in", core_chip, parts=(part_out[1],))
    part_2 = _wgrad(zb, dx3b, "mlp_out", core_chip)
    (dub, vec_pack, mat_pack), (arrived_1, arrived_2) = _mix_bwd(
        dy, u, xr, hs, c3, gates, conv_full, rconv_full, wa_bd, wx_bd, p["lru_lambda"], p["g_norm_conv"], p["g_norm_rnn"],
        st_mlp, (part_1[1], part_2[1]))
    arrived_mlp = (arrived_out, arrived_1, arrived_2)
    *part_in, vec_packs, mat_packs = _wgrad(h1b, dub, "in", core_chip, packs=(vec_pack, mat_pack))
    early = (("w_out", "out", part_out, arrived_mlp[0]), ("w_mlp_in", "mlp_in", part_1, arrived_mlp[1]),
             ("w_mlp_out", "mlp_out", part_2, arrived_mlp[2]))
    (grad_x, st_in), arrived_in, joined = _in_bwd(
        dub, w_in_g, xs, dx2, p["norm_mix_g"], (part_in[1],),
        [(tag, p[n].shape, part[0], arrived) for n, tag, part, arrived in early], core_chip)
    g_in, mix_g_blocks = _join("in", p["w_in"].shape, part_in[0], arrived_in[0], core_chip, st_in)
    big = {}
    for n, tag, g in [(n, tag, g) for (n, tag, _, _), g in zip(early, joined)] + [("w_in", "in", g_in)]:
        big[n] = _adamw_big(p[n], g, p["m_" + n], p["v_" + n], "adamw_" + tag)

    loss, outs = _small_step(vec_packs, mat_packs, mix_g_blocks, p)
    for kind, o in enumerate(outs):
        o["final_norm_g"] = o["final_norm_g"].reshape(-1)
        for n in SMALL_MATRICES + ("conv_w", "rnn_conv_w"):
            o[n] = o[n][None]
        for n in ("w_in", "w_out", "w_mlp_in", "w_mlp_out"):
            o[n] = big[n][kind][None]
    loss = loss.reshape(())
    return (loss, grad_x[None], *[o[n] for o in outs for n in _NAMES])
```

```python
import functools
import math

import jax
import jax.numpy as jnp
from jax import lax
from jax.experimental import pallas as pl
from jax.experimental.pallas import tpu as pltpu

F32 = jnp.float32
BF16 = jnp.bfloat16
MESH = pl.DeviceIdType.MESH
ANY = pl.BlockSpec(memory_space=pl.ANY)
VMEM = pl.BlockSpec(memory_space=pltpu.VMEM)

EPS = 1e-6
LRU_C = 8.0
D_MODEL = 1024
CONV_W = 512
LRU_W = 1024
IN_COLS = 3 * CONV_W + 2 * LRU_W
IN_SHARD = IN_COLS // 4
N_CHIPS = 4
N_DEVICES = 8
BD = 256
N_BD = LRU_W // BD

ADAM_LR = 0.001
ADAM_B1 = 0.9
ADAM_B2 = 0.999
ADAM_EPS = 1e-08
ADAM_WD = 0.01
ADAM_STEP = 10
ADAM_BC1 = 1.0 - ADAM_B1 ** ADAM_STEP
ADAM_BC2 = 1.0 - ADAM_B2 ** ADAM_STEP

TILE_ROWS, LANES = 8, 128
TOKEN_TILE = 256
MATMUL_TOKEN_TILE = 512
ADAMW_ROWS = 256
VMEM_LIMIT = 56 * 1024 * 1024

PK_G_NORM_RNN, PK_RCONV_B, PK_B_A, PK_B_X, PK_LAMBDA, PK_CONV_W = 0, 1, 2, 3, 4, 5
PK_RCONV_W, PK_G_NORM_CONV = 8, 12
PK_MIX_ROWS = 16
PK_FINAL_G, PK_MLP_G, PK_LOSS = 16, 17, 18
PK_MLP_ROWS = 8
PK_MIX_G = 24
PK_ROWS = 32
N_HEADS, HEAD_DIM = 16, 64


def _params(**kw):
    return pltpu.CompilerParams(vmem_limit_bytes=VMEM_LIMIT, **kw)


def _position():
    x, y, c = lax.axis_index("x"), lax.axis_index("y"), lax.axis_index("c")
    return x, y, c


def _sigmoid(v):
    return 1.0 / (1.0 + jnp.exp(-v))


def _one_minus_square(log_a, a):
    v = 2.0 * log_a
    series = -v * (1.0 + v * (0.5 + v * (1.0 / 6.0)))
    return jnp.where(v > -0.01, series, 1.0 - a * a)


_GELU_C = math.sqrt(2.0 / math.pi)
_GELU_K = 0.044715


def _gelu_and_grad(g):
    th = jnp.tanh(_GELU_C * (g + _GELU_K * g * g * g))
    gelu = 0.5 * g * (1.0 + th)
    dgelu = 0.5 * (1.0 + th) + 0.5 * g * (1.0 - th * th) * (_GELU_C * (1.0 + 3.0 * _GELU_K * g * g))
    return gelu, dgelu


def _rows(shape):
    return lax.broadcasted_iota(jnp.int32, shape, 0)


def _shift_down(v, k, prev8):
    rolled = pltpu.roll(v, k, 0)
    halo = pltpu.roll(prev8, k, 0)
    head = jnp.where(_rows(halo.shape) < k, halo, rolled[:TILE_ROWS])
    return jnp.concatenate([head, rolled[TILE_ROWS:]], axis=0)


def _shift_up(v, k, next8):
    n = v.shape[0]
    rolled = pltpu.roll(v, n - k, 0)
    halo = pltpu.roll(next8, TILE_ROWS - k, 0)
    tail = jnp.where(_rows(halo.shape) >= TILE_ROWS - k, halo, rolled[n - TILE_ROWS:])
    return jnp.concatenate([rolled[: n - TILE_ROWS], tail], axis=0)


def _scan_rows(a, b, carry, reverse=False):
    n, w = a.shape
    groups = n // TILE_ROWS
    a3 = a.reshape(groups, TILE_ROWS, w)
    b3 = b.reshape(groups, TILE_ROWS, w)
    sub = lax.broadcasted_iota(jnp.int32, a3.shape, 1)
    s = 1
    while s < TILE_ROWS:
        shift = TILE_ROWS - s if reverse else s
        keep = (sub < TILE_ROWS - s) if reverse else (sub >= s)
        b3 = b3 + jnp.where(keep, a3 * pltpu.roll(b3, shift, 1), 0.0)
        a3 = a3 * jnp.where(keep, pltpu.roll(a3, shift, 1), 1.0)
        s *= 2
    out = [None] * groups
    edge = 0 if reverse else TILE_ROWS - 1
    for g in (range(groups - 1, -1, -1) if reverse else range(groups)):
        out[g] = b3[g] + a3[g] * carry
        carry = out[g][edge:edge + 1]
    return jnp.concatenate(out, axis=0)


def _softplus_neg(lam):
    e = jnp.exp(-jnp.abs(lam))
    log1p_e = jnp.where(e < 1e-2, e * (1.0 - e * (0.5 - e * (1.0 / 3.0 - e * 0.25))), jnp.log(1.0 + e))
    sp = jnp.maximum(-lam, 0.0) + log1p_e
    dsp = -_sigmoid(-lam)
    return sp, dsp


def _block_diag_dot(vb, w_ref):
    return jnp.concatenate(
        [jnp.dot(vb[:, j * BD:(j + 1) * BD], w_ref[j], preferred_element_type=F32) for j in range(N_BD)], axis=1)


def _block_diag_dot_t(vb, w_ref):
    return jnp.concatenate(
        [lax.dot_general(vb[:, j * BD:(j + 1) * BD], w_ref[j], (((1,), (1,)), ((), ())), preferred_element_type=F32)
         for j in range(N_BD)], axis=1)


def _dot_nt(a, b):
    return lax.dot_general(a, b, (((1,), (1,)), ((), ())), preferred_element_type=F32)


def _dot_tn(a, b):
    return lax.dot_general(a, b, (((0,), (0,)), ((), ())), preferred_element_type=F32)


def _lru_gates(xr, wa_ref, ba, wx_ref, bx, sp):
    xrb = xr.astype(BF16)
    r = _sigmoid(_block_diag_dot(xrb, wa_ref) + ba)
    ig = _sigmoid(_block_diag_dot(xrb, wx_ref) + bx)
    log_a = (-LRU_C) * r * sp
    a = jnp.exp(log_a)
    mult = jnp.sqrt(_one_minus_square(log_a, a))
    return r, ig, a, mult


def _colsum(v):
    return jnp.sum(v, axis=0, keepdims=True)


N_FWD_OUT = 6


def _fwd_mix(h1b, w_in_g, conv_w, rconv_w, rconv_b, wa_bd, b_a, wx_bd, b_x, lam, g_nc, g_nr, later):
    t, d = h1b.shape
    tm = TOKEN_TILE
    nt = t // tm
    nl = len(later)
    assert nl == 3
    pass_on_at = [nt * f // 16 for f in (3, 5, 9)]
    neighbours_at = [nt * f // 16 for f in (10, 11, 12)]
    diagonal_at = [nt * f // 16 for f in (13, 14, 14)]

    def body(h1_ref, win_ref, cw_ref, rw_ref, rb_ref, wa_ref, ba_ref, wx_ref, bx_ref, lam_ref, gnc_ref, gnr_ref, *rest):
        later_in, outs, rest = rest[:nl], rest[nl:nl + N_FWD_OUT], rest[nl + N_FWD_OUT:]
        u_ref, xr_ref, hs_ref, c3_ref, y_ref, gates_ref = outs
        later_out, (cv_prev, xin_prev, h_prev, send_sems, recv_sems) = rest[:nl], rest[nl:]
        del later_in
        step = pl.program_id(0)
        plan = _ShardGather(later_out, send_sems, recv_sems)

        @pl.when(step == 0)
        def _():
            cv_prev[...] = jnp.zeros_like(cv_prev)
            xin_prev[...] = jnp.zeros_like(xin_prev)
            h_prev[...] = jnp.zeros_like(h_prev)
            for w in range(nl):
                plan.start_direct(w)

        for w in range(nl):
            @pl.when(step == pass_on_at[w])
            def _(w=w):
                plan.start_pass_on(w)

            @pl.when(step == neighbours_at[w])
            def _(w=w):
                plan.start_hand_over(w, diagonal=False)

            @pl.when(step == diagonal_at[w])
            def _(w=w):
                plan.start_hand_over(w, diagonal=True)

        h1b = h1_ref[...]
        for j in range(N_CHIPS):
            u_ref[:, j * IN_SHARD:(j + 1) * IN_SHARD] = jnp.dot(h1b, win_ref[j], preferred_element_type=F32)
        gate_b = u_ref[:, 0:CONV_W]
        cv = u_ref[:, CONV_W:2 * CONV_W] * u_ref[:, 2 * CONV_W:3 * CONV_W]
        x_r = u_ref[:, 3 * CONV_W:3 * CONV_W + LRU_W]
        g = u_ref[:, 3 * CONV_W + LRU_W:]

        cw = cw_ref[...]
        cvp = cv_prev[...]
        conv3 = cw[0:1] * _shift_down(cv, 2, cvp) + cw[1:2] * _shift_down(cv, 1, cvp) + cw[2:3] * cv
        cv_prev[...] = cv[tm - TILE_ROWS:]
        c3_ref[...] = conv3
        y_conv = gate_b * conv3

        rw = rw_ref[...]
        xp = xin_prev[...]
        xr = (rw[0:1] * _shift_down(x_r, 3, xp) + rw[1:2] * _shift_down(x_r, 2, xp)
              + rw[2:3] * _shift_down(x_r, 1, xp) + rw[3:4] * x_r) + rb_ref[...]
        xin_prev[...] = x_r[tm - TILE_ROWS:]
        xr_ref[...] = xr
        sp, _ = _softplus_neg(lam_ref[...])
        r, ig, a, mult = _lru_gates(xr, wa_ref, ba_ref[...], wx_ref, bx_ref[...], sp)
        for n, gate in enumerate((r, ig, a, mult)):
            gates_ref[:, n * LRU_W:(n + 1) * LRU_W] = gate
        h = _scan_rows(a, mult * (ig * xr), h_prev[...])
        h_prev[...] = h[tm - 1:tm]
        hs_ref[...] = h
        gelu, _ = _gelu_and_grad(g)
        y_rnn = h * gelu

        na = y_conv * lax.rsqrt(jnp.mean(y_conv * y_conv, axis=-1, keepdims=True) + EPS) * gnc_ref[...]
        nb = y_rnn * lax.rsqrt(jnp.mean(y_rnn * y_rnn, axis=-1, keepdims=True) + EPS) * gnr_ref[...]
        y_ref[:, :CONV_W] = na.astype(BF16)
        y_ref[:, CONV_W:] = nb.astype(BF16)

        @pl.when(step == nt - 1)
        def _():
            for w in range(nl):
                plan.finish(w)

    def full(a):
        nd = a.ndim
        return pl.BlockSpec(a.shape, lambda i: (0,) * nd)

    def tok(cols):
        return pl.BlockSpec((tm, cols), lambda i: (i, 0))

    def act(cols, dtype=F32):
        return jax.ShapeDtypeStruct((t, cols), dtype)

    smalls = (w_in_g, conv_w, rconv_w, rconv_b, wa_bd, b_a, wx_bd, b_x, lam, g_nc, g_nr)
    n_in = 1 + len(smalls)
    outs = pl.pallas_call(
        body, name="fwd_mix", grid=(nt,),
        in_specs=[tok(d)] + [full(a) for a in smalls] + [ANY] * nl,
        out_specs=[tok(IN_COLS), tok(LRU_W), tok(LRU_W), tok(CONV_W), tok(CONV_W + LRU_W)]
        + [tok(4 * LRU_W)] + [ANY] * nl,
        out_shape=[act(IN_COLS), act(LRU_W), act(LRU_W), act(CONV_W), act(CONV_W + LRU_W, BF16)]
        + [act(4 * LRU_W)] + [jax.ShapeDtypeStruct(a.shape, a.dtype) for a in later],
        input_output_aliases={n_in + w: N_FWD_OUT + w for w in range(nl)},
        scratch_shapes=[pltpu.VMEM((TILE_ROWS, CONV_W), F32), pltpu.VMEM((TILE_ROWS, LRU_W), F32),
                        pltpu.VMEM((1, LRU_W), F32), pltpu.SemaphoreType.DMA((nl, _ShardGather.PAIRS)),
                        pltpu.SemaphoreType.DMA((nl, _ShardGather.PAIRS))],
        compiler_params=_params(dimension_semantics=("arbitrary",)),
    )(h1b, *smalls, *later)
    return outs[:N_FWD_OUT], outs[N_FWD_OUT:]


def _mlp_fwd_bwd(x, yb, w_out_g, w1_g, w2_g, g2, gf, target):
    t, d = x.shape
    tm = TOKEN_TILE
    ff = w2_g.shape[0]
    mix = w_out_g.shape[0]
    ffs = ff // N_CHIPS

    def body(x_ref, y_ref, g2_ref, gf_ref, tgt_ref, wout_hbm, w1_hbm, w2_hbm,
             z_ref, dp_ref, h2_ref, dx3b_ref, dx2_ref, dx2b_ref, dy_ref, st_ref, wout, w1, w2, p_ref):
        @pl.when(pl.program_id(0) == 0)
        def _():
            pltpu.sync_copy(wout_hbm, wout)
            pltpu.sync_copy(w1_hbm, w1)
            pltpu.sync_copy(w2_hbm, w2)
            st_ref[...] = jnp.zeros_like(st_ref)

        x2 = x_ref[...] + jnp.dot(y_ref[...], wout[...], preferred_element_type=F32)
        r2 = lax.rsqrt(jnp.mean(x2 * x2, axis=-1, keepdims=True) + EPS)
        xh2 = x2 * r2
        g2v = g2_ref[...]
        h2b = (xh2 * g2v).astype(BF16)
        h2_ref[...] = h2b
        for j in range(N_CHIPS):
            p_ref[:, j * ffs:(j + 1) * ffs] = jnp.dot(h2b, w1[j], preferred_element_type=F32)
        rp = jnp.maximum(p_ref[...], 0.0)
        zb = (rp * rp).astype(BF16)
        z_ref[...] = zb
        x3 = x2 + jnp.dot(zb, w2[...], preferred_element_type=F32)
        r3 = lax.rsqrt(jnp.mean(x3 * x3, axis=-1, keepdims=True) + EPS)
        xh3 = x3 * r3
        gfv = gf_ref[...]
        err = xh3 * gfv - tgt_ref[...]
        loss = (0.5 / d) * jnp.sum(err * err)
        dout = err * (1.0 / d)
        st_ref[PK_FINAL_G - PK_MIX_ROWS:PK_FINAL_G - PK_MIX_ROWS + 1, :] += _colsum(dout * xh3)
        st_ref[PK_LOSS - PK_MIX_ROWS:PK_LOSS - PK_MIX_ROWS + 1, :] += jnp.zeros((1, d), F32) + loss
        dxh3 = dout * gfv
        dx3 = r3 * (dxh3 - xh3 * jnp.mean(dxh3 * xh3, axis=-1, keepdims=True))
        dx3b = dx3.astype(BF16)
        dx3b_ref[...] = dx3b
        dpb = (_dot_nt(dx3b, w2[...]) * (2.0 * rp)).astype(BF16)
        dp_ref[...] = dpb
        dh2 = _dot_nt(dpb[:, 0:ffs], w1[0])
        for j in range(1, N_CHIPS):
            dh2 = dh2 + _dot_nt(dpb[:, j * ffs:(j + 1) * ffs], w1[j])
        st_ref[PK_MLP_G - PK_MIX_ROWS:PK_MLP_G - PK_MIX_ROWS + 1, :] += _colsum(dh2 * xh2)
        dxh2 = dh2 * g2v
        dx2 = dx3 + r2 * (dxh2 - xh2 * jnp.mean(dxh2 * xh2, axis=-1, keepdims=True))
        dx2_ref[...] = dx2
        dx2b = dx2.astype(BF16)
        dx2b_ref[...] = dx2b
        dy_ref[...] = _dot_nt(dx2b, wout[...])

    def tok(cols):
        return pl.BlockSpec((tm, cols), lambda i: (i, 0))

    def row(cols):
        return pl.BlockSpec((1, cols), lambda i: (0, 0))

    return pl.pallas_call(
        body, name="mlp_fwd_bwd", grid=(t // tm,),
        in_specs=[tok(d), tok(mix), row(d), row(d), tok(d), ANY, ANY, ANY],
        out_specs=[tok(ff), tok(ff), tok(d), tok(d), tok(d), tok(d), tok(mix),
                   pl.BlockSpec((PK_MLP_ROWS, d), lambda i: (0, 0))],
        out_shape=[jax.ShapeDtypeStruct((t, ff), BF16), jax.ShapeDtypeStruct((t, ff), BF16),
                   jax.ShapeDtypeStruct((t, d), BF16), jax.ShapeDtypeStruct((t, d), BF16),
                   jax.ShapeDtypeStruct((t, d), F32), jax.ShapeDtypeStruct((t, d), BF16),
                   jax.ShapeDtypeStruct((t, mix), F32), jax.ShapeDtypeStruct((PK_MLP_ROWS, d), F32)],
        scratch_shapes=[pltpu.VMEM(w_out_g.shape, BF16), pltpu.VMEM(w1_g.shape, BF16), pltpu.VMEM(w2_g.shape, BF16),
                        pltpu.VMEM((tm, ff), F32)],
        compiler_params=_params(dimension_semantics=("arbitrary",)),
    )(x, yb, g2, gf, target, w_out_g, w1_g, w2_g)


def _mix_bwd(dy, u, xr_all, hs_all, c3_all, gates, conv_w, rconv_w, wa_bd, wx_bd, lam, g_nc, g_nr, st_mlp, parts):
    t = dy.shape[0]
    tm = TOKEN_TILE
    nt = t // tm
    hb = tm // TILE_ROWS
    npart = len(parts)

    def body(dy_ref, u_ref, uh_ref, xr_ref, hs_ref, hh_ref, c3_ref, gates_ref,
             cw_ref, rw_ref, wa_ref, wx_ref, lam_ref, gnc_ref, gnr_ref, stm_ref, *rest):
        part_refs, (du_ref, st_ref, heads_ref), rest = rest[:npart], rest[npart:npart + 3], rest[npart + 3:]
        arrived_refs, (dc_next, a_next, gs_next, dxr_next, dwa_ref, dwx_ref, send_sems, recv_sems) = rest[:npart], rest[npart:]
        exchange = _PartialExchange(part_refs, arrived_refs, send_sems, recv_sems)
        i = pl.program_id(0)

        @pl.when(i == 0)
        def _():
            exchange.start()
            dc_next[...] = jnp.zeros_like(dc_next)
            a_next[...] = jnp.zeros_like(a_next)
            gs_next[...] = jnp.zeros_like(gs_next)
            dxr_next[...] = jnp.zeros_like(dxr_next)
            st_ref[0:PK_MIX_ROWS, :] = jnp.zeros((PK_MIX_ROWS, LRU_W), F32)
            st_ref[PK_MIX_ROWS:, :] = stm_ref[...]
            dwa_ref[...] = jnp.zeros_like(dwa_ref)
            dwx_ref[...] = jnp.zeros_like(dwx_ref)

        first_tile = i == nt - 1
        gate_b = u_ref[:, 0:CONV_W]
        gate_c = u_ref[:, CONV_W:2 * CONV_W]
        v = u_ref[:, 2 * CONV_W:3 * CONV_W]
        x_r = u_ref[:, 3 * CONV_W:3 * CONV_W + LRU_W]
        g = u_ref[:, 3 * CONV_W + LRU_W:]
        cv = gate_c * v
        cv_prev = jnp.where(first_tile, 0.0, uh_ref[:, CONV_W:2 * CONV_W] * uh_ref[:, 2 * CONV_W:3 * CONV_W])
        xin_prev = jnp.where(first_tile, 0.0, uh_ref[:, 3 * CONV_W:3 * CONV_W + LRU_W])
        hs_prev = jnp.where(first_tile, 0.0, hh_ref[...])

        def acc(first_row, val, width=LRU_W, row=0):
            r0 = first_row + row
            st_ref[r0:r0 + 1, 0:width] += val

        conv3 = c3_ref[...]
        y_conv = gate_b * conv3
        ra = lax.rsqrt(jnp.mean(y_conv * y_conv, axis=-1, keepdims=True) + EPS)
        xha = y_conv * ra
        dna = dy_ref[:, :CONV_W]
        acc(PK_G_NORM_CONV, _colsum(dna * xha), CONV_W)
        dxha = dna * gnc_ref[...]
        dy_conv = ra * (dxha - xha * jnp.mean(dxha * xha, axis=-1, keepdims=True))
        du_ref[:, 0:CONV_W] = (dy_conv * conv3).astype(BF16)
        dc = dy_conv * gate_b
        cw = cw_ref[...]
        dcn = dc_next[...]
        dcv = cw[2:3] * dc + cw[1:2] * _shift_up(dc, 1, dcn) + cw[0:1] * _shift_up(dc, 2, dcn)
        dc_next[...] = dc[:TILE_ROWS]
        acc(PK_CONV_W, _colsum(dc * _shift_down(cv, 2, cv_prev)), CONV_W, 0)
        acc(PK_CONV_W, _colsum(dc * _shift_down(cv, 1, cv_prev)), CONV_W, 1)
        acc(PK_CONV_W, _colsum(dc * cv), CONV_W, 2)
        du_ref[:, CONV_W:2 * CONV_W] = (dcv * v).astype(BF16)
        du_ref[:, 2 * CONV_W:3 * CONV_W] = (dcv * gate_c).astype(BF16)

        hs = hs_ref[...]
        gelu, dgelu = _gelu_and_grad(g)
        y_rnn = hs * gelu
        rb = lax.rsqrt(jnp.mean(y_rnn * y_rnn, axis=-1, keepdims=True) + EPS)
        xhb = y_rnn * rb
        dnb = dy_ref[:, CONV_W:]
        acc(PK_G_NORM_RNN, _colsum(dnb * xhb))
        dxhb = dnb * gnr_ref[...]
        dy_rnn = rb * (dxhb - xhb * jnp.mean(dxhb * xhb, axis=-1, keepdims=True))
        du_ref[:, 3 * CONV_W + LRU_W:] = (dy_rnn * hs * dgelu).astype(BF16)
        dh = dy_rnn * gelu

        xr = xr_ref[...]
        xrb = xr.astype(BF16)
        sp, dsp = _softplus_neg(lam_ref[...])
        r, ig, a, mult = [gates_ref[:, n * LRU_W:(n + 1) * LRU_W] for n in range(4)]
        a_up = _shift_up(a, 1, a_next[...])
        a_next[...] = a[:TILE_ROWS]
        gs = _scan_rows(a_up, dh, gs_next[0:1, :], reverse=True)
        gs_next[...] = gs[:TILE_ROWS]
        da = gs * _shift_down(hs, 1, hs_prev)
        gx = gs * xr
        di = gx * mult
        dmult = gx * ig
        dxr = gs * (mult * ig)
        dlog_a = da * a - dmult * ((a * a) / mult)
        acc(PK_LAMBDA, _colsum(dlog_a * r) * ((-LRU_C) * dsp))
        dpa = (dlog_a * ((-LRU_C) * sp)) * (r * (1.0 - r))
        dpx = di * (ig * (1.0 - ig))
        acc(PK_B_A, _colsum(dpa))
        acc(PK_B_X, _colsum(dpx))
        dpab = dpa.astype(BF16)
        dpxb = dpx.astype(BF16)
        dxr = dxr + _block_diag_dot_t(dpab, wa_ref) + _block_diag_dot_t(dpxb, wx_ref)
        for j in range(N_BD):
            cols = slice(j * BD, (j + 1) * BD)
            dwa_ref[j] += _dot_tn(xrb[:, cols], dpab[:, cols])
            dwx_ref[j] += _dot_tn(xrb[:, cols], dpxb[:, cols])

        acc(PK_RCONV_B, _colsum(dxr))
        rw = rw_ref[...]
        dxn = dxr_next[...]
        dx_r = (rw[3:4] * dxr + rw[2:3] * _shift_up(dxr, 1, dxn) + rw[1:2] * _shift_up(dxr, 2, dxn)
                + rw[0:1] * _shift_up(dxr, 3, dxn))
        dxr_next[...] = dxr[:TILE_ROWS]
        for k in range(3):
            acc(PK_RCONV_W, _colsum(dxr * _shift_down(x_r, 3 - k, xin_prev)), LRU_W, k)
        acc(PK_RCONV_W, _colsum(dxr * x_r), LRU_W, 3)
        du_ref[:, 3 * CONV_W:3 * CONV_W + LRU_W] = dx_r.astype(BF16)

        @pl.when(i == nt - 1)
        def _():
            for n, d_ref in enumerate((dwa_ref, dwx_ref)):
                for b in range(N_BD):
                    for q in range(BD // HEAD_DIM):
                        lane0 = q * HEAD_DIM // LANES * LANES
                        wide = d_ref[b, q * HEAD_DIM:(q + 1) * HEAD_DIM, lane0:lane0 + LANES]
                        if q * HEAD_DIM != lane0:
                            wide = pltpu.roll(wide, LANES - (q * HEAD_DIM - lane0), axis=1)
                        heads_ref[n, b * (BD // HEAD_DIM) + q] = wide[:, 0:HEAD_DIM].astype(BF16)
            exchange.wait()

    def full(a):
        nd = a.ndim
        return pl.BlockSpec(a.shape, lambda i: (0,) * nd)

    def tok(cols):
        return pl.BlockSpec((tm, cols), lambda i: (nt - 1 - i, 0))

    def halo(cols):
        return pl.BlockSpec((TILE_ROWS, cols), lambda i: (jnp.maximum((nt - 1 - i) * hb - 1, 0), 0))

    smalls = (conv_w, rconv_w, wa_bd, wx_bd, lam, g_nc, g_nr, st_mlp)
    st_rows = PK_MIX_ROWS + st_mlp.shape[0]
    heads = (2, N_HEADS, HEAD_DIM, HEAD_DIM)
    outs = pl.pallas_call(
        body, name="mix_bwd", grid=(nt,),
        in_specs=[tok(CONV_W + LRU_W), tok(IN_COLS), halo(IN_COLS), tok(LRU_W), tok(LRU_W), halo(LRU_W), tok(CONV_W)]
        + [tok(4 * LRU_W)] + [full(a) for a in smalls] + [ANY] * npart,
        out_specs=[tok(IN_COLS), pl.BlockSpec((st_rows, LRU_W), lambda i: (0, 0)),
                   pl.BlockSpec(heads, lambda i: (0, 0, 0, 0))]
        + [ANY] * npart,
        out_shape=[jax.ShapeDtypeStruct((t, IN_COLS), BF16), jax.ShapeDtypeStruct((st_rows, LRU_W), F32),
                   jax.ShapeDtypeStruct(heads, BF16)]
        + [jax.ShapeDtypeStruct(a.shape, a.dtype) for a in parts],
        scratch_shapes=[pltpu.VMEM((TILE_ROWS, CONV_W), F32), pltpu.VMEM((TILE_ROWS, LRU_W), F32),
                        pltpu.VMEM((TILE_ROWS, LRU_W), F32), pltpu.VMEM((TILE_ROWS, LRU_W), F32),
                        pltpu.VMEM((N_BD, BD, BD), F32), pltpu.VMEM((N_BD, BD, BD), F32),
                        pltpu.SemaphoreType.DMA((npart, 3)), pltpu.SemaphoreType.DMA((npart, 3))],
        compiler_params=_params(dimension_semantics=("arbitrary",)),
    )(dy, u, u, xr_all, hs_all, hs_all, c3_all, gates, *smalls, *parts)
    return outs[:3], outs[3:]


def _in_bwd(dub, w_in_g, x, dx2, g1, parts, joins, core_chip):
    t, d = x.shape
    tm = min(t, MATMUL_TOKEN_TILE)
    nt = t // tm
    npart = len(parts)
    nj = len(joins)
    geometry = []
    for tag, shape, _, _ in joins:
        pr, pc = WGRAD_GEOMETRY[tag][:2]
        every = 1 if pr % (nt * 16) == 0 else 2
        geometry.append((pr, pc, pr * every // nt, every, shape[1] == pc))

    def body(cc_ref, du_ref, win_ref, x_ref, dx2_ref, g1_ref, *rest):
        sums, rest = [rest[4 * w:4 * w + 4] for w in range(nj)], rest[4 * nj:]
        part_refs, (gx_ref, st_ref), rest = rest[:npart], rest[npart:npart + 2], rest[npart + 2:]
        arrived_refs, joined, rest = rest[:npart], rest[npart:npart + nj], rest[npart + nj:]
        stages, (send_sems, recv_sems, j_local, j_send, j_recv) = rest[:nj], rest[nj:]
        exchange = _PartialExchange(part_refs, arrived_refs, send_sems, recv_sems)
        i = pl.program_id(0)
        c = cc_ref[0]

        def window(w, core, row0, rows):
            pr, pc, _, _, by_rows = geometry[w]
            if by_rows:
                return joined[w].at[pl.ds(core * pr + row0, rows), :]
            return joined[w].at[pl.ds(row0, rows), pl.ds(core * pc, pc)]

        def to_sibling(w, src, core, row0, rows):
            return pltpu.make_async_remote_copy(src_ref=src, dst_ref=window(w, core, row0, rows), send_sem=j_send.at[w],
                                                recv_sem=j_recv.at[w], device_id=_sibling(), device_id_type=MESH)

        @pl.when(i == 0)
        def _():
            exchange.start()
            st_ref[...] = jnp.zeros_like(st_ref)

        for w in range(nj):
            pr, pc, rb, every, _ = geometry[w]

            @pl.when(i % every == 0)
            def _(w=w, rb=rb, every=every):
                p_ref, r1_ref, r2_ref, r3_ref = sums[w]
                row0 = pl.multiple_of((i // every) * rb, rb)
                rows = stages[w].at[pl.ds(row0, rb), :]
                rows[...] = ((p_ref[0] + r1_ref[0].astype(F32)) + r2_ref[0].astype(F32)) + r3_ref[0].astype(F32)
                pltpu.make_async_copy(rows, window(w, c, row0, rb), j_local.at[w]).start()
                to_sibling(w, rows, c, row0, rb).start()

        dh1 = _dot_nt(du_ref[:, 0:IN_SHARD], win_ref[0])
        for j in range(1, N_CHIPS):
            dh1 = dh1 + _dot_nt(du_ref[:, j * IN_SHARD:(j + 1) * IN_SHARD], win_ref[j])
        xv = x_ref[...]
        rstd = lax.rsqrt(jnp.mean(xv * xv, axis=-1, keepdims=True) + EPS)
        xh = xv * rstd
        st_ref[0:1, :] += _colsum(dh1 * xh)
        dxh = dh1 * g1_ref[...]
        gx_ref[...] = dx2_ref[...] + rstd * (dxh - xh * jnp.mean(dxh * xh, axis=-1, keepdims=True))

        @pl.when(i == nt - 1)
        def _():
            exchange.wait()
            for w in range(nj):
                pr = geometry[w][0]
                pltpu.make_async_copy(stages[w], window(w, c, 0, pr), j_local.at[w]).wait()
                to_sibling(w, stages[w], 1 - c, 0, pr).wait()

    def tok(cols):
        return pl.BlockSpec((tm, cols), lambda i, cc: (i, 0))

    def partial(w, off):
        pr, pc, rb, every, _ = geometry[w]
        return pl.BlockSpec((1, rb, pc), lambda i, cc: ((cc[1] + off) % N_CHIPS, i // every, 0))

    sum_specs, sum_operands = [], []
    for w, (_, _, own, arrived) in enumerate(joins):
        sum_specs += [partial(w, off) for off in range(N_CHIPS)]
        sum_operands += [own, arrived, arrived, arrived]
    dma = pltpu.SemaphoreType.DMA
    outs = pl.pallas_call(
        body, name="in_bwd",
        grid_spec=pltpu.PrefetchScalarGridSpec(
            num_scalar_prefetch=1, grid=(nt,),
            in_specs=[tok(IN_COLS), pl.BlockSpec(w_in_g.shape, lambda i, cc: (0, 0, 0)), tok(d), tok(d),
                      pl.BlockSpec((1, d), lambda i, cc: (0, 0))] + sum_specs + [ANY] * npart,
            out_specs=[tok(d), pl.BlockSpec((TILE_ROWS, d), lambda i, cc: (0, 0))] + [ANY] * (npart + nj),
            scratch_shapes=[pltpu.VMEM((g[0], g[1]), F32) for g in geometry]
            + [dma((npart, 3)), dma((npart, 3)), dma((nj,)), dma((nj,)), dma((nj,))]),
        out_shape=[jax.ShapeDtypeStruct((t, d), F32), jax.ShapeDtypeStruct((TILE_ROWS, d), F32)]
        + [jax.ShapeDtypeStruct(a.shape, a.dtype) for a in parts]
        + [jax.ShapeDtypeStruct(shape, F32) for _, shape, _, _ in joins],
        compiler_params=_params(dimension_semantics=("arbitrary",)),
    )(core_chip, dub, w_in_g, x, dx2, g1, *sum_operands, *parts)
    return outs[:2], outs[2:2 + npart], outs[2 + npart:]


WGRAD_GEOMETRY = {
    "in": (512, IN_SHARD, lambda s, h: h, lambda s, h: s),
    "mlp_in": (512, D_MODEL, lambda s, h: h, lambda s, h: s),
    "mlp_out": (512, D_MODEL, lambda s, h: 2 * s + h, lambda s, h: 0),
    "out": (384, 512, lambda s, h: s, lambda s, h: h),
}
K_CHUNK = 512
TOKEN_STREAMS = 2


def _sibling():
    x, y, c = _position()
    return (x, y, 1 - c)


def _wgrad(a, b, tag, core_chip, packs=(), parts=()):
    t = a.shape[0]
    pr, pc, a_blk, b_blk = WGRAD_GEOMETRY[tag]
    ns = TOKEN_STREAMS
    ts = t // ns
    kc = min(K_CHUNK, ts)
    mine = N_CHIPS
    riding = len(packs)
    npart = len(parts)
    assert not (riding and npart)

    def body(cc_ref, *rest):
        a_refs, b_refs, rest = rest[:ns], rest[ns:2 * ns], rest[2 * ns:]
        if riding:
            pack_refs, (land_ref, p_ref, pb_ref), rest = rest[:riding], rest[riding:riding + 3], rest[riding + 3:]
            all_refs, (stage, rbuf, send_sems, recv_sems, rsem), g_sems = rest[:riding], rest[riding:riding + 5], rest[riding + 5:]
            gathers = [_PackGather(pack_refs[n], all_refs[n], *g_sems[3 * n:3 * n + 3]) for n in range(riding)]
        elif npart:
            part_refs, (land_ref, p_ref, pb_ref), rest = rest[:npart], rest[npart:npart + 3], rest[npart + 3:]
            arrived_refs, (stage, rbuf, send_sems, recv_sems, rsem, x_send, x_recv) = rest[:npart], rest[npart:]
            exchange = _PartialExchange(part_refs, arrived_refs, x_send, x_recv)
        else:
            land_ref, p_ref, pb_ref, stage, rbuf, send_sems, recv_sems, rsem = rest
        ph, s = pl.program_id(0), pl.program_id(1)
        if riding:
            @pl.when((ph == 0) & (s == 0))
            def _():
                for gather in gathers:
                    gather.start()

            @pl.when((ph == 1) & (s == N_CHIPS - 2))
            def _():
                for gather in gathers:
                    gather.hand_over()
        if npart:
            @pl.when((ph == 0) & (s == 0))
            def _():
                exchange.start()
        def push(k):
            return pltpu.make_async_remote_copy(src_ref=stage.at[k], dst_ref=land_ref.at[k], send_sem=send_sems.at[k],
                                                recv_sem=recv_sems.at[k], device_id=_sibling(), device_id_type=MESH)

        def landed():
            return pltpu.make_async_copy(land_ref.at[s], rbuf, rsem)

        @pl.when(ph == 1)
        def _():
            push(s).wait_recv()
            landed().start()

        slot = jnp.where(ph == 0, s, mine)
        acc = stage.at[slot]
        chunks = [(a_ref, b_ref, k) for a_ref, b_ref in zip(a_refs, b_refs) for k in range(0, ts, kc)]
        for n, (a_ref, b_ref, k) in enumerate(chunks):
            part = _dot_tn(a_ref[k:k + kc, :], b_ref[k:k + kc, :])
            if n == 0:
                acc[...] = part
            else:
                acc[...] += part

        @pl.when(ph == 0)
        def _():
            push(s).start()

        @pl.when(ph == 1)
        def _():
            landed().wait()
            p = stage[mine] + rbuf[...]
            p_ref[0] = p
            pb_ref[0] = p.astype(BF16)

        @pl.when((ph == 1) & (s == N_CHIPS - 1))
        def _():
            for k in range(N_CHIPS):
                push(k).wait_send()
            for gather in (gathers if riding else ()):
                gather.finish()
            if npart:
                exchange.wait()

    def half(ph, cc):
        return jnp.where(ph == 0, 1 - cc[0], cc[0])

    def out_slot(ph, s, cc):
        return (jnp.where(ph == 0, 0, s), 0, 0)

    piece = jax.ShapeDtypeStruct((N_CHIPS, pr, pc), F32)
    in_specs = [pl.BlockSpec((ts, pr), lambda ph, s, cc, n=n: (n, a_blk(s, half(ph, cc)))) for n in range(ns)]
    in_specs += [pl.BlockSpec((ts, pc), lambda ph, s, cc, n=n: (n, b_blk(s, half(ph, cc)))) for n in range(ns)]
    out_specs = [ANY, pl.BlockSpec((1, pr, pc), out_slot), pl.BlockSpec((1, pr, pc), out_slot)]
    out_shape = [piece, piece, jax.ShapeDtypeStruct((N_CHIPS, pr, pc), BF16)]
    scratch = [pltpu.VMEM((N_CHIPS + 1, pr, pc), F32), pltpu.VMEM((pr, pc), F32),
               pltpu.SemaphoreType.DMA((N_CHIPS,)), pltpu.SemaphoreType.DMA((N_CHIPS,)), pltpu.SemaphoreType.DMA]
    operands = [a] * ns + [b] * ns
    for pack in packs:
        in_specs.append(pl.BlockSpec(pack.shape, lambda ph, s, cc, nd=pack.ndim: (0,) * nd))
        out_specs.append(ANY)
        out_shape.append(jax.ShapeDtypeStruct((N_DEVICES,) + pack.shape, pack.dtype))
        operands.append(pack)
    for pack in packs:
        scratch += _PackGather.semaphores()
    if npart:
        in_specs += [ANY] * npart
        out_specs += [ANY] * npart
        out_shape += [jax.ShapeDtypeStruct(p.shape, p.dtype) for p in parts]
        scratch += [pltpu.SemaphoreType.DMA((npart, 3)), pltpu.SemaphoreType.DMA((npart, 3))]
        operands += list(parts)
    return pl.pallas_call(
        body, name="wgrad_" + tag,
        grid_spec=pltpu.PrefetchScalarGridSpec(
            num_scalar_prefetch=1, grid=(2, N_CHIPS), in_specs=in_specs, out_specs=out_specs, scratch_shapes=scratch),
        out_shape=out_shape,
        compiler_params=_params(dimension_semantics=("arbitrary", "arbitrary")),
    )(core_chip, *operands)[1:]


def _other_chips(x, y):
    return [(1 - x, y), (x, 1 - y), (1 - x, 1 - y)]


class _ShardGather:
    PAIRS = 9

    def __init__(self, outs, send_sems, recv_sems):
        self.outs, self.send_sems, self.recv_sems = outs, send_sems, recv_sems
        x, y, c = _position()
        self.c, self.j = c, 2 * x + y
        self.sibling = (x, y, 1 - c)
        self.chips = _other_chips(x, y)

    def _chip(self, k):
        px, py = self.chips[k]
        return 2 * px + py

    def _half(self, w, chip, which):
        hr = self.outs[w].shape[1] // 2
        return self.outs[w].at[chip, pl.ds(which * hr, hr), :]

    def _quarter(self, w, chip, q):
        qr = self.outs[w].shape[1] // 4
        return self.outs[w].at[chip, pl.ds(self.c * 2 * qr + q * qr, qr), :]

    def _copy(self, ref, w, pair, to, src=None):
        return pltpu.make_async_remote_copy(src_ref=ref if src is None else src, dst_ref=ref, send_sem=self.send_sems.at[w, pair],
                                            recv_sem=self.recv_sems.at[w, pair], device_id=to, device_id_type=MESH)

    def direct(self, w, k, q, src=None):
        return self._copy(self._quarter(w, self.j, q), w, 2 * k + q, (*self.chips[k], self.c), src)

    def direct_landed(self, w, k, q):
        return self._copy(self._quarter(w, self._chip(k), q), w, 2 * k + q, (*self.chips[k], self.c))

    def pass_on(self, w, q):
        return self._copy(self._quarter(w, self._chip(q), q), w, 4 + q, (*self.chips[1 - q], self.c))

    def passed_landed(self, w, q):
        return self._copy(self._quarter(w, self._chip(2), q), w, 4 + q, (*self.chips[1 - q], self.c))

    def hand_over(self, w, k):
        return self._copy(self._half(w, self._chip(k), self.c), w, 6 + k, self.sibling)

    def handed(self, w, k):
        return self._copy(self._half(w, self._chip(k), 1 - self.c), w, 6 + k, self.sibling)

    def start_direct(self, w, src_half=None):
        qr = self.outs[w].shape[1] // 4
        for k, q in ((0, 0), (1, 1), (0, 1), (1, 0)):
            self.direct(w, k, q, None if src_half is None else src_half.at[pl.ds(q * qr, qr), :]).start()

    def start_pass_on(self, w):
        for q in (0, 1):
            self.direct_landed(w, q, q).wait_recv()
            self.pass_on(w, q).start()

    def start_hand_over(self, w, diagonal):
        if diagonal:
            for q in (0, 1):
                self.passed_landed(w, q).wait_recv()
            self.hand_over(w, 2).start()
        else:
            for k in (0, 1):
                self.direct_landed(w, k, 1 - k).wait_recv()
                self.hand_over(w, k).start()

    def finish(self, w):
        for k in range(3):
            self.handed(w, k).wait_recv()
            self.hand_over(w, k).wait_send()
        for q in (0, 1):
            self.pass_on(w, q).wait_send()
            for k in (0, 1):
                self.direct(w, k, q).wait_send()


def _gather_first(w_in, w_out, w1, w2, conv_w, rconv_w, w_a, w_x, x, g1):
    t, d = x.shape
    tn = min(t, MATMUL_TOKEN_TILE)
    n_tiles = t // tn
    bigs = (w_in, w_out, w1, w2)
    convs = (conv_w, rconv_w)
    heads = (w_a, w_x)
    nb, nc = len(bigs), len(convs)

    def body(win_ref, wout_hbm, w1_hbm, w2_hbm, cw_ref, rw_ref, wa_ref, wx_ref, x_hbm, g1_ref, gin, gout, g1, g2, gcw, grw,
             bda, bdx, h1_hbm, st_in, st_out, st_1, st_2, f_out, f_1, f_2, st_cw, st_rw, xbuf, hbuf, send_sems, recv_sems,
             sm_send, sm_recv, local_sems, load_sems, x_sems, h_sems):
        stages = (st_in, st_out, st_1, st_2)
        outs = (gin, gout, g1, g2)
        conv_stages, conv_outs = (st_cw, st_rw), (gcw, grw)
        plan = _ShardGather(outs[:1], send_sems, recv_sems)
        j, c = plan.j, plan.c
        local = [pltpu.make_async_copy(stages[w], outs[w].at[j], local_sems.at[w]) for w in range(nb)]
        loads = [pltpu.make_async_copy(src, dst, load_sems.at[n])
                 for n, (src, dst) in enumerate(((wout_hbm, f_out), (w1_hbm, f_1), (w2_hbm, f_2)))]

        def columns(n, chip):
            width = convs[n].shape[1]
            return conv_outs[n].at[:, pl.ds(chip * width, width)]

        def x_load(i):
            return pltpu.make_async_copy(x_hbm.at[pl.ds(i * tn, tn), :], xbuf.at[i % 2], x_sems.at[i % 2])

        def h1_store(i):
            return pltpu.make_async_copy(hbuf.at[i % 2], h1_hbm.at[pl.ds(i * tn, tn), :], h_sems.at[i % 2])

        def first_norm():
            x_load(0).start()
            for i in range(n_tiles):
                if i + 1 < n_tiles:
                    x_load(i + 1).start()
                x_load(i).wait()
                if i >= 2:
                    h1_store(i - 2).wait()
                xv = xbuf[i % 2]
                rstd = lax.rsqrt(jnp.mean(xv * xv, axis=-1, keepdims=True) + EPS)
                hbuf[i % 2] = ((xv * rstd) * g1_ref[...]).astype(BF16)
                h1_store(i).start()
            for i in range(max(n_tiles - 2, 0), n_tiles):
                h1_store(i).wait()

        local += [pltpu.make_async_copy(conv_stages[n], columns(n, j), local_sems.at[nb + n]) for n in range(nc)]

        def small_copy(k, n, landed=False):
            px, py = plan.chips[k]
            return pltpu.make_async_remote_copy(
                src_ref=conv_stages[n], dst_ref=columns(n, 2 * px + py if landed else j), send_sem=sm_send.at[k, n],
                recv_sem=sm_recv.at[k, n], device_id=(px, py, c), device_id_type=MESH)

        for cp in loads:
            cp.start()
        hr = w_in.shape[0] // 2
        st_in[...] = win_ref[...].astype(BF16)
        plan.start_direct(0, st_in.at[pl.ds(c * hr, hr), :])
        for src, st in zip((cw_ref, rw_ref), conv_stages):
            st[...] = jnp.zeros_like(st)
            st[0:src.shape[0], :] = src[...]
        for k in range(3):
            for n in range(nc):
                small_copy(k, n).start()
        for src, bd in ((wa_ref, bda), (wx_ref, bdx)):
            bd[...] = jnp.zeros_like(bd)
            for h in range(N_HEADS):
                q = h % (BD // HEAD_DIM)
                bd[h // (BD // HEAD_DIM), q * HEAD_DIM:(q + 1) * HEAD_DIM, q * HEAD_DIM:(q + 1) * HEAD_DIM] = src[h].astype(BF16)
        for cp, full, st in zip(loads, (f_out, f_1, f_2), stages[1:]):
            cp.wait()
            st[...] = full[...].astype(BF16)
        for cp in local:
            cp.start()
        plan.start_pass_on(0)
        first_norm()
        plan.start_hand_over(0, diagonal=False)
        plan.start_hand_over(0, diagonal=True)
        for k in range(3):
            for n in range(nc):
                small_copy(k, n, landed=True).wait_recv()
                small_copy(k, n).wait_send()
        plan.finish(0)
        for cp in local:
            cp.wait()

    def gathered(a, dtype):
        return jax.ShapeDtypeStruct((N_CHIPS,) + a.shape, dtype)

    return pl.pallas_call(
        body, name="gather_first",
        in_specs=[VMEM] + [ANY] * (nb - 1) + [VMEM] * (nc + len(heads)) + [ANY, VMEM],
        out_specs=[ANY] * (nb + nc) + [VMEM] * len(heads) + [ANY],
        out_shape=[gathered(a, BF16) for a in bigs]
        + [jax.ShapeDtypeStruct((TILE_ROWS, N_CHIPS * a.shape[1]), F32) for a in convs]
        + [jax.ShapeDtypeStruct((N_BD, BD, BD), BF16) for _ in heads] + [jax.ShapeDtypeStruct((t, d), BF16)],
        scratch_shapes=[pltpu.VMEM(a.shape, BF16) for a in bigs] + [pltpu.VMEM(a.shape, F32) for a in bigs[1:]]
        + [pltpu.VMEM((TILE_ROWS, a.shape[1]), F32) for a in convs]
        + [pltpu.VMEM((2, tn, d), F32), pltpu.VMEM((2, tn, d), BF16)]
        + [pltpu.SemaphoreType.DMA((1, _ShardGather.PAIRS)), pltpu.SemaphoreType.DMA((1, _ShardGather.PAIRS)),
           pltpu.SemaphoreType.DMA((3, nc)), pltpu.SemaphoreType.DMA((3, nc)), pltpu.SemaphoreType.DMA((nb + nc,)),
           pltpu.SemaphoreType.DMA((nb - 1,)), pltpu.SemaphoreType.DMA((2,)), pltpu.SemaphoreType.DMA((2,))],
        compiler_params=_params(),
    )(*bigs, *convs, *heads, x, g1)


class _PartialExchange:
    def __init__(self, parts, arrived, send_sems, recv_sems):
        self.parts, self.arrived, self.send_sems, self.recv_sems = parts, arrived, send_sems, recv_sems
        x, y, c = _position()
        self.c, self.j = c, 2 * x + y
        self.chips = _other_chips(x, y)

    def _copy(self, w, k, slot):
        px, py = self.chips[k]
        return pltpu.make_async_remote_copy(
            src_ref=self.parts[w].at[2 * px + py], dst_ref=self.arrived[w].at[slot], send_sem=self.send_sems.at[w, k],
            recv_sem=self.recv_sems.at[w, k], device_id=(px, py, self.c), device_id_type=MESH)

    def start(self):
        for w in range(len(self.parts)):
            for k in range(3):
                self._copy(w, k, self.j).start()

    def wait(self):
        for w in range(len(self.parts)):
            for k in range(3):
                px, py = self.chips[k]
                self._copy(w, k, 2 * px + py).wait()


class _PackGather:
    def __init__(self, p_ref, all_ref, send_sems, recv_sems, local_sem):
        self.p_ref, self.all_ref, self.send_sems, self.recv_sems, self.local_sem = p_ref, all_ref, send_sems, recv_sems, local_sem
        x, y, c = _position()
        self.me, self.sibling, self.c = (x, y, c), (x, y, 1 - c), c
        self.chips = _other_chips(x, y)

    @staticmethod
    def semaphores():
        return [pltpu.SemaphoreType.DMA((7,)), pltpu.SemaphoreType.DMA((7,)), pltpu.SemaphoreType.DMA]

    def _copy(self, k, block, to, from_pack=False):
        px, py, pc = block
        slot = self.all_ref.at[4 * px + 2 * py + pc]
        return pltpu.make_async_remote_copy(src_ref=self.p_ref if from_pack else slot, dst_ref=slot, send_sem=self.send_sems.at[k],
                                            recv_sem=self.recv_sems.at[k], device_id=to, device_id_type=MESH)

    def _mine(self):
        x, y, c = self.me
        return pltpu.make_async_copy(self.p_ref, self.all_ref.at[4 * x + 2 * y + c], self.local_sem)

    def _first(self):
        return [self._copy(0, self.me, self.sibling, True)] + [
            self._copy(1 + k, self.me, (*chip, self.c), True) for k, chip in enumerate(self.chips)]

    def _passed(self):
        return [self._copy(4 + k, (*chip, self.c), self.sibling) for k, chip in enumerate(self.chips)]

    def start(self):
        self._mine().start()
        for cp in self._first():
            cp.start()

    def hand_over(self):
        for k, chip in enumerate(self.chips):
            self._copy(1 + k, (*chip, self.c), self.me).wait_recv()
            self._passed()[k].start()

    def finish(self):
        self._copy(0, self.sibling, self.me).wait_recv()
        for k, chip in enumerate(self.chips):
            self._copy(4 + k, (*chip, 1 - self.c), self.me).wait_recv()
        for cp in self._first() + self._passed():
            cp.wait_send()
        self._mine().wait()


class _DirectGather:
    def __init__(self, p_ref, all_ref, send_sems, recv_sems, local_sem):
        self.p_ref, self.all_ref, self.send_sems, self.recv_sems, self.local_sem = p_ref, all_ref, send_sems, recv_sems, local_sem
        self.me = _position()

    semaphores = _PackGather.semaphores

    def _peer(self, r):
        x, y, c = self.me
        return ((1 - x) if r & 4 else x, (1 - y) if r & 2 else y, (1 - c) if r & 1 else c)

    def _copy(self, r, slot_of):
        px, py, pc = slot_of
        return pltpu.make_async_remote_copy(src_ref=self.p_ref, dst_ref=self.all_ref.at[4 * px + 2 * py + pc],
                                            send_sem=self.send_sems.at[r - 1], recv_sem=self.recv_sems.at[r - 1],
                                            device_id=self._peer(r), device_id_type=MESH)

    def _mine(self):
        x, y, c = self.me
        return pltpu.make_async_copy(self.p_ref, self.all_ref.at[4 * x + 2 * y + c], self.local_sem)

    def start(self):
        self._mine().start()
        for r in range(1, N_DEVICES):
            self._copy(r, self.me).start()

    def finish(self):
        for r in range(1, N_DEVICES):
            self._copy(r, self._peer(r)).wait()
        self._mine().wait()


def _adamw(w, g, m, v):
    m = ADAM_B1 * m + (1.0 - ADAM_B1) * g
    v = ADAM_B2 * v + (1.0 - ADAM_B2) * (g * g)
    m_hat = m / ADAM_BC1
    v_hat = v / ADAM_BC2
    delta = -ADAM_LR * (m_hat / (jnp.sqrt(v_hat) + ADAM_EPS) + ADAM_WD * w)
    return delta, m, v


JOIN_SUB = 4


def _join(tag, shard_shape, part, arrived, core_chip, block=None):
    pr, pc = WGRAD_GEOMETRY[tag][:2]
    rb = pr // JOIN_SUB
    by_rows = shard_shape[1] == pc
    riding = block is not None

    def body(cc_ref, p_ref, r1_ref, r2_ref, r3_ref, *rest):
        if riding:
            blk_ref, g_ref, all_ref, stage, send_sems, recv_sems, local_sems, b_send, b_recv, b_local = rest
            gather = _DirectGather(blk_ref, all_ref, b_send, b_recv, b_local)
        else:
            g_ref, stage, send_sems, recv_sems, local_sems = rest
        i = pl.program_id(0)
        c = cc_ref[0]
        if riding:
            @pl.when(i == 0)
            def _():
                gather.start()

        def window(core, k):
            if by_rows:
                return g_ref.at[pl.ds((core * JOIN_SUB + k) * rb, rb), :]
            return g_ref.at[pl.ds(k * rb, rb), pl.ds(core * pc, pc)]

        def keep(k):
            return pltpu.make_async_copy(stage.at[k], window(c, k), local_sems.at[k])

        def push(k):
            return pltpu.make_async_remote_copy(src_ref=stage.at[k], dst_ref=window(c, k), send_sem=send_sems.at[k],
                                                recv_sem=recv_sems.at[k], device_id=_sibling(), device_id_type=MESH)

        def pushed(k):
            return pltpu.make_async_remote_copy(src_ref=stage.at[k], dst_ref=window(1 - c, k), send_sem=send_sems.at[k],
                                                recv_sem=recv_sems.at[k], device_id=_sibling(), device_id_type=MESH)

        stage[i] = ((p_ref[0] + r1_ref[0].astype(F32)) + r2_ref[0].astype(F32)) + r3_ref[0].astype(F32)
        keep(i).start()
        push(i).start()

        @pl.when(i == JOIN_SUB - 1)
        def _():
            for k in range(JOIN_SUB):
                keep(k).wait()
                push(k).wait_send()
                pushed(k).wait_recv()
            if riding:
                gather.finish()

    def partial(off):
        return pl.BlockSpec((1, rb, pc), lambda i, cc: ((cc[1] + off) % N_CHIPS, i, 0))

    in_specs = [partial(0), partial(1), partial(2), partial(3)]
    out_specs = [ANY]
    out_shape = [jax.ShapeDtypeStruct(shard_shape, F32)]
    scratch = [pltpu.VMEM((JOIN_SUB, rb, pc), F32), pltpu.SemaphoreType.DMA((JOIN_SUB,)),
               pltpu.SemaphoreType.DMA((JOIN_SUB,)), pltpu.SemaphoreType.DMA((JOIN_SUB,))]
    operands = [part, arrived, arrived, arrived]
    if riding:
        in_specs.append(pl.BlockSpec(block.shape, lambda i, cc: (0, 0)))
        out_specs.append(ANY)
        out_shape.append(jax.ShapeDtypeStruct((N_DEVICES,) + block.shape, block.dtype))
        scratch += _DirectGather.semaphores()
        operands.append(block)
    outs = pl.pallas_call(
        body, name="join_" + tag,
        grid_spec=pltpu.PrefetchScalarGridSpec(
            num_scalar_prefetch=1, grid=(JOIN_SUB,), in_specs=in_specs, out_specs=out_specs, scratch_shapes=scratch),
        out_shape=out_shape,
        compiler_params=_params(dimension_semantics=("arbitrary",)),
    )(core_chip, *operands)
    return outs if riding else outs[0]


def _adamw_big(w, g, m, v, name):
    rows, cols = w.shape
    rb = ADAMW_ROWS if rows % ADAMW_ROWS == 0 else rows

    def body(w_ref, g_ref, m_ref, v_ref, go_ref, d_ref, nm_ref, nv_ref):
        g = g_ref[...]
        go_ref[...] = g
        d_ref[...], nm_ref[...], nv_ref[...] = _adamw(w_ref[...], g, m_ref[...], v_ref[...])

    spec = pl.BlockSpec((rb, cols), lambda i: (i, 0))
    return pl.pallas_call(
        body, name=name, grid=(rows // rb,), in_specs=[spec] * 4, out_specs=[spec] * 4,
        out_shape=[jax.ShapeDtypeStruct(w.shape, F32)] * 4,
        compiler_params=_params(dimension_semantics=("arbitrary",)),
    )(w, g, m, v)


SMALL_VECTORS = {
    "norm_mix_g": (PK_MIX_G, D_MODEL), "rnn_conv_b": (PK_RCONV_B, LRU_W), "b_a": (PK_B_A, LRU_W), "b_x": (PK_B_X, LRU_W),
    "lru_lambda": (PK_LAMBDA, LRU_W), "g_norm_conv": (PK_G_NORM_CONV, CONV_W), "g_norm_rnn": (PK_G_NORM_RNN, LRU_W),
    "norm_mlp_g": (PK_MLP_G, D_MODEL), "final_norm_g": (PK_FINAL_G, D_MODEL),
}
SMALL_MATRICES = ("w_a", "w_x")


def _small_step(vec_packs, mat_packs, mix_g_blocks, p):
    vec_rows, cols = vec_packs.shape[1:]
    conv_rows, cshard = p["conv_w"].shape
    rconv_rows, rshard = p["rnn_conv_w"].shape
    names = list(SMALL_VECTORS) + list(SMALL_MATRICES) + ["conv_w", "rnn_conv_w"]
    shapes = ([(1, width) for _, width in SMALL_VECTORS.values()] + [mat_packs.shape[2:]] * len(SMALL_MATRICES)
              + [(conv_rows, cshard), (rconv_rows, rshard)])
    kinds = ("", "m_", "v_")
    params = [p[pre + n].reshape(1, -1) if n in SMALL_VECTORS else p[pre + n] for pre in kinds for n in names]

    def body(vec_ref, mat_ref, blk_ref, *rest):
        wmv = [dict(zip(names, rest[k * len(names):(k + 1) * len(names)])) for k in range(3)]
        loss_ref, rest = rest[3 * len(names)], rest[3 * len(names) + 1:]
        leaves, (g_ref, w_ref, m_ref, v_ref) = [rest[k * len(names):(k + 1) * len(names)] for k in range(4)], rest[4 * len(names):]
        total = vec_ref[0]
        mats = mat_ref[0].astype(F32)
        late = blk_ref[0]
        for k in range(1, N_DEVICES):
            total = total + vec_ref[k]
            mats = mats + mat_ref[k].astype(F32)
            late = late + blk_ref[k]
        g_ref[0:vec_rows, :] = total
        g_ref[vec_rows:, :] = late
        g = g_ref[...]
        loss_ref[...] = g[PK_LOSS:PK_LOSS + 1, 0:1]

        for pack_ref, given in zip((w_ref, m_ref, v_ref), wmv):
            pack_ref[...] = jnp.zeros_like(pack_ref)
            for name, (row, width) in SMALL_VECTORS.items():
                pack_ref[row:row + 1, 0:width] = given[name][...]

        x, y, _ = _position()
        j = 2 * x + y
        cblk = total[0:TILE_ROWS, :]
        rblk = total[PK_RCONV_W:PK_RCONV_W + TILE_ROWS, :]
        cg = cblk[:, 0:cshard]
        rg = rblk[:, 0:rshard]
        for k in range(1, N_CHIPS):
            cg = jnp.where(j == k, cblk[:, k * cshard:(k + 1) * cshard], cg)
            rg = jnp.where(j == k, rblk[:, k * rshard:(k + 1) * rshard], rg)
        cg = cg[PK_CONV_W:PK_CONV_W + conv_rows, :]
        rg = rg[0:rconv_rows, :]

        def step(name, grad):
            return (grad,) + _adamw(wmv[0][name][...], grad, wmv[1][name][...], wmv[2][name][...])

        packs = (g,) + _adamw(w_ref[...], g, m_ref[...], v_ref[...])
        matrices = [step(name, mats[n]) for n, name in enumerate(SMALL_MATRICES)]
        convs, rconvs = step("conv_w", cg), step("rnn_conv_w", rg)
        for kind in range(4):
            out = dict(zip(names, leaves[kind]))
            for name, (row, width) in SMALL_VECTORS.items():
                out[name][...] = packs[kind][row:row + 1, 0:width]
            for n, name in enumerate(SMALL_MATRICES):
                out[name][...] = matrices[n][kind]
            out["conv_w"][...] = convs[kind]
            out["rnn_conv_w"][...] = rconvs[kind]

    outs = pl.pallas_call(
        body, name="small_grads_step", in_specs=[VMEM] * (3 + len(params)), out_specs=[VMEM] * (1 + 4 * len(names)),
        out_shape=[jax.ShapeDtypeStruct((1, 1), F32)] + [jax.ShapeDtypeStruct(sh, F32) for sh in shapes] * 4,
        scratch_shapes=[pltpu.VMEM((PK_ROWS, cols), F32)] * 4,
        compiler_params=_params(),
    )(vec_packs, mat_packs, mix_g_blocks, *params)
    return outs[0], [dict(zip(names, outs[1 + k * len(names):1 + (k + 1) * len(names)])) for k in range(4)]


_NAMES = ['norm_mix_g', 'w_in', 'conv_w', 'rnn_conv_w', 'rnn_conv_b', 'w_a', 'b_a', 'w_x', 'b_x', 'lru_lambda',
          'g_norm_conv', 'g_norm_rnn', 'w_out', 'norm_mlp_g', 'w_mlp_in', 'w_mlp_out', 'final_norm_g']


def kernel(x, norm_mix_g, w_in, conv_w, rnn_conv_w, rnn_conv_b, w_a, b_a, w_x, b_x, lru_lambda, g_norm_conv, g_norm_rnn, w_out, norm_mlp_g, w_mlp_in, w_mlp_out, final_norm_g, loss_target, m_norm_mix_g, m_w_in, m_conv_w, m_rnn_conv_w, m_rnn_conv_b, m_w_a, m_b_a, m_w_x, m_b_x, m_lru_lambda, m_g_norm_conv, m_g_norm_rnn, m_w_out, m_norm_mlp_g, m_w_mlp_in, m_w_mlp_out, m_final_norm_g, v_norm_mix_g, v_w_in, v_conv_w, v_rnn_conv_w, v_rnn_conv_b, v_w_a, v_b_a, v_w_x, v_b_x, v_lru_lambda, v_g_norm_conv, v_g_norm_rnn, v_w_out, v_norm_mlp_g, v_w_mlp_in, v_w_mlp_out, v_final_norm_g):
    args = dict(locals())
    p = {}
    for n in _NAMES:
        for pre in ("", "m_", "v_"):
            a = args[pre + n]
            p[pre + n] = a[0] if a.ndim >= 3 else a
    xs = x[0]
    target = loss_target[0]
    core_chip = jnp.stack([lax.axis_index("c"), 2 * lax.axis_index("x") + lax.axis_index("y")]).astype(jnp.int32)

    w_in_g, w_out_g, w1_g, w2_g, conv_full, rconv_full, wa_bd, wx_bd, h1b = _gather_first(
        p["w_in"], p["w_out"], p["w_mlp_in"], p["w_mlp_out"], p["conv_w"], p["rnn_conv_w"], p["w_a"], p["w_x"],
        xs, p["norm_mix_g"])
    gf = p["final_norm_g"].reshape(1, -1)
    lru = (wa_bd, p["b_a"], wx_bd, p["b_x"], p["lru_lambda"], p["g_norm_conv"], p["g_norm_rnn"])

    (u, xr, hs, c3, yb, gates), (w_out_g, w1_g, w2_g) = _fwd_mix(
        h1b, w_in_g, conv_full, rconv_full, p["rnn_conv_b"], *lru, (w_out_g, w1_g, w2_g))
    zb, dpb, h2b, dx3b, dx2, dx2b, dy, st_mlp = _mlp_fwd_bwd(
        xs, yb, w_out_g.reshape(-1, D_MODEL), w1_g, w2_g.reshape(-1, D_MODEL), p["norm_mlp_g"], gf, target)

    part_out = _wgrad(yb, dx2b, "out", core_chip)
    *part_1, arrived_out = _wgrad(h2b, dpb, "mlp_in", core_chip, parts=(part_out[1],))
    part_2 = _wgrad(zb, dx3b, "mlp_out", core_chip)
    (dub, vec_pack, mat_pack), (arrived_1, arrived_2) = _mix_bwd(
        dy, u, xr, hs, c3, gates, conv_full, rconv_full, wa_bd, wx_bd, p["lru_lambda"], p["g_norm_conv"], p["g_norm_rnn"],
        st_mlp, (part_1[1], part_2[1]))
    arrived_mlp = (arrived_out, arrived_1, arrived_2)
    *part_in, vec_packs, mat_packs = _wgrad(h1b, dub, "in", core_chip, packs=(vec_pack, mat_pack))
    early = (("w_out", "out", part_out, arrived_mlp[0]), ("w_mlp_in", "mlp_in", part_1, arrived_mlp[1]),
             ("w_mlp_out", "mlp_out", part_2, arrived_mlp[2]))
    (grad_x, st_in), arrived_in, joined = _in_bwd(
        dub, w_in_g, xs, dx2, p["norm_mix_g"], (part_in[1],),
        [(tag, p[n].shape, part[0], arrived) for n, tag, part, arrived in early], core_chip)
    g_in, mix_g_blocks = _join("in", p["w_in"].shape, part_in[0], arrived_in[0], core_chip, st_in)
    big = {}
    for n, tag, g in [(n, tag, g) for (n, tag, _, _), g in zip(early, joined)] + [("w_in", "in", g_in)]:
        big[n] = _adamw_big(p[n], g, p["m_" + n], p["v_" + n], "adamw_" + tag)

    loss, outs = _small_step(vec_packs, mat_packs, mix_g_blocks, p)
    for kind, o in enumerate(outs):
        o["final_norm_g"] = o["final_norm_g"].reshape(-1)
        for n in SMALL_MATRICES + ("conv_w", "rnn_conv_w"):
            o[n] = o[n][None]
        for n in ("w_in", "w_out", "w_mlp_in", "w_mlp_out"):
            o[n] = big[n][kind][None]
    loss = loss.reshape(())
    return (loss, grad_x[None], *[o[n] for o in outs for n in _NAMES])
```

```python
import functools
import math

import jax
import jax.numpy as jnp
from jax import lax
from jax.experimental import pallas as pl
from jax.experimental.pallas import tpu as pltpu

F32 = jnp.float32
BF16 = jnp.bfloat16
MESH = pl.DeviceIdType.MESH
ANY = pl.BlockSpec(memory_space=pl.ANY)
VMEM = pl.BlockSpec(memory_space=pltpu.VMEM)

EPS = 1e-6
LRU_C = 8.0
D_MODEL = 1024
CONV_W = 512
LRU_W = 1024
IN_COLS = 3 * CONV_W + 2 * LRU_W
IN_SHARD = IN_COLS // 4
N_CHIPS = 4
N_DEVICES = 8
BD = 256
N_BD = LRU_W // BD

ADAM_LR = 0.001
ADAM_B1 = 0.9
ADAM_B2 = 0.999
ADAM_EPS = 1e-08
ADAM_WD = 0.01
ADAM_STEP = 10
ADAM_BC1 = 1.0 - ADAM_B1 ** ADAM_STEP
ADAM_BC2 = 1.0 - ADAM_B2 ** ADAM_STEP

TILE_ROWS, LANES = 8, 128
TOKEN_TILE = 256
MATMUL_TOKEN_TILE = 512
ADAMW_ROWS = 256
VMEM_LIMIT = 56 * 1024 * 1024

PK_G_NORM_RNN, PK_RCONV_B, PK_B_A, PK_B_X, PK_LAMBDA, PK_CONV_W = 0, 1, 2, 3, 4, 5
PK_RCONV_W, PK_G_NORM_CONV = 8, 12
PK_MIX_ROWS = 16
PK_FINAL_G, PK_MLP_G, PK_LOSS = 16, 17, 18
PK_MLP_ROWS = 8
PK_MIX_G = 24
PK_ROWS = 32
N_HEADS, HEAD_DIM = 16, 64


def _params(**kw):
    return pltpu.CompilerParams(vmem_limit_bytes=VMEM_LIMIT, **kw)


def _position():
    x, y, c = lax.axis_index("x"), lax.axis_index("y"), lax.axis_index("c")
    return x, y, c


def _sigmoid(v):
    return 1.0 / (1.0 + jnp.exp(-v))


def _one_minus_square(log_a, a):
    v = 2.0 * log_a
    series = -v * (1.0 + v * (0.5 + v * (1.0 / 6.0)))
    return jnp.where(v > -0.01, series, 1.0 - a * a)


_GELU_C = math.sqrt(2.0 / math.pi)
_GELU_K = 0.044715


def _gelu_and_grad(g):
    th = jnp.tanh(_GELU_C * (g + _GELU_K * g * g * g))
    gelu = 0.5 * g * (1.0 + th)
    dgelu = 0.5 * (1.0 + th) + 0.5 * g * (1.0 - th * th) * (_GELU_C * (1.0 + 3.0 * _GELU_K * g * g))
    return gelu, dgelu


def _rows(shape):
    return lax.broadcasted_iota(jnp.int32, shape, 0)


def _shift_down(v, k, prev8):
    rolled = pltpu.roll(v, k, 0)
    halo = pltpu.roll(prev8, k, 0)
    head = jnp.where(_rows(halo.shape) < k, halo, rolled[:TILE_ROWS])
    return jnp.concatenate([head, rolled[TILE_ROWS:]], axis=0)


def _shift_up(v, k, next8):
    n = v.shape[0]
    rolled = pltpu.roll(v, n - k, 0)
    halo = pltpu.roll(next8, TILE_ROWS - k, 0)
    tail = jnp.where(_rows(halo.shape) >= TILE_ROWS - k, halo, rolled[n - TILE_ROWS:])
    return jnp.concatenate([rolled[: n - TILE_ROWS], tail], axis=0)


def _scan_rows(a, b, carry, reverse=False):
    n, w = a.shape
    groups = n // TILE_ROWS
    a3 = a.reshape(groups, TILE_ROWS, w)
    b3 = b.reshape(groups, TILE_ROWS, w)
    sub = lax.broadcasted_iota(jnp.int32, a3.shape, 1)
    s = 1
    while s < TILE_ROWS:
        shift = TILE_ROWS - s if reverse else s
        keep = (sub < TILE_ROWS - s) if reverse else (sub >= s)
        b3 = b3 + jnp.where(keep, a3 * pltpu.roll(b3, shift, 1), 0.0)
        a3 = a3 * jnp.where(keep, pltpu.roll(a3, shift, 1), 1.0)
        s *= 2
    out = [None] * groups
    edge = 0 if reverse else TILE_ROWS - 1
    for g in (range(groups - 1, -1, -1) if reverse else range(groups)):
        out[g] = b3[g] + a3[g] * carry
        carry = out[g][edge:edge + 1]
    return jnp.concatenate(out, axis=0)


def _softplus_neg(lam):
    e = jnp.exp(-jnp.abs(lam))
    log1p_e = jnp.where(e < 1e-2, e * (1.0 - e * (0.5 - e * (1.0 / 3.0 - e * 0.25))), jnp.log(1.0 + e))
    sp = jnp.maximum(-lam, 0.0) + log1p_e
    dsp = -_sigmoid(-lam)
    return sp, dsp


def _block_diag_dot(vb, w_ref):
    return jnp.concatenate(
        [jnp.dot(vb[:, j * BD:(j + 1) * BD], w_ref[j], preferred_element_type=F32) for j in range(N_BD)], axis=1)


def _block_diag_dot_t(vb, w_ref):
    return jnp.concatenate(
        [lax.dot_general(vb[:, j * BD:(j + 1) * BD], w_ref[j], (((1,), (1,)), ((), ())), preferred_element_type=F32)
         for j in range(N_BD)], axis=1)


def _dot_nt(a, b):
    return lax.dot_general(a, b, (((1,), (1,)), ((), ())), preferred_element_type=F32)


def _dot_tn(a, b):
    return lax.dot_general(a, b, (((0,), (0,)), ((), ())), preferred_element_type=F32)


def _lru_gates(xr, wa_ref, ba, wx_ref, bx, sp):
    xrb = xr.astype(BF16)
    r = _sigmoid(_block_diag_dot(xrb, wa_ref) + ba)
    ig = _sigmoid(_block_diag_dot(xrb, wx_ref) + bx)
    log_a = (-LRU_C) * r * sp
    a = jnp.exp(log_a)
    mult = jnp.sqrt(_one_minus_square(log_a, a))
    return r, ig, a, mult


def _colsum(v):
    return jnp.sum(v, axis=0, keepdims=True)


N_FWD_OUT = 6


def _fwd_mix(h1b, w_in_g, conv_w, rconv_w, rconv_b, wa_bd, b_a, wx_bd, b_x, lam, g_nc, g_nr, later):
    t, d = h1b.shape
    tm = TOKEN_TILE
    nt = t // tm
    nl = len(later)
    assert nl == 3
    pass_on_at = [nt * f // 16 for f in (3, 5, 9)]
    neighbours_at = [nt * f // 16 for f in (10, 11, 12)]
    diagonal_at = [nt * f // 16 for f in (13, 14, 14)]

    def body(h1_ref, win_ref, cw_ref, rw_ref, rb_ref, wa_ref, ba_ref, wx_ref, bx_ref, lam_ref, gnc_ref, gnr_ref, *rest):
        later_in, outs, rest = rest[:nl], rest[nl:nl + N_FWD_OUT], rest[nl + N_FWD_OUT:]
        u_ref, xr_ref, hs_ref, c3_ref, y_ref, gates_ref = outs
        later_out, (cv_prev, xin_prev, h_prev, send_sems, recv_sems) = rest[:nl], rest[nl:]
        del later_in
        step = pl.program_id(0)
        plan = _ShardGather(later_out, send_sems, recv_sems)

        @pl.when(step == 0)
        def _():
            cv_prev[...] = jnp.zeros_like(cv_prev)
            xin_prev[...] = jnp.zeros_like(xin_prev)
            h_prev[...] = jnp.zeros_like(h_prev)
            for w in range(nl):
                plan.start_direct(w)

        for w in range(nl):
            @pl.when(step == pass_on_at[w])
            def _(w=w):
                plan.start_pass_on(w)

            @pl.when(step == neighbours_at[w])
            def _(w=w):
                plan.start_hand_over(w, diagonal=False)

            @pl.when(step == diagonal_at[w])
            def _(w=w):
                plan.start_hand_over(w, diagonal=True)

        h1b = h1_ref[...]
        for j in range(N_CHIPS):
            u_ref[:, j * IN_SHARD:(j + 1) * IN_SHARD] = jnp.dot(h1b, win_ref[j], preferred_element_type=F32)
        gate_b = u_ref[:, 0:CONV_W]
        cv = u_ref[:, CONV_W:2 * CONV_W] * u_ref[:, 2 * CONV_W:3 * CONV_W]
        x_r = u_ref[:, 3 * CONV_W:3 * CONV_W + LRU_W]
        g = u_ref[:, 3 * CONV_W + LRU_W:]

        cw = cw_ref[...]
        cvp = cv_prev[...]
        conv3 = cw[0:1] * _shift_down(cv, 2, cvp) + cw[1:2] * _shift_down(cv, 1, cvp) + cw[2:3] * cv
        cv_prev[...] = cv[tm - TILE_ROWS:]
        c3_ref[...] = conv3
        y_conv = gate_b * conv3

        rw = rw_ref[...]
        xp = xin_prev[...]
        xr = (rw[0:1] * _shift_down(x_r, 3, xp) + rw[1:2] * _shift_down(x_r, 2, xp)
              + rw[2:3] * _shift_down(x_r, 1, xp) + rw[3:4] * x_r) + rb_ref[...]
        xin_prev[...] = x_r[tm - TILE_ROWS:]
        xr_ref[...] = xr
        sp, _ = _softplus_neg(lam_ref[...])
        r, ig, a, mult = _lru_gates(xr, wa_ref, ba_ref[...], wx_ref, bx_ref[...], sp)
        for n, gate in enumerate((r, ig, a, mult)):
            gates_ref[:, n * LRU_W:(n + 1) * LRU_W] = gate
        h = _scan_rows(a, mult * (ig * xr), h_prev[...])
        h_prev[...] = h[tm - 1:tm]
        hs_ref[...] = h
        gelu, _ = _gelu_and_grad(g)
        y_rnn = h * gelu

        na = y_conv * lax.rsqrt(jnp.mean(y_conv * y_conv, axis=-1, keepdims=True) + EPS) * gnc_ref[...]
        nb = y_rnn * lax.rsqrt(jnp.mean(y_rnn * y_rnn, axis=-1, keepdims=True) + EPS) * gnr_ref[...]
        y_ref[:, :CONV_W] = na.astype(BF16)
        y_ref[:, CONV_W:] = nb.astype(BF16)

        @pl.when(step == nt - 1)
        def _():
            for w in range(nl):
                plan.finish(w)

    def full(a):
        nd = a.ndim
        return pl.BlockSpec(a.shape, lambda i: (0,) * nd)

    def tok(cols):
        return pl.BlockSpec((tm, cols), lambda i: (i, 0))

    def act(cols, dtype=F32):
        return jax.ShapeDtypeStruct((t, cols), dtype)

    smalls = (w_in_g, conv_w, rconv_w, rconv_b, wa_bd, b_a, wx_bd, b_x, lam, g_nc, g_nr)
    n_in = 1 + len(smalls)
    outs = pl.pallas_call(
        body, name="fwd_mix", grid=(nt,),
        in_specs=[tok(d)] + [full(a) for a in smalls] + [ANY] * nl,
        out_specs=[tok(IN_COLS), tok(LRU_W), tok(LRU_W), tok(CONV_W), tok(CONV_W + LRU_W)]
        + [tok(4 * LRU_W)] + [ANY] * nl,
        out_shape=[act(IN_COLS), act(LRU_W), act(LRU_W), act(CONV_W), act(CONV_W + LRU_W, BF16)]
        + [act(4 * LRU_W)] + [jax.ShapeDtypeStruct(a.shape, a.dtype) for a in later],
        input_output_aliases={n_in + w: N_FWD_OUT + w for w in range(nl)},
        scratch_shapes=[pltpu.VMEM((TILE_ROWS, CONV_W), F32), pltpu.VMEM((TILE_ROWS, LRU_W), F32),
                        pltpu.VMEM((1, LRU_W), F32), pltpu.SemaphoreType.DMA((nl, _ShardGather.PAIRS)),
                        pltpu.SemaphoreType.DMA((nl, _ShardGather.PAIRS))],
        compiler_params=_params(dimension_semantics=("arbitrary",)),
    )(h1b, *smalls, *later)
    return outs[:N_FWD_OUT], outs[N_FWD_OUT:]


def _mlp_fwd_bwd(x, yb, w_out_g, w1_g, w2_g, g2, gf, target):
    t, d = x.shape
    tm = TOKEN_TILE
    ff = w2_g.shape[0]
    mix = w_out_g.shape[0]
    ffs = ff // N_CHIPS

    def body(x_ref, y_ref, g2_ref, gf_ref, tgt_ref, wout_hbm, w1_hbm, w2_hbm,
             z_ref, dp_ref, h2_ref, dx3b_ref, dx2_ref, dx2b_ref, dy_ref, st_ref, wout, w1, w2, p_ref):
        @pl.when(pl.program_id(0) == 0)
        def _():
            pltpu.sync_copy(wout_hbm, wout)
            pltpu.sync_copy(w1_hbm, w1)
            pltpu.sync_copy(w2_hbm, w2)
            st_ref[...] = jnp.zeros_like(st_ref)

        x2 = x_ref[...] + jnp.dot(y_ref[...], wout[...], preferred_element_type=F32)
        r2 = lax.rsqrt(jnp.mean(x2 * x2, axis=-1, keepdims=True) + EPS)
        xh2 = x2 * r2
        g2v = g2_ref[...]
        h2b = (xh2 * g2v).astype(BF16)
        h2_ref[...] = h2b
        for j in range(N_CHIPS):
            p_ref[:, j * ffs:(j + 1) * ffs] = jnp.dot(h2b, w1[j], preferred_element_type=F32)
        rp = jnp.maximum(p_ref[...], 0.0)
        zb = (rp * rp).astype(BF16)
        z_ref[...] = zb
        x3 = x2 + jnp.dot(zb, w2[...], preferred_element_type=F32)
        r3 = lax.rsqrt(jnp.mean(x3 * x3, axis=-1, keepdims=True) + EPS)
        xh3 = x3 * r3
        gfv = gf_ref[...]
        err = xh3 * gfv - tgt_ref[...]
        loss = (0.5 / d) * jnp.sum(err * err)
        dout = err * (1.0 / d)
        st_ref[PK_FINAL_G - PK_MIX_ROWS:PK_FINAL_G - PK_MIX_ROWS + 1, :] += _colsum(dout * xh3)
        st_ref[PK_LOSS - PK_MIX_ROWS:PK_LOSS - PK_MIX_ROWS + 1, :] += jnp.zeros((1, d), F32) + loss
        dxh3 = dout * gfv
        dx3 = r3 * (dxh3 - xh3 * jnp.mean(dxh3 * xh3, axis=-1, keepdims=True))
        dx3b = dx3.astype(BF16)
        dx3b_ref[...] = dx3b
        dpb = (_dot_nt(dx3b, w2[...]) * (2.0 * rp)).astype(BF16)
        dp_ref[...] = dpb
        dh2 = _dot_nt(dpb[:, 0:ffs], w1[0])
        for j in range(1, N_CHIPS):
            dh2 = dh2 + _dot_nt(dpb[:, j * ffs:(j + 1) * ffs], w1[j])
        st_ref[PK_MLP_G - PK_MIX_ROWS:PK_MLP_G - PK_MIX_ROWS + 1, :] += _colsum(dh2 * xh2)
        dxh2 = dh2 * g2v
        dx2 = dx3 + r2 * (dxh2 - xh2 * jnp.mean(dxh2 * xh2, axis=-1, keepdims=True))
        dx2_ref[...] = dx2
        dx2b = dx2.astype(BF16)
        dx2b_ref[...] = dx2b
        dy_ref[...] = _dot_nt(dx2b, wout[...])

    def tok(cols):
        return pl.BlockSpec((tm, cols), lambda i: (i, 0))

    def row(cols):
        return pl.BlockSpec((1, cols), lambda i: (0, 0))

    return pl.pallas_call(
        body, name="mlp_fwd_bwd", grid=(t // tm,),
        in_specs=[tok(d), tok(mix), row(d), row(d), tok(d), ANY, ANY, ANY],
        out_specs=[tok(ff), tok(ff), tok(d), tok(d), tok(d), tok(d), tok(mix),
                   pl.BlockSpec((PK_MLP_ROWS, d), lambda i: (0, 0))],
        out_shape=[jax.ShapeDtypeStruct((t, ff), BF16), jax.ShapeDtypeStruct((t, ff), BF16),
                   jax.ShapeDtypeStruct((t, d), BF16), jax.ShapeDtypeStruct((t, d), BF16),
                   jax.ShapeDtypeStruct((t, d), F32), jax.ShapeDtypeStruct((t, d), BF16),
                   jax.ShapeDtypeStruct((t, mix), F32), jax.ShapeDtypeStruct((PK_MLP_ROWS, d), F32)],
        scratch_shapes=[pltpu.VMEM(w_out_g.shape, BF16), pltpu.VMEM(w1_g.shape, BF16), pltpu.VMEM(w2_g.shape, BF16),
                        pltpu.VMEM((tm, ff), F32)],
        compiler_params=_params(dimension_semantics=("arbitrary",)),
    )(x, yb, g2, gf, target, w_out_g, w1_g, w2_g)


def _mix_bwd(dy, u, xr_all, hs_all, c3_all, gates, conv_w, rconv_w, wa_bd, wx_bd, lam, g_nc, g_nr, st_mlp, parts):
    t = dy.shape[0]
    tm = TOKEN_TILE
    nt = t // tm
    hb = tm // TILE_ROWS
    npart = len(parts)

    def body(dy_ref, u_ref, uh_ref, xr_ref, hs_ref, hh_ref, c3_ref, gates_ref,
             cw_ref, rw_ref, wa_ref, wx_ref, lam_ref, gnc_ref, gnr_ref, stm_ref, *rest):
        part_refs, (du_ref, st_ref, heads_ref), rest = rest[:npart], rest[npart:npart + 3], rest[npart + 3:]
        arrived_refs, (dc_next, a_next, gs_next, dxr_next, dwa_ref, dwx_ref, send_sems, recv_sems) = rest[:npart], rest[npart:]
        exchange = _PartialExchange(part_refs, arrived_refs, send_sems, recv_sems)
        i = pl.program_id(0)

        @pl.when(i == 0)
        def _():
            exchange.start()
            dc_next[...] = jnp.zeros_like(dc_next)
            a_next[...] = jnp.zeros_like(a_next)
            gs_next[...] = jnp.zeros_like(gs_next)
            dxr_next[...] = jnp.zeros_like(dxr_next)
            st_ref[0:PK_MIX_ROWS, :] = jnp.zeros((PK_MIX_ROWS, LRU_W), F32)
            st_ref[PK_MIX_ROWS:, :] = stm_ref[...]
            dwa_ref[...] = jnp.zeros_like(dwa_ref)
            dwx_ref[...] = jnp.zeros_like(dwx_ref)

        first_tile = i == nt - 1
        gate_b = u_ref[:, 0:CONV_W]
        gate_c = u_ref[:, CONV_W:2 * CONV_W]
        v = u_ref[:, 2 * CONV_W:3 * CONV_W]
        x_r = u_ref[:, 3 * CONV_W:3 * CONV_W + LRU_W]
        g = u_ref[:, 3 * CONV_W + LRU_W:]
        cv = gate_c * v
        cv_prev = jnp.where(first_tile, 0.0, uh_ref[:, CONV_W:2 * CONV_W] * uh_ref[:, 2 * CONV_W:3 * CONV_W])
        xin_prev = jnp.where(first_tile, 0.0, uh_ref[:, 3 * CONV_W:3 * CONV_W + LRU_W])
        hs_prev = jnp.where(first_tile, 0.0, hh_ref[...])

        def acc(first_row, val, width=LRU_W, row=0):
            r0 = first_row + row
            st_ref[r0:r0 + 1, 0:width] += val

        conv3 = c3_ref[...]
        y_conv = gate_b * conv3
        ra = lax.rsqrt(jnp.mean(y_conv * y_conv, axis=-1, keepdims=True) + EPS)
        xha = y_conv * ra
        dna = dy_ref[:, :CONV_W]
        acc(PK_G_NORM_CONV, _colsum(dna * xha), CONV_W)
        dxha = dna * gnc_ref[...]
        dy_conv = ra * (dxha - xha * jnp.mean(dxha * xha, axis=-1, keepdims=True))
        du_ref[:, 0:CONV_W] = (dy_conv * conv3).astype(BF16)
        dc = dy_conv * gate_b
        cw = cw_ref[...]
        dcn = dc_next[...]
        dcv = cw[2:3] * dc + cw[1:2] * _shift_up(dc, 1, dcn) + cw[0:1] * _shift_up(dc, 2, dcn)
        dc_next[...] = dc[:TILE_ROWS]
        acc(PK_CONV_W, _colsum(dc * _shift_down(cv, 2, cv_prev)), CONV_W, 0)
        acc(PK_CONV_W, _colsum(dc * _shift_down(cv, 1, cv_prev)), CONV_W, 1)
        acc(PK_CONV_W, _colsum(dc * cv), CONV_W, 2)
        du_ref[:, CONV_W:2 * CONV_W] = (dcv * v).astype(BF16)
        du_ref[:, 2 * CONV_W:3 * CONV_W] = (dcv * gate_c).astype(BF16)

        hs = hs_ref[...]
        gelu, dgelu = _gelu_and_grad(g)
        y_rnn = hs * gelu
        rb = lax.rsqrt(jnp.mean(y_rnn * y_rnn, axis=-1, keepdims=True) + EPS)
        xhb = y_rnn * rb
        dnb = dy_ref[:, CONV_W:]
        acc(PK_G_NORM_RNN, _colsum(dnb * xhb))
        dxhb = dnb * gnr_ref[...]
        dy_rnn = rb * (dxhb - xhb * jnp.mean(dxhb * xhb, axis=-1, keepdims=True))
        du_ref[:, 3 * CONV_W + LRU_W:] = (dy_rnn * hs * dgelu).astype(BF16)
        dh = dy_rnn * gelu

        xr = xr_ref[...]
        xrb = xr.astype(BF16)
        sp, dsp = _softplus_neg(lam_ref[...])
        r, ig, a, mult = [gates_ref[:, n * LRU_W:(n + 1) * LRU_W] for n in range(4)]
        a_up = _shift_up(a, 1, a_next[...])
        a_next[...] = a[:TILE_ROWS]
        gs = _scan_rows(a_up, dh, gs_next[0:1, :], reverse=True)
        gs_next[...] = gs[:TILE_ROWS]
        da = gs * _shift_down(hs, 1, hs_prev)
        gx = gs * xr
        di = gx * mult
        dmult = gx * ig
        dxr = gs * (mult * ig)
        dlog_a = da * a - dmult * ((a * a) / mult)
        acc(PK_LAMBDA, _colsum(dlog_a * r) * ((-LRU_C) * dsp))
        dpa = (dlog_a * ((-LRU_C) * sp)) * (r * (1.0 - r))
        dpx = di * (ig * (1.0 - ig))
        acc(PK_B_A, _colsum(dpa))
        acc(PK_B_X, _colsum(dpx))
        dpab = dpa.astype(BF16)
        dpxb = dpx.astype(BF16)
        dxr = dxr + _block_diag_dot_t(dpab, wa_ref) + _block_diag_dot_t(dpxb, wx_ref)
        for j in range(N_BD):
            cols = slice(j * BD, (j + 1) * BD)
            dwa_ref[j] += _dot_tn(xrb[:, cols], dpab[:, cols])
            dwx_ref[j] += _dot_tn(xrb[:, cols], dpxb[:, cols])

        acc(PK_RCONV_B, _colsum(dxr))
        rw = rw_ref[...]
        dxn = dxr_next[...]
        dx_r = (rw[3:4] * dxr + rw[2:3] * _shift_up(dxr, 1, dxn) + rw[1:2] * _shift_up(dxr, 2, dxn)
                + rw[0:1] * _shift_up(dxr, 3, dxn))
        dxr_next[...] = dxr[:TILE_ROWS]
        for k in range(3):
            acc(PK_RCONV_W, _colsum(dxr * _shift_down(x_r, 3 - k, xin_prev)), LRU_W, k)
        acc(PK_RCONV_W, _colsum(dxr * x_r), LRU_W, 3)
        du_ref[:, 3 * CONV_W:3 * CONV_W + LRU_W] = dx_r.astype(BF16)

        @pl.when(i == nt - 1)
        def _():
            for n, d_ref in enumerate((dwa_ref, dwx_ref)):
                for b in range(N_BD):
                    for q in range(BD // HEAD_DIM):
                        lane0 = q * HEAD_DIM // LANES * LANES
                        wide = d_ref[b, q * HEAD_DIM:(q + 1) * HEAD_DIM, lane0:lane0 + LANES]
                        if q * HEAD_DIM != lane0:
                            wide = pltpu.roll(wide, LANES - (q * HEAD_DIM - lane0), axis=1)
                        heads_ref[n, b * (BD // HEAD_DIM) + q] = wide[:, 0:HEAD_DIM].astype(BF16)
            exchange.wait()

    def full(a):
        nd = a.ndim
        return pl.BlockSpec(a.shape, lambda i: (0,) * nd)

    def tok(cols):
        return pl.BlockSpec((tm, cols), lambda i: (nt - 1 - i, 0))

    def halo(cols):
        return pl.BlockSpec((TILE_ROWS, cols), lambda i: (jnp.maximum((nt - 1 - i) * hb - 1, 0), 0))

    smalls = (conv_w, rconv_w, wa_bd, wx_bd, lam, g_nc, g_nr, st_mlp)
    st_rows = PK_MIX_ROWS + st_mlp.shape[0]
    heads = (2, N_HEADS, HEAD_DIM, HEAD_DIM)
    outs = pl.pallas_call(
        body, name="mix_bwd", grid=(nt,),
        in_specs=[tok(CONV_W + LRU_W), tok(IN_COLS), halo(IN_COLS), tok(LRU_W), tok(LRU_W), halo(LRU_W), tok(CONV_W)]
        + [tok(4 * LRU_W)] + [full(a) for a in smalls] + [ANY] * npart,
        out_specs=[tok(IN_COLS), pl.BlockSpec((st_rows, LRU_W), lambda i: (0, 0)),
                   pl.BlockSpec(heads, lambda i: (0, 0, 0, 0))]
        + [ANY] * npart,
        out_shape=[jax.ShapeDtypeStruct((t, IN_COLS), BF16), jax.ShapeDtypeStruct((st_rows, LRU_W), F32),
                   jax.ShapeDtypeStruct(heads, BF16)]
        + [jax.ShapeDtypeStruct(a.shape, a.dtype) for a in parts],
        scratch_shapes=[pltpu.VMEM((TILE_ROWS, CONV_W), F32), pltpu.VMEM((TILE_ROWS, LRU_W), F32),
                        pltpu.VMEM((TILE_ROWS, LRU_W), F32), pltpu.VMEM((TILE_ROWS, LRU_W), F32),
                        pltpu.VMEM((N_BD, BD, BD), F32), pltpu.VMEM((N_BD, BD, BD), F32),
                        pltpu.SemaphoreType.DMA((npart, 3)), pltpu.SemaphoreType.DMA((npart, 3))],
        compiler_params=_params(dimension_semantics=("arbitrary",)),
    )(dy, u, u, xr_all, hs_all, hs_all, c3_all, gates, *smalls, *parts)
    return outs[:3], outs[3:]


def _in_bwd(dub, w_in_g, x, dx2, g1, parts, joins, core_chip):
    t, d = x.shape
    tm = min(t, MATMUL_TOKEN_TILE)
    nt = t // tm
    npart = len(parts)
    nj = len(joins)
    geometry = []
    for tag, shape, _, _ in joins:
        pr, pc = WGRAD_GEOMETRY[tag][:2]
        every = 1 if pr % (nt * 16) == 0 else 2
        geometry.append((pr, pc, pr * every // nt, every, shape[1] == pc))

    def body(cc_ref, du_ref, win_ref, x_ref, dx2_ref, g1_ref, *rest):
        sums, rest = [rest[4 * w:4 * w + 4] for w in range(nj)], rest[4 * nj:]
        part_refs, (gx_ref, st_ref), rest = rest[:npart], rest[npart:npart + 2], rest[npart + 2:]
        arrived_refs, joined, rest = rest[:npart], rest[npart:npart + nj], rest[npart + nj:]
        stages, (send_sems, recv_sems, j_local, j_send, j_recv) = rest[:nj], rest[nj:]
        exchange = _PartialExchange(part_refs, arrived_refs, send_sems, recv_sems)
        i = pl.program_id(0)
        c = cc_ref[0]

        def window(w, core, row0, rows):
            pr, pc, _, _, by_rows = geometry[w]
            if by_rows:
                return joined[w].at[pl.ds(core * pr + row0, rows), :]
            return joined[w].at[pl.ds(row0, rows), pl.ds(core * pc, pc)]

        def to_sibling(w, src, core, row0, rows):
            return pltpu.make_async_remote_copy(src_ref=src, dst_ref=window(w, core, row0, rows), send_sem=j_send.at[w],
                                                recv_sem=j_recv.at[w], device_id=_sibling(), device_id_type=MESH)

        @pl.when(i == 0)
        def _():
            exchange.start()
            st_ref[...] = jnp.zeros_like(st_ref)

        for w in range(nj):
            pr, pc, rb, every, _ = geometry[w]

            @pl.when(i % every == 0)
            def _(w=w, rb=rb, every=every):
                p_ref, r1_ref, r2_ref, r3_ref = sums[w]
                row0 = pl.multiple_of((i // every) * rb, rb)
                rows = stages[w].at[pl.ds(row0, rb), :]
                rows[...] = ((p_ref[0] + r1_ref[0].astype(F32)) + r2_ref[0].astype(F32)) + r3_ref[0].astype(F32)
                pltpu.make_async_copy(rows, window(w, c, row0, rb), j_local.at[w]).start()
                to_sibling(w, rows, c, row0, rb).start()

        dh1 = _dot_nt(du_ref[:, 0:IN_SHARD], win_ref[0])
        for j in range(1, N_CHIPS):
            dh1 = dh1 + _dot_nt(du_ref[:, j * IN_SHARD:(j + 1) * IN_SHARD], win_ref[j])
        xv = x_ref[...]
        rstd = lax.rsqrt(jnp.mean(xv * xv, axis=-1, keepdims=True) + EPS)
        xh = xv * rstd
        st_ref[0:1, :] += _colsum(dh1 * xh)
        dxh = dh1 * g1_ref[...]
        gx_ref[...] = dx2_ref[...] + rstd * (dxh - xh * jnp.mean(dxh * xh, axis=-1, keepdims=True))

        @pl.when(i == nt - 1)
        def _():
            exchange.wait()
            for w in range(nj):
                pr = geometry[w][0]
                pltpu.make_async_copy(stages[w], window(w, c, 0, pr), j_local.at[w]).wait()
                to_sibling(w, stages[w], 1 - c, 0, pr).wait()

    def tok(cols):
        return pl.BlockSpec((tm, cols), lambda i, cc: (i, 0))

    def partial(w, off):
        pr, pc, rb, every, _ = geometry[w]
        return pl.BlockSpec((1, rb, pc), lambda i, cc: ((cc[1] + off) % N_CHIPS, i // every, 0))

    sum_specs, sum_operands = [], []
    for w, (_, _, own, arrived) in enumerate(joins):
        sum_specs += [partial(w, off) for off in range(N_CHIPS)]
        sum_operands += [own, arrived, arrived, arrived]
    dma = pltpu.SemaphoreType.DMA
    outs = pl.pallas_call(
        body, name="in_bwd",
        grid_spec=pltpu.PrefetchScalarGridSpec(
            num_scalar_prefetch=1, grid=(nt,),
            in_specs=[tok(IN_COLS), pl.BlockSpec(w_in_g.shape, lambda i, cc: (0, 0, 0)), tok(d), tok(d),
                      pl.BlockSpec((1, d), lambda i, cc: (0, 0))] + sum_specs + [ANY] * npart,
            out_specs=[tok(d), pl.BlockSpec((TILE_ROWS, d), lambda i, cc: (0, 0))] + [ANY] * (npart + nj),
            scratch_shapes=[pltpu.VMEM((g[0], g[1]), F32) for g in geometry]
            + [dma((npart, 3)), dma((npart, 3)), dma((nj,)), dma((nj,)), dma((nj,))]),
        out_shape=[jax.ShapeDtypeStruct((t, d), F32), jax.ShapeDtypeStruct((TILE_ROWS, d), F32)]
        + [jax.ShapeDtypeStruct(a.shape, a.dtype) for a in parts]
        + [jax.ShapeDtypeStruct(shape, F32) for _, shape, _, _ in joins],
        compiler_params=_params(dimension_semantics=("arbitrary",)),
    )(core_chip, dub, w_in_g, x, dx2, g1, *sum_operands, *parts)
    return outs[:2], outs[2:2 + npart], outs[2 + npart:]


WGRAD_GEOMETRY = {
    "in": (512, IN_SHARD, lambda s, h: h, lambda s, h: s),
    "mlp_in": (512, D_MODEL, lambda s, h: h, lambda s, h: s),
    "mlp_out": (512, D_MODEL, lambda s, h: 2 * s + h, lambda s, h: 0),
    "out": (384, 512, lambda s, h: s, lambda s, h: h),
}
K_CHUNK = 512
TOKEN_STREAMS = 2


def _sibling():
    x, y, c = _position()
    return (x, y, 1 - c)


def _wgrad(a, b, tag, core_chip, packs=(), parts=()):
    t = a.shape[0]
    pr, pc, a_blk, b_blk = WGRAD_GEOMETRY[tag]
    ns = TOKEN_STREAMS
    ts = t // ns
    kc = min(K_CHUNK, ts)
    mine = N_CHIPS
    riding = len(packs)
    npart = len(parts)
    assert not (riding and npart)

    def body(cc_ref, *rest):
        a_refs, b_refs, rest = rest[:ns], rest[ns:2 * ns], rest[2 * ns:]
        if riding:
            pack_refs, (land_ref, p_ref, pb_ref), rest = rest[:riding], rest[riding:riding + 3], rest[riding + 3:]
            all_refs, (stage, rbuf, send_sems, recv_sems, rsem), g_sems = rest[:riding], rest[riding:riding + 5], rest[riding + 5:]
            gathers = [_PackGather(pack_refs[n], all_refs[n], *g_sems[3 * n:3 * n + 3]) for n in range(riding)]
        elif npart:
            part_refs, (land_ref, p_ref, pb_ref), rest = rest[:npart], rest[npart:npart + 3], rest[npart + 3:]
            arrived_refs, (stage, rbuf, send_sems, recv_sems, rsem, x_send, x_recv) = rest[:npart], rest[npart:]
            exchange = _PartialExchange(part_refs, arrived_refs, x_send, x_recv)
        else:
            land_ref, p_ref, pb_ref, stage, rbuf, send_sems, recv_sems, rsem = rest
        ph, s = pl.program_id(0), pl.program_id(1)
        if riding:
            @pl.when((ph == 0) & (s == 0))
            def _():
                for gather in gathers:
                    gather.start()

            @pl.when((ph == 1) & (s == N_CHIPS - 2))
            def _():
                for gather in gathers:
                    gather.hand_over()
        if npart:
            @pl.when((ph == 0) & (s == 0))
            def _():
                exchange.start()
        def push(k):
            return pltpu.make_async_remote_copy(src_ref=stage.at[k], dst_ref=land_ref.at[k], send_sem=send_sems.at[k],
                                                recv_sem=recv_sems.at[k], device_id=_sibling(), device_id_type=MESH)

        def landed():
            return pltpu.make_async_copy(land_ref.at[s], rbuf, rsem)

        @pl.when(ph == 1)
        def _():
            push(s).wait_recv()
            landed().start()

        slot = jnp.where(ph == 0, s, mine)
        acc = stage.at[slot]
        chunks = [(a_ref, b_ref, k) for a_ref, b_ref in zip(a_refs, b_refs) for k in range(0, ts, kc)]
        for n, (a_ref, b_ref, k) in enumerate(chunks):
            part = _dot_tn(a_ref[k:k + kc, :], b_ref[k:k + kc, :])
            if n == 0:
                acc[...] = part
            else:
                acc[...] += part

        @pl.when(ph == 0)
        def _():
            push(s).start()

        @pl.when(ph == 1)
        def _():
            landed().wait()
            p = stage[mine] + rbuf[...]
            p_ref[0] = p
            pb_ref[0] = p.astype(BF16)

        @pl.when((ph == 1) & (s == N_CHIPS - 1))
        def _():
            for k in range(N_CHIPS):
                push(k).wait_send()
            for gather in (gathers if riding else ()):
                gather.finish()
            if npart:
                exchange.wait()

    def half(ph, cc):
        return jnp.where(ph == 0, 1 - cc[0], cc[0])

    def out_slot(ph, s, cc):
        return (jnp.where(ph == 0, 0, s), 0, 0)

    piece = jax.ShapeDtypeStruct((N_CHIPS, pr, pc), F32)
    in_specs = [pl.BlockSpec((ts, pr), lambda ph, s, cc, n=n: (n, a_blk(s, half(ph, cc)))) for n in range(ns)]
    in_specs += [pl.BlockSpec((ts, pc), lambda ph, s, cc, n=n: (n, b_blk(s, half(ph, cc)))) for n in range(ns)]
    out_specs = [ANY, pl.BlockSpec((1, pr, pc), out_slot), pl.BlockSpec((1, pr, pc), out_slot)]
    out_shape = [piece, piece, jax.ShapeDtypeStruct((N_CHIPS, pr, pc), BF16)]
    scratch = [pltpu.VMEM((N_CHIPS + 1, pr, pc), F32), pltpu.VMEM((pr, pc), F32),
               pltpu.SemaphoreType.DMA((N_CHIPS,)), pltpu.SemaphoreType.DMA((N_CHIPS,)), pltpu.SemaphoreType.DMA]
    operands = [a] * ns + [b] * ns
    for pack in packs:
        in_specs.append(pl.BlockSpec(pack.shape, lambda ph, s, cc, nd=pack.ndim: (0,) * nd))
        out_specs.append(ANY)
        out_shape.append(jax.ShapeDtypeStruct((N_DEVICES,) + pack.shape, pack.dtype))
        operands.append(pack)
    for pack in packs:
        scratch += _PackGather.semaphores()
    if npart:
        in_specs += [ANY] * npart
        out_specs += [ANY] * npart
        out_shape += [jax.ShapeDtypeStruct(p.shape, p.dtype) for p in parts]
        scratch += [pltpu.SemaphoreType.DMA((npart, 3)), pltpu.SemaphoreType.DMA((npart, 3))]
        operands += list(parts)
    return pl.pallas_call(
        body, name="wgrad_" + tag,
        grid_spec=pltpu.PrefetchScalarGridSpec(
            num_scalar_prefetch=1, grid=(2, N_CHIPS), in_specs=in_specs, out_specs=out_specs, scratch_shapes=scratch),
        out_shape=out_shape,
        compiler_params=_params(dimension_semantics=("arbitrary", "arbitrary")),
    )(core_chip, *operands)[1:]


def _other_chips(x, y):
    return [(1 - x, y), (x, 1 - y), (1 - x, 1 - y)]


class _ShardGather:
    PAIRS = 9

    def __init__(self, outs, send_sems, recv_sems):
        self.outs, self.send_sems, self.recv_sems = outs, send_sems, recv_sems
        x, y, c = _position()
        self.c, self.j = c, 2 * x + y
        self.sibling = (x, y, 1 - c)
        self.chips = _other_chips(x, y)

    def _chip(self, k):
        px, py = self.chips[k]
        return 2 * px + py

    def _half(self, w, chip, which):
        hr = self.outs[w].shape[1] // 2
        return self.outs[w].at[chip, pl.ds(which * hr, hr), :]

    def _quarter(self, w, chip, q):
        qr = self.outs[w].shape[1] // 4
        return self.outs[w].at[chip, pl.ds(self.c * 2 * qr + q * qr, qr), :]

    def _copy(self, ref, w, pair, to, src=None):
        return pltpu.make_async_remote_copy(src_ref=ref if src is None else src, dst_ref=ref, send_sem=self.send_sems.at[w, pair],
                                            recv_sem=self.recv_sems.at[w, pair], device_id=to, device_id_type=MESH)

    def direct(self, w, k, q, src=None):
        return self._copy(self._quarter(w, self.j, q), w, 2 * k + q, (*self.chips[k], self.c), src)

    def direct_landed(self, w, k, q):
        return self._copy(self._quarter(w, self._chip(k), q), w, 2 * k + q, (*self.chips[k], self.c))

    def pass_on(self, w, q):
        return self._copy(self._quarter(w, self._chip(q), q), w, 4 + q, (*self.chips[1 - q], self.c))

    def passed_landed(self, w, q):
        return self._copy(self._quarter(w, self._chip(2), q), w, 4 + q, (*self.chips[1 - q], self.c))

    def hand_over(self, w, k):
        return self._copy(self._half(w, self._chip(k), self.c), w, 6 + k, self.sibling)

    def handed(self, w, k):
        return self._copy(self._half(w, self._chip(k), 1 - self.c), w, 6 + k, self.sibling)

    def start_direct(self, w, src_half=None):
        qr = self.outs[w].shape[1] // 4
        for k, q in ((0, 0), (1, 1), (0, 1), (1, 0)):
            self.direct(w, k, q, None if src_half is None else src_half.at[pl.ds(q * qr, qr), :]).start()

    def start_pass_on(self, w):
        for q in (0, 1):
            self.direct_landed(w, q, q).wait_recv()
            self.pass_on(w, q).start()

    def start_hand_over(self, w, diagonal):
        if diagonal:
            for q in (0, 1):
                self.passed_landed(w, q).wait_recv()
            self.hand_over(w, 2).start()
        else:
            for k in (0, 1):
                self.direct_landed(w, k, 1 - k).wait_recv()
                self.hand_over(w, k).start()

    def finish(self, w):
        for k in range(3):
            self.handed(w, k).wait_recv()
            self.hand_over(w, k).wait_send()
        for q in (0, 1):
            self.pass_on(w, q).wait_send()
            for k in (0, 1):
                self.direct(w, k, q).wait_send()


def _gather_first(w_in, w_out, w1, w2, conv_w, rconv_w, w_a, w_x, x, g1):
    t, d = x.shape
    tn = min(t, MATMUL_TOKEN_TILE)
    n_tiles = t // tn
    bigs = (w_in, w_out, w1, w2)
    convs = (conv_w, rconv_w)
    heads = (w_a, w_x)
    nb, nc = len(bigs), len(convs)

    def body(win_ref, wout_hbm, w1_hbm, w2_hbm, cw_ref, rw_ref, wa_ref, wx_ref, x_hbm, g1_ref, gin, gout, g1, g2, gcw, grw,
             bda, bdx, h1_hbm, st_in, st_out, st_1, st_2, f_out, f_1, f_2, st_cw, st_rw, xbuf, hbuf, send_sems, recv_sems,
             sm_send, sm_recv, local_sems, load_sems, x_sems, h_sems):
        stages = (st_in, st_out, st_1, st_2)
        outs = (gin, gout, g1, g2)
        conv_stages, conv_outs = (st_cw, st_rw), (gcw, grw)
        plan = _ShardGather(outs[:1], send_sems, recv_sems)
        j, c = plan.j, plan.c
        local = [pltpu.make_async_copy(stages[w], outs[w].at[j], local_sems.at[w]) for w in range(nb)]
        loads = [pltpu.make_async_copy(src, dst, load_sems.at[n])
                 for n, (src, dst) in enumerate(((wout_hbm, f_out), (w1_hbm, f_1), (w2_hbm, f_2)))]

        def columns(n, chip):
            width = convs[n].shape[1]
            return conv_outs[n].at[:, pl.ds(chip * width, width)]

        def x_load(i):
            return pltpu.make_async_copy(x_hbm.at[pl.ds(i * tn, tn), :], xbuf.at[i % 2], x_sems.at[i % 2])

        def h1_store(i):
            return pltpu.make_async_copy(hbuf.at[i % 2], h1_hbm.at[pl.ds(i * tn, tn), :], h_sems.at[i % 2])

        def first_norm():
            x_load(0).start()
            for i in range(n_tiles):
                if i + 1 < n_tiles:
                    x_load(i + 1).start()
                x_load(i).wait()
                if i >= 2:
                    h1_store(i - 2).wait()
                xv = xbuf[i % 2]
                rstd = lax.rsqrt(jnp.mean(xv * xv, axis=-1, keepdims=True) + EPS)
                hbuf[i % 2] = ((xv * rstd) * g1_ref[...]).astype(BF16)
                h1_store(i).start()
            for i in range(max(n_tiles - 2, 0), n_tiles):
                h1_store(i).wait()

        local += [pltpu.make_async_copy(conv_stages[n], columns(n, j), local_sems.at[nb + n]) for n in range(nc)]

        def small_copy(k, n, landed=False):
            px, py = plan.chips[k]
            return pltpu.make_async_remote_copy(
                src_ref=conv_stages[n], dst_ref=columns(n, 2 * px + py if landed else j), send_sem=sm_send.at[k, n],
                recv_sem=sm_recv.at[k, n], device_id=(px, py, c), device_id_type=MESH)

        for cp in loads:
            cp.start()
        hr = w_in.shape[0] // 2
        st_in[...] = win_ref[...].astype(BF16)
        plan.start_direct(0, st_in.at[pl.ds(c * hr, hr), :])
        for src, st in zip((cw_ref, rw_ref), conv_stages):
            st[...] = jnp.zeros_like(st)
            st[0:src.shape[0], :] = src[...]
        for k in range(3):
            for n in range(nc):
                small_copy(k, n).start()
        for src, bd in ((wa_ref, bda), (wx_ref, bdx)):
            bd[...] = jnp.zeros_like(bd)
            for h in range(N_HEADS):
                q = h % (BD // HEAD_DIM)
                bd[h // (BD // HEAD_DIM), q * HEAD_DIM:(q + 1) * HEAD_DIM, q * HEAD_DIM:(q + 1) * HEAD_DIM] = src[h].astype(BF16)
        for cp, full, st in zip(loads, (f_out, f_1, f_2), stages[1:]):
            cp.wait()
            st[...] = full[...].astype(BF16)
        for cp in local:
            cp.start()
        plan.start_pass_on(0)
        first_norm()
        plan.start_hand_over(0, diagonal=False)
        plan.start_hand_over(0, diagonal=True)
        for k in range(3):
            for n in range(nc):
                small_copy(k, n, landed=True).wait_recv()
                small_copy(k, n).wait_send()
        plan.finish(0)
        for cp in local:
            cp.wait()

    def gathered(a, dtype):
        return jax.ShapeDtypeStruct((N_CHIPS,) + a.shape, dtype)

    return pl.pallas_call(
        body, name="gather_first",
        in_specs=[VMEM] + [ANY] * (nb - 1) + [VMEM] * (nc + len(heads)) + [ANY, VMEM],
        out_specs=[ANY] * (nb + nc) + [VMEM] * len(heads) + [ANY],
        out_shape=[gathered(a, BF16) for a in bigs]
        + [jax.ShapeDtypeStruct((TILE_ROWS, N_CHIPS * a.shape[1]), F32) for a in convs]
        + [jax.ShapeDtypeStruct((N_BD, BD, BD), BF16) for _ in heads] + [jax.ShapeDtypeStruct((t, d), BF16)],
        scratch_shapes=[pltpu.VMEM(a.shape, BF16) for a in bigs] + [pltpu.VMEM(a.shape, F32) for a in bigs[1:]]
        + [pltpu.VMEM((TILE_ROWS, a.shape[1]), F32) for a in convs]
        + [pltpu.VMEM((2, tn, d), F32), pltpu.VMEM((2, tn, d), BF16)]
        + [pltpu.SemaphoreType.DMA((1, _ShardGather.PAIRS)), pltpu.SemaphoreType.DMA((1, _ShardGather.PAIRS)),
           pltpu.SemaphoreType.DMA((3, nc)), pltpu.SemaphoreType.DMA((3, nc)), pltpu.SemaphoreType.DMA((nb + nc,)),
           pltpu.SemaphoreType.DMA((nb - 1,)), pltpu.SemaphoreType.DMA((2,)), pltpu.SemaphoreType.DMA((2,))],
        compiler_params=_params(),
    )(*bigs, *convs, *heads, x, g1)


class _PartialExchange:
    def __init__(self, parts, arrived, send_sems, recv_sems):
        self.parts, self.arrived, self.send_sems, self.recv_sems = parts, arrived, send_sems, recv_sems
        x, y, c = _position()
        self.c, self.j = c, 2 * x + y
        self.chips = _other_chips(x, y)

    def _copy(self, w, k, slot):
        px, py = self.chips[k]
        return pltpu.make_async_remote_copy(
            src_ref=self.parts[w].at[2 * px + py], dst_ref=self.arrived[w].at[slot], send_sem=self.send_sems.at[w, k],
            recv_sem=self.recv_sems.at[w, k], device_id=(px, py, self.c), device_id_type=MESH)

    def start(self):
        for w in range(len(self.parts)):
            for k in range(3):
                self._copy(w, k, self.j).start()

    def wait(self):
        for w in range(len(self.parts)):
            for k in range(3):
                px, py = self.chips[k]
                self._copy(w, k, 2 * px + py).wait()


class _PackGather:
    def __init__(self, p_ref, all_ref, send_sems, recv_sems, local_sem):
        self.p_ref, self.all_ref, self.send_sems, self.recv_sems, self.local_sem = p_ref, all_ref, send_sems, recv_sems, local_sem
        x, y, c = _position()
        self.me, self.sibling, self.c = (x, y, c), (x, y, 1 - c), c
        self.chips = _other_chips(x, y)

    @staticmethod
    def semaphores():
        return [pltpu.SemaphoreType.DMA((7,)), pltpu.SemaphoreType.DMA((7,)), pltpu.SemaphoreType.DMA]

    def _copy(self, k, block, to, from_pack=False):
        px, py, pc = block
        slot = self.all_ref.at[4 * px + 2 * py + pc]
        return pltpu.make_async_remote_copy(src_ref=self.p_ref if from_pack else slot, dst_ref=slot, send_sem=self.send_sems.at[k],
                                            recv_sem=self.recv_sems.at[k], device_id=to, device_id_type=MESH)

    def _mine(self):
        x, y, c = self.me
        return pltpu.make_async_copy(self.p_ref, self.all_ref.at[4 * x + 2 * y + c], self.local_sem)

    def _first(self):
        return [self._copy(0, self.me, self.sibling, True)] + [
            self._copy(1 + k, self.me, (*chip, self.c), True) for k, chip in enumerate(self.chips)]

    def _passed(self):
        return [self._copy(4 + k, (*chip, self.c), self.sibling) for k, chip in enumerate(self.chips)]

    def start(self):
        self._mine().start()
        for cp in self._first():
            cp.start()

    def hand_over(self):
        for k, chip in enumerate(self.chips):
            self._copy(1 + k, (*chip, self.c), self.me).wait_recv()
            self._passed()[k].start()

    def finish(self):
        self._copy(0, self.sibling, self.me).wait_recv()
        for k, chip in enumerate(self.chips):
            self._copy(4 + k, (*chip, 1 - self.c), self.me).wait_recv()
        for cp in self._first() + self._passed():
            cp.wait_send()
        self._mine().wait()


class _DirectGather:
    def __init__(self, p_ref, all_ref, send_sems, recv_sems, local_sem):
        self.p_ref, self.all_ref, self.send_sems, self.recv_sems, self.local_sem = p_ref, all_ref, send_sems, recv_sems, local_sem
        self.me = _position()

    semaphores = _PackGather.semaphores

    def _peer(self, r):
        x, y, c = self.me
        return ((1 - x) if r & 4 else x, (1 - y) if r & 2 else y, (1 - c) if r & 1 else c)

    def _copy(self, r, slot_of):
        px, py, pc = slot_of
        return pltpu.make_async_remote_copy(src_ref=self.p_ref, dst_ref=self.all_ref.at[4 * px + 2 * py + pc],
                                            send_sem=self.send_sems.at[r - 1], recv_sem=self.recv_sems.at[r - 1],
                                            device_id=self._peer(r), device_id_type=MESH)

    def _mine(self):
        x, y, c = self.me
        return pltpu.make_async_copy(self.p_ref, self.all_ref.at[4 * x + 2 * y + c], self.local_sem)

    def start(self):
        self._mine().start()
        for r in range(1, N_DEVICES):
            self._copy(r, self.me).start()

    def finish(self):
        for r in range(1, N_DEVICES):
            self._copy(r, self._peer(r)).wait()
        self._mine().wait()


def _adamw(w, g, m, v):
    m = ADAM_B1 * m + (1.0 - ADAM_B1) * g
    v = ADAM_B2 * v + (1.0 - ADAM_B2) * (g * g)
    m_hat = m / ADAM_BC1
    v_hat = v / ADAM_BC2
    delta = -ADAM_LR * (m_hat / (jnp.sqrt(v_hat) + ADAM_EPS) + ADAM_WD * w)
    return delta, m, v


JOIN_SUB = 4


def _join(tag, shard_shape, part, arrived, core_chip, block=None):
    pr, pc = WGRAD_GEOMETRY[tag][:2]
    rb = pr // JOIN_SUB
    by_rows = shard_shape[1] == pc
    riding = block is not None

    def body(cc_ref, p_ref, r1_ref, r2_ref, r3_ref, *rest):
        if riding:
            blk_ref, g_ref, all_ref, stage, send_sems, recv_sems, local_sems, b_send, b_recv, b_local = rest
            gather = _DirectGather(blk_ref, all_ref, b_send, b_recv, b_local)
        else:
            g_ref, stage, send_sems, recv_sems, local_sems = rest
        i = pl.program_id(0)
        c = cc_ref[0]
        if riding:
            @pl.when(i == 0)
            def _():
                gather.start()

        def window(core, k):
            if by_rows:
                return g_ref.at[pl.ds((core * JOIN_SUB + k) * rb, rb), :]
            return g_ref.at[pl.ds(k * rb, rb), pl.ds(core * pc, pc)]

        def keep(k):
            return pltpu.make_async_copy(stage.at[k], window(c, k), local_sems.at[k])

        def push(k):
            return pltpu.make_async_remote_copy(src_ref=stage.at[k], dst_ref=window(c, k), send_sem=send_sems.at[k],
                                                recv_sem=recv_sems.at[k], device_id=_sibling(), device_id_type=MESH)

        def pushed(k):
            return pltpu.make_async_remote_copy(src_ref=stage.at[k], dst_ref=window(1 - c, k), send_sem=send_sems.at[k],
                                                recv_sem=recv_sems.at[k], device_id=_sibling(), device_id_type=MESH)

        stage[i] = ((p_ref[0] + r1_ref[0].astype(F32)) + r2_ref[0].astype(F32)) + r3_ref[0].astype(F32)
        keep(i).start()
        push(i).start()

        @pl.when(i == JOIN_SUB - 1)
        def _():
            for k in range(JOIN_SUB):
                keep(k).wait()
                push(k).wait_send()
                pushed(k).wait_recv()
            if riding:
                gather.finish()

    def partial(off):
        return pl.BlockSpec((1, rb, pc), lambda i, cc: ((cc[1] + off) % N_CHIPS, i, 0))

    in_specs = [partial(0), partial(1), partial(2), partial(3)]
    out_specs = [ANY]
    out_shape = [jax.ShapeDtypeStruct(shard_shape, F32)]
    scratch = [pltpu.VMEM((JOIN_SUB, rb, pc), F32), pltpu.SemaphoreType.DMA((JOIN_SUB,)),
               pltpu.SemaphoreType.DMA((JOIN_SUB,)), pltpu.SemaphoreType.DMA((JOIN_SUB,))]
    operands = [part, arrived, arrived, arrived]
    if riding:
        in_specs.append(pl.BlockSpec(block.shape, lambda i, cc: (0, 0)))
        out_specs.append(ANY)
        out_shape.append(jax.ShapeDtypeStruct((N_DEVICES,) + block.shape, block.dtype))
        scratch += _DirectGather.semaphores()
        operands.append(block)
    outs = pl.pallas_call(
        body, name="join_" + tag,
        grid_spec=pltpu.PrefetchScalarGridSpec(
            num_scalar_prefetch=1, grid=(JOIN_SUB,), in_specs=in_specs, out_specs=out_specs, scratch_shapes=scratch),
        out_shape=out_shape,
        compiler_params=_params(dimension_semantics=("arbitrary",)),
    )(core_chip, *operands)
    return outs if riding else outs[0]


def _adamw_big(w, g, m, v, name):
    rows, cols = w.shape
    rb = ADAMW_ROWS if rows % ADAMW_ROWS == 0 else rows

    def body(w_ref, g_ref, m_ref, v_ref, go_ref, d_ref, nm_ref, nv_ref):
        g = g_ref[...]
        go_ref[...] = g
        d_ref[...], nm_ref[...], nv_ref[...] = _adamw(w_ref[...], g, m_ref[...], v_ref[...])

    spec = pl.BlockSpec((rb, cols), lambda i: (i, 0))
    return pl.pallas_call(
        body, name=name, grid=(rows // rb,), in_specs=[spec] * 4, out_specs=[spec] * 4,
        out_shape=[jax.ShapeDtypeStruct(w.shape, F32)] * 4,
        compiler_params=_params(dimension_semantics=("arbitrary",)),
    )(w, g, m, v)


SMALL_VECTORS = {
    "norm_mix_g": (PK_MIX_G, D_MODEL), "rnn_conv_b": (PK_RCONV_B, LRU_W), "b_a": (PK_B_A, LRU_W), "b_x": (PK_B_X, LRU_W),
    "lru_lambda": (PK_LAMBDA, LRU_W), "g_norm_conv": (PK_G_NORM_CONV, CONV_W), "g_norm_rnn": (PK_G_NORM_RNN, LRU_W),
    "norm_mlp_g": (PK_MLP_G, D_MODEL), "final_norm_g": (PK_FINAL_G, D_MODEL),
}
SMALL_MATRICES = ("w_a", "w_x")


def _small_step(vec_packs, mat_packs, mix_g_blocks, p):
    vec_rows, cols = vec_packs.shape[1:]
    conv_rows, cshard = p["conv_w"].shape
    rconv_rows, rshard = p["rnn_conv_w"].shape
    names = list(SMALL_VECTORS) + list(SMALL_MATRICES) + ["conv_w", "rnn_conv_w"]
    shapes = ([(1, width) for _, width in SMALL_VECTORS.values()] + [mat_packs.shape[2:]] * len(SMALL_MATRICES)
              + [(conv_rows, cshard), (rconv_rows, rshard)])
    kinds = ("", "m_", "v_")
    params = [p[pre + n].reshape(1, -1) if n in SMALL_VECTORS else p[pre + n] for pre in kinds for n in names]

    def body(vec_ref, mat_ref, blk_ref, *rest):
        wmv = [dict(zip(names, rest[k * len(names):(k + 1) * len(names)])) for k in range(3)]
        loss_ref, rest = rest[3 * len(names)], rest[3 * len(names) + 1:]
        leaves, (g_ref, w_ref, m_ref, v_ref) = [rest[k * len(names):(k + 1) * len(names)] for k in range(4)], rest[4 * len(names):]
        total = vec_ref[0]
        mats = mat_ref[0].astype(F32)
        late = blk_ref[0]
        for k in range(1, N_DEVICES):
            total = total + vec_ref[k]
            mats = mats + mat_ref[k].astype(F32)
            late = late + blk_ref[k]
        g_ref[0:vec_rows, :] = total
        g_ref[vec_rows:, :] = late
        g = g_ref[...]
        loss_ref[...] = g[PK_LOSS:PK_LOSS + 1, 0:1]

        for pack_ref, given in zip((w_ref, m_ref, v_ref), wmv):
            pack_ref[...] = jnp.zeros_like(pack_ref)
            for name, (row, width) in SMALL_VECTORS.items():
                pack_ref[row:row + 1, 0:width] = given[name][...]

        x, y, _ = _position()
        j = 2 * x + y
        cblk = total[0:TILE_ROWS, :]
        rblk = total[PK_RCONV_W:PK_RCONV_W + TILE_ROWS, :]
        cg = cblk[:, 0:cshard]
        rg = rblk[:, 0:rshard]
        for k in range(1, N_CHIPS):
            cg = jnp.where(j == k, cblk[:, k * cshard:(k + 1) * cshard], cg)
            rg = jnp.where(j == k, rblk[:, k * rshard:(k + 1) * rshard], rg)
        cg = cg[PK_CONV_W:PK_CONV_W + conv_rows, :]
        rg = rg[0:rconv_rows, :]

        def step(name, grad):
            return (grad,) + _adamw(wmv[0][name][...], grad, wmv[1][name][...], wmv[2][name][...])

        packs = (g,) + _adamw(w_ref[...], g, m_ref[...], v_ref[...])
        matrices = [step(name, mats[n]) for n, name in enumerate(SMALL_MATRICES)]
        convs, rconvs = step("conv_w", cg), step("rnn_conv_w", rg)
        for kind in range(4):
            out = dict(zip(names, leaves[kind]))
            for name, (row, width) in SMALL_VECTORS.items():
                out[name][...] = packs[kind][row:row + 1, 0:width]
            for n, name in enumerate(SMALL_MATRICES):
                out[name][...] = matrices[n][kind]
            out["conv_w"][...] = convs[kind]
            out["rnn_conv_w"][...] = rconvs[kind]

    outs = pl.pallas_call(
        body, name="small_grads_step", in_specs=[VMEM] * (3 + len(params)), out_specs=[VMEM] * (1 + 4 * len(names)),
        out_shape=[jax.ShapeDtypeStruct((1, 1), F32)] + [jax.ShapeDtypeStruct(sh, F32) for sh in shapes] * 4,
        scratch_shapes=[pltpu.VMEM((PK_ROWS, cols), F32)] * 4,
        compiler_params=_params(),
    )(vec_packs, mat_packs, mix_g_blocks, *params)
    return outs[0], [dict(zip(names, outs[1 + k * len(names):1 + (k + 1) * len(names)])) for k in range(4)]


_NAMES = ['norm_mix_g', 'w_in', 'conv_w', 'rnn_conv_w', 'rnn_conv_b', 'w_a', 'b_a', 'w_x', 'b_x', 'lru_lambda',
          'g_norm_conv', 'g_norm_rnn', 'w_out', 'norm_mlp_g', 'w_mlp_in', 'w_mlp_out', 'final_norm_g']


def kernel(x, norm_mix_g, w_in, conv_w, rnn_conv_w, rnn_conv_b, w_a, b_a, w_x, b_x, lru_lambda, g_norm_conv, g_norm_rnn, w_out, norm_mlp_g, w_mlp_in, w_mlp_out, final_norm_g, loss_target, m_norm_mix_g, m_w_in, m_conv_w, m_rnn_conv_w, m_rnn_conv_b, m_w_a, m_b_a, m_w_x, m_b_x, m_lru_lambda, m_g_norm_conv, m_g_norm_rnn, m_w_out, m_norm_mlp_g, m_w_mlp_in, m_w_mlp_out, m_final_norm_g, v_norm_mix_g, v_w_in, v_conv_w, v_rnn_conv_w, v_rnn_conv_b, v_w_a, v_b_a, v_w_x, v_b_x, v_lru_lambda, v_g_norm_conv, v_g_norm_rnn, v_w_out, v_norm_mlp_g, v_w_mlp_in, v_w_mlp_out, v_final_norm_g):
    args = dict(locals())
    p = {}
    for n in _NAMES:
        for pre in ("", "m_", "v_"):
            a = args[pre + n]
            p[pre + n] = a[0] if a.ndim >= 3 else a
    xs = x[0]
    target = loss_target[0]
    core_chip = jnp.stack([lax.axis_index("c"), 2 * lax.axis_index("x") + lax.axis_index("y")]).astype(jnp.int32)

    w_in_g, w_out_g, w1_g, w2_g, conv_full, rconv_full, wa_bd, wx_bd, h1b = _gather_first(
        p["w_in"], p["w_out"], p["w_mlp_in"], p["w_mlp_out"], p["conv_w"], p["rnn_conv_w"], p["w_a"], p["w_x"],
        xs, p["norm_mix_g"])
    gf = p["final_norm_g"].reshape(1, -1)
    lru = (wa_bd, p["b_a"], wx_bd, p["b_x"], p["lru_lambda"], p["g_norm_conv"], p["g_norm_rnn"])

    (u, xr, hs, c3, yb, gates), (w_out_g, w1_g, w2_g) = _fwd_mix(
        h1b, w_in_g, conv_full, rconv_full, p["rnn_conv_b"], *lru, (w_out_g, w1_g, w2_g))
    zb, dpb, h2b, dx3b, dx2, dx2b, dy, st_mlp = _mlp_fwd_bwd(
        xs, yb, w_out_g.reshape(-1, D_MODEL), w1_g, w2_g.reshape(-1, D_MODEL), p["norm_mlp_g"], gf, target)

    part_out = _wgrad(yb, dx2b, "out", core_chip)
    *part_1, arrived_out = _wgrad(h2b, dpb, "mlp_in", core_chip, parts=(part_out[1],))
    *part_2, arrived_1 = _wgrad(zb, dx3b, "mlp_out", core_chip, parts=(part_1[1],))
    (dub, vec_pack, mat_pack), (arrived_2,) = _mix_bwd(
        dy, u, xr, hs, c3, gates, conv_full, rconv_full, wa_bd, wx_bd, p["lru_lambda"], p["g_norm_conv"], p["g_norm_rnn"],
        st_mlp, (part_2[1],))
    arrived_mlp = (arrived_out, arrived_1, arrived_2)
    *part_in, vec_packs, mat_packs = _wgrad(h1b, dub, "in", core_chip, packs=(vec_pack, mat_pack))
    early = (("w_out", "out", part_out, arrived_mlp[0]), ("w_mlp_in", "mlp_in", part_1, arrived_mlp[1]),
             ("w_mlp_out", "mlp_out", part_2, arrived_mlp[2]))
    (grad_x, st_in), arrived_in, joined = _in_bwd(
        dub, w_in_g, xs, dx2, p["norm_mix_g"], (part_in[1],),
        [(tag, p[n].shape, part[0], arrived) for n, tag, part, arrived in early], core_chip)
    g_in, mix_g_blocks = _join("in", p["w_in"].shape, part_in[0], arrived_in[0], core_chip, st_in)
    big = {}
    for n, tag, g in [(n, tag, g) for (n, tag, _, _), g in zip(early, joined)] + [("w_in", "in", g_in)]:
        big[n] = _adamw_big(p[n], g, p["m_" + n], p["v_" + n], "adamw_" + tag)

    loss, outs = _small_step(vec_packs, mat_packs, mix_g_blocks, p)
    for kind, o in enumerate(outs):
        o["final_norm_g"] = o["final_norm_g"].reshape(-1)
        for n in SMALL_MATRICES + ("conv_w", "rnn_conv_w"):
            o[n] = o[n][None]
        for n in ("w_in", "w_out", "w_mlp_in", "w_mlp_out"):
            o[n] = big[n][kind][None]
    loss = loss.reshape(())
    return (loss, grad_x[None], *[o[n] for o in outs for n in _NAMES])
```

```python
import functools
import math

import jax
import jax.numpy as jnp
from jax import lax
from jax.experimental import pallas as pl
from jax.experimental.pallas import tpu as pltpu

F32 = jnp.float32
BF16 = jnp.bfloat16
MESH = pl.DeviceIdType.MESH
ANY = pl.BlockSpec(memory_space=pl.ANY)
VMEM = pl.BlockSpec(memory_space=pltpu.VMEM)

EPS = 1e-6
LRU_C = 8.0
D_MODEL = 1024
CONV_W = 512
LRU_W = 1024
IN_COLS = 3 * CONV_W + 2 * LRU_W
IN_SHARD = IN_COLS // 4
N_CHIPS = 4
N_DEVICES = 8
BD = 256
N_BD = LRU_W // BD

ADAM_LR = 0.001
ADAM_B1 = 0.9
ADAM_B2 = 0.999
ADAM_EPS = 1e-08
ADAM_WD = 0.01
ADAM_STEP = 10
ADAM_BC1 = 1.0 - ADAM_B1 ** ADAM_STEP
ADAM_BC2 = 1.0 - ADAM_B2 ** ADAM_STEP

TILE_ROWS, LANES = 8, 128
TOKEN_TILE = 256
MATMUL_TOKEN_TILE = 512
ADAMW_ROWS = 256
VMEM_LIMIT = 56 * 1024 * 1024

PK_G_NORM_RNN, PK_RCONV_B, PK_B_A, PK_B_X, PK_LAMBDA, PK_CONV_W = 0, 1, 2, 3, 4, 5
PK_RCONV_W, PK_G_NORM_CONV = 8, 12
PK_MIX_ROWS = 16
PK_FINAL_G, PK_MLP_G, PK_LOSS = 16, 17, 18
PK_MLP_ROWS = 8
PK_MIX_G = 24
PK_ROWS = 32
N_HEADS, HEAD_DIM = 16, 64


def _params(**kw):
    return pltpu.CompilerParams(vmem_limit_bytes=VMEM_LIMIT, **kw)


def _position():
    x, y, c = lax.axis_index("x"), lax.axis_index("y"), lax.axis_index("c")
    return x, y, c


def _sigmoid(v):
    return 1.0 / (1.0 + jnp.exp(-v))


def _one_minus_square(log_a, a):
    v = 2.0 * log_a
    series = -v * (1.0 + v * (0.5 + v * (1.0 / 6.0)))
    return jnp.where(v > -0.01, series, 1.0 - a * a)


_GELU_C = math.sqrt(2.0 / math.pi)
_GELU_K = 0.044715


def _gelu_and_grad(g):
    th = jnp.tanh(_GELU_C * (g + _GELU_K * g * g * g))
    gelu = 0.5 * g * (1.0 + th)
    dgelu = 0.5 * (1.0 + th) + 0.5 * g * (1.0 - th * th) * (_GELU_C * (1.0 + 3.0 * _GELU_K * g * g))
    return gelu, dgelu


def _rows(shape):
    return lax.broadcasted_iota(jnp.int32, shape, 0)


def _shift_down(v, k, prev8):
    rolled = pltpu.roll(v, k, 0)
    halo = pltpu.roll(prev8, k, 0)
    head = jnp.where(_rows(halo.shape) < k, halo, rolled[:TILE_ROWS])
    return jnp.concatenate([head, rolled[TILE_ROWS:]], axis=0)


def _shift_up(v, k, next8):
    n = v.shape[0]
    rolled = pltpu.roll(v, n - k, 0)
    halo = pltpu.roll(next8, TILE_ROWS - k, 0)
    tail = jnp.where(_rows(halo.shape) >= TILE_ROWS - k, halo, rolled[n - TILE_ROWS:])
    return jnp.concatenate([rolled[: n - TILE_ROWS], tail], axis=0)


def _scan_rows(a, b, carry, reverse=False):
    n, w = a.shape
    groups = n // TILE_ROWS
    a3 = a.reshape(groups, TILE_ROWS, w)
    b3 = b.reshape(groups, TILE_ROWS, w)
    sub = lax.broadcasted_iota(jnp.int32, a3.shape, 1)
    s = 1
    while s < TILE_ROWS:
        shift = TILE_ROWS - s if reverse else s
        keep = (sub < TILE_ROWS - s) if reverse else (sub >= s)
        b3 = b3 + jnp.where(keep, a3 * pltpu.roll(b3, shift, 1), 0.0)
        a3 = a3 * jnp.where(keep, pltpu.roll(a3, shift, 1), 1.0)
        s *= 2
    out = [None] * groups
    edge = 0 if reverse else TILE_ROWS - 1
    for g in (range(groups - 1, -1, -1) if reverse else range(groups)):
        out[g] = b3[g] + a3[g] * carry
        carry = out[g][edge:edge + 1]
    return jnp.concatenate(out, axis=0)


def _softplus_neg(lam):
    e = jnp.exp(-jnp.abs(lam))
    log1p_e = jnp.where(e < 1e-2, e * (1.0 - e * (0.5 - e * (1.0 / 3.0 - e * 0.25))), jnp.log(1.0 + e))
    sp = jnp.maximum(-lam, 0.0) + log1p_e
    dsp = -_sigmoid(-lam)
    return sp, dsp


def _block_diag_dot(vb, w_ref):
    return jnp.concatenate(
        [jnp.dot(vb[:, j * BD:(j + 1) * BD], w_ref[j], preferred_element_type=F32) for j in range(N_BD)], axis=1)


def _block_diag_dot_t(vb, w_ref):
    return jnp.concatenate(
        [lax.dot_general(vb[:, j * BD:(j + 1) * BD], w_ref[j], (((1,), (1,)), ((), ())), preferred_element_type=F32)
         for j in range(N_BD)], axis=1)


def _dot_nt(a, b):
    return lax.dot_general(a, b, (((1,), (1,)), ((), ())), preferred_element_type=F32)


def _dot_tn(a, b):
    return lax.dot_general(a, b, (((0,), (0,)), ((), ())), preferred_element_type=F32)


def _lru_gates(xr, wa_ref, ba, wx_ref, bx, sp):
    xrb = xr.astype(BF16)
    r = _sigmoid(_block_diag_dot(xrb, wa_ref) + ba)
    ig = _sigmoid(_block_diag_dot(xrb, wx_ref) + bx)
    log_a = (-LRU_C) * r * sp
    a = jnp.exp(log_a)
    mult = jnp.sqrt(_one_minus_square(log_a, a))
    return r, ig, a, mult


def _colsum(v):
    return jnp.sum(v, axis=0, keepdims=True)


N_FWD_OUT = 6


def _fwd_mix(h1b, w_in_g, conv_w, rconv_w, rconv_b, wa_bd, b_a, wx_bd, b_x, lam, g_nc, g_nr, later):
    t, d = h1b.shape
    tm = TOKEN_TILE
    nt = t // tm
    nl = len(later)
    assert nl == 3
    pass_on_at = [nt * f // 16 for f in (3, 5, 9)]
    neighbours_at = [nt * f // 16 for f in (10, 11, 12)]
    diagonal_at = [nt * f // 16 for f in (13, 14, 14)]

    def body(h1_ref, win_ref, cw_ref, rw_ref, rb_ref, wa_ref, ba_ref, wx_ref, bx_ref, lam_ref, gnc_ref, gnr_ref, *rest):
        later_in, outs, rest = rest[:nl], rest[nl:nl + N_FWD_OUT], rest[nl + N_FWD_OUT:]
        u_ref, xr_ref, hs_ref, c3_ref, y_ref, gates_ref = outs
        later_out, (cv_prev, xin_prev, h_prev, send_sems, recv_sems) = rest[:nl], rest[nl:]
        del later_in
        step = pl.program_id(0)
        plan = _ShardGather(later_out, send_sems, recv_sems)

        @pl.when(step == 0)
        def _():
            cv_prev[...] = jnp.zeros_like(cv_prev)
            xin_prev[...] = jnp.zeros_like(xin_prev)
            h_prev[...] = jnp.zeros_like(h_prev)
            for w in range(nl):
                plan.start_direct(w)

        for w in range(nl):
            @pl.when(step == pass_on_at[w])
            def _(w=w):
                plan.start_pass_on(w)

            @pl.when(step == neighbours_at[w])
            def _(w=w):
                plan.start_hand_over(w, diagonal=False)

            @pl.when(step == diagonal_at[w])
            def _(w=w):
                plan.start_hand_over(w, diagonal=True)

        h1b = h1_ref[...]
        for j in range(N_CHIPS):
            u_ref[:, j * IN_SHARD:(j + 1) * IN_SHARD] = jnp.dot(h1b, win_ref[j], preferred_element_type=F32)
        gate_b = u_ref[:, 0:CONV_W]
        cv = u_ref[:, CONV_W:2 * CONV_W] * u_ref[:, 2 * CONV_W:3 * CONV_W]
        x_r = u_ref[:, 3 * CONV_W:3 * CONV_W + LRU_W]
        g = u_ref[:, 3 * CONV_W + LRU_W:]

        cw = cw_ref[...]
        cvp = cv_prev[...]
        conv3 = cw[0:1] * _shift_down(cv, 2, cvp) + cw[1:2] * _shift_down(cv, 1, cvp) + cw[2:3] * cv
        cv_prev[...] = cv[tm - TILE_ROWS:]
        c3_ref[...] = conv3
        y_conv = gate_b * conv3

        rw = rw_ref[...]
        xp = xin_prev[...]
        xr = (rw[0:1] * _shift_down(x_r, 3, xp) + rw[1:2] * _shift_down(x_r, 2, xp)
              + rw[2:3] * _shift_down(x_r, 1, xp) + rw[3:4] * x_r) + rb_ref[...]
        xin_prev[...] = x_r[tm - TILE_ROWS:]
        xr_ref[...] = xr
        sp, _ = _softplus_neg(lam_ref[...])
        r, ig, a, mult = _lru_gates(xr, wa_ref, ba_ref[...], wx_ref, bx_ref[...], sp)
        for n, gate in enumerate((r, ig, a, mult)):
            gates_ref[:, n * LRU_W:(n + 1) * LRU_W] = gate
        h = _scan_rows(a, mult * (ig * xr), h_prev[...])
        h_prev[...] = h[tm - 1:tm]
        hs_ref[...] = h
        gelu, _ = _gelu_and_grad(g)
        y_rnn = h * gelu

        na = y_conv * lax.rsqrt(jnp.mean(y_conv * y_conv, axis=-1, keepdims=True) + EPS) * gnc_ref[...]
        nb = y_rnn * lax.rsqrt(jnp.mean(y_rnn * y_rnn, axis=-1, keepdims=True) + EPS) * gnr_ref[...]
        y_ref[:, :CONV_W] = na.astype(BF16)
        y_ref[:, CONV_W:] = nb.astype(BF16)

        @pl.when(step == nt - 1)
        def _():
            for w in range(nl):
                plan.finish(w)

    def full(a):
        nd = a.ndim
        return pl.BlockSpec(a.shape, lambda i: (0,) * nd)

    def tok(cols):
        return pl.BlockSpec((tm, cols), lambda i: (i, 0))

    def act(cols, dtype=F32):
        return jax.ShapeDtypeStruct((t, cols), dtype)

    smalls = (w_in_g, conv_w, rconv_w, rconv_b, wa_bd, b_a, wx_bd, b_x, lam, g_nc, g_nr)
    n_in = 1 + len(smalls)
    outs = pl.pallas_call(
        body, name="fwd_mix", grid=(nt,),
        in_specs=[tok(d)] + [full(a) for a in smalls] + [ANY] * nl,
        out_specs=[tok(IN_COLS), tok(LRU_W), tok(LRU_W), tok(CONV_W), tok(CONV_W + LRU_W)]
        + [tok(4 * LRU_W)] + [ANY] * nl,
        out_shape=[act(IN_COLS), act(LRU_W), act(LRU_W), act(CONV_W), act(CONV_W + LRU_W, BF16)]
        + [act(4 * LRU_W)] + [jax.ShapeDtypeStruct(a.shape, a.dtype) for a in later],
        input_output_aliases={n_in + w: N_FWD_OUT + w for w in range(nl)},
        scratch_shapes=[pltpu.VMEM((TILE_ROWS, CONV_W), F32), pltpu.VMEM((TILE_ROWS, LRU_W), F32),
                        pltpu.VMEM((1, LRU_W), F32), pltpu.SemaphoreType.DMA((nl, _ShardGather.PAIRS)),
                        pltpu.SemaphoreType.DMA((nl, _ShardGather.PAIRS))],
        compiler_params=_params(dimension_semantics=("arbitrary",)),
    )(h1b, *smalls, *later)
    return outs[:N_FWD_OUT], outs[N_FWD_OUT:]


def _mlp_fwd_bwd(x, yb, w_out_g, w1_g, w2_g, g2, gf, target):
    t, d = x.shape
    tm = TOKEN_TILE
    ff = w2_g.shape[0]
    mix = w_out_g.shape[0]
    ffs = ff // N_CHIPS

    def body(x_ref, y_ref, g2_ref, gf_ref, tgt_ref, wout_hbm, w1_hbm, w2_hbm,
             z_ref, dp_ref, h2_ref, dx3b_ref, dx2_ref, dx2b_ref, dy_ref, st_ref, wout, w1, w2, p_ref):
        @pl.when(pl.program_id(0) == 0)
        def _():
            pltpu.sync_copy(wout_hbm, wout)
            pltpu.sync_copy(w1_hbm, w1)
            pltpu.sync_copy(w2_hbm, w2)
            st_ref[...] = jnp.zeros_like(st_ref)

        x2 = x_ref[...] + jnp.dot(y_ref[...], wout[...], preferred_element_type=F32)
        r2 = lax.rsqrt(jnp.mean(x2 * x2, axis=-1, keepdims=True) + EPS)
        xh2 = x2 * r2
        g2v = g2_ref[...]
        h2b = (xh2 * g2v).astype(BF16)
        h2_ref[...] = h2b
        for j in range(N_CHIPS):
            p_ref[:, j * ffs:(j + 1) * ffs] = jnp.dot(h2b, w1[j], preferred_element_type=F32)
        rp = jnp.maximum(p_ref[...], 0.0)
        zb = (rp * rp).astype(BF16)
        z_ref[...] = zb
        x3 = x2 + jnp.dot(zb, w2[...], preferred_element_type=F32)
        r3 = lax.rsqrt(jnp.mean(x3 * x3, axis=-1, keepdims=True) + EPS)
        xh3 = x3 * r3
        gfv = gf_ref[...]
        err = xh3 * gfv - tgt_ref[...]
        loss = (0.5 / d) * jnp.sum(err * err)
        dout = err * (1.0 / d)
        st_ref[PK_FINAL_G - PK_MIX_ROWS:PK_FINAL_G - PK_MIX_ROWS + 1, :] += _colsum(dout * xh3)
        st_ref[PK_LOSS - PK_MIX_ROWS:PK_LOSS - PK_MIX_ROWS + 1, :] += jnp.zeros((1, d), F32) + loss
        dxh3 = dout * gfv
        dx3 = r3 * (dxh3 - xh3 * jnp.mean(dxh3 * xh3, axis=-1, keepdims=True))
        dx3b = dx3.astype(BF16)
        dx3b_ref[...] = dx3b
        dpb = (_dot_nt(dx3b, w2[...]) * (2.0 * rp)).astype(BF16)
        dp_ref[...] = dpb
        dh2 = _dot_nt(dpb[:, 0:ffs], w1[0])
        for j in range(1, N_CHIPS):
            dh2 = dh2 + _dot_nt(dpb[:, j * ffs:(j + 1) * ffs], w1[j])
        st_ref[PK_MLP_G - PK_MIX_ROWS:PK_MLP_G - PK_MIX_ROWS + 1, :] += _colsum(dh2 * xh2)
        dxh2 = dh2 * g2v
        dx2 = dx3 + r2 * (dxh2 - xh2 * jnp.mean(dxh2 * xh2, axis=-1, keepdims=True))
        dx2_ref[...] = dx2
        dx2b = dx2.astype(BF16)
        dx2b_ref[...] = dx2b
        dy_ref[...] = _dot_nt(dx2b, wout[...])

    def tok(cols):
        return pl.BlockSpec((tm, cols), lambda i: (i, 0))

    def row(cols):
        return pl.BlockSpec((1, cols), lambda i: (0, 0))

    return pl.pallas_call(
        body, name="mlp_fwd_bwd", grid=(t // tm,),
        in_specs=[tok(d), tok(mix), row(d), row(d), tok(d), ANY, ANY, ANY],
        out_specs=[tok(ff), tok(ff), tok(d), tok(d), tok(d), tok(d), tok(mix),
                   pl.BlockSpec((PK_MLP_ROWS, d), lambda i: (0, 0))],
        out_shape=[jax.ShapeDtypeStruct((t, ff), BF16), jax.ShapeDtypeStruct((t, ff), BF16),
                   jax.ShapeDtypeStruct((t, d), BF16), jax.ShapeDtypeStruct((t, d), BF16),
                   jax.ShapeDtypeStruct((t, d), F32), jax.ShapeDtypeStruct((t, d), BF16),
                   jax.ShapeDtypeStruct((t, mix), F32), jax.ShapeDtypeStruct((PK_MLP_ROWS, d), F32)],
        scratch_shapes=[pltpu.VMEM(w_out_g.shape, BF16), pltpu.VMEM(w1_g.shape, BF16), pltpu.VMEM(w2_g.shape, BF16),
                        pltpu.VMEM((tm, ff), F32)],
        compiler_params=_params(dimension_semantics=("arbitrary",)),
    )(x, yb, g2, gf, target, w_out_g, w1_g, w2_g)


def _mix_bwd(dy, u, xr_all, hs_all, c3_all, gates, conv_w, rconv_w, wa_bd, wx_bd, lam, g_nc, g_nr, st_mlp, parts,
             routes=None, begun=()):
    t = dy.shape[0]
    tm = TOKEN_TILE
    nt = t // tm
    hb = tm // TILE_ROWS
    npart = len(parts)
    nbegun = len(begun)

    def body(dy_ref, u_ref, uh_ref, xr_ref, hs_ref, hh_ref, c3_ref, gates_ref,
             cw_ref, rw_ref, wa_ref, wx_ref, lam_ref, gnc_ref, gnr_ref, stm_ref, *rest):
        part_refs, rest = rest[:npart], rest[npart + nbegun:]
        (du_ref, st_ref, heads_ref), rest = rest[:3], rest[3:]
        arrived_refs, (dc_next, a_next, gs_next, dxr_next, dwa_ref, dwx_ref, send_sems, recv_sems) = rest[:npart], rest[npart:]
        exchange = _PartialExchange(part_refs, arrived_refs, send_sems, recv_sems, routes)
        i = pl.program_id(0)

        @pl.when(i == 0)
        def _():
            exchange.start()
            dc_next[...] = jnp.zeros_like(dc_next)
            a_next[...] = jnp.zeros_like(a_next)
            gs_next[...] = jnp.zeros_like(gs_next)
            dxr_next[...] = jnp.zeros_like(dxr_next)
            st_ref[0:PK_MIX_ROWS, :] = jnp.zeros((PK_MIX_ROWS, LRU_W), F32)
            st_ref[PK_MIX_ROWS:, :] = stm_ref[...]
            dwa_ref[...] = jnp.zeros_like(dwa_ref)
            dwx_ref[...] = jnp.zeros_like(dwx_ref)

        first_tile = i == nt - 1
        gate_b = u_ref[:, 0:CONV_W]
        gate_c = u_ref[:, CONV_W:2 * CONV_W]
        v = u_ref[:, 2 * CONV_W:3 * CONV_W]
        x_r = u_ref[:, 3 * CONV_W:3 * CONV_W + LRU_W]
        g = u_ref[:, 3 * CONV_W + LRU_W:]
        cv = gate_c * v
        cv_prev = jnp.where(first_tile, 0.0, uh_ref[:, CONV_W:2 * CONV_W] * uh_ref[:, 2 * CONV_W:3 * CONV_W])
        xin_prev = jnp.where(first_tile, 0.0, uh_ref[:, 3 * CONV_W:3 * CONV_W + LRU_W])
        hs_prev = jnp.where(first_tile, 0.0, hh_ref[...])

        def acc(first_row, val, width=LRU_W, row=0):
            r0 = first_row + row
            st_ref[r0:r0 + 1, 0:width] += val

        conv3 = c3_ref[...]
        y_conv = gate_b * conv3
        ra = lax.rsqrt(jnp.mean(y_conv * y_conv, axis=-1, keepdims=True) + EPS)
        xha = y_conv * ra
        dna = dy_ref[:, :CONV_W]
        acc(PK_G_NORM_CONV, _colsum(dna * xha), CONV_W)
        dxha = dna * gnc_ref[...]
        dy_conv = ra * (dxha - xha * jnp.mean(dxha * xha, axis=-1, keepdims=True))
        du_ref[:, 0:CONV_W] = (dy_conv * conv3).astype(BF16)
        dc = dy_conv * gate_b
        cw = cw_ref[...]
        dcn = dc_next[...]
        dcv = cw[2:3] * dc + cw[1:2] * _shift_up(dc, 1, dcn) + cw[0:1] * _shift_up(dc, 2, dcn)
        dc_next[...] = dc[:TILE_ROWS]
        acc(PK_CONV_W, _colsum(dc * _shift_down(cv, 2, cv_prev)), CONV_W, 0)
        acc(PK_CONV_W, _colsum(dc * _shift_down(cv, 1, cv_prev)), CONV_W, 1)
        acc(PK_CONV_W, _colsum(dc * cv), CONV_W, 2)
        du_ref[:, CONV_W:2 * CONV_W] = (dcv * v).astype(BF16)
        du_ref[:, 2 * CONV_W:3 * CONV_W] = (dcv * gate_c).astype(BF16)

        hs = hs_ref[...]
        gelu, dgelu = _gelu_and_grad(g)
        y_rnn = hs * gelu
        rb = lax.rsqrt(jnp.mean(y_rnn * y_rnn, axis=-1, keepdims=True) + EPS)
        xhb = y_rnn * rb
        dnb = dy_ref[:, CONV_W:]
        acc(PK_G_NORM_RNN, _colsum(dnb * xhb))
        dxhb = dnb * gnr_ref[...]
        dy_rnn = rb * (dxhb - xhb * jnp.mean(dxhb * xhb, axis=-1, keepdims=True))
        du_ref[:, 3 * CONV_W + LRU_W:] = (dy_rnn * hs * dgelu).astype(BF16)
        dh = dy_rnn * gelu

        xr = xr_ref[...]
        xrb = xr.astype(BF16)
        sp, dsp = _softplus_neg(lam_ref[...])
        r, ig, a, mult = [gates_ref[:, n * LRU_W:(n + 1) * LRU_W] for n in range(4)]
        a_up = _shift_up(a, 1, a_next[...])
        a_next[...] = a[:TILE_ROWS]
        gs = _scan_rows(a_up, dh, gs_next[0:1, :], reverse=True)
        gs_next[...] = gs[:TILE_ROWS]
        da = gs * _shift_down(hs, 1, hs_prev)
        gx = gs * xr
        di = gx * mult
        dmult = gx * ig
        dxr = gs * (mult * ig)
        dlog_a = da * a - dmult * ((a * a) / mult)
        acc(PK_LAMBDA, _colsum(dlog_a * r) * ((-LRU_C) * dsp))
        dpa = (dlog_a * ((-LRU_C) * sp)) * (r * (1.0 - r))
        dpx = di * (ig * (1.0 - ig))
        acc(PK_B_A, _colsum(dpa))
        acc(PK_B_X, _colsum(dpx))
        dpab = dpa.astype(BF16)
        dpxb = dpx.astype(BF16)
        dxr = dxr + _block_diag_dot_t(dpab, wa_ref) + _block_diag_dot_t(dpxb, wx_ref)
        for j in range(N_BD):
            cols = slice(j * BD, (j + 1) * BD)
            dwa_ref[j] += _dot_tn(xrb[:, cols], dpab[:, cols])
            dwx_ref[j] += _dot_tn(xrb[:, cols], dpxb[:, cols])

        acc(PK_RCONV_B, _colsum(dxr))
        rw = rw_ref[...]
        dxn = dxr_next[...]
        dx_r = (rw[3:4] * dxr + rw[2:3] * _shift_up(dxr, 1, dxn) + rw[1:2] * _shift_up(dxr, 2, dxn)
                + rw[0:1] * _shift_up(dxr, 3, dxn))
        dxr_next[...] = dxr[:TILE_ROWS]
        for k in range(3):
            acc(PK_RCONV_W, _colsum(dxr * _shift_down(x_r, 3 - k, xin_prev)), LRU_W, k)
        acc(PK_RCONV_W, _colsum(dxr * x_r), LRU_W, 3)
        du_ref[:, 3 * CONV_W:3 * CONV_W + LRU_W] = dx_r.astype(BF16)

        @pl.when(i == nt - 1)
        def _():
            for n, d_ref in enumerate((dwa_ref, dwx_ref)):
                for b in range(N_BD):
                    for q in range(BD // HEAD_DIM):
                        lane0 = q * HEAD_DIM // LANES * LANES
                        wide = d_ref[b, q * HEAD_DIM:(q + 1) * HEAD_DIM, lane0:lane0 + LANES]
                        if q * HEAD_DIM != lane0:
                            wide = pltpu.roll(wide, LANES - (q * HEAD_DIM - lane0), axis=1)
                        heads_ref[n, b * (BD // HEAD_DIM) + q] = wide[:, 0:HEAD_DIM].astype(BF16)
            exchange.wait()

    def full(a):
        nd = a.ndim
        return pl.BlockSpec(a.shape, lambda i: (0,) * nd)

    def tok(cols):
        return pl.BlockSpec((tm, cols), lambda i: (nt - 1 - i, 0))

    def halo(cols):
        return pl.BlockSpec((TILE_ROWS, cols), lambda i: (jnp.maximum((nt - 1 - i) * hb - 1, 0), 0))

    smalls = (conv_w, rconv_w, wa_bd, wx_bd, lam, g_nc, g_nr, st_mlp)
    st_rows = PK_MIX_ROWS + st_mlp.shape[0]
    heads = (2, N_HEADS, HEAD_DIM, HEAD_DIM)
    outs = pl.pallas_call(
        body, name="mix_bwd", grid=(nt,),
        in_specs=[tok(CONV_W + LRU_W), tok(IN_COLS), halo(IN_COLS), tok(LRU_W), tok(LRU_W), halo(LRU_W), tok(CONV_W)]
        + [tok(4 * LRU_W)] + [full(a) for a in smalls] + [ANY] * (npart + nbegun),
        out_specs=[tok(IN_COLS), pl.BlockSpec((st_rows, LRU_W), lambda i: (0, 0)),
                   pl.BlockSpec(heads, lambda i: (0, 0, 0, 0))]
        + [ANY] * npart,
        out_shape=[jax.ShapeDtypeStruct((t, IN_COLS), BF16), jax.ShapeDtypeStruct((st_rows, LRU_W), F32),
                   jax.ShapeDtypeStruct(heads, BF16)]
        + [jax.ShapeDtypeStruct(a.shape, a.dtype) for a in parts],
        scratch_shapes=[pltpu.VMEM((TILE_ROWS, CONV_W), F32), pltpu.VMEM((TILE_ROWS, LRU_W), F32),
                        pltpu.VMEM((TILE_ROWS, LRU_W), F32), pltpu.VMEM((TILE_ROWS, LRU_W), F32),
                        pltpu.VMEM((N_BD, BD, BD), F32), pltpu.VMEM((N_BD, BD, BD), F32),
                        pltpu.SemaphoreType.DMA((npart, 3)), pltpu.SemaphoreType.DMA((npart, 3))],
        compiler_params=_params(dimension_semantics=("arbitrary",)),
        input_output_aliases={8 + len(smalls) + npart + n: 3 + n for n in range(nbegun)},
    )(dy, u, u, xr_all, hs_all, hs_all, c3_all, gates, *smalls, *parts, *begun)
    return outs[:3], outs[3:]


def _in_bwd(dub, w_in_g, x, dx2, g1, parts, joins, core_chip):
    t, d = x.shape
    tm = min(t, MATMUL_TOKEN_TILE)
    nt = t // tm
    npart = len(parts)
    nj = len(joins)
    geometry = []
    for tag, shape, _, _ in joins:
        pr, pc = WGRAD_GEOMETRY[tag][:2]
        every = 1 if pr % (nt * 16) == 0 else 2
        geometry.append((pr, pc, pr * every // nt, every, shape[1] == pc))

    def body(cc_ref, du_ref, win_ref, x_ref, dx2_ref, g1_ref, *rest):
        sums, rest = [rest[4 * w:4 * w + 4] for w in range(nj)], rest[4 * nj:]
        part_refs, (gx_ref, st_ref), rest = rest[:npart], rest[npart:npart + 2], rest[npart + 2:]
        arrived_refs, joined, rest = rest[:npart], rest[npart:npart + nj], rest[npart + nj:]
        stages, (send_sems, recv_sems, j_local, j_send, j_recv) = rest[:nj], rest[nj:]
        exchange = _PartialExchange(part_refs, arrived_refs, send_sems, recv_sems)
        i = pl.program_id(0)
        c = cc_ref[0]

        def window(w, core, row0, rows):
            pr, pc, _, _, by_rows = geometry[w]
            if by_rows:
                return joined[w].at[pl.ds(core * pr + row0, rows), :]
            return joined[w].at[pl.ds(row0, rows), pl.ds(core * pc, pc)]

        def to_sibling(w, src, core, row0, rows):
            return pltpu.make_async_remote_copy(src_ref=src, dst_ref=window(w, core, row0, rows), send_sem=j_send.at[w],
                                                recv_sem=j_recv.at[w], device_id=_sibling(), device_id_type=MESH)

        @pl.when(i == 0)
        def _():
            exchange.start()
            st_ref[...] = jnp.zeros_like(st_ref)

        for w in range(nj):
            pr, pc, rb, every, _ = geometry[w]

            @pl.when(i % every == 0)
            def _(w=w, rb=rb, every=every):
                p_ref, r1_ref, r2_ref, r3_ref = sums[w]
                row0 = pl.multiple_of((i // every) * rb, rb)
                rows = stages[w].at[pl.ds(row0, rb), :]
                rows[...] = ((p_ref[0] + r1_ref[0].astype(F32)) + r2_ref[0].astype(F32)) + r3_ref[0].astype(F32)
                pltpu.make_async_copy(rows, window(w, c, row0, rb), j_local.at[w]).start()
                to_sibling(w, rows, c, row0, rb).start()

        dh1 = _dot_nt(du_ref[:, 0:IN_SHARD], win_ref[0])
        for j in range(1, N_CHIPS):
            dh1 = dh1 + _dot_nt(du_ref[:, j * IN_SHARD:(j + 1) * IN_SHARD], win_ref[j])
        xv = x_ref[...]
        rstd = lax.rsqrt(jnp.mean(xv * xv, axis=-1, keepdims=True) + EPS)
        xh = xv * rstd
        st_ref[0:1, :] += _colsum(dh1 * xh)
        dxh = dh1 * g1_ref[...]
        gx_ref[...] = dx2_ref[...] + rstd * (dxh - xh * jnp.mean(dxh * xh, axis=-1, keepdims=True))

        @pl.when(i == nt - 1)
        def _():
            exchange.wait()
            for w in range(nj):
                pr = geometry[w][0]
                pltpu.make_async_copy(stages[w], window(w, c, 0, pr), j_local.at[w]).wait()
                to_sibling(w, stages[w], 1 - c, 0, pr).wait()

    def tok(cols):
        return pl.BlockSpec((tm, cols), lambda i, cc: (i, 0))

    def partial(w, off):
        pr, pc, rb, every, _ = geometry[w]
        return pl.BlockSpec((1, rb, pc), lambda i, cc: ((cc[1] + off) % N_CHIPS, i // every, 0))

    sum_specs, sum_operands = [], []
    for w, (_, _, own, arrived) in enumerate(joins):
        sum_specs += [partial(w, off) for off in range(N_CHIPS)]
        sum_operands += [own, arrived, arrived, arrived]
    dma = pltpu.SemaphoreType.DMA
    outs = pl.pallas_call(
        body, name="in_bwd",
        grid_spec=pltpu.PrefetchScalarGridSpec(
            num_scalar_prefetch=1, grid=(nt,),
            in_specs=[tok(IN_COLS), pl.BlockSpec(w_in_g.shape, lambda i, cc: (0, 0, 0)), tok(d), tok(d),
                      pl.BlockSpec((1, d), lambda i, cc: (0, 0))] + sum_specs + [ANY] * npart,
            out_specs=[tok(d), pl.BlockSpec((TILE_ROWS, d), lambda i, cc: (0, 0))] + [ANY] * (npart + nj),
            scratch_shapes=[pltpu.VMEM((g[0], g[1]), F32) for g in geometry]
            + [dma((npart, 3)), dma((npart, 3)), dma((nj,)), dma((nj,)), dma((nj,))]),
        out_shape=[jax.ShapeDtypeStruct((t, d), F32), jax.ShapeDtypeStruct((TILE_ROWS, d), F32)]
        + [jax.ShapeDtypeStruct(a.shape, a.dtype) for a in parts]
        + [jax.ShapeDtypeStruct(shape, F32) for _, shape, _, _ in joins],
        compiler_params=_params(dimension_semantics=("arbitrary",)),
    )(core_chip, dub, w_in_g, x, dx2, g1, *sum_operands, *parts)
    return outs[:2], outs[2:2 + npart], outs[2 + npart:]


WGRAD_GEOMETRY = {
    "in": (512, IN_SHARD, lambda s, h: h, lambda s, h: s),
    "mlp_in": (512, D_MODEL, lambda s, h: h, lambda s, h: s),
    "mlp_out": (512, D_MODEL, lambda s, h: 2 * s + h, lambda s, h: 0),
    "out": (384, 512, lambda s, h: s, lambda s, h: h),
}
K_CHUNK = 512
TOKEN_STREAMS = 2


def _sibling():
    x, y, c = _position()
    return (x, y, 1 - c)


def _wgrad(a, b, tag, core_chip, packs=(), parts=(), routes=None):
    t = a.shape[0]
    pr, pc, a_blk, b_blk = WGRAD_GEOMETRY[tag]
    ns = TOKEN_STREAMS
    ts = t // ns
    kc = min(K_CHUNK, ts)
    mine = N_CHIPS
    riding = len(packs)
    npart = len(parts)
    assert not (riding and npart)

    def body(cc_ref, *rest):
        a_refs, b_refs, rest = rest[:ns], rest[ns:2 * ns], rest[2 * ns:]
        if riding:
            pack_refs, (land_ref, p_ref, pb_ref), rest = rest[:riding], rest[riding:riding + 3], rest[riding + 3:]
            all_refs, (stage, rbuf, send_sems, recv_sems, rsem), g_sems = rest[:riding], rest[riding:riding + 5], rest[riding + 5:]
            gathers = [_PackGather(pack_refs[n], all_refs[n], *g_sems[3 * n:3 * n + 3]) for n in range(riding)]
        elif npart:
            part_refs, (land_ref, p_ref, pb_ref), rest = rest[:npart], rest[npart:npart + 3], rest[npart + 3:]
            arrived_refs, (stage, rbuf, send_sems, recv_sems, rsem, x_send, x_recv) = rest[:npart], rest[npart:]
            exchange = _PartialExchange(part_refs, arrived_refs, x_send, x_recv, routes)
        else:
            land_ref, p_ref, pb_ref, stage, rbuf, send_sems, recv_sems, rsem = rest
        ph, s = pl.program_id(0), pl.program_id(1)
        if riding:
            @pl.when((ph == 0) & (s == 0))
            def _():
                for gather in gathers:
                    gather.start()

            @pl.when((ph == 1) & (s == N_CHIPS - 2))
            def _():
                for gather in gathers:
                    gather.hand_over()
        if npart:
            @pl.when((ph == 0) & (s == 0))
            def _():
                exchange.start()
        def push(k):
            return pltpu.make_async_remote_copy(src_ref=stage.at[k], dst_ref=land_ref.at[k], send_sem=send_sems.at[k],
                                                recv_sem=recv_sems.at[k], device_id=_sibling(), device_id_type=MESH)

        def landed():
            return pltpu.make_async_copy(land_ref.at[s], rbuf, rsem)

        @pl.when(ph == 1)
        def _():
            push(s).wait_recv()
            landed().start()

        slot = jnp.where(ph == 0, s, mine)
        acc = stage.at[slot]
        chunks = [(a_ref, b_ref, k) for a_ref, b_ref in zip(a_refs, b_refs) for k in range(0, ts, kc)]
        for n, (a_ref, b_ref, k) in enumerate(chunks):
            part = _dot_tn(a_ref[k:k + kc, :], b_ref[k:k + kc, :])
            if n == 0:
                acc[...] = part
            else:
                acc[...] += part

        @pl.when(ph == 0)
        def _():
            push(s).start()

        @pl.when(ph == 1)
        def _():
            landed().wait()
            p = stage[mine] + rbuf[...]
            p_ref[0] = p
            pb_ref[0] = p.astype(BF16)

        @pl.when((ph == 1) & (s == N_CHIPS - 1))
        def _():
            for k in range(N_CHIPS):
                push(k).wait_send()
            for gather in (gathers if riding else ()):
                gather.finish()
            if npart:
                exchange.wait()

    def half(ph, cc):
        return jnp.where(ph == 0, 1 - cc[0], cc[0])

    def out_slot(ph, s, cc):
        return (jnp.where(ph == 0, 0, s), 0, 0)

    piece = jax.ShapeDtypeStruct((N_CHIPS, pr, pc), F32)
    in_specs = [pl.BlockSpec((ts, pr), lambda ph, s, cc, n=n: (n, a_blk(s, half(ph, cc)))) for n in range(ns)]
    in_specs += [pl.BlockSpec((ts, pc), lambda ph, s, cc, n=n: (n, b_blk(s, half(ph, cc)))) for n in range(ns)]
    out_specs = [ANY, pl.BlockSpec((1, pr, pc), out_slot), pl.BlockSpec((1, pr, pc), out_slot)]
    out_shape = [piece, piece, jax.ShapeDtypeStruct((N_CHIPS, pr, pc), BF16)]
    scratch = [pltpu.VMEM((N_CHIPS + 1, pr, pc), F32), pltpu.VMEM((pr, pc), F32),
               pltpu.SemaphoreType.DMA((N_CHIPS,)), pltpu.SemaphoreType.DMA((N_CHIPS,)), pltpu.SemaphoreType.DMA]
    operands = [a] * ns + [b] * ns
    for pack in packs:
        in_specs.append(pl.BlockSpec(pack.shape, lambda ph, s, cc, nd=pack.ndim: (0,) * nd))
        out_specs.append(ANY)
        out_shape.append(jax.ShapeDtypeStruct((N_DEVICES,) + pack.shape, pack.dtype))
        operands.append(pack)
    for pack in packs:
        scratch += _PackGather.semaphores()
    if npart:
        in_specs += [ANY] * npart
        out_specs += [ANY] * npart
        out_shape += [jax.ShapeDtypeStruct(p.shape, p.dtype) for p in parts]
        scratch += [pltpu.SemaphoreType.DMA((npart, 3)), pltpu.SemaphoreType.DMA((npart, 3))]
        operands += list(parts)
    return pl.pallas_call(
        body, name="wgrad_" + tag,
        grid_spec=pltpu.PrefetchScalarGridSpec(
            num_scalar_prefetch=1, grid=(2, N_CHIPS), in_specs=in_specs, out_specs=out_specs, scratch_shapes=scratch),
        out_shape=out_shape,
        compiler_params=_params(dimension_semantics=("arbitrary", "arbitrary")),
    )(core_chip, *operands)[1:]


def _other_chips(x, y):
    return [(1 - x, y), (x, 1 - y), (1 - x, 1 - y)]


class _ShardGather:
    PAIRS = 9

    def __init__(self, outs, send_sems, recv_sems):
        self.outs, self.send_sems, self.recv_sems = outs, send_sems, recv_sems
        x, y, c = _position()
        self.c, self.j = c, 2 * x + y
        self.sibling = (x, y, 1 - c)
        self.chips = _other_chips(x, y)

    def _chip(self, k):
        px, py = self.chips[k]
        return 2 * px + py

    def _half(self, w, chip, which):
        hr = self.outs[w].shape[1] // 2
        return self.outs[w].at[chip, pl.ds(which * hr, hr), :]

    def _quarter(self, w, chip, q):
        qr = self.outs[w].shape[1] // 4
        return self.outs[w].at[chip, pl.ds(self.c * 2 * qr + q * qr, qr), :]

    def _copy(self, ref, w, pair, to, src=None):
        return pltpu.make_async_remote_copy(src_ref=ref if src is None else src, dst_ref=ref, send_sem=self.send_sems.at[w, pair],
                                            recv_sem=self.recv_sems.at[w, pair], device_id=to, device_id_type=MESH)

    def direct(self, w, k, q, src=None):
        return self._copy(self._quarter(w, self.j, q), w, 2 * k + q, (*self.chips[k], self.c), src)

    def direct_landed(self, w, k, q):
        return self._copy(self._quarter(w, self._chip(k), q), w, 2 * k + q, (*self.chips[k], self.c))

    def pass_on(self, w, q):
        return self._copy(self._quarter(w, self._chip(q), q), w, 4 + q, (*self.chips[1 - q], self.c))

    def passed_landed(self, w, q):
        return self._copy(self._quarter(w, self._chip(2), q), w, 4 + q, (*self.chips[1 - q], self.c))

    def hand_over(self, w, k):
        return self._copy(self._half(w, self._chip(k), self.c), w, 6 + k, self.sibling)

    def handed(self, w, k):
        return self._copy(self._half(w, self._chip(k), 1 - self.c), w, 6 + k, self.sibling)

    def start_direct(self, w, src_half=None):
        qr = self.outs[w].shape[1] // 4
        for k, q in ((0, 0), (1, 1), (0, 1), (1, 0)):
            self.direct(w, k, q, None if src_half is None else src_half.at[pl.ds(q * qr, qr), :]).start()

    def start_pass_on(self, w):
        for q in (0, 1):
            self.direct_landed(w, q, q).wait_recv()
            self.pass_on(w, q).start()

    def start_hand_over(self, w, diagonal):
        if diagonal:
            for q in (0, 1):
                self.passed_landed(w, q).wait_recv()
            self.hand_over(w, 2).start()
        else:
            for k in (0, 1):
                self.direct_landed(w, k, 1 - k).wait_recv()
                self.hand_over(w, k).start()

    def finish(self, w):
        for k in range(3):
            self.handed(w, k).wait_recv()
            self.hand_over(w, k).wait_send()
        for q in (0, 1):
            self.pass_on(w, q).wait_send()
            for k in (0, 1):
                self.direct(w, k, q).wait_send()


def _gather_first(w_in, w_out, w1, w2, conv_w, rconv_w, w_a, w_x, x, g1):
    t, d = x.shape
    tn = min(t, MATMUL_TOKEN_TILE)
    n_tiles = t // tn
    bigs = (w_in, w_out, w1, w2)
    convs = (conv_w, rconv_w)
    heads = (w_a, w_x)
    nb, nc = len(bigs), len(convs)

    def body(win_ref, wout_hbm, w1_hbm, w2_hbm, cw_ref, rw_ref, wa_ref, wx_ref, x_hbm, g1_ref, gin, gout, g1, g2, gcw, grw,
             bda, bdx, h1_hbm, st_in, st_out, st_1, st_2, f_out, f_1, f_2, st_cw, st_rw, xbuf, hbuf, send_sems, recv_sems,
             sm_send, sm_recv, local_sems, load_sems, x_sems, h_sems):
        stages = (st_in, st_out, st_1, st_2)
        outs = (gin, gout, g1, g2)
        conv_stages, conv_outs = (st_cw, st_rw), (gcw, grw)
        plan = _ShardGather(outs[:1], send_sems, recv_sems)
        j, c = plan.j, plan.c
        local = [pltpu.make_async_copy(stages[w], outs[w].at[j], local_sems.at[w]) for w in range(nb)]
        loads = [pltpu.make_async_copy(src, dst, load_sems.at[n])
                 for n, (src, dst) in enumerate(((wout_hbm, f_out), (w1_hbm, f_1), (w2_hbm, f_2)))]

        def columns(n, chip):
            width = convs[n].shape[1]
            return conv_outs[n].at[:, pl.ds(chip * width, width)]

        def x_load(i):
            return pltpu.make_async_copy(x_hbm.at[pl.ds(i * tn, tn), :], xbuf.at[i % 2], x_sems.at[i % 2])

        def h1_store(i):
            return pltpu.make_async_copy(hbuf.at[i % 2], h1_hbm.at[pl.ds(i * tn, tn), :], h_sems.at[i % 2])

        def first_norm():
            x_load(0).start()
            for i in range(n_tiles):
                if i + 1 < n_tiles:
                    x_load(i + 1).start()
                x_load(i).wait()
                if i >= 2:
                    h1_store(i - 2).wait()
                xv = xbuf[i % 2]
                rstd = lax.rsqrt(jnp.mean(xv * xv, axis=-1, keepdims=True) + EPS)
                hbuf[i % 2] = ((xv * rstd) * g1_ref[...]).astype(BF16)
                h1_store(i).start()
            for i in range(max(n_tiles - 2, 0), n_tiles):
                h1_store(i).wait()

        local += [pltpu.make_async_copy(conv_stages[n], columns(n, j), local_sems.at[nb + n]) for n in range(nc)]

        def small_copy(k, n, landed=False):
            px, py = plan.chips[k]
            return pltpu.make_async_remote_copy(
                src_ref=conv_stages[n], dst_ref=columns(n, 2 * px + py if landed else j), send_sem=sm_send.at[k, n],
                recv_sem=sm_recv.at[k, n], device_id=(px, py, c), device_id_type=MESH)

        for cp in loads:
            cp.start()
        hr = w_in.shape[0] // 2
        st_in[...] = win_ref[...].astype(BF16)
        plan.start_direct(0, st_in.at[pl.ds(c * hr, hr), :])
        for src, st in zip((cw_ref, rw_ref), conv_stages):
            st[...] = jnp.zeros_like(st)
            st[0:src.shape[0], :] = src[...]
        for k in range(3):
            for n in range(nc):
                small_copy(k, n).start()
        for src, bd in ((wa_ref, bda), (wx_ref, bdx)):
            bd[...] = jnp.zeros_like(bd)
            for h in range(N_HEADS):
                q = h % (BD // HEAD_DIM)
                bd[h // (BD // HEAD_DIM), q * HEAD_DIM:(q + 1) * HEAD_DIM, q * HEAD_DIM:(q + 1) * HEAD_DIM] = src[h].astype(BF16)
        for cp, full, st in zip(loads, (f_out, f_1, f_2), stages[1:]):
            cp.wait()
            st[...] = full[...].astype(BF16)
        for cp in local:
            cp.start()
        plan.start_pass_on(0)
        first_norm()
        plan.start_hand_over(0, diagonal=False)
        plan.start_hand_over(0, diagonal=True)
        for k in range(3):
            for n in range(nc):
                small_copy(k, n, landed=True).wait_recv()
                small_copy(k, n).wait_send()
        plan.finish(0)
        for cp in local:
            cp.wait()

    def gathered(a, dtype):
        return jax.ShapeDtypeStruct((N_CHIPS,) + a.shape, dtype)

    return pl.pallas_call(
        body, name="gather_first",
        in_specs=[VMEM] + [ANY] * (nb - 1) + [VMEM] * (nc + len(heads)) + [ANY, VMEM],
        out_specs=[ANY] * (nb + nc) + [VMEM] * len(heads) + [ANY],
        out_shape=[gathered(a, BF16) for a in bigs]
        + [jax.ShapeDtypeStruct((TILE_ROWS, N_CHIPS * a.shape[1]), F32) for a in convs]
        + [jax.ShapeDtypeStruct((N_BD, BD, BD), BF16) for _ in heads] + [jax.ShapeDtypeStruct((t, d), BF16)],
        scratch_shapes=[pltpu.VMEM(a.shape, BF16) for a in bigs] + [pltpu.VMEM(a.shape, F32) for a in bigs[1:]]
        + [pltpu.VMEM((TILE_ROWS, a.shape[1]), F32) for a in convs]
        + [pltpu.VMEM((2, tn, d), F32), pltpu.VMEM((2, tn, d), BF16)]
        + [pltpu.SemaphoreType.DMA((1, _ShardGather.PAIRS)), pltpu.SemaphoreType.DMA((1, _ShardGather.PAIRS)),
           pltpu.SemaphoreType.DMA((3, nc)), pltpu.SemaphoreType.DMA((3, nc)), pltpu.SemaphoreType.DMA((nb + nc,)),
           pltpu.SemaphoreType.DMA((nb - 1,)), pltpu.SemaphoreType.DMA((2,)), pltpu.SemaphoreType.DMA((2,))],
        compiler_params=_params(),
    )(*bigs, *convs, *heads, x, g1)


class _PartialExchange:
    def __init__(self, parts, arrived, send_sems, recv_sems, routes=None):
        self.parts, self.arrived, self.send_sems, self.recv_sems = parts, arrived, send_sems, recv_sems
        self.routes = routes or [(0, 1, 2)] * len(parts)
        x, y, c = _position()
        self.c, self.j = c, 2 * x + y
        self.chips = _other_chips(x, y)

    def _copy(self, w, k, slot):
        px, py = self.chips[k]
        return pltpu.make_async_remote_copy(
            src_ref=self.parts[w].at[2 * px + py], dst_ref=self.arrived[w].at[slot], send_sem=self.send_sems.at[w, k],
            recv_sem=self.recv_sems.at[w, k], device_id=(px, py, self.c), device_id_type=MESH)

    def start(self):
        for w in range(len(self.parts)):
            for k in self.routes[w]:
                self._copy(w, k, self.j).start()

    def wait(self):
        for w in range(len(self.parts)):
            for k in self.routes[w]:
                px, py = self.chips[k]
                self._copy(w, k, 2 * px + py).wait()


class _PackGather:
    def __init__(self, p_ref, all_ref, send_sems, recv_sems, local_sem):
        self.p_ref, self.all_ref, self.send_sems, self.recv_sems, self.local_sem = p_ref, all_ref, send_sems, recv_sems, local_sem
        x, y, c = _position()
        self.me, self.sibling, self.c = (x, y, c), (x, y, 1 - c), c
        self.chips = _other_chips(x, y)

    @staticmethod
    def semaphores():
        return [pltpu.SemaphoreType.DMA((7,)), pltpu.SemaphoreType.DMA((7,)), pltpu.SemaphoreType.DMA]

    def _copy(self, k, block, to, from_pack=False):
        px, py, pc = block
        slot = self.all_ref.at[4 * px + 2 * py + pc]
        return pltpu.make_async_remote_copy(src_ref=self.p_ref if from_pack else slot, dst_ref=slot, send_sem=self.send_sems.at[k],
                                            recv_sem=self.recv_sems.at[k], device_id=to, device_id_type=MESH)

    def _mine(self):
        x, y, c = self.me
        return pltpu.make_async_copy(self.p_ref, self.all_ref.at[4 * x + 2 * y + c], self.local_sem)

    def _first(self):
        return [self._copy(0, self.me, self.sibling, True)] + [
            self._copy(1 + k, self.me, (*chip, self.c), True) for k, chip in enumerate(self.chips)]

    def _passed(self):
        return [self._copy(4 + k, (*chip, self.c), self.sibling) for k, chip in enumerate(self.chips)]

    def start(self):
        self._mine().start()
        for cp in self._first():
            cp.start()

    def hand_over(self):
        for k, chip in enumerate(self.chips):
            self._copy(1 + k, (*chip, self.c), self.me).wait_recv()
            self._passed()[k].start()

    def finish(self):
        self._copy(0, self.sibling, self.me).wait_recv()
        for k, chip in enumerate(self.chips):
            self._copy(4 + k, (*chip, 1 - self.c), self.me).wait_recv()
        for cp in self._first() + self._passed():
            cp.wait_send()
        self._mine().wait()


class _DirectGather:
    def __init__(self, p_ref, all_ref, send_sems, recv_sems, local_sem):
        self.p_ref, self.all_ref, self.send_sems, self.recv_sems, self.local_sem = p_ref, all_ref, send_sems, recv_sems, local_sem
        self.me = _position()

    semaphores = _PackGather.semaphores

    def _peer(self, r):
        x, y, c = self.me
        return ((1 - x) if r & 4 else x, (1 - y) if r & 2 else y, (1 - c) if r & 1 else c)

    def _copy(self, r, slot_of):
        px, py, pc = slot_of
        return pltpu.make_async_remote_copy(src_ref=self.p_ref, dst_ref=self.all_ref.at[4 * px + 2 * py + pc],
                                            send_sem=self.send_sems.at[r - 1], recv_sem=self.recv_sems.at[r - 1],
                                            device_id=self._peer(r), device_id_type=MESH)

    def _mine(self):
        x, y, c = self.me
        return pltpu.make_async_copy(self.p_ref, self.all_ref.at[4 * x + 2 * y + c], self.local_sem)

    def start(self):
        self._mine().start()
        for r in range(1, N_DEVICES):
            self._copy(r, self.me).start()

    def finish(self):
        for r in range(1, N_DEVICES):
            self._copy(r, self._peer(r)).wait()
        self._mine().wait()


def _adamw(w, g, m, v):
    m = ADAM_B1 * m + (1.0 - ADAM_B1) * g
    v = ADAM_B2 * v + (1.0 - ADAM_B2) * (g * g)
    m_hat = m / ADAM_BC1
    v_hat = v / ADAM_BC2
    delta = -ADAM_LR * (m_hat / (jnp.sqrt(v_hat) + ADAM_EPS) + ADAM_WD * w)
    return delta, m, v


JOIN_SUB = 4


def _join(tag, shard_shape, part, arrived, core_chip, block=None):
    pr, pc = WGRAD_GEOMETRY[tag][:2]
    rb = pr // JOIN_SUB
    by_rows = shard_shape[1] == pc
    riding = block is not None

    def body(cc_ref, p_ref, r1_ref, r2_ref, r3_ref, *rest):
        if riding:
            blk_ref, g_ref, all_ref, stage, send_sems, recv_sems, local_sems, b_send, b_recv, b_local = rest
            gather = _DirectGather(blk_ref, all_ref, b_send, b_recv, b_local)
        else:
            g_ref, stage, send_sems, recv_sems, local_sems = rest
        i = pl.program_id(0)
        c = cc_ref[0]
        if riding:
            @pl.when(i == 0)
            def _():
                gather.start()

        def window(core, k):
            if by_rows:
                return g_ref.at[pl.ds((core * JOIN_SUB + k) * rb, rb), :]
            return g_ref.at[pl.ds(k * rb, rb), pl.ds(core * pc, pc)]

        def keep(k):
            return pltpu.make_async_copy(stage.at[k], window(c, k), local_sems.at[k])

        def push(k):
            return pltpu.make_async_remote_copy(src_ref=stage.at[k], dst_ref=window(c, k), send_sem=send_sems.at[k],
                                                recv_sem=recv_sems.at[k], device_id=_sibling(), device_id_type=MESH)

        def pushed(k):
            return pltpu.make_async_remote_copy(src_ref=stage.at[k], dst_ref=window(1 - c, k), send_sem=send_sems.at[k],
                                                recv_sem=recv_sems.at[k], device_id=_sibling(), device_id_type=MESH)

        stage[i] = ((p_ref[0] + r1_ref[0].astype(F32)) + r2_ref[0].astype(F32)) + r3_ref[0].astype(F32)
        keep(i).start()
        push(i).start()

        @pl.when(i == JOIN_SUB - 1)
        def _():
            for k in range(JOIN_SUB):
                keep(k).wait()
                push(k).wait_send()
                pushed(k).wait_recv()
            if riding:
                gather.finish()

    def partial(off):
        return pl.BlockSpec((1, rb, pc), lambda i, cc: ((cc[1] + off) % N_CHIPS, i, 0))

    in_specs = [partial(0), partial(1), partial(2), partial(3)]
    out_specs = [ANY]
    out_shape = [jax.ShapeDtypeStruct(shard_shape, F32)]
    scratch = [pltpu.VMEM((JOIN_SUB, rb, pc), F32), pltpu.SemaphoreType.DMA((JOIN_SUB,)),
               pltpu.SemaphoreType.DMA((JOIN_SUB,)), pltpu.SemaphoreType.DMA((JOIN_SUB,))]
    operands = [part, arrived, arrived, arrived]
    if riding:
        in_specs.append(pl.BlockSpec(block.shape, lambda i, cc: (0, 0)))
        out_specs.append(ANY)
        out_shape.append(jax.ShapeDtypeStruct((N_DEVICES,) + block.shape, block.dtype))
        scratch += _DirectGather.semaphores()
        operands.append(block)
    outs = pl.pallas_call(
        body, name="join_" + tag,
        grid_spec=pltpu.PrefetchScalarGridSpec(
            num_scalar_prefetch=1, grid=(JOIN_SUB,), in_specs=in_specs, out_specs=out_specs, scratch_shapes=scratch),
        out_shape=out_shape,
        compiler_params=_params(dimension_semantics=("arbitrary",)),
    )(core_chip, *operands)
    return outs if riding else outs[0]


def _adamw_big(w, g, m, v, name):
    rows, cols = w.shape
    rb = ADAMW_ROWS if rows % ADAMW_ROWS == 0 else rows

    def body(w_ref, g_ref, m_ref, v_ref, go_ref, d_ref, nm_ref, nv_ref):
        g = g_ref[...]
        go_ref[...] = g
        d_ref[...], nm_ref[...], nv_ref[...] = _adamw(w_ref[...], g, m_ref[...], v_ref[...])

    spec = pl.BlockSpec((rb, cols), lambda i: (i, 0))
    return pl.pallas_call(
        body, name=name, grid=(rows // rb,), in_specs=[spec] * 4, out_specs=[spec] * 4,
        out_shape=[jax.ShapeDtypeStruct(w.shape, F32)] * 4,
        compiler_params=_params(dimension_semantics=("arbitrary",)),
    )(w, g, m, v)


SMALL_VECTORS = {
    "norm_mix_g": (PK_MIX_G, D_MODEL), "rnn_conv_b": (PK_RCONV_B, LRU_W), "b_a": (PK_B_A, LRU_W), "b_x": (PK_B_X, LRU_W),
    "lru_lambda": (PK_LAMBDA, LRU_W), "g_norm_conv": (PK_G_NORM_CONV, CONV_W), "g_norm_rnn": (PK_G_NORM_RNN, LRU_W),
    "norm_mlp_g": (PK_MLP_G, D_MODEL), "final_norm_g": (PK_FINAL_G, D_MODEL),
}
SMALL_MATRICES = ("w_a", "w_x")


def _small_step(vec_packs, mat_packs, mix_g_blocks, p):
    vec_rows, cols = vec_packs.shape[1:]
    conv_rows, cshard = p["conv_w"].shape
    rconv_rows, rshard = p["rnn_conv_w"].shape
    names = list(SMALL_VECTORS) + list(SMALL_MATRICES) + ["conv_w", "rnn_conv_w"]
    shapes = ([(1, width) for _, width in SMALL_VECTORS.values()] + [mat_packs.shape[2:]] * len(SMALL_MATRICES)
              + [(conv_rows, cshard), (rconv_rows, rshard)])
    kinds = ("", "m_", "v_")
    params = [p[pre + n].reshape(1, -1) if n in SMALL_VECTORS else p[pre + n] for pre in kinds for n in names]

    def body(vec_ref, mat_ref, blk_ref, *rest):
        wmv = [dict(zip(names, rest[k * len(names):(k + 1) * len(names)])) for k in range(3)]
        loss_ref, rest = rest[3 * len(names)], rest[3 * len(names) + 1:]
        leaves, (g_ref, w_ref, m_ref, v_ref) = [rest[k * len(names):(k + 1) * len(names)] for k in range(4)], rest[4 * len(names):]
        total = vec_ref[0]
        mats = mat_ref[0].astype(F32)
        late = blk_ref[0]
        for k in range(1, N_DEVICES):
            total = total + vec_ref[k]
            mats = mats + mat_ref[k].astype(F32)
            late = late + blk_ref[k]
        g_ref[0:vec_rows, :] = total
        g_ref[vec_rows:, :] = late
        g = g_ref[...]
        loss_ref[...] = g[PK_LOSS:PK_LOSS + 1, 0:1]

        for pack_ref, given in zip((w_ref, m_ref, v_ref), wmv):
            pack_ref[...] = jnp.zeros_like(pack_ref)
            for name, (row, width) in SMALL_VECTORS.items():
                pack_ref[row:row + 1, 0:width] = given[name][...]

        x, y, _ = _position()
        j = 2 * x + y
        cblk = total[0:TILE_ROWS, :]
        rblk = total[PK_RCONV_W:PK_RCONV_W + TILE_ROWS, :]
        cg = cblk[:, 0:cshard]
        rg = rblk[:, 0:rshard]
        for k in range(1, N_CHIPS):
            cg = jnp.where(j == k, cblk[:, k * cshard:(k + 1) * cshard], cg)
            rg = jnp.where(j == k, rblk[:, k * rshard:(k + 1) * rshard], rg)
        cg = cg[PK_CONV_W:PK_CONV_W + conv_rows, :]
        rg = rg[0:rconv_rows, :]

        def step(name, grad):
            return (grad,) + _adamw(wmv[0][name][...], grad, wmv[1][name][...], wmv[2][name][...])

        packs = (g,) + _adamw(w_ref[...], g, m_ref[...], v_ref[...])
        matrices = [step(name, mats[n]) for n, name in enumerate(SMALL_MATRICES)]
        convs, rconvs = step("conv_w", cg), step("rnn_conv_w", rg)
        for kind in range(4):
            out = dict(zip(names, leaves[kind]))
            for name, (row, width) in SMALL_VECTORS.items():
                out[name][...] = packs[kind][row:row + 1, 0:width]
            for n, name in enumerate(SMALL_MATRICES):
                out[name][...] = matrices[n][kind]
            out["conv_w"][...] = convs[kind]
            out["rnn_conv_w"][...] = rconvs[kind]

    outs = pl.pallas_call(
        body, name="small_grads_step", in_specs=[VMEM] * (3 + len(params)), out_specs=[VMEM] * (1 + 4 * len(names)),
        out_shape=[jax.ShapeDtypeStruct((1, 1), F32)] + [jax.ShapeDtypeStruct(sh, F32) for sh in shapes] * 4,
        scratch_shapes=[pltpu.VMEM((PK_ROWS, cols), F32)] * 4,
        compiler_params=_params(),
    )(vec_packs, mat_packs, mix_g_blocks, *params)
    return outs[0], [dict(zip(names, outs[1 + k * len(names):1 + (k + 1) * len(names)])) for k in range(4)]


_NAMES = ['norm_mix_g', 'w_in', 'conv_w', 'rnn_conv_w', 'rnn_conv_b', 'w_a', 'b_a', 'w_x', 'b_x', 'lru_lambda',
          'g_norm_conv', 'g_norm_rnn', 'w_out', 'norm_mlp_g', 'w_mlp_in', 'w_mlp_out', 'final_norm_g']


def kernel(x, norm_mix_g, w_in, conv_w, rnn_conv_w, rnn_conv_b, w_a, b_a, w_x, b_x, lru_lambda, g_norm_conv, g_norm_rnn, w_out, norm_mlp_g, w_mlp_in, w_mlp_out, final_norm_g, loss_target, m_norm_mix_g, m_w_in, m_conv_w, m_rnn_conv_w, m_rnn_conv_b, m_w_a, m_b_a, m_w_x, m_b_x, m_lru_lambda, m_g_norm_conv, m_g_norm_rnn, m_w_out, m_norm_mlp_g, m_w_mlp_in, m_w_mlp_out, m_final_norm_g, v_norm_mix_g, v_w_in, v_conv_w, v_rnn_conv_w, v_rnn_conv_b, v_w_a, v_b_a, v_w_x, v_b_x, v_lru_lambda, v_g_norm_conv, v_g_norm_rnn, v_w_out, v_norm_mlp_g, v_w_mlp_in, v_w_mlp_out, v_final_norm_g):
    args = dict(locals())
    p = {}
    for n in _NAMES:
        for pre in ("", "m_", "v_"):
            a = args[pre + n]
            p[pre + n] = a[0] if a.ndim >= 3 else a
    xs = x[0]
    target = loss_target[0]
    core_chip = jnp.stack([lax.axis_index("c"), 2 * lax.axis_index("x") + lax.axis_index("y")]).astype(jnp.int32)

    w_in_g, w_out_g, w1_g, w2_g, conv_full, rconv_full, wa_bd, wx_bd, h1b = _gather_first(
        p["w_in"], p["w_out"], p["w_mlp_in"], p["w_mlp_out"], p["conv_w"], p["rnn_conv_w"], p["w_a"], p["w_x"],
        xs, p["norm_mix_g"])
    gf = p["final_norm_g"].reshape(1, -1)
    lru = (wa_bd, p["b_a"], wx_bd, p["b_x"], p["lru_lambda"], p["g_norm_conv"], p["g_norm_rnn"])

    (u, xr, hs, c3, yb, gates), (w_out_g, w1_g, w2_g) = _fwd_mix(
        h1b, w_in_g, conv_full, rconv_full, p["rnn_conv_b"], *lru, (w_out_g, w1_g, w2_g))
    zb, dpb, h2b, dx3b, dx2, dx2b, dy, st_mlp = _mlp_fwd_bwd(
        xs, yb, w_out_g.reshape(-1, D_MODEL), w1_g, w2_g.reshape(-1, D_MODEL), p["norm_mlp_g"], gf, target)

    part_out = _wgrad(yb, dx2b, "out", core_chip)
    *part_1, arrived_out = _wgrad(h2b, dpb, "mlp_in", core_chip, parts=(part_out[1],))
    *part_2, arrived_1 = _wgrad(zb, dx3b, "mlp_out", core_chip, parts=(part_1[1],), routes=[(0, 1)])
    (dub, vec_pack, mat_pack), (arrived_1, arrived_2) = _mix_bwd(
        dy, u, xr, hs, c3, gates, conv_full, rconv_full, wa_bd, wx_bd, p["lru_lambda"], p["g_norm_conv"], p["g_norm_rnn"],
        st_mlp, (part_1[1], part_2[1]), routes=[(2,), (0, 1, 2)], begun=(arrived_1,))
    arrived_mlp = (arrived_out, arrived_1, arrived_2)
    *part_in, vec_packs, mat_packs = _wgrad(h1b, dub, "in", core_chip, packs=(vec_pack, mat_pack))
    early = (("w_out", "out", part_out, arrived_mlp[0]), ("w_mlp_in", "mlp_in", part_1, arrived_mlp[1]),
             ("w_mlp_out", "mlp_out", part_2, arrived_mlp[2]))
    (grad_x, st_in), arrived_in, joined = _in_bwd(
        dub, w_in_g, xs, dx2, p["norm_mix_g"], (part_in[1],),
        [(tag, p[n].shape, part[0], arrived) for n, tag, part, arrived in early], core_chip)
    g_in, mix_g_blocks = _join("in", p["w_in"].shape, part_in[0], arrived_in[0], core_chip, st_in)
    big = {}
    for n, tag, g in [(n, tag, g) for (n, tag, _, _), g in zip(early, joined)] + [("w_in", "in", g_in)]:
        big[n] = _adamw_big(p[n], g, p["m_" + n], p["v_" + n], "adamw_" + tag)

    loss, outs = _small_step(vec_packs, mat_packs, mix_g_blocks, p)
    for kind, o in enumerate(outs):
        o["final_norm_g"] = o["final_norm_g"].reshape(-1)
        for n in SMALL_MATRICES + ("conv_w", "rnn_conv_w"):
            o[n] = o[n][None]
        for n in ("w_in", "w_out", "w_mlp_in", "w_mlp_out"):
            o[n] = big[n][kind][None]
    loss = loss.reshape(())
    return (loss, grad_x[None], *[o[n] for o in outs for n in _NAMES])
```

```python
import functools
import math

import jax
import jax.numpy as jnp
from jax import lax
from jax.experimental import pallas as pl
from jax.experimental.pallas import tpu as pltpu

F32 = jnp.float32
BF16 = jnp.bfloat16
MESH = pl.DeviceIdType.MESH
ANY = pl.BlockSpec(memory_space=pl.ANY)
VMEM = pl.BlockSpec(memory_space=pltpu.VMEM)

EPS = 1e-6
LRU_C = 8.0
D_MODEL = 1024
CONV_W = 512
LRU_W = 1024
IN_COLS = 3 * CONV_W + 2 * LRU_W
IN_SHARD = IN_COLS // 4
N_CHIPS = 4
N_DEVICES = 8
BD = 256
N_BD = LRU_W // BD

ADAM_LR = 0.001
ADAM_B1 = 0.9
ADAM_B2 = 0.999
ADAM_EPS = 1e-08
ADAM_WD = 0.01
ADAM_STEP = 10
ADAM_BC1 = 1.0 - ADAM_B1 ** ADAM_STEP
ADAM_BC2 = 1.0 - ADAM_B2 ** ADAM_STEP

TILE_ROWS, LANES = 8, 128
TOKEN_TILE = 256
MATMUL_TOKEN_TILE = 512
ADAMW_ROWS = 256
VMEM_LIMIT = 56 * 1024 * 1024

PK_G_NORM_RNN, PK_RCONV_B, PK_B_A, PK_B_X, PK_LAMBDA, PK_CONV_W = 0, 1, 2, 3, 4, 5
PK_RCONV_W, PK_G_NORM_CONV = 8, 12
PK_MIX_ROWS = 16
PK_FINAL_G, PK_MLP_G, PK_LOSS = 16, 17, 18
PK_MLP_ROWS = 8
PK_MIX_G = 24
PK_ROWS = 32
N_HEADS, HEAD_DIM = 16, 64


def _params(**kw):
    return pltpu.CompilerParams(vmem_limit_bytes=VMEM_LIMIT, **kw)


def _position():
    x, y, c = lax.axis_index("x"), lax.axis_index("y"), lax.axis_index("c")
    return x, y, c


def _sigmoid(v):
    return 1.0 / (1.0 + jnp.exp(-v))


def _one_minus_square(log_a, a):
    v = 2.0 * log_a
    series = -v * (1.0 + v * (0.5 + v * (1.0 / 6.0)))
    return jnp.where(v > -0.01, series, 1.0 - a * a)


_GELU_C = math.sqrt(2.0 / math.pi)
_GELU_K = 0.044715


def _gelu_and_grad(g):
    th = jnp.tanh(_GELU_C * (g + _GELU_K * g * g * g))
    gelu = 0.5 * g * (1.0 + th)
    dgelu = 0.5 * (1.0 + th) + 0.5 * g * (1.0 - th * th) * (_GELU_C * (1.0 + 3.0 * _GELU_K * g * g))
    return gelu, dgelu


def _rows(shape):
    return lax.broadcasted_iota(jnp.int32, shape, 0)


def _shift_down(v, k, prev8):
    rolled = pltpu.roll(v, k, 0)
    halo = pltpu.roll(prev8, k, 0)
    head = jnp.where(_rows(halo.shape) < k, halo, rolled[:TILE_ROWS])
    return jnp.concatenate([head, rolled[TILE_ROWS:]], axis=0)


def _shift_up(v, k, next8):
    n = v.shape[0]
    rolled = pltpu.roll(v, n - k, 0)
    halo = pltpu.roll(next8, TILE_ROWS - k, 0)
    tail = jnp.where(_rows(halo.shape) >= TILE_ROWS - k, halo, rolled[n - TILE_ROWS:])
    return jnp.concatenate([rolled[: n - TILE_ROWS], tail], axis=0)


def _scan_rows(a, b, carry, reverse=False):
    n, w = a.shape
    groups = n // TILE_ROWS
    a3 = a.reshape(groups, TILE_ROWS, w)
    b3 = b.reshape(groups, TILE_ROWS, w)
    sub = lax.broadcasted_iota(jnp.int32, a3.shape, 1)
    s = 1
    while s < TILE_ROWS:
        shift = TILE_ROWS - s if reverse else s
        keep = (sub < TILE_ROWS - s) if reverse else (sub >= s)
        b3 = b3 + jnp.where(keep, a3 * pltpu.roll(b3, shift, 1), 0.0)
        a3 = a3 * jnp.where(keep, pltpu.roll(a3, shift, 1), 1.0)
        s *= 2
    out = [None] * groups
    edge = 0 if reverse else TILE_ROWS - 1
    for g in (range(groups - 1, -1, -1) if reverse else range(groups)):
        out[g] = b3[g] + a3[g] * carry
        carry = out[g][edge:edge + 1]
    return jnp.concatenate(out, axis=0)


def _softplus_neg(lam):
    e = jnp.exp(-jnp.abs(lam))
    log1p_e = jnp.where(e < 1e-2, e * (1.0 - e * (0.5 - e * (1.0 / 3.0 - e * 0.25))), jnp.log(1.0 + e))
    sp = jnp.maximum(-lam, 0.0) + log1p_e
    dsp = -_sigmoid(-lam)
    return sp, dsp


def _block_diag_dot(vb, w_ref):
    return jnp.concatenate(
        [jnp.dot(vb[:, j * BD:(j + 1) * BD], w_ref[j], preferred_element_type=F32) for j in range(N_BD)], axis=1)


def _block_diag_dot_t(vb, w_ref):
    return jnp.concatenate(
        [lax.dot_general(vb[:, j * BD:(j + 1) * BD], w_ref[j], (((1,), (1,)), ((), ())), preferred_element_type=F32)
         for j in range(N_BD)], axis=1)


def _dot_nt(a, b):
    return lax.dot_general(a, b, (((1,), (1,)), ((), ())), preferred_element_type=F32)


def _dot_tn(a, b):
    return lax.dot_general(a, b, (((0,), (0,)), ((), ())), preferred_element_type=F32)


def _lru_gates(xr, wa_ref, ba, wx_ref, bx, sp):
    xrb = xr.astype(BF16)
    r = _sigmoid(_block_diag_dot(xrb, wa_ref) + ba)
    ig = _sigmoid(_block_diag_dot(xrb, wx_ref) + bx)
    log_a = (-LRU_C) * r * sp
    a = jnp.exp(log_a)
    mult = jnp.sqrt(_one_minus_square(log_a, a))
    return r, ig, a, mult


def _colsum(v):
    return jnp.sum(v, axis=0, keepdims=True)


N_FWD_OUT = 6


def _fwd_mix(h1b, w_in_g, conv_w, rconv_w, rconv_b, wa_bd, b_a, wx_bd, b_x, lam, g_nc, g_nr, later):
    t, d = h1b.shape
    tm = TOKEN_TILE
    nt = t // tm
    nl = len(later)
    assert nl == 3
    pass_on_at = [nt * f // 16 for f in (3, 5, 9)]
    neighbours_at = [nt * f // 16 for f in (10, 11, 12)]
    diagonal_at = [nt * f // 16 for f in (13, 14, 14)]

    def body(h1_ref, win_ref, cw_ref, rw_ref, rb_ref, wa_ref, ba_ref, wx_ref, bx_ref, lam_ref, gnc_ref, gnr_ref, *rest):
        later_in, outs, rest = rest[:nl], rest[nl:nl + N_FWD_OUT], rest[nl + N_FWD_OUT:]
        u_ref, xr_ref, hs_ref, c3_ref, y_ref, gates_ref = outs
        later_out, (cv_prev, xin_prev, h_prev, send_sems, recv_sems) = rest[:nl], rest[nl:]
        del later_in
        step = pl.program_id(0)
        plan = _ShardGather(later_out, send_sems, recv_sems)

        @pl.when(step == 0)
        def _():
            cv_prev[...] = jnp.zeros_like(cv_prev)
            xin_prev[...] = jnp.zeros_like(xin_prev)
            h_prev[...] = jnp.zeros_like(h_prev)
            for w in range(nl):
                plan.start_direct(w)

        for w in range(nl):
            @pl.when(step == pass_on_at[w])
            def _(w=w):
                plan.start_pass_on(w)

            @pl.when(step == neighbours_at[w])
            def _(w=w):
                plan.start_hand_over(w, diagonal=False)

            @pl.when(step == diagonal_at[w])
            def _(w=w):
                plan.start_hand_over(w, diagonal=True)

        h1b = h1_ref[...]
        for j in range(N_CHIPS):
            u_ref[:, j * IN_SHARD:(j + 1) * IN_SHARD] = jnp.dot(h1b, win_ref[j], preferred_element_type=F32)
        gate_b = u_ref[:, 0:CONV_W]
        cv = u_ref[:, CONV_W:2 * CONV_W] * u_ref[:, 2 * CONV_W:3 * CONV_W]
        x_r = u_ref[:, 3 * CONV_W:3 * CONV_W + LRU_W]
        g = u_ref[:, 3 * CONV_W + LRU_W:]

        cw = cw_ref[...]
        cvp = cv_prev[...]
        conv3 = cw[0:1] * _shift_down(cv, 2, cvp) + cw[1:2] * _shift_down(cv, 1, cvp) + cw[2:3] * cv
        cv_prev[...] = cv[tm - TILE_ROWS:]
        c3_ref[...] = conv3
        y_conv = gate_b * conv3

        rw = rw_ref[...]
        xp = xin_prev[...]
        xr = (rw[0:1] * _shift_down(x_r, 3, xp) + rw[1:2] * _shift_down(x_r, 2, xp)
              + rw[2:3] * _shift_down(x_r, 1, xp) + rw[3:4] * x_r) + rb_ref[...]
        xin_prev[...] = x_r[tm - TILE_ROWS:]
        xr_ref[...] = xr
        sp, _ = _softplus_neg(lam_ref[...])
        r, ig, a, mult = _lru_gates(xr, wa_ref, ba_ref[...], wx_ref, bx_ref[...], sp)
        for n, gate in enumerate((r, ig, a, mult)):
            gates_ref[:, n * LRU_W:(n + 1) * LRU_W] = gate
        h = _scan_rows(a, mult * (ig * xr), h_prev[...])
        h_prev[...] = h[tm - 1:tm]
        hs_ref[...] = h
        gelu, _ = _gelu_and_grad(g)
        y_rnn = h * gelu

        na = y_conv * lax.rsqrt(jnp.mean(y_conv * y_conv, axis=-1, keepdims=True) + EPS) * gnc_ref[...]
        nb = y_rnn * lax.rsqrt(jnp.mean(y_rnn * y_rnn, axis=-1, keepdims=True) + EPS) * gnr_ref[...]
        y_ref[:, :CONV_W] = na.astype(BF16)
        y_ref[:, CONV_W:] = nb.astype(BF16)

        @pl.when(step == nt - 1)
        def _():
            for w in range(nl):
                plan.finish(w)

    def full(a):
        nd = a.ndim
        return pl.BlockSpec(a.shape, lambda i: (0,) * nd)

    def tok(cols):
        return pl.BlockSpec((tm, cols), lambda i: (i, 0))

    def act(cols, dtype=F32):
        return jax.ShapeDtypeStruct((t, cols), dtype)

    smalls = (w_in_g, conv_w, rconv_w, rconv_b, wa_bd, b_a, wx_bd, b_x, lam, g_nc, g_nr)
    n_in = 1 + len(smalls)
    outs = pl.pallas_call(
        body, name="fwd_mix", grid=(nt,),
        in_specs=[tok(d)] + [full(a) for a in smalls] + [ANY] * nl,
        out_specs=[tok(IN_COLS), tok(LRU_W), tok(LRU_W), tok(CONV_W), tok(CONV_W + LRU_W)]
        + [tok(4 * LRU_W)] + [ANY] * nl,
        out_shape=[act(IN_COLS), act(LRU_W), act(LRU_W), act(CONV_W), act(CONV_W + LRU_W, BF16)]
        + [act(4 * LRU_W)] + [jax.ShapeDtypeStruct(a.shape, a.dtype) for a in later],
        input_output_aliases={n_in + w: N_FWD_OUT + w for w in range(nl)},
        scratch_shapes=[pltpu.VMEM((TILE_ROWS, CONV_W), F32), pltpu.VMEM((TILE_ROWS, LRU_W), F32),
                        pltpu.VMEM((1, LRU_W), F32), pltpu.SemaphoreType.DMA((nl, _ShardGather.PAIRS)),
                        pltpu.SemaphoreType.DMA((nl, _ShardGather.PAIRS))],
        compiler_params=_params(dimension_semantics=("arbitrary",)),
    )(h1b, *smalls, *later)
    return outs[:N_FWD_OUT], outs[N_FWD_OUT:]


def _mlp_fwd_bwd(x, yb, w_out_g, w1_g, w2_g, g2, gf, target):
    t, d = x.shape
    tm = TOKEN_TILE
    ff = w2_g.shape[0]
    mix = w_out_g.shape[0]
    ffs = ff // N_CHIPS

    def body(x_ref, y_ref, g2_ref, gf_ref, tgt_ref, wout_hbm, w1_hbm, w2_hbm,
             z_ref, dp_ref, h2_ref, dx3b_ref, dx2_ref, dx2b_ref, dy_ref, st_ref, wout, w1, w2, p_ref):
        @pl.when(pl.program_id(0) == 0)
        def _():
            pltpu.sync_copy(wout_hbm, wout)
            pltpu.sync_copy(w1_hbm, w1)
            pltpu.sync_copy(w2_hbm, w2)
            st_ref[...] = jnp.zeros_like(st_ref)

        x2 = x_ref[...] + jnp.dot(y_ref[...], wout[...], preferred_element_type=F32)
        r2 = lax.rsqrt(jnp.mean(x2 * x2, axis=-1, keepdims=True) + EPS)
        xh2 = x2 * r2
        g2v = g2_ref[...]
        h2b = (xh2 * g2v).astype(BF16)
        h2_ref[...] = h2b
        for j in range(N_CHIPS):
            p_ref[:, j * ffs:(j + 1) * ffs] = jnp.dot(h2b, w1[j], preferred_element_type=F32)
        rp = jnp.maximum(p_ref[...], 0.0)
        zb = (rp * rp).astype(BF16)
        z_ref[...] = zb
        x3 = x2 + jnp.dot(zb, w2[...], preferred_element_type=F32)
        r3 = lax.rsqrt(jnp.mean(x3 * x3, axis=-1, keepdims=True) + EPS)
        xh3 = x3 * r3
        gfv = gf_ref[...]
        err = xh3 * gfv - tgt_ref[...]
        loss = (0.5 / d) * jnp.sum(err * err)
        dout = err * (1.0 / d)
        st_ref[PK_FINAL_G - PK_MIX_ROWS:PK_FINAL_G - PK_MIX_ROWS + 1, :] += _colsum(dout * xh3)
        st_ref[PK_LOSS - PK_MIX_ROWS:PK_LOSS - PK_MIX_ROWS + 1, :] += jnp.zeros((1, d), F32) + loss
        dxh3 = dout * gfv
        dx3 = r3 * (dxh3 - xh3 * jnp.mean(dxh3 * xh3, axis=-1, keepdims=True))
        dx3b = dx3.astype(BF16)
        dx3b_ref[...] = dx3b
        dpb = (_dot_nt(dx3b, w2[...]) * (2.0 * rp)).astype(BF16)
        dp_ref[...] = dpb
        dh2 = _dot_nt(dpb[:, 0:ffs], w1[0])
        for j in range(1, N_CHIPS):
            dh2 = dh2 + _dot_nt(dpb[:, j * ffs:(j + 1) * ffs], w1[j])
        st_ref[PK_MLP_G - PK_MIX_ROWS:PK_MLP_G - PK_MIX_ROWS + 1, :] += _colsum(dh2 * xh2)
        dxh2 = dh2 * g2v
        dx2 = dx3 + r2 * (dxh2 - xh2 * jnp.mean(dxh2 * xh2, axis=-1, keepdims=True))
        dx2_ref[...] = dx2
        dx2b = dx2.astype(BF16)
        dx2b_ref[...] = dx2b
        dy_ref[...] = _dot_nt(dx2b, wout[...])

    def tok(cols):
        return pl.BlockSpec((tm, cols), lambda i: (i, 0))

    def row(cols):
        return pl.BlockSpec((1, cols), lambda i: (0, 0))

    return pl.pallas_call(
        body, name="mlp_fwd_bwd", grid=(t // tm,),
        in_specs=[tok(d), tok(mix), row(d), row(d), tok(d), ANY, ANY, ANY],
        out_specs=[tok(ff), tok(ff), tok(d), tok(d), tok(d), tok(d), tok(mix),
                   pl.BlockSpec((PK_MLP_ROWS, d), lambda i: (0, 0))],
        out_shape=[jax.ShapeDtypeStruct((t, ff), BF16), jax.ShapeDtypeStruct((t, ff), BF16),
                   jax.ShapeDtypeStruct((t, d), BF16), jax.ShapeDtypeStruct((t, d), BF16),
                   jax.ShapeDtypeStruct((t, d), F32), jax.ShapeDtypeStruct((t, d), BF16),
                   jax.ShapeDtypeStruct((t, mix), F32), jax.ShapeDtypeStruct((PK_MLP_ROWS, d), F32)],
        scratch_shapes=[pltpu.VMEM(w_out_g.shape, BF16), pltpu.VMEM(w1_g.shape, BF16), pltpu.VMEM(w2_g.shape, BF16),
                        pltpu.VMEM((tm, ff), F32)],
        compiler_params=_params(dimension_semantics=("arbitrary",)),
    )(x, yb, g2, gf, target, w_out_g, w1_g, w2_g)


def _mix_bwd(dy, u, xr_all, hs_all, c3_all, gates, conv_w, rconv_w, wa_bd, wx_bd, lam, g_nc, g_nr, st_mlp, parts):
    t = dy.shape[0]
    tm = TOKEN_TILE
    nt = t // tm
    hb = tm // TILE_ROWS
    npart = len(parts)

    def body(dy_ref, u_ref, uh_ref, xr_ref, hs_ref, hh_ref, c3_ref, gates_ref,
             cw_ref, rw_ref, wa_ref, wx_ref, lam_ref, gnc_ref, gnr_ref, stm_ref, *rest):
        part_refs, (du_ref, st_ref, heads_ref), rest = rest[:npart], rest[npart:npart + 3], rest[npart + 3:]
        arrived_refs, (dc_next, a_next, gs_next, dxr_next, dwa_ref, dwx_ref, send_sems, recv_sems) = rest[:npart], rest[npart:]
        exchange = _PartialExchange(part_refs, arrived_refs, send_sems, recv_sems)
        i = pl.program_id(0)

        @pl.when(i == 0)
        def _():
            exchange.start()
            dc_next[...] = jnp.zeros_like(dc_next)
            a_next[...] = jnp.zeros_like(a_next)
            gs_next[...] = jnp.zeros_like(gs_next)
            dxr_next[...] = jnp.zeros_like(dxr_next)
            st_ref[0:PK_MIX_ROWS, :] = jnp.zeros((PK_MIX_ROWS, LRU_W), F32)
            st_ref[PK_MIX_ROWS:, :] = stm_ref[...]
            dwa_ref[...] = jnp.zeros_like(dwa_ref)
            dwx_ref[...] = jnp.zeros_like(dwx_ref)

        first_tile = i == nt - 1
        gate_b = u_ref[:, 0:CONV_W]
        gate_c = u_ref[:, CONV_W:2 * CONV_W]
        v = u_ref[:, 2 * CONV_W:3 * CONV_W]
        x_r = u_ref[:, 3 * CONV_W:3 * CONV_W + LRU_W]
        g = u_ref[:, 3 * CONV_W + LRU_W:]
        cv = gate_c * v
        cv_prev = jnp.where(first_tile, 0.0, uh_ref[:, CONV_W:2 * CONV_W] * uh_ref[:, 2 * CONV_W:3 * CONV_W])
        xin_prev = jnp.where(first_tile, 0.0, uh_ref[:, 3 * CONV_W:3 * CONV_W + LRU_W])
        hs_prev = jnp.where(first_tile, 0.0, hh_ref[...])

        def acc(first_row, val, width=LRU_W, row=0):
            r0 = first_row + row
            st_ref[r0:r0 + 1, 0:width] += val

        conv3 = c3_ref[...]
        y_conv = gate_b * conv3
        ra = lax.rsqrt(jnp.mean(y_conv * y_conv, axis=-1, keepdims=True) + EPS)
        xha = y_conv * ra
        dna = dy_ref[:, :CONV_W]
        acc(PK_G_NORM_CONV, _colsum(dna * xha), CONV_W)
        dxha = dna * gnc_ref[...]
        dy_conv = ra * (dxha - xha * jnp.mean(dxha * xha, axis=-1, keepdims=True))
        du_ref[:, 0:CONV_W] = (dy_conv * conv3).astype(BF16)
        dc = dy_conv * gate_b
        cw = cw_ref[...]
        dcn = dc_next[...]
        dcv = cw[2:3] * dc + cw[1:2] * _shift_up(dc, 1, dcn) + cw[0:1] * _shift_up(dc, 2, dcn)
        dc_next[...] = dc[:TILE_ROWS]
        acc(PK_CONV_W, _colsum(dc * _shift_down(cv, 2, cv_prev)), CONV_W, 0)
        acc(PK_CONV_W, _colsum(dc * _shift_down(cv, 1, cv_prev)), CONV_W, 1)
        acc(PK_CONV_W, _colsum(dc * cv), CONV_W, 2)
        du_ref[:, CONV_W:2 * CONV_W] = (dcv * v).astype(BF16)
        du_ref[:, 2 * CONV_W:3 * CONV_W] = (dcv * gate_c).astype(BF16)

        hs = hs_ref[...]
        gelu, dgelu = _gelu_and_grad(g)
        y_rnn = hs * gelu
        rb = lax.rsqrt(jnp.mean(y_rnn * y_rnn, axis=-1, keepdims=True) + EPS)
        xhb = y_rnn * rb
        dnb = dy_ref[:, CONV_W:]
        acc(PK_G_NORM_RNN, _colsum(dnb * xhb))
        dxhb = dnb * gnr_ref[...]
        dy_rnn = rb * (dxhb - xhb * jnp.mean(dxhb * xhb, axis=-1, keepdims=True))
        du_ref[:, 3 * CONV_W + LRU_W:] = (dy_rnn * hs * dgelu).astype(BF16)
        dh = dy_rnn * gelu

        xr = xr_ref[...]
        xrb = xr.astype(BF16)
        sp, dsp = _softplus_neg(lam_ref[...])
        r, ig, a, mult = [gates_ref[:, n * LRU_W:(n + 1) * LRU_W] for n in range(4)]
        a_up = _shift_up(a, 1, a_next[...])
        a_next[...] = a[:TILE_ROWS]
        gs = _scan_rows(a_up, dh, gs_next[0:1, :], reverse=True)
        gs_next[...] = gs[:TILE_ROWS]
        da = gs * _shift_down(hs, 1, hs_prev)
        gx = gs * xr
        di = gx * mult
        dmult = gx * ig
        dxr = gs * (mult * ig)
        dlog_a = da * a - dmult * ((a * a) / mult)
        acc(PK_LAMBDA, _colsum(dlog_a * r) * ((-LRU_C) * dsp))
        dpa = (dlog_a * ((-LRU_C) * sp)) * (r * (1.0 - r))
        dpx = di * (ig * (1.0 - ig))
        acc(PK_B_A, _colsum(dpa))
        acc(PK_B_X, _colsum(dpx))
        dpab = dpa.astype(BF16)
        dpxb = dpx.astype(BF16)
        dxr = dxr + _block_diag_dot_t(dpab, wa_ref) + _block_diag_dot_t(dpxb, wx_ref)
        for j in range(N_BD):
            cols = slice(j * BD, (j + 1) * BD)
            dwa_ref[j] += _dot_tn(xrb[:, cols], dpab[:, cols])
            dwx_ref[j] += _dot_tn(xrb[:, cols], dpxb[:, cols])

        acc(PK_RCONV_B, _colsum(dxr))
        rw = rw_ref[...]
        dxn = dxr_next[...]
        dx_r = (rw[3:4] * dxr + rw[2:3] * _shift_up(dxr, 1, dxn) + rw[1:2] * _shift_up(dxr, 2, dxn)
                + rw[0:1] * _shift_up(dxr, 3, dxn))
        dxr_next[...] = dxr[:TILE_ROWS]
        for k in range(3):
            acc(PK_RCONV_W, _colsum(dxr * _shift_down(x_r, 3 - k, xin_prev)), LRU_W, k)
        acc(PK_RCONV_W, _colsum(dxr * x_r), LRU_W, 3)
        du_ref[:, 3 * CONV_W:3 * CONV_W + LRU_W] = dx_r.astype(BF16)

        @pl.when(i == nt - 1)
        def _():
            for n, d_ref in enumerate((dwa_ref, dwx_ref)):
                for b in range(N_BD):
                    for q in range(BD // HEAD_DIM):
                        lane0 = q * HEAD_DIM // LANES * LANES
                        wide = d_ref[b, q * HEAD_DIM:(q + 1) * HEAD_DIM, lane0:lane0 + LANES]
                        if q * HEAD_DIM != lane0:
                            wide = pltpu.roll(wide, LANES - (q * HEAD_DIM - lane0), axis=1)
                        heads_ref[n, b * (BD // HEAD_DIM) + q] = wide[:, 0:HEAD_DIM].astype(BF16)
            exchange.wait()

    def full(a):
        nd = a.ndim
        return pl.BlockSpec(a.shape, lambda i: (0,) * nd)

    def tok(cols):
        return pl.BlockSpec((tm, cols), lambda i: (nt - 1 - i, 0))

    def halo(cols):
        return pl.BlockSpec((TILE_ROWS, cols), lambda i: (jnp.maximum((nt - 1 - i) * hb - 1, 0), 0))

    smalls = (conv_w, rconv_w, wa_bd, wx_bd, lam, g_nc, g_nr, st_mlp)
    st_rows = PK_MIX_ROWS + st_mlp.shape[0]
    heads = (2, N_HEADS, HEAD_DIM, HEAD_DIM)
    outs = pl.pallas_call(
        body, name="mix_bwd", grid=(nt,),
        in_specs=[tok(CONV_W + LRU_W), tok(IN_COLS), halo(IN_COLS), tok(LRU_W), tok(LRU_W), halo(LRU_W), tok(CONV_W)]
        + [tok(4 * LRU_W)] + [full(a) for a in smalls] + [ANY] * npart,
        out_specs=[tok(IN_COLS), pl.BlockSpec((st_rows, LRU_W), lambda i: (0, 0)),
                   pl.BlockSpec(heads, lambda i: (0, 0, 0, 0))]
        + [ANY] * npart,
        out_shape=[jax.ShapeDtypeStruct((t, IN_COLS), BF16), jax.ShapeDtypeStruct((st_rows, LRU_W), F32),
                   jax.ShapeDtypeStruct(heads, BF16)]
        + [jax.ShapeDtypeStruct(a.shape, a.dtype) for a in parts],
        scratch_shapes=[pltpu.VMEM((TILE_ROWS, CONV_W), F32), pltpu.VMEM((TILE_ROWS, LRU_W), F32),
                        pltpu.VMEM((TILE_ROWS, LRU_W), F32), pltpu.VMEM((TILE_ROWS, LRU_W), F32),
                        pltpu.VMEM((N_BD, BD, BD), F32), pltpu.VMEM((N_BD, BD, BD), F32),
                        pltpu.SemaphoreType.DMA((npart, 3)), pltpu.SemaphoreType.DMA((npart, 3))],
        compiler_params=_params(dimension_semantics=("arbitrary",)),
    )(dy, u, u, xr_all, hs_all, hs_all, c3_all, gates, *smalls, *parts)
    return outs[:3], outs[3:]


def _in_bwd(dub, w_in_g, x, dx2, g1, parts, joins, core_chip):
    t, d = x.shape
    tm = min(t, MATMUL_TOKEN_TILE)
    nt = t // tm
    npart = len(parts)
    nj = len(joins)
    geometry = []
    for tag, shape, _, _ in joins:
        pr, pc = WGRAD_GEOMETRY[tag][:2]
        every = 1 if pr % (nt * 16) == 0 else 2
        geometry.append((pr, pc, pr * every // nt, every, shape[1] == pc))

    def body(cc_ref, du_ref, win_ref, x_ref, dx2_ref, g1_ref, *rest):
        sums, rest = [rest[4 * w:4 * w + 4] for w in range(nj)], rest[4 * nj:]
        part_refs, (gx_ref, st_ref), rest = rest[:npart], rest[npart:npart + 2], rest[npart + 2:]
        arrived_refs, joined, rest = rest[:npart], rest[npart:npart + nj], rest[npart + nj:]
        stages, (send_sems, recv_sems, j_local, j_send, j_recv) = rest[:nj], rest[nj:]
        exchange = _PartialExchange(part_refs, arrived_refs, send_sems, recv_sems)
        i = pl.program_id(0)
        c = cc_ref[0]

        def window(w, core, row0, rows):
            pr, pc, _, _, by_rows = geometry[w]
            if by_rows:
                return joined[w].at[pl.ds(core * pr + row0, rows), :]
            return joined[w].at[pl.ds(row0, rows), pl.ds(core * pc, pc)]

        def to_sibling(w, src, core, row0, rows):
            return pltpu.make_async_remote_copy(src_ref=src, dst_ref=window(w, core, row0, rows), send_sem=j_send.at[w],
                                                recv_sem=j_recv.at[w], device_id=_sibling(), device_id_type=MESH)

        @pl.when(i == 0)
        def _():
            exchange.start()
            st_ref[...] = jnp.zeros_like(st_ref)

        for w in range(nj):
            pr, pc, rb, every, _ = geometry[w]

            @pl.when(i % every == 0)
            def _(w=w, rb=rb, every=every):
                p_ref, r1_ref, r2_ref, r3_ref = sums[w]
                row0 = pl.multiple_of((i // every) * rb, rb)
                rows = stages[w].at[pl.ds(row0, rb), :]
                rows[...] = ((p_ref[0] + r1_ref[0].astype(F32)) + r2_ref[0].astype(F32)) + r3_ref[0].astype(F32)
                pltpu.make_async_copy(rows, window(w, c, row0, rb), j_local.at[w]).start()
                to_sibling(w, rows, c, row0, rb).start()

        dh1 = _dot_nt(du_ref[:, 0:IN_SHARD], win_ref[0])
        for j in range(1, N_CHIPS):
            dh1 = dh1 + _dot_nt(du_ref[:, j * IN_SHARD:(j + 1) * IN_SHARD], win_ref[j])
        xv = x_ref[...]
        rstd = lax.rsqrt(jnp.mean(xv * xv, axis=-1, keepdims=True) + EPS)
        xh = xv * rstd
        st_ref[0:1, :] += _colsum(dh1 * xh)
        dxh = dh1 * g1_ref[...]
        gx_ref[...] = dx2_ref[...] + rstd * (dxh - xh * jnp.mean(dxh * xh, axis=-1, keepdims=True))

        @pl.when(i == nt - 1)
        def _():
            exchange.wait()
            for w in range(nj):
                pr = geometry[w][0]
                pltpu.make_async_copy(stages[w], window(w, c, 0, pr), j_local.at[w]).wait()
                to_sibling(w, stages[w], 1 - c, 0, pr).wait()

    def tok(cols):
        return pl.BlockSpec((tm, cols), lambda i, cc: (i, 0))

    def partial(w, off):
        pr, pc, rb, every, _ = geometry[w]
        return pl.BlockSpec((1, rb, pc), lambda i, cc: ((cc[1] + off) % N_CHIPS, i // every, 0))

    sum_specs, sum_operands = [], []
    for w, (_, _, own, arrived) in enumerate(joins):
        sum_specs += [partial(w, off) for off in range(N_CHIPS)]
        sum_operands += [own, arrived, arrived, arrived]
    dma = pltpu.SemaphoreType.DMA
    outs = pl.pallas_call(
        body, name="in_bwd",
        grid_spec=pltpu.PrefetchScalarGridSpec(
            num_scalar_prefetch=1, grid=(nt,),
            in_specs=[tok(IN_COLS), pl.BlockSpec(w_in_g.shape, lambda i, cc: (0, 0, 0)), tok(d), tok(d),
                      pl.BlockSpec((1, d), lambda i, cc: (0, 0))] + sum_specs + [ANY] * npart,
            out_specs=[tok(d), pl.BlockSpec((TILE_ROWS, d), lambda i, cc: (0, 0))] + [ANY] * (npart + nj),
            scratch_shapes=[pltpu.VMEM((g[0], g[1]), F32) for g in geometry]
            + [dma((npart, 3)), dma((npart, 3)), dma((nj,)), dma((nj,)), dma((nj,))]),
        out_shape=[jax.ShapeDtypeStruct((t, d), F32), jax.ShapeDtypeStruct((TILE_ROWS, d), F32)]
        + [jax.ShapeDtypeStruct(a.shape, a.dtype) for a in parts]
        + [jax.ShapeDtypeStruct(shape, F32) for _, shape, _, _ in joins],
        compiler_params=_params(dimension_semantics=("arbitrary",)),
    )(core_chip, dub, w_in_g, x, dx2, g1, *sum_operands, *parts)
    return outs[:2], outs[2:2 + npart], outs[2 + npart:]


WGRAD_GEOMETRY = {
    "in": (512, IN_SHARD, lambda s, h: h, lambda s, h: s),
    "mlp_in": (512, D_MODEL, lambda s, h: h, lambda s, h: s),
    "mlp_out": (512, D_MODEL, lambda s, h: 2 * s + h, lambda s, h: 0),
    "out": (384, 512, lambda s, h: s, lambda s, h: h),
}
K_CHUNK = 512
TOKEN_STREAMS = 4


def _sibling():
    x, y, c = _position()
    return (x, y, 1 - c)


def _wgrad(a, b, tag, core_chip, packs=(), parts=()):
    t = a.shape[0]
    pr, pc, a_blk, b_blk = WGRAD_GEOMETRY[tag]
    ns = TOKEN_STREAMS
    ts = t // ns
    kc = min(K_CHUNK, ts)
    mine = N_CHIPS
    riding = len(packs)
    npart = len(parts)
    assert not (riding and npart)

    def body(cc_ref, *rest):
        a_refs, b_refs, rest = rest[:ns], rest[ns:2 * ns], rest[2 * ns:]
        if riding:
            pack_refs, (land_ref, p_ref, pb_ref), rest = rest[:riding], rest[riding:riding + 3], rest[riding + 3:]
            all_refs, (stage, rbuf, send_sems, recv_sems, rsem), g_sems = rest[:riding], rest[riding:riding + 5], rest[riding + 5:]
            gathers = [_PackGather(pack_refs[n], all_refs[n], *g_sems[3 * n:3 * n + 3]) for n in range(riding)]
        elif npart:
            part_refs, (land_ref, p_ref, pb_ref), rest = rest[:npart], rest[npart:npart + 3], rest[npart + 3:]
            arrived_refs, (stage, rbuf, send_sems, recv_sems, rsem, x_send, x_recv) = rest[:npart], rest[npart:]
            exchange = _PartialExchange(part_refs, arrived_refs, x_send, x_recv)
        else:
            land_ref, p_ref, pb_ref, stage, rbuf, send_sems, recv_sems, rsem = rest
        ph, s = pl.program_id(0), pl.program_id(1)
        if riding:
            @pl.when((ph == 0) & (s == 0))
            def _():
                for gather in gathers:
                    gather.start()

            @pl.when((ph == 1) & (s == N_CHIPS - 2))
            def _():
                for gather in gathers:
                    gather.hand_over()
        if npart:
            @pl.when((ph == 0) & (s == 0))
            def _():
                exchange.start()
        def push(k):
            return pltpu.make_async_remote_copy(src_ref=stage.at[k], dst_ref=land_ref.at[k], send_sem=send_sems.at[k],
                                                recv_sem=recv_sems.at[k], device_id=_sibling(), device_id_type=MESH)

        def landed():
            return pltpu.make_async_copy(land_ref.at[s], rbuf, rsem)

        @pl.when(ph == 1)
        def _():
            push(s).wait_recv()
            landed().start()

        slot = jnp.where(ph == 0, s, mine)
        acc = stage.at[slot]
        chunks = [(a_ref, b_ref, k) for a_ref, b_ref in zip(a_refs, b_refs) for k in range(0, ts, kc)]
        for n, (a_ref, b_ref, k) in enumerate(chunks):
            part = _dot_tn(a_ref[k:k + kc, :], b_ref[k:k + kc, :])
            if n == 0:
                acc[...] = part
            else:
                acc[...] += part

        @pl.when(ph == 0)
        def _():
            push(s).start()

        @pl.when(ph == 1)
        def _():
            landed().wait()
            p = stage[mine] + rbuf[...]
            p_ref[0] = p
            pb_ref[0] = p.astype(BF16)

        @pl.when((ph == 1) & (s == N_CHIPS - 1))
        def _():
            for k in range(N_CHIPS):
                push(k).wait_send()
            for gather in (gathers if riding else ()):
                gather.finish()
            if npart:
                exchange.wait()

    def half(ph, cc):
        return jnp.where(ph == 0, 1 - cc[0], cc[0])

    def out_slot(ph, s, cc):
        return (jnp.where(ph == 0, 0, s), 0, 0)

    piece = jax.ShapeDtypeStruct((N_CHIPS, pr, pc), F32)
    in_specs = [pl.BlockSpec((ts, pr), lambda ph, s, cc, n=n: (n, a_blk(s, half(ph, cc)))) for n in range(ns)]
    in_specs += [pl.BlockSpec((ts, pc), lambda ph, s, cc, n=n: (n, b_blk(s, half(ph, cc)))) for n in range(ns)]
    out_specs = [ANY, pl.BlockSpec((1, pr, pc), out_slot), pl.BlockSpec((1, pr, pc), out_slot)]
    out_shape = [piece, piece, jax.ShapeDtypeStruct((N_CHIPS, pr, pc), BF16)]
    scratch = [pltpu.VMEM((N_CHIPS + 1, pr, pc), F32), pltpu.VMEM((pr, pc), F32),
               pltpu.SemaphoreType.DMA((N_CHIPS,)), pltpu.SemaphoreType.DMA((N_CHIPS,)), pltpu.SemaphoreType.DMA]
    operands = [a] * ns + [b] * ns
    for pack in packs:
        in_specs.append(pl.BlockSpec(pack.shape, lambda ph, s, cc, nd=pack.ndim: (0,) * nd))
        out_specs.append(ANY)
        out_shape.append(jax.ShapeDtypeStruct((N_DEVICES,) + pack.shape, pack.dtype))
        operands.append(pack)
    for pack in packs:
        scratch += _PackGather.semaphores()
    if npart:
        in_specs += [ANY] * npart
        out_specs += [ANY] * npart
        out_shape += [jax.ShapeDtypeStruct(p.shape, p.dtype) for p in parts]
        scratch += [pltpu.SemaphoreType.DMA((npart, 3)), pltpu.SemaphoreType.DMA((npart, 3))]
        operands += list(parts)
    return pl.pallas_call(
        body, name="wgrad_" + tag,
        grid_spec=pltpu.PrefetchScalarGridSpec(
            num_scalar_prefetch=1, grid=(2, N_CHIPS), in_specs=in_specs, out_specs=out_specs, scratch_shapes=scratch),
        out_shape=out_shape,
        compiler_params=_params(dimension_semantics=("arbitrary", "arbitrary")),
    )(core_chip, *operands)[1:]


def _other_chips(x, y):
    return [(1 - x, y), (x, 1 - y), (1 - x, 1 - y)]


class _ShardGather:
    PAIRS = 9

    def __init__(self, outs, send_sems, recv_sems):
        self.outs, self.send_sems, self.recv_sems = outs, send_sems, recv_sems
        x, y, c = _position()
        self.c, self.j = c, 2 * x + y
        self.sibling = (x, y, 1 - c)
        self.chips = _other_chips(x, y)

    def _chip(self, k):
        px, py = self.chips[k]
        return 2 * px + py

    def _half(self, w, chip, which):
        hr = self.outs[w].shape[1] // 2
        return self.outs[w].at[chip, pl.ds(which * hr, hr), :]

    def _quarter(self, w, chip, q):
        qr = self.outs[w].shape[1] // 4
        return self.outs[w].at[chip, pl.ds(self.c * 2 * qr + q * qr, qr), :]

    def _copy(self, ref, w, pair, to, src=None):
        return pltpu.make_async_remote_copy(src_ref=ref if src is None else src, dst_ref=ref, send_sem=self.send_sems.at[w, pair],
                                            recv_sem=self.recv_sems.at[w, pair], device_id=to, device_id_type=MESH)

    def direct(self, w, k, q, src=None):
        return self._copy(self._quarter(w, self.j, q), w, 2 * k + q, (*self.chips[k], self.c), src)

    def direct_landed(self, w, k, q):
        return self._copy(self._quarter(w, self._chip(k), q), w, 2 * k + q, (*self.chips[k], self.c))

    def pass_on(self, w, q):
        return self._copy(self._quarter(w, self._chip(q), q), w, 4 + q, (*self.chips[1 - q], self.c))

    def passed_landed(self, w, q):
        return self._copy(self._quarter(w, self._chip(2), q), w, 4 + q, (*self.chips[1 - q], self.c))

    def hand_over(self, w, k):
        return self._copy(self._half(w, self._chip(k), self.c), w, 6 + k, self.sibling)

    def handed(self, w, k):
        return self._copy(self._half(w, self._chip(k), 1 - self.c), w, 6 + k, self.sibling)

    def start_direct(self, w, src_half=None):
        qr = self.outs[w].shape[1] // 4
        for k, q in ((0, 0), (1, 1), (0, 1), (1, 0)):
            self.direct(w, k, q, None if src_half is None else src_half.at[pl.ds(q * qr, qr), :]).start()

    def start_pass_on(self, w):
        for q in (0, 1):
            self.direct_landed(w, q, q).wait_recv()
            self.pass_on(w, q).start()

    def start_hand_over(self, w, diagonal):
        if diagonal:
            for q in (0, 1):
                self.passed_landed(w, q).wait_recv()
            self.hand_over(w, 2).start()
        else:
            for k in (0, 1):
                self.direct_landed(w, k, 1 - k).wait_recv()
                self.hand_over(w, k).start()

    def finish(self, w):
        for k in range(3):
            self.handed(w, k).wait_recv()
            self.hand_over(w, k).wait_send()
        for q in (0, 1):
            self.pass_on(w, q).wait_send()
            for k in (0, 1):
                self.direct(w, k, q).wait_send()


def _gather_first(w_in, w_out, w1, w2, conv_w, rconv_w, w_a, w_x, x, g1):
    t, d = x.shape
    tn = min(t, MATMUL_TOKEN_TILE)
    n_tiles = t // tn
    bigs = (w_in, w_out, w1, w2)
    convs = (conv_w, rconv_w)
    heads = (w_a, w_x)
    nb, nc = len(bigs), len(convs)

    def body(win_ref, wout_hbm, w1_hbm, w2_hbm, cw_ref, rw_ref, wa_ref, wx_ref, x_hbm, g1_ref, gin, gout, g1, g2, gcw, grw,
             bda, bdx, h1_hbm, st_in, st_out, st_1, st_2, f_out, f_1, f_2, st_cw, st_rw, xbuf, hbuf, send_sems, recv_sems,
             sm_send, sm_recv, local_sems, load_sems, x_sems, h_sems):
        stages = (st_in, st_out, st_1, st_2)
        outs = (gin, gout, g1, g2)
        conv_stages, conv_outs = (st_cw, st_rw), (gcw, grw)
        plan = _ShardGather(outs[:1], send_sems, recv_sems)
        j, c = plan.j, plan.c
        local = [pltpu.make_async_copy(stages[w], outs[w].at[j], local_sems.at[w]) for w in range(nb)]
        loads = [pltpu.make_async_copy(src, dst, load_sems.at[n])
                 for n, (src, dst) in enumerate(((wout_hbm, f_out), (w1_hbm, f_1), (w2_hbm, f_2)))]

        def columns(n, chip):
            width = convs[n].shape[1]
            return conv_outs[n].at[:, pl.ds(chip * width, width)]

        def x_load(i):
            return pltpu.make_async_copy(x_hbm.at[pl.ds(i * tn, tn), :], xbuf.at[i % 2], x_sems.at[i % 2])

        def h1_store(i):
            return pltpu.make_async_copy(hbuf.at[i % 2], h1_hbm.at[pl.ds(i * tn, tn), :], h_sems.at[i % 2])

        def first_norm():
            x_load(0).start()
            for i in range(n_tiles):
                if i + 1 < n_tiles:
                    x_load(i + 1).start()
                x_load(i).wait()
                if i >= 2:
                    h1_store(i - 2).wait()
                xv = xbuf[i % 2]
                rstd = lax.rsqrt(jnp.mean(xv * xv, axis=-1, keepdims=True) + EPS)
                hbuf[i % 2] = ((xv * rstd) * g1_ref[...]).astype(BF16)
                h1_store(i).start()
            for i in range(max(n_tiles - 2, 0), n_tiles):
                h1_store(i).wait()

        local += [pltpu.make_async_copy(conv_stages[n], columns(n, j), local_sems.at[nb + n]) for n in range(nc)]

        def small_copy(k, n, landed=False):
            px, py = plan.chips[k]
            return pltpu.make_async_remote_copy(
                src_ref=conv_stages[n], dst_ref=columns(n, 2 * px + py if landed else j), send_sem=sm_send.at[k, n],
                recv_sem=sm_recv.at[k, n], device_id=(px, py, c), device_id_type=MESH)

        for cp in loads:
            cp.start()
        hr = w_in.shape[0] // 2
        st_in[...] = win_ref[...].astype(BF16)
        plan.start_direct(0, st_in.at[pl.ds(c * hr, hr), :])
        for src, st in zip((cw_ref, rw_ref), conv_stages):
            st[...] = jnp.zeros_like(st)
            st[0:src.shape[0], :] = src[...]
        for k in range(3):
            for n in range(nc):
                small_copy(k, n).start()
        for src, bd in ((wa_ref, bda), (wx_ref, bdx)):
            bd[...] = jnp.zeros_like(bd)
            for h in range(N_HEADS):
                q = h % (BD // HEAD_DIM)
                bd[h // (BD // HEAD_DIM), q * HEAD_DIM:(q + 1) * HEAD_DIM, q * HEAD_DIM:(q + 1) * HEAD_DIM] = src[h].astype(BF16)
        for cp, full, st in zip(loads, (f_out, f_1, f_2), stages[1:]):
            cp.wait()
            st[...] = full[...].astype(BF16)
        for cp in local:
            cp.start()
        plan.start_pass_on(0)
        first_norm()
        plan.start_hand_over(0, diagonal=False)
        plan.start_hand_over(0, diagonal=True)
        for k in range(3):
            for n in range(nc):
                small_copy(k, n, landed=True).wait_recv()
                small_copy(k, n).wait_send()
        plan.finish(0)
        for cp in local:
            cp.wait()

    def gathered(a, dtype):
        return jax.ShapeDtypeStruct((N_CHIPS,) + a.shape, dtype)

    return pl.pallas_call(
        body, name="gather_first",
        in_specs=[VMEM] + [ANY] * (nb - 1) + [VMEM] * (nc + len(heads)) + [ANY, VMEM],
        out_specs=[ANY] * (nb + nc) + [VMEM] * len(heads) + [ANY],
        out_shape=[gathered(a, BF16) for a in bigs]
        + [jax.ShapeDtypeStruct((TILE_ROWS, N_CHIPS * a.shape[1]), F32) for a in convs]
        + [jax.ShapeDtypeStruct((N_BD, BD, BD), BF16) for _ in heads] + [jax.ShapeDtypeStruct((t, d), BF16)],
        scratch_shapes=[pltpu.VMEM(a.shape, BF16) for a in bigs] + [pltpu.VMEM(a.shape, F32) for a in bigs[1:]]
        + [pltpu.VMEM((TILE_ROWS, a.shape[1]), F32) for a in convs]
        + [pltpu.VMEM((2, tn, d), F32), pltpu.VMEM((2, tn, d), BF16)]
        + [pltpu.SemaphoreType.DMA((1, _ShardGather.PAIRS)), pltpu.SemaphoreType.DMA((1, _ShardGather.PAIRS)),
           pltpu.SemaphoreType.DMA((3, nc)), pltpu.SemaphoreType.DMA((3, nc)), pltpu.SemaphoreType.DMA((nb + nc,)),
           pltpu.SemaphoreType.DMA((nb - 1,)), pltpu.SemaphoreType.DMA((2,)), pltpu.SemaphoreType.DMA((2,))],
        compiler_params=_params(),
    )(*bigs, *convs, *heads, x, g1)


class _PartialExchange:
    def __init__(self, parts, arrived, send_sems, recv_sems):
        self.parts, self.arrived, self.send_sems, self.recv_sems = parts, arrived, send_sems, recv_sems
        x, y, c = _position()
        self.c, self.j = c, 2 * x + y
        self.chips = _other_chips(x, y)

    def _copy(self, w, k, slot):
        px, py = self.chips[k]
        return pltpu.make_async_remote_copy(
            src_ref=self.parts[w].at[2 * px + py], dst_ref=self.arrived[w].at[slot], send_sem=self.send_sems.at[w, k],
            recv_sem=self.recv_sems.at[w, k], device_id=(px, py, self.c), device_id_type=MESH)

    def start(self):
        for w in range(len(self.parts)):
            for k in range(3):
                self._copy(w, k, self.j).start()

    def wait(self):
        for w in range(len(self.parts)):
            for k in range(3):
                px, py = self.chips[k]
                self._copy(w, k, 2 * px + py).wait()


class _PackGather:
    def __init__(self, p_ref, all_ref, send_sems, recv_sems, local_sem):
        self.p_ref, self.all_ref, self.send_sems, self.recv_sems, self.local_sem = p_ref, all_ref, send_sems, recv_sems, local_sem
        x, y, c = _position()
        self.me, self.sibling, self.c = (x, y, c), (x, y, 1 - c), c
        self.chips = _other_chips(x, y)

    @staticmethod
    def semaphores():
        return [pltpu.SemaphoreType.DMA((7,)), pltpu.SemaphoreType.DMA((7,)), pltpu.SemaphoreType.DMA]

    def _copy(self, k, block, to, from_pack=False):
        px, py, pc = block
        slot = self.all_ref.at[4 * px + 2 * py + pc]
        return pltpu.make_async_remote_copy(src_ref=self.p_ref if from_pack else slot, dst_ref=slot, send_sem=self.send_sems.at[k],
                                            recv_sem=self.recv_sems.at[k], device_id=to, device_id_type=MESH)

    def _mine(self):
        x, y, c = self.me
        return pltpu.make_async_copy(self.p_ref, self.all_ref.at[4 * x + 2 * y + c], self.local_sem)

    def _first(self):
        return [self._copy(0, self.me, self.sibling, True)] + [
            self._copy(1 + k, self.me, (*chip, self.c), True) for k, chip in enumerate(self.chips)]

    def _passed(self):
        return [self._copy(4 + k, (*chip, self.c), self.sibling) for k, chip in enumerate(self.chips)]

    def start(self):
        self._mine().start()
        for cp in self._first():
            cp.start()

    def hand_over(self):
        for k, chip in enumerate(self.chips):
            self._copy(1 + k, (*chip, self.c), self.me).wait_recv()
            self._passed()[k].start()

    def finish(self):
        self._copy(0, self.sibling, self.me).wait_recv()
        for k, chip in enumerate(self.chips):
            self._copy(4 + k, (*chip, 1 - self.c), self.me).wait_recv()
        for cp in self._first() + self._passed():
            cp.wait_send()
        self._mine().wait()


class _DirectGather:
    def __init__(self, p_ref, all_ref, send_sems, recv_sems, local_sem):
        self.p_ref, self.all_ref, self.send_sems, self.recv_sems, self.local_sem = p_ref, all_ref, send_sems, recv_sems, local_sem
        self.me = _position()

    semaphores = _PackGather.semaphores

    def _peer(self, r):
        x, y, c = self.me
        return ((1 - x) if r & 4 else x, (1 - y) if r & 2 else y, (1 - c) if r & 1 else c)

    def _copy(self, r, slot_of):
        px, py, pc = slot_of
        return pltpu.make_async_remote_copy(src_ref=self.p_ref, dst_ref=self.all_ref.at[4 * px + 2 * py + pc],
                                            send_sem=self.send_sems.at[r - 1], recv_sem=self.recv_sems.at[r - 1],
                                            device_id=self._peer(r), device_id_type=MESH)

    def _mine(self):
        x, y, c = self.me
        return pltpu.make_async_copy(self.p_ref, self.all_ref.at[4 * x + 2 * y + c], self.local_sem)

    def start(self):
        self._mine().start()
        for r in range(1, N_DEVICES):
            self._copy(r, self.me).start()

    def finish(self):
        for r in range(1, N_DEVICES):
            self._copy(r, self._peer(r)).wait()
        self._mine().wait()


def _adamw(w, g, m, v):
    m = ADAM_B1 * m + (1.0 - ADAM_B1) * g
    v = ADAM_B2 * v + (1.0 - ADAM_B2) * (g * g)
    m_hat = m / ADAM_BC1
    v_hat = v / ADAM_BC2
    delta = -ADAM_LR * (m_hat / (jnp.sqrt(v_hat) + ADAM_EPS) + ADAM_WD * w)
    return delta, m, v


JOIN_SUB = 4


def _join(tag, shard_shape, part, arrived, core_chip, block=None):
    pr, pc = WGRAD_GEOMETRY[tag][:2]
    rb = pr // JOIN_SUB
    by_rows = shard_shape[1] == pc
    riding = block is not None

    def body(cc_ref, p_ref, r1_ref, r2_ref, r3_ref, *rest):
        if riding:
            blk_ref, g_ref, all_ref, stage, send_sems, recv_sems, local_sems, b_send, b_recv, b_local = rest
            gather = _DirectGather(blk_ref, all_ref, b_send, b_recv, b_local)
        else:
            g_ref, stage, send_sems, recv_sems, local_sems = rest
        i = pl.program_id(0)
        c = cc_ref[0]
        if riding:
            @pl.when(i == 0)
            def _():
                gather.start()

        def window(core, k):
            if by_rows:
                return g_ref.at[pl.ds((core * JOIN_SUB + k) * rb, rb), :]
            return g_ref.at[pl.ds(k * rb, rb), pl.ds(core * pc, pc)]

        def keep(k):
            return pltpu.make_async_copy(stage.at[k], window(c, k), local_sems.at[k])

        def push(k):
            return pltpu.make_async_remote_copy(src_ref=stage.at[k], dst_ref=window(c, k), send_sem=send_sems.at[k],
                                                recv_sem=recv_sems.at[k], device_id=_sibling(), device_id_type=MESH)

        def pushed(k):
            return pltpu.make_async_remote_copy(src_ref=stage.at[k], dst_ref=window(1 - c, k), send_sem=send_sems.at[k],
                                                recv_sem=recv_sems.at[k], device_id=_sibling(), device_id_type=MESH)

        stage[i] = ((p_ref[0] + r1_ref[0].astype(F32)) + r2_ref[0].astype(F32)) + r3_ref[0].astype(F32)
        keep(i).start()
        push(i).start()

        @pl.when(i == JOIN_SUB - 1)
        def _():
            for k in range(JOIN_SUB):
                keep(k).wait()
                push(k).wait_send()
                pushed(k).wait_recv()
            if riding:
                gather.finish()

    def partial(off):
        return pl.BlockSpec((1, rb, pc), lambda i, cc: ((cc[1] + off) % N_CHIPS, i, 0))

    in_specs = [partial(0), partial(1), partial(2), partial(3)]
    out_specs = [ANY]
    out_shape = [jax.ShapeDtypeStruct(shard_shape, F32)]
    scratch = [pltpu.VMEM((JOIN_SUB, rb, pc), F32), pltpu.SemaphoreType.DMA((JOIN_SUB,)),
               pltpu.SemaphoreType.DMA((JOIN_SUB,)), pltpu.SemaphoreType.DMA((JOIN_SUB,))]
    operands = [part, arrived, arrived, arrived]
    if riding:
        in_specs.append(pl.BlockSpec(block.shape, lambda i, cc: (0, 0)))
        out_specs.append(ANY)
        out_shape.append(jax.ShapeDtypeStruct((N_DEVICES,) + block.shape, block.dtype))
        scratch += _DirectGather.semaphores()
        operands.append(block)
    outs = pl.pallas_call(
        body, name="join_" + tag,
        grid_spec=pltpu.PrefetchScalarGridSpec(
            num_scalar_prefetch=1, grid=(JOIN_SUB,), in_specs=in_specs, out_specs=out_specs, scratch_shapes=scratch),
        out_shape=out_shape,
        compiler_params=_params(dimension_semantics=("arbitrary",)),
    )(core_chip, *operands)
    return outs if riding else outs[0]


def _adamw_big(w, g, m, v, name):
    rows, cols = w.shape
    rb = ADAMW_ROWS if rows % ADAMW_ROWS == 0 else rows

    def body(w_ref, g_ref, m_ref, v_ref, go_ref, d_ref, nm_ref, nv_ref):
        g = g_ref[...]
        go_ref[...] = g
        d_ref[...], nm_ref[...], nv_ref[...] = _adamw(w_ref[...], g, m_ref[...], v_ref[...])

    spec = pl.BlockSpec((rb, cols), lambda i: (i, 0))
    return pl.pallas_call(
        body, name=name, grid=(rows // rb,), in_specs=[spec] * 4, out_specs=[spec] * 4,
        out_shape=[jax.ShapeDtypeStruct(w.shape, F32)] * 4,
        compiler_params=_params(dimension_semantics=("arbitrary",)),
    )(w, g, m, v)


SMALL_VECTORS = {
    "norm_mix_g": (PK_MIX_G, D_MODEL), "rnn_conv_b": (PK_RCONV_B, LRU_W), "b_a": (PK_B_A, LRU_W), "b_x": (PK_B_X, LRU_W),
    "lru_lambda": (PK_LAMBDA, LRU_W), "g_norm_conv": (PK_G_NORM_CONV, CONV_W), "g_norm_rnn": (PK_G_NORM_RNN, LRU_W),
    "norm_mlp_g": (PK_MLP_G, D_MODEL), "final_norm_g": (PK_FINAL_G, D_MODEL),
}
SMALL_MATRICES = ("w_a", "w_x")


def _small_step(vec_packs, mat_packs, mix_g_blocks, p):
    vec_rows, cols = vec_packs.shape[1:]
    conv_rows, cshard = p["conv_w"].shape
    rconv_rows, rshard = p["rnn_conv_w"].shape
    names = list(SMALL_VECTORS) + list(SMALL_MATRICES) + ["conv_w", "rnn_conv_w"]
    shapes = ([(1, width) for _, width in SMALL_VECTORS.values()] + [mat_packs.shape[2:]] * len(SMALL_MATRICES)
              + [(conv_rows, cshard), (rconv_rows, rshard)])
    kinds = ("", "m_", "v_")
    params = [p[pre + n].reshape(1, -1) if n in SMALL_VECTORS else p[pre + n] for pre in kinds for n in names]

    def body(vec_ref, mat_ref, blk_ref, *rest):
        wmv = [dict(zip(names, rest[k * len(names):(k + 1) * len(names)])) for k in range(3)]
        loss_ref, rest = rest[3 * len(names)], rest[3 * len(names) + 1:]
        leaves, (g_ref, w_ref, m_ref, v_ref) = [rest[k * len(names):(k + 1) * len(names)] for k in range(4)], rest[4 * len(names):]
        total = vec_ref[0]
        mats = mat_ref[0].astype(F32)
        late = blk_ref[0]
        for k in range(1, N_DEVICES):
            total = total + vec_ref[k]
            mats = mats + mat_ref[k].astype(F32)
            late = late + blk_ref[k]
        g_ref[0:vec_rows, :] = total
        g_ref[vec_rows:, :] = late
        g = g_ref[...]
        loss_ref[...] = g[PK_LOSS:PK_LOSS + 1, 0:1]

        for pack_ref, given in zip((w_ref, m_ref, v_ref), wmv):
            pack_ref[...] = jnp.zeros_like(pack_ref)
            for name, (row, width) in SMALL_VECTORS.items():
                pack_ref[row:row + 1, 0:width] = given[name][...]

        x, y, _ = _position()
        j = 2 * x + y
        cblk = total[0:TILE_ROWS, :]
        rblk = total[PK_RCONV_W:PK_RCONV_W + TILE_ROWS, :]
        cg = cblk[:, 0:cshard]
        rg = rblk[:, 0:rshard]
        for k in range(1, N_CHIPS):
            cg = jnp.where(j == k, cblk[:, k * cshard:(k + 1) * cshard], cg)
            rg = jnp.where(j == k, rblk[:, k * rshard:(k + 1) * rshard], rg)
        cg = cg[PK_CONV_W:PK_CONV_W + conv_rows, :]
        rg = rg[0:rconv_rows, :]

        def step(name, grad):
            return (grad,) + _adamw(wmv[0][name][...], grad, wmv[1][name][...], wmv[2][name][...])

        packs = (g,) + _adamw(w_ref[...], g, m_ref[...], v_ref[...])
        matrices = [step(name, mats[n]) for n, name in enumerate(SMALL_MATRICES)]
        convs, rconvs = step("conv_w", cg), step("rnn_conv_w", rg)
        for kind in range(4):
            out = dict(zip(names, leaves[kind]))
            for name, (row, width) in SMALL_VECTORS.items():
                out[name][...] = packs[kind][row:row + 1, 0:width]
            for n, name in enumerate(SMALL_MATRICES):
                out[name][...] = matrices[n][kind]
            out["conv_w"][...] = convs[kind]
            out["rnn_conv_w"][...] = rconvs[kind]

    outs = pl.pallas_call(
        body, name="small_grads_step", in_specs=[VMEM] * (3 + len(params)), out_specs=[VMEM] * (1 + 4 * len(names)),
        out_shape=[jax.ShapeDtypeStruct((1, 1), F32)] + [jax.ShapeDtypeStruct(sh, F32) for sh in shapes] * 4,
        scratch_shapes=[pltpu.VMEM((PK_ROWS, cols), F32)] * 4,
        compiler_params=_params(),
    )(vec_packs, mat_packs, mix_g_blocks, *params)
    return outs[0], [dict(zip(names, outs[1 + k * len(names):1 + (k + 1) * len(names)])) for k in range(4)]


_NAMES = ['norm_mix_g', 'w_in', 'conv_w', 'rnn_conv_w', 'rnn_conv_b', 'w_a', 'b_a', 'w_x', 'b_x', 'lru_lambda',
          'g_norm_conv', 'g_norm_rnn', 'w_out', 'norm_mlp_g', 'w_mlp_in', 'w_mlp_out', 'final_norm_g']


def kernel(x, norm_mix_g, w_in, conv_w, rnn_conv_w, rnn_conv_b, w_a, b_a, w_x, b_x, lru_lambda, g_norm_conv, g_norm_rnn, w_out, norm_mlp_g, w_mlp_in, w_mlp_out, final_norm_g, loss_target, m_norm_mix_g, m_w_in, m_conv_w, m_rnn_conv_w, m_rnn_conv_b, m_w_a, m_b_a, m_w_x, m_b_x, m_lru_lambda, m_g_norm_conv, m_g_norm_rnn, m_w_out, m_norm_mlp_g, m_w_mlp_in, m_w_mlp_out, m_final_norm_g, v_norm_mix_g, v_w_in, v_conv_w, v_rnn_conv_w, v_rnn_conv_b, v_w_a, v_b_a, v_w_x, v_b_x, v_lru_lambda, v_g_norm_conv, v_g_norm_rnn, v_w_out, v_norm_mlp_g, v_w_mlp_in, v_w_mlp_out, v_final_norm_g):
    args = dict(locals())
    p = {}
    for n in _NAMES:
        for pre in ("", "m_", "v_"):
            a = args[pre + n]
            p[pre + n] = a[0] if a.ndim >= 3 else a
    xs = x[0]
    target = loss_target[0]
    core_chip = jnp.stack([lax.axis_index("c"), 2 * lax.axis_index("x") + lax.axis_index("y")]).astype(jnp.int32)

    w_in_g, w_out_g, w1_g, w2_g, conv_full, rconv_full, wa_bd, wx_bd, h1b = _gather_first(
        p["w_in"], p["w_out"], p["w_mlp_in"], p["w_mlp_out"], p["conv_w"], p["rnn_conv_w"], p["w_a"], p["w_x"],
        xs, p["norm_mix_g"])
    gf = p["final_norm_g"].reshape(1, -1)
    lru = (wa_bd, p["b_a"], wx_bd, p["b_x"], p["lru_lambda"], p["g_norm_conv"], p["g_norm_rnn"])

    (u, xr, hs, c3, yb, gates), (w_out_g, w1_g, w2_g) = _fwd_mix(
        h1b, w_in_g, conv_full, rconv_full, p["rnn_conv_b"], *lru, (w_out_g, w1_g, w2_g))
    zb, dpb, h2b, dx3b, dx2, dx2b, dy, st_mlp = _mlp_fwd_bwd(
        xs, yb, w_out_g.reshape(-1, D_MODEL), w1_g, w2_g.reshape(-1, D_MODEL), p["norm_mlp_g"], gf, target)

    part_out = _wgrad(yb, dx2b, "out", core_chip)
    *part_1, arrived_out = _wgrad(h2b, dpb, "mlp_in", core_chip, parts=(part_out[1],))
    part_2 = _wgrad(zb, dx3b, "mlp_out", core_chip)
    (dub, vec_pack, mat_pack), (arrived_1, arrived_2) = _mix_bwd(
        dy, u, xr, hs, c3, gates, conv_full, rconv_full, wa_bd, wx_bd, p["lru_lambda"], p["g_norm_conv"], p["g_norm_rnn"],
        st_mlp, (part_1[1], part_2[1]))
    arrived_mlp = (arrived_out, arrived_1, arrived_2)
    *part_in, vec_packs, mat_packs = _wgrad(h1b, dub, "in", core_chip, packs=(vec_pack, mat_pack))
    early = (("w_out", "out", part_out, arrived_mlp[0]), ("w_mlp_in", "mlp_in", part_1, arrived_mlp[1]),
             ("w_mlp_out", "mlp_out", part_2, arrived_mlp[2]))
    (grad_x, st_in), arrived_in, joined = _in_bwd(
        dub, w_in_g, xs, dx2, p["norm_mix_g"], (part_in[1],),
        [(tag, p[n].shape, part[0], arrived) for n, tag, part, arrived in early], core_chip)
    g_in, mix_g_blocks = _join("in", p["w_in"].shape, part_in[0], arrived_in[0], core_chip, st_in)
    big = {}
    for n, tag, g in [(n, tag, g) for (n, tag, _, _), g in zip(early, joined)] + [("w_in", "in", g_in)]:
        big[n] = _adamw_big(p[n], g, p["m_" + n], p["v_" + n], "adamw_" + tag)

    loss, outs = _small_step(vec_packs, mat_packs, mix_g_blocks, p)
    for kind, o in enumerate(outs):
        o["final_norm_g"] = o["final_norm_g"].reshape(-1)
        for n in SMALL_MATRICES + ("conv_w", "rnn_conv_w"):
            o[n] = o[n][None]
        for n in ("w_in", "w_out", "w_mlp_in", "w_mlp_out"):
            o[n] = big[n][kind][None]
    loss = loss.reshape(())
    return (loss, grad_x[None], *[o[n] for o in outs for n in _NAMES])
```

```python
import functools
import math

import jax
import jax.numpy as jnp
from jax import lax
from jax.experimental import pallas as pl
from jax.experimental.pallas import tpu as pltpu

F32 = jnp.float32
BF16 = jnp.bfloat16
MESH = pl.DeviceIdType.MESH
ANY = pl.BlockSpec(memory_space=pl.ANY)
VMEM = pl.BlockSpec(memory_space=pltpu.VMEM)

EPS = 1e-6
LRU_C = 8.0
D_MODEL = 1024
CONV_W = 512
LRU_W = 1024
IN_COLS = 3 * CONV_W + 2 * LRU_W
IN_SHARD = IN_COLS // 4
N_CHIPS = 4
N_DEVICES = 8
BD = 256
N_BD = LRU_W // BD

ADAM_LR = 0.001
ADAM_B1 = 0.9
ADAM_B2 = 0.999
ADAM_EPS = 1e-08
ADAM_WD = 0.01
ADAM_STEP = 10
ADAM_BC1 = 1.0 - ADAM_B1 ** ADAM_STEP
ADAM_BC2 = 1.0 - ADAM_B2 ** ADAM_STEP

TILE_ROWS, LANES = 8, 128
TOKEN_TILE = 256
RING = 3
MATMUL_TOKEN_TILE = 512
ADAMW_ROWS = 256
VMEM_LIMIT = 56 * 1024 * 1024

PK_G_NORM_RNN, PK_RCONV_B, PK_B_A, PK_B_X, PK_LAMBDA, PK_CONV_W = 0, 1, 2, 3, 4, 5
PK_RCONV_W, PK_G_NORM_CONV = 8, 12
PK_MIX_ROWS = 16
PK_FINAL_G, PK_MLP_G, PK_LOSS = 16, 17, 18
PK_MLP_ROWS = 8
PK_MIX_G = 24
PK_ROWS = 32
N_HEADS, HEAD_DIM = 16, 64


def _params(**kw):
    return pltpu.CompilerParams(vmem_limit_bytes=VMEM_LIMIT, **kw)


def _position():
    x, y, c = lax.axis_index("x"), lax.axis_index("y"), lax.axis_index("c")
    return x, y, c


def _sigmoid(v):
    return 1.0 / (1.0 + jnp.exp(-v))


def _one_minus_square(log_a, a):
    v = 2.0 * log_a
    series = -v * (1.0 + v * (0.5 + v * (1.0 / 6.0)))
    return jnp.where(v > -0.01, series, 1.0 - a * a)


_GELU_C = math.sqrt(2.0 / math.pi)
_GELU_K = 0.044715


def _gelu_and_grad(g):
    th = jnp.tanh(_GELU_C * (g + _GELU_K * g * g * g))
    gelu = 0.5 * g * (1.0 + th)
    dgelu = 0.5 * (1.0 + th) + 0.5 * g * (1.0 - th * th) * (_GELU_C * (1.0 + 3.0 * _GELU_K * g * g))
    return gelu, dgelu


def _rows(shape):
    return lax.broadcasted_iota(jnp.int32, shape, 0)


def _shift_down(v, k, prev8):
    rolled = pltpu.roll(v, k, 0)
    halo = pltpu.roll(prev8, k, 0)
    head = jnp.where(_rows(halo.shape) < k, halo, rolled[:TILE_ROWS])
    return jnp.concatenate([head, rolled[TILE_ROWS:]], axis=0)


def _shift_up(v, k, next8):
    n = v.shape[0]
    rolled = pltpu.roll(v, n - k, 0)
    halo = pltpu.roll(next8, TILE_ROWS - k, 0)
    tail = jnp.where(_rows(halo.shape) >= TILE_ROWS - k, halo, rolled[n - TILE_ROWS:])
    return jnp.concatenate([rolled[: n - TILE_ROWS], tail], axis=0)


def _scan_rows(a, b, carry, reverse=False):
    n, w = a.shape
    groups = n // TILE_ROWS
    a3 = a.reshape(groups, TILE_ROWS, w)
    b3 = b.reshape(groups, TILE_ROWS, w)
    sub = lax.broadcasted_iota(jnp.int32, a3.shape, 1)
    s = 1
    while s < TILE_ROWS:
        shift = TILE_ROWS - s if reverse else s
        keep = (sub < TILE_ROWS - s) if reverse else (sub >= s)
        b3 = b3 + jnp.where(keep, a3 * pltpu.roll(b3, shift, 1), 0.0)
        a3 = a3 * jnp.where(keep, pltpu.roll(a3, shift, 1), 1.0)
        s *= 2
    out = [None] * groups
    edge = 0 if reverse else TILE_ROWS - 1
    for g in (range(groups - 1, -1, -1) if reverse else range(groups)):
        out[g] = b3[g] + a3[g] * carry
        carry = out[g][edge:edge + 1]
    return jnp.concatenate(out, axis=0)


def _softplus_neg(lam):
    e = jnp.exp(-jnp.abs(lam))
    log1p_e = jnp.where(e < 1e-2, e * (1.0 - e * (0.5 - e * (1.0 / 3.0 - e * 0.25))), jnp.log(1.0 + e))
    sp = jnp.maximum(-lam, 0.0) + log1p_e
    dsp = -_sigmoid(-lam)
    return sp, dsp


def _block_diag_dot(vb, w_ref):
    return jnp.concatenate(
        [jnp.dot(vb[:, j * BD:(j + 1) * BD], w_ref[j], preferred_element_type=F32) for j in range(N_BD)], axis=1)


def _block_diag_dot_t(vb, w_ref):
    return jnp.concatenate(
        [lax.dot_general(vb[:, j * BD:(j + 1) * BD], w_ref[j], (((1,), (1,)), ((), ())), preferred_element_type=F32)
         for j in range(N_BD)], axis=1)


def _dot_nt(a, b):
    return lax.dot_general(a, b, (((1,), (1,)), ((), ())), preferred_element_type=F32)


def _dot_tn(a, b):
    return lax.dot_general(a, b, (((0,), (0,)), ((), ())), preferred_element_type=F32)


def _lru_gates(xr, wa_ref, ba, wx_ref, bx, sp):
    xrb = xr.astype(BF16)
    r = _sigmoid(_block_diag_dot(xrb, wa_ref) + ba)
    ig = _sigmoid(_block_diag_dot(xrb, wx_ref) + bx)
    log_a = (-LRU_C) * r * sp
    a = jnp.exp(log_a)
    mult = jnp.sqrt(_one_minus_square(log_a, a))
    return r, ig, a, mult


def _colsum(v):
    return jnp.sum(v, axis=0, keepdims=True)


N_FWD_OUT = 6


def _fwd_mix(h1b, w_in_g, conv_w, rconv_w, rconv_b, wa_bd, b_a, wx_bd, b_x, lam, g_nc, g_nr, later):
    t, d = h1b.shape
    tm = TOKEN_TILE
    nt = t // tm
    nl = len(later)
    assert nl == 3
    pass_on_at = [nt * f // 16 for f in (3, 5, 9)]
    neighbours_at = [nt * f // 16 for f in (10, 11, 12)]
    diagonal_at = [nt * f // 16 for f in (13, 14, 14)]

    def body(h1_ref, win_ref, cw_ref, rw_ref, rb_ref, wa_ref, ba_ref, wx_ref, bx_ref, lam_ref, gnc_ref, gnr_ref, *rest):
        later_in, outs, rest = rest[:nl], rest[nl:nl + N_FWD_OUT], rest[nl + N_FWD_OUT:]
        u_ref, xr_ref, hs_ref, c3_ref, y_ref, gates_ref = outs
        later_out, (cv_prev, xin_prev, h_prev, send_sems, recv_sems) = rest[:nl], rest[nl:]
        del later_in
        step = pl.program_id(0)
        plan = _ShardGather(later_out, send_sems, recv_sems)

        @pl.when(step == 0)
        def _():
            cv_prev[...] = jnp.zeros_like(cv_prev)
            xin_prev[...] = jnp.zeros_like(xin_prev)
            h_prev[...] = jnp.zeros_like(h_prev)
            for w in range(nl):
                plan.start_direct(w)

        for w in range(nl):
            @pl.when(step == pass_on_at[w])
            def _(w=w):
                plan.start_pass_on(w)

            @pl.when(step == neighbours_at[w])
            def _(w=w):
                plan.start_hand_over(w, diagonal=False)

            @pl.when(step == diagonal_at[w])
            def _(w=w):
                plan.start_hand_over(w, diagonal=True)

        h1b = h1_ref[...]
        for j in range(N_CHIPS):
            u_ref[:, j * IN_SHARD:(j + 1) * IN_SHARD] = jnp.dot(h1b, win_ref[j], preferred_element_type=F32)
        gate_b = u_ref[:, 0:CONV_W]
        cv = u_ref[:, CONV_W:2 * CONV_W] * u_ref[:, 2 * CONV_W:3 * CONV_W]
        x_r = u_ref[:, 3 * CONV_W:3 * CONV_W + LRU_W]
        g = u_ref[:, 3 * CONV_W + LRU_W:]

        cw = cw_ref[...]
        cvp = cv_prev[...]
        conv3 = cw[0:1] * _shift_down(cv, 2, cvp) + cw[1:2] * _shift_down(cv, 1, cvp) + cw[2:3] * cv
        cv_prev[...] = cv[tm - TILE_ROWS:]
        c3_ref[...] = conv3
        y_conv = gate_b * conv3

        rw = rw_ref[...]
        xp = xin_prev[...]
        xr = (rw[0:1] * _shift_down(x_r, 3, xp) + rw[1:2] * _shift_down(x_r, 2, xp)
              + rw[2:3] * _shift_down(x_r, 1, xp) + rw[3:4] * x_r) + rb_ref[...]
        xin_prev[...] = x_r[tm - TILE_ROWS:]
        xr_ref[...] = xr
        sp, _ = _softplus_neg(lam_ref[...])
        r, ig, a, mult = _lru_gates(xr, wa_ref, ba_ref[...], wx_ref, bx_ref[...], sp)
        for n, gate in enumerate((r, ig, a, mult)):
            gates_ref[:, n * LRU_W:(n + 1) * LRU_W] = gate
        h = _scan_rows(a, mult * (ig * xr), h_prev[...])
        h_prev[...] = h[tm - 1:tm]
        hs_ref[...] = h
        gelu, _ = _gelu_and_grad(g)
        y_rnn = h * gelu

        na = y_conv * lax.rsqrt(jnp.mean(y_conv * y_conv, axis=-1, keepdims=True) + EPS) * gnc_ref[...]
        nb = y_rnn * lax.rsqrt(jnp.mean(y_rnn * y_rnn, axis=-1, keepdims=True) + EPS) * gnr_ref[...]
        y_ref[:, :CONV_W] = na.astype(BF16)
        y_ref[:, CONV_W:] = nb.astype(BF16)

        @pl.when(step == nt - 1)
        def _():
            for w in range(nl):
                plan.finish(w)

    def full(a):
        nd = a.ndim
        return pl.BlockSpec(a.shape, lambda i: (0,) * nd)

    def tok(cols):
        return pl.BlockSpec((tm, cols), lambda i: (i, 0))

    def act(cols, dtype=F32):
        return jax.ShapeDtypeStruct((t, cols), dtype)

    smalls = (w_in_g, conv_w, rconv_w, rconv_b, wa_bd, b_a, wx_bd, b_x, lam, g_nc, g_nr)
    n_in = 1 + len(smalls)
    outs = pl.pallas_call(
        body, name="fwd_mix", grid=(nt,),
        in_specs=[tok(d)] + [full(a) for a in smalls] + [ANY] * nl,
        out_specs=[tok(IN_COLS), tok(LRU_W), tok(LRU_W), tok(CONV_W), tok(CONV_W + LRU_W)]
        + [tok(4 * LRU_W)] + [ANY] * nl,
        out_shape=[act(IN_COLS), act(LRU_W), act(LRU_W), act(CONV_W), act(CONV_W + LRU_W, BF16)]
        + [act(4 * LRU_W)] + [jax.ShapeDtypeStruct(a.shape, a.dtype) for a in later],
        input_output_aliases={n_in + w: N_FWD_OUT + w for w in range(nl)},
        scratch_shapes=[pltpu.VMEM((TILE_ROWS, CONV_W), F32), pltpu.VMEM((TILE_ROWS, LRU_W), F32),
                        pltpu.VMEM((1, LRU_W), F32), pltpu.SemaphoreType.DMA((nl, _ShardGather.PAIRS)),
                        pltpu.SemaphoreType.DMA((nl, _ShardGather.PAIRS))],
        compiler_params=_params(dimension_semantics=("arbitrary",)),
    )(h1b, *smalls, *later)
    return outs[:N_FWD_OUT], outs[N_FWD_OUT:]


def _mlp_fwd_bwd(x, yb, w_out_g, w1_g, w2_g, g2, gf, target):
    t, d = x.shape
    tm = TOKEN_TILE
    ff = w2_g.shape[0]
    mix = w_out_g.shape[0]
    ffs = ff // N_CHIPS

    def body(x_ref, y_ref, g2_ref, gf_ref, tgt_ref, wout_hbm, w1_hbm, w2_hbm,
             z_ref, dp_ref, h2_ref, dx3b_ref, dx2_ref, dx2b_ref, dy_ref, st_ref, wout, w1, w2, p_ref):
        @pl.when(pl.program_id(0) == 0)
        def _():
            pltpu.sync_copy(wout_hbm, wout)
            pltpu.sync_copy(w1_hbm, w1)
            pltpu.sync_copy(w2_hbm, w2)
            st_ref[...] = jnp.zeros_like(st_ref)

        x2 = x_ref[...] + jnp.dot(y_ref[...], wout[...], preferred_element_type=F32)
        r2 = lax.rsqrt(jnp.mean(x2 * x2, axis=-1, keepdims=True) + EPS)
        xh2 = x2 * r2
        g2v = g2_ref[...]
        h2b = (xh2 * g2v).astype(BF16)
        h2_ref[...] = h2b
        for j in range(N_CHIPS):
            p_ref[:, j * ffs:(j + 1) * ffs] = jnp.dot(h2b, w1[j], preferred_element_type=F32)
        rp = jnp.maximum(p_ref[...], 0.0)
        zb = (rp * rp).astype(BF16)
        z_ref[...] = zb
        x3 = x2 + jnp.dot(zb, w2[...], preferred_element_type=F32)
        r3 = lax.rsqrt(jnp.mean(x3 * x3, axis=-1, keepdims=True) + EPS)
        xh3 = x3 * r3
        gfv = gf_ref[...]
        err = xh3 * gfv - tgt_ref[...]
        loss = (0.5 / d) * jnp.sum(err * err)
        dout = err * (1.0 / d)
        st_ref[PK_FINAL_G - PK_MIX_ROWS:PK_FINAL_G - PK_MIX_ROWS + 1, :] += _colsum(dout * xh3)
        st_ref[PK_LOSS - PK_MIX_ROWS:PK_LOSS - PK_MIX_ROWS + 1, :] += jnp.zeros((1, d), F32) + loss
        dxh3 = dout * gfv
        dx3 = r3 * (dxh3 - xh3 * jnp.mean(dxh3 * xh3, axis=-1, keepdims=True))
        dx3b = dx3.astype(BF16)
        dx3b_ref[...] = dx3b
        dpb = (_dot_nt(dx3b, w2[...]) * (2.0 * rp)).astype(BF16)
        dp_ref[...] = dpb
        dh2 = _dot_nt(dpb[:, 0:ffs], w1[0])
        for j in range(1, N_CHIPS):
            dh2 = dh2 + _dot_nt(dpb[:, j * ffs:(j + 1) * ffs], w1[j])
        st_ref[PK_MLP_G - PK_MIX_ROWS:PK_MLP_G - PK_MIX_ROWS + 1, :] += _colsum(dh2 * xh2)
        dxh2 = dh2 * g2v
        dx2 = dx3 + r2 * (dxh2 - xh2 * jnp.mean(dxh2 * xh2, axis=-1, keepdims=True))
        dx2_ref[...] = dx2
        dx2b = dx2.astype(BF16)
        dx2b_ref[...] = dx2b
        dy_ref[...] = _dot_nt(dx2b, wout[...])

    def tok(cols):
        return pl.BlockSpec((tm, cols), lambda i: (i, 0))

    def row(cols):
        return pl.BlockSpec((1, cols), lambda i: (0, 0))

    return pl.pallas_call(
        body, name="mlp_fwd_bwd", grid=(t // tm,),
        in_specs=[tok(d), tok(mix), row(d), row(d), tok(d), ANY, ANY, ANY],
        out_specs=[tok(ff), tok(ff), tok(d), tok(d), tok(d), tok(d), tok(mix),
                   pl.BlockSpec((PK_MLP_ROWS, d), lambda i: (0, 0))],
        out_shape=[jax.ShapeDtypeStruct((t, ff), BF16), jax.ShapeDtypeStruct((t, ff), BF16),
                   jax.ShapeDtypeStruct((t, d), BF16), jax.ShapeDtypeStruct((t, d), BF16),
                   jax.ShapeDtypeStruct((t, d), F32), jax.ShapeDtypeStruct((t, d), BF16),
                   jax.ShapeDtypeStruct((t, mix), F32), jax.ShapeDtypeStruct((PK_MLP_ROWS, d), F32)],
        scratch_shapes=[pltpu.VMEM(w_out_g.shape, BF16), pltpu.VMEM(w1_g.shape, BF16), pltpu.VMEM(w2_g.shape, BF16),
                        pltpu.VMEM((tm, ff), F32)],
        compiler_params=_params(dimension_semantics=("arbitrary",)),
    )(x, yb, g2, gf, target, w_out_g, w1_g, w2_g)


def _mix_bwd(dy, u, xr_all, hs_all, c3_all, gates, conv_w, rconv_w, wa_bd, wx_bd, lam, g_nc, g_nr, st_mlp, parts):
    t = dy.shape[0]
    tm = TOKEN_TILE
    nt = t // tm
    hb = tm // TILE_ROWS
    npart = len(parts)

    def body(dy_ref, u_hbm, uh_ref, xr_ref, hs_ref, hh_ref, c3_ref, gates_hbm,
             cw_ref, rw_ref, wa_ref, wx_ref, lam_ref, gnc_ref, gnr_ref, stm_ref, *rest):
        part_refs, (du_ref, st_ref, heads_ref), rest = rest[:npart], rest[npart:npart + 3], rest[npart + 3:]
        arrived_refs, rest = rest[:npart], rest[npart:]
        dc_next, a_next, gs_next, dxr_next, dwa_ref, dwx_ref, ubuf, gbuf, ring_sems, send_sems, recv_sems = rest
        exchange = _PartialExchange(part_refs, arrived_refs, send_sems, recv_sems)
        i = pl.program_id(0)

        def ring(step):
            slot = step % RING
            rows = pl.ds(pl.multiple_of((nt - 1 - step) * tm, tm), tm)
            return [pltpu.make_async_copy(u_hbm.at[rows, :], ubuf.at[slot], ring_sems.at[0, slot]),
                    pltpu.make_async_copy(gates_hbm.at[rows, :], gbuf.at[slot], ring_sems.at[1, slot])]

        @pl.when(i == 0)
        def _():
            for step in range(min(RING - 1, nt)):
                for cp in ring(step):
                    cp.start()

        @pl.when(i + RING - 1 < nt)
        def _():
            for cp in ring(i + RING - 1):
                cp.start()

        for cp in ring(i):
            cp.wait()
        u_ref, gates_ref = ubuf.at[i % RING], gbuf.at[i % RING]

        @pl.when(i == 0)
        def _():
            exchange.start()
            dc_next[...] = jnp.zeros_like(dc_next)
            a_next[...] = jnp.zeros_like(a_next)
            gs_next[...] = jnp.zeros_like(gs_next)
            dxr_next[...] = jnp.zeros_like(dxr_next)
            st_ref[0:PK_MIX_ROWS, :] = jnp.zeros((PK_MIX_ROWS, LRU_W), F32)
            st_ref[PK_MIX_ROWS:, :] = stm_ref[...]
            dwa_ref[...] = jnp.zeros_like(dwa_ref)
            dwx_ref[...] = jnp.zeros_like(dwx_ref)

        first_tile = i == nt - 1
        gate_b = u_ref[:, 0:CONV_W]
        gate_c = u_ref[:, CONV_W:2 * CONV_W]
        v = u_ref[:, 2 * CONV_W:3 * CONV_W]
        x_r = u_ref[:, 3 * CONV_W:3 * CONV_W + LRU_W]
        g = u_ref[:, 3 * CONV_W + LRU_W:]
        cv = gate_c * v
        cv_prev = jnp.where(first_tile, 0.0, uh_ref[:, CONV_W:2 * CONV_W] * uh_ref[:, 2 * CONV_W:3 * CONV_W])
        xin_prev = jnp.where(first_tile, 0.0, uh_ref[:, 3 * CONV_W:3 * CONV_W + LRU_W])
        hs_prev = jnp.where(first_tile, 0.0, hh_ref[...])

        def acc(first_row, val, width=LRU_W, row=0):
            r0 = first_row + row
            st_ref[r0:r0 + 1, 0:width] += val

        conv3 = c3_ref[...]
        y_conv = gate_b * conv3
        ra = lax.rsqrt(jnp.mean(y_conv * y_conv, axis=-1, keepdims=True) + EPS)
        xha = y_conv * ra
        dna = dy_ref[:, :CONV_W]
        acc(PK_G_NORM_CONV, _colsum(dna * xha), CONV_W)
        dxha = dna * gnc_ref[...]
        dy_conv = ra * (dxha - xha * jnp.mean(dxha * xha, axis=-1, keepdims=True))
        du_ref[:, 0:CONV_W] = (dy_conv * conv3).astype(BF16)
        dc = dy_conv * gate_b
        cw = cw_ref[...]
        dcn = dc_next[...]
        dcv = cw[2:3] * dc + cw[1:2] * _shift_up(dc, 1, dcn) + cw[0:1] * _shift_up(dc, 2, dcn)
        dc_next[...] = dc[:TILE_ROWS]
        acc(PK_CONV_W, _colsum(dc * _shift_down(cv, 2, cv_prev)), CONV_W, 0)
        acc(PK_CONV_W, _colsum(dc * _shift_down(cv, 1, cv_prev)), CONV_W, 1)
        acc(PK_CONV_W, _colsum(dc * cv), CONV_W, 2)
        du_ref[:, CONV_W:2 * CONV_W] = (dcv * v).astype(BF16)
        du_ref[:, 2 * CONV_W:3 * CONV_W] = (dcv * gate_c).astype(BF16)

        hs = hs_ref[...]
        gelu, dgelu = _gelu_and_grad(g)
        y_rnn = hs * gelu
        rb = lax.rsqrt(jnp.mean(y_rnn * y_rnn, axis=-1, keepdims=True) + EPS)
        xhb = y_rnn * rb
        dnb = dy_ref[:, CONV_W:]
        acc(PK_G_NORM_RNN, _colsum(dnb * xhb))
        dxhb = dnb * gnr_ref[...]
        dy_rnn = rb * (dxhb - xhb * jnp.mean(dxhb * xhb, axis=-1, keepdims=True))
        du_ref[:, 3 * CONV_W + LRU_W:] = (dy_rnn * hs * dgelu).astype(BF16)
        dh = dy_rnn * gelu

        xr = xr_ref[...]
        xrb = xr.astype(BF16)
        sp, dsp = _softplus_neg(lam_ref[...])
        r, ig, a, mult = [gates_ref[:, n * LRU_W:(n + 1) * LRU_W] for n in range(4)]
        a_up = _shift_up(a, 1, a_next[...])
        a_next[...] = a[:TILE_ROWS]
        gs = _scan_rows(a_up, dh, gs_next[0:1, :], reverse=True)
        gs_next[...] = gs[:TILE_ROWS]
        da = gs * _shift_down(hs, 1, hs_prev)
        gx = gs * xr
        di = gx * mult
        dmult = gx * ig
        dxr = gs * (mult * ig)
        dlog_a = da * a - dmult * ((a * a) / mult)
        acc(PK_LAMBDA, _colsum(dlog_a * r) * ((-LRU_C) * dsp))
        dpa = (dlog_a * ((-LRU_C) * sp)) * (r * (1.0 - r))
        dpx = di * (ig * (1.0 - ig))
        acc(PK_B_A, _colsum(dpa))
        acc(PK_B_X, _colsum(dpx))
        dpab = dpa.astype(BF16)
        dpxb = dpx.astype(BF16)
        dxr = dxr + _block_diag_dot_t(dpab, wa_ref) + _block_diag_dot_t(dpxb, wx_ref)
        for j in range(N_BD):
            cols = slice(j * BD, (j + 1) * BD)
            dwa_ref[j] += _dot_tn(xrb[:, cols], dpab[:, cols])
            dwx_ref[j] += _dot_tn(xrb[:, cols], dpxb[:, cols])

        acc(PK_RCONV_B, _colsum(dxr))
        rw = rw_ref[...]
        dxn = dxr_next[...]
        dx_r = (rw[3:4] * dxr + rw[2:3] * _shift_up(dxr, 1, dxn) + rw[1:2] * _shift_up(dxr, 2, dxn)
                + rw[0:1] * _shift_up(dxr, 3, dxn))
        dxr_next[...] = dxr[:TILE_ROWS]
        for k in range(3):
            acc(PK_RCONV_W, _colsum(dxr * _shift_down(x_r, 3 - k, xin_prev)), LRU_W, k)
        acc(PK_RCONV_W, _colsum(dxr * x_r), LRU_W, 3)
        du_ref[:, 3 * CONV_W:3 * CONV_W + LRU_W] = dx_r.astype(BF16)

        @pl.when(i == nt - 1)
        def _():
            for n, d_ref in enumerate((dwa_ref, dwx_ref)):
                for b in range(N_BD):
                    for q in range(BD // HEAD_DIM):
                        lane0 = q * HEAD_DIM // LANES * LANES
                        wide = d_ref[b, q * HEAD_DIM:(q + 1) * HEAD_DIM, lane0:lane0 + LANES]
                        if q * HEAD_DIM != lane0:
                            wide = pltpu.roll(wide, LANES - (q * HEAD_DIM - lane0), axis=1)
                        heads_ref[n, b * (BD // HEAD_DIM) + q] = wide[:, 0:HEAD_DIM].astype(BF16)
            exchange.wait()

    def full(a):
        nd = a.ndim
        return pl.BlockSpec(a.shape, lambda i: (0,) * nd)

    def tok(cols):
        return pl.BlockSpec((tm, cols), lambda i: (nt - 1 - i, 0))

    def halo(cols):
        return pl.BlockSpec((TILE_ROWS, cols), lambda i: (jnp.maximum((nt - 1 - i) * hb - 1, 0), 0))

    smalls = (conv_w, rconv_w, wa_bd, wx_bd, lam, g_nc, g_nr, st_mlp)
    st_rows = PK_MIX_ROWS + st_mlp.shape[0]
    heads = (2, N_HEADS, HEAD_DIM, HEAD_DIM)
    outs = pl.pallas_call(
        body, name="mix_bwd", grid=(nt,),
        in_specs=[tok(CONV_W + LRU_W), ANY, halo(IN_COLS), tok(LRU_W), tok(LRU_W), halo(LRU_W), tok(CONV_W), ANY]
        + [full(a) for a in smalls] + [ANY] * npart,
        out_specs=[tok(IN_COLS), pl.BlockSpec((st_rows, LRU_W), lambda i: (0, 0)),
                   pl.BlockSpec(heads, lambda i: (0, 0, 0, 0))]
        + [ANY] * npart,
        out_shape=[jax.ShapeDtypeStruct((t, IN_COLS), BF16), jax.ShapeDtypeStruct((st_rows, LRU_W), F32),
                   jax.ShapeDtypeStruct(heads, BF16)]
        + [jax.ShapeDtypeStruct(a.shape, a.dtype) for a in parts],
        scratch_shapes=[pltpu.VMEM((TILE_ROWS, CONV_W), F32), pltpu.VMEM((TILE_ROWS, LRU_W), F32),
                        pltpu.VMEM((TILE_ROWS, LRU_W), F32), pltpu.VMEM((TILE_ROWS, LRU_W), F32),
                        pltpu.VMEM((N_BD, BD, BD), F32), pltpu.VMEM((N_BD, BD, BD), F32),
                        pltpu.VMEM((RING, tm, IN_COLS), F32), pltpu.VMEM((RING, tm, 4 * LRU_W), F32),
                        pltpu.SemaphoreType.DMA((2, RING)),
                        pltpu.SemaphoreType.DMA((npart, 3)), pltpu.SemaphoreType.DMA((npart, 3))],
        compiler_params=_params(dimension_semantics=("arbitrary",)),
    )(dy, u, u, xr_all, hs_all, hs_all, c3_all, gates, *smalls, *parts)
    return outs[:3], outs[3:]


def _in_bwd(dub, w_in_g, x, dx2, g1, parts, joins, core_chip):
    t, d = x.shape
    tm = min(t, MATMUL_TOKEN_TILE)
    nt = t // tm
    npart = len(parts)
    nj = len(joins)
    geometry = []
    for tag, shape, _, _ in joins:
        pr, pc = WGRAD_GEOMETRY[tag][:2]
        every = 1 if pr % (nt * 16) == 0 else 2
        geometry.append((pr, pc, pr * every // nt, every, shape[1] == pc))

    def body(cc_ref, du_ref, win_ref, x_ref, dx2_ref, g1_ref, *rest):
        sums, rest = [rest[4 * w:4 * w + 4] for w in range(nj)], rest[4 * nj:]
        part_refs, (gx_ref, st_ref), rest = rest[:npart], rest[npart:npart + 2], rest[npart + 2:]
        arrived_refs, joined, rest = rest[:npart], rest[npart:npart + nj], rest[npart + nj:]
        stages, (send_sems, recv_sems, j_local, j_send, j_recv) = rest[:nj], rest[nj:]
        exchange = _PartialExchange(part_refs, arrived_refs, send_sems, recv_sems)
        i = pl.program_id(0)
        c = cc_ref[0]

        def window(w, core, row0, rows):
            pr, pc, _, _, by_rows = geometry[w]
            if by_rows:
                return joined[w].at[pl.ds(core * pr + row0, rows), :]
            return joined[w].at[pl.ds(row0, rows), pl.ds(core * pc, pc)]

        def to_sibling(w, src, core, row0, rows):
            return pltpu.make_async_remote_copy(src_ref=src, dst_ref=window(w, core, row0, rows), send_sem=j_send.at[w],
                                                recv_sem=j_recv.at[w], device_id=_sibling(), device_id_type=MESH)

        @pl.when(i == 0)
        def _():
            exchange.start()
            st_ref[...] = jnp.zeros_like(st_ref)

        for w in range(nj):
            pr, pc, rb, every, _ = geometry[w]

            @pl.when(i % every == 0)
            def _(w=w, rb=rb, every=every):
                p_ref, r1_ref, r2_ref, r3_ref = sums[w]
                row0 = pl.multiple_of((i // every) * rb, rb)
                rows = stages[w].at[pl.ds(row0, rb), :]
                rows[...] = ((p_ref[0] + r1_ref[0].astype(F32)) + r2_ref[0].astype(F32)) + r3_ref[0].astype(F32)
                pltpu.make_async_copy(rows, window(w, c, row0, rb), j_local.at[w]).start()
                to_sibling(w, rows, c, row0, rb).start()

        dh1 = _dot_nt(du_ref[:, 0:IN_SHARD], win_ref[0])
        for j in range(1, N_CHIPS):
            dh1 = dh1 + _dot_nt(du_ref[:, j * IN_SHARD:(j + 1) * IN_SHARD], win_ref[j])
        xv = x_ref[...]
        rstd = lax.rsqrt(jnp.mean(xv * xv, axis=-1, keepdims=True) + EPS)
        xh = xv * rstd
        st_ref[0:1, :] += _colsum(dh1 * xh)
        dxh = dh1 * g1_ref[...]
        gx_ref[...] = dx2_ref[...] + rstd * (dxh - xh * jnp.mean(dxh * xh, axis=-1, keepdims=True))

        @pl.when(i == nt - 1)
        def _():
            exchange.wait()
            for w in range(nj):
                pr = geometry[w][0]
                pltpu.make_async_copy(stages[w], window(w, c, 0, pr), j_local.at[w]).wait()
                to_sibling(w, stages[w], 1 - c, 0, pr).wait()

    def tok(cols):
        return pl.BlockSpec((tm, cols), lambda i, cc: (i, 0))

    def partial(w, off):
        pr, pc, rb, every, _ = geometry[w]
        return pl.BlockSpec((1, rb, pc), lambda i, cc: ((cc[1] + off) % N_CHIPS, i // every, 0))

    sum_specs, sum_operands = [], []
    for w, (_, _, own, arrived) in enumerate(joins):
        sum_specs += [partial(w, off) for off in range(N_CHIPS)]
        sum_operands += [own, arrived, arrived, arrived]
    dma = pltpu.SemaphoreType.DMA
    outs = pl.pallas_call(
        body, name="in_bwd",
        grid_spec=pltpu.PrefetchScalarGridSpec(
            num_scalar_prefetch=1, grid=(nt,),
            in_specs=[tok(IN_COLS), pl.BlockSpec(w_in_g.shape, lambda i, cc: (0, 0, 0)), tok(d), tok(d),
                      pl.BlockSpec((1, d), lambda i, cc: (0, 0))] + sum_specs + [ANY] * npart,
            out_specs=[tok(d), pl.BlockSpec((TILE_ROWS, d), lambda i, cc: (0, 0))] + [ANY] * (npart + nj),
            scratch_shapes=[pltpu.VMEM((g[0], g[1]), F32) for g in geometry]
            + [dma((npart, 3)), dma((npart, 3)), dma((nj,)), dma((nj,)), dma((nj,))]),
        out_shape=[jax.ShapeDtypeStruct((t, d), F32), jax.ShapeDtypeStruct((TILE_ROWS, d), F32)]
        + [jax.ShapeDtypeStruct(a.shape, a.dtype) for a in parts]
        + [jax.ShapeDtypeStruct(shape, F32) for _, shape, _, _ in joins],
        compiler_params=_params(dimension_semantics=("arbitrary",)),
    )(core_chip, dub, w_in_g, x, dx2, g1, *sum_operands, *parts)
    return outs[:2], outs[2:2 + npart], outs[2 + npart:]


WGRAD_GEOMETRY = {
    "in": (512, IN_SHARD, lambda s, h: h, lambda s, h: s),
    "mlp_in": (512, D_MODEL, lambda s, h: h, lambda s, h: s),
    "mlp_out": (512, D_MODEL, lambda s, h: 2 * s + h, lambda s, h: 0),
    "out": (384, 512, lambda s, h: s, lambda s, h: h),
}
K_CHUNK = 512
TOKEN_STREAMS = 2


def _sibling():
    x, y, c = _position()
    return (x, y, 1 - c)


def _wgrad(a, b, tag, core_chip, packs=(), parts=()):
    t = a.shape[0]
    pr, pc, a_blk, b_blk = WGRAD_GEOMETRY[tag]
    ns = TOKEN_STREAMS
    ts = t // ns
    kc = min(K_CHUNK, ts)
    mine = N_CHIPS
    riding = len(packs)
    npart = len(parts)
    assert not (riding and npart)

    def body(cc_ref, *rest):
        a_refs, b_refs, rest = rest[:ns], rest[ns:2 * ns], rest[2 * ns:]
        if riding:
            pack_refs, (land_ref, p_ref, pb_ref), rest = rest[:riding], rest[riding:riding + 3], rest[riding + 3:]
            all_refs, (stage, rbuf, send_sems, recv_sems, rsem), g_sems = rest[:riding], rest[riding:riding + 5], rest[riding + 5:]
            gathers = [_PackGather(pack_refs[n], all_refs[n], *g_sems[3 * n:3 * n + 3]) for n in range(riding)]
        elif npart:
            part_refs, (land_ref, p_ref, pb_ref), rest = rest[:npart], rest[npart:npart + 3], rest[npart + 3:]
            arrived_refs, (stage, rbuf, send_sems, recv_sems, rsem, x_send, x_recv) = rest[:npart], rest[npart:]
            exchange = _PartialExchange(part_refs, arrived_refs, x_send, x_recv)
        else:
            land_ref, p_ref, pb_ref, stage, rbuf, send_sems, recv_sems, rsem = rest
        ph, s = pl.program_id(0), pl.program_id(1)
        if riding:
            @pl.when((ph == 0) & (s == 0))
            def _():
                for gather in gathers:
                    gather.start()

            @pl.when((ph == 1) & (s == N_CHIPS - 2))
            def _():
                for gather in gathers:
                    gather.hand_over()
        if npart:
            @pl.when((ph == 0) & (s == 0))
            def _():
                exchange.start()
        def push(k):
            return pltpu.make_async_remote_copy(src_ref=stage.at[k], dst_ref=land_ref.at[k], send_sem=send_sems.at[k],
                                                recv_sem=recv_sems.at[k], device_id=_sibling(), device_id_type=MESH)

        def landed():
            return pltpu.make_async_copy(land_ref.at[s], rbuf, rsem)

        @pl.when(ph == 1)
        def _():
            push(s).wait_recv()
            landed().start()

        slot = jnp.where(ph == 0, s, mine)
        acc = stage.at[slot]
        chunks = [(a_ref, b_ref, k) for a_ref, b_ref in zip(a_refs, b_refs) for k in range(0, ts, kc)]
        for n, (a_ref, b_ref, k) in enumerate(chunks):
            part = _dot_tn(a_ref[k:k + kc, :], b_ref[k:k + kc, :])
            if n == 0:
                acc[...] = part
            else:
                acc[...] += part

        @pl.when(ph == 0)
        def _():
            push(s).start()

        @pl.when(ph == 1)
        def _():
            landed().wait()
            p = stage[mine] + rbuf[...]
            p_ref[0] = p
            pb_ref[0] = p.astype(BF16)

        @pl.when((ph == 1) & (s == N_CHIPS - 1))
        def _():
            for k in range(N_CHIPS):
                push(k).wait_send()
            for gather in (gathers if riding else ()):
                gather.finish()
            if npart:
                exchange.wait()

    def half(ph, cc):
        return jnp.where(ph == 0, 1 - cc[0], cc[0])

    def out_slot(ph, s, cc):
        return (jnp.where(ph == 0, 0, s), 0, 0)

    piece = jax.ShapeDtypeStruct((N_CHIPS, pr, pc), F32)
    in_specs = [pl.BlockSpec((ts, pr), lambda ph, s, cc, n=n: (n, a_blk(s, half(ph, cc)))) for n in range(ns)]
    in_specs += [pl.BlockSpec((ts, pc), lambda ph, s, cc, n=n: (n, b_blk(s, half(ph, cc)))) for n in range(ns)]
    out_specs = [ANY, pl.BlockSpec((1, pr, pc), out_slot), pl.BlockSpec((1, pr, pc), out_slot)]
    out_shape = [piece, piece, jax.ShapeDtypeStruct((N_CHIPS, pr, pc), BF16)]
    scratch = [pltpu.VMEM((N_CHIPS + 1, pr, pc), F32), pltpu.VMEM((pr, pc), F32),
               pltpu.SemaphoreType.DMA((N_CHIPS,)), pltpu.SemaphoreType.DMA((N_CHIPS,)), pltpu.SemaphoreType.DMA]
    operands = [a] * ns + [b] * ns
    for pack in packs:
        in_specs.append(pl.BlockSpec(pack.shape, lambda ph, s, cc, nd=pack.ndim: (0,) * nd))
        out_specs.append(ANY)
        out_shape.append(jax.ShapeDtypeStruct((N_DEVICES,) + pack.shape, pack.dtype))
        operands.append(pack)
    for pack in packs:
        scratch += _PackGather.semaphores()
    if npart:
        in_specs += [ANY] * npart
        out_specs += [ANY] * npart
        out_shape += [jax.ShapeDtypeStruct(p.shape, p.dtype) for p in parts]
        scratch += [pltpu.SemaphoreType.DMA((npart, 3)), pltpu.SemaphoreType.DMA((npart, 3))]
        operands += list(parts)
    return pl.pallas_call(
        body, name="wgrad_" + tag,
        grid_spec=pltpu.PrefetchScalarGridSpec(
            num_scalar_prefetch=1, grid=(2, N_CHIPS), in_specs=in_specs, out_specs=out_specs, scratch_shapes=scratch),
        out_shape=out_shape,
        compiler_params=_params(dimension_semantics=("arbitrary", "arbitrary")),
    )(core_chip, *operands)[1:]


def _other_chips(x, y):
    return [(1 - x, y), (x, 1 - y), (1 - x, 1 - y)]


class _ShardGather:
    PAIRS = 9

    def __init__(self, outs, send_sems, recv_sems):
        self.outs, self.send_sems, self.recv_sems = outs, send_sems, recv_sems
        x, y, c = _position()
        self.c, self.j = c, 2 * x + y
        self.sibling = (x, y, 1 - c)
        self.chips = _other_chips(x, y)

    def _chip(self, k):
        px, py = self.chips[k]
        return 2 * px + py

    def _half(self, w, chip, which):
        hr = self.outs[w].shape[1] // 2
        return self.outs[w].at[chip, pl.ds(which * hr, hr), :]

    def _quarter(self, w, chip, q):
        qr = self.outs[w].shape[1] // 4
        return self.outs[w].at[chip, pl.ds(self.c * 2 * qr + q * qr, qr), :]

    def _copy(self, ref, w, pair, to, src=None):
        return pltpu.make_async_remote_copy(src_ref=ref if src is None else src, dst_ref=ref, send_sem=self.send_sems.at[w, pair],
                                            recv_sem=self.recv_sems.at[w, pair], device_id=to, device_id_type=MESH)

    def direct(self, w, k, q, src=None):
        return self._copy(self._quarter(w, self.j, q), w, 2 * k + q, (*self.chips[k], self.c), src)

    def direct_landed(self, w, k, q):
        return self._copy(self._quarter(w, self._chip(k), q), w, 2 * k + q, (*self.chips[k], self.c))

    def pass_on(self, w, q):
        return self._copy(self._quarter(w, self._chip(q), q), w, 4 + q, (*self.chips[1 - q], self.c))

    def passed_landed(self, w, q):
        return self._copy(self._quarter(w, self._chip(2), q), w, 4 + q, (*self.chips[1 - q], self.c))

    def hand_over(self, w, k):
        return self._copy(self._half(w, self._chip(k), self.c), w, 6 + k, self.sibling)

    def handed(self, w, k):
        return self._copy(self._half(w, self._chip(k), 1 - self.c), w, 6 + k, self.sibling)

    def start_direct(self, w, src_half=None):
        qr = self.outs[w].shape[1] // 4
        for k, q in ((0, 0), (1, 1), (0, 1), (1, 0)):
            self.direct(w, k, q, None if src_half is None else src_half.at[pl.ds(q * qr, qr), :]).start()

    def start_pass_on(self, w):
        for q in (0, 1):
            self.direct_landed(w, q, q).wait_recv()
            self.pass_on(w, q).start()

    def start_hand_over(self, w, diagonal):
        if diagonal:
            for q in (0, 1):
                self.passed_landed(w, q).wait_recv()
            self.hand_over(w, 2).start()
        else:
            for k in (0, 1):
                self.direct_landed(w, k, 1 - k).wait_recv()
                self.hand_over(w, k).start()

    def finish(self, w):
        for k in range(3):
            self.handed(w, k).wait_recv()
            self.hand_over(w, k).wait_send()
        for q in (0, 1):
            self.pass_on(w, q).wait_send()
            for k in (0, 1):
                self.direct(w, k, q).wait_send()


def _gather_first(w_in, w_out, w1, w2, conv_w, rconv_w, w_a, w_x, x, g1):
    t, d = x.shape
    tn = min(t, MATMUL_TOKEN_TILE)
    n_tiles = t // tn
    bigs = (w_in, w_out, w1, w2)
    convs = (conv_w, rconv_w)
    heads = (w_a, w_x)
    nb, nc = len(bigs), len(convs)

    def body(win_ref, wout_hbm, w1_hbm, w2_hbm, cw_ref, rw_ref, wa_ref, wx_ref, x_hbm, g1_ref, gin, gout, g1, g2, gcw, grw,
             bda, bdx, h1_hbm, st_in, st_out, st_1, st_2, f_out, f_1, f_2, st_cw, st_rw, xbuf, hbuf, send_sems, recv_sems,
             sm_send, sm_recv, local_sems, load_sems, x_sems, h_sems):
        stages = (st_in, st_out, st_1, st_2)
        outs = (gin, gout, g1, g2)
        conv_stages, conv_outs = (st_cw, st_rw), (gcw, grw)
        plan = _ShardGather(outs[:1], send_sems, recv_sems)
        j, c = plan.j, plan.c
        local = [pltpu.make_async_copy(stages[w], outs[w].at[j], local_sems.at[w]) for w in range(nb)]
        loads = [pltpu.make_async_copy(src, dst, load_sems.at[n])
                 for n, (src, dst) in enumerate(((wout_hbm, f_out), (w1_hbm, f_1), (w2_hbm, f_2)))]

        def columns(n, chip):
            width = convs[n].shape[1]
            return conv_outs[n].at[:, pl.ds(chip * width, width)]

        def x_load(i):
            return pltpu.make_async_copy(x_hbm.at[pl.ds(i * tn, tn), :], xbuf.at[i % 2], x_sems.at[i % 2])

        def h1_store(i):
            return pltpu.make_async_copy(hbuf.at[i % 2], h1_hbm.at[pl.ds(i * tn, tn), :], h_sems.at[i % 2])

        def first_norm():
            x_load(0).start()
            for i in range(n_tiles):
                if i + 1 < n_tiles:
                    x_load(i + 1).start()
                x_load(i).wait()
                if i >= 2:
                    h1_store(i - 2).wait()
                xv = xbuf[i % 2]
                rstd = lax.rsqrt(jnp.mean(xv * xv, axis=-1, keepdims=True) + EPS)
                hbuf[i % 2] = ((xv * rstd) * g1_ref[...]).astype(BF16)
                h1_store(i).start()
            for i in range(max(n_tiles - 2, 0), n_tiles):
                h1_store(i).wait()

        local += [pltpu.make_async_copy(conv_stages[n], columns(n, j), local_sems.at[nb + n]) for n in range(nc)]

        def small_copy(k, n, landed=False):
            px, py = plan.chips[k]
            return pltpu.make_async_remote_copy(
                src_ref=conv_stages[n], dst_ref=columns(n, 2 * px + py if landed else j), send_sem=sm_send.at[k, n],
                recv_sem=sm_recv.at[k, n], device_id=(px, py, c), device_id_type=MESH)

        for cp in loads:
            cp.start()
        hr = w_in.shape[0] // 2
        st_in[...] = win_ref[...].astype(BF16)
        plan.start_direct(0, st_in.at[pl.ds(c * hr, hr), :])
        for src, st in zip((cw_ref, rw_ref), conv_stages):
            st[...] = jnp.zeros_like(st)
            st[0:src.shape[0], :] = src[...]
        for k in range(3):
            for n in range(nc):
                small_copy(k, n).start()
        for src, bd in ((wa_ref, bda), (wx_ref, bdx)):
            bd[...] = jnp.zeros_like(bd)
            for h in range(N_HEADS):
                q = h % (BD // HEAD_DIM)
                bd[h // (BD // HEAD_DIM), q * HEAD_DIM:(q + 1) * HEAD_DIM, q * HEAD_DIM:(q + 1) * HEAD_DIM] = src[h].astype(BF16)
        for cp, full, st in zip(loads, (f_out, f_1, f_2), stages[1:]):
            cp.wait()
            st[...] = full[...].astype(BF16)
        for cp in local:
            cp.start()
        plan.start_pass_on(0)
        first_norm()
        plan.start_hand_over(0, diagonal=False)
        plan.start_hand_over(0, diagonal=True)
        for k in range(3):
            for n in range(nc):
                small_copy(k, n, landed=True).wait_recv()
                small_copy(k, n).wait_send()
        plan.finish(0)
        for cp in local:
            cp.wait()

    def gathered(a, dtype):
        return jax.ShapeDtypeStruct((N_CHIPS,) + a.shape, dtype)

    return pl.pallas_call(
        body, name="gather_first",
        in_specs=[VMEM] + [ANY] * (nb - 1) + [VMEM] * (nc + len(heads)) + [ANY, VMEM],
        out_specs=[ANY] * (nb + nc) + [VMEM] * len(heads) + [ANY],
        out_shape=[gathered(a, BF16) for a in bigs]
        + [jax.ShapeDtypeStruct((TILE_ROWS, N_CHIPS * a.shape[1]), F32) for a in convs]
        + [jax.ShapeDtypeStruct((N_BD, BD, BD), BF16) for _ in heads] + [jax.ShapeDtypeStruct((t, d), BF16)],
        scratch_shapes=[pltpu.VMEM(a.shape, BF16) for a in bigs] + [pltpu.VMEM(a.shape, F32) for a in bigs[1:]]
        + [pltpu.VMEM((TILE_ROWS, a.shape[1]), F32) for a in convs]
        + [pltpu.VMEM((2, tn, d), F32), pltpu.VMEM((2, tn, d), BF16)]
        + [pltpu.SemaphoreType.DMA((1, _ShardGather.PAIRS)), pltpu.SemaphoreType.DMA((1, _ShardGather.PAIRS)),
           pltpu.SemaphoreType.DMA((3, nc)), pltpu.SemaphoreType.DMA((3, nc)), pltpu.SemaphoreType.DMA((nb + nc,)),
           pltpu.SemaphoreType.DMA((nb - 1,)), pltpu.SemaphoreType.DMA((2,)), pltpu.SemaphoreType.DMA((2,))],
        compiler_params=_params(),
    )(*bigs, *convs, *heads, x, g1)


class _PartialExchange:
    def __init__(self, parts, arrived, send_sems, recv_sems):
        self.parts, self.arrived, self.send_sems, self.recv_sems = parts, arrived, send_sems, recv_sems
        x, y, c = _position()
        self.c, self.j = c, 2 * x + y
        self.chips = _other_chips(x, y)

    def _copy(self, w, k, slot):
        px, py = self.chips[k]
        return pltpu.make_async_remote_copy(
            src_ref=self.parts[w].at[2 * px + py], dst_ref=self.arrived[w].at[slot], send_sem=self.send_sems.at[w, k],
            recv_sem=self.recv_sems.at[w, k], device_id=(px, py, self.c), device_id_type=MESH)

    def start(self):
        for w in range(len(self.parts)):
            for k in range(3):
                self._copy(w, k, self.j).start()

    def wait(self):
        for w in range(len(self.parts)):
            for k in range(3):
                px, py = self.chips[k]
                self._copy(w, k, 2 * px + py).wait()


class _PackGather:
    def __init__(self, p_ref, all_ref, send_sems, recv_sems, local_sem):
        self.p_ref, self.all_ref, self.send_sems, self.recv_sems, self.local_sem = p_ref, all_ref, send_sems, recv_sems, local_sem
        x, y, c = _position()
        self.me, self.sibling, self.c = (x, y, c), (x, y, 1 - c), c
        self.chips = _other_chips(x, y)

    @staticmethod
    def semaphores():
        return [pltpu.SemaphoreType.DMA((7,)), pltpu.SemaphoreType.DMA((7,)), pltpu.SemaphoreType.DMA]

    def _copy(self, k, block, to, from_pack=False):
        px, py, pc = block
        slot = self.all_ref.at[4 * px + 2 * py + pc]
        return pltpu.make_async_remote_copy(src_ref=self.p_ref if from_pack else slot, dst_ref=slot, send_sem=self.send_sems.at[k],
                                            recv_sem=self.recv_sems.at[k], device_id=to, device_id_type=MESH)

    def _mine(self):
        x, y, c = self.me
        return pltpu.make_async_copy(self.p_ref, self.all_ref.at[4 * x + 2 * y + c], self.local_sem)

    def _first(self):
        return [self._copy(0, self.me, self.sibling, True)] + [
            self._copy(1 + k, self.me, (*chip, self.c), True) for k, chip in enumerate(self.chips)]

    def _passed(self):
        return [self._copy(4 + k, (*chip, self.c), self.sibling) for k, chip in enumerate(self.chips)]

    def start(self):
        self._mine().start()
        for cp in self._first():
            cp.start()

    def hand_over(self):
        for k, chip in enumerate(self.chips):
            self._copy(1 + k, (*chip, self.c), self.me).wait_recv()
            self._passed()[k].start()

    def finish(self):
        self._copy(0, self.sibling, self.me).wait_recv()
        for k, chip in enumerate(self.chips):
            self._copy(4 + k, (*chip, 1 - self.c), self.me).wait_recv()
        for cp in self._first() + self._passed():
            cp.wait_send()
        self._mine().wait()


class _DirectGather:
    def __init__(self, p_ref, all_ref, send_sems, recv_sems, local_sem):
        self.p_ref, self.all_ref, self.send_sems, self.recv_sems, self.local_sem = p_ref, all_ref, send_sems, recv_sems, local_sem
        self.me = _position()

    semaphores = _PackGather.semaphores

    def _peer(self, r):
        x, y, c = self.me
        return ((1 - x) if r & 4 else x, (1 - y) if r & 2 else y, (1 - c) if r & 1 else c)

    def _copy(self, r, slot_of):
        px, py, pc = slot_of
        return pltpu.make_async_remote_copy(src_ref=self.p_ref, dst_ref=self.all_ref.at[4 * px + 2 * py + pc],
                                            send_sem=self.send_sems.at[r - 1], recv_sem=self.recv_sems.at[r - 1],
                                            device_id=self._peer(r), device_id_type=MESH)

    def _mine(self):
        x, y, c = self.me
        return pltpu.make_async_copy(self.p_ref, self.all_ref.at[4 * x + 2 * y + c], self.local_sem)

    def start(self):
        self._mine().start()
        for r in range(1, N_DEVICES):
            self._copy(r, self.me).start()

    def finish(self):
        for r in range(1, N_DEVICES):
            self._copy(r, self._peer(r)).wait()
        self._mine().wait()


def _adamw(w, g, m, v):
    m = ADAM_B1 * m + (1.0 - ADAM_B1) * g
    v = ADAM_B2 * v + (1.0 - ADAM_B2) * (g * g)
    m_hat = m / ADAM_BC1
    v_hat = v / ADAM_BC2
    delta = -ADAM_LR * (m_hat / (jnp.sqrt(v_hat) + ADAM_EPS) + ADAM_WD * w)
    return delta, m, v


JOIN_SUB = 4


def _join(tag, shard_shape, part, arrived, core_chip, block=None):
    pr, pc = WGRAD_GEOMETRY[tag][:2]
    rb = pr // JOIN_SUB
    by_rows = shard_shape[1] == pc
    riding = block is not None

    def body(cc_ref, p_ref, r1_ref, r2_ref, r3_ref, *rest):
        if riding:
            blk_ref, g_ref, all_ref, stage, send_sems, recv_sems, local_sems, b_send, b_recv, b_local = rest
            gather = _DirectGather(blk_ref, all_ref, b_send, b_recv, b_local)
        else:
            g_ref, stage, send_sems, recv_sems, local_sems = rest
        i = pl.program_id(0)
        c = cc_ref[0]
        if riding:
            @pl.when(i == 0)
            def _():
                gather.start()

        def window(core, k):
            if by_rows:
                return g_ref.at[pl.ds((core * JOIN_SUB + k) * rb, rb), :]
            return g_ref.at[pl.ds(k * rb, rb), pl.ds(core * pc, pc)]

        def keep(k):
            return pltpu.make_async_copy(stage.at[k], window(c, k), local_sems.at[k])

        def push(k):
            return pltpu.make_async_remote_copy(src_ref=stage.at[k], dst_ref=window(c, k), send_sem=send_sems.at[k],
                                                recv_sem=recv_sems.at[k], device_id=_sibling(), device_id_type=MESH)

        def pushed(k):
            return pltpu.make_async_remote_copy(src_ref=stage.at[k], dst_ref=window(1 - c, k), send_sem=send_sems.at[k],
                                                recv_sem=recv_sems.at[k], device_id=_sibling(), device_id_type=MESH)

        stage[i] = ((p_ref[0] + r1_ref[0].astype(F32)) + r2_ref[0].astype(F32)) + r3_ref[0].astype(F32)
        keep(i).start()
        push(i).start()

        @pl.when(i == JOIN_SUB - 1)
        def _():
            for k in range(JOIN_SUB):
                keep(k).wait()
                push(k).wait_send()
                pushed(k).wait_recv()
            if riding:
                gather.finish()

    def partial(off):
        return pl.BlockSpec((1, rb, pc), lambda i, cc: ((cc[1] + off) % N_CHIPS, i, 0))

    in_specs = [partial(0), partial(1), partial(2), partial(3)]
    out_specs = [ANY]
    out_shape = [jax.ShapeDtypeStruct(shard_shape, F32)]
    scratch = [pltpu.VMEM((JOIN_SUB, rb, pc), F32), pltpu.SemaphoreType.DMA((JOIN_SUB,)),
               pltpu.SemaphoreType.DMA((JOIN_SUB,)), pltpu.SemaphoreType.DMA((JOIN_SUB,))]
    operands = [part, arrived, arrived, arrived]
    if riding:
        in_specs.append(pl.BlockSpec(block.shape, lambda i, cc: (0, 0)))
        out_specs.append(ANY)
        out_shape.append(jax.ShapeDtypeStruct((N_DEVICES,) + block.shape, block.dtype))
        scratch += _DirectGather.semaphores()
        operands.append(block)
    outs = pl.pallas_call(
        body, name="join_" + tag,
        grid_spec=pltpu.PrefetchScalarGridSpec(
            num_scalar_prefetch=1, grid=(JOIN_SUB,), in_specs=in_specs, out_specs=out_specs, scratch_shapes=scratch),
        out_shape=out_shape,
        compiler_params=_params(dimension_semantics=("arbitrary",)),
    )(core_chip, *operands)
    return outs if riding else outs[0]


def _adamw_big(w, g, m, v, name):
    rows, cols = w.shape
    rb = ADAMW_ROWS if rows % ADAMW_ROWS == 0 else rows

    def body(w_ref, g_ref, m_ref, v_ref, go_ref, d_ref, nm_ref, nv_ref):
        g = g_ref[...]
        go_ref[...] = g
        d_ref[...], nm_ref[...], nv_ref[...] = _adamw(w_ref[...], g, m_ref[...], v_ref[...])

    spec = pl.BlockSpec((rb, cols), lambda i: (i, 0))
    return pl.pallas_call(
        body, name=name, grid=(rows // rb,), in_specs=[spec] * 4, out_specs=[spec] * 4,
        out_shape=[jax.ShapeDtypeStruct(w.shape, F32)] * 4,
        compiler_params=_params(dimension_semantics=("arbitrary",)),
    )(w, g, m, v)


SMALL_VECTORS = {
    "norm_mix_g": (PK_MIX_G, D_MODEL), "rnn_conv_b": (PK_RCONV_B, LRU_W), "b_a": (PK_B_A, LRU_W), "b_x": (PK_B_X, LRU_W),
    "lru_lambda": (PK_LAMBDA, LRU_W), "g_norm_conv": (PK_G_NORM_CONV, CONV_W), "g_norm_rnn": (PK_G_NORM_RNN, LRU_W),
    "norm_mlp_g": (PK_MLP_G, D_MODEL), "final_norm_g": (PK_FINAL_G, D_MODEL),
}
SMALL_MATRICES = ("w_a", "w_x")


def _small_step(vec_packs, mat_packs, mix_g_blocks, p):
    vec_rows, cols = vec_packs.shape[1:]
    conv_rows, cshard = p["conv_w"].shape
    rconv_rows, rshard = p["rnn_conv_w"].shape
    names = list(SMALL_VECTORS) + list(SMALL_MATRICES) + ["conv_w", "rnn_conv_w"]
    shapes = ([(1, width) for _, width in SMALL_VECTORS.values()] + [mat_packs.shape[2:]] * len(SMALL_MATRICES)
              + [(conv_rows, cshard), (rconv_rows, rshard)])
    kinds = ("", "m_", "v_")
    params = [p[pre + n].reshape(1, -1) if n in SMALL_VECTORS else p[pre + n] for pre in kinds for n in names]

    def body(vec_ref, mat_ref, blk_ref, *rest):
        wmv = [dict(zip(names, rest[k * len(names):(k + 1) * len(names)])) for k in range(3)]
        loss_ref, rest = rest[3 * len(names)], rest[3 * len(names) + 1:]
        leaves, (g_ref, w_ref, m_ref, v_ref) = [rest[k * len(names):(k + 1) * len(names)] for k in range(4)], rest[4 * len(names):]
        total = vec_ref[0]
        mats = mat_ref[0].astype(F32)
        late = blk_ref[0]
        for k in range(1, N_DEVICES):
            total = total + vec_ref[k]
            mats = mats + mat_ref[k].astype(F32)
            late = late + blk_ref[k]
        g_ref[0:vec_rows, :] = total
        g_ref[vec_rows:, :] = late
        g = g_ref[...]
        loss_ref[...] = g[PK_LOSS:PK_LOSS + 1, 0:1]

        for pack_ref, given in zip((w_ref, m_ref, v_ref), wmv):
            pack_ref[...] = jnp.zeros_like(pack_ref)
            for name, (row, width) in SMALL_VECTORS.items():
                pack_ref[row:row + 1, 0:width] = given[name][...]

        x, y, _ = _position()
        j = 2 * x + y
        cblk = total[0:TILE_ROWS, :]
        rblk = total[PK_RCONV_W:PK_RCONV_W + TILE_ROWS, :]
        cg = cblk[:, 0:cshard]
        rg = rblk[:, 0:rshard]
        for k in range(1, N_CHIPS):
            cg = jnp.where(j == k, cblk[:, k * cshard:(k + 1) * cshard], cg)
            rg = jnp.where(j == k, rblk[:, k * rshard:(k + 1) * rshard], rg)
        cg = cg[PK_CONV_W:PK_CONV_W + conv_rows, :]
        rg = rg[0:rconv_rows, :]

        def step(name, grad):
            return (grad,) + _adamw(wmv[0][name][...], grad, wmv[1][name][...], wmv[2][name][...])

        packs = (g,) + _adamw(w_ref[...], g, m_ref[...], v_ref[...])
        matrices = [step(name, mats[n]) for n, name in enumerate(SMALL_MATRICES)]
        convs, rconvs = step("conv_w", cg), step("rnn_conv_w", rg)
        for kind in range(4):
            out = dict(zip(names, leaves[kind]))
            for name, (row, width) in SMALL_VECTORS.items():
                out[name][...] = packs[kind][row:row + 1, 0:width]
            for n, name in enumerate(SMALL_MATRICES):
                out[name][...] = matrices[n][kind]
            out["conv_w"][...] = convs[kind]
            out["rnn_conv_w"][...] = rconvs[kind]

    outs = pl.pallas_call(
        body, name="small_grads_step", in_specs=[VMEM] * (3 + len(params)), out_specs=[VMEM] * (1 + 4 * len(names)),
        out_shape=[jax.ShapeDtypeStruct((1, 1), F32)] + [jax.ShapeDtypeStruct(sh, F32) for sh in shapes] * 4,
        scratch_shapes=[pltpu.VMEM((PK_ROWS, cols), F32)] * 4,
        compiler_params=_params(),
    )(vec_packs, mat_packs, mix_g_blocks, *params)
    return outs[0], [dict(zip(names, outs[1 + k * len(names):1 + (k + 1) * len(names)])) for k in range(4)]


_NAMES = ['norm_mix_g', 'w_in', 'conv_w', 'rnn_conv_w', 'rnn_conv_b', 'w_a', 'b_a', 'w_x', 'b_x', 'lru_lambda',
          'g_norm_conv', 'g_norm_rnn', 'w_out', 'norm_mlp_g', 'w_mlp_in', 'w_mlp_out', 'final_norm_g']


def kernel(x, norm_mix_g, w_in, conv_w, rnn_conv_w, rnn_conv_b, w_a, b_a, w_x, b_x, lru_lambda, g_norm_conv, g_norm_rnn, w_out, norm_mlp_g, w_mlp_in, w_mlp_out, final_norm_g, loss_target, m_norm_mix_g, m_w_in, m_conv_w, m_rnn_conv_w, m_rnn_conv_b, m_w_a, m_b_a, m_w_x, m_b_x, m_lru_lambda, m_g_norm_conv, m_g_norm_rnn, m_w_out, m_norm_mlp_g, m_w_mlp_in, m_w_mlp_out, m_final_norm_g, v_norm_mix_g, v_w_in, v_conv_w, v_rnn_conv_w, v_rnn_conv_b, v_w_a, v_b_a, v_w_x, v_b_x, v_lru_lambda, v_g_norm_conv, v_g_norm_rnn, v_w_out, v_norm_mlp_g, v_w_mlp_in, v_w_mlp_out, v_final_norm_g):
    args = dict(locals())
    p = {}
    for n in _NAMES:
        for pre in ("", "m_", "v_"):
            a = args[pre + n]
            p[pre + n] = a[0] if a.ndim >= 3 else a
    xs = x[0]
    target = loss_target[0]
    core_chip = jnp.stack([lax.axis_index("c"), 2 * lax.axis_index("x") + lax.axis_index("y")]).astype(jnp.int32)

    w_in_g, w_out_g, w1_g, w2_g, conv_full, rconv_full, wa_bd, wx_bd, h1b = _gather_first(
        p["w_in"], p["w_out"], p["w_mlp_in"], p["w_mlp_out"], p["conv_w"], p["rnn_conv_w"], p["w_a"], p["w_x"],
        xs, p["norm_mix_g"])
    gf = p["final_norm_g"].reshape(1, -1)
    lru = (wa_bd, p["b_a"], wx_bd, p["b_x"], p["lru_lambda"], p["g_norm_conv"], p["g_norm_rnn"])

    (u, xr, hs, c3, yb, gates), (w_out_g, w1_g, w2_g) = _fwd_mix(
        h1b, w_in_g, conv_full, rconv_full, p["rnn_conv_b"], *lru, (w_out_g, w1_g, w2_g))
    zb, dpb, h2b, dx3b, dx2, dx2b, dy, st_mlp = _mlp_fwd_bwd(
        xs, yb, w_out_g.reshape(-1, D_MODEL), w1_g, w2_g.reshape(-1, D_MODEL), p["norm_mlp_g"], gf, target)

    part_out = _wgrad(yb, dx2b, "out", core_chip)
    *part_1, arrived_out = _wgrad(h2b, dpb, "mlp_in", core_chip, parts=(part_out[1],))
    part_2 = _wgrad(zb, dx3b, "mlp_out", core_chip)
    (dub, vec_pack, mat_pack), (arrived_1, arrived_2) = _mix_bwd(
        dy, u, xr, hs, c3, gates, conv_full, rconv_full, wa_bd, wx_bd, p["lru_lambda"], p["g_norm_conv"], p["g_norm_rnn"],
        st_mlp, (part_1[1], part_2[1]))
    arrived_mlp = (arrived_out, arrived_1, arrived_2)
    *part_in, vec_packs, mat_packs = _wgrad(h1b, dub, "in", core_chip, packs=(vec_pack, mat_pack))
    early = (("w_out", "out", part_out, arrived_mlp[0]), ("w_mlp_in", "mlp_in", part_1, arrived_mlp[1]),
             ("w_mlp_out", "mlp_out", part_2, arrived_mlp[2]))
    (grad_x, st_in), arrived_in, joined = _in_bwd(
        dub, w_in_g, xs, dx2, p["norm_mix_g"], (part_in[1],),
        [(tag, p[n].shape, part[0], arrived) for n, tag, part, arrived in early], core_chip)
    g_in, mix_g_blocks = _join("in", p["w_in"].shape, part_in[0], arrived_in[0], core_chip, st_in)
    big = {}
    for n, tag, g in [(n, tag, g) for (n, tag, _, _), g in zip(early, joined)] + [("w_in", "in", g_in)]:
        big[n] = _adamw_big(p[n], g, p["m_" + n], p["v_" + n], "adamw_" + tag)

    loss, outs = _small_step(vec_packs, mat_packs, mix_g_blocks, p)
    for kind, o in enumerate(outs):
        o["final_norm_g"] = o["final_norm_g"].reshape(-1)
        for n in SMALL_MATRICES + ("conv_w", "rnn_conv_w"):
            o[n] = o[n][None]
        for n in ("w_in", "w_out", "w_mlp_in", "w_mlp_out"):
            o[n] = big[n][kind][None]
    loss = loss.reshape(())
    return (loss, grad_x[None], *[o[n] for o in outs for n in _NAMES])
```

```python
import functools
import math

import jax
import jax.numpy as jnp
from jax import lax
from jax.experimental import pallas as pl
from jax.experimental.pallas import tpu as pltpu

F32 = jnp.float32
BF16 = jnp.bfloat16
MESH = pl.DeviceIdType.MESH
ANY = pl.BlockSpec(memory_space=pl.ANY)
VMEM = pl.BlockSpec(memory_space=pltpu.VMEM)

EPS = 1e-6
LRU_C = 8.0
D_MODEL = 1024
CONV_W = 512
LRU_W = 1024
IN_COLS = 3 * CONV_W + 2 * LRU_W
IN_SHARD = IN_COLS // 4
N_CHIPS = 4
N_DEVICES = 8
BD = 256
N_BD = LRU_W // BD

ADAM_LR = 0.001
ADAM_B1 = 0.9
ADAM_B2 = 0.999
ADAM_EPS = 1e-08
ADAM_WD = 0.01
ADAM_STEP = 10
ADAM_BC1 = 1.0 - ADAM_B1 ** ADAM_STEP
ADAM_BC2 = 1.0 - ADAM_B2 ** ADAM_STEP

TILE_ROWS, LANES = 8, 128
TOKEN_TILE = 256
MATMUL_TOKEN_TILE = 512
ADAMW_ROWS = 256
VMEM_LIMIT = 56 * 1024 * 1024

PK_G_NORM_RNN, PK_RCONV_B, PK_B_A, PK_B_X, PK_LAMBDA, PK_CONV_W = 0, 1, 2, 3, 4, 5
PK_RCONV_W, PK_G_NORM_CONV = 8, 12
PK_MIX_ROWS = 16
PK_FINAL_G, PK_MLP_G, PK_LOSS = 16, 17, 18
PK_MLP_ROWS = 8
PK_MIX_G = 24
PK_ROWS = 32
N_HEADS, HEAD_DIM = 16, 64


def _params(**kw):
    return pltpu.CompilerParams(vmem_limit_bytes=VMEM_LIMIT, **kw)


def _position():
    x, y, c = lax.axis_index("x"), lax.axis_index("y"), lax.axis_index("c")
    return x, y, c


def _sigmoid(v):
    return 1.0 / (1.0 + jnp.exp(-v))


def _one_minus_square(log_a, a):
    v = 2.0 * log_a
    series = -v * (1.0 + v * (0.5 + v * (1.0 / 6.0)))
    return jnp.where(v > -0.01, series, 1.0 - a * a)


_GELU_C = math.sqrt(2.0 / math.pi)
_GELU_K = 0.044715


def _gelu_and_grad(g):
    th = jnp.tanh(_GELU_C * (g + _GELU_K * g * g * g))
    gelu = 0.5 * g * (1.0 + th)
    dgelu = 0.5 * (1.0 + th) + 0.5 * g * (1.0 - th * th) * (_GELU_C * (1.0 + 3.0 * _GELU_K * g * g))
    return gelu, dgelu


def _rows(shape):
    return lax.broadcasted_iota(jnp.int32, shape, 0)


def _shift_down(v, k, prev8):
    rolled = pltpu.roll(v, k, 0)
    halo = pltpu.roll(prev8, k, 0)
    head = jnp.where(_rows(halo.shape) < k, halo, rolled[:TILE_ROWS])
    return jnp.concatenate([head, rolled[TILE_ROWS:]], axis=0)


def _shift_up(v, k, next8):
    n = v.shape[0]
    rolled = pltpu.roll(v, n - k, 0)
    halo = pltpu.roll(next8, TILE_ROWS - k, 0)
    tail = jnp.where(_rows(halo.shape) >= TILE_ROWS - k, halo, rolled[n - TILE_ROWS:])
    return jnp.concatenate([rolled[: n - TILE_ROWS], tail], axis=0)


def _scan_rows(a, b, carry, reverse=False):
    n, w = a.shape
    groups = n // TILE_ROWS
    a3 = a.reshape(groups, TILE_ROWS, w)
    b3 = b.reshape(groups, TILE_ROWS, w)
    sub = lax.broadcasted_iota(jnp.int32, a3.shape, 1)
    s = 1
    while s < TILE_ROWS:
        shift = TILE_ROWS - s if reverse else s
        keep = (sub < TILE_ROWS - s) if reverse else (sub >= s)
        b3 = b3 + jnp.where(keep, a3 * pltpu.roll(b3, shift, 1), 0.0)
        a3 = a3 * jnp.where(keep, pltpu.roll(a3, shift, 1), 1.0)
        s *= 2
    out = [None] * groups
    edge = 0 if reverse else TILE_ROWS - 1
    for g in (range(groups - 1, -1, -1) if reverse else range(groups)):
        out[g] = b3[g] + a3[g] * carry
        carry = out[g][edge:edge + 1]
    return jnp.concatenate(out, axis=0)


def _softplus_neg(lam):
    e = jnp.exp(-jnp.abs(lam))
    log1p_e = jnp.where(e < 1e-2, e * (1.0 - e * (0.5 - e * (1.0 / 3.0 - e * 0.25))), jnp.log(1.0 + e))
    sp = jnp.maximum(-lam, 0.0) + log1p_e
    dsp = -_sigmoid(-lam)
    return sp, dsp


def _block_diag_dot(vb, w_ref):
    return jnp.concatenate(
        [jnp.dot(vb[:, j * BD:(j + 1) * BD], w_ref[j], preferred_element_type=F32) for j in range(N_BD)], axis=1)


def _block_diag_dot_t(vb, w_ref):
    return jnp.concatenate(
        [lax.dot_general(vb[:, j * BD:(j + 1) * BD], w_ref[j], (((1,), (1,)), ((), ())), preferred_element_type=F32)
         for j in range(N_BD)], axis=1)


def _dot_nt(a, b):
    return lax.dot_general(a, b, (((1,), (1,)), ((), ())), preferred_element_type=F32)


def _dot_tn(a, b):
    return lax.dot_general(a, b, (((0,), (0,)), ((), ())), preferred_element_type=F32)


def _lru_gates(xr, wa_ref, ba, wx_ref, bx, sp):
    xrb = xr.astype(BF16)
    r = _sigmoid(_block_diag_dot(xrb, wa_ref) + ba)
    ig = _sigmoid(_block_diag_dot(xrb, wx_ref) + bx)
    log_a = (-LRU_C) * r * sp
    a = jnp.exp(log_a)
    mult = jnp.sqrt(_one_minus_square(log_a, a))
    return r, ig, a, mult


def _colsum(v):
    return jnp.sum(v, axis=0, keepdims=True)


N_FWD_OUT = 6


def _fwd_mix(h1b, w_in_g, conv_w, rconv_w, rconv_b, wa_bd, b_a, wx_bd, b_x, lam, g_nc, g_nr, later):
    t, d = h1b.shape
    tm = TOKEN_TILE
    nt = t // tm
    nl = len(later)
    assert nl == 3
    pass_on_at = [nt * f // 16 for f in (3, 5, 9)]
    neighbours_at = [nt * f // 16 for f in (10, 11, 12)]
    diagonal_at = [nt * f // 16 for f in (13, 14, 14)]

    def body(h1_ref, win_ref, cw_ref, rw_ref, rb_ref, wa_ref, ba_ref, wx_ref, bx_ref, lam_ref, gnc_ref, gnr_ref, *rest):
        later_in, outs, rest = rest[:nl], rest[nl:nl + N_FWD_OUT], rest[nl + N_FWD_OUT:]
        u_ref, xr_ref, hs_ref, c3_ref, y_ref, gates_ref = outs
        later_out, (cv_prev, xin_prev, h_prev, send_sems, recv_sems) = rest[:nl], rest[nl:]
        del later_in
        step = pl.program_id(0)
        plan = _ShardGather(later_out, send_sems, recv_sems)

        @pl.when(step == 0)
        def _():
            cv_prev[...] = jnp.zeros_like(cv_prev)
            xin_prev[...] = jnp.zeros_like(xin_prev)
            h_prev[...] = jnp.zeros_like(h_prev)
            for w in range(nl):
                plan.start_direct(w)

        for w in range(nl):
            @pl.when(step == pass_on_at[w])
            def _(w=w):
                plan.start_pass_on(w)

            @pl.when(step == neighbours_at[w])
            def _(w=w):
                plan.start_hand_over(w, diagonal=False)

            @pl.when(step == diagonal_at[w])
            def _(w=w):
                plan.start_hand_over(w, diagonal=True)

        h1b = h1_ref[...]
        for j in range(N_CHIPS):
            u_ref[:, j * IN_SHARD:(j + 1) * IN_SHARD] = jnp.dot(h1b, win_ref[j], preferred_element_type=F32)
        gate_b = u_ref[:, 0:CONV_W]
        cv = u_ref[:, CONV_W:2 * CONV_W] * u_ref[:, 2 * CONV_W:3 * CONV_W]
        x_r = u_ref[:, 3 * CONV_W:3 * CONV_W + LRU_W]
        g = u_ref[:, 3 * CONV_W + LRU_W:]

        cw = cw_ref[...]
        cvp = cv_prev[...]
        conv3 = cw[0:1] * _shift_down(cv, 2, cvp) + cw[1:2] * _shift_down(cv, 1, cvp) + cw[2:3] * cv
        cv_prev[...] = cv[tm - TILE_ROWS:]
        c3_ref[...] = conv3
        y_conv = gate_b * conv3

        rw = rw_ref[...]
        xp = xin_prev[...]
        xr = (rw[0:1] * _shift_down(x_r, 3, xp) + rw[1:2] * _shift_down(x_r, 2, xp)
              + rw[2:3] * _shift_down(x_r, 1, xp) + rw[3:4] * x_r) + rb_ref[...]
        xin_prev[...] = x_r[tm - TILE_ROWS:]
        xr_ref[...] = xr
        sp, _ = _softplus_neg(lam_ref[...])
        r, ig, a, mult = _lru_gates(xr, wa_ref, ba_ref[...], wx_ref, bx_ref[...], sp)
        for n, gate in enumerate((r, ig, a, mult)):
            gates_ref[:, n * LRU_W:(n + 1) * LRU_W] = gate
        h = _scan_rows(a, mult * (ig * xr), h_prev[...])
        h_prev[...] = h[tm - 1:tm]
        hs_ref[...] = h
        gelu, _ = _gelu_and_grad(g)
        y_rnn = h * gelu

        na = y_conv * lax.rsqrt(jnp.mean(y_conv * y_conv, axis=-1, keepdims=True) + EPS) * gnc_ref[...]
        nb = y_rnn * lax.rsqrt(jnp.mean(y_rnn * y_rnn, axis=-1, keepdims=True) + EPS) * gnr_ref[...]
        y_ref[:, :CONV_W] = na.astype(BF16)
        y_ref[:, CONV_W:] = nb.astype(BF16)

        @pl.when(step == nt - 1)
        def _():
            for w in range(nl):
                plan.finish(w)

    def full(a):
        nd = a.ndim
        return pl.BlockSpec(a.shape, lambda i: (0,) * nd)

    def tok(cols):
        return pl.BlockSpec((tm, cols), lambda i: (i, 0))

    def act(cols, dtype=F32):
        return jax.ShapeDtypeStruct((t, cols), dtype)

    smalls = (w_in_g, conv_w, rconv_w, rconv_b, wa_bd, b_a, wx_bd, b_x, lam, g_nc, g_nr)
    n_in = 1 + len(smalls)
    outs = pl.pallas_call(
        body, name="fwd_mix", grid=(nt,),
        in_specs=[tok(d)] + [full(a) for a in smalls] + [ANY] * nl,
        out_specs=[tok(IN_COLS), tok(LRU_W), tok(LRU_W), tok(CONV_W), tok(CONV_W + LRU_W)]
        + [tok(4 * LRU_W)] + [ANY] * nl,
        out_shape=[act(IN_COLS), act(LRU_W), act(LRU_W), act(CONV_W), act(CONV_W + LRU_W, BF16)]
        + [act(4 * LRU_W)] + [jax.ShapeDtypeStruct(a.shape, a.dtype) for a in later],
        input_output_aliases={n_in + w: N_FWD_OUT + w for w in range(nl)},
        scratch_shapes=[pltpu.VMEM((TILE_ROWS, CONV_W), F32), pltpu.VMEM((TILE_ROWS, LRU_W), F32),
                        pltpu.VMEM((1, LRU_W), F32), pltpu.SemaphoreType.DMA((nl, _ShardGather.PAIRS)),
                        pltpu.SemaphoreType.DMA((nl, _ShardGather.PAIRS))],
        compiler_params=_params(dimension_semantics=("arbitrary",)),
    )(h1b, *smalls, *later)
    return outs[:N_FWD_OUT], outs[N_FWD_OUT:]


def _mlp_fwd_bwd(x, yb, w_out_g, w1_g, w2_g, g2, gf, target):
    t, d = x.shape
    tm = TOKEN_TILE
    ff = w2_g.shape[0]
    mix = w_out_g.shape[0]
    ffs = ff // N_CHIPS

    def body(x_ref, y_ref, g2_ref, gf_ref, tgt_ref, wout_hbm, w1_hbm, w2_hbm,
             z_ref, dp_ref, h2_ref, dx3b_ref, dx2_ref, dx2b_ref, dy_ref, st_ref, wout, w1, w2, p_ref):
        @pl.when(pl.program_id(0) == 0)
        def _():
            pltpu.sync_copy(wout_hbm, wout)
            pltpu.sync_copy(w1_hbm, w1)
            pltpu.sync_copy(w2_hbm, w2)
            st_ref[...] = jnp.zeros_like(st_ref)

        x2 = x_ref[...] + jnp.dot(y_ref[...], wout[...], preferred_element_type=F32)
        r2 = lax.rsqrt(jnp.mean(x2 * x2, axis=-1, keepdims=True) + EPS)
        xh2 = x2 * r2
        g2v = g2_ref[...]
        h2b = (xh2 * g2v).astype(BF16)
        h2_ref[...] = h2b
        for j in range(N_CHIPS):
            p_ref[:, j * ffs:(j + 1) * ffs] = jnp.dot(h2b, w1[j], preferred_element_type=F32)
        rp = jnp.maximum(p_ref[...], 0.0)
        zb = (rp * rp).astype(BF16)
        z_ref[...] = zb
        x3 = x2 + jnp.dot(zb, w2[...], preferred_element_type=F32)
        r3 = lax.rsqrt(jnp.mean(x3 * x3, axis=-1, keepdims=True) + EPS)
        xh3 = x3 * r3
        gfv = gf_ref[...]
        err = xh3 * gfv - tgt_ref[...]
        loss = (0.5 / d) * jnp.sum(err * err)
        dout = err * (1.0 / d)
        st_ref[PK_FINAL_G - PK_MIX_ROWS:PK_FINAL_G - PK_MIX_ROWS + 1, :] += _colsum(dout * xh3)
        st_ref[PK_LOSS - PK_MIX_ROWS:PK_LOSS - PK_MIX_ROWS + 1, :] += jnp.zeros((1, d), F32) + loss
        dxh3 = dout * gfv
        dx3 = r3 * (dxh3 - xh3 * jnp.mean(dxh3 * xh3, axis=-1, keepdims=True))
        dx3b = dx3.astype(BF16)
        dx3b_ref[...] = dx3b
        dpb = (_dot_nt(dx3b, w2[...]) * (2.0 * rp)).astype(BF16)
        dp_ref[...] = dpb
        dh2 = _dot_nt(dpb[:, 0:ffs], w1[0])
        for j in range(1, N_CHIPS):
            dh2 = dh2 + _dot_nt(dpb[:, j * ffs:(j + 1) * ffs], w1[j])
        st_ref[PK_MLP_G - PK_MIX_ROWS:PK_MLP_G - PK_MIX_ROWS + 1, :] += _colsum(dh2 * xh2)
        dxh2 = dh2 * g2v
        dx2 = dx3 + r2 * (dxh2 - xh2 * jnp.mean(dxh2 * xh2, axis=-1, keepdims=True))
        dx2_ref[...] = dx2
        dx2b = dx2.astype(BF16)
        dx2b_ref[...] = dx2b
        dy_ref[...] = _dot_nt(dx2b, wout[...])

    def tok(cols):
        return pl.BlockSpec((tm, cols), lambda i: (i, 0))

    def row(cols):
        return pl.BlockSpec((1, cols), lambda i: (0, 0))

    return pl.pallas_call(
        body, name="mlp_fwd_bwd", grid=(t // tm,),
        in_specs=[tok(d), tok(mix), row(d), row(d), tok(d), ANY, ANY, ANY],
        out_specs=[tok(ff), tok(ff), tok(d), tok(d), tok(d), tok(d), tok(mix),
                   pl.BlockSpec((PK_MLP_ROWS, d), lambda i: (0, 0))],
        out_shape=[jax.ShapeDtypeStruct((t, ff), BF16), jax.ShapeDtypeStruct((t, ff), BF16),
                   jax.ShapeDtypeStruct((t, d), BF16), jax.ShapeDtypeStruct((t, d), BF16),
                   jax.ShapeDtypeStruct((t, d), F32), jax.ShapeDtypeStruct((t, d), BF16),
                   jax.ShapeDtypeStruct((t, mix), F32), jax.ShapeDtypeStruct((PK_MLP_ROWS, d), F32)],
        scratch_shapes=[pltpu.VMEM(w_out_g.shape, BF16), pltpu.VMEM(w1_g.shape, BF16), pltpu.VMEM(w2_g.shape, BF16),
                        pltpu.VMEM((tm, ff), F32)],
        compiler_params=_params(dimension_semantics=("arbitrary",)),
    )(x, yb, g2, gf, target, w_out_g, w1_g, w2_g)


def _mix_bwd(dy, u, xr_all, hs_all, c3_all, gates, conv_w, rconv_w, wa_bd, wx_bd, lam, g_nc, g_nr, st_mlp, parts):
    t = dy.shape[0]
    tm = TOKEN_TILE
    nt = t // tm
    hb = tm // TILE_ROWS
    npart = len(parts)

    def body(dy_ref, u_ref, uh_ref, xr_ref, hs_ref, hh_ref, c3_ref, gates_ref,
             cw_ref, rw_ref, wa_ref, wx_ref, lam_ref, gnc_ref, gnr_ref, stm_ref, *rest):
        part_refs, (du_ref, st_ref, heads_ref), rest = rest[:npart], rest[npart:npart + 3], rest[npart + 3:]
        arrived_refs, (dc_next, a_next, gs_next, dxr_next, dwa_ref, dwx_ref, send_sems, recv_sems) = rest[:npart], rest[npart:]
        exchange = _PartialExchange(part_refs, arrived_refs, send_sems, recv_sems)
        i = pl.program_id(0)

        @pl.when(i == 0)
        def _():
            exchange.start()
            dc_next[...] = jnp.zeros_like(dc_next)
            a_next[...] = jnp.zeros_like(a_next)
            gs_next[...] = jnp.zeros_like(gs_next)
            dxr_next[...] = jnp.zeros_like(dxr_next)
            st_ref[0:PK_MIX_ROWS, :] = jnp.zeros((PK_MIX_ROWS, LRU_W), F32)
            st_ref[PK_MIX_ROWS:, :] = stm_ref[...]
            dwa_ref[...] = jnp.zeros_like(dwa_ref)
            dwx_ref[...] = jnp.zeros_like(dwx_ref)

        first_tile = i == nt - 1
        gate_b = u_ref[:, 0:CONV_W]
        gate_c = u_ref[:, CONV_W:2 * CONV_W]
        v = u_ref[:, 2 * CONV_W:3 * CONV_W]
        x_r = u_ref[:, 3 * CONV_W:3 * CONV_W + LRU_W]
        g = u_ref[:, 3 * CONV_W + LRU_W:]
        cv = gate_c * v
        cv_prev = jnp.where(first_tile, 0.0, uh_ref[:, CONV_W:2 * CONV_W] * uh_ref[:, 2 * CONV_W:3 * CONV_W])
        xin_prev = jnp.where(first_tile, 0.0, uh_ref[:, 3 * CONV_W:3 * CONV_W + LRU_W])
        hs_prev = jnp.where(first_tile, 0.0, hh_ref[...])

        def acc(first_row, val, width=LRU_W, row=0):
            r0 = first_row + row
            st_ref[r0:r0 + 1, 0:width] += val

        conv3 = c3_ref[...]
        y_conv = gate_b * conv3
        ra = lax.rsqrt(jnp.mean(y_conv * y_conv, axis=-1, keepdims=True) + EPS)
        xha = y_conv * ra
        dna = dy_ref[:, :CONV_W]
        acc(PK_G_NORM_CONV, _colsum(dna * xha), CONV_W)
        dxha = dna * gnc_ref[...]
        dy_conv = ra * (dxha - xha * jnp.mean(dxha * xha, axis=-1, keepdims=True))
        du_ref[:, 0:CONV_W] = (dy_conv * conv3).astype(BF16)
        dc = dy_conv * gate_b
        cw = cw_ref[...]
        dcn = dc_next[...]
        dcv = cw[2:3] * dc + cw[1:2] * _shift_up(dc, 1, dcn) + cw[0:1] * _shift_up(dc, 2, dcn)
        dc_next[...] = dc[:TILE_ROWS]
        acc(PK_CONV_W, _colsum(dc * _shift_down(cv, 2, cv_prev)), CONV_W, 0)
        acc(PK_CONV_W, _colsum(dc * _shift_down(cv, 1, cv_prev)), CONV_W, 1)
        acc(PK_CONV_W, _colsum(dc * cv), CONV_W, 2)
        du_ref[:, CONV_W:2 * CONV_W] = (dcv * v).astype(BF16)
        du_ref[:, 2 * CONV_W:3 * CONV_W] = (dcv * gate_c).astype(BF16)

        hs = hs_ref[...]
        gelu, dgelu = _gelu_and_grad(g)
        y_rnn = hs * gelu
        rb = lax.rsqrt(jnp.mean(y_rnn * y_rnn, axis=-1, keepdims=True) + EPS)
        xhb = y_rnn * rb
        dnb = dy_ref[:, CONV_W:]
        acc(PK_G_NORM_RNN, _colsum(dnb * xhb))
        dxhb = dnb * gnr_ref[...]
        dy_rnn = rb * (dxhb - xhb * jnp.mean(dxhb * xhb, axis=-1, keepdims=True))
        du_ref[:, 3 * CONV_W + LRU_W:] = (dy_rnn * hs * dgelu).astype(BF16)
        dh = dy_rnn * gelu

        xr = xr_ref[...]
        xrb = xr.astype(BF16)
        sp, dsp = _softplus_neg(lam_ref[...])
        r, ig, a, mult = [gates_ref[:, n * LRU_W:(n + 1) * LRU_W] for n in range(4)]
        a_up = _shift_up(a, 1, a_next[...])
        a_next[...] = a[:TILE_ROWS]
        gs = _scan_rows(a_up, dh, gs_next[0:1, :], reverse=True)
        gs_next[...] = gs[:TILE_ROWS]
        da = gs * _shift_down(hs, 1, hs_prev)
        gx = gs * xr
        di = gx * mult
        dmult = gx * ig
        dxr = gs * (mult * ig)
        dlog_a = da * a - dmult * ((a * a) / mult)
        acc(PK_LAMBDA, _colsum(dlog_a * r) * ((-LRU_C) * dsp))
        dpa = (dlog_a * ((-LRU_C) * sp)) * (r * (1.0 - r))
        dpx = di * (ig * (1.0 - ig))
        acc(PK_B_A, _colsum(dpa))
        acc(PK_B_X, _colsum(dpx))
        dpab = dpa.astype(BF16)
        dpxb = dpx.astype(BF16)
        dxr = dxr + _block_diag_dot_t(dpab, wa_ref) + _block_diag_dot_t(dpxb, wx_ref)
        for j in range(N_BD):
            cols = slice(j * BD, (j + 1) * BD)
            dwa_ref[j] += _dot_tn(xrb[:, cols], dpab[:, cols])
            dwx_ref[j] += _dot_tn(xrb[:, cols], dpxb[:, cols])

        acc(PK_RCONV_B, _colsum(dxr))
        rw = rw_ref[...]
        dxn = dxr_next[...]
        dx_r = (rw[3:4] * dxr + rw[2:3] * _shift_up(dxr, 1, dxn) + rw[1:2] * _shift_up(dxr, 2, dxn)
                + rw[0:1] * _shift_up(dxr, 3, dxn))
        dxr_next[...] = dxr[:TILE_ROWS]
        for k in range(3):
            acc(PK_RCONV_W, _colsum(dxr * _shift_down(x_r, 3 - k, xin_prev)), LRU_W, k)
        acc(PK_RCONV_W, _colsum(dxr * x_r), LRU_W, 3)
        du_ref[:, 3 * CONV_W:3 * CONV_W + LRU_W] = dx_r.astype(BF16)

        @pl.when(i == nt - 1)
        def _():
            for n, d_ref in enumerate((dwa_ref, dwx_ref)):
                for b in range(N_BD):
                    for q in range(BD // HEAD_DIM):
                        lane0 = q * HEAD_DIM // LANES * LANES
                        wide = d_ref[b, q * HEAD_DIM:(q + 1) * HEAD_DIM, lane0:lane0 + LANES]
                        if q * HEAD_DIM != lane0:
                            wide = pltpu.roll(wide, LANES - (q * HEAD_DIM - lane0), axis=1)
                        heads_ref[n, b * (BD // HEAD_DIM) + q] = wide[:, 0:HEAD_DIM].astype(BF16)
            exchange.wait()

    def full(a):
        nd = a.ndim
        return pl.BlockSpec(a.shape, lambda i: (0,) * nd)

    def tok(cols):
        return pl.BlockSpec((tm, cols), lambda i: (nt - 1 - i, 0))

    def halo(cols):
        return pl.BlockSpec((TILE_ROWS, cols), lambda i: (jnp.maximum((nt - 1 - i) * hb - 1, 0), 0))

    smalls = (conv_w, rconv_w, wa_bd, wx_bd, lam, g_nc, g_nr, st_mlp)
    st_rows = PK_MIX_ROWS + st_mlp.shape[0]
    heads = (2, N_HEADS, HEAD_DIM, HEAD_DIM)
    outs = pl.pallas_call(
        body, name="mix_bwd", grid=(nt,),
        in_specs=[tok(CONV_W + LRU_W), tok(IN_COLS), halo(IN_COLS), tok(LRU_W), tok(LRU_W), halo(LRU_W), tok(CONV_W)]
        + [tok(4 * LRU_W)] + [full(a) for a in smalls] + [ANY] * npart,
        out_specs=[tok(IN_COLS), pl.BlockSpec((st_rows, LRU_W), lambda i: (0, 0)),
                   pl.BlockSpec(heads, lambda i: (0, 0, 0, 0))]
        + [ANY] * npart,
        out_shape=[jax.ShapeDtypeStruct((t, IN_COLS), BF16), jax.ShapeDtypeStruct((st_rows, LRU_W), F32),
                   jax.ShapeDtypeStruct(heads, BF16)]
        + [jax.ShapeDtypeStruct(a.shape, a.dtype) for a in parts],
        scratch_shapes=[pltpu.VMEM((TILE_ROWS, CONV_W), F32), pltpu.VMEM((TILE_ROWS, LRU_W), F32),
                        pltpu.VMEM((TILE_ROWS, LRU_W), F32), pltpu.VMEM((TILE_ROWS, LRU_W), F32),
                        pltpu.VMEM((N_BD, BD, BD), F32), pltpu.VMEM((N_BD, BD, BD), F32),
                        pltpu.SemaphoreType.DMA((npart, 3)), pltpu.SemaphoreType.DMA((npart, 3))],
        compiler_params=_params(dimension_semantics=("arbitrary",)),
    )(dy, u, u, xr_all, hs_all, hs_all, c3_all, gates, *smalls, *parts)
    return outs[:3], outs[3:]


def _in_bwd(dub, w_in_g, x, dx2, g1, parts, joins, core_chip):
    t, d = x.shape
    tm = min(t, MATMUL_TOKEN_TILE)
    nt = t // tm
    npart = len(parts)
    nj = len(joins)
    geometry = []
    for tag, shape, _, _ in joins:
        pr, pc = WGRAD_GEOMETRY[tag][:2]
        every = 1 if pr % (nt * 16) == 0 else 2
        geometry.append((pr, pc, pr * every // nt, every, shape[1] == pc))

    def body(cc_ref, du_ref, win_ref, x_ref, dx2_ref, g1_ref, *rest):
        sums, rest = [rest[4 * w:4 * w + 4] for w in range(nj)], rest[4 * nj:]
        part_refs, (gx_ref, st_ref), rest = rest[:npart], rest[npart:npart + 2], rest[npart + 2:]
        arrived_refs, joined, rest = rest[:npart], rest[npart:npart + nj], rest[npart + nj:]
        stages, (send_sems, recv_sems, j_local, j_send, j_recv) = rest[:nj], rest[nj:]
        exchange = _PartialExchange(part_refs, arrived_refs, send_sems, recv_sems)
        i = pl.program_id(0)
        c = cc_ref[0]

        def window(w, core, row0, rows):
            pr, pc, _, _, by_rows = geometry[w]
            if by_rows:
                return joined[w].at[pl.ds(core * pr + row0, rows), :]
            return joined[w].at[pl.ds(row0, rows), pl.ds(core * pc, pc)]

        def to_sibling(w, src, core, row0, rows):
            return pltpu.make_async_remote_copy(src_ref=src, dst_ref=window(w, core, row0, rows), send_sem=j_send.at[w],
                                                recv_sem=j_recv.at[w], device_id=_sibling(), device_id_type=MESH)

        @pl.when(i == 0)
        def _():
            exchange.start()
            st_ref[...] = jnp.zeros_like(st_ref)

        for w in range(nj):
            pr, pc, rb, every, _ = geometry[w]

            @pl.when(i % every == 0)
            def _(w=w, rb=rb, every=every):
                p_ref, r1_ref, r2_ref, r3_ref = sums[w]
                row0 = pl.multiple_of((i // every) * rb, rb)
                rows = stages[w].at[pl.ds(row0, rb), :]
                rows[...] = ((p_ref[0] + r1_ref[0].astype(F32)) + r2_ref[0].astype(F32)) + r3_ref[0].astype(F32)
                pltpu.make_async_copy(rows, window(w, c, row0, rb), j_local.at[w]).start()
                to_sibling(w, rows, c, row0, rb).start()

        dh1 = _dot_nt(du_ref[:, 0:IN_SHARD], win_ref[0])
        for j in range(1, N_CHIPS):
            dh1 = dh1 + _dot_nt(du_ref[:, j * IN_SHARD:(j + 1) * IN_SHARD], win_ref[j])
        xv = x_ref[...]
        rstd = lax.rsqrt(jnp.mean(xv * xv, axis=-1, keepdims=True) + EPS)
        xh = xv * rstd
        st_ref[0:1, :] += _colsum(dh1 * xh)
        dxh = dh1 * g1_ref[...]
        gx_ref[...] = dx2_ref[...] + rstd * (dxh - xh * jnp.mean(dxh * xh, axis=-1, keepdims=True))

        @pl.when(i == nt - 1)
        def _():
            exchange.wait()
            for w in range(nj):
                pr = geometry[w][0]
                pltpu.make_async_copy(stages[w], window(w, c, 0, pr), j_local.at[w]).wait()
                to_sibling(w, stages[w], 1 - c, 0, pr).wait()

    def tok(cols):
        return pl.BlockSpec((tm, cols), lambda i, cc: (i, 0))

    def partial(w, off):
        pr, pc, rb, every, _ = geometry[w]
        return pl.BlockSpec((1, rb, pc), lambda i, cc: ((cc[1] + off) % N_CHIPS, i // every, 0))

    sum_specs, sum_operands = [], []
    for w, (_, _, own, arrived) in enumerate(joins):
        sum_specs += [partial(w, off) for off in range(N_CHIPS)]
        sum_operands += [own, arrived, arrived, arrived]
    dma = pltpu.SemaphoreType.DMA
    outs = pl.pallas_call(
        body, name="in_bwd",
        grid_spec=pltpu.PrefetchScalarGridSpec(
            num_scalar_prefetch=1, grid=(nt,),
            in_specs=[tok(IN_COLS), pl.BlockSpec(w_in_g.shape, lambda i, cc: (0, 0, 0)), tok(d), tok(d),
                      pl.BlockSpec((1, d), lambda i, cc: (0, 0))] + sum_specs + [ANY] * npart,
            out_specs=[tok(d), pl.BlockSpec((TILE_ROWS, d), lambda i, cc: (0, 0))] + [ANY] * (npart + nj),
            scratch_shapes=[pltpu.VMEM((g[0], g[1]), F32) for g in geometry]
            + [dma((npart, 3)), dma((npart, 3)), dma((nj,)), dma((nj,)), dma((nj,))]),
        out_shape=[jax.ShapeDtypeStruct((t, d), F32), jax.ShapeDtypeStruct((TILE_ROWS, d), F32)]
        + [jax.ShapeDtypeStruct(a.shape, a.dtype) for a in parts]
        + [jax.ShapeDtypeStruct(shape, F32) for _, shape, _, _ in joins],
        compiler_params=_params(dimension_semantics=("arbitrary",)),
    )(core_chip, dub, w_in_g, x, dx2, g1, *sum_operands, *parts)
    return outs[:2], outs[2:2 + npart], outs[2 + npart:]


WGRAD_GEOMETRY = {
    "in": (512, IN_SHARD, lambda s, h: h, lambda s, h: s),
    "mlp_in": (512, D_MODEL, lambda s, h: h, lambda s, h: s),
    "mlp_out": (512, D_MODEL, lambda s, h: 2 * s + h, lambda s, h: 0),
    "out": (384, 512, lambda s, h: s, lambda s, h: h),
}
K_CHUNK = 512
TOKEN_STREAMS = 2


def _sibling():
    x, y, c = _position()
    return (x, y, 1 - c)


def _wgrad(a, b, tag, core_chip, packs=(), parts=()):
    t = a.shape[0]
    pr, pc, a_blk, b_blk = WGRAD_GEOMETRY[tag]
    ns = TOKEN_STREAMS
    ts = t // ns
    kc = min(K_CHUNK, ts)
    mine = N_CHIPS
    riding = len(packs)
    npart = len(parts)
    assert not (riding and npart)

    def body(cc_ref, *rest):
        a_refs, b_refs, rest = rest[:ns], rest[ns:2 * ns], rest[2 * ns:]
        if riding:
            pack_refs, (land_ref, p_ref, pb_ref), rest = rest[:riding], rest[riding:riding + 3], rest[riding + 3:]
            all_refs, (stage, rbuf, send_sems, recv_sems, rsem), g_sems = rest[:riding], rest[riding:riding + 5], rest[riding + 5:]
            gathers = [_PackGather(pack_refs[n], all_refs[n], *g_sems[3 * n:3 * n + 3]) for n in range(riding)]
        elif npart:
            part_refs, (land_ref, p_ref, pb_ref), rest = rest[:npart], rest[npart:npart + 3], rest[npart + 3:]
            arrived_refs, (stage, rbuf, send_sems, recv_sems, rsem, x_send, x_recv) = rest[:npart], rest[npart:]
            exchange = _PartialExchange(part_refs, arrived_refs, x_send, x_recv)
        else:
            land_ref, p_ref, pb_ref, stage, rbuf, send_sems, recv_sems, rsem = rest
        ph, s = pl.program_id(0), pl.program_id(1)
        if riding:
            @pl.when((ph == 0) & (s == 0))
            def _():
                for gather in gathers:
                    gather.start()

            @pl.when((ph == 1) & (s == N_CHIPS - 2))
            def _():
                for gather in gathers:
                    gather.hand_over()
        if npart:
            @pl.when((ph == 0) & (s == 0))
            def _():
                exchange.start()
        def push(k):
            return pltpu.make_async_remote_copy(src_ref=stage.at[k], dst_ref=land_ref.at[k], send_sem=send_sems.at[k],
                                                recv_sem=recv_sems.at[k], device_id=_sibling(), device_id_type=MESH)

        def landed():
            return pltpu.make_async_copy(land_ref.at[s], rbuf, rsem)

        @pl.when(ph == 1)
        def _():
            push(s).wait_recv()
            landed().start()

        slot = jnp.where(ph == 0, s, mine)
        acc = stage.at[slot]
        chunks = [(a_ref, b_ref, k) for a_ref, b_ref in zip(a_refs, b_refs) for k in range(0, ts, kc)]
        for n, (a_ref, b_ref, k) in enumerate(chunks):
            part = _dot_tn(a_ref[k:k + kc, :], b_ref[k:k + kc, :])
            if n == 0:
                acc[...] = part
            else:
                acc[...] += part

        @pl.when(ph == 0)
        def _():
            push(s).start()

        @pl.when(ph == 1)
        def _():
            landed().wait()
            p = stage[mine] + rbuf[...]
            p_ref[0] = p
            pb_ref[0] = p.astype(BF16)

        @pl.when((ph == 1) & (s == N_CHIPS - 1))
        def _():
            for k in range(N_CHIPS):
                push(k).wait_send()
            for gather in (gathers if riding else ()):
                gather.finish()
            if npart:
                exchange.wait()

    def half(ph, cc):
        return jnp.where(ph == 0, 1 - cc[0], cc[0])

    def out_slot(ph, s, cc):
        return (jnp.where(ph == 0, 0, s), 0, 0)

    piece = jax.ShapeDtypeStruct((N_CHIPS, pr, pc), F32)
    in_specs = [pl.BlockSpec((ts, pr), lambda ph, s, cc, n=n: (n, a_blk(s, half(ph, cc)))) for n in range(ns)]
    in_specs += [pl.BlockSpec((ts, pc), lambda ph, s, cc, n=n: (n, b_blk(s, half(ph, cc)))) for n in range(ns)]
    out_specs = [ANY, pl.BlockSpec((1, pr, pc), out_slot), pl.BlockSpec((1, pr, pc), out_slot)]
    out_shape = [piece, piece, jax.ShapeDtypeStruct((N_CHIPS, pr, pc), BF16)]
    scratch = [pltpu.VMEM((N_CHIPS + 1, pr, pc), F32), pltpu.VMEM((pr, pc), F32),
               pltpu.SemaphoreType.DMA((N_CHIPS,)), pltpu.SemaphoreType.DMA((N_CHIPS,)), pltpu.SemaphoreType.DMA]
    operands = [a] * ns + [b] * ns
    for pack in packs:
        in_specs.append(pl.BlockSpec(pack.shape, lambda ph, s, cc, nd=pack.ndim: (0,) * nd))
        out_specs.append(ANY)
        out_shape.append(jax.ShapeDtypeStruct((N_DEVICES,) + pack.shape, pack.dtype))
        operands.append(pack)
    for pack in packs:
        scratch += _PackGather.semaphores()
    if npart:
        in_specs += [ANY] * npart
        out_specs += [ANY] * npart
        out_shape += [jax.ShapeDtypeStruct(p.shape, p.dtype) for p in parts]
        scratch += [pltpu.SemaphoreType.DMA((npart, 3)), pltpu.SemaphoreType.DMA((npart, 3))]
        operands += list(parts)
    return pl.pallas_call(
        body, name="wgrad_" + tag,
        grid_spec=pltpu.PrefetchScalarGridSpec(
            num_scalar_prefetch=1, grid=(2, N_CHIPS), in_specs=in_specs, out_specs=out_specs, scratch_shapes=scratch),
        out_shape=out_shape,
        compiler_params=_params(dimension_semantics=("arbitrary", "arbitrary")),
    )(core_chip, *operands)[1:]


def _other_chips(x, y):
    return [(1 - x, y), (x, 1 - y), (1 - x, 1 - y)]


class _ShardGather:
    PAIRS = 9

    def __init__(self, outs, send_sems, recv_sems):
        self.outs, self.send_sems, self.recv_sems = outs, send_sems, recv_sems
        x, y, c = _position()
        self.c, self.j = c, 2 * x + y
        self.sibling = (x, y, 1 - c)
        self.chips = _other_chips(x, y)

    def _chip(self, k):
        px, py = self.chips[k]
        return 2 * px + py

    def _half(self, w, chip, which):
        hr = self.outs[w].shape[1] // 2
        return self.outs[w].at[chip, pl.ds(which * hr, hr), :]

    def _quarter(self, w, chip, q):
        qr = self.outs[w].shape[1] // 4
        return self.outs[w].at[chip, pl.ds(self.c * 2 * qr + q * qr, qr), :]

    def _copy(self, ref, w, pair, to, src=None):
        return pltpu.make_async_remote_copy(src_ref=ref if src is None else src, dst_ref=ref, send_sem=self.send_sems.at[w, pair],
                                            recv_sem=self.recv_sems.at[w, pair], device_id=to, device_id_type=MESH)

    def direct(self, w, k, q, src=None):
        return self._copy(self._quarter(w, self.j, q), w, 2 * k + q, (*self.chips[k], self.c), src)

    def direct_landed(self, w, k, q):
        return self._copy(self._quarter(w, self._chip(k), q), w, 2 * k + q, (*self.chips[k], self.c))

    def pass_on(self, w, q):
        return self._copy(self._quarter(w, self._chip(q), q), w, 4 + q, (*self.chips[1 - q], self.c))

    def passed_landed(self, w, q):
        return self._copy(self._quarter(w, self._chip(2), q), w, 4 + q, (*self.chips[1 - q], self.c))

    def hand_over(self, w, k):
        return self._copy(self._half(w, self._chip(k), self.c), w, 6 + k, self.sibling)

    def handed(self, w, k):
        return self._copy(self._half(w, self._chip(k), 1 - self.c), w, 6 + k, self.sibling)

    def start_direct(self, w, src_half=None):
        qr = self.outs[w].shape[1] // 4
        for k, q in ((0, 0), (1, 1), (0, 1), (1, 0)):
            self.direct(w, k, q, None if src_half is None else src_half.at[pl.ds(q * qr, qr), :]).start()

    def start_pass_on(self, w):
        for q in (0, 1):
            self.direct_landed(w, q, q).wait_recv()
            self.pass_on(w, q).start()

    def start_hand_over(self, w, diagonal):
        if diagonal:
            for q in (0, 1):
                self.passed_landed(w, q).wait_recv()
            self.hand_over(w, 2).start()
        else:
            for k in (0, 1):
                self.direct_landed(w, k, 1 - k).wait_recv()
                self.hand_over(w, k).start()

    def finish(self, w):
        for k in range(3):
            self.handed(w, k).wait_recv()
            self.hand_over(w, k).wait_send()
        for q in (0, 1):
            self.pass_on(w, q).wait_send()
            for k in (0, 1):
                self.direct(w, k, q).wait_send()


def _gather_first(w_in, w_out, w1, w2, conv_w, rconv_w, w_a, w_x, x, g1):
    t, d = x.shape
    tn = min(t, MATMUL_TOKEN_TILE)
    n_tiles = t // tn
    bigs = (w_in, w_out, w1, w2)
    convs = (conv_w, rconv_w)
    heads = (w_a, w_x)
    nb, nc = len(bigs), len(convs)

    def body(win_ref, wout_hbm, w1_hbm, w2_hbm, cw_ref, rw_ref, wa_ref, wx_ref, x_hbm, g1_ref, gin, gout, g1, g2, gcw, grw,
             bda, bdx, h1_hbm, st_in, st_out, st_1, st_2, f_out, f_1, f_2, st_cw, st_rw, xbuf, hbuf, send_sems, recv_sems,
             sm_send, sm_recv, local_sems, load_sems, x_sems, h_sems):
        stages = (st_in, st_out, st_1, st_2)
        outs = (gin, gout, g1, g2)
        conv_stages, conv_outs = (st_cw, st_rw), (gcw, grw)
        plan = _ShardGather(outs[:1], send_sems, recv_sems)
        j, c = plan.j, plan.c
        local = [pltpu.make_async_copy(stages[w], outs[w].at[j], local_sems.at[w]) for w in range(nb)]
        loads = [pltpu.make_async_copy(src, dst, load_sems.at[n])
                 for n, (src, dst) in enumerate(((wout_hbm, f_out), (w1_hbm, f_1), (w2_hbm, f_2)))]

        def columns(n, chip):
            width = convs[n].shape[1]
            return conv_outs[n].at[:, pl.ds(chip * width, width)]

        def x_load(i):
            return pltpu.make_async_copy(x_hbm.at[pl.ds(i * tn, tn), :], xbuf.at[i % 2], x_sems.at[i % 2])

        def h1_store(i):
            return pltpu.make_async_copy(hbuf.at[i % 2], h1_hbm.at[pl.ds(i * tn, tn), :], h_sems.at[i % 2])

        def first_norm():
            x_load(0).start()
            for i in range(n_tiles):
                if i + 1 < n_tiles:
                    x_load(i + 1).start()
                x_load(i).wait()
                if i >= 2:
                    h1_store(i - 2).wait()
                xv = xbuf[i % 2]
                rstd = lax.rsqrt(jnp.mean(xv * xv, axis=-1, keepdims=True) + EPS)
                hbuf[i % 2] = ((xv * rstd) * g1_ref[...]).astype(BF16)
                h1_store(i).start()
            for i in range(max(n_tiles - 2, 0), n_tiles):
                h1_store(i).wait()

        local += [pltpu.make_async_copy(conv_stages[n], columns(n, j), local_sems.at[nb + n]) for n in range(nc)]

        def small_copy(k, n, landed=False):
            px, py = plan.chips[k]
            return pltpu.make_async_remote_copy(
                src_ref=conv_stages[n], dst_ref=columns(n, 2 * px + py if landed else j), send_sem=sm_send.at[k, n],
                recv_sem=sm_recv.at[k, n], device_id=(px, py, c), device_id_type=MESH)

        for cp in loads:
            cp.start()
        hr = w_in.shape[0] // 2
        st_in[...] = win_ref[...].astype(BF16)
        plan.start_direct(0, st_in.at[pl.ds(c * hr, hr), :])
        for src, st in zip((cw_ref, rw_ref), conv_stages):
            st[...] = jnp.zeros_like(st)
            st[0:src.shape[0], :] = src[...]
        for k in range(3):
            for n in range(nc):
                small_copy(k, n).start()
        for src, bd in ((wa_ref, bda), (wx_ref, bdx)):
            bd[...] = jnp.zeros_like(bd)
            for h in range(N_HEADS):
                q = h % (BD // HEAD_DIM)
                bd[h // (BD // HEAD_DIM), q * HEAD_DIM:(q + 1) * HEAD_DIM, q * HEAD_DIM:(q + 1) * HEAD_DIM] = src[h].astype(BF16)
        for cp, full, st in zip(loads, (f_out, f_1, f_2), stages[1:]):
            cp.wait()
            st[...] = full[...].astype(BF16)
        for cp in local:
            cp.start()
        plan.start_pass_on(0)
        first_norm()
        plan.start_hand_over(0, diagonal=False)
        plan.start_hand_over(0, diagonal=True)
        for k in range(3):
            for n in range(nc):
                small_copy(k, n, landed=True).wait_recv()
                small_copy(k, n).wait_send()
        plan.finish(0)
        for cp in local:
            cp.wait()

    def gathered(a, dtype):
        return jax.ShapeDtypeStruct((N_CHIPS,) + a.shape, dtype)

    return pl.pallas_call(
        body, name="gather_first",
        in_specs=[VMEM] + [ANY] * (nb - 1) + [VMEM] * (nc + len(heads)) + [ANY, VMEM],
        out_specs=[ANY] * (nb + nc) + [VMEM] * len(heads) + [ANY],
        out_shape=[gathered(a, BF16) for a in bigs]
        + [jax.ShapeDtypeStruct((TILE_ROWS, N_CHIPS * a.shape[1]), F32) for a in convs]
        + [jax.ShapeDtypeStruct((N_BD, BD, BD), BF16) for _ in heads] + [jax.ShapeDtypeStruct((t, d), BF16)],
        scratch_shapes=[pltpu.VMEM(a.shape, BF16) for a in bigs] + [pltpu.VMEM(a.shape, F32) for a in bigs[1:]]
        + [pltpu.VMEM((TILE_ROWS, a.shape[1]), F32) for a in convs]
        + [pltpu.VMEM((2, tn, d), F32), pltpu.VMEM((2, tn, d), BF16)]
        + [pltpu.SemaphoreType.DMA((1, _ShardGather.PAIRS)), pltpu.SemaphoreType.DMA((1, _ShardGather.PAIRS)),
           pltpu.SemaphoreType.DMA((3, nc)), pltpu.SemaphoreType.DMA((3, nc)), pltpu.SemaphoreType.DMA((nb + nc,)),
           pltpu.SemaphoreType.DMA((nb - 1,)), pltpu.SemaphoreType.DMA((2,)), pltpu.SemaphoreType.DMA((2,))],
        compiler_params=_params(),
    )(*bigs, *convs, *heads, x, g1)


class _PartialExchange:
    def __init__(self, parts, arrived, send_sems, recv_sems):
        self.parts, self.arrived, self.send_sems, self.recv_sems = parts, arrived, send_sems, recv_sems
        x, y, c = _position()
        self.c, self.j = c, 2 * x + y
        self.chips = _other_chips(x, y)

    def _copy(self, w, k, slot):
        px, py = self.chips[k]
        return pltpu.make_async_remote_copy(
            src_ref=self.parts[w].at[2 * px + py], dst_ref=self.arrived[w].at[slot], send_sem=self.send_sems.at[w, k],
            recv_sem=self.recv_sems.at[w, k], device_id=(px, py, self.c), device_id_type=MESH)

    def start(self):
        for w in range(len(self.parts)):
            for k in range(3):
                self._copy(w, k, self.j).start()

    def wait(self):
        for w in range(len(self.parts)):
            for k in range(3):
                px, py = self.chips[k]
                self._copy(w, k, 2 * px + py).wait()


class _PackGather:
    def __init__(self, p_ref, all_ref, send_sems, recv_sems, local_sem):
        self.p_ref, self.all_ref, self.send_sems, self.recv_sems, self.local_sem = p_ref, all_ref, send_sems, recv_sems, local_sem
        x, y, c = _position()
        self.me, self.sibling, self.c = (x, y, c), (x, y, 1 - c), c
        self.chips = _other_chips(x, y)

    @staticmethod
    def semaphores():
        return [pltpu.SemaphoreType.DMA((7,)), pltpu.SemaphoreType.DMA((7,)), pltpu.SemaphoreType.DMA]

    def _copy(self, k, block, to, from_pack=False):
        px, py, pc = block
        slot = self.all_ref.at[4 * px + 2 * py + pc]
        return pltpu.make_async_remote_copy(src_ref=self.p_ref if from_pack else slot, dst_ref=slot, send_sem=self.send_sems.at[k],
                                            recv_sem=self.recv_sems.at[k], device_id=to, device_id_type=MESH)

    def _mine(self):
        x, y, c = self.me
        return pltpu.make_async_copy(self.p_ref, self.all_ref.at[4 * x + 2 * y + c], self.local_sem)

    def _first(self):
        return [self._copy(0, self.me, self.sibling, True)] + [
            self._copy(1 + k, self.me, (*chip, self.c), True) for k, chip in enumerate(self.chips)]

    def _passed(self):
        return [self._copy(4 + k, (*chip, self.c), self.sibling) for k, chip in enumerate(self.chips)]

    def start(self):
        self._mine().start()
        for cp in self._first():
            cp.start()

    def hand_over(self):
        for k, chip in enumerate(self.chips):
            self._copy(1 + k, (*chip, self.c), self.me).wait_recv()
            self._passed()[k].start()

    def finish(self):
        self._copy(0, self.sibling, self.me).wait_recv()
        for k, chip in enumerate(self.chips):
            self._copy(4 + k, (*chip, 1 - self.c), self.me).wait_recv()
        for cp in self._first() + self._passed():
            cp.wait_send()
        self._mine().wait()


class _DirectGather:
    def __init__(self, p_ref, all_ref, send_sems, recv_sems, local_sem):
        self.p_ref, self.all_ref, self.send_sems, self.recv_sems, self.local_sem = p_ref, all_ref, send_sems, recv_sems, local_sem
        self.me = _position()

    semaphores = _PackGather.semaphores

    def _peer(self, r):
        x, y, c = self.me
        return ((1 - x) if r & 4 else x, (1 - y) if r & 2 else y, (1 - c) if r & 1 else c)

    def _copy(self, r, slot_of):
        px, py, pc = slot_of
        return pltpu.make_async_remote_copy(src_ref=self.p_ref, dst_ref=self.all_ref.at[4 * px + 2 * py + pc],
                                            send_sem=self.send_sems.at[r - 1], recv_sem=self.recv_sems.at[r - 1],
                                            device_id=self._peer(r), device_id_type=MESH)

    def _mine(self):
        x, y, c = self.me
        return pltpu.make_async_copy(self.p_ref, self.all_ref.at[4 * x + 2 * y + c], self.local_sem)

    def start(self):
        self._mine().start()
        for r in range(1, N_DEVICES):
            self._copy(r, self.me).start()

    def finish(self):
        for r in range(1, N_DEVICES):
            self._copy(r, self._peer(r)).wait()
        self._mine().wait()


def _adamw(w, g, m, v):
    m = ADAM_B1 * m + (1.0 - ADAM_B1) * g
    v = ADAM_B2 * v + (1.0 - ADAM_B2) * (g * g)
    m_hat = m / ADAM_BC1
    v_hat = v / ADAM_BC2
    delta = -ADAM_LR * (m_hat / (jnp.sqrt(v_hat) + ADAM_EPS) + ADAM_WD * w)
    return delta, m, v


JOIN_SUB = 4


def _join(tag, shard_shape, part, arrived, core_chip, block=None):
    pr, pc = WGRAD_GEOMETRY[tag][:2]
    rb = pr // JOIN_SUB
    by_rows = shard_shape[1] == pc
    riding = block is not None

    def body(cc_ref, p_ref, r1_ref, r2_ref, r3_ref, *rest):
        if riding:
            blk_ref, g_ref, all_ref, stage, send_sems, recv_sems, local_sems, b_send, b_recv, b_local = rest
            gather = _DirectGather(blk_ref, all_ref, b_send, b_recv, b_local)
        else:
            g_ref, stage, send_sems, recv_sems, local_sems = rest
        i = pl.program_id(0)
        c = cc_ref[0]
        if riding:
            @pl.when(i == 0)
            def _():
                gather.start()

        def window(core, k):
            if by_rows:
                return g_ref.at[pl.ds((core * JOIN_SUB + k) * rb, rb), :]
            return g_ref.at[pl.ds(k * rb, rb), pl.ds(core * pc, pc)]

        def keep(k):
            return pltpu.make_async_copy(stage.at[k], window(c, k), local_sems.at[k])

        def push(k):
            return pltpu.make_async_remote_copy(src_ref=stage.at[k], dst_ref=window(c, k), send_sem=send_sems.at[k],
                                                recv_sem=recv_sems.at[k], device_id=_sibling(), device_id_type=MESH)

        def pushed(k):
            return pltpu.make_async_remote_copy(src_ref=stage.at[k], dst_ref=window(1 - c, k), send_sem=send_sems.at[k],
                                                recv_sem=recv_sems.at[k], device_id=_sibling(), device_id_type=MESH)

        stage[i] = ((p_ref[0] + r1_ref[0].astype(F32)) + r2_ref[0].astype(F32)) + r3_ref[0].astype(F32)
        keep(i).start()
        push(i).start()

        @pl.when(i == JOIN_SUB - 1)
        def _():
            for k in range(JOIN_SUB):
                keep(k).wait()
                push(k).wait_send()
                pushed(k).wait_recv()
            if riding:
                gather.finish()

    def partial(off):
        return pl.BlockSpec((1, rb, pc), lambda i, cc: ((cc[1] + off) % N_CHIPS, i, 0))

    in_specs = [partial(0), partial(1), partial(2), partial(3)]
    out_specs = [ANY]
    out_shape = [jax.ShapeDtypeStruct(shard_shape, F32)]
    scratch = [pltpu.VMEM((JOIN_SUB, rb, pc), F32), pltpu.SemaphoreType.DMA((JOIN_SUB,)),
               pltpu.SemaphoreType.DMA((JOIN_SUB,)), pltpu.SemaphoreType.DMA((JOIN_SUB,))]
    operands = [part, arrived, arrived, arrived]
    if riding:
        in_specs.append(pl.BlockSpec(block.shape, lambda i, cc: (0, 0)))
        out_specs.append(ANY)
        out_shape.append(jax.ShapeDtypeStruct((N_DEVICES,) + block.shape, block.dtype))
        scratch += _DirectGather.semaphores()
        operands.append(block)
    outs = pl.pallas_call(
        body, name="join_" + tag,
        grid_spec=pltpu.PrefetchScalarGridSpec(
            num_scalar_prefetch=1, grid=(JOIN_SUB,), in_specs=in_specs, out_specs=out_specs, scratch_shapes=scratch),
        out_shape=out_shape,
        compiler_params=_params(dimension_semantics=("arbitrary",)),
    )(core_chip, *operands)
    return outs if riding else outs[0]


def _adamw_big(w, g, m, v, name):
    rows, cols = w.shape
    rb = ADAMW_ROWS if rows % ADAMW_ROWS == 0 else rows // 2

    def body(w_ref, g_ref, m_ref, v_ref, go_ref, d_ref, nm_ref, nv_ref):
        g = g_ref[...]
        go_ref[...] = g
        d_ref[...], nm_ref[...], nv_ref[...] = _adamw(w_ref[...], g, m_ref[...], v_ref[...])

    spec = pl.BlockSpec((rb, cols), lambda i: (i, 0))
    return pl.pallas_call(
        body, name=name, grid=(rows // rb,), in_specs=[spec] * 4, out_specs=[spec] * 4,
        out_shape=[jax.ShapeDtypeStruct(w.shape, F32)] * 4,
        compiler_params=_params(dimension_semantics=("arbitrary",)),
    )(w, g, m, v)


SMALL_VECTORS = {
    "norm_mix_g": (PK_MIX_G, D_MODEL), "rnn_conv_b": (PK_RCONV_B, LRU_W), "b_a": (PK_B_A, LRU_W), "b_x": (PK_B_X, LRU_W),
    "lru_lambda": (PK_LAMBDA, LRU_W), "g_norm_conv": (PK_G_NORM_CONV, CONV_W), "g_norm_rnn": (PK_G_NORM_RNN, LRU_W),
    "norm_mlp_g": (PK_MLP_G, D_MODEL), "final_norm_g": (PK_FINAL_G, D_MODEL),
}
SMALL_MATRICES = ("w_a", "w_x")


def _small_step(vec_packs, mat_packs, mix_g_blocks, p):
    vec_rows, cols = vec_packs.shape[1:]
    conv_rows, cshard = p["conv_w"].shape
    rconv_rows, rshard = p["rnn_conv_w"].shape
    names = list(SMALL_VECTORS) + list(SMALL_MATRICES) + ["conv_w", "rnn_conv_w"]
    shapes = ([(1, width) for _, width in SMALL_VECTORS.values()] + [mat_packs.shape[2:]] * len(SMALL_MATRICES)
              + [(conv_rows, cshard), (rconv_rows, rshard)])
    kinds = ("", "m_", "v_")
    params = [p[pre + n].reshape(1, -1) if n in SMALL_VECTORS else p[pre + n] for pre in kinds for n in names]

    def body(vec_ref, mat_ref, blk_ref, *rest):
        wmv = [dict(zip(names, rest[k * len(names):(k + 1) * len(names)])) for k in range(3)]
        loss_ref, rest = rest[3 * len(names)], rest[3 * len(names) + 1:]
        leaves, (g_ref, w_ref, m_ref, v_ref) = [rest[k * len(names):(k + 1) * len(names)] for k in range(4)], rest[4 * len(names):]
        total = vec_ref[0]
        mats = mat_ref[0].astype(F32)
        late = blk_ref[0]
        for k in range(1, N_DEVICES):
            total = total + vec_ref[k]
            mats = mats + mat_ref[k].astype(F32)
            late = late + blk_ref[k]
        g_ref[0:vec_rows, :] = total
        g_ref[vec_rows:, :] = late
        g = g_ref[...]
        loss_ref[...] = g[PK_LOSS:PK_LOSS + 1, 0:1]

        for pack_ref, given in zip((w_ref, m_ref, v_ref), wmv):
            pack_ref[...] = jnp.zeros_like(pack_ref)
            for name, (row, width) in SMALL_VECTORS.items():
                pack_ref[row:row + 1, 0:width] = given[name][...]

        x, y, _ = _position()
        j = 2 * x + y
        cblk = total[0:TILE_ROWS, :]
        rblk = total[PK_RCONV_W:PK_RCONV_W + TILE_ROWS, :]
        cg = cblk[:, 0:cshard]
        rg = rblk[:, 0:rshard]
        for k in range(1, N_CHIPS):
            cg = jnp.where(j == k, cblk[:, k * cshard:(k + 1) * cshard], cg)
            rg = jnp.where(j == k, rblk[:, k * rshard:(k + 1) * rshard], rg)
        cg = cg[PK_CONV_W:PK_CONV_W + conv_rows, :]
        rg = rg[0:rconv_rows, :]

        def step(name, grad):
            return (grad,) + _adamw(wmv[0][name][...], grad, wmv[1][name][...], wmv[2][name][...])

        packs = (g,) + _adamw(w_ref[...], g, m_ref[...], v_ref[...])
        matrices = [step(name, mats[n]) for n, name in enumerate(SMALL_MATRICES)]
        convs, rconvs = step("conv_w", cg), step("rnn_conv_w", rg)
        for kind in range(4):
            out = dict(zip(names, leaves[kind]))
            for name, (row, width) in SMALL_VECTORS.items():
                out[name][...] = packs[kind][row:row + 1, 0:width]
            for n, name in enumerate(SMALL_MATRICES):
                out[name][...] = matrices[n][kind]
            out["conv_w"][...] = convs[kind]
            out["rnn_conv_w"][...] = rconvs[kind]

    outs = pl.pallas_call(
        body, name="small_grads_step", in_specs=[VMEM] * (3 + len(params)), out_specs=[VMEM] * (1 + 4 * len(names)),
        out_shape=[jax.ShapeDtypeStruct((1, 1), F32)] + [jax.ShapeDtypeStruct(sh, F32) for sh in shapes] * 4,
        scratch_shapes=[pltpu.VMEM((PK_ROWS, cols), F32)] * 4,
        compiler_params=_params(),
    )(vec_packs, mat_packs, mix_g_blocks, *params)
    return outs[0], [dict(zip(names, outs[1 + k * len(names):1 + (k + 1) * len(names)])) for k in range(4)]


_NAMES = ['norm_mix_g', 'w_in', 'conv_w', 'rnn_conv_w', 'rnn_conv_b', 'w_a', 'b_a', 'w_x', 'b_x', 'lru_lambda',
          'g_norm_conv', 'g_norm_rnn', 'w_out', 'norm_mlp_g', 'w_mlp_in', 'w_mlp_out', 'final_norm_g']


def kernel(x, norm_mix_g, w_in, conv_w, rnn_conv_w, rnn_conv_b, w_a, b_a, w_x, b_x, lru_lambda, g_norm_conv, g_norm_rnn, w_out, norm_mlp_g, w_mlp_in, w_mlp_out, final_norm_g, loss_target, m_norm_mix_g, m_w_in, m_conv_w, m_rnn_conv_w, m_rnn_conv_b, m_w_a, m_b_a, m_w_x, m_b_x, m_lru_lambda, m_g_norm_conv, m_g_norm_rnn, m_w_out, m_norm_mlp_g, m_w_mlp_in, m_w_mlp_out, m_final_norm_g, v_norm_mix_g, v_w_in, v_conv_w, v_rnn_conv_w, v_rnn_conv_b, v_w_a, v_b_a, v_w_x, v_b_x, v_lru_lambda, v_g_norm_conv, v_g_norm_rnn, v_w_out, v_norm_mlp_g, v_w_mlp_in, v_w_mlp_out, v_final_norm_g):
    args = dict(locals())
    p = {}
    for n in _NAMES:
        for pre in ("", "m_", "v_"):
            a = args[pre + n]
            p[pre + n] = a[0] if a.ndim >= 3 else a
    xs = x[0]
    target = loss_target[0]
    core_chip = jnp.stack([lax.axis_index("c"), 2 * lax.axis_index("x") + lax.axis_index("y")]).astype(jnp.int32)

    w_in_g, w_out_g, w1_g, w2_g, conv_full, rconv_full, wa_bd, wx_bd, h1b = _gather_first(
        p["w_in"], p["w_out"], p["w_mlp_in"], p["w_mlp_out"], p["conv_w"], p["rnn_conv_w"], p["w_a"], p["w_x"],
        xs, p["norm_mix_g"])
    gf = p["final_norm_g"].reshape(1, -1)
    lru = (wa_bd, p["b_a"], wx_bd, p["b_x"], p["lru_lambda"], p["g_norm_conv"], p["g_norm_rnn"])

    (u, xr, hs, c3, yb, gates), (w_out_g, w1_g, w2_g) = _fwd_mix(
        h1b, w_in_g, conv_full, rconv_full, p["rnn_conv_b"], *lru, (w_out_g, w1_g, w2_g))
    zb, dpb, h2b, dx3b, dx2, dx2b, dy, st_mlp = _mlp_fwd_bwd(
        xs, yb, w_out_g.reshape(-1, D_MODEL), w1_g, w2_g.reshape(-1, D_MODEL), p["norm_mlp_g"], gf, target)

    part_out = _wgrad(yb, dx2b, "out", core_chip)
    *part_1, arrived_out = _wgrad(h2b, dpb, "mlp_in", core_chip, parts=(part_out[1],))
    part_2 = _wgrad(zb, dx3b, "mlp_out", core_chip)
    (dub, vec_pack, mat_pack), (arrived_1, arrived_2) = _mix_bwd(
        dy, u, xr, hs, c3, gates, conv_full, rconv_full, wa_bd, wx_bd, p["lru_lambda"], p["g_norm_conv"], p["g_norm_rnn"],
        st_mlp, (part_1[1], part_2[1]))
    arrived_mlp = (arrived_out, arrived_1, arrived_2)
    *part_in, vec_packs, mat_packs = _wgrad(h1b, dub, "in", core_chip, packs=(vec_pack, mat_pack))
    early = (("w_out", "out", part_out, arrived_mlp[0]), ("w_mlp_in", "mlp_in", part_1, arrived_mlp[1]),
             ("w_mlp_out", "mlp_out", part_2, arrived_mlp[2]))
    (grad_x, st_in), arrived_in, joined = _in_bwd(
        dub, w_in_g, xs, dx2, p["norm_mix_g"], (part_in[1],),
        [(tag, p[n].shape, part[0], arrived) for n, tag, part, arrived in early], core_chip)
    g_in, mix_g_blocks = _join("in", p["w_in"].shape, part_in[0], arrived_in[0], core_chip, st_in)
    big = {}
    for n, tag, g in [(n, tag, g) for (n, tag, _, _), g in zip(early, joined)] + [("w_in", "in", g_in)]:
        big[n] = _adamw_big(p[n], g, p["m_" + n], p["v_" + n], "adamw_" + tag)

    loss, outs = _small_step(vec_packs, mat_packs, mix_g_blocks, p)
    for kind, o in enumerate(outs):
        o["final_norm_g"] = o["final_norm_g"].reshape(-1)
        for n in SMALL_MATRICES + ("conv_w", "rnn_conv_w"):
            o[n] = o[n][None]
        for n in ("w_in", "w_out", "w_mlp_in", "w_mlp_out"):
            o[n] = big[n][kind][None]
    loss = loss.reshape(())
    return (loss, grad_x[None], *[o[n] for o in outs for n in _NAMES])
```
